```python
import jax, jax.numpy as jnp
from jax import lax
import numpy as np


D_MODEL = 1024
BATCH = 8
SEQ = 4096
DEPTH = 4

N_MIXERS = 3
POOL_WINDOWS = (2, 4, 8, 16)
POOL_GROUPS = len(POOL_WINDOWS)
POOL_GROUP_DIM = D_MODEL // POOL_GROUPS
SGU_CHUNK = 128
SGU_WIDTH = D_MODEL
SGU_HEAD_DIM = 128
SGU_HEADS = SGU_WIDTH // SGU_HEAD_DIM
MLA_HEADS = 16
MLA_Q_LORA = 256
MLA_KV_LORA = 128
MLA_NOPE = 128
MLA_ROPE = 64
MLA_V = 128
ROPE_THETA = 10000.0
Q_BLOCK = 128
D_FF = 4 * D_MODEL
RMS_EPS = 1e-6
LN_EPS = 1e-5
MAX_POS_OFFSET = 4096
N_POOL_LAYERS = len(range(0, DEPTH, N_MIXERS))
N_SGU_LAYERS = len(range(1, DEPTH, N_MIXERS))
N_MLA_LAYERS = len(range(2, DEPTH, N_MIXERS))

kernel_name = 'hybrid_pool_sgu_mla_decoder'


def rmsnorm(x, g):
    xf = x.astype(jnp.float32)
    y = xf * lax.rsqrt(jnp.mean(xf * xf, axis=-1, keepdims=True) + RMS_EPS)
    return (y * g.astype(jnp.float32)).astype(x.dtype)


def layernorm(x, g, b):
    xf = x.astype(jnp.float32)
    mu = jnp.mean(xf, axis=-1, keepdims=True)
    xc = xf - mu
    var = jnp.mean(xc * xc, axis=-1, keepdims=True)
    y = xc * lax.rsqrt(var + LN_EPS)
    return (y * g.astype(jnp.float32) + b.astype(jnp.float32)).astype(x.dtype)


def modulate(h, shift, scale):
    return h * (1.0 + scale[:, None, :]) + shift[:, None, :]


def pool_mixer(h, w, scale):
    B_, S_, _ = h.shape
    hf = h.astype(jnp.float32).reshape(B_, S_, POOL_GROUPS, POOL_GROUP_DIM)
    cs = jnp.cumsum(hf, axis=1)
    cs = jnp.concatenate([jnp.zeros_like(cs[:, :1]), cs], axis=1)
    t = jnp.arange(S_, dtype=jnp.float32)
    outs = []
    for gi, win in enumerate(POOL_WINDOWS):
        csg = cs[:, :, gi]
        upper = csg[:, 1:]
        lower = jnp.concatenate([jnp.zeros_like(csg[:, :win - 1]), csg[:, :S_ - win + 1]], axis=1)
        count = jnp.minimum(t + 1.0, float(win))[None, :, None]
        outs.append((upper - lower) / count - hf[:, :, gi])
    pooled = jnp.stack(outs, axis=2).astype(h.dtype)
    y = jnp.einsum('bsgc,gcd->bsgd', pooled, w).reshape(B_, S_, D_MODEL)
    return y * scale


def sgu_mixer(h, w_in, ln_g, ln_b, w_s, b_s, w_out):
    B_, S_, _ = h.shape
    z = jax.nn.gelu(h @ w_in, approximate=False)
    u, v = jnp.split(z, 2, axis=-1)
    v = layernorm(v, ln_g, ln_b)
    nc = S_ // SGU_CHUNK
    v = v.reshape(B_, nc, SGU_CHUNK, SGU_HEADS, SGU_HEAD_DIM)
    mask = jnp.tril(jnp.ones((SGU_CHUNK, SGU_CHUNK), dtype=bool))
    ws = jnp.where(mask[None], w_s, 0)
    mixed = jnp.einsum('hts,bnshc->bnthc', ws, v) + b_s.T[None, None, :, :, None]
    gated = u * mixed.reshape(B_, S_, SGU_WIDTH)
    return gated @ w_out


def apply_rope(x, cos, sin):
    x1, x2 = jnp.split(x, 2, axis=-1)
    return jnp.concatenate([x1 * cos - x2 * sin, x2 * cos + x1 * sin], axis=-1)


def mla_mixer(h, positions, w_dq_dkv, q_norm_g, kv_norm_g, w_uq, w_ukv, w_o):
    B_, S_, _ = h.shape
    lat = h @ w_dq_dkv
    c_q, c_kv, k_rope = jnp.split(lat, [MLA_Q_LORA, MLA_Q_LORA + MLA_KV_LORA], axis=-1)
    c_q = rmsnorm(c_q, q_norm_g)
    c_kv = rmsnorm(c_kv, kv_norm_g)
    q = (c_q @ w_uq).reshape(B_, S_, MLA_HEADS, MLA_NOPE + MLA_ROPE)
    q_nope, q_rope = jnp.split(q, [MLA_NOPE], axis=-1)
    kv = (c_kv @ w_ukv).reshape(B_, S_, MLA_HEADS, MLA_NOPE + MLA_V)
    k_nope, v = jnp.split(kv, [MLA_NOPE], axis=-1)
    inv_freq = ROPE_THETA ** (-jnp.arange(0, MLA_ROPE, 2, dtype=jnp.float32) / MLA_ROPE)
    ang = positions.astype(jnp.float32)[..., None] * inv_freq
    cos, sin = jnp.cos(ang), jnp.sin(ang)
    q_rope = apply_rope(q_rope.astype(jnp.float32), cos[:, :, None], sin[:, :, None]).astype(h.dtype)
    k_rope = apply_rope(k_rope.astype(jnp.float32), cos, sin).astype(h.dtype)
    sm_scale = (MLA_NOPE + MLA_ROPE) ** -0.5
    nb = S_ // Q_BLOCK

    def to_blocks(t):
        return jnp.moveaxis(t.reshape(B_, nb, Q_BLOCK, *t.shape[2:]), 1, 0)

    k_idx = jnp.arange(S_)

    def attend_block(args):
        qn, qr, blk = args
        s = jnp.einsum('bqhd,bkhd->bhqk', qn, k_nope, preferred_element_type=jnp.float32)
        s = s + jnp.einsum('bqhr,bkr->bhqk', qr, k_rope, preferred_element_type=jnp.float32)
        q_idx = blk * Q_BLOCK + jnp.arange(Q_BLOCK)
        causal = k_idx[None, :] <= q_idx[:, None]
        s = jnp.where(causal[None, None], s * sm_scale, -1e30)
        p = jax.nn.softmax(s, axis=-1).astype(v.dtype)
        return jnp.einsum('bhqk,bkhd->bqhd', p, v)

    out = lax.map(attend_block, (to_blocks(q_nope), to_blocks(q_rope), jnp.arange(nb)))
    out = jnp.moveaxis(out, 0, 1).reshape(B_, S_, MLA_HEADS * MLA_V)
    return out @ w_o


def sq_relu_mlp(h, w1, w2):
    return jnp.square(jax.nn.relu(h @ w1)) @ w2


def _fwd_setup_inputs(seed: int = 0) -> dict:
    key = jax.random.key(seed)
    ks = jax.random.split(key, 32)
    f32 = jnp.float32

    def nrm(k, shape, std):
        return jax.random.normal(k, shape, f32) * std

    def gain(k, shape):
        return 1.0 + 0.1 * jax.random.normal(k, shape, f32)

    x = jax.random.normal(ks[0], (BATCH, SEQ, D_MODEL), f32)
    c = jax.random.normal(ks[1], (BATCH, D_MODEL), f32)
    offset = jax.random.randint(ks[2], (BATCH, 1), 0, MAX_POS_OFFSET, dtype=jnp.int32)
    positions = (offset + jnp.arange(SEQ, dtype=jnp.int32)[None, :]).astype(jnp.int32)
    return {
        'x': x,
        'c': c,
        'positions': positions,
        'ada_w': nrm(ks[3], (DEPTH, D_MODEL, 6 * D_MODEL), 0.5 * D_MODEL ** -0.5),
        'ada_b': nrm(ks[4], (DEPTH, 6 * D_MODEL), 0.02),
        'norm_mix_g': gain(ks[5], (DEPTH, D_MODEL)),
        'norm_mlp_g': gain(ks[6], (DEPTH, D_MODEL)),
        'pool_w': nrm(ks[7], (N_POOL_LAYERS, POOL_GROUPS, POOL_GROUP_DIM, POOL_GROUP_DIM), POOL_GROUP_DIM ** -0.5),
        'pool_scale': gain(ks[8], (N_POOL_LAYERS, D_MODEL)),
        'sgu_w_in': nrm(ks[9], (N_SGU_LAYERS, D_MODEL, 2 * SGU_WIDTH), D_MODEL ** -0.5),
        'sgu_ln_g': gain(ks[10], (N_SGU_LAYERS, SGU_WIDTH)),
        'sgu_ln_b': nrm(ks[11], (N_SGU_LAYERS, SGU_WIDTH), 0.02),
        'sgu_w_s': nrm(ks[12], (N_SGU_LAYERS, SGU_HEADS, SGU_CHUNK, SGU_CHUNK), SGU_CHUNK ** -0.5),
        'sgu_b_s': gain(ks[13], (N_SGU_LAYERS, SGU_HEADS, SGU_CHUNK)),
        'sgu_w_out': nrm(ks[14], (N_SGU_LAYERS, SGU_WIDTH, D_MODEL), SGU_WIDTH ** -0.5),
        'mla_w_dq_dkv': nrm(ks[15], (N_MLA_LAYERS, D_MODEL, MLA_Q_LORA + MLA_KV_LORA + MLA_ROPE), D_MODEL ** -0.5),
        'mla_q_norm_g': gain(ks[16], (N_MLA_LAYERS, MLA_Q_LORA)),
        'mla_kv_norm_g': gain(ks[17], (N_MLA_LAYERS, MLA_KV_LORA)),
        'mla_w_uq': nrm(ks[18], (N_MLA_LAYERS, MLA_Q_LORA, MLA_HEADS * (MLA_NOPE + MLA_ROPE)), MLA_Q_LORA ** -0.5),
        'mla_w_ukv': nrm(ks[19], (N_MLA_LAYERS, MLA_KV_LORA, MLA_HEADS * (MLA_NOPE + MLA_V)), MLA_KV_LORA ** -0.5),
        'mla_w_o': nrm(ks[20], (N_MLA_LAYERS, MLA_HEADS * MLA_V, D_MODEL), (MLA_HEADS * MLA_V) ** -0.5),
        'mlp_w1': nrm(ks[21], (DEPTH, D_MODEL, D_FF), D_MODEL ** -0.5),
        'mlp_w2': nrm(ks[22], (DEPTH, D_FF, D_MODEL), D_FF ** -0.5),
        'final_g': gain(ks[23], (D_MODEL,)),
    }


def _fwd_reference(x, c, positions, ada_w, ada_b, norm_mix_g, norm_mlp_g, pool_w, pool_scale,
              sgu_w_in, sgu_ln_g, sgu_ln_b, sgu_w_s, sgu_b_s, sgu_w_out,
              mla_w_dq_dkv, mla_q_norm_g, mla_kv_norm_g, mla_w_uq, mla_w_ukv, mla_w_o,
              mlp_w1, mlp_w2, final_g):
    c_act = jax.nn.silu(c)
    for i in range(DEPTH):
        mod = c_act @ ada_w[i] + ada_b[i]
        sh1, sc1, g1, sh2, sc2, g2 = jnp.split(mod, 6, axis=-1)
        h = modulate(rmsnorm(x, norm_mix_g[i]), sh1, sc1)
        kind, j = i % N_MIXERS, i // N_MIXERS
        if kind == 0:
            y = pool_mixer(h, pool_w[j], pool_scale[j])
        elif kind == 1:
            y = sgu_mixer(h, sgu_w_in[j], sgu_ln_g[j], sgu_ln_b[j], sgu_w_s[j], sgu_b_s[j], sgu_w_out[j])
        else:
            y = mla_mixer(h, positions, mla_w_dq_dkv[j], mla_q_norm_g[j], mla_kv_norm_g[j],
                          mla_w_uq[j], mla_w_ukv[j], mla_w_o[j])
        x = x + g1[:, None, :] * y
        h = modulate(rmsnorm(x, norm_mlp_g[i]), sh2, sc2)
        x = x + g2[:, None, :] * sq_relu_mlp(h, mlp_w1[i], mlp_w2[i])
    return rmsnorm(x, final_g)


import jax as _jax
import jax.numpy as _jnp

TWIN_FORMAT = 'train_step'
FWD_PARAMS = ['x', 'c', 'positions', 'ada_w', 'ada_b', 'norm_mix_g', 'norm_mlp_g', 'pool_w', 'pool_scale', 'sgu_w_in', 'sgu_ln_g', 'sgu_ln_b', 'sgu_w_s', 'sgu_b_s', 'sgu_w_out', 'mla_w_dq_dkv', 'mla_q_norm_g', 'mla_kv_norm_g', 'mla_w_uq', 'mla_w_ukv', 'mla_w_o', 'mlp_w1', 'mlp_w2', 'final_g']
TWIN_WEIGHTS = ['ada_w', 'ada_b', 'norm_mix_g', 'norm_mlp_g', 'pool_w', 'pool_scale', 'sgu_w_in', 'sgu_ln_g', 'sgu_ln_b', 'sgu_w_s', 'sgu_b_s', 'sgu_w_out', 'mla_w_dq_dkv', 'mla_q_norm_g', 'mla_kv_norm_g', 'mla_w_uq', 'mla_w_ukv', 'mla_w_o', 'mlp_w1', 'mlp_w2', 'final_g']
TWIN_DIFF_INPUT = 'x'
TWIN_INPUTS = ['x', 'c', 'positions', 'ada_w', 'ada_b', 'norm_mix_g', 'norm_mlp_g', 'pool_w', 'pool_scale', 'sgu_w_in', 'sgu_ln_g', 'sgu_ln_b', 'sgu_w_s', 'sgu_b_s', 'sgu_w_out', 'mla_w_dq_dkv', 'mla_q_norm_g', 'mla_kv_norm_g', 'mla_w_uq', 'mla_w_ukv', 'mla_w_o', 'mlp_w1', 'mlp_w2', 'final_g', 'loss_target', 'm_ada_w', 'm_ada_b', 'm_norm_mix_g', 'm_norm_mlp_g', 'm_pool_w', 'm_pool_scale', 'm_sgu_w_in', 'm_sgu_ln_g', 'm_sgu_ln_b', 'm_sgu_w_s', 'm_sgu_b_s', 'm_sgu_w_out', 'm_mla_w_dq_dkv', 'm_mla_q_norm_g', 'm_mla_kv_norm_g', 'm_mla_w_uq', 'm_mla_w_ukv', 'm_mla_w_o', 'm_mlp_w1', 'm_mlp_w2', 'm_final_g', 'v_ada_w', 'v_ada_b', 'v_norm_mix_g', 'v_norm_mlp_g', 'v_pool_w', 'v_pool_scale', 'v_sgu_w_in', 'v_sgu_ln_g', 'v_sgu_ln_b', 'v_sgu_w_s', 'v_sgu_b_s', 'v_sgu_w_out', 'v_mla_w_dq_dkv', 'v_mla_q_norm_g', 'v_mla_kv_norm_g', 'v_mla_w_uq', 'v_mla_w_ukv', 'v_mla_w_o', 'v_mlp_w1', 'v_mlp_w2', 'v_final_g']
TWIN_OUTPUTS = ['loss', 'grad_x', 'grad_ada_w', 'grad_ada_b', 'grad_norm_mix_g', 'grad_norm_mlp_g', 'grad_pool_w', 'grad_pool_scale', 'grad_sgu_w_in', 'grad_sgu_ln_g', 'grad_sgu_ln_b', 'grad_sgu_w_s', 'grad_sgu_b_s', 'grad_sgu_w_out', 'grad_mla_w_dq_dkv', 'grad_mla_q_norm_g', 'grad_mla_kv_norm_g', 'grad_mla_w_uq', 'grad_mla_w_ukv', 'grad_mla_w_o', 'grad_mlp_w1', 'grad_mlp_w2', 'grad_final_g', 'delta_ada_w', 'delta_ada_b', 'delta_norm_mix_g', 'delta_norm_mlp_g', 'delta_pool_w', 'delta_pool_scale', 'delta_sgu_w_in', 'delta_sgu_ln_g', 'delta_sgu_ln_b', 'delta_sgu_w_s', 'delta_sgu_b_s', 'delta_sgu_w_out', 'delta_mla_w_dq_dkv', 'delta_mla_q_norm_g', 'delta_mla_kv_norm_g', 'delta_mla_w_uq', 'delta_mla_w_ukv', 'delta_mla_w_o', 'delta_mlp_w1', 'delta_mlp_w2', 'delta_final_g', 'new_m_ada_w', 'new_m_ada_b', 'new_m_norm_mix_g', 'new_m_norm_mlp_g', 'new_m_pool_w', 'new_m_pool_scale', 'new_m_sgu_w_in', 'new_m_sgu_ln_g', 'new_m_sgu_ln_b', 'new_m_sgu_w_s', 'new_m_sgu_b_s', 'new_m_sgu_w_out', 'new_m_mla_w_dq_dkv', 'new_m_mla_q_norm_g', 'new_m_mla_kv_norm_g', 'new_m_mla_w_uq', 'new_m_mla_w_ukv', 'new_m_mla_w_o', 'new_m_mlp_w1', 'new_m_mlp_w2', 'new_m_final_g', 'new_v_ada_w', 'new_v_ada_b', 'new_v_norm_mix_g', 'new_v_norm_mlp_g', 'new_v_pool_w', 'new_v_pool_scale', 'new_v_sgu_w_in', 'new_v_sgu_ln_g', 'new_v_sgu_ln_b', 'new_v_sgu_w_s', 'new_v_sgu_b_s', 'new_v_sgu_w_out', 'new_v_mla_w_dq_dkv', 'new_v_mla_q_norm_g', 'new_v_mla_kv_norm_g', 'new_v_mla_w_uq', 'new_v_mla_w_ukv', 'new_v_mla_w_o', 'new_v_mlp_w1', 'new_v_mlp_w2', 'new_v_final_g']
TWIN_LEAF_KINDS = {'loss': 'loss', 'grad_x': 'grad_x', 'grad_ada_w': 'grad_w', 'grad_ada_b': 'grad_w', 'grad_norm_mix_g': 'grad_w', 'grad_norm_mlp_g': 'grad_w', 'grad_pool_w': 'grad_w', 'grad_pool_scale': 'grad_w', 'grad_sgu_w_in': 'grad_w', 'grad_sgu_ln_g': 'grad_w', 'grad_sgu_ln_b': 'grad_w', 'grad_sgu_w_s': 'grad_w', 'grad_sgu_b_s': 'grad_w', 'grad_sgu_w_out': 'grad_w', 'grad_mla_w_dq_dkv': 'grad_w', 'grad_mla_q_norm_g': 'grad_w', 'grad_mla_kv_norm_g': 'grad_w', 'grad_mla_w_uq': 'grad_w', 'grad_mla_w_ukv': 'grad_w', 'grad_mla_w_o': 'grad_w', 'grad_mlp_w1': 'grad_w', 'grad_mlp_w2': 'grad_w', 'grad_final_g': 'grad_w', 'delta_ada_w': 'delta_w', 'delta_ada_b': 'delta_w', 'delta_norm_mix_g': 'delta_w', 'delta_norm_mlp_g': 'delta_w', 'delta_pool_w': 'delta_w', 'delta_pool_scale': 'delta_w', 'delta_sgu_w_in': 'delta_w', 'delta_sgu_ln_g': 'delta_w', 'delta_sgu_ln_b': 'delta_w', 'delta_sgu_w_s': 'delta_w', 'delta_sgu_b_s': 'delta_w', 'delta_sgu_w_out': 'delta_w', 'delta_mla_w_dq_dkv': 'delta_w', 'delta_mla_q_norm_g': 'delta_w', 'delta_mla_kv_norm_g': 'delta_w', 'delta_mla_w_uq': 'delta_w', 'delta_mla_w_ukv': 'delta_w', 'delta_mla_w_o': 'delta_w', 'delta_mlp_w1': 'delta_w', 'delta_mlp_w2': 'delta_w', 'delta_final_g': 'delta_w', 'new_m_ada_w': 'new_m', 'new_m_ada_b': 'new_m', 'new_m_norm_mix_g': 'new_m', 'new_m_norm_mlp_g': 'new_m', 'new_m_pool_w': 'new_m', 'new_m_pool_scale': 'new_m', 'new_m_sgu_w_in': 'new_m', 'new_m_sgu_ln_g': 'new_m', 'new_m_sgu_ln_b': 'new_m', 'new_m_sgu_w_s': 'new_m', 'new_m_sgu_b_s': 'new_m', 'new_m_sgu_w_out': 'new_m', 'new_m_mla_w_dq_dkv': 'new_m', 'new_m_mla_q_norm_g': 'new_m', 'new_m_mla_kv_norm_g': 'new_m', 'new_m_mla_w_uq': 'new_m', 'new_m_mla_w_ukv': 'new_m', 'new_m_mla_w_o': 'new_m', 'new_m_mlp_w1': 'new_m', 'new_m_mlp_w2': 'new_m', 'new_m_final_g': 'new_m', 'new_v_ada_w': 'new_v', 'new_v_ada_b': 'new_v', 'new_v_norm_mix_g': 'new_v', 'new_v_norm_mlp_g': 'new_v', 'new_v_pool_w': 'new_v', 'new_v_pool_scale': 'new_v', 'new_v_sgu_w_in': 'new_v', 'new_v_sgu_ln_g': 'new_v', 'new_v_sgu_ln_b': 'new_v', 'new_v_sgu_w_s': 'new_v', 'new_v_sgu_b_s': 'new_v', 'new_v_sgu_w_out': 'new_v', 'new_v_mla_w_dq_dkv': 'new_v', 'new_v_mla_q_norm_g': 'new_v', 'new_v_mla_kv_norm_g': 'new_v', 'new_v_mla_w_uq': 'new_v', 'new_v_mla_w_ukv': 'new_v', 'new_v_mla_w_o': 'new_v', 'new_v_mlp_w1': 'new_v', 'new_v_mlp_w2': 'new_v', 'new_v_final_g': 'new_v'}


def _forward(args):
    return _fwd_reference(*[args[k] for k in FWD_PARAMS])


def _output_shape():
    out = _jax.eval_shape(lambda: _forward(_fwd_setup_inputs(0)))
    return out.shape, out.dtype

N_MICROBATCH = 1
ADAM_LR = 0.001
ADAM_B1 = 0.9
ADAM_B2 = 0.999
ADAM_EPS = 1e-08
ADAM_WD = 0.01
ADAM_STEP = 10
PER_EXAMPLE_BATCH_AXIS = {'x': 0, 'c': 0, 'positions': 0, 'loss_target': 0}
SHARED_INPUTS = []
_WEIGHT_DTYPES = {'ada_w': _jnp.float32, 'ada_b': _jnp.float32, 'norm_mix_g': _jnp.float32, 'norm_mlp_g': _jnp.float32, 'pool_w': _jnp.float32, 'pool_scale': _jnp.float32, 'sgu_w_in': _jnp.float32, 'sgu_ln_g': _jnp.float32, 'sgu_ln_b': _jnp.float32, 'sgu_w_s': _jnp.float32, 'sgu_b_s': _jnp.float32, 'sgu_w_out': _jnp.float32, 'mla_w_dq_dkv': _jnp.float32, 'mla_q_norm_g': _jnp.float32, 'mla_kv_norm_g': _jnp.float32, 'mla_w_uq': _jnp.float32, 'mla_w_ukv': _jnp.float32, 'mla_w_o': _jnp.float32, 'mlp_w1': _jnp.float32, 'mlp_w2': _jnp.float32, 'final_g': _jnp.float32}
MOMENT_SCALE = {'ada_w': 2.520503e-01, 'ada_b': 4.684777e-01, 'norm_mix_g': 4.948901e-02, 'norm_mlp_g': 8.237068e-02, 'pool_w': 5.084733e-02, 'pool_scale': 2.402087e-01, 'sgu_w_in': 4.236161e-02, 'sgu_ln_g': 2.534083e-02, 'sgu_ln_b': 2.739634e-02, 'sgu_w_s': 2.522552e-02, 'sgu_b_s': 3.623437e-02, 'sgu_w_out': 7.483273e-02, 'mla_w_dq_dkv': 1.004236e-01, 'mla_q_norm_g': 1.687096e-02, 'mla_kv_norm_g': 2.999500e-01, 'mla_w_uq': 4.509606e-03, 'mla_w_ukv': 3.394276e-02, 'mla_w_o': 6.976646e-02, 'mlp_w1': 4.920584e-02, 'mlp_w2': 1.578344e-01, 'final_g': 3.254503e+01}


def _to_microbatches(a, axis):
    t = _jnp.moveaxis(a, axis, 0)
    t = t.reshape((N_MICROBATCH, t.shape[0] // N_MICROBATCH) + t.shape[1:])
    return _jnp.moveaxis(t, 1, axis + 1)


def setup_inputs(seed: int = 0) -> dict:
    inp = _fwd_setup_inputs(seed)
    key = _jax.random.fold_in(_jax.random.key(seed), 7919)
    shape, _ = _output_shape()
    out = dict(inp)
    out["loss_target"] = _jax.random.normal(_jax.random.fold_in(key, 0), shape, _jnp.float32)
    for i, name in enumerate(TWIN_WEIGHTS):
        w = inp[name].astype(_jnp.float32)
        if MOMENT_SCALE is None:
            s = _jnp.sqrt(_jnp.mean(_jnp.square(w)) + 1e-30)
        else:
            s = MOMENT_SCALE[name]
        km, kv = _jax.random.split(_jax.random.fold_in(key, i + 1))
        out[name] = w
        out["m_" + name] = s * _jax.random.normal(km, w.shape, _jnp.float32)
        out["v_" + name] = (s * s) * _jax.random.uniform(kv, w.shape, _jnp.float32, 0.5, 1.5)
    if N_MICROBATCH > 1:
        for name, axis in PER_EXAMPLE_BATCH_AXIS.items():
            out[name] = _to_microbatches(out[name], axis)
    return {'x': out['x'], 'c': out['c'], 'positions': out['positions'], 'ada_w': out['ada_w'], 'ada_b': out['ada_b'], 'norm_mix_g': out['norm_mix_g'], 'norm_mlp_g': out['norm_mlp_g'], 'pool_w': out['pool_w'], 'pool_scale': out['pool_scale'], 'sgu_w_in': out['sgu_w_in'], 'sgu_ln_g': out['sgu_ln_g'], 'sgu_ln_b': out['sgu_ln_b'], 'sgu_w_s': out['sgu_w_s'], 'sgu_b_s': out['sgu_b_s'], 'sgu_w_out': out['sgu_w_out'], 'mla_w_dq_dkv': out['mla_w_dq_dkv'], 'mla_q_norm_g': out['mla_q_norm_g'], 'mla_kv_norm_g': out['mla_kv_norm_g'], 'mla_w_uq': out['mla_w_uq'], 'mla_w_ukv': out['mla_w_ukv'], 'mla_w_o': out['mla_w_o'], 'mlp_w1': out['mlp_w1'], 'mlp_w2': out['mlp_w2'], 'final_g': out['final_g'], 'loss_target': out['loss_target'], 'm_ada_w': out['m_ada_w'], 'm_ada_b': out['m_ada_b'], 'm_norm_mix_g': out['m_norm_mix_g'], 'm_norm_mlp_g': out['m_norm_mlp_g'], 'm_pool_w': out['m_pool_w'], 'm_pool_scale': out['m_pool_scale'], 'm_sgu_w_in': out['m_sgu_w_in'], 'm_sgu_ln_g': out['m_sgu_ln_g'], 'm_sgu_ln_b': out['m_sgu_ln_b'], 'm_sgu_w_s': out['m_sgu_w_s'], 'm_sgu_b_s': out['m_sgu_b_s'], 'm_sgu_w_out': out['m_sgu_w_out'], 'm_mla_w_dq_dkv': out['m_mla_w_dq_dkv'], 'm_mla_q_norm_g': out['m_mla_q_norm_g'], 'm_mla_kv_norm_g': out['m_mla_kv_norm_g'], 'm_mla_w_uq': out['m_mla_w_uq'], 'm_mla_w_ukv': out['m_mla_w_ukv'], 'm_mla_w_o': out['m_mla_w_o'], 'm_mlp_w1': out['m_mlp_w1'], 'm_mlp_w2': out['m_mlp_w2'], 'm_final_g': out['m_final_g'], 'v_ada_w': out['v_ada_w'], 'v_ada_b': out['v_ada_b'], 'v_norm_mix_g': out['v_norm_mix_g'], 'v_norm_mlp_g': out['v_norm_mlp_g'], 'v_pool_w': out['v_pool_w'], 'v_pool_scale': out['v_pool_scale'], 'v_sgu_w_in': out['v_sgu_w_in'], 'v_sgu_ln_g': out['v_sgu_ln_g'], 'v_sgu_ln_b': out['v_sgu_ln_b'], 'v_sgu_w_s': out['v_sgu_w_s'], 'v_sgu_b_s': out['v_sgu_b_s'], 'v_sgu_w_out': out['v_sgu_w_out'], 'v_mla_w_dq_dkv': out['v_mla_w_dq_dkv'], 'v_mla_q_norm_g': out['v_mla_q_norm_g'], 'v_mla_kv_norm_g': out['v_mla_kv_norm_g'], 'v_mla_w_uq': out['v_mla_w_uq'], 'v_mla_w_ukv': out['v_mla_w_ukv'], 'v_mla_w_o': out['v_mla_w_o'], 'v_mlp_w1': out['v_mlp_w1'], 'v_mlp_w2': out['v_mlp_w2'], 'v_final_g': out['v_final_g']}


def _loss(weights, diff, rest, loss_target):
    with _jax.named_scope("forward"):
        args = {**rest, TWIN_DIFF_INPUT: diff, **{k: w.astype(_WEIGHT_DTYPES[k]) for k, w in weights.items()}}
        y = _forward(args)
    with _jax.named_scope("loss_head"):
        err = _jnp.square(y.astype(_jnp.float32) - loss_target)
        return 0.5 * _jnp.sum(_jnp.mean(err, axis=-1)) if err.ndim else 0.5 * err


def _adamw(w, g, m, v):
    m = ADAM_B1 * m + (1.0 - ADAM_B1) * g
    v = ADAM_B2 * v + (1.0 - ADAM_B2) * _jnp.square(g)
    m_hat = m / (1.0 - ADAM_B1 ** ADAM_STEP)
    v_hat = v / (1.0 - ADAM_B2 ** ADAM_STEP)
    delta = -ADAM_LR * (m_hat / (_jnp.sqrt(v_hat) + ADAM_EPS) + ADAM_WD * w)
    return delta, m, v


def reference(x, c, positions, ada_w, ada_b, norm_mix_g, norm_mlp_g, pool_w, pool_scale, sgu_w_in, sgu_ln_g, sgu_ln_b, sgu_w_s, sgu_b_s, sgu_w_out, mla_w_dq_dkv, mla_q_norm_g, mla_kv_norm_g, mla_w_uq, mla_w_ukv, mla_w_o, mlp_w1, mlp_w2, final_g, loss_target, m_ada_w, m_ada_b, m_norm_mix_g, m_norm_mlp_g, m_pool_w, m_pool_scale, m_sgu_w_in, m_sgu_ln_g, m_sgu_ln_b, m_sgu_w_s, m_sgu_b_s, m_sgu_w_out, m_mla_w_dq_dkv, m_mla_q_norm_g, m_mla_kv_norm_g, m_mla_w_uq, m_mla_w_ukv, m_mla_w_o, m_mlp_w1, m_mlp_w2, m_final_g, v_ada_w, v_ada_b, v_norm_mix_g, v_norm_mlp_g, v_pool_w, v_pool_scale, v_sgu_w_in, v_sgu_ln_g, v_sgu_ln_b, v_sgu_w_s, v_sgu_b_s, v_sgu_w_out, v_mla_w_dq_dkv, v_mla_q_norm_g, v_mla_kv_norm_g, v_mla_w_uq, v_mla_w_ukv, v_mla_w_o, v_mlp_w1, v_mlp_w2, v_final_g):
    given = dict(x=x, c=c, positions=positions, ada_w=ada_w, ada_b=ada_b, norm_mix_g=norm_mix_g, norm_mlp_g=norm_mlp_g, pool_w=pool_w, pool_scale=pool_scale, sgu_w_in=sgu_w_in, sgu_ln_g=sgu_ln_g, sgu_ln_b=sgu_ln_b, sgu_w_s=sgu_w_s, sgu_b_s=sgu_b_s, sgu_w_out=sgu_w_out, mla_w_dq_dkv=mla_w_dq_dkv, mla_q_norm_g=mla_q_norm_g, mla_kv_norm_g=mla_kv_norm_g, mla_w_uq=mla_w_uq, mla_w_ukv=mla_w_ukv, mla_w_o=mla_w_o, mlp_w1=mlp_w1, mlp_w2=mlp_w2, final_g=final_g, loss_target=loss_target, m_ada_w=m_ada_w, m_ada_b=m_ada_b, m_norm_mix_g=m_norm_mix_g, m_norm_mlp_g=m_norm_mlp_g, m_pool_w=m_pool_w, m_pool_scale=m_pool_scale, m_sgu_w_in=m_sgu_w_in, m_sgu_ln_g=m_sgu_ln_g, m_sgu_ln_b=m_sgu_ln_b, m_sgu_w_s=m_sgu_w_s, m_sgu_b_s=m_sgu_b_s, m_sgu_w_out=m_sgu_w_out, m_mla_w_dq_dkv=m_mla_w_dq_dkv, m_mla_q_norm_g=m_mla_q_norm_g, m_mla_kv_norm_g=m_mla_kv_norm_g, m_mla_w_uq=m_mla_w_uq, m_mla_w_ukv=m_mla_w_ukv, m_mla_w_o=m_mla_w_o, m_mlp_w1=m_mlp_w1, m_mlp_w2=m_mlp_w2, m_final_g=m_final_g, v_ada_w=v_ada_w, v_ada_b=v_ada_b, v_norm_mix_g=v_norm_mix_g, v_norm_mlp_g=v_norm_mlp_g, v_pool_w=v_pool_w, v_pool_scale=v_pool_scale, v_sgu_w_in=v_sgu_w_in, v_sgu_ln_g=v_sgu_ln_g, v_sgu_ln_b=v_sgu_ln_b, v_sgu_w_s=v_sgu_w_s, v_sgu_b_s=v_sgu_b_s, v_sgu_w_out=v_sgu_w_out, v_mla_w_dq_dkv=v_mla_w_dq_dkv, v_mla_q_norm_g=v_mla_q_norm_g, v_mla_kv_norm_g=v_mla_kv_norm_g, v_mla_w_uq=v_mla_w_uq, v_mla_w_ukv=v_mla_w_ukv, v_mla_w_o=v_mla_w_o, v_mlp_w1=v_mlp_w1, v_mlp_w2=v_mlp_w2, v_final_g=v_final_g)
    weights = {n: given[n] for n in TWIN_WEIGHTS}
    shared = {n: given[n] for n in SHARED_INPUTS}
    per_example = {n: given[n] for n in ['x', 'c', 'positions']}
    grad_fn = _jax.value_and_grad(_loss, argnums=(0, 1))

    def one_microbatch(ex, loss_target):
        ex = dict(ex)
        diff = ex.pop(TWIN_DIFF_INPUT)
        return grad_fn(weights, diff, {**shared, **ex}, loss_target)

    if N_MICROBATCH == 1:
        loss, (grad_w, grad_x) = one_microbatch(per_example, given["loss_target"])
    else:
        def body(carry, xs):
            loss_sum, grad_sum = carry
            l_k, (gw_k, gx_k) = one_microbatch(xs[0], xs[1])
            with _jax.named_scope("update"):
                return (loss_sum + l_k, _jax.tree.map(_jnp.add, grad_sum, gw_k)), gx_k

        init = (_jnp.zeros((), _jnp.float32), _jax.tree.map(_jnp.zeros_like, weights))
        (loss, grad_w), grad_x = _jax.lax.scan(body, init, (per_example, given["loss_target"]))
    with _jax.named_scope("update"):
        delta_w, new_m, new_v = {}, {}, {}
        for n in TWIN_WEIGHTS:
            delta_w[n], new_m[n], new_v[n] = _adamw(weights[n], grad_w[n], given["m_" + n], given["v_" + n])
    return (loss, grad_x, *[grad_w[n] for n in TWIN_WEIGHTS], *[delta_w[n] for n in TWIN_WEIGHTS],
            *[new_m[n] for n in TWIN_WEIGHTS], *[new_v[n] for n in TWIN_WEIGHTS])
```

```python
import math

import jax
import jax.numpy as jnp
from jax import lax
from jax.experimental import pallas as pl
from jax.experimental.pallas import tpu as pltpu

F32, BF16 = jnp.float32, jnp.bfloat16
MESH = pl.DeviceIdType.MESH

D_MODEL = 1024
DEPTH = 4
N_MIXERS = 3
POOL_WINDOWS = (2, 4, 8, 16)
POOL_GD = D_MODEL // len(POOL_WINDOWS)
POOL_HALO = 16
SGU_CHUNK = 128
SGU_W = D_MODEL
SGU_HD = 128
SGU_H = SGU_W // SGU_HD
MLA_H = 16
MLA_QL = 256
MLA_KVL = 128
MLA_NOPE = 128
MLA_ROPE = 64
MLA_V = 128
MLA_HP = 256
MLA_LATP = 512
ROPE_THETA = 10000.0
RMS_EPS = 1e-6
LN_EPS = 1e-5
SM_SCALE = (MLA_NOPE + MLA_ROPE) ** -0.5
NEG_INF = -1e30
ADAM_LR, ADAM_B1, ADAM_B2, ADAM_EPS, ADAM_WD, ADAM_STEP = 0.001, 0.9, 0.999, 1e-08, 0.01, 10
N_CHIPS = 4
N_DEV = 8
ROW_TILE = 512
ATT_TILE = 512


def _idx():
    return lax.axis_index("x"), lax.axis_index("y"), lax.axis_index("c")


def _mm(a, b, *, name, ta=False, tb=False, epi=None, extras=(), out_dtypes=(BF16,), tm=1024, tn=1024, tk=1024):
    if ta:
        K, M = a.shape
    else:
        M, K = a.shape
    if tb:
        N, Kb = b.shape
    else:
        Kb, N = b.shape
    assert K == Kb, (a.shape, b.shape, ta, tb)
    tm, tn, tk = min(tm, M), min(tn, N), min(tk, K)
    assert M % tm == 0 and N % tn == 0 and K % tk == 0, (M, N, K, tm, tn, tk)
    nk = K // tk
    a_spec = pl.BlockSpec((tk, tm), lambda i, j, k: (k, i)) if ta else pl.BlockSpec((tm, tk), lambda i, j, k: (i, k))
    b_spec = pl.BlockSpec((tn, tk), lambda i, j, k: (j, k)) if tb else pl.BlockSpec((tk, tn), lambda i, j, k: (k, j))
    ex_specs = []
    for arr, kind in extras:
        if kind == "mn":
            ex_specs.append(pl.BlockSpec((tm, tn), lambda i, j, k: (i, j)))
        elif kind == "n":
            ex_specs.append(pl.BlockSpec((1, tn), lambda i, j, k: (0, j)))
        else:
            ex_specs.append(pl.BlockSpec((tm, arr.shape[1]), lambda i, j, k: (i, 0)))
    n_ex, n_out = len(extras), len(out_dtypes)
    dims = (((0 if ta else 1,), (1 if tb else 0,)), ((), ()))

    def body(*refs):
        a_ref, b_ref = refs[0], refs[1]
        ex_refs = refs[2:2 + n_ex]
        out_refs = refs[2 + n_ex:2 + n_ex + n_out]
        part = lax.dot_general(a_ref[...].astype(BF16), b_ref[...].astype(BF16), dims, preferred_element_type=F32)

        def finish(acc):
            outs = epi(acc, *[r[...] for r in ex_refs]) if epi is not None else (acc,)
            for r, o in zip(out_refs, outs, strict=True):
                r[...] = o.astype(r.dtype)

        if nk == 1:
            finish(part)
        else:
            acc_ref = refs[-1]
            k = pl.program_id(2)

            @pl.when(k == 0)
            def _():
                acc_ref[...] = part

            @pl.when(k > 0)
            def _():
                acc_ref[...] += part

            @pl.when(k == nk - 1)
            def _():
                finish(acc_ref[...])

    outs = pl.pallas_call(
        body,
        name=name,
        grid=(M // tm, N // tn, nk),
        in_specs=[a_spec, b_spec, *ex_specs],
        out_specs=[pl.BlockSpec((tm, tn), lambda i, j, k: (i, j)) for _ in range(n_out)],
        out_shape=[jax.ShapeDtypeStruct((M, N), dt) for dt in out_dtypes],
        scratch_shapes=[pltpu.VMEM((tm, tn), F32)] if nk > 1 else [],
        compiler_params=pltpu.CompilerParams(dimension_semantics=("parallel", "parallel", "arbitrary")),
    )(a, b, *[arr for arr, _ in extras])
    return outs[0] if n_out == 1 else tuple(outs)


def _epi_residual(acc, x, g):
    return x + g * acc, acc


def _row_spec(tr, d):
    return pl.BlockSpec((tr, d), lambda i: (i, 0))


def _vec_spec(d):
    return pl.BlockSpec((1, d), lambda i: (0, 0))


def _colsum(v):
    return jnp.sum(v, axis=0, keepdims=True)


def _norm_mod_fwd(x, gain, sc, sh, out_dtype, name):
    T, D = x.shape
    tr = min(T, ROW_TILE)

    def body(x_ref, g_ref, sc_ref, sh_ref, o_ref):
        xv = x_ref[...]
        r = lax.rsqrt(jnp.mean(xv * xv, axis=-1, keepdims=True) + RMS_EPS)
        o_ref[...] = (((xv * r) * g_ref[...]) * (1.0 + sc_ref[...]) + sh_ref[...]).astype(o_ref.dtype)

    return pl.pallas_call(
        body, name=name, grid=(T // tr,),
        in_specs=[_row_spec(tr, D), _vec_spec(D), _vec_spec(D), _vec_spec(D)],
        out_specs=_row_spec(tr, D),
        out_shape=jax.ShapeDtypeStruct((T, D), out_dtype),
        compiler_params=pltpu.CompilerParams(dimension_semantics=("parallel",)),
    )(x, gain, sc, sh)


def _norm_mod_bwd(x, dh, dres, gain, sc, name):
    T, D = x.shape
    tr = min(T, ROW_TILE)

    def body(x_ref, dh_ref, dres_ref, g_ref, sc_ref, dx_ref, dg_ref, dsc_ref, dsh_ref):
        @pl.when(pl.program_id(0) == 0)
        def _():
            dg_ref[...] = jnp.zeros_like(dg_ref)
            dsc_ref[...] = jnp.zeros_like(dsc_ref)
            dsh_ref[...] = jnp.zeros_like(dsh_ref)

        xv = x_ref[...]
        r = lax.rsqrt(jnp.mean(xv * xv, axis=-1, keepdims=True) + RMS_EPS)
        xn = xv * r
        dhv = dh_ref[...].astype(F32)
        dsh_ref[...] += _colsum(dhv)
        dsc_ref[...] += _colsum(dhv * (xn * g_ref[...]))
        dt = dhv * (1.0 + sc_ref[...])
        dg_ref[...] += _colsum(dt * xn)
        dxn = dt * g_ref[...]
        dx_ref[...] = dres_ref[...] + r * (dxn - xn * jnp.mean(dxn * xn, axis=-1, keepdims=True))

    return pl.pallas_call(
        body, name=name, grid=(T // tr,),
        in_specs=[_row_spec(tr, D), _row_spec(tr, D), _row_spec(tr, D), _vec_spec(D), _vec_spec(D)],
        out_specs=[_row_spec(tr, D), _vec_spec(D), _vec_spec(D), _vec_spec(D)],
        out_shape=[jax.ShapeDtypeStruct((T, D), F32)] + [jax.ShapeDtypeStruct((1, D), F32)] * 3,
        compiler_params=pltpu.CompilerParams(dimension_semantics=("arbitrary",)),
    )(x, dh, dres, gain, sc)


def _resid_bwd(dx, y, g, name):
    T, D = dx.shape
    tr = min(T, ROW_TILE)

    def body(dx_ref, y_ref, g_ref, dy_ref, q_ref):
        @pl.when(pl.program_id(0) == 0)
        def _():
            q_ref[...] = jnp.zeros_like(q_ref)

        dxv = dx_ref[...]
        dy_ref[...] = (g_ref[...] * dxv).astype(BF16)
        q_ref[...] += _colsum(dxv * y_ref[...].astype(F32))

    return pl.pallas_call(
        body, name=name, grid=(T // tr,),
        in_specs=[_row_spec(tr, D), _row_spec(tr, D), _vec_spec(D)],
        out_specs=[_row_spec(tr, D), _vec_spec(D)],
        out_shape=[jax.ShapeDtypeStruct((T, D), BF16), jax.ShapeDtypeStruct((1, D), F32)],
        compiler_params=pltpu.CompilerParams(dimension_semantics=("arbitrary",)),
    )(dx, y, g)


def _loss_head(x, target, gain, name):
    T, D = x.shape
    tr = min(T, ROW_TILE)

    def body(x_ref, t_ref, g_ref, loss_ref, dx_ref, dg_ref):
        @pl.when(pl.program_id(0) == 0)
        def _():
            loss_ref[...] = jnp.zeros_like(loss_ref)
            dg_ref[...] = jnp.zeros_like(dg_ref)

        xv = x_ref[...]
        r = lax.rsqrt(jnp.mean(xv * xv, axis=-1, keepdims=True) + RMS_EPS)
        xn = xv * r
        err = xn * g_ref[...] - t_ref[...]
        row = jnp.mean(err * err, axis=-1, keepdims=True)
        loss_ref[...] += 0.5 * jnp.sum(row, axis=0, keepdims=True)
        dy = err * (1.0 / D)
        dg_ref[...] += _colsum(dy * xn)
        dxn = dy * g_ref[...]
        dx_ref[...] = r * (dxn - xn * jnp.mean(dxn * xn, axis=-1, keepdims=True))

    return pl.pallas_call(
        body, name=name, grid=(T // tr,),
        in_specs=[_row_spec(tr, D), _row_spec(tr, D), _vec_spec(D)],
        out_specs=[_vec_spec(128), _row_spec(tr, D), _vec_spec(D)],
        out_shape=[jax.ShapeDtypeStruct((1, 128), F32), jax.ShapeDtypeStruct((T, D), F32), jax.ShapeDtypeStruct((1, D), F32)],
        compiler_params=pltpu.CompilerParams(dimension_semantics=("arbitrary",)),
    )(x, target, gain)


def _pool_fwd(h, w, scale, x, g1, name):
    T, D = h.shape
    tr = min(T, ROW_TILE)

    def body(h_ref, w_ref, sc_ref, x_ref, g_ref, x2_ref, pooled_ref, ypre_ref, halo_ref):
        i = pl.program_id(0)

        @pl.when(i == 0)
        def _():
            halo_ref[...] = jnp.zeros_like(halo_ref)

        hv = h_ref[...]
        buf = jnp.concatenate([halo_ref[...], hv], axis=0)
        halo_ref[...] = hv[tr - POOL_HALO:, :]
        t = (i * tr + lax.broadcasted_iota(jnp.int32, (tr, 1), 0)).astype(F32)
        for gi, win in enumerate(POOL_WINDOWS):
            cols = slice(gi * POOL_GD, (gi + 1) * POOL_GD)
            val = buf[:, cols]
            sh = 1
            while sh < win:
                val = val + pltpu.roll(val, sh, axis=0)
                sh *= 2
            pooled = val[POOL_HALO:, :] / jnp.minimum(t + 1.0, float(win)) - hv[:, cols]
            pb = pooled.astype(BF16)
            pooled_ref[:, cols] = pb
            yp = jnp.dot(pb, w_ref[gi], preferred_element_type=F32)
            ypre_ref[:, cols] = yp.astype(BF16)
            x2_ref[:, cols] = x_ref[:, cols] + g_ref[:, cols] * (yp * sc_ref[:, cols])

    return pl.pallas_call(
        body, name=name, grid=(T // tr,),
        in_specs=[_row_spec(tr, D), pl.BlockSpec(w.shape, lambda i: (0, 0, 0)), _vec_spec(D), _row_spec(tr, D), _vec_spec(D)],
        out_specs=[_row_spec(tr, D)] * 3,
        out_shape=[jax.ShapeDtypeStruct((T, D), F32), jax.ShapeDtypeStruct((T, D), BF16), jax.ShapeDtypeStruct((T, D), BF16)],
        scratch_shapes=[pltpu.VMEM((POOL_HALO, D), F32)],
        compiler_params=pltpu.CompilerParams(dimension_semantics=("arbitrary",)),
    )(h, w, scale, x, g1)


def _pool_bwd(dy, pooled, w, scale, g1, q, name):
    T, D = dy.shape
    tr = min(T, ROW_TILE)
    nt = T // tr
    ltot = tr + POOL_HALO

    def body(dy_ref, pooled_ref, w_ref, sc_ref, g_ref, q_ref, dh_ref, dw_ref, dsc_ref, dg_ref, halo_ref):
        i = pl.program_id(0)

        @pl.when(i == 0)
        def _():
            halo_ref[...] = jnp.zeros_like(halo_ref)
            dw_ref[...] = jnp.zeros_like(dw_ref)
            dsc_ref[...] = g_ref[...] * q_ref[...]
            dg_ref[...] = sc_ref[...] * q_ref[...]

        t = ((nt - 1 - i) * tr + lax.broadcasted_iota(jnp.int32, (tr, 1), 0)).astype(F32)
        for gi, win in enumerate(POOL_WINDOWS):
            cols = slice(gi * POOL_GD, (gi + 1) * POOL_GD)
            dyb = (dy_ref[:, cols].astype(F32) * sc_ref[:, cols]).astype(BF16)
            dw_ref[gi] += lax.dot_general(pooled_ref[:, cols], dyb, (((0,), (0,)), ((), ())), preferred_element_type=F32)
            dpool = lax.dot_general(dyb, w_ref[gi], (((1,), (1,)), ((), ())), preferred_element_type=F32)
            qv = dpool / jnp.minimum(t + 1.0, float(win))
            val = jnp.concatenate([qv, halo_ref[:, cols]], axis=0)
            halo_ref[:, cols] = qv[:POOL_HALO, :]
            sh = 1
            while sh < win:
                val = val + pltpu.roll(val, ltot - sh, axis=0)
                sh *= 2
            dh_ref[:, cols] = val[:tr, :] - dpool

    rev = pl.BlockSpec((tr, D), lambda i: (nt - 1 - i, 0))
    return pl.pallas_call(
        body, name=name, grid=(nt,),
        in_specs=[rev, rev, pl.BlockSpec(w.shape, lambda i: (0, 0, 0)), _vec_spec(D), _vec_spec(D), _vec_spec(D)],
        out_specs=[rev, pl.BlockSpec(w.shape, lambda i: (0, 0, 0)), _vec_spec(D), _vec_spec(D)],
        out_shape=[jax.ShapeDtypeStruct((T, D), F32), jax.ShapeDtypeStruct(w.shape, F32),
                   jax.ShapeDtypeStruct((1, D), F32), jax.ShapeDtypeStruct((1, D), F32)],
        scratch_shapes=[pltpu.VMEM((POOL_HALO, D), F32)],
        compiler_params=pltpu.CompilerParams(dimension_semantics=("arbitrary",)),
    )(dy, pooled, w, scale, g1, q)


_INV_SQRT2 = 0.7071067811865476
_INV_SQRT2PI = 0.3989422804014327


def _gelu(v):
    return 0.5 * v * (1.0 + lax.erf(v * _INV_SQRT2))


def _gelu_grad(v):
    return 0.5 * (1.0 + lax.erf(v * _INV_SQRT2)) + v * jnp.exp(-0.5 * v * v) * _INV_SQRT2PI


def _sgu_ln(v, g, b):
    mu = jnp.mean(v, axis=-1, keepdims=True)
    xc = v - mu
    rstd = lax.rsqrt(jnp.mean(xc * xc, axis=-1, keepdims=True) + LN_EPS)
    xh = xc * rstd
    return xh, rstd, xh * g + b


def _tril_mask():
    return lax.broadcasted_iota(jnp.int32, (SGU_CHUNK, SGU_CHUNK), 0) >= lax.broadcasted_iota(jnp.int32, (SGU_CHUNK, SGU_CHUNK), 1)


SGU_TILE = 256


def _sgu_gate_fwd(zz, ln_g, ln_b, ws, bs_t, name):
    T = zz.shape[0]
    ts = min(T, SGU_TILE)

    def body(zz_ref, g_ref, b_ref, ws_ref, bs_ref, out_ref):
        z = _gelu(zz_ref[...])
        u = z[:, :SGU_W]
        _, _, vn = _sgu_ln(z[:, SGU_W:], g_ref[...], b_ref[...])
        vb = vn.astype(BF16)
        tril = _tril_mask()
        for hh in range(SGU_H):
            wm = jnp.where(tril, ws_ref[hh], 0.0).astype(BF16)
            bcol = bs_ref[:, hh:hh + 1]
            cs = slice(hh * SGU_HD, (hh + 1) * SGU_HD)
            for j in range(ts // SGU_CHUNK):
                rs = slice(j * SGU_CHUNK, (j + 1) * SGU_CHUNK)
                mixed = jnp.dot(wm, vb[rs, cs], preferred_element_type=F32) + bcol
                out_ref[rs, cs] = (u[rs, cs] * mixed).astype(BF16)

    return pl.pallas_call(
        body, name=name, grid=(T // ts,),
        in_specs=[_row_spec(ts, 2 * SGU_W), _vec_spec(SGU_W), _vec_spec(SGU_W),
                  pl.BlockSpec(ws.shape, lambda i: (0, 0, 0)), pl.BlockSpec(bs_t.shape, lambda i: (0, 0))],
        out_specs=_row_spec(ts, SGU_W),
        out_shape=jax.ShapeDtypeStruct((T, SGU_W), BF16),
        compiler_params=pltpu.CompilerParams(dimension_semantics=("parallel",)),
    )(zz, ln_g, ln_b, ws, bs_t)


def _sgu_gate_bwd(zz, dgated, ln_g, ln_b, ws, bs_t, name):
    T = zz.shape[0]
    ts = min(T, SGU_TILE)
    nt = T // ts

    def body(zz_ref, dg_ref, g_ref, b_ref, ws_ref, bs_ref, dzz_ref, dws_ref, dbs_ref, dlg_ref, dlb_ref, dlo_ref, dmx_ref):
        i = pl.program_id(0)

        @pl.when(i == 0)
        def _():
            dws_ref[...] = jnp.zeros_like(dws_ref)
            dmx_ref[...] = jnp.zeros_like(dmx_ref)
            dlg_ref[...] = jnp.zeros_like(dlg_ref)
            dlb_ref[...] = jnp.zeros_like(dlb_ref)

        zzv = zz_ref[...]
        z = _gelu(zzv)
        u = z[:, :SGU_W]
        xh, rstd, vn = _sgu_ln(z[:, SGU_W:], g_ref[...], b_ref[...])
        vb = vn.astype(BF16)
        dgv = dg_ref[...].astype(F32)
        tril = _tril_mask()
        for hh in range(SGU_H):
            wm = jnp.where(tril, ws_ref[hh], 0.0).astype(BF16)
            bcol = bs_ref[:, hh:hh + 1]
            cs = slice(hh * SGU_HD, (hh + 1) * SGU_HD)
            for j in range(ts // SGU_CHUNK):
                rs = slice(j * SGU_CHUNK, (j + 1) * SGU_CHUNK)
                mixed = jnp.dot(wm, vb[rs, cs], preferred_element_type=F32) + bcol
                dmixed = dgv[rs, cs] * u[rs, cs]
                dzz_ref[rs, cs] = (dgv[rs, cs] * mixed * _gelu_grad(zzv[rs, cs])).astype(BF16)
                dmb = dmixed.astype(BF16)
                dws_ref[hh] += lax.dot_general(dmb, vb[rs, cs], (((1,), (1,)), ((), ())), preferred_element_type=F32)
                dmx_ref[hh] += dmixed
                dlo_ref[rs, cs] = lax.dot_general(wm, dmb, (((0,), (0,)), ((), ())), preferred_element_type=F32)
        dlo = dlo_ref[...]
        dlg_ref[...] += _colsum(dlo * xh)
        dlb_ref[...] += _colsum(dlo)
        dxh = dlo * g_ref[...]
        dv = rstd * (dxh - jnp.mean(dxh, axis=-1, keepdims=True) - xh * jnp.mean(dxh * xh, axis=-1, keepdims=True))
        dzz_ref[:, SGU_W:] = (dv * _gelu_grad(zzv[:, SGU_W:])).astype(BF16)

        @pl.when(i == nt - 1)
        def _():
            tril_f = tril.astype(F32)
            for hh in range(SGU_H):
                dws_ref[hh] = dws_ref[hh] * tril_f
                dbs_ref[hh] = jnp.broadcast_to(jnp.sum(dmx_ref[hh], axis=-1, keepdims=True), (SGU_CHUNK, SGU_HD))

    full3 = pl.BlockSpec(ws.shape, lambda i: (0, 0, 0))
    return pl.pallas_call(
        body, name=name, grid=(nt,),
        in_specs=[_row_spec(ts, 2 * SGU_W), _row_spec(ts, SGU_W), _vec_spec(SGU_W), _vec_spec(SGU_W), full3,
                  pl.BlockSpec(bs_t.shape, lambda i: (0, 0))],
        out_specs=[_row_spec(ts, 2 * SGU_W), full3, full3, _vec_spec(SGU_W), _vec_spec(SGU_W)],
        out_shape=[jax.ShapeDtypeStruct((T, 2 * SGU_W), BF16), jax.ShapeDtypeStruct(ws.shape, F32), jax.ShapeDtypeStruct(ws.shape, F32),
                   jax.ShapeDtypeStruct((1, SGU_W), F32), jax.ShapeDtypeStruct((1, SGU_W), F32)],
        scratch_shapes=[pltpu.VMEM((ts, SGU_W), F32), pltpu.VMEM(ws.shape, F32)],
        compiler_params=pltpu.CompilerParams(dimension_semantics=("arbitrary",)),
    )(zz, dgated, ln_g, ln_b, ws, bs_t)


def _rope_fwd(blk, cc, sa, sb):
    return blk * cc + pltpu.roll(blk, 96, axis=1) * sa + pltpu.roll(blk, 32, axis=1) * sb


def _rope_bwd(d, cc, sa, sb):
    return d * cc + pltpu.roll(d * sa, 32, axis=1) + pltpu.roll(d * sb, 96, axis=1)


def _rms(v, g):
    r = lax.rsqrt(jnp.mean(v * v, axis=-1, keepdims=True) + RMS_EPS)
    vn = v * r
    return vn, r, vn * g


def _rms_bwd(dy, vn, r, g):
    dvn = dy * g
    return r * (dvn - vn * jnp.mean(dvn * vn, axis=-1, keepdims=True))


MLA_TILE = 256
_KV0 = MLA_QL
_KR0 = MLA_QL + MLA_KVL


def _mla_lat_fwd(lat, qg, kvg, cc, sa, sb, name):
    T = lat.shape[0]
    tr = min(T, ROW_TILE)

    def body(lat_ref, qg_ref, kvg_ref, cc_ref, sa_ref, sb_ref, cq_ref, ckv_ref, kr_ref):
        lv = lat_ref[...]
        cq_ref[...] = _rms(lv[:, :_KV0], qg_ref[...])[2].astype(BF16)
        ckv_ref[...] = _rms(lv[:, _KV0:_KR0], kvg_ref[...])[2].astype(BF16)
        kr_ref[...] = _rope_fwd(lv[:, _KR0:], cc_ref[...], sa_ref[...], sb_ref[...])

    return pl.pallas_call(
        body, name=name, grid=(T // tr,),
        in_specs=[_row_spec(tr, MLA_LATP), _vec_spec(MLA_QL), _vec_spec(MLA_KVL), _row_spec(tr, 128), _row_spec(tr, 128), _row_spec(tr, 128)],
        out_specs=[_row_spec(tr, MLA_QL), _row_spec(tr, MLA_KVL), _row_spec(tr, 128)],
        out_shape=[jax.ShapeDtypeStruct((T, MLA_QL), BF16), jax.ShapeDtypeStruct((T, MLA_KVL), BF16), jax.ShapeDtypeStruct((T, 128), F32)],
        compiler_params=pltpu.CompilerParams(dimension_semantics=("parallel",)),
    )(lat, qg, kvg, cc, sa, sb)


def _mla_lat_bwd(lat, dcqn, dckvn, dkrot, qg, kvg, cc, sa, sb, name):
    T = lat.shape[0]
    tr = min(T, ROW_TILE)

    def body(lat_ref, dcq_ref, dckv_ref, dkr_ref, qg_ref, kvg_ref, cc_ref, sa_ref, sb_ref, dlat_ref, dqg_ref, dkvg_ref):
        @pl.when(pl.program_id(0) == 0)
        def _():
            dqg_ref[...] = jnp.zeros_like(dqg_ref)
            dkvg_ref[...] = jnp.zeros_like(dkvg_ref)

        lv = lat_ref[...]
        qn, qr, _ = _rms(lv[:, :_KV0], qg_ref[...])
        kn, kr, _ = _rms(lv[:, _KV0:_KR0], kvg_ref[...])
        dcq = dcq_ref[...]
        dckv = dckv_ref[...]
        dqg_ref[...] += _colsum(dcq * qn)
        dkvg_ref[...] += _colsum(dckv * kn)
        dlat_ref[:, :_KV0] = _rms_bwd(dcq, qn, qr, qg_ref[...]).astype(BF16)
        dlat_ref[:, _KV0:_KR0] = _rms_bwd(dckv, kn, kr, kvg_ref[...]).astype(BF16)
        dlat_ref[:, _KR0:] = _rope_bwd(dkr_ref[...], cc_ref[...], sa_ref[...], sb_ref[...]).astype(BF16)

    return pl.pallas_call(
        body, name=name, grid=(T // tr,),
        in_specs=[_row_spec(tr, MLA_LATP), _row_spec(tr, MLA_QL), _row_spec(tr, MLA_KVL), _row_spec(tr, 128),
                  _vec_spec(MLA_QL), _vec_spec(MLA_KVL), _row_spec(tr, 128), _row_spec(tr, 128), _row_spec(tr, 128)],
        out_specs=[_row_spec(tr, MLA_LATP), _vec_spec(MLA_QL), _vec_spec(MLA_KVL)],
        out_shape=[jax.ShapeDtypeStruct((T, MLA_LATP), BF16), jax.ShapeDtypeStruct((1, MLA_QL), F32), jax.ShapeDtypeStruct((1, MLA_KVL), F32)],
        compiler_params=pltpu.CompilerParams(dimension_semantics=("arbitrary",)),
    )(lat, dcqn, dckvn, dkrot, qg, kvg, cc, sa, sb)


def _mla_prep(qpad, kv, krot, cc, sa, sb, name):
    T = qpad.shape[0]
    tr = min(T, MLA_TILE)
    HW = MLA_H * MLA_HP

    def body(q_ref, kv_ref, kr_ref, cc_ref, sa_ref, sb_ref, qo_ref, ko_ref, vo_ref):
        cc, sa, sb = cc_ref[...], sa_ref[...], sb_ref[...]
        krb = kr_ref[...].astype(BF16)
        for hh in range(MLA_H):
            a, m, b = hh * MLA_HP, hh * MLA_HP + MLA_NOPE, (hh + 1) * MLA_HP
            qo_ref[:, a:m] = q_ref[:, a:m].astype(BF16)
            qo_ref[:, m:b] = _rope_fwd(q_ref[:, m:b], cc, sa, sb).astype(BF16)
            ko_ref[:, a:m] = kv_ref[:, a:m].astype(BF16)
            ko_ref[:, m:b] = krb
            vo_ref[:, hh * MLA_V:(hh + 1) * MLA_V] = kv_ref[:, m:b].astype(BF16)

    return pl.pallas_call(
        body, name=name, grid=(T // tr,),
        in_specs=[_row_spec(tr, HW), _row_spec(tr, HW), _row_spec(tr, 128), _row_spec(tr, 128), _row_spec(tr, 128), _row_spec(tr, 128)],
        out_specs=[_row_spec(tr, HW), _row_spec(tr, HW), _row_spec(tr, MLA_H * MLA_V)],
        out_shape=[jax.ShapeDtypeStruct((T, HW), BF16), jax.ShapeDtypeStruct((T, HW), BF16), jax.ShapeDtypeStruct((T, MLA_H * MLA_V), BF16)],
        compiler_params=pltpu.CompilerParams(dimension_semantics=("parallel",)),
    )(qpad, kv, krot, cc, sa, sb)


def _mla_prep_bwd(dq, dk, dv, cc, sa, sb, name):
    T = dq.shape[0]
    tr = min(T, MLA_TILE)
    HW = MLA_H * MLA_HP

    def body(dq_ref, dk_ref, dv_ref, cc_ref, sa_ref, sb_ref, dqp_ref, dkv_ref, dkr_ref):
        cc, sa, sb = cc_ref[...], sa_ref[...], sb_ref[...]
        acc = jnp.zeros((tr, 128), F32)
        for hh in range(MLA_H):
            a, m, b = hh * MLA_HP, hh * MLA_HP + MLA_NOPE, (hh + 1) * MLA_HP
            dqp_ref[:, a:m] = dq_ref[:, a:m].astype(BF16)
            dqp_ref[:, m:b] = _rope_bwd(dq_ref[:, m:b], cc, sa, sb).astype(BF16)
            dkv_ref[:, a:m] = dk_ref[:, a:m].astype(BF16)
            dkv_ref[:, m:b] = dv_ref[:, hh * MLA_V:(hh + 1) * MLA_V].astype(BF16)
            acc = acc + dk_ref[:, m:b]
        dkr_ref[...] = acc

    return pl.pallas_call(
        body, name=name, grid=(T // tr,),
        in_specs=[_row_spec(tr, HW), _row_spec(tr, HW), _row_spec(tr, MLA_H * MLA_V), _row_spec(tr, 128), _row_spec(tr, 128), _row_spec(tr, 128)],
        out_specs=[_row_spec(tr, HW), _row_spec(tr, HW), _row_spec(tr, 128)],
        out_shape=[jax.ShapeDtypeStruct((T, HW), BF16), jax.ShapeDtypeStruct((T, HW), BF16), jax.ShapeDtypeStruct((T, 128), F32)],
        compiler_params=pltpu.CompilerParams(dimension_semantics=("parallel",)),
    )(dq, dk, dv, cc, sa, sb)


def _causal_mask(i, j, tq, tk):
    qi = i * tq + lax.broadcasted_iota(jnp.int32, (tq, tk), 0)
    kj = j * tk + lax.broadcasted_iota(jnp.int32, (tq, tk), 1)
    return kj <= qi


_NT = (((1,), (1,)), ((), ()))
_TN = (((0,), (0,)), ((), ()))


def _attn_fwd(q, k, v, name):
    T = q.shape[0]
    tq = tk = min(T, ATT_TILE)
    nq, nk = T // tq, T // tk

    def body(q_ref, k_ref, v_ref, o_ref, lse_ref, m_ref, l_ref, acc_ref):
        i, j = pl.program_id(1), pl.program_id(2)

        @pl.when(j == 0)
        def _():
            m_ref[...] = jnp.full_like(m_ref, NEG_INF)
            l_ref[...] = jnp.zeros_like(l_ref)
            acc_ref[...] = jnp.zeros_like(acc_ref)

        @pl.when(j <= i)
        def _():
            s = lax.dot_general(q_ref[...], k_ref[...], _NT, preferred_element_type=F32)
            s = jnp.where(_causal_mask(i, j, tq, tk), s * SM_SCALE, NEG_INF)
            m_prev = m_ref[...]
            m_new = jnp.maximum(m_prev, jnp.max(s, axis=-1, keepdims=True))
            alpha = jnp.exp(m_prev - m_new)
            p = jnp.exp(s - m_new[:, :1])
            l_ref[...] = alpha * l_ref[...] + jnp.sum(p, axis=-1, keepdims=True)
            acc_ref[...] = alpha * acc_ref[...] + jnp.dot(p.astype(BF16), v_ref[...], preferred_element_type=F32)
            m_ref[...] = m_new

        @pl.when(j == i)
        def _():
            o_ref[...] = (acc_ref[...] / l_ref[...]).astype(BF16)
            lse_ref[...] = m_ref[...] + jnp.log(l_ref[...])

    return pl.pallas_call(
        body, name=name, grid=(MLA_H, nq, nk),
        in_specs=[pl.BlockSpec((tq, MLA_HP), lambda h, i, j: (i, h)),
                  pl.BlockSpec((tk, MLA_HP), lambda h, i, j: (jnp.minimum(j, i), h)),
                  pl.BlockSpec((tk, MLA_V), lambda h, i, j: (jnp.minimum(j, i), h))],
        out_specs=[pl.BlockSpec((tq, MLA_V), lambda h, i, j: (i, h)),
                   pl.BlockSpec((None, tq, 128), lambda h, i, j: (h, i, 0))],
        out_shape=[jax.ShapeDtypeStruct((T, MLA_H * MLA_V), BF16), jax.ShapeDtypeStruct((MLA_H, T, 128), F32)],
        scratch_shapes=[pltpu.VMEM((tq, 128), F32), pltpu.VMEM((tq, 128), F32), pltpu.VMEM((tq, MLA_V), F32)],
        compiler_params=pltpu.CompilerParams(dimension_semantics=("parallel", "parallel", "arbitrary")),
    )(q, k, v)


def _attn_delta(do, o, name):
    T = do.shape[0]
    tr = min(T, ROW_TILE)

    def body(do_ref, o_ref, d_ref):
        for hh in range(MLA_H):
            cs = slice(hh * MLA_V, (hh + 1) * MLA_V)
            s = jnp.sum(do_ref[:, cs].astype(F32) * o_ref[:, cs].astype(F32), axis=-1, keepdims=True)
            d_ref[hh] = jnp.broadcast_to(s, (tr, 128))

    return pl.pallas_call(
        body, name=name, grid=(T // tr,),
        in_specs=[_row_spec(tr, MLA_H * MLA_V), _row_spec(tr, MLA_H * MLA_V)],
        out_specs=pl.BlockSpec((MLA_H, tr, 128), lambda i: (0, i, 0)),
        out_shape=jax.ShapeDtypeStruct((MLA_H, T, 128), F32),
        compiler_params=pltpu.CompilerParams(dimension_semantics=("parallel",)),
    )(do, o)


def _attn_p_ds(q, k, v, do, lse, delta, i, j, tq, tk):
    s = lax.dot_general(q, k, _NT, preferred_element_type=F32)
    s = jnp.where(_causal_mask(i, j, tq, tk), s * SM_SCALE, NEG_INF)
    p = jnp.exp(s - lse[:, :1])
    dp = lax.dot_general(do, v, _NT, preferred_element_type=F32)
    ds = p * (dp - delta[:, :1]) * SM_SCALE
    return p, ds


def _attn_bwd_dq(q, k, v, do, lse, delta, name):
    T = q.shape[0]
    tq = tk = min(T, ATT_TILE)
    nq, nk = T // tq, T // tk

    def body(q_ref, k_ref, v_ref, do_ref, lse_ref, dl_ref, dq_ref, acc_ref):
        i, j = pl.program_id(1), pl.program_id(2)

        @pl.when(j == 0)
        def _():
            acc_ref[...] = jnp.zeros_like(acc_ref)

        @pl.when(j <= i)
        def _():
            _, ds = _attn_p_ds(q_ref[...], k_ref[...], v_ref[...], do_ref[...], lse_ref[...], dl_ref[...], i, j, tq, tk)
            acc_ref[...] += jnp.dot(ds.astype(BF16), k_ref[...], preferred_element_type=F32)

        @pl.when(j == i)
        def _():
            dq_ref[...] = acc_ref[...]

    kv_idx = lambda h, i, j: (jnp.minimum(j, i), h)
    return pl.pallas_call(
        body, name=name, grid=(MLA_H, nq, nk),
        in_specs=[pl.BlockSpec((tq, MLA_HP), lambda h, i, j: (i, h)), pl.BlockSpec((tk, MLA_HP), kv_idx), pl.BlockSpec((tk, MLA_V), kv_idx),
                  pl.BlockSpec((tq, MLA_V), lambda h, i, j: (i, h)),
                  pl.BlockSpec((None, tq, 128), lambda h, i, j: (h, i, 0)), pl.BlockSpec((None, tq, 128), lambda h, i, j: (h, i, 0))],
        out_specs=pl.BlockSpec((tq, MLA_HP), lambda h, i, j: (i, h)),
        out_shape=jax.ShapeDtypeStruct((T, MLA_H * MLA_HP), F32),
        scratch_shapes=[pltpu.VMEM((tq, MLA_HP), F32)],
        compiler_params=pltpu.CompilerParams(dimension_semantics=("parallel", "parallel", "arbitrary")),
    )(q, k, v, do, lse, delta)


def _attn_bwd_dkv(q, k, v, do, lse, delta, name):
    T = q.shape[0]
    tq = tk = min(T, ATT_TILE)
    nq, nk = T // tq, T // tk

    def body(q_ref, k_ref, v_ref, do_ref, lse_ref, dl_ref, dk_ref, dv_ref, dk_acc, dv_acc):
        j, i = pl.program_id(1), pl.program_id(2)

        @pl.when(i == 0)
        def _():
            dk_acc[...] = jnp.zeros_like(dk_acc)
            dv_acc[...] = jnp.zeros_like(dv_acc)

        @pl.when(i >= j)
        def _():
            p, ds = _attn_p_ds(q_ref[...], k_ref[...], v_ref[...], do_ref[...], lse_ref[...], dl_ref[...], i, j, tq, tk)
            dv_acc[...] += lax.dot_general(p.astype(BF16), do_ref[...], _TN, preferred_element_type=F32)
            dk_acc[...] += lax.dot_general(ds.astype(BF16), q_ref[...], _TN, preferred_element_type=F32)

        @pl.when(i == nq - 1)
        def _():
            dk_ref[...] = dk_acc[...]
            dv_ref[...] = dv_acc[...]

    q_idx = lambda h, j, i: (jnp.maximum(i, j), h)
    st_idx = lambda h, j, i: (h, jnp.maximum(i, j), 0)
    return pl.pallas_call(
        body, name=name, grid=(MLA_H, nk, nq),
        in_specs=[pl.BlockSpec((tq, MLA_HP), q_idx), pl.BlockSpec((tk, MLA_HP), lambda h, j, i: (j, h)), pl.BlockSpec((tk, MLA_V), lambda h, j, i: (j, h)),
                  pl.BlockSpec((tq, MLA_V), q_idx), pl.BlockSpec((None, tq, 128), st_idx), pl.BlockSpec((None, tq, 128), st_idx)],
        out_specs=[pl.BlockSpec((tk, MLA_HP), lambda h, j, i: (j, h)), pl.BlockSpec((tk, MLA_V), lambda h, j, i: (j, h))],
        out_shape=[jax.ShapeDtypeStruct((T, MLA_H * MLA_HP), F32), jax.ShapeDtypeStruct((T, MLA_H * MLA_V), F32)],
        scratch_shapes=[pltpu.VMEM((tk, MLA_HP), F32), pltpu.VMEM((tk, MLA_V), F32)],
        compiler_params=pltpu.CompilerParams(dimension_semantics=("parallel", "parallel", "arbitrary")),
    )(q, k, v, do, lse, delta)


ADA_TN = 512


def _silu(v):
    return v * (1.0 / (1.0 + jnp.exp(-v)))


def _ada_fwd(c_all, ada_w, ada_b_loc, name):
    L, D, Nc = ada_w.shape
    B = c_all.shape[0]

    def body(c_ref, w_ref, b_ref, o_ref):
        ca = _silu(c_ref[...]).astype(BF16)
        o_ref[...] = jnp.dot(ca, w_ref[...].astype(BF16), preferred_element_type=F32) + b_ref[...]

    return pl.pallas_call(
        body, name=name, grid=(L, Nc // ADA_TN),
        in_specs=[pl.BlockSpec((B, D), lambda l, n: (0, 0)), pl.BlockSpec((None, D, ADA_TN), lambda l, n: (l, 0, n)),
                  pl.BlockSpec((None, 1, ADA_TN), lambda l, n: (l, 0, n))],
        out_specs=pl.BlockSpec((None, B, ADA_TN), lambda l, n: (l, 0, n)),
        out_shape=jax.ShapeDtypeStruct((L, B, Nc), F32),
        compiler_params=pltpu.CompilerParams(dimension_semantics=("parallel", "parallel")),
    )(c_all, ada_w, ada_b_loc)


def _ada_bwd(c_all_t, dmod_loc, name):
    D, B = c_all_t.shape
    L, _, Nc = dmod_loc.shape

    def body(c_ref, d_ref, o_ref):
        ca = _silu(c_ref[...])
        dv = d_ref[...]
        acc = ca[:, 0:1] * dv[0:1, :]
        for b in range(1, B):
            acc = acc + ca[:, b:b + 1] * dv[b:b + 1, :]
        o_ref[...] = acc

    return pl.pallas_call(
        body, name=name, grid=(L, Nc // ADA_TN),
        in_specs=[pl.BlockSpec((D, B), lambda l, n: (0, 0)), pl.BlockSpec((None, B, ADA_TN), lambda l, n: (l, 0, n))],
        out_specs=pl.BlockSpec((None, D, ADA_TN), lambda l, n: (l, 0, n)),
        out_shape=jax.ShapeDtypeStruct((L, D, Nc), F32),
        compiler_params=pltpu.CompilerParams(dimension_semantics=("parallel", "parallel")),
    )(c_all_t, dmod_loc)


def _sum_lead(v, name, out_dtype=F32):
    n, R, C = v.shape
    tr = R
    for cand in (512, 256, 128, 64, 32, 16):
        if R % cand == 0 and cand * C * 4 <= (2 << 20):
            tr = cand
            break

    def body(v_ref, o_ref):
        acc = v_ref[0].astype(F32)
        for s in range(1, n):
            acc = acc + v_ref[s].astype(F32)
        o_ref[...] = acc.astype(o_ref.dtype)

    return pl.pallas_call(
        body, name=name, grid=(R // tr,),
        in_specs=[pl.BlockSpec((n, tr, C), lambda i: (0, i, 0))],
        out_specs=pl.BlockSpec((tr, C), lambda i: (i, 0)),
        out_shape=jax.ShapeDtypeStruct((R, C), out_dtype),
        compiler_params=pltpu.CompilerParams(dimension_semantics=("parallel",)),
    )(v)


_ADAM_C1 = 1.0 - ADAM_B1 ** ADAM_STEP
_ADAM_C2 = 1.0 - ADAM_B2 ** ADAM_STEP


def _adamw(w, g, m, v, name):
    shape = w.shape
    C = shape[-1]
    R = math.prod(shape[:-1]) if len(shape) > 1 else 1
    w2, g2, m2, v2 = (a.reshape(R, C) for a in (w, g, m, v))
    tr = R
    for cand in (1024, 512, 256, 128, 64, 32, 16, 8):
        if R % cand == 0 and cand * C * 4 <= (1 << 20):
            tr = cand
            break

    def body(w_ref, g_ref, m_ref, v_ref, d_ref, nm_ref, nv_ref):
        gv = g_ref[...]
        mn = ADAM_B1 * m_ref[...] + (1.0 - ADAM_B1) * gv
        vn = ADAM_B2 * v_ref[...] + (1.0 - ADAM_B2) * (gv * gv)
        nm_ref[...] = mn
        nv_ref[...] = vn
        m_hat = mn / _ADAM_C1
        v_hat = vn / _ADAM_C2
        d_ref[...] = -ADAM_LR * (m_hat / (jnp.sqrt(v_hat) + ADAM_EPS) + ADAM_WD * w_ref[...])

    spec = pl.BlockSpec((tr, C), lambda i: (i, 0))
    outs = pl.pallas_call(
        body, name=name, grid=(R // tr,),
        in_specs=[spec] * 4, out_specs=[spec] * 3,
        out_shape=[jax.ShapeDtypeStruct((R, C), F32)] * 3,
        compiler_params=pltpu.CompilerParams(dimension_semantics=("parallel",)),
    )(w2, g2, m2, v2)
    return tuple(o.reshape(shape) for o in outs)


_VMEM_SPEC = pl.BlockSpec(memory_space=pltpu.VMEM)
_HBM_SPEC = pl.BlockSpec(memory_space=pltpu.HBM)


def _flip(v, bit):
    return (1 - v) if bit else v


def _allgather8(v, name):
    def body(v_ref, out_ref, send_sems, recv_sems, local_sem):
        x, y, c = _idx()
        me = 4 * x + 2 * y + c
        mine = pltpu.make_async_copy(v_ref, out_ref.at[me], local_sem)
        mine.start()
        sends = []
        for k in range(1, N_DEV):
            peer = (_flip(x, k & 4), _flip(y, k & 2), _flip(c, k & 1))
            cp = pltpu.make_async_remote_copy(src_ref=v_ref, dst_ref=out_ref.at[me], send_sem=send_sems.at[k - 1], recv_sem=recv_sems.at[k - 1],
                                              device_id=peer, device_id_type=MESH)
            cp.start()
            sends.append(cp)
        for k in range(1, N_DEV):
            px, py, pc = _flip(x, k & 4), _flip(y, k & 2), _flip(c, k & 1)
            src = 4 * px + 2 * py + pc
            pltpu.make_async_remote_copy(src_ref=v_ref, dst_ref=out_ref.at[src], send_sem=send_sems.at[k - 1], recv_sem=recv_sems.at[k - 1],
                                         device_id=(px, py, pc), device_id_type=MESH).wait_recv()
        for cp in sends:
            cp.wait_send()
        mine.wait()

    return pl.pallas_call(
        body, name=name,
        out_shape=jax.ShapeDtypeStruct((N_DEV, *v.shape), v.dtype),
        in_specs=[_VMEM_SPEC], out_specs=_VMEM_SPEC,
        scratch_shapes=[pltpu.SemaphoreType.DMA((N_DEV - 1,)), pltpu.SemaphoreType.DMA((N_DEV - 1,)), pltpu.SemaphoreType.DMA],
    )(v)


def _mod_exchange(modp, name):
    _, L, Nc = modp.shape

    def body(p_ref, out_ref, send_sems, recv_sems, local_sem):
        x, y, c = _idx()
        me, chip = 4 * x + 2 * y + c, 2 * x + y
        mine = pltpu.make_async_copy(p_ref.at[me], out_ref.at[chip], local_sem)
        mine.start()
        sends = []
        for k in range(1, N_CHIPS):
            px, py = _flip(x, k & 2), _flip(y, k & 1)
            cp = pltpu.make_async_remote_copy(src_ref=p_ref.at[4 * px + 2 * py + c], dst_ref=out_ref.at[chip],
                                              send_sem=send_sems.at[k - 1], recv_sem=recv_sems.at[k - 1], device_id=(px, py, c), device_id_type=MESH)
            cp.start()
            sends.append(cp)
        for k in range(1, N_CHIPS):
            px, py = _flip(x, k & 2), _flip(y, k & 1)
            pltpu.make_async_remote_copy(src_ref=p_ref.at[me], dst_ref=out_ref.at[2 * px + py], send_sem=send_sems.at[k - 1],
                                         recv_sem=recv_sems.at[k - 1], device_id=(px, py, c), device_id_type=MESH).wait_recv()
        for cp in sends:
            cp.wait_send()
        mine.wait()

    return pl.pallas_call(
        body, name=name,
        out_shape=jax.ShapeDtypeStruct((N_CHIPS, L, Nc), modp.dtype),
        in_specs=[_VMEM_SPEC], out_specs=_VMEM_SPEC,
        scratch_shapes=[pltpu.SemaphoreType.DMA((N_CHIPS - 1,)), pltpu.SemaphoreType.DMA((N_CHIPS - 1,)), pltpu.SemaphoreType.DMA],
    )(modp)


def _allgather_chips(shards, name):
    n = len(shards)

    def body(*refs):
        ins, outs = refs[:n], refs[n:2 * n]
        send_sems, recv_sems, local_sems = refs[2 * n:]
        x, y, c = _idx()
        chip = 2 * x + y
        sib = (x, y, 1 - c)
        locals_, sends = [], []
        for a in range(n):
            cp = pltpu.make_async_copy(ins[a], outs[a].at[chip], local_sems.at[a])
            cp.start()
            locals_.append(cp)
            for k in range(1, N_CHIPS):
                px, py = _flip(x, k & 2), _flip(y, k & 1)
                cp = pltpu.make_async_remote_copy(src_ref=ins[a].at[c], dst_ref=outs[a].at[chip, c], send_sem=send_sems.at[a, k - 1],
                                                  recv_sem=recv_sems.at[a, k - 1], device_id=(px, py, c), device_id_type=MESH)
                cp.start()
                sends.append(cp)
        for a in range(n):
            for k in range(1, N_CHIPS):
                px, py = _flip(x, k & 2), _flip(y, k & 1)
                src = 2 * px + py
                pltpu.make_async_remote_copy(src_ref=ins[a].at[c], dst_ref=outs[a].at[src, c], send_sem=send_sems.at[a, k - 1],
                                             recv_sem=recv_sems.at[a, k - 1], device_id=(px, py, c), device_id_type=MESH).wait_recv()
                cp = pltpu.make_async_remote_copy(src_ref=outs[a].at[src, c], dst_ref=outs[a].at[src, c], send_sem=send_sems.at[a, 2 + k],
                                                  recv_sem=recv_sems.at[a, 2 + k], device_id=sib, device_id_type=MESH)
                cp.start()
                sends.append(cp)
        for a in range(n):
            for k in range(1, N_CHIPS):
                px, py = _flip(x, k & 2), _flip(y, k & 1)
                src = 2 * px + py
                pltpu.make_async_remote_copy(src_ref=ins[a].at[c], dst_ref=outs[a].at[src, 1 - c], send_sem=send_sems.at[a, 2 + k],
                                             recv_sem=recv_sems.at[a, 2 + k], device_id=sib, device_id_type=MESH).wait_recv()
        for cp in sends:
            cp.wait_send()
        for cp in locals_:
            cp.wait()

    return pl.pallas_call(
        body, name=name,
        out_shape=[jax.ShapeDtypeStruct((N_CHIPS, *s.shape), s.dtype) for s in shards],
        in_specs=[_HBM_SPEC] * n, out_specs=[_HBM_SPEC] * n,
        scratch_shapes=[pltpu.SemaphoreType.DMA((n, 6)), pltpu.SemaphoreType.DMA((n, 6)), pltpu.SemaphoreType.DMA((n,))],
    )(*shards)


def _sibling_swap(parts, name):
    n = len(parts)

    def body(*refs):
        ins, outs = refs[:n], refs[n:2 * n]
        send_sems, recv_sems = refs[2 * n:]
        x, y, c = _idx()
        cps = []
        for a in range(n):
            cp = pltpu.make_async_remote_copy(src_ref=ins[a].at[1 - c], dst_ref=outs[a], send_sem=send_sems.at[a], recv_sem=recv_sems.at[a],
                                              device_id=(x, y, 1 - c), device_id_type=MESH)
            cp.start()
            cps.append(cp)
        for cp in cps:
            cp.wait()

    return pl.pallas_call(
        body, name=name,
        out_shape=[jax.ShapeDtypeStruct(p.shape[1:], p.dtype) for p in parts],
        in_specs=[_HBM_SPEC] * n, out_specs=[_HBM_SPEC] * n,
        scratch_shapes=[pltpu.SemaphoreType.DMA((n,)), pltpu.SemaphoreType.DMA((n,))],
    )(*parts)


def _chip_scatter(parts, name):
    n = len(parts)

    def body(*refs):
        ins, outs = refs[:n], refs[n:2 * n]
        send_sems, recv_sems, local_sems = refs[2 * n:]
        x, y, c = _idx()
        chip = 2 * x + y
        sends, locals_ = [], []
        for a in range(n):
            cp = pltpu.make_async_copy(ins[a].at[chip], outs[a].at[chip], local_sems.at[a])
            cp.start()
            locals_.append(cp)
            for k in range(1, N_CHIPS):
                px, py = _flip(x, k & 2), _flip(y, k & 1)
                cp = pltpu.make_async_remote_copy(src_ref=ins[a].at[2 * px + py], dst_ref=outs[a].at[chip], send_sem=send_sems.at[a, k - 1],
                                                  recv_sem=recv_sems.at[a, k - 1], device_id=(px, py, c), device_id_type=MESH)
                cp.start()
                sends.append(cp)
        for a in range(n):
            for k in range(1, N_CHIPS):
                px, py = _flip(x, k & 2), _flip(y, k & 1)
                pltpu.make_async_remote_copy(src_ref=ins[a].at[chip], dst_ref=outs[a].at[2 * px + py], send_sem=send_sems.at[a, k - 1],
                                             recv_sem=recv_sems.at[a, k - 1], device_id=(px, py, c), device_id_type=MESH).wait_recv()
        for cp in sends:
            cp.wait_send()
        for cp in locals_:
            cp.wait()

    return pl.pallas_call(
        body, name=name,
        out_shape=[jax.ShapeDtypeStruct(p.shape, p.dtype) for p in parts],
        in_specs=[_HBM_SPEC] * n, out_specs=[_HBM_SPEC] * n,
        scratch_shapes=[pltpu.SemaphoreType.DMA((n, 3)), pltpu.SemaphoreType.DMA((n, 3)), pltpu.SemaphoreType.DMA((n,))],
    )(*parts)


def _sibling_merge(halves, name):
    n = len(halves)

    def body(*refs):
        ins, outs = refs[:n], refs[n:2 * n]
        send_sems, recv_sems, local_sems = refs[2 * n:]
        x, y, c = _idx()
        cps, locals_ = [], []
        for a in range(n):
            lc = pltpu.make_async_copy(ins[a], outs[a].at[c], local_sems.at[a])
            lc.start()
            locals_.append(lc)
            cp = pltpu.make_async_remote_copy(src_ref=ins[a], dst_ref=outs[a].at[c], send_sem=send_sems.at[a], recv_sem=recv_sems.at[a],
                                              device_id=(x, y, 1 - c), device_id_type=MESH)
            cp.start()
            cps.append(cp)
        for a in range(n):
            pltpu.make_async_remote_copy(src_ref=ins[a], dst_ref=outs[a].at[1 - c], send_sem=send_sems.at[a], recv_sem=recv_sems.at[a],
                                         device_id=(x, y, 1 - c), device_id_type=MESH).wait_recv()
        for cp in cps:
            cp.wait_send()
        for lc in locals_:
            lc.wait()

    return pl.pallas_call(
        body, name=name,
        out_shape=[jax.ShapeDtypeStruct((2, *h.shape), h.dtype) for h in halves],
        in_specs=[_HBM_SPEC] * n, out_specs=[_HBM_SPEC] * n,
        scratch_shapes=[pltpu.SemaphoreType.DMA((n,)), pltpu.SemaphoreType.DMA((n,)), pltpu.SemaphoreType.DMA((n,))],
    )(*halves)


def _col_full(g):
    k, n = g.shape[1], g.shape[2]
    return g.transpose(1, 0, 2).reshape(k, N_CHIPS * n)


def _col_blocks(w):
    k, n = w.shape
    return w.reshape(k, N_CHIPS, n // N_CHIPS).transpose(1, 0, 2)


def _row_blocks(w):
    k, n = w.shape
    return w.reshape(N_CHIPS, k // N_CHIPS, n)


_UQ_HEAD = MLA_NOPE + MLA_ROPE

_SHARDED = {
    "mlp_w1": (DEPTH * D_MODEL, D_MODEL,
               lambda g: g.reshape(N_CHIPS, DEPTH, D_MODEL, D_MODEL).transpose(1, 2, 0, 3).reshape(DEPTH, D_MODEL, 4 * D_MODEL),
               lambda w: w.reshape(DEPTH, D_MODEL, N_CHIPS, D_MODEL).transpose(2, 0, 1, 3).reshape(N_CHIPS, DEPTH * D_MODEL, D_MODEL)),
    "mlp_w2": (DEPTH * D_MODEL, D_MODEL,
               lambda g: g.reshape(N_CHIPS, DEPTH, D_MODEL, D_MODEL).transpose(1, 0, 2, 3).reshape(DEPTH, 4 * D_MODEL, D_MODEL),
               lambda w: w.reshape(DEPTH, N_CHIPS, D_MODEL, D_MODEL).transpose(1, 0, 2, 3).reshape(N_CHIPS, DEPTH * D_MODEL, D_MODEL)),
    "pool_w": (2 * 4 * 64, POOL_GD,
               lambda g: g.reshape(N_CHIPS, 2, 4, 64, POOL_GD).transpose(1, 2, 0, 3, 4).reshape(2, 4, POOL_GD, POOL_GD),
               lambda w: w.reshape(2, 4, N_CHIPS, 64, POOL_GD).transpose(2, 0, 1, 3, 4).reshape(N_CHIPS, 2 * 4 * 64, POOL_GD)),
    "sgu_w_in": (D_MODEL, 2 * SGU_W // N_CHIPS, _col_full, _col_blocks),
    "sgu_w_out": (SGU_W // N_CHIPS, D_MODEL, lambda g: g.reshape(SGU_W, D_MODEL), _row_blocks),
    "mla_w_dq_dkv": (D_MODEL // N_CHIPS, MLA_QL + MLA_KVL + MLA_ROPE,
                     lambda g: jnp.pad(g.reshape(D_MODEL, MLA_QL + MLA_KVL + MLA_ROPE), ((0, 0), (0, MLA_LATP - (MLA_QL + MLA_KVL + MLA_ROPE)))),
                     lambda w: _row_blocks(w[:, :MLA_QL + MLA_KVL + MLA_ROPE])),
    "mla_w_uq": (MLA_QL, MLA_H * _UQ_HEAD // N_CHIPS,
                 lambda g: jnp.pad(_col_full(g).reshape(MLA_QL, MLA_H, _UQ_HEAD), ((0, 0), (0, 0), (0, MLA_HP - _UQ_HEAD))).reshape(MLA_QL, MLA_H * MLA_HP),
                 lambda w: _col_blocks(w.reshape(MLA_QL, MLA_H, MLA_HP)[:, :, :_UQ_HEAD].reshape(MLA_QL, MLA_H * _UQ_HEAD))),
    "mla_w_ukv": (MLA_KVL, MLA_H * (MLA_NOPE + MLA_V) // N_CHIPS, _col_full, _col_blocks),
    "mla_w_o": (MLA_H * MLA_V // N_CHIPS, D_MODEL, lambda g: g.reshape(MLA_H * MLA_V, D_MODEL), _row_blocks),
}
_SHARDED_ORDER = tuple(_SHARDED)


def _rope_tables(positions):
    inv_freq = ROPE_THETA ** (-jnp.arange(0, MLA_ROPE, 2, dtype=F32) / MLA_ROPE)
    ang = positions.astype(F32)[:, None] * inv_freq
    cos, sin = jnp.cos(ang), jnp.sin(ang)
    z32, z64 = jnp.zeros_like(cos), jnp.zeros((positions.shape[0], 64), F32)
    return (jnp.concatenate([cos, cos, z64], axis=1), jnp.concatenate([-sin, z32, z64], axis=1), jnp.concatenate([z32, sin, z64], axis=1))


def _local_step(x, positions, target, mod, W, S):
    D = D_MODEL
    cc, sa, sb = _rope_tables(positions)
    saved = []
    for i in range(DEPTH):
        sh1, sc1, g1, sh2, sc2, g2 = (mod[i:i + 1, n * D:(n + 1) * D] for n in range(6))
        kind, j = i % N_MIXERS, i // N_MIXERS
        gmix, gmlp = S["norm_mix_g"][i:i + 1], S["norm_mlp_g"][i:i + 1]
        st = {"x": x}
        if kind == 0:
            h = _norm_mod_fwd(x, gmix, sc1, sh1, F32, f"l{i}_norm1")
            x2, pooled, ypre = _pool_fwd(h, W["pool_w"][j], S["pool_scale"][j:j + 1], x, g1, f"l{i}_pool")
            st.update(pooled=pooled, y=ypre)
        elif kind == 1:
            h = _norm_mod_fwd(x, gmix, sc1, sh1, BF16, f"l{i}_norm1")
            zz = _mm(h, W["sgu_w_in"], out_dtypes=(F32,), name=f"l{i}_sgu_in")
            bs_t = S["sgu_b_s"].T
            gated = _sgu_gate_fwd(zz, S["sgu_ln_g"], S["sgu_ln_b"], S["sgu_w_s"], bs_t, f"l{i}_sgu_gate")
            x2, y = _mm(gated, W["sgu_w_out"], epi=_epi_residual, extras=((x, "mn"), (g1, "n")), out_dtypes=(F32, BF16), name=f"l{i}_sgu_out")
            st.update(h=h, zz=zz, gated=gated, y=y, bs_t=bs_t)
        else:
            h = _norm_mod_fwd(x, gmix, sc1, sh1, BF16, f"l{i}_norm1")
            lat = _mm(h, W["mla_w_dq_dkv"], out_dtypes=(F32,), name=f"l{i}_mla_lat")
            cqn, ckvn, krot = _mla_lat_fwd(lat, S["mla_q_norm_g"], S["mla_kv_norm_g"], cc, sa, sb, f"l{i}_mla_latn")
            qpad = _mm(cqn, W["mla_w_uq"], out_dtypes=(F32,), name=f"l{i}_mla_uq")
            kv = _mm(ckvn, W["mla_w_ukv"], out_dtypes=(F32,), name=f"l{i}_mla_ukv")
            q, k, v = _mla_prep(qpad, kv, krot, cc, sa, sb, f"l{i}_mla_prep")
            o, lse = _attn_fwd(q, k, v, f"l{i}_attn")
            x2, y = _mm(o, W["mla_w_o"], epi=_epi_residual, extras=((x, "mn"), (g1, "n")), out_dtypes=(F32, BF16), name=f"l{i}_mla_o")
            st.update(h=h, lat=lat, cqn=cqn, ckvn=ckvn, q=q, k=k, v=v, o=o, lse=lse, y=y)
        h2 = _norm_mod_fwd(x2, gmlp, sc2, sh2, BF16, f"l{i}_norm2")
        z = _mm(h2, W["mlp_w1"][i], epi=lambda acc: (jnp.square(jnp.maximum(acc, 0.0)),), name=f"l{i}_mlp1")
        x3, o2 = _mm(z, W["mlp_w2"][i], epi=_epi_residual, extras=((x2, "mn"), (g2, "n")), out_dtypes=(F32, BF16), name=f"l{i}_mlp2")
        st.update(x2=x2, h2=h2, z=z, o2=o2)
        saved.append(st)
        x = x3

    loss, dx, dfinal_g = _loss_head(x, target, S["final_g"], "loss_head")

    gW = {"mlp_w1": [None] * DEPTH, "mlp_w2": [None] * DEPTH, "pool_w": [None] * 2}
    gS = {"final_g": dfinal_g, "norm_mix_g": [None] * DEPTH, "norm_mlp_g": [None] * DEPTH, "pool_scale": [None] * 2}
    dmod = [None] * DEPTH
    for i in reversed(range(DEPTH)):
        st = saved[i]
        sh1, sc1, g1, sh2, sc2, g2 = (mod[i:i + 1, n * D:(n + 1) * D] for n in range(6))
        kind, j = i % N_MIXERS, i // N_MIXERS
        gmix, gmlp = S["norm_mix_g"][i:i + 1], S["norm_mlp_g"][i:i + 1]
        do2, dg2 = _resid_bwd(dx, st["o2"], g2, f"l{i}_b_res2")
        da = _mm(do2, W["mlp_w2"][i], tb=True, epi=lambda acc, zt: (acc * (2.0 * jnp.sqrt(zt.astype(F32))),), extras=((st["z"], "mn"),), name=f"l{i}_b_dz")
        gW["mlp_w2"][i] = _mm(st["z"], do2, ta=True, name=f"l{i}_b_dw2")
        dh2 = _mm(da, W["mlp_w1"][i], tb=True, out_dtypes=(F32,), name=f"l{i}_b_dh2")
        gW["mlp_w1"][i] = _mm(st["h2"], da, ta=True, name=f"l{i}_b_dw1")
        dx2, dgmlp, dsc2, dsh2 = _norm_mod_bwd(st["x2"], dh2, dx, gmlp, sc2, f"l{i}_b_norm2")
        gS["norm_mlp_g"][i] = dgmlp
        dy, q1 = _resid_bwd(dx2, st["y"], g1, f"l{i}_b_res1")
        if kind == 0:
            dh, dpw, dpsc, dg1 = _pool_bwd(dy, st["pooled"], W["pool_w"][j], S["pool_scale"][j:j + 1], g1, q1, f"l{i}_b_pool")
            gW["pool_w"][j] = dpw
            gS["pool_scale"][j] = dpsc
        elif kind == 1:
            dg1 = q1
            dgated = _mm(dy, W["sgu_w_out"], tb=True, name=f"l{i}_b_dgated")
            gW["sgu_w_out"] = _mm(st["gated"], dy, ta=True, name=f"l{i}_b_dwout")
            dzz, dws, dbs, dlg, dlb = _sgu_gate_bwd(st["zz"], dgated, S["sgu_ln_g"], S["sgu_ln_b"], S["sgu_w_s"], st["bs_t"], f"l{i}_b_sgu_gate")
            gS.update(sgu_w_s=dws, sgu_b_s=dbs[:, :, 0], sgu_ln_g=dlg, sgu_ln_b=dlb)
            dh = _mm(dzz, W["sgu_w_in"], tb=True, out_dtypes=(F32,), name=f"l{i}_b_dh_sgu")
            gW["sgu_w_in"] = _mm(st["h"], dzz, ta=True, name=f"l{i}_b_dwin")
        else:
            dg1 = q1
            do = _mm(dy, W["mla_w_o"], tb=True, name=f"l{i}_b_do")
            gW["mla_w_o"] = _mm(st["o"], dy, ta=True, name=f"l{i}_b_dwo")
            delta = _attn_delta(do, st["o"], f"l{i}_b_delta")
            dq = _attn_bwd_dq(st["q"], st["k"], st["v"], do, st["lse"], delta, f"l{i}_b_attn_dq")
            dk, dv = _attn_bwd_dkv(st["q"], st["k"], st["v"], do, st["lse"], delta, f"l{i}_b_attn_dkv")
            dqpad, dkv, dkrot = _mla_prep_bwd(dq, dk, dv, cc, sa, sb, f"l{i}_b_mla_prep")
            dcqn = _mm(dqpad, W["mla_w_uq"], tb=True, out_dtypes=(F32,), name=f"l{i}_b_dcq")
            gW["mla_w_uq"] = _mm(st["cqn"], dqpad, ta=True, name=f"l{i}_b_dwuq")
            dckvn = _mm(dkv, W["mla_w_ukv"], tb=True, out_dtypes=(F32,), name=f"l{i}_b_dckv")
            gW["mla_w_ukv"] = _mm(st["ckvn"], dkv, ta=True, name=f"l{i}_b_dwukv")
            dlat, dqg, dkvg = _mla_lat_bwd(st["lat"], dcqn, dckvn, dkrot, S["mla_q_norm_g"], S["mla_kv_norm_g"], cc, sa, sb, f"l{i}_b_mla_latn")
            gS.update(mla_q_norm_g=dqg, mla_kv_norm_g=dkvg)
            dh = _mm(dlat, W["mla_w_dq_dkv"], tb=True, out_dtypes=(F32,), name=f"l{i}_b_dh_mla")
            gW["mla_w_dq_dkv"] = _mm(st["h"], dlat, ta=True, name=f"l{i}_b_dwdq")
        dx, dgmix, dsc1, dsh1 = _norm_mod_bwd(st["x"], dh, dx2, gmix, sc1, f"l{i}_b_norm1")
        gS["norm_mix_g"][i] = dgmix
        dmod[i] = jnp.concatenate([dsh1, dsc1, dg1, dsh2, dsc2, dg2], axis=1)

    gW["mlp_w1"] = jnp.stack(gW["mlp_w1"])
    gW["mlp_w2"] = jnp.stack(gW["mlp_w2"])
    gW["pool_w"] = jnp.stack(gW["pool_w"]).astype(BF16)
    for n in ("norm_mix_g", "norm_mlp_g", "pool_scale"):
        gS[n] = jnp.concatenate(gS[n], axis=0)
    return loss, dx, gW, gS, jnp.concatenate(dmod, axis=0)


_SMALL = {
    "norm_mix_g": (DEPTH, D_MODEL), "norm_mlp_g": (DEPTH, D_MODEL), "sgu_ln_g": (1, SGU_W), "sgu_ln_b": (1, SGU_W),
    "sgu_w_s": (SGU_H, SGU_CHUNK, SGU_CHUNK), "sgu_b_s": (SGU_H, SGU_CHUNK), "mla_kv_norm_g": (1, MLA_KVL), "final_g": (1, D_MODEL),
    "pool_scale": (2, D_MODEL), "mla_q_norm_g": (1, MLA_QL), "dmod": (DEPTH, 6 * D_MODEL),
}
_PACK_W = 1024


def _pack(vals):
    flat = jnp.concatenate([v.reshape(-1) for v in vals])
    rows = -(-flat.shape[0] // (8 * _PACK_W)) * 8
    return jnp.pad(flat, (0, rows * _PACK_W - flat.shape[0])).reshape(rows, _PACK_W)


def _unpack(buf, shapes):
    flat, out, off = buf.reshape(-1), [], 0
    for s in shapes:
        n = math.prod(s)
        out.append(flat[off:off + n].reshape(s))
        off += n
    return out


def kernel(x, c, positions, ada_w, ada_b, norm_mix_g, norm_mlp_g, pool_w, pool_scale, sgu_w_in, sgu_ln_g, sgu_ln_b, sgu_w_s, sgu_b_s, sgu_w_out, mla_w_dq_dkv, mla_q_norm_g, mla_kv_norm_g, mla_w_uq, mla_w_ukv, mla_w_o, mlp_w1, mlp_w2, final_g, loss_target, m_ada_w, m_ada_b, m_norm_mix_g, m_norm_mlp_g, m_pool_w, m_pool_scale, m_sgu_w_in, m_sgu_ln_g, m_sgu_ln_b, m_sgu_w_s, m_sgu_b_s, m_sgu_w_out, m_mla_w_dq_dkv, m_mla_q_norm_g, m_mla_kv_norm_g, m_mla_w_uq, m_mla_w_ukv, m_mla_w_o, m_mlp_w1, m_mlp_w2, m_final_g, v_ada_w, v_ada_b, v_norm_mix_g, v_norm_mlp_g, v_pool_w, v_pool_scale, v_sgu_w_in, v_sgu_ln_g, v_sgu_ln_b, v_sgu_w_s, v_sgu_b_s, v_sgu_w_out, v_mla_w_dq_dkv, v_mla_q_norm_g, v_mla_kv_norm_g, v_mla_w_uq, v_mla_w_ukv, v_mla_w_o, v_mlp_w1, v_mlp_w2, v_final_g):
    P = dict(ada_w=ada_w, ada_b=ada_b, norm_mix_g=norm_mix_g, norm_mlp_g=norm_mlp_g, pool_w=pool_w, pool_scale=pool_scale, sgu_w_in=sgu_w_in,
             sgu_ln_g=sgu_ln_g, sgu_ln_b=sgu_ln_b, sgu_w_s=sgu_w_s, sgu_b_s=sgu_b_s, sgu_w_out=sgu_w_out, mla_w_dq_dkv=mla_w_dq_dkv,
             mla_q_norm_g=mla_q_norm_g, mla_kv_norm_g=mla_kv_norm_g, mla_w_uq=mla_w_uq, mla_w_ukv=mla_w_ukv, mla_w_o=mla_w_o, mlp_w1=mlp_w1,
             mlp_w2=mlp_w2, final_g=final_g)
    M = dict(ada_w=m_ada_w, ada_b=m_ada_b, norm_mix_g=m_norm_mix_g, norm_mlp_g=m_norm_mlp_g, pool_w=m_pool_w, pool_scale=m_pool_scale,
             sgu_w_in=m_sgu_w_in, sgu_ln_g=m_sgu_ln_g, sgu_ln_b=m_sgu_ln_b, sgu_w_s=m_sgu_w_s, sgu_b_s=m_sgu_b_s, sgu_w_out=m_sgu_w_out,
             mla_w_dq_dkv=m_mla_w_dq_dkv, mla_q_norm_g=m_mla_q_norm_g, mla_kv_norm_g=m_mla_kv_norm_g, mla_w_uq=m_mla_w_uq, mla_w_ukv=m_mla_w_ukv,
             mla_w_o=m_mla_w_o, mlp_w1=m_mlp_w1, mlp_w2=m_mlp_w2, final_g=m_final_g)
    V = dict(ada_w=v_ada_w, ada_b=v_ada_b, norm_mix_g=v_norm_mix_g, norm_mlp_g=v_norm_mlp_g, pool_w=v_pool_w, pool_scale=v_pool_scale,
             sgu_w_in=v_sgu_w_in, sgu_ln_g=v_sgu_ln_g, sgu_ln_b=v_sgu_ln_b, sgu_w_s=v_sgu_w_s, sgu_b_s=v_sgu_b_s, sgu_w_out=v_sgu_w_out,
             mla_w_dq_dkv=v_mla_w_dq_dkv, mla_q_norm_g=v_mla_q_norm_g, mla_kv_norm_g=v_mla_kv_norm_g, mla_w_uq=v_mla_w_uq, mla_w_ukv=v_mla_w_ukv,
             mla_w_o=v_mla_w_o, mlp_w1=v_mlp_w1, mlp_w2=v_mlp_w2, final_g=v_final_g)
    order = list(P)
    xi, yi, ci = _idx()
    chip = 2 * xi + yi
    D = D_MODEL
    n_ada = ada_w.shape[2]

    pre = _allgather8(_pack([c, pool_scale, mla_q_norm_g]), "ag_small")
    flat = pre.reshape(N_DEV, -1)
    c_all = flat[:, :D]
    ps_all = flat[0::2, D:D + 2 * (D // N_CHIPS)].reshape(N_CHIPS, 2, D // N_CHIPS).transpose(1, 0, 2).reshape(2, D)
    q0 = D + 2 * (D // N_CHIPS)
    qg_all = flat[0::2, q0:q0 + MLA_QL // N_CHIPS].reshape(1, MLA_QL)

    ada_b_loc = lax.dynamic_slice_in_dim(ada_b, chip * n_ada, n_ada, axis=1)[:, None, :]
    modp = _ada_fwd(c_all, ada_w, ada_b_loc, "ada_fwd")
    mod = _mod_exchange(modp.transpose(1, 0, 2), "mod_exchange").transpose(1, 0, 2).reshape(DEPTH, 6 * D)

    shards = [P[n].astype(BF16).reshape(2, _SHARDED[n][0] // 2, _SHARDED[n][1]) for n in _SHARDED_ORDER]
    gathered = _allgather_chips(shards, "ag_weights")
    W = {n: _SHARDED[n][2](g.reshape(N_CHIPS, _SHARDED[n][0], _SHARDED[n][1])) for n, g in zip(_SHARDED_ORDER, gathered, strict=True)}
    S = dict(norm_mix_g=norm_mix_g, norm_mlp_g=norm_mlp_g, pool_scale=ps_all, sgu_ln_g=sgu_ln_g, sgu_ln_b=sgu_ln_b, sgu_w_s=sgu_w_s[0],
             sgu_b_s=sgu_b_s[0], mla_q_norm_g=qg_all, mla_kv_norm_g=mla_kv_norm_g, final_g=final_g[None, :])

    loss_l, dx, gW, gS, dmod = _local_step(x[0], positions[0], loss_target[0], mod, W, S)
    loss = lax.psum(loss_l[0, 0], ("x", "y", "c"))

    gS["dmod"] = dmod
    small = _allgather8(_pack([gS[n] for n in _SMALL]), "ag_small_grads")
    small_sum = _unpack(_sum_lead(small, "sum_small_grads"), list(_SMALL.values()))
    G = dict(zip(_SMALL, small_sum, strict=True))
    grads = {
        "ada_b": G["dmod"], "norm_mix_g": G["norm_mix_g"], "norm_mlp_g": G["norm_mlp_g"], "sgu_ln_g": G["sgu_ln_g"], "sgu_ln_b": G["sgu_ln_b"],
        "sgu_w_s": G["sgu_w_s"][None], "sgu_b_s": G["sgu_b_s"][None], "mla_kv_norm_g": G["mla_kv_norm_g"], "final_g": G["final_g"][0],
        "pool_scale": lax.dynamic_slice_in_dim(G["pool_scale"], chip * (D // N_CHIPS), D // N_CHIPS, axis=1),
        "mla_q_norm_g": lax.dynamic_slice_in_dim(G["mla_q_norm_g"], chip * (MLA_QL // N_CHIPS), MLA_QL // N_CHIPS, axis=1),
    }
    dmod_all = _unpack(small, [(N_DEV,) + (small.shape[1] * _PACK_W,)])[0]
    off = sum(math.prod(s) for n, s in _SMALL.items() if n != "dmod")
    dmod_all = dmod_all[:, off:off + DEPTH * 6 * D].reshape(N_DEV, DEPTH, 6 * D)
    dmod_loc = lax.dynamic_slice_in_dim(dmod_all, chip * n_ada, n_ada, axis=2).transpose(1, 0, 2)
    grads["ada_w"] = _ada_bwd(c_all.T, dmod_loc, "ada_bwd")

    blocked = []
    for n in _SHARDED_ORDER:
        r, cdim = _SHARDED[n][0], _SHARDED[n][1]
        b = _SHARDED[n][3](gW[n].astype(BF16))
        blocked.append(b.reshape(N_CHIPS, 2, r // 2, cdim).transpose(1, 0, 2, 3))
    from_sib = _sibling_swap(blocked, "rs_sibling")
    pair = []
    for n, b, f in zip(_SHARDED_ORDER, blocked, from_sib, strict=True):
        mine = lax.dynamic_index_in_dim(b, ci, 0, keepdims=False)
        hr, cdim = mine.shape[1], mine.shape[2]
        both = jnp.stack([mine.reshape(N_CHIPS * hr, cdim), f.reshape(N_CHIPS * hr, cdim)])
        pair.append(_sum_lead(both, f"rs_pair_{n}", BF16).reshape(N_CHIPS, hr, cdim))
    landed = _chip_scatter(pair, "rs_chips")
    halves = [_sum_lead(l, f"rs_sum_{n}") for n, l in zip(_SHARDED_ORDER, landed, strict=True)]
    merged = _sibling_merge(halves, "rs_merge")
    for n, mg in zip(_SHARDED_ORDER, merged, strict=True):
        grads[n] = mg.reshape(P[n].shape)

    deltas, new_m, new_v = {}, {}, {}
    for n in order:
        deltas[n], new_m[n], new_v[n] = _adamw(P[n], grads[n].reshape(P[n].shape), M[n], V[n], f"adamw_{n}")
    return (loss, dx[None], *[grads[n].reshape(P[n].shape) for n in order], *[deltas[n] for n in order], *[new_m[n] for n in order],
            *[new_v[n] for n in order])
```

```python
import math

import jax
import jax.numpy as jnp
from jax import lax
from jax.experimental import pallas as pl
from jax.experimental.pallas import tpu as pltpu

F32, BF16 = jnp.float32, jnp.bfloat16
MESH = pl.DeviceIdType.MESH

D_MODEL = 1024
DEPTH = 4
N_MIXERS = 3
POOL_WINDOWS = (2, 4, 8, 16)
POOL_GD = D_MODEL // len(POOL_WINDOWS)
POOL_HALO = 16
SGU_CHUNK = 128
SGU_W = D_MODEL
SGU_HD = 128
SGU_H = SGU_W // SGU_HD
MLA_H = 16
MLA_QL = 256
MLA_KVL = 128
MLA_NOPE = 128
MLA_ROPE = 64
MLA_V = 128
MLA_HP = 256
MLA_LATP = 512
ROPE_THETA = 10000.0
RMS_EPS = 1e-6
LN_EPS = 1e-5
SM_SCALE = (MLA_NOPE + MLA_ROPE) ** -0.5
NEG_INF = -1e30
ADAM_LR, ADAM_B1, ADAM_B2, ADAM_EPS, ADAM_WD, ADAM_STEP = 0.001, 0.9, 0.999, 1e-08, 0.01, 10
N_CHIPS = 4
N_DEV = 8
ROW_TILE = 512
ATT_TILE = 512
ATT_SUB = 256


def _idx():
    return lax.axis_index("x"), lax.axis_index("y"), lax.axis_index("c")


def _mm(a, b, *, name, ta=False, tb=False, epi=None, extras=(), out_dtypes=(BF16,), tm=1024, tn=1024, tk=1024):
    if ta:
        K, M = a.shape
    else:
        M, K = a.shape
    if tb:
        N, Kb = b.shape
    else:
        Kb, N = b.shape
    assert K == Kb, (a.shape, b.shape, ta, tb)
    tm, tn, tk = min(tm, M), min(tn, N), min(tk, K)
    assert M % tm == 0 and N % tn == 0 and K % tk == 0, (M, N, K, tm, tn, tk)
    nk = K // tk
    a_spec = pl.BlockSpec((tk, tm), lambda i, j, k: (k, i)) if ta else pl.BlockSpec((tm, tk), lambda i, j, k: (i, k))
    b_spec = pl.BlockSpec((tn, tk), lambda i, j, k: (j, k)) if tb else pl.BlockSpec((tk, tn), lambda i, j, k: (k, j))
    ex_specs = []
    for arr, kind in extras:
        if kind == "mn":
            ex_specs.append(pl.BlockSpec((tm, tn), lambda i, j, k: (i, j)))
        elif kind == "n":
            ex_specs.append(pl.BlockSpec((1, tn), lambda i, j, k: (0, j)))
        else:
            ex_specs.append(pl.BlockSpec((tm, arr.shape[1]), lambda i, j, k: (i, 0)))
    n_ex, n_out = len(extras), len(out_dtypes)
    dims = (((0 if ta else 1,), (1 if tb else 0,)), ((), ()))

    def body(*refs):
        a_ref, b_ref = refs[0], refs[1]
        ex_refs = refs[2:2 + n_ex]
        out_refs = refs[2 + n_ex:2 + n_ex + n_out]
        part = lax.dot_general(a_ref[...].astype(BF16), b_ref[...].astype(BF16), dims, preferred_element_type=F32)

        def finish(acc):
            outs = epi(acc, *[r[...] for r in ex_refs]) if epi is not None else (acc,)
            for r, o in zip(out_refs, outs, strict=True):
                r[...] = o.astype(r.dtype)

        if nk == 1:
            finish(part)
        else:
            acc_ref = refs[-1]
            k = pl.program_id(2)

            @pl.when(k == 0)
            def _():
                acc_ref[...] = part

            @pl.when(k > 0)
            def _():
                acc_ref[...] += part

            @pl.when(k == nk - 1)
            def _():
                finish(acc_ref[...])

    outs = pl.pallas_call(
        body,
        name=name,
        grid=(M // tm, N // tn, nk),
        in_specs=[a_spec, b_spec, *ex_specs],
        out_specs=[pl.BlockSpec((tm, tn), lambda i, j, k: (i, j)) for _ in range(n_out)],
        out_shape=[jax.ShapeDtypeStruct((M, N), dt) for dt in out_dtypes],
        scratch_shapes=[pltpu.VMEM((tm, tn), F32)] if nk > 1 else [],
        compiler_params=pltpu.CompilerParams(dimension_semantics=("parallel", "parallel", "arbitrary")),
    )(a, b, *[arr for arr, _ in extras])
    return outs[0] if n_out == 1 else tuple(outs)


def _epi_residual(acc, x, g):
    return x + g * acc, acc


def _row_spec(tr, d):
    return pl.BlockSpec((tr, d), lambda i: (i, 0))


def _vec_spec(d):
    return pl.BlockSpec((1, d), lambda i: (0, 0))


def _colsum(v):
    return jnp.sum(v, axis=0, keepdims=True)


def _norm_mod_fwd(x, gain, sc, sh, out_dtype, name):
    T, D = x.shape
    tr = min(T, ROW_TILE)

    def body(x_ref, g_ref, sc_ref, sh_ref, o_ref):
        xv = x_ref[...]
        r = lax.rsqrt(jnp.mean(xv * xv, axis=-1, keepdims=True) + RMS_EPS)
        o_ref[...] = (((xv * r) * g_ref[...]) * (1.0 + sc_ref[...]) + sh_ref[...]).astype(o_ref.dtype)

    return pl.pallas_call(
        body, name=name, grid=(T // tr,),
        in_specs=[_row_spec(tr, D), _vec_spec(D), _vec_spec(D), _vec_spec(D)],
        out_specs=_row_spec(tr, D),
        out_shape=jax.ShapeDtypeStruct((T, D), out_dtype),
        compiler_params=pltpu.CompilerParams(dimension_semantics=("parallel",)),
    )(x, gain, sc, sh)


def _norm_mod_bwd(x, dh, dres, gain, sc, name):
    T, D = x.shape
    tr = min(T, ROW_TILE)

    def body(x_ref, dh_ref, dres_ref, g_ref, sc_ref, dx_ref, dg_ref, dsc_ref, dsh_ref):
        @pl.when(pl.program_id(0) == 0)
        def _():
            dg_ref[...] = jnp.zeros_like(dg_ref)
            dsc_ref[...] = jnp.zeros_like(dsc_ref)
            dsh_ref[...] = jnp.zeros_like(dsh_ref)

        xv = x_ref[...]
        r = lax.rsqrt(jnp.mean(xv * xv, axis=-1, keepdims=True) + RMS_EPS)
        xn = xv * r
        dhv = dh_ref[...].astype(F32)
        dsh_ref[...] += _colsum(dhv)
        dsc_ref[...] += _colsum(dhv * (xn * g_ref[...]))
        dt = dhv * (1.0 + sc_ref[...])
        dg_ref[...] += _colsum(dt * xn)
        dxn = dt * g_ref[...]
        dx_ref[...] = dres_ref[...] + r * (dxn - xn * jnp.mean(dxn * xn, axis=-1, keepdims=True))

    return pl.pallas_call(
        body, name=name, grid=(T // tr,),
        in_specs=[_row_spec(tr, D), _row_spec(tr, D), _row_spec(tr, D), _vec_spec(D), _vec_spec(D)],
        out_specs=[_row_spec(tr, D), _vec_spec(D), _vec_spec(D), _vec_spec(D)],
        out_shape=[jax.ShapeDtypeStruct((T, D), F32)] + [jax.ShapeDtypeStruct((1, D), F32)] * 3,
        compiler_params=pltpu.CompilerParams(dimension_semantics=("arbitrary",)),
    )(x, dh, dres, gain, sc)


def _resid_bwd(dx, y, g, name):
    T, D = dx.shape
    tr = min(T, ROW_TILE)

    def body(dx_ref, y_ref, g_ref, dy_ref, q_ref):
        @pl.when(pl.program_id(0) == 0)
        def _():
            q_ref[...] = jnp.zeros_like(q_ref)

        dxv = dx_ref[...]
        dy_ref[...] = (g_ref[...] * dxv).astype(BF16)
        q_ref[...] += _colsum(dxv * y_ref[...].astype(F32))

    return pl.pallas_call(
        body, name=name, grid=(T // tr,),
        in_specs=[_row_spec(tr, D), _row_spec(tr, D), _vec_spec(D)],
        out_specs=[_row_spec(tr, D), _vec_spec(D)],
        out_shape=[jax.ShapeDtypeStruct((T, D), BF16), jax.ShapeDtypeStruct((1, D), F32)],
        compiler_params=pltpu.CompilerParams(dimension_semantics=("arbitrary",)),
    )(dx, y, g)


def _loss_head(x, target, gain, name):
    T, D = x.shape
    tr = min(T, ROW_TILE)

    def body(x_ref, t_ref, g_ref, loss_ref, dx_ref, dg_ref):
        @pl.when(pl.program_id(0) == 0)
        def _():
            loss_ref[...] = jnp.zeros_like(loss_ref)
            dg_ref[...] = jnp.zeros_like(dg_ref)

        xv = x_ref[...]
        r = lax.rsqrt(jnp.mean(xv * xv, axis=-1, keepdims=True) + RMS_EPS)
        xn = xv * r
        err = xn * g_ref[...] - t_ref[...]
        row = jnp.mean(err * err, axis=-1, keepdims=True)
        loss_ref[...] += 0.5 * jnp.sum(row, axis=0, keepdims=True)
        dy = err * (1.0 / D)
        dg_ref[...] += _colsum(dy * xn)
        dxn = dy * g_ref[...]
        dx_ref[...] = r * (dxn - xn * jnp.mean(dxn * xn, axis=-1, keepdims=True))

    return pl.pallas_call(
        body, name=name, grid=(T // tr,),
        in_specs=[_row_spec(tr, D), _row_spec(tr, D), _vec_spec(D)],
        out_specs=[_vec_spec(128), _row_spec(tr, D), _vec_spec(D)],
        out_shape=[jax.ShapeDtypeStruct((1, 128), F32), jax.ShapeDtypeStruct((T, D), F32), jax.ShapeDtypeStruct((1, D), F32)],
        compiler_params=pltpu.CompilerParams(dimension_semantics=("arbitrary",)),
    )(x, target, gain)


def _pool_fwd(h, w, scale, x, g1, name):
    T, D = h.shape
    tr = min(T, ROW_TILE)

    def body(h_ref, w_ref, sc_ref, x_ref, g_ref, x2_ref, pooled_ref, ypre_ref, halo_ref):
        i = pl.program_id(0)

        @pl.when(i == 0)
        def _():
            halo_ref[...] = jnp.zeros_like(halo_ref)

        hv = h_ref[...]
        buf = jnp.concatenate([halo_ref[...], hv], axis=0)
        halo_ref[...] = hv[tr - POOL_HALO:, :]
        t = (i * tr + lax.broadcasted_iota(jnp.int32, (tr, 1), 0)).astype(F32)
        for gi, win in enumerate(POOL_WINDOWS):
            cols = slice(gi * POOL_GD, (gi + 1) * POOL_GD)
            val = buf[:, cols]
            sh = 1
            while sh < win:
                val = val + pltpu.roll(val, sh, axis=0)
                sh *= 2
            pooled = val[POOL_HALO:, :] / jnp.minimum(t + 1.0, float(win)) - hv[:, cols]
            pb = pooled.astype(BF16)
            pooled_ref[:, cols] = pb
            yp = jnp.dot(pb, w_ref[gi], preferred_element_type=F32)
            ypre_ref[:, cols] = yp.astype(BF16)
            x2_ref[:, cols] = x_ref[:, cols] + g_ref[:, cols] * (yp * sc_ref[:, cols])

    return pl.pallas_call(
        body, name=name, grid=(T // tr,),
        in_specs=[_row_spec(tr, D), pl.BlockSpec(w.shape, lambda i: (0, 0, 0)), _vec_spec(D), _row_spec(tr, D), _vec_spec(D)],
        out_specs=[_row_spec(tr, D)] * 3,
        out_shape=[jax.ShapeDtypeStruct((T, D), F32), jax.ShapeDtypeStruct((T, D), BF16), jax.ShapeDtypeStruct((T, D), BF16)],
        scratch_shapes=[pltpu.VMEM((POOL_HALO, D), F32)],
        compiler_params=pltpu.CompilerParams(dimension_semantics=("arbitrary",)),
    )(h, w, scale, x, g1)


def _pool_bwd(dy, pooled, w, scale, g1, q, name):
    T, D = dy.shape
    tr = min(T, ROW_TILE)
    nt = T // tr
    ltot = tr + POOL_HALO

    def body(dy_ref, pooled_ref, w_ref, sc_ref, g_ref, q_ref, dh_ref, dw_ref, dsc_ref, dg_ref, halo_ref):
        i = pl.program_id(0)

        @pl.when(i == 0)
        def _():
            halo_ref[...] = jnp.zeros_like(halo_ref)
            dw_ref[...] = jnp.zeros_like(dw_ref)
            dsc_ref[...] = g_ref[...] * q_ref[...]
            dg_ref[...] = sc_ref[...] * q_ref[...]

        t = ((nt - 1 - i) * tr + lax.broadcasted_iota(jnp.int32, (tr, 1), 0)).astype(F32)
        for gi, win in enumerate(POOL_WINDOWS):
            cols = slice(gi * POOL_GD, (gi + 1) * POOL_GD)
            dyb = (dy_ref[:, cols].astype(F32) * sc_ref[:, cols]).astype(BF16)
            dw_ref[gi] += lax.dot_general(pooled_ref[:, cols], dyb, (((0,), (0,)), ((), ())), preferred_element_type=F32)
            dpool = lax.dot_general(dyb, w_ref[gi], (((1,), (1,)), ((), ())), preferred_element_type=F32)
            qv = dpool / jnp.minimum(t + 1.0, float(win))
            val = jnp.concatenate([qv, halo_ref[:, cols]], axis=0)
            halo_ref[:, cols] = qv[:POOL_HALO, :]
            sh = 1
            while sh < win:
                val = val + pltpu.roll(val, ltot - sh, axis=0)
                sh *= 2
            dh_ref[:, cols] = val[:tr, :] - dpool

    rev = pl.BlockSpec((tr, D), lambda i: (nt - 1 - i, 0))
    return pl.pallas_call(
        body, name=name, grid=(nt,),
        in_specs=[rev, rev, pl.BlockSpec(w.shape, lambda i: (0, 0, 0)), _vec_spec(D), _vec_spec(D), _vec_spec(D)],
        out_specs=[rev, pl.BlockSpec(w.shape, lambda i: (0, 0, 0)), _vec_spec(D), _vec_spec(D)],
        out_shape=[jax.ShapeDtypeStruct((T, D), F32), jax.ShapeDtypeStruct(w.shape, F32),
                   jax.ShapeDtypeStruct((1, D), F32), jax.ShapeDtypeStruct((1, D), F32)],
        scratch_shapes=[pltpu.VMEM((POOL_HALO, D), F32)],
        compiler_params=pltpu.CompilerParams(dimension_semantics=("arbitrary",)),
    )(dy, pooled, w, scale, g1, q)


_INV_SQRT2 = 0.7071067811865476
_INV_SQRT2PI = 0.3989422804014327


def _gelu(v):
    return 0.5 * v * (1.0 + lax.erf(v * _INV_SQRT2))


def _gelu_grad(v):
    return 0.5 * (1.0 + lax.erf(v * _INV_SQRT2)) + v * jnp.exp(-0.5 * v * v) * _INV_SQRT2PI


def _sgu_ln(v, g, b):
    mu = jnp.mean(v, axis=-1, keepdims=True)
    xc = v - mu
    rstd = lax.rsqrt(jnp.mean(xc * xc, axis=-1, keepdims=True) + LN_EPS)
    xh = xc * rstd
    return xh, rstd, xh * g + b


def _tril_mask():
    return lax.broadcasted_iota(jnp.int32, (SGU_CHUNK, SGU_CHUNK), 0) >= lax.broadcasted_iota(jnp.int32, (SGU_CHUNK, SGU_CHUNK), 1)


SGU_TILE = 256


def _sgu_gate_fwd(zz, ln_g, ln_b, ws, bs_t, name):
    T = zz.shape[0]
    ts = min(T, SGU_TILE)

    def body(zz_ref, g_ref, b_ref, ws_ref, bs_ref, out_ref):
        z = _gelu(zz_ref[...])
        u = z[:, :SGU_W]
        _, _, vn = _sgu_ln(z[:, SGU_W:], g_ref[...], b_ref[...])
        vb = vn.astype(BF16)
        tril = _tril_mask()
        for hh in range(SGU_H):
            wm = jnp.where(tril, ws_ref[hh], 0.0).astype(BF16)
            bcol = bs_ref[:, hh:hh + 1]
            cs = slice(hh * SGU_HD, (hh + 1) * SGU_HD)
            for j in range(ts // SGU_CHUNK):
                rs = slice(j * SGU_CHUNK, (j + 1) * SGU_CHUNK)
                mixed = jnp.dot(wm, vb[rs, cs], preferred_element_type=F32) + bcol
                out_ref[rs, cs] = (u[rs, cs] * mixed).astype(BF16)

    return pl.pallas_call(
        body, name=name, grid=(T // ts,),
        in_specs=[_row_spec(ts, 2 * SGU_W), _vec_spec(SGU_W), _vec_spec(SGU_W),
                  pl.BlockSpec(ws.shape, lambda i: (0, 0, 0)), pl.BlockSpec(bs_t.shape, lambda i: (0, 0))],
        out_specs=_row_spec(ts, SGU_W),
        out_shape=jax.ShapeDtypeStruct((T, SGU_W), BF16),
        compiler_params=pltpu.CompilerParams(dimension_semantics=("parallel",)),
    )(zz, ln_g, ln_b, ws, bs_t)


def _sgu_gate_bwd(zz, dgated, ln_g, ln_b, ws, bs_t, name):
    T = zz.shape[0]
    ts = min(T, SGU_TILE)
    nt = T // ts

    def body(zz_ref, dg_ref, g_ref, b_ref, ws_ref, bs_ref, dzz_ref, dws_ref, dbs_ref, dlg_ref, dlb_ref, dlo_ref, dmx_ref):
        i = pl.program_id(0)

        @pl.when(i == 0)
        def _():
            dws_ref[...] = jnp.zeros_like(dws_ref)
            dmx_ref[...] = jnp.zeros_like(dmx_ref)
            dlg_ref[...] = jnp.zeros_like(dlg_ref)
            dlb_ref[...] = jnp.zeros_like(dlb_ref)

        zzv = zz_ref[...]
        z = _gelu(zzv)
        u = z[:, :SGU_W]
        xh, rstd, vn = _sgu_ln(z[:, SGU_W:], g_ref[...], b_ref[...])
        vb = vn.astype(BF16)
        dgv = dg_ref[...].astype(F32)
        tril = _tril_mask()
        for hh in range(SGU_H):
            wm = jnp.where(tril, ws_ref[hh], 0.0).astype(BF16)
            bcol = bs_ref[:, hh:hh + 1]
            cs = slice(hh * SGU_HD, (hh + 1) * SGU_HD)
            for j in range(ts // SGU_CHUNK):
                rs = slice(j * SGU_CHUNK, (j + 1) * SGU_CHUNK)
                mixed = jnp.dot(wm, vb[rs, cs], preferred_element_type=F32) + bcol
                dmixed = dgv[rs, cs] * u[rs, cs]
                dzz_ref[rs, cs] = (dgv[rs, cs] * mixed * _gelu_grad(zzv[rs, cs])).astype(BF16)
                dmb = dmixed.astype(BF16)
                dws_ref[hh] += lax.dot_general(dmb, vb[rs, cs], (((1,), (1,)), ((), ())), preferred_element_type=F32)
                dmx_ref[hh] += dmixed
                dlo_ref[rs, cs] = lax.dot_general(wm, dmb, (((0,), (0,)), ((), ())), preferred_element_type=F32)
        dlo = dlo_ref[...]
        dlg_ref[...] += _colsum(dlo * xh)
        dlb_ref[...] += _colsum(dlo)
        dxh = dlo * g_ref[...]
        dv = rstd * (dxh - jnp.mean(dxh, axis=-1, keepdims=True) - xh * jnp.mean(dxh * xh, axis=-1, keepdims=True))
        dzz_ref[:, SGU_W:] = (dv * _gelu_grad(zzv[:, SGU_W:])).astype(BF16)

        @pl.when(i == nt - 1)
        def _():
            tril_f = tril.astype(F32)
            for hh in range(SGU_H):
                dws_ref[hh] = dws_ref[hh] * tril_f
                dbs_ref[hh] = jnp.broadcast_to(jnp.sum(dmx_ref[hh], axis=-1, keepdims=True), (SGU_CHUNK, SGU_HD))

    full3 = pl.BlockSpec(ws.shape, lambda i: (0, 0, 0))
    return pl.pallas_call(
        body, name=name, grid=(nt,),
        in_specs=[_row_spec(ts, 2 * SGU_W), _row_spec(ts, SGU_W), _vec_spec(SGU_W), _vec_spec(SGU_W), full3,
                  pl.BlockSpec(bs_t.shape, lambda i: (0, 0))],
        out_specs=[_row_spec(ts, 2 * SGU_W), full3, full3, _vec_spec(SGU_W), _vec_spec(SGU_W)],
        out_shape=[jax.ShapeDtypeStruct((T, 2 * SGU_W), BF16), jax.ShapeDtypeStruct(ws.shape, F32), jax.ShapeDtypeStruct(ws.shape, F32),
                   jax.ShapeDtypeStruct((1, SGU_W), F32), jax.ShapeDtypeStruct((1, SGU_W), F32)],
        scratch_shapes=[pltpu.VMEM((ts, SGU_W), F32), pltpu.VMEM(ws.shape, F32)],
        compiler_params=pltpu.CompilerParams(dimension_semantics=("arbitrary",)),
    )(zz, dgated, ln_g, ln_b, ws, bs_t)


def _rope_fwd(blk, cc, sa, sb):
    return blk * cc + pltpu.roll(blk, 96, axis=1) * sa + pltpu.roll(blk, 32, axis=1) * sb


def _rope_bwd(d, cc, sa, sb):
    return d * cc + pltpu.roll(d * sa, 32, axis=1) + pltpu.roll(d * sb, 96, axis=1)


def _rms(v, g):
    r = lax.rsqrt(jnp.mean(v * v, axis=-1, keepdims=True) + RMS_EPS)
    vn = v * r
    return vn, r, vn * g


def _rms_bwd(dy, vn, r, g):
    dvn = dy * g
    return r * (dvn - vn * jnp.mean(dvn * vn, axis=-1, keepdims=True))


MLA_TILE = 256
_KV0 = MLA_QL
_KR0 = MLA_QL + MLA_KVL


def _mla_lat_fwd(lat, qg, kvg, cc, sa, sb, name):
    T = lat.shape[0]
    tr = min(T, ROW_TILE)

    def body(lat_ref, qg_ref, kvg_ref, cc_ref, sa_ref, sb_ref, cq_ref, ckv_ref, kr_ref):
        lv = lat_ref[...]
        cq_ref[...] = _rms(lv[:, :_KV0], qg_ref[...])[2].astype(BF16)
        ckv_ref[...] = _rms(lv[:, _KV0:_KR0], kvg_ref[...])[2].astype(BF16)
        kr_ref[...] = _rope_fwd(lv[:, _KR0:], cc_ref[...], sa_ref[...], sb_ref[...])

    return pl.pallas_call(
        body, name=name, grid=(T // tr,),
        in_specs=[_row_spec(tr, MLA_LATP), _vec_spec(MLA_QL), _vec_spec(MLA_KVL), _row_spec(tr, 128), _row_spec(tr, 128), _row_spec(tr, 128)],
        out_specs=[_row_spec(tr, MLA_QL), _row_spec(tr, MLA_KVL), _row_spec(tr, 128)],
        out_shape=[jax.ShapeDtypeStruct((T, MLA_QL), BF16), jax.ShapeDtypeStruct((T, MLA_KVL), BF16), jax.ShapeDtypeStruct((T, 128), F32)],
        compiler_params=pltpu.CompilerParams(dimension_semantics=("parallel",)),
    )(lat, qg, kvg, cc, sa, sb)


def _mla_lat_bwd(lat, dcqn, dckvn, dkrot, qg, kvg, cc, sa, sb, name):
    T = lat.shape[0]
    tr = min(T, ROW_TILE)

    def body(lat_ref, dcq_ref, dckv_ref, dkr_ref, qg_ref, kvg_ref, cc_ref, sa_ref, sb_ref, dlat_ref, dqg_ref, dkvg_ref):
        @pl.when(pl.program_id(0) == 0)
        def _():
            dqg_ref[...] = jnp.zeros_like(dqg_ref)
            dkvg_ref[...] = jnp.zeros_like(dkvg_ref)

        lv = lat_ref[...]
        qn, qr, _ = _rms(lv[:, :_KV0], qg_ref[...])
        kn, kr, _ = _rms(lv[:, _KV0:_KR0], kvg_ref[...])
        dcq = dcq_ref[...]
        dckv = dckv_ref[...]
        dqg_ref[...] += _colsum(dcq * qn)
        dkvg_ref[...] += _colsum(dckv * kn)
        dlat_ref[:, :_KV0] = _rms_bwd(dcq, qn, qr, qg_ref[...]).astype(BF16)
        dlat_ref[:, _KV0:_KR0] = _rms_bwd(dckv, kn, kr, kvg_ref[...]).astype(BF16)
        dlat_ref[:, _KR0:] = _rope_bwd(dkr_ref[...], cc_ref[...], sa_ref[...], sb_ref[...]).astype(BF16)

    return pl.pallas_call(
        body, name=name, grid=(T // tr,),
        in_specs=[_row_spec(tr, MLA_LATP), _row_spec(tr, MLA_QL), _row_spec(tr, MLA_KVL), _row_spec(tr, 128),
                  _vec_spec(MLA_QL), _vec_spec(MLA_KVL), _row_spec(tr, 128), _row_spec(tr, 128), _row_spec(tr, 128)],
        out_specs=[_row_spec(tr, MLA_LATP), _vec_spec(MLA_QL), _vec_spec(MLA_KVL)],
        out_shape=[jax.ShapeDtypeStruct((T, MLA_LATP), BF16), jax.ShapeDtypeStruct((1, MLA_QL), F32), jax.ShapeDtypeStruct((1, MLA_KVL), F32)],
        compiler_params=pltpu.CompilerParams(dimension_semantics=("arbitrary",)),
    )(lat, dcqn, dckvn, dkrot, qg, kvg, cc, sa, sb)


def _mla_prep(qpad, kv, krot, cc, sa, sb, name):
    T = qpad.shape[0]
    tr = min(T, MLA_TILE)
    HW = MLA_H * MLA_HP

    def body(q_ref, kv_ref, kr_ref, cc_ref, sa_ref, sb_ref, qo_ref, ko_ref, vo_ref):
        cc, sa, sb = cc_ref[...], sa_ref[...], sb_ref[...]
        krb = kr_ref[...].astype(BF16)
        for hh in range(MLA_H):
            a, m, b = hh * MLA_HP, hh * MLA_HP + MLA_NOPE, (hh + 1) * MLA_HP
            qo_ref[:, a:m] = (q_ref[:, a:m] * SM_SCALE).astype(BF16)
            qo_ref[:, m:b] = (_rope_fwd(q_ref[:, m:b], cc, sa, sb) * SM_SCALE).astype(BF16)
            ko_ref[:, a:m] = kv_ref[:, a:m].astype(BF16)
            ko_ref[:, m:b] = krb
            vo_ref[:, hh * MLA_V:(hh + 1) * MLA_V] = kv_ref[:, m:b].astype(BF16)

    return pl.pallas_call(
        body, name=name, grid=(T // tr,),
        in_specs=[_row_spec(tr, HW), _row_spec(tr, HW), _row_spec(tr, 128), _row_spec(tr, 128), _row_spec(tr, 128), _row_spec(tr, 128)],
        out_specs=[_row_spec(tr, HW), _row_spec(tr, HW), _row_spec(tr, MLA_H * MLA_V)],
        out_shape=[jax.ShapeDtypeStruct((T, HW), BF16), jax.ShapeDtypeStruct((T, HW), BF16), jax.ShapeDtypeStruct((T, MLA_H * MLA_V), BF16)],
        compiler_params=pltpu.CompilerParams(dimension_semantics=("parallel",)),
    )(qpad, kv, krot, cc, sa, sb)


ATT_HG = 4


def _mla_prep_bwd(dqt, dk, dv, cc, sa, sb, name):
    _, nq, _, tq = dqt.shape
    T = nq * tq
    gw = ATT_HG * MLA_HP

    def body(dq_ref, dk_ref, dv_ref, cc_ref, sa_ref, sb_ref, dqp_ref, dkv_ref, dkr_ref):
        @pl.when(pl.program_id(1) == 0)
        def _():
            dkr_ref[...] = jnp.zeros_like(dkr_ref)

        cc, sa, sb = cc_ref[...], sa_ref[...], sb_ref[...]
        acc = jnp.zeros((tq, 128), F32)
        for hh in range(ATT_HG):
            a, m, b = hh * MLA_HP, hh * MLA_HP + MLA_NOPE, (hh + 1) * MLA_HP
            dqh = dq_ref[hh].astype(F32).T * SM_SCALE
            dqp_ref[:, a:m] = dqh[:, :MLA_NOPE].astype(BF16)
            dqp_ref[:, m:b] = _rope_bwd(dqh[:, MLA_NOPE:], cc, sa, sb).astype(BF16)
            dkv_ref[:, a:m] = dk_ref[:, a:m]
            dkv_ref[:, m:b] = dv_ref[:, hh * MLA_V:(hh + 1) * MLA_V]
            acc = acc + dk_ref[:, m:b].astype(F32)
        dkr_ref[...] += acc

    tab = pl.BlockSpec((tq, 128), lambda i, g: (i, 0))
    return pl.pallas_call(
        body, name=name, grid=(nq, MLA_H // ATT_HG),
        in_specs=[pl.BlockSpec((ATT_HG, None, MLA_HP, tq), lambda i, g: (g, i, 0, 0)), pl.BlockSpec((tq, gw), lambda i, g: (i, g)),
                  pl.BlockSpec((tq, ATT_HG * MLA_V), lambda i, g: (i, g)), tab, tab, tab],
        out_specs=[pl.BlockSpec((tq, gw), lambda i, g: (i, g)), pl.BlockSpec((tq, gw), lambda i, g: (i, g)), tab],
        out_shape=[jax.ShapeDtypeStruct((T, MLA_H * MLA_HP), BF16), jax.ShapeDtypeStruct((T, MLA_H * MLA_HP), BF16), jax.ShapeDtypeStruct((T, 128), F32)],
        compiler_params=pltpu.CompilerParams(dimension_semantics=("parallel", "arbitrary")),
    )(dqt, dk, dv, cc, sa, sb)


_NT = (((1,), (1,)), ((), ()))


def _as_row(col, n):
    return jnp.broadcast_to(col, (n, 128)).T[0:1, :]


def _attn_fwd(q, k, vt, name):
    T = q.shape[0]
    tq = tk = min(T, ATT_TILE)
    nq = T // tq

    def body(q_ref, k_ref, vt_ref, o_ref, lse_ref, m_ref, l_ref, acc_ref):
        i = pl.program_id(1)
        qv = q_ref[...]
        m_ref[...] = jnp.full_like(m_ref, NEG_INF)
        l_ref[...] = jnp.zeros_like(l_ref)
        acc_ref[...] = jnp.zeros_like(acc_ref)

        def step(j, diag):
            off = pl.multiple_of(j * tk, tk)
            st = lax.dot_general(k_ref[pl.ds(off, tk), :], qv, _NT, preferred_element_type=F32)
            if diag:
                st = jnp.where(lax.broadcasted_iota(jnp.int32, (tk, tq), 0) <= lax.broadcasted_iota(jnp.int32, (tk, tq), 1), st, NEG_INF)
            m_prev = m_ref[...]
            m_new = jnp.maximum(m_prev, jnp.max(st, axis=0, keepdims=True))
            alpha = jnp.exp(m_prev - m_new)
            pt = jnp.exp(st - m_new)
            l_ref[...] = alpha * l_ref[...] + jnp.sum(pt, axis=0, keepdims=True)
            acc_ref[...] = alpha * acc_ref[...] + jnp.dot(vt_ref[j], pt.astype(BF16), preferred_element_type=F32)
            m_ref[...] = m_new

        def loop_body(j, carry):
            step(j, False)
            return carry

        lax.fori_loop(0, i, loop_body, 0)
        step(i, True)
        o_ref[...] = (acc_ref[...] / l_ref[...]).T.astype(BF16)
        lse_ref[...] = m_ref[...] + jnp.log(l_ref[...])

    return pl.pallas_call(
        body, name=name, grid=(MLA_H, nq),
        in_specs=[pl.BlockSpec((tq, MLA_HP), lambda h, i: (i, h)), pl.BlockSpec((T, MLA_HP), lambda h, i: (0, h)),
                  pl.BlockSpec((None, nq, MLA_V, tk), lambda h, i: (h, 0, 0, 0))],
        out_specs=[pl.BlockSpec((tq, MLA_V), lambda h, i: (i, h)), pl.BlockSpec((None, None, 1, tq), lambda h, i: (h, i, 0, 0))],
        out_shape=[jax.ShapeDtypeStruct((T, MLA_H * MLA_V), BF16), jax.ShapeDtypeStruct((MLA_H, nq, 1, tq), F32)],
        scratch_shapes=[pltpu.VMEM((1, tq), F32), pltpu.VMEM((1, tq), F32), pltpu.VMEM((MLA_V, tq), F32)],
        compiler_params=pltpu.CompilerParams(dimension_semantics=("parallel", "arbitrary")),
    )(q, k, vt)


def _attn_delta(do, o, name):
    T = do.shape[0]
    tq = min(T, ATT_TILE)

    def body(do_ref, o_ref, d_ref):
        for hh in range(MLA_H):
            cs = slice(hh * MLA_V, (hh + 1) * MLA_V)
            s = jnp.sum(do_ref[:, cs].astype(F32) * o_ref[:, cs].astype(F32), axis=-1, keepdims=True)
            d_ref[hh] = _as_row(s, tq)

    return pl.pallas_call(
        body, name=name, grid=(T // tq,),
        in_specs=[_row_spec(tq, MLA_H * MLA_V), _row_spec(tq, MLA_H * MLA_V)],
        out_specs=pl.BlockSpec((MLA_H, None, 1, tq), lambda i: (0, i, 0, 0)),
        out_shape=jax.ShapeDtypeStruct((MLA_H, T // tq, 1, tq), F32),
        compiler_params=pltpu.CompilerParams(dimension_semantics=("parallel",)),
    )(do, o)


def _attn_bwd(q, k, kt, v, do, lse, delta, name):
    T = q.shape[0]
    tq = tk = min(T, ATT_TILE)
    nq = nk = T // tq
    tsd = min(tq, ATT_SUB)

    def body(q_ref, k_ref, kt_ref, v_ref, do_ref, lse_ref, dl_ref, dqt_ref, dk_ref, dv_ref, dq_acc, dk_acc, dv_acc):
        j = pl.program_id(1)

        @pl.when(j == 0)
        def _():
            dq_acc[...] = jnp.zeros_like(dq_acc)

        dk_acc[...] = jnp.zeros_like(dk_acc)
        dv_acc[...] = jnp.zeros_like(dv_acc)

        def step(i, diag):
            off = pl.multiple_of(i * tq, tq)
            lse_i, dl_i = lse_ref[i], dl_ref[i]
            ts, nsub = (tsd, tq // tsd) if diag else (tq, 1)
            for u in range(nsub):
                cols = slice(u * ts, (u + 1) * ts)
                nk_u = (u + 1) * ts if diag else tk
                qi, doi = q_ref[pl.ds(off + u * ts, ts), :], do_ref[pl.ds(off + u * ts, ts), :]
                st = lax.dot_general(k_ref[:nk_u, :], qi, _NT, preferred_element_type=F32)
                if diag:
                    qcol = u * ts + lax.broadcasted_iota(jnp.int32, (nk_u, ts), 1)
                    st = jnp.where(lax.broadcasted_iota(jnp.int32, (nk_u, ts), 0) <= qcol, st, NEG_INF)
                pt = jnp.exp(st - lse_i[:, cols])
                dv_acc[:nk_u, :] += jnp.dot(pt.astype(BF16), doi, preferred_element_type=F32)
                dpt = lax.dot_general(v_ref[:nk_u, :], doi, _NT, preferred_element_type=F32)
                dsb = (pt * (dpt - dl_i[:, cols])).astype(BF16)
                dk_acc[:nk_u, :] += jnp.dot(dsb, qi, preferred_element_type=F32)
                dq_acc[i, :, cols] += jnp.dot(kt_ref[:, :nk_u], dsb, preferred_element_type=F32)

        def loop_body(i, carry):
            step(i, False)
            return carry

        step(j, True)
        lax.fori_loop(j + 1, nq, loop_body, 0)
        dk_ref[...] = dk_acc[...].astype(BF16)
        dv_ref[...] = dv_acc[...].astype(BF16)

        @pl.when(j == nk - 1)
        def _():
            dqt_ref[...] = dq_acc[...].astype(BF16)

    stat = pl.BlockSpec((None, nq, 1, tq), lambda h, j: (h, 0, 0, 0))
    return pl.pallas_call(
        body, name=name, grid=(MLA_H, nk),
        in_specs=[pl.BlockSpec((T, MLA_HP), lambda h, j: (0, h)), pl.BlockSpec((tk, MLA_HP), lambda h, j: (j, h)),
                  pl.BlockSpec((MLA_HP, tk), lambda h, j: (h, j)), pl.BlockSpec((tk, MLA_V), lambda h, j: (j, h)),
                  pl.BlockSpec((T, MLA_V), lambda h, j: (0, h)), stat, stat],
        out_specs=[pl.BlockSpec((None, nq, MLA_HP, tq), lambda h, j: (h, 0, 0, 0)), pl.BlockSpec((tk, MLA_HP), lambda h, j: (j, h)),
                   pl.BlockSpec((tk, MLA_V), lambda h, j: (j, h))],
        out_shape=[jax.ShapeDtypeStruct((MLA_H, nq, MLA_HP, tq), BF16), jax.ShapeDtypeStruct((T, MLA_H * MLA_HP), BF16),
                   jax.ShapeDtypeStruct((T, MLA_H * MLA_V), BF16)],
        scratch_shapes=[pltpu.VMEM((nq, MLA_HP, tq), F32), pltpu.VMEM((tk, MLA_HP), F32), pltpu.VMEM((tk, MLA_V), F32)],
        compiler_params=pltpu.CompilerParams(dimension_semantics=("parallel", "arbitrary")),
    )(q, k, kt, v, do, lse, delta)


ADA_TN = 512


def _silu(v):
    return v * (1.0 / (1.0 + jnp.exp(-v)))


def _ada_fwd(c_all, ada_w, ada_b_loc, name):
    L, D, Nc = ada_w.shape
    B = c_all.shape[0]

    def body(c_ref, w_ref, b_ref, o_ref):
        ca = _silu(c_ref[...]).astype(BF16)
        o_ref[...] = jnp.dot(ca, w_ref[...].astype(BF16), preferred_element_type=F32) + b_ref[...]

    return pl.pallas_call(
        body, name=name, grid=(L, Nc // ADA_TN),
        in_specs=[pl.BlockSpec((B, D), lambda l, n: (0, 0)), pl.BlockSpec((None, D, ADA_TN), lambda l, n: (l, 0, n)),
                  pl.BlockSpec((None, 1, ADA_TN), lambda l, n: (l, 0, n))],
        out_specs=pl.BlockSpec((None, B, ADA_TN), lambda l, n: (l, 0, n)),
        out_shape=jax.ShapeDtypeStruct((L, B, Nc), F32),
        compiler_params=pltpu.CompilerParams(dimension_semantics=("parallel", "parallel")),
    )(c_all, ada_w, ada_b_loc)


def _ada_bwd(c_all_t, dmod_loc, name):
    D, B = c_all_t.shape
    L, _, Nc = dmod_loc.shape

    def body(c_ref, d_ref, o_ref):
        ca = _silu(c_ref[...])
        dv = d_ref[...]
        acc = ca[:, 0:1] * dv[0:1, :]
        for b in range(1, B):
            acc = acc + ca[:, b:b + 1] * dv[b:b + 1, :]
        o_ref[...] = acc

    return pl.pallas_call(
        body, name=name, grid=(L, Nc // ADA_TN),
        in_specs=[pl.BlockSpec((D, B), lambda l, n: (0, 0)), pl.BlockSpec((None, B, ADA_TN), lambda l, n: (l, 0, n))],
        out_specs=pl.BlockSpec((None, D, ADA_TN), lambda l, n: (l, 0, n)),
        out_shape=jax.ShapeDtypeStruct((L, D, Nc), F32),
        compiler_params=pltpu.CompilerParams(dimension_semantics=("parallel", "parallel")),
    )(c_all_t, dmod_loc)


def _sum_lead(parts, name, out_dtype=F32):
    R, C = parts[0].shape[1:]
    n_tot = sum(p.shape[0] for p in parts)
    tr = R
    for cand in (512, 256, 128, 64, 32, 16):
        if R % cand == 0 and cand * C * 4 * n_tot <= (8 << 20):
            tr = cand
            break

    def body(*refs):
        o_ref = refs[-1]
        acc = None
        for r in refs[:-1]:
            for s in range(r.shape[0]):
                acc = r[s].astype(F32) if acc is None else acc + r[s].astype(F32)
        o_ref[...] = acc.astype(o_ref.dtype)

    return pl.pallas_call(
        body, name=name, grid=(R // tr,),
        in_specs=[pl.BlockSpec((p.shape[0], tr, C), lambda i: (0, i, 0)) for p in parts],
        out_specs=pl.BlockSpec((tr, C), lambda i: (i, 0)),
        out_shape=jax.ShapeDtypeStruct((R, C), out_dtype),
        compiler_params=pltpu.CompilerParams(dimension_semantics=("parallel",)),
    )(*parts)


_ADAM_C1 = 1.0 - ADAM_B1 ** ADAM_STEP
_ADAM_C2 = 1.0 - ADAM_B2 ** ADAM_STEP


def _adamw(w, g, m, v, name):
    shape = w.shape
    C = shape[-1]
    R = math.prod(shape[:-1]) if len(shape) > 1 else 1
    w2, g2, m2, v2 = (a.reshape(R, C) for a in (w, g, m, v))
    tr = R
    for cand in (1024, 512, 256, 128, 64, 32, 16, 8):
        if R % cand == 0 and cand * C * 4 <= (1 << 20):
            tr = cand
            break

    def body(w_ref, g_ref, m_ref, v_ref, d_ref, nm_ref, nv_ref):
        gv = g_ref[...]
        mn = ADAM_B1 * m_ref[...] + (1.0 - ADAM_B1) * gv
        vn = ADAM_B2 * v_ref[...] + (1.0 - ADAM_B2) * (gv * gv)
        nm_ref[...] = mn
        nv_ref[...] = vn
        m_hat = mn / _ADAM_C1
        v_hat = vn / _ADAM_C2
        d_ref[...] = -ADAM_LR * (m_hat / (jnp.sqrt(v_hat) + ADAM_EPS) + ADAM_WD * w_ref[...])

    spec = pl.BlockSpec((tr, C), lambda i: (i, 0))
    outs = pl.pallas_call(
        body, name=name, grid=(R // tr,),
        in_specs=[spec] * 4, out_specs=[spec] * 3,
        out_shape=[jax.ShapeDtypeStruct((R, C), F32)] * 3,
        compiler_params=pltpu.CompilerParams(dimension_semantics=("parallel",)),
    )(w2, g2, m2, v2)
    return tuple(o.reshape(shape) for o in outs)


_VMEM_SPEC = pl.BlockSpec(memory_space=pltpu.VMEM)
_HBM_SPEC = pl.BlockSpec(memory_space=pltpu.HBM)


def _flip(v, bit):
    return (1 - v) if bit else v


def _allgather8(v, name):
    def body(v_ref, out_ref, send_sems, recv_sems, local_sem):
        x, y, c = _idx()
        me = 4 * x + 2 * y + c
        mine = pltpu.make_async_copy(v_ref, out_ref.at[me], local_sem)
        mine.start()
        sends = []
        for k in range(1, N_DEV):
            peer = (_flip(x, k & 4), _flip(y, k & 2), _flip(c, k & 1))
            cp = pltpu.make_async_remote_copy(src_ref=v_ref, dst_ref=out_ref.at[me], send_sem=send_sems.at[k - 1], recv_sem=recv_sems.at[k - 1],
                                              device_id=peer, device_id_type=MESH)
            cp.start()
            sends.append(cp)
        for k in range(1, N_DEV):
            px, py, pc = _flip(x, k & 4), _flip(y, k & 2), _flip(c, k & 1)
            src = 4 * px + 2 * py + pc
            pltpu.make_async_remote_copy(src_ref=v_ref, dst_ref=out_ref.at[src], send_sem=send_sems.at[k - 1], recv_sem=recv_sems.at[k - 1],
                                         device_id=(px, py, pc), device_id_type=MESH).wait_recv()
        for cp in sends:
            cp.wait_send()
        mine.wait()

    return pl.pallas_call(
        body, name=name,
        out_shape=jax.ShapeDtypeStruct((N_DEV, *v.shape), v.dtype),
        in_specs=[_VMEM_SPEC], out_specs=_VMEM_SPEC,
        scratch_shapes=[pltpu.SemaphoreType.DMA((N_DEV - 1,)), pltpu.SemaphoreType.DMA((N_DEV - 1,)), pltpu.SemaphoreType.DMA],
    )(v)


def _mod_exchange(modp, name):
    _, L, Nc = modp.shape

    def body(p_ref, out_ref, send_sems, recv_sems, local_sem):
        x, y, c = _idx()
        me, chip = 4 * x + 2 * y + c, 2 * x + y
        mine = pltpu.make_async_copy(p_ref.at[me], out_ref.at[chip], local_sem)
        mine.start()
        sends = []
        for k in range(1, N_CHIPS):
            px, py = _flip(x, k & 2), _flip(y, k & 1)
            cp = pltpu.make_async_remote_copy(src_ref=p_ref.at[4 * px + 2 * py + c], dst_ref=out_ref.at[chip],
                                              send_sem=send_sems.at[k - 1], recv_sem=recv_sems.at[k - 1], device_id=(px, py, c), device_id_type=MESH)
            cp.start()
            sends.append(cp)
        for k in range(1, N_CHIPS):
            px, py = _flip(x, k & 2), _flip(y, k & 1)
            pltpu.make_async_remote_copy(src_ref=p_ref.at[me], dst_ref=out_ref.at[2 * px + py], send_sem=send_sems.at[k - 1],
                                         recv_sem=recv_sems.at[k - 1], device_id=(px, py, c), device_id_type=MESH).wait_recv()
        for cp in sends:
            cp.wait_send()
        mine.wait()

    return pl.pallas_call(
        body, name=name,
        out_shape=jax.ShapeDtypeStruct((N_CHIPS, L, Nc), modp.dtype),
        in_specs=[_VMEM_SPEC], out_specs=_VMEM_SPEC,
        scratch_shapes=[pltpu.SemaphoreType.DMA((N_CHIPS - 1,)), pltpu.SemaphoreType.DMA((N_CHIPS - 1,)), pltpu.SemaphoreType.DMA],
    )(modp)


def _allgather_chips(shards, name):
    n = len(shards)

    def body(*refs):
        ins, outs = refs[:n], refs[n:2 * n]
        send_sems, recv_sems = refs[2 * n:]
        x, y, c = _idx()
        chip = 2 * x + y
        sib = (x, y, 1 - c)
        sends = []
        for a in range(n):
            for k in range(1, N_CHIPS):
                px, py = _flip(x, k & 2), _flip(y, k & 1)
                cp = pltpu.make_async_remote_copy(src_ref=ins[a].at[c], dst_ref=outs[a].at[chip, c], send_sem=send_sems.at[a, k - 1],
                                                  recv_sem=recv_sems.at[a, k - 1], device_id=(px, py, c), device_id_type=MESH)
                cp.start()
                sends.append(cp)
        for a in range(n):
            for k in range(1, N_CHIPS):
                px, py = _flip(x, k & 2), _flip(y, k & 1)
                src = 2 * px + py
                pltpu.make_async_remote_copy(src_ref=ins[a].at[c], dst_ref=outs[a].at[src, c], send_sem=send_sems.at[a, k - 1],
                                             recv_sem=recv_sems.at[a, k - 1], device_id=(px, py, c), device_id_type=MESH).wait_recv()
                cp = pltpu.make_async_remote_copy(src_ref=outs[a].at[src, c], dst_ref=outs[a].at[src, c], send_sem=send_sems.at[a, 2 + k],
                                                  recv_sem=recv_sems.at[a, 2 + k], device_id=sib, device_id_type=MESH)
                cp.start()
                sends.append(cp)
        for a in range(n):
            for k in range(1, N_CHIPS):
                px, py = _flip(x, k & 2), _flip(y, k & 1)
                src = 2 * px + py
                pltpu.make_async_remote_copy(src_ref=ins[a].at[c], dst_ref=outs[a].at[src, 1 - c], send_sem=send_sems.at[a, 2 + k],
                                             recv_sem=recv_sems.at[a, 2 + k], device_id=sib, device_id_type=MESH).wait_recv()
        for cp in sends:
            cp.wait_send()

    return pl.pallas_call(
        body, name=name,
        out_shape=[jax.ShapeDtypeStruct((N_CHIPS, *s.shape), s.dtype) for s in shards],
        in_specs=[_HBM_SPEC] * n, out_specs=[_HBM_SPEC] * n,
        scratch_shapes=[pltpu.SemaphoreType.DMA((n, 6)), pltpu.SemaphoreType.DMA((n, 6))],
    )(*shards)


def _sibling_swap(parts, name):
    n = len(parts)

    def body(*refs):
        ins, outs = refs[:n], refs[n:2 * n]
        send_sems, recv_sems = refs[2 * n:]
        x, y, c = _idx()
        cps = []
        for a in range(n):
            cp = pltpu.make_async_remote_copy(src_ref=ins[a].at[1 - c], dst_ref=outs[a], send_sem=send_sems.at[a], recv_sem=recv_sems.at[a],
                                              device_id=(x, y, 1 - c), device_id_type=MESH)
            cp.start()
            cps.append(cp)
        for cp in cps:
            cp.wait()

    return pl.pallas_call(
        body, name=name,
        out_shape=[jax.ShapeDtypeStruct(p.shape[1:], p.dtype) for p in parts],
        in_specs=[_HBM_SPEC] * n, out_specs=[_HBM_SPEC] * n,
        scratch_shapes=[pltpu.SemaphoreType.DMA((n,)), pltpu.SemaphoreType.DMA((n,))],
    )(*parts)


def _chip_scatter(parts, name):
    n = len(parts)

    def body(*refs):
        ins, outs = refs[:n], refs[n:2 * n]
        send_sems, recv_sems = refs[2 * n:]
        x, y, c = _idx()
        sends = []
        for a in range(n):
            for k in range(1, N_CHIPS):
                px, py = _flip(x, k & 2), _flip(y, k & 1)
                cp = pltpu.make_async_remote_copy(src_ref=ins[a].at[2 * px + py], dst_ref=outs[a].at[k - 1], send_sem=send_sems.at[a, k - 1],
                                                  recv_sem=recv_sems.at[a, k - 1], device_id=(px, py, c), device_id_type=MESH)
                cp.start()
                sends.append(cp)
        for cp in sends:
            cp.wait_recv()
        for cp in sends:
            cp.wait_send()

    return pl.pallas_call(
        body, name=name,
        out_shape=[jax.ShapeDtypeStruct((N_CHIPS - 1, *p.shape[1:]), p.dtype) for p in parts],
        in_specs=[_HBM_SPEC] * n, out_specs=[_HBM_SPEC] * n,
        scratch_shapes=[pltpu.SemaphoreType.DMA((n, 3)), pltpu.SemaphoreType.DMA((n, 3))],
    )(*parts)


def _sibling_send(halves, name):
    n = len(halves)

    def body(*refs):
        ins, outs = refs[:n], refs[n:2 * n]
        send_sems, recv_sems = refs[2 * n:]
        x, y, c = _idx()
        cps = []
        for a in range(n):
            cp = pltpu.make_async_remote_copy(src_ref=ins[a], dst_ref=outs[a], send_sem=send_sems.at[a], recv_sem=recv_sems.at[a],
                                              device_id=(x, y, 1 - c), device_id_type=MESH)
            cp.start()
            cps.append(cp)
        for cp in cps:
            cp.wait()

    return pl.pallas_call(
        body, name=name,
        out_shape=[jax.ShapeDtypeStruct(h.shape, h.dtype) for h in halves],
        in_specs=[_HBM_SPEC] * n, out_specs=[_HBM_SPEC] * n,
        scratch_shapes=[pltpu.SemaphoreType.DMA((n,)), pltpu.SemaphoreType.DMA((n,))],
    )(*halves)


def _col_full(g):
    k, n = g.shape[1], g.shape[2]
    return g.transpose(1, 0, 2).reshape(k, N_CHIPS * n)


def _col_blocks(w):
    k, n = w.shape
    return w.reshape(k, N_CHIPS, n // N_CHIPS).transpose(1, 0, 2)


def _row_blocks(w):
    k, n = w.shape
    return w.reshape(N_CHIPS, k // N_CHIPS, n)


_UQ_HEAD = MLA_NOPE + MLA_ROPE

_SHARDED = {
    "mlp_w1": (DEPTH * D_MODEL, D_MODEL,
               lambda g: g.reshape(N_CHIPS, DEPTH, D_MODEL, D_MODEL).transpose(1, 2, 0, 3).reshape(DEPTH, D_MODEL, 4 * D_MODEL),
               lambda w: w.reshape(DEPTH, D_MODEL, N_CHIPS, D_MODEL).transpose(2, 0, 1, 3).reshape(N_CHIPS, DEPTH * D_MODEL, D_MODEL)),
    "mlp_w2": (DEPTH * D_MODEL, D_MODEL,
               lambda g: g.reshape(N_CHIPS, DEPTH, D_MODEL, D_MODEL).transpose(1, 0, 2, 3).reshape(DEPTH, 4 * D_MODEL, D_MODEL),
               lambda w: w.reshape(DEPTH, N_CHIPS, D_MODEL, D_MODEL).transpose(1, 0, 2, 3).reshape(N_CHIPS, DEPTH * D_MODEL, D_MODEL)),
    "pool_w": (2 * 4 * 64, POOL_GD,
               lambda g: g.reshape(N_CHIPS, 2, 4, 64, POOL_GD).transpose(1, 2, 0, 3, 4).reshape(2, 4, POOL_GD, POOL_GD),
               lambda w: w.reshape(2, 4, N_CHIPS, 64, POOL_GD).transpose(2, 0, 1, 3, 4).reshape(N_CHIPS, 2 * 4 * 64, POOL_GD)),
    "sgu_w_in": (D_MODEL, 2 * SGU_W // N_CHIPS, _col_full, _col_blocks),
    "sgu_w_out": (SGU_W // N_CHIPS, D_MODEL, lambda g: g.reshape(SGU_W, D_MODEL), _row_blocks),
    "mla_w_dq_dkv": (D_MODEL // N_CHIPS, MLA_QL + MLA_KVL + MLA_ROPE,
                     lambda g: jnp.pad(g.reshape(D_MODEL, MLA_QL + MLA_KVL + MLA_ROPE), ((0, 0), (0, MLA_LATP - (MLA_QL + MLA_KVL + MLA_ROPE)))),
                     lambda w: _row_blocks(w[:, :MLA_QL + MLA_KVL + MLA_ROPE])),
    "mla_w_uq": (MLA_QL, MLA_H * _UQ_HEAD // N_CHIPS,
                 lambda g: jnp.pad(_col_full(g).reshape(MLA_QL, MLA_H, _UQ_HEAD), ((0, 0), (0, 0), (0, MLA_HP - _UQ_HEAD))).reshape(MLA_QL, MLA_H * MLA_HP),
                 lambda w: _col_blocks(w.reshape(MLA_QL, MLA_H, MLA_HP)[:, :, :_UQ_HEAD].reshape(MLA_QL, MLA_H * _UQ_HEAD))),
    "mla_w_ukv": (MLA_KVL, MLA_H * (MLA_NOPE + MLA_V) // N_CHIPS, _col_full, _col_blocks),
    "mla_w_o": (MLA_H * MLA_V // N_CHIPS, D_MODEL, lambda g: g.reshape(MLA_H * MLA_V, D_MODEL), _row_blocks),
}
_SHARDED_ORDER = tuple(_SHARDED)


def _rope_tables(positions):
    inv_freq = ROPE_THETA ** (-jnp.arange(0, MLA_ROPE, 2, dtype=F32) / MLA_ROPE)
    ang = positions.astype(F32)[:, None] * inv_freq
    cos, sin = jnp.cos(ang), jnp.sin(ang)
    z32, z64 = jnp.zeros_like(cos), jnp.zeros((positions.shape[0], 64), F32)
    return (jnp.concatenate([cos, cos, z64], axis=1), jnp.concatenate([-sin, z32, z64], axis=1), jnp.concatenate([z32, sin, z64], axis=1))


def _local_step(x, positions, target, mod, W, S):
    D = D_MODEL
    cc, sa, sb = _rope_tables(positions)
    saved = []
    for i in range(DEPTH):
        sh1, sc1, g1, sh2, sc2, g2 = (mod[i:i + 1, n * D:(n + 1) * D] for n in range(6))
        kind, j = i % N_MIXERS, i // N_MIXERS
        gmix, gmlp = S["norm_mix_g"][i:i + 1], S["norm_mlp_g"][i:i + 1]
        st = {"x": x}
        if kind == 0:
            h = _norm_mod_fwd(x, gmix, sc1, sh1, F32, f"l{i}_norm1")
            x2, pooled, ypre = _pool_fwd(h, W["pool_w"][j], S["pool_scale"][j:j + 1], x, g1, f"l{i}_pool")
            st.update(pooled=pooled, y=ypre)
        elif kind == 1:
            h = _norm_mod_fwd(x, gmix, sc1, sh1, BF16, f"l{i}_norm1")
            zz = _mm(h, W["sgu_w_in"], out_dtypes=(F32,), name=f"l{i}_sgu_in")
            bs_t = S["sgu_b_s"].T
            gated = _sgu_gate_fwd(zz, S["sgu_ln_g"], S["sgu_ln_b"], S["sgu_w_s"], bs_t, f"l{i}_sgu_gate")
            x2, y = _mm(gated, W["sgu_w_out"], epi=_epi_residual, extras=((x, "mn"), (g1, "n")), out_dtypes=(F32, BF16), name=f"l{i}_sgu_out")
            st.update(h=h, zz=zz, gated=gated, y=y, bs_t=bs_t)
        else:
            h = _norm_mod_fwd(x, gmix, sc1, sh1, BF16, f"l{i}_norm1")
            lat = _mm(h, W["mla_w_dq_dkv"], out_dtypes=(F32,), name=f"l{i}_mla_lat")
            cqn, ckvn, krot = _mla_lat_fwd(lat, S["mla_q_norm_g"], S["mla_kv_norm_g"], cc, sa, sb, f"l{i}_mla_latn")
            qpad = _mm(cqn, W["mla_w_uq"], out_dtypes=(F32,), name=f"l{i}_mla_uq")
            kv = _mm(ckvn, W["mla_w_ukv"], out_dtypes=(F32,), name=f"l{i}_mla_ukv")
            q, k, v = _mla_prep(qpad, kv, krot, cc, sa, sb, f"l{i}_mla_prep")
            tkb = min(q.shape[0], ATT_TILE)
            vt = v.reshape(q.shape[0] // tkb, tkb, MLA_H, MLA_V).transpose(2, 0, 3, 1)
            o, lse = _attn_fwd(q, k, vt, f"l{i}_attn")
            x2, y = _mm(o, W["mla_w_o"], epi=_epi_residual, extras=((x, "mn"), (g1, "n")), out_dtypes=(F32, BF16), name=f"l{i}_mla_o")
            st.update(h=h, lat=lat, cqn=cqn, ckvn=ckvn, q=q, k=k, kt=k.T, v=v, o=o, lse=lse, y=y)
        h2 = _norm_mod_fwd(x2, gmlp, sc2, sh2, BF16, f"l{i}_norm2")
        z = _mm(h2, W["mlp_w1"][i], epi=lambda acc: (jnp.square(jnp.maximum(acc, 0.0)),), name=f"l{i}_mlp1")
        x3, o2 = _mm(z, W["mlp_w2"][i], epi=_epi_residual, extras=((x2, "mn"), (g2, "n")), out_dtypes=(F32, BF16), name=f"l{i}_mlp2")
        st.update(x2=x2, h2=h2, z=z, o2=o2)
        saved.append(st)
        x = x3

    loss, dx, dfinal_g = _loss_head(x, target, S["final_g"], "loss_head")

    gW = {"mlp_w1": [None] * DEPTH, "mlp_w2": [None] * DEPTH, "pool_w": [None] * 2}
    gS = {"final_g": dfinal_g, "norm_mix_g": [None] * DEPTH, "norm_mlp_g": [None] * DEPTH, "pool_scale": [None] * 2}
    dmod = [None] * DEPTH
    for i in reversed(range(DEPTH)):
        st = saved[i]
        sh1, sc1, g1, sh2, sc2, g2 = (mod[i:i + 1, n * D:(n + 1) * D] for n in range(6))
        kind, j = i % N_MIXERS, i // N_MIXERS
        gmix, gmlp = S["norm_mix_g"][i:i + 1], S["norm_mlp_g"][i:i + 1]
        do2, dg2 = _resid_bwd(dx, st["o2"], g2, f"l{i}_b_res2")
        da = _mm(do2, W["mlp_w2"][i], tb=True, epi=lambda acc, zt: (acc * (2.0 * jnp.sqrt(zt.astype(F32))),), extras=((st["z"], "mn"),), name=f"l{i}_b_dz")
        gW["mlp_w2"][i] = _mm(st["z"], do2, ta=True, name=f"l{i}_b_dw2")
        dh2 = _mm(da, W["mlp_w1"][i], tb=True, out_dtypes=(F32,), name=f"l{i}_b_dh2")
        gW["mlp_w1"][i] = _mm(st["h2"], da, ta=True, name=f"l{i}_b_dw1")
        dx2, dgmlp, dsc2, dsh2 = _norm_mod_bwd(st["x2"], dh2, dx, gmlp, sc2, f"l{i}_b_norm2")
        gS["norm_mlp_g"][i] = dgmlp
        dy, q1 = _resid_bwd(dx2, st["y"], g1, f"l{i}_b_res1")
        if kind == 0:
            dh, dpw, dpsc, dg1 = _pool_bwd(dy, st["pooled"], W["pool_w"][j], S["pool_scale"][j:j + 1], g1, q1, f"l{i}_b_pool")
            gW["pool_w"][j] = dpw
            gS["pool_scale"][j] = dpsc
        elif kind == 1:
            dg1 = q1
            dgated = _mm(dy, W["sgu_w_out"], tb=True, name=f"l{i}_b_dgated")
            gW["sgu_w_out"] = _mm(st["gated"], dy, ta=True, name=f"l{i}_b_dwout")
            dzz, dws, dbs, dlg, dlb = _sgu_gate_bwd(st["zz"], dgated, S["sgu_ln_g"], S["sgu_ln_b"], S["sgu_w_s"], st["bs_t"], f"l{i}_b_sgu_gate")
            gS.update(sgu_w_s=dws, sgu_b_s=dbs[:, :, 0], sgu_ln_g=dlg, sgu_ln_b=dlb)
            dh = _mm(dzz, W["sgu_w_in"], tb=True, out_dtypes=(F32,), name=f"l{i}_b_dh_sgu")
            gW["sgu_w_in"] = _mm(st["h"], dzz, ta=True, name=f"l{i}_b_dwin")
        else:
            dg1 = q1
            do = _mm(dy, W["mla_w_o"], tb=True, name=f"l{i}_b_do")
            gW["mla_w_o"] = _mm(st["o"], dy, ta=True, name=f"l{i}_b_dwo")
            delta = _attn_delta(do, st["o"], f"l{i}_b_delta")
            dqt, dk, dv = _attn_bwd(st["q"], st["k"], st["kt"], st["v"], do, st["lse"], delta, f"l{i}_b_attn")
            dqpad, dkv, dkrot = _mla_prep_bwd(dqt, dk, dv, cc, sa, sb, f"l{i}_b_mla_prep")
            dcqn = _mm(dqpad, W["mla_w_uq"], tb=True, out_dtypes=(F32,), name=f"l{i}_b_dcq")
            gW["mla_w_uq"] = _mm(st["cqn"], dqpad, ta=True, name=f"l{i}_b_dwuq")
            dckvn = _mm(dkv, W["mla_w_ukv"], tb=True, out_dtypes=(F32,), name=f"l{i}_b_dckv")
            gW["mla_w_ukv"] = _mm(st["ckvn"], dkv, ta=True, name=f"l{i}_b_dwukv")
            dlat, dqg, dkvg = _mla_lat_bwd(st["lat"], dcqn, dckvn, dkrot, S["mla_q_norm_g"], S["mla_kv_norm_g"], cc, sa, sb, f"l{i}_b_mla_latn")
            gS.update(mla_q_norm_g=dqg, mla_kv_norm_g=dkvg)
            dh = _mm(dlat, W["mla_w_dq_dkv"], tb=True, out_dtypes=(F32,), name=f"l{i}_b_dh_mla")
            gW["mla_w_dq_dkv"] = _mm(st["h"], dlat, ta=True, name=f"l{i}_b_dwdq")
        dx, dgmix, dsc1, dsh1 = _norm_mod_bwd(st["x"], dh, dx2, gmix, sc1, f"l{i}_b_norm1")
        gS["norm_mix_g"][i] = dgmix
        dmod[i] = jnp.concatenate([dsh1, dsc1, dg1, dsh2, dsc2, dg2], axis=1)

    gW["mlp_w1"] = jnp.stack(gW["mlp_w1"])
    gW["mlp_w2"] = jnp.stack(gW["mlp_w2"])
    gW["pool_w"] = jnp.stack(gW["pool_w"]).astype(BF16)
    for n in ("norm_mix_g", "norm_mlp_g", "pool_scale"):
        gS[n] = jnp.concatenate(gS[n], axis=0)
    return loss, dx, gW, gS, jnp.concatenate(dmod, axis=0)


_SMALL = {
    "norm_mix_g": (DEPTH, D_MODEL), "norm_mlp_g": (DEPTH, D_MODEL), "sgu_ln_g": (1, SGU_W), "sgu_ln_b": (1, SGU_W),
    "sgu_w_s": (SGU_H, SGU_CHUNK, SGU_CHUNK), "sgu_b_s": (SGU_H, SGU_CHUNK), "mla_kv_norm_g": (1, MLA_KVL), "final_g": (1, D_MODEL),
    "pool_scale": (2, D_MODEL), "mla_q_norm_g": (1, MLA_QL), "dmod": (DEPTH, 6 * D_MODEL),
}
_PACK_W = 1024


def _pack(vals):
    flat = jnp.concatenate([v.reshape(-1) for v in vals])
    rows = -(-flat.shape[0] // (8 * _PACK_W)) * 8
    return jnp.pad(flat, (0, rows * _PACK_W - flat.shape[0])).reshape(rows, _PACK_W)


def _unpack(buf, shapes):
    flat, out, off = buf.reshape(-1), [], 0
    for s in shapes:
        n = math.prod(s)
        out.append(flat[off:off + n].reshape(s))
        off += n
    return out


def kernel(x, c, positions, ada_w, ada_b, norm_mix_g, norm_mlp_g, pool_w, pool_scale, sgu_w_in, sgu_ln_g, sgu_ln_b, sgu_w_s, sgu_b_s, sgu_w_out, mla_w_dq_dkv, mla_q_norm_g, mla_kv_norm_g, mla_w_uq, mla_w_ukv, mla_w_o, mlp_w1, mlp_w2, final_g, loss_target, m_ada_w, m_ada_b, m_norm_mix_g, m_norm_mlp_g, m_pool_w, m_pool_scale, m_sgu_w_in, m_sgu_ln_g, m_sgu_ln_b, m_sgu_w_s, m_sgu_b_s, m_sgu_w_out, m_mla_w_dq_dkv, m_mla_q_norm_g, m_mla_kv_norm_g, m_mla_w_uq, m_mla_w_ukv, m_mla_w_o, m_mlp_w1, m_mlp_w2, m_final_g, v_ada_w, v_ada_b, v_norm_mix_g, v_norm_mlp_g, v_pool_w, v_pool_scale, v_sgu_w_in, v_sgu_ln_g, v_sgu_ln_b, v_sgu_w_s, v_sgu_b_s, v_sgu_w_out, v_mla_w_dq_dkv, v_mla_q_norm_g, v_mla_kv_norm_g, v_mla_w_uq, v_mla_w_ukv, v_mla_w_o, v_mlp_w1, v_mlp_w2, v_final_g):
    P = dict(ada_w=ada_w, ada_b=ada_b, norm_mix_g=norm_mix_g, norm_mlp_g=norm_mlp_g, pool_w=pool_w, pool_scale=pool_scale, sgu_w_in=sgu_w_in,
             sgu_ln_g=sgu_ln_g, sgu_ln_b=sgu_ln_b, sgu_w_s=sgu_w_s, sgu_b_s=sgu_b_s, sgu_w_out=sgu_w_out, mla_w_dq_dkv=mla_w_dq_dkv,
             mla_q_norm_g=mla_q_norm_g, mla_kv_norm_g=mla_kv_norm_g, mla_w_uq=mla_w_uq, mla_w_ukv=mla_w_ukv, mla_w_o=mla_w_o, mlp_w1=mlp_w1,
             mlp_w2=mlp_w2, final_g=final_g)
    M = dict(ada_w=m_ada_w, ada_b=m_ada_b, norm_mix_g=m_norm_mix_g, norm_mlp_g=m_norm_mlp_g, pool_w=m_pool_w, pool_scale=m_pool_scale,
             sgu_w_in=m_sgu_w_in, sgu_ln_g=m_sgu_ln_g, sgu_ln_b=m_sgu_ln_b, sgu_w_s=m_sgu_w_s, sgu_b_s=m_sgu_b_s, sgu_w_out=m_sgu_w_out,
             mla_w_dq_dkv=m_mla_w_dq_dkv, mla_q_norm_g=m_mla_q_norm_g, mla_kv_norm_g=m_mla_kv_norm_g, mla_w_uq=m_mla_w_uq, mla_w_ukv=m_mla_w_ukv,
             mla_w_o=m_mla_w_o, mlp_w1=m_mlp_w1, mlp_w2=m_mlp_w2, final_g=m_final_g)
    V = dict(ada_w=v_ada_w, ada_b=v_ada_b, norm_mix_g=v_norm_mix_g, norm_mlp_g=v_norm_mlp_g, pool_w=v_pool_w, pool_scale=v_pool_scale,
             sgu_w_in=v_sgu_w_in, sgu_ln_g=v_sgu_ln_g, sgu_ln_b=v_sgu_ln_b, sgu_w_s=v_sgu_w_s, sgu_b_s=v_sgu_b_s, sgu_w_out=v_sgu_w_out,
             mla_w_dq_dkv=v_mla_w_dq_dkv, mla_q_norm_g=v_mla_q_norm_g, mla_kv_norm_g=v_mla_kv_norm_g, mla_w_uq=v_mla_w_uq, mla_w_ukv=v_mla_w_ukv,
             mla_w_o=v_mla_w_o, mlp_w1=v_mlp_w1, mlp_w2=v_mlp_w2, final_g=v_final_g)
    order = list(P)
    xi, yi, ci = _idx()
    chip = 2 * xi + yi
    D = D_MODEL
    n_ada = ada_w.shape[2]

    pre = _allgather8(_pack([c, pool_scale, mla_q_norm_g]), "ag_small")
    flat = pre.reshape(N_DEV, -1)
    c_all = flat[:, :D]
    ps_all = flat[0::2, D:D + 2 * (D // N_CHIPS)].reshape(N_CHIPS, 2, D // N_CHIPS).transpose(1, 0, 2).reshape(2, D)
    q0 = D + 2 * (D // N_CHIPS)
    qg_all = flat[0::2, q0:q0 + MLA_QL // N_CHIPS].reshape(1, MLA_QL)

    ada_b_loc = lax.dynamic_slice_in_dim(ada_b, chip * n_ada, n_ada, axis=1)[:, None, :]
    modp = _ada_fwd(c_all, ada_w, ada_b_loc, "ada_fwd")
    mod = _mod_exchange(modp.transpose(1, 0, 2), "mod_exchange").transpose(1, 0, 2).reshape(DEPTH, 6 * D)

    shards = [P[n].astype(BF16).reshape(2, _SHARDED[n][0] // 2, _SHARDED[n][1]) for n in _SHARDED_ORDER]
    gathered = _allgather_chips(shards, "ag_weights")
    gathered = [lax.dynamic_update_index_in_dim(g, s, chip, 0) for g, s in zip(gathered, shards, strict=True)]
    W = {n: _SHARDED[n][2](g.reshape(N_CHIPS, _SHARDED[n][0], _SHARDED[n][1])) for n, g in zip(_SHARDED_ORDER, gathered, strict=True)}
    S = dict(norm_mix_g=norm_mix_g, norm_mlp_g=norm_mlp_g, pool_scale=ps_all, sgu_ln_g=sgu_ln_g, sgu_ln_b=sgu_ln_b, sgu_w_s=sgu_w_s[0],
             sgu_b_s=sgu_b_s[0], mla_q_norm_g=qg_all, mla_kv_norm_g=mla_kv_norm_g, final_g=final_g[None, :])

    loss_l, dx, gW, gS, dmod = _local_step(x[0], positions[0], loss_target[0], mod, W, S)
    loss = lax.psum(loss_l[0, 0], ("x", "y", "c"))

    gS["dmod"] = dmod
    small = _allgather8(_pack([gS[n] for n in _SMALL]), "ag_small_grads")
    small_sum = _unpack(_sum_lead([small], "sum_small_grads"), list(_SMALL.values()))
    G = dict(zip(_SMALL, small_sum, strict=True))
    grads = {
        "ada_b": G["dmod"], "norm_mix_g": G["norm_mix_g"], "norm_mlp_g": G["norm_mlp_g"], "sgu_ln_g": G["sgu_ln_g"], "sgu_ln_b": G["sgu_ln_b"],
        "sgu_w_s": G["sgu_w_s"][None], "sgu_b_s": G["sgu_b_s"][None], "mla_kv_norm_g": G["mla_kv_norm_g"], "final_g": G["final_g"][0],
        "pool_scale": lax.dynamic_slice_in_dim(G["pool_scale"], chip * (D // N_CHIPS), D // N_CHIPS, axis=1),
        "mla_q_norm_g": lax.dynamic_slice_in_dim(G["mla_q_norm_g"], chip * (MLA_QL // N_CHIPS), MLA_QL // N_CHIPS, axis=1),
    }
    dmod_all = _unpack(small, [(N_DEV,) + (small.shape[1] * _PACK_W,)])[0]
    off = sum(math.prod(s) for n, s in _SMALL.items() if n != "dmod")
    dmod_all = dmod_all[:, off:off + DEPTH * 6 * D].reshape(N_DEV, DEPTH, 6 * D)
    dmod_loc = lax.dynamic_slice_in_dim(dmod_all, chip * n_ada, n_ada, axis=2).transpose(1, 0, 2)
    grads["ada_w"] = _ada_bwd(c_all.T, dmod_loc, "ada_bwd")

    blocked = []
    for n in _SHARDED_ORDER:
        r, cdim = _SHARDED[n][0], _SHARDED[n][1]
        b = _SHARDED[n][3](gW[n].astype(BF16))
        blocked.append(b.reshape(N_CHIPS, 2, r // 2, cdim).transpose(1, 0, 2, 3))
    from_sib = _sibling_swap(blocked, "rs_sibling")
    pair = []
    for n, b, f in zip(_SHARDED_ORDER, blocked, from_sib, strict=True):
        mine = lax.dynamic_index_in_dim(b, ci, 0, keepdims=False)
        hr, cdim = mine.shape[1], mine.shape[2]
        both = [mine.reshape(1, N_CHIPS * hr, cdim), f.reshape(1, N_CHIPS * hr, cdim)]
        pair.append(_sum_lead(both, f"rs_pair_{n}", BF16).reshape(N_CHIPS, hr, cdim))
    landed = _chip_scatter(pair, "rs_chips")
    halves = [_sum_lead([lax.dynamic_index_in_dim(p, chip, 0, keepdims=True), l], f"rs_sum_{n}")
              for n, p, l in zip(_SHARDED_ORDER, pair, landed, strict=True)]
    from_sib2 = _sibling_send(halves, "rs_merge")
    for n, mine, got in zip(_SHARDED_ORDER, halves, from_sib2, strict=True):
        grads[n] = jnp.where(ci == 0, jnp.stack([mine, got]), jnp.stack([got, mine])).reshape(P[n].shape)

    deltas, new_m, new_v = {}, {}, {}
    for n in order:
        deltas[n], new_m[n], new_v[n] = _adamw(P[n], grads[n].reshape(P[n].shape), M[n], V[n], f"adamw_{n}")
    return (loss, dx[None], *[grads[n].reshape(P[n].shape) for n in order], *[deltas[n] for n in order], *[new_m[n] for n in order],
            *[new_v[n] for n in order])
```

```python
import math

import jax
import jax.numpy as jnp
from jax import lax
from jax.experimental import pallas as pl
from jax.experimental.pallas import tpu as pltpu

F32, BF16 = jnp.float32, jnp.bfloat16
MESH = pl.DeviceIdType.MESH

D_MODEL = 1024
DEPTH = 4
N_MIXERS = 3
POOL_WINDOWS = (2, 4, 8, 16)
POOL_GD = D_MODEL // len(POOL_WINDOWS)
POOL_HALO = 16
SGU_CHUNK = 128
SGU_W = D_MODEL
SGU_HD = 128
SGU_H = SGU_W // SGU_HD
MLA_H = 16
MLA_QL = 256
MLA_KVL = 128
MLA_NOPE = 128
MLA_ROPE = 64
MLA_V = 128
MLA_HP = 256
MLA_LATP = 512
ROPE_THETA = 10000.0
RMS_EPS = 1e-6
LN_EPS = 1e-5
SM_SCALE = (MLA_NOPE + MLA_ROPE) ** -0.5
NEG_INF = -1e30
ADAM_LR, ADAM_B1, ADAM_B2, ADAM_EPS, ADAM_WD, ADAM_STEP = 0.001, 0.9, 0.999, 1e-08, 0.01, 10
N_CHIPS = 4
N_DEV = 8
ROW_TILE = 512
ATT_TILE = 512
ATT_SUB = 256


def _idx():
    return lax.axis_index("x"), lax.axis_index("y"), lax.axis_index("c")


def _mm(a, b, *, name, ta=False, tb=False, epi=None, extras=(), out_dtypes=(BF16,), tm=1024, tn=1024, tk=1024):
    if ta:
        K, M = a.shape
    else:
        M, K = a.shape
    if tb:
        N, Kb = b.shape
    else:
        Kb, N = b.shape
    assert K == Kb, (a.shape, b.shape, ta, tb)
    tm, tn, tk = min(tm, M), min(tn, N), min(tk, K)
    assert M % tm == 0 and N % tn == 0 and K % tk == 0, (M, N, K, tm, tn, tk)
    nk = K // tk
    a_spec = pl.BlockSpec((tk, tm), lambda i, j, k: (k, i)) if ta else pl.BlockSpec((tm, tk), lambda i, j, k: (i, k))
    b_spec = pl.BlockSpec((tn, tk), lambda i, j, k: (j, k)) if tb else pl.BlockSpec((tk, tn), lambda i, j, k: (k, j))
    ex_specs = []
    for arr, kind in extras:
        if kind == "mn":
            ex_specs.append(pl.BlockSpec((tm, tn), lambda i, j, k: (i, j)))
        elif kind == "n":
            ex_specs.append(pl.BlockSpec((1, tn), lambda i, j, k: (0, j)))
        else:
            ex_specs.append(pl.BlockSpec((tm, arr.shape[1]), lambda i, j, k: (i, 0)))
    n_ex, n_out = len(extras), len(out_dtypes)
    dims = (((0 if ta else 1,), (1 if tb else 0,)), ((), ()))

    def body(*refs):
        a_ref, b_ref = refs[0], refs[1]
        ex_refs = refs[2:2 + n_ex]
        out_refs = refs[2 + n_ex:2 + n_ex + n_out]
        part = lax.dot_general(a_ref[...].astype(BF16), b_ref[...].astype(BF16), dims, preferred_element_type=F32)

        def finish(acc):
            outs = epi(acc, *[r[...] for r in ex_refs]) if epi is not None else (acc,)
            for r, o in zip(out_refs, outs, strict=True):
                r[...] = o.astype(r.dtype)

        if nk == 1:
            finish(part)
        else:
            acc_ref = refs[-1]
            k = pl.program_id(2)

            @pl.when(k == 0)
            def _():
                acc_ref[...] = part

            @pl.when(k > 0)
            def _():
                acc_ref[...] += part

            @pl.when(k == nk - 1)
            def _():
                finish(acc_ref[...])

    outs = pl.pallas_call(
        body,
        name=name,
        grid=(M // tm, N // tn, nk),
        in_specs=[a_spec, b_spec, *ex_specs],
        out_specs=[pl.BlockSpec((tm, tn), lambda i, j, k: (i, j)) for _ in range(n_out)],
        out_shape=[jax.ShapeDtypeStruct((M, N), dt) for dt in out_dtypes],
        scratch_shapes=[pltpu.VMEM((tm, tn), F32)] if nk > 1 else [],
        compiler_params=pltpu.CompilerParams(dimension_semantics=("parallel", "parallel", "arbitrary")),
    )(a, b, *[arr for arr, _ in extras])
    return outs[0] if n_out == 1 else tuple(outs)


def _epi_residual(acc, x, g):
    return x + g * acc, acc


def _row_spec(tr, d):
    return pl.BlockSpec((tr, d), lambda i: (i, 0))


def _vec_spec(d):
    return pl.BlockSpec((1, d), lambda i: (0, 0))


def _colsum(v):
    return jnp.sum(v, axis=0, keepdims=True)


def _norm_mod_fwd(x, gain, sc, sh, out_dtype, name):
    T, D = x.shape
    tr = min(T, ROW_TILE)

    def body(x_ref, g_ref, sc_ref, sh_ref, o_ref):
        xv = x_ref[...]
        r = lax.rsqrt(jnp.mean(xv * xv, axis=-1, keepdims=True) + RMS_EPS)
        o_ref[...] = (((xv * r) * g_ref[...]) * (1.0 + sc_ref[...]) + sh_ref[...]).astype(o_ref.dtype)

    return pl.pallas_call(
        body, name=name, grid=(T // tr,),
        in_specs=[_row_spec(tr, D), _vec_spec(D), _vec_spec(D), _vec_spec(D)],
        out_specs=_row_spec(tr, D),
        out_shape=jax.ShapeDtypeStruct((T, D), out_dtype),
        compiler_params=pltpu.CompilerParams(dimension_semantics=("parallel",)),
    )(x, gain, sc, sh)


def _norm_mod_bwd(x, dh, dres, gain, sc, name):
    T, D = x.shape
    tr = min(T, ROW_TILE)

    def body(x_ref, dh_ref, dres_ref, g_ref, sc_ref, dx_ref, dg_ref, dsc_ref, dsh_ref):
        @pl.when(pl.program_id(0) == 0)
        def _():
            dg_ref[...] = jnp.zeros_like(dg_ref)
            dsc_ref[...] = jnp.zeros_like(dsc_ref)
            dsh_ref[...] = jnp.zeros_like(dsh_ref)

        xv = x_ref[...]
        r = lax.rsqrt(jnp.mean(xv * xv, axis=-1, keepdims=True) + RMS_EPS)
        xn = xv * r
        dhv = dh_ref[...].astype(F32)
        dsh_ref[...] += _colsum(dhv)
        dsc_ref[...] += _colsum(dhv * (xn * g_ref[...]))
        dt = dhv * (1.0 + sc_ref[...])
        dg_ref[...] += _colsum(dt * xn)
        dxn = dt * g_ref[...]
        dx_ref[...] = dres_ref[...] + r * (dxn - xn * jnp.mean(dxn * xn, axis=-1, keepdims=True))

    return pl.pallas_call(
        body, name=name, grid=(T // tr,),
        in_specs=[_row_spec(tr, D), _row_spec(tr, D), _row_spec(tr, D), _vec_spec(D), _vec_spec(D)],
        out_specs=[_row_spec(tr, D), _vec_spec(D), _vec_spec(D), _vec_spec(D)],
        out_shape=[jax.ShapeDtypeStruct((T, D), F32)] + [jax.ShapeDtypeStruct((1, D), F32)] * 3,
        compiler_params=pltpu.CompilerParams(dimension_semantics=("arbitrary",)),
    )(x, dh, dres, gain, sc)


def _resid_bwd(dx, y, g, name):
    T, D = dx.shape
    tr = min(T, ROW_TILE)

    def body(dx_ref, y_ref, g_ref, dy_ref, q_ref):
        @pl.when(pl.program_id(0) == 0)
        def _():
            q_ref[...] = jnp.zeros_like(q_ref)

        dxv = dx_ref[...]
        dy_ref[...] = (g_ref[...] * dxv).astype(BF16)
        q_ref[...] += _colsum(dxv * y_ref[...].astype(F32))

    return pl.pallas_call(
        body, name=name, grid=(T // tr,),
        in_specs=[_row_spec(tr, D), _row_spec(tr, D), _vec_spec(D)],
        out_specs=[_row_spec(tr, D), _vec_spec(D)],
        out_shape=[jax.ShapeDtypeStruct((T, D), BF16), jax.ShapeDtypeStruct((1, D), F32)],
        compiler_params=pltpu.CompilerParams(dimension_semantics=("arbitrary",)),
    )(dx, y, g)


def _loss_head(x, target, gain, name):
    T, D = x.shape
    tr = min(T, ROW_TILE)

    def body(x_ref, t_ref, g_ref, loss_ref, dx_ref, dg_ref):
        @pl.when(pl.program_id(0) == 0)
        def _():
            loss_ref[...] = jnp.zeros_like(loss_ref)
            dg_ref[...] = jnp.zeros_like(dg_ref)

        xv = x_ref[...]
        r = lax.rsqrt(jnp.mean(xv * xv, axis=-1, keepdims=True) + RMS_EPS)
        xn = xv * r
        err = xn * g_ref[...] - t_ref[...]
        row = jnp.mean(err * err, axis=-1, keepdims=True)
        loss_ref[...] += 0.5 * jnp.sum(row, axis=0, keepdims=True)
        dy = err * (1.0 / D)
        dg_ref[...] += _colsum(dy * xn)
        dxn = dy * g_ref[...]
        dx_ref[...] = r * (dxn - xn * jnp.mean(dxn * xn, axis=-1, keepdims=True))

    return pl.pallas_call(
        body, name=name, grid=(T // tr,),
        in_specs=[_row_spec(tr, D), _row_spec(tr, D), _vec_spec(D)],
        out_specs=[_vec_spec(128), _row_spec(tr, D), _vec_spec(D)],
        out_shape=[jax.ShapeDtypeStruct((1, 128), F32), jax.ShapeDtypeStruct((T, D), F32), jax.ShapeDtypeStruct((1, D), F32)],
        compiler_params=pltpu.CompilerParams(dimension_semantics=("arbitrary",)),
    )(x, target, gain)


def _pool_fwd(h, w, scale, x, g1, name):
    T, D = h.shape
    tr = min(T, ROW_TILE)

    def body(h_ref, w_ref, sc_ref, x_ref, g_ref, x2_ref, pooled_ref, ypre_ref, halo_ref):
        i = pl.program_id(0)

        @pl.when(i == 0)
        def _():
            halo_ref[...] = jnp.zeros_like(halo_ref)

        hv = h_ref[...]
        buf = jnp.concatenate([halo_ref[...], hv], axis=0)
        halo_ref[...] = hv[tr - POOL_HALO:, :]
        t = (i * tr + lax.broadcasted_iota(jnp.int32, (tr, 1), 0)).astype(F32)
        for gi, win in enumerate(POOL_WINDOWS):
            cols = slice(gi * POOL_GD, (gi + 1) * POOL_GD)
            val = buf[:, cols]
            sh = 1
            while sh < win:
                val = val + pltpu.roll(val, sh, axis=0)
                sh *= 2
            pooled = val[POOL_HALO:, :] / jnp.minimum(t + 1.0, float(win)) - hv[:, cols]
            pb = pooled.astype(BF16)
            pooled_ref[:, cols] = pb
            yp = jnp.dot(pb, w_ref[gi], preferred_element_type=F32)
            ypre_ref[:, cols] = yp.astype(BF16)
            x2_ref[:, cols] = x_ref[:, cols] + g_ref[:, cols] * (yp * sc_ref[:, cols])

    return pl.pallas_call(
        body, name=name, grid=(T // tr,),
        in_specs=[_row_spec(tr, D), pl.BlockSpec(w.shape, lambda i: (0, 0, 0)), _vec_spec(D), _row_spec(tr, D), _vec_spec(D)],
        out_specs=[_row_spec(tr, D)] * 3,
        out_shape=[jax.ShapeDtypeStruct((T, D), F32), jax.ShapeDtypeStruct((T, D), BF16), jax.ShapeDtypeStruct((T, D), BF16)],
        scratch_shapes=[pltpu.VMEM((POOL_HALO, D), F32)],
        compiler_params=pltpu.CompilerParams(dimension_semantics=("arbitrary",)),
    )(h, w, scale, x, g1)


def _pool_bwd(dy, pooled, w, scale, g1, q, name):
    T, D = dy.shape
    tr = min(T, ROW_TILE)
    nt = T // tr
    ltot = tr + POOL_HALO

    def body(dy_ref, pooled_ref, w_ref, sc_ref, g_ref, q_ref, dh_ref, dw_ref, dsc_ref, dg_ref, halo_ref):
        i = pl.program_id(0)

        @pl.when(i == 0)
        def _():
            halo_ref[...] = jnp.zeros_like(halo_ref)
            dw_ref[...] = jnp.zeros_like(dw_ref)
            dsc_ref[...] = g_ref[...] * q_ref[...]
            dg_ref[...] = sc_ref[...] * q_ref[...]

        t = ((nt - 1 - i) * tr + lax.broadcasted_iota(jnp.int32, (tr, 1), 0)).astype(F32)
        for gi, win in enumerate(POOL_WINDOWS):
            cols = slice(gi * POOL_GD, (gi + 1) * POOL_GD)
            dyb = (dy_ref[:, cols].astype(F32) * sc_ref[:, cols]).astype(BF16)
            dw_ref[gi] += lax.dot_general(pooled_ref[:, cols], dyb, (((0,), (0,)), ((), ())), preferred_element_type=F32)
            dpool = lax.dot_general(dyb, w_ref[gi], (((1,), (1,)), ((), ())), preferred_element_type=F32)
            qv = dpool / jnp.minimum(t + 1.0, float(win))
            val = jnp.concatenate([qv, halo_ref[:, cols]], axis=0)
            halo_ref[:, cols] = qv[:POOL_HALO, :]
            sh = 1
            while sh < win:
                val = val + pltpu.roll(val, ltot - sh, axis=0)
                sh *= 2
            dh_ref[:, cols] = val[:tr, :] - dpool

    rev = pl.BlockSpec((tr, D), lambda i: (nt - 1 - i, 0))
    return pl.pallas_call(
        body, name=name, grid=(nt,),
        in_specs=[rev, rev, pl.BlockSpec(w.shape, lambda i: (0, 0, 0)), _vec_spec(D), _vec_spec(D), _vec_spec(D)],
        out_specs=[rev, pl.BlockSpec(w.shape, lambda i: (0, 0, 0)), _vec_spec(D), _vec_spec(D)],
        out_shape=[jax.ShapeDtypeStruct((T, D), F32), jax.ShapeDtypeStruct(w.shape, F32),
                   jax.ShapeDtypeStruct((1, D), F32), jax.ShapeDtypeStruct((1, D), F32)],
        scratch_shapes=[pltpu.VMEM((POOL_HALO, D), F32)],
        compiler_params=pltpu.CompilerParams(dimension_semantics=("arbitrary",)),
    )(dy, pooled, w, scale, g1, q)


_INV_SQRT2 = 0.7071067811865476
_INV_SQRT2PI = 0.3989422804014327


def _gelu(v):
    return 0.5 * v * (1.0 + lax.erf(v * _INV_SQRT2))


def _gelu_grad(v):
    return 0.5 * (1.0 + lax.erf(v * _INV_SQRT2)) + v * jnp.exp(-0.5 * v * v) * _INV_SQRT2PI


def _sgu_ln(v, g, b):
    mu = jnp.mean(v, axis=-1, keepdims=True)
    xc = v - mu
    rstd = lax.rsqrt(jnp.mean(xc * xc, axis=-1, keepdims=True) + LN_EPS)
    xh = xc * rstd
    return xh, rstd, xh * g + b


def _tril_mask():
    return lax.broadcasted_iota(jnp.int32, (SGU_CHUNK, SGU_CHUNK), 0) >= lax.broadcasted_iota(jnp.int32, (SGU_CHUNK, SGU_CHUNK), 1)


SGU_TILE = 256


def _sgu_gate_fwd(zz, ln_g, ln_b, ws, bs_t, name):
    T = zz.shape[0]
    ts = min(T, SGU_TILE)

    def body(zz_ref, g_ref, b_ref, ws_ref, bs_ref, out_ref):
        z = _gelu(zz_ref[...])
        u = z[:, :SGU_W]
        _, _, vn = _sgu_ln(z[:, SGU_W:], g_ref[...], b_ref[...])
        vb = vn.astype(BF16)
        tril = _tril_mask()
        for hh in range(SGU_H):
            wm = jnp.where(tril, ws_ref[hh], 0.0).astype(BF16)
            bcol = bs_ref[:, hh:hh + 1]
            cs = slice(hh * SGU_HD, (hh + 1) * SGU_HD)
            for j in range(ts // SGU_CHUNK):
                rs = slice(j * SGU_CHUNK, (j + 1) * SGU_CHUNK)
                mixed = jnp.dot(wm, vb[rs, cs], preferred_element_type=F32) + bcol
                out_ref[rs, cs] = (u[rs, cs] * mixed).astype(BF16)

    return pl.pallas_call(
        body, name=name, grid=(T // ts,),
        in_specs=[_row_spec(ts, 2 * SGU_W), _vec_spec(SGU_W), _vec_spec(SGU_W),
                  pl.BlockSpec(ws.shape, lambda i: (0, 0, 0)), pl.BlockSpec(bs_t.shape, lambda i: (0, 0))],
        out_specs=_row_spec(ts, SGU_W),
        out_shape=jax.ShapeDtypeStruct((T, SGU_W), BF16),
        compiler_params=pltpu.CompilerParams(dimension_semantics=("parallel",)),
    )(zz, ln_g, ln_b, ws, bs_t)


def _sgu_gate_bwd(zz, dgated, ln_g, ln_b, ws, bs_t, name):
    T = zz.shape[0]
    ts = min(T, SGU_TILE)
    nt = T // ts

    def body(zz_ref, dg_ref, g_ref, b_ref, ws_ref, bs_ref, dzz_ref, dws_ref, dbs_ref, dlg_ref, dlb_ref, dlo_ref, dmx_ref):
        i = pl.program_id(0)

        @pl.when(i == 0)
        def _():
            dws_ref[...] = jnp.zeros_like(dws_ref)
            dmx_ref[...] = jnp.zeros_like(dmx_ref)
            dlg_ref[...] = jnp.zeros_like(dlg_ref)
            dlb_ref[...] = jnp.zeros_like(dlb_ref)

        zzv = zz_ref[...]
        z = _gelu(zzv)
        u = z[:, :SGU_W]
        xh, rstd, vn = _sgu_ln(z[:, SGU_W:], g_ref[...], b_ref[...])
        vb = vn.astype(BF16)
        dgv = dg_ref[...].astype(F32)
        tril = _tril_mask()
        for hh in range(SGU_H):
            wm = jnp.where(tril, ws_ref[hh], 0.0).astype(BF16)
            bcol = bs_ref[:, hh:hh + 1]
            cs = slice(hh * SGU_HD, (hh + 1) * SGU_HD)
            for j in range(ts // SGU_CHUNK):
                rs = slice(j * SGU_CHUNK, (j + 1) * SGU_CHUNK)
                mixed = jnp.dot(wm, vb[rs, cs], preferred_element_type=F32) + bcol
                dmixed = dgv[rs, cs] * u[rs, cs]
                dzz_ref[rs, cs] = (dgv[rs, cs] * mixed * _gelu_grad(zzv[rs, cs])).astype(BF16)
                dmb = dmixed.astype(BF16)
                dws_ref[hh] += lax.dot_general(dmb, vb[rs, cs], (((1,), (1,)), ((), ())), preferred_element_type=F32)
                dmx_ref[hh] += dmixed
                dlo_ref[rs, cs] = lax.dot_general(wm, dmb, (((0,), (0,)), ((), ())), preferred_element_type=F32)
        dlo = dlo_ref[...]
        dlg_ref[...] += _colsum(dlo * xh)
        dlb_ref[...] += _colsum(dlo)
        dxh = dlo * g_ref[...]
        dv = rstd * (dxh - jnp.mean(dxh, axis=-1, keepdims=True) - xh * jnp.mean(dxh * xh, axis=-1, keepdims=True))
        dzz_ref[:, SGU_W:] = (dv * _gelu_grad(zzv[:, SGU_W:])).astype(BF16)

        @pl.when(i == nt - 1)
        def _():
            tril_f = tril.astype(F32)
            for hh in range(SGU_H):
                dws_ref[hh] = dws_ref[hh] * tril_f
                dbs_ref[hh] = jnp.broadcast_to(jnp.sum(dmx_ref[hh], axis=-1, keepdims=True), (SGU_CHUNK, SGU_HD))

    full3 = pl.BlockSpec(ws.shape, lambda i: (0, 0, 0))
    return pl.pallas_call(
        body, name=name, grid=(nt,),
        in_specs=[_row_spec(ts, 2 * SGU_W), _row_spec(ts, SGU_W), _vec_spec(SGU_W), _vec_spec(SGU_W), full3,
                  pl.BlockSpec(bs_t.shape, lambda i: (0, 0))],
        out_specs=[_row_spec(ts, 2 * SGU_W), full3, full3, _vec_spec(SGU_W), _vec_spec(SGU_W)],
        out_shape=[jax.ShapeDtypeStruct((T, 2 * SGU_W), BF16), jax.ShapeDtypeStruct(ws.shape, F32), jax.ShapeDtypeStruct(ws.shape, F32),
                   jax.ShapeDtypeStruct((1, SGU_W), F32), jax.ShapeDtypeStruct((1, SGU_W), F32)],
        scratch_shapes=[pltpu.VMEM((ts, SGU_W), F32), pltpu.VMEM(ws.shape, F32)],
        compiler_params=pltpu.CompilerParams(dimension_semantics=("arbitrary",)),
    )(zz, dgated, ln_g, ln_b, ws, bs_t)


def _rope_fwd(blk, cc, sa, sb):
    return blk * cc + pltpu.roll(blk, 96, axis=1) * sa + pltpu.roll(blk, 32, axis=1) * sb


def _rope_bwd(d, cc, sa, sb):
    return d * cc + pltpu.roll(d * sa, 32, axis=1) + pltpu.roll(d * sb, 96, axis=1)


def _rms(v, g):
    r = lax.rsqrt(jnp.mean(v * v, axis=-1, keepdims=True) + RMS_EPS)
    vn = v * r
    return vn, r, vn * g


def _rms_bwd(dy, vn, r, g):
    dvn = dy * g
    return r * (dvn - vn * jnp.mean(dvn * vn, axis=-1, keepdims=True))


MLA_TILE = 256
_KV0 = MLA_QL
_KR0 = MLA_QL + MLA_KVL


def _mla_lat_fwd(lat, qg, kvg, cc, sa, sb, name):
    T = lat.shape[0]
    tr = min(T, ROW_TILE)

    def body(lat_ref, qg_ref, kvg_ref, cc_ref, sa_ref, sb_ref, cq_ref, ckv_ref, kr_ref):
        lv = lat_ref[...]
        cq_ref[...] = _rms(lv[:, :_KV0], qg_ref[...])[2].astype(BF16)
        ckv_ref[...] = _rms(lv[:, _KV0:_KR0], kvg_ref[...])[2].astype(BF16)
        kr_ref[...] = _rope_fwd(lv[:, _KR0:], cc_ref[...], sa_ref[...], sb_ref[...])

    return pl.pallas_call(
        body, name=name, grid=(T // tr,),
        in_specs=[_row_spec(tr, MLA_LATP), _vec_spec(MLA_QL), _vec_spec(MLA_KVL), _row_spec(tr, 128), _row_spec(tr, 128), _row_spec(tr, 128)],
        out_specs=[_row_spec(tr, MLA_QL), _row_spec(tr, MLA_KVL), _row_spec(tr, 128)],
        out_shape=[jax.ShapeDtypeStruct((T, MLA_QL), BF16), jax.ShapeDtypeStruct((T, MLA_KVL), BF16), jax.ShapeDtypeStruct((T, 128), F32)],
        compiler_params=pltpu.CompilerParams(dimension_semantics=("parallel",)),
    )(lat, qg, kvg, cc, sa, sb)


def _mla_lat_bwd(lat, dcqn, dckvn, dkrot, qg, kvg, cc, sa, sb, name):
    T = lat.shape[0]
    tr = min(T, ROW_TILE)

    def body(lat_ref, dcq_ref, dckv_ref, dkr_ref, qg_ref, kvg_ref, cc_ref, sa_ref, sb_ref, dlat_ref, dqg_ref, dkvg_ref):
        @pl.when(pl.program_id(0) == 0)
        def _():
            dqg_ref[...] = jnp.zeros_like(dqg_ref)
            dkvg_ref[...] = jnp.zeros_like(dkvg_ref)

        lv = lat_ref[...]
        qn, qr, _ = _rms(lv[:, :_KV0], qg_ref[...])
        kn, kr, _ = _rms(lv[:, _KV0:_KR0], kvg_ref[...])
        dcq = dcq_ref[...]
        dckv = dckv_ref[...]
        dqg_ref[...] += _colsum(dcq * qn)
        dkvg_ref[...] += _colsum(dckv * kn)
        dlat_ref[:, :_KV0] = _rms_bwd(dcq, qn, qr, qg_ref[...]).astype(BF16)
        dlat_ref[:, _KV0:_KR0] = _rms_bwd(dckv, kn, kr, kvg_ref[...]).astype(BF16)
        dlat_ref[:, _KR0:] = _rope_bwd(dkr_ref[...], cc_ref[...], sa_ref[...], sb_ref[...]).astype(BF16)

    return pl.pallas_call(
        body, name=name, grid=(T // tr,),
        in_specs=[_row_spec(tr, MLA_LATP), _row_spec(tr, MLA_QL), _row_spec(tr, MLA_KVL), _row_spec(tr, 128),
                  _vec_spec(MLA_QL), _vec_spec(MLA_KVL), _row_spec(tr, 128), _row_spec(tr, 128), _row_spec(tr, 128)],
        out_specs=[_row_spec(tr, MLA_LATP), _vec_spec(MLA_QL), _vec_spec(MLA_KVL)],
        out_shape=[jax.ShapeDtypeStruct((T, MLA_LATP), BF16), jax.ShapeDtypeStruct((1, MLA_QL), F32), jax.ShapeDtypeStruct((1, MLA_KVL), F32)],
        compiler_params=pltpu.CompilerParams(dimension_semantics=("arbitrary",)),
    )(lat, dcqn, dckvn, dkrot, qg, kvg, cc, sa, sb)


def _mla_prep(qpad, kv, krot, cc, sa, sb, name):
    T = qpad.shape[0]
    tr = min(T, MLA_TILE)
    HW = MLA_H * MLA_HP

    def body(q_ref, kv_ref, kr_ref, cc_ref, sa_ref, sb_ref, qo_ref, ko_ref, vo_ref):
        cc, sa, sb = cc_ref[...], sa_ref[...], sb_ref[...]
        krb = kr_ref[...].astype(BF16)
        for hh in range(MLA_H):
            a, m, b = hh * MLA_HP, hh * MLA_HP + MLA_NOPE, (hh + 1) * MLA_HP
            qo_ref[:, a:m] = (q_ref[:, a:m] * SM_SCALE).astype(BF16)
            qo_ref[:, m:b] = (_rope_fwd(q_ref[:, m:b], cc, sa, sb) * SM_SCALE).astype(BF16)
            ko_ref[:, a:m] = kv_ref[:, a:m].astype(BF16)
            ko_ref[:, m:b] = krb
            vo_ref[:, hh * MLA_V:(hh + 1) * MLA_V] = kv_ref[:, m:b].astype(BF16)

    return pl.pallas_call(
        body, name=name, grid=(T // tr,),
        in_specs=[_row_spec(tr, HW), _row_spec(tr, HW), _row_spec(tr, 128), _row_spec(tr, 128), _row_spec(tr, 128), _row_spec(tr, 128)],
        out_specs=[_row_spec(tr, HW), _row_spec(tr, HW), _row_spec(tr, MLA_H * MLA_V)],
        out_shape=[jax.ShapeDtypeStruct((T, HW), BF16), jax.ShapeDtypeStruct((T, HW), BF16), jax.ShapeDtypeStruct((T, MLA_H * MLA_V), BF16)],
        compiler_params=pltpu.CompilerParams(dimension_semantics=("parallel",)),
    )(qpad, kv, krot, cc, sa, sb)


ATT_HG = 4


def _mla_prep_bwd(dqt, dk, dv, cc, sa, sb, name):
    _, nq, _, tq = dqt.shape
    T = nq * tq
    gw = ATT_HG * MLA_HP

    def body(dq_ref, dk_ref, dv_ref, cc_ref, sa_ref, sb_ref, dqp_ref, dkv_ref, dkr_ref):
        @pl.when(pl.program_id(1) == 0)
        def _():
            dkr_ref[...] = jnp.zeros_like(dkr_ref)

        cc, sa, sb = cc_ref[...], sa_ref[...], sb_ref[...]
        acc = jnp.zeros((tq, 128), F32)
        for hh in range(ATT_HG):
            a, m, b = hh * MLA_HP, hh * MLA_HP + MLA_NOPE, (hh + 1) * MLA_HP
            dqh = dq_ref[hh].astype(F32).T * SM_SCALE
            dqp_ref[:, a:m] = dqh[:, :MLA_NOPE].astype(BF16)
            dqp_ref[:, m:b] = _rope_bwd(dqh[:, MLA_NOPE:], cc, sa, sb).astype(BF16)
            dkv_ref[:, a:m] = dk_ref[:, a:m]
            dkv_ref[:, m:b] = dv_ref[:, hh * MLA_V:(hh + 1) * MLA_V]
            acc = acc + dk_ref[:, m:b].astype(F32)
        dkr_ref[...] += acc

    tab = pl.BlockSpec((tq, 128), lambda i, g: (i, 0))
    return pl.pallas_call(
        body, name=name, grid=(nq, MLA_H // ATT_HG),
        in_specs=[pl.BlockSpec((ATT_HG, None, MLA_HP, tq), lambda i, g: (g, i, 0, 0)), pl.BlockSpec((tq, gw), lambda i, g: (i, g)),
                  pl.BlockSpec((tq, ATT_HG * MLA_V), lambda i, g: (i, g)), tab, tab, tab],
        out_specs=[pl.BlockSpec((tq, gw), lambda i, g: (i, g)), pl.BlockSpec((tq, gw), lambda i, g: (i, g)), tab],
        out_shape=[jax.ShapeDtypeStruct((T, MLA_H * MLA_HP), BF16), jax.ShapeDtypeStruct((T, MLA_H * MLA_HP), BF16), jax.ShapeDtypeStruct((T, 128), F32)],
        compiler_params=pltpu.CompilerParams(dimension_semantics=("parallel", "arbitrary")),
    )(dqt, dk, dv, cc, sa, sb)


_NT = (((1,), (1,)), ((), ()))


def _as_row(col, n):
    return jnp.broadcast_to(col, (n, 128)).T[0:1, :]


def _attn_fwd(q, k, vt, name):
    T = q.shape[0]
    tq = tk = min(T, ATT_TILE)
    nq = T // tq

    def body(q_ref, k_ref, vt_ref, o_ref, lse_ref, m_ref, l_ref, acc_ref):
        i = pl.program_id(1)
        qv = q_ref[...]
        m_ref[...] = jnp.full_like(m_ref, NEG_INF)
        l_ref[...] = jnp.zeros_like(l_ref)
        acc_ref[...] = jnp.zeros_like(acc_ref)

        def step(j, diag):
            off = pl.multiple_of(j * tk, tk)
            st = lax.dot_general(k_ref[pl.ds(off, tk), :], qv, _NT, preferred_element_type=F32)
            if diag:
                st = jnp.where(lax.broadcasted_iota(jnp.int32, (tk, tq), 0) <= lax.broadcasted_iota(jnp.int32, (tk, tq), 1), st, NEG_INF)
            m_prev = m_ref[...]
            m_new = jnp.maximum(m_prev, jnp.max(st, axis=0, keepdims=True))
            alpha = jnp.exp(m_prev - m_new)
            pt = jnp.exp(st - m_new)
            l_ref[...] = alpha * l_ref[...] + jnp.sum(pt, axis=0, keepdims=True)
            acc_ref[...] = alpha * acc_ref[...] + jnp.dot(vt_ref[j], pt.astype(BF16), preferred_element_type=F32)
            m_ref[...] = m_new

        def loop_body(j, carry):
            step(j, False)
            return carry

        lax.fori_loop(0, i, loop_body, 0)
        step(i, True)
        o_ref[...] = (acc_ref[...] / l_ref[...]).T.astype(BF16)
        lse_ref[...] = m_ref[...] + jnp.log(l_ref[...])

    return pl.pallas_call(
        body, name=name, grid=(MLA_H, nq),
        in_specs=[pl.BlockSpec((tq, MLA_HP), lambda h, i: (i, h)), pl.BlockSpec((T, MLA_HP), lambda h, i: (0, h)),
                  pl.BlockSpec((None, nq, MLA_V, tk), lambda h, i: (h, 0, 0, 0))],
        out_specs=[pl.BlockSpec((tq, MLA_V), lambda h, i: (i, h)), pl.BlockSpec((None, None, 1, tq), lambda h, i: (h, i, 0, 0))],
        out_shape=[jax.ShapeDtypeStruct((T, MLA_H * MLA_V), BF16), jax.ShapeDtypeStruct((MLA_H, nq, 1, tq), F32)],
        scratch_shapes=[pltpu.VMEM((1, tq), F32), pltpu.VMEM((1, tq), F32), pltpu.VMEM((MLA_V, tq), F32)],
        compiler_params=pltpu.CompilerParams(dimension_semantics=("parallel", "arbitrary")),
    )(q, k, vt)


def _attn_delta(do, o, name):
    T = do.shape[0]
    tq = min(T, ATT_TILE)

    def body(do_ref, o_ref, d_ref):
        for hh in range(MLA_H):
            cs = slice(hh * MLA_V, (hh + 1) * MLA_V)
            s = jnp.sum(do_ref[:, cs].astype(F32) * o_ref[:, cs].astype(F32), axis=-1, keepdims=True)
            d_ref[hh] = _as_row(s, tq)

    return pl.pallas_call(
        body, name=name, grid=(T // tq,),
        in_specs=[_row_spec(tq, MLA_H * MLA_V), _row_spec(tq, MLA_H * MLA_V)],
        out_specs=pl.BlockSpec((MLA_H, None, 1, tq), lambda i: (0, i, 0, 0)),
        out_shape=jax.ShapeDtypeStruct((MLA_H, T // tq, 1, tq), F32),
        compiler_params=pltpu.CompilerParams(dimension_semantics=("parallel",)),
    )(do, o)


def _attn_bwd(q, k, kt, v, do, lse, delta, name):
    T = q.shape[0]
    tq = tk = min(T, ATT_TILE)
    nq = nk = T // tq
    tsd = min(tq, ATT_SUB)

    def body(q_ref, k_ref, kt_ref, v_ref, do_ref, lse_ref, dl_ref, dqt_ref, dk_ref, dv_ref, dq_acc, dk_acc, dv_acc):
        j = pl.program_id(1)

        @pl.when(j == 0)
        def _():
            dq_acc[...] = jnp.zeros_like(dq_acc)

        dk_acc[...] = jnp.zeros_like(dk_acc)
        dv_acc[...] = jnp.zeros_like(dv_acc)

        def step(i, diag):
            off = pl.multiple_of(i * tq, tq)
            lse_i, dl_i = lse_ref[i], dl_ref[i]
            ts, nsub = (tsd, tq // tsd) if diag else (tq, 1)
            for u in range(nsub):
                cols = slice(u * ts, (u + 1) * ts)
                nk_u = (u + 1) * ts if diag else tk
                qi, doi = q_ref[pl.ds(off + u * ts, ts), :], do_ref[pl.ds(off + u * ts, ts), :]
                st = lax.dot_general(k_ref[:nk_u, :], qi, _NT, preferred_element_type=F32)
                if diag:
                    qcol = u * ts + lax.broadcasted_iota(jnp.int32, (nk_u, ts), 1)
                    st = jnp.where(lax.broadcasted_iota(jnp.int32, (nk_u, ts), 0) <= qcol, st, NEG_INF)
                pt = jnp.exp(st - lse_i[:, cols])
                dv_acc[:nk_u, :] += jnp.dot(pt.astype(BF16), doi, preferred_element_type=F32)
                dpt = lax.dot_general(v_ref[:nk_u, :], doi, _NT, preferred_element_type=F32)
                dsb = (pt * (dpt - dl_i[:, cols])).astype(BF16)
                dk_acc[:nk_u, :] += jnp.dot(dsb, qi, preferred_element_type=F32)
                dq_acc[i, :, cols] += jnp.dot(kt_ref[:, :nk_u], dsb, preferred_element_type=F32)

        def loop_body(i, carry):
            step(i, False)
            return carry

        step(j, True)
        lax.fori_loop(j + 1, nq, loop_body, 0)
        dk_ref[...] = dk_acc[...].astype(BF16)
        dv_ref[...] = dv_acc[...].astype(BF16)

        @pl.when(j == nk - 1)
        def _():
            dqt_ref[...] = dq_acc[...].astype(BF16)

    stat = pl.BlockSpec((None, nq, 1, tq), lambda h, j: (h, 0, 0, 0))
    return pl.pallas_call(
        body, name=name, grid=(MLA_H, nk),
        in_specs=[pl.BlockSpec((T, MLA_HP), lambda h, j: (0, h)), pl.BlockSpec((tk, MLA_HP), lambda h, j: (j, h)),
                  pl.BlockSpec((MLA_HP, tk), lambda h, j: (h, j)), pl.BlockSpec((tk, MLA_V), lambda h, j: (j, h)),
                  pl.BlockSpec((T, MLA_V), lambda h, j: (0, h)), stat, stat],
        out_specs=[pl.BlockSpec((None, nq, MLA_HP, tq), lambda h, j: (h, 0, 0, 0)), pl.BlockSpec((tk, MLA_HP), lambda h, j: (j, h)),
                   pl.BlockSpec((tk, MLA_V), lambda h, j: (j, h))],
        out_shape=[jax.ShapeDtypeStruct((MLA_H, nq, MLA_HP, tq), BF16), jax.ShapeDtypeStruct((T, MLA_H * MLA_HP), BF16),
                   jax.ShapeDtypeStruct((T, MLA_H * MLA_V), BF16)],
        scratch_shapes=[pltpu.VMEM((nq, MLA_HP, tq), F32), pltpu.VMEM((tk, MLA_HP), F32), pltpu.VMEM((tk, MLA_V), F32)],
        compiler_params=pltpu.CompilerParams(dimension_semantics=("parallel", "arbitrary")),
    )(q, k, kt, v, do, lse, delta)


ADA_TN = 512


def _silu(v):
    return v * (1.0 / (1.0 + jnp.exp(-v)))


def _ada_fwd(c_all, ada_w, ada_b_loc, name):
    L, D, Nc = ada_w.shape
    B = c_all.shape[0]

    def body(c_ref, w_ref, b_ref, o_ref):
        ca = _silu(c_ref[...]).astype(BF16)
        o_ref[...] = jnp.dot(ca, w_ref[...].astype(BF16), preferred_element_type=F32) + b_ref[...]

    return pl.pallas_call(
        body, name=name, grid=(L, Nc // ADA_TN),
        in_specs=[pl.BlockSpec((B, D), lambda l, n: (0, 0)), pl.BlockSpec((None, D, ADA_TN), lambda l, n: (l, 0, n)),
                  pl.BlockSpec((None, 1, ADA_TN), lambda l, n: (l, 0, n))],
        out_specs=pl.BlockSpec((None, B, ADA_TN), lambda l, n: (l, 0, n)),
        out_shape=jax.ShapeDtypeStruct((L, B, Nc), F32),
        compiler_params=pltpu.CompilerParams(dimension_semantics=("parallel", "parallel")),
    )(c_all, ada_w, ada_b_loc)


def _ada_bwd(c_all_t, dmod_loc, name):
    D, B = c_all_t.shape
    L, _, Nc = dmod_loc.shape

    def body(c_ref, d_ref, o_ref):
        ca = _silu(c_ref[...])
        dv = d_ref[...]
        acc = ca[:, 0:1] * dv[0:1, :]
        for b in range(1, B):
            acc = acc + ca[:, b:b + 1] * dv[b:b + 1, :]
        o_ref[...] = acc

    return pl.pallas_call(
        body, name=name, grid=(L, Nc // ADA_TN),
        in_specs=[pl.BlockSpec((D, B), lambda l, n: (0, 0)), pl.BlockSpec((None, B, ADA_TN), lambda l, n: (l, 0, n))],
        out_specs=pl.BlockSpec((None, D, ADA_TN), lambda l, n: (l, 0, n)),
        out_shape=jax.ShapeDtypeStruct((L, D, Nc), F32),
        compiler_params=pltpu.CompilerParams(dimension_semantics=("parallel", "parallel")),
    )(c_all_t, dmod_loc)


def _sum_lead(parts, name, out_dtype=F32):
    R, C = parts[0].shape[1:]
    n_tot = sum(p.shape[0] for p in parts)
    tr = R
    for cand in (512, 256, 128, 64, 32, 16):
        if R % cand == 0 and cand * C * 4 * n_tot <= (8 << 20):
            tr = cand
            break

    def body(*refs):
        o_ref = refs[-1]
        acc = None
        for r in refs[:-1]:
            for s in range(r.shape[0]):
                acc = r[s].astype(F32) if acc is None else acc + r[s].astype(F32)
        o_ref[...] = acc.astype(o_ref.dtype)

    return pl.pallas_call(
        body, name=name, grid=(R // tr,),
        in_specs=[pl.BlockSpec((p.shape[0], tr, C), lambda i: (0, i, 0)) for p in parts],
        out_specs=pl.BlockSpec((tr, C), lambda i: (i, 0)),
        out_shape=jax.ShapeDtypeStruct((R, C), out_dtype),
        compiler_params=pltpu.CompilerParams(dimension_semantics=("parallel",)),
    )(*parts)


_ADAM_C1 = 1.0 - ADAM_B1 ** ADAM_STEP
_ADAM_C2 = 1.0 - ADAM_B2 ** ADAM_STEP


def _adamw(w, g, m, v, name):
    shape = w.shape
    C = shape[-1]
    R = math.prod(shape[:-1]) if len(shape) > 1 else 1
    w2, g2, m2, v2 = (a.reshape(R, C) for a in (w, g, m, v))
    tr = R
    for cand in (1024, 512, 256, 128, 64, 32, 16, 8):
        if R % cand == 0 and cand * C * 4 <= (1 << 20):
            tr = cand
            break

    def body(w_ref, g_ref, m_ref, v_ref, d_ref, nm_ref, nv_ref):
        gv = g_ref[...]
        mn = ADAM_B1 * m_ref[...] + (1.0 - ADAM_B1) * gv
        vn = ADAM_B2 * v_ref[...] + (1.0 - ADAM_B2) * (gv * gv)
        nm_ref[...] = mn
        nv_ref[...] = vn
        m_hat = mn / _ADAM_C1
        v_hat = vn / _ADAM_C2
        d_ref[...] = -ADAM_LR * (m_hat / (jnp.sqrt(v_hat) + ADAM_EPS) + ADAM_WD * w_ref[...])

    spec = pl.BlockSpec((tr, C), lambda i: (i, 0))
    outs = pl.pallas_call(
        body, name=name, grid=(R // tr,),
        in_specs=[spec] * 4, out_specs=[spec] * 3,
        out_shape=[jax.ShapeDtypeStruct((R, C), F32)] * 3,
        compiler_params=pltpu.CompilerParams(dimension_semantics=("parallel",)),
    )(w2, g2, m2, v2)
    return tuple(o.reshape(shape) for o in outs)


def _row_tile(rows, cols, itemsize, budget):
    for cand in (1024, 512, 256, 128, 64, 32, 16):
        if rows % cand == 0 and cand * cols * itemsize <= budget:
            return cand
    return rows


def _sum_sel(sel, stacked, others, name, out_dtype):
    R, C = stacked.shape[1:]
    n_tot = 1 + sum(o.shape[0] for o in others)
    tr = _row_tile(R, C, 4 * n_tot, 8 << 20)

    def body(sel_ref, s_ref, *refs):
        o_ref = refs[-1]
        acc = s_ref[...].astype(F32)
        for r in refs[:-1]:
            for s in range(r.shape[0]):
                acc = acc + r[s].astype(F32)
        o_ref[...] = acc.astype(o_ref.dtype)

    return pl.pallas_call(
        body, name=name,
        grid_spec=pltpu.PrefetchScalarGridSpec(
            num_scalar_prefetch=1, grid=(R // tr,),
            in_specs=[pl.BlockSpec((None, tr, C), lambda i, s: (s[0], i, 0))] + [pl.BlockSpec((o.shape[0], tr, C), lambda i, s: (0, i, 0)) for o in others],
            out_specs=pl.BlockSpec((tr, C), lambda i, s: (i, 0))),
        out_shape=jax.ShapeDtypeStruct((R, C), out_dtype),
        compiler_params=pltpu.CompilerParams(dimension_semantics=("parallel",)),
    )(sel, stacked, *others)


def _adamw_piece(cidx, w2, m2, v2, mine, got, bufs, row0, name):
    hr, C = mine.shape
    tr = _row_tile(math.gcd(hr, row0) if row0 else hr, C, 4, 1 << 20)
    nt = hr // tr

    def body(c_ref, w_ref, m_ref, v_ref, a_ref, b_ref, _g, _d, _nm, _nv, g_ref, d_ref, nm_ref, nv_ref):
        gv = jnp.where(pl.program_id(0) == c_ref[0], a_ref[...], b_ref[...])
        mn = ADAM_B1 * m_ref[...] + (1.0 - ADAM_B1) * gv
        vn = ADAM_B2 * v_ref[...] + (1.0 - ADAM_B2) * (gv * gv)
        g_ref[...] = gv
        nm_ref[...] = mn
        nv_ref[...] = vn
        d_ref[...] = -ADAM_LR * ((mn / _ADAM_C1) / (jnp.sqrt(vn / _ADAM_C2) + ADAM_EPS) + ADAM_WD * w_ref[...])

    rows = pl.BlockSpec((tr, C), lambda hf, t, c: (row0 // tr + hf * nt + t, 0))
    half = pl.BlockSpec((tr, C), lambda hf, t, c: (t, 0))
    return pl.pallas_call(
        body, name=name,
        grid_spec=pltpu.PrefetchScalarGridSpec(num_scalar_prefetch=1, grid=(2, nt), in_specs=[rows] * 3 + [half] * 2 + [_ANY_SPEC] * 4,
                                               out_specs=[rows] * 4),
        out_shape=[jax.ShapeDtypeStruct(w2.shape, F32)] * 4,
        input_output_aliases={6 + n: n for n in range(4)},
        compiler_params=pltpu.CompilerParams(dimension_semantics=("parallel", "parallel")),
    )(cidx, w2, m2, v2, mine, got, *bufs)


_VMEM_SPEC = pl.BlockSpec(memory_space=pltpu.VMEM)
_HBM_SPEC = pl.BlockSpec(memory_space=pltpu.HBM)


def _flip(v, bit):
    return (1 - v) if bit else v


def _allgather8(v, name):
    def body(v_ref, out_ref, send_sems, recv_sems, local_sem):
        x, y, c = _idx()
        me = 4 * x + 2 * y + c
        mine = pltpu.make_async_copy(v_ref, out_ref.at[me], local_sem)
        mine.start()
        sends = []
        for k in range(1, N_DEV):
            peer = (_flip(x, k & 4), _flip(y, k & 2), _flip(c, k & 1))
            cp = pltpu.make_async_remote_copy(src_ref=v_ref, dst_ref=out_ref.at[me], send_sem=send_sems.at[k - 1], recv_sem=recv_sems.at[k - 1],
                                              device_id=peer, device_id_type=MESH)
            cp.start()
            sends.append(cp)
        for k in range(1, N_DEV):
            px, py, pc = _flip(x, k & 4), _flip(y, k & 2), _flip(c, k & 1)
            src = 4 * px + 2 * py + pc
            pltpu.make_async_remote_copy(src_ref=v_ref, dst_ref=out_ref.at[src], send_sem=send_sems.at[k - 1], recv_sem=recv_sems.at[k - 1],
                                         device_id=(px, py, pc), device_id_type=MESH).wait_recv()
        for cp in sends:
            cp.wait_send()
        mine.wait()

    return pl.pallas_call(
        body, name=name,
        out_shape=jax.ShapeDtypeStruct((N_DEV, *v.shape), v.dtype),
        in_specs=[_VMEM_SPEC], out_specs=_VMEM_SPEC,
        scratch_shapes=[pltpu.SemaphoreType.DMA((N_DEV - 1,)), pltpu.SemaphoreType.DMA((N_DEV - 1,)), pltpu.SemaphoreType.DMA],
    )(v)


def _mod_exchange(modp, name):
    _, L, Nc = modp.shape

    def body(p_ref, out_ref, send_sems, recv_sems, local_sem):
        x, y, c = _idx()
        me, chip = 4 * x + 2 * y + c, 2 * x + y
        mine = pltpu.make_async_copy(p_ref.at[me], out_ref.at[chip], local_sem)
        mine.start()
        sends = []
        for k in range(1, N_CHIPS):
            px, py = _flip(x, k & 2), _flip(y, k & 1)
            cp = pltpu.make_async_remote_copy(src_ref=p_ref.at[4 * px + 2 * py + c], dst_ref=out_ref.at[chip],
                                              send_sem=send_sems.at[k - 1], recv_sem=recv_sems.at[k - 1], device_id=(px, py, c), device_id_type=MESH)
            cp.start()
            sends.append(cp)
        for k in range(1, N_CHIPS):
            px, py = _flip(x, k & 2), _flip(y, k & 1)
            pltpu.make_async_remote_copy(src_ref=p_ref.at[me], dst_ref=out_ref.at[2 * px + py], send_sem=send_sems.at[k - 1],
                                         recv_sem=recv_sems.at[k - 1], device_id=(px, py, c), device_id_type=MESH).wait_recv()
        for cp in sends:
            cp.wait_send()
        mine.wait()

    return pl.pallas_call(
        body, name=name,
        out_shape=jax.ShapeDtypeStruct((N_CHIPS, L, Nc), modp.dtype),
        in_specs=[_VMEM_SPEC], out_specs=_VMEM_SPEC,
        scratch_shapes=[pltpu.SemaphoreType.DMA((N_CHIPS - 1,)), pltpu.SemaphoreType.DMA((N_CHIPS - 1,)), pltpu.SemaphoreType.DMA],
    )(modp)


_SEM_SPEC = pl.BlockSpec(memory_space=pltpu.SEMAPHORE)
_ANY_SPEC = pl.BlockSpec(memory_space=pl.ANY)
_EFFECT = pltpu.SideEffectType.DATAFLOW_SIDE_EFFECTING


def _hbm(a):
    return pltpu.with_memory_space_constraint(a, pltpu.HBM)


def _xchip_copies(mode, srcs, lands, send_sems, recv_sems, waiting):
    x, y, c = _idx()
    chip = 2 * x + y
    out = []
    for a in range(len(srcs)):
        for k in range(1, N_CHIPS):
            px, py = _flip(x, k & 2), _flip(y, k & 1)
            peer = 2 * px + py
            if mode == "gather":
                src, dst, mine = srcs[a].at[c], lands[a].at[chip, c], lands[a].at[peer, c]
            else:
                src, dst, mine = srcs[a].at[peer], lands[a].at[k - 1], lands[a].at[k - 1]
            q = a * (N_CHIPS - 1) + k - 1
            out.append(pltpu.make_async_remote_copy(src_ref=src, dst_ref=mine if waiting else dst, send_sem=send_sems[q], recv_sem=recv_sems[q],
                                                    device_id=(px, py, c), device_id_type=MESH))
    return out


def _xchip_start(mode, srcs, land_shapes, dep, name):
    n = len(srcs)
    ns = n * (N_CHIPS - 1)

    def body(*refs):
        src_refs, land_refs = refs[:n], refs[n:2 * n]
        outs = refs[2 * n + 1:]
        for cp in _xchip_copies(mode, src_refs, land_refs, outs[:ns], outs[ns:2 * ns], waiting=False):
            cp.start()
        outs[-1][...] = jnp.zeros_like(outs[-1])

    lands = [_hbm(lax.empty(s.shape, s.dtype)) for s in land_shapes]
    outs = pl.pallas_call(
        body, name=name,
        out_shape=(*[pltpu.SemaphoreType.DMA(())] * (2 * ns), *[pltpu.HBM(s.shape, s.dtype) for s in srcs],
                   *[pltpu.HBM(s.shape, s.dtype) for s in land_shapes], jax.ShapeDtypeStruct((8, 128), F32)),
        in_specs=[_HBM_SPEC] * (2 * n) + [_ANY_SPEC],
        out_specs=(*[_SEM_SPEC] * (2 * ns), *[_HBM_SPEC] * (2 * n), _VMEM_SPEC),
        input_output_aliases={i: 2 * ns + i for i in range(2 * n)},
        compiler_params=pltpu.CompilerParams(has_side_effects=_EFFECT),
    )(*[_hbm(s) for s in srcs], *lands, dep)
    return list(outs[:ns]), list(outs[ns:2 * ns]), list(outs[2 * ns:2 * ns + n]), list(outs[2 * ns + n:2 * ns + 2 * n]), outs[-1]


def _xchip_wait(mode, send_sems, recv_sems, srcs, lands, after, name):
    n = len(srcs)
    ns = n * (N_CHIPS - 1)

    def body(*refs):
        src_refs, land_refs = refs[:n], refs[n:2 * n]
        sems = refs[2 * n:2 * n + 2 * ns]
        for cp in _xchip_copies(mode, src_refs, land_refs, sems[:ns], sems[ns:], waiting=True):
            cp.wait_send()
            cp.wait_recv()

    outs = pl.pallas_call(
        body, name=name,
        out_shape=(*[pltpu.HBM(s.shape, s.dtype) for s in srcs], *[pltpu.HBM(s.shape, s.dtype) for s in lands]),
        in_specs=[_HBM_SPEC] * (2 * n) + [_SEM_SPEC] * (2 * ns) + [_ANY_SPEC],
        out_specs=tuple([_HBM_SPEC] * (2 * n)),
        input_output_aliases={i: i for i in range(2 * n)},
        compiler_params=pltpu.CompilerParams(has_side_effects=_EFFECT),
    )(*srcs, *lands, *send_sems, *recv_sems, after)
    return list(outs[:n]), list(outs[n:])


def _sibling_fwd(lands, name):
    n = len(lands)

    def body(*refs):
        outs = refs[n:2 * n]
        send_sems, recv_sems = refs[2 * n:]
        x, y, c = _idx()
        sib = (x, y, 1 - c)
        sends = []
        for a in range(n):
            for k in range(1, N_CHIPS):
                src = 2 * _flip(x, k & 2) + _flip(y, k & 1)
                cp = pltpu.make_async_remote_copy(src_ref=outs[a].at[src, c], dst_ref=outs[a].at[src, c], send_sem=send_sems.at[a, k - 1],
                                                  recv_sem=recv_sems.at[a, k - 1], device_id=sib, device_id_type=MESH)
                cp.start()
                sends.append(cp)
        for a in range(n):
            for k in range(1, N_CHIPS):
                src = 2 * _flip(x, k & 2) + _flip(y, k & 1)
                pltpu.make_async_remote_copy(src_ref=outs[a].at[src, c], dst_ref=outs[a].at[src, 1 - c], send_sem=send_sems.at[a, k - 1],
                                             recv_sem=recv_sems.at[a, k - 1], device_id=sib, device_id_type=MESH).wait_recv()
        for cp in sends:
            cp.wait_send()

    return pl.pallas_call(
        body, name=name,
        out_shape=[jax.ShapeDtypeStruct(s.shape, s.dtype) for s in lands],
        in_specs=[_HBM_SPEC] * n, out_specs=[_HBM_SPEC] * n,
        input_output_aliases={i: i for i in range(n)},
        scratch_shapes=[pltpu.SemaphoreType.DMA((n, N_CHIPS - 1)), pltpu.SemaphoreType.DMA((n, N_CHIPS - 1))],
    )(*lands)


def _sibling_swap(parts, name):
    n = len(parts)

    def body(*refs):
        ins, outs = refs[:n], refs[n:2 * n]
        send_sems, recv_sems = refs[2 * n:]
        x, y, c = _idx()
        cps = []
        for a in range(n):
            cp = pltpu.make_async_remote_copy(src_ref=ins[a].at[1 - c], dst_ref=outs[a], send_sem=send_sems.at[a], recv_sem=recv_sems.at[a],
                                              device_id=(x, y, 1 - c), device_id_type=MESH)
            cp.start()
            cps.append(cp)
        for cp in cps:
            cp.wait()

    return pl.pallas_call(
        body, name=name,
        out_shape=[jax.ShapeDtypeStruct(p.shape[1:], p.dtype) for p in parts],
        in_specs=[_HBM_SPEC] * n, out_specs=[_HBM_SPEC] * n,
        scratch_shapes=[pltpu.SemaphoreType.DMA((n,)), pltpu.SemaphoreType.DMA((n,))],
    )(*parts)


def _sibling_send(halves, name):
    n = len(halves)

    def body(*refs):
        ins, outs = refs[:n], refs[n:2 * n]
        send_sems, recv_sems = refs[2 * n:]
        x, y, c = _idx()
        cps = []
        for a in range(n):
            cp = pltpu.make_async_remote_copy(src_ref=ins[a], dst_ref=outs[a], send_sem=send_sems.at[a], recv_sem=recv_sems.at[a],
                                              device_id=(x, y, 1 - c), device_id_type=MESH)
            cp.start()
            cps.append(cp)
        for cp in cps:
            cp.wait()

    return pl.pallas_call(
        body, name=name,
        out_shape=[jax.ShapeDtypeStruct(h.shape, h.dtype) for h in halves],
        in_specs=[_HBM_SPEC] * n, out_specs=[_HBM_SPEC] * n,
        scratch_shapes=[pltpu.SemaphoreType.DMA((n,)), pltpu.SemaphoreType.DMA((n,))],
    )(*halves)


def _col_full(g):
    k, n = g.shape[1], g.shape[2]
    return g.transpose(1, 0, 2).reshape(k, N_CHIPS * n)


def _col_blocks(w):
    k, n = w.shape
    return w.reshape(k, N_CHIPS, n // N_CHIPS).transpose(1, 0, 2)


def _row_blocks(w):
    k, n = w.shape
    return w.reshape(N_CHIPS, k // N_CHIPS, n)


_UQ_HEAD = MLA_NOPE + MLA_ROPE

_LAT = MLA_QL + MLA_KVL + MLA_ROPE
_POOL_R = len(POOL_WINDOWS) * (POOL_GD // N_CHIPS)

_PIECE_KINDS = {
    "mlp_w1": (D_MODEL, D_MODEL, _col_full, _col_blocks),
    "mlp_w2": (D_MODEL, D_MODEL, lambda g: g.reshape(4 * D_MODEL, D_MODEL), _row_blocks),
    "pool_w": (_POOL_R, POOL_GD,
               lambda g: g.reshape(N_CHIPS, len(POOL_WINDOWS), POOL_GD // N_CHIPS, POOL_GD).transpose(1, 0, 2, 3).reshape(len(POOL_WINDOWS), POOL_GD, POOL_GD),
               lambda w: w.reshape(len(POOL_WINDOWS), N_CHIPS, POOL_GD // N_CHIPS, POOL_GD).transpose(1, 0, 2, 3).reshape(N_CHIPS, _POOL_R, POOL_GD)),
    "sgu_w_in": (D_MODEL, 2 * SGU_W // N_CHIPS, _col_full, _col_blocks),
    "sgu_w_out": (SGU_W // N_CHIPS, D_MODEL, lambda g: g.reshape(SGU_W, D_MODEL), _row_blocks),
    "mla_w_dq_dkv": (D_MODEL // N_CHIPS, _LAT, lambda g: jnp.pad(g.reshape(D_MODEL, _LAT), ((0, 0), (0, MLA_LATP - _LAT))),
                     lambda w: _row_blocks(w[:, :_LAT])),
    "mla_w_uq": (MLA_QL, MLA_H * _UQ_HEAD // N_CHIPS,
                 lambda g: jnp.pad(_col_full(g).reshape(MLA_QL, MLA_H, _UQ_HEAD), ((0, 0), (0, 0), (0, MLA_HP - _UQ_HEAD))).reshape(MLA_QL, MLA_H * MLA_HP),
                 lambda w: _col_blocks(w.reshape(MLA_QL, MLA_H, MLA_HP)[:, :, :_UQ_HEAD].reshape(MLA_QL, MLA_H * _UQ_HEAD))),
    "mla_w_ukv": (MLA_KVL, MLA_H * (MLA_NOPE + MLA_V) // N_CHIPS, _col_full, _col_blocks),
    "mla_w_o": (MLA_H * MLA_V // N_CHIPS, D_MODEL, lambda g: g.reshape(MLA_H * MLA_V, D_MODEL), _row_blocks),
}
_MIXER_KINDS = (("pool_w",), ("sgu_w_in", "sgu_w_out"), ("mla_w_dq_dkv", "mla_w_uq", "mla_w_ukv", "mla_w_o"))


def _layer_pieces(i):
    return [(k, i // N_MIXERS) for k in _MIXER_KINDS[i % N_MIXERS]] + [("mlp_w1", i), ("mlp_w2", i)]


def _rope_tables(positions):
    inv_freq = ROPE_THETA ** (-jnp.arange(0, MLA_ROPE, 2, dtype=F32) / MLA_ROPE)
    ang = positions.astype(F32)[:, None] * inv_freq
    cos, sin = jnp.cos(ang), jnp.sin(ang)
    z32, z64 = jnp.zeros_like(cos), jnp.zeros((positions.shape[0], 64), F32)
    return (jnp.concatenate([cos, cos, z64], axis=1), jnp.concatenate([-sin, z32, z64], axis=1), jnp.concatenate([z32, sin, z64], axis=1))


def _local_step(x, positions, target, mod, S, weights_of, grads_of):
    D = D_MODEL
    cc, sa, sb = _rope_tables(positions)
    saved = []
    for i in range(DEPTH):
        sh1, sc1, g1, sh2, sc2, g2 = (mod[i:i + 1, n * D:(n + 1) * D] for n in range(6))
        kind, j = i % N_MIXERS, i // N_MIXERS
        gmix, gmlp = S["norm_mix_g"][i:i + 1], S["norm_mlp_g"][i:i + 1]
        W = weights_of(i, x)
        gmix = gmix + W["_tok"]
        st = {"x": x, "W": W}
        if kind == 0:
            h = _norm_mod_fwd(x, gmix, sc1, sh1, F32, f"l{i}_norm1")
            x2, pooled, ypre = _pool_fwd(h, W["pool_w"], S["pool_scale"][j:j + 1], x, g1, f"l{i}_pool")
            st.update(pooled=pooled, y=ypre)
        elif kind == 1:
            h = _norm_mod_fwd(x, gmix, sc1, sh1, BF16, f"l{i}_norm1")
            zz = _mm(h, W["sgu_w_in"], out_dtypes=(F32,), name=f"l{i}_sgu_in")
            bs_t = S["sgu_b_s"].T
            gated = _sgu_gate_fwd(zz, S["sgu_ln_g"], S["sgu_ln_b"], S["sgu_w_s"], bs_t, f"l{i}_sgu_gate")
            x2, y = _mm(gated, W["sgu_w_out"], epi=_epi_residual, extras=((x, "mn"), (g1, "n")), out_dtypes=(F32, BF16), name=f"l{i}_sgu_out")
            st.update(h=h, zz=zz, gated=gated, y=y, bs_t=bs_t)
        else:
            h = _norm_mod_fwd(x, gmix, sc1, sh1, BF16, f"l{i}_norm1")
            lat = _mm(h, W["mla_w_dq_dkv"], out_dtypes=(F32,), name=f"l{i}_mla_lat")
            cqn, ckvn, krot = _mla_lat_fwd(lat, S["mla_q_norm_g"], S["mla_kv_norm_g"], cc, sa, sb, f"l{i}_mla_latn")
            qpad = _mm(cqn, W["mla_w_uq"], out_dtypes=(F32,), name=f"l{i}_mla_uq")
            kv = _mm(ckvn, W["mla_w_ukv"], out_dtypes=(F32,), name=f"l{i}_mla_ukv")
            q, k, v = _mla_prep(qpad, kv, krot, cc, sa, sb, f"l{i}_mla_prep")
            tkb = min(q.shape[0], ATT_TILE)
            vt = v.reshape(q.shape[0] // tkb, tkb, MLA_H, MLA_V).transpose(2, 0, 3, 1)
            o, lse = _attn_fwd(q, k, vt, f"l{i}_attn")
            x2, y = _mm(o, W["mla_w_o"], epi=_epi_residual, extras=((x, "mn"), (g1, "n")), out_dtypes=(F32, BF16), name=f"l{i}_mla_o")
            st.update(h=h, lat=lat, cqn=cqn, ckvn=ckvn, q=q, k=k, kt=k.T, v=v, o=o, lse=lse, y=y)
        h2 = _norm_mod_fwd(x2, gmlp, sc2, sh2, BF16, f"l{i}_norm2")
        z = _mm(h2, W["mlp_w1"], epi=lambda acc: (jnp.square(jnp.maximum(acc, 0.0)),), name=f"l{i}_mlp1")
        x3, o2 = _mm(z, W["mlp_w2"], epi=_epi_residual, extras=((x2, "mn"), (g2, "n")), out_dtypes=(F32, BF16), name=f"l{i}_mlp2")
        st.update(x2=x2, h2=h2, z=z, o2=o2)
        saved.append(st)
        x = x3

    loss, dx, dfinal_g = _loss_head(x, target, S["final_g"], "loss_head")

    gS = {"final_g": dfinal_g, "norm_mix_g": [None] * DEPTH, "norm_mlp_g": [None] * DEPTH, "pool_scale": [None] * 2}
    dmod = [None] * DEPTH
    for i in reversed(range(DEPTH)):
        st = saved[i]
        W, gW = st["W"], {}
        sh1, sc1, g1, sh2, sc2, g2 = (mod[i:i + 1, n * D:(n + 1) * D] for n in range(6))
        kind, j = i % N_MIXERS, i // N_MIXERS
        gmix, gmlp = S["norm_mix_g"][i:i + 1], S["norm_mlp_g"][i:i + 1]
        do2, dg2 = _resid_bwd(dx, st["o2"], g2, f"l{i}_b_res2")
        da = _mm(do2, W["mlp_w2"], tb=True, epi=lambda acc, zt: (acc * (2.0 * jnp.sqrt(zt.astype(F32))),), extras=((st["z"], "mn"),), name=f"l{i}_b_dz")
        gW["mlp_w2"] = _mm(st["z"], do2, ta=True, name=f"l{i}_b_dw2")
        dh2 = _mm(da, W["mlp_w1"], tb=True, out_dtypes=(F32,), name=f"l{i}_b_dh2")
        gW["mlp_w1"] = _mm(st["h2"], da, ta=True, name=f"l{i}_b_dw1")
        dx2, dgmlp, dsc2, dsh2 = _norm_mod_bwd(st["x2"], dh2, dx, gmlp, sc2, f"l{i}_b_norm2")
        gS["norm_mlp_g"][i] = dgmlp
        dy, q1 = _resid_bwd(dx2, st["y"], g1, f"l{i}_b_res1")
        if kind == 0:
            dh, dpw, dpsc, dg1 = _pool_bwd(dy, st["pooled"], W["pool_w"], S["pool_scale"][j:j + 1], g1, q1, f"l{i}_b_pool")
            gW["pool_w"] = dpw.astype(BF16)
            gS["pool_scale"][j] = dpsc
        elif kind == 1:
            dg1 = q1
            dgated = _mm(dy, W["sgu_w_out"], tb=True, name=f"l{i}_b_dgated")
            gW["sgu_w_out"] = _mm(st["gated"], dy, ta=True, name=f"l{i}_b_dwout")
            dzz, dws, dbs, dlg, dlb = _sgu_gate_bwd(st["zz"], dgated, S["sgu_ln_g"], S["sgu_ln_b"], S["sgu_w_s"], st["bs_t"], f"l{i}_b_sgu_gate")
            gS.update(sgu_w_s=dws, sgu_b_s=dbs[:, :, 0], sgu_ln_g=dlg, sgu_ln_b=dlb)
            dh = _mm(dzz, W["sgu_w_in"], tb=True, out_dtypes=(F32,), name=f"l{i}_b_dh_sgu")
            gW["sgu_w_in"] = _mm(st["h"], dzz, ta=True, name=f"l{i}_b_dwin")
        else:
            dg1 = q1
            do = _mm(dy, W["mla_w_o"], tb=True, name=f"l{i}_b_do")
            gW["mla_w_o"] = _mm(st["o"], dy, ta=True, name=f"l{i}_b_dwo")
            delta = _attn_delta(do, st["o"], f"l{i}_b_delta")
            dqt, dk, dv = _attn_bwd(st["q"], st["k"], st["kt"], st["v"], do, st["lse"], delta, f"l{i}_b_attn")
            dqpad, dkv, dkrot = _mla_prep_bwd(dqt, dk, dv, cc, sa, sb, f"l{i}_b_mla_prep")
            dcqn = _mm(dqpad, W["mla_w_uq"], tb=True, out_dtypes=(F32,), name=f"l{i}_b_dcq")
            gW["mla_w_uq"] = _mm(st["cqn"], dqpad, ta=True, name=f"l{i}_b_dwuq")
            dckvn = _mm(dkv, W["mla_w_ukv"], tb=True, out_dtypes=(F32,), name=f"l{i}_b_dckv")
            gW["mla_w_ukv"] = _mm(st["ckvn"], dkv, ta=True, name=f"l{i}_b_dwukv")
            dlat, dqg, dkvg = _mla_lat_bwd(st["lat"], dcqn, dckvn, dkrot, S["mla_q_norm_g"], S["mla_kv_norm_g"], cc, sa, sb, f"l{i}_b_mla_latn")
            gS.update(mla_q_norm_g=dqg, mla_kv_norm_g=dkvg)
            dh = _mm(dlat, W["mla_w_dq_dkv"], tb=True, out_dtypes=(F32,), name=f"l{i}_b_dh_mla")
            gW["mla_w_dq_dkv"] = _mm(st["h"], dlat, ta=True, name=f"l{i}_b_dwdq")
        dx, dgmix, dsc1, dsh1 = _norm_mod_bwd(st["x"], dh, dx2, gmix, sc1, f"l{i}_b_norm1")
        gS["norm_mix_g"][i] = dgmix
        dmod[i] = jnp.concatenate([dsh1, dsc1, dg1, dsh2, dsc2, dg2], axis=1)
        grads_of(i, gW, dx)

    for n in ("norm_mix_g", "norm_mlp_g", "pool_scale"):
        gS[n] = jnp.concatenate(gS[n], axis=0)
    return loss, dx, gS, jnp.concatenate(dmod, axis=0)


_SMALL = {
    "norm_mix_g": (DEPTH, D_MODEL), "norm_mlp_g": (DEPTH, D_MODEL), "sgu_ln_g": (1, SGU_W), "sgu_ln_b": (1, SGU_W),
    "sgu_w_s": (SGU_H, SGU_CHUNK, SGU_CHUNK), "sgu_b_s": (SGU_H, SGU_CHUNK), "mla_kv_norm_g": (1, MLA_KVL), "final_g": (1, D_MODEL),
    "pool_scale": (2, D_MODEL), "mla_q_norm_g": (1, MLA_QL), "dmod": (DEPTH, 6 * D_MODEL),
}
_PACK_W = 1024


def _pack(vals):
    flat = jnp.concatenate([v.reshape(-1) for v in vals])
    rows = -(-flat.shape[0] // (8 * _PACK_W)) * 8
    return jnp.pad(flat, (0, rows * _PACK_W - flat.shape[0])).reshape(rows, _PACK_W)


def _unpack(buf, shapes):
    flat, out, off = buf.reshape(-1), [], 0
    for s in shapes:
        n = math.prod(s)
        out.append(flat[off:off + n].reshape(s))
        off += n
    return out


def kernel(x, c, positions, ada_w, ada_b, norm_mix_g, norm_mlp_g, pool_w, pool_scale, sgu_w_in, sgu_ln_g, sgu_ln_b, sgu_w_s, sgu_b_s, sgu_w_out, mla_w_dq_dkv, mla_q_norm_g, mla_kv_norm_g, mla_w_uq, mla_w_ukv, mla_w_o, mlp_w1, mlp_w2, final_g, loss_target, m_ada_w, m_ada_b, m_norm_mix_g, m_norm_mlp_g, m_pool_w, m_pool_scale, m_sgu_w_in, m_sgu_ln_g, m_sgu_ln_b, m_sgu_w_s, m_sgu_b_s, m_sgu_w_out, m_mla_w_dq_dkv, m_mla_q_norm_g, m_mla_kv_norm_g, m_mla_w_uq, m_mla_w_ukv, m_mla_w_o, m_mlp_w1, m_mlp_w2, m_final_g, v_ada_w, v_ada_b, v_norm_mix_g, v_norm_mlp_g, v_pool_w, v_pool_scale, v_sgu_w_in, v_sgu_ln_g, v_sgu_ln_b, v_sgu_w_s, v_sgu_b_s, v_sgu_w_out, v_mla_w_dq_dkv, v_mla_q_norm_g, v_mla_kv_norm_g, v_mla_w_uq, v_mla_w_ukv, v_mla_w_o, v_mlp_w1, v_mlp_w2, v_final_g):
    P = dict(ada_w=ada_w, ada_b=ada_b, norm_mix_g=norm_mix_g, norm_mlp_g=norm_mlp_g, pool_w=pool_w, pool_scale=pool_scale, sgu_w_in=sgu_w_in,
             sgu_ln_g=sgu_ln_g, sgu_ln_b=sgu_ln_b, sgu_w_s=sgu_w_s, sgu_b_s=sgu_b_s, sgu_w_out=sgu_w_out, mla_w_dq_dkv=mla_w_dq_dkv,
             mla_q_norm_g=mla_q_norm_g, mla_kv_norm_g=mla_kv_norm_g, mla_w_uq=mla_w_uq, mla_w_ukv=mla_w_ukv, mla_w_o=mla_w_o, mlp_w1=mlp_w1,
             mlp_w2=mlp_w2, final_g=final_g)
    M = dict(ada_w=m_ada_w, ada_b=m_ada_b, norm_mix_g=m_norm_mix_g, norm_mlp_g=m_norm_mlp_g, pool_w=m_pool_w, pool_scale=m_pool_scale,
             sgu_w_in=m_sgu_w_in, sgu_ln_g=m_sgu_ln_g, sgu_ln_b=m_sgu_ln_b, sgu_w_s=m_sgu_w_s, sgu_b_s=m_sgu_b_s, sgu_w_out=m_sgu_w_out,
             mla_w_dq_dkv=m_mla_w_dq_dkv, mla_q_norm_g=m_mla_q_norm_g, mla_kv_norm_g=m_mla_kv_norm_g, mla_w_uq=m_mla_w_uq, mla_w_ukv=m_mla_w_ukv,
             mla_w_o=m_mla_w_o, mlp_w1=m_mlp_w1, mlp_w2=m_mlp_w2, final_g=m_final_g)
    V = dict(ada_w=v_ada_w, ada_b=v_ada_b, norm_mix_g=v_norm_mix_g, norm_mlp_g=v_norm_mlp_g, pool_w=v_pool_w, pool_scale=v_pool_scale,
             sgu_w_in=v_sgu_w_in, sgu_ln_g=v_sgu_ln_g, sgu_ln_b=v_sgu_ln_b, sgu_w_s=v_sgu_w_s, sgu_b_s=v_sgu_b_s, sgu_w_out=v_sgu_w_out,
             mla_w_dq_dkv=v_mla_w_dq_dkv, mla_q_norm_g=v_mla_q_norm_g, mla_kv_norm_g=v_mla_kv_norm_g, mla_w_uq=v_mla_w_uq, mla_w_ukv=v_mla_w_ukv,
             mla_w_o=v_mla_w_o, mlp_w1=v_mlp_w1, mlp_w2=v_mlp_w2, final_g=v_final_g)
    order = list(P)
    xi, yi, ci = _idx()
    chip = 2 * xi + yi
    D = D_MODEL
    n_ada = ada_w.shape[2]

    pre = _allgather8(_pack([c, pool_scale, mla_q_norm_g]), "ag_small")
    flat = pre.reshape(N_DEV, -1)
    c_all = flat[:, :D]
    ps_all = flat[0::2, D:D + 2 * (D // N_CHIPS)].reshape(N_CHIPS, 2, D // N_CHIPS).transpose(1, 0, 2).reshape(2, D)
    q0 = D + 2 * (D // N_CHIPS)
    qg_all = flat[0::2, q0:q0 + MLA_QL // N_CHIPS].reshape(1, MLA_QL)

    ada_b_loc = lax.dynamic_slice_in_dim(ada_b, chip * n_ada, n_ada, axis=1)[:, None, :]
    modp = _ada_fwd(c_all, ada_w, ada_b_loc, "ada_fwd")
    mod = _mod_exchange(modp.transpose(1, 0, 2), "mod_exchange").transpose(1, 0, 2).reshape(DEPTH, 6 * D)

    S = dict(norm_mix_g=norm_mix_g, norm_mlp_g=norm_mlp_g, pool_scale=ps_all, sgu_ln_g=sgu_ln_g, sgu_ln_b=sgu_ln_b, sgu_w_s=sgu_w_s[0],
             sgu_b_s=sgu_b_s[0], mla_q_norm_g=qg_all, mla_kv_norm_g=mla_kv_norm_g, final_g=final_g[None, :])
    cidx, chipidx = jnp.reshape(ci, (1,)).astype(jnp.int32), jnp.reshape(chip, (1,)).astype(jnp.int32)
    view2d = lambda a: a.reshape(-1, a.shape[-1])

    def piece_rows(kind, blk):
        r = _PIECE_KINDS[kind][0]
        return blk * r, r

    gathers = {}

    def gather_start(i, dep):
        pcs = _layer_pieces(i)
        srcs, shapes = [], []
        for kind, blk in pcs:
            r0, r = piece_rows(kind, blk)
            cdim = _PIECE_KINDS[kind][1]
            srcs.append(view2d(P[kind])[r0:r0 + r].astype(BF16).reshape(2, r // 2, cdim))
            shapes.append(jax.ShapeDtypeStruct((N_CHIPS, 2, r // 2, cdim), BF16))
        gathers[i] = (pcs, *_xchip_start("gather", srcs, shapes, dep, f"ag_start_l{i}"))

    def weights_of(i, x_i):
        pcs, ssem, rsem, srcs, lands, _ = gathers.pop(i)
        srcs, lands = _xchip_wait("gather", ssem, rsem, srcs, lands, x_i, f"ag_wait_l{i}")
        lands = _sibling_fwd(lands, f"ag_sibling_l{i}")
        tok = jnp.zeros((), F32)
        if i + 1 < DEPTH:
            gather_start(i + 1, lands[0])
            tok = gathers[i + 1][-1][0, 0]
        W = {"_tok": tok}
        for (kind, _), s, land in zip(pcs, srcs, lands, strict=True):
            r, cdim, to_full, _ = _PIECE_KINDS[kind]
            W[kind] = to_full(lax.dynamic_update_index_in_dim(land, s, chip, 0).reshape(N_CHIPS, r, cdim))
        return W

    scatters = {}
    bufs = {n: tuple(lax.empty(view2d(P[n]).shape, F32) for _ in range(4)) for n in _PIECE_KINDS}

    def scatter_start(i, gW, dep):
        pcs = _layer_pieces(i)
        blocked = []
        for kind, _ in pcs:
            r, cdim, _, to_blocks = _PIECE_KINDS[kind]
            blocked.append(to_blocks(gW[kind]).reshape(N_CHIPS, 2, r // 2, cdim).transpose(1, 0, 2, 3))
        from_sib = _sibling_swap(blocked, f"rs_sibling_l{i}")
        pair, shapes = [], []
        for (kind, _), b, f in zip(pcs, blocked, from_sib, strict=True):
            _, _, hr, cdim = b.shape
            p = _sum_sel(cidx, b.reshape(2, N_CHIPS * hr, cdim), [f.reshape(1, N_CHIPS * hr, cdim)], f"rs_pair_l{i}_{kind}", BF16)
            pair.append(p.reshape(N_CHIPS, hr, cdim))
            shapes.append(jax.ShapeDtypeStruct((N_CHIPS - 1, hr, cdim), BF16))
        scatters[i] = (pcs, *_xchip_start("scatter", pair, shapes, dep, f"rs_start_l{i}"))

    def scatter_finish(i, after):
        pcs, ssem, rsem, pair, lands, _ = scatters.pop(i)
        pair, lands = _xchip_wait("scatter", ssem, rsem, pair, lands, after, f"rs_wait_l{i}")
        halves = [_sum_sel(chipidx, p, [l], f"rs_sum_l{i}_{kind}", F32) for (kind, _), p, l in zip(pcs, pair, lands, strict=True)]
        got = _sibling_send(halves, f"rs_merge_l{i}")
        for (kind, blk), mine, other in zip(pcs, halves, got, strict=True):
            r0, _ = piece_rows(kind, blk)
            bufs[kind] = tuple(_adamw_piece(cidx, view2d(P[kind]), view2d(M[kind]), view2d(V[kind]), mine, other, bufs[kind], r0,
                                            f"adamw_l{i}_{kind}"))

    def grads_of(i, gW, dx_i):
        if i + 1 in scatters:
            scatter_finish(i + 1, dx_i)
        scatter_start(i, gW, dx_i)

    gather_start(0, c)
    loss_l, dx, gS, dmod = _local_step(x[0], positions[0], loss_target[0], mod, S, weights_of, grads_of)
    loss = lax.psum(loss_l[0, 0], ("x", "y", "c"))

    gS["dmod"] = dmod
    small = _allgather8(_pack([gS[n] for n in _SMALL]), "ag_small_grads")
    small_sum = _unpack(_sum_lead([small], "sum_small_grads"), list(_SMALL.values()))
    G = dict(zip(_SMALL, small_sum, strict=True))
    grads = {
        "ada_b": G["dmod"], "norm_mix_g": G["norm_mix_g"], "norm_mlp_g": G["norm_mlp_g"], "sgu_ln_g": G["sgu_ln_g"], "sgu_ln_b": G["sgu_ln_b"],
        "sgu_w_s": G["sgu_w_s"][None], "sgu_b_s": G["sgu_b_s"][None], "mla_kv_norm_g": G["mla_kv_norm_g"], "final_g": G["final_g"][0],
        "pool_scale": lax.dynamic_slice_in_dim(G["pool_scale"], chip * (D // N_CHIPS), D // N_CHIPS, axis=1),
        "mla_q_norm_g": lax.dynamic_slice_in_dim(G["mla_q_norm_g"], chip * (MLA_QL // N_CHIPS), MLA_QL // N_CHIPS, axis=1),
    }
    dmod_all = _unpack(small, [(N_DEV,) + (small.shape[1] * _PACK_W,)])[0]
    off = sum(math.prod(s) for n, s in _SMALL.items() if n != "dmod")
    dmod_all = dmod_all[:, off:off + DEPTH * 6 * D].reshape(N_DEV, DEPTH, 6 * D)
    dmod_loc = lax.dynamic_slice_in_dim(dmod_all, chip * n_ada, n_ada, axis=2).transpose(1, 0, 2)
    grads["ada_w"] = _ada_bwd(c_all.T, dmod_loc, "ada_bwd")

    deltas, new_m, new_v = {}, {}, {}
    for n in order:
        if n not in _PIECE_KINDS:
            deltas[n], new_m[n], new_v[n] = _adamw(P[n], grads[n].reshape(P[n].shape), M[n], V[n], f"adamw_{n}")
    scatter_finish(0, deltas["ada_w"])
    for n in _PIECE_KINDS:
        grads[n], deltas[n], new_m[n], new_v[n] = (b.reshape(P[n].shape) for b in bufs[n])
    return (loss, dx[None], *[grads[n].reshape(P[n].shape) for n in order], *[deltas[n] for n in order], *[new_m[n] for n in order],
            *[new_v[n] for n in order])
```

```python
import math

import jax
import jax.numpy as jnp
from jax import lax
from jax.experimental import pallas as pl
from jax.experimental.pallas import tpu as pltpu

F32, BF16 = jnp.float32, jnp.bfloat16
MESH = pl.DeviceIdType.MESH

D_MODEL = 1024
DEPTH = 4
N_MIXERS = 3
POOL_WINDOWS = (2, 4, 8, 16)
POOL_GD = D_MODEL // len(POOL_WINDOWS)
POOL_HALO = 16
SGU_CHUNK = 128
SGU_W = D_MODEL
SGU_HD = 128
SGU_H = SGU_W // SGU_HD
MLA_H = 16
MLA_QL = 256
MLA_KVL = 128
MLA_NOPE = 128
MLA_ROPE = 64
MLA_V = 128
MLA_HP = 256
MLA_LATP = 512
ROPE_THETA = 10000.0
RMS_EPS = 1e-6
LN_EPS = 1e-5
SM_SCALE = (MLA_NOPE + MLA_ROPE) ** -0.5
NEG_INF = -1e30
ADAM_LR, ADAM_B1, ADAM_B2, ADAM_EPS, ADAM_WD, ADAM_STEP = 0.001, 0.9, 0.999, 1e-08, 0.01, 10
N_CHIPS = 4
N_DEV = 8
ROW_TILE = 512
ATT_TILE = 512
ATT_SUB = 256


def _idx():
    return lax.axis_index("x"), lax.axis_index("y"), lax.axis_index("c")


def _mm(a, b, *, name, ta=False, tb=False, epi=None, extras=(), out_dtypes=(BF16,), tm=1024, tn=1024, tk=1024):
    if ta:
        K, M = a.shape
    else:
        M, K = a.shape
    if tb:
        N, Kb = b.shape
    else:
        Kb, N = b.shape
    assert K == Kb, (a.shape, b.shape, ta, tb)
    tm, tn, tk = min(tm, M), min(tn, N), min(tk, K)
    assert M % tm == 0 and N % tn == 0 and K % tk == 0, (M, N, K, tm, tn, tk)
    nk = K // tk
    a_spec = pl.BlockSpec((tk, tm), lambda i, j, k: (k, i)) if ta else pl.BlockSpec((tm, tk), lambda i, j, k: (i, k))
    b_spec = pl.BlockSpec((tn, tk), lambda i, j, k: (j, k)) if tb else pl.BlockSpec((tk, tn), lambda i, j, k: (k, j))
    ex_specs = []
    for arr, kind in extras:
        if kind == "mn":
            ex_specs.append(pl.BlockSpec((tm, tn), lambda i, j, k: (i, j)))
        elif kind == "n":
            ex_specs.append(pl.BlockSpec((1, tn), lambda i, j, k: (0, j)))
        else:
            ex_specs.append(pl.BlockSpec((tm, arr.shape[1]), lambda i, j, k: (i, 0)))
    n_ex, n_out = len(extras), len(out_dtypes)
    dims = (((0 if ta else 1,), (1 if tb else 0,)), ((), ()))

    def body(*refs):
        a_ref, b_ref = refs[0], refs[1]
        ex_refs = refs[2:2 + n_ex]
        out_refs = refs[2 + n_ex:2 + n_ex + n_out]
        part = lax.dot_general(a_ref[...].astype(BF16), b_ref[...].astype(BF16), dims, preferred_element_type=F32)

        def finish(acc):
            outs = epi(acc, *[r[...] for r in ex_refs]) if epi is not None else (acc,)
            for r, o in zip(out_refs, outs, strict=True):
                r[...] = o.astype(r.dtype)

        if nk == 1:
            finish(part)
        else:
            acc_ref = refs[-1]
            k = pl.program_id(2)

            @pl.when(k == 0)
            def _():
                acc_ref[...] = part

            @pl.when(k > 0)
            def _():
                acc_ref[...] += part

            @pl.when(k == nk - 1)
            def _():
                finish(acc_ref[...])

    outs = pl.pallas_call(
        body,
        name=name,
        grid=(M // tm, N // tn, nk),
        in_specs=[a_spec, b_spec, *ex_specs],
        out_specs=[pl.BlockSpec((tm, tn), lambda i, j, k: (i, j)) for _ in range(n_out)],
        out_shape=[jax.ShapeDtypeStruct((M, N), dt) for dt in out_dtypes],
        scratch_shapes=[pltpu.VMEM((tm, tn), F32)] if nk > 1 else [],
        compiler_params=pltpu.CompilerParams(dimension_semantics=("parallel", "parallel", "arbitrary")),
    )(a, b, *[arr for arr, _ in extras])
    return outs[0] if n_out == 1 else tuple(outs)


def _epi_residual(acc, x, g):
    return x + g * acc, acc


def _row_spec(tr, d):
    return pl.BlockSpec((tr, d), lambda i: (i, 0))


def _vec_spec(d):
    return pl.BlockSpec((1, d), lambda i: (0, 0))


def _colsum(v):
    return jnp.sum(v, axis=0, keepdims=True)


def _norm_mod_fwd(x, gain, sc, sh, out_dtype, name):
    T, D = x.shape
    tr = min(T, ROW_TILE)

    def body(x_ref, g_ref, sc_ref, sh_ref, o_ref):
        xv = x_ref[...]
        r = lax.rsqrt(jnp.mean(xv * xv, axis=-1, keepdims=True) + RMS_EPS)
        o_ref[...] = (((xv * r) * g_ref[...]) * (1.0 + sc_ref[...]) + sh_ref[...]).astype(o_ref.dtype)

    return pl.pallas_call(
        body, name=name, grid=(T // tr,),
        in_specs=[_row_spec(tr, D), _vec_spec(D), _vec_spec(D), _vec_spec(D)],
        out_specs=_row_spec(tr, D),
        out_shape=jax.ShapeDtypeStruct((T, D), out_dtype),
        compiler_params=pltpu.CompilerParams(dimension_semantics=("parallel",)),
    )(x, gain, sc, sh)


def _norm_mod_bwd(x, dh, dres, gain, sc, name):
    T, D = x.shape
    tr = min(T, ROW_TILE)

    def body(x_ref, dh_ref, dres_ref, g_ref, sc_ref, dx_ref, dg_ref, dsc_ref, dsh_ref):
        @pl.when(pl.program_id(0) == 0)
        def _():
            dg_ref[...] = jnp.zeros_like(dg_ref)
            dsc_ref[...] = jnp.zeros_like(dsc_ref)
            dsh_ref[...] = jnp.zeros_like(dsh_ref)

        xv = x_ref[...]
        r = lax.rsqrt(jnp.mean(xv * xv, axis=-1, keepdims=True) + RMS_EPS)
        xn = xv * r
        dhv = dh_ref[...].astype(F32)
        dsh_ref[...] += _colsum(dhv)
        dsc_ref[...] += _colsum(dhv * (xn * g_ref[...]))
        dt = dhv * (1.0 + sc_ref[...])
        dg_ref[...] += _colsum(dt * xn)
        dxn = dt * g_ref[...]
        dx_ref[...] = dres_ref[...] + r * (dxn - xn * jnp.mean(dxn * xn, axis=-1, keepdims=True))

    return pl.pallas_call(
        body, name=name, grid=(T // tr,),
        in_specs=[_row_spec(tr, D), _row_spec(tr, D), _row_spec(tr, D), _vec_spec(D), _vec_spec(D)],
        out_specs=[_row_spec(tr, D), _vec_spec(D), _vec_spec(D), _vec_spec(D)],
        out_shape=[jax.ShapeDtypeStruct((T, D), F32)] + [jax.ShapeDtypeStruct((1, D), F32)] * 3,
        compiler_params=pltpu.CompilerParams(dimension_semantics=("arbitrary",)),
    )(x, dh, dres, gain, sc)


def _resid_bwd(dx, y, g, name):
    T, D = dx.shape
    tr = min(T, ROW_TILE)

    def body(dx_ref, y_ref, g_ref, dy_ref, q_ref):
        @pl.when(pl.program_id(0) == 0)
        def _():
            q_ref[...] = jnp.zeros_like(q_ref)

        dxv = dx_ref[...]
        dy_ref[...] = (g_ref[...] * dxv).astype(BF16)
        q_ref[...] += _colsum(dxv * y_ref[...].astype(F32))

    return pl.pallas_call(
        body, name=name, grid=(T // tr,),
        in_specs=[_row_spec(tr, D), _row_spec(tr, D), _vec_spec(D)],
        out_specs=[_row_spec(tr, D), _vec_spec(D)],
        out_shape=[jax.ShapeDtypeStruct((T, D), BF16), jax.ShapeDtypeStruct((1, D), F32)],
        compiler_params=pltpu.CompilerParams(dimension_semantics=("arbitrary",)),
    )(dx, y, g)


def _loss_head(x, target, gain, name):
    T, D = x.shape
    tr = min(T, ROW_TILE)

    def body(x_ref, t_ref, g_ref, loss_ref, dx_ref, dg_ref):
        @pl.when(pl.program_id(0) == 0)
        def _():
            loss_ref[...] = jnp.zeros_like(loss_ref)
            dg_ref[...] = jnp.zeros_like(dg_ref)

        xv = x_ref[...]
        r = lax.rsqrt(jnp.mean(xv * xv, axis=-1, keepdims=True) + RMS_EPS)
        xn = xv * r
        err = xn * g_ref[...] - t_ref[...]
        row = jnp.mean(err * err, axis=-1, keepdims=True)
        loss_ref[...] += 0.5 * jnp.sum(row, axis=0, keepdims=True)
        dy = err * (1.0 / D)
        dg_ref[...] += _colsum(dy * xn)
        dxn = dy * g_ref[...]
        dx_ref[...] = r * (dxn - xn * jnp.mean(dxn * xn, axis=-1, keepdims=True))

    return pl.pallas_call(
        body, name=name, grid=(T // tr,),
        in_specs=[_row_spec(tr, D), _row_spec(tr, D), _vec_spec(D)],
        out_specs=[_vec_spec(128), _row_spec(tr, D), _vec_spec(D)],
        out_shape=[jax.ShapeDtypeStruct((1, 128), F32), jax.ShapeDtypeStruct((T, D), F32), jax.ShapeDtypeStruct((1, D), F32)],
        compiler_params=pltpu.CompilerParams(dimension_semantics=("arbitrary",)),
    )(x, target, gain)


def _pool_fwd(h, w, scale, x, g1, name):
    T, D = h.shape
    tr = min(T, ROW_TILE)

    def body(h_ref, w_ref, sc_ref, x_ref, g_ref, x2_ref, pooled_ref, ypre_ref, halo_ref):
        i = pl.program_id(0)

        @pl.when(i == 0)
        def _():
            halo_ref[...] = jnp.zeros_like(halo_ref)

        hv = h_ref[...]
        buf = jnp.concatenate([halo_ref[...], hv], axis=0)
        halo_ref[...] = hv[tr - POOL_HALO:, :]
        t = (i * tr + lax.broadcasted_iota(jnp.int32, (tr, 1), 0)).astype(F32)
        for gi, win in enumerate(POOL_WINDOWS):
            cols = slice(gi * POOL_GD, (gi + 1) * POOL_GD)
            val = buf[:, cols]
            sh = 1
            while sh < win:
                val = val + pltpu.roll(val, sh, axis=0)
                sh *= 2
            pooled = val[POOL_HALO:, :] / jnp.minimum(t + 1.0, float(win)) - hv[:, cols]
            pb = pooled.astype(BF16)
            pooled_ref[:, cols] = pb
            yp = jnp.dot(pb, w_ref[gi], preferred_element_type=F32)
            ypre_ref[:, cols] = yp.astype(BF16)
            x2_ref[:, cols] = x_ref[:, cols] + g_ref[:, cols] * (yp * sc_ref[:, cols])

    return pl.pallas_call(
        body, name=name, grid=(T // tr,),
        in_specs=[_row_spec(tr, D), pl.BlockSpec(w.shape, lambda i: (0, 0, 0)), _vec_spec(D), _row_spec(tr, D), _vec_spec(D)],
        out_specs=[_row_spec(tr, D)] * 3,
        out_shape=[jax.ShapeDtypeStruct((T, D), F32), jax.ShapeDtypeStruct((T, D), BF16), jax.ShapeDtypeStruct((T, D), BF16)],
        scratch_shapes=[pltpu.VMEM((POOL_HALO, D), F32)],
        compiler_params=pltpu.CompilerParams(dimension_semantics=("arbitrary",)),
    )(h, w, scale, x, g1)


def _pool_bwd(dy, pooled, w, scale, g1, q, name):
    T, D = dy.shape
    tr = min(T, ROW_TILE)
    nt = T // tr
    ltot = tr + POOL_HALO

    def body(dy_ref, pooled_ref, w_ref, sc_ref, g_ref, q_ref, dh_ref, dw_ref, dsc_ref, dg_ref, halo_ref):
        i = pl.program_id(0)

        @pl.when(i == 0)
        def _():
            halo_ref[...] = jnp.zeros_like(halo_ref)
            dw_ref[...] = jnp.zeros_like(dw_ref)
            dsc_ref[...] = g_ref[...] * q_ref[...]
            dg_ref[...] = sc_ref[...] * q_ref[...]

        t = ((nt - 1 - i) * tr + lax.broadcasted_iota(jnp.int32, (tr, 1), 0)).astype(F32)
        for gi, win in enumerate(POOL_WINDOWS):
            cols = slice(gi * POOL_GD, (gi + 1) * POOL_GD)
            dyb = (dy_ref[:, cols].astype(F32) * sc_ref[:, cols]).astype(BF16)
            dw_ref[gi] += lax.dot_general(pooled_ref[:, cols], dyb, (((0,), (0,)), ((), ())), preferred_element_type=F32)
            dpool = lax.dot_general(dyb, w_ref[gi], (((1,), (1,)), ((), ())), preferred_element_type=F32)
            qv = dpool / jnp.minimum(t + 1.0, float(win))
            val = jnp.concatenate([qv, halo_ref[:, cols]], axis=0)
            halo_ref[:, cols] = qv[:POOL_HALO, :]
            sh = 1
            while sh < win:
                val = val + pltpu.roll(val, ltot - sh, axis=0)
                sh *= 2
            dh_ref[:, cols] = val[:tr, :] - dpool

    rev = pl.BlockSpec((tr, D), lambda i: (nt - 1 - i, 0))
    return pl.pallas_call(
        body, name=name, grid=(nt,),
        in_specs=[rev, rev, pl.BlockSpec(w.shape, lambda i: (0, 0, 0)), _vec_spec(D), _vec_spec(D), _vec_spec(D)],
        out_specs=[rev, pl.BlockSpec(w.shape, lambda i: (0, 0, 0)), _vec_spec(D), _vec_spec(D)],
        out_shape=[jax.ShapeDtypeStruct((T, D), F32), jax.ShapeDtypeStruct(w.shape, F32),
                   jax.ShapeDtypeStruct((1, D), F32), jax.ShapeDtypeStruct((1, D), F32)],
        scratch_shapes=[pltpu.VMEM((POOL_HALO, D), F32)],
        compiler_params=pltpu.CompilerParams(dimension_semantics=("arbitrary",)),
    )(dy, pooled, w, scale, g1, q)


_INV_SQRT2 = 0.7071067811865476
_INV_SQRT2PI = 0.3989422804014327


def _gelu(v):
    return 0.5 * v * (1.0 + lax.erf(v * _INV_SQRT2))


def _gelu_grad(v):
    return 0.5 * (1.0 + lax.erf(v * _INV_SQRT2)) + v * jnp.exp(-0.5 * v * v) * _INV_SQRT2PI


def _sgu_ln(v, g, b):
    mu = jnp.mean(v, axis=-1, keepdims=True)
    xc = v - mu
    rstd = lax.rsqrt(jnp.mean(xc * xc, axis=-1, keepdims=True) + LN_EPS)
    xh = xc * rstd
    return xh, rstd, xh * g + b


def _tril_mask():
    return lax.broadcasted_iota(jnp.int32, (SGU_CHUNK, SGU_CHUNK), 0) >= lax.broadcasted_iota(jnp.int32, (SGU_CHUNK, SGU_CHUNK), 1)


SGU_TILE = 256


def _sgu_gate_fwd(zz, ln_g, ln_b, ws, bs_t, name):
    T = zz.shape[0]
    ts = min(T, SGU_TILE)

    def body(zz_ref, g_ref, b_ref, ws_ref, bs_ref, out_ref):
        z = _gelu(zz_ref[...])
        u = z[:, :SGU_W]
        _, _, vn = _sgu_ln(z[:, SGU_W:], g_ref[...], b_ref[...])
        vb = vn.astype(BF16)
        tril = _tril_mask()
        for hh in range(SGU_H):
            wm = jnp.where(tril, ws_ref[hh], 0.0).astype(BF16)
            bcol = bs_ref[:, hh:hh + 1]
            cs = slice(hh * SGU_HD, (hh + 1) * SGU_HD)
            for j in range(ts // SGU_CHUNK):
                rs = slice(j * SGU_CHUNK, (j + 1) * SGU_CHUNK)
                mixed = jnp.dot(wm, vb[rs, cs], preferred_element_type=F32) + bcol
                out_ref[rs, cs] = (u[rs, cs] * mixed).astype(BF16)

    return pl.pallas_call(
        body, name=name, grid=(T // ts,),
        in_specs=[_row_spec(ts, 2 * SGU_W), _vec_spec(SGU_W), _vec_spec(SGU_W),
                  pl.BlockSpec(ws.shape, lambda i: (0, 0, 0)), pl.BlockSpec(bs_t.shape, lambda i: (0, 0))],
        out_specs=_row_spec(ts, SGU_W),
        out_shape=jax.ShapeDtypeStruct((T, SGU_W), BF16),
        compiler_params=pltpu.CompilerParams(dimension_semantics=("parallel",)),
    )(zz, ln_g, ln_b, ws, bs_t)


def _sgu_gate_bwd(zz, dgated, ln_g, ln_b, ws, bs_t, name):
    T = zz.shape[0]
    ts = min(T, SGU_TILE)
    nt = T // ts

    def body(zz_ref, dg_ref, g_ref, b_ref, ws_ref, bs_ref, dzz_ref, dws_ref, dbs_ref, dlg_ref, dlb_ref, dlo_ref, dmx_ref):
        i = pl.program_id(0)

        @pl.when(i == 0)
        def _():
            dws_ref[...] = jnp.zeros_like(dws_ref)
            dmx_ref[...] = jnp.zeros_like(dmx_ref)
            dlg_ref[...] = jnp.zeros_like(dlg_ref)
            dlb_ref[...] = jnp.zeros_like(dlb_ref)

        zzv = zz_ref[...]
        z = _gelu(zzv)
        u = z[:, :SGU_W]
        xh, rstd, vn = _sgu_ln(z[:, SGU_W:], g_ref[...], b_ref[...])
        vb = vn.astype(BF16)
        dgv = dg_ref[...].astype(F32)
        tril = _tril_mask()
        for hh in range(SGU_H):
            wm = jnp.where(tril, ws_ref[hh], 0.0).astype(BF16)
            bcol = bs_ref[:, hh:hh + 1]
            cs = slice(hh * SGU_HD, (hh + 1) * SGU_HD)
            for j in range(ts // SGU_CHUNK):
                rs = slice(j * SGU_CHUNK, (j + 1) * SGU_CHUNK)
                mixed = jnp.dot(wm, vb[rs, cs], preferred_element_type=F32) + bcol
                dmixed = dgv[rs, cs] * u[rs, cs]
                dzz_ref[rs, cs] = (dgv[rs, cs] * mixed * _gelu_grad(zzv[rs, cs])).astype(BF16)
                dmb = dmixed.astype(BF16)
                dws_ref[hh] += lax.dot_general(dmb, vb[rs, cs], (((1,), (1,)), ((), ())), preferred_element_type=F32)
                dmx_ref[hh] += dmixed
                dlo_ref[rs, cs] = lax.dot_general(wm, dmb, (((0,), (0,)), ((), ())), preferred_element_type=F32)
        dlo = dlo_ref[...]
        dlg_ref[...] += _colsum(dlo * xh)
        dlb_ref[...] += _colsum(dlo)
        dxh = dlo * g_ref[...]
        dv = rstd * (dxh - jnp.mean(dxh, axis=-1, keepdims=True) - xh * jnp.mean(dxh * xh, axis=-1, keepdims=True))
        dzz_ref[:, SGU_W:] = (dv * _gelu_grad(zzv[:, SGU_W:])).astype(BF16)

        @pl.when(i == nt - 1)
        def _():
            tril_f = tril.astype(F32)
            for hh in range(SGU_H):
                dws_ref[hh] = dws_ref[hh] * tril_f
                dbs_ref[hh] = jnp.broadcast_to(jnp.sum(dmx_ref[hh], axis=-1, keepdims=True), (SGU_CHUNK, SGU_HD))

    full3 = pl.BlockSpec(ws.shape, lambda i: (0, 0, 0))
    return pl.pallas_call(
        body, name=name, grid=(nt,),
        in_specs=[_row_spec(ts, 2 * SGU_W), _row_spec(ts, SGU_W), _vec_spec(SGU_W), _vec_spec(SGU_W), full3,
                  pl.BlockSpec(bs_t.shape, lambda i: (0, 0))],
        out_specs=[_row_spec(ts, 2 * SGU_W), full3, full3, _vec_spec(SGU_W), _vec_spec(SGU_W)],
        out_shape=[jax.ShapeDtypeStruct((T, 2 * SGU_W), BF16), jax.ShapeDtypeStruct(ws.shape, F32), jax.ShapeDtypeStruct(ws.shape, F32),
                   jax.ShapeDtypeStruct((1, SGU_W), F32), jax.ShapeDtypeStruct((1, SGU_W), F32)],
        scratch_shapes=[pltpu.VMEM((ts, SGU_W), F32), pltpu.VMEM(ws.shape, F32)],
        compiler_params=pltpu.CompilerParams(dimension_semantics=("arbitrary",)),
    )(zz, dgated, ln_g, ln_b, ws, bs_t)


def _rope_fwd(blk, cc, sa, sb):
    return blk * cc + pltpu.roll(blk, 96, axis=1) * sa + pltpu.roll(blk, 32, axis=1) * sb


def _rope_bwd(d, cc, sa, sb):
    return d * cc + pltpu.roll(d * sa, 32, axis=1) + pltpu.roll(d * sb, 96, axis=1)


def _rms(v, g):
    r = lax.rsqrt(jnp.mean(v * v, axis=-1, keepdims=True) + RMS_EPS)
    vn = v * r
    return vn, r, vn * g


def _rms_bwd(dy, vn, r, g):
    dvn = dy * g
    return r * (dvn - vn * jnp.mean(dvn * vn, axis=-1, keepdims=True))


MLA_TILE = 256
_KV0 = MLA_QL
_KR0 = MLA_QL + MLA_KVL


def _mla_lat_fwd(lat, qg, kvg, cc, sa, sb, name):
    T = lat.shape[0]
    tr = min(T, ROW_TILE)

    def body(lat_ref, qg_ref, kvg_ref, cc_ref, sa_ref, sb_ref, cq_ref, ckv_ref, kr_ref):
        lv = lat_ref[...]
        cq_ref[...] = _rms(lv[:, :_KV0], qg_ref[...])[2].astype(BF16)
        ckv_ref[...] = _rms(lv[:, _KV0:_KR0], kvg_ref[...])[2].astype(BF16)
        kr_ref[...] = _rope_fwd(lv[:, _KR0:], cc_ref[...], sa_ref[...], sb_ref[...])

    return pl.pallas_call(
        body, name=name, grid=(T // tr,),
        in_specs=[_row_spec(tr, MLA_LATP), _vec_spec(MLA_QL), _vec_spec(MLA_KVL), _row_spec(tr, 128), _row_spec(tr, 128), _row_spec(tr, 128)],
        out_specs=[_row_spec(tr, MLA_QL), _row_spec(tr, MLA_KVL), _row_spec(tr, 128)],
        out_shape=[jax.ShapeDtypeStruct((T, MLA_QL), BF16), jax.ShapeDtypeStruct((T, MLA_KVL), BF16), jax.ShapeDtypeStruct((T, 128), F32)],
        compiler_params=pltpu.CompilerParams(dimension_semantics=("parallel",)),
    )(lat, qg, kvg, cc, sa, sb)


def _mla_lat_bwd(lat, dcqn, dckvn, dkrot, qg, kvg, cc, sa, sb, name):
    T = lat.shape[0]
    tr = min(T, ROW_TILE)

    def body(lat_ref, dcq_ref, dckv_ref, dkr_ref, qg_ref, kvg_ref, cc_ref, sa_ref, sb_ref, dlat_ref, dqg_ref, dkvg_ref):
        @pl.when(pl.program_id(0) == 0)
        def _():
            dqg_ref[...] = jnp.zeros_like(dqg_ref)
            dkvg_ref[...] = jnp.zeros_like(dkvg_ref)

        lv = lat_ref[...]
        qn, qr, _ = _rms(lv[:, :_KV0], qg_ref[...])
        kn, kr, _ = _rms(lv[:, _KV0:_KR0], kvg_ref[...])
        dcq = dcq_ref[...]
        dckv = dckv_ref[...]
        dqg_ref[...] += _colsum(dcq * qn)
        dkvg_ref[...] += _colsum(dckv * kn)
        dlat_ref[:, :_KV0] = _rms_bwd(dcq, qn, qr, qg_ref[...]).astype(BF16)
        dlat_ref[:, _KV0:_KR0] = _rms_bwd(dckv, kn, kr, kvg_ref[...]).astype(BF16)
        dlat_ref[:, _KR0:] = _rope_bwd(dkr_ref[...], cc_ref[...], sa_ref[...], sb_ref[...]).astype(BF16)

    return pl.pallas_call(
        body, name=name, grid=(T // tr,),
        in_specs=[_row_spec(tr, MLA_LATP), _row_spec(tr, MLA_QL), _row_spec(tr, MLA_KVL), _row_spec(tr, 128),
                  _vec_spec(MLA_QL), _vec_spec(MLA_KVL), _row_spec(tr, 128), _row_spec(tr, 128), _row_spec(tr, 128)],
        out_specs=[_row_spec(tr, MLA_LATP), _vec_spec(MLA_QL), _vec_spec(MLA_KVL)],
        out_shape=[jax.ShapeDtypeStruct((T, MLA_LATP), BF16), jax.ShapeDtypeStruct((1, MLA_QL), F32), jax.ShapeDtypeStruct((1, MLA_KVL), F32)],
        compiler_params=pltpu.CompilerParams(dimension_semantics=("arbitrary",)),
    )(lat, dcqn, dckvn, dkrot, qg, kvg, cc, sa, sb)


def _mla_prep(qpad, kv, krot, cc, sa, sb, name):
    T = qpad.shape[0]
    tr = min(T, MLA_TILE)
    HW = MLA_H * MLA_HP

    def body(q_ref, kv_ref, kr_ref, cc_ref, sa_ref, sb_ref, qo_ref, ko_ref, vo_ref):
        cc, sa, sb = cc_ref[...], sa_ref[...], sb_ref[...]
        krb = kr_ref[...].astype(BF16)
        for hh in range(MLA_H):
            a, m, b = hh * MLA_HP, hh * MLA_HP + MLA_NOPE, (hh + 1) * MLA_HP
            qo_ref[:, a:m] = (q_ref[:, a:m] * SM_SCALE).astype(BF16)
            qo_ref[:, m:b] = (_rope_fwd(q_ref[:, m:b], cc, sa, sb) * SM_SCALE).astype(BF16)
            ko_ref[:, a:m] = kv_ref[:, a:m].astype(BF16)
            ko_ref[:, m:b] = krb
            vo_ref[:, hh * MLA_V:(hh + 1) * MLA_V] = kv_ref[:, m:b].astype(BF16)

    return pl.pallas_call(
        body, name=name, grid=(T // tr,),
        in_specs=[_row_spec(tr, HW), _row_spec(tr, HW), _row_spec(tr, 128), _row_spec(tr, 128), _row_spec(tr, 128), _row_spec(tr, 128)],
        out_specs=[_row_spec(tr, HW), _row_spec(tr, HW), _row_spec(tr, MLA_H * MLA_V)],
        out_shape=[jax.ShapeDtypeStruct((T, HW), BF16), jax.ShapeDtypeStruct((T, HW), BF16), jax.ShapeDtypeStruct((T, MLA_H * MLA_V), BF16)],
        compiler_params=pltpu.CompilerParams(dimension_semantics=("parallel",)),
    )(qpad, kv, krot, cc, sa, sb)


ATT_HG = 4


def _mla_prep_bwd(dqt, dk, dv, cc, sa, sb, name):
    _, nq, _, tq = dqt.shape
    T = nq * tq
    gw = ATT_HG * MLA_HP

    def body(dq_ref, dk_ref, dv_ref, cc_ref, sa_ref, sb_ref, dqp_ref, dkv_ref, dkr_ref):
        @pl.when(pl.program_id(1) == 0)
        def _():
            dkr_ref[...] = jnp.zeros_like(dkr_ref)

        cc, sa, sb = cc_ref[...], sa_ref[...], sb_ref[...]
        acc = jnp.zeros((tq, 128), F32)
        for hh in range(ATT_HG):
            a, m, b = hh * MLA_HP, hh * MLA_HP + MLA_NOPE, (hh + 1) * MLA_HP
            dqh = dq_ref[hh].astype(F32).T * SM_SCALE
            dqp_ref[:, a:m] = dqh[:, :MLA_NOPE].astype(BF16)
            dqp_ref[:, m:b] = _rope_bwd(dqh[:, MLA_NOPE:], cc, sa, sb).astype(BF16)
            dkv_ref[:, a:m] = dk_ref[:, a:m]
            dkv_ref[:, m:b] = dv_ref[:, hh * MLA_V:(hh + 1) * MLA_V]
            acc = acc + dk_ref[:, m:b].astype(F32)
        dkr_ref[...] += acc

    tab = pl.BlockSpec((tq, 128), lambda i, g: (i, 0))
    return pl.pallas_call(
        body, name=name, grid=(nq, MLA_H // ATT_HG),
        in_specs=[pl.BlockSpec((ATT_HG, None, MLA_HP, tq), lambda i, g: (g, i, 0, 0)), pl.BlockSpec((tq, gw), lambda i, g: (i, g)),
                  pl.BlockSpec((tq, ATT_HG * MLA_V), lambda i, g: (i, g)), tab, tab, tab],
        out_specs=[pl.BlockSpec((tq, gw), lambda i, g: (i, g)), pl.BlockSpec((tq, gw), lambda i, g: (i, g)), tab],
        out_shape=[jax.ShapeDtypeStruct((T, MLA_H * MLA_HP), BF16), jax.ShapeDtypeStruct((T, MLA_H * MLA_HP), BF16), jax.ShapeDtypeStruct((T, 128), F32)],
        compiler_params=pltpu.CompilerParams(dimension_semantics=("parallel", "arbitrary")),
    )(dqt, dk, dv, cc, sa, sb)


_NT = (((1,), (1,)), ((), ()))


def _as_row(col, n):
    return jnp.broadcast_to(col, (n, 128)).T[0:1, :]


def _attn_fwd(q, k, vt, name):
    T = q.shape[0]
    tq = tk = min(T, ATT_TILE)
    nq = T // tq

    def body(q_ref, k_ref, vt_ref, o_ref, lse_ref, m_ref, l_ref, acc_ref):
        i = pl.program_id(1)
        qv = q_ref[...]
        m_ref[...] = jnp.full_like(m_ref, NEG_INF)
        l_ref[...] = jnp.zeros_like(l_ref)
        acc_ref[...] = jnp.zeros_like(acc_ref)

        def step(j, diag):
            off = pl.multiple_of(j * tk, tk)
            st = lax.dot_general(k_ref[pl.ds(off, tk), :], qv, _NT, preferred_element_type=F32)
            if diag:
                st = jnp.where(lax.broadcasted_iota(jnp.int32, (tk, tq), 0) <= lax.broadcasted_iota(jnp.int32, (tk, tq), 1), st, NEG_INF)
            m_prev = m_ref[...]
            m_new = jnp.maximum(m_prev, jnp.max(st, axis=0, keepdims=True))
            alpha = jnp.exp(m_prev - m_new)
            pt = jnp.exp(st - m_new)
            l_ref[...] = alpha * l_ref[...] + jnp.sum(pt, axis=0, keepdims=True)
            acc_ref[...] = alpha * acc_ref[...] + jnp.dot(vt_ref[j], pt.astype(BF16), preferred_element_type=F32)
            m_ref[...] = m_new

        def loop_body(j, carry):
            step(j, False)
            return carry

        lax.fori_loop(0, i, loop_body, 0)
        step(i, True)
        o_ref[...] = (acc_ref[...] / l_ref[...]).T.astype(BF16)
        lse_ref[...] = m_ref[...] + jnp.log(l_ref[...])

    return pl.pallas_call(
        body, name=name, grid=(MLA_H, nq),
        in_specs=[pl.BlockSpec((tq, MLA_HP), lambda h, i: (i, h)), pl.BlockSpec((T, MLA_HP), lambda h, i: (0, h)),
                  pl.BlockSpec((None, nq, MLA_V, tk), lambda h, i: (h, 0, 0, 0))],
        out_specs=[pl.BlockSpec((tq, MLA_V), lambda h, i: (i, h)), pl.BlockSpec((None, None, 1, tq), lambda h, i: (h, i, 0, 0))],
        out_shape=[jax.ShapeDtypeStruct((T, MLA_H * MLA_V), BF16), jax.ShapeDtypeStruct((MLA_H, nq, 1, tq), F32)],
        scratch_shapes=[pltpu.VMEM((1, tq), F32), pltpu.VMEM((1, tq), F32), pltpu.VMEM((MLA_V, tq), F32)],
        compiler_params=pltpu.CompilerParams(dimension_semantics=("parallel", "arbitrary")),
    )(q, k, vt)


def _attn_delta(do, o, name):
    T = do.shape[0]
    tq = min(T, ATT_TILE)

    def body(do_ref, o_ref, d_ref):
        for hh in range(MLA_H):
            cs = slice(hh * MLA_V, (hh + 1) * MLA_V)
            s = jnp.sum(do_ref[:, cs].astype(F32) * o_ref[:, cs].astype(F32), axis=-1, keepdims=True)
            d_ref[hh] = _as_row(s, tq)

    return pl.pallas_call(
        body, name=name, grid=(T // tq,),
        in_specs=[_row_spec(tq, MLA_H * MLA_V), _row_spec(tq, MLA_H * MLA_V)],
        out_specs=pl.BlockSpec((MLA_H, None, 1, tq), lambda i: (0, i, 0, 0)),
        out_shape=jax.ShapeDtypeStruct((MLA_H, T // tq, 1, tq), F32),
        compiler_params=pltpu.CompilerParams(dimension_semantics=("parallel",)),
    )(do, o)


def _attn_bwd(q, k, kt, v, do, lse, delta, name):
    T = q.shape[0]
    tq = tk = min(T, ATT_TILE)
    nq = nk = T // tq
    tsd = min(tq, ATT_SUB)

    def body(q_ref, k_ref, kt_ref, v_ref, do_ref, lse_ref, dl_ref, dqt_ref, dk_ref, dv_ref, dq_acc, dk_acc, dv_acc):
        j = pl.program_id(1)

        @pl.when(j == 0)
        def _():
            dq_acc[...] = jnp.zeros_like(dq_acc)

        dk_acc[...] = jnp.zeros_like(dk_acc)
        dv_acc[...] = jnp.zeros_like(dv_acc)

        def step(i, diag):
            off = pl.multiple_of(i * tq, tq)
            lse_i, dl_i = lse_ref[i], dl_ref[i]
            ts, nsub = (tsd, tq // tsd) if diag else (tq, 1)
            for u in range(nsub):
                cols = slice(u * ts, (u + 1) * ts)
                nk_u = (u + 1) * ts if diag else tk
                qi, doi = q_ref[pl.ds(off + u * ts, ts), :], do_ref[pl.ds(off + u * ts, ts), :]
                st = lax.dot_general(k_ref[:nk_u, :], qi, _NT, preferred_element_type=F32)
                if diag:
                    qcol = u * ts + lax.broadcasted_iota(jnp.int32, (nk_u, ts), 1)
                    st = jnp.where(lax.broadcasted_iota(jnp.int32, (nk_u, ts), 0) <= qcol, st, NEG_INF)
                pt = jnp.exp(st - lse_i[:, cols])
                dv_acc[:nk_u, :] += jnp.dot(pt.astype(BF16), doi, preferred_element_type=F32)
                dpt = lax.dot_general(v_ref[:nk_u, :], doi, _NT, preferred_element_type=F32)
                dsb = (pt * (dpt - dl_i[:, cols])).astype(BF16)
                dk_acc[:nk_u, :] += jnp.dot(dsb, qi, preferred_element_type=F32)
                dq_acc[i, :, cols] += jnp.dot(kt_ref[:, :nk_u], dsb, preferred_element_type=F32)

        def loop_body(i, carry):
            step(i, False)
            return carry

        step(j, True)
        lax.fori_loop(j + 1, nq, loop_body, 0)
        dk_ref[...] = dk_acc[...].astype(BF16)
        dv_ref[...] = dv_acc[...].astype(BF16)

        @pl.when(j == nk - 1)
        def _():
            dqt_ref[...] = dq_acc[...].astype(BF16)

    stat = pl.BlockSpec((None, nq, 1, tq), lambda h, j: (h, 0, 0, 0))
    return pl.pallas_call(
        body, name=name, grid=(MLA_H, nk),
        in_specs=[pl.BlockSpec((T, MLA_HP), lambda h, j: (0, h)), pl.BlockSpec((tk, MLA_HP), lambda h, j: (j, h)),
                  pl.BlockSpec((MLA_HP, tk), lambda h, j: (h, j)), pl.BlockSpec((tk, MLA_V), lambda h, j: (j, h)),
                  pl.BlockSpec((T, MLA_V), lambda h, j: (0, h)), stat, stat],
        out_specs=[pl.BlockSpec((None, nq, MLA_HP, tq), lambda h, j: (h, 0, 0, 0)), pl.BlockSpec((tk, MLA_HP), lambda h, j: (j, h)),
                   pl.BlockSpec((tk, MLA_V), lambda h, j: (j, h))],
        out_shape=[jax.ShapeDtypeStruct((MLA_H, nq, MLA_HP, tq), BF16), jax.ShapeDtypeStruct((T, MLA_H * MLA_HP), BF16),
                   jax.ShapeDtypeStruct((T, MLA_H * MLA_V), BF16)],
        scratch_shapes=[pltpu.VMEM((nq, MLA_HP, tq), F32), pltpu.VMEM((tk, MLA_HP), F32), pltpu.VMEM((tk, MLA_V), F32)],
        compiler_params=pltpu.CompilerParams(dimension_semantics=("parallel", "arbitrary")),
    )(q, k, kt, v, do, lse, delta)


ADA_TN = 512


def _silu(v):
    return v * (1.0 / (1.0 + jnp.exp(-v)))


def _ada_fwd(c_all, ada_w, ada_b_loc, name):
    L, D, Nc = ada_w.shape
    B = c_all.shape[0]

    def body(c_ref, w_ref, b_ref, o_ref):
        ca = _silu(c_ref[...]).astype(BF16)
        o_ref[...] = jnp.dot(ca, w_ref[...].astype(BF16), preferred_element_type=F32) + b_ref[...]

    return pl.pallas_call(
        body, name=name, grid=(L, Nc // ADA_TN),
        in_specs=[pl.BlockSpec((B, D), lambda l, n: (0, 0)), pl.BlockSpec((None, D, ADA_TN), lambda l, n: (l, 0, n)),
                  pl.BlockSpec((None, 1, ADA_TN), lambda l, n: (l, 0, n))],
        out_specs=pl.BlockSpec((None, B, ADA_TN), lambda l, n: (l, 0, n)),
        out_shape=jax.ShapeDtypeStruct((L, B, Nc), F32),
        compiler_params=pltpu.CompilerParams(dimension_semantics=("parallel", "parallel")),
    )(c_all, ada_w, ada_b_loc)


def _ada_bwd(c_all_t, dmod_loc, name):
    D, B = c_all_t.shape
    L, _, Nc = dmod_loc.shape

    def body(c_ref, d_ref, o_ref):
        ca = _silu(c_ref[...])
        dv = d_ref[...]
        acc = ca[:, 0:1] * dv[0:1, :]
        for b in range(1, B):
            acc = acc + ca[:, b:b + 1] * dv[b:b + 1, :]
        o_ref[...] = acc

    return pl.pallas_call(
        body, name=name, grid=(L, Nc // ADA_TN),
        in_specs=[pl.BlockSpec((D, B), lambda l, n: (0, 0)), pl.BlockSpec((None, B, ADA_TN), lambda l, n: (l, 0, n))],
        out_specs=pl.BlockSpec((None, D, ADA_TN), lambda l, n: (l, 0, n)),
        out_shape=jax.ShapeDtypeStruct((L, D, Nc), F32),
        compiler_params=pltpu.CompilerParams(dimension_semantics=("parallel", "parallel")),
    )(c_all_t, dmod_loc)


def _sum_lead(parts, name, out_dtype=F32):
    R, C = parts[0].shape[1:]
    n_tot = sum(p.shape[0] for p in parts)
    tr = R
    for cand in (512, 256, 128, 64, 32, 16):
        if R % cand == 0 and cand * C * 4 * n_tot <= (8 << 20):
            tr = cand
            break

    def body(*refs):
        o_ref = refs[-1]
        acc = None
        for r in refs[:-1]:
            for s in range(r.shape[0]):
                acc = r[s].astype(F32) if acc is None else acc + r[s].astype(F32)
        o_ref[...] = acc.astype(o_ref.dtype)

    return pl.pallas_call(
        body, name=name, grid=(R // tr,),
        in_specs=[pl.BlockSpec((p.shape[0], tr, C), lambda i: (0, i, 0)) for p in parts],
        out_specs=pl.BlockSpec((tr, C), lambda i: (i, 0)),
        out_shape=jax.ShapeDtypeStruct((R, C), out_dtype),
        compiler_params=pltpu.CompilerParams(dimension_semantics=("parallel",)),
    )(*parts)


_ADAM_C1 = 1.0 - ADAM_B1 ** ADAM_STEP
_ADAM_C2 = 1.0 - ADAM_B2 ** ADAM_STEP


def _adamw(w, g, m, v, name):
    shape = w.shape
    C = shape[-1]
    R = math.prod(shape[:-1]) if len(shape) > 1 else 1
    w2, g2, m2, v2 = (a.reshape(R, C) for a in (w, g, m, v))
    tr = R
    for cand in (1024, 512, 256, 128, 64, 32, 16, 8):
        if R % cand == 0 and cand * C * 4 <= (1 << 20):
            tr = cand
            break

    def body(w_ref, g_ref, m_ref, v_ref, d_ref, nm_ref, nv_ref):
        gv = g_ref[...]
        mn = ADAM_B1 * m_ref[...] + (1.0 - ADAM_B1) * gv
        vn = ADAM_B2 * v_ref[...] + (1.0 - ADAM_B2) * (gv * gv)
        nm_ref[...] = mn
        nv_ref[...] = vn
        m_hat = mn / _ADAM_C1
        v_hat = vn / _ADAM_C2
        d_ref[...] = -ADAM_LR * (m_hat / (jnp.sqrt(v_hat) + ADAM_EPS) + ADAM_WD * w_ref[...])

    spec = pl.BlockSpec((tr, C), lambda i: (i, 0))
    outs = pl.pallas_call(
        body, name=name, grid=(R // tr,),
        in_specs=[spec] * 4, out_specs=[spec] * 3,
        out_shape=[jax.ShapeDtypeStruct((R, C), F32)] * 3,
        compiler_params=pltpu.CompilerParams(dimension_semantics=("parallel",)),
    )(w2, g2, m2, v2)
    return tuple(o.reshape(shape) for o in outs)


def _row_tile(rows, cols, itemsize, budget):
    for cand in (1024, 512, 256, 128, 64, 32, 16):
        if rows % cand == 0 and cand * cols * itemsize <= budget:
            return cand
    return rows


def _sum_sel(sel, stacked, others, name, out_dtype):
    R, C = stacked.shape[1:]
    n_tot = 1 + sum(o.shape[0] for o in others)
    tr = _row_tile(R, C, 4 * n_tot, 8 << 20)

    def body(sel_ref, s_ref, *refs):
        o_ref = refs[-1]
        acc = s_ref[...].astype(F32)
        for r in refs[:-1]:
            for s in range(r.shape[0]):
                acc = acc + r[s].astype(F32)
        o_ref[...] = acc.astype(o_ref.dtype)

    return pl.pallas_call(
        body, name=name,
        grid_spec=pltpu.PrefetchScalarGridSpec(
            num_scalar_prefetch=1, grid=(R // tr,),
            in_specs=[pl.BlockSpec((None, tr, C), lambda i, s: (s[0], i, 0))] + [pl.BlockSpec((o.shape[0], tr, C), lambda i, s: (0, i, 0)) for o in others],
            out_specs=pl.BlockSpec((tr, C), lambda i, s: (i, 0))),
        out_shape=jax.ShapeDtypeStruct((R, C), out_dtype),
        compiler_params=pltpu.CompilerParams(dimension_semantics=("parallel",)),
    )(sel, stacked, *others)


def _adamw_piece(cidx, w2, m2, v2, mine, got, bufs, row0, name):
    hr, C = mine.shape
    tr = _row_tile(math.gcd(hr, row0) if row0 else hr, C, 4, 1 << 20)
    nt = hr // tr

    def body(c_ref, w_ref, m_ref, v_ref, a_ref, b_ref, _g, _d, _nm, _nv, g_ref, d_ref, nm_ref, nv_ref):
        gv = jnp.where(pl.program_id(0) == c_ref[0], a_ref[...], b_ref[...])
        mn = ADAM_B1 * m_ref[...] + (1.0 - ADAM_B1) * gv
        vn = ADAM_B2 * v_ref[...] + (1.0 - ADAM_B2) * (gv * gv)
        g_ref[...] = gv
        nm_ref[...] = mn
        nv_ref[...] = vn
        d_ref[...] = -ADAM_LR * ((mn / _ADAM_C1) / (jnp.sqrt(vn / _ADAM_C2) + ADAM_EPS) + ADAM_WD * w_ref[...])

    rows = pl.BlockSpec((tr, C), lambda hf, t, c: (row0 // tr + hf * nt + t, 0))
    half = pl.BlockSpec((tr, C), lambda hf, t, c: (t, 0))
    return pl.pallas_call(
        body, name=name,
        grid_spec=pltpu.PrefetchScalarGridSpec(num_scalar_prefetch=1, grid=(2, nt), in_specs=[rows] * 3 + [half] * 2 + [_ANY_SPEC] * 4,
                                               out_specs=[rows] * 4),
        out_shape=[jax.ShapeDtypeStruct(w2.shape, F32)] * 4,
        input_output_aliases={6 + n: n for n in range(4)},
        compiler_params=pltpu.CompilerParams(dimension_semantics=("parallel", "parallel")),
    )(cidx, w2, m2, v2, mine, got, *bufs)


_VMEM_SPEC = pl.BlockSpec(memory_space=pltpu.VMEM)
_HBM_SPEC = pl.BlockSpec(memory_space=pltpu.HBM)


def _flip(v, bit):
    return (1 - v) if bit else v


def _allgather8(v, name):
    def body(v_ref, out_ref, send_sems, recv_sems, local_sem):
        x, y, c = _idx()
        me = 4 * x + 2 * y + c
        mine = pltpu.make_async_copy(v_ref, out_ref.at[me], local_sem)
        mine.start()
        sends = []
        for k in range(1, N_DEV):
            peer = (_flip(x, k & 4), _flip(y, k & 2), _flip(c, k & 1))
            cp = pltpu.make_async_remote_copy(src_ref=v_ref, dst_ref=out_ref.at[me], send_sem=send_sems.at[k - 1], recv_sem=recv_sems.at[k - 1],
                                              device_id=peer, device_id_type=MESH)
            cp.start()
            sends.append(cp)
        for k in range(1, N_DEV):
            px, py, pc = _flip(x, k & 4), _flip(y, k & 2), _flip(c, k & 1)
            src = 4 * px + 2 * py + pc
            pltpu.make_async_remote_copy(src_ref=v_ref, dst_ref=out_ref.at[src], send_sem=send_sems.at[k - 1], recv_sem=recv_sems.at[k - 1],
                                         device_id=(px, py, pc), device_id_type=MESH).wait_recv()
        for cp in sends:
            cp.wait_send()
        mine.wait()

    return pl.pallas_call(
        body, name=name,
        out_shape=jax.ShapeDtypeStruct((N_DEV, *v.shape), v.dtype),
        in_specs=[_VMEM_SPEC], out_specs=_VMEM_SPEC,
        scratch_shapes=[pltpu.SemaphoreType.DMA((N_DEV - 1,)), pltpu.SemaphoreType.DMA((N_DEV - 1,)), pltpu.SemaphoreType.DMA],
    )(v)


def _mod_exchange(modp, name):
    _, L, Nc = modp.shape

    def body(p_ref, out_ref, send_sems, recv_sems, local_sem):
        x, y, c = _idx()
        me, chip = 4 * x + 2 * y + c, 2 * x + y
        mine = pltpu.make_async_copy(p_ref.at[me], out_ref.at[chip], local_sem)
        mine.start()
        sends = []
        for k in range(1, N_CHIPS):
            px, py = _flip(x, k & 2), _flip(y, k & 1)
            cp = pltpu.make_async_remote_copy(src_ref=p_ref.at[4 * px + 2 * py + c], dst_ref=out_ref.at[chip],
                                              send_sem=send_sems.at[k - 1], recv_sem=recv_sems.at[k - 1], device_id=(px, py, c), device_id_type=MESH)
            cp.start()
            sends.append(cp)
        for k in range(1, N_CHIPS):
            px, py = _flip(x, k & 2), _flip(y, k & 1)
            pltpu.make_async_remote_copy(src_ref=p_ref.at[me], dst_ref=out_ref.at[2 * px + py], send_sem=send_sems.at[k - 1],
                                         recv_sem=recv_sems.at[k - 1], device_id=(px, py, c), device_id_type=MESH).wait_recv()
        for cp in sends:
            cp.wait_send()
        mine.wait()

    return pl.pallas_call(
        body, name=name,
        out_shape=jax.ShapeDtypeStruct((N_CHIPS, L, Nc), modp.dtype),
        in_specs=[_VMEM_SPEC], out_specs=_VMEM_SPEC,
        scratch_shapes=[pltpu.SemaphoreType.DMA((N_CHIPS - 1,)), pltpu.SemaphoreType.DMA((N_CHIPS - 1,)), pltpu.SemaphoreType.DMA],
    )(modp)


_SEM_SPEC = pl.BlockSpec(memory_space=pltpu.SEMAPHORE)
_ANY_SPEC = pl.BlockSpec(memory_space=pl.ANY)
_EFFECT = pltpu.SideEffectType.DATAFLOW_SIDE_EFFECTING


def _hbm(a):
    return pltpu.with_memory_space_constraint(a, pltpu.HBM)


def _xchip_copies(mode, srcs, lands, send_sems, recv_sems, waiting):
    x, y, c = _idx()
    chip = 2 * x + y
    out = []
    for a in range(len(srcs)):
        for k in range(1, N_CHIPS):
            px, py = _flip(x, k & 2), _flip(y, k & 1)
            peer = 2 * px + py
            if mode == "gather":
                src, dst, mine = srcs[a].at[c], lands[a].at[chip, c], lands[a].at[peer, c]
            else:
                src, dst, mine = srcs[a].at[peer], lands[a].at[k - 1], lands[a].at[k - 1]
            q = a * (N_CHIPS - 1) + k - 1
            out.append(pltpu.make_async_remote_copy(src_ref=src, dst_ref=mine if waiting else dst, send_sem=send_sems[q], recv_sem=recv_sems[q],
                                                    device_id=(px, py, c), device_id_type=MESH))
    return out


def _xchip_start(mode, srcs, land_shapes, dep, name):
    n = len(srcs)
    ns = n * (N_CHIPS - 1)

    def body(*refs):
        src_refs, land_refs = refs[:n], refs[n:2 * n]
        outs = refs[2 * n + 1:]
        for cp in _xchip_copies(mode, src_refs, land_refs, outs[:ns], outs[ns:2 * ns], waiting=False):
            cp.start()
        outs[-1][...] = jnp.zeros_like(outs[-1])

    lands = [_hbm(lax.empty(s.shape, s.dtype)) for s in land_shapes]
    outs = pl.pallas_call(
        body, name=name,
        out_shape=(*[pltpu.SemaphoreType.DMA(())] * (2 * ns), *[pltpu.HBM(s.shape, s.dtype) for s in srcs],
                   *[pltpu.HBM(s.shape, s.dtype) for s in land_shapes], jax.ShapeDtypeStruct((8, 128), F32)),
        in_specs=[_HBM_SPEC] * (2 * n) + [_ANY_SPEC],
        out_specs=(*[_SEM_SPEC] * (2 * ns), *[_HBM_SPEC] * (2 * n), _VMEM_SPEC),
        input_output_aliases={i: 2 * ns + i for i in range(2 * n)},
        compiler_params=pltpu.CompilerParams(has_side_effects=_EFFECT),
    )(*[_hbm(s) for s in srcs], *lands, dep)
    return list(outs[:ns]), list(outs[ns:2 * ns]), list(outs[2 * ns:2 * ns + n]), list(outs[2 * ns + n:2 * ns + 2 * n]), outs[-1]


def _xchip_wait(mode, send_sems, recv_sems, srcs, lands, after, name):
    n = len(srcs)
    ns = n * (N_CHIPS - 1)

    def body(*refs):
        src_refs, land_refs = refs[:n], refs[n:2 * n]
        sems = refs[2 * n:2 * n + 2 * ns]
        for cp in _xchip_copies(mode, src_refs, land_refs, sems[:ns], sems[ns:], waiting=True):
            cp.wait_send()
            cp.wait_recv()

    outs = pl.pallas_call(
        body, name=name,
        out_shape=(*[pltpu.HBM(s.shape, s.dtype) for s in srcs], *[pltpu.HBM(s.shape, s.dtype) for s in lands]),
        in_specs=[_HBM_SPEC] * (2 * n) + [_SEM_SPEC] * (2 * ns) + [_ANY_SPEC] * len(after),
        out_specs=tuple([_HBM_SPEC] * (2 * n)),
        input_output_aliases={i: i for i in range(2 * n)},
        compiler_params=pltpu.CompilerParams(has_side_effects=_EFFECT),
    )(*srcs, *lands, *send_sems, *recv_sems, *after)
    return list(outs[:n]), list(outs[n:])


def _sibling_fwd(lands, name):
    n = len(lands)

    def body(*refs):
        outs = refs[n:2 * n]
        send_sems, recv_sems = refs[2 * n:]
        x, y, c = _idx()
        sib = (x, y, 1 - c)
        sends = []
        for a in range(n):
            for k in range(1, N_CHIPS):
                src = 2 * _flip(x, k & 2) + _flip(y, k & 1)
                cp = pltpu.make_async_remote_copy(src_ref=outs[a].at[src, c], dst_ref=outs[a].at[src, c], send_sem=send_sems.at[a, k - 1],
                                                  recv_sem=recv_sems.at[a, k - 1], device_id=sib, device_id_type=MESH)
                cp.start()
                sends.append(cp)
        for a in range(n):
            for k in range(1, N_CHIPS):
                src = 2 * _flip(x, k & 2) + _flip(y, k & 1)
                pltpu.make_async_remote_copy(src_ref=outs[a].at[src, c], dst_ref=outs[a].at[src, 1 - c], send_sem=send_sems.at[a, k - 1],
                                             recv_sem=recv_sems.at[a, k - 1], device_id=sib, device_id_type=MESH).wait_recv()
        for cp in sends:
            cp.wait_send()

    return pl.pallas_call(
        body, name=name,
        out_shape=[jax.ShapeDtypeStruct(s.shape, s.dtype) for s in lands],
        in_specs=[_HBM_SPEC] * n, out_specs=[_HBM_SPEC] * n,
        input_output_aliases={i: i for i in range(n)},
        scratch_shapes=[pltpu.SemaphoreType.DMA((n, N_CHIPS - 1)), pltpu.SemaphoreType.DMA((n, N_CHIPS - 1))],
    )(*lands)


def _sibling_swap(parts, name):
    n = len(parts)

    def body(*refs):
        ins, outs = refs[:n], refs[n:2 * n]
        send_sems, recv_sems = refs[2 * n:]
        x, y, c = _idx()
        cps = []
        for a in range(n):
            cp = pltpu.make_async_remote_copy(src_ref=ins[a].at[1 - c], dst_ref=outs[a], send_sem=send_sems.at[a], recv_sem=recv_sems.at[a],
                                              device_id=(x, y, 1 - c), device_id_type=MESH)
            cp.start()
            cps.append(cp)
        for cp in cps:
            cp.wait()

    return pl.pallas_call(
        body, name=name,
        out_shape=[jax.ShapeDtypeStruct(p.shape[1:], p.dtype) for p in parts],
        in_specs=[_HBM_SPEC] * n, out_specs=[_HBM_SPEC] * n,
        scratch_shapes=[pltpu.SemaphoreType.DMA((n,)), pltpu.SemaphoreType.DMA((n,))],
    )(*parts)


def _sibling_send(halves, name):
    n = len(halves)

    def body(*refs):
        ins, outs = refs[:n], refs[n:2 * n]
        send_sems, recv_sems = refs[2 * n:]
        x, y, c = _idx()
        cps = []
        for a in range(n):
            cp = pltpu.make_async_remote_copy(src_ref=ins[a], dst_ref=outs[a], send_sem=send_sems.at[a], recv_sem=recv_sems.at[a],
                                              device_id=(x, y, 1 - c), device_id_type=MESH)
            cp.start()
            cps.append(cp)
        for cp in cps:
            cp.wait()

    return pl.pallas_call(
        body, name=name,
        out_shape=[jax.ShapeDtypeStruct(h.shape, h.dtype) for h in halves],
        in_specs=[_HBM_SPEC] * n, out_specs=[_HBM_SPEC] * n,
        scratch_shapes=[pltpu.SemaphoreType.DMA((n,)), pltpu.SemaphoreType.DMA((n,))],
    )(*halves)


def _col_full(g):
    k, n = g.shape[1], g.shape[2]
    return g.transpose(1, 0, 2).reshape(k, N_CHIPS * n)


def _col_blocks(w):
    k, n = w.shape
    return w.reshape(k, N_CHIPS, n // N_CHIPS).transpose(1, 0, 2)


def _row_blocks(w):
    k, n = w.shape
    return w.reshape(N_CHIPS, k // N_CHIPS, n)


_UQ_HEAD = MLA_NOPE + MLA_ROPE

_LAT = MLA_QL + MLA_KVL + MLA_ROPE
_POOL_R = len(POOL_WINDOWS) * (POOL_GD // N_CHIPS)

_PIECE_KINDS = {
    "mlp_w1": (D_MODEL, D_MODEL, _col_full, _col_blocks),
    "mlp_w2": (D_MODEL, D_MODEL, lambda g: g.reshape(4 * D_MODEL, D_MODEL), _row_blocks),
    "pool_w": (_POOL_R, POOL_GD,
               lambda g: g.reshape(N_CHIPS, len(POOL_WINDOWS), POOL_GD // N_CHIPS, POOL_GD).transpose(1, 0, 2, 3).reshape(len(POOL_WINDOWS), POOL_GD, POOL_GD),
               lambda w: w.reshape(len(POOL_WINDOWS), N_CHIPS, POOL_GD // N_CHIPS, POOL_GD).transpose(1, 0, 2, 3).reshape(N_CHIPS, _POOL_R, POOL_GD)),
    "sgu_w_in": (D_MODEL, 2 * SGU_W // N_CHIPS, _col_full, _col_blocks),
    "sgu_w_out": (SGU_W // N_CHIPS, D_MODEL, lambda g: g.reshape(SGU_W, D_MODEL), _row_blocks),
    "mla_w_dq_dkv": (D_MODEL // N_CHIPS, _LAT, lambda g: jnp.pad(g.reshape(D_MODEL, _LAT), ((0, 0), (0, MLA_LATP - _LAT))),
                     lambda w: _row_blocks(w[:, :_LAT])),
    "mla_w_uq": (MLA_QL, MLA_H * _UQ_HEAD // N_CHIPS,
                 lambda g: jnp.pad(_col_full(g).reshape(MLA_QL, MLA_H, _UQ_HEAD), ((0, 0), (0, 0), (0, MLA_HP - _UQ_HEAD))).reshape(MLA_QL, MLA_H * MLA_HP),
                 lambda w: _col_blocks(w.reshape(MLA_QL, MLA_H, MLA_HP)[:, :, :_UQ_HEAD].reshape(MLA_QL, MLA_H * _UQ_HEAD))),
    "mla_w_ukv": (MLA_KVL, MLA_H * (MLA_NOPE + MLA_V) // N_CHIPS, _col_full, _col_blocks),
    "mla_w_o": (MLA_H * MLA_V // N_CHIPS, D_MODEL, lambda g: g.reshape(MLA_H * MLA_V, D_MODEL), _row_blocks),
}
_MIXER_KINDS = (("pool_w",), ("sgu_w_in", "sgu_w_out"), ("mla_w_dq_dkv", "mla_w_uq", "mla_w_ukv", "mla_w_o"))


def _layer_pieces(i):
    return [(k, i // N_MIXERS) for k in _MIXER_KINDS[i % N_MIXERS]] + [("mlp_w1", i), ("mlp_w2", i)]


def _rope_tables(positions):
    inv_freq = ROPE_THETA ** (-jnp.arange(0, MLA_ROPE, 2, dtype=F32) / MLA_ROPE)
    ang = positions.astype(F32)[:, None] * inv_freq
    cos, sin = jnp.cos(ang), jnp.sin(ang)
    z32, z64 = jnp.zeros_like(cos), jnp.zeros((positions.shape[0], 64), F32)
    return (jnp.concatenate([cos, cos, z64], axis=1), jnp.concatenate([-sin, z32, z64], axis=1), jnp.concatenate([z32, sin, z64], axis=1))


def _local_step(x, positions, target, mod, S, weights_of, grads_of):
    D = D_MODEL
    cc, sa, sb = _rope_tables(positions)
    saved = []
    for i in range(DEPTH):
        sh1, sc1, g1, sh2, sc2, g2 = (mod[i:i + 1, n * D:(n + 1) * D] for n in range(6))
        kind, j = i % N_MIXERS, i // N_MIXERS
        gmix, gmlp = S["norm_mix_g"][i:i + 1], S["norm_mlp_g"][i:i + 1]
        W = weights_of(i, "mix", x)
        gmix = gmix + W["_tok"]
        st = {"x": x}
        if kind == 0:
            h = _norm_mod_fwd(x, gmix, sc1, sh1, F32, f"l{i}_norm1")
            x2, pooled, ypre = _pool_fwd(h, W["pool_w"], S["pool_scale"][j:j + 1], x, g1, f"l{i}_pool")
            st.update(pooled=pooled, y=ypre)
        elif kind == 1:
            h = _norm_mod_fwd(x, gmix, sc1, sh1, BF16, f"l{i}_norm1")
            zz = _mm(h, W["sgu_w_in"], out_dtypes=(F32,), name=f"l{i}_sgu_in")
            bs_t = S["sgu_b_s"].T
            gated = _sgu_gate_fwd(zz, S["sgu_ln_g"], S["sgu_ln_b"], S["sgu_w_s"], bs_t, f"l{i}_sgu_gate")
            x2, y = _mm(gated, W["sgu_w_out"], epi=_epi_residual, extras=((x, "mn"), (g1, "n")), out_dtypes=(F32, BF16), name=f"l{i}_sgu_out")
            st.update(h=h, zz=zz, gated=gated, y=y, bs_t=bs_t)
        else:
            h = _norm_mod_fwd(x, gmix, sc1, sh1, BF16, f"l{i}_norm1")
            lat = _mm(h, W["mla_w_dq_dkv"], out_dtypes=(F32,), name=f"l{i}_mla_lat")
            cqn, ckvn, krot = _mla_lat_fwd(lat, S["mla_q_norm_g"], S["mla_kv_norm_g"], cc, sa, sb, f"l{i}_mla_latn")
            qpad = _mm(cqn, W["mla_w_uq"], out_dtypes=(F32,), name=f"l{i}_mla_uq")
            kv = _mm(ckvn, W["mla_w_ukv"], out_dtypes=(F32,), name=f"l{i}_mla_ukv")
            q, k, v = _mla_prep(qpad, kv, krot, cc, sa, sb, f"l{i}_mla_prep")
            tkb = min(q.shape[0], ATT_TILE)
            vt = v.reshape(q.shape[0] // tkb, tkb, MLA_H, MLA_V).transpose(2, 0, 3, 1)
            o, lse = _attn_fwd(q, k, vt, f"l{i}_attn")
            x2, y = _mm(o, W["mla_w_o"], epi=_epi_residual, extras=((x, "mn"), (g1, "n")), out_dtypes=(F32, BF16), name=f"l{i}_mla_o")
            st.update(h=h, lat=lat, cqn=cqn, ckvn=ckvn, q=q, k=k, kt=k.T, v=v, o=o, lse=lse, y=y)
        Wm = weights_of(i, "mlp", x2)
        W = {**W, **Wm}
        h2 = _norm_mod_fwd(x2, gmlp + Wm["_tok"], sc2, sh2, BF16, f"l{i}_norm2")
        z = _mm(h2, W["mlp_w1"], epi=lambda acc: (jnp.square(jnp.maximum(acc, 0.0)),), name=f"l{i}_mlp1")
        x3, o2 = _mm(z, W["mlp_w2"], epi=_epi_residual, extras=((x2, "mn"), (g2, "n")), out_dtypes=(F32, BF16), name=f"l{i}_mlp2")
        st.update(x2=x2, h2=h2, z=z, o2=o2, W=W)
        saved.append(st)
        x = x3

    loss, dx, dfinal_g = _loss_head(x, target, S["final_g"], "loss_head")

    gS = {"final_g": dfinal_g, "norm_mix_g": [None] * DEPTH, "norm_mlp_g": [None] * DEPTH, "pool_scale": [None] * 2}
    dmod = [None] * DEPTH
    tok = jnp.zeros((), F32)
    for i in reversed(range(DEPTH)):
        st = saved[i]
        W, gW = st["W"], {}
        sh1, sc1, g1, sh2, sc2, g2 = (mod[i:i + 1, n * D:(n + 1) * D] for n in range(6))
        kind, j = i % N_MIXERS, i // N_MIXERS
        gmix, gmlp = S["norm_mix_g"][i:i + 1], S["norm_mlp_g"][i:i + 1]
        do2, dg2 = _resid_bwd(dx, st["o2"], g2 + tok, f"l{i}_b_res2")
        da = _mm(do2, W["mlp_w2"], tb=True, epi=lambda acc, zt: (acc * (2.0 * jnp.sqrt(zt.astype(F32))),), extras=((st["z"], "mn"),), name=f"l{i}_b_dz")
        gW["mlp_w2"] = _mm(st["z"], do2, ta=True, name=f"l{i}_b_dw2")
        dh2 = _mm(da, W["mlp_w1"], tb=True, out_dtypes=(F32,), name=f"l{i}_b_dh2")
        gW["mlp_w1"] = _mm(st["h2"], da, ta=True, name=f"l{i}_b_dw1")
        dx2, dgmlp, dsc2, dsh2 = _norm_mod_bwd(st["x2"], dh2, dx, gmlp, sc2, f"l{i}_b_norm2")
        gS["norm_mlp_g"][i] = dgmlp
        dy, q1 = _resid_bwd(dx2, st["y"], g1, f"l{i}_b_res1")
        if kind == 0:
            dh, dpw, dpsc, dg1 = _pool_bwd(dy, st["pooled"], W["pool_w"], S["pool_scale"][j:j + 1], g1, q1, f"l{i}_b_pool")
            gW["pool_w"] = dpw.astype(BF16)
            gS["pool_scale"][j] = dpsc
        elif kind == 1:
            dg1 = q1
            dgated = _mm(dy, W["sgu_w_out"], tb=True, name=f"l{i}_b_dgated")
            gW["sgu_w_out"] = _mm(st["gated"], dy, ta=True, name=f"l{i}_b_dwout")
            dzz, dws, dbs, dlg, dlb = _sgu_gate_bwd(st["zz"], dgated, S["sgu_ln_g"], S["sgu_ln_b"], S["sgu_w_s"], st["bs_t"], f"l{i}_b_sgu_gate")
            gS.update(sgu_w_s=dws, sgu_b_s=dbs[:, :, 0], sgu_ln_g=dlg, sgu_ln_b=dlb)
            dh = _mm(dzz, W["sgu_w_in"], tb=True, out_dtypes=(F32,), name=f"l{i}_b_dh_sgu")
            gW["sgu_w_in"] = _mm(st["h"], dzz, ta=True, name=f"l{i}_b_dwin")
        else:
            dg1 = q1
            do = _mm(dy, W["mla_w_o"], tb=True, name=f"l{i}_b_do")
            gW["mla_w_o"] = _mm(st["o"], dy, ta=True, name=f"l{i}_b_dwo")
            delta = _attn_delta(do, st["o"], f"l{i}_b_delta")
            dqt, dk, dv = _attn_bwd(st["q"], st["k"], st["kt"], st["v"], do, st["lse"], delta, f"l{i}_b_attn")
            dqpad, dkv, dkrot = _mla_prep_bwd(dqt, dk, dv, cc, sa, sb, f"l{i}_b_mla_prep")
            dcqn = _mm(dqpad, W["mla_w_uq"], tb=True, out_dtypes=(F32,), name=f"l{i}_b_dcq")
            gW["mla_w_uq"] = _mm(st["cqn"], dqpad, ta=True, name=f"l{i}_b_dwuq")
            dckvn = _mm(dkv, W["mla_w_ukv"], tb=True, out_dtypes=(F32,), name=f"l{i}_b_dckv")
            gW["mla_w_ukv"] = _mm(st["ckvn"], dkv, ta=True, name=f"l{i}_b_dwukv")
            dlat, dqg, dkvg = _mla_lat_bwd(st["lat"], dcqn, dckvn, dkrot, S["mla_q_norm_g"], S["mla_kv_norm_g"], cc, sa, sb, f"l{i}_b_mla_latn")
            gS.update(mla_q_norm_g=dqg, mla_kv_norm_g=dkvg)
            dh = _mm(dlat, W["mla_w_dq_dkv"], tb=True, out_dtypes=(F32,), name=f"l{i}_b_dh_mla")
            gW["mla_w_dq_dkv"] = _mm(st["h"], dlat, ta=True, name=f"l{i}_b_dwdq")
        dx, dgmix, dsc1, dsh1 = _norm_mod_bwd(st["x"], dh, dx2, gmix, sc1, f"l{i}_b_norm1")
        gS["norm_mix_g"][i] = dgmix
        dmod[i] = jnp.concatenate([dsh1, dsc1, dg1, dsh2, dsc2, dg2], axis=1)
        tok = grads_of(i, gW, dx)

    for n in ("norm_mix_g", "norm_mlp_g", "pool_scale"):
        gS[n] = jnp.concatenate(gS[n], axis=0)
    return loss, dx, gS, jnp.concatenate(dmod, axis=0)


_SMALL = {
    "norm_mix_g": (DEPTH, D_MODEL), "norm_mlp_g": (DEPTH, D_MODEL), "sgu_ln_g": (1, SGU_W), "sgu_ln_b": (1, SGU_W),
    "sgu_w_s": (SGU_H, SGU_CHUNK, SGU_CHUNK), "sgu_b_s": (SGU_H, SGU_CHUNK), "mla_kv_norm_g": (1, MLA_KVL), "final_g": (1, D_MODEL),
    "pool_scale": (2, D_MODEL), "mla_q_norm_g": (1, MLA_QL), "dmod": (DEPTH, 6 * D_MODEL),
}
_PACK_W = 1024


def _pack(vals):
    flat = jnp.concatenate([v.reshape(-1) for v in vals])
    rows = -(-flat.shape[0] // (8 * _PACK_W)) * 8
    return jnp.pad(flat, (0, rows * _PACK_W - flat.shape[0])).reshape(rows, _PACK_W)


def _unpack(buf, shapes):
    flat, out, off = buf.reshape(-1), [], 0
    for s in shapes:
        n = math.prod(s)
        out.append(flat[off:off + n].reshape(s))
        off += n
    return out


def kernel(x, c, positions, ada_w, ada_b, norm_mix_g, norm_mlp_g, pool_w, pool_scale, sgu_w_in, sgu_ln_g, sgu_ln_b, sgu_w_s, sgu_b_s, sgu_w_out, mla_w_dq_dkv, mla_q_norm_g, mla_kv_norm_g, mla_w_uq, mla_w_ukv, mla_w_o, mlp_w1, mlp_w2, final_g, loss_target, m_ada_w, m_ada_b, m_norm_mix_g, m_norm_mlp_g, m_pool_w, m_pool_scale, m_sgu_w_in, m_sgu_ln_g, m_sgu_ln_b, m_sgu_w_s, m_sgu_b_s, m_sgu_w_out, m_mla_w_dq_dkv, m_mla_q_norm_g, m_mla_kv_norm_g, m_mla_w_uq, m_mla_w_ukv, m_mla_w_o, m_mlp_w1, m_mlp_w2, m_final_g, v_ada_w, v_ada_b, v_norm_mix_g, v_norm_mlp_g, v_pool_w, v_pool_scale, v_sgu_w_in, v_sgu_ln_g, v_sgu_ln_b, v_sgu_w_s, v_sgu_b_s, v_sgu_w_out, v_mla_w_dq_dkv, v_mla_q_norm_g, v_mla_kv_norm_g, v_mla_w_uq, v_mla_w_ukv, v_mla_w_o, v_mlp_w1, v_mlp_w2, v_final_g):
    P = dict(ada_w=ada_w, ada_b=ada_b, norm_mix_g=norm_mix_g, norm_mlp_g=norm_mlp_g, pool_w=pool_w, pool_scale=pool_scale, sgu_w_in=sgu_w_in,
             sgu_ln_g=sgu_ln_g, sgu_ln_b=sgu_ln_b, sgu_w_s=sgu_w_s, sgu_b_s=sgu_b_s, sgu_w_out=sgu_w_out, mla_w_dq_dkv=mla_w_dq_dkv,
             mla_q_norm_g=mla_q_norm_g, mla_kv_norm_g=mla_kv_norm_g, mla_w_uq=mla_w_uq, mla_w_ukv=mla_w_ukv, mla_w_o=mla_w_o, mlp_w1=mlp_w1,
             mlp_w2=mlp_w2, final_g=final_g)
    M = dict(ada_w=m_ada_w, ada_b=m_ada_b, norm_mix_g=m_norm_mix_g, norm_mlp_g=m_norm_mlp_g, pool_w=m_pool_w, pool_scale=m_pool_scale,
             sgu_w_in=m_sgu_w_in, sgu_ln_g=m_sgu_ln_g, sgu_ln_b=m_sgu_ln_b, sgu_w_s=m_sgu_w_s, sgu_b_s=m_sgu_b_s, sgu_w_out=m_sgu_w_out,
             mla_w_dq_dkv=m_mla_w_dq_dkv, mla_q_norm_g=m_mla_q_norm_g, mla_kv_norm_g=m_mla_kv_norm_g, mla_w_uq=m_mla_w_uq, mla_w_ukv=m_mla_w_ukv,
             mla_w_o=m_mla_w_o, mlp_w1=m_mlp_w1, mlp_w2=m_mlp_w2, final_g=m_final_g)
    V = dict(ada_w=v_ada_w, ada_b=v_ada_b, norm_mix_g=v_norm_mix_g, norm_mlp_g=v_norm_mlp_g, pool_w=v_pool_w, pool_scale=v_pool_scale,
             sgu_w_in=v_sgu_w_in, sgu_ln_g=v_sgu_ln_g, sgu_ln_b=v_sgu_ln_b, sgu_w_s=v_sgu_w_s, sgu_b_s=v_sgu_b_s, sgu_w_out=v_sgu_w_out,
             mla_w_dq_dkv=v_mla_w_dq_dkv, mla_q_norm_g=v_mla_q_norm_g, mla_kv_norm_g=v_mla_kv_norm_g, mla_w_uq=v_mla_w_uq, mla_w_ukv=v_mla_w_ukv,
             mla_w_o=v_mla_w_o, mlp_w1=v_mlp_w1, mlp_w2=v_mlp_w2, final_g=v_final_g)
    order = list(P)
    xi, yi, ci = _idx()
    chip = 2 * xi + yi
    D = D_MODEL
    n_ada = ada_w.shape[2]

    pre = _allgather8(_pack([c, pool_scale, mla_q_norm_g]), "ag_small")
    flat = pre.reshape(N_DEV, -1)
    c_all = flat[:, :D]
    ps_all = flat[0::2, D:D + 2 * (D // N_CHIPS)].reshape(N_CHIPS, 2, D // N_CHIPS).transpose(1, 0, 2).reshape(2, D)
    q0 = D + 2 * (D // N_CHIPS)
    qg_all = flat[0::2, q0:q0 + MLA_QL // N_CHIPS].reshape(1, MLA_QL)

    ada_b_loc = lax.dynamic_slice_in_dim(ada_b, chip * n_ada, n_ada, axis=1)[:, None, :]
    modp = _ada_fwd(c_all, ada_w, ada_b_loc, "ada_fwd")
    mod = _mod_exchange(modp.transpose(1, 0, 2), "mod_exchange").transpose(1, 0, 2).reshape(DEPTH, 6 * D)

    S = dict(norm_mix_g=norm_mix_g, norm_mlp_g=norm_mlp_g, pool_scale=ps_all, sgu_ln_g=sgu_ln_g, sgu_ln_b=sgu_ln_b, sgu_w_s=sgu_w_s[0],
             sgu_b_s=sgu_b_s[0], mla_q_norm_g=qg_all, mla_kv_norm_g=mla_kv_norm_g, final_g=final_g[None, :])
    cidx, chipidx = jnp.reshape(ci, (1,)).astype(jnp.int32), jnp.reshape(chip, (1,)).astype(jnp.int32)
    view2d = lambda a: a.reshape(-1, a.shape[-1])

    def piece_rows(kind, blk):
        r = _PIECE_KINDS[kind][0]
        return blk * r, r

    groups = [_layer_pieces(0)[:-2], _layer_pieces(0)[-2:]] + [_layer_pieces(i) for i in range(1, DEPTH)]
    gathers = {}

    def gather_start(g, dep):
        srcs, shapes = [], []
        for kind, blk in groups[g]:
            r0, r = piece_rows(kind, blk)
            cdim = _PIECE_KINDS[kind][1]
            srcs.append(view2d(P[kind])[r0:r0 + r].astype(BF16).reshape(2, r // 2, cdim))
            shapes.append(jax.ShapeDtypeStruct((N_CHIPS, 2, r // 2, cdim), BF16))
        gathers[g] = _xchip_start("gather", srcs, shapes, dep, f"ag_start_g{g}")

    def gather_finish(g, after):
        ssem, rsem, srcs, lands, _ = gathers.pop(g)
        srcs, lands = _xchip_wait("gather", ssem, rsem, srcs, lands, [after], f"ag_wait_g{g}")
        lands = _sibling_fwd(lands, f"ag_sibling_g{g}")
        tok = jnp.zeros((), F32)
        if g + 2 < len(groups):
            gather_start(g + 2, lands[0])
            tok = gathers[g + 2][-1][0, 0]
        W = {"_tok": tok}
        for (kind, _), s, land in zip(groups[g], srcs, lands, strict=True):
            r, cdim, to_full, _ = _PIECE_KINDS[kind]
            W[kind] = to_full(lax.dynamic_update_index_in_dim(land, s, chip, 0).reshape(N_CHIPS, r, cdim))
        return W

    def weights_of(i, part, x_i):
        if i == 0:
            return gather_finish(0 if part == "mix" else 1, x_i)
        return gather_finish(i + 1, x_i) if part == "mix" else {"_tok": jnp.zeros((), F32)}

    scatters = {}
    bufs = {n: tuple(lax.empty(view2d(P[n]).shape, F32) for _ in range(4)) for n in _PIECE_KINDS}

    def scatter_start(i, gW, dep):
        pcs = _layer_pieces(i)
        blocked = []
        for kind, _ in pcs:
            r, cdim, _, to_blocks = _PIECE_KINDS[kind]
            blocked.append(to_blocks(gW[kind]).reshape(N_CHIPS, 2, r // 2, cdim).transpose(1, 0, 2, 3))
        from_sib = _sibling_swap(blocked, f"rs_sibling_l{i}")
        pair, shapes = [], []
        for (kind, _), b, f in zip(pcs, blocked, from_sib, strict=True):
            _, _, hr, cdim = b.shape
            p = _sum_sel(cidx, b.reshape(2, N_CHIPS * hr, cdim), [f.reshape(1, N_CHIPS * hr, cdim)], f"rs_pair_l{i}_{kind}", BF16)
            pair.append(p.reshape(N_CHIPS, hr, cdim))
            shapes.append(jax.ShapeDtypeStruct((N_CHIPS - 1, hr, cdim), BF16))
        scatters[i] = (pcs, *_xchip_start("scatter", pair, shapes, dep, f"rs_start_l{i}"))
        return scatters[i][-1][0, 0]

    def scatter_finish(i, after):
        pcs, ssem, rsem, pair, lands, _ = scatters.pop(i)
        pair, lands = _xchip_wait("scatter", ssem, rsem, pair, lands, after, f"rs_wait_l{i}")
        halves = [_sum_sel(chipidx, p, [l], f"rs_sum_l{i}_{kind}", F32) for (kind, _), p, l in zip(pcs, pair, lands, strict=True)]
        got = _sibling_send(halves, f"rs_merge_l{i}")
        for (kind, blk), mine, other in zip(pcs, halves, got, strict=True):
            r0, _ = piece_rows(kind, blk)
            bufs[kind] = tuple(_adamw_piece(cidx, view2d(P[kind]), view2d(M[kind]), view2d(V[kind]), mine, other, bufs[kind], r0,
                                            f"adamw_l{i}_{kind}"))
        return lands[0]

    first_layer = {}

    def grads_of(i, gW, dx_i):
        dep = scatter_finish(i + 1, [dx_i]) if i + 1 in scatters else dx_i
        if i == 0:
            first_layer.update(gW)
            return jnp.zeros((), F32)
        return scatter_start(i, gW, dep)

    gather_start(0, mod)
    gather_start(1, mod)
    loss_l, dx, gS, dmod = _local_step(x[0], positions[0], loss_target[0], mod, S, weights_of, grads_of)
    loss = lax.psum(loss_l[0, 0], ("x", "y", "c"))

    gS["dmod"] = dmod
    small = _allgather8(_pack([gS[n] for n in _SMALL]), "ag_small_grads")
    small = small + scatter_start(0, first_layer, small)
    small_sum = _unpack(_sum_lead([small], "sum_small_grads"), list(_SMALL.values()))
    G = dict(zip(_SMALL, small_sum, strict=True))
    grads = {
        "ada_b": G["dmod"], "norm_mix_g": G["norm_mix_g"], "norm_mlp_g": G["norm_mlp_g"], "sgu_ln_g": G["sgu_ln_g"], "sgu_ln_b": G["sgu_ln_b"],
        "sgu_w_s": G["sgu_w_s"][None], "sgu_b_s": G["sgu_b_s"][None], "mla_kv_norm_g": G["mla_kv_norm_g"], "final_g": G["final_g"][0],
        "pool_scale": lax.dynamic_slice_in_dim(G["pool_scale"], chip * (D // N_CHIPS), D // N_CHIPS, axis=1),
        "mla_q_norm_g": lax.dynamic_slice_in_dim(G["mla_q_norm_g"], chip * (MLA_QL // N_CHIPS), MLA_QL // N_CHIPS, axis=1),
    }
    dmod_all = _unpack(small, [(N_DEV,) + (small.shape[1] * _PACK_W,)])[0]
    off = sum(math.prod(s) for n, s in _SMALL.items() if n != "dmod")
    dmod_all = dmod_all[:, off:off + DEPTH * 6 * D].reshape(N_DEV, DEPTH, 6 * D)
    dmod_loc = lax.dynamic_slice_in_dim(dmod_all, chip * n_ada, n_ada, axis=2).transpose(1, 0, 2)
    grads["ada_w"] = _ada_bwd(c_all.T, dmod_loc, "ada_bwd")

    deltas, new_m, new_v = {}, {}, {}
    for n in order:
        if n not in _PIECE_KINDS:
            deltas[n], new_m[n], new_v[n] = _adamw(P[n], grads[n].reshape(P[n].shape), M[n], V[n], f"adamw_{n}")
    scatter_finish(0, [deltas["ada_w"], deltas["sgu_w_s"]] + [bufs[n][0] for n in ("mlp_w1", "mlp_w2", "sgu_w_in", "mla_w_o")])
    for n in _PIECE_KINDS:
        grads[n], deltas[n], new_m[n], new_v[n] = (b.reshape(P[n].shape) for b in bufs[n])
    return (loss, dx[None], *[grads[n].reshape(P[n].shape) for n in order], *[deltas[n] for n in order], *[new_m[n] for n in order],
            *[new_v[n] for n in order])
```

```python
import math

import jax
import jax.numpy as jnp
from jax import lax
from jax.experimental import pallas as pl
from jax.experimental.pallas import tpu as pltpu

F32, BF16 = jnp.float32, jnp.bfloat16
MESH = pl.DeviceIdType.MESH

D_MODEL = 1024
DEPTH = 4
N_MIXERS = 3
POOL_WINDOWS = (2, 4, 8, 16)
POOL_GD = D_MODEL // len(POOL_WINDOWS)
POOL_HALO = 16
SGU_CHUNK = 128
SGU_W = D_MODEL
SGU_HD = 128
SGU_H = SGU_W // SGU_HD
MLA_H = 16
MLA_QL = 256
MLA_KVL = 128
MLA_NOPE = 128
MLA_ROPE = 64
MLA_V = 128
MLA_HP = 256
MLA_LATP = 512
ROPE_THETA = 10000.0
RMS_EPS = 1e-6
LN_EPS = 1e-5
SM_SCALE = (MLA_NOPE + MLA_ROPE) ** -0.5
NEG_INF = -1e30
ADAM_LR, ADAM_B1, ADAM_B2, ADAM_EPS, ADAM_WD, ADAM_STEP = 0.001, 0.9, 0.999, 1e-08, 0.01, 10
N_CHIPS = 4
N_DEV = 8
ROW_TILE = 512
ATT_TILE = 512
ATT_SUB = 256
MM_VMEM_BUDGET = 40 << 20


def _idx():
    return lax.axis_index("x"), lax.axis_index("y"), lax.axis_index("c")


def _mm(a, b, *, name, ta=False, tb=False, epi=None, extras=(), out_dtypes=(BF16,), tm=1024, tn=1024, tk=1024):
    if ta:
        K, M = a.shape
    else:
        M, K = a.shape
    if tb:
        N, Kb = b.shape
    else:
        Kb, N = b.shape
    assert K == Kb, (a.shape, b.shape, ta, tb)
    tm, tn, tk = min(tm, M), min(tn, N), min(tk, K)

    def vmem_bytes(tm_, tk_):
        per_mn = sum(arr.dtype.itemsize for arr, kind in extras if kind == "mn") + sum(jnp.dtype(dt).itemsize for dt in out_dtypes)
        return 2 * (tm_ * tk_ * a.dtype.itemsize + tk_ * tn * b.dtype.itemsize + tm_ * tn * per_mn)

    if vmem_bytes(tm, K) <= MM_VMEM_BUDGET:
        tk = K
    elif tm >= 512 and vmem_bytes(tm // 2, K) <= MM_VMEM_BUDGET:
        tm, tk = tm // 2, K
    assert M % tm == 0 and N % tn == 0 and K % tk == 0, (M, N, K, tm, tn, tk)
    nk = K // tk
    a_spec = pl.BlockSpec((tk, tm), lambda i, j, k: (k, i)) if ta else pl.BlockSpec((tm, tk), lambda i, j, k: (i, k))
    b_spec = pl.BlockSpec((tn, tk), lambda i, j, k: (j, k)) if tb else pl.BlockSpec((tk, tn), lambda i, j, k: (k, j))
    ex_specs = []
    for arr, kind in extras:
        if kind == "mn":
            ex_specs.append(pl.BlockSpec((tm, tn), lambda i, j, k: (i, j)))
        elif kind == "n":
            ex_specs.append(pl.BlockSpec((1, tn), lambda i, j, k: (0, j)))
        else:
            ex_specs.append(pl.BlockSpec((tm, arr.shape[1]), lambda i, j, k: (i, 0)))
    n_ex, n_out = len(extras), len(out_dtypes)
    dims = (((0 if ta else 1,), (1 if tb else 0,)), ((), ()))

    def body(*refs):
        a_ref, b_ref = refs[0], refs[1]
        ex_refs = refs[2:2 + n_ex]
        out_refs = refs[2 + n_ex:2 + n_ex + n_out]
        part = lax.dot_general(a_ref[...].astype(BF16), b_ref[...].astype(BF16), dims, preferred_element_type=F32)

        def finish(acc):
            outs = epi(acc, *[r[...] for r in ex_refs]) if epi is not None else (acc,)
            for r, o in zip(out_refs, outs, strict=True):
                r[...] = o.astype(r.dtype)

        if nk == 1:
            finish(part)
        else:
            acc_ref = refs[-1]
            k = pl.program_id(2)

            @pl.when(k == 0)
            def _():
                acc_ref[...] = part

            @pl.when(k > 0)
            def _():
                acc_ref[...] += part

            @pl.when(k == nk - 1)
            def _():
                finish(acc_ref[...])

    outs = pl.pallas_call(
        body,
        name=name,
        grid=(M // tm, N // tn, nk),
        in_specs=[a_spec, b_spec, *ex_specs],
        out_specs=[pl.BlockSpec((tm, tn), lambda i, j, k: (i, j)) for _ in range(n_out)],
        out_shape=[jax.ShapeDtypeStruct((M, N), dt) for dt in out_dtypes],
        scratch_shapes=[pltpu.VMEM((tm, tn), F32)] if nk > 1 else [],
        compiler_params=pltpu.CompilerParams(dimension_semantics=("parallel", "parallel", "arbitrary")),
    )(a, b, *[arr for arr, _ in extras])
    return outs[0] if n_out == 1 else tuple(outs)


def _epi_residual(acc, x, g):
    return x + g * acc, acc


def _row_spec(tr, d):
    return pl.BlockSpec((tr, d), lambda i: (i, 0))


def _vec_spec(d):
    return pl.BlockSpec((1, d), lambda i: (0, 0))


def _colsum(v):
    return jnp.sum(v, axis=0, keepdims=True)


def _norm_mod_fwd(x, gain, sc, sh, out_dtype, name):
    T, D = x.shape
    tr = min(T, ROW_TILE)

    def body(x_ref, g_ref, sc_ref, sh_ref, o_ref):
        xv = x_ref[...]
        r = lax.rsqrt(jnp.mean(xv * xv, axis=-1, keepdims=True) + RMS_EPS)
        o_ref[...] = (((xv * r) * g_ref[...]) * (1.0 + sc_ref[...]) + sh_ref[...]).astype(o_ref.dtype)

    return pl.pallas_call(
        body, name=name, grid=(T // tr,),
        in_specs=[_row_spec(tr, D), _vec_spec(D), _vec_spec(D), _vec_spec(D)],
        out_specs=_row_spec(tr, D),
        out_shape=jax.ShapeDtypeStruct((T, D), out_dtype),
        compiler_params=pltpu.CompilerParams(dimension_semantics=("parallel",)),
    )(x, gain, sc, sh)


def _norm_mod_bwd(x, dh, dres, gain, sc, name):
    T, D = x.shape
    tr = min(T, ROW_TILE)

    def body(x_ref, dh_ref, dres_ref, g_ref, sc_ref, dx_ref, dg_ref, dsc_ref, dsh_ref):
        @pl.when(pl.program_id(0) == 0)
        def _():
            dg_ref[...] = jnp.zeros_like(dg_ref)
            dsc_ref[...] = jnp.zeros_like(dsc_ref)
            dsh_ref[...] = jnp.zeros_like(dsh_ref)

        xv = x_ref[...]
        r = lax.rsqrt(jnp.mean(xv * xv, axis=-1, keepdims=True) + RMS_EPS)
        xn = xv * r
        dhv = dh_ref[...].astype(F32)
        dsh_ref[...] += _colsum(dhv)
        dsc_ref[...] += _colsum(dhv * (xn * g_ref[...]))
        dt = dhv * (1.0 + sc_ref[...])
        dg_ref[...] += _colsum(dt * xn)
        dxn = dt * g_ref[...]
        dx_ref[...] = dres_ref[...] + r * (dxn - xn * jnp.mean(dxn * xn, axis=-1, keepdims=True))

    return pl.pallas_call(
        body, name=name, grid=(T // tr,),
        in_specs=[_row_spec(tr, D), _row_spec(tr, D), _row_spec(tr, D), _vec_spec(D), _vec_spec(D)],
        out_specs=[_row_spec(tr, D), _vec_spec(D), _vec_spec(D), _vec_spec(D)],
        out_shape=[jax.ShapeDtypeStruct((T, D), F32)] + [jax.ShapeDtypeStruct((1, D), F32)] * 3,
        compiler_params=pltpu.CompilerParams(dimension_semantics=("arbitrary",)),
    )(x, dh, dres, gain, sc)


def _resid_bwd(dx, y, g, name):
    T, D = dx.shape
    tr = min(T, ROW_TILE)

    def body(dx_ref, y_ref, g_ref, dy_ref, q_ref):
        @pl.when(pl.program_id(0) == 0)
        def _():
            q_ref[...] = jnp.zeros_like(q_ref)

        dxv = dx_ref[...]
        dy_ref[...] = (g_ref[...] * dxv).astype(BF16)
        q_ref[...] += _colsum(dxv * y_ref[...].astype(F32))

    return pl.pallas_call(
        body, name=name, grid=(T // tr,),
        in_specs=[_row_spec(tr, D), _row_spec(tr, D), _vec_spec(D)],
        out_specs=[_row_spec(tr, D), _vec_spec(D)],
        out_shape=[jax.ShapeDtypeStruct((T, D), BF16), jax.ShapeDtypeStruct((1, D), F32)],
        compiler_params=pltpu.CompilerParams(dimension_semantics=("arbitrary",)),
    )(dx, y, g)


def _loss_head(x, target, gain, name):
    T, D = x.shape
    tr = min(T, ROW_TILE)

    def body(x_ref, t_ref, g_ref, loss_ref, dx_ref, dg_ref):
        @pl.when(pl.program_id(0) == 0)
        def _():
            loss_ref[...] = jnp.zeros_like(loss_ref)
            dg_ref[...] = jnp.zeros_like(dg_ref)

        xv = x_ref[...]
        r = lax.rsqrt(jnp.mean(xv * xv, axis=-1, keepdims=True) + RMS_EPS)
        xn = xv * r
        err = xn * g_ref[...] - t_ref[...]
        row = jnp.mean(err * err, axis=-1, keepdims=True)
        loss_ref[...] += 0.5 * jnp.sum(row, axis=0, keepdims=True)
        dy = err * (1.0 / D)
        dg_ref[...] += _colsum(dy * xn)
        dxn = dy * g_ref[...]
        dx_ref[...] = r * (dxn - xn * jnp.mean(dxn * xn, axis=-1, keepdims=True))

    return pl.pallas_call(
        body, name=name, grid=(T // tr,),
        in_specs=[_row_spec(tr, D), _row_spec(tr, D), _vec_spec(D)],
        out_specs=[_vec_spec(128), _row_spec(tr, D), _vec_spec(D)],
        out_shape=[jax.ShapeDtypeStruct((1, 128), F32), jax.ShapeDtypeStruct((T, D), F32), jax.ShapeDtypeStruct((1, D), F32)],
        compiler_params=pltpu.CompilerParams(dimension_semantics=("arbitrary",)),
    )(x, target, gain)


def _pool_fwd(h, w, scale, x, g1, name):
    T, D = h.shape
    tr = min(T, ROW_TILE)

    def body(h_ref, w_ref, sc_ref, x_ref, g_ref, x2_ref, pooled_ref, ypre_ref, halo_ref):
        i = pl.program_id(0)

        @pl.when(i == 0)
        def _():
            halo_ref[...] = jnp.zeros_like(halo_ref)

        hv = h_ref[...]
        buf = jnp.concatenate([halo_ref[...], hv], axis=0)
        halo_ref[...] = hv[tr - POOL_HALO:, :]
        t = (i * tr + lax.broadcasted_iota(jnp.int32, (tr, 1), 0)).astype(F32)
        for gi, win in enumerate(POOL_WINDOWS):
            cols = slice(gi * POOL_GD, (gi + 1) * POOL_GD)
            val = buf[:, cols]
            sh = 1
            while sh < win:
                val = val + pltpu.roll(val, sh, axis=0)
                sh *= 2
            pooled = val[POOL_HALO:, :] / jnp.minimum(t + 1.0, float(win)) - hv[:, cols]
            pb = pooled.astype(BF16)
            pooled_ref[:, cols] = pb
            yp = jnp.dot(pb, w_ref[gi], preferred_element_type=F32)
            ypre_ref[:, cols] = yp.astype(BF16)
            x2_ref[:, cols] = x_ref[:, cols] + g_ref[:, cols] * (yp * sc_ref[:, cols])

    return pl.pallas_call(
        body, name=name, grid=(T // tr,),
        in_specs=[_row_spec(tr, D), pl.BlockSpec(w.shape, lambda i: (0, 0, 0)), _vec_spec(D), _row_spec(tr, D), _vec_spec(D)],
        out_specs=[_row_spec(tr, D)] * 3,
        out_shape=[jax.ShapeDtypeStruct((T, D), F32), jax.ShapeDtypeStruct((T, D), BF16), jax.ShapeDtypeStruct((T, D), BF16)],
        scratch_shapes=[pltpu.VMEM((POOL_HALO, D), F32)],
        compiler_params=pltpu.CompilerParams(dimension_semantics=("arbitrary",)),
    )(h, w, scale, x, g1)


def _pool_bwd(dy, pooled, w, scale, g1, q, name):
    T, D = dy.shape
    tr = min(T, ROW_TILE)
    nt = T // tr
    ltot = tr + POOL_HALO

    def body(dy_ref, pooled_ref, w_ref, sc_ref, g_ref, q_ref, dh_ref, dw_ref, dsc_ref, dg_ref, halo_ref):
        i = pl.program_id(0)

        @pl.when(i == 0)
        def _():
            halo_ref[...] = jnp.zeros_like(halo_ref)
            dw_ref[...] = jnp.zeros_like(dw_ref)
            dsc_ref[...] = g_ref[...] * q_ref[...]
            dg_ref[...] = sc_ref[...] * q_ref[...]

        t = ((nt - 1 - i) * tr + lax.broadcasted_iota(jnp.int32, (tr, 1), 0)).astype(F32)
        for gi, win in enumerate(POOL_WINDOWS):
            cols = slice(gi * POOL_GD, (gi + 1) * POOL_GD)
            dyb = (dy_ref[:, cols].astype(F32) * sc_ref[:, cols]).astype(BF16)
            dw_ref[gi] += lax.dot_general(pooled_ref[:, cols], dyb, (((0,), (0,)), ((), ())), preferred_element_type=F32)
            dpool = lax.dot_general(dyb, w_ref[gi], (((1,), (1,)), ((), ())), preferred_element_type=F32)
            qv = dpool / jnp.minimum(t + 1.0, float(win))
            val = jnp.concatenate([qv, halo_ref[:, cols]], axis=0)
            halo_ref[:, cols] = qv[:POOL_HALO, :]
            sh = 1
            while sh < win:
                val = val + pltpu.roll(val, ltot - sh, axis=0)
                sh *= 2
            dh_ref[:, cols] = val[:tr, :] - dpool

    rev = pl.BlockSpec((tr, D), lambda i: (nt - 1 - i, 0))
    return pl.pallas_call(
        body, name=name, grid=(nt,),
        in_specs=[rev, rev, pl.BlockSpec(w.shape, lambda i: (0, 0, 0)), _vec_spec(D), _vec_spec(D), _vec_spec(D)],
        out_specs=[rev, pl.BlockSpec(w.shape, lambda i: (0, 0, 0)), _vec_spec(D), _vec_spec(D)],
        out_shape=[jax.ShapeDtypeStruct((T, D), F32), jax.ShapeDtypeStruct(w.shape, F32),
                   jax.ShapeDtypeStruct((1, D), F32), jax.ShapeDtypeStruct((1, D), F32)],
        scratch_shapes=[pltpu.VMEM((POOL_HALO, D), F32)],
        compiler_params=pltpu.CompilerParams(dimension_semantics=("arbitrary",)),
    )(dy, pooled, w, scale, g1, q)


_INV_SQRT2 = 0.7071067811865476
_INV_SQRT2PI = 0.3989422804014327


def _gelu(v):
    return 0.5 * v * (1.0 + lax.erf(v * _INV_SQRT2))


def _gelu_grad(v):
    return 0.5 * (1.0 + lax.erf(v * _INV_SQRT2)) + v * jnp.exp(-0.5 * v * v) * _INV_SQRT2PI


def _sgu_ln(v, g, b):
    mu = jnp.mean(v, axis=-1, keepdims=True)
    xc = v - mu
    rstd = lax.rsqrt(jnp.mean(xc * xc, axis=-1, keepdims=True) + LN_EPS)
    xh = xc * rstd
    return xh, rstd, xh * g + b


def _tril_mask():
    return lax.broadcasted_iota(jnp.int32, (SGU_CHUNK, SGU_CHUNK), 0) >= lax.broadcasted_iota(jnp.int32, (SGU_CHUNK, SGU_CHUNK), 1)


SGU_TILE = 256


def _sgu_gate_fwd(zz, ln_g, ln_b, ws, bs_t, name):
    T = zz.shape[0]
    ts = min(T, SGU_TILE)

    def body(zz_ref, g_ref, b_ref, ws_ref, bs_ref, out_ref):
        z = _gelu(zz_ref[...])
        u = z[:, :SGU_W]
        _, _, vn = _sgu_ln(z[:, SGU_W:], g_ref[...], b_ref[...])
        vb = vn.astype(BF16)
        tril = _tril_mask()
        for hh in range(SGU_H):
            wm = jnp.where(tril, ws_ref[hh], 0.0).astype(BF16)
            bcol = bs_ref[:, hh:hh + 1]
            cs = slice(hh * SGU_HD, (hh + 1) * SGU_HD)
            for j in range(ts // SGU_CHUNK):
                rs = slice(j * SGU_CHUNK, (j + 1) * SGU_CHUNK)
                mixed = jnp.dot(wm, vb[rs, cs], preferred_element_type=F32) + bcol
                out_ref[rs, cs] = (u[rs, cs] * mixed).astype(BF16)

    return pl.pallas_call(
        body, name=name, grid=(T // ts,),
        in_specs=[_row_spec(ts, 2 * SGU_W), _vec_spec(SGU_W), _vec_spec(SGU_W),
                  pl.BlockSpec(ws.shape, lambda i: (0, 0, 0)), pl.BlockSpec(bs_t.shape, lambda i: (0, 0))],
        out_specs=_row_spec(ts, SGU_W),
        out_shape=jax.ShapeDtypeStruct((T, SGU_W), BF16),
        compiler_params=pltpu.CompilerParams(dimension_semantics=("parallel",)),
    )(zz, ln_g, ln_b, ws, bs_t)


def _sgu_gate_bwd(zz, dgated, ln_g, ln_b, ws, bs_t, name):
    T = zz.shape[0]
    ts = min(T, SGU_TILE)
    nt = T // ts

    def body(zz_ref, dg_ref, g_ref, b_ref, ws_ref, bs_ref, dzz_ref, dws_ref, dbs_ref, dlg_ref, dlb_ref, dlo_ref, dmx_ref):
        i = pl.program_id(0)

        @pl.when(i == 0)
        def _():
            dws_ref[...] = jnp.zeros_like(dws_ref)
            dmx_ref[...] = jnp.zeros_like(dmx_ref)
            dlg_ref[...] = jnp.zeros_like(dlg_ref)
            dlb_ref[...] = jnp.zeros_like(dlb_ref)

        zzv = zz_ref[...]
        z = _gelu(zzv)
        u = z[:, :SGU_W]
        xh, rstd, vn = _sgu_ln(z[:, SGU_W:], g_ref[...], b_ref[...])
        vb = vn.astype(BF16)
        dgv = dg_ref[...].astype(F32)
        tril = _tril_mask()
        for hh in range(SGU_H):
            wm = jnp.where(tril, ws_ref[hh], 0.0).astype(BF16)
            bcol = bs_ref[:, hh:hh + 1]
            cs = slice(hh * SGU_HD, (hh + 1) * SGU_HD)
            for j in range(ts // SGU_CHUNK):
                rs = slice(j * SGU_CHUNK, (j + 1) * SGU_CHUNK)
                mixed = jnp.dot(wm, vb[rs, cs], preferred_element_type=F32) + bcol
                dmixed = dgv[rs, cs] * u[rs, cs]
                dzz_ref[rs, cs] = (dgv[rs, cs] * mixed * _gelu_grad(zzv[rs, cs])).astype(BF16)
                dmb = dmixed.astype(BF16)
                dws_ref[hh] += lax.dot_general(dmb, vb[rs, cs], (((1,), (1,)), ((), ())), preferred_element_type=F32)
                dmx_ref[hh] += dmixed
                dlo_ref[rs, cs] = lax.dot_general(wm, dmb, (((0,), (0,)), ((), ())), preferred_element_type=F32)
        dlo = dlo_ref[...]
        dlg_ref[...] += _colsum(dlo * xh)
        dlb_ref[...] += _colsum(dlo)
        dxh = dlo * g_ref[...]
        dv = rstd * (dxh - jnp.mean(dxh, axis=-1, keepdims=True) - xh * jnp.mean(dxh * xh, axis=-1, keepdims=True))
        dzz_ref[:, SGU_W:] = (dv * _gelu_grad(zzv[:, SGU_W:])).astype(BF16)

        @pl.when(i == nt - 1)
        def _():
            tril_f = tril.astype(F32)
            for hh in range(SGU_H):
                dws_ref[hh] = dws_ref[hh] * tril_f
                dbs_ref[hh] = jnp.broadcast_to(jnp.sum(dmx_ref[hh], axis=-1, keepdims=True), (SGU_CHUNK, SGU_HD))

    full3 = pl.BlockSpec(ws.shape, lambda i: (0, 0, 0))
    return pl.pallas_call(
        body, name=name, grid=(nt,),
        in_specs=[_row_spec(ts, 2 * SGU_W), _row_spec(ts, SGU_W), _vec_spec(SGU_W), _vec_spec(SGU_W), full3,
                  pl.BlockSpec(bs_t.shape, lambda i: (0, 0))],
        out_specs=[_row_spec(ts, 2 * SGU_W), full3, full3, _vec_spec(SGU_W), _vec_spec(SGU_W)],
        out_shape=[jax.ShapeDtypeStruct((T, 2 * SGU_W), BF16), jax.ShapeDtypeStruct(ws.shape, F32), jax.ShapeDtypeStruct(ws.shape, F32),
                   jax.ShapeDtypeStruct((1, SGU_W), F32), jax.ShapeDtypeStruct((1, SGU_W), F32)],
        scratch_shapes=[pltpu.VMEM((ts, SGU_W), F32), pltpu.VMEM(ws.shape, F32)],
        compiler_params=pltpu.CompilerParams(dimension_semantics=("arbitrary",)),
    )(zz, dgated, ln_g, ln_b, ws, bs_t)


def _rope_fwd(blk, cc, sa, sb):
    return blk * cc + pltpu.roll(blk, 96, axis=1) * sa + pltpu.roll(blk, 32, axis=1) * sb


def _rope_bwd(d, cc, sa, sb):
    return d * cc + pltpu.roll(d * sa, 32, axis=1) + pltpu.roll(d * sb, 96, axis=1)


def _rms(v, g):
    r = lax.rsqrt(jnp.mean(v * v, axis=-1, keepdims=True) + RMS_EPS)
    vn = v * r
    return vn, r, vn * g


def _rms_bwd(dy, vn, r, g):
    dvn = dy * g
    return r * (dvn - vn * jnp.mean(dvn * vn, axis=-1, keepdims=True))


MLA_TILE = 256
_KV0 = MLA_QL
_KR0 = MLA_QL + MLA_KVL


def _mla_lat_fwd(lat, qg, kvg, cc, sa, sb, name):
    T = lat.shape[0]
    tr = min(T, ROW_TILE)

    def body(lat_ref, qg_ref, kvg_ref, cc_ref, sa_ref, sb_ref, cq_ref, ckv_ref, kr_ref):
        lv = lat_ref[...]
        cq_ref[...] = _rms(lv[:, :_KV0], qg_ref[...])[2].astype(BF16)
        ckv_ref[...] = _rms(lv[:, _KV0:_KR0], kvg_ref[...])[2].astype(BF16)
        kr_ref[...] = _rope_fwd(lv[:, _KR0:], cc_ref[...], sa_ref[...], sb_ref[...])

    return pl.pallas_call(
        body, name=name, grid=(T // tr,),
        in_specs=[_row_spec(tr, MLA_LATP), _vec_spec(MLA_QL), _vec_spec(MLA_KVL), _row_spec(tr, 128), _row_spec(tr, 128), _row_spec(tr, 128)],
        out_specs=[_row_spec(tr, MLA_QL), _row_spec(tr, MLA_KVL), _row_spec(tr, 128)],
        out_shape=[jax.ShapeDtypeStruct((T, MLA_QL), BF16), jax.ShapeDtypeStruct((T, MLA_KVL), BF16), jax.ShapeDtypeStruct((T, 128), F32)],
        compiler_params=pltpu.CompilerParams(dimension_semantics=("parallel",)),
    )(lat, qg, kvg, cc, sa, sb)


def _mla_lat_bwd(lat, dcqn, dckvn, dkrot, qg, kvg, cc, sa, sb, name):
    T = lat.shape[0]
    tr = min(T, ROW_TILE)

    def body(lat_ref, dcq_ref, dckv_ref, dkr_ref, qg_ref, kvg_ref, cc_ref, sa_ref, sb_ref, dlat_ref, dqg_ref, dkvg_ref):
        @pl.when(pl.program_id(0) == 0)
        def _():
            dqg_ref[...] = jnp.zeros_like(dqg_ref)
            dkvg_ref[...] = jnp.zeros_like(dkvg_ref)

        lv = lat_ref[...]
        qn, qr, _ = _rms(lv[:, :_KV0], qg_ref[...])
        kn, kr, _ = _rms(lv[:, _KV0:_KR0], kvg_ref[...])
        dcq = dcq_ref[...]
        dckv = dckv_ref[...]
        dqg_ref[...] += _colsum(dcq * qn)
        dkvg_ref[...] += _colsum(dckv * kn)
        dlat_ref[:, :_KV0] = _rms_bwd(dcq, qn, qr, qg_ref[...]).astype(BF16)
        dlat_ref[:, _KV0:_KR0] = _rms_bwd(dckv, kn, kr, kvg_ref[...]).astype(BF16)
        dlat_ref[:, _KR0:] = _rope_bwd(dkr_ref[...], cc_ref[...], sa_ref[...], sb_ref[...]).astype(BF16)

    return pl.pallas_call(
        body, name=name, grid=(T // tr,),
        in_specs=[_row_spec(tr, MLA_LATP), _row_spec(tr, MLA_QL), _row_spec(tr, MLA_KVL), _row_spec(tr, 128),
                  _vec_spec(MLA_QL), _vec_spec(MLA_KVL), _row_spec(tr, 128), _row_spec(tr, 128), _row_spec(tr, 128)],
        out_specs=[_row_spec(tr, MLA_LATP), _vec_spec(MLA_QL), _vec_spec(MLA_KVL)],
        out_shape=[jax.ShapeDtypeStruct((T, MLA_LATP), BF16), jax.ShapeDtypeStruct((1, MLA_QL), F32), jax.ShapeDtypeStruct((1, MLA_KVL), F32)],
        compiler_params=pltpu.CompilerParams(dimension_semantics=("arbitrary",)),
    )(lat, dcqn, dckvn, dkrot, qg, kvg, cc, sa, sb)


def _mla_prep(qpad, kv, krot, cc, sa, sb, name):
    T = qpad.shape[0]
    tr = min(T, MLA_TILE)
    HW = MLA_H * MLA_HP

    def body(q_ref, kv_ref, kr_ref, cc_ref, sa_ref, sb_ref, qo_ref, ko_ref, kt_ref, vo_ref, vt_ref):
        cc, sa, sb = cc_ref[...], sa_ref[...], sb_ref[...]
        kr = kr_ref[...]
        krb, krt = kr.astype(BF16), kr.T.astype(BF16)
        for hh in range(MLA_H):
            a, m, b = hh * MLA_HP, hh * MLA_HP + MLA_NOPE, (hh + 1) * MLA_HP
            qo_ref[:, a:m] = (q_ref[:, a:m] * SM_SCALE).astype(BF16)
            qo_ref[:, m:b] = (_rope_fwd(q_ref[:, m:b], cc, sa, sb) * SM_SCALE).astype(BF16)
            kn = kv_ref[:, a:m]
            ko_ref[:, a:m] = kn.astype(BF16)
            ko_ref[:, m:b] = krb
            kt_ref[a:m, :] = kn.T.astype(BF16)
            kt_ref[m:b, :] = krt
            vh = kv_ref[:, m:b]
            vo_ref[:, hh * MLA_V:(hh + 1) * MLA_V] = vh.astype(BF16)
            vt_ref[hh] = vh.T.astype(BF16)

    tk = min(T, ATT_TILE)
    per = tk // tr
    return pl.pallas_call(
        body, name=name, grid=(T // tr,),
        in_specs=[_row_spec(tr, HW), _row_spec(tr, HW), _row_spec(tr, 128), _row_spec(tr, 128), _row_spec(tr, 128), _row_spec(tr, 128)],
        out_specs=[_row_spec(tr, HW), _row_spec(tr, HW), pl.BlockSpec((HW, tr), lambda i: (0, i)), _row_spec(tr, MLA_H * MLA_V),
                   pl.BlockSpec((MLA_H, None, MLA_V, tr), lambda i: (0, i // per, 0, i % per))],
        out_shape=[jax.ShapeDtypeStruct((T, HW), BF16), jax.ShapeDtypeStruct((T, HW), BF16), jax.ShapeDtypeStruct((HW, T), BF16),
                   jax.ShapeDtypeStruct((T, MLA_H * MLA_V), BF16), jax.ShapeDtypeStruct((MLA_H, T // tk, MLA_V, tk), BF16)],
        compiler_params=pltpu.CompilerParams(dimension_semantics=("parallel",)),
    )(qpad, kv, krot, cc, sa, sb)


ATT_HG = 4


def _mla_prep_bwd(dqt, dk, dv, cc, sa, sb, name):
    _, nq, _, tq = dqt.shape
    T = nq * tq
    gw = ATT_HG * MLA_HP

    def body(dq_ref, dk_ref, dv_ref, cc_ref, sa_ref, sb_ref, dqp_ref, dkv_ref, dkr_ref):
        @pl.when(pl.program_id(1) == 0)
        def _():
            dkr_ref[...] = jnp.zeros_like(dkr_ref)

        cc, sa, sb = cc_ref[...], sa_ref[...], sb_ref[...]
        acc = jnp.zeros((tq, 128), F32)
        for hh in range(ATT_HG):
            a, m, b = hh * MLA_HP, hh * MLA_HP + MLA_NOPE, (hh + 1) * MLA_HP
            dqh = dq_ref[hh].astype(F32).T * SM_SCALE
            dqp_ref[:, a:m] = dqh[:, :MLA_NOPE].astype(BF16)
            dqp_ref[:, m:b] = _rope_bwd(dqh[:, MLA_NOPE:], cc, sa, sb).astype(BF16)
            dkv_ref[:, a:m] = dk_ref[:, a:m]
            dkv_ref[:, m:b] = dv_ref[:, hh * MLA_V:(hh + 1) * MLA_V]
            acc = acc + dk_ref[:, m:b].astype(F32)
        dkr_ref[...] += acc

    tab = pl.BlockSpec((tq, 128), lambda i, g: (i, 0))
    return pl.pallas_call(
        body, name=name, grid=(nq, MLA_H // ATT_HG),
        in_specs=[pl.BlockSpec((ATT_HG, None, MLA_HP, tq), lambda i, g: (g, i, 0, 0)), pl.BlockSpec((tq, gw), lambda i, g: (i, g)),
                  pl.BlockSpec((tq, ATT_HG * MLA_V), lambda i, g: (i, g)), tab, tab, tab],
        out_specs=[pl.BlockSpec((tq, gw), lambda i, g: (i, g)), pl.BlockSpec((tq, gw), lambda i, g: (i, g)), tab],
        out_shape=[jax.ShapeDtypeStruct((T, MLA_H * MLA_HP), BF16), jax.ShapeDtypeStruct((T, MLA_H * MLA_HP), BF16), jax.ShapeDtypeStruct((T, 128), F32)],
        compiler_params=pltpu.CompilerParams(dimension_semantics=("parallel", "arbitrary")),
    )(dqt, dk, dv, cc, sa, sb)


_NT = (((1,), (1,)), ((), ()))


def _as_row(col, n):
    return jnp.broadcast_to(col, (n, 128)).T[0:1, :]


def _attn_fwd(q, k, vt, name):
    T = q.shape[0]
    tq = tk = min(T, ATT_TILE)
    nq = T // tq

    def body(q_ref, k_ref, vt_ref, o_ref, lse_ref, m_ref, l_ref, acc_ref):
        i = pl.program_id(1)
        qv = q_ref[...]
        m_ref[...] = jnp.full_like(m_ref, NEG_INF)
        l_ref[...] = jnp.zeros_like(l_ref)
        acc_ref[...] = jnp.zeros_like(acc_ref)

        def step(j, diag):
            off = pl.multiple_of(j * tk, tk)
            st = lax.dot_general(k_ref[pl.ds(off, tk), :], qv, _NT, preferred_element_type=F32)
            if diag:
                st = jnp.where(lax.broadcasted_iota(jnp.int32, (tk, tq), 0) <= lax.broadcasted_iota(jnp.int32, (tk, tq), 1), st, NEG_INF)
            m_prev = m_ref[...]
            m_new = jnp.maximum(m_prev, jnp.max(st, axis=0, keepdims=True))
            alpha = jnp.exp(m_prev - m_new)
            pt = jnp.exp(st - m_new)
            l_ref[...] = alpha * l_ref[...] + jnp.sum(pt, axis=0, keepdims=True)
            acc_ref[...] = alpha * acc_ref[...] + jnp.dot(vt_ref[j], pt.astype(BF16), preferred_element_type=F32)
            m_ref[...] = m_new

        def loop_body(j, carry):
            step(j, False)
            return carry

        lax.fori_loop(0, i, loop_body, 0)
        step(i, True)
        o_ref[...] = (acc_ref[...] / l_ref[...]).T.astype(BF16)
        lse_ref[...] = m_ref[...] + jnp.log(l_ref[...])

    return pl.pallas_call(
        body, name=name, grid=(MLA_H, nq),
        in_specs=[pl.BlockSpec((tq, MLA_HP), lambda h, i: (i, h)), pl.BlockSpec((T, MLA_HP), lambda h, i: (0, h)),
                  pl.BlockSpec((None, nq, MLA_V, tk), lambda h, i: (h, 0, 0, 0))],
        out_specs=[pl.BlockSpec((tq, MLA_V), lambda h, i: (i, h)), pl.BlockSpec((None, None, 1, tq), lambda h, i: (h, i, 0, 0))],
        out_shape=[jax.ShapeDtypeStruct((T, MLA_H * MLA_V), BF16), jax.ShapeDtypeStruct((MLA_H, nq, 1, tq), F32)],
        scratch_shapes=[pltpu.VMEM((1, tq), F32), pltpu.VMEM((1, tq), F32), pltpu.VMEM((MLA_V, tq), F32)],
        compiler_params=pltpu.CompilerParams(dimension_semantics=("parallel", "arbitrary")),
    )(q, k, vt)


def _attn_delta(do, o, name):
    T = do.shape[0]
    tq = min(T, ATT_TILE)

    def body(do_ref, o_ref, d_ref):
        for hh in range(MLA_H):
            cs = slice(hh * MLA_V, (hh + 1) * MLA_V)
            s = jnp.sum(do_ref[:, cs].astype(F32) * o_ref[:, cs].astype(F32), axis=-1, keepdims=True)
            d_ref[hh] = _as_row(s, tq)

    return pl.pallas_call(
        body, name=name, grid=(T // tq,),
        in_specs=[_row_spec(tq, MLA_H * MLA_V), _row_spec(tq, MLA_H * MLA_V)],
        out_specs=pl.BlockSpec((MLA_H, None, 1, tq), lambda i: (0, i, 0, 0)),
        out_shape=jax.ShapeDtypeStruct((MLA_H, T // tq, 1, tq), F32),
        compiler_params=pltpu.CompilerParams(dimension_semantics=("parallel",)),
    )(do, o)


def _attn_bwd(q, k, kt, v, do, lse, delta, name):
    T = q.shape[0]
    tq = tk = min(T, ATT_TILE)
    nq = nk = T // tq
    tsd = min(tq, ATT_SUB)

    def body(q_ref, k_ref, kt_ref, v_ref, do_ref, lse_ref, dl_ref, dqt_ref, dk_ref, dv_ref, dq_acc, dk_acc, dv_acc):
        j = pl.program_id(1)

        @pl.when(j == 0)
        def _():
            dq_acc[...] = jnp.zeros_like(dq_acc)

        dk_acc[...] = jnp.zeros_like(dk_acc)
        dv_acc[...] = jnp.zeros_like(dv_acc)

        def step(i, diag):
            off = pl.multiple_of(i * tq, tq)
            lse_i, dl_i = lse_ref[i], dl_ref[i]
            ts, nsub = (tsd, tq // tsd) if diag else (tq, 1)
            for u in range(nsub):
                cols = slice(u * ts, (u + 1) * ts)
                nk_u = (u + 1) * ts if diag else tk
                qi, doi = q_ref[pl.ds(off + u * ts, ts), :], do_ref[pl.ds(off + u * ts, ts), :]
                st = lax.dot_general(k_ref[:nk_u, :], qi, _NT, preferred_element_type=F32)
                if diag:
                    qcol = u * ts + lax.broadcasted_iota(jnp.int32, (nk_u, ts), 1)
                    st = jnp.where(lax.broadcasted_iota(jnp.int32, (nk_u, ts), 0) <= qcol, st, NEG_INF)
                pt = jnp.exp(st - lse_i[:, cols])
                dv_acc[:nk_u, :] += jnp.dot(pt.astype(BF16), doi, preferred_element_type=F32)
                dpt = lax.dot_general(v_ref[:nk_u, :], doi, _NT, preferred_element_type=F32)
                dsb = (pt * (dpt - dl_i[:, cols])).astype(BF16)
                dk_acc[:nk_u, :] += jnp.dot(dsb, qi, preferred_element_type=F32)
                dq_acc[i, :, cols] += jnp.dot(kt_ref[:, :nk_u], dsb, preferred_element_type=F32)

        def loop_body(i, carry):
            step(i, False)
            return carry

        step(j, True)
        lax.fori_loop(j + 1, nq, loop_body, 0)
        dk_ref[...] = dk_acc[...].astype(BF16)
        dv_ref[...] = dv_acc[...].astype(BF16)

        @pl.when(j == nk - 1)
        def _():
            dqt_ref[...] = dq_acc[...].astype(BF16)

    stat = pl.BlockSpec((None, nq, 1, tq), lambda h, j: (h, 0, 0, 0))
    return pl.pallas_call(
        body, name=name, grid=(MLA_H, nk),
        in_specs=[pl.BlockSpec((T, MLA_HP), lambda h, j: (0, h)), pl.BlockSpec((tk, MLA_HP), lambda h, j: (j, h)),
                  pl.BlockSpec((MLA_HP, tk), lambda h, j: (h, j)), pl.BlockSpec((tk, MLA_V), lambda h, j: (j, h)),
                  pl.BlockSpec((T, MLA_V), lambda h, j: (0, h)), stat, stat],
        out_specs=[pl.BlockSpec((None, nq, MLA_HP, tq), lambda h, j: (h, 0, 0, 0)), pl.BlockSpec((tk, MLA_HP), lambda h, j: (j, h)),
                   pl.BlockSpec((tk, MLA_V), lambda h, j: (j, h))],
        out_shape=[jax.ShapeDtypeStruct((MLA_H, nq, MLA_HP, tq), BF16), jax.ShapeDtypeStruct((T, MLA_H * MLA_HP), BF16),
                   jax.ShapeDtypeStruct((T, MLA_H * MLA_V), BF16)],
        scratch_shapes=[pltpu.VMEM((nq, MLA_HP, tq), F32), pltpu.VMEM((tk, MLA_HP), F32), pltpu.VMEM((tk, MLA_V), F32)],
        compiler_params=pltpu.CompilerParams(dimension_semantics=("parallel", "arbitrary")),
    )(q, k, kt, v, do, lse, delta)


ADA_TN = 512


def _silu(v):
    return v * (1.0 / (1.0 + jnp.exp(-v)))


def _ada_fwd(c_all, ada_w, ada_b_loc, name):
    L, D, Nc = ada_w.shape
    B = c_all.shape[0]

    def body(c_ref, w_ref, b_ref, o_ref):
        ca = _silu(c_ref[...]).astype(BF16)
        o_ref[...] = jnp.dot(ca, w_ref[...].astype(BF16), preferred_element_type=F32) + b_ref[...]

    return pl.pallas_call(
        body, name=name, grid=(L, Nc // ADA_TN),
        in_specs=[pl.BlockSpec((B, D), lambda l, n: (0, 0)), pl.BlockSpec((None, D, ADA_TN), lambda l, n: (l, 0, n)),
                  pl.BlockSpec((None, 1, ADA_TN), lambda l, n: (l, 0, n))],
        out_specs=pl.BlockSpec((None, B, ADA_TN), lambda l, n: (l, 0, n)),
        out_shape=jax.ShapeDtypeStruct((L, B, Nc), F32),
        compiler_params=pltpu.CompilerParams(dimension_semantics=("parallel", "parallel")),
    )(c_all, ada_w, ada_b_loc)


def _ada_bwd(c_all_t, dmod_loc, name):
    D, B = c_all_t.shape
    L, _, Nc = dmod_loc.shape

    def body(c_ref, d_ref, o_ref):
        ca = _silu(c_ref[...])
        dv = d_ref[...]
        acc = ca[:, 0:1] * dv[0:1, :]
        for b in range(1, B):
            acc = acc + ca[:, b:b + 1] * dv[b:b + 1, :]
        o_ref[...] = acc

    return pl.pallas_call(
        body, name=name, grid=(L, Nc // ADA_TN),
        in_specs=[pl.BlockSpec((D, B), lambda l, n: (0, 0)), pl.BlockSpec((None, B, ADA_TN), lambda l, n: (l, 0, n))],
        out_specs=pl.BlockSpec((None, D, ADA_TN), lambda l, n: (l, 0, n)),
        out_shape=jax.ShapeDtypeStruct((L, D, Nc), F32),
        compiler_params=pltpu.CompilerParams(dimension_semantics=("parallel", "parallel")),
    )(c_all_t, dmod_loc)


def _sum_lead(parts, name, out_dtype=F32):
    R, C = parts[0].shape[1:]
    n_tot = sum(p.shape[0] for p in parts)
    tr = R
    for cand in (512, 256, 128, 64, 32, 16):
        if R % cand == 0 and cand * C * 4 * n_tot <= (8 << 20):
            tr = cand
            break

    def body(*refs):
        o_ref = refs[-1]
        acc = None
        for r in refs[:-1]:
            for s in range(r.shape[0]):
                acc = r[s].astype(F32) if acc is None else acc + r[s].astype(F32)
        o_ref[...] = acc.astype(o_ref.dtype)

    return pl.pallas_call(
        body, name=name, grid=(R // tr,),
        in_specs=[pl.BlockSpec((p.shape[0], tr, C), lambda i: (0, i, 0)) for p in parts],
        out_specs=pl.BlockSpec((tr, C), lambda i: (i, 0)),
        out_shape=jax.ShapeDtypeStruct((R, C), out_dtype),
        compiler_params=pltpu.CompilerParams(dimension_semantics=("parallel",)),
    )(*parts)


_ADAM_C1 = 1.0 - ADAM_B1 ** ADAM_STEP
_ADAM_C2 = 1.0 - ADAM_B2 ** ADAM_STEP


def _adamw(w, g, m, v, name):
    shape = w.shape
    C = shape[-1]
    R = math.prod(shape[:-1]) if len(shape) > 1 else 1
    w2, g2, m2, v2 = (a.reshape(R, C) for a in (w, g, m, v))
    tr = R
    for cand in (1024, 512, 256, 128, 64, 32, 16, 8):
        if R % cand == 0 and cand * C * 4 <= (1 << 20):
            tr = cand
            break

    def body(w_ref, g_ref, m_ref, v_ref, d_ref, nm_ref, nv_ref):
        gv = g_ref[...]
        mn = ADAM_B1 * m_ref[...] + (1.0 - ADAM_B1) * gv
        vn = ADAM_B2 * v_ref[...] + (1.0 - ADAM_B2) * (gv * gv)
        nm_ref[...] = mn
        nv_ref[...] = vn
        m_hat = mn / _ADAM_C1
        v_hat = vn / _ADAM_C2
        d_ref[...] = -ADAM_LR * (m_hat / (jnp.sqrt(v_hat) + ADAM_EPS) + ADAM_WD * w_ref[...])

    spec = pl.BlockSpec((tr, C), lambda i: (i, 0))
    outs = pl.pallas_call(
        body, name=name, grid=(R // tr,),
        in_specs=[spec] * 4, out_specs=[spec] * 3,
        out_shape=[jax.ShapeDtypeStruct((R, C), F32)] * 3,
        compiler_params=pltpu.CompilerParams(dimension_semantics=("parallel",)),
    )(w2, g2, m2, v2)
    return tuple(o.reshape(shape) for o in outs)


def _row_tile(rows, cols, itemsize, budget):
    for cand in (1024, 512, 256, 128, 64, 32, 16):
        if rows % cand == 0 and cand * cols * itemsize <= budget:
            return cand
    return rows


def _sum_sel(sel, stacked, others, name, out_dtype):
    R, C = stacked.shape[1:]
    n_tot = 1 + sum(o.shape[0] for o in others)
    tr = _row_tile(R, C, 4 * n_tot, 8 << 20)

    def body(sel_ref, s_ref, *refs):
        o_ref = refs[-1]
        acc = s_ref[...].astype(F32)
        for r in refs[:-1]:
            for s in range(r.shape[0]):
                acc = acc + r[s].astype(F32)
        o_ref[...] = acc.astype(o_ref.dtype)

    return pl.pallas_call(
        body, name=name,
        grid_spec=pltpu.PrefetchScalarGridSpec(
            num_scalar_prefetch=1, grid=(R // tr,),
            in_specs=[pl.BlockSpec((None, tr, C), lambda i, s: (s[0], i, 0))] + [pl.BlockSpec((o.shape[0], tr, C), lambda i, s: (0, i, 0)) for o in others],
            out_specs=pl.BlockSpec((tr, C), lambda i, s: (i, 0))),
        out_shape=jax.ShapeDtypeStruct((R, C), out_dtype),
        compiler_params=pltpu.CompilerParams(dimension_semantics=("parallel",)),
    )(sel, stacked, *others)


def _adamw_piece(cidx, w2, m2, v2, mine, got, bufs, row0, name):
    hr, C = mine.shape
    tr = _row_tile(math.gcd(hr, row0) if row0 else hr, C, 4, 1 << 20)
    nt = hr // tr

    def body(c_ref, w_ref, m_ref, v_ref, a_ref, b_ref, _g, _d, _nm, _nv, g_ref, d_ref, nm_ref, nv_ref):
        gv = jnp.where(pl.program_id(0) == c_ref[0], a_ref[...], b_ref[...])
        mn = ADAM_B1 * m_ref[...] + (1.0 - ADAM_B1) * gv
        vn = ADAM_B2 * v_ref[...] + (1.0 - ADAM_B2) * (gv * gv)
        g_ref[...] = gv
        nm_ref[...] = mn
        nv_ref[...] = vn
        d_ref[...] = -ADAM_LR * ((mn / _ADAM_C1) / (jnp.sqrt(vn / _ADAM_C2) + ADAM_EPS) + ADAM_WD * w_ref[...])

    rows = pl.BlockSpec((tr, C), lambda hf, t, c: (row0 // tr + hf * nt + t, 0))
    half = pl.BlockSpec((tr, C), lambda hf, t, c: (t, 0))
    return pl.pallas_call(
        body, name=name,
        grid_spec=pltpu.PrefetchScalarGridSpec(num_scalar_prefetch=1, grid=(2, nt), in_specs=[rows] * 3 + [half] * 2 + [_ANY_SPEC] * 4,
                                               out_specs=[rows] * 4),
        out_shape=[jax.ShapeDtypeStruct(w2.shape, F32)] * 4,
        input_output_aliases={6 + n: n for n in range(4)},
        compiler_params=pltpu.CompilerParams(dimension_semantics=("parallel", "parallel")),
    )(cidx, w2, m2, v2, mine, got, *bufs)


_VMEM_SPEC = pl.BlockSpec(memory_space=pltpu.VMEM)
_HBM_SPEC = pl.BlockSpec(memory_space=pltpu.HBM)


def _flip(v, bit):
    return (1 - v) if bit else v


def _allgather8(v, name):
    def body(v_ref, out_ref, send_sems, recv_sems, local_sem):
        x, y, c = _idx()
        me = 4 * x + 2 * y + c
        mine = pltpu.make_async_copy(v_ref, out_ref.at[me], local_sem)
        mine.start()
        sends = []
        for k in range(1, N_DEV):
            peer = (_flip(x, k & 4), _flip(y, k & 2), _flip(c, k & 1))
            cp = pltpu.make_async_remote_copy(src_ref=v_ref, dst_ref=out_ref.at[me], send_sem=send_sems.at[k - 1], recv_sem=recv_sems.at[k - 1],
                                              device_id=peer, device_id_type=MESH)
            cp.start()
            sends.append(cp)
        for k in range(1, N_DEV):
            px, py, pc = _flip(x, k & 4), _flip(y, k & 2), _flip(c, k & 1)
            src = 4 * px + 2 * py + pc
            pltpu.make_async_remote_copy(src_ref=v_ref, dst_ref=out_ref.at[src], send_sem=send_sems.at[k - 1], recv_sem=recv_sems.at[k - 1],
                                         device_id=(px, py, pc), device_id_type=MESH).wait_recv()
        for cp in sends:
            cp.wait_send()
        mine.wait()

    return pl.pallas_call(
        body, name=name,
        out_shape=jax.ShapeDtypeStruct((N_DEV, *v.shape), v.dtype),
        in_specs=[_VMEM_SPEC], out_specs=_VMEM_SPEC,
        scratch_shapes=[pltpu.SemaphoreType.DMA((N_DEV - 1,)), pltpu.SemaphoreType.DMA((N_DEV - 1,)), pltpu.SemaphoreType.DMA],
    )(v)


def _mod_exchange(modp, name):
    _, L, Nc = modp.shape

    def body(p_ref, out_ref, send_sems, recv_sems, local_sem):
        x, y, c = _idx()
        me, chip = 4 * x + 2 * y + c, 2 * x + y
        mine = pltpu.make_async_copy(p_ref.at[me], out_ref.at[chip], local_sem)
        mine.start()
        sends = []
        for k in range(1, N_CHIPS):
            px, py = _flip(x, k & 2), _flip(y, k & 1)
            cp = pltpu.make_async_remote_copy(src_ref=p_ref.at[4 * px + 2 * py + c], dst_ref=out_ref.at[chip],
                                              send_sem=send_sems.at[k - 1], recv_sem=recv_sems.at[k - 1], device_id=(px, py, c), device_id_type=MESH)
            cp.start()
            sends.append(cp)
        for k in range(1, N_CHIPS):
            px, py = _flip(x, k & 2), _flip(y, k & 1)
            pltpu.make_async_remote_copy(src_ref=p_ref.at[me], dst_ref=out_ref.at[2 * px + py], send_sem=send_sems.at[k - 1],
                                         recv_sem=recv_sems.at[k - 1], device_id=(px, py, c), device_id_type=MESH).wait_recv()
        for cp in sends:
            cp.wait_send()
        mine.wait()

    return pl.pallas_call(
        body, name=name,
        out_shape=jax.ShapeDtypeStruct((N_CHIPS, L, Nc), modp.dtype),
        in_specs=[_VMEM_SPEC], out_specs=_VMEM_SPEC,
        scratch_shapes=[pltpu.SemaphoreType.DMA((N_CHIPS - 1,)), pltpu.SemaphoreType.DMA((N_CHIPS - 1,)), pltpu.SemaphoreType.DMA],
    )(modp)


_SEM_SPEC = pl.BlockSpec(memory_space=pltpu.SEMAPHORE)
_ANY_SPEC = pl.BlockSpec(memory_space=pl.ANY)
_EFFECT = pltpu.SideEffectType.DATAFLOW_SIDE_EFFECTING


def _hbm(a):
    return pltpu.with_memory_space_constraint(a, pltpu.HBM)


def _xchip_copies(mode, srcs, lands, send_sems, recv_sems, waiting):
    x, y, c = _idx()
    chip = 2 * x + y
    out = []
    for a in range(len(srcs)):
        for k in range(1, N_CHIPS):
            px, py = _flip(x, k & 2), _flip(y, k & 1)
            peer = 2 * px + py
            if mode == "gather":
                src, dst, mine = srcs[a].at[c], lands[a].at[chip, c], lands[a].at[peer, c]
            else:
                src, dst, mine = srcs[a].at[peer], lands[a].at[k - 1], lands[a].at[k - 1]
            q = a * (N_CHIPS - 1) + k - 1
            out.append(pltpu.make_async_remote_copy(src_ref=src, dst_ref=mine if waiting else dst, send_sem=send_sems[q], recv_sem=recv_sems[q],
                                                    device_id=(px, py, c), device_id_type=MESH))
    return out


def _xchip_start(mode, srcs, land_shapes, dep, name):
    n = len(srcs)
    ns = n * (N_CHIPS - 1)

    def body(*refs):
        src_refs, land_refs = refs[:n], refs[n:2 * n]
        outs = refs[2 * n + 1:]
        for cp in _xchip_copies(mode, src_refs, land_refs, outs[:ns], outs[ns:2 * ns], waiting=False):
            cp.start()
        outs[-1][...] = jnp.zeros_like(outs[-1])

    lands = [_hbm(lax.empty(s.shape, s.dtype)) for s in land_shapes]
    outs = pl.pallas_call(
        body, name=name,
        out_shape=(*[pltpu.SemaphoreType.DMA(())] * (2 * ns), *[pltpu.HBM(s.shape, s.dtype) for s in srcs],
                   *[pltpu.HBM(s.shape, s.dtype) for s in land_shapes], jax.ShapeDtypeStruct((8, 128), F32)),
        in_specs=[_HBM_SPEC] * (2 * n) + [_ANY_SPEC],
        out_specs=(*[_SEM_SPEC] * (2 * ns), *[_HBM_SPEC] * (2 * n), _VMEM_SPEC),
        input_output_aliases={i: 2 * ns + i for i in range(2 * n)},
        compiler_params=pltpu.CompilerParams(has_side_effects=_EFFECT),
    )(*[_hbm(s) for s in srcs], *lands, dep)
    return list(outs[:ns]), list(outs[ns:2 * ns]), list(outs[2 * ns:2 * ns + n]), list(outs[2 * ns + n:2 * ns + 2 * n]), outs[-1]


def _xchip_wait(mode, send_sems, recv_sems, srcs, lands, after, name):
    n = len(srcs)
    ns = n * (N_CHIPS - 1)

    def body(*refs):
        src_refs, land_refs = refs[:n], refs[n:2 * n]
        sems = refs[2 * n:2 * n + 2 * ns]
        for cp in _xchip_copies(mode, src_refs, land_refs, sems[:ns], sems[ns:], waiting=True):
            cp.wait_send()
            cp.wait_recv()

    outs = pl.pallas_call(
        body, name=name,
        out_shape=(*[pltpu.HBM(s.shape, s.dtype) for s in srcs], *[pltpu.HBM(s.shape, s.dtype) for s in lands]),
        in_specs=[_HBM_SPEC] * (2 * n) + [_SEM_SPEC] * (2 * ns) + [_ANY_SPEC] * len(after),
        out_specs=tuple([_HBM_SPEC] * (2 * n)),
        input_output_aliases={i: i for i in range(2 * n)},
        compiler_params=pltpu.CompilerParams(has_side_effects=_EFFECT),
    )(*srcs, *lands, *send_sems, *recv_sems, *after)
    return list(outs[:n]), list(outs[n:])


def _sibling_fwd(lands, name):
    n = len(lands)

    def body(*refs):
        outs = refs[n:2 * n]
        send_sems, recv_sems = refs[2 * n:]
        x, y, c = _idx()
        sib = (x, y, 1 - c)
        sends = []
        for a in range(n):
            for k in range(1, N_CHIPS):
                src = 2 * _flip(x, k & 2) + _flip(y, k & 1)
                cp = pltpu.make_async_remote_copy(src_ref=outs[a].at[src, c], dst_ref=outs[a].at[src, c], send_sem=send_sems.at[a, k - 1],
                                                  recv_sem=recv_sems.at[a, k - 1], device_id=sib, device_id_type=MESH)
                cp.start()
                sends.append(cp)
        for a in range(n):
            for k in range(1, N_CHIPS):
                src = 2 * _flip(x, k & 2) + _flip(y, k & 1)
                pltpu.make_async_remote_copy(src_ref=outs[a].at[src, c], dst_ref=outs[a].at[src, 1 - c], send_sem=send_sems.at[a, k - 1],
                                             recv_sem=recv_sems.at[a, k - 1], device_id=sib, device_id_type=MESH).wait_recv()
        for cp in sends:
            cp.wait_send()

    return pl.pallas_call(
        body, name=name,
        out_shape=[jax.ShapeDtypeStruct(s.shape, s.dtype) for s in lands],
        in_specs=[_HBM_SPEC] * n, out_specs=[_HBM_SPEC] * n,
        input_output_aliases={i: i for i in range(n)},
        scratch_shapes=[pltpu.SemaphoreType.DMA((n, N_CHIPS - 1)), pltpu.SemaphoreType.DMA((n, N_CHIPS - 1))],
    )(*lands)


def _sibling_swap(parts, name):
    n = len(parts)

    def body(*refs):
        ins, outs = refs[:n], refs[n:2 * n]
        send_sems, recv_sems = refs[2 * n:]
        x, y, c = _idx()
        cps = []
        for a in range(n):
            cp = pltpu.make_async_remote_copy(src_ref=ins[a].at[1 - c], dst_ref=outs[a], send_sem=send_sems.at[a], recv_sem=recv_sems.at[a],
                                              device_id=(x, y, 1 - c), device_id_type=MESH)
            cp.start()
            cps.append(cp)
        for cp in cps:
            cp.wait()

    return pl.pallas_call(
        body, name=name,
        out_shape=[jax.ShapeDtypeStruct(p.shape[1:], p.dtype) for p in parts],
        in_specs=[_HBM_SPEC] * n, out_specs=[_HBM_SPEC] * n,
        scratch_shapes=[pltpu.SemaphoreType.DMA((n,)), pltpu.SemaphoreType.DMA((n,))],
    )(*parts)


def _sibling_send(halves, name):
    n = len(halves)

    def body(*refs):
        ins, outs = refs[:n], refs[n:2 * n]
        send_sems, recv_sems = refs[2 * n:]
        x, y, c = _idx()
        cps = []
        for a in range(n):
            cp = pltpu.make_async_remote_copy(src_ref=ins[a], dst_ref=outs[a], send_sem=send_sems.at[a], recv_sem=recv_sems.at[a],
                                              device_id=(x, y, 1 - c), device_id_type=MESH)
            cp.start()
            cps.append(cp)
        for cp in cps:
            cp.wait()

    return pl.pallas_call(
        body, name=name,
        out_shape=[jax.ShapeDtypeStruct(h.shape, h.dtype) for h in halves],
        in_specs=[_HBM_SPEC] * n, out_specs=[_HBM_SPEC] * n,
        scratch_shapes=[pltpu.SemaphoreType.DMA((n,)), pltpu.SemaphoreType.DMA((n,))],
    )(*halves)


def _col_full(g):
    k, n = g.shape[1], g.shape[2]
    return g.transpose(1, 0, 2).reshape(k, N_CHIPS * n)


def _col_blocks(w):
    k, n = w.shape
    return w.reshape(k, N_CHIPS, n // N_CHIPS).transpose(1, 0, 2)


def _row_blocks(w):
    k, n = w.shape
    return w.reshape(N_CHIPS, k // N_CHIPS, n)


_UQ_HEAD = MLA_NOPE + MLA_ROPE

_LAT = MLA_QL + MLA_KVL + MLA_ROPE
_POOL_R = len(POOL_WINDOWS) * (POOL_GD // N_CHIPS)

_PIECE_KINDS = {
    "mlp_w1": (D_MODEL, D_MODEL, _col_full, _col_blocks),
    "mlp_w2": (D_MODEL, D_MODEL, lambda g: g.reshape(4 * D_MODEL, D_MODEL), _row_blocks),
    "pool_w": (_POOL_R, POOL_GD,
               lambda g: g.reshape(N_CHIPS, len(POOL_WINDOWS), POOL_GD // N_CHIPS, POOL_GD).transpose(1, 0, 2, 3).reshape(len(POOL_WINDOWS), POOL_GD, POOL_GD),
               lambda w: w.reshape(len(POOL_WINDOWS), N_CHIPS, POOL_GD // N_CHIPS, POOL_GD).transpose(1, 0, 2, 3).reshape(N_CHIPS, _POOL_R, POOL_GD)),
    "sgu_w_in": (D_MODEL, 2 * SGU_W // N_CHIPS, _col_full, _col_blocks),
    "sgu_w_out": (SGU_W // N_CHIPS, D_MODEL, lambda g: g.reshape(SGU_W, D_MODEL), _row_blocks),
    "mla_w_dq_dkv": (D_MODEL // N_CHIPS, _LAT, lambda g: jnp.pad(g.reshape(D_MODEL, _LAT), ((0, 0), (0, MLA_LATP - _LAT))),
                     lambda w: _row_blocks(w[:, :_LAT])),
    "mla_w_uq": (MLA_QL, MLA_H * _UQ_HEAD // N_CHIPS,
                 lambda g: jnp.pad(_col_full(g).reshape(MLA_QL, MLA_H, _UQ_HEAD), ((0, 0), (0, 0), (0, MLA_HP - _UQ_HEAD))).reshape(MLA_QL, MLA_H * MLA_HP),
                 lambda w: _col_blocks(w.reshape(MLA_QL, MLA_H, MLA_HP)[:, :, :_UQ_HEAD].reshape(MLA_QL, MLA_H * _UQ_HEAD))),
    "mla_w_ukv": (MLA_KVL, MLA_H * (MLA_NOPE + MLA_V) // N_CHIPS, _col_full, _col_blocks),
    "mla_w_o": (MLA_H * MLA_V // N_CHIPS, D_MODEL, lambda g: g.reshape(MLA_H * MLA_V, D_MODEL), _row_blocks),
}
_MIXER_KINDS = (("pool_w",), ("sgu_w_in", "sgu_w_out"), ("mla_w_dq_dkv", "mla_w_uq", "mla_w_ukv", "mla_w_o"))


def _layer_pieces(i):
    return [(k, i // N_MIXERS) for k in _MIXER_KINDS[i % N_MIXERS]] + [("mlp_w1", i), ("mlp_w2", i)]


def _rope_tables(positions):
    inv_freq = ROPE_THETA ** (-jnp.arange(0, MLA_ROPE, 2, dtype=F32) / MLA_ROPE)
    ang = positions.astype(F32)[:, None] * inv_freq
    cos, sin = jnp.cos(ang), jnp.sin(ang)
    z32, z64 = jnp.zeros_like(cos), jnp.zeros((positions.shape[0], 64), F32)
    return (jnp.concatenate([cos, cos, z64], axis=1), jnp.concatenate([-sin, z32, z64], axis=1), jnp.concatenate([z32, sin, z64], axis=1))


def _local_step(x, positions, target, mod, S, weights_of, grads_of):
    D = D_MODEL
    cc, sa, sb = _rope_tables(positions)
    saved = []
    for i in range(DEPTH):
        sh1, sc1, g1, sh2, sc2, g2 = (mod[i:i + 1, n * D:(n + 1) * D] for n in range(6))
        kind, j = i % N_MIXERS, i // N_MIXERS
        gmix, gmlp = S["norm_mix_g"][i:i + 1], S["norm_mlp_g"][i:i + 1]
        W = weights_of(i, "mix", x)
        gmix = gmix + W["_tok"]
        st = {"x": x}
        if kind == 0:
            h = _norm_mod_fwd(x, gmix, sc1, sh1, F32, f"l{i}_norm1")
            x2, pooled, ypre = _pool_fwd(h, W["pool_w"], S["pool_scale"][j:j + 1], x, g1, f"l{i}_pool")
            st.update(pooled=pooled, y=ypre)
        elif kind == 1:
            h = _norm_mod_fwd(x, gmix, sc1, sh1, BF16, f"l{i}_norm1")
            zz = _mm(h, W["sgu_w_in"], out_dtypes=(F32,), name=f"l{i}_sgu_in")
            bs_t = S["sgu_b_s"].T
            gated = _sgu_gate_fwd(zz, S["sgu_ln_g"], S["sgu_ln_b"], S["sgu_w_s"], bs_t, f"l{i}_sgu_gate")
            x2, y = _mm(gated, W["sgu_w_out"], epi=_epi_residual, extras=((x, "mn"), (g1, "n")), out_dtypes=(F32, BF16), name=f"l{i}_sgu_out")
            st.update(h=h, zz=zz, gated=gated, y=y, bs_t=bs_t)
        else:
            h = _norm_mod_fwd(x, gmix, sc1, sh1, BF16, f"l{i}_norm1")
            lat = _mm(h, W["mla_w_dq_dkv"], out_dtypes=(F32,), name=f"l{i}_mla_lat")
            cqn, ckvn, krot = _mla_lat_fwd(lat, S["mla_q_norm_g"], S["mla_kv_norm_g"], cc, sa, sb, f"l{i}_mla_latn")
            qpad = _mm(cqn, W["mla_w_uq"], out_dtypes=(F32,), name=f"l{i}_mla_uq")
            kv = _mm(ckvn, W["mla_w_ukv"], out_dtypes=(F32,), name=f"l{i}_mla_ukv")
            q, k, kt, v, vt = _mla_prep(qpad, kv, krot, cc, sa, sb, f"l{i}_mla_prep")
            o, lse = _attn_fwd(q, k, vt, f"l{i}_attn")
            x2, y = _mm(o, W["mla_w_o"], epi=_epi_residual, extras=((x, "mn"), (g1, "n")), out_dtypes=(F32, BF16), name=f"l{i}_mla_o")
            st.update(h=h, lat=lat, cqn=cqn, ckvn=ckvn, q=q, k=k, kt=kt, v=v, o=o, lse=lse, y=y)
        Wm = weights_of(i, "mlp", x2)
        W = {**W, **Wm}
        h2 = _norm_mod_fwd(x2, gmlp + Wm["_tok"], sc2, sh2, BF16, f"l{i}_norm2")
        z = _mm(h2, W["mlp_w1"], epi=lambda acc: (jnp.square(jnp.maximum(acc, 0.0)),), name=f"l{i}_mlp1")
        x3, o2 = _mm(z, W["mlp_w2"], epi=_epi_residual, extras=((x2, "mn"), (g2, "n")), out_dtypes=(F32, BF16), name=f"l{i}_mlp2")
        st.update(x2=x2, h2=h2, z=z, o2=o2, W=W)
        saved.append(st)
        x = x3

    loss, dx, dfinal_g = _loss_head(x, target, S["final_g"], "loss_head")

    gS = {"final_g": dfinal_g, "norm_mix_g": [None] * DEPTH, "norm_mlp_g": [None] * DEPTH, "pool_scale": [None] * 2}
    dmod = [None] * DEPTH
    tok = jnp.zeros((), F32)
    for i in reversed(range(DEPTH)):
        st = saved[i]
        W, gW = st["W"], {}
        sh1, sc1, g1, sh2, sc2, g2 = (mod[i:i + 1, n * D:(n + 1) * D] for n in range(6))
        kind, j = i % N_MIXERS, i // N_MIXERS
        gmix, gmlp = S["norm_mix_g"][i:i + 1], S["norm_mlp_g"][i:i + 1]
        do2, dg2 = _resid_bwd(dx, st["o2"], g2 + tok, f"l{i}_b_res2")
        da = _mm(do2, W["mlp_w2"], tb=True, epi=lambda acc, zt: (acc * (2.0 * jnp.sqrt(zt.astype(F32))),), extras=((st["z"], "mn"),), name=f"l{i}_b_dz")
        gW["mlp_w2"] = _mm(st["z"], do2, ta=True, name=f"l{i}_b_dw2")
        dh2 = _mm(da, W["mlp_w1"], tb=True, out_dtypes=(F32,), name=f"l{i}_b_dh2")
        gW["mlp_w1"] = _mm(st["h2"], da, ta=True, name=f"l{i}_b_dw1")
        dx2, dgmlp, dsc2, dsh2 = _norm_mod_bwd(st["x2"], dh2, dx, gmlp, sc2, f"l{i}_b_norm2")
        gS["norm_mlp_g"][i] = dgmlp
        dy, q1 = _resid_bwd(dx2, st["y"], g1, f"l{i}_b_res1")
        if kind == 0:
            dh, dpw, dpsc, dg1 = _pool_bwd(dy, st["pooled"], W["pool_w"], S["pool_scale"][j:j + 1], g1, q1, f"l{i}_b_pool")
            gW["pool_w"] = dpw.astype(BF16)
            gS["pool_scale"][j] = dpsc
        elif kind == 1:
            dg1 = q1
            dgated = _mm(dy, W["sgu_w_out"], tb=True, name=f"l{i}_b_dgated")
            gW["sgu_w_out"] = _mm(st["gated"], dy, ta=True, name=f"l{i}_b_dwout")
            dzz, dws, dbs, dlg, dlb = _sgu_gate_bwd(st["zz"], dgated, S["sgu_ln_g"], S["sgu_ln_b"], S["sgu_w_s"], st["bs_t"], f"l{i}_b_sgu_gate")
            gS.update(sgu_w_s=dws, sgu_b_s=dbs[:, :, 0], sgu_ln_g=dlg, sgu_ln_b=dlb)
            dh = _mm(dzz, W["sgu_w_in"], tb=True, out_dtypes=(F32,), name=f"l{i}_b_dh_sgu")
            gW["sgu_w_in"] = _mm(st["h"], dzz, ta=True, name=f"l{i}_b_dwin")
        else:
            dg1 = q1
            do = _mm(dy, W["mla_w_o"], tb=True, name=f"l{i}_b_do")
            gW["mla_w_o"] = _mm(st["o"], dy, ta=True, name=f"l{i}_b_dwo")
            delta = _attn_delta(do, st["o"], f"l{i}_b_delta")
            dqt, dk, dv = _attn_bwd(st["q"], st["k"], st["kt"], st["v"], do, st["lse"], delta, f"l{i}_b_attn")
            dqpad, dkv, dkrot = _mla_prep_bwd(dqt, dk, dv, cc, sa, sb, f"l{i}_b_mla_prep")
            dcqn = _mm(dqpad, W["mla_w_uq"], tb=True, out_dtypes=(F32,), name=f"l{i}_b_dcq")
            gW["mla_w_uq"] = _mm(st["cqn"], dqpad, ta=True, name=f"l{i}_b_dwuq")
            dckvn = _mm(dkv, W["mla_w_ukv"], tb=True, out_dtypes=(F32,), name=f"l{i}_b_dckv")
            gW["mla_w_ukv"] = _mm(st["ckvn"], dkv, ta=True, name=f"l{i}_b_dwukv")
            dlat, dqg, dkvg = _mla_lat_bwd(st["lat"], dcqn, dckvn, dkrot, S["mla_q_norm_g"], S["mla_kv_norm_g"], cc, sa, sb, f"l{i}_b_mla_latn")
            gS.update(mla_q_norm_g=dqg, mla_kv_norm_g=dkvg)
            dh = _mm(dlat, W["mla_w_dq_dkv"], tb=True, out_dtypes=(F32,), name=f"l{i}_b_dh_mla")
            gW["mla_w_dq_dkv"] = _mm(st["h"], dlat, ta=True, name=f"l{i}_b_dwdq")
        dx, dgmix, dsc1, dsh1 = _norm_mod_bwd(st["x"], dh, dx2, gmix, sc1, f"l{i}_b_norm1")
        gS["norm_mix_g"][i] = dgmix
        dmod[i] = jnp.concatenate([dsh1, dsc1, dg1, dsh2, dsc2, dg2], axis=1)
        tok = grads_of(i, gW, dx)

    for n in ("norm_mix_g", "norm_mlp_g", "pool_scale"):
        gS[n] = jnp.concatenate(gS[n], axis=0)
    return loss, dx, gS, jnp.concatenate(dmod, axis=0)


_SMALL = {
    "norm_mix_g": (DEPTH, D_MODEL), "norm_mlp_g": (DEPTH, D_MODEL), "sgu_ln_g": (1, SGU_W), "sgu_ln_b": (1, SGU_W),
    "sgu_w_s": (SGU_H, SGU_CHUNK, SGU_CHUNK), "sgu_b_s": (SGU_H, SGU_CHUNK), "mla_kv_norm_g": (1, MLA_KVL), "final_g": (1, D_MODEL),
    "pool_scale": (2, D_MODEL), "mla_q_norm_g": (1, MLA_QL), "dmod": (DEPTH, 6 * D_MODEL),
}
_PACK_W = 1024


def _pack(vals):
    flat = jnp.concatenate([v.reshape(-1) for v in vals])
    rows = -(-flat.shape[0] // (8 * _PACK_W)) * 8
    return jnp.pad(flat, (0, rows * _PACK_W - flat.shape[0])).reshape(rows, _PACK_W)


def _unpack(buf, shapes):
    flat, out, off = buf.reshape(-1), [], 0
    for s in shapes:
        n = math.prod(s)
        out.append(flat[off:off + n].reshape(s))
        off += n
    return out


def kernel(x, c, positions, ada_w, ada_b, norm_mix_g, norm_mlp_g, pool_w, pool_scale, sgu_w_in, sgu_ln_g, sgu_ln_b, sgu_w_s, sgu_b_s, sgu_w_out, mla_w_dq_dkv, mla_q_norm_g, mla_kv_norm_g, mla_w_uq, mla_w_ukv, mla_w_o, mlp_w1, mlp_w2, final_g, loss_target, m_ada_w, m_ada_b, m_norm_mix_g, m_norm_mlp_g, m_pool_w, m_pool_scale, m_sgu_w_in, m_sgu_ln_g, m_sgu_ln_b, m_sgu_w_s, m_sgu_b_s, m_sgu_w_out, m_mla_w_dq_dkv, m_mla_q_norm_g, m_mla_kv_norm_g, m_mla_w_uq, m_mla_w_ukv, m_mla_w_o, m_mlp_w1, m_mlp_w2, m_final_g, v_ada_w, v_ada_b, v_norm_mix_g, v_norm_mlp_g, v_pool_w, v_pool_scale, v_sgu_w_in, v_sgu_ln_g, v_sgu_ln_b, v_sgu_w_s, v_sgu_b_s, v_sgu_w_out, v_mla_w_dq_dkv, v_mla_q_norm_g, v_mla_kv_norm_g, v_mla_w_uq, v_mla_w_ukv, v_mla_w_o, v_mlp_w1, v_mlp_w2, v_final_g):
    P = dict(ada_w=ada_w, ada_b=ada_b, norm_mix_g=norm_mix_g, norm_mlp_g=norm_mlp_g, pool_w=pool_w, pool_scale=pool_scale, sgu_w_in=sgu_w_in,
             sgu_ln_g=sgu_ln_g, sgu_ln_b=sgu_ln_b, sgu_w_s=sgu_w_s, sgu_b_s=sgu_b_s, sgu_w_out=sgu_w_out, mla_w_dq_dkv=mla_w_dq_dkv,
             mla_q_norm_g=mla_q_norm_g, mla_kv_norm_g=mla_kv_norm_g, mla_w_uq=mla_w_uq, mla_w_ukv=mla_w_ukv, mla_w_o=mla_w_o, mlp_w1=mlp_w1,
             mlp_w2=mlp_w2, final_g=final_g)
    M = dict(ada_w=m_ada_w, ada_b=m_ada_b, norm_mix_g=m_norm_mix_g, norm_mlp_g=m_norm_mlp_g, pool_w=m_pool_w, pool_scale=m_pool_scale,
             sgu_w_in=m_sgu_w_in, sgu_ln_g=m_sgu_ln_g, sgu_ln_b=m_sgu_ln_b, sgu_w_s=m_sgu_w_s, sgu_b_s=m_sgu_b_s, sgu_w_out=m_sgu_w_out,
             mla_w_dq_dkv=m_mla_w_dq_dkv, mla_q_norm_g=m_mla_q_norm_g, mla_kv_norm_g=m_mla_kv_norm_g, mla_w_uq=m_mla_w_uq, mla_w_ukv=m_mla_w_ukv,
             mla_w_o=m_mla_w_o, mlp_w1=m_mlp_w1, mlp_w2=m_mlp_w2, final_g=m_final_g)
    V = dict(ada_w=v_ada_w, ada_b=v_ada_b, norm_mix_g=v_norm_mix_g, norm_mlp_g=v_norm_mlp_g, pool_w=v_pool_w, pool_scale=v_pool_scale,
             sgu_w_in=v_sgu_w_in, sgu_ln_g=v_sgu_ln_g, sgu_ln_b=v_sgu_ln_b, sgu_w_s=v_sgu_w_s, sgu_b_s=v_sgu_b_s, sgu_w_out=v_sgu_w_out,
             mla_w_dq_dkv=v_mla_w_dq_dkv, mla_q_norm_g=v_mla_q_norm_g, mla_kv_norm_g=v_mla_kv_norm_g, mla_w_uq=v_mla_w_uq, mla_w_ukv=v_mla_w_ukv,
             mla_w_o=v_mla_w_o, mlp_w1=v_mlp_w1, mlp_w2=v_mlp_w2, final_g=v_final_g)
    order = list(P)
    xi, yi, ci = _idx()
    chip = 2 * xi + yi
    D = D_MODEL
    n_ada = ada_w.shape[2]

    pre = _allgather8(_pack([c, pool_scale, mla_q_norm_g]), "ag_small")
    flat = pre.reshape(N_DEV, -1)
    c_all = flat[:, :D]
    ps_all = flat[0::2, D:D + 2 * (D // N_CHIPS)].reshape(N_CHIPS, 2, D // N_CHIPS).transpose(1, 0, 2).reshape(2, D)
    q0 = D + 2 * (D // N_CHIPS)
    qg_all = flat[0::2, q0:q0 + MLA_QL // N_CHIPS].reshape(1, MLA_QL)

    ada_b_loc = lax.dynamic_slice_in_dim(ada_b, chip * n_ada, n_ada, axis=1)[:, None, :]
    modp = _ada_fwd(c_all, ada_w, ada_b_loc, "ada_fwd")
    mod = _mod_exchange(modp.transpose(1, 0, 2), "mod_exchange").transpose(1, 0, 2).reshape(DEPTH, 6 * D)

    S = dict(norm_mix_g=norm_mix_g, norm_mlp_g=norm_mlp_g, pool_scale=ps_all, sgu_ln_g=sgu_ln_g, sgu_ln_b=sgu_ln_b, sgu_w_s=sgu_w_s[0],
             sgu_b_s=sgu_b_s[0], mla_q_norm_g=qg_all, mla_kv_norm_g=mla_kv_norm_g, final_g=final_g[None, :])
    cidx, chipidx = jnp.reshape(ci, (1,)).astype(jnp.int32), jnp.reshape(chip, (1,)).astype(jnp.int32)
    view2d = lambda a: a.reshape(-1, a.shape[-1])

    def piece_rows(kind, blk):
        r = _PIECE_KINDS[kind][0]
        return blk * r, r

    groups = [_layer_pieces(0)[:-2], _layer_pieces(0)[-2:]] + [_layer_pieces(i) for i in range(1, DEPTH)]
    gathers = {}

    def gather_start(g, dep):
        srcs, shapes = [], []
        for kind, blk in groups[g]:
            r0, r = piece_rows(kind, blk)
            cdim = _PIECE_KINDS[kind][1]
            srcs.append(view2d(P[kind])[r0:r0 + r].astype(BF16).reshape(2, r // 2, cdim))
            shapes.append(jax.ShapeDtypeStruct((N_CHIPS, 2, r // 2, cdim), BF16))
        gathers[g] = _xchip_start("gather", srcs, shapes, dep, f"ag_start_g{g}")

    def gather_finish(g, after):
        ssem, rsem, srcs, lands, _ = gathers.pop(g)
        srcs, lands = _xchip_wait("gather", ssem, rsem, srcs, lands, [after], f"ag_wait_g{g}")
        lands = _sibling_fwd(lands, f"ag_sibling_g{g}")
        if g + 2 < len(groups):
            gather_start(g + 2, lands[0])
        W = {"_tok": sum((st[-1][0, 0] for st in gathers.values()), jnp.zeros((), F32))}
        for (kind, _), s, land in zip(groups[g], srcs, lands, strict=True):
            r, cdim, to_full, _ = _PIECE_KINDS[kind]
            W[kind] = to_full(lax.dynamic_update_index_in_dim(land, s, chip, 0).reshape(N_CHIPS, r, cdim))
        return W

    def weights_of(i, part, x_i):
        if i == 0:
            return gather_finish(0 if part == "mix" else 1, x_i)
        return gather_finish(i + 1, x_i) if part == "mix" else {"_tok": jnp.zeros((), F32)}

    scatters = {}
    bufs = {n: tuple(lax.empty(view2d(P[n]).shape, F32) for _ in range(4)) for n in _PIECE_KINDS}

    def scatter_start(i, gW, dep):
        pcs = _layer_pieces(i)
        blocked = []
        for kind, _ in pcs:
            r, cdim, _, to_blocks = _PIECE_KINDS[kind]
            blocked.append(to_blocks(gW[kind]).reshape(N_CHIPS, 2, r // 2, cdim).transpose(1, 0, 2, 3))
        from_sib = _sibling_swap(blocked, f"rs_sibling_l{i}")
        pair, shapes = [], []
        for (kind, _), b, f in zip(pcs, blocked, from_sib, strict=True):
            _, _, hr, cdim = b.shape
            p = _sum_sel(cidx, b.reshape(2, N_CHIPS * hr, cdim), [f.reshape(1, N_CHIPS * hr, cdim)], f"rs_pair_l{i}_{kind}", BF16)
            pair.append(p.reshape(N_CHIPS, hr, cdim))
            shapes.append(jax.ShapeDtypeStruct((N_CHIPS - 1, hr, cdim), BF16))
        scatters[i] = (pcs, *_xchip_start("scatter", pair, shapes, dep, f"rs_start_l{i}"))
        return scatters[i][-1][0, 0]

    def scatter_finish(i, after):
        pcs, ssem, rsem, pair, lands, _ = scatters.pop(i)
        pair, lands = _xchip_wait("scatter", ssem, rsem, pair, lands, after, f"rs_wait_l{i}")
        halves = [_sum_sel(chipidx, p, [l], f"rs_sum_l{i}_{kind}", F32) for (kind, _), p, l in zip(pcs, pair, lands, strict=True)]
        got = _sibling_send(halves, f"rs_merge_l{i}")
        for (kind, blk), mine, other in zip(pcs, halves, got, strict=True):
            r0, _ = piece_rows(kind, blk)
            bufs[kind] = tuple(_adamw_piece(cidx, view2d(P[kind]), view2d(M[kind]), view2d(V[kind]), mine, other, bufs[kind], r0,
                                            f"adamw_l{i}_{kind}"))
        return lands[0]

    first_layer = {}

    def grads_of(i, gW, dx_i):
        dep = scatter_finish(i + 1, [dx_i]) if i + 1 in scatters else dx_i
        if i == 0:
            first_layer.update(gW)
            return jnp.zeros((), F32)
        return scatter_start(i, gW, dep)

    gather_start(0, mod)
    gather_start(1, mod)
    loss_l, dx, gS, dmod = _local_step(x[0], positions[0], loss_target[0], mod, S, weights_of, grads_of)
    loss = lax.psum(loss_l[0, 0], ("x", "y", "c"))

    gS["dmod"] = dmod
    small = _allgather8(_pack([gS[n] for n in _SMALL]), "ag_small_grads")
    small = small + scatter_start(0, first_layer, small)
    small_sum = _unpack(_sum_lead([small], "sum_small_grads"), list(_SMALL.values()))
    G = dict(zip(_SMALL, small_sum, strict=True))
    grads = {
        "ada_b": G["dmod"], "norm_mix_g": G["norm_mix_g"], "norm_mlp_g": G["norm_mlp_g"], "sgu_ln_g": G["sgu_ln_g"], "sgu_ln_b": G["sgu_ln_b"],
        "sgu_w_s": G["sgu_w_s"][None], "sgu_b_s": G["sgu_b_s"][None], "mla_kv_norm_g": G["mla_kv_norm_g"], "final_g": G["final_g"][0],
        "pool_scale": lax.dynamic_slice_in_dim(G["pool_scale"], chip * (D // N_CHIPS), D // N_CHIPS, axis=1),
        "mla_q_norm_g": lax.dynamic_slice_in_dim(G["mla_q_norm_g"], chip * (MLA_QL // N_CHIPS), MLA_QL // N_CHIPS, axis=1),
    }
    dmod_all = _unpack(small, [(N_DEV,) + (small.shape[1] * _PACK_W,)])[0]
    off = sum(math.prod(s) for n, s in _SMALL.items() if n != "dmod")
    dmod_all = dmod_all[:, off:off + DEPTH * 6 * D].reshape(N_DEV, DEPTH, 6 * D)
    dmod_loc = lax.dynamic_slice_in_dim(dmod_all, chip * n_ada, n_ada, axis=2).transpose(1, 0, 2)
    grads["ada_w"] = _ada_bwd(c_all.T, dmod_loc, "ada_bwd")

    deltas, new_m, new_v = {}, {}, {}
    for n in order:
        if n not in _PIECE_KINDS:
            deltas[n], new_m[n], new_v[n] = _adamw(P[n], grads[n].reshape(P[n].shape), M[n], V[n], f"adamw_{n}")
    scatter_finish(0, [deltas["ada_w"], deltas["sgu_w_s"]] + [bufs[n][0] for n in ("mlp_w1", "mlp_w2", "sgu_w_in", "mla_w_o")])
    for n in _PIECE_KINDS:
        grads[n], deltas[n], new_m[n], new_v[n] = (b.reshape(P[n].shape) for b in bufs[n])
    return (loss, dx[None], *[grads[n].reshape(P[n].shape) for n in order], *[deltas[n] for n in order], *[new_m[n] for n in order],
            *[new_v[n] for n in order])
```

```python
import math

import jax
import jax.numpy as jnp
from jax import lax
from jax.experimental import pallas as pl
from jax.experimental.pallas import tpu as pltpu

F32, BF16 = jnp.float32, jnp.bfloat16
MESH = pl.DeviceIdType.MESH

D_MODEL = 1024
DEPTH = 4
N_MIXERS = 3
POOL_WINDOWS = (2, 4, 8, 16)
POOL_GD = D_MODEL // len(POOL_WINDOWS)
POOL_HALO = 16
SGU_CHUNK = 128
SGU_W = D_MODEL
SGU_HD = 128
SGU_H = SGU_W // SGU_HD
MLA_H = 16
MLA_QL = 256
MLA_KVL = 128
MLA_NOPE = 128
MLA_ROPE = 64
MLA_V = 128
MLA_HP = 256
MLA_LATP = 512
ROPE_THETA = 10000.0
RMS_EPS = 1e-6
LN_EPS = 1e-5
SM_SCALE = (MLA_NOPE + MLA_ROPE) ** -0.5
NEG_INF = -1e30
ADAM_LR, ADAM_B1, ADAM_B2, ADAM_EPS, ADAM_WD, ADAM_STEP = 0.001, 0.9, 0.999, 1e-08, 0.01, 10
N_CHIPS = 4
N_DEV = 8
ROW_TILE = 512
ATT_TILE = 512
ATT_SUB = 256
ATT_FWD_HEADS = 4
ATT_BWD_HEADS = 2
MM_VMEM_BUDGET = 40 << 20


def _idx():
    return lax.axis_index("x"), lax.axis_index("y"), lax.axis_index("c")


def _mm(a, b, *, name, ta=False, tb=False, epi=None, extras=(), out_dtypes=(BF16,), tm=1024, tn=1024, tk=1024):
    if ta:
        K, M = a.shape
    else:
        M, K = a.shape
    if tb:
        N, Kb = b.shape
    else:
        Kb, N = b.shape
    assert K == Kb, (a.shape, b.shape, ta, tb)
    tm, tn, tk = min(tm, M), min(tn, N), min(tk, K)

    def vmem_bytes(tm_, tk_):
        per_mn = sum(arr.dtype.itemsize for arr, kind in extras if kind == "mn") + sum(jnp.dtype(dt).itemsize for dt in out_dtypes)
        return 2 * (tm_ * tk_ * a.dtype.itemsize + tk_ * tn * b.dtype.itemsize + tm_ * tn * per_mn)

    if vmem_bytes(tm, K) <= MM_VMEM_BUDGET:
        tk = K
    elif tm >= 512 and vmem_bytes(tm // 2, K) <= MM_VMEM_BUDGET:
        tm, tk = tm // 2, K
    assert M % tm == 0 and N % tn == 0 and K % tk == 0, (M, N, K, tm, tn, tk)
    nk = K // tk
    a_spec = pl.BlockSpec((tk, tm), lambda i, j, k: (k, i)) if ta else pl.BlockSpec((tm, tk), lambda i, j, k: (i, k))
    b_spec = pl.BlockSpec((tn, tk), lambda i, j, k: (j, k)) if tb else pl.BlockSpec((tk, tn), lambda i, j, k: (k, j))
    ex_specs = []
    for arr, kind in extras:
        if kind == "mn":
            ex_specs.append(pl.BlockSpec((tm, tn), lambda i, j, k: (i, j)))
        elif kind == "n":
            ex_specs.append(pl.BlockSpec((1, tn), lambda i, j, k: (0, j)))
        else:
            ex_specs.append(pl.BlockSpec((tm, arr.shape[1]), lambda i, j, k: (i, 0)))
    n_ex, n_out = len(extras), len(out_dtypes)
    dims = (((0 if ta else 1,), (1 if tb else 0,)), ((), ()))

    def body(*refs):
        a_ref, b_ref = refs[0], refs[1]
        ex_refs = refs[2:2 + n_ex]
        out_refs = refs[2 + n_ex:2 + n_ex + n_out]
        part = lax.dot_general(a_ref[...].astype(BF16), b_ref[...].astype(BF16), dims, preferred_element_type=F32)

        def finish(acc):
            outs = epi(acc, *[r[...] for r in ex_refs]) if epi is not None else (acc,)
            for r, o in zip(out_refs, outs, strict=True):
                r[...] = o.astype(r.dtype)

        if nk == 1:
            finish(part)
        else:
            acc_ref = refs[-1]
            k = pl.program_id(2)

            @pl.when(k == 0)
            def _():
                acc_ref[...] = part

            @pl.when(k > 0)
            def _():
                acc_ref[...] += part

            @pl.when(k == nk - 1)
            def _():
                finish(acc_ref[...])

    outs = pl.pallas_call(
        body,
        name=name,
        grid=(M // tm, N // tn, nk),
        in_specs=[a_spec, b_spec, *ex_specs],
        out_specs=[pl.BlockSpec((tm, tn), lambda i, j, k: (i, j)) for _ in range(n_out)],
        out_shape=[jax.ShapeDtypeStruct((M, N), dt) for dt in out_dtypes],
        scratch_shapes=[pltpu.VMEM((tm, tn), F32)] if nk > 1 else [],
        compiler_params=pltpu.CompilerParams(dimension_semantics=("parallel", "parallel", "arbitrary")),
    )(a, b, *[arr for arr, _ in extras])
    return outs[0] if n_out == 1 else tuple(outs)


def _epi_sq_relu(acc):
    r = jnp.maximum(acc, 0.0)
    return r * r, 2.0 * r


def _epi_residual(acc, x, g):
    return x + g * acc, acc


def _row_spec(tr, d):
    return pl.BlockSpec((tr, d), lambda i: (i, 0))


def _vec_spec(d):
    return pl.BlockSpec((1, d), lambda i: (0, 0))


def _colsum(v):
    return jnp.sum(v, axis=0, keepdims=True)


def _norm_mod_fwd(x, gain, sc, sh, out_dtype, name):
    T, D = x.shape
    tr = min(T, ROW_TILE)

    def body(x_ref, g_ref, sc_ref, sh_ref, o_ref):
        xv = x_ref[...]
        r = lax.rsqrt(jnp.mean(xv * xv, axis=-1, keepdims=True) + RMS_EPS)
        o_ref[...] = (((xv * r) * g_ref[...]) * (1.0 + sc_ref[...]) + sh_ref[...]).astype(o_ref.dtype)

    return pl.pallas_call(
        body, name=name, grid=(T // tr,),
        in_specs=[_row_spec(tr, D), _vec_spec(D), _vec_spec(D), _vec_spec(D)],
        out_specs=_row_spec(tr, D),
        out_shape=jax.ShapeDtypeStruct((T, D), out_dtype),
        compiler_params=pltpu.CompilerParams(dimension_semantics=("parallel",)),
    )(x, gain, sc, sh)


def _norm_mod_bwd(x, dh, dres, gain, sc, name):
    T, D = x.shape
    tr = min(T, ROW_TILE)

    def body(x_ref, dh_ref, dres_ref, g_ref, sc_ref, dx_ref, dg_ref, dsc_ref, dsh_ref):
        @pl.when(pl.program_id(0) == 0)
        def _():
            dg_ref[...] = jnp.zeros_like(dg_ref)
            dsc_ref[...] = jnp.zeros_like(dsc_ref)
            dsh_ref[...] = jnp.zeros_like(dsh_ref)

        xv = x_ref[...]
        r = lax.rsqrt(jnp.mean(xv * xv, axis=-1, keepdims=True) + RMS_EPS)
        xn = xv * r
        dhv = dh_ref[...].astype(F32)
        dsh_ref[...] += _colsum(dhv)
        dsc_ref[...] += _colsum(dhv * (xn * g_ref[...]))
        dt = dhv * (1.0 + sc_ref[...])
        dg_ref[...] += _colsum(dt * xn)
        dxn = dt * g_ref[...]
        dx_ref[...] = dres_ref[...] + r * (dxn - xn * jnp.mean(dxn * xn, axis=-1, keepdims=True))

    return pl.pallas_call(
        body, name=name, grid=(T // tr,),
        in_specs=[_row_spec(tr, D), _row_spec(tr, D), _row_spec(tr, D), _vec_spec(D), _vec_spec(D)],
        out_specs=[_row_spec(tr, D), _vec_spec(D), _vec_spec(D), _vec_spec(D)],
        out_shape=[jax.ShapeDtypeStruct((T, D), F32)] + [jax.ShapeDtypeStruct((1, D), F32)] * 3,
        compiler_params=pltpu.CompilerParams(dimension_semantics=("arbitrary",)),
    )(x, dh, dres, gain, sc)


def _resid_bwd(dx, y, g, name):
    T, D = dx.shape
    tr = min(T, ROW_TILE)

    def body(dx_ref, y_ref, g_ref, dy_ref, q_ref):
        @pl.when(pl.program_id(0) == 0)
        def _():
            q_ref[...] = jnp.zeros_like(q_ref)

        dxv = dx_ref[...]
        dy_ref[...] = (g_ref[...] * dxv).astype(BF16)
        q_ref[...] += _colsum(dxv * y_ref[...].astype(F32))

    return pl.pallas_call(
        body, name=name, grid=(T // tr,),
        in_specs=[_row_spec(tr, D), _row_spec(tr, D), _vec_spec(D)],
        out_specs=[_row_spec(tr, D), _vec_spec(D)],
        out_shape=[jax.ShapeDtypeStruct((T, D), BF16), jax.ShapeDtypeStruct((1, D), F32)],
        compiler_params=pltpu.CompilerParams(dimension_semantics=("arbitrary",)),
    )(dx, y, g)


def _loss_head(x, target, gain, name):
    T, D = x.shape
    tr = min(T, ROW_TILE)

    def body(x_ref, t_ref, g_ref, loss_ref, dx_ref, dg_ref):
        @pl.when(pl.program_id(0) == 0)
        def _():
            loss_ref[...] = jnp.zeros_like(loss_ref)
            dg_ref[...] = jnp.zeros_like(dg_ref)

        xv = x_ref[...]
        r = lax.rsqrt(jnp.mean(xv * xv, axis=-1, keepdims=True) + RMS_EPS)
        xn = xv * r
        err = xn * g_ref[...] - t_ref[...]
        row = jnp.mean(err * err, axis=-1, keepdims=True)
        loss_ref[...] += 0.5 * jnp.sum(row, axis=0, keepdims=True)
        dy = err * (1.0 / D)
        dg_ref[...] += _colsum(dy * xn)
        dxn = dy * g_ref[...]
        dx_ref[...] = r * (dxn - xn * jnp.mean(dxn * xn, axis=-1, keepdims=True))

    return pl.pallas_call(
        body, name=name, grid=(T // tr,),
        in_specs=[_row_spec(tr, D), _row_spec(tr, D), _vec_spec(D)],
        out_specs=[_vec_spec(128), _row_spec(tr, D), _vec_spec(D)],
        out_shape=[jax.ShapeDtypeStruct((1, 128), F32), jax.ShapeDtypeStruct((T, D), F32), jax.ShapeDtypeStruct((1, D), F32)],
        compiler_params=pltpu.CompilerParams(dimension_semantics=("arbitrary",)),
    )(x, target, gain)


def _pool_fwd(h, w, scale, x, g1, name):
    T, D = h.shape
    tr = min(T, ROW_TILE)

    def body(h_ref, w_ref, sc_ref, x_ref, g_ref, x2_ref, pooled_ref, ypre_ref, halo_ref):
        i = pl.program_id(0)

        @pl.when(i == 0)
        def _():
            halo_ref[...] = jnp.zeros_like(halo_ref)

        hv = h_ref[...]
        buf = jnp.concatenate([halo_ref[...], hv], axis=0)
        halo_ref[...] = hv[tr - POOL_HALO:, :]
        t = (i * tr + lax.broadcasted_iota(jnp.int32, (tr, 1), 0)).astype(F32)
        for gi, win in enumerate(POOL_WINDOWS):
            cols = slice(gi * POOL_GD, (gi + 1) * POOL_GD)
            val = buf[:, cols]
            sh = 1
            while sh < win:
                val = val + pltpu.roll(val, sh, axis=0)
                sh *= 2
            pooled = val[POOL_HALO:, :] / jnp.minimum(t + 1.0, float(win)) - hv[:, cols]
            pb = pooled.astype(BF16)
            pooled_ref[:, cols] = pb
            yp = jnp.dot(pb, w_ref[gi], preferred_element_type=F32)
            ypre_ref[:, cols] = yp.astype(BF16)
            x2_ref[:, cols] = x_ref[:, cols] + g_ref[:, cols] * (yp * sc_ref[:, cols])

    return pl.pallas_call(
        body, name=name, grid=(T // tr,),
        in_specs=[_row_spec(tr, D), pl.BlockSpec(w.shape, lambda i: (0, 0, 0)), _vec_spec(D), _row_spec(tr, D), _vec_spec(D)],
        out_specs=[_row_spec(tr, D)] * 3,
        out_shape=[jax.ShapeDtypeStruct((T, D), F32), jax.ShapeDtypeStruct((T, D), BF16), jax.ShapeDtypeStruct((T, D), BF16)],
        scratch_shapes=[pltpu.VMEM((POOL_HALO, D), F32)],
        compiler_params=pltpu.CompilerParams(dimension_semantics=("arbitrary",)),
    )(h, w, scale, x, g1)


def _pool_bwd(dy, pooled, w, scale, g1, q, name):
    T, D = dy.shape
    tr = min(T, ROW_TILE)
    nt = T // tr
    ltot = tr + POOL_HALO

    def body(dy_ref, pooled_ref, w_ref, sc_ref, g_ref, q_ref, dh_ref, dw_ref, dsc_ref, dg_ref, halo_ref):
        i = pl.program_id(0)

        @pl.when(i == 0)
        def _():
            halo_ref[...] = jnp.zeros_like(halo_ref)
            dw_ref[...] = jnp.zeros_like(dw_ref)
            dsc_ref[...] = g_ref[...] * q_ref[...]
            dg_ref[...] = sc_ref[...] * q_ref[...]

        t = ((nt - 1 - i) * tr + lax.broadcasted_iota(jnp.int32, (tr, 1), 0)).astype(F32)
        for gi, win in enumerate(POOL_WINDOWS):
            cols = slice(gi * POOL_GD, (gi + 1) * POOL_GD)
            dyb = (dy_ref[:, cols].astype(F32) * sc_ref[:, cols]).astype(BF16)
            dw_ref[gi] += lax.dot_general(pooled_ref[:, cols], dyb, (((0,), (0,)), ((), ())), preferred_element_type=F32)
            dpool = lax.dot_general(dyb, w_ref[gi], (((1,), (1,)), ((), ())), preferred_element_type=F32)
            qv = dpool / jnp.minimum(t + 1.0, float(win))
            val = jnp.concatenate([qv, halo_ref[:, cols]], axis=0)
            halo_ref[:, cols] = qv[:POOL_HALO, :]
            sh = 1
            while sh < win:
                val = val + pltpu.roll(val, ltot - sh, axis=0)
                sh *= 2
            dh_ref[:, cols] = val[:tr, :] - dpool

    rev = pl.BlockSpec((tr, D), lambda i: (nt - 1 - i, 0))
    return pl.pallas_call(
        body, name=name, grid=(nt,),
        in_specs=[rev, rev, pl.BlockSpec(w.shape, lambda i: (0, 0, 0)), _vec_spec(D), _vec_spec(D), _vec_spec(D)],
        out_specs=[rev, pl.BlockSpec(w.shape, lambda i: (0, 0, 0)), _vec_spec(D), _vec_spec(D)],
        out_shape=[jax.ShapeDtypeStruct((T, D), F32), jax.ShapeDtypeStruct(w.shape, F32),
                   jax.ShapeDtypeStruct((1, D), F32), jax.ShapeDtypeStruct((1, D), F32)],
        scratch_shapes=[pltpu.VMEM((POOL_HALO, D), F32)],
        compiler_params=pltpu.CompilerParams(dimension_semantics=("arbitrary",)),
    )(dy, pooled, w, scale, g1, q)


_INV_SQRT2 = 0.7071067811865476
_INV_SQRT2PI = 0.3989422804014327


def _gelu(v):
    return 0.5 * v * (1.0 + lax.erf(v * _INV_SQRT2))


def _gelu_grad(v):
    return 0.5 * (1.0 + lax.erf(v * _INV_SQRT2)) + v * jnp.exp(-0.5 * v * v) * _INV_SQRT2PI


def _sgu_ln(v, g, b):
    mu = jnp.mean(v, axis=-1, keepdims=True)
    xc = v - mu
    rstd = lax.rsqrt(jnp.mean(xc * xc, axis=-1, keepdims=True) + LN_EPS)
    xh = xc * rstd
    return xh, rstd, xh * g + b


def _tril_mask():
    return lax.broadcasted_iota(jnp.int32, (SGU_CHUNK, SGU_CHUNK), 0) >= lax.broadcasted_iota(jnp.int32, (SGU_CHUNK, SGU_CHUNK), 1)


SGU_TILE = 256


def _sgu_gate_fwd(zz, ln_g, ln_b, ws, bs_t, name):
    T = zz.shape[0]
    ts = min(T, SGU_TILE)

    def body(zz_ref, g_ref, b_ref, ws_ref, bs_ref, out_ref):
        z = _gelu(zz_ref[...])
        u = z[:, :SGU_W]
        _, _, vn = _sgu_ln(z[:, SGU_W:], g_ref[...], b_ref[...])
        vb = vn.astype(BF16)
        tril = _tril_mask()
        for hh in range(SGU_H):
            wm = jnp.where(tril, ws_ref[hh], 0.0).astype(BF16)
            bcol = bs_ref[:, hh:hh + 1]
            cs = slice(hh * SGU_HD, (hh + 1) * SGU_HD)
            for j in range(ts // SGU_CHUNK):
                rs = slice(j * SGU_CHUNK, (j + 1) * SGU_CHUNK)
                mixed = jnp.dot(wm, vb[rs, cs], preferred_element_type=F32) + bcol
                out_ref[rs, cs] = (u[rs, cs] * mixed).astype(BF16)

    return pl.pallas_call(
        body, name=name, grid=(T // ts,),
        in_specs=[_row_spec(ts, 2 * SGU_W), _vec_spec(SGU_W), _vec_spec(SGU_W),
                  pl.BlockSpec(ws.shape, lambda i: (0, 0, 0)), pl.BlockSpec(bs_t.shape, lambda i: (0, 0))],
        out_specs=_row_spec(ts, SGU_W),
        out_shape=jax.ShapeDtypeStruct((T, SGU_W), BF16),
        compiler_params=pltpu.CompilerParams(dimension_semantics=("parallel",)),
    )(zz, ln_g, ln_b, ws, bs_t)


def _sgu_gate_bwd(zz, dgated, ln_g, ln_b, ws, bs_t, name):
    T = zz.shape[0]
    ts = min(T, SGU_TILE)
    nt = T // ts

    def body(zz_ref, dg_ref, g_ref, b_ref, ws_ref, bs_ref, dzz_ref, dws_ref, dbs_ref, dlg_ref, dlb_ref, dlo_ref, dmx_ref):
        i = pl.program_id(0)

        @pl.when(i == 0)
        def _():
            dws_ref[...] = jnp.zeros_like(dws_ref)
            dmx_ref[...] = jnp.zeros_like(dmx_ref)
            dlg_ref[...] = jnp.zeros_like(dlg_ref)
            dlb_ref[...] = jnp.zeros_like(dlb_ref)

        zzv = zz_ref[...]
        z = _gelu(zzv)
        u = z[:, :SGU_W]
        xh, rstd, vn = _sgu_ln(z[:, SGU_W:], g_ref[...], b_ref[...])
        vb = vn.astype(BF16)
        dgv = dg_ref[...].astype(F32)
        tril = _tril_mask()
        for hh in range(SGU_H):
            wm = jnp.where(tril, ws_ref[hh], 0.0).astype(BF16)
            bcol = bs_ref[:, hh:hh + 1]
            cs = slice(hh * SGU_HD, (hh + 1) * SGU_HD)
            for j in range(ts // SGU_CHUNK):
                rs = slice(j * SGU_CHUNK, (j + 1) * SGU_CHUNK)
                mixed = jnp.dot(wm, vb[rs, cs], preferred_element_type=F32) + bcol
                dmixed = dgv[rs, cs] * u[rs, cs]
                dzz_ref[rs, cs] = (dgv[rs, cs] * mixed * _gelu_grad(zzv[rs, cs])).astype(BF16)
                dmb = dmixed.astype(BF16)
                dws_ref[hh] += lax.dot_general(dmb, vb[rs, cs], (((1,), (1,)), ((), ())), preferred_element_type=F32)
                dmx_ref[hh] += dmixed
                dlo_ref[rs, cs] = lax.dot_general(wm, dmb, (((0,), (0,)), ((), ())), preferred_element_type=F32)
        dlo = dlo_ref[...]
        dlg_ref[...] += _colsum(dlo * xh)
        dlb_ref[...] += _colsum(dlo)
        dxh = dlo * g_ref[...]
        dv = rstd * (dxh - jnp.mean(dxh, axis=-1, keepdims=True) - xh * jnp.mean(dxh * xh, axis=-1, keepdims=True))
        dzz_ref[:, SGU_W:] = (dv * _gelu_grad(zzv[:, SGU_W:])).astype(BF16)

        @pl.when(i == nt - 1)
        def _():
            tril_f = tril.astype(F32)
            for hh in range(SGU_H):
                dws_ref[hh] = dws_ref[hh] * tril_f
                dbs_ref[hh] = jnp.broadcast_to(jnp.sum(dmx_ref[hh], axis=-1, keepdims=True), (SGU_CHUNK, SGU_HD))

    full3 = pl.BlockSpec(ws.shape, lambda i: (0, 0, 0))
    return pl.pallas_call(
        body, name=name, grid=(nt,),
        in_specs=[_row_spec(ts, 2 * SGU_W), _row_spec(ts, SGU_W), _vec_spec(SGU_W), _vec_spec(SGU_W), full3,
                  pl.BlockSpec(bs_t.shape, lambda i: (0, 0))],
        out_specs=[_row_spec(ts, 2 * SGU_W), full3, full3, _vec_spec(SGU_W), _vec_spec(SGU_W)],
        out_shape=[jax.ShapeDtypeStruct((T, 2 * SGU_W), BF16), jax.ShapeDtypeStruct(ws.shape, F32), jax.ShapeDtypeStruct(ws.shape, F32),
                   jax.ShapeDtypeStruct((1, SGU_W), F32), jax.ShapeDtypeStruct((1, SGU_W), F32)],
        scratch_shapes=[pltpu.VMEM((ts, SGU_W), F32), pltpu.VMEM(ws.shape, F32)],
        compiler_params=pltpu.CompilerParams(dimension_semantics=("arbitrary",)),
    )(zz, dgated, ln_g, ln_b, ws, bs_t)


def _rope_fwd(blk, cc, sa, sb):
    return blk * cc + pltpu.roll(blk, 96, axis=1) * sa + pltpu.roll(blk, 32, axis=1) * sb


def _rope_bwd(d, cc, sa, sb):
    return d * cc + pltpu.roll(d * sa, 32, axis=1) + pltpu.roll(d * sb, 96, axis=1)


def _rms(v, g):
    r = lax.rsqrt(jnp.mean(v * v, axis=-1, keepdims=True) + RMS_EPS)
    vn = v * r
    return vn, r, vn * g


def _rms_bwd(dy, vn, r, g):
    dvn = dy * g
    return r * (dvn - vn * jnp.mean(dvn * vn, axis=-1, keepdims=True))


MLA_TILE = 256
_KV0 = MLA_QL
_KR0 = MLA_QL + MLA_KVL


def _mla_lat_fwd(lat, qg, kvg, cc, sa, sb, name):
    T = lat.shape[0]
    tr = min(T, ROW_TILE)

    def body(lat_ref, qg_ref, kvg_ref, cc_ref, sa_ref, sb_ref, cq_ref, ckv_ref, kr_ref):
        lv = lat_ref[...]
        cq_ref[...] = _rms(lv[:, :_KV0], qg_ref[...])[2].astype(BF16)
        ckv_ref[...] = _rms(lv[:, _KV0:_KR0], kvg_ref[...])[2].astype(BF16)
        kr_ref[...] = _rope_fwd(lv[:, _KR0:], cc_ref[...], sa_ref[...], sb_ref[...])

    return pl.pallas_call(
        body, name=name, grid=(T // tr,),
        in_specs=[_row_spec(tr, MLA_LATP), _vec_spec(MLA_QL), _vec_spec(MLA_KVL), _row_spec(tr, 128), _row_spec(tr, 128), _row_spec(tr, 128)],
        out_specs=[_row_spec(tr, MLA_QL), _row_spec(tr, MLA_KVL), _row_spec(tr, 128)],
        out_shape=[jax.ShapeDtypeStruct((T, MLA_QL), BF16), jax.ShapeDtypeStruct((T, MLA_KVL), BF16), jax.ShapeDtypeStruct((T, 128), F32)],
        compiler_params=pltpu.CompilerParams(dimension_semantics=("parallel",)),
    )(lat, qg, kvg, cc, sa, sb)


def _mla_lat_bwd(lat, dcqn, dckvn, dkrot, qg, kvg, cc, sa, sb, name):
    T = lat.shape[0]
    tr = min(T, ROW_TILE)

    def body(lat_ref, dcq_ref, dckv_ref, dkr_ref, qg_ref, kvg_ref, cc_ref, sa_ref, sb_ref, dlat_ref, dqg_ref, dkvg_ref):
        @pl.when(pl.program_id(0) == 0)
        def _():
            dqg_ref[...] = jnp.zeros_like(dqg_ref)
            dkvg_ref[...] = jnp.zeros_like(dkvg_ref)

        lv = lat_ref[...]
        qn, qr, _ = _rms(lv[:, :_KV0], qg_ref[...])
        kn, kr, _ = _rms(lv[:, _KV0:_KR0], kvg_ref[...])
        dcq = dcq_ref[...]
        dckv = dckv_ref[...]
        dqg_ref[...] += _colsum(dcq * qn)
        dkvg_ref[...] += _colsum(dckv * kn)
        dlat_ref[:, :_KV0] = _rms_bwd(dcq, qn, qr, qg_ref[...]).astype(BF16)
        dlat_ref[:, _KV0:_KR0] = _rms_bwd(dckv, kn, kr, kvg_ref[...]).astype(BF16)
        dlat_ref[:, _KR0:] = _rope_bwd(dkr_ref[...], cc_ref[...], sa_ref[...], sb_ref[...]).astype(BF16)

    return pl.pallas_call(
        body, name=name, grid=(T // tr,),
        in_specs=[_row_spec(tr, MLA_LATP), _row_spec(tr, MLA_QL), _row_spec(tr, MLA_KVL), _row_spec(tr, 128),
                  _vec_spec(MLA_QL), _vec_spec(MLA_KVL), _row_spec(tr, 128), _row_spec(tr, 128), _row_spec(tr, 128)],
        out_specs=[_row_spec(tr, MLA_LATP), _vec_spec(MLA_QL), _vec_spec(MLA_KVL)],
        out_shape=[jax.ShapeDtypeStruct((T, MLA_LATP), BF16), jax.ShapeDtypeStruct((1, MLA_QL), F32), jax.ShapeDtypeStruct((1, MLA_KVL), F32)],
        compiler_params=pltpu.CompilerParams(dimension_semantics=("arbitrary",)),
    )(lat, dcqn, dckvn, dkrot, qg, kvg, cc, sa, sb)


def _mla_prep(qpad, kv, krot, cc, sa, sb, name):
    T = qpad.shape[0]
    tr = min(T, MLA_TILE)
    HW = MLA_H * MLA_HP

    def body(q_ref, kv_ref, kr_ref, cc_ref, sa_ref, sb_ref, qo_ref, ko_ref, kt_ref, vo_ref, vt_ref):
        cc, sa, sb = cc_ref[...], sa_ref[...], sb_ref[...]
        kr = kr_ref[...]
        krb, krt = kr.astype(BF16), kr.T.astype(BF16)
        for hh in range(MLA_H):
            a, m, b = hh * MLA_HP, hh * MLA_HP + MLA_NOPE, (hh + 1) * MLA_HP
            qo_ref[:, a:m] = (q_ref[:, a:m] * SM_SCALE).astype(BF16)
            qo_ref[:, m:b] = (_rope_fwd(q_ref[:, m:b], cc, sa, sb) * SM_SCALE).astype(BF16)
            kn = kv_ref[:, a:m]
            ko_ref[:, a:m] = kn.astype(BF16)
            ko_ref[:, m:b] = krb
            kt_ref[a:m, :] = kn.T.astype(BF16)
            kt_ref[m:b, :] = krt
            vh = kv_ref[:, m:b]
            vo_ref[:, hh * MLA_V:(hh + 1) * MLA_V] = vh.astype(BF16)
            vt_ref[hh] = vh.T.astype(BF16)

    tk = min(T, ATT_TILE)
    per = tk // tr
    return pl.pallas_call(
        body, name=name, grid=(T // tr,),
        in_specs=[_row_spec(tr, HW), _row_spec(tr, HW), _row_spec(tr, 128), _row_spec(tr, 128), _row_spec(tr, 128), _row_spec(tr, 128)],
        out_specs=[_row_spec(tr, HW), _row_spec(tr, HW), pl.BlockSpec((HW, tr), lambda i: (0, i)), _row_spec(tr, MLA_H * MLA_V),
                   pl.BlockSpec((MLA_H, None, MLA_V, tr), lambda i: (0, i // per, 0, i % per))],
        out_shape=[jax.ShapeDtypeStruct((T, HW), BF16), jax.ShapeDtypeStruct((T, HW), BF16), jax.ShapeDtypeStruct((HW, T), BF16),
                   jax.ShapeDtypeStruct((T, MLA_H * MLA_V), BF16), jax.ShapeDtypeStruct((MLA_H, T // tk, MLA_V, tk), BF16)],
        compiler_params=pltpu.CompilerParams(dimension_semantics=("parallel",)),
    )(qpad, kv, krot, cc, sa, sb)


ATT_HG = 4


def _mla_prep_bwd(dqt, dk, dv, cc, sa, sb, name):
    _, nq, _, tq = dqt.shape
    T = nq * tq
    gw = ATT_HG * MLA_HP

    def body(dq_ref, dk_ref, dv_ref, cc_ref, sa_ref, sb_ref, dqp_ref, dkv_ref, dkr_ref):
        @pl.when(pl.program_id(1) == 0)
        def _():
            dkr_ref[...] = jnp.zeros_like(dkr_ref)

        cc, sa, sb = cc_ref[...], sa_ref[...], sb_ref[...]
        acc = jnp.zeros((tq, 128), F32)
        for hh in range(ATT_HG):
            a, m, b = hh * MLA_HP, hh * MLA_HP + MLA_NOPE, (hh + 1) * MLA_HP
            dqh = dq_ref[hh].astype(F32).T * SM_SCALE
            dqp_ref[:, a:m] = dqh[:, :MLA_NOPE].astype(BF16)
            dqp_ref[:, m:b] = _rope_bwd(dqh[:, MLA_NOPE:], cc, sa, sb).astype(BF16)
            dkv_ref[:, a:m] = dk_ref[:, a:m]
            dkv_ref[:, m:b] = dv_ref[:, hh * MLA_V:(hh + 1) * MLA_V]
            acc = acc + dk_ref[:, m:b].astype(F32)
        dkr_ref[...] += acc

    tab = pl.BlockSpec((tq, 128), lambda i, g: (i, 0))
    return pl.pallas_call(
        body, name=name, grid=(nq, MLA_H // ATT_HG),
        in_specs=[pl.BlockSpec((ATT_HG, None, MLA_HP, tq), lambda i, g: (g, i, 0, 0)), pl.BlockSpec((tq, gw), lambda i, g: (i, g)),
                  pl.BlockSpec((tq, ATT_HG * MLA_V), lambda i, g: (i, g)), tab, tab, tab],
        out_specs=[pl.BlockSpec((tq, gw), lambda i, g: (i, g)), pl.BlockSpec((tq, gw), lambda i, g: (i, g)), tab],
        out_shape=[jax.ShapeDtypeStruct((T, MLA_H * MLA_HP), BF16), jax.ShapeDtypeStruct((T, MLA_H * MLA_HP), BF16), jax.ShapeDtypeStruct((T, 128), F32)],
        compiler_params=pltpu.CompilerParams(dimension_semantics=("parallel", "arbitrary")),
    )(dqt, dk, dv, cc, sa, sb)


_NT = (((1,), (1,)), ((), ()))


def _as_row(col, n):
    return jnp.broadcast_to(col, (n, 128)).T[0:1, :]


def _attn_fwd(q, k, vt, name):
    T = q.shape[0]
    tq = tk = min(T, ATT_TILE)
    nq = T // tq
    hg = ATT_FWD_HEADS

    def body(q_ref, k_ref, vt_ref, o_ref, lse_ref, m_ref, l_ref, acc_ref):
        i = pl.program_id(1)
        m_ref[...] = jnp.full_like(m_ref, NEG_INF)
        l_ref[...] = jnp.zeros_like(l_ref)
        acc_ref[...] = jnp.zeros_like(acc_ref)

        def step(j, diag):
            off = pl.multiple_of(j * tk, tk)
            sts = [lax.dot_general(k_ref[pl.ds(off, tk), hh * MLA_HP:(hh + 1) * MLA_HP], q_ref[:, hh * MLA_HP:(hh + 1) * MLA_HP], _NT,
                                   preferred_element_type=F32) for hh in range(hg)]
            for hh in range(hg):
                st = sts[hh]
                if diag:
                    st = jnp.where(lax.broadcasted_iota(jnp.int32, (tk, tq), 0) <= lax.broadcasted_iota(jnp.int32, (tk, tq), 1), st, NEG_INF)
                m_prev = m_ref[hh]
                m_new = jnp.maximum(m_prev, jnp.max(st, axis=0, keepdims=True))
                alpha = jnp.exp(m_prev - m_new)
                pt = jnp.exp(st - m_new)
                l_ref[hh] = alpha * l_ref[hh] + jnp.sum(pt, axis=0, keepdims=True)
                acc_ref[hh] = alpha * acc_ref[hh] + jnp.dot(vt_ref[hh, j], pt.astype(BF16), preferred_element_type=F32)
                m_ref[hh] = m_new

        def loop_body(j, carry):
            step(j, False)
            return carry

        lax.fori_loop(0, i, loop_body, 0)
        step(i, True)
        for hh in range(hg):
            o_ref[:, hh * MLA_V:(hh + 1) * MLA_V] = (acc_ref[hh] / l_ref[hh]).T.astype(BF16)
            lse_ref[hh] = m_ref[hh] + jnp.log(l_ref[hh])

    return pl.pallas_call(
        body, name=name, grid=(MLA_H // hg, nq),
        in_specs=[pl.BlockSpec((tq, hg * MLA_HP), lambda h, i: (i, h)), pl.BlockSpec((T, hg * MLA_HP), lambda h, i: (0, h)),
                  pl.BlockSpec((hg, nq, MLA_V, tk), lambda h, i: (h, 0, 0, 0))],
        out_specs=[pl.BlockSpec((tq, hg * MLA_V), lambda h, i: (i, h)), pl.BlockSpec((hg, None, 1, tq), lambda h, i: (h, i, 0, 0))],
        out_shape=[jax.ShapeDtypeStruct((T, MLA_H * MLA_V), BF16), jax.ShapeDtypeStruct((MLA_H, nq, 1, tq), F32)],
        scratch_shapes=[pltpu.VMEM((hg, 1, tq), F32), pltpu.VMEM((hg, 1, tq), F32), pltpu.VMEM((hg, MLA_V, tq), F32)],
        compiler_params=pltpu.CompilerParams(dimension_semantics=("parallel", "arbitrary")),
    )(q, k, vt)


def _attn_delta(do, o, name):
    T = do.shape[0]
    tq = min(T, ATT_TILE)

    def body(do_ref, o_ref, d_ref):
        for hh in range(MLA_H):
            cs = slice(hh * MLA_V, (hh + 1) * MLA_V)
            s = jnp.sum(do_ref[:, cs].astype(F32) * o_ref[:, cs].astype(F32), axis=-1, keepdims=True)
            d_ref[hh] = _as_row(s, tq)

    return pl.pallas_call(
        body, name=name, grid=(T // tq,),
        in_specs=[_row_spec(tq, MLA_H * MLA_V), _row_spec(tq, MLA_H * MLA_V)],
        out_specs=pl.BlockSpec((MLA_H, None, 1, tq), lambda i: (0, i, 0, 0)),
        out_shape=jax.ShapeDtypeStruct((MLA_H, T // tq, 1, tq), F32),
        compiler_params=pltpu.CompilerParams(dimension_semantics=("parallel",)),
    )(do, o)


def _attn_bwd(q, k, kt, v, do, lse, delta, name):
    T = q.shape[0]
    tq = tk = min(T, ATT_TILE)
    nq = nk = T // tq
    tsd = min(tq, ATT_SUB)
    hg = ATT_BWD_HEADS

    def body(q_ref, k_ref, kt_ref, v_ref, do_ref, lse_ref, dl_ref, dqt_ref, dk_ref, dv_ref, dq_acc, dk_acc, dv_acc):
        j = pl.program_id(1)

        @pl.when(j == 0)
        def _():
            dq_acc[...] = jnp.zeros_like(dq_acc)

        dk_acc[...] = jnp.zeros_like(dk_acc)
        dv_acc[...] = jnp.zeros_like(dv_acc)

        def step(i, diag):
            off = pl.multiple_of(i * tq, tq)
            ts, nsub = (tsd, tq // tsd) if diag else (tq, 1)
            for u in range(nsub):
                cols = slice(u * ts, (u + 1) * ts)
                nk_u = (u + 1) * ts if diag else tk
                rows = pl.ds(off + u * ts, ts)
                pre = []
                for hh in range(hg):
                    hq, hv = slice(hh * MLA_HP, (hh + 1) * MLA_HP), slice(hh * MLA_V, (hh + 1) * MLA_V)
                    qi, doi = q_ref[rows, hq], do_ref[rows, hv]
                    st = lax.dot_general(k_ref[:nk_u, hq], qi, _NT, preferred_element_type=F32)
                    dpt = lax.dot_general(v_ref[:nk_u, hv], doi, _NT, preferred_element_type=F32)
                    pre.append((qi, doi, st, dpt))
                for hh in range(hg):
                    hq, hv = slice(hh * MLA_HP, (hh + 1) * MLA_HP), slice(hh * MLA_V, (hh + 1) * MLA_V)
                    qi, doi, st, dpt = pre[hh]
                    if diag:
                        qcol = u * ts + lax.broadcasted_iota(jnp.int32, (nk_u, ts), 1)
                        st = jnp.where(lax.broadcasted_iota(jnp.int32, (nk_u, ts), 0) <= qcol, st, NEG_INF)
                    pt = jnp.exp(st - lse_ref[hh, i][:, cols])
                    dv_acc[:nk_u, hv] += jnp.dot(pt.astype(BF16), doi, preferred_element_type=F32)
                    dsb = (pt * (dpt - dl_ref[hh, i][:, cols])).astype(BF16)
                    dk_acc[:nk_u, hq] += jnp.dot(dsb, qi, preferred_element_type=F32)
                    dq_acc[hh, i, :, cols] += jnp.dot(kt_ref[hq, :nk_u], dsb, preferred_element_type=F32)

        def loop_body(i, carry):
            step(i, False)
            return carry

        step(j, True)
        lax.fori_loop(j + 1, nq, loop_body, 0)
        dk_ref[...] = dk_acc[...].astype(BF16)
        dv_ref[...] = dv_acc[...].astype(BF16)

        @pl.when(j == nk - 1)
        def _():
            dqt_ref[...] = dq_acc[...].astype(BF16)

    stat = pl.BlockSpec((hg, nq, 1, tq), lambda h, j: (h, 0, 0, 0))
    return pl.pallas_call(
        body, name=name, grid=(MLA_H // hg, nk),
        in_specs=[pl.BlockSpec((T, hg * MLA_HP), lambda h, j: (0, h)), pl.BlockSpec((tk, hg * MLA_HP), lambda h, j: (j, h)),
                  pl.BlockSpec((hg * MLA_HP, tk), lambda h, j: (h, j)), pl.BlockSpec((tk, hg * MLA_V), lambda h, j: (j, h)),
                  pl.BlockSpec((T, hg * MLA_V), lambda h, j: (0, h)), stat, stat],
        out_specs=[pl.BlockSpec((hg, nq, MLA_HP, tq), lambda h, j: (h, 0, 0, 0)), pl.BlockSpec((tk, hg * MLA_HP), lambda h, j: (j, h)),
                   pl.BlockSpec((tk, hg * MLA_V), lambda h, j: (j, h))],
        out_shape=[jax.ShapeDtypeStruct((MLA_H, nq, MLA_HP, tq), BF16), jax.ShapeDtypeStruct((T, MLA_H * MLA_HP), BF16),
                   jax.ShapeDtypeStruct((T, MLA_H * MLA_V), BF16)],
        scratch_shapes=[pltpu.VMEM((hg, nq, MLA_HP, tq), F32), pltpu.VMEM((tk, hg * MLA_HP), F32), pltpu.VMEM((tk, hg * MLA_V), F32)],
        compiler_params=pltpu.CompilerParams(dimension_semantics=("parallel", "arbitrary")),
    )(q, k, kt, v, do, lse, delta)


ADA_TN = 512


def _silu(v):
    return v * (1.0 / (1.0 + jnp.exp(-v)))


def _ada_fwd(c_all, ada_w, ada_b_loc, name):
    L, D, Nc = ada_w.shape
    B = c_all.shape[0]

    def body(c_ref, w_ref, b_ref, o_ref):
        ca = _silu(c_ref[...]).astype(BF16)
        o_ref[...] = jnp.dot(ca, w_ref[...].astype(BF16), preferred_element_type=F32) + b_ref[...]

    return pl.pallas_call(
        body, name=name, grid=(L, Nc // ADA_TN),
        in_specs=[pl.BlockSpec((B, D), lambda l, n: (0, 0)), pl.BlockSpec((None, D, ADA_TN), lambda l, n: (l, 0, n)),
                  pl.BlockSpec((None, 1, ADA_TN), lambda l, n: (l, 0, n))],
        out_specs=pl.BlockSpec((None, B, ADA_TN), lambda l, n: (l, 0, n)),
        out_shape=jax.ShapeDtypeStruct((L, B, Nc), F32),
        compiler_params=pltpu.CompilerParams(dimension_semantics=("parallel", "parallel")),
    )(c_all, ada_w, ada_b_loc)


def _ada_bwd(c_all_t, dmod_loc, name):
    D, B = c_all_t.shape
    L, _, Nc = dmod_loc.shape

    def body(c_ref, d_ref, o_ref):
        ca = _silu(c_ref[...])
        dv = d_ref[...]
        acc = ca[:, 0:1] * dv[0:1, :]
        for b in range(1, B):
            acc = acc + ca[:, b:b + 1] * dv[b:b + 1, :]
        o_ref[...] = acc

    return pl.pallas_call(
        body, name=name, grid=(L, Nc // ADA_TN),
        in_specs=[pl.BlockSpec((D, B), lambda l, n: (0, 0)), pl.BlockSpec((None, B, ADA_TN), lambda l, n: (l, 0, n))],
        out_specs=pl.BlockSpec((None, D, ADA_TN), lambda l, n: (l, 0, n)),
        out_shape=jax.ShapeDtypeStruct((L, D, Nc), F32),
        compiler_params=pltpu.CompilerParams(dimension_semantics=("parallel", "parallel")),
    )(c_all_t, dmod_loc)


def _sum_lead(parts, name, out_dtype=F32):
    R, C = parts[0].shape[1:]
    n_tot = sum(p.shape[0] for p in parts)
    tr = R
    for cand in (512, 256, 128, 64, 32, 16):
        if R % cand == 0 and cand * C * 4 * n_tot <= (8 << 20):
            tr = cand
            break

    def body(*refs):
        o_ref = refs[-1]
        acc = None
        for r in refs[:-1]:
            for s in range(r.shape[0]):
                acc = r[s].astype(F32) if acc is None else acc + r[s].astype(F32)
        o_ref[...] = acc.astype(o_ref.dtype)

    return pl.pallas_call(
        body, name=name, grid=(R // tr,),
        in_specs=[pl.BlockSpec((p.shape[0], tr, C), lambda i: (0, i, 0)) for p in parts],
        out_specs=pl.BlockSpec((tr, C), lambda i: (i, 0)),
        out_shape=jax.ShapeDtypeStruct((R, C), out_dtype),
        compiler_params=pltpu.CompilerParams(dimension_semantics=("parallel",)),
    )(*parts)


_ADAM_C1 = 1.0 - ADAM_B1 ** ADAM_STEP
_ADAM_C2 = 1.0 - ADAM_B2 ** ADAM_STEP


def _adamw(w, g, m, v, name):
    shape = w.shape
    C = shape[-1]
    R = math.prod(shape[:-1]) if len(shape) > 1 else 1
    w2, g2, m2, v2 = (a.reshape(R, C) for a in (w, g, m, v))
    tr = R
    for cand in (1024, 512, 256, 128, 64, 32, 16, 8):
        if R % cand == 0 and cand * C * 4 <= (1 << 20):
            tr = cand
            break

    def body(w_ref, g_ref, m_ref, v_ref, d_ref, nm_ref, nv_ref):
        gv = g_ref[...]
        mn = ADAM_B1 * m_ref[...] + (1.0 - ADAM_B1) * gv
        vn = ADAM_B2 * v_ref[...] + (1.0 - ADAM_B2) * (gv * gv)
        nm_ref[...] = mn
        nv_ref[...] = vn
        m_hat = mn / _ADAM_C1
        v_hat = vn / _ADAM_C2
        d_ref[...] = -ADAM_LR * (m_hat / (jnp.sqrt(v_hat) + ADAM_EPS) + ADAM_WD * w_ref[...])

    spec = pl.BlockSpec((tr, C), lambda i: (i, 0))
    outs = pl.pallas_call(
        body, name=name, grid=(R // tr,),
        in_specs=[spec] * 4, out_specs=[spec] * 3,
        out_shape=[jax.ShapeDtypeStruct((R, C), F32)] * 3,
        compiler_params=pltpu.CompilerParams(dimension_semantics=("parallel",)),
    )(w2, g2, m2, v2)
    return tuple(o.reshape(shape) for o in outs)


def _row_tile(rows, cols, itemsize, budget):
    for cand in (1024, 512, 256, 128, 64, 32, 16):
        if rows % cand == 0 and cand * cols * itemsize <= budget:
            return cand
    return rows


def _sum_sel(sel, stacked, others, name, out_dtype):
    R, C = stacked.shape[1:]
    n_tot = 1 + sum(o.shape[0] for o in others)
    tr = _row_tile(R, C, 4 * n_tot, 8 << 20)

    def body(sel_ref, s_ref, *refs):
        o_ref = refs[-1]
        acc = s_ref[...].astype(F32)
        for r in refs[:-1]:
            for s in range(r.shape[0]):
                acc = acc + r[s].astype(F32)
        o_ref[...] = acc.astype(o_ref.dtype)

    return pl.pallas_call(
        body, name=name,
        grid_spec=pltpu.PrefetchScalarGridSpec(
            num_scalar_prefetch=1, grid=(R // tr,),
            in_specs=[pl.BlockSpec((None, tr, C), lambda i, s: (s[0], i, 0))] + [pl.BlockSpec((o.shape[0], tr, C), lambda i, s: (0, i, 0)) for o in others],
            out_specs=pl.BlockSpec((tr, C), lambda i, s: (i, 0))),
        out_shape=jax.ShapeDtypeStruct((R, C), out_dtype),
        compiler_params=pltpu.CompilerParams(dimension_semantics=("parallel",)),
    )(sel, stacked, *others)


def _adamw_piece(cidx, w2, m2, v2, mine, got, bufs, row0, name):
    hr, C = mine.shape
    tr = _row_tile(math.gcd(hr, row0) if row0 else hr, C, 4, 1 << 20)
    nt = hr // tr

    def body(c_ref, w_ref, m_ref, v_ref, a_ref, b_ref, _g, _d, _nm, _nv, g_ref, d_ref, nm_ref, nv_ref):
        gv = jnp.where(pl.program_id(0) == c_ref[0], a_ref[...], b_ref[...])
        mn = ADAM_B1 * m_ref[...] + (1.0 - ADAM_B1) * gv
        vn = ADAM_B2 * v_ref[...] + (1.0 - ADAM_B2) * (gv * gv)
        g_ref[...] = gv
        nm_ref[...] = mn
        nv_ref[...] = vn
        d_ref[...] = -ADAM_LR * ((mn / _ADAM_C1) / (jnp.sqrt(vn / _ADAM_C2) + ADAM_EPS) + ADAM_WD * w_ref[...])

    rows = pl.BlockSpec((tr, C), lambda hf, t, c: (row0 // tr + hf * nt + t, 0))
    half = pl.BlockSpec((tr, C), lambda hf, t, c: (t, 0))
    return pl.pallas_call(
        body, name=name,
        grid_spec=pltpu.PrefetchScalarGridSpec(num_scalar_prefetch=1, grid=(2, nt), in_specs=[rows] * 3 + [half] * 2 + [_ANY_SPEC] * 4,
                                               out_specs=[rows] * 4),
        out_shape=[jax.ShapeDtypeStruct(w2.shape, F32)] * 4,
        input_output_aliases={6 + n: n for n in range(4)},
        compiler_params=pltpu.CompilerParams(dimension_semantics=("parallel", "parallel")),
    )(cidx, w2, m2, v2, mine, got, *bufs)


_VMEM_SPEC = pl.BlockSpec(memory_space=pltpu.VMEM)
_HBM_SPEC = pl.BlockSpec(memory_space=pltpu.HBM)


def _flip(v, bit):
    return (1 - v) if bit else v


def _allgather8(v, name):
    def body(v_ref, out_ref, send_sems, recv_sems, local_sem):
        x, y, c = _idx()
        me = 4 * x + 2 * y + c
        mine = pltpu.make_async_copy(v_ref, out_ref.at[me], local_sem)
        mine.start()
        sends = []
        for k in range(1, N_DEV):
            peer = (_flip(x, k & 4), _flip(y, k & 2), _flip(c, k & 1))
            cp = pltpu.make_async_remote_copy(src_ref=v_ref, dst_ref=out_ref.at[me], send_sem=send_sems.at[k - 1], recv_sem=recv_sems.at[k - 1],
                                              device_id=peer, device_id_type=MESH)
            cp.start()
            sends.append(cp)
        for k in range(1, N_DEV):
            px, py, pc = _flip(x, k & 4), _flip(y, k & 2), _flip(c, k & 1)
            src = 4 * px + 2 * py + pc
            pltpu.make_async_remote_copy(src_ref=v_ref, dst_ref=out_ref.at[src], send_sem=send_sems.at[k - 1], recv_sem=recv_sems.at[k - 1],
                                         device_id=(px, py, pc), device_id_type=MESH).wait_recv()
        for cp in sends:
            cp.wait_send()
        mine.wait()

    return pl.pallas_call(
        body, name=name,
        out_shape=jax.ShapeDtypeStruct((N_DEV, *v.shape), v.dtype),
        in_specs=[_VMEM_SPEC], out_specs=_VMEM_SPEC,
        scratch_shapes=[pltpu.SemaphoreType.DMA((N_DEV - 1,)), pltpu.SemaphoreType.DMA((N_DEV - 1,)), pltpu.SemaphoreType.DMA],
    )(v)


def _mod_exchange(modp, name):
    _, L, Nc = modp.shape

    def body(p_ref, out_ref, send_sems, recv_sems, local_sem):
        x, y, c = _idx()
        me, chip = 4 * x + 2 * y + c, 2 * x + y
        mine = pltpu.make_async_copy(p_ref.at[me], out_ref.at[chip], local_sem)
        mine.start()
        sends = []
        for k in range(1, N_CHIPS):
            px, py = _flip(x, k & 2), _flip(y, k & 1)
            cp = pltpu.make_async_remote_copy(src_ref=p_ref.at[4 * px + 2 * py + c], dst_ref=out_ref.at[chip],
                                              send_sem=send_sems.at[k - 1], recv_sem=recv_sems.at[k - 1], device_id=(px, py, c), device_id_type=MESH)
            cp.start()
            sends.append(cp)
        for k in range(1, N_CHIPS):
            px, py = _flip(x, k & 2), _flip(y, k & 1)
            pltpu.make_async_remote_copy(src_ref=p_ref.at[me], dst_ref=out_ref.at[2 * px + py], send_sem=send_sems.at[k - 1],
                                         recv_sem=recv_sems.at[k - 1], device_id=(px, py, c), device_id_type=MESH).wait_recv()
        for cp in sends:
            cp.wait_send()
        mine.wait()

    return pl.pallas_call(
        body, name=name,
        out_shape=jax.ShapeDtypeStruct((N_CHIPS, L, Nc), modp.dtype),
        in_specs=[_VMEM_SPEC], out_specs=_VMEM_SPEC,
        scratch_shapes=[pltpu.SemaphoreType.DMA((N_CHIPS - 1,)), pltpu.SemaphoreType.DMA((N_CHIPS - 1,)), pltpu.SemaphoreType.DMA],
    )(modp)


_SEM_SPEC = pl.BlockSpec(memory_space=pltpu.SEMAPHORE)
_ANY_SPEC = pl.BlockSpec(memory_space=pl.ANY)
_EFFECT = pltpu.SideEffectType.DATAFLOW_SIDE_EFFECTING


def _hbm(a):
    return pltpu.with_memory_space_constraint(a, pltpu.HBM)


def _xchip_copies(mode, srcs, lands, send_sems, recv_sems, waiting):
    x, y, c = _idx()
    chip = 2 * x + y
    out = []
    for a in range(len(srcs)):
        for k in range(1, N_CHIPS):
            px, py = _flip(x, k & 2), _flip(y, k & 1)
            peer = 2 * px + py
            if mode == "gather":
                src, dst, mine = srcs[a].at[c], lands[a].at[chip, c], lands[a].at[peer, c]
            else:
                src, dst, mine = srcs[a].at[peer], lands[a].at[k - 1], lands[a].at[k - 1]
            q = a * (N_CHIPS - 1) + k - 1
            out.append(pltpu.make_async_remote_copy(src_ref=src, dst_ref=mine if waiting else dst, send_sem=send_sems[q], recv_sem=recv_sems[q],
                                                    device_id=(px, py, c), device_id_type=MESH))
    return out


def _xchip_start(mode, srcs, land_shapes, dep, name):
    n = len(srcs)
    ns = n * (N_CHIPS - 1)

    def body(*refs):
        src_refs, land_refs = refs[:n], refs[n:2 * n]
        outs = refs[2 * n + 1:]
        for cp in _xchip_copies(mode, src_refs, land_refs, outs[:ns], outs[ns:2 * ns], waiting=False):
            cp.start()
        outs[-1][...] = jnp.zeros_like(outs[-1])

    lands = [_hbm(lax.empty(s.shape, s.dtype)) for s in land_shapes]
    outs = pl.pallas_call(
        body, name=name,
        out_shape=(*[pltpu.SemaphoreType.DMA(())] * (2 * ns), *[pltpu.HBM(s.shape, s.dtype) for s in srcs],
                   *[pltpu.HBM(s.shape, s.dtype) for s in land_shapes], jax.ShapeDtypeStruct((8, 128), F32)),
        in_specs=[_HBM_SPEC] * (2 * n) + [_ANY_SPEC],
        out_specs=(*[_SEM_SPEC] * (2 * ns), *[_HBM_SPEC] * (2 * n), _VMEM_SPEC),
        input_output_aliases={i: 2 * ns + i for i in range(2 * n)},
        compiler_params=pltpu.CompilerParams(has_side_effects=_EFFECT),
    )(*[_hbm(s) for s in srcs], *lands, dep)
    return list(outs[:ns]), list(outs[ns:2 * ns]), list(outs[2 * ns:2 * ns + n]), list(outs[2 * ns + n:2 * ns + 2 * n]), outs[-1]


def _xchip_wait(mode, send_sems, recv_sems, srcs, lands, after, name):
    n = len(srcs)
    ns = n * (N_CHIPS - 1)

    def body(*refs):
        src_refs, land_refs = refs[:n], refs[n:2 * n]
        sems = refs[2 * n:2 * n + 2 * ns]
        for cp in _xchip_copies(mode, src_refs, land_refs, sems[:ns], sems[ns:], waiting=True):
            cp.wait_send()
            cp.wait_recv()

    outs = pl.pallas_call(
        body, name=name,
        out_shape=(*[pltpu.HBM(s.shape, s.dtype) for s in srcs], *[pltpu.HBM(s.shape, s.dtype) for s in lands]),
        in_specs=[_HBM_SPEC] * (2 * n) + [_SEM_SPEC] * (2 * ns) + [_ANY_SPEC] * len(after),
        out_specs=tuple([_HBM_SPEC] * (2 * n)),
        input_output_aliases={i: i for i in range(2 * n)},
        compiler_params=pltpu.CompilerParams(has_side_effects=_EFFECT),
    )(*srcs, *lands, *send_sems, *recv_sems, *after)
    return list(outs[:n]), list(outs[n:])


def _sibling_fwd(lands, name):
    n = len(lands)

    def body(*refs):
        outs = refs[n:2 * n]
        send_sems, recv_sems = refs[2 * n:]
        x, y, c = _idx()
        sib = (x, y, 1 - c)
        sends = []
        for a in range(n):
            for k in range(1, N_CHIPS):
                src = 2 * _flip(x, k & 2) + _flip(y, k & 1)
                cp = pltpu.make_async_remote_copy(src_ref=outs[a].at[src, c], dst_ref=outs[a].at[src, c], send_sem=send_sems.at[a, k - 1],
                                                  recv_sem=recv_sems.at[a, k - 1], device_id=sib, device_id_type=MESH)
                cp.start()
                sends.append(cp)
        for a in range(n):
            for k in range(1, N_CHIPS):
                src = 2 * _flip(x, k & 2) + _flip(y, k & 1)
                pltpu.make_async_remote_copy(src_ref=outs[a].at[src, c], dst_ref=outs[a].at[src, 1 - c], send_sem=send_sems.at[a, k - 1],
                                             recv_sem=recv_sems.at[a, k - 1], device_id=sib, device_id_type=MESH).wait_recv()
        for cp in sends:
            cp.wait_send()

    return pl.pallas_call(
        body, name=name,
        out_shape=[jax.ShapeDtypeStruct(s.shape, s.dtype) for s in lands],
        in_specs=[_HBM_SPEC] * n, out_specs=[_HBM_SPEC] * n,
        input_output_aliases={i: i for i in range(n)},
        scratch_shapes=[pltpu.SemaphoreType.DMA((n, N_CHIPS - 1)), pltpu.SemaphoreType.DMA((n, N_CHIPS - 1))],
    )(*lands)


def _sibling_swap(parts, name):
    n = len(parts)

    def body(*refs):
        ins, outs = refs[:n], refs[n:2 * n]
        send_sems, recv_sems = refs[2 * n:]
        x, y, c = _idx()
        cps = []
        for a in range(n):
            cp = pltpu.make_async_remote_copy(src_ref=ins[a].at[1 - c], dst_ref=outs[a], send_sem=send_sems.at[a], recv_sem=recv_sems.at[a],
                                              device_id=(x, y, 1 - c), device_id_type=MESH)
            cp.start()
            cps.append(cp)
        for cp in cps:
            cp.wait()

    return pl.pallas_call(
        body, name=name,
        out_shape=[jax.ShapeDtypeStruct(p.shape[1:], p.dtype) for p in parts],
        in_specs=[_HBM_SPEC] * n, out_specs=[_HBM_SPEC] * n,
        scratch_shapes=[pltpu.SemaphoreType.DMA((n,)), pltpu.SemaphoreType.DMA((n,))],
    )(*parts)


def _sibling_send(halves, name):
    n = len(halves)

    def body(*refs):
        ins, outs = refs[:n], refs[n:2 * n]
        send_sems, recv_sems = refs[2 * n:]
        x, y, c = _idx()
        cps = []
        for a in range(n):
            cp = pltpu.make_async_remote_copy(src_ref=ins[a], dst_ref=outs[a], send_sem=send_sems.at[a], recv_sem=recv_sems.at[a],
                                              device_id=(x, y, 1 - c), device_id_type=MESH)
            cp.start()
            cps.append(cp)
        for cp in cps:
            cp.wait()

    return pl.pallas_call(
        body, name=name,
        out_shape=[jax.ShapeDtypeStruct(h.shape, h.dtype) for h in halves],
        in_specs=[_HBM_SPEC] * n, out_specs=[_HBM_SPEC] * n,
        scratch_shapes=[pltpu.SemaphoreType.DMA((n,)), pltpu.SemaphoreType.DMA((n,))],
    )(*halves)


def _col_full(g):
    k, n = g.shape[1], g.shape[2]
    return g.transpose(1, 0, 2).reshape(k, N_CHIPS * n)


def _col_blocks(w):
    k, n = w.shape
    return w.reshape(k, N_CHIPS, n // N_CHIPS).transpose(1, 0, 2)


def _row_blocks(w):
    k, n = w.shape
    return w.reshape(N_CHIPS, k // N_CHIPS, n)


_UQ_HEAD = MLA_NOPE + MLA_ROPE

_LAT = MLA_QL + MLA_KVL + MLA_ROPE
_POOL_R = len(POOL_WINDOWS) * (POOL_GD // N_CHIPS)

_PIECE_KINDS = {
    "mlp_w1": (D_MODEL, D_MODEL, _col_full, _col_blocks),
    "mlp_w2": (D_MODEL, D_MODEL, lambda g: g.reshape(4 * D_MODEL, D_MODEL), _row_blocks),
    "pool_w": (_POOL_R, POOL_GD,
               lambda g: g.reshape(N_CHIPS, len(POOL_WINDOWS), POOL_GD // N_CHIPS, POOL_GD).transpose(1, 0, 2, 3).reshape(len(POOL_WINDOWS), POOL_GD, POOL_GD),
               lambda w: w.reshape(len(POOL_WINDOWS), N_CHIPS, POOL_GD // N_CHIPS, POOL_GD).transpose(1, 0, 2, 3).reshape(N_CHIPS, _POOL_R, POOL_GD)),
    "sgu_w_in": (D_MODEL, 2 * SGU_W // N_CHIPS, _col_full, _col_blocks),
    "sgu_w_out": (SGU_W // N_CHIPS, D_MODEL, lambda g: g.reshape(SGU_W, D_MODEL), _row_blocks),
    "mla_w_dq_dkv": (D_MODEL // N_CHIPS, _LAT, lambda g: jnp.pad(g.reshape(D_MODEL, _LAT), ((0, 0), (0, MLA_LATP - _LAT))),
                     lambda w: _row_blocks(w[:, :_LAT])),
    "mla_w_uq": (MLA_QL, MLA_H * _UQ_HEAD // N_CHIPS,
                 lambda g: jnp.pad(_col_full(g).reshape(MLA_QL, MLA_H, _UQ_HEAD), ((0, 0), (0, 0), (0, MLA_HP - _UQ_HEAD))).reshape(MLA_QL, MLA_H * MLA_HP),
                 lambda w: _col_blocks(w.reshape(MLA_QL, MLA_H, MLA_HP)[:, :, :_UQ_HEAD].reshape(MLA_QL, MLA_H * _UQ_HEAD))),
    "mla_w_ukv": (MLA_KVL, MLA_H * (MLA_NOPE + MLA_V) // N_CHIPS, _col_full, _col_blocks),
    "mla_w_o": (MLA_H * MLA_V // N_CHIPS, D_MODEL, lambda g: g.reshape(MLA_H * MLA_V, D_MODEL), _row_blocks),
}
_MIXER_KINDS = (("pool_w",), ("sgu_w_in", "sgu_w_out"), ("mla_w_dq_dkv", "mla_w_uq", "mla_w_ukv", "mla_w_o"))


def _layer_pieces(i):
    return [(k, i // N_MIXERS) for k in _MIXER_KINDS[i % N_MIXERS]] + [("mlp_w1", i), ("mlp_w2", i)]


def _rope_tables(positions):
    inv_freq = ROPE_THETA ** (-jnp.arange(0, MLA_ROPE, 2, dtype=F32) / MLA_ROPE)
    ang = positions.astype(F32)[:, None] * inv_freq
    cos, sin = jnp.cos(ang), jnp.sin(ang)
    z32, z64 = jnp.zeros_like(cos), jnp.zeros((positions.shape[0], 64), F32)
    return (jnp.concatenate([cos, cos, z64], axis=1), jnp.concatenate([-sin, z32, z64], axis=1), jnp.concatenate([z32, sin, z64], axis=1))


def _local_step(x, positions, target, mod, S, weights_of, grads_of):
    D = D_MODEL
    cc, sa, sb = _rope_tables(positions)
    saved = []
    for i in range(DEPTH):
        sh1, sc1, g1, sh2, sc2, g2 = (mod[i:i + 1, n * D:(n + 1) * D] for n in range(6))
        kind, j = i % N_MIXERS, i // N_MIXERS
        gmix, gmlp = S["norm_mix_g"][i:i + 1], S["norm_mlp_g"][i:i + 1]
        W = weights_of(i, "mix", x)
        gmix = gmix + W["_tok"]
        st = {"x": x}
        if kind == 0:
            h = _norm_mod_fwd(x, gmix, sc1, sh1, F32, f"l{i}_norm1")
            x2, pooled, ypre = _pool_fwd(h, W["pool_w"], S["pool_scale"][j:j + 1], x, g1, f"l{i}_pool")
            st.update(pooled=pooled, y=ypre)
        elif kind == 1:
            h = _norm_mod_fwd(x, gmix, sc1, sh1, BF16, f"l{i}_norm1")
            zz = _mm(h, W["sgu_w_in"], out_dtypes=(F32,), name=f"l{i}_sgu_in")
            bs_t = S["sgu_b_s"].T
            gated = _sgu_gate_fwd(zz, S["sgu_ln_g"], S["sgu_ln_b"], S["sgu_w_s"], bs_t, f"l{i}_sgu_gate")
            x2, y = _mm(gated, W["sgu_w_out"], epi=_epi_residual, extras=((x, "mn"), (g1, "n")), out_dtypes=(F32, BF16), name=f"l{i}_sgu_out")
            st.update(h=h, zz=zz, gated=gated, y=y, bs_t=bs_t)
        else:
            h = _norm_mod_fwd(x, gmix, sc1, sh1, BF16, f"l{i}_norm1")
            lat = _mm(h, W["mla_w_dq_dkv"], out_dtypes=(F32,), name=f"l{i}_mla_lat")
            cqn, ckvn, krot = _mla_lat_fwd(lat, S["mla_q_norm_g"], S["mla_kv_norm_g"], cc, sa, sb, f"l{i}_mla_latn")
            qpad = _mm(cqn, W["mla_w_uq"], out_dtypes=(F32,), name=f"l{i}_mla_uq")
            kv = _mm(ckvn, W["mla_w_ukv"], out_dtypes=(F32,), name=f"l{i}_mla_ukv")
            q, k, kt, v, vt = _mla_prep(qpad, kv, krot, cc, sa, sb, f"l{i}_mla_prep")
            o, lse = _attn_fwd(q, k, vt, f"l{i}_attn")
            x2, y = _mm(o, W["mla_w_o"], epi=_epi_residual, extras=((x, "mn"), (g1, "n")), out_dtypes=(F32, BF16), name=f"l{i}_mla_o")
            st.update(h=h, lat=lat, cqn=cqn, ckvn=ckvn, q=q, k=k, kt=kt, v=v, o=o, lse=lse, y=y)
        Wm = weights_of(i, "mlp", x2)
        W = {**W, **Wm}
        h2 = _norm_mod_fwd(x2, gmlp + Wm["_tok"], sc2, sh2, BF16, f"l{i}_norm2")
        z, r2 = _mm(h2, W["mlp_w1"], epi=_epi_sq_relu, out_dtypes=(BF16, BF16), name=f"l{i}_mlp1")
        x3, o2 = _mm(z, W["mlp_w2"], epi=_epi_residual, extras=((x2, "mn"), (g2, "n")), out_dtypes=(F32, BF16), name=f"l{i}_mlp2")
        st.update(x2=x2, h2=h2, z=z, r2=r2, o2=o2, W=W)
        saved.append(st)
        x = x3

    loss, dx, dfinal_g = _loss_head(x, target, S["final_g"], "loss_head")

    gS = {"final_g": dfinal_g, "norm_mix_g": [None] * DEPTH, "norm_mlp_g": [None] * DEPTH, "pool_scale": [None] * 2}
    dmod = [None] * DEPTH
    tok = jnp.zeros((), F32)
    for i in reversed(range(DEPTH)):
        st = saved[i]
        W, gW = st["W"], {}
        sh1, sc1, g1, sh2, sc2, g2 = (mod[i:i + 1, n * D:(n + 1) * D] for n in range(6))
        kind, j = i % N_MIXERS, i // N_MIXERS
        gmix, gmlp = S["norm_mix_g"][i:i + 1], S["norm_mlp_g"][i:i + 1]
        do2, dg2 = _resid_bwd(dx, st["o2"], g2 + tok, f"l{i}_b_res2")
        da = _mm(do2, W["mlp_w2"], tb=True, epi=lambda acc, rt: (acc * rt.astype(F32),), extras=((st["r2"], "mn"),), name=f"l{i}_b_dz")
        gW["mlp_w2"] = _mm(st["z"], do2, ta=True, name=f"l{i}_b_dw2")
        dh2 = _mm(da, W["mlp_w1"], tb=True, out_dtypes=(F32,), name=f"l{i}_b_dh2")
        gW["mlp_w1"] = _mm(st["h2"], da, ta=True, name=f"l{i}_b_dw1")
        dx2, dgmlp, dsc2, dsh2 = _norm_mod_bwd(st["x2"], dh2, dx, gmlp, sc2, f"l{i}_b_norm2")
        gS["norm_mlp_g"][i] = dgmlp
        dy, q1 = _resid_bwd(dx2, st["y"], g1, f"l{i}_b_res1")
        if kind == 0:
            dh, dpw, dpsc, dg1 = _pool_bwd(dy, st["pooled"], W["pool_w"], S["pool_scale"][j:j + 1], g1, q1, f"l{i}_b_pool")
            gW["pool_w"] = dpw.astype(BF16)
            gS["pool_scale"][j] = dpsc
        elif kind == 1:
            dg1 = q1
            dgated = _mm(dy, W["sgu_w_out"], tb=True, name=f"l{i}_b_dgated")
            gW["sgu_w_out"] = _mm(st["gated"], dy, ta=True, name=f"l{i}_b_dwout")
            dzz, dws, dbs, dlg, dlb = _sgu_gate_bwd(st["zz"], dgated, S["sgu_ln_g"], S["sgu_ln_b"], S["sgu_w_s"], st["bs_t"], f"l{i}_b_sgu_gate")
            gS.update(sgu_w_s=dws, sgu_b_s=dbs[:, :, 0], sgu_ln_g=dlg, sgu_ln_b=dlb)
            dh = _mm(dzz, W["sgu_w_in"], tb=True, out_dtypes=(F32,), name=f"l{i}_b_dh_sgu")
            gW["sgu_w_in"] = _mm(st["h"], dzz, ta=True, name=f"l{i}_b_dwin")
        else:
            dg1 = q1
            do = _mm(dy, W["mla_w_o"], tb=True, name=f"l{i}_b_do")
            gW["mla_w_o"] = _mm(st["o"], dy, ta=True, name=f"l{i}_b_dwo")
            delta = _attn_delta(do, st["o"], f"l{i}_b_delta")
            dqt, dk, dv = _attn_bwd(st["q"], st["k"], st["kt"], st["v"], do, st["lse"], delta, f"l{i}_b_attn")
            dqpad, dkv, dkrot = _mla_prep_bwd(dqt, dk, dv, cc, sa, sb, f"l{i}_b_mla_prep")
            dcqn = _mm(dqpad, W["mla_w_uq"], tb=True, out_dtypes=(F32,), name=f"l{i}_b_dcq")
            gW["mla_w_uq"] = _mm(st["cqn"], dqpad, ta=True, name=f"l{i}_b_dwuq")
            dckvn = _mm(dkv, W["mla_w_ukv"], tb=True, out_dtypes=(F32,), name=f"l{i}_b_dckv")
            gW["mla_w_ukv"] = _mm(st["ckvn"], dkv, ta=True, name=f"l{i}_b_dwukv")
            dlat, dqg, dkvg = _mla_lat_bwd(st["lat"], dcqn, dckvn, dkrot, S["mla_q_norm_g"], S["mla_kv_norm_g"], cc, sa, sb, f"l{i}_b_mla_latn")
            gS.update(mla_q_norm_g=dqg, mla_kv_norm_g=dkvg)
            dh = _mm(dlat, W["mla_w_dq_dkv"], tb=True, out_dtypes=(F32,), name=f"l{i}_b_dh_mla")
            gW["mla_w_dq_dkv"] = _mm(st["h"], dlat, ta=True, name=f"l{i}_b_dwdq")
        dx, dgmix, dsc1, dsh1 = _norm_mod_bwd(st["x"], dh, dx2, gmix, sc1, f"l{i}_b_norm1")
        gS["norm_mix_g"][i] = dgmix
        dmod[i] = jnp.concatenate([dsh1, dsc1, dg1, dsh2, dsc2, dg2], axis=1)
        tok = grads_of(i, gW, dx)

    for n in ("norm_mix_g", "norm_mlp_g", "pool_scale"):
        gS[n] = jnp.concatenate(gS[n], axis=0)
    return loss, dx, gS, jnp.concatenate(dmod, axis=0)


_SMALL = {
    "norm_mix_g": (DEPTH, D_MODEL), "norm_mlp_g": (DEPTH, D_MODEL), "sgu_ln_g": (1, SGU_W), "sgu_ln_b": (1, SGU_W),
    "sgu_w_s": (SGU_H, SGU_CHUNK, SGU_CHUNK), "sgu_b_s": (SGU_H, SGU_CHUNK), "mla_kv_norm_g": (1, MLA_KVL), "final_g": (1, D_MODEL),
    "pool_scale": (2, D_MODEL), "mla_q_norm_g": (1, MLA_QL), "dmod": (DEPTH, 6 * D_MODEL),
}
_PACK_W = 1024


def _pack(vals):
    flat = jnp.concatenate([v.reshape(-1) for v in vals])
    rows = -(-flat.shape[0] // (8 * _PACK_W)) * 8
    return jnp.pad(flat, (0, rows * _PACK_W - flat.shape[0])).reshape(rows, _PACK_W)


def _unpack(buf, shapes):
    flat, out, off = buf.reshape(-1), [], 0
    for s in shapes:
        n = math.prod(s)
        out.append(flat[off:off + n].reshape(s))
        off += n
    return out


def kernel(x, c, positions, ada_w, ada_b, norm_mix_g, norm_mlp_g, pool_w, pool_scale, sgu_w_in, sgu_ln_g, sgu_ln_b, sgu_w_s, sgu_b_s, sgu_w_out, mla_w_dq_dkv, mla_q_norm_g, mla_kv_norm_g, mla_w_uq, mla_w_ukv, mla_w_o, mlp_w1, mlp_w2, final_g, loss_target, m_ada_w, m_ada_b, m_norm_mix_g, m_norm_mlp_g, m_pool_w, m_pool_scale, m_sgu_w_in, m_sgu_ln_g, m_sgu_ln_b, m_sgu_w_s, m_sgu_b_s, m_sgu_w_out, m_mla_w_dq_dkv, m_mla_q_norm_g, m_mla_kv_norm_g, m_mla_w_uq, m_mla_w_ukv, m_mla_w_o, m_mlp_w1, m_mlp_w2, m_final_g, v_ada_w, v_ada_b, v_norm_mix_g, v_norm_mlp_g, v_pool_w, v_pool_scale, v_sgu_w_in, v_sgu_ln_g, v_sgu_ln_b, v_sgu_w_s, v_sgu_b_s, v_sgu_w_out, v_mla_w_dq_dkv, v_mla_q_norm_g, v_mla_kv_norm_g, v_mla_w_uq, v_mla_w_ukv, v_mla_w_o, v_mlp_w1, v_mlp_w2, v_final_g):
    P = dict(ada_w=ada_w, ada_b=ada_b, norm_mix_g=norm_mix_g, norm_mlp_g=norm_mlp_g, pool_w=pool_w, pool_scale=pool_scale, sgu_w_in=sgu_w_in,
             sgu_ln_g=sgu_ln_g, sgu_ln_b=sgu_ln_b, sgu_w_s=sgu_w_s, sgu_b_s=sgu_b_s, sgu_w_out=sgu_w_out, mla_w_dq_dkv=mla_w_dq_dkv,
             mla_q_norm_g=mla_q_norm_g, mla_kv_norm_g=mla_kv_norm_g, mla_w_uq=mla_w_uq, mla_w_ukv=mla_w_ukv, mla_w_o=mla_w_o, mlp_w1=mlp_w1,
             mlp_w2=mlp_w2, final_g=final_g)
    M = dict(ada_w=m_ada_w, ada_b=m_ada_b, norm_mix_g=m_norm_mix_g, norm_mlp_g=m_norm_mlp_g, pool_w=m_pool_w, pool_scale=m_pool_scale,
             sgu_w_in=m_sgu_w_in, sgu_ln_g=m_sgu_ln_g, sgu_ln_b=m_sgu_ln_b, sgu_w_s=m_sgu_w_s, sgu_b_s=m_sgu_b_s, sgu_w_out=m_sgu_w_out,
             mla_w_dq_dkv=m_mla_w_dq_dkv, mla_q_norm_g=m_mla_q_norm_g, mla_kv_norm_g=m_mla_kv_norm_g, mla_w_uq=m_mla_w_uq, mla_w_ukv=m_mla_w_ukv,
             mla_w_o=m_mla_w_o, mlp_w1=m_mlp_w1, mlp_w2=m_mlp_w2, final_g=m_final_g)
    V = dict(ada_w=v_ada_w, ada_b=v_ada_b, norm_mix_g=v_norm_mix_g, norm_mlp_g=v_norm_mlp_g, pool_w=v_pool_w, pool_scale=v_pool_scale,
             sgu_w_in=v_sgu_w_in, sgu_ln_g=v_sgu_ln_g, sgu_ln_b=v_sgu_ln_b, sgu_w_s=v_sgu_w_s, sgu_b_s=v_sgu_b_s, sgu_w_out=v_sgu_w_out,
             mla_w_dq_dkv=v_mla_w_dq_dkv, mla_q_norm_g=v_mla_q_norm_g, mla_kv_norm_g=v_mla_kv_norm_g, mla_w_uq=v_mla_w_uq, mla_w_ukv=v_mla_w_ukv,
             mla_w_o=v_mla_w_o, mlp_w1=v_mlp_w1, mlp_w2=v_mlp_w2, final_g=v_final_g)
    order = list(P)
    xi, yi, ci = _idx()
    chip = 2 * xi + yi
    D = D_MODEL
    n_ada = ada_w.shape[2]

    pre = _allgather8(_pack([c, pool_scale, mla_q_norm_g]), "ag_small")
    flat = pre.reshape(N_DEV, -1)
    c_all = flat[:, :D]
    ps_all = flat[0::2, D:D + 2 * (D // N_CHIPS)].reshape(N_CHIPS, 2, D // N_CHIPS).transpose(1, 0, 2).reshape(2, D)
    q0 = D + 2 * (D // N_CHIPS)
    qg_all = flat[0::2, q0:q0 + MLA_QL // N_CHIPS].reshape(1, MLA_QL)

    ada_b_loc = lax.dynamic_slice_in_dim(ada_b, chip * n_ada, n_ada, axis=1)[:, None, :]
    modp = _ada_fwd(c_all, ada_w, ada_b_loc, "ada_fwd")
    mod = _mod_exchange(modp.transpose(1, 0, 2), "mod_exchange").transpose(1, 0, 2).reshape(DEPTH, 6 * D)

    S = dict(norm_mix_g=norm_mix_g, norm_mlp_g=norm_mlp_g, pool_scale=ps_all, sgu_ln_g=sgu_ln_g, sgu_ln_b=sgu_ln_b, sgu_w_s=sgu_w_s[0],
             sgu_b_s=sgu_b_s[0], mla_q_norm_g=qg_all, mla_kv_norm_g=mla_kv_norm_g, final_g=final_g[None, :])
    cidx, chipidx = jnp.reshape(ci, (1,)).astype(jnp.int32), jnp.reshape(chip, (1,)).astype(jnp.int32)
    view2d = lambda a: a.reshape(-1, a.shape[-1])

    def piece_rows(kind, blk):
        r = _PIECE_KINDS[kind][0]
        return blk * r, r

    groups = [_layer_pieces(0)[:-2], _layer_pieces(0)[-2:], _layer_pieces(1)[:-2], _layer_pieces(1)[-2:], _layer_pieces(2), _layer_pieces(3)]
    start_after = {1: (2, 3), 2: (4,), 4: (5,)}
    gathers = {}

    def gather_start(g, dep):
        srcs, shapes = [], []
        for kind, blk in groups[g]:
            r0, r = piece_rows(kind, blk)
            cdim = _PIECE_KINDS[kind][1]
            srcs.append(view2d(P[kind])[r0:r0 + r].astype(BF16).reshape(2, r // 2, cdim))
            shapes.append(jax.ShapeDtypeStruct((N_CHIPS, 2, r // 2, cdim), BF16))
        gathers[g] = _xchip_start("gather", srcs, shapes, dep, f"ag_start_g{g}")

    def gather_finish(g, after):
        ssem, rsem, srcs, lands, _ = gathers.pop(g)
        srcs, lands = _xchip_wait("gather", ssem, rsem, srcs, lands, [after], f"ag_wait_g{g}")
        lands = _sibling_fwd(lands, f"ag_sibling_g{g}")
        dep = lands[0]
        for nxt in start_after.get(g, ()):
            gather_start(nxt, dep)
            dep = gathers[nxt][-1]
        W = {"_tok": sum((st[-1][0, 0] for st in gathers.values()), jnp.zeros((), F32))}
        for (kind, _), s, land in zip(groups[g], srcs, lands, strict=True):
            r, cdim, to_full, _ = _PIECE_KINDS[kind]
            W[kind] = to_full(lax.dynamic_update_index_in_dim(land, s, chip, 0).reshape(N_CHIPS, r, cdim))
        return W

    def weights_of(i, part, x_i):
        if i < 2:
            return gather_finish(2 * i + (part == "mlp"), x_i)
        return gather_finish(i + 2, x_i) if part == "mix" else {"_tok": jnp.zeros((), F32)}

    scatters = {}
    bufs = {n: tuple(lax.empty(view2d(P[n]).shape, F32) for _ in range(4)) for n in _PIECE_KINDS}

    def scatter_start(i, gW, dep):
        pcs = _layer_pieces(i)
        blocked = []
        for kind, _ in pcs:
            r, cdim, _, to_blocks = _PIECE_KINDS[kind]
            blocked.append(to_blocks(gW[kind]).reshape(N_CHIPS, 2, r // 2, cdim).transpose(1, 0, 2, 3))
        from_sib = _sibling_swap(blocked, f"rs_sibling_l{i}")
        pair, shapes = [], []
        for (kind, _), b, f in zip(pcs, blocked, from_sib, strict=True):
            _, _, hr, cdim = b.shape
            p = _sum_sel(cidx, b.reshape(2, N_CHIPS * hr, cdim), [f.reshape(1, N_CHIPS * hr, cdim)], f"rs_pair_l{i}_{kind}", BF16)
            pair.append(p.reshape(N_CHIPS, hr, cdim))
            shapes.append(jax.ShapeDtypeStruct((N_CHIPS - 1, hr, cdim), BF16))
        scatters[i] = (pcs, *_xchip_start("scatter", pair, shapes, dep, f"rs_start_l{i}"))
        return scatters[i][-1][0, 0]

    def scatter_finish(i, after):
        pcs, ssem, rsem, pair, lands, _ = scatters.pop(i)
        pair, lands = _xchip_wait("scatter", ssem, rsem, pair, lands, after, f"rs_wait_l{i}")
        halves = [_sum_sel(chipidx, p, [l], f"rs_sum_l{i}_{kind}", F32) for (kind, _), p, l in zip(pcs, pair, lands, strict=True)]
        got = _sibling_send(halves, f"rs_merge_l{i}")
        for (kind, blk), mine, other in zip(pcs, halves, got, strict=True):
            r0, _ = piece_rows(kind, blk)
            bufs[kind] = tuple(_adamw_piece(cidx, view2d(P[kind]), view2d(M[kind]), view2d(V[kind]), mine, other, bufs[kind], r0,
                                            f"adamw_l{i}_{kind}"))
        return lands[0]

    first_layer = {}

    def grads_of(i, gW, dx_i):
        dep = scatter_finish(i + 1, [dx_i]) if i + 1 in scatters else dx_i
        if i == 0:
            first_layer.update(gW)
            return jnp.zeros((), F32)
        return scatter_start(i, gW, dep)

    gather_start(0, mod)
    gather_start(1, gathers[0][-1])
    loss_l, dx, gS, dmod = _local_step(x[0], positions[0], loss_target[0], mod, S, weights_of, grads_of)
    loss = lax.psum(loss_l[0, 0], ("x", "y", "c"))

    gS["dmod"] = dmod
    small = _allgather8(_pack([gS[n] for n in _SMALL]), "ag_small_grads")
    small = small + scatter_start(0, first_layer, small)
    small_sum = _unpack(_sum_lead([small], "sum_small_grads"), list(_SMALL.values()))
    G = dict(zip(_SMALL, small_sum, strict=True))
    grads = {
        "ada_b": G["dmod"], "norm_mix_g": G["norm_mix_g"], "norm_mlp_g": G["norm_mlp_g"], "sgu_ln_g": G["sgu_ln_g"], "sgu_ln_b": G["sgu_ln_b"],
        "sgu_w_s": G["sgu_w_s"][None], "sgu_b_s": G["sgu_b_s"][None], "mla_kv_norm_g": G["mla_kv_norm_g"], "final_g": G["final_g"][0],
        "pool_scale": lax.dynamic_slice_in_dim(G["pool_scale"], chip * (D // N_CHIPS), D // N_CHIPS, axis=1),
        "mla_q_norm_g": lax.dynamic_slice_in_dim(G["mla_q_norm_g"], chip * (MLA_QL // N_CHIPS), MLA_QL // N_CHIPS, axis=1),
    }
    dmod_all = _unpack(small, [(N_DEV,) + (small.shape[1] * _PACK_W,)])[0]
    off = sum(math.prod(s) for n, s in _SMALL.items() if n != "dmod")
    dmod_all = dmod_all[:, off:off + DEPTH * 6 * D].reshape(N_DEV, DEPTH, 6 * D)
    dmod_loc = lax.dynamic_slice_in_dim(dmod_all, chip * n_ada, n_ada, axis=2).transpose(1, 0, 2)
    grads["ada_w"] = _ada_bwd(c_all.T, dmod_loc, "ada_bwd")

    deltas, new_m, new_v = {}, {}, {}
    for n in order:
        if n not in _PIECE_KINDS:
            deltas[n], new_m[n], new_v[n] = _adamw(P[n], grads[n].reshape(P[n].shape), M[n], V[n], f"adamw_{n}")
    scatter_finish(0, [deltas["ada_w"], deltas["sgu_w_s"]] + [bufs[n][0] for n in ("mlp_w1", "mlp_w2", "sgu_w_in", "mla_w_o")])
    for n in _PIECE_KINDS:
        grads[n], deltas[n], new_m[n], new_v[n] = (b.reshape(P[n].shape) for b in bufs[n])
    return (loss, dx[None], *[grads[n].reshape(P[n].shape) for n in order], *[deltas[n] for n in order], *[new_m[n] for n in order],
            *[new_v[n] for n in order])
```

```python
import math

import jax
import jax.numpy as jnp
from jax import lax
from jax.experimental import pallas as pl
from jax.experimental.pallas import tpu as pltpu

F32, BF16 = jnp.float32, jnp.bfloat16
MESH = pl.DeviceIdType.MESH

D_MODEL = 1024
DEPTH = 4
N_MIXERS = 3
POOL_WINDOWS = (2, 4, 8, 16)
POOL_GD = D_MODEL // len(POOL_WINDOWS)
POOL_HALO = 16
SGU_CHUNK = 128
SGU_W = D_MODEL
SGU_HD = 128
SGU_H = SGU_W // SGU_HD
MLA_H = 16
MLA_QL = 256
MLA_KVL = 128
MLA_NOPE = 128
MLA_ROPE = 64
MLA_V = 128
MLA_HP = 256
MLA_LATP = 512
ROPE_THETA = 10000.0
RMS_EPS = 1e-6
LN_EPS = 1e-5
SM_SCALE = (MLA_NOPE + MLA_ROPE) ** -0.5
NEG_INF = -1e30
ADAM_LR, ADAM_B1, ADAM_B2, ADAM_EPS, ADAM_WD, ADAM_STEP = 0.001, 0.9, 0.999, 1e-08, 0.01, 10
N_CHIPS = 4
N_DEV = 8
ROW_TILE = 512
ATT_TILE = 512
ATT_SUB = 256
ATT_FWD_HEADS = 4
ATT_BWD_HEADS = 2
MM_VMEM_BUDGET = 40 << 20


def _idx():
    return lax.axis_index("x"), lax.axis_index("y"), lax.axis_index("c")


def _mm(a, b, *, name, ta=False, tb=False, epi=None, extras=(), out_dtypes=(BF16,), tm=1024, tn=1024, tk=1024, chip_blocks=None):
    if ta:
        K, M = a.shape
    else:
        M, K = a.shape
    b_chips = b.ndim == 3
    if b_chips:
        assert b.shape[0] == N_CHIPS
        Kb, N = (N_CHIPS * b.shape[2], b.shape[1]) if tb else (b.shape[1], N_CHIPS * b.shape[2])
    elif tb:
        N, Kb = b.shape
    else:
        Kb, N = b.shape
    assert K == Kb, (a.shape, b.shape, ta, tb)
    if b_chips and not tb:
        tn = min(tn, N // N_CHIPS)
    if chip_blocks == "col":
        tm, tn = min(tm, M // 2), min(tn, N // N_CHIPS)
    elif chip_blocks == "row":
        tm = min(tm, M // N_CHIPS // 2)
    tm, tn, tk = min(tm, M), min(tn, N), min(tk, K)

    def vmem_bytes(tm_, tk_):
        per_mn = sum(arr.dtype.itemsize for arr, kind in extras if kind == "mn") + sum(jnp.dtype(dt).itemsize for dt in out_dtypes)
        return 2 * (tm_ * tk_ * a.dtype.itemsize + tk_ * tn * b.dtype.itemsize + tm_ * tn * per_mn)

    if vmem_bytes(tm, K) <= MM_VMEM_BUDGET:
        tk = K
    elif tm >= 512 and vmem_bytes(tm // 2, K) <= MM_VMEM_BUDGET:
        tm, tk = tm // 2, K
    assert M % tm == 0 and N % tn == 0 and K % tk == 0, (M, N, K, tm, tn, tk)
    nk = K // tk
    a_spec = pl.BlockSpec((tk, tm), lambda i, j, k: (k, i)) if ta else pl.BlockSpec((tm, tk), lambda i, j, k: (i, k))
    b_spec = pl.BlockSpec((tn, tk), lambda i, j, k: (j, k)) if tb else pl.BlockSpec((tk, tn), lambda i, j, k: (k, j))
    if b_chips and tb:
        assert nk == 1 and not ta
        b_spec = pl.BlockSpec((N_CHIPS, tn, K // N_CHIPS), lambda i, j, k: (0, j, 0))
    elif b_chips:
        per = N // N_CHIPS // tn
        b_spec = pl.BlockSpec((None, tk, tn), lambda i, j, k: (j // per, k, j % per))
    ex_specs = []
    for arr, kind in extras:
        if kind == "mn":
            ex_specs.append(pl.BlockSpec((tm, tn), lambda i, j, k: (i, j)))
        elif kind == "n":
            ex_specs.append(pl.BlockSpec((1, tn), lambda i, j, k: (0, j)))
        else:
            ex_specs.append(pl.BlockSpec((tm, arr.shape[1]), lambda i, j, k: (i, 0)))
    n_ex, n_out = len(extras), len(out_dtypes)
    dims = (((0 if ta else 1,), (1 if tb else 0,)), ((), ()))

    def body(*refs):
        a_ref, b_ref = refs[0], refs[1]
        ex_refs = refs[2:2 + n_ex]
        out_refs = refs[2 + n_ex:2 + n_ex + n_out]
        if b_chips and tb:
            kc = K // N_CHIPS
            part = None
            for cb in range(N_CHIPS):
                p = lax.dot_general(a_ref[:, cb * kc:(cb + 1) * kc].astype(BF16), b_ref[cb].astype(BF16), dims, preferred_element_type=F32)
                part = p if part is None else part + p
        else:
            part = lax.dot_general(a_ref[...].astype(BF16), b_ref[...].astype(BF16), dims, preferred_element_type=F32)

        def finish(acc):
            outs = epi(acc, *[r[...] for r in ex_refs]) if epi is not None else (acc,)
            for r, o in zip(out_refs, outs, strict=True):
                r[...] = o.astype(r.dtype)

        if nk == 1:
            finish(part)
        else:
            acc_ref = refs[-1]
            k = pl.program_id(2)

            @pl.when(k == 0)
            def _():
                acc_ref[...] = part

            @pl.when(k > 0)
            def _():
                acc_ref[...] += part

            @pl.when(k == nk - 1)
            def _():
                finish(acc_ref[...])

    out_specs = [pl.BlockSpec((tm, tn), lambda i, j, k: (i, j)) for _ in range(n_out)]
    out_shape = [jax.ShapeDtypeStruct((M, N), dt) for dt in out_dtypes]
    if chip_blocks is not None:
        assert n_out == 1
        if chip_blocks == "col":
            rh, cb = M // 2 // tm, N // N_CHIPS // tn
            out_specs = [pl.BlockSpec((None, None, tm, tn), lambda i, j, k: (i // rh, j // cb, i % rh, j % cb))]
            out_shape = [jax.ShapeDtypeStruct((2, N_CHIPS, M // 2, N // N_CHIPS), out_dtypes[0])]
        else:
            rh = M // N_CHIPS // 2 // tm
            out_specs = [pl.BlockSpec((None, None, tm, tn), lambda i, j, k: ((i // rh) % 2, i // (2 * rh), i % rh, j))]
            out_shape = [jax.ShapeDtypeStruct((2, N_CHIPS, M // N_CHIPS // 2, N), out_dtypes[0])]
    outs = pl.pallas_call(
        body,
        name=name,
        grid=(M // tm, N // tn, nk),
        in_specs=[a_spec, b_spec, *ex_specs],
        out_specs=out_specs,
        out_shape=out_shape,
        scratch_shapes=[pltpu.VMEM((tm, tn), F32)] if nk > 1 else [],
        compiler_params=pltpu.CompilerParams(dimension_semantics=("parallel", "parallel", "arbitrary")),
    )(a, b, *[arr for arr, _ in extras])
    return outs[0] if n_out == 1 else tuple(outs)


def _epi_sq_relu(acc):
    r = jnp.maximum(acc, 0.0)
    return r * r, 2.0 * r


def _epi_residual(acc, x, g):
    return x + g * acc, acc


def _rms_mod(xv, gain, sc, sh):
    r = lax.rsqrt(jnp.mean(xv * xv, axis=-1, keepdims=True) + RMS_EPS)
    return ((xv * r) * gain) * (1.0 + sc) + sh


def _epi_residual_norm(acc, x, g, gain, sc, sh):
    xn = x + g * acc
    return xn, acc, _rms_mod(xn, gain, sc, sh)


def _row_spec(tr, d):
    return pl.BlockSpec((tr, d), lambda i: (i, 0))


def _vec_spec(d):
    return pl.BlockSpec((1, d), lambda i: (0, 0))


def _colsum(v):
    return jnp.sum(v, axis=0, keepdims=True)


def _norm_mod_fwd(x, gain, sc, sh, out_dtype, name):
    T, D = x.shape
    tr = min(T, ROW_TILE)

    def body(x_ref, g_ref, sc_ref, sh_ref, o_ref):
        o_ref[...] = _rms_mod(x_ref[...], g_ref[...], sc_ref[...], sh_ref[...]).astype(o_ref.dtype)

    return pl.pallas_call(
        body, name=name, grid=(T // tr,),
        in_specs=[_row_spec(tr, D), _vec_spec(D), _vec_spec(D), _vec_spec(D)],
        out_specs=_row_spec(tr, D),
        out_shape=jax.ShapeDtypeStruct((T, D), out_dtype),
        compiler_params=pltpu.CompilerParams(dimension_semantics=("parallel",)),
    )(x, gain, sc, sh)


def _norm_mod_bwd(x, dh, dres, gain, sc, name):
    T, D = x.shape
    tr = min(T, ROW_TILE)

    def body(x_ref, dh_ref, dres_ref, g_ref, sc_ref, dx_ref, dg_ref, dsc_ref, dsh_ref):
        @pl.when(pl.program_id(0) == 0)
        def _():
            dg_ref[...] = jnp.zeros_like(dg_ref)
            dsc_ref[...] = jnp.zeros_like(dsc_ref)
            dsh_ref[...] = jnp.zeros_like(dsh_ref)

        xv = x_ref[...]
        r = lax.rsqrt(jnp.mean(xv * xv, axis=-1, keepdims=True) + RMS_EPS)
        xn = xv * r
        dhv = dh_ref[...].astype(F32)
        dsh_ref[...] += _colsum(dhv)
        dsc_ref[...] += _colsum(dhv * (xn * g_ref[...]))
        dt = dhv * (1.0 + sc_ref[...])
        dg_ref[...] += _colsum(dt * xn)
        dxn = dt * g_ref[...]
        dx_ref[...] = dres_ref[...] + r * (dxn - xn * jnp.mean(dxn * xn, axis=-1, keepdims=True))

    return pl.pallas_call(
        body, name=name, grid=(T // tr,),
        in_specs=[_row_spec(tr, D), _row_spec(tr, D), _row_spec(tr, D), _vec_spec(D), _vec_spec(D)],
        out_specs=[_row_spec(tr, D), _vec_spec(D), _vec_spec(D), _vec_spec(D)],
        out_shape=[jax.ShapeDtypeStruct((T, D), F32)] + [jax.ShapeDtypeStruct((1, D), F32)] * 3,
        compiler_params=pltpu.CompilerParams(dimension_semantics=("arbitrary",)),
    )(x, dh, dres, gain, sc)


def _resid_bwd(dx, y, g, name):
    T, D = dx.shape
    tr = min(T, ROW_TILE)

    def body(dx_ref, y_ref, g_ref, dy_ref, q_ref):
        @pl.when(pl.program_id(0) == 0)
        def _():
            q_ref[...] = jnp.zeros_like(q_ref)

        dxv = dx_ref[...]
        dy_ref[...] = (g_ref[...] * dxv).astype(BF16)
        q_ref[...] += _colsum(dxv * y_ref[...].astype(F32))

    return pl.pallas_call(
        body, name=name, grid=(T // tr,),
        in_specs=[_row_spec(tr, D), _row_spec(tr, D), _vec_spec(D)],
        out_specs=[_row_spec(tr, D), _vec_spec(D)],
        out_shape=[jax.ShapeDtypeStruct((T, D), BF16), jax.ShapeDtypeStruct((1, D), F32)],
        compiler_params=pltpu.CompilerParams(dimension_semantics=("arbitrary",)),
    )(dx, y, g)


def _loss_head(x, target, gain, name):
    T, D = x.shape
    tr = min(T, ROW_TILE)

    def body(x_ref, t_ref, g_ref, loss_ref, dx_ref, dg_ref):
        @pl.when(pl.program_id(0) == 0)
        def _():
            loss_ref[...] = jnp.zeros_like(loss_ref)
            dg_ref[...] = jnp.zeros_like(dg_ref)

        xv = x_ref[...]
        r = lax.rsqrt(jnp.mean(xv * xv, axis=-1, keepdims=True) + RMS_EPS)
        xn = xv * r
        err = xn * g_ref[...] - t_ref[...]
        row = jnp.mean(err * err, axis=-1, keepdims=True)
        loss_ref[...] += 0.5 * jnp.sum(row, axis=0, keepdims=True)
        dy = err * (1.0 / D)
        dg_ref[...] += _colsum(dy * xn)
        dxn = dy * g_ref[...]
        dx_ref[...] = r * (dxn - xn * jnp.mean(dxn * xn, axis=-1, keepdims=True))

    return pl.pallas_call(
        body, name=name, grid=(T // tr,),
        in_specs=[_row_spec(tr, D), _row_spec(tr, D), _vec_spec(D)],
        out_specs=[_vec_spec(128), _row_spec(tr, D), _vec_spec(D)],
        out_shape=[jax.ShapeDtypeStruct((1, 128), F32), jax.ShapeDtypeStruct((T, D), F32), jax.ShapeDtypeStruct((1, D), F32)],
        compiler_params=pltpu.CompilerParams(dimension_semantics=("arbitrary",)),
    )(x, target, gain)


def _pool_fwd(h, w, scale, x, g1, gmlp, sc2, sh2, name):
    T, D = h.shape
    tr = min(T, ROW_TILE)

    def body(h_ref, w_ref, sc_ref, x_ref, g_ref, gm_ref, sc2_ref, sh2_ref, x2_ref, pooled_ref, ypre_ref, h2_ref, halo_ref):
        i = pl.program_id(0)

        @pl.when(i == 0)
        def _():
            halo_ref[...] = jnp.zeros_like(halo_ref)

        hv = h_ref[...]
        buf = jnp.concatenate([halo_ref[...], hv], axis=0)
        halo_ref[...] = hv[tr - POOL_HALO:, :]
        t = (i * tr + lax.broadcasted_iota(jnp.int32, (tr, 1), 0)).astype(F32)
        for gi, win in enumerate(POOL_WINDOWS):
            cols = slice(gi * POOL_GD, (gi + 1) * POOL_GD)
            val = buf[:, cols]
            sh = 1
            while sh < win:
                val = val + pltpu.roll(val, sh, axis=0)
                sh *= 2
            pooled = val[POOL_HALO:, :] / jnp.minimum(t + 1.0, float(win)) - hv[:, cols]
            pb = pooled.astype(BF16)
            pooled_ref[:, cols] = pb
            yp = jnp.dot(pb, w_ref[gi], preferred_element_type=F32)
            ypre_ref[:, cols] = yp.astype(BF16)
            x2_ref[:, cols] = x_ref[:, cols] + g_ref[:, cols] * (yp * sc_ref[:, cols])
        h2_ref[...] = _rms_mod(x2_ref[...], gm_ref[...], sc2_ref[...], sh2_ref[...]).astype(BF16)

    return pl.pallas_call(
        body, name=name, grid=(T // tr,),
        in_specs=[_row_spec(tr, D), pl.BlockSpec(w.shape, lambda i: (0, 0, 0)), _vec_spec(D), _row_spec(tr, D), _vec_spec(D), _vec_spec(D),
                  _vec_spec(D), _vec_spec(D)],
        out_specs=[_row_spec(tr, D)] * 4,
        out_shape=[jax.ShapeDtypeStruct((T, D), F32), jax.ShapeDtypeStruct((T, D), BF16), jax.ShapeDtypeStruct((T, D), BF16),
                   jax.ShapeDtypeStruct((T, D), BF16)],
        scratch_shapes=[pltpu.VMEM((POOL_HALO, D), F32)],
        compiler_params=pltpu.CompilerParams(dimension_semantics=("arbitrary",)),
    )(h, w, scale, x, g1, gmlp, sc2, sh2)


def _pool_bwd(dy, pooled, w, scale, g1, q, name):
    T, D = dy.shape
    tr = min(T, ROW_TILE)
    nt = T // tr
    ltot = tr + POOL_HALO

    def body(dy_ref, pooled_ref, w_ref, sc_ref, g_ref, q_ref, dh_ref, dw_ref, dsc_ref, dg_ref, halo_ref):
        i = pl.program_id(0)

        @pl.when(i == 0)
        def _():
            halo_ref[...] = jnp.zeros_like(halo_ref)
            dw_ref[...] = jnp.zeros_like(dw_ref)
            dsc_ref[...] = g_ref[...] * q_ref[...]
            dg_ref[...] = sc_ref[...] * q_ref[...]

        t = ((nt - 1 - i) * tr + lax.broadcasted_iota(jnp.int32, (tr, 1), 0)).astype(F32)
        for gi, win in enumerate(POOL_WINDOWS):
            cols = slice(gi * POOL_GD, (gi + 1) * POOL_GD)
            dyb = (dy_ref[:, cols].astype(F32) * sc_ref[:, cols]).astype(BF16)
            dw_ref[gi] += lax.dot_general(pooled_ref[:, cols], dyb, (((0,), (0,)), ((), ())), preferred_element_type=F32)
            dpool = lax.dot_general(dyb, w_ref[gi], (((1,), (1,)), ((), ())), preferred_element_type=F32)
            qv = dpool / jnp.minimum(t + 1.0, float(win))
            val = jnp.concatenate([qv, halo_ref[:, cols]], axis=0)
            halo_ref[:, cols] = qv[:POOL_HALO, :]
            sh = 1
            while sh < win:
                val = val + pltpu.roll(val, ltot - sh, axis=0)
                sh *= 2
            dh_ref[:, cols] = val[:tr, :] - dpool

    rev = pl.BlockSpec((tr, D), lambda i: (nt - 1 - i, 0))
    return pl.pallas_call(
        body, name=name, grid=(nt,),
        in_specs=[rev, rev, pl.BlockSpec(w.shape, lambda i: (0, 0, 0)), _vec_spec(D), _vec_spec(D), _vec_spec(D)],
        out_specs=[rev, pl.BlockSpec(w.shape, lambda i: (0, 0, 0)), _vec_spec(D), _vec_spec(D)],
        out_shape=[jax.ShapeDtypeStruct((T, D), F32), jax.ShapeDtypeStruct(w.shape, F32),
                   jax.ShapeDtypeStruct((1, D), F32), jax.ShapeDtypeStruct((1, D), F32)],
        scratch_shapes=[pltpu.VMEM((POOL_HALO, D), F32)],
        compiler_params=pltpu.CompilerParams(dimension_semantics=("arbitrary",)),
    )(dy, pooled, w, scale, g1, q)


_INV_SQRT2 = 0.7071067811865476
_INV_SQRT2PI = 0.3989422804014327


def _gelu(v):
    return 0.5 * v * (1.0 + lax.erf(v * _INV_SQRT2))


def _gelu_grad(v):
    return 0.5 * (1.0 + lax.erf(v * _INV_SQRT2)) + v * jnp.exp(-0.5 * v * v) * _INV_SQRT2PI


def _sgu_ln(v, g, b):
    mu = jnp.mean(v, axis=-1, keepdims=True)
    xc = v - mu
    rstd = lax.rsqrt(jnp.mean(xc * xc, axis=-1, keepdims=True) + LN_EPS)
    xh = xc * rstd
    return xh, rstd, xh * g + b


def _tril_mask():
    return lax.broadcasted_iota(jnp.int32, (SGU_CHUNK, SGU_CHUNK), 0) >= lax.broadcasted_iota(jnp.int32, (SGU_CHUNK, SGU_CHUNK), 1)


SGU_TILE = 256


def _sgu_gate_fwd(zz, ln_g, ln_b, ws, bs_t, name):
    T = zz.shape[0]
    ts = min(T, SGU_TILE)

    def body(zz_ref, g_ref, b_ref, ws_ref, bs_ref, out_ref):
        z = _gelu(zz_ref[...])
        u = z[:, :SGU_W]
        _, _, vn = _sgu_ln(z[:, SGU_W:], g_ref[...], b_ref[...])
        vb = vn.astype(BF16)
        tril = _tril_mask()
        for hh in range(SGU_H):
            wm = jnp.where(tril, ws_ref[hh], 0.0).astype(BF16)
            bcol = bs_ref[:, hh:hh + 1]
            cs = slice(hh * SGU_HD, (hh + 1) * SGU_HD)
            for j in range(ts // SGU_CHUNK):
                rs = slice(j * SGU_CHUNK, (j + 1) * SGU_CHUNK)
                mixed = jnp.dot(wm, vb[rs, cs], preferred_element_type=F32) + bcol
                out_ref[rs, cs] = (u[rs, cs] * mixed).astype(BF16)

    return pl.pallas_call(
        body, name=name, grid=(T // ts,),
        in_specs=[_row_spec(ts, 2 * SGU_W), _vec_spec(SGU_W), _vec_spec(SGU_W),
                  pl.BlockSpec(ws.shape, lambda i: (0, 0, 0)), pl.BlockSpec(bs_t.shape, lambda i: (0, 0))],
        out_specs=_row_spec(ts, SGU_W),
        out_shape=jax.ShapeDtypeStruct((T, SGU_W), BF16),
        compiler_params=pltpu.CompilerParams(dimension_semantics=("parallel",)),
    )(zz, ln_g, ln_b, ws, bs_t)


def _sgu_gate_bwd(zz, dgated, ln_g, ln_b, ws, bs_t, name):
    T = zz.shape[0]
    ts = min(T, SGU_TILE)
    nt = T // ts

    def body(zz_ref, dg_ref, g_ref, b_ref, ws_ref, bs_ref, dzz_ref, dws_ref, dbs_ref, dlg_ref, dlb_ref, dlo_ref, dmx_ref):
        i = pl.program_id(0)

        @pl.when(i == 0)
        def _():
            dws_ref[...] = jnp.zeros_like(dws_ref)
            dmx_ref[...] = jnp.zeros_like(dmx_ref)
            dlg_ref[...] = jnp.zeros_like(dlg_ref)
            dlb_ref[...] = jnp.zeros_like(dlb_ref)

        zzv = zz_ref[...]
        z = _gelu(zzv)
        u = z[:, :SGU_W]
        xh, rstd, vn = _sgu_ln(z[:, SGU_W:], g_ref[...], b_ref[...])
        vb = vn.astype(BF16)
        dgv = dg_ref[...].astype(F32)
        tril = _tril_mask()
        for hh in range(SGU_H):
            wm = jnp.where(tril, ws_ref[hh], 0.0).astype(BF16)
            bcol = bs_ref[:, hh:hh + 1]
            cs = slice(hh * SGU_HD, (hh + 1) * SGU_HD)
            for j in range(ts // SGU_CHUNK):
                rs = slice(j * SGU_CHUNK, (j + 1) * SGU_CHUNK)
                mixed = jnp.dot(wm, vb[rs, cs], preferred_element_type=F32) + bcol
                dmixed = dgv[rs, cs] * u[rs, cs]
                dzz_ref[rs, cs] = (dgv[rs, cs] * mixed * _gelu_grad(zzv[rs, cs])).astype(BF16)
                dmb = dmixed.astype(BF16)
                dws_ref[hh] += lax.dot_general(dmb, vb[rs, cs], (((1,), (1,)), ((), ())), preferred_element_type=F32)
                dmx_ref[hh] += dmixed
                dlo_ref[rs, cs] = lax.dot_general(wm, dmb, (((0,), (0,)), ((), ())), preferred_element_type=F32)
        dlo = dlo_ref[...]
        dlg_ref[...] += _colsum(dlo * xh)
        dlb_ref[...] += _colsum(dlo)
        dxh = dlo * g_ref[...]
        dv = rstd * (dxh - jnp.mean(dxh, axis=-1, keepdims=True) - xh * jnp.mean(dxh * xh, axis=-1, keepdims=True))
        dzz_ref[:, SGU_W:] = (dv * _gelu_grad(zzv[:, SGU_W:])).astype(BF16)

        @pl.when(i == nt - 1)
        def _():
            tril_f = tril.astype(F32)
            for hh in range(SGU_H):
                dws_ref[hh] = dws_ref[hh] * tril_f
                dbs_ref[hh] = jnp.broadcast_to(jnp.sum(dmx_ref[hh], axis=-1, keepdims=True), (SGU_CHUNK, SGU_HD))

    full3 = pl.BlockSpec(ws.shape, lambda i: (0, 0, 0))
    return pl.pallas_call(
        body, name=name, grid=(nt,),
        in_specs=[_row_spec(ts, 2 * SGU_W), _row_spec(ts, SGU_W), _vec_spec(SGU_W), _vec_spec(SGU_W), full3,
                  pl.BlockSpec(bs_t.shape, lambda i: (0, 0))],
        out_specs=[_row_spec(ts, 2 * SGU_W), full3, full3, _vec_spec(SGU_W), _vec_spec(SGU_W)],
        out_shape=[jax.ShapeDtypeStruct((T, 2 * SGU_W), BF16), jax.ShapeDtypeStruct(ws.shape, F32), jax.ShapeDtypeStruct(ws.shape, F32),
                   jax.ShapeDtypeStruct((1, SGU_W), F32), jax.ShapeDtypeStruct((1, SGU_W), F32)],
        scratch_shapes=[pltpu.VMEM((ts, SGU_W), F32), pltpu.VMEM(ws.shape, F32)],
        compiler_params=pltpu.CompilerParams(dimension_semantics=("arbitrary",)),
    )(zz, dgated, ln_g, ln_b, ws, bs_t)


def _rope_fwd(blk, cc, sa, sb):
    return blk * cc + pltpu.roll(blk, 96, axis=1) * sa + pltpu.roll(blk, 32, axis=1) * sb


def _rope_bwd(d, cc, sa, sb):
    return d * cc + pltpu.roll(d * sa, 32, axis=1) + pltpu.roll(d * sb, 96, axis=1)


def _rms(v, g):
    r = lax.rsqrt(jnp.mean(v * v, axis=-1, keepdims=True) + RMS_EPS)
    vn = v * r
    return vn, r, vn * g


def _rms_bwd(dy, vn, r, g):
    dvn = dy * g
    return r * (dvn - vn * jnp.mean(dvn * vn, axis=-1, keepdims=True))


MLA_TILE = 256
_KV0 = MLA_QL
_KR0 = MLA_QL + MLA_KVL


def _mla_lat_fwd(lat, qg, kvg, cc, sa, sb, name):
    T = lat.shape[0]
    tr = min(T, ROW_TILE)

    def body(lat_ref, qg_ref, kvg_ref, cc_ref, sa_ref, sb_ref, cq_ref, ckv_ref, kr_ref):
        lv = lat_ref[...]
        cq_ref[...] = _rms(lv[:, :_KV0], qg_ref[...])[2].astype(BF16)
        ckv_ref[...] = _rms(lv[:, _KV0:_KR0], kvg_ref[...])[2].astype(BF16)
        kr_ref[...] = _rope_fwd(lv[:, _KR0:], cc_ref[...], sa_ref[...], sb_ref[...])

    return pl.pallas_call(
        body, name=name, grid=(T // tr,),
        in_specs=[_row_spec(tr, MLA_LATP), _vec_spec(MLA_QL), _vec_spec(MLA_KVL), _row_spec(tr, 128), _row_spec(tr, 128), _row_spec(tr, 128)],
        out_specs=[_row_spec(tr, MLA_QL), _row_spec(tr, MLA_KVL), _row_spec(tr, 128)],
        out_shape=[jax.ShapeDtypeStruct((T, MLA_QL), BF16), jax.ShapeDtypeStruct((T, MLA_KVL), BF16), jax.ShapeDtypeStruct((T, 128), F32)],
        compiler_params=pltpu.CompilerParams(dimension_semantics=("parallel",)),
    )(lat, qg, kvg, cc, sa, sb)


def _mla_lat_bwd(lat, dcqn, dckvn, dkrot, qg, kvg, cc, sa, sb, name):
    T = lat.shape[0]
    tr = min(T, ROW_TILE)

    def body(lat_ref, dcq_ref, dckv_ref, dkr_ref, qg_ref, kvg_ref, cc_ref, sa_ref, sb_ref, dlat_ref, dqg_ref, dkvg_ref):
        @pl.when(pl.program_id(0) == 0)
        def _():
            dqg_ref[...] = jnp.zeros_like(dqg_ref)
            dkvg_ref[...] = jnp.zeros_like(dkvg_ref)

        lv = lat_ref[...]
        qn, qr, _ = _rms(lv[:, :_KV0], qg_ref[...])
        kn, kr, _ = _rms(lv[:, _KV0:_KR0], kvg_ref[...])
        dcq = dcq_ref[...]
        dckv = dckv_ref[...]
        dqg_ref[...] += _colsum(dcq * qn)
        dkvg_ref[...] += _colsum(dckv * kn)
        dlat_ref[:, :_KV0] = _rms_bwd(dcq, qn, qr, qg_ref[...]).astype(BF16)
        dlat_ref[:, _KV0:_KR0] = _rms_bwd(dckv, kn, kr, kvg_ref[...]).astype(BF16)
        dlat_ref[:, _KR0:] = _rope_bwd(dkr_ref[...], cc_ref[...], sa_ref[...], sb_ref[...]).astype(BF16)

    return pl.pallas_call(
        body, name=name, grid=(T // tr,),
        in_specs=[_row_spec(tr, MLA_LATP), _row_spec(tr, MLA_QL), _row_spec(tr, MLA_KVL), _row_spec(tr, 128),
                  _vec_spec(MLA_QL), _vec_spec(MLA_KVL), _row_spec(tr, 128), _row_spec(tr, 128), _row_spec(tr, 128)],
        out_specs=[_row_spec(tr, MLA_LATP), _vec_spec(MLA_QL), _vec_spec(MLA_KVL)],
        out_shape=[jax.ShapeDtypeStruct((T, MLA_LATP), BF16), jax.ShapeDtypeStruct((1, MLA_QL), F32), jax.ShapeDtypeStruct((1, MLA_KVL), F32)],
        compiler_params=pltpu.CompilerParams(dimension_semantics=("arbitrary",)),
    )(lat, dcqn, dckvn, dkrot, qg, kvg, cc, sa, sb)


def _mla_prep(qpad, kv, krot, cc, sa, sb, name):
    T = qpad.shape[0]
    tr = min(T, MLA_TILE)
    HW = MLA_H * MLA_HP

    def body(q_ref, kv_ref, kr_ref, cc_ref, sa_ref, sb_ref, qo_ref, ko_ref, kt_ref, vo_ref, vt_ref):
        cc, sa, sb = cc_ref[...], sa_ref[...], sb_ref[...]
        kr = kr_ref[...]
        krb, krt = kr.astype(BF16), kr.T.astype(BF16)
        for hh in range(MLA_H):
            a, m, b = hh * MLA_HP, hh * MLA_HP + MLA_NOPE, (hh + 1) * MLA_HP
            qo_ref[:, a:m] = (q_ref[:, a:m] * SM_SCALE).astype(BF16)
            qo_ref[:, m:b] = (_rope_fwd(q_ref[:, m:b], cc, sa, sb) * SM_SCALE).astype(BF16)
            kn = kv_ref[:, a:m]
            ko_ref[:, a:m] = kn.astype(BF16)
            ko_ref[:, m:b] = krb
            kt_ref[a:m, :] = kn.T.astype(BF16)
            kt_ref[m:b, :] = krt
            vh = kv_ref[:, m:b]
            vo_ref[:, hh * MLA_V:(hh + 1) * MLA_V] = vh.astype(BF16)
            vt_ref[hh] = vh.T.astype(BF16)

    tk = min(T, ATT_TILE)
    per = tk // tr
    return pl.pallas_call(
        body, name=name, grid=(T // tr,),
        in_specs=[_row_spec(tr, HW), _row_spec(tr, HW), _row_spec(tr, 128), _row_spec(tr, 128), _row_spec(tr, 128), _row_spec(tr, 128)],
        out_specs=[_row_spec(tr, HW), _row_spec(tr, HW), pl.BlockSpec((HW, tr), lambda i: (0, i)), _row_spec(tr, MLA_H * MLA_V),
                   pl.BlockSpec((MLA_H, None, MLA_V, tr), lambda i: (0, i // per, 0, i % per))],
        out_shape=[jax.ShapeDtypeStruct((T, HW), BF16), jax.ShapeDtypeStruct((T, HW), BF16), jax.ShapeDtypeStruct((HW, T), BF16),
                   jax.ShapeDtypeStruct((T, MLA_H * MLA_V), BF16), jax.ShapeDtypeStruct((MLA_H, T // tk, MLA_V, tk), BF16)],
        compiler_params=pltpu.CompilerParams(dimension_semantics=("parallel",)),
    )(qpad, kv, krot, cc, sa, sb)


ATT_HG = 4


def _mla_prep_bwd(dqt, dk, dv, cc, sa, sb, name):
    _, nq, _, tq = dqt.shape
    T = nq * tq
    gw = ATT_HG * MLA_HP

    def body(dq_ref, dk_ref, dv_ref, cc_ref, sa_ref, sb_ref, dqp_ref, dkv_ref, dkr_ref):
        @pl.when(pl.program_id(1) == 0)
        def _():
            dkr_ref[...] = jnp.zeros_like(dkr_ref)

        cc, sa, sb = cc_ref[...], sa_ref[...], sb_ref[...]
        acc = jnp.zeros((tq, 128), F32)
        for hh in range(ATT_HG):
            a, m, b = hh * MLA_HP, hh * MLA_HP + MLA_NOPE, (hh + 1) * MLA_HP
            dqh = dq_ref[hh].astype(F32).T * SM_SCALE
            dqp_ref[:, a:m] = dqh[:, :MLA_NOPE].astype(BF16)
            dqp_ref[:, m:b] = _rope_bwd(dqh[:, MLA_NOPE:], cc, sa, sb).astype(BF16)
            dkv_ref[:, a:m] = dk_ref[:, a:m]
            dkv_ref[:, m:b] = dv_ref[:, hh * MLA_V:(hh + 1) * MLA_V]
            acc = acc + dk_ref[:, m:b].astype(F32)
        dkr_ref[...] += acc

    tab = pl.BlockSpec((tq, 128), lambda i, g: (i, 0))
    return pl.pallas_call(
        body, name=name, grid=(nq, MLA_H // ATT_HG),
        in_specs=[pl.BlockSpec((ATT_HG, None, MLA_HP, tq), lambda i, g: (g, i, 0, 0)), pl.BlockSpec((tq, gw), lambda i, g: (i, g)),
                  pl.BlockSpec((tq, ATT_HG * MLA_V), lambda i, g: (i, g)), tab, tab, tab],
        out_specs=[pl.BlockSpec((tq, gw), lambda i, g: (i, g)), pl.BlockSpec((tq, gw), lambda i, g: (i, g)), tab],
        out_shape=[jax.ShapeDtypeStruct((T, MLA_H * MLA_HP), BF16), jax.ShapeDtypeStruct((T, MLA_H * MLA_HP), BF16), jax.ShapeDtypeStruct((T, 128), F32)],
        compiler_params=pltpu.CompilerParams(dimension_semantics=("parallel", "arbitrary")),
    )(dqt, dk, dv, cc, sa, sb)


_NT = (((1,), (1,)), ((), ()))


def _as_row(col, n):
    return jnp.broadcast_to(col, (n, 128)).T[0:1, :]


def _attn_fwd(q, k, vt, name):
    T = q.shape[0]
    tq = tk = min(T, ATT_TILE)
    nq = T // tq
    hg = ATT_FWD_HEADS

    def body(q_ref, k_ref, vt_ref, o_ref, lse_ref, m_ref, l_ref, acc_ref):
        i = pl.program_id(1)
        m_ref[...] = jnp.full_like(m_ref, NEG_INF)
        l_ref[...] = jnp.zeros_like(l_ref)
        acc_ref[...] = jnp.zeros_like(acc_ref)

        def step(j, diag):
            off = pl.multiple_of(j * tk, tk)
            sts = [lax.dot_general(k_ref[pl.ds(off, tk), hh * MLA_HP:(hh + 1) * MLA_HP], q_ref[:, hh * MLA_HP:(hh + 1) * MLA_HP], _NT,
                                   preferred_element_type=F32) for hh in range(hg)]
            for hh in range(hg):
                st = sts[hh]
                if diag:
                    st = jnp.where(lax.broadcasted_iota(jnp.int32, (tk, tq), 0) <= lax.broadcasted_iota(jnp.int32, (tk, tq), 1), st, NEG_INF)
                m_prev = m_ref[hh]
                m_new = jnp.maximum(m_prev, jnp.max(st, axis=0, keepdims=True))
                alpha = jnp.exp(m_prev - m_new)
                pt = jnp.exp(st - m_new)
                l_ref[hh] = alpha * l_ref[hh] + jnp.sum(pt, axis=0, keepdims=True)
                acc_ref[hh] = alpha * acc_ref[hh] + jnp.dot(vt_ref[hh, j], pt.astype(BF16), preferred_element_type=F32)
                m_ref[hh] = m_new

        def loop_body(j, carry):
            step(j, False)
            return carry

        lax.fori_loop(0, i, loop_body, 0)
        step(i, True)
        for hh in range(hg):
            o_ref[:, hh * MLA_V:(hh + 1) * MLA_V] = (acc_ref[hh] / l_ref[hh]).T.astype(BF16)
            lse_ref[hh] = m_ref[hh] + jnp.log(l_ref[hh])

    return pl.pallas_call(
        body, name=name, grid=(MLA_H // hg, nq),
        in_specs=[pl.BlockSpec((tq, hg * MLA_HP), lambda h, i: (i, h)), pl.BlockSpec((T, hg * MLA_HP), lambda h, i: (0, h)),
                  pl.BlockSpec((hg, nq, MLA_V, tk), lambda h, i: (h, 0, 0, 0))],
        out_specs=[pl.BlockSpec((tq, hg * MLA_V), lambda h, i: (i, h)), pl.BlockSpec((hg, None, 1, tq), lambda h, i: (h, i, 0, 0))],
        out_shape=[jax.ShapeDtypeStruct((T, MLA_H * MLA_V), BF16), jax.ShapeDtypeStruct((MLA_H, nq, 1, tq), F32)],
        scratch_shapes=[pltpu.VMEM((hg, 1, tq), F32), pltpu.VMEM((hg, 1, tq), F32), pltpu.VMEM((hg, MLA_V, tq), F32)],
        compiler_params=pltpu.CompilerParams(dimension_semantics=("parallel", "arbitrary")),
    )(q, k, vt)


def _attn_delta(do, o, name):
    T = do.shape[0]
    tq = min(T, ATT_TILE)

    def body(do_ref, o_ref, d_ref):
        for hh in range(MLA_H):
            cs = slice(hh * MLA_V, (hh + 1) * MLA_V)
            s = jnp.sum(do_ref[:, cs].astype(F32) * o_ref[:, cs].astype(F32), axis=-1, keepdims=True)
            d_ref[hh] = _as_row(s, tq)

    return pl.pallas_call(
        body, name=name, grid=(T // tq,),
        in_specs=[_row_spec(tq, MLA_H * MLA_V), _row_spec(tq, MLA_H * MLA_V)],
        out_specs=pl.BlockSpec((MLA_H, None, 1, tq), lambda i: (0, i, 0, 0)),
        out_shape=jax.ShapeDtypeStruct((MLA_H, T // tq, 1, tq), F32),
        compiler_params=pltpu.CompilerParams(dimension_semantics=("parallel",)),
    )(do, o)


def _attn_bwd(q, k, kt, v, do, lse, delta, name):
    T = q.shape[0]
    tq = tk = min(T, ATT_TILE)
    nq = nk = T // tq
    tsd = min(tq, ATT_SUB)
    hg = ATT_BWD_HEADS

    def body(q_ref, k_ref, kt_ref, v_ref, do_ref, lse_ref, dl_ref, dqt_ref, dk_ref, dv_ref, dq_acc, dk_acc, dv_acc):
        j = pl.program_id(1)

        @pl.when(j == 0)
        def _():
            dq_acc[...] = jnp.zeros_like(dq_acc)

        dk_acc[...] = jnp.zeros_like(dk_acc)
        dv_acc[...] = jnp.zeros_like(dv_acc)

        def step(i, diag):
            off = pl.multiple_of(i * tq, tq)
            ts, nsub = (tsd, tq // tsd) if diag else (tq, 1)
            for u in range(nsub):
                cols = slice(u * ts, (u + 1) * ts)
                nk_u = (u + 1) * ts if diag else tk
                rows = pl.ds(off + u * ts, ts)
                pre = []
                for hh in range(hg):
                    hq, hv = slice(hh * MLA_HP, (hh + 1) * MLA_HP), slice(hh * MLA_V, (hh + 1) * MLA_V)
                    qi, doi = q_ref[rows, hq], do_ref[rows, hv]
                    st = lax.dot_general(k_ref[:nk_u, hq], qi, _NT, preferred_element_type=F32)
                    dpt = lax.dot_general(v_ref[:nk_u, hv], doi, _NT, preferred_element_type=F32)
                    pre.append((qi, doi, st, dpt))
                for hh in range(hg):
                    hq, hv = slice(hh * MLA_HP, (hh + 1) * MLA_HP), slice(hh * MLA_V, (hh + 1) * MLA_V)
                    qi, doi, st, dpt = pre[hh]
                    if diag:
                        qcol = u * ts + lax.broadcasted_iota(jnp.int32, (nk_u, ts), 1)
                        st = jnp.where(lax.broadcasted_iota(jnp.int32, (nk_u, ts), 0) <= qcol, st, NEG_INF)
                    pt = jnp.exp(st - lse_ref[hh, i][:, cols])
                    dv_acc[:nk_u, hv] += jnp.dot(pt.astype(BF16), doi, preferred_element_type=F32)
                    dsb = (pt * (dpt - dl_ref[hh, i][:, cols])).astype(BF16)
                    dk_acc[:nk_u, hq] += jnp.dot(dsb, qi, preferred_element_type=F32)
                    dq_acc[hh, i, :, cols] += jnp.dot(kt_ref[hq, :nk_u], dsb, preferred_element_type=F32)

        def loop_body(i, carry):
            step(i, False)
            return carry

        step(j, True)
        lax.fori_loop(j + 1, nq, loop_body, 0)
        dk_ref[...] = dk_acc[...].astype(BF16)
        dv_ref[...] = dv_acc[...].astype(BF16)

        @pl.when(j == nk - 1)
        def _():
            dqt_ref[...] = dq_acc[...].astype(BF16)

    stat = pl.BlockSpec((hg, nq, 1, tq), lambda h, j: (h, 0, 0, 0))
    return pl.pallas_call(
        body, name=name, grid=(MLA_H // hg, nk),
        in_specs=[pl.BlockSpec((T, hg * MLA_HP), lambda h, j: (0, h)), pl.BlockSpec((tk, hg * MLA_HP), lambda h, j: (j, h)),
                  pl.BlockSpec((hg * MLA_HP, tk), lambda h, j: (h, j)), pl.BlockSpec((tk, hg * MLA_V), lambda h, j: (j, h)),
                  pl.BlockSpec((T, hg * MLA_V), lambda h, j: (0, h)), stat, stat],
        out_specs=[pl.BlockSpec((hg, nq, MLA_HP, tq), lambda h, j: (h, 0, 0, 0)), pl.BlockSpec((tk, hg * MLA_HP), lambda h, j: (j, h)),
                   pl.BlockSpec((tk, hg * MLA_V), lambda h, j: (j, h))],
        out_shape=[jax.ShapeDtypeStruct((MLA_H, nq, MLA_HP, tq), BF16), jax.ShapeDtypeStruct((T, MLA_H * MLA_HP), BF16),
                   jax.ShapeDtypeStruct((T, MLA_H * MLA_V), BF16)],
        scratch_shapes=[pltpu.VMEM((hg, nq, MLA_HP, tq), F32), pltpu.VMEM((tk, hg * MLA_HP), F32), pltpu.VMEM((tk, hg * MLA_V), F32)],
        compiler_params=pltpu.CompilerParams(dimension_semantics=("parallel", "arbitrary")),
    )(q, k, kt, v, do, lse, delta)


ADA_TN = 512


def _silu(v):
    return v * (1.0 / (1.0 + jnp.exp(-v)))


def _ada_fwd(c_all, ada_w, ada_b_loc, name):
    L, D, Nc = ada_w.shape
    B = c_all.shape[0]

    def body(c_ref, w_ref, b_ref, o_ref):
        ca = _silu(c_ref[...]).astype(BF16)
        o_ref[...] = jnp.dot(ca, w_ref[...].astype(BF16), preferred_element_type=F32) + b_ref[...]

    return pl.pallas_call(
        body, name=name, grid=(L, Nc // ADA_TN),
        in_specs=[pl.BlockSpec((B, D), lambda l, n: (0, 0)), pl.BlockSpec((None, D, ADA_TN), lambda l, n: (l, 0, n)),
                  pl.BlockSpec((None, 1, ADA_TN), lambda l, n: (l, 0, n))],
        out_specs=pl.BlockSpec((None, B, ADA_TN), lambda l, n: (l, 0, n)),
        out_shape=jax.ShapeDtypeStruct((L, B, Nc), F32),
        compiler_params=pltpu.CompilerParams(dimension_semantics=("parallel", "parallel")),
    )(c_all, ada_w, ada_b_loc)


def _ada_bwd(c_all_t, dmod_loc, name):
    D, B = c_all_t.shape
    L, _, Nc = dmod_loc.shape

    def body(c_ref, d_ref, o_ref):
        ca = _silu(c_ref[...])
        dv = d_ref[...]
        acc = ca[:, 0:1] * dv[0:1, :]
        for b in range(1, B):
            acc = acc + ca[:, b:b + 1] * dv[b:b + 1, :]
        o_ref[...] = acc

    return pl.pallas_call(
        body, name=name, grid=(L, Nc // ADA_TN),
        in_specs=[pl.BlockSpec((D, B), lambda l, n: (0, 0)), pl.BlockSpec((None, B, ADA_TN), lambda l, n: (l, 0, n))],
        out_specs=pl.BlockSpec((None, D, ADA_TN), lambda l, n: (l, 0, n)),
        out_shape=jax.ShapeDtypeStruct((L, D, Nc), F32),
        compiler_params=pltpu.CompilerParams(dimension_semantics=("parallel", "parallel")),
    )(c_all_t, dmod_loc)


def _sum_lead(parts, name, out_dtype=F32):
    R, C = parts[0].shape[1:]
    n_tot = sum(p.shape[0] for p in parts)
    tr = R
    for cand in (512, 256, 128, 64, 32, 16):
        if R % cand == 0 and cand * C * 4 * n_tot <= (8 << 20):
            tr = cand
            break

    def body(*refs):
        o_ref = refs[-1]
        acc = None
        for r in refs[:-1]:
            for s in range(r.shape[0]):
                acc = r[s].astype(F32) if acc is None else acc + r[s].astype(F32)
        o_ref[...] = acc.astype(o_ref.dtype)

    return pl.pallas_call(
        body, name=name, grid=(R // tr,),
        in_specs=[pl.BlockSpec((p.shape[0], tr, C), lambda i: (0, i, 0)) for p in parts],
        out_specs=pl.BlockSpec((tr, C), lambda i: (i, 0)),
        out_shape=jax.ShapeDtypeStruct((R, C), out_dtype),
        compiler_params=pltpu.CompilerParams(dimension_semantics=("parallel",)),
    )(*parts)


_ADAM_C1 = 1.0 - ADAM_B1 ** ADAM_STEP
_ADAM_C2 = 1.0 - ADAM_B2 ** ADAM_STEP


def _adamw(w, g, m, v, name):
    shape = w.shape
    C = shape[-1]
    R = math.prod(shape[:-1]) if len(shape) > 1 else 1
    w2, g2, m2, v2 = (a.reshape(R, C) for a in (w, g, m, v))
    tr = R
    for cand in (1024, 512, 256, 128, 64, 32, 16, 8):
        if R % cand == 0 and cand * C * 4 <= (1 << 20):
            tr = cand
            break

    def body(w_ref, g_ref, m_ref, v_ref, d_ref, nm_ref, nv_ref):
        gv = g_ref[...]
        mn = ADAM_B1 * m_ref[...] + (1.0 - ADAM_B1) * gv
        vn = ADAM_B2 * v_ref[...] + (1.0 - ADAM_B2) * (gv * gv)
        nm_ref[...] = mn
        nv_ref[...] = vn
        m_hat = mn / _ADAM_C1
        v_hat = vn / _ADAM_C2
        d_ref[...] = -ADAM_LR * (m_hat / (jnp.sqrt(v_hat) + ADAM_EPS) + ADAM_WD * w_ref[...])

    spec = pl.BlockSpec((tr, C), lambda i: (i, 0))
    outs = pl.pallas_call(
        body, name=name, grid=(R // tr,),
        in_specs=[spec] * 4, out_specs=[spec] * 3,
        out_shape=[jax.ShapeDtypeStruct((R, C), F32)] * 3,
        compiler_params=pltpu.CompilerParams(dimension_semantics=("parallel",)),
    )(w2, g2, m2, v2)
    return tuple(o.reshape(shape) for o in outs)


def _row_tile(rows, cols, itemsize, budget):
    for cand in (1024, 512, 256, 128, 64, 32, 16):
        if rows % cand == 0 and cand * cols * itemsize <= budget:
            return cand
    return rows


def _sum_sel(sel, stacked, others, name, out_dtype):
    R, C = stacked.shape[1:]
    n_tot = 1 + sum(o.shape[0] for o in others)
    tr = _row_tile(R, C, 4 * n_tot, 8 << 20)

    def body(sel_ref, s_ref, *refs):
        o_ref = refs[-1]
        acc = s_ref[...].astype(F32)
        for r in refs[:-1]:
            for s in range(r.shape[0]):
                acc = acc + r[s].astype(F32)
        o_ref[...] = acc.astype(o_ref.dtype)

    return pl.pallas_call(
        body, name=name,
        grid_spec=pltpu.PrefetchScalarGridSpec(
            num_scalar_prefetch=1, grid=(R // tr,),
            in_specs=[pl.BlockSpec((None, tr, C), lambda i, s: (s[0], i, 0))] + [pl.BlockSpec((o.shape[0], tr, C), lambda i, s: (0, i, 0)) for o in others],
            out_specs=pl.BlockSpec((tr, C), lambda i, s: (i, 0))),
        out_shape=jax.ShapeDtypeStruct((R, C), out_dtype),
        compiler_params=pltpu.CompilerParams(dimension_semantics=("parallel",)),
    )(sel, stacked, *others)


def _adamw_piece(cidx, w2, m2, v2, mine, got, bufs, row0, name):
    hr, C = mine.shape
    tr = _row_tile(math.gcd(hr, row0) if row0 else hr, C, 4, 1 << 20)
    nt = hr // tr

    def body(c_ref, w_ref, m_ref, v_ref, a_ref, b_ref, _g, _d, _nm, _nv, g_ref, d_ref, nm_ref, nv_ref):
        gv = jnp.where(pl.program_id(0) == c_ref[0], a_ref[...], b_ref[...])
        mn = ADAM_B1 * m_ref[...] + (1.0 - ADAM_B1) * gv
        vn = ADAM_B2 * v_ref[...] + (1.0 - ADAM_B2) * (gv * gv)
        g_ref[...] = gv
        nm_ref[...] = mn
        nv_ref[...] = vn
        d_ref[...] = -ADAM_LR * ((mn / _ADAM_C1) / (jnp.sqrt(vn / _ADAM_C2) + ADAM_EPS) + ADAM_WD * w_ref[...])

    rows = pl.BlockSpec((tr, C), lambda hf, t, c: (row0 // tr + hf * nt + t, 0))
    half = pl.BlockSpec((tr, C), lambda hf, t, c: (t, 0))
    return pl.pallas_call(
        body, name=name,
        grid_spec=pltpu.PrefetchScalarGridSpec(num_scalar_prefetch=1, grid=(2, nt), in_specs=[rows] * 3 + [half] * 2 + [_ANY_SPEC] * 4,
                                               out_specs=[rows] * 4),
        out_shape=[jax.ShapeDtypeStruct(w2.shape, F32)] * 4,
        input_output_aliases={6 + n: n for n in range(4)},
        compiler_params=pltpu.CompilerParams(dimension_semantics=("parallel", "parallel")),
    )(cidx, w2, m2, v2, mine, got, *bufs)


_VMEM_SPEC = pl.BlockSpec(memory_space=pltpu.VMEM)
_HBM_SPEC = pl.BlockSpec(memory_space=pltpu.HBM)


def _flip(v, bit):
    return (1 - v) if bit else v


def _allgather8(v, name):
    def body(v_ref, out_ref, send_sems, recv_sems, local_sem):
        x, y, c = _idx()
        me = 4 * x + 2 * y + c
        mine = pltpu.make_async_copy(v_ref, out_ref.at[me], local_sem)
        mine.start()
        sends = []
        for k in range(1, N_DEV):
            peer = (_flip(x, k & 4), _flip(y, k & 2), _flip(c, k & 1))
            cp = pltpu.make_async_remote_copy(src_ref=v_ref, dst_ref=out_ref.at[me], send_sem=send_sems.at[k - 1], recv_sem=recv_sems.at[k - 1],
                                              device_id=peer, device_id_type=MESH)
            cp.start()
            sends.append(cp)
        for k in range(1, N_DEV):
            px, py, pc = _flip(x, k & 4), _flip(y, k & 2), _flip(c, k & 1)
            src = 4 * px + 2 * py + pc
            pltpu.make_async_remote_copy(src_ref=v_ref, dst_ref=out_ref.at[src], send_sem=send_sems.at[k - 1], recv_sem=recv_sems.at[k - 1],
                                         device_id=(px, py, pc), device_id_type=MESH).wait_recv()
        for cp in sends:
            cp.wait_send()
        mine.wait()

    return pl.pallas_call(
        body, name=name,
        out_shape=jax.ShapeDtypeStruct((N_DEV, *v.shape), v.dtype),
        in_specs=[_VMEM_SPEC], out_specs=_VMEM_SPEC,
        scratch_shapes=[pltpu.SemaphoreType.DMA((N_DEV - 1,)), pltpu.SemaphoreType.DMA((N_DEV - 1,)), pltpu.SemaphoreType.DMA],
    )(v)


def _mod_exchange(modp, name):
    _, L, Nc = modp.shape

    def body(p_ref, out_ref, send_sems, recv_sems, local_sem):
        x, y, c = _idx()
        me, chip = 4 * x + 2 * y + c, 2 * x + y
        mine = pltpu.make_async_copy(p_ref.at[me], out_ref.at[chip], local_sem)
        mine.start()
        sends = []
        for k in range(1, N_CHIPS):
            px, py = _flip(x, k & 2), _flip(y, k & 1)
            cp = pltpu.make_async_remote_copy(src_ref=p_ref.at[4 * px + 2 * py + c], dst_ref=out_ref.at[chip],
                                              send_sem=send_sems.at[k - 1], recv_sem=recv_sems.at[k - 1], device_id=(px, py, c), device_id_type=MESH)
            cp.start()
            sends.append(cp)
        for k in range(1, N_CHIPS):
            px, py = _flip(x, k & 2), _flip(y, k & 1)
            pltpu.make_async_remote_copy(src_ref=p_ref.at[me], dst_ref=out_ref.at[2 * px + py], send_sem=send_sems.at[k - 1],
                                         recv_sem=recv_sems.at[k - 1], device_id=(px, py, c), device_id_type=MESH).wait_recv()
        for cp in sends:
            cp.wait_send()
        mine.wait()

    return pl.pallas_call(
        body, name=name,
        out_shape=jax.ShapeDtypeStruct((N_CHIPS, L, Nc), modp.dtype),
        in_specs=[_VMEM_SPEC], out_specs=_VMEM_SPEC,
        scratch_shapes=[pltpu.SemaphoreType.DMA((N_CHIPS - 1,)), pltpu.SemaphoreType.DMA((N_CHIPS - 1,)), pltpu.SemaphoreType.DMA],
    )(modp)


_SEM_SPEC = pl.BlockSpec(memory_space=pltpu.SEMAPHORE)
_ANY_SPEC = pl.BlockSpec(memory_space=pl.ANY)
_EFFECT = pltpu.SideEffectType.DATAFLOW_SIDE_EFFECTING


def _hbm(a):
    return pltpu.with_memory_space_constraint(a, pltpu.HBM)


def _xchip_copies(mode, srcs, lands, send_sems, recv_sems, waiting):
    x, y, c = _idx()
    chip = 2 * x + y
    out = []
    for a in range(len(srcs)):
        for k in range(1, N_CHIPS):
            px, py = _flip(x, k & 2), _flip(y, k & 1)
            peer = 2 * px + py
            if mode == "gather":
                src, dst, mine = srcs[a].at[c], lands[a].at[chip, c], lands[a].at[peer, c]
            else:
                src, dst, mine = srcs[a].at[peer], lands[a].at[k - 1], lands[a].at[k - 1]
            q = a * (N_CHIPS - 1) + k - 1
            out.append(pltpu.make_async_remote_copy(src_ref=src, dst_ref=mine if waiting else dst, send_sem=send_sems[q], recv_sem=recv_sems[q],
                                                    device_id=(px, py, c), device_id_type=MESH))
    return out


def _xchip_start(mode, srcs, land_shapes, dep, name):
    n = len(srcs)
    ns = n * (N_CHIPS - 1)

    def body(*refs):
        src_refs, land_refs = refs[:n], refs[n:2 * n]
        outs = refs[2 * n + 1:]
        for cp in _xchip_copies(mode, src_refs, land_refs, outs[:ns], outs[ns:2 * ns], waiting=False):
            cp.start()
        outs[-1][...] = jnp.zeros_like(outs[-1])

    lands = [_hbm(lax.empty(s.shape, s.dtype)) for s in land_shapes]
    outs = pl.pallas_call(
        body, name=name,
        out_shape=(*[pltpu.SemaphoreType.DMA(())] * (2 * ns), *[pltpu.HBM(s.shape, s.dtype) for s in srcs],
                   *[pltpu.HBM(s.shape, s.dtype) for s in land_shapes], jax.ShapeDtypeStruct((8, 128), F32)),
        in_specs=[_HBM_SPEC] * (2 * n) + [_ANY_SPEC],
        out_specs=(*[_SEM_SPEC] * (2 * ns), *[_HBM_SPEC] * (2 * n), _VMEM_SPEC),
        input_output_aliases={i: 2 * ns + i for i in range(2 * n)},
        compiler_params=pltpu.CompilerParams(has_side_effects=_EFFECT),
    )(*[_hbm(s) for s in srcs], *lands, dep)
    return list(outs[:ns]), list(outs[ns:2 * ns]), list(outs[2 * ns:2 * ns + n]), list(outs[2 * ns + n:2 * ns + 2 * n]), outs[-1]


def _xchip_wait(mode, send_sems, recv_sems, srcs, lands, after, name):
    n = len(srcs)
    ns = n * (N_CHIPS - 1)

    def body(*refs):
        src_refs, land_refs = refs[:n], refs[n:2 * n]
        sems = refs[2 * n:2 * n + 2 * ns]
        for cp in _xchip_copies(mode, src_refs, land_refs, sems[:ns], sems[ns:], waiting=True):
            cp.wait_send()
            cp.wait_recv()

    outs = pl.pallas_call(
        body, name=name,
        out_shape=(*[pltpu.HBM(s.shape, s.dtype) for s in srcs], *[pltpu.HBM(s.shape, s.dtype) for s in lands]),
        in_specs=[_HBM_SPEC] * (2 * n) + [_SEM_SPEC] * (2 * ns) + [_ANY_SPEC] * len(after),
        out_specs=tuple([_HBM_SPEC] * (2 * n)),
        input_output_aliases={i: i for i in range(2 * n)},
        compiler_params=pltpu.CompilerParams(has_side_effects=_EFFECT),
    )(*srcs, *lands, *send_sems, *recv_sems, *after)
    return list(outs[:n]), list(outs[n:])


def _sibling_fwd(lands, name):
    n = len(lands)

    def body(*refs):
        outs = refs[n:2 * n]
        send_sems, recv_sems = refs[2 * n:]
        x, y, c = _idx()
        sib = (x, y, 1 - c)
        sends = []
        for a in range(n):
            for k in range(1, N_CHIPS):
                src = 2 * _flip(x, k & 2) + _flip(y, k & 1)
                cp = pltpu.make_async_remote_copy(src_ref=outs[a].at[src, c], dst_ref=outs[a].at[src, c], send_sem=send_sems.at[a, k - 1],
                                                  recv_sem=recv_sems.at[a, k - 1], device_id=sib, device_id_type=MESH)
                cp.start()
                sends.append(cp)
        for a in range(n):
            for k in range(1, N_CHIPS):
                src = 2 * _flip(x, k & 2) + _flip(y, k & 1)
                pltpu.make_async_remote_copy(src_ref=outs[a].at[src, c], dst_ref=outs[a].at[src, 1 - c], send_sem=send_sems.at[a, k - 1],
                                             recv_sem=recv_sems.at[a, k - 1], device_id=sib, device_id_type=MESH).wait_recv()
        for cp in sends:
            cp.wait_send()

    return pl.pallas_call(
        body, name=name,
        out_shape=[jax.ShapeDtypeStruct(s.shape, s.dtype) for s in lands],
        in_specs=[_HBM_SPEC] * n, out_specs=[_HBM_SPEC] * n,
        input_output_aliases={i: i for i in range(n)},
        scratch_shapes=[pltpu.SemaphoreType.DMA((n, N_CHIPS - 1)), pltpu.SemaphoreType.DMA((n, N_CHIPS - 1))],
    )(*lands)


def _sibling_swap(parts, name):
    n = len(parts)

    def body(*refs):
        ins, outs = refs[:n], refs[n:2 * n]
        send_sems, recv_sems = refs[2 * n:]
        x, y, c = _idx()
        cps = []
        for a in range(n):
            cp = pltpu.make_async_remote_copy(src_ref=ins[a].at[1 - c], dst_ref=outs[a], send_sem=send_sems.at[a], recv_sem=recv_sems.at[a],
                                              device_id=(x, y, 1 - c), device_id_type=MESH)
            cp.start()
            cps.append(cp)
        for cp in cps:
            cp.wait()

    return pl.pallas_call(
        body, name=name,
        out_shape=[jax.ShapeDtypeStruct(p.shape[1:], p.dtype) for p in parts],
        in_specs=[_HBM_SPEC] * n, out_specs=[_HBM_SPEC] * n,
        scratch_shapes=[pltpu.SemaphoreType.DMA((n,)), pltpu.SemaphoreType.DMA((n,))],
    )(*parts)


def _sibling_send(halves, name):
    n = len(halves)

    def body(*refs):
        ins, outs = refs[:n], refs[n:2 * n]
        send_sems, recv_sems = refs[2 * n:]
        x, y, c = _idx()
        cps = []
        for a in range(n):
            cp = pltpu.make_async_remote_copy(src_ref=ins[a], dst_ref=outs[a], send_sem=send_sems.at[a], recv_sem=recv_sems.at[a],
                                              device_id=(x, y, 1 - c), device_id_type=MESH)
            cp.start()
            cps.append(cp)
        for cp in cps:
            cp.wait()

    return pl.pallas_call(
        body, name=name,
        out_shape=[jax.ShapeDtypeStruct(h.shape, h.dtype) for h in halves],
        in_specs=[_HBM_SPEC] * n, out_specs=[_HBM_SPEC] * n,
        scratch_shapes=[pltpu.SemaphoreType.DMA((n,)), pltpu.SemaphoreType.DMA((n,))],
    )(*halves)


def _col_full(g):
    k, n = g.shape[1], g.shape[2]
    return g.transpose(1, 0, 2).reshape(k, N_CHIPS * n)


def _col_blocks(w):
    k, n = w.shape
    return w.reshape(k, N_CHIPS, n // N_CHIPS).transpose(1, 0, 2)


def _row_blocks(w):
    k, n = w.shape
    return w.reshape(N_CHIPS, k // N_CHIPS, n)


_UQ_HEAD = MLA_NOPE + MLA_ROPE

_LAT = MLA_QL + MLA_KVL + MLA_ROPE
_POOL_R = len(POOL_WINDOWS) * (POOL_GD // N_CHIPS)

_PIECE_KINDS = {
    "mlp_w1": (D_MODEL, D_MODEL, lambda g: g, _col_blocks),
    "mlp_w2": (D_MODEL, D_MODEL, lambda g: g.reshape(4 * D_MODEL, D_MODEL), _row_blocks),
    "pool_w": (_POOL_R, POOL_GD,
               lambda g: g.reshape(N_CHIPS, len(POOL_WINDOWS), POOL_GD // N_CHIPS, POOL_GD).transpose(1, 0, 2, 3).reshape(len(POOL_WINDOWS), POOL_GD, POOL_GD),
               lambda w: w.reshape(len(POOL_WINDOWS), N_CHIPS, POOL_GD // N_CHIPS, POOL_GD).transpose(1, 0, 2, 3).reshape(N_CHIPS, _POOL_R, POOL_GD)),
    "sgu_w_in": (D_MODEL, 2 * SGU_W // N_CHIPS, _col_full, _col_blocks),
    "sgu_w_out": (SGU_W // N_CHIPS, D_MODEL, lambda g: g.reshape(SGU_W, D_MODEL), _row_blocks),
    "mla_w_dq_dkv": (D_MODEL // N_CHIPS, _LAT, lambda g: jnp.pad(g.reshape(D_MODEL, _LAT), ((0, 0), (0, MLA_LATP - _LAT))),
                     lambda w: _row_blocks(w[:, :_LAT])),
    "mla_w_uq": (MLA_QL, MLA_H * _UQ_HEAD // N_CHIPS,
                 lambda g: jnp.pad(_col_full(g).reshape(MLA_QL, MLA_H, _UQ_HEAD), ((0, 0), (0, 0), (0, MLA_HP - _UQ_HEAD))).reshape(MLA_QL, MLA_H * MLA_HP),
                 lambda w: _col_blocks(w.reshape(MLA_QL, MLA_H, MLA_HP)[:, :, :_UQ_HEAD].reshape(MLA_QL, MLA_H * _UQ_HEAD))),
    "mla_w_ukv": (MLA_KVL, MLA_H * (MLA_NOPE + MLA_V) // N_CHIPS, _col_full, _col_blocks),
    "mla_w_o": (MLA_H * MLA_V // N_CHIPS, D_MODEL, lambda g: g.reshape(MLA_H * MLA_V, D_MODEL), _row_blocks),
}
_MIXER_KINDS = (("pool_w",), ("sgu_w_in", "sgu_w_out"), ("mla_w_dq_dkv", "mla_w_uq", "mla_w_ukv", "mla_w_o"))


def _layer_pieces(i):
    return [(k, i // N_MIXERS) for k in _MIXER_KINDS[i % N_MIXERS]] + [("mlp_w1", i), ("mlp_w2", i)]


def _rope_tables(positions):
    inv_freq = ROPE_THETA ** (-jnp.arange(0, MLA_ROPE, 2, dtype=F32) / MLA_ROPE)
    ang = positions.astype(F32)[:, None] * inv_freq
    cos, sin = jnp.cos(ang), jnp.sin(ang)
    z32, z64 = jnp.zeros_like(cos), jnp.zeros((positions.shape[0], 64), F32)
    return (jnp.concatenate([cos, cos, z64], axis=1), jnp.concatenate([-sin, z32, z64], axis=1), jnp.concatenate([z32, sin, z64], axis=1))


def _local_step(x, positions, target, mod, S, weights_of, grads_of):
    D = D_MODEL
    cc, sa, sb = _rope_tables(positions)
    mods = [[mod[i:i + 1, n * D:(n + 1) * D] for n in range(6)] for i in range(DEPTH)]
    h_dtype = lambda i: F32 if i % N_MIXERS == 0 else BF16
    saved = []
    h = _norm_mod_fwd(x, S["norm_mix_g"][0:1], mods[0][1], mods[0][0], h_dtype(0), "l0_norm1")
    for i in range(DEPTH):
        sh1, sc1, g1, sh2, sc2, g2 = mods[i]
        kind, j = i % N_MIXERS, i // N_MIXERS
        gmlp = S["norm_mlp_g"][i:i + 1]
        W = weights_of(i, "mix", x)
        st = {"x": x}
        norm2 = ((gmlp, "n"), (sc2, "n"), (sh2, "n"))
        if kind == 0:
            x2, pooled, ypre, h2 = _pool_fwd(h, W["pool_w"], S["pool_scale"][j:j + 1], x, g1, gmlp, sc2, sh2, f"l{i}_pool")
            st.update(pooled=pooled, y=ypre)
        elif kind == 1:
            zz = _mm(h, W["sgu_w_in"], out_dtypes=(F32,), name=f"l{i}_sgu_in")
            bs_t = S["sgu_b_s"].T
            gated = _sgu_gate_fwd(zz, S["sgu_ln_g"], S["sgu_ln_b"], S["sgu_w_s"], bs_t, f"l{i}_sgu_gate")
            x2, y, h2 = _mm(gated, W["sgu_w_out"], epi=_epi_residual_norm, extras=((x, "mn"), (g1, "n"), *norm2), out_dtypes=(F32, BF16, BF16),
                            tn=D, name=f"l{i}_sgu_out")
            st.update(h=h, zz=zz, gated=gated, y=y, bs_t=bs_t)
        else:
            lat = _mm(h, W["mla_w_dq_dkv"], out_dtypes=(F32,), name=f"l{i}_mla_lat")
            cqn, ckvn, krot = _mla_lat_fwd(lat, S["mla_q_norm_g"], S["mla_kv_norm_g"], cc, sa, sb, f"l{i}_mla_latn")
            qpad = _mm(cqn, W["mla_w_uq"], out_dtypes=(F32,), name=f"l{i}_mla_uq")
            kv = _mm(ckvn, W["mla_w_ukv"], out_dtypes=(F32,), name=f"l{i}_mla_ukv")
            q, k, kt, v, vt = _mla_prep(qpad, kv, krot, cc, sa, sb, f"l{i}_mla_prep")
            o, lse = _attn_fwd(q, k, vt, f"l{i}_attn")
            x2, y, h2 = _mm(o, W["mla_w_o"], epi=_epi_residual_norm, extras=((x, "mn"), (g1, "n"), *norm2), out_dtypes=(F32, BF16, BF16),
                            tn=D, name=f"l{i}_mla_o")
            st.update(h=h, lat=lat, cqn=cqn, ckvn=ckvn, q=q, k=k, kt=kt, v=v, o=o, lse=lse, y=y)
        W = {**W, **weights_of(i, "mlp", x2)}
        z, r2 = _mm(h2, W["mlp_w1"], epi=_epi_sq_relu, out_dtypes=(BF16, BF16), name=f"l{i}_mlp1")
        if i + 1 < DEPTH:
            norm1 = ((S["norm_mix_g"][i + 1:i + 2], "n"), (mods[i + 1][1], "n"), (mods[i + 1][0], "n"))
            x3, o2, h = _mm(z, W["mlp_w2"], epi=_epi_residual_norm, extras=((x2, "mn"), (g2, "n"), *norm1), out_dtypes=(F32, BF16, h_dtype(i + 1)),
                            tn=D, name=f"l{i}_mlp2")
        else:
            x3, o2 = _mm(z, W["mlp_w2"], epi=_epi_residual, extras=((x2, "mn"), (g2, "n")), out_dtypes=(F32, BF16), name=f"l{i}_mlp2")
        st.update(x2=x2, h2=h2, z=z, r2=r2, o2=o2, W=W)
        saved.append(st)
        x = x3

    loss, dx, dfinal_g = _loss_head(x, target, S["final_g"], "loss_head")

    gS = {"final_g": dfinal_g, "norm_mix_g": [None] * DEPTH, "norm_mlp_g": [None] * DEPTH, "pool_scale": [None] * 2}
    dmod = [None] * DEPTH
    tok = jnp.zeros((), F32)
    for i in reversed(range(DEPTH)):
        st = saved[i]
        W, gW = st["W"], {}
        sh1, sc1, g1, sh2, sc2, g2 = (mod[i:i + 1, n * D:(n + 1) * D] for n in range(6))
        kind, j = i % N_MIXERS, i // N_MIXERS
        gmix, gmlp = S["norm_mix_g"][i:i + 1], S["norm_mlp_g"][i:i + 1]
        do2, dg2 = _resid_bwd(dx, st["o2"], g2 + tok, f"l{i}_b_res2")
        da = _mm(do2, W["mlp_w2"], tb=True, epi=lambda acc, rt: (acc * rt.astype(F32),), extras=((st["r2"], "mn"),), name=f"l{i}_b_dz")
        gW["mlp_w2"] = _mm(st["z"], do2, ta=True, chip_blocks="row", name=f"l{i}_b_dw2")
        dh2 = _mm(da, W["mlp_w1"], tb=True, out_dtypes=(F32,), name=f"l{i}_b_dh2")
        gW["mlp_w1"] = _mm(st["h2"], da, ta=True, chip_blocks="col", name=f"l{i}_b_dw1")
        dx2, dgmlp, dsc2, dsh2 = _norm_mod_bwd(st["x2"], dh2, dx, gmlp, sc2, f"l{i}_b_norm2")
        gS["norm_mlp_g"][i] = dgmlp
        dy, q1 = _resid_bwd(dx2, st["y"], g1, f"l{i}_b_res1")
        if kind == 0:
            dh, dpw, dpsc, dg1 = _pool_bwd(dy, st["pooled"], W["pool_w"], S["pool_scale"][j:j + 1], g1, q1, f"l{i}_b_pool")
            gW["pool_w"] = dpw.astype(BF16)
            gS["pool_scale"][j] = dpsc
        elif kind == 1:
            dg1 = q1
            dgated = _mm(dy, W["sgu_w_out"], tb=True, name=f"l{i}_b_dgated")
            gW["sgu_w_out"] = _mm(st["gated"], dy, ta=True, name=f"l{i}_b_dwout")
            dzz, dws, dbs, dlg, dlb = _sgu_gate_bwd(st["zz"], dgated, S["sgu_ln_g"], S["sgu_ln_b"], S["sgu_w_s"], st["bs_t"], f"l{i}_b_sgu_gate")
            gS.update(sgu_w_s=dws, sgu_b_s=dbs[:, :, 0], sgu_ln_g=dlg, sgu_ln_b=dlb)
            dh = _mm(dzz, W["sgu_w_in"], tb=True, out_dtypes=(F32,), name=f"l{i}_b_dh_sgu")
            gW["sgu_w_in"] = _mm(st["h"], dzz, ta=True, name=f"l{i}_b_dwin")
        else:
            dg1 = q1
            do = _mm(dy, W["mla_w_o"], tb=True, name=f"l{i}_b_do")
            gW["mla_w_o"] = _mm(st["o"], dy, ta=True, name=f"l{i}_b_dwo")
            delta = _attn_delta(do, st["o"], f"l{i}_b_delta")
            dqt, dk, dv = _attn_bwd(st["q"], st["k"], st["kt"], st["v"], do, st["lse"], delta, f"l{i}_b_attn")
            dqpad, dkv, dkrot = _mla_prep_bwd(dqt, dk, dv, cc, sa, sb, f"l{i}_b_mla_prep")
            dcqn = _mm(dqpad, W["mla_w_uq"], tb=True, out_dtypes=(F32,), name=f"l{i}_b_dcq")
            gW["mla_w_uq"] = _mm(st["cqn"], dqpad, ta=True, name=f"l{i}_b_dwuq")
            dckvn = _mm(dkv, W["mla_w_ukv"], tb=True, out_dtypes=(F32,), name=f"l{i}_b_dckv")
            gW["mla_w_ukv"] = _mm(st["ckvn"], dkv, ta=True, name=f"l{i}_b_dwukv")
            dlat, dqg, dkvg = _mla_lat_bwd(st["lat"], dcqn, dckvn, dkrot, S["mla_q_norm_g"], S["mla_kv_norm_g"], cc, sa, sb, f"l{i}_b_mla_latn")
            gS.update(mla_q_norm_g=dqg, mla_kv_norm_g=dkvg)
            dh = _mm(dlat, W["mla_w_dq_dkv"], tb=True, out_dtypes=(F32,), name=f"l{i}_b_dh_mla")
            gW["mla_w_dq_dkv"] = _mm(st["h"], dlat, ta=True, name=f"l{i}_b_dwdq")
        dx, dgmix, dsc1, dsh1 = _norm_mod_bwd(st["x"], dh, dx2, gmix, sc1, f"l{i}_b_norm1")
        gS["norm_mix_g"][i] = dgmix
        dmod[i] = jnp.concatenate([dsh1, dsc1, dg1, dsh2, dsc2, dg2], axis=1)
        tok = grads_of(i, gW, dx)

    for n in ("norm_mix_g", "norm_mlp_g", "pool_scale"):
        gS[n] = jnp.concatenate(gS[n], axis=0)
    return loss, dx, gS, jnp.concatenate(dmod, axis=0)


_SMALL = {
    "norm_mix_g": (DEPTH, D_MODEL), "norm_mlp_g": (DEPTH, D_MODEL), "sgu_ln_g": (1, SGU_W), "sgu_ln_b": (1, SGU_W),
    "sgu_w_s": (SGU_H, SGU_CHUNK, SGU_CHUNK), "sgu_b_s": (SGU_H, SGU_CHUNK), "mla_kv_norm_g": (1, MLA_KVL), "final_g": (1, D_MODEL),
    "pool_scale": (2, D_MODEL), "mla_q_norm_g": (1, MLA_QL), "dmod": (DEPTH, 6 * D_MODEL),
}
_PACK_W = 1024


def _pack(vals):
    flat = jnp.concatenate([v.reshape(-1) for v in vals])
    rows = -(-flat.shape[0] // (8 * _PACK_W)) * 8
    return jnp.pad(flat, (0, rows * _PACK_W - flat.shape[0])).reshape(rows, _PACK_W)


def _unpack(buf, shapes):
    flat, out, off = buf.reshape(-1), [], 0
    for s in shapes:
        n = math.prod(s)
        out.append(flat[off:off + n].reshape(s))
        off += n
    return out


def kernel(x, c, positions, ada_w, ada_b, norm_mix_g, norm_mlp_g, pool_w, pool_scale, sgu_w_in, sgu_ln_g, sgu_ln_b, sgu_w_s, sgu_b_s, sgu_w_out, mla_w_dq_dkv, mla_q_norm_g, mla_kv_norm_g, mla_w_uq, mla_w_ukv, mla_w_o, mlp_w1, mlp_w2, final_g, loss_target, m_ada_w, m_ada_b, m_norm_mix_g, m_norm_mlp_g, m_pool_w, m_pool_scale, m_sgu_w_in, m_sgu_ln_g, m_sgu_ln_b, m_sgu_w_s, m_sgu_b_s, m_sgu_w_out, m_mla_w_dq_dkv, m_mla_q_norm_g, m_mla_kv_norm_g, m_mla_w_uq, m_mla_w_ukv, m_mla_w_o, m_mlp_w1, m_mlp_w2, m_final_g, v_ada_w, v_ada_b, v_norm_mix_g, v_norm_mlp_g, v_pool_w, v_pool_scale, v_sgu_w_in, v_sgu_ln_g, v_sgu_ln_b, v_sgu_w_s, v_sgu_b_s, v_sgu_w_out, v_mla_w_dq_dkv, v_mla_q_norm_g, v_mla_kv_norm_g, v_mla_w_uq, v_mla_w_ukv, v_mla_w_o, v_mlp_w1, v_mlp_w2, v_final_g):
    P = dict(ada_w=ada_w, ada_b=ada_b, norm_mix_g=norm_mix_g, norm_mlp_g=norm_mlp_g, pool_w=pool_w, pool_scale=pool_scale, sgu_w_in=sgu_w_in,
             sgu_ln_g=sgu_ln_g, sgu_ln_b=sgu_ln_b, sgu_w_s=sgu_w_s, sgu_b_s=sgu_b_s, sgu_w_out=sgu_w_out, mla_w_dq_dkv=mla_w_dq_dkv,
             mla_q_norm_g=mla_q_norm_g, mla_kv_norm_g=mla_kv_norm_g, mla_w_uq=mla_w_uq, mla_w_ukv=mla_w_ukv, mla_w_o=mla_w_o, mlp_w1=mlp_w1,
             mlp_w2=mlp_w2, final_g=final_g)
    M = dict(ada_w=m_ada_w, ada_b=m_ada_b, norm_mix_g=m_norm_mix_g, norm_mlp_g=m_norm_mlp_g, pool_w=m_pool_w, pool_scale=m_pool_scale,
             sgu_w_in=m_sgu_w_in, sgu_ln_g=m_sgu_ln_g, sgu_ln_b=m_sgu_ln_b, sgu_w_s=m_sgu_w_s, sgu_b_s=m_sgu_b_s, sgu_w_out=m_sgu_w_out,
             mla_w_dq_dkv=m_mla_w_dq_dkv, mla_q_norm_g=m_mla_q_norm_g, mla_kv_norm_g=m_mla_kv_norm_g, mla_w_uq=m_mla_w_uq, mla_w_ukv=m_mla_w_ukv,
             mla_w_o=m_mla_w_o, mlp_w1=m_mlp_w1, mlp_w2=m_mlp_w2, final_g=m_final_g)
    V = dict(ada_w=v_ada_w, ada_b=v_ada_b, norm_mix_g=v_norm_mix_g, norm_mlp_g=v_norm_mlp_g, pool_w=v_pool_w, pool_scale=v_pool_scale,
             sgu_w_in=v_sgu_w_in, sgu_ln_g=v_sgu_ln_g, sgu_ln_b=v_sgu_ln_b, sgu_w_s=v_sgu_w_s, sgu_b_s=v_sgu_b_s, sgu_w_out=v_sgu_w_out,
             mla_w_dq_dkv=v_mla_w_dq_dkv, mla_q_norm_g=v_mla_q_norm_g, mla_kv_norm_g=v_mla_kv_norm_g, mla_w_uq=v_mla_w_uq, mla_w_ukv=v_mla_w_ukv,
             mla_w_o=v_mla_w_o, mlp_w1=v_mlp_w1, mlp_w2=v_mlp_w2, final_g=v_final_g)
    order = list(P)
    xi, yi, ci = _idx()
    chip = 2 * xi + yi
    D = D_MODEL
    n_ada = ada_w.shape[2]

    pre = _allgather8(_pack([c, pool_scale, mla_q_norm_g]), "ag_small")
    flat = pre.reshape(N_DEV, -1)
    c_all = flat[:, :D]
    ps_all = flat[0::2, D:D + 2 * (D // N_CHIPS)].reshape(N_CHIPS, 2, D // N_CHIPS).transpose(1, 0, 2).reshape(2, D)
    q0 = D + 2 * (D // N_CHIPS)
    qg_all = flat[0::2, q0:q0 + MLA_QL // N_CHIPS].reshape(1, MLA_QL)

    ada_b_loc = lax.dynamic_slice_in_dim(ada_b, chip * n_ada, n_ada, axis=1)[:, None, :]
    modp = _ada_fwd(c_all, ada_w, ada_b_loc, "ada_fwd")
    mod = _mod_exchange(modp.transpose(1, 0, 2), "mod_exchange").transpose(1, 0, 2).reshape(DEPTH, 6 * D)

    S = dict(norm_mix_g=norm_mix_g, norm_mlp_g=norm_mlp_g, pool_scale=ps_all, sgu_ln_g=sgu_ln_g, sgu_ln_b=sgu_ln_b, sgu_w_s=sgu_w_s[0],
             sgu_b_s=sgu_b_s[0], mla_q_norm_g=qg_all, mla_kv_norm_g=mla_kv_norm_g, final_g=final_g[None, :])
    cidx, chipidx = jnp.reshape(ci, (1,)).astype(jnp.int32), jnp.reshape(chip, (1,)).astype(jnp.int32)
    view2d = lambda a: a.reshape(-1, a.shape[-1])

    def piece_rows(kind, blk):
        r = _PIECE_KINDS[kind][0]
        return blk * r, r

    groups = [_layer_pieces(0)[:-2], _layer_pieces(0)[-2:], _layer_pieces(1)[:-2], _layer_pieces(1)[-2:], _layer_pieces(2), _layer_pieces(3)]
    start_after = {1: (2, 3), 2: (4,), 4: (5,)}
    gathers = {}

    def gather_start(g, dep):
        srcs, shapes = [], []
        for kind, blk in groups[g]:
            r0, r = piece_rows(kind, blk)
            cdim = _PIECE_KINDS[kind][1]
            srcs.append(view2d(P[kind])[r0:r0 + r].astype(BF16).reshape(2, r // 2, cdim))
            shapes.append(jax.ShapeDtypeStruct((N_CHIPS, 2, r // 2, cdim), BF16))
        gathers[g] = _xchip_start("gather", srcs, shapes, dep, f"ag_start_g{g}")

    def gather_finish(g, after):
        ssem, rsem, srcs, lands, _ = gathers.pop(g)
        deps = [after]
        for nxt in start_after.get(g, ()):
            gather_start(nxt, deps[-1])
            deps.append(gathers[nxt][-1])
        srcs, lands = _xchip_wait("gather", ssem, rsem, srcs, lands, deps, f"ag_wait_g{g}")
        lands = _sibling_fwd(lands, f"ag_sibling_g{g}")
        W = {}
        for (kind, _), s, land in zip(groups[g], srcs, lands, strict=True):
            r, cdim, to_full, _ = _PIECE_KINDS[kind]
            W[kind] = to_full(lax.dynamic_update_index_in_dim(land, s, chip, 0).reshape(N_CHIPS, r, cdim))
        return W

    def weights_of(i, part, x_i):
        if i < 2:
            return gather_finish(2 * i + (part == "mlp"), x_i)
        return gather_finish(i + 2, x_i) if part == "mix" else {}

    scatters = {}
    bufs = {n: tuple(lax.empty(view2d(P[n]).shape, F32) for _ in range(4)) for n in _PIECE_KINDS}

    def scatter_start(i, gW, dep):
        pcs = _layer_pieces(i)
        blocked = []
        for kind, _ in pcs:
            r, cdim, _, to_blocks = _PIECE_KINDS[kind]
            g = gW[kind]
            blocked.append(g if g.ndim == 4 else to_blocks(g).reshape(N_CHIPS, 2, r // 2, cdim).transpose(1, 0, 2, 3))
        from_sib = _sibling_swap(blocked, f"rs_sibling_l{i}")
        pair, shapes = [], []
        for (kind, _), b, f in zip(pcs, blocked, from_sib, strict=True):
            _, _, hr, cdim = b.shape
            p = _sum_sel(cidx, b.reshape(2, N_CHIPS * hr, cdim), [f.reshape(1, N_CHIPS * hr, cdim)], f"rs_pair_l{i}_{kind}", BF16)
            pair.append(p.reshape(N_CHIPS, hr, cdim))
            shapes.append(jax.ShapeDtypeStruct((N_CHIPS - 1, hr, cdim), BF16))
        scatters[i] = (pcs, *_xchip_start("scatter", pair, shapes, dep, f"rs_start_l{i}"))
        return scatters[i][-1][0, 0]

    def scatter_finish(i, after):
        pcs, ssem, rsem, pair, lands, _ = scatters.pop(i)
        pair, lands = _xchip_wait("scatter", ssem, rsem, pair, lands, after, f"rs_wait_l{i}")
        halves = [_sum_sel(chipidx, p, [l], f"rs_sum_l{i}_{kind}", F32) for (kind, _), p, l in zip(pcs, pair, lands, strict=True)]
        got = _sibling_send(halves, f"rs_merge_l{i}")
        for (kind, blk), mine, other in zip(pcs, halves, got, strict=True):
            r0, _ = piece_rows(kind, blk)
            bufs[kind] = tuple(_adamw_piece(cidx, view2d(P[kind]), view2d(M[kind]), view2d(V[kind]), mine, other, bufs[kind], r0,
                                            f"adamw_l{i}_{kind}"))
        return lands[0]

    first_layer = {}

    def grads_of(i, gW, dx_i):
        dep = scatter_finish(i + 1, [dx_i]) if i + 1 in scatters else dx_i
        if i == 0:
            first_layer.update(gW)
            return jnp.zeros((), F32)
        return scatter_start(i, gW, dep)

    gather_start(0, mod)
    gather_start(1, gathers[0][-1])
    loss_l, dx, gS, dmod = _local_step(x[0], positions[0], loss_target[0], mod, S, weights_of, grads_of)
    loss = lax.psum(loss_l[0, 0], ("x", "y", "c"))

    gS["dmod"] = dmod
    small = _allgather8(_pack([gS[n] for n in _SMALL]), "ag_small_grads")
    small = small + scatter_start(0, first_layer, small)
    small_sum = _unpack(_sum_lead([small], "sum_small_grads"), list(_SMALL.values()))
    G = dict(zip(_SMALL, small_sum, strict=True))
    grads = {
        "ada_b": G["dmod"], "norm_mix_g": G["norm_mix_g"], "norm_mlp_g": G["norm_mlp_g"], "sgu_ln_g": G["sgu_ln_g"], "sgu_ln_b": G["sgu_ln_b"],
        "sgu_w_s": G["sgu_w_s"][None], "sgu_b_s": G["sgu_b_s"][None], "mla_kv_norm_g": G["mla_kv_norm_g"], "final_g": G["final_g"][0],
        "pool_scale": lax.dynamic_slice_in_dim(G["pool_scale"], chip * (D // N_CHIPS), D // N_CHIPS, axis=1),
        "mla_q_norm_g": lax.dynamic_slice_in_dim(G["mla_q_norm_g"], chip * (MLA_QL // N_CHIPS), MLA_QL // N_CHIPS, axis=1),
    }
    dmod_all = _unpack(small, [(N_DEV,) + (small.shape[1] * _PACK_W,)])[0]
    off = sum(math.prod(s) for n, s in _SMALL.items() if n != "dmod")
    dmod_all = dmod_all[:, off:off + DEPTH * 6 * D].reshape(N_DEV, DEPTH, 6 * D)
    dmod_loc = lax.dynamic_slice_in_dim(dmod_all, chip * n_ada, n_ada, axis=2).transpose(1, 0, 2)
    grads["ada_w"] = _ada_bwd(c_all.T, dmod_loc, "ada_bwd")

    deltas, new_m, new_v = {}, {}, {}
    for n in order:
        if n not in _PIECE_KINDS:
            deltas[n], new_m[n], new_v[n] = _adamw(P[n], grads[n].reshape(P[n].shape), M[n], V[n], f"adamw_{n}")
    scatter_finish(0, [deltas["ada_w"], deltas["sgu_w_s"]] + [bufs[n][0] for n in ("mlp_w1", "mlp_w2", "sgu_w_in", "mla_w_o")])
    for n in _PIECE_KINDS:
        grads[n], deltas[n], new_m[n], new_v[n] = (b.reshape(P[n].shape) for b in bufs[n])
    return (loss, dx[None], *[grads[n].reshape(P[n].shape) for n in order], *[deltas[n] for n in order], *[new_m[n] for n in order],
            *[new_v[n] for n in order])
```

```python
import math

import jax
import jax.numpy as jnp
from jax import lax
from jax.experimental import pallas as pl
from jax.experimental.pallas import tpu as pltpu

F32, BF16 = jnp.float32, jnp.bfloat16
MESH = pl.DeviceIdType.MESH

D_MODEL = 1024
DEPTH = 4
N_MIXERS = 3
POOL_WINDOWS = (2, 4, 8, 16)
POOL_GD = D_MODEL // len(POOL_WINDOWS)
POOL_HALO = 16
SGU_CHUNK = 128
SGU_W = D_MODEL
SGU_HD = 128
SGU_H = SGU_W // SGU_HD
MLA_H = 16
MLA_QL = 256
MLA_KVL = 128
MLA_NOPE = 128
MLA_ROPE = 64
MLA_V = 128
MLA_HP = 256
MLA_LATP = 512
ROPE_THETA = 10000.0
RMS_EPS = 1e-6
LN_EPS = 1e-5
SM_SCALE = (MLA_NOPE + MLA_ROPE) ** -0.5
NEG_INF = -1e30
ADAM_LR, ADAM_B1, ADAM_B2, ADAM_EPS, ADAM_WD, ADAM_STEP = 0.001, 0.9, 0.999, 1e-08, 0.01, 10
N_CHIPS = 4
N_DEV = 8
ROW_TILE = 512
ATT_TILE = 512
ATT_SUB = 256
ATT_FWD_HEADS = 4
ATT_BWD_HEADS = 2
MM_VMEM_BUDGET = 40 << 20


def _idx():
    return lax.axis_index("x"), lax.axis_index("y"), lax.axis_index("c")


def _mm(a, b, *, name, ta=False, tb=False, epi=None, extras=(), out_dtypes=(BF16,), tm=1024, tn=1024, tk=1024, chip_blocks=None):
    if ta:
        K, M = a.shape
    else:
        M, K = a.shape
    b_chips = b.ndim == 3
    if b_chips:
        assert b.shape[0] == N_CHIPS
        Kb, N = (N_CHIPS * b.shape[2], b.shape[1]) if tb else (b.shape[1], N_CHIPS * b.shape[2])
    elif tb:
        N, Kb = b.shape
    else:
        Kb, N = b.shape
    assert K == Kb, (a.shape, b.shape, ta, tb)
    if b_chips and not tb:
        tn = min(tn, N // N_CHIPS)
    if chip_blocks == "col":
        tm, tn = min(tm, M // 2), min(tn, N // N_CHIPS)
    elif chip_blocks == "row":
        tm = min(tm, M // N_CHIPS // 2)
    tm, tn, tk = min(tm, M), min(tn, N), min(tk, K)

    def vmem_bytes(tm_, tk_):
        per_mn = sum(arr.dtype.itemsize for arr, kind in extras if kind == "mn") + sum(jnp.dtype(dt).itemsize for dt in out_dtypes)
        return 2 * (tm_ * tk_ * a.dtype.itemsize + tk_ * tn * b.dtype.itemsize + tm_ * tn * per_mn)

    if vmem_bytes(tm, K) <= MM_VMEM_BUDGET:
        tk = K
    elif tm >= 512 and vmem_bytes(tm // 2, K) <= MM_VMEM_BUDGET:
        tm, tk = tm // 2, K
    assert M % tm == 0 and N % tn == 0 and K % tk == 0, (M, N, K, tm, tn, tk)
    nk = K // tk
    a_spec = pl.BlockSpec((tk, tm), lambda i, j, k: (k, i)) if ta else pl.BlockSpec((tm, tk), lambda i, j, k: (i, k))
    b_spec = pl.BlockSpec((tn, tk), lambda i, j, k: (j, k)) if tb else pl.BlockSpec((tk, tn), lambda i, j, k: (k, j))
    if b_chips and tb:
        assert nk == 1 and not ta
        b_spec = pl.BlockSpec((N_CHIPS, tn, K // N_CHIPS), lambda i, j, k: (0, j, 0))
    elif b_chips:
        per = N // N_CHIPS // tn
        b_spec = pl.BlockSpec((None, tk, tn), lambda i, j, k: (j // per, k, j % per))
    ex_specs = []
    for arr, kind in extras:
        if kind == "mn":
            ex_specs.append(pl.BlockSpec((tm, tn), lambda i, j, k: (i, j)))
        elif kind == "n":
            ex_specs.append(pl.BlockSpec((1, tn), lambda i, j, k: (0, j)))
        else:
            ex_specs.append(pl.BlockSpec((tm, arr.shape[1]), lambda i, j, k: (i, 0)))
    n_ex, n_out = len(extras), len(out_dtypes)
    dims = (((0 if ta else 1,), (1 if tb else 0,)), ((), ()))

    def body(*refs):
        a_ref, b_ref = refs[0], refs[1]
        ex_refs = refs[2:2 + n_ex]
        out_refs = refs[2 + n_ex:2 + n_ex + n_out]
        if b_chips and tb:
            kc = K // N_CHIPS
            part = None
            for cb in range(N_CHIPS):
                p = lax.dot_general(a_ref[:, cb * kc:(cb + 1) * kc].astype(BF16), b_ref[cb].astype(BF16), dims, preferred_element_type=F32)
                part = p if part is None else part + p
        else:
            part = lax.dot_general(a_ref[...].astype(BF16), b_ref[...].astype(BF16), dims, preferred_element_type=F32)

        def finish(acc):
            outs = epi(acc, *[r[...] for r in ex_refs]) if epi is not None else (acc,)
            for r, o in zip(out_refs, outs, strict=True):
                r[...] = o.astype(r.dtype)

        if nk == 1:
            finish(part)
        else:
            acc_ref = refs[-1]
            k = pl.program_id(2)

            @pl.when(k == 0)
            def _():
                acc_ref[...] = part

            @pl.when(k > 0)
            def _():
                acc_ref[...] += part

            @pl.when(k == nk - 1)
            def _():
                finish(acc_ref[...])

    out_specs = [pl.BlockSpec((tm, tn), lambda i, j, k: (i, j)) for _ in range(n_out)]
    out_shape = [jax.ShapeDtypeStruct((M, N), dt) for dt in out_dtypes]
    if chip_blocks is not None:
        assert n_out == 1
        if chip_blocks == "col":
            rh, cb = M // 2 // tm, N // N_CHIPS // tn
            out_specs = [pl.BlockSpec((None, None, tm, tn), lambda i, j, k: (i // rh, j // cb, i % rh, j % cb))]
            out_shape = [jax.ShapeDtypeStruct((2, N_CHIPS, M // 2, N // N_CHIPS), out_dtypes[0])]
        else:
            rh = M // N_CHIPS // 2 // tm
            out_specs = [pl.BlockSpec((None, None, tm, tn), lambda i, j, k: ((i // rh) % 2, i // (2 * rh), i % rh, j))]
            out_shape = [jax.ShapeDtypeStruct((2, N_CHIPS, M // N_CHIPS // 2, N), out_dtypes[0])]
    outs = pl.pallas_call(
        body,
        name=name,
        grid=(M // tm, N // tn, nk),
        in_specs=[a_spec, b_spec, *ex_specs],
        out_specs=out_specs,
        out_shape=out_shape,
        scratch_shapes=[pltpu.VMEM((tm, tn), F32)] if nk > 1 else [],
        compiler_params=pltpu.CompilerParams(dimension_semantics=("parallel", "parallel", "arbitrary")),
    )(a, b, *[arr for arr, _ in extras])
    return outs[0] if n_out == 1 else tuple(outs)


def _epi_sq_relu(acc):
    r = jnp.maximum(acc, 0.0)
    return r * r, 2.0 * r


def _epi_residual(acc, x, g):
    return x + g * acc, acc


def _rms_mod(xv, gain, sc, sh):
    r = lax.rsqrt(jnp.mean(xv * xv, axis=-1, keepdims=True) + RMS_EPS)
    return ((xv * r) * gain) * (1.0 + sc) + sh


def _epi_residual_norm(acc, x, g, gain, sc, sh):
    xn = x + g * acc
    return xn, acc, _rms_mod(xn, gain, sc, sh)


def _row_spec(tr, d):
    return pl.BlockSpec((tr, d), lambda i: (i, 0))


def _vec_spec(d):
    return pl.BlockSpec((1, d), lambda i: (0, 0))


def _colsum(v):
    return jnp.sum(v, axis=0, keepdims=True)


def _norm_mod_fwd(x, gain, sc, sh, out_dtype, name):
    T, D = x.shape
    tr = min(T, ROW_TILE)

    def body(x_ref, g_ref, sc_ref, sh_ref, o_ref):
        o_ref[...] = _rms_mod(x_ref[...], g_ref[...], sc_ref[...], sh_ref[...]).astype(o_ref.dtype)

    return pl.pallas_call(
        body, name=name, grid=(T // tr,),
        in_specs=[_row_spec(tr, D), _vec_spec(D), _vec_spec(D), _vec_spec(D)],
        out_specs=_row_spec(tr, D),
        out_shape=jax.ShapeDtypeStruct((T, D), out_dtype),
        compiler_params=pltpu.CompilerParams(dimension_semantics=("parallel",)),
    )(x, gain, sc, sh)


def _norm_mod_bwd(x, dh, dres, gain, sc, name):
    T, D = x.shape
    tr = min(T, ROW_TILE)

    def body(x_ref, dh_ref, dres_ref, g_ref, sc_ref, dx_ref, dg_ref, dsc_ref, dsh_ref):
        @pl.when(pl.program_id(0) == 0)
        def _():
            dg_ref[...] = jnp.zeros_like(dg_ref)
            dsc_ref[...] = jnp.zeros_like(dsc_ref)
            dsh_ref[...] = jnp.zeros_like(dsh_ref)

        xv = x_ref[...]
        r = lax.rsqrt(jnp.mean(xv * xv, axis=-1, keepdims=True) + RMS_EPS)
        xn = xv * r
        dhv = dh_ref[...].astype(F32)
        dsh_ref[...] += _colsum(dhv)
        dsc_ref[...] += _colsum(dhv * (xn * g_ref[...]))
        dt = dhv * (1.0 + sc_ref[...])
        dg_ref[...] += _colsum(dt * xn)
        dxn = dt * g_ref[...]
        dx_ref[...] = dres_ref[...] + r * (dxn - xn * jnp.mean(dxn * xn, axis=-1, keepdims=True))

    return pl.pallas_call(
        body, name=name, grid=(T // tr,),
        in_specs=[_row_spec(tr, D), _row_spec(tr, D), _row_spec(tr, D), _vec_spec(D), _vec_spec(D)],
        out_specs=[_row_spec(tr, D), _vec_spec(D), _vec_spec(D), _vec_spec(D)],
        out_shape=[jax.ShapeDtypeStruct((T, D), F32)] + [jax.ShapeDtypeStruct((1, D), F32)] * 3,
        compiler_params=pltpu.CompilerParams(dimension_semantics=("arbitrary",)),
    )(x, dh, dres, gain, sc)


def _resid_bwd(dx, y, g, name):
    T, D = dx.shape
    tr = min(T, ROW_TILE)

    def body(dx_ref, y_ref, g_ref, dy_ref, q_ref):
        @pl.when(pl.program_id(0) == 0)
        def _():
            q_ref[...] = jnp.zeros_like(q_ref)

        dxv = dx_ref[...]
        dy_ref[...] = (g_ref[...] * dxv).astype(BF16)
        q_ref[...] += _colsum(dxv * y_ref[...].astype(F32))

    return pl.pallas_call(
        body, name=name, grid=(T // tr,),
        in_specs=[_row_spec(tr, D), _row_spec(tr, D), _vec_spec(D)],
        out_specs=[_row_spec(tr, D), _vec_spec(D)],
        out_shape=[jax.ShapeDtypeStruct((T, D), BF16), jax.ShapeDtypeStruct((1, D), F32)],
        compiler_params=pltpu.CompilerParams(dimension_semantics=("arbitrary",)),
    )(dx, y, g)


def _loss_head(x, target, gain, name):
    T, D = x.shape
    tr = min(T, ROW_TILE)

    def body(x_ref, t_ref, g_ref, loss_ref, dx_ref, dg_ref):
        @pl.when(pl.program_id(0) == 0)
        def _():
            loss_ref[...] = jnp.zeros_like(loss_ref)
            dg_ref[...] = jnp.zeros_like(dg_ref)

        xv = x_ref[...]
        r = lax.rsqrt(jnp.mean(xv * xv, axis=-1, keepdims=True) + RMS_EPS)
        xn = xv * r
        err = xn * g_ref[...] - t_ref[...]
        row = jnp.mean(err * err, axis=-1, keepdims=True)
        loss_ref[...] += 0.5 * jnp.sum(row, axis=0, keepdims=True)
        dy = err * (1.0 / D)
        dg_ref[...] += _colsum(dy * xn)
        dxn = dy * g_ref[...]
        dx_ref[...] = r * (dxn - xn * jnp.mean(dxn * xn, axis=-1, keepdims=True))

    return pl.pallas_call(
        body, name=name, grid=(T // tr,),
        in_specs=[_row_spec(tr, D), _row_spec(tr, D), _vec_spec(D)],
        out_specs=[_vec_spec(128), _row_spec(tr, D), _vec_spec(D)],
        out_shape=[jax.ShapeDtypeStruct((1, 128), F32), jax.ShapeDtypeStruct((T, D), F32), jax.ShapeDtypeStruct((1, D), F32)],
        compiler_params=pltpu.CompilerParams(dimension_semantics=("arbitrary",)),
    )(x, target, gain)


def _pool_fwd(h, w, scale, x, g1, gmlp, sc2, sh2, name):
    T, D = h.shape
    tr = min(T, ROW_TILE)

    def body(h_ref, w_ref, sc_ref, x_ref, g_ref, gm_ref, sc2_ref, sh2_ref, x2_ref, pooled_ref, ypre_ref, h2_ref, halo_ref):
        i = pl.program_id(0)

        @pl.when(i == 0)
        def _():
            halo_ref[...] = jnp.zeros_like(halo_ref)

        hv = h_ref[...]
        buf = jnp.concatenate([halo_ref[...], hv], axis=0)
        halo_ref[...] = hv[tr - POOL_HALO:, :]
        t = (i * tr + lax.broadcasted_iota(jnp.int32, (tr, 1), 0)).astype(F32)
        for gi, win in enumerate(POOL_WINDOWS):
            cols = slice(gi * POOL_GD, (gi + 1) * POOL_GD)
            val = buf[:, cols]
            sh = 1
            while sh < win:
                val = val + pltpu.roll(val, sh, axis=0)
                sh *= 2
            pooled = val[POOL_HALO:, :] / jnp.minimum(t + 1.0, float(win)) - hv[:, cols]
            pb = pooled.astype(BF16)
            pooled_ref[:, cols] = pb
            yp = jnp.dot(pb, w_ref[gi], preferred_element_type=F32)
            ypre_ref[:, cols] = yp.astype(BF16)
            x2_ref[:, cols] = x_ref[:, cols] + g_ref[:, cols] * (yp * sc_ref[:, cols])
        h2_ref[...] = _rms_mod(x2_ref[...], gm_ref[...], sc2_ref[...], sh2_ref[...]).astype(BF16)

    return pl.pallas_call(
        body, name=name, grid=(T // tr,),
        in_specs=[_row_spec(tr, D), pl.BlockSpec(w.shape, lambda i: (0, 0, 0)), _vec_spec(D), _row_spec(tr, D), _vec_spec(D), _vec_spec(D),
                  _vec_spec(D), _vec_spec(D)],
        out_specs=[_row_spec(tr, D)] * 4,
        out_shape=[jax.ShapeDtypeStruct((T, D), F32), jax.ShapeDtypeStruct((T, D), BF16), jax.ShapeDtypeStruct((T, D), BF16),
                   jax.ShapeDtypeStruct((T, D), BF16)],
        scratch_shapes=[pltpu.VMEM((POOL_HALO, D), F32)],
        compiler_params=pltpu.CompilerParams(dimension_semantics=("arbitrary",)),
    )(h, w, scale, x, g1, gmlp, sc2, sh2)


def _pool_bwd(dy, pooled, w, scale, g1, q, name):
    T, D = dy.shape
    tr = min(T, ROW_TILE)
    nt = T // tr
    ltot = tr + POOL_HALO

    def body(dy_ref, pooled_ref, w_ref, sc_ref, g_ref, q_ref, dh_ref, dw_ref, dsc_ref, dg_ref, halo_ref):
        i = pl.program_id(0)

        @pl.when(i == 0)
        def _():
            halo_ref[...] = jnp.zeros_like(halo_ref)
            dw_ref[...] = jnp.zeros_like(dw_ref)
            dsc_ref[...] = g_ref[...] * q_ref[...]
            dg_ref[...] = sc_ref[...] * q_ref[...]

        t = ((nt - 1 - i) * tr + lax.broadcasted_iota(jnp.int32, (tr, 1), 0)).astype(F32)
        for gi, win in enumerate(POOL_WINDOWS):
            cols = slice(gi * POOL_GD, (gi + 1) * POOL_GD)
            dyb = (dy_ref[:, cols].astype(F32) * sc_ref[:, cols]).astype(BF16)
            dw_ref[gi] += lax.dot_general(pooled_ref[:, cols], dyb, (((0,), (0,)), ((), ())), preferred_element_type=F32)
            dpool = lax.dot_general(dyb, w_ref[gi], (((1,), (1,)), ((), ())), preferred_element_type=F32)
            qv = dpool / jnp.minimum(t + 1.0, float(win))
            val = jnp.concatenate([qv, halo_ref[:, cols]], axis=0)
            halo_ref[:, cols] = qv[:POOL_HALO, :]
            sh = 1
            while sh < win:
                val = val + pltpu.roll(val, ltot - sh, axis=0)
                sh *= 2
            dh_ref[:, cols] = val[:tr, :] - dpool

    rev = pl.BlockSpec((tr, D), lambda i: (nt - 1 - i, 0))
    return pl.pallas_call(
        body, name=name, grid=(nt,),
        in_specs=[rev, rev, pl.BlockSpec(w.shape, lambda i: (0, 0, 0)), _vec_spec(D), _vec_spec(D), _vec_spec(D)],
        out_specs=[rev, pl.BlockSpec(w.shape, lambda i: (0, 0, 0)), _vec_spec(D), _vec_spec(D)],
        out_shape=[jax.ShapeDtypeStruct((T, D), F32), jax.ShapeDtypeStruct(w.shape, F32),
                   jax.ShapeDtypeStruct((1, D), F32), jax.ShapeDtypeStruct((1, D), F32)],
        scratch_shapes=[pltpu.VMEM((POOL_HALO, D), F32)],
        compiler_params=pltpu.CompilerParams(dimension_semantics=("arbitrary",)),
    )(dy, pooled, w, scale, g1, q)


_INV_SQRT2 = 0.7071067811865476
_INV_SQRT2PI = 0.3989422804014327


def _gelu(v):
    return 0.5 * v * (1.0 + lax.erf(v * _INV_SQRT2))


def _gelu_grad(v):
    return 0.5 * (1.0 + lax.erf(v * _INV_SQRT2)) + v * jnp.exp(-0.5 * v * v) * _INV_SQRT2PI


def _sgu_ln(v, g, b):
    mu = jnp.mean(v, axis=-1, keepdims=True)
    xc = v - mu
    rstd = lax.rsqrt(jnp.mean(xc * xc, axis=-1, keepdims=True) + LN_EPS)
    xh = xc * rstd
    return xh, rstd, xh * g + b


def _tril_mask():
    return lax.broadcasted_iota(jnp.int32, (SGU_CHUNK, SGU_CHUNK), 0) >= lax.broadcasted_iota(jnp.int32, (SGU_CHUNK, SGU_CHUNK), 1)


SGU_TILE = 256


def _sgu_gate_fwd(zz, ln_g, ln_b, ws, bs_t, name):
    T = zz.shape[0]
    ts = min(T, SGU_TILE)

    def body(zz_ref, g_ref, b_ref, ws_ref, bs_ref, out_ref):
        z = _gelu(zz_ref[...])
        u = z[:, :SGU_W]
        _, _, vn = _sgu_ln(z[:, SGU_W:], g_ref[...], b_ref[...])
        vb = vn.astype(BF16)
        tril = _tril_mask()
        for hh in range(SGU_H):
            wm = jnp.where(tril, ws_ref[hh], 0.0).astype(BF16)
            bcol = bs_ref[:, hh:hh + 1]
            cs = slice(hh * SGU_HD, (hh + 1) * SGU_HD)
            for j in range(ts // SGU_CHUNK):
                rs = slice(j * SGU_CHUNK, (j + 1) * SGU_CHUNK)
                mixed = jnp.dot(wm, vb[rs, cs], preferred_element_type=F32) + bcol
                out_ref[rs, cs] = (u[rs, cs] * mixed).astype(BF16)

    return pl.pallas_call(
        body, name=name, grid=(T // ts,),
        in_specs=[_row_spec(ts, 2 * SGU_W), _vec_spec(SGU_W), _vec_spec(SGU_W),
                  pl.BlockSpec(ws.shape, lambda i: (0, 0, 0)), pl.BlockSpec(bs_t.shape, lambda i: (0, 0))],
        out_specs=_row_spec(ts, SGU_W),
        out_shape=jax.ShapeDtypeStruct((T, SGU_W), BF16),
        compiler_params=pltpu.CompilerParams(dimension_semantics=("parallel",)),
    )(zz, ln_g, ln_b, ws, bs_t)


def _sgu_gate_bwd(zz, dgated, ln_g, ln_b, ws, bs_t, name):
    T = zz.shape[0]
    ts = min(T, SGU_TILE)
    nt = T // ts

    def body(zz_ref, dg_ref, g_ref, b_ref, ws_ref, bs_ref, dzz_ref, dws_ref, dbs_ref, dlg_ref, dlb_ref, dlo_ref, dmx_ref):
        i = pl.program_id(0)

        @pl.when(i == 0)
        def _():
            dws_ref[...] = jnp.zeros_like(dws_ref)
            dmx_ref[...] = jnp.zeros_like(dmx_ref)
            dlg_ref[...] = jnp.zeros_like(dlg_ref)
            dlb_ref[...] = jnp.zeros_like(dlb_ref)

        zzv = zz_ref[...]
        z = _gelu(zzv)
        u = z[:, :SGU_W]
        xh, rstd, vn = _sgu_ln(z[:, SGU_W:], g_ref[...], b_ref[...])
        vb = vn.astype(BF16)
        dgv = dg_ref[...].astype(F32)
        tril = _tril_mask()
        for hh in range(SGU_H):
            wm = jnp.where(tril, ws_ref[hh], 0.0).astype(BF16)
            bcol = bs_ref[:, hh:hh + 1]
            cs = slice(hh * SGU_HD, (hh + 1) * SGU_HD)
            for j in range(ts // SGU_CHUNK):
                rs = slice(j * SGU_CHUNK, (j + 1) * SGU_CHUNK)
                mixed = jnp.dot(wm, vb[rs, cs], preferred_element_type=F32) + bcol
                dmixed = dgv[rs, cs] * u[rs, cs]
                dzz_ref[rs, cs] = (dgv[rs, cs] * mixed * _gelu_grad(zzv[rs, cs])).astype(BF16)
                dmb = dmixed.astype(BF16)
                dws_ref[hh] += lax.dot_general(dmb, vb[rs, cs], (((1,), (1,)), ((), ())), preferred_element_type=F32)
                dmx_ref[hh] += dmixed
                dlo_ref[rs, cs] = lax.dot_general(wm, dmb, (((0,), (0,)), ((), ())), preferred_element_type=F32)
        dlo = dlo_ref[...]
        dlg_ref[...] += _colsum(dlo * xh)
        dlb_ref[...] += _colsum(dlo)
        dxh = dlo * g_ref[...]
        dv = rstd * (dxh - jnp.mean(dxh, axis=-1, keepdims=True) - xh * jnp.mean(dxh * xh, axis=-1, keepdims=True))
        dzz_ref[:, SGU_W:] = (dv * _gelu_grad(zzv[:, SGU_W:])).astype(BF16)

        @pl.when(i == nt - 1)
        def _():
            tril_f = tril.astype(F32)
            for hh in range(SGU_H):
                dws_ref[hh] = dws_ref[hh] * tril_f
                dbs_ref[hh] = jnp.broadcast_to(jnp.sum(dmx_ref[hh], axis=-1, keepdims=True), (SGU_CHUNK, SGU_HD))

    full3 = pl.BlockSpec(ws.shape, lambda i: (0, 0, 0))
    return pl.pallas_call(
        body, name=name, grid=(nt,),
        in_specs=[_row_spec(ts, 2 * SGU_W), _row_spec(ts, SGU_W), _vec_spec(SGU_W), _vec_spec(SGU_W), full3,
                  pl.BlockSpec(bs_t.shape, lambda i: (0, 0))],
        out_specs=[_row_spec(ts, 2 * SGU_W), full3, full3, _vec_spec(SGU_W), _vec_spec(SGU_W)],
        out_shape=[jax.ShapeDtypeStruct((T, 2 * SGU_W), BF16), jax.ShapeDtypeStruct(ws.shape, F32), jax.ShapeDtypeStruct(ws.shape, F32),
                   jax.ShapeDtypeStruct((1, SGU_W), F32), jax.ShapeDtypeStruct((1, SGU_W), F32)],
        scratch_shapes=[pltpu.VMEM((ts, SGU_W), F32), pltpu.VMEM(ws.shape, F32)],
        compiler_params=pltpu.CompilerParams(dimension_semantics=("arbitrary",)),
    )(zz, dgated, ln_g, ln_b, ws, bs_t)


def _rope_fwd(blk, cc, sa, sb):
    return blk * cc + pltpu.roll(blk, 96, axis=1) * sa + pltpu.roll(blk, 32, axis=1) * sb


def _rope_bwd(d, cc, sa, sb):
    return d * cc + pltpu.roll(d * sa, 32, axis=1) + pltpu.roll(d * sb, 96, axis=1)


def _rms(v, g):
    r = lax.rsqrt(jnp.mean(v * v, axis=-1, keepdims=True) + RMS_EPS)
    vn = v * r
    return vn, r, vn * g


def _rms_bwd(dy, vn, r, g):
    dvn = dy * g
    return r * (dvn - vn * jnp.mean(dvn * vn, axis=-1, keepdims=True))


MLA_TILE = 256
_KV0 = MLA_QL
_KR0 = MLA_QL + MLA_KVL


def _mla_lat_fwd(lat, qg, kvg, cc, sa, sb, name):
    T = lat.shape[0]
    tr = min(T, ROW_TILE)

    def body(lat_ref, qg_ref, kvg_ref, cc_ref, sa_ref, sb_ref, cq_ref, ckv_ref, kr_ref):
        lv = lat_ref[...]
        cq_ref[...] = _rms(lv[:, :_KV0], qg_ref[...])[2].astype(BF16)
        ckv_ref[...] = _rms(lv[:, _KV0:_KR0], kvg_ref[...])[2].astype(BF16)
        kr_ref[...] = _rope_fwd(lv[:, _KR0:], cc_ref[...], sa_ref[...], sb_ref[...])

    return pl.pallas_call(
        body, name=name, grid=(T // tr,),
        in_specs=[_row_spec(tr, MLA_LATP), _vec_spec(MLA_QL), _vec_spec(MLA_KVL), _row_spec(tr, 128), _row_spec(tr, 128), _row_spec(tr, 128)],
        out_specs=[_row_spec(tr, MLA_QL), _row_spec(tr, MLA_KVL), _row_spec(tr, 128)],
        out_shape=[jax.ShapeDtypeStruct((T, MLA_QL), BF16), jax.ShapeDtypeStruct((T, MLA_KVL), BF16), jax.ShapeDtypeStruct((T, 128), F32)],
        compiler_params=pltpu.CompilerParams(dimension_semantics=("parallel",)),
    )(lat, qg, kvg, cc, sa, sb)


def _mla_lat_bwd(lat, dcqn, dckvn, dkrot, qg, kvg, cc, sa, sb, name):
    T = lat.shape[0]
    tr = min(T, ROW_TILE)

    def body(lat_ref, dcq_ref, dckv_ref, dkr_ref, qg_ref, kvg_ref, cc_ref, sa_ref, sb_ref, dlat_ref, dqg_ref, dkvg_ref):
        @pl.when(pl.program_id(0) == 0)
        def _():
            dqg_ref[...] = jnp.zeros_like(dqg_ref)
            dkvg_ref[...] = jnp.zeros_like(dkvg_ref)

        lv = lat_ref[...]
        qn, qr, _ = _rms(lv[:, :_KV0], qg_ref[...])
        kn, kr, _ = _rms(lv[:, _KV0:_KR0], kvg_ref[...])
        dcq = dcq_ref[...]
        dckv = dckv_ref[...]
        dqg_ref[...] += _colsum(dcq * qn)
        dkvg_ref[...] += _colsum(dckv * kn)
        dlat_ref[:, :_KV0] = _rms_bwd(dcq, qn, qr, qg_ref[...]).astype(BF16)
        dlat_ref[:, _KV0:_KR0] = _rms_bwd(dckv, kn, kr, kvg_ref[...]).astype(BF16)
        dlat_ref[:, _KR0:] = _rope_bwd(dkr_ref[...], cc_ref[...], sa_ref[...], sb_ref[...]).astype(BF16)

    return pl.pallas_call(
        body, name=name, grid=(T // tr,),
        in_specs=[_row_spec(tr, MLA_LATP), _row_spec(tr, MLA_QL), _row_spec(tr, MLA_KVL), _row_spec(tr, 128),
                  _vec_spec(MLA_QL), _vec_spec(MLA_KVL), _row_spec(tr, 128), _row_spec(tr, 128), _row_spec(tr, 128)],
        out_specs=[_row_spec(tr, MLA_LATP), _vec_spec(MLA_QL), _vec_spec(MLA_KVL)],
        out_shape=[jax.ShapeDtypeStruct((T, MLA_LATP), BF16), jax.ShapeDtypeStruct((1, MLA_QL), F32), jax.ShapeDtypeStruct((1, MLA_KVL), F32)],
        compiler_params=pltpu.CompilerParams(dimension_semantics=("arbitrary",)),
    )(lat, dcqn, dckvn, dkrot, qg, kvg, cc, sa, sb)


def _mla_prep(qpad, kv, krot, cc, sa, sb, name):
    T = qpad.shape[0]
    tr = min(T, MLA_TILE)
    HW = MLA_H * MLA_HP

    def body(q_ref, kv_ref, kr_ref, cc_ref, sa_ref, sb_ref, qo_ref, ko_ref, kt_ref, vo_ref, vt_ref):
        cc, sa, sb = cc_ref[...], sa_ref[...], sb_ref[...]
        kr = kr_ref[...]
        krb, krt = kr.astype(BF16), kr.T.astype(BF16)
        for hh in range(MLA_H):
            a, m, b = hh * MLA_HP, hh * MLA_HP + MLA_NOPE, (hh + 1) * MLA_HP
            qo_ref[:, a:m] = (q_ref[:, a:m] * SM_SCALE).astype(BF16)
            qo_ref[:, m:b] = (_rope_fwd(q_ref[:, m:b], cc, sa, sb) * SM_SCALE).astype(BF16)
            kn = kv_ref[:, a:m]
            ko_ref[:, a:m] = kn.astype(BF16)
            ko_ref[:, m:b] = krb
            kt_ref[a:m, :] = kn.T.astype(BF16)
            kt_ref[m:b, :] = krt
            vh = kv_ref[:, m:b]
            vo_ref[:, hh * MLA_V:(hh + 1) * MLA_V] = vh.astype(BF16)
            vt_ref[hh] = vh.T.astype(BF16)

    tk = min(T, ATT_TILE)
    per = tk // tr
    return pl.pallas_call(
        body, name=name, grid=(T // tr,),
        in_specs=[_row_spec(tr, HW), _row_spec(tr, HW), _row_spec(tr, 128), _row_spec(tr, 128), _row_spec(tr, 128), _row_spec(tr, 128)],
        out_specs=[_row_spec(tr, HW), _row_spec(tr, HW), pl.BlockSpec((HW, tr), lambda i: (0, i)), _row_spec(tr, MLA_H * MLA_V),
                   pl.BlockSpec((MLA_H, None, MLA_V, tr), lambda i: (0, i // per, 0, i % per))],
        out_shape=[jax.ShapeDtypeStruct((T, HW), BF16), jax.ShapeDtypeStruct((T, HW), BF16), jax.ShapeDtypeStruct((HW, T), BF16),
                   jax.ShapeDtypeStruct((T, MLA_H * MLA_V), BF16), jax.ShapeDtypeStruct((MLA_H, T // tk, MLA_V, tk), BF16)],
        compiler_params=pltpu.CompilerParams(dimension_semantics=("parallel",)),
    )(qpad, kv, krot, cc, sa, sb)


ATT_HG = 4


def _mla_prep_bwd(dqt, dk, dv, cc, sa, sb, name):
    _, nq, _, tq = dqt.shape
    T = nq * tq
    gw = ATT_HG * MLA_HP

    def body(dq_ref, dk_ref, dv_ref, cc_ref, sa_ref, sb_ref, dqp_ref, dkv_ref, dkr_ref):
        @pl.when(pl.program_id(1) == 0)
        def _():
            dkr_ref[...] = jnp.zeros_like(dkr_ref)

        cc, sa, sb = cc_ref[...], sa_ref[...], sb_ref[...]
        acc = jnp.zeros((tq, 128), F32)
        for hh in range(ATT_HG):
            a, m, b = hh * MLA_HP, hh * MLA_HP + MLA_NOPE, (hh + 1) * MLA_HP
            dqh = dq_ref[hh].astype(F32).T * SM_SCALE
            dqp_ref[:, a:m] = dqh[:, :MLA_NOPE].astype(BF16)
            dqp_ref[:, m:b] = _rope_bwd(dqh[:, MLA_NOPE:], cc, sa, sb).astype(BF16)
            dkv_ref[:, a:m] = dk_ref[:, a:m]
            dkv_ref[:, m:b] = dv_ref[:, hh * MLA_V:(hh + 1) * MLA_V]
            acc = acc + dk_ref[:, m:b].astype(F32)
        dkr_ref[...] += acc

    tab = pl.BlockSpec((tq, 128), lambda i, g: (i, 0))
    return pl.pallas_call(
        body, name=name, grid=(nq, MLA_H // ATT_HG),
        in_specs=[pl.BlockSpec((ATT_HG, None, MLA_HP, tq), lambda i, g: (g, i, 0, 0)), pl.BlockSpec((tq, gw), lambda i, g: (i, g)),
                  pl.BlockSpec((tq, ATT_HG * MLA_V), lambda i, g: (i, g)), tab, tab, tab],
        out_specs=[pl.BlockSpec((tq, gw), lambda i, g: (i, g)), pl.BlockSpec((tq, gw), lambda i, g: (i, g)), tab],
        out_shape=[jax.ShapeDtypeStruct((T, MLA_H * MLA_HP), BF16), jax.ShapeDtypeStruct((T, MLA_H * MLA_HP), BF16), jax.ShapeDtypeStruct((T, 128), F32)],
        compiler_params=pltpu.CompilerParams(dimension_semantics=("parallel", "arbitrary")),
    )(dqt, dk, dv, cc, sa, sb)


_NT = (((1,), (1,)), ((), ()))


def _as_row(col, n):
    return jnp.broadcast_to(col, (n, 128)).T[0:1, :]


def _attn_fwd(q, k, vt, name):
    T = q.shape[0]
    tq = tk = min(T, ATT_TILE)
    nq = T // tq
    hg = ATT_FWD_HEADS

    def body(q_ref, k_ref, vt_ref, o_ref, lse_ref, m_ref, l_ref, acc_ref):
        i = pl.program_id(1)
        m_ref[...] = jnp.full_like(m_ref, NEG_INF)
        l_ref[...] = jnp.zeros_like(l_ref)
        acc_ref[...] = jnp.zeros_like(acc_ref)

        def step(j, diag):
            off = pl.multiple_of(j * tk, tk)
            sts = [lax.dot_general(k_ref[pl.ds(off, tk), hh * MLA_HP:(hh + 1) * MLA_HP], q_ref[:, hh * MLA_HP:(hh + 1) * MLA_HP], _NT,
                                   preferred_element_type=F32) for hh in range(hg)]
            for hh in range(hg):
                st = sts[hh]
                if diag:
                    st = jnp.where(lax.broadcasted_iota(jnp.int32, (tk, tq), 0) <= lax.broadcasted_iota(jnp.int32, (tk, tq), 1), st, NEG_INF)
                m_prev = m_ref[hh]
                m_new = jnp.maximum(m_prev, jnp.max(st, axis=0, keepdims=True))
                alpha = jnp.exp(m_prev - m_new)
                pt = jnp.exp(st - m_new)
                l_ref[hh] = alpha * l_ref[hh] + jnp.sum(pt, axis=0, keepdims=True)
                acc_ref[hh] = alpha * acc_ref[hh] + jnp.dot(vt_ref[hh, j], pt.astype(BF16), preferred_element_type=F32)
                m_ref[hh] = m_new

        def loop_body(j, carry):
            step(j, False)
            return carry

        lax.fori_loop(0, i, loop_body, 0)
        step(i, True)
        for hh in range(hg):
            o_ref[:, hh * MLA_V:(hh + 1) * MLA_V] = (acc_ref[hh] / l_ref[hh]).T.astype(BF16)
            lse_ref[hh] = m_ref[hh] + jnp.log(l_ref[hh])

    return pl.pallas_call(
        body, name=name, grid=(MLA_H // hg, nq),
        in_specs=[pl.BlockSpec((tq, hg * MLA_HP), lambda h, i: (i, h)), pl.BlockSpec((T, hg * MLA_HP), lambda h, i: (0, h)),
                  pl.BlockSpec((hg, nq, MLA_V, tk), lambda h, i: (h, 0, 0, 0))],
        out_specs=[pl.BlockSpec((tq, hg * MLA_V), lambda h, i: (i, h)), pl.BlockSpec((hg, None, 1, tq), lambda h, i: (h, i, 0, 0))],
        out_shape=[jax.ShapeDtypeStruct((T, MLA_H * MLA_V), BF16), jax.ShapeDtypeStruct((MLA_H, nq, 1, tq), F32)],
        scratch_shapes=[pltpu.VMEM((hg, 1, tq), F32), pltpu.VMEM((hg, 1, tq), F32), pltpu.VMEM((hg, MLA_V, tq), F32)],
        compiler_params=pltpu.CompilerParams(dimension_semantics=("parallel", "arbitrary")),
    )(q, k, vt)


def _attn_delta(do, o, name):
    T = do.shape[0]
    tq = min(T, ATT_TILE)

    def body(do_ref, o_ref, d_ref):
        for hh in range(MLA_H):
            cs = slice(hh * MLA_V, (hh + 1) * MLA_V)
            s = jnp.sum(do_ref[:, cs].astype(F32) * o_ref[:, cs].astype(F32), axis=-1, keepdims=True)
            d_ref[hh] = _as_row(s, tq)

    return pl.pallas_call(
        body, name=name, grid=(T // tq,),
        in_specs=[_row_spec(tq, MLA_H * MLA_V), _row_spec(tq, MLA_H * MLA_V)],
        out_specs=pl.BlockSpec((MLA_H, None, 1, tq), lambda i: (0, i, 0, 0)),
        out_shape=jax.ShapeDtypeStruct((MLA_H, T // tq, 1, tq), F32),
        compiler_params=pltpu.CompilerParams(dimension_semantics=("parallel",)),
    )(do, o)


def _attn_bwd(q, k, kt, v, do, lse, delta, name):
    T = q.shape[0]
    tq = tk = min(T, ATT_TILE)
    nq = nk = T // tq
    tsd = min(tq, ATT_SUB)
    hg = ATT_BWD_HEADS

    def body(q_ref, k_ref, kt_ref, v_ref, do_ref, lse_ref, dl_ref, dqt_ref, dk_ref, dv_ref, dq_acc, dk_acc, dv_acc):
        j = pl.program_id(1)

        @pl.when(j == 0)
        def _():
            dq_acc[...] = jnp.zeros_like(dq_acc)

        dk_acc[...] = jnp.zeros_like(dk_acc)
        dv_acc[...] = jnp.zeros_like(dv_acc)

        def step(i, diag):
            off = pl.multiple_of(i * tq, tq)
            ts, nsub = (tsd, tq // tsd) if diag else (tq, 1)
            for u in range(nsub):
                cols = slice(u * ts, (u + 1) * ts)
                nk_u = (u + 1) * ts if diag else tk
                rows = pl.ds(off + u * ts, ts)
                pre = []
                for hh in range(hg):
                    hq, hv = slice(hh * MLA_HP, (hh + 1) * MLA_HP), slice(hh * MLA_V, (hh + 1) * MLA_V)
                    qi, doi = q_ref[rows, hq], do_ref[rows, hv]
                    st = lax.dot_general(k_ref[:nk_u, hq], qi, _NT, preferred_element_type=F32)
                    dpt = lax.dot_general(v_ref[:nk_u, hv], doi, _NT, preferred_element_type=F32)
                    pre.append((qi, doi, st, dpt))
                for hh in range(hg):
                    hq, hv = slice(hh * MLA_HP, (hh + 1) * MLA_HP), slice(hh * MLA_V, (hh + 1) * MLA_V)
                    qi, doi, st, dpt = pre[hh]
                    if diag:
                        qcol = u * ts + lax.broadcasted_iota(jnp.int32, (nk_u, ts), 1)
                        st = jnp.where(lax.broadcasted_iota(jnp.int32, (nk_u, ts), 0) <= qcol, st, NEG_INF)
                    pt = jnp.exp(st - lse_ref[hh, i][:, cols])
                    dv_acc[:nk_u, hv] += jnp.dot(pt.astype(BF16), doi, preferred_element_type=F32)
                    dsb = (pt * (dpt - dl_ref[hh, i][:, cols])).astype(BF16)
                    dk_acc[:nk_u, hq] += jnp.dot(dsb, qi, preferred_element_type=F32)
                    dq_acc[hh, i, :, cols] += jnp.dot(kt_ref[hq, :nk_u], dsb, preferred_element_type=F32)

        def loop_body(i, carry):
            step(i, False)
            return carry

        step(j, True)
        lax.fori_loop(j + 1, nq, loop_body, 0)
        dk_ref[...] = dk_acc[...].astype(BF16)
        dv_ref[...] = dv_acc[...].astype(BF16)

        @pl.when(j == nk - 1)
        def _():
            dqt_ref[...] = dq_acc[...].astype(BF16)

    stat = pl.BlockSpec((hg, nq, 1, tq), lambda h, j: (h, 0, 0, 0))
    return pl.pallas_call(
        body, name=name, grid=(MLA_H // hg, nk),
        in_specs=[pl.BlockSpec((T, hg * MLA_HP), lambda h, j: (0, h)), pl.BlockSpec((tk, hg * MLA_HP), lambda h, j: (j, h)),
                  pl.BlockSpec((hg * MLA_HP, tk), lambda h, j: (h, j)), pl.BlockSpec((tk, hg * MLA_V), lambda h, j: (j, h)),
                  pl.BlockSpec((T, hg * MLA_V), lambda h, j: (0, h)), stat, stat],
        out_specs=[pl.BlockSpec((hg, nq, MLA_HP, tq), lambda h, j: (h, 0, 0, 0)), pl.BlockSpec((tk, hg * MLA_HP), lambda h, j: (j, h)),
                   pl.BlockSpec((tk, hg * MLA_V), lambda h, j: (j, h))],
        out_shape=[jax.ShapeDtypeStruct((MLA_H, nq, MLA_HP, tq), BF16), jax.ShapeDtypeStruct((T, MLA_H * MLA_HP), BF16),
                   jax.ShapeDtypeStruct((T, MLA_H * MLA_V), BF16)],
        scratch_shapes=[pltpu.VMEM((hg, nq, MLA_HP, tq), F32), pltpu.VMEM((tk, hg * MLA_HP), F32), pltpu.VMEM((tk, hg * MLA_V), F32)],
        compiler_params=pltpu.CompilerParams(dimension_semantics=("parallel", "arbitrary")),
    )(q, k, kt, v, do, lse, delta)


ADA_TN = 512


def _silu(v):
    return v * (1.0 / (1.0 + jnp.exp(-v)))


def _ada_fwd(c_all, ada_w, ada_b_loc, name):
    L, D, Nc = ada_w.shape
    B = c_all.shape[0]

    def body(c_ref, w_ref, b_ref, o_ref):
        ca = _silu(c_ref[...]).astype(BF16)
        o_ref[...] = jnp.dot(ca, w_ref[...].astype(BF16), preferred_element_type=F32) + b_ref[...]

    return pl.pallas_call(
        body, name=name, grid=(L, Nc // ADA_TN),
        in_specs=[pl.BlockSpec((B, D), lambda l, n: (0, 0)), pl.BlockSpec((None, D, ADA_TN), lambda l, n: (l, 0, n)),
                  pl.BlockSpec((None, 1, ADA_TN), lambda l, n: (l, 0, n))],
        out_specs=pl.BlockSpec((None, B, ADA_TN), lambda l, n: (l, 0, n)),
        out_shape=jax.ShapeDtypeStruct((L, B, Nc), F32),
        compiler_params=pltpu.CompilerParams(dimension_semantics=("parallel", "parallel")),
    )(c_all, ada_w, ada_b_loc)


def _ada_bwd(c_all_t, dmod_loc, name):
    D, B = c_all_t.shape
    L, _, Nc = dmod_loc.shape

    def body(c_ref, d_ref, o_ref):
        ca = _silu(c_ref[...])
        dv = d_ref[...]
        acc = ca[:, 0:1] * dv[0:1, :]
        for b in range(1, B):
            acc = acc + ca[:, b:b + 1] * dv[b:b + 1, :]
        o_ref[...] = acc

    return pl.pallas_call(
        body, name=name, grid=(L, Nc // ADA_TN),
        in_specs=[pl.BlockSpec((D, B), lambda l, n: (0, 0)), pl.BlockSpec((None, B, ADA_TN), lambda l, n: (l, 0, n))],
        out_specs=pl.BlockSpec((None, D, ADA_TN), lambda l, n: (l, 0, n)),
        out_shape=jax.ShapeDtypeStruct((L, D, Nc), F32),
        compiler_params=pltpu.CompilerParams(dimension_semantics=("parallel", "parallel")),
    )(c_all_t, dmod_loc)


def _sum_lead(parts, name, out_dtype=F32):
    R, C = parts[0].shape[1:]
    n_tot = sum(p.shape[0] for p in parts)
    tr = R
    for cand in (512, 256, 128, 64, 32, 16):
        if R % cand == 0 and cand * C * 4 * n_tot <= (8 << 20):
            tr = cand
            break

    def body(*refs):
        o_ref = refs[-1]
        acc = None
        for r in refs[:-1]:
            for s in range(r.shape[0]):
                acc = r[s].astype(F32) if acc is None else acc + r[s].astype(F32)
        o_ref[...] = acc.astype(o_ref.dtype)

    return pl.pallas_call(
        body, name=name, grid=(R // tr,),
        in_specs=[pl.BlockSpec((p.shape[0], tr, C), lambda i: (0, i, 0)) for p in parts],
        out_specs=pl.BlockSpec((tr, C), lambda i: (i, 0)),
        out_shape=jax.ShapeDtypeStruct((R, C), out_dtype),
        compiler_params=pltpu.CompilerParams(dimension_semantics=("parallel",)),
    )(*parts)


_ADAM_C1 = 1.0 - ADAM_B1 ** ADAM_STEP
_ADAM_C2 = 1.0 - ADAM_B2 ** ADAM_STEP


def _adamw(w, g, m, v, name):
    shape = w.shape
    C = shape[-1]
    R = math.prod(shape[:-1]) if len(shape) > 1 else 1
    w2, g2, m2, v2 = (a.reshape(R, C) for a in (w, g, m, v))
    tr = R
    for cand in (1024, 512, 256, 128, 64, 32, 16, 8):
        if R % cand == 0 and cand * C * 4 <= (1 << 20):
            tr = cand
            break

    def body(w_ref, g_ref, m_ref, v_ref, d_ref, nm_ref, nv_ref):
        gv = g_ref[...]
        mn = ADAM_B1 * m_ref[...] + (1.0 - ADAM_B1) * gv
        vn = ADAM_B2 * v_ref[...] + (1.0 - ADAM_B2) * (gv * gv)
        nm_ref[...] = mn
        nv_ref[...] = vn
        m_hat = mn / _ADAM_C1
        v_hat = vn / _ADAM_C2
        d_ref[...] = -ADAM_LR * (m_hat / (jnp.sqrt(v_hat) + ADAM_EPS) + ADAM_WD * w_ref[...])

    spec = pl.BlockSpec((tr, C), lambda i: (i, 0))
    outs = pl.pallas_call(
        body, name=name, grid=(R // tr,),
        in_specs=[spec] * 4, out_specs=[spec] * 3,
        out_shape=[jax.ShapeDtypeStruct((R, C), F32)] * 3,
        compiler_params=pltpu.CompilerParams(dimension_semantics=("parallel",)),
    )(w2, g2, m2, v2)
    return tuple(o.reshape(shape) for o in outs)


def _row_tile(rows, cols, itemsize, budget):
    for cand in (1024, 512, 256, 128, 64, 32, 16):
        if rows % cand == 0 and cand * cols * itemsize <= budget:
            return cand
    return rows


def _sum_sel(sel, stacked, others, name, out_dtype):
    R, C = stacked.shape[1:]
    n_tot = 1 + sum(o.shape[0] for o in others)
    tr = _row_tile(R, C, 4 * n_tot, 8 << 20)

    def body(sel_ref, s_ref, *refs):
        o_ref = refs[-1]
        acc = s_ref[...].astype(F32)
        for r in refs[:-1]:
            for s in range(r.shape[0]):
                acc = acc + r[s].astype(F32)
        o_ref[...] = acc.astype(o_ref.dtype)

    return pl.pallas_call(
        body, name=name,
        grid_spec=pltpu.PrefetchScalarGridSpec(
            num_scalar_prefetch=1, grid=(R // tr,),
            in_specs=[pl.BlockSpec((None, tr, C), lambda i, s: (s[0], i, 0))] + [pl.BlockSpec((o.shape[0], tr, C), lambda i, s: (0, i, 0)) for o in others],
            out_specs=pl.BlockSpec((tr, C), lambda i, s: (i, 0))),
        out_shape=jax.ShapeDtypeStruct((R, C), out_dtype),
        compiler_params=pltpu.CompilerParams(dimension_semantics=("parallel",)),
    )(sel, stacked, *others)


def _adamw_piece(cidx, w2, m2, v2, mine, got, bufs, row0, name):
    hr, C = mine.shape
    tr = _row_tile(math.gcd(hr, row0) if row0 else hr, C, 4, 1 << 20)
    nt = hr // tr

    def body(c_ref, w_ref, m_ref, v_ref, a_ref, b_ref, _g, _d, _nm, _nv, g_ref, d_ref, nm_ref, nv_ref):
        gv = jnp.where(pl.program_id(0) == c_ref[0], a_ref[...], b_ref[...])
        mn = ADAM_B1 * m_ref[...] + (1.0 - ADAM_B1) * gv
        vn = ADAM_B2 * v_ref[...] + (1.0 - ADAM_B2) * (gv * gv)
        g_ref[...] = gv
        nm_ref[...] = mn
        nv_ref[...] = vn
        d_ref[...] = -ADAM_LR * ((mn / _ADAM_C1) / (jnp.sqrt(vn / _ADAM_C2) + ADAM_EPS) + ADAM_WD * w_ref[...])

    rows = pl.BlockSpec((tr, C), lambda hf, t, c: (row0 // tr + hf * nt + t, 0))
    half = pl.BlockSpec((tr, C), lambda hf, t, c: (t, 0))
    return pl.pallas_call(
        body, name=name,
        grid_spec=pltpu.PrefetchScalarGridSpec(num_scalar_prefetch=1, grid=(2, nt), in_specs=[rows] * 3 + [half] * 2 + [_ANY_SPEC] * 4,
                                               out_specs=[rows] * 4),
        out_shape=[jax.ShapeDtypeStruct(w2.shape, F32)] * 4,
        input_output_aliases={6 + n: n for n in range(4)},
        compiler_params=pltpu.CompilerParams(dimension_semantics=("parallel", "parallel")),
    )(cidx, w2, m2, v2, mine, got, *bufs)


_VMEM_SPEC = pl.BlockSpec(memory_space=pltpu.VMEM)
_HBM_SPEC = pl.BlockSpec(memory_space=pltpu.HBM)


def _flip(v, bit):
    return (1 - v) if bit else v


def _allgather8(v, name):
    def body(v_ref, out_ref, send_sems, recv_sems, local_sem):
        x, y, c = _idx()
        me = 4 * x + 2 * y + c
        mine = pltpu.make_async_copy(v_ref, out_ref.at[me], local_sem)
        mine.start()
        sends = []
        for k in range(1, N_DEV):
            peer = (_flip(x, k & 4), _flip(y, k & 2), _flip(c, k & 1))
            cp = pltpu.make_async_remote_copy(src_ref=v_ref, dst_ref=out_ref.at[me], send_sem=send_sems.at[k - 1], recv_sem=recv_sems.at[k - 1],
                                              device_id=peer, device_id_type=MESH)
            cp.start()
            sends.append(cp)
        for k in range(1, N_DEV):
            px, py, pc = _flip(x, k & 4), _flip(y, k & 2), _flip(c, k & 1)
            src = 4 * px + 2 * py + pc
            pltpu.make_async_remote_copy(src_ref=v_ref, dst_ref=out_ref.at[src], send_sem=send_sems.at[k - 1], recv_sem=recv_sems.at[k - 1],
                                         device_id=(px, py, pc), device_id_type=MESH).wait_recv()
        for cp in sends:
            cp.wait_send()
        mine.wait()

    return pl.pallas_call(
        body, name=name,
        out_shape=jax.ShapeDtypeStruct((N_DEV, *v.shape), v.dtype),
        in_specs=[_VMEM_SPEC], out_specs=_VMEM_SPEC,
        scratch_shapes=[pltpu.SemaphoreType.DMA((N_DEV - 1,)), pltpu.SemaphoreType.DMA((N_DEV - 1,)), pltpu.SemaphoreType.DMA],
    )(v)


def _mod_exchange(modp, name):
    _, L, Nc = modp.shape

    def body(p_ref, out_ref, send_sems, recv_sems, local_sem):
        x, y, c = _idx()
        me, chip = 4 * x + 2 * y + c, 2 * x + y
        mine = pltpu.make_async_copy(p_ref.at[me], out_ref.at[chip], local_sem)
        mine.start()
        sends = []
        for k in range(1, N_CHIPS):
            px, py = _flip(x, k & 2), _flip(y, k & 1)
            cp = pltpu.make_async_remote_copy(src_ref=p_ref.at[4 * px + 2 * py + c], dst_ref=out_ref.at[chip],
                                              send_sem=send_sems.at[k - 1], recv_sem=recv_sems.at[k - 1], device_id=(px, py, c), device_id_type=MESH)
            cp.start()
            sends.append(cp)
        for k in range(1, N_CHIPS):
            px, py = _flip(x, k & 2), _flip(y, k & 1)
            pltpu.make_async_remote_copy(src_ref=p_ref.at[me], dst_ref=out_ref.at[2 * px + py], send_sem=send_sems.at[k - 1],
                                         recv_sem=recv_sems.at[k - 1], device_id=(px, py, c), device_id_type=MESH).wait_recv()
        for cp in sends:
            cp.wait_send()
        mine.wait()

    return pl.pallas_call(
        body, name=name,
        out_shape=jax.ShapeDtypeStruct((N_CHIPS, L, Nc), modp.dtype),
        in_specs=[_VMEM_SPEC], out_specs=_VMEM_SPEC,
        scratch_shapes=[pltpu.SemaphoreType.DMA((N_CHIPS - 1,)), pltpu.SemaphoreType.DMA((N_CHIPS - 1,)), pltpu.SemaphoreType.DMA],
    )(modp)


_SEM_SPEC = pl.BlockSpec(memory_space=pltpu.SEMAPHORE)
_ANY_SPEC = pl.BlockSpec(memory_space=pl.ANY)
_EFFECT = pltpu.SideEffectType.DATAFLOW_SIDE_EFFECTING


def _hbm(a):
    return pltpu.with_memory_space_constraint(a, pltpu.HBM)


def _xchip_copies(mode, srcs, lands, send_sems, recv_sems, waiting):
    x, y, c = _idx()
    chip = 2 * x + y
    out = []
    for a in range(len(srcs)):
        for k in range(1, _n_peers(mode) + 1):
            if mode == "all8":
                px, py, pc = _flip(x, k & 4), _flip(y, k & 2), _flip(c, k & 1)
                src, dst, mine = srcs[a], lands[a].at[4 * x + 2 * y + c], lands[a].at[4 * px + 2 * py + pc]
            else:
                px, py, pc = _flip(x, k & 2), _flip(y, k & 1), c
                peer = 2 * px + py
                if mode == "gather":
                    src, dst, mine = srcs[a].at[c], lands[a].at[chip, c], lands[a].at[peer, c]
                else:
                    src, dst, mine = srcs[a].at[peer], lands[a].at[k - 1], lands[a].at[k - 1]
            q = a * _n_peers(mode) + k - 1
            out.append(pltpu.make_async_remote_copy(src_ref=src, dst_ref=mine if waiting else dst, send_sem=send_sems[q], recv_sem=recv_sems[q],
                                                    device_id=(px, py, pc), device_id_type=MESH))
    return out


def _n_peers(mode):
    return N_DEV - 1 if mode == "all8" else N_CHIPS - 1


def _xchip_start(mode, srcs, land_shapes, dep, name):
    n = len(srcs)
    ns = n * _n_peers(mode)

    def body(*refs):
        src_refs, land_refs = refs[:n], refs[n:2 * n]
        outs = refs[2 * n + 1:]
        for cp in _xchip_copies(mode, src_refs, land_refs, outs[:ns], outs[ns:2 * ns], waiting=False):
            cp.start()
        outs[-1][...] = jnp.zeros_like(outs[-1])

    lands = [_hbm(lax.empty(s.shape, s.dtype)) for s in land_shapes]
    outs = pl.pallas_call(
        body, name=name,
        out_shape=(*[pltpu.SemaphoreType.DMA(())] * (2 * ns), *[pltpu.HBM(s.shape, s.dtype) for s in srcs],
                   *[pltpu.HBM(s.shape, s.dtype) for s in land_shapes], jax.ShapeDtypeStruct((8, 128), F32)),
        in_specs=[_HBM_SPEC] * (2 * n) + [_ANY_SPEC],
        out_specs=(*[_SEM_SPEC] * (2 * ns), *[_HBM_SPEC] * (2 * n), _VMEM_SPEC),
        input_output_aliases={i: 2 * ns + i for i in range(2 * n)},
        compiler_params=pltpu.CompilerParams(has_side_effects=_EFFECT),
    )(*[_hbm(s) for s in srcs], *lands, dep)
    return list(outs[:ns]), list(outs[ns:2 * ns]), list(outs[2 * ns:2 * ns + n]), list(outs[2 * ns + n:2 * ns + 2 * n]), outs[-1]


def _xchip_wait(mode, send_sems, recv_sems, srcs, lands, after, name):
    n = len(srcs)
    ns = n * _n_peers(mode)

    def body(*refs):
        src_refs, land_refs = refs[:n], refs[n:2 * n]
        sems = refs[2 * n:2 * n + 2 * ns]
        for cp in _xchip_copies(mode, src_refs, land_refs, sems[:ns], sems[ns:], waiting=True):
            cp.wait_send()
            cp.wait_recv()

    outs = pl.pallas_call(
        body, name=name,
        out_shape=(*[pltpu.HBM(s.shape, s.dtype) for s in srcs], *[pltpu.HBM(s.shape, s.dtype) for s in lands]),
        in_specs=[_HBM_SPEC] * (2 * n) + [_SEM_SPEC] * (2 * ns) + [_ANY_SPEC] * len(after),
        out_specs=tuple([_HBM_SPEC] * (2 * n)),
        input_output_aliases={i: i for i in range(2 * n)},
        compiler_params=pltpu.CompilerParams(has_side_effects=_EFFECT),
    )(*srcs, *lands, *send_sems, *recv_sems, *after)
    return list(outs[:n]), list(outs[n:])


def _sibling_fwd(lands, name):
    n = len(lands)

    def body(*refs):
        outs = refs[n:2 * n]
        send_sems, recv_sems = refs[2 * n:]
        x, y, c = _idx()
        sib = (x, y, 1 - c)
        sends = []
        for a in range(n):
            for k in range(1, N_CHIPS):
                src = 2 * _flip(x, k & 2) + _flip(y, k & 1)
                cp = pltpu.make_async_remote_copy(src_ref=outs[a].at[src, c], dst_ref=outs[a].at[src, c], send_sem=send_sems.at[a, k - 1],
                                                  recv_sem=recv_sems.at[a, k - 1], device_id=sib, device_id_type=MESH)
                cp.start()
                sends.append(cp)
        for a in range(n):
            for k in range(1, N_CHIPS):
                src = 2 * _flip(x, k & 2) + _flip(y, k & 1)
                pltpu.make_async_remote_copy(src_ref=outs[a].at[src, c], dst_ref=outs[a].at[src, 1 - c], send_sem=send_sems.at[a, k - 1],
                                             recv_sem=recv_sems.at[a, k - 1], device_id=sib, device_id_type=MESH).wait_recv()
        for cp in sends:
            cp.wait_send()

    return pl.pallas_call(
        body, name=name,
        out_shape=[jax.ShapeDtypeStruct(s.shape, s.dtype) for s in lands],
        in_specs=[_HBM_SPEC] * n, out_specs=[_HBM_SPEC] * n,
        input_output_aliases={i: i for i in range(n)},
        scratch_shapes=[pltpu.SemaphoreType.DMA((n, N_CHIPS - 1)), pltpu.SemaphoreType.DMA((n, N_CHIPS - 1))],
    )(*lands)


def _sibling_swap(parts, name):
    n = len(parts)

    def body(*refs):
        ins, outs = refs[:n], refs[n:2 * n]
        send_sems, recv_sems = refs[2 * n:]
        x, y, c = _idx()
        cps = []
        for a in range(n):
            cp = pltpu.make_async_remote_copy(src_ref=ins[a].at[1 - c], dst_ref=outs[a], send_sem=send_sems.at[a], recv_sem=recv_sems.at[a],
                                              device_id=(x, y, 1 - c), device_id_type=MESH)
            cp.start()
            cps.append(cp)
        for cp in cps:
            cp.wait()

    return pl.pallas_call(
        body, name=name,
        out_shape=[jax.ShapeDtypeStruct(p.shape[1:], p.dtype) for p in parts],
        in_specs=[_HBM_SPEC] * n, out_specs=[_HBM_SPEC] * n,
        scratch_shapes=[pltpu.SemaphoreType.DMA((n,)), pltpu.SemaphoreType.DMA((n,))],
    )(*parts)


def _sibling_send(halves, name):
    n = len(halves)

    def body(*refs):
        ins, outs = refs[:n], refs[n:2 * n]
        send_sems, recv_sems = refs[2 * n:]
        x, y, c = _idx()
        cps = []
        for a in range(n):
            cp = pltpu.make_async_remote_copy(src_ref=ins[a], dst_ref=outs[a], send_sem=send_sems.at[a], recv_sem=recv_sems.at[a],
                                              device_id=(x, y, 1 - c), device_id_type=MESH)
            cp.start()
            cps.append(cp)
        for cp in cps:
            cp.wait()

    return pl.pallas_call(
        body, name=name,
        out_shape=[jax.ShapeDtypeStruct(h.shape, h.dtype) for h in halves],
        in_specs=[_HBM_SPEC] * n, out_specs=[_HBM_SPEC] * n,
        scratch_shapes=[pltpu.SemaphoreType.DMA((n,)), pltpu.SemaphoreType.DMA((n,))],
    )(*halves)


def _col_full(g):
    k, n = g.shape[1], g.shape[2]
    return g.transpose(1, 0, 2).reshape(k, N_CHIPS * n)


def _col_blocks(w):
    k, n = w.shape
    return w.reshape(k, N_CHIPS, n // N_CHIPS).transpose(1, 0, 2)


def _row_blocks(w):
    k, n = w.shape
    return w.reshape(N_CHIPS, k // N_CHIPS, n)


_UQ_HEAD = MLA_NOPE + MLA_ROPE

_LAT = MLA_QL + MLA_KVL + MLA_ROPE
_POOL_R = len(POOL_WINDOWS) * (POOL_GD // N_CHIPS)

_PIECE_KINDS = {
    "mlp_w1": (D_MODEL, D_MODEL, lambda g: g, _col_blocks),
    "mlp_w2": (D_MODEL, D_MODEL, lambda g: g.reshape(4 * D_MODEL, D_MODEL), _row_blocks),
    "pool_w": (_POOL_R, POOL_GD,
               lambda g: g.reshape(N_CHIPS, len(POOL_WINDOWS), POOL_GD // N_CHIPS, POOL_GD).transpose(1, 0, 2, 3).reshape(len(POOL_WINDOWS), POOL_GD, POOL_GD),
               lambda w: w.reshape(len(POOL_WINDOWS), N_CHIPS, POOL_GD // N_CHIPS, POOL_GD).transpose(1, 0, 2, 3).reshape(N_CHIPS, _POOL_R, POOL_GD)),
    "sgu_w_in": (D_MODEL, 2 * SGU_W // N_CHIPS, _col_full, _col_blocks),
    "sgu_w_out": (SGU_W // N_CHIPS, D_MODEL, lambda g: g.reshape(SGU_W, D_MODEL), _row_blocks),
    "mla_w_dq_dkv": (D_MODEL // N_CHIPS, _LAT, lambda g: jnp.pad(g.reshape(D_MODEL, _LAT), ((0, 0), (0, MLA_LATP - _LAT))),
                     lambda w: _row_blocks(w[:, :_LAT])),
    "mla_w_uq": (MLA_QL, MLA_H * _UQ_HEAD // N_CHIPS,
                 lambda g: jnp.pad(_col_full(g).reshape(MLA_QL, MLA_H, _UQ_HEAD), ((0, 0), (0, 0), (0, MLA_HP - _UQ_HEAD))).reshape(MLA_QL, MLA_H * MLA_HP),
                 lambda w: _col_blocks(w.reshape(MLA_QL, MLA_H, MLA_HP)[:, :, :_UQ_HEAD].reshape(MLA_QL, MLA_H * _UQ_HEAD))),
    "mla_w_ukv": (MLA_KVL, MLA_H * (MLA_NOPE + MLA_V) // N_CHIPS, _col_full, _col_blocks),
    "mla_w_o": (MLA_H * MLA_V // N_CHIPS, D_MODEL, lambda g: g.reshape(MLA_H * MLA_V, D_MODEL), _row_blocks),
}
_MIXER_KINDS = (("pool_w",), ("sgu_w_in", "sgu_w_out"), ("mla_w_dq_dkv", "mla_w_uq", "mla_w_ukv", "mla_w_o"))


def _layer_pieces(i):
    return [(k, i // N_MIXERS) for k in _MIXER_KINDS[i % N_MIXERS]] + [("mlp_w1", i), ("mlp_w2", i)]


def _rope_tables(positions):
    inv_freq = ROPE_THETA ** (-jnp.arange(0, MLA_ROPE, 2, dtype=F32) / MLA_ROPE)
    ang = positions.astype(F32)[:, None] * inv_freq
    cos, sin = jnp.cos(ang), jnp.sin(ang)
    z32, z64 = jnp.zeros_like(cos), jnp.zeros((positions.shape[0], 64), F32)
    return (jnp.concatenate([cos, cos, z64], axis=1), jnp.concatenate([-sin, z32, z64], axis=1), jnp.concatenate([z32, sin, z64], axis=1))


def _local_step(x, positions, target, mod, S, weights_of, grads_of):
    D = D_MODEL
    cc, sa, sb = _rope_tables(positions)
    mods = [[mod[i:i + 1, n * D:(n + 1) * D] for n in range(6)] for i in range(DEPTH)]
    h_dtype = lambda i: F32 if i % N_MIXERS == 0 else BF16
    saved = []
    h = _norm_mod_fwd(x, S["norm_mix_g"][0:1], mods[0][1], mods[0][0], h_dtype(0), "l0_norm1")
    for i in range(DEPTH):
        sh1, sc1, g1, sh2, sc2, g2 = mods[i]
        kind, j = i % N_MIXERS, i // N_MIXERS
        gmlp = S["norm_mlp_g"][i:i + 1]
        W = weights_of(i, "mix", x)
        st = {"x": x}
        norm2 = ((gmlp, "n"), (sc2, "n"), (sh2, "n"))
        if kind == 0:
            x2, pooled, ypre, h2 = _pool_fwd(h, W["pool_w"], S["pool_scale"][j:j + 1], x, g1, gmlp, sc2, sh2, f"l{i}_pool")
            st.update(pooled=pooled, y=ypre)
        elif kind == 1:
            zz = _mm(h, W["sgu_w_in"], out_dtypes=(F32,), name=f"l{i}_sgu_in")
            bs_t = S["sgu_b_s"].T
            gated = _sgu_gate_fwd(zz, S["sgu_ln_g"], S["sgu_ln_b"], S["sgu_w_s"], bs_t, f"l{i}_sgu_gate")
            x2, y, h2 = _mm(gated, W["sgu_w_out"], epi=_epi_residual_norm, extras=((x, "mn"), (g1, "n"), *norm2), out_dtypes=(F32, BF16, BF16),
                            tn=D, name=f"l{i}_sgu_out")
            st.update(h=h, zz=zz, gated=gated, y=y, bs_t=bs_t)
        else:
            lat = _mm(h, W["mla_w_dq_dkv"], out_dtypes=(F32,), name=f"l{i}_mla_lat")
            cqn, ckvn, krot = _mla_lat_fwd(lat, S["mla_q_norm_g"], S["mla_kv_norm_g"], cc, sa, sb, f"l{i}_mla_latn")
            qpad = _mm(cqn, W["mla_w_uq"], out_dtypes=(F32,), name=f"l{i}_mla_uq")
            kv = _mm(ckvn, W["mla_w_ukv"], out_dtypes=(F32,), name=f"l{i}_mla_ukv")
            q, k, kt, v, vt = _mla_prep(qpad, kv, krot, cc, sa, sb, f"l{i}_mla_prep")
            o, lse = _attn_fwd(q, k, vt, f"l{i}_attn")
            x2, y, h2 = _mm(o, W["mla_w_o"], epi=_epi_residual_norm, extras=((x, "mn"), (g1, "n"), *norm2), out_dtypes=(F32, BF16, BF16),
                            tn=D, name=f"l{i}_mla_o")
            st.update(h=h, lat=lat, cqn=cqn, ckvn=ckvn, q=q, k=k, kt=kt, v=v, o=o, lse=lse, y=y)
        W = {**W, **weights_of(i, "mlp", x2)}
        z, r2 = _mm(h2, W["mlp_w1"], epi=_epi_sq_relu, out_dtypes=(BF16, BF16), name=f"l{i}_mlp1")
        if i + 1 < DEPTH:
            norm1 = ((S["norm_mix_g"][i + 1:i + 2], "n"), (mods[i + 1][1], "n"), (mods[i + 1][0], "n"))
            x3, o2, h = _mm(z, W["mlp_w2"], epi=_epi_residual_norm, extras=((x2, "mn"), (g2, "n"), *norm1), out_dtypes=(F32, BF16, h_dtype(i + 1)),
                            tn=D, name=f"l{i}_mlp2")
        else:
            x3, o2 = _mm(z, W["mlp_w2"], epi=_epi_residual, extras=((x2, "mn"), (g2, "n")), out_dtypes=(F32, BF16), name=f"l{i}_mlp2")
        st.update(x2=x2, h2=h2, z=z, r2=r2, o2=o2, W=W)
        saved.append(st)
        x = x3

    loss, dx, dfinal_g = _loss_head(x, target, S["final_g"], "loss_head")

    gS = {"final_g": dfinal_g, "norm_mix_g": [None] * DEPTH, "norm_mlp_g": [None] * DEPTH, "pool_scale": [None] * 2}
    dmod = [None] * DEPTH
    tok = jnp.zeros((), F32)
    for i in reversed(range(DEPTH)):
        st = saved[i]
        W, gW = st["W"], {}
        sh1, sc1, g1, sh2, sc2, g2 = (mod[i:i + 1, n * D:(n + 1) * D] for n in range(6))
        kind, j = i % N_MIXERS, i // N_MIXERS
        gmix, gmlp = S["norm_mix_g"][i:i + 1], S["norm_mlp_g"][i:i + 1]
        do2, dg2 = _resid_bwd(dx, st["o2"], g2 + tok, f"l{i}_b_res2")
        da = _mm(do2, W["mlp_w2"], tb=True, epi=lambda acc, rt: (acc * rt.astype(F32),), extras=((st["r2"], "mn"),), name=f"l{i}_b_dz")
        gW["mlp_w2"] = _mm(st["z"], do2, ta=True, chip_blocks="row", name=f"l{i}_b_dw2")
        dh2 = _mm(da, W["mlp_w1"], tb=True, out_dtypes=(F32,), name=f"l{i}_b_dh2")
        gW["mlp_w1"] = _mm(st["h2"], da, ta=True, chip_blocks="col", name=f"l{i}_b_dw1")
        dx2, dgmlp, dsc2, dsh2 = _norm_mod_bwd(st["x2"], dh2, dx, gmlp, sc2, f"l{i}_b_norm2")
        gS["norm_mlp_g"][i] = dgmlp
        dy, q1 = _resid_bwd(dx2, st["y"], g1, f"l{i}_b_res1")
        if kind == 0:
            dh, dpw, dpsc, dg1 = _pool_bwd(dy, st["pooled"], W["pool_w"], S["pool_scale"][j:j + 1], g1, q1, f"l{i}_b_pool")
            gW["pool_w"] = dpw.astype(BF16)
            gS["pool_scale"][j] = dpsc
        elif kind == 1:
            dg1 = q1
            dgated = _mm(dy, W["sgu_w_out"], tb=True, name=f"l{i}_b_dgated")
            gW["sgu_w_out"] = _mm(st["gated"], dy, ta=True, name=f"l{i}_b_dwout")
            dzz, dws, dbs, dlg, dlb = _sgu_gate_bwd(st["zz"], dgated, S["sgu_ln_g"], S["sgu_ln_b"], S["sgu_w_s"], st["bs_t"], f"l{i}_b_sgu_gate")
            gS.update(sgu_w_s=dws, sgu_b_s=dbs[:, :, 0], sgu_ln_g=dlg, sgu_ln_b=dlb)
            dh = _mm(dzz, W["sgu_w_in"], tb=True, out_dtypes=(F32,), name=f"l{i}_b_dh_sgu")
            gW["sgu_w_in"] = _mm(st["h"], dzz, ta=True, name=f"l{i}_b_dwin")
        else:
            dg1 = q1
            do = _mm(dy, W["mla_w_o"], tb=True, name=f"l{i}_b_do")
            gW["mla_w_o"] = _mm(st["o"], dy, ta=True, name=f"l{i}_b_dwo")
            delta = _attn_delta(do, st["o"], f"l{i}_b_delta")
            dqt, dk, dv = _attn_bwd(st["q"], st["k"], st["kt"], st["v"], do, st["lse"], delta, f"l{i}_b_attn")
            dqpad, dkv, dkrot = _mla_prep_bwd(dqt, dk, dv, cc, sa, sb, f"l{i}_b_mla_prep")
            dcqn = _mm(dqpad, W["mla_w_uq"], tb=True, out_dtypes=(F32,), name=f"l{i}_b_dcq")
            gW["mla_w_uq"] = _mm(st["cqn"], dqpad, ta=True, name=f"l{i}_b_dwuq")
            dckvn = _mm(dkv, W["mla_w_ukv"], tb=True, out_dtypes=(F32,), name=f"l{i}_b_dckv")
            gW["mla_w_ukv"] = _mm(st["ckvn"], dkv, ta=True, name=f"l{i}_b_dwukv")
            dlat, dqg, dkvg = _mla_lat_bwd(st["lat"], dcqn, dckvn, dkrot, S["mla_q_norm_g"], S["mla_kv_norm_g"], cc, sa, sb, f"l{i}_b_mla_latn")
            gS.update(mla_q_norm_g=dqg, mla_kv_norm_g=dkvg)
            dh = _mm(dlat, W["mla_w_dq_dkv"], tb=True, out_dtypes=(F32,), name=f"l{i}_b_dh_mla")
            gW["mla_w_dq_dkv"] = _mm(st["h"], dlat, ta=True, name=f"l{i}_b_dwdq")
        dx, dgmix, dsc1, dsh1 = _norm_mod_bwd(st["x"], dh, dx2, gmix, sc1, f"l{i}_b_norm1")
        gS["norm_mix_g"][i] = dgmix
        dmod[i] = jnp.concatenate([dsh1, dsc1, dg1, dsh2, dsc2, dg2], axis=1)
        tok = grads_of(i, gW, dx)

    for n in ("norm_mix_g", "norm_mlp_g", "pool_scale"):
        gS[n] = jnp.concatenate(gS[n], axis=0)
    return loss, dx, gS, jnp.concatenate(dmod, axis=0)


_SMALL = {
    "norm_mix_g": (DEPTH, D_MODEL), "norm_mlp_g": (DEPTH, D_MODEL), "sgu_ln_g": (1, SGU_W), "sgu_ln_b": (1, SGU_W),
    "sgu_w_s": (SGU_H, SGU_CHUNK, SGU_CHUNK), "sgu_b_s": (SGU_H, SGU_CHUNK), "mla_kv_norm_g": (1, MLA_KVL), "final_g": (1, D_MODEL),
    "pool_scale": (2, D_MODEL), "mla_q_norm_g": (1, MLA_QL), "dmod": (DEPTH, 6 * D_MODEL),
}
_PACK_W = 1024


def _pack(vals):
    flat = jnp.concatenate([v.reshape(-1) for v in vals])
    rows = -(-flat.shape[0] // (8 * _PACK_W)) * 8
    return jnp.pad(flat, (0, rows * _PACK_W - flat.shape[0])).reshape(rows, _PACK_W)


def _unpack(buf, shapes):
    flat, out, off = buf.reshape(-1), [], 0
    for s in shapes:
        n = math.prod(s)
        out.append(flat[off:off + n].reshape(s))
        off += n
    return out


def kernel(x, c, positions, ada_w, ada_b, norm_mix_g, norm_mlp_g, pool_w, pool_scale, sgu_w_in, sgu_ln_g, sgu_ln_b, sgu_w_s, sgu_b_s, sgu_w_out, mla_w_dq_dkv, mla_q_norm_g, mla_kv_norm_g, mla_w_uq, mla_w_ukv, mla_w_o, mlp_w1, mlp_w2, final_g, loss_target, m_ada_w, m_ada_b, m_norm_mix_g, m_norm_mlp_g, m_pool_w, m_pool_scale, m_sgu_w_in, m_sgu_ln_g, m_sgu_ln_b, m_sgu_w_s, m_sgu_b_s, m_sgu_w_out, m_mla_w_dq_dkv, m_mla_q_norm_g, m_mla_kv_norm_g, m_mla_w_uq, m_mla_w_ukv, m_mla_w_o, m_mlp_w1, m_mlp_w2, m_final_g, v_ada_w, v_ada_b, v_norm_mix_g, v_norm_mlp_g, v_pool_w, v_pool_scale, v_sgu_w_in, v_sgu_ln_g, v_sgu_ln_b, v_sgu_w_s, v_sgu_b_s, v_sgu_w_out, v_mla_w_dq_dkv, v_mla_q_norm_g, v_mla_kv_norm_g, v_mla_w_uq, v_mla_w_ukv, v_mla_w_o, v_mlp_w1, v_mlp_w2, v_final_g):
    P = dict(ada_w=ada_w, ada_b=ada_b, norm_mix_g=norm_mix_g, norm_mlp_g=norm_mlp_g, pool_w=pool_w, pool_scale=pool_scale, sgu_w_in=sgu_w_in,
             sgu_ln_g=sgu_ln_g, sgu_ln_b=sgu_ln_b, sgu_w_s=sgu_w_s, sgu_b_s=sgu_b_s, sgu_w_out=sgu_w_out, mla_w_dq_dkv=mla_w_dq_dkv,
             mla_q_norm_g=mla_q_norm_g, mla_kv_norm_g=mla_kv_norm_g, mla_w_uq=mla_w_uq, mla_w_ukv=mla_w_ukv, mla_w_o=mla_w_o, mlp_w1=mlp_w1,
             mlp_w2=mlp_w2, final_g=final_g)
    M = dict(ada_w=m_ada_w, ada_b=m_ada_b, norm_mix_g=m_norm_mix_g, norm_mlp_g=m_norm_mlp_g, pool_w=m_pool_w, pool_scale=m_pool_scale,
             sgu_w_in=m_sgu_w_in, sgu_ln_g=m_sgu_ln_g, sgu_ln_b=m_sgu_ln_b, sgu_w_s=m_sgu_w_s, sgu_b_s=m_sgu_b_s, sgu_w_out=m_sgu_w_out,
             mla_w_dq_dkv=m_mla_w_dq_dkv, mla_q_norm_g=m_mla_q_norm_g, mla_kv_norm_g=m_mla_kv_norm_g, mla_w_uq=m_mla_w_uq, mla_w_ukv=m_mla_w_ukv,
             mla_w_o=m_mla_w_o, mlp_w1=m_mlp_w1, mlp_w2=m_mlp_w2, final_g=m_final_g)
    V = dict(ada_w=v_ada_w, ada_b=v_ada_b, norm_mix_g=v_norm_mix_g, norm_mlp_g=v_norm_mlp_g, pool_w=v_pool_w, pool_scale=v_pool_scale,
             sgu_w_in=v_sgu_w_in, sgu_ln_g=v_sgu_ln_g, sgu_ln_b=v_sgu_ln_b, sgu_w_s=v_sgu_w_s, sgu_b_s=v_sgu_b_s, sgu_w_out=v_sgu_w_out,
             mla_w_dq_dkv=v_mla_w_dq_dkv, mla_q_norm_g=v_mla_q_norm_g, mla_kv_norm_g=v_mla_kv_norm_g, mla_w_uq=v_mla_w_uq, mla_w_ukv=v_mla_w_ukv,
             mla_w_o=v_mla_w_o, mlp_w1=v_mlp_w1, mlp_w2=v_mlp_w2, final_g=v_final_g)
    order = list(P)
    xi, yi, ci = _idx()
    chip = 2 * xi + yi
    D = D_MODEL
    n_ada = ada_w.shape[2]

    pre = _allgather8(_pack([c, pool_scale, mla_q_norm_g]), "ag_small")
    flat = pre.reshape(N_DEV, -1)
    c_all = flat[:, :D]
    ps_all = flat[0::2, D:D + 2 * (D // N_CHIPS)].reshape(N_CHIPS, 2, D // N_CHIPS).transpose(1, 0, 2).reshape(2, D)
    q0 = D + 2 * (D // N_CHIPS)
    qg_all = flat[0::2, q0:q0 + MLA_QL // N_CHIPS].reshape(1, MLA_QL)

    ada_b_loc = lax.dynamic_slice_in_dim(ada_b, chip * n_ada, n_ada, axis=1)[:, None, :]
    modp = _ada_fwd(c_all, ada_w, ada_b_loc, "ada_fwd")
    mod = _mod_exchange(modp.transpose(1, 0, 2), "mod_exchange").transpose(1, 0, 2).reshape(DEPTH, 6 * D)

    S = dict(norm_mix_g=norm_mix_g, norm_mlp_g=norm_mlp_g, pool_scale=ps_all, sgu_ln_g=sgu_ln_g, sgu_ln_b=sgu_ln_b, sgu_w_s=sgu_w_s[0],
             sgu_b_s=sgu_b_s[0], mla_q_norm_g=qg_all, mla_kv_norm_g=mla_kv_norm_g, final_g=final_g[None, :])
    cidx, chipidx = jnp.reshape(ci, (1,)).astype(jnp.int32), jnp.reshape(chip, (1,)).astype(jnp.int32)
    view2d = lambda a: a.reshape(-1, a.shape[-1])

    def piece_rows(kind, blk):
        r = _PIECE_KINDS[kind][0]
        return blk * r, r

    groups = [_layer_pieces(0)[:-2], _layer_pieces(0)[-2:], _layer_pieces(1)[:-2], _layer_pieces(1)[-2:], _layer_pieces(2), _layer_pieces(3)]
    start_after = {1: (2, 3), 2: (4,), 4: (5,)}
    gathers = {}

    def gather_start(g, dep):
        srcs, shapes = [], []
        for kind, blk in groups[g]:
            r0, r = piece_rows(kind, blk)
            cdim = _PIECE_KINDS[kind][1]
            srcs.append(view2d(P[kind])[r0:r0 + r].astype(BF16).reshape(2, r // 2, cdim))
            shapes.append(jax.ShapeDtypeStruct((N_CHIPS, 2, r // 2, cdim), BF16))
        gathers[g] = _xchip_start("gather", srcs, shapes, dep, f"ag_start_g{g}")

    def gather_finish(g, after):
        ssem, rsem, srcs, lands, _ = gathers.pop(g)
        deps = [after]
        for nxt in start_after.get(g, ()):
            gather_start(nxt, deps[-1])
            deps.append(gathers[nxt][-1])
        srcs, lands = _xchip_wait("gather", ssem, rsem, srcs, lands, deps, f"ag_wait_g{g}")
        lands = _sibling_fwd(lands, f"ag_sibling_g{g}")
        W = {}
        for (kind, _), s, land in zip(groups[g], srcs, lands, strict=True):
            r, cdim, to_full, _ = _PIECE_KINDS[kind]
            W[kind] = to_full(lax.dynamic_update_index_in_dim(land, s, chip, 0).reshape(N_CHIPS, r, cdim))
        return W

    def weights_of(i, part, x_i):
        if i < 2:
            return gather_finish(2 * i + (part == "mlp"), x_i)
        return gather_finish(i + 2, x_i) if part == "mix" else {}

    scatters = {}
    bufs = {n: tuple(lax.empty(view2d(P[n]).shape, F32) for _ in range(4)) for n in _PIECE_KINDS}

    def scatter_start(i, gW, dep):
        pcs = _layer_pieces(i)
        blocked = []
        for kind, _ in pcs:
            r, cdim, _, to_blocks = _PIECE_KINDS[kind]
            g = gW[kind]
            blocked.append(g if g.ndim == 4 else to_blocks(g).reshape(N_CHIPS, 2, r // 2, cdim).transpose(1, 0, 2, 3))
        from_sib = _sibling_swap(blocked, f"rs_sibling_l{i}")
        pair, shapes = [], []
        for (kind, _), b, f in zip(pcs, blocked, from_sib, strict=True):
            _, _, hr, cdim = b.shape
            p = _sum_sel(cidx, b.reshape(2, N_CHIPS * hr, cdim), [f.reshape(1, N_CHIPS * hr, cdim)], f"rs_pair_l{i}_{kind}", BF16)
            pair.append(p.reshape(N_CHIPS, hr, cdim))
            shapes.append(jax.ShapeDtypeStruct((N_CHIPS - 1, hr, cdim), BF16))
        scatters[i] = (pcs, *_xchip_start("scatter", pair, shapes, dep, f"rs_start_l{i}"))
        return scatters[i][-1][0, 0]

    def scatter_finish(i, after):
        pcs, ssem, rsem, pair, lands, _ = scatters.pop(i)
        pair, lands = _xchip_wait("scatter", ssem, rsem, pair, lands, after, f"rs_wait_l{i}")
        halves = [_sum_sel(chipidx, p, [l], f"rs_sum_l{i}_{kind}", F32) for (kind, _), p, l in zip(pcs, pair, lands, strict=True)]
        got = _sibling_send(halves, f"rs_merge_l{i}")
        for (kind, blk), mine, other in zip(pcs, halves, got, strict=True):
            r0, _ = piece_rows(kind, blk)
            bufs[kind] = tuple(_adamw_piece(cidx, view2d(P[kind]), view2d(M[kind]), view2d(V[kind]), mine, other, bufs[kind], r0,
                                            f"adamw_l{i}_{kind}"))
        return lands[0]

    first_layer = {}

    def grads_of(i, gW, dx_i):
        if i == 0:
            first_layer.update(gW)
            return jnp.zeros((), F32)
        dep = scatter_finish(i + 1, [dx_i]) if i + 1 in scatters else dx_i
        return scatter_start(i, gW, dep)

    gather_start(0, mod)
    gather_start(1, gathers[0][-1])
    mod = mod + gathers[1][-1][0, 0]
    loss_l, dx, gS, dmod = _local_step(x[0], positions[0], loss_target[0], mod, S, weights_of, grads_of)
    loss = lax.psum(loss_l[0, 0], ("x", "y", "c"))

    gS["dmod"] = dmod
    packed = _pack([gS[n] for n in _SMALL])
    sg = _xchip_start("all8", [packed], [jax.ShapeDtypeStruct((N_DEV, *packed.shape), F32)], dx, "sg_start")
    tok0 = scatter_start(0, first_layer, sg[-1])
    scatter_finish(1, [dx, scatters[0][-1]])
    sg_src, sg_land = _xchip_wait("all8", sg[0], sg[1], sg[2], sg[3], [bufs[n][0] for n in ("mlp_w1", "mlp_w2", "sgu_w_in", "sgu_w_out")], "sg_wait")
    small = lax.dynamic_update_index_in_dim(sg_land[0], sg_src[0], 4 * xi + 2 * yi + ci, 0) + tok0
    small_sum = _unpack(_sum_lead([small], "sum_small_grads"), list(_SMALL.values()))
    G = dict(zip(_SMALL, small_sum, strict=True))
    grads = {
        "ada_b": G["dmod"], "norm_mix_g": G["norm_mix_g"], "norm_mlp_g": G["norm_mlp_g"], "sgu_ln_g": G["sgu_ln_g"], "sgu_ln_b": G["sgu_ln_b"],
        "sgu_w_s": G["sgu_w_s"][None], "sgu_b_s": G["sgu_b_s"][None], "mla_kv_norm_g": G["mla_kv_norm_g"], "final_g": G["final_g"][0],
        "pool_scale": lax.dynamic_slice_in_dim(G["pool_scale"], chip * (D // N_CHIPS), D // N_CHIPS, axis=1),
        "mla_q_norm_g": lax.dynamic_slice_in_dim(G["mla_q_norm_g"], chip * (MLA_QL // N_CHIPS), MLA_QL // N_CHIPS, axis=1),
    }
    dmod_all = _unpack(small, [(N_DEV,) + (small.shape[1] * _PACK_W,)])[0]
    off = sum(math.prod(s) for n, s in _SMALL.items() if n != "dmod")
    dmod_all = dmod_all[:, off:off + DEPTH * 6 * D].reshape(N_DEV, DEPTH, 6 * D)
    dmod_loc = lax.dynamic_slice_in_dim(dmod_all, chip * n_ada, n_ada, axis=2).transpose(1, 0, 2)
    grads["ada_w"] = _ada_bwd(c_all.T, dmod_loc, "ada_bwd")

    deltas, new_m, new_v = {}, {}, {}
    for n in order:
        if n not in _PIECE_KINDS:
            deltas[n], new_m[n], new_v[n] = _adamw(P[n], grads[n].reshape(P[n].shape), M[n], V[n], f"adamw_{n}")
    scatter_finish(0, [deltas["ada_w"], deltas["sgu_w_s"]] + [bufs[n][0] for n in ("mlp_w1", "mlp_w2", "sgu_w_in", "mla_w_o")])
    for n in _PIECE_KINDS:
        grads[n], deltas[n], new_m[n], new_v[n] = (b.reshape(P[n].shape) for b in bufs[n])
    return (loss, dx[None], *[grads[n].reshape(P[n].shape) for n in order], *[deltas[n] for n in order], *[new_m[n] for n in order],
            *[new_v[n] for n in order])
```

```python
import math

import jax
import jax.numpy as jnp
from jax import lax
from jax.experimental import pallas as pl
from jax.experimental.pallas import tpu as pltpu

F32, BF16 = jnp.float32, jnp.bfloat16
MESH = pl.DeviceIdType.MESH

D_MODEL = 1024
DEPTH = 4
N_MIXERS = 3
POOL_WINDOWS = (2, 4, 8, 16)
POOL_GD = D_MODEL // len(POOL_WINDOWS)
POOL_HALO = 16
SGU_CHUNK = 128
SGU_W = D_MODEL
SGU_HD = 128
SGU_H = SGU_W // SGU_HD
MLA_H = 16
MLA_QL = 256
MLA_KVL = 128
MLA_NOPE = 128
MLA_ROPE = 64
MLA_V = 128
MLA_HP = 256
MLA_LATP = 512
ROPE_THETA = 10000.0
RMS_EPS = 1e-6
LN_EPS = 1e-5
SM_SCALE = (MLA_NOPE + MLA_ROPE) ** -0.5
NEG_INF = -1e30
ADAM_LR, ADAM_B1, ADAM_B2, ADAM_EPS, ADAM_WD, ADAM_STEP = 0.001, 0.9, 0.999, 1e-08, 0.01, 10
N_CHIPS = 4
N_DEV = 8
ROW_TILE = 512
ATT_TILE = 512
ATT_SUB = 256
ATT_FWD_HEADS = 4
ATT_BWD_HEADS = 2
MM_VMEM_BUDGET = 40 << 20


def _idx():
    return lax.axis_index("x"), lax.axis_index("y"), lax.axis_index("c")


def _mm(a, b, *, name, ta=False, tb=False, epi=None, extras=(), out_dtypes=(BF16,), tm=1024, tn=1024, tk=1024, chip_blocks=None, after=None):
    if ta:
        K, M = a.shape
    else:
        M, K = a.shape
    b_chips = b.ndim == 3
    if b_chips:
        assert b.shape[0] == N_CHIPS
        Kb, N = (N_CHIPS * b.shape[2], b.shape[1]) if tb else (b.shape[1], N_CHIPS * b.shape[2])
    elif tb:
        N, Kb = b.shape
    else:
        Kb, N = b.shape
    assert K == Kb, (a.shape, b.shape, ta, tb)
    if b_chips and not tb:
        tn = min(tn, N // N_CHIPS)
    if chip_blocks == "col":
        tm, tn = min(tm, M // 2), min(tn, N // N_CHIPS)
    elif chip_blocks == "row":
        tm = min(tm, M // N_CHIPS // 2)
    tm, tn, tk = min(tm, M), min(tn, N), min(tk, K)

    def vmem_bytes(tm_, tk_):
        per_mn = sum(arr.dtype.itemsize for arr, kind in extras if kind == "mn") + sum(jnp.dtype(dt).itemsize for dt in out_dtypes)
        return 2 * (tm_ * tk_ * a.dtype.itemsize + tk_ * tn * b.dtype.itemsize + tm_ * tn * per_mn)

    if vmem_bytes(tm, K) <= MM_VMEM_BUDGET:
        tk = K
    elif tm >= 512 and vmem_bytes(tm // 2, K) <= MM_VMEM_BUDGET:
        tm, tk = tm // 2, K
    assert M % tm == 0 and N % tn == 0 and K % tk == 0, (M, N, K, tm, tn, tk)
    nk = K // tk
    a_spec = pl.BlockSpec((tk, tm), lambda i, j, k: (k, i)) if ta else pl.BlockSpec((tm, tk), lambda i, j, k: (i, k))
    b_spec = pl.BlockSpec((tn, tk), lambda i, j, k: (j, k)) if tb else pl.BlockSpec((tk, tn), lambda i, j, k: (k, j))
    if b_chips and tb:
        assert nk == 1 and not ta
        b_spec = pl.BlockSpec((N_CHIPS, tn, K // N_CHIPS), lambda i, j, k: (0, j, 0))
    elif b_chips:
        per = N // N_CHIPS // tn
        b_spec = pl.BlockSpec((None, tk, tn), lambda i, j, k: (j // per, k, j % per))
    ex_specs = []
    for arr, kind in extras:
        if kind == "mn":
            ex_specs.append(pl.BlockSpec((tm, tn), lambda i, j, k: (i, j)))
        elif kind == "n":
            ex_specs.append(pl.BlockSpec((1, tn), lambda i, j, k: (0, j)))
        else:
            ex_specs.append(pl.BlockSpec((tm, arr.shape[1]), lambda i, j, k: (i, 0)))
    n_ex, n_out = len(extras), len(out_dtypes)
    n_in = 2 + n_ex + (after is not None)
    dims = (((0 if ta else 1,), (1 if tb else 0,)), ((), ()))

    def body(*refs):
        a_ref, b_ref = refs[0], refs[1]
        ex_refs = refs[2:2 + n_ex]
        out_refs = refs[n_in:n_in + n_out]
        if b_chips and tb:
            kc = K // N_CHIPS
            part = None
            for cb in range(N_CHIPS):
                p = lax.dot_general(a_ref[:, cb * kc:(cb + 1) * kc].astype(BF16), b_ref[cb].astype(BF16), dims, preferred_element_type=F32)
                part = p if part is None else part + p
        else:
            part = lax.dot_general(a_ref[...].astype(BF16), b_ref[...].astype(BF16), dims, preferred_element_type=F32)

        def finish(acc):
            outs = epi(acc, *[r[...] for r in ex_refs]) if epi is not None else (acc,)
            for r, o in zip(out_refs, outs, strict=True):
                r[...] = o.astype(r.dtype)

        if nk == 1:
            finish(part)
        else:
            acc_ref = refs[-1]
            k = pl.program_id(2)

            @pl.when(k == 0)
            def _():
                acc_ref[...] = part

            @pl.when(k > 0)
            def _():
                acc_ref[...] += part

            @pl.when(k == nk - 1)
            def _():
                finish(acc_ref[...])

    out_specs = [pl.BlockSpec((tm, tn), lambda i, j, k: (i, j)) for _ in range(n_out)]
    out_shape = [jax.ShapeDtypeStruct((M, N), dt) for dt in out_dtypes]
    if chip_blocks is not None:
        assert n_out == 1
        if chip_blocks == "col":
            rh, cb = M // 2 // tm, N // N_CHIPS // tn
            out_specs = [pl.BlockSpec((None, None, tm, tn), lambda i, j, k: (i // rh, j // cb, i % rh, j % cb))]
            out_shape = [jax.ShapeDtypeStruct((2, N_CHIPS, M // 2, N // N_CHIPS), out_dtypes[0])]
        else:
            rh = M // N_CHIPS // 2 // tm
            out_specs = [pl.BlockSpec((None, None, tm, tn), lambda i, j, k: ((i // rh) % 2, i // (2 * rh), i % rh, j))]
            out_shape = [jax.ShapeDtypeStruct((2, N_CHIPS, M // N_CHIPS // 2, N), out_dtypes[0])]
    outs = pl.pallas_call(
        body,
        name=name,
        grid=(M // tm, N // tn, nk),
        in_specs=[a_spec, b_spec, *ex_specs] + ([pl.BlockSpec(memory_space=pl.ANY)] if after is not None else []),
        out_specs=out_specs,
        out_shape=out_shape,
        scratch_shapes=[pltpu.VMEM((tm, tn), F32)] if nk > 1 else [],
        compiler_params=pltpu.CompilerParams(dimension_semantics=("parallel", "parallel", "arbitrary")),
    )(a, b, *[arr for arr, _ in extras], *([after] if after is not None else []))
    return outs[0] if n_out == 1 else tuple(outs)


def _epi_sq_relu(acc):
    r = jnp.maximum(acc, 0.0)
    return r * r, 2.0 * r


def _epi_residual(acc, x, g):
    return x + g * acc, acc


def _rms_mod(xv, gain, sc, sh):
    r = lax.rsqrt(jnp.mean(xv * xv, axis=-1, keepdims=True) + RMS_EPS)
    return ((xv * r) * gain) * (1.0 + sc) + sh


def _epi_residual_norm(acc, x, g, gain, sc, sh):
    xn = x + g * acc
    return xn, acc, _rms_mod(xn, gain, sc, sh)


def _row_spec(tr, d):
    return pl.BlockSpec((tr, d), lambda i: (i, 0))


def _vec_spec(d):
    return pl.BlockSpec((1, d), lambda i: (0, 0))


def _colsum(v):
    return jnp.sum(v, axis=0, keepdims=True)


def _norm_mod_fwd(x, gain, sc, sh, out_dtype, name):
    T, D = x.shape
    tr = min(T, ROW_TILE)

    def body(x_ref, g_ref, sc_ref, sh_ref, o_ref):
        o_ref[...] = _rms_mod(x_ref[...], g_ref[...], sc_ref[...], sh_ref[...]).astype(o_ref.dtype)

    return pl.pallas_call(
        body, name=name, grid=(T // tr,),
        in_specs=[_row_spec(tr, D), _vec_spec(D), _vec_spec(D), _vec_spec(D)],
        out_specs=_row_spec(tr, D),
        out_shape=jax.ShapeDtypeStruct((T, D), out_dtype),
        compiler_params=pltpu.CompilerParams(dimension_semantics=("parallel",)),
    )(x, gain, sc, sh)


def _norm_mod_bwd(x, dh, dres, gain, sc, name, res=None):
    T, D = x.shape
    tr = min(T, ROW_TILE)

    def body(x_ref, dh_ref, dres_ref, g_ref, sc_ref, *refs):
        dx_ref, dg_ref, dsc_ref, dsh_ref = refs[-6:-2] if res is not None else refs

        @pl.when(pl.program_id(0) == 0)
        def _():
            dg_ref[...] = jnp.zeros_like(dg_ref)
            dsc_ref[...] = jnp.zeros_like(dsc_ref)
            dsh_ref[...] = jnp.zeros_like(dsh_ref)
            if res is not None:
                refs[-1][...] = jnp.zeros_like(refs[-1])

        xv = x_ref[...]
        r = lax.rsqrt(jnp.mean(xv * xv, axis=-1, keepdims=True) + RMS_EPS)
        xn = xv * r
        dhv = dh_ref[...].astype(F32)
        dsh_ref[...] += _colsum(dhv)
        dsc_ref[...] += _colsum(dhv * (xn * g_ref[...]))
        dt = dhv * (1.0 + sc_ref[...])
        dg_ref[...] += _colsum(dt * xn)
        dxn = dt * g_ref[...]
        dxv = dres_ref[...] + r * (dxn - xn * jnp.mean(dxn * xn, axis=-1, keepdims=True))
        dx_ref[...] = dxv
        if res is not None:
            y_ref, gr_ref, dy_ref, q_ref = refs[0], refs[1], refs[-2], refs[-1]
            dy_ref[...] = (gr_ref[...] * dxv).astype(BF16)
            q_ref[...] += _colsum(dxv * y_ref[...].astype(F32))

    extra_in, extra_spec = ([], []) if res is None else (list(res), [_row_spec(tr, D), _vec_spec(D)])
    return pl.pallas_call(
        body, name=name, grid=(T // tr,),
        in_specs=[_row_spec(tr, D), _row_spec(tr, D), _row_spec(tr, D), _vec_spec(D), _vec_spec(D), *extra_spec],
        out_specs=[_row_spec(tr, D), _vec_spec(D), _vec_spec(D), _vec_spec(D)] + ([_row_spec(tr, D), _vec_spec(D)] if res is not None else []),
        out_shape=[jax.ShapeDtypeStruct((T, D), F32)] + [jax.ShapeDtypeStruct((1, D), F32)] * 3
        + ([jax.ShapeDtypeStruct((T, D), BF16), jax.ShapeDtypeStruct((1, D), F32)] if res is not None else []),
        compiler_params=pltpu.CompilerParams(dimension_semantics=("arbitrary",)),
    )(x, dh, dres, gain, sc, *extra_in)


def _resid_bwd(dx, y, g, name):
    T, D = dx.shape
    tr = min(T, ROW_TILE)

    def body(dx_ref, y_ref, g_ref, dy_ref, q_ref):
        @pl.when(pl.program_id(0) == 0)
        def _():
            q_ref[...] = jnp.zeros_like(q_ref)

        dxv = dx_ref[...]
        dy_ref[...] = (g_ref[...] * dxv).astype(BF16)
        q_ref[...] += _colsum(dxv * y_ref[...].astype(F32))

    return pl.pallas_call(
        body, name=name, grid=(T // tr,),
        in_specs=[_row_spec(tr, D), _row_spec(tr, D), _vec_spec(D)],
        out_specs=[_row_spec(tr, D), _vec_spec(D)],
        out_shape=[jax.ShapeDtypeStruct((T, D), BF16), jax.ShapeDtypeStruct((1, D), F32)],
        compiler_params=pltpu.CompilerParams(dimension_semantics=("arbitrary",)),
    )(dx, y, g)


def _loss_head(x, target, gain, name):
    T, D = x.shape
    tr = min(T, ROW_TILE)

    def body(x_ref, t_ref, g_ref, loss_ref, dx_ref, dg_ref):
        @pl.when(pl.program_id(0) == 0)
        def _():
            loss_ref[...] = jnp.zeros_like(loss_ref)
            dg_ref[...] = jnp.zeros_like(dg_ref)

        xv = x_ref[...]
        r = lax.rsqrt(jnp.mean(xv * xv, axis=-1, keepdims=True) + RMS_EPS)
        xn = xv * r
        err = xn * g_ref[...] - t_ref[...]
        row = jnp.mean(err * err, axis=-1, keepdims=True)
        loss_ref[...] += 0.5 * jnp.sum(row, axis=0, keepdims=True)
        dy = err * (1.0 / D)
        dg_ref[...] += _colsum(dy * xn)
        dxn = dy * g_ref[...]
        dx_ref[...] = r * (dxn - xn * jnp.mean(dxn * xn, axis=-1, keepdims=True))

    return pl.pallas_call(
        body, name=name, grid=(T // tr,),
        in_specs=[_row_spec(tr, D), _row_spec(tr, D), _vec_spec(D)],
        out_specs=[_vec_spec(128), _row_spec(tr, D), _vec_spec(D)],
        out_shape=[jax.ShapeDtypeStruct((1, 128), F32), jax.ShapeDtypeStruct((T, D), F32), jax.ShapeDtypeStruct((1, D), F32)],
        compiler_params=pltpu.CompilerParams(dimension_semantics=("arbitrary",)),
    )(x, target, gain)


def _pool_fwd(h, w, scale, x, g1, gmlp, sc2, sh2, name):
    T, D = h.shape
    tr = min(T, ROW_TILE)

    def body(h_ref, w_ref, sc_ref, x_ref, g_ref, gm_ref, sc2_ref, sh2_ref, x2_ref, pooled_ref, ypre_ref, h2_ref, halo_ref):
        i = pl.program_id(0)

        @pl.when(i == 0)
        def _():
            halo_ref[...] = jnp.zeros_like(halo_ref)

        hv = h_ref[...]
        buf = jnp.concatenate([halo_ref[...], hv], axis=0)
        halo_ref[...] = hv[tr - POOL_HALO:, :]
        t = (i * tr + lax.broadcasted_iota(jnp.int32, (tr, 1), 0)).astype(F32)
        for gi, win in enumerate(POOL_WINDOWS):
            cols = slice(gi * POOL_GD, (gi + 1) * POOL_GD)
            val = buf[:, cols]
            sh = 1
            while sh < win:
                val = val + pltpu.roll(val, sh, axis=0)
                sh *= 2
            pooled = val[POOL_HALO:, :] / jnp.minimum(t + 1.0, float(win)) - hv[:, cols]
            pb = pooled.astype(BF16)
            pooled_ref[:, cols] = pb
            yp = jnp.dot(pb, w_ref[gi], preferred_element_type=F32)
            ypre_ref[:, cols] = yp.astype(BF16)
            x2_ref[:, cols] = x_ref[:, cols] + g_ref[:, cols] * (yp * sc_ref[:, cols])
        h2_ref[...] = _rms_mod(x2_ref[...], gm_ref[...], sc2_ref[...], sh2_ref[...]).astype(BF16)

    return pl.pallas_call(
        body, name=name, grid=(T // tr,),
        in_specs=[_row_spec(tr, D), pl.BlockSpec(w.shape, lambda i: (0, 0, 0)), _vec_spec(D), _row_spec(tr, D), _vec_spec(D), _vec_spec(D),
                  _vec_spec(D), _vec_spec(D)],
        out_specs=[_row_spec(tr, D)] * 4,
        out_shape=[jax.ShapeDtypeStruct((T, D), F32), jax.ShapeDtypeStruct((T, D), BF16), jax.ShapeDtypeStruct((T, D), BF16),
                   jax.ShapeDtypeStruct((T, D), BF16)],
        scratch_shapes=[pltpu.VMEM((POOL_HALO, D), F32)],
        compiler_params=pltpu.CompilerParams(dimension_semantics=("arbitrary",)),
    )(h, w, scale, x, g1, gmlp, sc2, sh2)


def _pool_bwd(dy, pooled, w, scale, g1, q, name):
    T, D = dy.shape
    tr = min(T, ROW_TILE)
    nt = T // tr
    ltot = tr + POOL_HALO

    def body(dy_ref, pooled_ref, w_ref, sc_ref, g_ref, q_ref, dh_ref, dw_ref, dsc_ref, dg_ref, halo_ref):
        i = pl.program_id(0)

        @pl.when(i == 0)
        def _():
            halo_ref[...] = jnp.zeros_like(halo_ref)
            dw_ref[...] = jnp.zeros_like(dw_ref)
            dsc_ref[...] = g_ref[...] * q_ref[...]
            dg_ref[...] = sc_ref[...] * q_ref[...]

        t = ((nt - 1 - i) * tr + lax.broadcasted_iota(jnp.int32, (tr, 1), 0)).astype(F32)
        for gi, win in enumerate(POOL_WINDOWS):
            cols = slice(gi * POOL_GD, (gi + 1) * POOL_GD)
            dyb = (dy_ref[:, cols].astype(F32) * sc_ref[:, cols]).astype(BF16)
            dw_ref[gi] += lax.dot_general(pooled_ref[:, cols], dyb, (((0,), (0,)), ((), ())), preferred_element_type=F32)
            dpool = lax.dot_general(dyb, w_ref[gi], (((1,), (1,)), ((), ())), preferred_element_type=F32)
            qv = dpool / jnp.minimum(t + 1.0, float(win))
            val = jnp.concatenate([qv, halo_ref[:, cols]], axis=0)
            halo_ref[:, cols] = qv[:POOL_HALO, :]
            sh = 1
            while sh < win:
                val = val + pltpu.roll(val, ltot - sh, axis=0)
                sh *= 2
            dh_ref[:, cols] = val[:tr, :] - dpool

    rev = pl.BlockSpec((tr, D), lambda i: (nt - 1 - i, 0))
    return pl.pallas_call(
        body, name=name, grid=(nt,),
        in_specs=[rev, rev, pl.BlockSpec(w.shape, lambda i: (0, 0, 0)), _vec_spec(D), _vec_spec(D), _vec_spec(D)],
        out_specs=[rev, pl.BlockSpec(w.shape, lambda i: (0, 0, 0)), _vec_spec(D), _vec_spec(D)],
        out_shape=[jax.ShapeDtypeStruct((T, D), F32), jax.ShapeDtypeStruct(w.shape, F32),
                   jax.ShapeDtypeStruct((1, D), F32), jax.ShapeDtypeStruct((1, D), F32)],
        scratch_shapes=[pltpu.VMEM((POOL_HALO, D), F32)],
        compiler_params=pltpu.CompilerParams(dimension_semantics=("arbitrary",)),
    )(dy, pooled, w, scale, g1, q)


_INV_SQRT2 = 0.7071067811865476
_INV_SQRT2PI = 0.3989422804014327


def _gelu(v):
    return 0.5 * v * (1.0 + lax.erf(v * _INV_SQRT2))


def _gelu_grad(v):
    return 0.5 * (1.0 + lax.erf(v * _INV_SQRT2)) + v * jnp.exp(-0.5 * v * v) * _INV_SQRT2PI


def _sgu_ln(v, g, b):
    mu = jnp.mean(v, axis=-1, keepdims=True)
    xc = v - mu
    rstd = lax.rsqrt(jnp.mean(xc * xc, axis=-1, keepdims=True) + LN_EPS)
    xh = xc * rstd
    return xh, rstd, xh * g + b


def _tril_mask():
    return lax.broadcasted_iota(jnp.int32, (SGU_CHUNK, SGU_CHUNK), 0) >= lax.broadcasted_iota(jnp.int32, (SGU_CHUNK, SGU_CHUNK), 1)


SGU_TILE = 256


def _sgu_gate_fwd(zz, ln_g, ln_b, ws, bs_t, name):
    T = zz.shape[0]
    ts = min(T, SGU_TILE)

    def body(zz_ref, g_ref, b_ref, ws_ref, bs_ref, out_ref):
        z = _gelu(zz_ref[...])
        u = z[:, :SGU_W]
        _, _, vn = _sgu_ln(z[:, SGU_W:], g_ref[...], b_ref[...])
        vb = vn.astype(BF16)
        tril = _tril_mask()
        for hh in range(SGU_H):
            wm = jnp.where(tril, ws_ref[hh], 0.0).astype(BF16)
            bcol = bs_ref[:, hh:hh + 1]
            cs = slice(hh * SGU_HD, (hh + 1) * SGU_HD)
            for j in range(ts // SGU_CHUNK):
                rs = slice(j * SGU_CHUNK, (j + 1) * SGU_CHUNK)
                mixed = jnp.dot(wm, vb[rs, cs], preferred_element_type=F32) + bcol
                out_ref[rs, cs] = (u[rs, cs] * mixed).astype(BF16)

    return pl.pallas_call(
        body, name=name, grid=(T // ts,),
        in_specs=[_row_spec(ts, 2 * SGU_W), _vec_spec(SGU_W), _vec_spec(SGU_W),
                  pl.BlockSpec(ws.shape, lambda i: (0, 0, 0)), pl.BlockSpec(bs_t.shape, lambda i: (0, 0))],
        out_specs=_row_spec(ts, SGU_W),
        out_shape=jax.ShapeDtypeStruct((T, SGU_W), BF16),
        compiler_params=pltpu.CompilerParams(dimension_semantics=("parallel",)),
    )(zz, ln_g, ln_b, ws, bs_t)


def _sgu_gate_bwd(zz, dgated, ln_g, ln_b, ws, bs_t, name):
    T = zz.shape[0]
    ts = min(T, SGU_TILE)
    nt = T // ts

    def body(zz_ref, dg_ref, g_ref, b_ref, ws_ref, bs_ref, dzz_ref, dws_ref, dbs_ref, dlg_ref, dlb_ref, dlo_ref, dmx_ref):
        i = pl.program_id(0)

        @pl.when(i == 0)
        def _():
            dws_ref[...] = jnp.zeros_like(dws_ref)
            dmx_ref[...] = jnp.zeros_like(dmx_ref)
            dlg_ref[...] = jnp.zeros_like(dlg_ref)
            dlb_ref[...] = jnp.zeros_like(dlb_ref)

        zzv = zz_ref[...]
        z = _gelu(zzv)
        u = z[:, :SGU_W]
        xh, rstd, vn = _sgu_ln(z[:, SGU_W:], g_ref[...], b_ref[...])
        vb = vn.astype(BF16)
        dgv = dg_ref[...].astype(F32)
        tril = _tril_mask()
        for hh in range(SGU_H):
            wm = jnp.where(tril, ws_ref[hh], 0.0).astype(BF16)
            bcol = bs_ref[:, hh:hh + 1]
            cs = slice(hh * SGU_HD, (hh + 1) * SGU_HD)
            for j in range(ts // SGU_CHUNK):
                rs = slice(j * SGU_CHUNK, (j + 1) * SGU_CHUNK)
                mixed = jnp.dot(wm, vb[rs, cs], preferred_element_type=F32) + bcol
                dmixed = dgv[rs, cs] * u[rs, cs]
                dzz_ref[rs, cs] = (dgv[rs, cs] * mixed * _gelu_grad(zzv[rs, cs])).astype(BF16)
                dmb = dmixed.astype(BF16)
                dws_ref[hh] += lax.dot_general(dmb, vb[rs, cs], (((1,), (1,)), ((), ())), preferred_element_type=F32)
                dmx_ref[hh] += dmixed
                dlo_ref[rs, cs] = lax.dot_general(wm, dmb, (((0,), (0,)), ((), ())), preferred_element_type=F32)
        dlo = dlo_ref[...]
        dlg_ref[...] += _colsum(dlo * xh)
        dlb_ref[...] += _colsum(dlo)
        dxh = dlo * g_ref[...]
        dv = rstd * (dxh - jnp.mean(dxh, axis=-1, keepdims=True) - xh * jnp.mean(dxh * xh, axis=-1, keepdims=True))
        dzz_ref[:, SGU_W:] = (dv * _gelu_grad(zzv[:, SGU_W:])).astype(BF16)

        @pl.when(i == nt - 1)
        def _():
            tril_f = tril.astype(F32)
            for hh in range(SGU_H):
                dws_ref[hh] = dws_ref[hh] * tril_f
                dbs_ref[hh] = jnp.broadcast_to(jnp.sum(dmx_ref[hh], axis=-1, keepdims=True), (SGU_CHUNK, SGU_HD))

    full3 = pl.BlockSpec(ws.shape, lambda i: (0, 0, 0))
    return pl.pallas_call(
        body, name=name, grid=(nt,),
        in_specs=[_row_spec(ts, 2 * SGU_W), _row_spec(ts, SGU_W), _vec_spec(SGU_W), _vec_spec(SGU_W), full3,
                  pl.BlockSpec(bs_t.shape, lambda i: (0, 0))],
        out_specs=[_row_spec(ts, 2 * SGU_W), full3, full3, _vec_spec(SGU_W), _vec_spec(SGU_W)],
        out_shape=[jax.ShapeDtypeStruct((T, 2 * SGU_W), BF16), jax.ShapeDtypeStruct(ws.shape, F32), jax.ShapeDtypeStruct(ws.shape, F32),
                   jax.ShapeDtypeStruct((1, SGU_W), F32), jax.ShapeDtypeStruct((1, SGU_W), F32)],
        scratch_shapes=[pltpu.VMEM((ts, SGU_W), F32), pltpu.VMEM(ws.shape, F32)],
        compiler_params=pltpu.CompilerParams(dimension_semantics=("arbitrary",)),
    )(zz, dgated, ln_g, ln_b, ws, bs_t)


def _rope_fwd(blk, cc, sa, sb):
    return blk * cc + pltpu.roll(blk, 96, axis=1) * sa + pltpu.roll(blk, 32, axis=1) * sb


def _rope_bwd(d, cc, sa, sb):
    return d * cc + pltpu.roll(d * sa, 32, axis=1) + pltpu.roll(d * sb, 96, axis=1)


def _rms(v, g):
    r = lax.rsqrt(jnp.mean(v * v, axis=-1, keepdims=True) + RMS_EPS)
    vn = v * r
    return vn, r, vn * g


def _rms_bwd(dy, vn, r, g):
    dvn = dy * g
    return r * (dvn - vn * jnp.mean(dvn * vn, axis=-1, keepdims=True))


MLA_TILE = 256
_KV0 = MLA_QL
_KR0 = MLA_QL + MLA_KVL


def _mla_lat_fwd(lat, qg, kvg, cc, sa, sb, name):
    T = lat.shape[0]
    tr = min(T, ROW_TILE)

    def body(lat_ref, qg_ref, kvg_ref, cc_ref, sa_ref, sb_ref, cq_ref, ckv_ref, kr_ref):
        lv = lat_ref[...]
        cq_ref[...] = _rms(lv[:, :_KV0], qg_ref[...])[2].astype(BF16)
        ckv_ref[...] = _rms(lv[:, _KV0:_KR0], kvg_ref[...])[2].astype(BF16)
        kr_ref[...] = _rope_fwd(lv[:, _KR0:], cc_ref[...], sa_ref[...], sb_ref[...])

    return pl.pallas_call(
        body, name=name, grid=(T // tr,),
        in_specs=[_row_spec(tr, MLA_LATP), _vec_spec(MLA_QL), _vec_spec(MLA_KVL), _row_spec(tr, 128), _row_spec(tr, 128), _row_spec(tr, 128)],
        out_specs=[_row_spec(tr, MLA_QL), _row_spec(tr, MLA_KVL), _row_spec(tr, 128)],
        out_shape=[jax.ShapeDtypeStruct((T, MLA_QL), BF16), jax.ShapeDtypeStruct((T, MLA_KVL), BF16), jax.ShapeDtypeStruct((T, 128), F32)],
        compiler_params=pltpu.CompilerParams(dimension_semantics=("parallel",)),
    )(lat, qg, kvg, cc, sa, sb)


def _mla_lat_bwd(lat, dcqn, dckvn, dkrot, qg, kvg, cc, sa, sb, name):
    T = lat.shape[0]
    tr = min(T, ROW_TILE)

    def body(lat_ref, dcq_ref, dckv_ref, dkr_ref, qg_ref, kvg_ref, cc_ref, sa_ref, sb_ref, dlat_ref, dqg_ref, dkvg_ref):
        @pl.when(pl.program_id(0) == 0)
        def _():
            dqg_ref[...] = jnp.zeros_like(dqg_ref)
            dkvg_ref[...] = jnp.zeros_like(dkvg_ref)

        lv = lat_ref[...]
        qn, qr, _ = _rms(lv[:, :_KV0], qg_ref[...])
        kn, kr, _ = _rms(lv[:, _KV0:_KR0], kvg_ref[...])
        dcq = dcq_ref[...]
        dckv = dckv_ref[...]
        dqg_ref[...] += _colsum(dcq * qn)
        dkvg_ref[...] += _colsum(dckv * kn)
        dlat_ref[:, :_KV0] = _rms_bwd(dcq, qn, qr, qg_ref[...]).astype(BF16)
        dlat_ref[:, _KV0:_KR0] = _rms_bwd(dckv, kn, kr, kvg_ref[...]).astype(BF16)
        dlat_ref[:, _KR0:] = _rope_bwd(dkr_ref[...], cc_ref[...], sa_ref[...], sb_ref[...]).astype(BF16)

    return pl.pallas_call(
        body, name=name, grid=(T // tr,),
        in_specs=[_row_spec(tr, MLA_LATP), _row_spec(tr, MLA_QL), _row_spec(tr, MLA_KVL), _row_spec(tr, 128),
                  _vec_spec(MLA_QL), _vec_spec(MLA_KVL), _row_spec(tr, 128), _row_spec(tr, 128), _row_spec(tr, 128)],
        out_specs=[_row_spec(tr, MLA_LATP), _vec_spec(MLA_QL), _vec_spec(MLA_KVL)],
        out_shape=[jax.ShapeDtypeStruct((T, MLA_LATP), BF16), jax.ShapeDtypeStruct((1, MLA_QL), F32), jax.ShapeDtypeStruct((1, MLA_KVL), F32)],
        compiler_params=pltpu.CompilerParams(dimension_semantics=("arbitrary",)),
    )(lat, dcqn, dckvn, dkrot, qg, kvg, cc, sa, sb)


def _epi_q_rope(acc, cc, sa, sb):
    out = []
    for hh in range(acc.shape[1] // MLA_HP):
        a, m, b = hh * MLA_HP, hh * MLA_HP + MLA_NOPE, (hh + 1) * MLA_HP
        out += [acc[:, a:m] * SM_SCALE, _rope_fwd(acc[:, m:b], cc, sa, sb) * SM_SCALE]
    return (jnp.concatenate(out, axis=1),)


def _mla_ukv(ckvn, w_ukv, krot, name):
    T = ckvn.shape[0]
    tr = min(T, MLA_TILE)
    hg = ATT_HG
    gw = hg * MLA_HP

    def body(a_ref, w_ref, kr_ref, ko_ref, kt_ref, vo_ref, vt_ref):
        acc = jnp.dot(a_ref[...], w_ref[...], preferred_element_type=F32)
        kr = kr_ref[...]
        krb, krt = kr.astype(BF16), kr.T.astype(BF16)
        for hh in range(hg):
            a, m, b = hh * MLA_HP, hh * MLA_HP + MLA_NOPE, (hh + 1) * MLA_HP
            kn, vh = acc[:, a:m], acc[:, m:b]
            ko_ref[:, a:m] = kn.astype(BF16)
            ko_ref[:, m:b] = krb
            kt_ref[a:m, :] = kn.T.astype(BF16)
            kt_ref[m:b, :] = krt
            vo_ref[:, hh * MLA_V:(hh + 1) * MLA_V] = vh.astype(BF16)
            vt_ref[hh] = vh.T.astype(BF16)

    tk = min(T, ATT_TILE)
    per = tk // tr
    HW = MLA_H * MLA_HP
    return pl.pallas_call(
        body, name=name, grid=(T // tr, MLA_H // hg),
        in_specs=[pl.BlockSpec((tr, MLA_KVL), lambda i, g: (i, 0)), pl.BlockSpec((MLA_KVL, gw), lambda i, g: (0, g)),
                  pl.BlockSpec((tr, 128), lambda i, g: (i, 0))],
        out_specs=[pl.BlockSpec((tr, gw), lambda i, g: (i, g)), pl.BlockSpec((gw, tr), lambda i, g: (g, i)),
                   pl.BlockSpec((tr, hg * MLA_V), lambda i, g: (i, g)),
                   pl.BlockSpec((hg, None, MLA_V, tr), lambda i, g: (g, i // per, 0, i % per))],
        out_shape=[jax.ShapeDtypeStruct((T, HW), BF16), jax.ShapeDtypeStruct((HW, T), BF16), jax.ShapeDtypeStruct((T, MLA_H * MLA_V), BF16),
                   jax.ShapeDtypeStruct((MLA_H, T // tk, MLA_V, tk), BF16)],
        compiler_params=pltpu.CompilerParams(dimension_semantics=("parallel", "parallel")),
    )(ckvn, w_ukv, krot)


ATT_HG = 4


def _mla_prep_bwd(dqt, dk, dv, cc, sa, sb, name):
    _, nq, _, tq = dqt.shape
    T = nq * tq
    gw = ATT_HG * MLA_HP

    def body(dq_ref, dk_ref, dv_ref, cc_ref, sa_ref, sb_ref, dqp_ref, dkv_ref, dkr_ref):
        @pl.when(pl.program_id(1) == 0)
        def _():
            dkr_ref[...] = jnp.zeros_like(dkr_ref)

        cc, sa, sb = cc_ref[...], sa_ref[...], sb_ref[...]
        acc = jnp.zeros((tq, 128), F32)
        for hh in range(ATT_HG):
            a, m, b = hh * MLA_HP, hh * MLA_HP + MLA_NOPE, (hh + 1) * MLA_HP
            dqh = dq_ref[hh].astype(F32).T * SM_SCALE
            dqp_ref[:, a:m] = dqh[:, :MLA_NOPE].astype(BF16)
            dqp_ref[:, m:b] = _rope_bwd(dqh[:, MLA_NOPE:], cc, sa, sb).astype(BF16)
            dkv_ref[:, a:m] = dk_ref[:, a:m]
            dkv_ref[:, m:b] = dv_ref[:, hh * MLA_V:(hh + 1) * MLA_V]
            acc = acc + dk_ref[:, m:b].astype(F32)
        dkr_ref[...] += acc

    tab = pl.BlockSpec((tq, 128), lambda i, g: (i, 0))
    return pl.pallas_call(
        body, name=name, grid=(nq, MLA_H // ATT_HG),
        in_specs=[pl.BlockSpec((ATT_HG, None, MLA_HP, tq), lambda i, g: (g, i, 0, 0)), pl.BlockSpec((tq, gw), lambda i, g: (i, g)),
                  pl.BlockSpec((tq, ATT_HG * MLA_V), lambda i, g: (i, g)), tab, tab, tab],
        out_specs=[pl.BlockSpec((tq, gw), lambda i, g: (i, g)), pl.BlockSpec((tq, gw), lambda i, g: (i, g)), tab],
        out_shape=[jax.ShapeDtypeStruct((T, MLA_H * MLA_HP), BF16), jax.ShapeDtypeStruct((T, MLA_H * MLA_HP), BF16), jax.ShapeDtypeStruct((T, 128), F32)],
        compiler_params=pltpu.CompilerParams(dimension_semantics=("parallel", "arbitrary")),
    )(dqt, dk, dv, cc, sa, sb)


_NT = (((1,), (1,)), ((), ()))


def _as_row(col, n):
    return jnp.broadcast_to(col, (n, 128)).T[0:1, :]


def _attn_fwd(q, k, vt, name):
    T = q.shape[0]
    tq = tk = min(T, ATT_TILE)
    nq = T // tq
    hg = ATT_FWD_HEADS

    def body(q_ref, k_ref, vt_ref, o_ref, lse_ref, m_ref, l_ref, acc_ref):
        i = pl.program_id(1)
        m_ref[...] = jnp.full_like(m_ref, NEG_INF)
        l_ref[...] = jnp.zeros_like(l_ref)
        acc_ref[...] = jnp.zeros_like(acc_ref)

        def step(j, diag):
            off = pl.multiple_of(j * tk, tk)
            sts = [lax.dot_general(k_ref[pl.ds(off, tk), hh * MLA_HP:(hh + 1) * MLA_HP], q_ref[:, hh * MLA_HP:(hh + 1) * MLA_HP], _NT,
                                   preferred_element_type=F32) for hh in range(hg)]
            for hh in range(hg):
                st = sts[hh]
                if diag:
                    st = jnp.where(lax.broadcasted_iota(jnp.int32, (tk, tq), 0) <= lax.broadcasted_iota(jnp.int32, (tk, tq), 1), st, NEG_INF)
                m_prev = m_ref[hh]
                m_new = jnp.maximum(m_prev, jnp.max(st, axis=0, keepdims=True))
                alpha = jnp.exp(m_prev - m_new)
                pt = jnp.exp(st - m_new)
                l_ref[hh] = alpha * l_ref[hh] + jnp.sum(pt, axis=0, keepdims=True)
                acc_ref[hh] = alpha * acc_ref[hh] + jnp.dot(vt_ref[hh, j], pt.astype(BF16), preferred_element_type=F32)
                m_ref[hh] = m_new

        def loop_body(j, carry):
            step(j, False)
            return carry

        lax.fori_loop(0, i, loop_body, 0)
        step(i, True)
        for hh in range(hg):
            o_ref[:, hh * MLA_V:(hh + 1) * MLA_V] = (acc_ref[hh] / l_ref[hh]).T.astype(BF16)
            lse_ref[hh] = m_ref[hh] + jnp.log(l_ref[hh])

    return pl.pallas_call(
        body, name=name, grid=(MLA_H // hg, nq),
        in_specs=[pl.BlockSpec((tq, hg * MLA_HP), lambda h, i: (i, h)), pl.BlockSpec((T, hg * MLA_HP), lambda h, i: (0, h)),
                  pl.BlockSpec((hg, nq, MLA_V, tk), lambda h, i: (h, 0, 0, 0))],
        out_specs=[pl.BlockSpec((tq, hg * MLA_V), lambda h, i: (i, h)), pl.BlockSpec((hg, None, 1, tq), lambda h, i: (h, i, 0, 0))],
        out_shape=[jax.ShapeDtypeStruct((T, MLA_H * MLA_V), BF16), jax.ShapeDtypeStruct((MLA_H, nq, 1, tq), F32)],
        scratch_shapes=[pltpu.VMEM((hg, 1, tq), F32), pltpu.VMEM((hg, 1, tq), F32), pltpu.VMEM((hg, MLA_V, tq), F32)],
        compiler_params=pltpu.CompilerParams(dimension_semantics=("parallel", "arbitrary")),
    )(q, k, vt)


def _attn_delta(do, o, name):
    T = do.shape[0]
    tq = min(T, ATT_TILE)

    def body(do_ref, o_ref, d_ref):
        for hh in range(MLA_H):
            cs = slice(hh * MLA_V, (hh + 1) * MLA_V)
            s = jnp.sum(do_ref[:, cs].astype(F32) * o_ref[:, cs].astype(F32), axis=-1, keepdims=True)
            d_ref[hh] = _as_row(s, tq)

    return pl.pallas_call(
        body, name=name, grid=(T // tq,),
        in_specs=[_row_spec(tq, MLA_H * MLA_V), _row_spec(tq, MLA_H * MLA_V)],
        out_specs=pl.BlockSpec((MLA_H, None, 1, tq), lambda i: (0, i, 0, 0)),
        out_shape=jax.ShapeDtypeStruct((MLA_H, T // tq, 1, tq), F32),
        compiler_params=pltpu.CompilerParams(dimension_semantics=("parallel",)),
    )(do, o)


def _attn_bwd(q, k, kt, v, do, lse, delta, name):
    T = q.shape[0]
    tq = tk = min(T, ATT_TILE)
    nq = nk = T // tq
    tsd = min(tq, ATT_SUB)
    hg = ATT_BWD_HEADS

    def body(q_ref, k_ref, kt_ref, v_ref, do_ref, lse_ref, dl_ref, dqt_ref, dk_ref, dv_ref, dq_acc, dk_acc, dv_acc):
        j = pl.program_id(1)

        @pl.when(j == 0)
        def _():
            dq_acc[...] = jnp.zeros_like(dq_acc)

        dk_acc[...] = jnp.zeros_like(dk_acc)
        dv_acc[...] = jnp.zeros_like(dv_acc)

        def step(i, diag):
            off = pl.multiple_of(i * tq, tq)
            ts, nsub = (tsd, tq // tsd) if diag else (tq, 1)
            for u in range(nsub):
                cols = slice(u * ts, (u + 1) * ts)
                nk_u = (u + 1) * ts if diag else tk
                rows = pl.ds(off + u * ts, ts)
                pre = []
                for hh in range(hg):
                    hq, hv = slice(hh * MLA_HP, (hh + 1) * MLA_HP), slice(hh * MLA_V, (hh + 1) * MLA_V)
                    qi, doi = q_ref[rows, hq], do_ref[rows, hv]
                    st = lax.dot_general(k_ref[:nk_u, hq], qi, _NT, preferred_element_type=F32)
                    dpt = lax.dot_general(v_ref[:nk_u, hv], doi, _NT, preferred_element_type=F32)
                    pre.append((qi, doi, st, dpt))
                for hh in range(hg):
                    hq, hv = slice(hh * MLA_HP, (hh + 1) * MLA_HP), slice(hh * MLA_V, (hh + 1) * MLA_V)
                    qi, doi, st, dpt = pre[hh]
                    if diag:
                        qcol = u * ts + lax.broadcasted_iota(jnp.int32, (nk_u, ts), 1)
                        st = jnp.where(lax.broadcasted_iota(jnp.int32, (nk_u, ts), 0) <= qcol, st, NEG_INF)
                    pt = jnp.exp(st - lse_ref[hh, i][:, cols])
                    dv_acc[:nk_u, hv] += jnp.dot(pt.astype(BF16), doi, preferred_element_type=F32)
                    dsb = (pt * (dpt - dl_ref[hh, i][:, cols])).astype(BF16)
                    dk_acc[:nk_u, hq] += jnp.dot(dsb, qi, preferred_element_type=F32)
                    dq_acc[hh, i, :, cols] += jnp.dot(kt_ref[hq, :nk_u], dsb, preferred_element_type=F32)

        def loop_body(i, carry):
            step(i, False)
            return carry

        step(j, True)
        lax.fori_loop(j + 1, nq, loop_body, 0)
        dk_ref[...] = dk_acc[...].astype(BF16)
        dv_ref[...] = dv_acc[...].astype(BF16)

        @pl.when(j == nk - 1)
        def _():
            dqt_ref[...] = dq_acc[...].astype(BF16)

    stat = pl.BlockSpec((hg, nq, 1, tq), lambda h, j: (h, 0, 0, 0))
    return pl.pallas_call(
        body, name=name, grid=(MLA_H // hg, nk),
        in_specs=[pl.BlockSpec((T, hg * MLA_HP), lambda h, j: (0, h)), pl.BlockSpec((tk, hg * MLA_HP), lambda h, j: (j, h)),
                  pl.BlockSpec((hg * MLA_HP, tk), lambda h, j: (h, j)), pl.BlockSpec((tk, hg * MLA_V), lambda h, j: (j, h)),
                  pl.BlockSpec((T, hg * MLA_V), lambda h, j: (0, h)), stat, stat],
        out_specs=[pl.BlockSpec((hg, nq, MLA_HP, tq), lambda h, j: (h, 0, 0, 0)), pl.BlockSpec((tk, hg * MLA_HP), lambda h, j: (j, h)),
                   pl.BlockSpec((tk, hg * MLA_V), lambda h, j: (j, h))],
        out_shape=[jax.ShapeDtypeStruct((MLA_H, nq, MLA_HP, tq), BF16), jax.ShapeDtypeStruct((T, MLA_H * MLA_HP), BF16),
                   jax.ShapeDtypeStruct((T, MLA_H * MLA_V), BF16)],
        scratch_shapes=[pltpu.VMEM((hg, nq, MLA_HP, tq), F32), pltpu.VMEM((tk, hg * MLA_HP), F32), pltpu.VMEM((tk, hg * MLA_V), F32)],
        compiler_params=pltpu.CompilerParams(dimension_semantics=("parallel", "arbitrary")),
    )(q, k, kt, v, do, lse, delta)


ADA_TN = 512


def _silu(v):
    return v * (1.0 / (1.0 + jnp.exp(-v)))


def _ada_fwd(c_all, ada_w, ada_b_loc, name):
    L, D, Nc = ada_w.shape
    B = c_all.shape[0]

    def body(c_ref, w_ref, b_ref, o_ref):
        ca = _silu(c_ref[...]).astype(BF16)
        o_ref[...] = jnp.dot(ca, w_ref[...].astype(BF16), preferred_element_type=F32) + b_ref[...]

    return pl.pallas_call(
        body, name=name, grid=(L, Nc // ADA_TN),
        in_specs=[pl.BlockSpec((B, D), lambda l, n: (0, 0)), pl.BlockSpec((None, D, ADA_TN), lambda l, n: (l, 0, n)),
                  pl.BlockSpec((None, 1, ADA_TN), lambda l, n: (l, 0, n))],
        out_specs=pl.BlockSpec((None, B, ADA_TN), lambda l, n: (l, 0, n)),
        out_shape=jax.ShapeDtypeStruct((L, B, Nc), F32),
        compiler_params=pltpu.CompilerParams(dimension_semantics=("parallel", "parallel")),
    )(c_all, ada_w, ada_b_loc)


def _ada_bwd(c_all_t, dmod_loc, name):
    D, B = c_all_t.shape
    L, _, Nc = dmod_loc.shape

    def body(c_ref, d_ref, o_ref):
        ca = _silu(c_ref[...])
        dv = d_ref[...]
        acc = ca[:, 0:1] * dv[0:1, :]
        for b in range(1, B):
            acc = acc + ca[:, b:b + 1] * dv[b:b + 1, :]
        o_ref[...] = acc

    return pl.pallas_call(
        body, name=name, grid=(L, Nc // ADA_TN),
        in_specs=[pl.BlockSpec((D, B), lambda l, n: (0, 0)), pl.BlockSpec((None, B, ADA_TN), lambda l, n: (l, 0, n))],
        out_specs=pl.BlockSpec((None, D, ADA_TN), lambda l, n: (l, 0, n)),
        out_shape=jax.ShapeDtypeStruct((L, D, Nc), F32),
        compiler_params=pltpu.CompilerParams(dimension_semantics=("parallel", "parallel")),
    )(c_all_t, dmod_loc)


def _sum_lead(parts, name, out_dtype=F32):
    R, C = parts[0].shape[1:]
    n_tot = sum(p.shape[0] for p in parts)
    tr = R
    for cand in (512, 256, 128, 64, 32, 16):
        if R % cand == 0 and cand * C * 4 * n_tot <= (8 << 20):
            tr = cand
            break

    def body(*refs):
        o_ref = refs[-1]
        acc = None
        for r in refs[:-1]:
            for s in range(r.shape[0]):
                acc = r[s].astype(F32) if acc is None else acc + r[s].astype(F32)
        o_ref[...] = acc.astype(o_ref.dtype)

    return pl.pallas_call(
        body, name=name, grid=(R // tr,),
        in_specs=[pl.BlockSpec((p.shape[0], tr, C), lambda i: (0, i, 0)) for p in parts],
        out_specs=pl.BlockSpec((tr, C), lambda i: (i, 0)),
        out_shape=jax.ShapeDtypeStruct((R, C), out_dtype),
        compiler_params=pltpu.CompilerParams(dimension_semantics=("parallel",)),
    )(*parts)


_ADAM_C1 = 1.0 - ADAM_B1 ** ADAM_STEP
_ADAM_C2 = 1.0 - ADAM_B2 ** ADAM_STEP


def _adamw(w, g, m, v, name):
    shape = w.shape
    C = shape[-1]
    R = math.prod(shape[:-1]) if len(shape) > 1 else 1
    w2, g2, m2, v2 = (a.reshape(R, C) for a in (w, g, m, v))
    tr = R
    for cand in (1024, 512, 256, 128, 64, 32, 16, 8):
        if R % cand == 0 and cand * C * 4 <= (1 << 20):
            tr = cand
            break

    def body(w_ref, g_ref, m_ref, v_ref, d_ref, nm_ref, nv_ref):
        gv = g_ref[...]
        mn = ADAM_B1 * m_ref[...] + (1.0 - ADAM_B1) * gv
        vn = ADAM_B2 * v_ref[...] + (1.0 - ADAM_B2) * (gv * gv)
        nm_ref[...] = mn
        nv_ref[...] = vn
        m_hat = mn / _ADAM_C1
        v_hat = vn / _ADAM_C2
        d_ref[...] = -ADAM_LR * (m_hat / (jnp.sqrt(v_hat) + ADAM_EPS) + ADAM_WD * w_ref[...])

    spec = pl.BlockSpec((tr, C), lambda i: (i, 0))
    outs = pl.pallas_call(
        body, name=name, grid=(R // tr,),
        in_specs=[spec] * 4, out_specs=[spec] * 3,
        out_shape=[jax.ShapeDtypeStruct((R, C), F32)] * 3,
        compiler_params=pltpu.CompilerParams(dimension_semantics=("parallel",)),
    )(w2, g2, m2, v2)
    return tuple(o.reshape(shape) for o in outs)


def _row_tile(rows, cols, itemsize, budget):
    for cand in (1024, 512, 256, 128, 64, 32, 16):
        if rows % cand == 0 and cand * cols * itemsize <= budget:
            return cand
    return rows


def _sum_sel(sel, stacked, others, name, out_dtype):
    R, C = stacked.shape[1:]
    n_tot = 1 + sum(o.shape[0] for o in others)
    tr = _row_tile(R, C, 4 * n_tot, 8 << 20)

    def body(sel_ref, s_ref, *refs):
        o_ref = refs[-1]
        acc = s_ref[...].astype(F32)
        for r in refs[:-1]:
            for s in range(r.shape[0]):
                acc = acc + r[s].astype(F32)
        o_ref[...] = acc.astype(o_ref.dtype)

    return pl.pallas_call(
        body, name=name,
        grid_spec=pltpu.PrefetchScalarGridSpec(
            num_scalar_prefetch=1, grid=(R // tr,),
            in_specs=[pl.BlockSpec((None, tr, C), lambda i, s: (s[0], i, 0))] + [pl.BlockSpec((o.shape[0], tr, C), lambda i, s: (0, i, 0)) for o in others],
            out_specs=pl.BlockSpec((tr, C), lambda i, s: (i, 0))),
        out_shape=jax.ShapeDtypeStruct((R, C), out_dtype),
        compiler_params=pltpu.CompilerParams(dimension_semantics=("parallel",)),
    )(sel, stacked, *others)


def _adamw_piece(cidx, w2, m2, v2, mine, got, bufs, row0, name):
    hr, C = mine.shape
    tr = _row_tile(math.gcd(hr, row0) if row0 else hr, C, 4, 1 << 20)
    nt = hr // tr

    def body(c_ref, w_ref, m_ref, v_ref, a_ref, b_ref, _g, _d, _nm, _nv, g_ref, d_ref, nm_ref, nv_ref):
        gv = jnp.where(pl.program_id(0) == c_ref[0], a_ref[...], b_ref[...])
        mn = ADAM_B1 * m_ref[...] + (1.0 - ADAM_B1) * gv
        vn = ADAM_B2 * v_ref[...] + (1.0 - ADAM_B2) * (gv * gv)
        g_ref[...] = gv
        nm_ref[...] = mn
        nv_ref[...] = vn
        d_ref[...] = -ADAM_LR * ((mn / _ADAM_C1) / (jnp.sqrt(vn / _ADAM_C2) + ADAM_EPS) + ADAM_WD * w_ref[...])

    rows = pl.BlockSpec((tr, C), lambda hf, t, c: (row0 // tr + hf * nt + t, 0))
    half = pl.BlockSpec((tr, C), lambda hf, t, c: (t, 0))
    return pl.pallas_call(
        body, name=name,
        grid_spec=pltpu.PrefetchScalarGridSpec(num_scalar_prefetch=1, grid=(2, nt), in_specs=[rows] * 3 + [half] * 2 + [_ANY_SPEC] * 4,
                                               out_specs=[rows] * 4),
        out_shape=[jax.ShapeDtypeStruct(w2.shape, F32)] * 4,
        input_output_aliases={6 + n: n for n in range(4)},
        compiler_params=pltpu.CompilerParams(dimension_semantics=("parallel", "parallel")),
    )(cidx, w2, m2, v2, mine, got, *bufs)


_VMEM_SPEC = pl.BlockSpec(memory_space=pltpu.VMEM)
_HBM_SPEC = pl.BlockSpec(memory_space=pltpu.HBM)


def _flip(v, bit):
    return (1 - v) if bit else v


def _allgather8(v, name):
    def body(v_ref, out_ref, send_sems, recv_sems, local_sem):
        x, y, c = _idx()
        me = 4 * x + 2 * y + c
        mine = pltpu.make_async_copy(v_ref, out_ref.at[me], local_sem)
        mine.start()
        sends = []
        for k in range(1, N_DEV):
            peer = (_flip(x, k & 4), _flip(y, k & 2), _flip(c, k & 1))
            cp = pltpu.make_async_remote_copy(src_ref=v_ref, dst_ref=out_ref.at[me], send_sem=send_sems.at[k - 1], recv_sem=recv_sems.at[k - 1],
                                              device_id=peer, device_id_type=MESH)
            cp.start()
            sends.append(cp)
        for k in range(1, N_DEV):
            px, py, pc = _flip(x, k & 4), _flip(y, k & 2), _flip(c, k & 1)
            src = 4 * px + 2 * py + pc
            pltpu.make_async_remote_copy(src_ref=v_ref, dst_ref=out_ref.at[src], send_sem=send_sems.at[k - 1], recv_sem=recv_sems.at[k - 1],
                                         device_id=(px, py, pc), device_id_type=MESH).wait_recv()
        for cp in sends:
            cp.wait_send()
        mine.wait()

    return pl.pallas_call(
        body, name=name,
        out_shape=jax.ShapeDtypeStruct((N_DEV, *v.shape), v.dtype),
        in_specs=[_VMEM_SPEC], out_specs=_VMEM_SPEC,
        scratch_shapes=[pltpu.SemaphoreType.DMA((N_DEV - 1,)), pltpu.SemaphoreType.DMA((N_DEV - 1,)), pltpu.SemaphoreType.DMA],
    )(v)


def _mod_exchange(modp, name):
    _, L, Nc = modp.shape

    def body(p_ref, out_ref, send_sems, recv_sems, local_sem):
        x, y, c = _idx()
        me, chip = 4 * x + 2 * y + c, 2 * x + y
        mine = pltpu.make_async_copy(p_ref.at[me], out_ref.at[chip], local_sem)
        mine.start()
        sends = []
        for k in range(1, N_CHIPS):
            px, py = _flip(x, k & 2), _flip(y, k & 1)
            cp = pltpu.make_async_remote_copy(src_ref=p_ref.at[4 * px + 2 * py + c], dst_ref=out_ref.at[chip],
                                              send_sem=send_sems.at[k - 1], recv_sem=recv_sems.at[k - 1], device_id=(px, py, c), device_id_type=MESH)
            cp.start()
            sends.append(cp)
        for k in range(1, N_CHIPS):
            px, py = _flip(x, k & 2), _flip(y, k & 1)
            pltpu.make_async_remote_copy(src_ref=p_ref.at[me], dst_ref=out_ref.at[2 * px + py], send_sem=send_sems.at[k - 1],
                                         recv_sem=recv_sems.at[k - 1], device_id=(px, py, c), device_id_type=MESH).wait_recv()
        for cp in sends:
            cp.wait_send()
        mine.wait()

    return pl.pallas_call(
        body, name=name,
        out_shape=jax.ShapeDtypeStruct((N_CHIPS, L, Nc), modp.dtype),
        in_specs=[_VMEM_SPEC], out_specs=_VMEM_SPEC,
        scratch_shapes=[pltpu.SemaphoreType.DMA((N_CHIPS - 1,)), pltpu.SemaphoreType.DMA((N_CHIPS - 1,)), pltpu.SemaphoreType.DMA],
    )(modp)


_SEM_SPEC = pl.BlockSpec(memory_space=pltpu.SEMAPHORE)
_ANY_SPEC = pl.BlockSpec(memory_space=pl.ANY)
_EFFECT = pltpu.SideEffectType.DATAFLOW_SIDE_EFFECTING


def _hbm(a):
    return pltpu.with_memory_space_constraint(a, pltpu.HBM)


def _xchip_copies(mode, srcs, lands, send_sems, recv_sems, waiting):
    x, y, c = _idx()
    chip = 2 * x + y
    out = []
    for a in range(len(srcs)):
        for k in range(1, _n_peers(mode) + 1):
            if mode == "all8":
                px, py, pc = _flip(x, k & 4), _flip(y, k & 2), _flip(c, k & 1)
                src, dst, mine = srcs[a], lands[a].at[4 * x + 2 * y + c], lands[a].at[4 * px + 2 * py + pc]
            else:
                px, py, pc = _flip(x, k & 2), _flip(y, k & 1), c
                peer = 2 * px + py
                if mode == "gather":
                    src, dst, mine = srcs[a].at[c], lands[a].at[chip, c], lands[a].at[peer, c]
                else:
                    src, dst, mine = srcs[a].at[peer], lands[a].at[k - 1], lands[a].at[k - 1]
            q = a * _n_peers(mode) + k - 1
            out.append(pltpu.make_async_remote_copy(src_ref=src, dst_ref=mine if waiting else dst, send_sem=send_sems[q], recv_sem=recv_sems[q],
                                                    device_id=(px, py, pc), device_id_type=MESH))
    return out


def _n_peers(mode):
    return N_DEV - 1 if mode == "all8" else N_CHIPS - 1


def _xchip_start(mode, srcs, land_shapes, dep, name):
    n = len(srcs)
    ns = n * _n_peers(mode)

    def body(*refs):
        src_refs, land_refs = refs[:n], refs[n:2 * n]
        outs = refs[2 * n + 1:]
        for cp in _xchip_copies(mode, src_refs, land_refs, outs[:ns], outs[ns:2 * ns], waiting=False):
            cp.start()
        outs[-1][...] = jnp.zeros_like(outs[-1])

    lands = [_hbm(lax.empty(s.shape, s.dtype)) for s in land_shapes]
    outs = pl.pallas_call(
        body, name=name,
        out_shape=(*[pltpu.SemaphoreType.DMA(())] * (2 * ns), *[pltpu.HBM(s.shape, s.dtype) for s in srcs],
                   *[pltpu.HBM(s.shape, s.dtype) for s in land_shapes], jax.ShapeDtypeStruct((8, 128), F32)),
        in_specs=[_HBM_SPEC] * (2 * n) + [_ANY_SPEC],
        out_specs=(*[_SEM_SPEC] * (2 * ns), *[_HBM_SPEC] * (2 * n), _VMEM_SPEC),
        input_output_aliases={i: 2 * ns + i for i in range(2 * n)},
        compiler_params=pltpu.CompilerParams(has_side_effects=_EFFECT),
    )(*[_hbm(s) for s in srcs], *lands, dep)
    return list(outs[:ns]), list(outs[ns:2 * ns]), list(outs[2 * ns:2 * ns + n]), list(outs[2 * ns + n:2 * ns + 2 * n]), outs[-1]


def _xchip_wait(mode, send_sems, recv_sems, srcs, lands, after, name):
    n = len(srcs)
    ns = n * _n_peers(mode)

    def body(*refs):
        src_refs, land_refs = refs[:n], refs[n:2 * n]
        sems = refs[2 * n:2 * n + 2 * ns]
        for cp in _xchip_copies(mode, src_refs, land_refs, sems[:ns], sems[ns:], waiting=True):
            cp.wait_send()
            cp.wait_recv()

    outs = pl.pallas_call(
        body, name=name,
        out_shape=(*[pltpu.HBM(s.shape, s.dtype) for s in srcs], *[pltpu.HBM(s.shape, s.dtype) for s in lands]),
        in_specs=[_HBM_SPEC] * (2 * n) + [_SEM_SPEC] * (2 * ns) + [_ANY_SPEC] * len(after),
        out_specs=tuple([_HBM_SPEC] * (2 * n)),
        input_output_aliases={i: i for i in range(2 * n)},
        compiler_params=pltpu.CompilerParams(has_side_effects=_EFFECT),
    )(*srcs, *lands, *send_sems, *recv_sems, *after)
    return list(outs[:n]), list(outs[n:])


def _sibling_fwd(lands, name):
    n = len(lands)

    def body(*refs):
        outs = refs[n:2 * n]
        send_sems, recv_sems = refs[2 * n:]
        x, y, c = _idx()
        sib = (x, y, 1 - c)
        sends = []
        for a in range(n):
            for k in range(1, N_CHIPS):
                src = 2 * _flip(x, k & 2) + _flip(y, k & 1)
                cp = pltpu.make_async_remote_copy(src_ref=outs[a].at[src, c], dst_ref=outs[a].at[src, c], send_sem=send_sems.at[a, k - 1],
                                                  recv_sem=recv_sems.at[a, k - 1], device_id=sib, device_id_type=MESH)
                cp.start()
                sends.append(cp)
        for a in range(n):
            for k in range(1, N_CHIPS):
                src = 2 * _flip(x, k & 2) + _flip(y, k & 1)
                pltpu.make_async_remote_copy(src_ref=outs[a].at[src, c], dst_ref=outs[a].at[src, 1 - c], send_sem=send_sems.at[a, k - 1],
                                             recv_sem=recv_sems.at[a, k - 1], device_id=sib, device_id_type=MESH).wait_recv()
        for cp in sends:
            cp.wait_send()

    return pl.pallas_call(
        body, name=name,
        out_shape=[jax.ShapeDtypeStruct(s.shape, s.dtype) for s in lands],
        in_specs=[_HBM_SPEC] * n, out_specs=[_HBM_SPEC] * n,
        input_output_aliases={i: i for i in range(n)},
        scratch_shapes=[pltpu.SemaphoreType.DMA((n, N_CHIPS - 1)), pltpu.SemaphoreType.DMA((n, N_CHIPS - 1))],
    )(*lands)


def _sibling_swap(parts, name):
    n = len(parts)

    def body(*refs):
        ins, outs = refs[:n], refs[n:2 * n]
        send_sems, recv_sems = refs[2 * n:]
        x, y, c = _idx()
        cps = []
        for a in range(n):
            cp = pltpu.make_async_remote_copy(src_ref=ins[a].at[1 - c], dst_ref=outs[a], send_sem=send_sems.at[a], recv_sem=recv_sems.at[a],
                                              device_id=(x, y, 1 - c), device_id_type=MESH)
            cp.start()
            cps.append(cp)
        for cp in cps:
            cp.wait()

    return pl.pallas_call(
        body, name=name,
        out_shape=[jax.ShapeDtypeStruct(p.shape[1:], p.dtype) for p in parts],
        in_specs=[_HBM_SPEC] * n, out_specs=[_HBM_SPEC] * n,
        scratch_shapes=[pltpu.SemaphoreType.DMA((n,)), pltpu.SemaphoreType.DMA((n,))],
    )(*parts)


def _sibling_send(halves, name):
    n = len(halves)

    def body(*refs):
        ins, outs = refs[:n], refs[n:2 * n]
        send_sems, recv_sems = refs[2 * n:]
        x, y, c = _idx()
        cps = []
        for a in range(n):
            cp = pltpu.make_async_remote_copy(src_ref=ins[a], dst_ref=outs[a], send_sem=send_sems.at[a], recv_sem=recv_sems.at[a],
                                              device_id=(x, y, 1 - c), device_id_type=MESH)
            cp.start()
            cps.append(cp)
        for cp in cps:
            cp.wait()

    return pl.pallas_call(
        body, name=name,
        out_shape=[jax.ShapeDtypeStruct(h.shape, h.dtype) for h in halves],
        in_specs=[_HBM_SPEC] * n, out_specs=[_HBM_SPEC] * n,
        scratch_shapes=[pltpu.SemaphoreType.DMA((n,)), pltpu.SemaphoreType.DMA((n,))],
    )(*halves)


def _col_full(g):
    k, n = g.shape[1], g.shape[2]
    return g.transpose(1, 0, 2).reshape(k, N_CHIPS * n)


def _col_blocks(w):
    k, n = w.shape
    return w.reshape(k, N_CHIPS, n // N_CHIPS).transpose(1, 0, 2)


def _row_blocks(w):
    k, n = w.shape
    return w.reshape(N_CHIPS, k // N_CHIPS, n)


_UQ_HEAD = MLA_NOPE + MLA_ROPE

_LAT = MLA_QL + MLA_KVL + MLA_ROPE
_POOL_R = len(POOL_WINDOWS) * (POOL_GD // N_CHIPS)

_PIECE_KINDS = {
    "mlp_w1": (D_MODEL, D_MODEL, lambda g: g, _col_blocks),
    "mlp_w2": (D_MODEL, D_MODEL, lambda g: g.reshape(4 * D_MODEL, D_MODEL), _row_blocks),
    "pool_w": (_POOL_R, POOL_GD,
               lambda g: g.reshape(N_CHIPS, len(POOL_WINDOWS), POOL_GD // N_CHIPS, POOL_GD).transpose(1, 0, 2, 3).reshape(len(POOL_WINDOWS), POOL_GD, POOL_GD),
               lambda w: w.reshape(len(POOL_WINDOWS), N_CHIPS, POOL_GD // N_CHIPS, POOL_GD).transpose(1, 0, 2, 3).reshape(N_CHIPS, _POOL_R, POOL_GD)),
    "sgu_w_in": (D_MODEL, 2 * SGU_W // N_CHIPS, _col_full, _col_blocks),
    "sgu_w_out": (SGU_W // N_CHIPS, D_MODEL, lambda g: g.reshape(SGU_W, D_MODEL), _row_blocks),
    "mla_w_dq_dkv": (D_MODEL // N_CHIPS, _LAT, lambda g: jnp.pad(g.reshape(D_MODEL, _LAT), ((0, 0), (0, MLA_LATP - _LAT))),
                     lambda w: _row_blocks(w[:, :_LAT])),
    "mla_w_uq": (MLA_QL, MLA_H * _UQ_HEAD // N_CHIPS,
                 lambda g: jnp.pad(_col_full(g).reshape(MLA_QL, MLA_H, _UQ_HEAD), ((0, 0), (0, 0), (0, MLA_HP - _UQ_HEAD))).reshape(MLA_QL, MLA_H * MLA_HP),
                 lambda w: _col_blocks(w.reshape(MLA_QL, MLA_H, MLA_HP)[:, :, :_UQ_HEAD].reshape(MLA_QL, MLA_H * _UQ_HEAD))),
    "mla_w_ukv": (MLA_KVL, MLA_H * (MLA_NOPE + MLA_V) // N_CHIPS, _col_full, _col_blocks),
    "mla_w_o": (MLA_H * MLA_V // N_CHIPS, D_MODEL, lambda g: g.reshape(MLA_H * MLA_V, D_MODEL), _row_blocks),
}
_MIXER_KINDS = (("pool_w",), ("sgu_w_in", "sgu_w_out"), ("mla_w_dq_dkv", "mla_w_uq", "mla_w_ukv", "mla_w_o"))


def _layer_pieces(i):
    return [(k, i // N_MIXERS) for k in _MIXER_KINDS[i % N_MIXERS]] + [("mlp_w1", i), ("mlp_w2", i)]


def _rope_tables(positions):
    inv_freq = ROPE_THETA ** (-jnp.arange(0, MLA_ROPE, 2, dtype=F32) / MLA_ROPE)
    ang = positions.astype(F32)[:, None] * inv_freq
    cos, sin = jnp.cos(ang), jnp.sin(ang)
    z32, z64 = jnp.zeros_like(cos), jnp.zeros((positions.shape[0], 64), F32)
    return (jnp.concatenate([cos, cos, z64], axis=1), jnp.concatenate([-sin, z32, z64], axis=1), jnp.concatenate([z32, sin, z64], axis=1))


def _local_step(x, positions, target, mod, S, weights_of, grads_of):
    D = D_MODEL
    cc, sa, sb = _rope_tables(positions)
    mods = [[mod[i:i + 1, n * D:(n + 1) * D] for n in range(6)] for i in range(DEPTH)]
    h_dtype = lambda i: F32 if i % N_MIXERS == 0 else BF16
    saved = []
    h = _norm_mod_fwd(x, S["norm_mix_g"][0:1], mods[0][1], mods[0][0], h_dtype(0), "l0_norm1")
    for i in range(DEPTH):
        sh1, sc1, g1, sh2, sc2, g2 = mods[i]
        kind, j = i % N_MIXERS, i // N_MIXERS
        gmlp = S["norm_mlp_g"][i:i + 1]
        W = weights_of(i, "mix", x)
        st = {"x": x}
        norm2 = ((gmlp, "n"), (sc2, "n"), (sh2, "n"))
        if kind == 0:
            x2, pooled, ypre, h2 = _pool_fwd(h, W["pool_w"], S["pool_scale"][j:j + 1], x, g1, gmlp, sc2, sh2, f"l{i}_pool")
            st.update(pooled=pooled, y=ypre)
        elif kind == 1:
            zz = _mm(h, W["sgu_w_in"], out_dtypes=(F32,), name=f"l{i}_sgu_in")
            bs_t = S["sgu_b_s"].T
            gated = _sgu_gate_fwd(zz, S["sgu_ln_g"], S["sgu_ln_b"], S["sgu_w_s"], bs_t, f"l{i}_sgu_gate")
            x2, y, h2 = _mm(gated, W["sgu_w_out"], epi=_epi_residual_norm, extras=((x, "mn"), (g1, "n"), *norm2), out_dtypes=(F32, BF16, BF16),
                            tn=D, name=f"l{i}_sgu_out")
            st.update(h=h, zz=zz, gated=gated, y=y, bs_t=bs_t)
        else:
            lat = _mm(h, W["mla_w_dq_dkv"], out_dtypes=(F32,), name=f"l{i}_mla_lat")
            cqn, ckvn, krot = _mla_lat_fwd(lat, S["mla_q_norm_g"], S["mla_kv_norm_g"], cc, sa, sb, f"l{i}_mla_latn")
            q = _mm(cqn, W["mla_w_uq"], epi=_epi_q_rope, extras=((cc, "m"), (sa, "m"), (sb, "m")), name=f"l{i}_mla_uq")
            k, kt, v, vt = _mla_ukv(ckvn, W["mla_w_ukv"], krot, f"l{i}_mla_ukv")
            o, lse = _attn_fwd(q, k, vt, f"l{i}_attn")
            x2, y, h2 = _mm(o, W["mla_w_o"], epi=_epi_residual_norm, extras=((x, "mn"), (g1, "n"), *norm2), out_dtypes=(F32, BF16, BF16),
                            tn=D, name=f"l{i}_mla_o")
            st.update(h=h, lat=lat, cqn=cqn, ckvn=ckvn, q=q, k=k, kt=kt, v=v, o=o, lse=lse, y=y)
        W = {**W, **weights_of(i, "mlp", x2)}
        z, r2 = _mm(h2, W["mlp_w1"], epi=_epi_sq_relu, out_dtypes=(BF16, BF16), name=f"l{i}_mlp1")
        if i + 1 < DEPTH:
            norm1 = ((S["norm_mix_g"][i + 1:i + 2], "n"), (mods[i + 1][1], "n"), (mods[i + 1][0], "n"))
            x3, o2, h = _mm(z, W["mlp_w2"], epi=_epi_residual_norm, extras=((x2, "mn"), (g2, "n"), *norm1), out_dtypes=(F32, BF16, h_dtype(i + 1)),
                            tn=D, name=f"l{i}_mlp2")
        else:
            x3, o2 = _mm(z, W["mlp_w2"], epi=_epi_residual, extras=((x2, "mn"), (g2, "n")), out_dtypes=(F32, BF16), name=f"l{i}_mlp2")
        st.update(x2=x2, h2=h2, z=z, r2=r2, o2=o2, W=W)
        saved.append(st)
        x = x3

    loss, dx, dfinal_g = _loss_head(x, target, S["final_g"], "loss_head")

    gS = {"final_g": dfinal_g, "norm_mix_g": [None] * DEPTH, "norm_mlp_g": [None] * DEPTH, "pool_scale": [None] * 2}
    dmod = [None] * DEPTH
    do2, dg2 = _resid_bwd(dx, saved[-1]["o2"], mods[-1][5], f"l{DEPTH - 1}_b_res2")
    started = None
    for i in reversed(range(DEPTH)):
        st = saved[i]
        W, gW = st["W"], {}
        sh1, sc1, g1, sh2, sc2, g2 = mods[i]
        kind, j = i % N_MIXERS, i // N_MIXERS
        gmix, gmlp = S["norm_mix_g"][i:i + 1], S["norm_mlp_g"][i:i + 1]
        da = _mm(do2, W["mlp_w2"], tb=True, epi=lambda acc, rt: (acc * rt.astype(F32),), extras=((st["r2"], "mn"),), after=started, name=f"l{i}_b_dz")
        gW["mlp_w2"] = _mm(st["z"], do2, ta=True, chip_blocks="row", name=f"l{i}_b_dw2")
        dh2 = _mm(da, W["mlp_w1"], tb=True, out_dtypes=(F32,), name=f"l{i}_b_dh2")
        gW["mlp_w1"] = _mm(st["h2"], da, ta=True, chip_blocks="col", name=f"l{i}_b_dw1")
        dx2, dgmlp, dsc2, dsh2, dy, q1 = _norm_mod_bwd(st["x2"], dh2, dx, gmlp, sc2, f"l{i}_b_norm2", res=(st["y"], g1))
        gS["norm_mlp_g"][i] = dgmlp
        if kind == 0:
            dh, dpw, dpsc, dg1 = _pool_bwd(dy, st["pooled"], W["pool_w"], S["pool_scale"][j:j + 1], g1, q1, f"l{i}_b_pool")
            gW["pool_w"] = dpw.astype(BF16)
            gS["pool_scale"][j] = dpsc
        elif kind == 1:
            dg1 = q1
            dgated = _mm(dy, W["sgu_w_out"], tb=True, name=f"l{i}_b_dgated")
            gW["sgu_w_out"] = _mm(st["gated"], dy, ta=True, name=f"l{i}_b_dwout")
            dzz, dws, dbs, dlg, dlb = _sgu_gate_bwd(st["zz"], dgated, S["sgu_ln_g"], S["sgu_ln_b"], S["sgu_w_s"], st["bs_t"], f"l{i}_b_sgu_gate")
            gS.update(sgu_w_s=dws, sgu_b_s=dbs[:, :, 0], sgu_ln_g=dlg, sgu_ln_b=dlb)
            dh = _mm(dzz, W["sgu_w_in"], tb=True, out_dtypes=(F32,), name=f"l{i}_b_dh_sgu")
            gW["sgu_w_in"] = _mm(st["h"], dzz, ta=True, name=f"l{i}_b_dwin")
        else:
            dg1 = q1
            do = _mm(dy, W["mla_w_o"], tb=True, name=f"l{i}_b_do")
            gW["mla_w_o"] = _mm(st["o"], dy, ta=True, name=f"l{i}_b_dwo")
            delta = _attn_delta(do, st["o"], f"l{i}_b_delta")
            dqt, dk, dv = _attn_bwd(st["q"], st["k"], st["kt"], st["v"], do, st["lse"], delta, f"l{i}_b_attn")
            dqpad, dkv, dkrot = _mla_prep_bwd(dqt, dk, dv, cc, sa, sb, f"l{i}_b_mla_prep")
            dcqn = _mm(dqpad, W["mla_w_uq"], tb=True, out_dtypes=(F32,), name=f"l{i}_b_dcq")
            gW["mla_w_uq"] = _mm(st["cqn"], dqpad, ta=True, name=f"l{i}_b_dwuq")
            dckvn = _mm(dkv, W["mla_w_ukv"], tb=True, out_dtypes=(F32,), name=f"l{i}_b_dckv")
            gW["mla_w_ukv"] = _mm(st["ckvn"], dkv, ta=True, name=f"l{i}_b_dwukv")
            dlat, dqg, dkvg = _mla_lat_bwd(st["lat"], dcqn, dckvn, dkrot, S["mla_q_norm_g"], S["mla_kv_norm_g"], cc, sa, sb, f"l{i}_b_mla_latn")
            gS.update(mla_q_norm_g=dqg, mla_kv_norm_g=dkvg)
            dh = _mm(dlat, W["mla_w_dq_dkv"], tb=True, out_dtypes=(F32,), name=f"l{i}_b_dh_mla")
            gW["mla_w_dq_dkv"] = _mm(st["h"], dlat, ta=True, name=f"l{i}_b_dwdq")
        if i > 0:
            dx, dgmix, dsc1, dsh1, do2_prev, dg2_prev = _norm_mod_bwd(st["x"], dh, dx2, gmix, sc1, f"l{i}_b_norm1", res=(saved[i - 1]["o2"], mods[i - 1][5]))
        else:
            dx, dgmix, dsc1, dsh1 = _norm_mod_bwd(st["x"], dh, dx2, gmix, sc1, f"l{i}_b_norm1")
        gS["norm_mix_g"][i] = dgmix
        dmod[i] = jnp.concatenate([dsh1, dsc1, dg1, dsh2, dsc2, dg2], axis=1)
        started = grads_of(i, gW, dx)
        if i > 0:
            do2, dg2 = do2_prev, dg2_prev

    for n in ("norm_mix_g", "norm_mlp_g", "pool_scale"):
        gS[n] = jnp.concatenate(gS[n], axis=0)
    return loss, dx, gS, jnp.concatenate(dmod, axis=0)


_SMALL = {
    "norm_mix_g": (DEPTH, D_MODEL), "norm_mlp_g": (DEPTH, D_MODEL), "sgu_ln_g": (1, SGU_W), "sgu_ln_b": (1, SGU_W),
    "sgu_w_s": (SGU_H, SGU_CHUNK, SGU_CHUNK), "sgu_b_s": (SGU_H, SGU_CHUNK), "mla_kv_norm_g": (1, MLA_KVL), "final_g": (1, D_MODEL),
    "pool_scale": (2, D_MODEL), "mla_q_norm_g": (1, MLA_QL), "dmod": (DEPTH, 6 * D_MODEL),
}
_PACK_W = 1024


def _pack(vals):
    flat = jnp.concatenate([v.reshape(-1) for v in vals])
    rows = -(-flat.shape[0] // (8 * _PACK_W)) * 8
    return jnp.pad(flat, (0, rows * _PACK_W - flat.shape[0])).reshape(rows, _PACK_W)


def _unpack(buf, shapes):
    flat, out, off = buf.reshape(-1), [], 0
    for s in shapes:
        n = math.prod(s)
        out.append(flat[off:off + n].reshape(s))
        off += n
    return out


def kernel(x, c, positions, ada_w, ada_b, norm_mix_g, norm_mlp_g, pool_w, pool_scale, sgu_w_in, sgu_ln_g, sgu_ln_b, sgu_w_s, sgu_b_s, sgu_w_out, mla_w_dq_dkv, mla_q_norm_g, mla_kv_norm_g, mla_w_uq, mla_w_ukv, mla_w_o, mlp_w1, mlp_w2, final_g, loss_target, m_ada_w, m_ada_b, m_norm_mix_g, m_norm_mlp_g, m_pool_w, m_pool_scale, m_sgu_w_in, m_sgu_ln_g, m_sgu_ln_b, m_sgu_w_s, m_sgu_b_s, m_sgu_w_out, m_mla_w_dq_dkv, m_mla_q_norm_g, m_mla_kv_norm_g, m_mla_w_uq, m_mla_w_ukv, m_mla_w_o, m_mlp_w1, m_mlp_w2, m_final_g, v_ada_w, v_ada_b, v_norm_mix_g, v_norm_mlp_g, v_pool_w, v_pool_scale, v_sgu_w_in, v_sgu_ln_g, v_sgu_ln_b, v_sgu_w_s, v_sgu_b_s, v_sgu_w_out, v_mla_w_dq_dkv, v_mla_q_norm_g, v_mla_kv_norm_g, v_mla_w_uq, v_mla_w_ukv, v_mla_w_o, v_mlp_w1, v_mlp_w2, v_final_g):
    P = dict(ada_w=ada_w, ada_b=ada_b, norm_mix_g=norm_mix_g, norm_mlp_g=norm_mlp_g, pool_w=pool_w, pool_scale=pool_scale, sgu_w_in=sgu_w_in,
             sgu_ln_g=sgu_ln_g, sgu_ln_b=sgu_ln_b, sgu_w_s=sgu_w_s, sgu_b_s=sgu_b_s, sgu_w_out=sgu_w_out, mla_w_dq_dkv=mla_w_dq_dkv,
             mla_q_norm_g=mla_q_norm_g, mla_kv_norm_g=mla_kv_norm_g, mla_w_uq=mla_w_uq, mla_w_ukv=mla_w_ukv, mla_w_o=mla_w_o, mlp_w1=mlp_w1,
             mlp_w2=mlp_w2, final_g=final_g)
    M = dict(ada_w=m_ada_w, ada_b=m_ada_b, norm_mix_g=m_norm_mix_g, norm_mlp_g=m_norm_mlp_g, pool_w=m_pool_w, pool_scale=m_pool_scale,
             sgu_w_in=m_sgu_w_in, sgu_ln_g=m_sgu_ln_g, sgu_ln_b=m_sgu_ln_b, sgu_w_s=m_sgu_w_s, sgu_b_s=m_sgu_b_s, sgu_w_out=m_sgu_w_out,
             mla_w_dq_dkv=m_mla_w_dq_dkv, mla_q_norm_g=m_mla_q_norm_g, mla_kv_norm_g=m_mla_kv_norm_g, mla_w_uq=m_mla_w_uq, mla_w_ukv=m_mla_w_ukv,
             mla_w_o=m_mla_w_o, mlp_w1=m_mlp_w1, mlp_w2=m_mlp_w2, final_g=m_final_g)
    V = dict(ada_w=v_ada_w, ada_b=v_ada_b, norm_mix_g=v_norm_mix_g, norm_mlp_g=v_norm_mlp_g, pool_w=v_pool_w, pool_scale=v_pool_scale,
             sgu_w_in=v_sgu_w_in, sgu_ln_g=v_sgu_ln_g, sgu_ln_b=v_sgu_ln_b, sgu_w_s=v_sgu_w_s, sgu_b_s=v_sgu_b_s, sgu_w_out=v_sgu_w_out,
             mla_w_dq_dkv=v_mla_w_dq_dkv, mla_q_norm_g=v_mla_q_norm_g, mla_kv_norm_g=v_mla_kv_norm_g, mla_w_uq=v_mla_w_uq, mla_w_ukv=v_mla_w_ukv,
             mla_w_o=v_mla_w_o, mlp_w1=v_mlp_w1, mlp_w2=v_mlp_w2, final_g=v_final_g)
    order = list(P)
    xi, yi, ci = _idx()
    chip = 2 * xi + yi
    D = D_MODEL
    n_ada = ada_w.shape[2]

    pre = _allgather8(_pack([c, pool_scale, mla_q_norm_g]), "ag_small")
    flat = pre.reshape(N_DEV, -1)
    c_all = flat[:, :D]
    ps_all = flat[0::2, D:D + 2 * (D // N_CHIPS)].reshape(N_CHIPS, 2, D // N_CHIPS).transpose(1, 0, 2).reshape(2, D)
    q0 = D + 2 * (D // N_CHIPS)
    qg_all = flat[0::2, q0:q0 + MLA_QL // N_CHIPS].reshape(1, MLA_QL)

    ada_b_loc = lax.dynamic_slice_in_dim(ada_b, chip * n_ada, n_ada, axis=1)[:, None, :]
    modp = _ada_fwd(c_all, ada_w, ada_b_loc, "ada_fwd")
    mod = _mod_exchange(modp.transpose(1, 0, 2), "mod_exchange").transpose(1, 0, 2).reshape(DEPTH, 6 * D)

    S = dict(norm_mix_g=norm_mix_g, norm_mlp_g=norm_mlp_g, pool_scale=ps_all, sgu_ln_g=sgu_ln_g, sgu_ln_b=sgu_ln_b, sgu_w_s=sgu_w_s[0],
             sgu_b_s=sgu_b_s[0], mla_q_norm_g=qg_all, mla_kv_norm_g=mla_kv_norm_g, final_g=final_g[None, :])
    cidx, chipidx = jnp.reshape(ci, (1,)).astype(jnp.int32), jnp.reshape(chip, (1,)).astype(jnp.int32)
    view2d = lambda a: a.reshape(-1, a.shape[-1])

    def piece_rows(kind, blk):
        r = _PIECE_KINDS[kind][0]
        return blk * r, r

    groups = [_layer_pieces(0)[:-2], _layer_pieces(0)[-2:], _layer_pieces(1)[:-2], _layer_pieces(1)[-2:], _layer_pieces(2), _layer_pieces(3)]
    start_after = {1: (2, 3), 2: (4,), 4: (5,)}
    gathers = {}

    def gather_start(g, dep):
        srcs, shapes = [], []
        for kind, blk in groups[g]:
            r0, r = piece_rows(kind, blk)
            cdim = _PIECE_KINDS[kind][1]
            srcs.append(view2d(P[kind])[r0:r0 + r].astype(BF16).reshape(2, r // 2, cdim))
            shapes.append(jax.ShapeDtypeStruct((N_CHIPS, 2, r // 2, cdim), BF16))
        gathers[g] = _xchip_start("gather", srcs, shapes, dep, f"ag_start_g{g}")

    def gather_finish(g, after):
        ssem, rsem, srcs, lands, _ = gathers.pop(g)
        deps = [after]
        for nxt in start_after.get(g, ()):
            gather_start(nxt, deps[-1])
            deps.append(gathers[nxt][-1])
        srcs, lands = _xchip_wait("gather", ssem, rsem, srcs, lands, deps, f"ag_wait_g{g}")
        lands = _sibling_fwd(lands, f"ag_sibling_g{g}")
        W = {}
        for (kind, _), s, land in zip(groups[g], srcs, lands, strict=True):
            r, cdim, to_full, _ = _PIECE_KINDS[kind]
            W[kind] = to_full(lax.dynamic_update_index_in_dim(land, s, chip, 0).reshape(N_CHIPS, r, cdim))
        return W

    def weights_of(i, part, x_i):
        if i < 2:
            return gather_finish(2 * i + (part == "mlp"), x_i)
        return gather_finish(i + 2, x_i) if part == "mix" else {}

    scatters = {}
    bufs = {n: tuple(lax.empty(view2d(P[n]).shape, F32) for _ in range(4)) for n in _PIECE_KINDS}

    def scatter_start(i, gW, dep):
        pcs = _layer_pieces(i)
        blocked = []
        for kind, _ in pcs:
            r, cdim, _, to_blocks = _PIECE_KINDS[kind]
            g = gW[kind]
            blocked.append(g if g.ndim == 4 else to_blocks(g).reshape(N_CHIPS, 2, r // 2, cdim).transpose(1, 0, 2, 3))
        from_sib = _sibling_swap(blocked, f"rs_sibling_l{i}")
        pair, shapes = [], []
        for (kind, _), b, f in zip(pcs, blocked, from_sib, strict=True):
            _, _, hr, cdim = b.shape
            p = _sum_sel(cidx, b.reshape(2, N_CHIPS * hr, cdim), [f.reshape(1, N_CHIPS * hr, cdim)], f"rs_pair_l{i}_{kind}", BF16)
            pair.append(p.reshape(N_CHIPS, hr, cdim))
            shapes.append(jax.ShapeDtypeStruct((N_CHIPS - 1, hr, cdim), BF16))
        scatters[i] = (pcs, *_xchip_start("scatter", pair, shapes, dep, f"rs_start_l{i}"))
        return scatters[i][-1]

    def scatter_finish(i, after):
        pcs, ssem, rsem, pair, lands, _ = scatters.pop(i)
        pair, lands = _xchip_wait("scatter", ssem, rsem, pair, lands, after, f"rs_wait_l{i}")
        halves = [_sum_sel(chipidx, p, [l], f"rs_sum_l{i}_{kind}", F32) for (kind, _), p, l in zip(pcs, pair, lands, strict=True)]
        got = _sibling_send(halves, f"rs_merge_l{i}")
        for (kind, blk), mine, other in zip(pcs, halves, got, strict=True):
            r0, _ = piece_rows(kind, blk)
            bufs[kind] = tuple(_adamw_piece(cidx, view2d(P[kind]), view2d(M[kind]), view2d(V[kind]), mine, other, bufs[kind], r0,
                                            f"adamw_l{i}_{kind}"))
        return lands[0]

    first_layer = {}

    def grads_of(i, gW, dx_i):
        if i == 0:
            first_layer.update(gW)
            return None
        dep = scatter_finish(i + 1, [dx_i]) if i + 1 in scatters else dx_i
        return scatter_start(i, gW, dep)

    gather_start(0, mod)
    gather_start(1, gathers[0][-1])
    mod = mod + gathers[1][-1][0, 0]
    loss_l, dx, gS, dmod = _local_step(x[0], positions[0], loss_target[0], mod, S, weights_of, grads_of)
    loss = lax.psum(loss_l[0, 0], ("x", "y", "c"))

    gS["dmod"] = dmod
    packed = _pack([gS[n] for n in _SMALL])
    sg = _xchip_start("all8", [packed], [jax.ShapeDtypeStruct((N_DEV, *packed.shape), F32)], dx, "sg_start")
    tok0 = scatter_start(0, first_layer, sg[-1])[0, 0]
    scatter_finish(1, [dx, scatters[0][-1]])
    sg_src, sg_land = _xchip_wait("all8", sg[0], sg[1], sg[2], sg[3], [bufs[n][0] for n in ("mlp_w1", "mlp_w2", "sgu_w_in", "sgu_w_out")], "sg_wait")
    small = lax.dynamic_update_index_in_dim(sg_land[0], sg_src[0], 4 * xi + 2 * yi + ci, 0) + tok0
    small_sum = _unpack(_sum_lead([small], "sum_small_grads"), list(_SMALL.values()))
    G = dict(zip(_SMALL, small_sum, strict=True))
    grads = {
        "ada_b": G["dmod"], "norm_mix_g": G["norm_mix_g"], "norm_mlp_g": G["norm_mlp_g"], "sgu_ln_g": G["sgu_ln_g"], "sgu_ln_b": G["sgu_ln_b"],
        "sgu_w_s": G["sgu_w_s"][None], "sgu_b_s": G["sgu_b_s"][None], "mla_kv_norm_g": G["mla_kv_norm_g"], "final_g": G["final_g"][0],
        "pool_scale": lax.dynamic_slice_in_dim(G["pool_scale"], chip * (D // N_CHIPS), D // N_CHIPS, axis=1),
        "mla_q_norm_g": lax.dynamic_slice_in_dim(G["mla_q_norm_g"], chip * (MLA_QL // N_CHIPS), MLA_QL // N_CHIPS, axis=1),
    }
    dmod_all = _unpack(small, [(N_DEV,) + (small.shape[1] * _PACK_W,)])[0]
    off = sum(math.prod(s) for n, s in _SMALL.items() if n != "dmod")
    dmod_all = dmod_all[:, off:off + DEPTH * 6 * D].reshape(N_DEV, DEPTH, 6 * D)
    dmod_loc = lax.dynamic_slice_in_dim(dmod_all, chip * n_ada, n_ada, axis=2).transpose(1, 0, 2)
    grads["ada_w"] = _ada_bwd(c_all.T, dmod_loc, "ada_bwd")

    deltas, new_m, new_v = {}, {}, {}
    for n in order:
        if n not in _PIECE_KINDS:
            deltas[n], new_m[n], new_v[n] = _adamw(P[n], grads[n].reshape(P[n].shape), M[n], V[n], f"adamw_{n}")
    scatter_finish(0, [deltas["ada_w"], deltas["sgu_w_s"]] + [bufs[n][0] for n in ("mlp_w1", "mlp_w2", "sgu_w_in", "mla_w_o")])
    for n in _PIECE_KINDS:
        grads[n], deltas[n], new_m[n], new_v[n] = (b.reshape(P[n].shape) for b in bufs[n])
    return (loss, dx[None], *[grads[n].reshape(P[n].shape) for n in order], *[deltas[n] for n in order], *[new_m[n] for n in order],
            *[new_v[n] for n in order])
```

```python
import math

import jax
import jax.numpy as jnp
from jax import lax
from jax.experimental import pallas as pl
from jax.experimental.pallas import tpu as pltpu

F32, BF16 = jnp.float32, jnp.bfloat16
MESH = pl.DeviceIdType.MESH

D_MODEL = 1024
DEPTH = 4
N_MIXERS = 3
POOL_WINDOWS = (2, 4, 8, 16)
POOL_GD = D_MODEL // len(POOL_WINDOWS)
POOL_HALO = 16
SGU_CHUNK = 128
SGU_W = D_MODEL
SGU_HD = 128
SGU_H = SGU_W // SGU_HD
MLA_H = 16
MLA_QL = 256
MLA_KVL = 128
MLA_NOPE = 128
MLA_ROPE = 64
MLA_V = 128
MLA_HP = 256
MLA_LATP = 512
ROPE_THETA = 10000.0
RMS_EPS = 1e-6
LN_EPS = 1e-5
SM_SCALE = (MLA_NOPE + MLA_ROPE) ** -0.5
NEG_INF = -1e30
ADAM_LR, ADAM_B1, ADAM_B2, ADAM_EPS, ADAM_WD, ADAM_STEP = 0.001, 0.9, 0.999, 1e-08, 0.01, 10
N_CHIPS = 4
N_DEV = 8
ROW_TILE = 512
ATT_TILE = 512
ATT_SUB = 256
ATT_FWD_HEADS = 4
ATT_BWD_HEADS = 2
MM_VMEM_BUDGET = 40 << 20


def _idx():
    return lax.axis_index("x"), lax.axis_index("y"), lax.axis_index("c")


def _mm(a, b, *, name, ta=False, tb=False, epi=None, extras=(), out_dtypes=(BF16,), tm=1024, tn=1024, tk=1024, chip_blocks=None, after=None):
    if ta:
        K, M = a.shape
    else:
        M, K = a.shape
    b_chips = b.ndim == 3
    if b_chips:
        assert b.shape[0] == N_CHIPS
        Kb, N = (N_CHIPS * b.shape[2], b.shape[1]) if tb else (b.shape[1], N_CHIPS * b.shape[2])
    elif tb:
        N, Kb = b.shape
    else:
        Kb, N = b.shape
    assert K == Kb, (a.shape, b.shape, ta, tb)
    if b_chips and not tb:
        tn = min(tn, N // N_CHIPS)
    if chip_blocks == "col":
        tm, tn = min(tm, M // 2), min(tn, N // N_CHIPS)
    elif chip_blocks == "row":
        tm = min(tm, M // N_CHIPS // 2)
    tm, tn, tk = min(tm, M), min(tn, N), min(tk, K)

    def vmem_bytes(tm_, tk_):
        per_mn = sum(arr.dtype.itemsize for arr, kind in extras if kind == "mn") + sum(jnp.dtype(dt).itemsize for dt in out_dtypes)
        return 2 * (tm_ * tk_ * a.dtype.itemsize + tk_ * tn * b.dtype.itemsize + tm_ * tn * per_mn)

    if vmem_bytes(tm, K) <= MM_VMEM_BUDGET:
        tk = K
    elif tm >= 512 and vmem_bytes(tm // 2, K) <= MM_VMEM_BUDGET:
        tm, tk = tm // 2, K
    assert M % tm == 0 and N % tn == 0 and K % tk == 0, (M, N, K, tm, tn, tk)
    nk = K // tk
    a_spec = pl.BlockSpec((tk, tm), lambda i, j, k: (k, i)) if ta else pl.BlockSpec((tm, tk), lambda i, j, k: (i, k))
    b_spec = pl.BlockSpec((tn, tk), lambda i, j, k: (j, k)) if tb else pl.BlockSpec((tk, tn), lambda i, j, k: (k, j))
    if b_chips and tb:
        assert nk == 1 and not ta
        b_spec = pl.BlockSpec((N_CHIPS, tn, K // N_CHIPS), lambda i, j, k: (0, j, 0))
    elif b_chips:
        per = N // N_CHIPS // tn
        b_spec = pl.BlockSpec((None, tk, tn), lambda i, j, k: (j // per, k, j % per))
    ex_specs = []
    for arr, kind in extras:
        if kind == "mn":
            ex_specs.append(pl.BlockSpec((tm, tn), lambda i, j, k: (i, j)))
        elif kind == "n":
            ex_specs.append(pl.BlockSpec((1, tn), lambda i, j, k: (0, j)))
        else:
            ex_specs.append(pl.BlockSpec((tm, arr.shape[1]), lambda i, j, k: (i, 0)))
    n_ex, n_out = len(extras), len(out_dtypes)
    n_in = 2 + n_ex + (after is not None)
    dims = (((0 if ta else 1,), (1 if tb else 0,)), ((), ()))

    def body(*refs):
        a_ref, b_ref = refs[0], refs[1]
        ex_refs = refs[2:2 + n_ex]
        out_refs = refs[n_in:n_in + n_out]
        if b_chips and tb:
            kc = K // N_CHIPS
            part = None
            for cb in range(N_CHIPS):
                p = lax.dot_general(a_ref[:, cb * kc:(cb + 1) * kc].astype(BF16), b_ref[cb].astype(BF16), dims, preferred_element_type=F32)
                part = p if part is None else part + p
        else:
            part = lax.dot_general(a_ref[...].astype(BF16), b_ref[...].astype(BF16), dims, preferred_element_type=F32)

        def finish(acc):
            outs = epi(acc, *[r[...] for r in ex_refs]) if epi is not None else (acc,)
            for r, o in zip(out_refs, outs, strict=True):
                r[...] = o.astype(r.dtype)

        if nk == 1:
            finish(part)
        else:
            acc_ref = refs[-1]
            k = pl.program_id(2)

            @pl.when(k == 0)
            def _():
                acc_ref[...] = part

            @pl.when(k > 0)
            def _():
                acc_ref[...] += part

            @pl.when(k == nk - 1)
            def _():
                finish(acc_ref[...])

    out_specs = [pl.BlockSpec((tm, tn), lambda i, j, k: (i, j)) for _ in range(n_out)]
    out_shape = [jax.ShapeDtypeStruct((M, N), dt) for dt in out_dtypes]
    if chip_blocks is not None:
        assert n_out == 1
        if chip_blocks == "col":
            rh, cb = M // 2 // tm, N // N_CHIPS // tn
            out_specs = [pl.BlockSpec((None, None, tm, tn), lambda i, j, k: (i // rh, j // cb, i % rh, j % cb))]
            out_shape = [jax.ShapeDtypeStruct((2, N_CHIPS, M // 2, N // N_CHIPS), out_dtypes[0])]
        else:
            rh = M // N_CHIPS // 2 // tm
            out_specs = [pl.BlockSpec((None, None, tm, tn), lambda i, j, k: ((i // rh) % 2, i // (2 * rh), i % rh, j))]
            out_shape = [jax.ShapeDtypeStruct((2, N_CHIPS, M // N_CHIPS // 2, N), out_dtypes[0])]
    outs = pl.pallas_call(
        body,
        name=name,
        grid=(M // tm, N // tn, nk),
        in_specs=[a_spec, b_spec, *ex_specs] + ([pl.BlockSpec(memory_space=pl.ANY)] if after is not None else []),
        out_specs=out_specs,
        out_shape=out_shape,
        scratch_shapes=[pltpu.VMEM((tm, tn), F32)] if nk > 1 else [],
        compiler_params=pltpu.CompilerParams(dimension_semantics=("parallel", "parallel", "arbitrary")),
    )(a, b, *[arr for arr, _ in extras], *([after] if after is not None else []))
    return outs[0] if n_out == 1 else tuple(outs)


def _epi_sq_relu(acc):
    r = jnp.maximum(acc, 0.0)
    return r * r, 2.0 * r


def _epi_residual(acc, x, g):
    return x + g * acc, acc


def _rms_mod(xv, gain, sc, sh):
    r = lax.rsqrt(jnp.mean(xv * xv, axis=-1, keepdims=True) + RMS_EPS)
    return ((xv * r) * gain) * (1.0 + sc) + sh


def _epi_residual_norm(acc, x, g, gain, sc, sh):
    xn = x + g * acc
    return xn, acc, _rms_mod(xn, gain, sc, sh)


def _row_spec(tr, d):
    return pl.BlockSpec((tr, d), lambda i: (i, 0))


def _vec_spec(d):
    return pl.BlockSpec((1, d), lambda i: (0, 0))


def _colsum(v):
    return jnp.sum(v, axis=0, keepdims=True)


def _norm_mod_fwd(x, gain, sc, sh, out_dtype, name):
    T, D = x.shape
    tr = min(T, ROW_TILE)

    def body(x_ref, g_ref, sc_ref, sh_ref, o_ref):
        o_ref[...] = _rms_mod(x_ref[...], g_ref[...], sc_ref[...], sh_ref[...]).astype(o_ref.dtype)

    return pl.pallas_call(
        body, name=name, grid=(T // tr,),
        in_specs=[_row_spec(tr, D), _vec_spec(D), _vec_spec(D), _vec_spec(D)],
        out_specs=_row_spec(tr, D),
        out_shape=jax.ShapeDtypeStruct((T, D), out_dtype),
        compiler_params=pltpu.CompilerParams(dimension_semantics=("parallel",)),
    )(x, gain, sc, sh)


def _norm_mod_bwd(x, dh, dres, gain, sc, name, res=None):
    T, D = x.shape
    tr = min(T, ROW_TILE)

    def body(x_ref, dh_ref, dres_ref, g_ref, sc_ref, *refs):
        dx_ref, dg_ref, dsc_ref, dsh_ref = refs[-6:-2] if res is not None else refs

        @pl.when(pl.program_id(0) == 0)
        def _():
            dg_ref[...] = jnp.zeros_like(dg_ref)
            dsc_ref[...] = jnp.zeros_like(dsc_ref)
            dsh_ref[...] = jnp.zeros_like(dsh_ref)
            if res is not None:
                refs[-1][...] = jnp.zeros_like(refs[-1])

        xv = x_ref[...]
        r = lax.rsqrt(jnp.mean(xv * xv, axis=-1, keepdims=True) + RMS_EPS)
        xn = xv * r
        dhv = dh_ref[...].astype(F32)
        dsh_ref[...] += _colsum(dhv)
        dsc_ref[...] += _colsum(dhv * (xn * g_ref[...]))
        dt = dhv * (1.0 + sc_ref[...])
        dg_ref[...] += _colsum(dt * xn)
        dxn = dt * g_ref[...]
        dxv = dres_ref[...] + r * (dxn - xn * jnp.mean(dxn * xn, axis=-1, keepdims=True))
        dx_ref[...] = dxv
        if res is not None:
            y_ref, gr_ref, dy_ref, q_ref = refs[0], refs[1], refs[-2], refs[-1]
            dy_ref[...] = (gr_ref[...] * dxv).astype(BF16)
            q_ref[...] += _colsum(dxv * y_ref[...].astype(F32))

    extra_in, extra_spec = ([], []) if res is None else (list(res), [_row_spec(tr, D), _vec_spec(D)])
    return pl.pallas_call(
        body, name=name, grid=(T // tr,),
        in_specs=[_row_spec(tr, D), _row_spec(tr, D), _row_spec(tr, D), _vec_spec(D), _vec_spec(D), *extra_spec],
        out_specs=[_row_spec(tr, D), _vec_spec(D), _vec_spec(D), _vec_spec(D)] + ([_row_spec(tr, D), _vec_spec(D)] if res is not None else []),
        out_shape=[jax.ShapeDtypeStruct((T, D), F32)] + [jax.ShapeDtypeStruct((1, D), F32)] * 3
        + ([jax.ShapeDtypeStruct((T, D), BF16), jax.ShapeDtypeStruct((1, D), F32)] if res is not None else []),
        compiler_params=pltpu.CompilerParams(dimension_semantics=("arbitrary",)),
    )(x, dh, dres, gain, sc, *extra_in)


def _resid_bwd(dx, y, g, name):
    T, D = dx.shape
    tr = min(T, ROW_TILE)

    def body(dx_ref, y_ref, g_ref, dy_ref, q_ref):
        @pl.when(pl.program_id(0) == 0)
        def _():
            q_ref[...] = jnp.zeros_like(q_ref)

        dxv = dx_ref[...]
        dy_ref[...] = (g_ref[...] * dxv).astype(BF16)
        q_ref[...] += _colsum(dxv * y_ref[...].astype(F32))

    return pl.pallas_call(
        body, name=name, grid=(T // tr,),
        in_specs=[_row_spec(tr, D), _row_spec(tr, D), _vec_spec(D)],
        out_specs=[_row_spec(tr, D), _vec_spec(D)],
        out_shape=[jax.ShapeDtypeStruct((T, D), BF16), jax.ShapeDtypeStruct((1, D), F32)],
        compiler_params=pltpu.CompilerParams(dimension_semantics=("arbitrary",)),
    )(dx, y, g)


def _loss_head(x, target, gain, name):
    T, D = x.shape
    tr = min(T, ROW_TILE)

    def body(x_ref, t_ref, g_ref, loss_ref, dx_ref, dg_ref):
        @pl.when(pl.program_id(0) == 0)
        def _():
            loss_ref[...] = jnp.zeros_like(loss_ref)
            dg_ref[...] = jnp.zeros_like(dg_ref)

        xv = x_ref[...]
        r = lax.rsqrt(jnp.mean(xv * xv, axis=-1, keepdims=True) + RMS_EPS)
        xn = xv * r
        err = xn * g_ref[...] - t_ref[...]
        row = jnp.mean(err * err, axis=-1, keepdims=True)
        loss_ref[...] += 0.5 * jnp.sum(row, axis=0, keepdims=True)
        dy = err * (1.0 / D)
        dg_ref[...] += _colsum(dy * xn)
        dxn = dy * g_ref[...]
        dx_ref[...] = r * (dxn - xn * jnp.mean(dxn * xn, axis=-1, keepdims=True))

    return pl.pallas_call(
        body, name=name, grid=(T // tr,),
        in_specs=[_row_spec(tr, D), _row_spec(tr, D), _vec_spec(D)],
        out_specs=[_vec_spec(128), _row_spec(tr, D), _vec_spec(D)],
        out_shape=[jax.ShapeDtypeStruct((1, 128), F32), jax.ShapeDtypeStruct((T, D), F32), jax.ShapeDtypeStruct((1, D), F32)],
        compiler_params=pltpu.CompilerParams(dimension_semantics=("arbitrary",)),
    )(x, target, gain)


def _pool_fwd(h, w, scale, x, g1, gmlp, sc2, sh2, name):
    T, D = h.shape
    tr = min(T, ROW_TILE)

    def body(h_ref, w_ref, sc_ref, x_ref, g_ref, gm_ref, sc2_ref, sh2_ref, x2_ref, pooled_ref, ypre_ref, h2_ref, halo_ref):
        i = pl.program_id(0)

        @pl.when(i == 0)
        def _():
            halo_ref[...] = jnp.zeros_like(halo_ref)

        hv = h_ref[...]
        buf = jnp.concatenate([halo_ref[...], hv], axis=0)
        halo_ref[...] = hv[tr - POOL_HALO:, :]
        t = (i * tr + lax.broadcasted_iota(jnp.int32, (tr, 1), 0)).astype(F32)
        for gi, win in enumerate(POOL_WINDOWS):
            cols = slice(gi * POOL_GD, (gi + 1) * POOL_GD)
            val = buf[:, cols]
            sh = 1
            while sh < win:
                val = val + pltpu.roll(val, sh, axis=0)
                sh *= 2
            pooled = val[POOL_HALO:, :] / jnp.minimum(t + 1.0, float(win)) - hv[:, cols]
            pb = pooled.astype(BF16)
            pooled_ref[:, cols] = pb
            yp = jnp.dot(pb, w_ref[gi], preferred_element_type=F32)
            ypre_ref[:, cols] = yp.astype(BF16)
            x2_ref[:, cols] = x_ref[:, cols] + g_ref[:, cols] * (yp * sc_ref[:, cols])
        h2_ref[...] = _rms_mod(x2_ref[...], gm_ref[...], sc2_ref[...], sh2_ref[...]).astype(BF16)

    return pl.pallas_call(
        body, name=name, grid=(T // tr,),
        in_specs=[_row_spec(tr, D), pl.BlockSpec(w.shape, lambda i: (0, 0, 0)), _vec_spec(D), _row_spec(tr, D), _vec_spec(D), _vec_spec(D),
                  _vec_spec(D), _vec_spec(D)],
        out_specs=[_row_spec(tr, D)] * 4,
        out_shape=[jax.ShapeDtypeStruct((T, D), F32), jax.ShapeDtypeStruct((T, D), BF16), jax.ShapeDtypeStruct((T, D), BF16),
                   jax.ShapeDtypeStruct((T, D), BF16)],
        scratch_shapes=[pltpu.VMEM((POOL_HALO, D), F32)],
        compiler_params=pltpu.CompilerParams(dimension_semantics=("arbitrary",)),
    )(h, w, scale, x, g1, gmlp, sc2, sh2)


def _pool_bwd(dy, pooled, w, scale, g1, q, name):
    T, D = dy.shape
    tr = min(T, ROW_TILE)
    nt = T // tr
    ltot = tr + POOL_HALO

    def body(dy_ref, pooled_ref, w_ref, sc_ref, g_ref, q_ref, dh_ref, dw_ref, dsc_ref, dg_ref, halo_ref):
        i = pl.program_id(0)

        @pl.when(i == 0)
        def _():
            halo_ref[...] = jnp.zeros_like(halo_ref)
            dw_ref[...] = jnp.zeros_like(dw_ref)
            dsc_ref[...] = g_ref[...] * q_ref[...]
            dg_ref[...] = sc_ref[...] * q_ref[...]

        t = ((nt - 1 - i) * tr + lax.broadcasted_iota(jnp.int32, (tr, 1), 0)).astype(F32)
        for gi, win in enumerate(POOL_WINDOWS):
            cols = slice(gi * POOL_GD, (gi + 1) * POOL_GD)
            dyb = (dy_ref[:, cols].astype(F32) * sc_ref[:, cols]).astype(BF16)
            dw_ref[gi] += lax.dot_general(pooled_ref[:, cols], dyb, (((0,), (0,)), ((), ())), preferred_element_type=F32)
            dpool = lax.dot_general(dyb, w_ref[gi], (((1,), (1,)), ((), ())), preferred_element_type=F32)
            qv = dpool / jnp.minimum(t + 1.0, float(win))
            val = jnp.concatenate([qv, halo_ref[:, cols]], axis=0)
            halo_ref[:, cols] = qv[:POOL_HALO, :]
            sh = 1
            while sh < win:
                val = val + pltpu.roll(val, ltot - sh, axis=0)
                sh *= 2
            dh_ref[:, cols] = val[:tr, :] - dpool

    rev = pl.BlockSpec((tr, D), lambda i: (nt - 1 - i, 0))
    return pl.pallas_call(
        body, name=name, grid=(nt,),
        in_specs=[rev, rev, pl.BlockSpec(w.shape, lambda i: (0, 0, 0)), _vec_spec(D), _vec_spec(D), _vec_spec(D)],
        out_specs=[rev, pl.BlockSpec(w.shape, lambda i: (0, 0, 0)), _vec_spec(D), _vec_spec(D)],
        out_shape=[jax.ShapeDtypeStruct((T, D), F32), jax.ShapeDtypeStruct(w.shape, F32),
                   jax.ShapeDtypeStruct((1, D), F32), jax.ShapeDtypeStruct((1, D), F32)],
        scratch_shapes=[pltpu.VMEM((POOL_HALO, D), F32)],
        compiler_params=pltpu.CompilerParams(dimension_semantics=("arbitrary",)),
    )(dy, pooled, w, scale, g1, q)


_INV_SQRT2 = 0.7071067811865476
_INV_SQRT2PI = 0.3989422804014327


def _gelu(v):
    return 0.5 * v * (1.0 + lax.erf(v * _INV_SQRT2))


def _gelu_grad(v):
    return 0.5 * (1.0 + lax.erf(v * _INV_SQRT2)) + v * jnp.exp(-0.5 * v * v) * _INV_SQRT2PI


def _sgu_ln(v, g, b):
    mu = jnp.mean(v, axis=-1, keepdims=True)
    xc = v - mu
    rstd = lax.rsqrt(jnp.mean(xc * xc, axis=-1, keepdims=True) + LN_EPS)
    xh = xc * rstd
    return xh, rstd, xh * g + b


def _tril_mask():
    return lax.broadcasted_iota(jnp.int32, (SGU_CHUNK, SGU_CHUNK), 0) >= lax.broadcasted_iota(jnp.int32, (SGU_CHUNK, SGU_CHUNK), 1)


SGU_TILE = 256


def _sgu_gate_fwd(zz, ln_g, ln_b, ws, bs_t, name):
    T = zz.shape[0]
    ts = min(T, SGU_TILE)

    def body(zz_ref, g_ref, b_ref, ws_ref, bs_ref, out_ref):
        z = _gelu(zz_ref[...])
        u = z[:, :SGU_W]
        _, _, vn = _sgu_ln(z[:, SGU_W:], g_ref[...], b_ref[...])
        vb = vn.astype(BF16)
        tril = _tril_mask()
        for hh in range(SGU_H):
            wm = jnp.where(tril, ws_ref[hh], 0.0).astype(BF16)
            bcol = bs_ref[:, hh:hh + 1]
            cs = slice(hh * SGU_HD, (hh + 1) * SGU_HD)
            for j in range(ts // SGU_CHUNK):
                rs = slice(j * SGU_CHUNK, (j + 1) * SGU_CHUNK)
                mixed = jnp.dot(wm, vb[rs, cs], preferred_element_type=F32) + bcol
                out_ref[rs, cs] = (u[rs, cs] * mixed).astype(BF16)

    return pl.pallas_call(
        body, name=name, grid=(T // ts,),
        in_specs=[_row_spec(ts, 2 * SGU_W), _vec_spec(SGU_W), _vec_spec(SGU_W),
                  pl.BlockSpec(ws.shape, lambda i: (0, 0, 0)), pl.BlockSpec(bs_t.shape, lambda i: (0, 0))],
        out_specs=_row_spec(ts, SGU_W),
        out_shape=jax.ShapeDtypeStruct((T, SGU_W), BF16),
        compiler_params=pltpu.CompilerParams(dimension_semantics=("parallel",)),
    )(zz, ln_g, ln_b, ws, bs_t)


def _sgu_gate_bwd(zz, dgated, ln_g, ln_b, ws, bs_t, name):
    T = zz.shape[0]
    ts = min(T, SGU_TILE)
    nt = T // ts

    def body(zz_ref, dg_ref, g_ref, b_ref, ws_ref, bs_ref, dzz_ref, dws_ref, dbs_ref, dlg_ref, dlb_ref, dlo_ref, dmx_ref):
        i = pl.program_id(0)

        @pl.when(i == 0)
        def _():
            dws_ref[...] = jnp.zeros_like(dws_ref)
            dmx_ref[...] = jnp.zeros_like(dmx_ref)
            dlg_ref[...] = jnp.zeros_like(dlg_ref)
            dlb_ref[...] = jnp.zeros_like(dlb_ref)

        zzv = zz_ref[...]
        z = _gelu(zzv)
        u = z[:, :SGU_W]
        xh, rstd, vn = _sgu_ln(z[:, SGU_W:], g_ref[...], b_ref[...])
        vb = vn.astype(BF16)
        dgv = dg_ref[...].astype(F32)
        tril = _tril_mask()
        for hh in range(SGU_H):
            wm = jnp.where(tril, ws_ref[hh], 0.0).astype(BF16)
            bcol = bs_ref[:, hh:hh + 1]
            cs = slice(hh * SGU_HD, (hh + 1) * SGU_HD)
            for j in range(ts // SGU_CHUNK):
                rs = slice(j * SGU_CHUNK, (j + 1) * SGU_CHUNK)
                mixed = jnp.dot(wm, vb[rs, cs], preferred_element_type=F32) + bcol
                dmixed = dgv[rs, cs] * u[rs, cs]
                dzz_ref[rs, cs] = (dgv[rs, cs] * mixed * _gelu_grad(zzv[rs, cs])).astype(BF16)
                dmb = dmixed.astype(BF16)
                dws_ref[hh] += lax.dot_general(dmb, vb[rs, cs], (((1,), (1,)), ((), ())), preferred_element_type=F32)
                dmx_ref[hh] += dmixed
                dlo_ref[rs, cs] = lax.dot_general(wm, dmb, (((0,), (0,)), ((), ())), preferred_element_type=F32)
        dlo = dlo_ref[...]
        dlg_ref[...] += _colsum(dlo * xh)
        dlb_ref[...] += _colsum(dlo)
        dxh = dlo * g_ref[...]
        dv = rstd * (dxh - jnp.mean(dxh, axis=-1, keepdims=True) - xh * jnp.mean(dxh * xh, axis=-1, keepdims=True))
        dzz_ref[:, SGU_W:] = (dv * _gelu_grad(zzv[:, SGU_W:])).astype(BF16)

        @pl.when(i == nt - 1)
        def _():
            tril_f = tril.astype(F32)
            for hh in range(SGU_H):
                dws_ref[hh] = dws_ref[hh] * tril_f
                dbs_ref[hh] = jnp.broadcast_to(jnp.sum(dmx_ref[hh], axis=-1, keepdims=True), (SGU_CHUNK, SGU_HD))

    full3 = pl.BlockSpec(ws.shape, lambda i: (0, 0, 0))
    return pl.pallas_call(
        body, name=name, grid=(nt,),
        in_specs=[_row_spec(ts, 2 * SGU_W), _row_spec(ts, SGU_W), _vec_spec(SGU_W), _vec_spec(SGU_W), full3,
                  pl.BlockSpec(bs_t.shape, lambda i: (0, 0))],
        out_specs=[_row_spec(ts, 2 * SGU_W), full3, full3, _vec_spec(SGU_W), _vec_spec(SGU_W)],
        out_shape=[jax.ShapeDtypeStruct((T, 2 * SGU_W), BF16), jax.ShapeDtypeStruct(ws.shape, F32), jax.ShapeDtypeStruct(ws.shape, F32),
                   jax.ShapeDtypeStruct((1, SGU_W), F32), jax.ShapeDtypeStruct((1, SGU_W), F32)],
        scratch_shapes=[pltpu.VMEM((ts, SGU_W), F32), pltpu.VMEM(ws.shape, F32)],
        compiler_params=pltpu.CompilerParams(dimension_semantics=("arbitrary",)),
    )(zz, dgated, ln_g, ln_b, ws, bs_t)


def _rope_fwd(blk, cc, sa, sb):
    return blk * cc + pltpu.roll(blk, 96, axis=1) * sa + pltpu.roll(blk, 32, axis=1) * sb


def _rope_bwd(d, cc, sa, sb):
    return d * cc + pltpu.roll(d * sa, 32, axis=1) + pltpu.roll(d * sb, 96, axis=1)


def _rms(v, g):
    r = lax.rsqrt(jnp.mean(v * v, axis=-1, keepdims=True) + RMS_EPS)
    vn = v * r
    return vn, r, vn * g


def _rms_bwd(dy, vn, r, g):
    dvn = dy * g
    return r * (dvn - vn * jnp.mean(dvn * vn, axis=-1, keepdims=True))


MLA_TILE = 256
_KV0 = MLA_QL
_KR0 = MLA_QL + MLA_KVL


def _mla_lat_fwd(lat, qg, kvg, cc, sa, sb, name):
    T = lat.shape[0]
    tr = min(T, ROW_TILE)

    def body(lat_ref, qg_ref, kvg_ref, cc_ref, sa_ref, sb_ref, cq_ref, ckv_ref, kr_ref):
        lv = lat_ref[...]
        cq_ref[...] = _rms(lv[:, :_KV0], qg_ref[...])[2].astype(BF16)
        ckv_ref[...] = _rms(lv[:, _KV0:_KR0], kvg_ref[...])[2].astype(BF16)
        kr_ref[...] = _rope_fwd(lv[:, _KR0:], cc_ref[...], sa_ref[...], sb_ref[...])

    return pl.pallas_call(
        body, name=name, grid=(T // tr,),
        in_specs=[_row_spec(tr, MLA_LATP), _vec_spec(MLA_QL), _vec_spec(MLA_KVL), _row_spec(tr, 128), _row_spec(tr, 128), _row_spec(tr, 128)],
        out_specs=[_row_spec(tr, MLA_QL), _row_spec(tr, MLA_KVL), _row_spec(tr, 128)],
        out_shape=[jax.ShapeDtypeStruct((T, MLA_QL), BF16), jax.ShapeDtypeStruct((T, MLA_KVL), BF16), jax.ShapeDtypeStruct((T, 128), F32)],
        compiler_params=pltpu.CompilerParams(dimension_semantics=("parallel",)),
    )(lat, qg, kvg, cc, sa, sb)


def _mla_lat_bwd(lat, dcqn, dckvn, dkrot, qg, kvg, cc, sa, sb, name):
    T = lat.shape[0]
    tr = min(T, ROW_TILE)

    def body(lat_ref, dcq_ref, dckv_ref, dkr_ref, qg_ref, kvg_ref, cc_ref, sa_ref, sb_ref, dlat_ref, dqg_ref, dkvg_ref):
        @pl.when(pl.program_id(0) == 0)
        def _():
            dqg_ref[...] = jnp.zeros_like(dqg_ref)
            dkvg_ref[...] = jnp.zeros_like(dkvg_ref)

        lv = lat_ref[...]
        qn, qr, _ = _rms(lv[:, :_KV0], qg_ref[...])
        kn, kr, _ = _rms(lv[:, _KV0:_KR0], kvg_ref[...])
        dcq = dcq_ref[...]
        dckv = dckv_ref[...]
        dqg_ref[...] += _colsum(dcq * qn)
        dkvg_ref[...] += _colsum(dckv * kn)
        dlat_ref[:, :_KV0] = _rms_bwd(dcq, qn, qr, qg_ref[...]).astype(BF16)
        dlat_ref[:, _KV0:_KR0] = _rms_bwd(dckv, kn, kr, kvg_ref[...]).astype(BF16)
        dlat_ref[:, _KR0:] = _rope_bwd(dkr_ref[...], cc_ref[...], sa_ref[...], sb_ref[...]).astype(BF16)

    return pl.pallas_call(
        body, name=name, grid=(T // tr,),
        in_specs=[_row_spec(tr, MLA_LATP), _row_spec(tr, MLA_QL), _row_spec(tr, MLA_KVL), _row_spec(tr, 128),
                  _vec_spec(MLA_QL), _vec_spec(MLA_KVL), _row_spec(tr, 128), _row_spec(tr, 128), _row_spec(tr, 128)],
        out_specs=[_row_spec(tr, MLA_LATP), _vec_spec(MLA_QL), _vec_spec(MLA_KVL)],
        out_shape=[jax.ShapeDtypeStruct((T, MLA_LATP), BF16), jax.ShapeDtypeStruct((1, MLA_QL), F32), jax.ShapeDtypeStruct((1, MLA_KVL), F32)],
        compiler_params=pltpu.CompilerParams(dimension_semantics=("arbitrary",)),
    )(lat, dcqn, dckvn, dkrot, qg, kvg, cc, sa, sb)


def _epi_q_rope(acc, cc, sa, sb):
    out = []
    for hh in range(acc.shape[1] // MLA_HP):
        a, m, b = hh * MLA_HP, hh * MLA_HP + MLA_NOPE, (hh + 1) * MLA_HP
        out += [acc[:, a:m] * SM_SCALE, _rope_fwd(acc[:, m:b], cc, sa, sb) * SM_SCALE]
    return (jnp.concatenate(out, axis=1),)


def _mla_ukv(ckvn, w_ukv, krot, name):
    T = ckvn.shape[0]
    tr = min(T, ATT_TILE)
    hg = ATT_HG
    gw = hg * MLA_HP

    def body(a_ref, w_ref, kr_ref, ko_ref, kt_ref, vo_ref, vt_ref):
        acc = jnp.dot(a_ref[...], w_ref[...], preferred_element_type=F32)
        kr = kr_ref[...]
        krb, krt = kr.astype(BF16), kr.T.astype(BF16)
        for hh in range(hg):
            a, m, b = hh * MLA_HP, hh * MLA_HP + MLA_NOPE, (hh + 1) * MLA_HP
            kn, vh = acc[:, a:m], acc[:, m:b]
            ko_ref[:, a:m] = kn.astype(BF16)
            ko_ref[:, m:b] = krb
            kt_ref[a:m, :] = kn.T.astype(BF16)
            kt_ref[m:b, :] = krt
            vo_ref[:, hh * MLA_V:(hh + 1) * MLA_V] = vh.astype(BF16)
            vt_ref[hh] = vh.T.astype(BF16)

    tk = min(T, ATT_TILE)
    per = tk // tr
    HW = MLA_H * MLA_HP
    return pl.pallas_call(
        body, name=name, grid=(T // tr, MLA_H // hg),
        in_specs=[pl.BlockSpec((tr, MLA_KVL), lambda i, g: (i, 0)), pl.BlockSpec((MLA_KVL, gw), lambda i, g: (0, g)),
                  pl.BlockSpec((tr, 128), lambda i, g: (i, 0))],
        out_specs=[pl.BlockSpec((tr, gw), lambda i, g: (i, g)), pl.BlockSpec((gw, tr), lambda i, g: (g, i)),
                   pl.BlockSpec((tr, hg * MLA_V), lambda i, g: (i, g)),
                   pl.BlockSpec((hg, None, MLA_V, tr), lambda i, g: (g, i // per, 0, i % per))],
        out_shape=[jax.ShapeDtypeStruct((T, HW), BF16), jax.ShapeDtypeStruct((HW, T), BF16), jax.ShapeDtypeStruct((T, MLA_H * MLA_V), BF16),
                   jax.ShapeDtypeStruct((MLA_H, T // tk, MLA_V, tk), BF16)],
        compiler_params=pltpu.CompilerParams(dimension_semantics=("parallel", "parallel")),
    )(ckvn, w_ukv, krot)


ATT_HG = 4


def _mla_prep_bwd(dqt, dkr, cc, sa, sb, name):
    _, nq, _, tq = dqt.shape
    T = nq * tq
    gw = ATT_HG * MLA_HP

    def body(dq_ref, dk_ref, cc_ref, sa_ref, sb_ref, dqp_ref, dkr_ref):
        @pl.when(pl.program_id(1) == 0)
        def _():
            dkr_ref[...] = jnp.zeros_like(dkr_ref)

        cc, sa, sb = cc_ref[...], sa_ref[...], sb_ref[...]
        acc = jnp.zeros((tq, 128), F32)
        for hh in range(ATT_HG):
            a, m, b = hh * MLA_HP, hh * MLA_HP + MLA_NOPE, (hh + 1) * MLA_HP
            dqh = dq_ref[hh].astype(F32).T * SM_SCALE
            dqp_ref[:, a:m] = dqh[:, :MLA_NOPE].astype(BF16)
            dqp_ref[:, m:b] = _rope_bwd(dqh[:, MLA_NOPE:], cc, sa, sb).astype(BF16)
            acc = acc + dk_ref[:, hh * 128:(hh + 1) * 128].astype(F32)
        dkr_ref[...] += acc

    tab = pl.BlockSpec((tq, 128), lambda i, g: (i, 0))
    return pl.pallas_call(
        body, name=name, grid=(nq, MLA_H // ATT_HG),
        in_specs=[pl.BlockSpec((ATT_HG, None, MLA_HP, tq), lambda i, g: (g, i, 0, 0)), pl.BlockSpec((tq, ATT_HG * 128), lambda i, g: (i, g)),
                  tab, tab, tab],
        out_specs=[pl.BlockSpec((tq, gw), lambda i, g: (i, g)), tab],
        out_shape=[jax.ShapeDtypeStruct((T, MLA_H * MLA_HP), BF16), jax.ShapeDtypeStruct((T, 128), F32)],
        compiler_params=pltpu.CompilerParams(dimension_semantics=("parallel", "arbitrary")),
    )(dqt, dkr, cc, sa, sb)


_NT = (((1,), (1,)), ((), ()))


def _as_row(col, n):
    return jnp.broadcast_to(col, (n, 128)).T[0:1, :]


def _attn_fwd(q, k, vt, name):
    T = q.shape[0]
    tq = tk = min(T, ATT_TILE)
    nq = T // tq
    hg = ATT_FWD_HEADS

    def body(q_ref, k_ref, vt_ref, o_ref, lse_ref, m_ref, l_ref, acc_ref):
        i = pl.program_id(1)
        m_ref[...] = jnp.full_like(m_ref, NEG_INF)
        l_ref[...] = jnp.zeros_like(l_ref)
        acc_ref[...] = jnp.zeros_like(acc_ref)

        def step(j, diag):
            off = pl.multiple_of(j * tk, tk)
            sts = [lax.dot_general(k_ref[pl.ds(off, tk), hh * MLA_HP:(hh + 1) * MLA_HP], q_ref[:, hh * MLA_HP:(hh + 1) * MLA_HP], _NT,
                                   preferred_element_type=F32) for hh in range(hg)]
            for hh in range(hg):
                st = sts[hh]
                if diag:
                    st = jnp.where(lax.broadcasted_iota(jnp.int32, (tk, tq), 0) <= lax.broadcasted_iota(jnp.int32, (tk, tq), 1), st, NEG_INF)
                m_prev = m_ref[hh]
                m_new = jnp.maximum(m_prev, jnp.max(st, axis=0, keepdims=True))
                alpha = jnp.exp(m_prev - m_new)
                pt = jnp.exp(st - m_new)
                l_ref[hh] = alpha * l_ref[hh] + jnp.sum(pt, axis=0, keepdims=True)
                acc_ref[hh] = alpha * acc_ref[hh] + jnp.dot(vt_ref[hh, j], pt.astype(BF16), preferred_element_type=F32)
                m_ref[hh] = m_new

        def loop_body(j, carry):
            step(j, False)
            return carry

        lax.fori_loop(0, i, loop_body, 0)
        step(i, True)
        for hh in range(hg):
            o_ref[:, hh * MLA_V:(hh + 1) * MLA_V] = (acc_ref[hh] / l_ref[hh]).T.astype(BF16)
            lse_ref[hh] = m_ref[hh] + jnp.log(l_ref[hh])

    return pl.pallas_call(
        body, name=name, grid=(MLA_H // hg, nq),
        in_specs=[pl.BlockSpec((tq, hg * MLA_HP), lambda h, i: (i, h)), pl.BlockSpec((T, hg * MLA_HP), lambda h, i: (0, h)),
                  pl.BlockSpec((hg, nq, MLA_V, tk), lambda h, i: (h, 0, 0, 0))],
        out_specs=[pl.BlockSpec((tq, hg * MLA_V), lambda h, i: (i, h)), pl.BlockSpec((hg, None, 1, tq), lambda h, i: (h, i, 0, 0))],
        out_shape=[jax.ShapeDtypeStruct((T, MLA_H * MLA_V), BF16), jax.ShapeDtypeStruct((MLA_H, nq, 1, tq), F32)],
        scratch_shapes=[pltpu.VMEM((hg, 1, tq), F32), pltpu.VMEM((hg, 1, tq), F32), pltpu.VMEM((hg, MLA_V, tq), F32)],
        compiler_params=pltpu.CompilerParams(dimension_semantics=("parallel", "arbitrary")),
    )(q, k, vt)


def _attn_delta(do, o, name):
    T = do.shape[0]
    tq = min(T, ATT_TILE)

    def body(do_ref, o_ref, d_ref):
        for hh in range(MLA_H):
            cs = slice(hh * MLA_V, (hh + 1) * MLA_V)
            s = jnp.sum(do_ref[:, cs].astype(F32) * o_ref[:, cs].astype(F32), axis=-1, keepdims=True)
            d_ref[hh] = _as_row(s, tq)

    return pl.pallas_call(
        body, name=name, grid=(T // tq,),
        in_specs=[_row_spec(tq, MLA_H * MLA_V), _row_spec(tq, MLA_H * MLA_V)],
        out_specs=pl.BlockSpec((MLA_H, None, 1, tq), lambda i: (0, i, 0, 0)),
        out_shape=jax.ShapeDtypeStruct((MLA_H, T // tq, 1, tq), F32),
        compiler_params=pltpu.CompilerParams(dimension_semantics=("parallel",)),
    )(do, o)


def _attn_bwd(q, k, kt, v, do, lse, delta, name):
    T = q.shape[0]
    tq = tk = min(T, ATT_TILE)
    nq = nk = T // tq
    tsd = min(tq, ATT_SUB)
    hg = ATT_BWD_HEADS

    def body(q_ref, k_ref, kt_ref, v_ref, do_ref, lse_ref, dl_ref, dqt_ref, dkv_ref, dkr_ref, dq_acc, dk_acc, dv_acc):
        j = pl.program_id(1)

        @pl.when(j == 0)
        def _():
            dq_acc[...] = jnp.zeros_like(dq_acc)

        dk_acc[...] = jnp.zeros_like(dk_acc)
        dv_acc[...] = jnp.zeros_like(dv_acc)

        def step(i, diag):
            off = pl.multiple_of(i * tq, tq)
            ts, nsub = (tsd, tq // tsd) if diag else (tq, 1)
            for u in range(nsub):
                cols = slice(u * ts, (u + 1) * ts)
                nk_u = (u + 1) * ts if diag else tk
                rows = pl.ds(off + u * ts, ts)
                pre = []
                for hh in range(hg):
                    hq, hv = slice(hh * MLA_HP, (hh + 1) * MLA_HP), slice(hh * MLA_V, (hh + 1) * MLA_V)
                    qi, doi = q_ref[rows, hq], do_ref[rows, hv]
                    st = lax.dot_general(k_ref[:nk_u, hq], qi, _NT, preferred_element_type=F32)
                    dpt = lax.dot_general(v_ref[:nk_u, hv], doi, _NT, preferred_element_type=F32)
                    pre.append((qi, doi, st, dpt))
                for hh in range(hg):
                    hq, hv = slice(hh * MLA_HP, (hh + 1) * MLA_HP), slice(hh * MLA_V, (hh + 1) * MLA_V)
                    qi, doi, st, dpt = pre[hh]
                    if diag:
                        qcol = u * ts + lax.broadcasted_iota(jnp.int32, (nk_u, ts), 1)
                        st = jnp.where(lax.broadcasted_iota(jnp.int32, (nk_u, ts), 0) <= qcol, st, NEG_INF)
                    pt = jnp.exp(st - lse_ref[hh, i][:, cols])
                    dv_acc[:nk_u, hv] += jnp.dot(pt.astype(BF16), doi, preferred_element_type=F32)
                    dsb = (pt * (dpt - dl_ref[hh, i][:, cols])).astype(BF16)
                    dk_acc[:nk_u, hq] += jnp.dot(dsb, qi, preferred_element_type=F32)
                    dq_acc[hh, i, :, cols] += jnp.dot(kt_ref[hq, :nk_u], dsb, preferred_element_type=F32)

        def loop_body(i, carry):
            step(i, False)
            return carry

        step(j, True)
        lax.fori_loop(j + 1, nq, loop_body, 0)
        for hh in range(hg):
            a, m, b = hh * MLA_HP, hh * MLA_HP + MLA_NOPE, (hh + 1) * MLA_HP
            dkv_ref[:, a:m] = dk_acc[:, a:m].astype(BF16)
            dkv_ref[:, m:b] = dv_acc[:, hh * MLA_V:(hh + 1) * MLA_V].astype(BF16)
            dkr_ref[:, hh * 128:(hh + 1) * 128] = dk_acc[:, m:b].astype(BF16)

        @pl.when(j == nk - 1)
        def _():
            dqt_ref[...] = dq_acc[...].astype(BF16)

    stat = pl.BlockSpec((hg, nq, 1, tq), lambda h, j: (h, 0, 0, 0))
    return pl.pallas_call(
        body, name=name, grid=(MLA_H // hg, nk),
        in_specs=[pl.BlockSpec((T, hg * MLA_HP), lambda h, j: (0, h)), pl.BlockSpec((tk, hg * MLA_HP), lambda h, j: (j, h)),
                  pl.BlockSpec((hg * MLA_HP, tk), lambda h, j: (h, j)), pl.BlockSpec((tk, hg * MLA_V), lambda h, j: (j, h)),
                  pl.BlockSpec((T, hg * MLA_V), lambda h, j: (0, h)), stat, stat],
        out_specs=[pl.BlockSpec((hg, nq, MLA_HP, tq), lambda h, j: (h, 0, 0, 0)), pl.BlockSpec((tk, hg * MLA_HP), lambda h, j: (j, h)),
                   pl.BlockSpec((tk, hg * 128), lambda h, j: (j, h))],
        out_shape=[jax.ShapeDtypeStruct((MLA_H, nq, MLA_HP, tq), BF16), jax.ShapeDtypeStruct((T, MLA_H * MLA_HP), BF16),
                   jax.ShapeDtypeStruct((T, MLA_H * 128), BF16)],
        scratch_shapes=[pltpu.VMEM((hg, nq, MLA_HP, tq), F32), pltpu.VMEM((tk, hg * MLA_HP), F32), pltpu.VMEM((tk, hg * MLA_V), F32)],
        compiler_params=pltpu.CompilerParams(dimension_semantics=("parallel", "arbitrary")),
    )(q, k, kt, v, do, lse, delta)


ADA_TN = 512


def _silu(v):
    return v * (1.0 / (1.0 + jnp.exp(-v)))


def _ada_fwd(c_all, ada_w, ada_b_loc, name):
    L, D, Nc = ada_w.shape
    B = c_all.shape[0]

    def body(c_ref, w_ref, b_ref, o_ref):
        ca = _silu(c_ref[...]).astype(BF16)
        o_ref[...] = jnp.dot(ca, w_ref[...].astype(BF16), preferred_element_type=F32) + b_ref[...]

    return pl.pallas_call(
        body, name=name, grid=(L, Nc // ADA_TN),
        in_specs=[pl.BlockSpec((B, D), lambda l, n: (0, 0)), pl.BlockSpec((None, D, ADA_TN), lambda l, n: (l, 0, n)),
                  pl.BlockSpec((None, 1, ADA_TN), lambda l, n: (l, 0, n))],
        out_specs=pl.BlockSpec((None, B, ADA_TN), lambda l, n: (l, 0, n)),
        out_shape=jax.ShapeDtypeStruct((L, B, Nc), F32),
        compiler_params=pltpu.CompilerParams(dimension_semantics=("parallel", "parallel")),
    )(c_all, ada_w, ada_b_loc)


def _ada_bwd(c_all_t, dmod_loc, name):
    D, B = c_all_t.shape
    L, _, Nc = dmod_loc.shape

    def body(c_ref, d_ref, o_ref):
        ca = _silu(c_ref[...])
        dv = d_ref[...]
        acc = ca[:, 0:1] * dv[0:1, :]
        for b in range(1, B):
            acc = acc + ca[:, b:b + 1] * dv[b:b + 1, :]
        o_ref[...] = acc

    return pl.pallas_call(
        body, name=name, grid=(L, Nc // ADA_TN),
        in_specs=[pl.BlockSpec((D, B), lambda l, n: (0, 0)), pl.BlockSpec((None, B, ADA_TN), lambda l, n: (l, 0, n))],
        out_specs=pl.BlockSpec((None, D, ADA_TN), lambda l, n: (l, 0, n)),
        out_shape=jax.ShapeDtypeStruct((L, D, Nc), F32),
        compiler_params=pltpu.CompilerParams(dimension_semantics=("parallel", "parallel")),
    )(c_all_t, dmod_loc)


def _sum_lead(parts, name, out_dtype=F32):
    R, C = parts[0].shape[1:]
    n_tot = sum(p.shape[0] for p in parts)
    tr = R
    for cand in (512, 256, 128, 64, 32, 16):
        if R % cand == 0 and cand * C * 4 * n_tot <= (8 << 20):
            tr = cand
            break

    def body(*refs):
        o_ref = refs[-1]
        acc = None
        for r in refs[:-1]:
            for s in range(r.shape[0]):
                acc = r[s].astype(F32) if acc is None else acc + r[s].astype(F32)
        o_ref[...] = acc.astype(o_ref.dtype)

    return pl.pallas_call(
        body, name=name, grid=(R // tr,),
        in_specs=[pl.BlockSpec((p.shape[0], tr, C), lambda i: (0, i, 0)) for p in parts],
        out_specs=pl.BlockSpec((tr, C), lambda i: (i, 0)),
        out_shape=jax.ShapeDtypeStruct((R, C), out_dtype),
        compiler_params=pltpu.CompilerParams(dimension_semantics=("parallel",)),
    )(*parts)


_ADAM_C1 = 1.0 - ADAM_B1 ** ADAM_STEP
_ADAM_C2 = 1.0 - ADAM_B2 ** ADAM_STEP


def _adamw(w, g, m, v, name):
    shape = w.shape
    C = shape[-1]
    R = math.prod(shape[:-1]) if len(shape) > 1 else 1
    w2, g2, m2, v2 = (a.reshape(R, C) for a in (w, g, m, v))
    tr = R
    for cand in (1024, 512, 256, 128, 64, 32, 16, 8):
        if R % cand == 0 and cand * C * 4 <= (1 << 20):
            tr = cand
            break

    def body(w_ref, g_ref, m_ref, v_ref, d_ref, nm_ref, nv_ref):
        gv = g_ref[...]
        mn = ADAM_B1 * m_ref[...] + (1.0 - ADAM_B1) * gv
        vn = ADAM_B2 * v_ref[...] + (1.0 - ADAM_B2) * (gv * gv)
        nm_ref[...] = mn
        nv_ref[...] = vn
        m_hat = mn / _ADAM_C1
        v_hat = vn / _ADAM_C2
        d_ref[...] = -ADAM_LR * (m_hat / (jnp.sqrt(v_hat) + ADAM_EPS) + ADAM_WD * w_ref[...])

    spec = pl.BlockSpec((tr, C), lambda i: (i, 0))
    outs = pl.pallas_call(
        body, name=name, grid=(R // tr,),
        in_specs=[spec] * 4, out_specs=[spec] * 3,
        out_shape=[jax.ShapeDtypeStruct((R, C), F32)] * 3,
        compiler_params=pltpu.CompilerParams(dimension_semantics=("parallel",)),
    )(w2, g2, m2, v2)
    return tuple(o.reshape(shape) for o in outs)


def _row_tile(rows, cols, itemsize, budget):
    for cand in (1024, 512, 256, 128, 64, 32, 16):
        if rows % cand == 0 and cand * cols * itemsize <= budget:
            return cand
    return rows


def _sum_sel(sel, stacked, others, name, out_dtype):
    R, C = stacked.shape[1:]
    n_tot = 1 + sum(o.shape[0] for o in others)
    tr = _row_tile(R, C, 4 * n_tot, 8 << 20)

    def body(sel_ref, s_ref, *refs):
        o_ref = refs[-1]
        acc = s_ref[...].astype(F32)
        for r in refs[:-1]:
            for s in range(r.shape[0]):
                acc = acc + r[s].astype(F32)
        o_ref[...] = acc.astype(o_ref.dtype)

    return pl.pallas_call(
        body, name=name,
        grid_spec=pltpu.PrefetchScalarGridSpec(
            num_scalar_prefetch=1, grid=(R // tr,),
            in_specs=[pl.BlockSpec((None, tr, C), lambda i, s: (s[0], i, 0))] + [pl.BlockSpec((o.shape[0], tr, C), lambda i, s: (0, i, 0)) for o in others],
            out_specs=pl.BlockSpec((tr, C), lambda i, s: (i, 0))),
        out_shape=jax.ShapeDtypeStruct((R, C), out_dtype),
        compiler_params=pltpu.CompilerParams(dimension_semantics=("parallel",)),
    )(sel, stacked, *others)


def _adamw_piece(cidx, w2, m2, v2, mine, got, bufs, row0, name):
    hr, C = mine.shape
    tr = _row_tile(math.gcd(hr, row0) if row0 else hr, C, 4, 1 << 20)
    nt = hr // tr

    def body(c_ref, w_ref, m_ref, v_ref, a_ref, b_ref, _g, _d, _nm, _nv, g_ref, d_ref, nm_ref, nv_ref):
        gv = jnp.where(pl.program_id(0) == c_ref[0], a_ref[...], b_ref[...])
        mn = ADAM_B1 * m_ref[...] + (1.0 - ADAM_B1) * gv
        vn = ADAM_B2 * v_ref[...] + (1.0 - ADAM_B2) * (gv * gv)
        g_ref[...] = gv
        nm_ref[...] = mn
        nv_ref[...] = vn
        d_ref[...] = -ADAM_LR * ((mn / _ADAM_C1) / (jnp.sqrt(vn / _ADAM_C2) + ADAM_EPS) + ADAM_WD * w_ref[...])

    rows = pl.BlockSpec((tr, C), lambda hf, t, c: (row0 // tr + hf * nt + t, 0))
    mine_spec = pl.BlockSpec((tr, C), lambda hf, t, c: (jnp.where(hf == c[0], t, 0), 0))
    got_spec = pl.BlockSpec((tr, C), lambda hf, t, c: (jnp.where(hf == c[0], 0, t), 0))
    return pl.pallas_call(
        body, name=name,
        grid_spec=pltpu.PrefetchScalarGridSpec(num_scalar_prefetch=1, grid=(2, nt), in_specs=[rows] * 3 + [mine_spec, got_spec] + [_ANY_SPEC] * 4,
                                               out_specs=[rows] * 4),
        out_shape=[jax.ShapeDtypeStruct(w2.shape, F32)] * 4,
        input_output_aliases={6 + n: n for n in range(4)},
        compiler_params=pltpu.CompilerParams(dimension_semantics=("parallel", "parallel")),
    )(cidx, w2, m2, v2, mine, got, *bufs)


_VMEM_SPEC = pl.BlockSpec(memory_space=pltpu.VMEM)
_HBM_SPEC = pl.BlockSpec(memory_space=pltpu.HBM)


def _flip(v, bit):
    return (1 - v) if bit else v


def _allgather8(v, name):
    def body(v_ref, out_ref, send_sems, recv_sems, local_sem):
        x, y, c = _idx()
        me = 4 * x + 2 * y + c
        mine = pltpu.make_async_copy(v_ref, out_ref.at[me], local_sem)
        mine.start()
        sends = []
        for k in range(1, N_DEV):
            peer = (_flip(x, k & 4), _flip(y, k & 2), _flip(c, k & 1))
            cp = pltpu.make_async_remote_copy(src_ref=v_ref, dst_ref=out_ref.at[me], send_sem=send_sems.at[k - 1], recv_sem=recv_sems.at[k - 1],
                                              device_id=peer, device_id_type=MESH)
            cp.start()
            sends.append(cp)
        for k in range(1, N_DEV):
            px, py, pc = _flip(x, k & 4), _flip(y, k & 2), _flip(c, k & 1)
            src = 4 * px + 2 * py + pc
            pltpu.make_async_remote_copy(src_ref=v_ref, dst_ref=out_ref.at[src], send_sem=send_sems.at[k - 1], recv_sem=recv_sems.at[k - 1],
                                         device_id=(px, py, pc), device_id_type=MESH).wait_recv()
        for cp in sends:
            cp.wait_send()
        mine.wait()

    return pl.pallas_call(
        body, name=name,
        out_shape=jax.ShapeDtypeStruct((N_DEV, *v.shape), v.dtype),
        in_specs=[_VMEM_SPEC], out_specs=_VMEM_SPEC,
        scratch_shapes=[pltpu.SemaphoreType.DMA((N_DEV - 1,)), pltpu.SemaphoreType.DMA((N_DEV - 1,)), pltpu.SemaphoreType.DMA],
    )(v)


def _mod_exchange(modp, name):
    _, L, Nc = modp.shape

    def body(p_ref, out_ref, send_sems, recv_sems, local_sem):
        x, y, c = _idx()
        me, chip = 4 * x + 2 * y + c, 2 * x + y
        mine = pltpu.make_async_copy(p_ref.at[me], out_ref.at[chip], local_sem)
        mine.start()
        sends = []
        for k in range(1, N_CHIPS):
            px, py = _flip(x, k & 2), _flip(y, k & 1)
            cp = pltpu.make_async_remote_copy(src_ref=p_ref.at[4 * px + 2 * py + c], dst_ref=out_ref.at[chip],
                                              send_sem=send_sems.at[k - 1], recv_sem=recv_sems.at[k - 1], device_id=(px, py, c), device_id_type=MESH)
            cp.start()
            sends.append(cp)
        for k in range(1, N_CHIPS):
            px, py = _flip(x, k & 2), _flip(y, k & 1)
            pltpu.make_async_remote_copy(src_ref=p_ref.at[me], dst_ref=out_ref.at[2 * px + py], send_sem=send_sems.at[k - 1],
                                         recv_sem=recv_sems.at[k - 1], device_id=(px, py, c), device_id_type=MESH).wait_recv()
        for cp in sends:
            cp.wait_send()
        mine.wait()

    return pl.pallas_call(
        body, name=name,
        out_shape=jax.ShapeDtypeStruct((N_CHIPS, L, Nc), modp.dtype),
        in_specs=[_VMEM_SPEC], out_specs=_VMEM_SPEC,
        scratch_shapes=[pltpu.SemaphoreType.DMA((N_CHIPS - 1,)), pltpu.SemaphoreType.DMA((N_CHIPS - 1,)), pltpu.SemaphoreType.DMA],
    )(modp)


_SEM_SPEC = pl.BlockSpec(memory_space=pltpu.SEMAPHORE)
_ANY_SPEC = pl.BlockSpec(memory_space=pl.ANY)
_EFFECT = pltpu.SideEffectType.DATAFLOW_SIDE_EFFECTING


def _hbm(a):
    return pltpu.with_memory_space_constraint(a, pltpu.HBM)


def _xchip_copies(mode, srcs, lands, send_sems, recv_sems, waiting):
    x, y, c = _idx()
    chip = 2 * x + y
    out = []
    for a in range(len(srcs)):
        for k in range(1, _n_peers(mode) + 1):
            if mode == "all8":
                px, py, pc = _flip(x, k & 4), _flip(y, k & 2), _flip(c, k & 1)
                src, dst, mine = srcs[a], lands[a].at[4 * x + 2 * y + c], lands[a].at[4 * px + 2 * py + pc]
            else:
                px, py, pc = _flip(x, k & 2), _flip(y, k & 1), c
                peer = 2 * px + py
                if mode == "gather":
                    src, dst, mine = srcs[a].at[c], lands[a].at[chip, c], lands[a].at[peer, c]
                else:
                    src, dst, mine = srcs[a].at[peer], lands[a].at[k - 1], lands[a].at[k - 1]
            q = a * _n_peers(mode) + k - 1
            out.append(pltpu.make_async_remote_copy(src_ref=src, dst_ref=mine if waiting else dst, send_sem=send_sems[q], recv_sem=recv_sems[q],
                                                    device_id=(px, py, pc), device_id_type=MESH))
    return out


def _n_peers(mode):
    return N_DEV - 1 if mode == "all8" else N_CHIPS - 1


def _xchip_start(mode, srcs, land_shapes, dep, name):
    n = len(srcs)
    ns = n * _n_peers(mode)

    def body(*refs):
        src_refs, land_refs = refs[:n], refs[n:2 * n]
        outs = refs[2 * n + 1:]
        for cp in _xchip_copies(mode, src_refs, land_refs, outs[:ns], outs[ns:2 * ns], waiting=False):
            cp.start()
        outs[-1][...] = jnp.zeros_like(outs[-1])

    lands = [_hbm(lax.empty(s.shape, s.dtype)) for s in land_shapes]
    outs = pl.pallas_call(
        body, name=name,
        out_shape=(*[pltpu.SemaphoreType.DMA(())] * (2 * ns), *[pltpu.HBM(s.shape, s.dtype) for s in srcs],
                   *[pltpu.HBM(s.shape, s.dtype) for s in land_shapes], jax.ShapeDtypeStruct((8, 128), F32)),
        in_specs=[_HBM_SPEC] * (2 * n) + [_ANY_SPEC],
        out_specs=(*[_SEM_SPEC] * (2 * ns), *[_HBM_SPEC] * (2 * n), _VMEM_SPEC),
        input_output_aliases={i: 2 * ns + i for i in range(2 * n)},
        compiler_params=pltpu.CompilerParams(has_side_effects=_EFFECT),
    )(*[_hbm(s) for s in srcs], *lands, dep)
    return list(outs[:ns]), list(outs[ns:2 * ns]), list(outs[2 * ns:2 * ns + n]), list(outs[2 * ns + n:2 * ns + 2 * n]), outs[-1]


def _xchip_wait(mode, send_sems, recv_sems, srcs, lands, after, name):
    n = len(srcs)
    ns = n * _n_peers(mode)

    def body(*refs):
        src_refs, land_refs = refs[:n], refs[n:2 * n]
        sems = refs[2 * n:2 * n + 2 * ns]
        for cp in _xchip_copies(mode, src_refs, land_refs, sems[:ns], sems[ns:], waiting=True):
            cp.wait_send()
            cp.wait_recv()

    outs = pl.pallas_call(
        body, name=name,
        out_shape=(*[pltpu.HBM(s.shape, s.dtype) for s in srcs], *[pltpu.HBM(s.shape, s.dtype) for s in lands]),
        in_specs=[_HBM_SPEC] * (2 * n) + [_SEM_SPEC] * (2 * ns) + [_ANY_SPEC] * len(after),
        out_specs=tuple([_HBM_SPEC] * (2 * n)),
        input_output_aliases={i: i for i in range(2 * n)},
        compiler_params=pltpu.CompilerParams(has_side_effects=_EFFECT),
    )(*srcs, *lands, *send_sems, *recv_sems, *after)
    return list(outs[:n]), list(outs[n:])


def _sibling_fwd(lands, name):
    n = len(lands)

    def body(*refs):
        outs = refs[n:2 * n]
        send_sems, recv_sems = refs[2 * n:]
        x, y, c = _idx()
        sib = (x, y, 1 - c)
        sends = []
        for a in range(n):
            for k in range(1, N_CHIPS):
                src = 2 * _flip(x, k & 2) + _flip(y, k & 1)
                cp = pltpu.make_async_remote_copy(src_ref=outs[a].at[src, c], dst_ref=outs[a].at[src, c], send_sem=send_sems.at[a, k - 1],
                                                  recv_sem=recv_sems.at[a, k - 1], device_id=sib, device_id_type=MESH)
                cp.start()
                sends.append(cp)
        for a in range(n):
            for k in range(1, N_CHIPS):
                src = 2 * _flip(x, k & 2) + _flip(y, k & 1)
                pltpu.make_async_remote_copy(src_ref=outs[a].at[src, c], dst_ref=outs[a].at[src, 1 - c], send_sem=send_sems.at[a, k - 1],
                                             recv_sem=recv_sems.at[a, k - 1], device_id=sib, device_id_type=MESH).wait_recv()
        for cp in sends:
            cp.wait_send()

    return pl.pallas_call(
        body, name=name,
        out_shape=[jax.ShapeDtypeStruct(s.shape, s.dtype) for s in lands],
        in_specs=[_HBM_SPEC] * n, out_specs=[_HBM_SPEC] * n,
        input_output_aliases={i: i for i in range(n)},
        scratch_shapes=[pltpu.SemaphoreType.DMA((n, N_CHIPS - 1)), pltpu.SemaphoreType.DMA((n, N_CHIPS - 1))],
    )(*lands)


def _sibling_swap(parts, name):
    n = len(parts)

    def body(*refs):
        ins, outs = refs[:n], refs[n:2 * n]
        send_sems, recv_sems = refs[2 * n:]
        x, y, c = _idx()
        cps = []
        for a in range(n):
            cp = pltpu.make_async_remote_copy(src_ref=ins[a].at[1 - c], dst_ref=outs[a], send_sem=send_sems.at[a], recv_sem=recv_sems.at[a],
                                              device_id=(x, y, 1 - c), device_id_type=MESH)
            cp.start()
            cps.append(cp)
        for cp in cps:
            cp.wait()

    return pl.pallas_call(
        body, name=name,
        out_shape=[jax.ShapeDtypeStruct(p.shape[1:], p.dtype) for p in parts],
        in_specs=[_HBM_SPEC] * n, out_specs=[_HBM_SPEC] * n,
        scratch_shapes=[pltpu.SemaphoreType.DMA((n,)), pltpu.SemaphoreType.DMA((n,))],
    )(*parts)


def _sibling_send(halves, name):
    n = len(halves)

    def body(*refs):
        ins, outs = refs[:n], refs[n:2 * n]
        send_sems, recv_sems = refs[2 * n:]
        x, y, c = _idx()
        cps = []
        for a in range(n):
            cp = pltpu.make_async_remote_copy(src_ref=ins[a], dst_ref=outs[a], send_sem=send_sems.at[a], recv_sem=recv_sems.at[a],
                                              device_id=(x, y, 1 - c), device_id_type=MESH)
            cp.start()
            cps.append(cp)
        for cp in cps:
            cp.wait()

    return pl.pallas_call(
        body, name=name,
        out_shape=[jax.ShapeDtypeStruct(h.shape, h.dtype) for h in halves],
        in_specs=[_HBM_SPEC] * n, out_specs=[_HBM_SPEC] * n,
        scratch_shapes=[pltpu.SemaphoreType.DMA((n,)), pltpu.SemaphoreType.DMA((n,))],
    )(*halves)


def _col_full(g):
    k, n = g.shape[1], g.shape[2]
    return g.transpose(1, 0, 2).reshape(k, N_CHIPS * n)


def _col_blocks(w):
    k, n = w.shape
    return w.reshape(k, N_CHIPS, n // N_CHIPS).transpose(1, 0, 2)


def _row_blocks(w):
    k, n = w.shape
    return w.reshape(N_CHIPS, k // N_CHIPS, n)


_UQ_HEAD = MLA_NOPE + MLA_ROPE

_LAT = MLA_QL + MLA_KVL + MLA_ROPE
_POOL_R = len(POOL_WINDOWS) * (POOL_GD // N_CHIPS)

_PIECE_KINDS = {
    "mlp_w1": (D_MODEL, D_MODEL, lambda g: g, _col_blocks),
    "mlp_w2": (D_MODEL, D_MODEL, lambda g: g.reshape(4 * D_MODEL, D_MODEL), _row_blocks),
    "pool_w": (_POOL_R, POOL_GD,
               lambda g: g.reshape(N_CHIPS, len(POOL_WINDOWS), POOL_GD // N_CHIPS, POOL_GD).transpose(1, 0, 2, 3).reshape(len(POOL_WINDOWS), POOL_GD, POOL_GD),
               lambda w: w.reshape(len(POOL_WINDOWS), N_CHIPS, POOL_GD // N_CHIPS, POOL_GD).transpose(1, 0, 2, 3).reshape(N_CHIPS, _POOL_R, POOL_GD)),
    "sgu_w_in": (D_MODEL, 2 * SGU_W // N_CHIPS, _col_full, _col_blocks),
    "sgu_w_out": (SGU_W // N_CHIPS, D_MODEL, lambda g: g.reshape(SGU_W, D_MODEL), _row_blocks),
    "mla_w_dq_dkv": (D_MODEL // N_CHIPS, _LAT, lambda g: jnp.pad(g.reshape(D_MODEL, _LAT), ((0, 0), (0, MLA_LATP - _LAT))),
                     lambda w: _row_blocks(w[:, :_LAT])),
    "mla_w_uq": (MLA_QL, MLA_H * _UQ_HEAD // N_CHIPS,
                 lambda g: jnp.pad(_col_full(g).reshape(MLA_QL, MLA_H, _UQ_HEAD), ((0, 0), (0, 0), (0, MLA_HP - _UQ_HEAD))).reshape(MLA_QL, MLA_H * MLA_HP),
                 lambda w: _col_blocks(w.reshape(MLA_QL, MLA_H, MLA_HP)[:, :, :_UQ_HEAD].reshape(MLA_QL, MLA_H * _UQ_HEAD))),
    "mla_w_ukv": (MLA_KVL, MLA_H * (MLA_NOPE + MLA_V) // N_CHIPS, _col_full, _col_blocks),
    "mla_w_o": (MLA_H * MLA_V // N_CHIPS, D_MODEL, lambda g: g.reshape(MLA_H * MLA_V, D_MODEL), _row_blocks),
}
_MIXER_KINDS = (("pool_w",), ("sgu_w_in", "sgu_w_out"), ("mla_w_dq_dkv", "mla_w_uq", "mla_w_ukv", "mla_w_o"))


def _layer_pieces(i):
    return [(k, i // N_MIXERS) for k in _MIXER_KINDS[i % N_MIXERS]] + [("mlp_w1", i), ("mlp_w2", i)]


def _rope_tables(positions):
    inv_freq = ROPE_THETA ** (-jnp.arange(0, MLA_ROPE, 2, dtype=F32) / MLA_ROPE)
    ang = positions.astype(F32)[:, None] * inv_freq
    cos, sin = jnp.cos(ang), jnp.sin(ang)
    z32, z64 = jnp.zeros_like(cos), jnp.zeros((positions.shape[0], 64), F32)
    return (jnp.concatenate([cos, cos, z64], axis=1), jnp.concatenate([-sin, z32, z64], axis=1), jnp.concatenate([z32, sin, z64], axis=1))


def _local_step(x, positions, target, mod, S, weights_of, grads_of):
    D = D_MODEL
    cc, sa, sb = _rope_tables(positions)
    mods = [[mod[i:i + 1, n * D:(n + 1) * D] for n in range(6)] for i in range(DEPTH)]
    h_dtype = lambda i: F32 if i % N_MIXERS == 0 else BF16
    saved = []
    h = _norm_mod_fwd(x, S["norm_mix_g"][0:1], mods[0][1], mods[0][0], h_dtype(0), "l0_norm1")
    for i in range(DEPTH):
        sh1, sc1, g1, sh2, sc2, g2 = mods[i]
        kind, j = i % N_MIXERS, i // N_MIXERS
        gmlp = S["norm_mlp_g"][i:i + 1]
        W = weights_of(i, "mix", x)
        st = {"x": x}
        norm2 = ((gmlp, "n"), (sc2, "n"), (sh2, "n"))
        if kind == 0:
            x2, pooled, ypre, h2 = _pool_fwd(h, W["pool_w"], S["pool_scale"][j:j + 1], x, g1, gmlp, sc2, sh2, f"l{i}_pool")
            st.update(pooled=pooled, y=ypre)
        elif kind == 1:
            zz = _mm(h, W["sgu_w_in"], out_dtypes=(F32,), name=f"l{i}_sgu_in")
            bs_t = S["sgu_b_s"].T
            gated = _sgu_gate_fwd(zz, S["sgu_ln_g"], S["sgu_ln_b"], S["sgu_w_s"], bs_t, f"l{i}_sgu_gate")
            x2, y, h2 = _mm(gated, W["sgu_w_out"], epi=_epi_residual_norm, extras=((x, "mn"), (g1, "n"), *norm2), out_dtypes=(F32, BF16, BF16),
                            tn=D, name=f"l{i}_sgu_out")
            st.update(h=h, zz=zz, gated=gated, y=y, bs_t=bs_t)
        else:
            lat = _mm(h, W["mla_w_dq_dkv"], out_dtypes=(F32,), name=f"l{i}_mla_lat")
            cqn, ckvn, krot = _mla_lat_fwd(lat, S["mla_q_norm_g"], S["mla_kv_norm_g"], cc, sa, sb, f"l{i}_mla_latn")
            q = _mm(cqn, W["mla_w_uq"], epi=_epi_q_rope, extras=((cc, "m"), (sa, "m"), (sb, "m")), name=f"l{i}_mla_uq")
            k, kt, v, vt = _mla_ukv(ckvn, W["mla_w_ukv"], krot, f"l{i}_mla_ukv")
            o, lse = _attn_fwd(q, k, vt, f"l{i}_attn")
            x2, y, h2 = _mm(o, W["mla_w_o"], epi=_epi_residual_norm, extras=((x, "mn"), (g1, "n"), *norm2), out_dtypes=(F32, BF16, BF16),
                            tn=D, name=f"l{i}_mla_o")
            st.update(h=h, lat=lat, cqn=cqn, ckvn=ckvn, q=q, k=k, kt=kt, v=v, o=o, lse=lse, y=y)
        W = {**W, **weights_of(i, "mlp", x2)}
        z, r2 = _mm(h2, W["mlp_w1"], epi=_epi_sq_relu, out_dtypes=(BF16, BF16), name=f"l{i}_mlp1")
        if i + 1 < DEPTH:
            norm1 = ((S["norm_mix_g"][i + 1:i + 2], "n"), (mods[i + 1][1], "n"), (mods[i + 1][0], "n"))
            x3, o2, h = _mm(z, W["mlp_w2"], epi=_epi_residual_norm, extras=((x2, "mn"), (g2, "n"), *norm1), out_dtypes=(F32, BF16, h_dtype(i + 1)),
                            tn=D, name=f"l{i}_mlp2")
        else:
            x3, o2 = _mm(z, W["mlp_w2"], epi=_epi_residual, extras=((x2, "mn"), (g2, "n")), out_dtypes=(F32, BF16), name=f"l{i}_mlp2")
        st.update(x2=x2, h2=h2, z=z, r2=r2, o2=o2, W=W)
        saved.append(st)
        x = x3

    loss, dx, dfinal_g = _loss_head(x, target, S["final_g"], "loss_head")

    gS = {"final_g": dfinal_g, "norm_mix_g": [None] * DEPTH, "norm_mlp_g": [None] * DEPTH, "pool_scale": [None] * 2}
    dmod = [None] * DEPTH
    do2, dg2 = _resid_bwd(dx, saved[-1]["o2"], mods[-1][5], f"l{DEPTH - 1}_b_res2")
    started = None
    for i in reversed(range(DEPTH)):
        st = saved[i]
        W, gW = st["W"], {}
        sh1, sc1, g1, sh2, sc2, g2 = mods[i]
        kind, j = i % N_MIXERS, i // N_MIXERS
        gmix, gmlp = S["norm_mix_g"][i:i + 1], S["norm_mlp_g"][i:i + 1]
        da = _mm(do2, W["mlp_w2"], tb=True, epi=lambda acc, rt: (acc * rt.astype(F32),), extras=((st["r2"], "mn"),), after=started, name=f"l{i}_b_dz")
        gW["mlp_w2"] = _mm(st["z"], do2, ta=True, chip_blocks="row", name=f"l{i}_b_dw2")
        dh2 = _mm(da, W["mlp_w1"], tb=True, out_dtypes=(F32,), name=f"l{i}_b_dh2")
        gW["mlp_w1"] = _mm(st["h2"], da, ta=True, chip_blocks="col", name=f"l{i}_b_dw1")
        dx2, dgmlp, dsc2, dsh2, dy, q1 = _norm_mod_bwd(st["x2"], dh2, dx, gmlp, sc2, f"l{i}_b_norm2", res=(st["y"], g1))
        gS["norm_mlp_g"][i] = dgmlp
        if kind == 0:
            dh, dpw, dpsc, dg1 = _pool_bwd(dy, st["pooled"], W["pool_w"], S["pool_scale"][j:j + 1], g1, q1, f"l{i}_b_pool")
            gW["pool_w"] = dpw.astype(BF16)
            gS["pool_scale"][j] = dpsc
        elif kind == 1:
            dg1 = q1
            dgated = _mm(dy, W["sgu_w_out"], tb=True, name=f"l{i}_b_dgated")
            gW["sgu_w_out"] = _mm(st["gated"], dy, ta=True, name=f"l{i}_b_dwout")
            dzz, dws, dbs, dlg, dlb = _sgu_gate_bwd(st["zz"], dgated, S["sgu_ln_g"], S["sgu_ln_b"], S["sgu_w_s"], st["bs_t"], f"l{i}_b_sgu_gate")
            gS.update(sgu_w_s=dws, sgu_b_s=dbs[:, :, 0], sgu_ln_g=dlg, sgu_ln_b=dlb)
            dh = _mm(dzz, W["sgu_w_in"], tb=True, out_dtypes=(F32,), name=f"l{i}_b_dh_sgu")
            gW["sgu_w_in"] = _mm(st["h"], dzz, ta=True, name=f"l{i}_b_dwin")
        else:
            dg1 = q1
            do = _mm(dy, W["mla_w_o"], tb=True, name=f"l{i}_b_do")
            gW["mla_w_o"] = _mm(st["o"], dy, ta=True, name=f"l{i}_b_dwo")
            delta = _attn_delta(do, st["o"], f"l{i}_b_delta")
            dqt, dkv, dkr = _attn_bwd(st["q"], st["k"], st["kt"], st["v"], do, st["lse"], delta, f"l{i}_b_attn")
            dqpad, dkrot = _mla_prep_bwd(dqt, dkr, cc, sa, sb, f"l{i}_b_mla_prep")
            dcqn = _mm(dqpad, W["mla_w_uq"], tb=True, out_dtypes=(F32,), name=f"l{i}_b_dcq")
            gW["mla_w_uq"] = _mm(st["cqn"], dqpad, ta=True, name=f"l{i}_b_dwuq")
            dckvn = _mm(dkv, W["mla_w_ukv"], tb=True, out_dtypes=(F32,), name=f"l{i}_b_dckv")
            gW["mla_w_ukv"] = _mm(st["ckvn"], dkv, ta=True, name=f"l{i}_b_dwukv")
            dlat, dqg, dkvg = _mla_lat_bwd(st["lat"], dcqn, dckvn, dkrot, S["mla_q_norm_g"], S["mla_kv_norm_g"], cc, sa, sb, f"l{i}_b_mla_latn")
            gS.update(mla_q_norm_g=dqg, mla_kv_norm_g=dkvg)
            dh = _mm(dlat, W["mla_w_dq_dkv"], tb=True, out_dtypes=(F32,), name=f"l{i}_b_dh_mla")
            gW["mla_w_dq_dkv"] = _mm(st["h"], dlat, ta=True, name=f"l{i}_b_dwdq")
        if i > 0:
            dx, dgmix, dsc1, dsh1, do2_prev, dg2_prev = _norm_mod_bwd(st["x"], dh, dx2, gmix, sc1, f"l{i}_b_norm1", res=(saved[i - 1]["o2"], mods[i - 1][5]))
        else:
            dx, dgmix, dsc1, dsh1 = _norm_mod_bwd(st["x"], dh, dx2, gmix, sc1, f"l{i}_b_norm1")
        gS["norm_mix_g"][i] = dgmix
        dmod[i] = jnp.concatenate([dsh1, dsc1, dg1, dsh2, dsc2, dg2], axis=1)
        started = grads_of(i, gW, dx)
        if i > 0:
            do2, dg2 = do2_prev, dg2_prev

    for n in ("norm_mix_g", "norm_mlp_g", "pool_scale"):
        gS[n] = jnp.concatenate(gS[n], axis=0)
    return loss, dx, gS, jnp.concatenate(dmod, axis=0)


_SMALL = {
    "norm_mix_g": (DEPTH, D_MODEL), "norm_mlp_g": (DEPTH, D_MODEL), "sgu_ln_g": (1, SGU_W), "sgu_ln_b": (1, SGU_W),
    "sgu_w_s": (SGU_H, SGU_CHUNK, SGU_CHUNK), "sgu_b_s": (SGU_H, SGU_CHUNK), "mla_kv_norm_g": (1, MLA_KVL), "final_g": (1, D_MODEL),
    "pool_scale": (2, D_MODEL), "mla_q_norm_g": (1, MLA_QL), "dmod": (DEPTH, 6 * D_MODEL),
}
_PACK_W = 1024


def _pack(vals):
    flat = jnp.concatenate([v.reshape(-1) for v in vals])
    rows = -(-flat.shape[0] // (8 * _PACK_W)) * 8
    return jnp.pad(flat, (0, rows * _PACK_W - flat.shape[0])).reshape(rows, _PACK_W)


def _unpack(buf, shapes):
    flat, out, off = buf.reshape(-1), [], 0
    for s in shapes:
        n = math.prod(s)
        out.append(flat[off:off + n].reshape(s))
        off += n
    return out


def kernel(x, c, positions, ada_w, ada_b, norm_mix_g, norm_mlp_g, pool_w, pool_scale, sgu_w_in, sgu_ln_g, sgu_ln_b, sgu_w_s, sgu_b_s, sgu_w_out, mla_w_dq_dkv, mla_q_norm_g, mla_kv_norm_g, mla_w_uq, mla_w_ukv, mla_w_o, mlp_w1, mlp_w2, final_g, loss_target, m_ada_w, m_ada_b, m_norm_mix_g, m_norm_mlp_g, m_pool_w, m_pool_scale, m_sgu_w_in, m_sgu_ln_g, m_sgu_ln_b, m_sgu_w_s, m_sgu_b_s, m_sgu_w_out, m_mla_w_dq_dkv, m_mla_q_norm_g, m_mla_kv_norm_g, m_mla_w_uq, m_mla_w_ukv, m_mla_w_o, m_mlp_w1, m_mlp_w2, m_final_g, v_ada_w, v_ada_b, v_norm_mix_g, v_norm_mlp_g, v_pool_w, v_pool_scale, v_sgu_w_in, v_sgu_ln_g, v_sgu_ln_b, v_sgu_w_s, v_sgu_b_s, v_sgu_w_out, v_mla_w_dq_dkv, v_mla_q_norm_g, v_mla_kv_norm_g, v_mla_w_uq, v_mla_w_ukv, v_mla_w_o, v_mlp_w1, v_mlp_w2, v_final_g):
    P = dict(ada_w=ada_w, ada_b=ada_b, norm_mix_g=norm_mix_g, norm_mlp_g=norm_mlp_g, pool_w=pool_w, pool_scale=pool_scale, sgu_w_in=sgu_w_in,
             sgu_ln_g=sgu_ln_g, sgu_ln_b=sgu_ln_b, sgu_w_s=sgu_w_s, sgu_b_s=sgu_b_s, sgu_w_out=sgu_w_out, mla_w_dq_dkv=mla_w_dq_dkv,
             mla_q_norm_g=mla_q_norm_g, mla_kv_norm_g=mla_kv_norm_g, mla_w_uq=mla_w_uq, mla_w_ukv=mla_w_ukv, mla_w_o=mla_w_o, mlp_w1=mlp_w1,
             mlp_w2=mlp_w2, final_g=final_g)
    M = dict(ada_w=m_ada_w, ada_b=m_ada_b, norm_mix_g=m_norm_mix_g, norm_mlp_g=m_norm_mlp_g, pool_w=m_pool_w, pool_scale=m_pool_scale,
             sgu_w_in=m_sgu_w_in, sgu_ln_g=m_sgu_ln_g, sgu_ln_b=m_sgu_ln_b, sgu_w_s=m_sgu_w_s, sgu_b_s=m_sgu_b_s, sgu_w_out=m_sgu_w_out,
             mla_w_dq_dkv=m_mla_w_dq_dkv, mla_q_norm_g=m_mla_q_norm_g, mla_kv_norm_g=m_mla_kv_norm_g, mla_w_uq=m_mla_w_uq, mla_w_ukv=m_mla_w_ukv,
             mla_w_o=m_mla_w_o, mlp_w1=m_mlp_w1, mlp_w2=m_mlp_w2, final_g=m_final_g)
    V = dict(ada_w=v_ada_w, ada_b=v_ada_b, norm_mix_g=v_norm_mix_g, norm_mlp_g=v_norm_mlp_g, pool_w=v_pool_w, pool_scale=v_pool_scale,
             sgu_w_in=v_sgu_w_in, sgu_ln_g=v_sgu_ln_g, sgu_ln_b=v_sgu_ln_b, sgu_w_s=v_sgu_w_s, sgu_b_s=v_sgu_b_s, sgu_w_out=v_sgu_w_out,
             mla_w_dq_dkv=v_mla_w_dq_dkv, mla_q_norm_g=v_mla_q_norm_g, mla_kv_norm_g=v_mla_kv_norm_g, mla_w_uq=v_mla_w_uq, mla_w_ukv=v_mla_w_ukv,
             mla_w_o=v_mla_w_o, mlp_w1=v_mlp_w1, mlp_w2=v_mlp_w2, final_g=v_final_g)
    order = list(P)
    xi, yi, ci = _idx()
    chip = 2 * xi + yi
    D = D_MODEL
    n_ada = ada_w.shape[2]

    pre = _allgather8(_pack([c, pool_scale, mla_q_norm_g]), "ag_small")
    flat = pre.reshape(N_DEV, -1)
    c_all = flat[:, :D]
    ps_all = flat[0::2, D:D + 2 * (D // N_CHIPS)].reshape(N_CHIPS, 2, D // N_CHIPS).transpose(1, 0, 2).reshape(2, D)
    q0 = D + 2 * (D // N_CHIPS)
    qg_all = flat[0::2, q0:q0 + MLA_QL // N_CHIPS].reshape(1, MLA_QL)

    ada_b_loc = lax.dynamic_slice_in_dim(ada_b, chip * n_ada, n_ada, axis=1)[:, None, :]
    modp = _ada_fwd(c_all, ada_w, ada_b_loc, "ada_fwd")
    mod = _mod_exchange(modp.transpose(1, 0, 2), "mod_exchange").transpose(1, 0, 2).reshape(DEPTH, 6 * D)

    S = dict(norm_mix_g=norm_mix_g, norm_mlp_g=norm_mlp_g, pool_scale=ps_all, sgu_ln_g=sgu_ln_g, sgu_ln_b=sgu_ln_b, sgu_w_s=sgu_w_s[0],
             sgu_b_s=sgu_b_s[0], mla_q_norm_g=qg_all, mla_kv_norm_g=mla_kv_norm_g, final_g=final_g[None, :])
    cidx, chipidx = jnp.reshape(ci, (1,)).astype(jnp.int32), jnp.reshape(chip, (1,)).astype(jnp.int32)
    view2d = lambda a: a.reshape(-1, a.shape[-1])

    def piece_rows(kind, blk):
        r = _PIECE_KINDS[kind][0]
        return blk * r, r

    groups = [_layer_pieces(0)[:-2], _layer_pieces(0)[-2:], _layer_pieces(1)[:-2], _layer_pieces(1)[-2:], _layer_pieces(2), _layer_pieces(3)]
    start_after = {1: (2, 3), 2: (4,), 4: (5,)}
    gathers = {}

    def gather_start(g, dep):
        srcs, shapes = [], []
        for kind, blk in groups[g]:
            r0, r = piece_rows(kind, blk)
            cdim = _PIECE_KINDS[kind][1]
            srcs.append(view2d(P[kind])[r0:r0 + r].astype(BF16).reshape(2, r // 2, cdim))
            shapes.append(jax.ShapeDtypeStruct((N_CHIPS, 2, r // 2, cdim), BF16))
        gathers[g] = _xchip_start("gather", srcs, shapes, dep, f"ag_start_g{g}")

    def gather_finish(g, after):
        ssem, rsem, srcs, lands, _ = gathers.pop(g)
        deps = [after]
        for nxt in start_after.get(g, ()):
            gather_start(nxt, deps[-1])
            deps.append(gathers[nxt][-1])
        srcs, lands = _xchip_wait("gather", ssem, rsem, srcs, lands, deps, f"ag_wait_g{g}")
        lands = _sibling_fwd(lands, f"ag_sibling_g{g}")
        W = {}
        for (kind, _), s, land in zip(groups[g], srcs, lands, strict=True):
            r, cdim, to_full, _ = _PIECE_KINDS[kind]
            W[kind] = to_full(lax.dynamic_update_index_in_dim(land, s, chip, 0).reshape(N_CHIPS, r, cdim))
        return W

    def weights_of(i, part, x_i):
        if i < 2:
            return gather_finish(2 * i + (part == "mlp"), x_i)
        return gather_finish(i + 2, x_i) if part == "mix" else {}

    scatters = {}
    bufs = {n: tuple(lax.empty(view2d(P[n]).shape, F32) for _ in range(4)) for n in _PIECE_KINDS}

    def scatter_start(i, gW, dep):
        pcs = _layer_pieces(i)
        blocked = []
        for kind, _ in pcs:
            r, cdim, _, to_blocks = _PIECE_KINDS[kind]
            g = gW[kind]
            blocked.append(g if g.ndim == 4 else to_blocks(g).reshape(N_CHIPS, 2, r // 2, cdim).transpose(1, 0, 2, 3))
        from_sib = _sibling_swap(blocked, f"rs_sibling_l{i}")
        pair, shapes = [], []
        for (kind, _), b, f in zip(pcs, blocked, from_sib, strict=True):
            _, _, hr, cdim = b.shape
            p = _sum_sel(cidx, b.reshape(2, N_CHIPS * hr, cdim), [f.reshape(1, N_CHIPS * hr, cdim)], f"rs_pair_l{i}_{kind}", BF16)
            pair.append(p.reshape(N_CHIPS, hr, cdim))
            shapes.append(jax.ShapeDtypeStruct((N_CHIPS - 1, hr, cdim), BF16))
        scatters[i] = (pcs, *_xchip_start("scatter", pair, shapes, dep, f"rs_start_l{i}"))
        return scatters[i][-1]

    def scatter_finish(i, after):
        pcs, ssem, rsem, pair, lands, _ = scatters.pop(i)
        pair, lands = _xchip_wait("scatter", ssem, rsem, pair, lands, after, f"rs_wait_l{i}")
        halves = [_sum_sel(chipidx, p, [l], f"rs_sum_l{i}_{kind}", F32) for (kind, _), p, l in zip(pcs, pair, lands, strict=True)]
        got = _sibling_send(halves, f"rs_merge_l{i}")
        for (kind, blk), mine, other in zip(pcs, halves, got, strict=True):
            r0, _ = piece_rows(kind, blk)
            bufs[kind] = tuple(_adamw_piece(cidx, view2d(P[kind]), view2d(M[kind]), view2d(V[kind]), mine, other, bufs[kind], r0,
                                            f"adamw_l{i}_{kind}"))
        return lands[0]

    first_layer = {}

    def grads_of(i, gW, dx_i):
        if i == 0:
            first_layer.update(gW)
            return None
        dep = scatter_finish(i + 1, [dx_i]) if i + 1 in scatters else dx_i
        return scatter_start(i, gW, dep)

    gather_start(0, mod)
    gather_start(1, gathers[0][-1])
    mod = mod + gathers[1][-1][0, 0]
    loss_l, dx, gS, dmod = _local_step(x[0], positions[0], loss_target[0], mod, S, weights_of, grads_of)
    loss = lax.psum(loss_l[0, 0], ("x", "y", "c"))

    gS["dmod"] = dmod
    packed = _pack([gS[n] for n in _SMALL])
    sg = _xchip_start("all8", [packed], [jax.ShapeDtypeStruct((N_DEV, *packed.shape), F32)], dx, "sg_start")
    tok0 = scatter_start(0, first_layer, sg[-1])[0, 0]
    scatter_finish(1, [dx, scatters[0][-1]])
    sg_src, sg_land = _xchip_wait("all8", sg[0], sg[1], sg[2], sg[3], [bufs[n][0] for n in ("mlp_w1", "mlp_w2", "sgu_w_in", "sgu_w_out")], "sg_wait")
    small = lax.dynamic_update_index_in_dim(sg_land[0], sg_src[0], 4 * xi + 2 * yi + ci, 0) + tok0
    small_sum = _unpack(_sum_lead([small], "sum_small_grads"), list(_SMALL.values()))
    G = dict(zip(_SMALL, small_sum, strict=True))
    grads = {
        "ada_b": G["dmod"], "norm_mix_g": G["norm_mix_g"], "norm_mlp_g": G["norm_mlp_g"], "sgu_ln_g": G["sgu_ln_g"], "sgu_ln_b": G["sgu_ln_b"],
        "sgu_w_s": G["sgu_w_s"][None], "sgu_b_s": G["sgu_b_s"][None], "mla_kv_norm_g": G["mla_kv_norm_g"], "final_g": G["final_g"][0],
        "pool_scale": lax.dynamic_slice_in_dim(G["pool_scale"], chip * (D // N_CHIPS), D // N_CHIPS, axis=1),
        "mla_q_norm_g": lax.dynamic_slice_in_dim(G["mla_q_norm_g"], chip * (MLA_QL // N_CHIPS), MLA_QL // N_CHIPS, axis=1),
    }
    dmod_all = _unpack(small, [(N_DEV,) + (small.shape[1] * _PACK_W,)])[0]
    off = sum(math.prod(s) for n, s in _SMALL.items() if n != "dmod")
    dmod_all = dmod_all[:, off:off + DEPTH * 6 * D].reshape(N_DEV, DEPTH, 6 * D)
    dmod_loc = lax.dynamic_slice_in_dim(dmod_all, chip * n_ada, n_ada, axis=2).transpose(1, 0, 2)
    grads["ada_w"] = _ada_bwd(c_all.T, dmod_loc, "ada_bwd")

    deltas, new_m, new_v = {}, {}, {}
    for n in order:
        if n not in _PIECE_KINDS:
            deltas[n], new_m[n], new_v[n] = _adamw(P[n], grads[n].reshape(P[n].shape), M[n], V[n], f"adamw_{n}")
    scatter_finish(0, [deltas["ada_w"], deltas["sgu_w_s"]] + [bufs[n][0] for n in ("mlp_w1", "mlp_w2", "sgu_w_in", "mla_w_o")])
    for n in _PIECE_KINDS:
        grads[n], deltas[n], new_m[n], new_v[n] = (b.reshape(P[n].shape) for b in bufs[n])
    return (loss, dx[None], *[grads[n].reshape(P[n].shape) for n in order], *[deltas[n] for n in order], *[new_m[n] for n in order],
            *[new_v[n] for n in order])
```

```python
import math

import jax
import jax.numpy as jnp
from jax import lax
from jax.experimental import pallas as pl
from jax.experimental.pallas import tpu as pltpu

F32, BF16 = jnp.float32, jnp.bfloat16
MESH = pl.DeviceIdType.MESH

D_MODEL = 1024
DEPTH = 4
N_MIXERS = 3
POOL_WINDOWS = (2, 4, 8, 16)
POOL_GD = D_MODEL // len(POOL_WINDOWS)
POOL_HALO = 16
SGU_CHUNK = 128
SGU_W = D_MODEL
SGU_HD = 128
SGU_H = SGU_W // SGU_HD
MLA_H = 16
MLA_QL = 256
MLA_KVL = 128
MLA_NOPE = 128
MLA_ROPE = 64
MLA_V = 128
MLA_HP = 256
MLA_LATP = 512
ROPE_THETA = 10000.0
RMS_EPS = 1e-6
LN_EPS = 1e-5
SM_SCALE = (MLA_NOPE + MLA_ROPE) ** -0.5
NEG_INF = -1e30
ADAM_LR, ADAM_B1, ADAM_B2, ADAM_EPS, ADAM_WD, ADAM_STEP = 0.001, 0.9, 0.999, 1e-08, 0.01, 10
N_CHIPS = 4
N_DEV = 8
ROW_TILE = 512
ATT_TILE = 512
ATT_SUB = 256
ATT_FWD_HEADS = 4
ATT_BWD_HEADS = 2
MM_VMEM_BUDGET = 40 << 20


def _idx():
    return lax.axis_index("x"), lax.axis_index("y"), lax.axis_index("c")


def _mm(a, b, *, name, ta=False, tb=False, epi=None, extras=(), out_dtypes=(BF16,), tm=1024, tn=1024, tk=1024, chip_blocks=None, after=None):
    if ta:
        K, M = a.shape
    else:
        M, K = a.shape
    b_chips = b.ndim == 3
    if b_chips:
        assert b.shape[0] == N_CHIPS
        Kb, N = (N_CHIPS * b.shape[2], b.shape[1]) if tb else (b.shape[1], N_CHIPS * b.shape[2])
    elif tb:
        N, Kb = b.shape
    else:
        Kb, N = b.shape
    assert K == Kb, (a.shape, b.shape, ta, tb)
    if b_chips and not tb:
        tn = min(tn, N // N_CHIPS)
    if chip_blocks == "col":
        tm, tn = min(tm, M // 2), min(tn, N // N_CHIPS)
    elif chip_blocks == "row":
        tm = min(tm, M // N_CHIPS // 2)
    tm, tn, tk = min(tm, M), min(tn, N), min(tk, K)

    def vmem_bytes(tm_, tk_):
        per_mn = sum(arr.dtype.itemsize for arr, kind in extras if kind == "mn") + sum(jnp.dtype(dt).itemsize for dt in out_dtypes)
        return 2 * (tm_ * tk_ * a.dtype.itemsize + tk_ * tn * b.dtype.itemsize + tm_ * tn * per_mn)

    if vmem_bytes(tm, K) <= MM_VMEM_BUDGET:
        tk = K
    elif tm >= 512 and vmem_bytes(tm // 2, K) <= MM_VMEM_BUDGET:
        tm, tk = tm // 2, K
    assert M % tm == 0 and N % tn == 0 and K % tk == 0, (M, N, K, tm, tn, tk)
    nk = K // tk
    a_spec = pl.BlockSpec((tk, tm), lambda i, j, k: (k, i)) if ta else pl.BlockSpec((tm, tk), lambda i, j, k: (i, k))
    b_spec = pl.BlockSpec((tn, tk), lambda i, j, k: (j, k)) if tb else pl.BlockSpec((tk, tn), lambda i, j, k: (k, j))
    if b_chips and tb:
        assert nk == 1 and not ta
        b_spec = pl.BlockSpec((N_CHIPS, tn, K // N_CHIPS), lambda i, j, k: (0, j, 0))
    elif b_chips:
        per = N // N_CHIPS // tn
        b_spec = pl.BlockSpec((None, tk, tn), lambda i, j, k: (j // per, k, j % per))
    ex_specs = []
    for arr, kind in extras:
        if kind == "mn":
            ex_specs.append(pl.BlockSpec((tm, tn), lambda i, j, k: (i, j)))
        elif kind == "n":
            ex_specs.append(pl.BlockSpec((1, tn), lambda i, j, k: (0, j)))
        else:
            ex_specs.append(pl.BlockSpec((tm, arr.shape[1]), lambda i, j, k: (i, 0)))
    n_ex, n_out = len(extras), len(out_dtypes)
    n_in = 2 + n_ex + (after is not None)
    dims = (((0 if ta else 1,), (1 if tb else 0,)), ((), ()))

    def body(*refs):
        a_ref, b_ref = refs[0], refs[1]
        ex_refs = refs[2:2 + n_ex]
        out_refs = refs[n_in:n_in + n_out]
        if b_chips and tb:
            kc = K // N_CHIPS
            part = None
            for cb in range(N_CHIPS):
                p = lax.dot_general(a_ref[:, cb * kc:(cb + 1) * kc].astype(BF16), b_ref[cb].astype(BF16), dims, preferred_element_type=F32)
                part = p if part is None else part + p
        else:
            part = lax.dot_general(a_ref[...].astype(BF16), b_ref[...].astype(BF16), dims, preferred_element_type=F32)

        def finish(acc):
            outs = epi(acc, *[r[...] for r in ex_refs]) if epi is not None else (acc,)
            for r, o in zip(out_refs, outs, strict=True):
                r[...] = o.astype(r.dtype)

        if nk == 1:
            finish(part)
        else:
            acc_ref = refs[-1]
            k = pl.program_id(2)

            @pl.when(k == 0)
            def _():
                acc_ref[...] = part

            @pl.when(k > 0)
            def _():
                acc_ref[...] += part

            @pl.when(k == nk - 1)
            def _():
                finish(acc_ref[...])

    out_specs = [pl.BlockSpec((tm, tn), lambda i, j, k: (i, j)) for _ in range(n_out)]
    out_shape = [jax.ShapeDtypeStruct((M, N), dt) for dt in out_dtypes]
    if chip_blocks is not None:
        assert n_out == 1
        if chip_blocks == "col":
            rh, cb = M // 2 // tm, N // N_CHIPS // tn
            out_specs = [pl.BlockSpec((None, None, tm, tn), lambda i, j, k: (i // rh, j // cb, i % rh, j % cb))]
            out_shape = [jax.ShapeDtypeStruct((2, N_CHIPS, M // 2, N // N_CHIPS), out_dtypes[0])]
        else:
            rh = M // N_CHIPS // 2 // tm
            out_specs = [pl.BlockSpec((None, None, tm, tn), lambda i, j, k: ((i // rh) % 2, i // (2 * rh), i % rh, j))]
            out_shape = [jax.ShapeDtypeStruct((2, N_CHIPS, M // N_CHIPS // 2, N), out_dtypes[0])]
    outs = pl.pallas_call(
        body,
        name=name,
        grid=(M // tm, N // tn, nk),
        in_specs=[a_spec, b_spec, *ex_specs] + ([pl.BlockSpec(memory_space=pl.ANY)] if after is not None else []),
        out_specs=out_specs,
        out_shape=out_shape,
        scratch_shapes=[pltpu.VMEM((tm, tn), F32)] if nk > 1 else [],
        compiler_params=pltpu.CompilerParams(dimension_semantics=("parallel", "parallel", "arbitrary")),
    )(a, b, *[arr for arr, _ in extras], *([after] if after is not None else []))
    return outs[0] if n_out == 1 else tuple(outs)


def _epi_sq_relu(acc):
    r = jnp.maximum(acc, 0.0)
    return r * r, 2.0 * r


def _epi_residual(acc, x, g):
    return x + g * acc, acc


def _rms_mod(xv, gain, sc, sh):
    r = lax.rsqrt(jnp.mean(xv * xv, axis=-1, keepdims=True) + RMS_EPS)
    return ((xv * r) * gain) * (1.0 + sc) + sh


def _epi_residual_norm(acc, x, g, gain, sc, sh):
    xn = x + g * acc
    return xn, acc, _rms_mod(xn, gain, sc, sh)


def _row_spec(tr, d):
    return pl.BlockSpec((tr, d), lambda i: (i, 0))


def _vec_spec(d):
    return pl.BlockSpec((1, d), lambda i: (0, 0))


def _colsum(v):
    return jnp.sum(v, axis=0, keepdims=True)


def _norm_mod_fwd(x, gain, sc, sh, out_dtype, name):
    T, D = x.shape
    tr = min(T, ROW_TILE)

    def body(x_ref, g_ref, sc_ref, sh_ref, o_ref):
        o_ref[...] = _rms_mod(x_ref[...], g_ref[...], sc_ref[...], sh_ref[...]).astype(o_ref.dtype)

    return pl.pallas_call(
        body, name=name, grid=(T // tr,),
        in_specs=[_row_spec(tr, D), _vec_spec(D), _vec_spec(D), _vec_spec(D)],
        out_specs=_row_spec(tr, D),
        out_shape=jax.ShapeDtypeStruct((T, D), out_dtype),
        compiler_params=pltpu.CompilerParams(dimension_semantics=("parallel",)),
    )(x, gain, sc, sh)


def _norm_mod_bwd(x, dh, dres, gain, sc, name, res=None):
    T, D = x.shape
    tr = min(T, ROW_TILE)

    def body(x_ref, dh_ref, dres_ref, g_ref, sc_ref, *refs):
        dx_ref, dg_ref, dsc_ref, dsh_ref = refs[-6:-2] if res is not None else refs

        @pl.when(pl.program_id(0) == 0)
        def _():
            dg_ref[...] = jnp.zeros_like(dg_ref)
            dsc_ref[...] = jnp.zeros_like(dsc_ref)
            dsh_ref[...] = jnp.zeros_like(dsh_ref)
            if res is not None:
                refs[-1][...] = jnp.zeros_like(refs[-1])

        xv = x_ref[...]
        r = lax.rsqrt(jnp.mean(xv * xv, axis=-1, keepdims=True) + RMS_EPS)
        xn = xv * r
        dhv = dh_ref[...].astype(F32)
        dsh_ref[...] += _colsum(dhv)
        dsc_ref[...] += _colsum(dhv * (xn * g_ref[...]))
        dt = dhv * (1.0 + sc_ref[...])
        dg_ref[...] += _colsum(dt * xn)
        dxn = dt * g_ref[...]
        dxv = dres_ref[...] + r * (dxn - xn * jnp.mean(dxn * xn, axis=-1, keepdims=True))
        dx_ref[...] = dxv
        if res is not None:
            y_ref, gr_ref, dy_ref, q_ref = refs[0], refs[1], refs[-2], refs[-1]
            dy_ref[...] = (gr_ref[...] * dxv).astype(BF16)
            q_ref[...] += _colsum(dxv * y_ref[...].astype(F32))

    extra_in, extra_spec = ([], []) if res is None else (list(res), [_row_spec(tr, D), _vec_spec(D)])
    return pl.pallas_call(
        body, name=name, grid=(T // tr,),
        in_specs=[_row_spec(tr, D), _row_spec(tr, D), _row_spec(tr, D), _vec_spec(D), _vec_spec(D), *extra_spec],
        out_specs=[_row_spec(tr, D), _vec_spec(D), _vec_spec(D), _vec_spec(D)] + ([_row_spec(tr, D), _vec_spec(D)] if res is not None else []),
        out_shape=[jax.ShapeDtypeStruct((T, D), F32)] + [jax.ShapeDtypeStruct((1, D), F32)] * 3
        + ([jax.ShapeDtypeStruct((T, D), BF16), jax.ShapeDtypeStruct((1, D), F32)] if res is not None else []),
        compiler_params=pltpu.CompilerParams(dimension_semantics=("arbitrary",)),
    )(x, dh, dres, gain, sc, *extra_in)


def _resid_bwd(dx, y, g, name):
    T, D = dx.shape
    tr = min(T, ROW_TILE)

    def body(dx_ref, y_ref, g_ref, dy_ref, q_ref):
        @pl.when(pl.program_id(0) == 0)
        def _():
            q_ref[...] = jnp.zeros_like(q_ref)

        dxv = dx_ref[...]
        dy_ref[...] = (g_ref[...] * dxv).astype(BF16)
        q_ref[...] += _colsum(dxv * y_ref[...].astype(F32))

    return pl.pallas_call(
        body, name=name, grid=(T // tr,),
        in_specs=[_row_spec(tr, D), _row_spec(tr, D), _vec_spec(D)],
        out_specs=[_row_spec(tr, D), _vec_spec(D)],
        out_shape=[jax.ShapeDtypeStruct((T, D), BF16), jax.ShapeDtypeStruct((1, D), F32)],
        compiler_params=pltpu.CompilerParams(dimension_semantics=("arbitrary",)),
    )(dx, y, g)


def _loss_head(x, target, gain, name):
    T, D = x.shape
    tr = min(T, ROW_TILE)

    def body(x_ref, t_ref, g_ref, loss_ref, dx_ref, dg_ref):
        @pl.when(pl.program_id(0) == 0)
        def _():
            loss_ref[...] = jnp.zeros_like(loss_ref)
            dg_ref[...] = jnp.zeros_like(dg_ref)

        xv = x_ref[...]
        r = lax.rsqrt(jnp.mean(xv * xv, axis=-1, keepdims=True) + RMS_EPS)
        xn = xv * r
        err = xn * g_ref[...] - t_ref[...]
        row = jnp.mean(err * err, axis=-1, keepdims=True)
        loss_ref[...] += 0.5 * jnp.sum(row, axis=0, keepdims=True)
        dy = err * (1.0 / D)
        dg_ref[...] += _colsum(dy * xn)
        dxn = dy * g_ref[...]
        dx_ref[...] = r * (dxn - xn * jnp.mean(dxn * xn, axis=-1, keepdims=True))

    return pl.pallas_call(
        body, name=name, grid=(T // tr,),
        in_specs=[_row_spec(tr, D), _row_spec(tr, D), _vec_spec(D)],
        out_specs=[_vec_spec(128), _row_spec(tr, D), _vec_spec(D)],
        out_shape=[jax.ShapeDtypeStruct((1, 128), F32), jax.ShapeDtypeStruct((T, D), F32), jax.ShapeDtypeStruct((1, D), F32)],
        compiler_params=pltpu.CompilerParams(dimension_semantics=("arbitrary",)),
    )(x, target, gain)


def _pool_fwd(h, w, scale, x, g1, gmlp, sc2, sh2, name):
    T, D = h.shape
    tr = min(T, ROW_TILE)

    def body(h_ref, w_ref, sc_ref, x_ref, g_ref, gm_ref, sc2_ref, sh2_ref, x2_ref, pooled_ref, ypre_ref, h2_ref, halo_ref):
        i = pl.program_id(0)

        @pl.when(i == 0)
        def _():
            halo_ref[...] = jnp.zeros_like(halo_ref)

        hv = h_ref[...]
        buf = jnp.concatenate([halo_ref[...], hv], axis=0)
        halo_ref[...] = hv[tr - POOL_HALO:, :]
        t = (i * tr + lax.broadcasted_iota(jnp.int32, (tr, 1), 0)).astype(F32)
        for gi, win in enumerate(POOL_WINDOWS):
            cols = slice(gi * POOL_GD, (gi + 1) * POOL_GD)
            val = buf[:, cols]
            sh = 1
            while sh < win:
                val = val + pltpu.roll(val, sh, axis=0)
                sh *= 2
            pooled = val[POOL_HALO:, :] / jnp.minimum(t + 1.0, float(win)) - hv[:, cols]
            pb = pooled.astype(BF16)
            pooled_ref[:, cols] = pb
            yp = jnp.dot(pb, w_ref[gi], preferred_element_type=F32)
            ypre_ref[:, cols] = yp.astype(BF16)
            x2_ref[:, cols] = x_ref[:, cols] + g_ref[:, cols] * (yp * sc_ref[:, cols])
        h2_ref[...] = _rms_mod(x2_ref[...], gm_ref[...], sc2_ref[...], sh2_ref[...]).astype(BF16)

    return pl.pallas_call(
        body, name=name, grid=(T // tr,),
        in_specs=[_row_spec(tr, D), pl.BlockSpec(w.shape, lambda i: (0, 0, 0)), _vec_spec(D), _row_spec(tr, D), _vec_spec(D), _vec_spec(D),
                  _vec_spec(D), _vec_spec(D)],
        out_specs=[_row_spec(tr, D)] * 4,
        out_shape=[jax.ShapeDtypeStruct((T, D), F32), jax.ShapeDtypeStruct((T, D), BF16), jax.ShapeDtypeStruct((T, D), BF16),
                   jax.ShapeDtypeStruct((T, D), BF16)],
        scratch_shapes=[pltpu.VMEM((POOL_HALO, D), F32)],
        compiler_params=pltpu.CompilerParams(dimension_semantics=("arbitrary",)),
    )(h, w, scale, x, g1, gmlp, sc2, sh2)


def _pool_bwd(dy, pooled, w, scale, g1, q, name):
    T, D = dy.shape
    tr = min(T, ROW_TILE)
    nt = T // tr
    ltot = tr + POOL_HALO

    def body(dy_ref, pooled_ref, w_ref, sc_ref, g_ref, q_ref, dh_ref, dw_ref, dsc_ref, dg_ref, halo_ref):
        i = pl.program_id(0)

        @pl.when(i == 0)
        def _():
            halo_ref[...] = jnp.zeros_like(halo_ref)
            dw_ref[...] = jnp.zeros_like(dw_ref)
            dsc_ref[...] = g_ref[...] * q_ref[...]
            dg_ref[...] = sc_ref[...] * q_ref[...]

        t = ((nt - 1 - i) * tr + lax.broadcasted_iota(jnp.int32, (tr, 1), 0)).astype(F32)
        for gi, win in enumerate(POOL_WINDOWS):
            cols = slice(gi * POOL_GD, (gi + 1) * POOL_GD)
            dyb = (dy_ref[:, cols].astype(F32) * sc_ref[:, cols]).astype(BF16)
            dw_ref[gi] += lax.dot_general(pooled_ref[:, cols], dyb, (((0,), (0,)), ((), ())), preferred_element_type=F32)
            dpool = lax.dot_general(dyb, w_ref[gi], (((1,), (1,)), ((), ())), preferred_element_type=F32)
            qv = dpool / jnp.minimum(t + 1.0, float(win))
            val = jnp.concatenate([qv, halo_ref[:, cols]], axis=0)
            halo_ref[:, cols] = qv[:POOL_HALO, :]
            sh = 1
            while sh < win:
                val = val + pltpu.roll(val, ltot - sh, axis=0)
                sh *= 2
            dh_ref[:, cols] = val[:tr, :] - dpool

    rev = pl.BlockSpec((tr, D), lambda i: (nt - 1 - i, 0))
    return pl.pallas_call(
        body, name=name, grid=(nt,),
        in_specs=[rev, rev, pl.BlockSpec(w.shape, lambda i: (0, 0, 0)), _vec_spec(D), _vec_spec(D), _vec_spec(D)],
        out_specs=[rev, pl.BlockSpec(w.shape, lambda i: (0, 0, 0)), _vec_spec(D), _vec_spec(D)],
        out_shape=[jax.ShapeDtypeStruct((T, D), F32), jax.ShapeDtypeStruct(w.shape, F32),
                   jax.ShapeDtypeStruct((1, D), F32), jax.ShapeDtypeStruct((1, D), F32)],
        scratch_shapes=[pltpu.VMEM((POOL_HALO, D), F32)],
        compiler_params=pltpu.CompilerParams(dimension_semantics=("arbitrary",)),
    )(dy, pooled, w, scale, g1, q)


_INV_SQRT2 = 0.7071067811865476
_INV_SQRT2PI = 0.3989422804014327


def _gelu(v):
    return 0.5 * v * (1.0 + lax.erf(v * _INV_SQRT2))


def _gelu_grad(v):
    return 0.5 * (1.0 + lax.erf(v * _INV_SQRT2)) + v * jnp.exp(-0.5 * v * v) * _INV_SQRT2PI


def _sgu_ln(v, g, b):
    mu = jnp.mean(v, axis=-1, keepdims=True)
    xc = v - mu
    rstd = lax.rsqrt(jnp.mean(xc * xc, axis=-1, keepdims=True) + LN_EPS)
    xh = xc * rstd
    return xh, rstd, xh * g + b


def _tril_mask():
    return lax.broadcasted_iota(jnp.int32, (SGU_CHUNK, SGU_CHUNK), 0) >= lax.broadcasted_iota(jnp.int32, (SGU_CHUNK, SGU_CHUNK), 1)


SGU_TILE = 256


def _sgu_gate_fwd(zz, ln_g, ln_b, ws, bs_t, name):
    T = zz.shape[0]
    ts = min(T, SGU_TILE)

    def body(zz_ref, g_ref, b_ref, ws_ref, bs_ref, out_ref):
        z = _gelu(zz_ref[...])
        u = z[:, :SGU_W]
        _, _, vn = _sgu_ln(z[:, SGU_W:], g_ref[...], b_ref[...])
        vb = vn.astype(BF16)
        tril = _tril_mask()
        for hh in range(SGU_H):
            wm = jnp.where(tril, ws_ref[hh], 0.0).astype(BF16)
            bcol = bs_ref[:, hh:hh + 1]
            cs = slice(hh * SGU_HD, (hh + 1) * SGU_HD)
            for j in range(ts // SGU_CHUNK):
                rs = slice(j * SGU_CHUNK, (j + 1) * SGU_CHUNK)
                mixed = jnp.dot(wm, vb[rs, cs], preferred_element_type=F32) + bcol
                out_ref[rs, cs] = (u[rs, cs] * mixed).astype(BF16)

    return pl.pallas_call(
        body, name=name, grid=(T // ts,),
        in_specs=[_row_spec(ts, 2 * SGU_W), _vec_spec(SGU_W), _vec_spec(SGU_W),
                  pl.BlockSpec(ws.shape, lambda i: (0, 0, 0)), pl.BlockSpec(bs_t.shape, lambda i: (0, 0))],
        out_specs=_row_spec(ts, SGU_W),
        out_shape=jax.ShapeDtypeStruct((T, SGU_W), BF16),
        compiler_params=pltpu.CompilerParams(dimension_semantics=("parallel",)),
    )(zz, ln_g, ln_b, ws, bs_t)


def _sgu_gate_bwd(zz, dgated, ln_g, ln_b, ws, bs_t, name):
    T = zz.shape[0]
    ts = min(T, SGU_TILE)
    nt = T // ts

    def body(zz_ref, dg_ref, g_ref, b_ref, ws_ref, bs_ref, dzz_ref, dws_ref, dbs_ref, dlg_ref, dlb_ref, dlo_ref, dmx_ref):
        i = pl.program_id(0)

        @pl.when(i == 0)
        def _():
            dws_ref[...] = jnp.zeros_like(dws_ref)
            dmx_ref[...] = jnp.zeros_like(dmx_ref)
            dlg_ref[...] = jnp.zeros_like(dlg_ref)
            dlb_ref[...] = jnp.zeros_like(dlb_ref)

        zzv = zz_ref[...]
        z = _gelu(zzv)
        u = z[:, :SGU_W]
        xh, rstd, vn = _sgu_ln(z[:, SGU_W:], g_ref[...], b_ref[...])
        vb = vn.astype(BF16)
        dgv = dg_ref[...].astype(F32)
        tril = _tril_mask()
        for hh in range(SGU_H):
            wm = jnp.where(tril, ws_ref[hh], 0.0).astype(BF16)
            bcol = bs_ref[:, hh:hh + 1]
            cs = slice(hh * SGU_HD, (hh + 1) * SGU_HD)
            for j in range(ts // SGU_CHUNK):
                rs = slice(j * SGU_CHUNK, (j + 1) * SGU_CHUNK)
                mixed = jnp.dot(wm, vb[rs, cs], preferred_element_type=F32) + bcol
                dmixed = dgv[rs, cs] * u[rs, cs]
                dzz_ref[rs, cs] = (dgv[rs, cs] * mixed * _gelu_grad(zzv[rs, cs])).astype(BF16)
                dmb = dmixed.astype(BF16)
                dws_ref[hh] += lax.dot_general(dmb, vb[rs, cs], (((1,), (1,)), ((), ())), preferred_element_type=F32)
                dmx_ref[hh] += dmixed
                dlo_ref[rs, cs] = lax.dot_general(wm, dmb, (((0,), (0,)), ((), ())), preferred_element_type=F32)
        dlo = dlo_ref[...]
        dlg_ref[...] += _colsum(dlo * xh)
        dlb_ref[...] += _colsum(dlo)
        dxh = dlo * g_ref[...]
        dv = rstd * (dxh - jnp.mean(dxh, axis=-1, keepdims=True) - xh * jnp.mean(dxh * xh, axis=-1, keepdims=True))
        dzz_ref[:, SGU_W:] = (dv * _gelu_grad(zzv[:, SGU_W:])).astype(BF16)

        @pl.when(i == nt - 1)
        def _():
            tril_f = tril.astype(F32)
            for hh in range(SGU_H):
                dws_ref[hh] = dws_ref[hh] * tril_f
                dbs_ref[hh] = jnp.broadcast_to(jnp.sum(dmx_ref[hh], axis=-1, keepdims=True), (SGU_CHUNK, SGU_HD))

    full3 = pl.BlockSpec(ws.shape, lambda i: (0, 0, 0))
    return pl.pallas_call(
        body, name=name, grid=(nt,),
        in_specs=[_row_spec(ts, 2 * SGU_W), _row_spec(ts, SGU_W), _vec_spec(SGU_W), _vec_spec(SGU_W), full3,
                  pl.BlockSpec(bs_t.shape, lambda i: (0, 0))],
        out_specs=[_row_spec(ts, 2 * SGU_W), full3, full3, _vec_spec(SGU_W), _vec_spec(SGU_W)],
        out_shape=[jax.ShapeDtypeStruct((T, 2 * SGU_W), BF16), jax.ShapeDtypeStruct(ws.shape, F32), jax.ShapeDtypeStruct(ws.shape, F32),
                   jax.ShapeDtypeStruct((1, SGU_W), F32), jax.ShapeDtypeStruct((1, SGU_W), F32)],
        scratch_shapes=[pltpu.VMEM((ts, SGU_W), F32), pltpu.VMEM(ws.shape, F32)],
        compiler_params=pltpu.CompilerParams(dimension_semantics=("arbitrary",)),
    )(zz, dgated, ln_g, ln_b, ws, bs_t)


def _rope_fwd(blk, cc, sa, sb):
    return blk * cc + pltpu.roll(blk, 96, axis=1) * sa + pltpu.roll(blk, 32, axis=1) * sb


def _rope_bwd(d, cc, sa, sb):
    return d * cc + pltpu.roll(d * sa, 32, axis=1) + pltpu.roll(d * sb, 96, axis=1)


def _rms(v, g):
    r = lax.rsqrt(jnp.mean(v * v, axis=-1, keepdims=True) + RMS_EPS)
    vn = v * r
    return vn, r, vn * g


def _rms_bwd(dy, vn, r, g):
    dvn = dy * g
    return r * (dvn - vn * jnp.mean(dvn * vn, axis=-1, keepdims=True))


MLA_TILE = 256
_KV0 = MLA_QL
_KR0 = MLA_QL + MLA_KVL


def _mla_lat_fwd(lat, qg, kvg, cc, sa, sb, name):
    T = lat.shape[0]
    tr = min(T, ROW_TILE)

    def body(lat_ref, qg_ref, kvg_ref, cc_ref, sa_ref, sb_ref, cq_ref, ckv_ref, kr_ref):
        lv = lat_ref[...]
        cq_ref[...] = _rms(lv[:, :_KV0], qg_ref[...])[2].astype(BF16)
        ckv_ref[...] = _rms(lv[:, _KV0:_KR0], kvg_ref[...])[2].astype(BF16)
        kr_ref[...] = _rope_fwd(lv[:, _KR0:], cc_ref[...], sa_ref[...], sb_ref[...])

    return pl.pallas_call(
        body, name=name, grid=(T // tr,),
        in_specs=[_row_spec(tr, MLA_LATP), _vec_spec(MLA_QL), _vec_spec(MLA_KVL), _row_spec(tr, 128), _row_spec(tr, 128), _row_spec(tr, 128)],
        out_specs=[_row_spec(tr, MLA_QL), _row_spec(tr, MLA_KVL), _row_spec(tr, 128)],
        out_shape=[jax.ShapeDtypeStruct((T, MLA_QL), BF16), jax.ShapeDtypeStruct((T, MLA_KVL), BF16), jax.ShapeDtypeStruct((T, 128), F32)],
        compiler_params=pltpu.CompilerParams(dimension_semantics=("parallel",)),
    )(lat, qg, kvg, cc, sa, sb)


def _mla_lat_bwd(lat, dcqn, dckvn, dkrot, qg, kvg, cc, sa, sb, name):
    T = lat.shape[0]
    tr = min(T, ROW_TILE)

    def body(lat_ref, dcq_ref, dckv_ref, dkr_ref, qg_ref, kvg_ref, cc_ref, sa_ref, sb_ref, dlat_ref, dqg_ref, dkvg_ref):
        @pl.when(pl.program_id(0) == 0)
        def _():
            dqg_ref[...] = jnp.zeros_like(dqg_ref)
            dkvg_ref[...] = jnp.zeros_like(dkvg_ref)

        lv = lat_ref[...]
        qn, qr, _ = _rms(lv[:, :_KV0], qg_ref[...])
        kn, kr, _ = _rms(lv[:, _KV0:_KR0], kvg_ref[...])
        dcq = dcq_ref[...]
        dckv = dckv_ref[...]
        dqg_ref[...] += _colsum(dcq * qn)
        dkvg_ref[...] += _colsum(dckv * kn)
        dlat_ref[:, :_KV0] = _rms_bwd(dcq, qn, qr, qg_ref[...]).astype(BF16)
        dlat_ref[:, _KV0:_KR0] = _rms_bwd(dckv, kn, kr, kvg_ref[...]).astype(BF16)
        dlat_ref[:, _KR0:] = _rope_bwd(dkr_ref[...], cc_ref[...], sa_ref[...], sb_ref[...]).astype(BF16)

    return pl.pallas_call(
        body, name=name, grid=(T // tr,),
        in_specs=[_row_spec(tr, MLA_LATP), _row_spec(tr, MLA_QL), _row_spec(tr, MLA_KVL), _row_spec(tr, 128),
                  _vec_spec(MLA_QL), _vec_spec(MLA_KVL), _row_spec(tr, 128), _row_spec(tr, 128), _row_spec(tr, 128)],
        out_specs=[_row_spec(tr, MLA_LATP), _vec_spec(MLA_QL), _vec_spec(MLA_KVL)],
        out_shape=[jax.ShapeDtypeStruct((T, MLA_LATP), BF16), jax.ShapeDtypeStruct((1, MLA_QL), F32), jax.ShapeDtypeStruct((1, MLA_KVL), F32)],
        compiler_params=pltpu.CompilerParams(dimension_semantics=("arbitrary",)),
    )(lat, dcqn, dckvn, dkrot, qg, kvg, cc, sa, sb)


def _epi_q_rope(acc, cc, sa, sb):
    out = []
    for hh in range(acc.shape[1] // MLA_HP):
        a, m, b = hh * MLA_HP, hh * MLA_HP + MLA_NOPE, (hh + 1) * MLA_HP
        out += [acc[:, a:m] * SM_SCALE, _rope_fwd(acc[:, m:b], cc, sa, sb) * SM_SCALE]
    return (jnp.concatenate(out, axis=1),)


def _mla_ukv(ckvn, w_ukv, krot, name):
    T = ckvn.shape[0]
    tr = min(T, ATT_TILE)
    hg = ATT_HG
    gw = hg * MLA_HP

    def body(a_ref, w_ref, kr_ref, ko_ref, kt_ref, vo_ref, vt_ref):
        acc = jnp.dot(a_ref[...], w_ref[...], preferred_element_type=F32)
        kr = kr_ref[...]
        krb, krt = kr.astype(BF16), kr.T.astype(BF16)
        for hh in range(hg):
            a, m, b = hh * MLA_HP, hh * MLA_HP + MLA_NOPE, (hh + 1) * MLA_HP
            kn, vh = acc[:, a:m], acc[:, m:b]
            ko_ref[:, a:m] = kn.astype(BF16)
            ko_ref[:, m:b] = krb
            kt_ref[a:m, :] = kn.T.astype(BF16)
            kt_ref[m:b, :] = krt
            vo_ref[:, hh * MLA_V:(hh + 1) * MLA_V] = vh.astype(BF16)
            vt_ref[hh] = vh.T.astype(BF16)

    tk = min(T, ATT_TILE)
    per = tk // tr
    HW = MLA_H * MLA_HP
    return pl.pallas_call(
        body, name=name, grid=(T // tr, MLA_H // hg),
        in_specs=[pl.BlockSpec((tr, MLA_KVL), lambda i, g: (i, 0)), pl.BlockSpec((MLA_KVL, gw), lambda i, g: (0, g)),
                  pl.BlockSpec((tr, 128), lambda i, g: (i, 0))],
        out_specs=[pl.BlockSpec((tr, gw), lambda i, g: (i, g)), pl.BlockSpec((gw, tr), lambda i, g: (g, i)),
                   pl.BlockSpec((tr, hg * MLA_V), lambda i, g: (i, g)),
                   pl.BlockSpec((hg, None, MLA_V, tr), lambda i, g: (g, i // per, 0, i % per))],
        out_shape=[jax.ShapeDtypeStruct((T, HW), BF16), jax.ShapeDtypeStruct((HW, T), BF16), jax.ShapeDtypeStruct((T, MLA_H * MLA_V), BF16),
                   jax.ShapeDtypeStruct((MLA_H, T // tk, MLA_V, tk), BF16)],
        compiler_params=pltpu.CompilerParams(dimension_semantics=("parallel", "parallel")),
    )(ckvn, w_ukv, krot)


ATT_HG = 4


def _mla_prep_bwd(dqt, dkr, cc, sa, sb, name):
    _, nq, _, tq = dqt.shape
    T = nq * tq
    gw = ATT_HG * MLA_HP

    def body(dq_ref, dk_ref, cc_ref, sa_ref, sb_ref, dqp_ref, dkr_ref):
        @pl.when(pl.program_id(1) == 0)
        def _():
            dkr_ref[...] = jnp.zeros_like(dkr_ref)

        cc, sa, sb = cc_ref[...], sa_ref[...], sb_ref[...]
        acc = jnp.zeros((tq, 128), F32)
        for hh in range(ATT_HG):
            a, m, b = hh * MLA_HP, hh * MLA_HP + MLA_NOPE, (hh + 1) * MLA_HP
            dqh = dq_ref[hh].astype(F32).T * SM_SCALE
            dqp_ref[:, a:m] = dqh[:, :MLA_NOPE].astype(BF16)
            dqp_ref[:, m:b] = _rope_bwd(dqh[:, MLA_NOPE:], cc, sa, sb).astype(BF16)
            acc = acc + dk_ref[:, hh * 128:(hh + 1) * 128].astype(F32)
        dkr_ref[...] += acc

    tab = pl.BlockSpec((tq, 128), lambda i, g: (i, 0))
    return pl.pallas_call(
        body, name=name, grid=(nq, MLA_H // ATT_HG),
        in_specs=[pl.BlockSpec((ATT_HG, None, MLA_HP, tq), lambda i, g: (g, i, 0, 0)), pl.BlockSpec((tq, ATT_HG * 128), lambda i, g: (i, g)),
                  tab, tab, tab],
        out_specs=[pl.BlockSpec((tq, gw), lambda i, g: (i, g)), tab],
        out_shape=[jax.ShapeDtypeStruct((T, MLA_H * MLA_HP), BF16), jax.ShapeDtypeStruct((T, 128), F32)],
        compiler_params=pltpu.CompilerParams(dimension_semantics=("parallel", "arbitrary")),
    )(dqt, dkr, cc, sa, sb)


_NT = (((1,), (1,)), ((), ()))


def _as_row(col, n):
    return jnp.broadcast_to(col, (n, 128)).T[0:1, :]


def _attn_fwd(q, k, vt, name):
    T = q.shape[0]
    tq = tk = min(T, ATT_TILE)
    nq = T // tq
    hg = ATT_FWD_HEADS

    def body(q_ref, k_ref, vt_ref, o_ref, lse_ref, m_ref, l_ref, acc_ref):
        i = pl.program_id(1)
        m_ref[...] = jnp.full_like(m_ref, NEG_INF)
        l_ref[...] = jnp.zeros_like(l_ref)
        acc_ref[...] = jnp.zeros_like(acc_ref)

        def step(j, diag):
            off = pl.multiple_of(j * tk, tk)
            sts = [lax.dot_general(k_ref[pl.ds(off, tk), hh * MLA_HP:(hh + 1) * MLA_HP], q_ref[:, hh * MLA_HP:(hh + 1) * MLA_HP], _NT,
                                   preferred_element_type=F32) for hh in range(hg)]
            for hh in range(hg):
                st = sts[hh]
                if diag:
                    st = jnp.where(lax.broadcasted_iota(jnp.int32, (tk, tq), 0) <= lax.broadcasted_iota(jnp.int32, (tk, tq), 1), st, NEG_INF)
                m_prev = m_ref[hh]
                m_new = jnp.maximum(m_prev, jnp.max(st, axis=0, keepdims=True))
                alpha = jnp.exp(m_prev - m_new)
                pt = jnp.exp(st - m_new)
                l_ref[hh] = alpha * l_ref[hh] + jnp.sum(pt, axis=0, keepdims=True)
                acc_ref[hh] = alpha * acc_ref[hh] + jnp.dot(vt_ref[hh, j], pt.astype(BF16), preferred_element_type=F32)
                m_ref[hh] = m_new

        def loop_body(j, carry):
            step(j, False)
            return carry

        lax.fori_loop(0, i, loop_body, 0)
        step(i, True)
        for hh in range(hg):
            o_ref[:, hh * MLA_V:(hh + 1) * MLA_V] = (acc_ref[hh] / l_ref[hh]).T.astype(BF16)
            lse_ref[hh] = m_ref[hh] + jnp.log(l_ref[hh])

    return pl.pallas_call(
        body, name=name, grid=(MLA_H // hg, nq),
        in_specs=[pl.BlockSpec((tq, hg * MLA_HP), lambda h, i: (i, h)), pl.BlockSpec((T, hg * MLA_HP), lambda h, i: (0, h)),
                  pl.BlockSpec((hg, nq, MLA_V, tk), lambda h, i: (h, 0, 0, 0))],
        out_specs=[pl.BlockSpec((tq, hg * MLA_V), lambda h, i: (i, h)), pl.BlockSpec((hg, None, 1, tq), lambda h, i: (h, i, 0, 0))],
        out_shape=[jax.ShapeDtypeStruct((T, MLA_H * MLA_V), BF16), jax.ShapeDtypeStruct((MLA_H, nq, 1, tq), F32)],
        scratch_shapes=[pltpu.VMEM((hg, 1, tq), F32), pltpu.VMEM((hg, 1, tq), F32), pltpu.VMEM((hg, MLA_V, tq), F32)],
        compiler_params=pltpu.CompilerParams(dimension_semantics=("parallel", "arbitrary")),
    )(q, k, vt)


def _attn_delta(do, o, name):
    T = do.shape[0]
    tq = min(T, ATT_TILE)

    def body(do_ref, o_ref, d_ref):
        for hh in range(MLA_H):
            cs = slice(hh * MLA_V, (hh + 1) * MLA_V)
            s = jnp.sum(do_ref[:, cs].astype(F32) * o_ref[:, cs].astype(F32), axis=-1, keepdims=True)
            d_ref[hh] = _as_row(s, tq)

    return pl.pallas_call(
        body, name=name, grid=(T // tq,),
        in_specs=[_row_spec(tq, MLA_H * MLA_V), _row_spec(tq, MLA_H * MLA_V)],
        out_specs=pl.BlockSpec((MLA_H, None, 1, tq), lambda i: (0, i, 0, 0)),
        out_shape=jax.ShapeDtypeStruct((MLA_H, T // tq, 1, tq), F32),
        compiler_params=pltpu.CompilerParams(dimension_semantics=("parallel",)),
    )(do, o)


def _attn_bwd(q, k, kt, v, do, lse, delta, name):
    T = q.shape[0]
    tq = tk = min(T, ATT_TILE)
    nq = nk = T // tq
    tsd = min(tq, ATT_SUB)
    hg = ATT_BWD_HEADS

    def body(q_ref, k_ref, kt_ref, v_ref, do_ref, lse_ref, dl_ref, dqt_ref, dkv_ref, dkr_ref, dq_acc, dk_acc, dv_acc):
        j = pl.program_id(1)

        @pl.when(j == 0)
        def _():
            dq_acc[...] = jnp.zeros_like(dq_acc)

        dk_acc[...] = jnp.zeros_like(dk_acc)
        dv_acc[...] = jnp.zeros_like(dv_acc)

        def step(i, diag):
            off = pl.multiple_of(i * tq, tq)
            ts, nsub = (tsd, tq // tsd) if diag else (tq, 1)
            for u in range(nsub):
                cols = slice(u * ts, (u + 1) * ts)
                nk_u = (u + 1) * ts if diag else tk
                rows = pl.ds(off + u * ts, ts)
                pre = []
                for hh in range(hg):
                    hq, hv = slice(hh * MLA_HP, (hh + 1) * MLA_HP), slice(hh * MLA_V, (hh + 1) * MLA_V)
                    qi, doi = q_ref[rows, hq], do_ref[rows, hv]
                    st = lax.dot_general(k_ref[:nk_u, hq], qi, _NT, preferred_element_type=F32)
                    dpt = lax.dot_general(v_ref[:nk_u, hv], doi, _NT, preferred_element_type=F32)
                    pre.append((qi, doi, st, dpt))
                for hh in range(hg):
                    hq, hv = slice(hh * MLA_HP, (hh + 1) * MLA_HP), slice(hh * MLA_V, (hh + 1) * MLA_V)
                    qi, doi, st, dpt = pre[hh]
                    if diag:
                        qcol = u * ts + lax.broadcasted_iota(jnp.int32, (nk_u, ts), 1)
                        st = jnp.where(lax.broadcasted_iota(jnp.int32, (nk_u, ts), 0) <= qcol, st, NEG_INF)
                    pt = jnp.exp(st - lse_ref[hh, i][:, cols])
                    dv_acc[:nk_u, hv] += jnp.dot(pt.astype(BF16), doi, preferred_element_type=F32)
                    dsb = (pt * (dpt - dl_ref[hh, i][:, cols])).astype(BF16)
                    dk_acc[:nk_u, hq] += jnp.dot(dsb, qi, preferred_element_type=F32)
                    dq_acc[hh, i, :, cols] += jnp.dot(kt_ref[hq, :nk_u], dsb, preferred_element_type=F32)

        def loop_body(i, carry):
            step(i, False)
            return carry

        step(j, True)
        lax.fori_loop(j + 1, nq, loop_body, 0)
        for hh in range(hg):
            a, m, b = hh * MLA_HP, hh * MLA_HP + MLA_NOPE, (hh + 1) * MLA_HP
            dkv_ref[:, a:m] = dk_acc[:, a:m].astype(BF16)
            dkv_ref[:, m:b] = dv_acc[:, hh * MLA_V:(hh + 1) * MLA_V].astype(BF16)
            dkr_ref[:, hh * 128:(hh + 1) * 128] = dk_acc[:, m:b].astype(BF16)

        @pl.when(j == nk - 1)
        def _():
            dqt_ref[...] = dq_acc[...].astype(BF16)

    stat = pl.BlockSpec((hg, nq, 1, tq), lambda h, j: (h, 0, 0, 0))
    return pl.pallas_call(
        body, name=name, grid=(MLA_H // hg, nk),
        in_specs=[pl.BlockSpec((T, hg * MLA_HP), lambda h, j: (0, h)), pl.BlockSpec((tk, hg * MLA_HP), lambda h, j: (j, h)),
                  pl.BlockSpec((hg * MLA_HP, tk), lambda h, j: (h, j)), pl.BlockSpec((tk, hg * MLA_V), lambda h, j: (j, h)),
                  pl.BlockSpec((T, hg * MLA_V), lambda h, j: (0, h)), stat, stat],
        out_specs=[pl.BlockSpec((hg, nq, MLA_HP, tq), lambda h, j: (h, 0, 0, 0)), pl.BlockSpec((tk, hg * MLA_HP), lambda h, j: (j, h)),
                   pl.BlockSpec((tk, hg * 128), lambda h, j: (j, h))],
        out_shape=[jax.ShapeDtypeStruct((MLA_H, nq, MLA_HP, tq), BF16), jax.ShapeDtypeStruct((T, MLA_H * MLA_HP), BF16),
                   jax.ShapeDtypeStruct((T, MLA_H * 128), BF16)],
        scratch_shapes=[pltpu.VMEM((hg, nq, MLA_HP, tq), F32), pltpu.VMEM((tk, hg * MLA_HP), F32), pltpu.VMEM((tk, hg * MLA_V), F32)],
        compiler_params=pltpu.CompilerParams(dimension_semantics=("parallel", "arbitrary")),
    )(q, k, kt, v, do, lse, delta)


ADA_TN = 512


def _silu(v):
    return v * (1.0 / (1.0 + jnp.exp(-v)))


def _ada_fwd(c_all, ada_w, ada_b_loc, name):
    L, D, Nc = ada_w.shape
    B = c_all.shape[0]

    def body(c_ref, w_ref, b_ref, o_ref):
        ca = _silu(c_ref[...]).astype(BF16)
        o_ref[...] = jnp.dot(ca, w_ref[...].astype(BF16), preferred_element_type=F32) + b_ref[...]

    return pl.pallas_call(
        body, name=name, grid=(L, Nc // ADA_TN),
        in_specs=[pl.BlockSpec((B, D), lambda l, n: (0, 0)), pl.BlockSpec((None, D, ADA_TN), lambda l, n: (l, 0, n)),
                  pl.BlockSpec((None, 1, ADA_TN), lambda l, n: (l, 0, n))],
        out_specs=pl.BlockSpec((None, B, ADA_TN), lambda l, n: (l, 0, n)),
        out_shape=jax.ShapeDtypeStruct((L, B, Nc), F32),
        compiler_params=pltpu.CompilerParams(dimension_semantics=("parallel", "parallel")),
    )(c_all, ada_w, ada_b_loc)


def _ada_bwd(c_all_t, dmod_loc, name):
    D, B = c_all_t.shape
    L, _, Nc = dmod_loc.shape

    def body(c_ref, d_ref, o_ref):
        ca = _silu(c_ref[...])
        dv = d_ref[...]
        acc = ca[:, 0:1] * dv[0:1, :]
        for b in range(1, B):
            acc = acc + ca[:, b:b + 1] * dv[b:b + 1, :]
        o_ref[...] = acc

    return pl.pallas_call(
        body, name=name, grid=(L, Nc // ADA_TN),
        in_specs=[pl.BlockSpec((D, B), lambda l, n: (0, 0)), pl.BlockSpec((None, B, ADA_TN), lambda l, n: (l, 0, n))],
        out_specs=pl.BlockSpec((None, D, ADA_TN), lambda l, n: (l, 0, n)),
        out_shape=jax.ShapeDtypeStruct((L, D, Nc), F32),
        compiler_params=pltpu.CompilerParams(dimension_semantics=("parallel", "parallel")),
    )(c_all_t, dmod_loc)


def _sum_lead(parts, name, out_dtype=F32):
    R, C = parts[0].shape[1:]
    n_tot = sum(p.shape[0] for p in parts)
    tr = R
    for cand in (512, 256, 128, 64, 32, 16):
        if R % cand == 0 and cand * C * 4 * n_tot <= (8 << 20):
            tr = cand
            break

    def body(*refs):
        o_ref = refs[-1]
        acc = None
        for r in refs[:-1]:
            for s in range(r.shape[0]):
                acc = r[s].astype(F32) if acc is None else acc + r[s].astype(F32)
        o_ref[...] = acc.astype(o_ref.dtype)

    return pl.pallas_call(
        body, name=name, grid=(R // tr,),
        in_specs=[pl.BlockSpec((p.shape[0], tr, C), lambda i: (0, i, 0)) for p in parts],
        out_specs=pl.BlockSpec((tr, C), lambda i: (i, 0)),
        out_shape=jax.ShapeDtypeStruct((R, C), out_dtype),
        compiler_params=pltpu.CompilerParams(dimension_semantics=("parallel",)),
    )(*parts)


_ADAM_C1 = 1.0 - ADAM_B1 ** ADAM_STEP
_ADAM_C2 = 1.0 - ADAM_B2 ** ADAM_STEP


def _adamw(w, g, m, v, name):
    shape = w.shape
    C = shape[-1]
    R = math.prod(shape[:-1]) if len(shape) > 1 else 1
    w2, g2, m2, v2 = (a.reshape(R, C) for a in (w, g, m, v))
    tr = R
    for cand in (1024, 512, 256, 128, 64, 32, 16, 8):
        if R % cand == 0 and cand * C * 4 <= (1 << 20):
            tr = cand
            break

    def body(w_ref, g_ref, m_ref, v_ref, d_ref, nm_ref, nv_ref):
        gv = g_ref[...]
        mn = ADAM_B1 * m_ref[...] + (1.0 - ADAM_B1) * gv
        vn = ADAM_B2 * v_ref[...] + (1.0 - ADAM_B2) * (gv * gv)
        nm_ref[...] = mn
        nv_ref[...] = vn
        m_hat = mn / _ADAM_C1
        v_hat = vn / _ADAM_C2
        d_ref[...] = -ADAM_LR * (m_hat / (jnp.sqrt(v_hat) + ADAM_EPS) + ADAM_WD * w_ref[...])

    spec = pl.BlockSpec((tr, C), lambda i: (i, 0))
    outs = pl.pallas_call(
        body, name=name, grid=(R // tr,),
        in_specs=[spec] * 4, out_specs=[spec] * 3,
        out_shape=[jax.ShapeDtypeStruct((R, C), F32)] * 3,
        compiler_params=pltpu.CompilerParams(dimension_semantics=("parallel",)),
    )(w2, g2, m2, v2)
    return tuple(o.reshape(shape) for o in outs)


def _row_tile(rows, cols, itemsize, budget):
    for cand in (1024, 512, 256, 128, 64, 32, 16):
        if rows % cand == 0 and cand * cols * itemsize <= budget:
            return cand
    return rows


def _sum_sel(sel, stacked, others, name, out_dtype):
    R, C = stacked.shape[1:]
    n_tot = 1 + sum(o.shape[0] for o in others)
    tr = _row_tile(R, C, 4 * n_tot, 8 << 20)

    def body(sel_ref, s_ref, *refs):
        o_ref = refs[-1]
        acc = s_ref[...].astype(F32)
        for r in refs[:-1]:
            for s in range(r.shape[0]):
                acc = acc + r[s].astype(F32)
        o_ref[...] = acc.astype(o_ref.dtype)

    return pl.pallas_call(
        body, name=name,
        grid_spec=pltpu.PrefetchScalarGridSpec(
            num_scalar_prefetch=1, grid=(R // tr,),
            in_specs=[pl.BlockSpec((None, tr, C), lambda i, s: (s[0], i, 0))] + [pl.BlockSpec((o.shape[0], tr, C), lambda i, s: (0, i, 0)) for o in others],
            out_specs=pl.BlockSpec((tr, C), lambda i, s: (i, 0))),
        out_shape=jax.ShapeDtypeStruct((R, C), out_dtype),
        compiler_params=pltpu.CompilerParams(dimension_semantics=("parallel",)),
    )(sel, stacked, *others)


def _adamw_piece(cidx, w2, m2, v2, mine, got, bufs, row0, name):
    hr, C = mine.shape
    tr = _row_tile(math.gcd(hr, row0) if row0 else hr, C, 4, 1 << 20)
    nt = hr // tr

    def body(c_ref, w_ref, m_ref, v_ref, a_ref, b_ref, _g, _d, _nm, _nv, g_ref, d_ref, nm_ref, nv_ref):
        gv = jnp.where(pl.program_id(0) == c_ref[0], a_ref[...], b_ref[...])
        mn = ADAM_B1 * m_ref[...] + (1.0 - ADAM_B1) * gv
        vn = ADAM_B2 * v_ref[...] + (1.0 - ADAM_B2) * (gv * gv)
        g_ref[...] = gv
        nm_ref[...] = mn
        nv_ref[...] = vn
        d_ref[...] = -ADAM_LR * ((mn / _ADAM_C1) / (jnp.sqrt(vn / _ADAM_C2) + ADAM_EPS) + ADAM_WD * w_ref[...])

    rows = pl.BlockSpec((tr, C), lambda hf, t, c: (row0 // tr + hf * nt + t, 0))
    mine_spec = pl.BlockSpec((tr, C), lambda hf, t, c: (jnp.where(hf == c[0], t, 0), 0))
    got_spec = pl.BlockSpec((tr, C), lambda hf, t, c: (jnp.where(hf == c[0], 0, t), 0))
    return pl.pallas_call(
        body, name=name,
        grid_spec=pltpu.PrefetchScalarGridSpec(num_scalar_prefetch=1, grid=(2, nt), in_specs=[rows] * 3 + [mine_spec, got_spec] + [_ANY_SPEC] * 4,
                                               out_specs=[rows] * 4),
        out_shape=[jax.ShapeDtypeStruct(w2.shape, F32)] * 4,
        input_output_aliases={6 + n: n for n in range(4)},
        compiler_params=pltpu.CompilerParams(dimension_semantics=("parallel", "parallel")),
    )(cidx, w2, m2, v2, mine, got, *bufs)


_VMEM_SPEC = pl.BlockSpec(memory_space=pltpu.VMEM)
_HBM_SPEC = pl.BlockSpec(memory_space=pltpu.HBM)


def _flip(v, bit):
    return (1 - v) if bit else v


def _allgather8(v, name):
    def body(v_ref, out_ref, send_sems, recv_sems, local_sem):
        x, y, c = _idx()
        me = 4 * x + 2 * y + c
        mine = pltpu.make_async_copy(v_ref, out_ref.at[me], local_sem)
        mine.start()
        sends = []
        for k in range(1, N_DEV):
            peer = (_flip(x, k & 4), _flip(y, k & 2), _flip(c, k & 1))
            cp = pltpu.make_async_remote_copy(src_ref=v_ref, dst_ref=out_ref.at[me], send_sem=send_sems.at[k - 1], recv_sem=recv_sems.at[k - 1],
                                              device_id=peer, device_id_type=MESH)
            cp.start()
            sends.append(cp)
        for k in range(1, N_DEV):
            px, py, pc = _flip(x, k & 4), _flip(y, k & 2), _flip(c, k & 1)
            src = 4 * px + 2 * py + pc
            pltpu.make_async_remote_copy(src_ref=v_ref, dst_ref=out_ref.at[src], send_sem=send_sems.at[k - 1], recv_sem=recv_sems.at[k - 1],
                                         device_id=(px, py, pc), device_id_type=MESH).wait_recv()
        for cp in sends:
            cp.wait_send()
        mine.wait()

    return pl.pallas_call(
        body, name=name,
        out_shape=jax.ShapeDtypeStruct((N_DEV, *v.shape), v.dtype),
        in_specs=[_VMEM_SPEC], out_specs=_VMEM_SPEC,
        scratch_shapes=[pltpu.SemaphoreType.DMA((N_DEV - 1,)), pltpu.SemaphoreType.DMA((N_DEV - 1,)), pltpu.SemaphoreType.DMA],
    )(v)


def _mod_exchange(modp, name):
    _, L, Nc = modp.shape

    def body(p_ref, out_ref, send_sems, recv_sems, local_sem):
        x, y, c = _idx()
        me, chip = 4 * x + 2 * y + c, 2 * x + y
        mine = pltpu.make_async_copy(p_ref.at[me], out_ref.at[chip], local_sem)
        mine.start()
        sends = []
        for k in range(1, N_CHIPS):
            px, py = _flip(x, k & 2), _flip(y, k & 1)
            cp = pltpu.make_async_remote_copy(src_ref=p_ref.at[4 * px + 2 * py + c], dst_ref=out_ref.at[chip],
                                              send_sem=send_sems.at[k - 1], recv_sem=recv_sems.at[k - 1], device_id=(px, py, c), device_id_type=MESH)
            cp.start()
            sends.append(cp)
        for k in range(1, N_CHIPS):
            px, py = _flip(x, k & 2), _flip(y, k & 1)
            pltpu.make_async_remote_copy(src_ref=p_ref.at[me], dst_ref=out_ref.at[2 * px + py], send_sem=send_sems.at[k - 1],
                                         recv_sem=recv_sems.at[k - 1], device_id=(px, py, c), device_id_type=MESH).wait_recv()
        for cp in sends:
            cp.wait_send()
        mine.wait()

    return pl.pallas_call(
        body, name=name,
        out_shape=jax.ShapeDtypeStruct((N_CHIPS, L, Nc), modp.dtype),
        in_specs=[_VMEM_SPEC], out_specs=_VMEM_SPEC,
        scratch_shapes=[pltpu.SemaphoreType.DMA((N_CHIPS - 1,)), pltpu.SemaphoreType.DMA((N_CHIPS - 1,)), pltpu.SemaphoreType.DMA],
    )(modp)


_SEM_SPEC = pl.BlockSpec(memory_space=pltpu.SEMAPHORE)
_ANY_SPEC = pl.BlockSpec(memory_space=pl.ANY)
_EFFECT = pltpu.SideEffectType.DATAFLOW_SIDE_EFFECTING


def _hbm(a):
    return pltpu.with_memory_space_constraint(a, pltpu.HBM)


def _xchip_copies(mode, srcs, lands, send_sems, recv_sems, waiting):
    x, y, c = _idx()
    chip = 2 * x + y
    out = []
    for a in range(len(srcs)):
        for k in range(1, _n_peers(mode) + 1):
            if mode == "all8":
                px, py, pc = _flip(x, k & 4), _flip(y, k & 2), _flip(c, k & 1)
                src, dst, mine = srcs[a], lands[a].at[4 * x + 2 * y + c], lands[a].at[4 * px + 2 * py + pc]
            elif mode == "scatter8":
                px, py, pc = _flip(x, k & 4), _flip(y, k & 2), _flip(c, k & 1)
                src, dst, mine = srcs[a].at[pc, 2 * px + py], lands[a].at[k - 1], lands[a].at[k - 1]
            else:
                px, py, pc = _flip(x, k & 2), _flip(y, k & 1), c
                peer = 2 * px + py
                if mode == "gather":
                    src, dst, mine = srcs[a].at[c], lands[a].at[chip, c], lands[a].at[peer, c]
                else:
                    src, dst, mine = srcs[a].at[peer], lands[a].at[k - 1], lands[a].at[k - 1]
            q = a * _n_peers(mode) + k - 1
            out.append(pltpu.make_async_remote_copy(src_ref=src, dst_ref=mine if waiting else dst, send_sem=send_sems[q], recv_sem=recv_sems[q],
                                                    device_id=(px, py, pc), device_id_type=MESH))
    return out


def _n_peers(mode):
    return N_DEV - 1 if mode in ("all8", "scatter8") else N_CHIPS - 1


def _xchip_start(mode, srcs, land_shapes, dep, name):
    n = len(srcs)
    ns = n * _n_peers(mode)

    def body(*refs):
        src_refs, land_refs = refs[:n], refs[n:2 * n]
        outs = refs[2 * n + 1:]
        for cp in _xchip_copies(mode, src_refs, land_refs, outs[:ns], outs[ns:2 * ns], waiting=False):
            cp.start()
        outs[-1][...] = jnp.zeros_like(outs[-1])

    lands = [_hbm(lax.empty(s.shape, s.dtype)) for s in land_shapes]
    outs = pl.pallas_call(
        body, name=name,
        out_shape=(*[pltpu.SemaphoreType.DMA(())] * (2 * ns), *[pltpu.HBM(s.shape, s.dtype) for s in srcs],
                   *[pltpu.HBM(s.shape, s.dtype) for s in land_shapes], jax.ShapeDtypeStruct((8, 128), F32)),
        in_specs=[_HBM_SPEC] * (2 * n) + [_ANY_SPEC],
        out_specs=(*[_SEM_SPEC] * (2 * ns), *[_HBM_SPEC] * (2 * n), _VMEM_SPEC),
        input_output_aliases={i: 2 * ns + i for i in range(2 * n)},
        compiler_params=pltpu.CompilerParams(has_side_effects=_EFFECT),
    )(*[_hbm(s) for s in srcs], *lands, dep)
    return list(outs[:ns]), list(outs[ns:2 * ns]), list(outs[2 * ns:2 * ns + n]), list(outs[2 * ns + n:2 * ns + 2 * n]), outs[-1]


def _xchip_wait(mode, send_sems, recv_sems, srcs, lands, after, name):
    n = len(srcs)
    ns = n * _n_peers(mode)

    def body(*refs):
        src_refs, land_refs = refs[:n], refs[n:2 * n]
        sems = refs[2 * n:2 * n + 2 * ns]
        for cp in _xchip_copies(mode, src_refs, land_refs, sems[:ns], sems[ns:], waiting=True):
            cp.wait_send()
            cp.wait_recv()

    outs = pl.pallas_call(
        body, name=name,
        out_shape=(*[pltpu.HBM(s.shape, s.dtype) for s in srcs], *[pltpu.HBM(s.shape, s.dtype) for s in lands]),
        in_specs=[_HBM_SPEC] * (2 * n) + [_SEM_SPEC] * (2 * ns) + [_ANY_SPEC] * len(after),
        out_specs=tuple([_HBM_SPEC] * (2 * n)),
        input_output_aliases={i: i for i in range(2 * n)},
        compiler_params=pltpu.CompilerParams(has_side_effects=_EFFECT),
    )(*srcs, *lands, *send_sems, *recv_sems, *after)
    return list(outs[:n]), list(outs[n:])


def _sibling_fwd(lands, name):
    n = len(lands)

    def body(*refs):
        outs = refs[n:2 * n]
        send_sems, recv_sems = refs[2 * n:]
        x, y, c = _idx()
        sib = (x, y, 1 - c)
        sends = []
        for a in range(n):
            for k in range(1, N_CHIPS):
                src = 2 * _flip(x, k & 2) + _flip(y, k & 1)
                cp = pltpu.make_async_remote_copy(src_ref=outs[a].at[src, c], dst_ref=outs[a].at[src, c], send_sem=send_sems.at[a, k - 1],
                                                  recv_sem=recv_sems.at[a, k - 1], device_id=sib, device_id_type=MESH)
                cp.start()
                sends.append(cp)
        for a in range(n):
            for k in range(1, N_CHIPS):
                src = 2 * _flip(x, k & 2) + _flip(y, k & 1)
                pltpu.make_async_remote_copy(src_ref=outs[a].at[src, c], dst_ref=outs[a].at[src, 1 - c], send_sem=send_sems.at[a, k - 1],
                                             recv_sem=recv_sems.at[a, k - 1], device_id=sib, device_id_type=MESH).wait_recv()
        for cp in sends:
            cp.wait_send()

    return pl.pallas_call(
        body, name=name,
        out_shape=[jax.ShapeDtypeStruct(s.shape, s.dtype) for s in lands],
        in_specs=[_HBM_SPEC] * n, out_specs=[_HBM_SPEC] * n,
        input_output_aliases={i: i for i in range(n)},
        scratch_shapes=[pltpu.SemaphoreType.DMA((n, N_CHIPS - 1)), pltpu.SemaphoreType.DMA((n, N_CHIPS - 1))],
    )(*lands)


def _sibling_swap(parts, name):
    n = len(parts)

    def body(*refs):
        ins, outs = refs[:n], refs[n:2 * n]
        send_sems, recv_sems = refs[2 * n:]
        x, y, c = _idx()
        cps = []
        for a in range(n):
            cp = pltpu.make_async_remote_copy(src_ref=ins[a].at[1 - c], dst_ref=outs[a], send_sem=send_sems.at[a], recv_sem=recv_sems.at[a],
                                              device_id=(x, y, 1 - c), device_id_type=MESH)
            cp.start()
            cps.append(cp)
        for cp in cps:
            cp.wait()

    return pl.pallas_call(
        body, name=name,
        out_shape=[jax.ShapeDtypeStruct(p.shape[1:], p.dtype) for p in parts],
        in_specs=[_HBM_SPEC] * n, out_specs=[_HBM_SPEC] * n,
        scratch_shapes=[pltpu.SemaphoreType.DMA((n,)), pltpu.SemaphoreType.DMA((n,))],
    )(*parts)


def _sibling_send(halves, name):
    n = len(halves)

    def body(*refs):
        ins, outs = refs[:n], refs[n:2 * n]
        send_sems, recv_sems = refs[2 * n:]
        x, y, c = _idx()
        cps = []
        for a in range(n):
            cp = pltpu.make_async_remote_copy(src_ref=ins[a], dst_ref=outs[a], send_sem=send_sems.at[a], recv_sem=recv_sems.at[a],
                                              device_id=(x, y, 1 - c), device_id_type=MESH)
            cp.start()
            cps.append(cp)
        for cp in cps:
            cp.wait()

    return pl.pallas_call(
        body, name=name,
        out_shape=[jax.ShapeDtypeStruct(h.shape, h.dtype) for h in halves],
        in_specs=[_HBM_SPEC] * n, out_specs=[_HBM_SPEC] * n,
        scratch_shapes=[pltpu.SemaphoreType.DMA((n,)), pltpu.SemaphoreType.DMA((n,))],
    )(*halves)


def _col_full(g):
    k, n = g.shape[1], g.shape[2]
    return g.transpose(1, 0, 2).reshape(k, N_CHIPS * n)


def _col_blocks(w):
    k, n = w.shape
    return w.reshape(k, N_CHIPS, n // N_CHIPS).transpose(1, 0, 2)


def _row_blocks(w):
    k, n = w.shape
    return w.reshape(N_CHIPS, k // N_CHIPS, n)


_UQ_HEAD = MLA_NOPE + MLA_ROPE

_LAT = MLA_QL + MLA_KVL + MLA_ROPE
_POOL_R = len(POOL_WINDOWS) * (POOL_GD // N_CHIPS)

_PIECE_KINDS = {
    "mlp_w1": (D_MODEL, D_MODEL, lambda g: g, _col_blocks),
    "mlp_w2": (D_MODEL, D_MODEL, lambda g: g.reshape(4 * D_MODEL, D_MODEL), _row_blocks),
    "pool_w": (_POOL_R, POOL_GD,
               lambda g: g.reshape(N_CHIPS, len(POOL_WINDOWS), POOL_GD // N_CHIPS, POOL_GD).transpose(1, 0, 2, 3).reshape(len(POOL_WINDOWS), POOL_GD, POOL_GD),
               lambda w: w.reshape(len(POOL_WINDOWS), N_CHIPS, POOL_GD // N_CHIPS, POOL_GD).transpose(1, 0, 2, 3).reshape(N_CHIPS, _POOL_R, POOL_GD)),
    "sgu_w_in": (D_MODEL, 2 * SGU_W // N_CHIPS, _col_full, _col_blocks),
    "sgu_w_out": (SGU_W // N_CHIPS, D_MODEL, lambda g: g.reshape(SGU_W, D_MODEL), _row_blocks),
    "mla_w_dq_dkv": (D_MODEL // N_CHIPS, _LAT, lambda g: jnp.pad(g.reshape(D_MODEL, _LAT), ((0, 0), (0, MLA_LATP - _LAT))),
                     lambda w: _row_blocks(w[:, :_LAT])),
    "mla_w_uq": (MLA_QL, MLA_H * _UQ_HEAD // N_CHIPS,
                 lambda g: jnp.pad(_col_full(g).reshape(MLA_QL, MLA_H, _UQ_HEAD), ((0, 0), (0, 0), (0, MLA_HP - _UQ_HEAD))).reshape(MLA_QL, MLA_H * MLA_HP),
                 lambda w: _col_blocks(w.reshape(MLA_QL, MLA_H, MLA_HP)[:, :, :_UQ_HEAD].reshape(MLA_QL, MLA_H * _UQ_HEAD))),
    "mla_w_ukv": (MLA_KVL, MLA_H * (MLA_NOPE + MLA_V) // N_CHIPS, _col_full, _col_blocks),
    "mla_w_o": (MLA_H * MLA_V // N_CHIPS, D_MODEL, lambda g: g.reshape(MLA_H * MLA_V, D_MODEL), _row_blocks),
}
_MIXER_KINDS = (("pool_w",), ("sgu_w_in", "sgu_w_out"), ("mla_w_dq_dkv", "mla_w_uq", "mla_w_ukv", "mla_w_o"))


def _layer_pieces(i):
    return [(k, i // N_MIXERS) for k in _MIXER_KINDS[i % N_MIXERS]] + [("mlp_w1", i), ("mlp_w2", i)]


def _rope_tables(positions):
    inv_freq = ROPE_THETA ** (-jnp.arange(0, MLA_ROPE, 2, dtype=F32) / MLA_ROPE)
    ang = positions.astype(F32)[:, None] * inv_freq
    cos, sin = jnp.cos(ang), jnp.sin(ang)
    z32, z64 = jnp.zeros_like(cos), jnp.zeros((positions.shape[0], 64), F32)
    return (jnp.concatenate([cos, cos, z64], axis=1), jnp.concatenate([-sin, z32, z64], axis=1), jnp.concatenate([z32, sin, z64], axis=1))


def _local_step(x, positions, target, mod, S, weights_of, grads_of):
    D = D_MODEL
    cc, sa, sb = _rope_tables(positions)
    mods = [[mod[i:i + 1, n * D:(n + 1) * D] for n in range(6)] for i in range(DEPTH)]
    h_dtype = lambda i: F32 if i % N_MIXERS == 0 else BF16
    saved = []
    h = _norm_mod_fwd(x, S["norm_mix_g"][0:1], mods[0][1], mods[0][0], h_dtype(0), "l0_norm1")
    for i in range(DEPTH):
        sh1, sc1, g1, sh2, sc2, g2 = mods[i]
        kind, j = i % N_MIXERS, i // N_MIXERS
        gmlp = S["norm_mlp_g"][i:i + 1]
        W = weights_of(i, "mix", x)
        st = {"x": x}
        norm2 = ((gmlp, "n"), (sc2, "n"), (sh2, "n"))
        if kind == 0:
            x2, pooled, ypre, h2 = _pool_fwd(h, W["pool_w"], S["pool_scale"][j:j + 1], x, g1, gmlp, sc2, sh2, f"l{i}_pool")
            st.update(pooled=pooled, y=ypre)
        elif kind == 1:
            zz = _mm(h, W["sgu_w_in"], out_dtypes=(F32,), name=f"l{i}_sgu_in")
            bs_t = S["sgu_b_s"].T
            gated = _sgu_gate_fwd(zz, S["sgu_ln_g"], S["sgu_ln_b"], S["sgu_w_s"], bs_t, f"l{i}_sgu_gate")
            x2, y, h2 = _mm(gated, W["sgu_w_out"], epi=_epi_residual_norm, extras=((x, "mn"), (g1, "n"), *norm2), out_dtypes=(F32, BF16, BF16),
                            tn=D, name=f"l{i}_sgu_out")
            st.update(h=h, zz=zz, gated=gated, y=y, bs_t=bs_t)
        else:
            lat = _mm(h, W["mla_w_dq_dkv"], out_dtypes=(F32,), name=f"l{i}_mla_lat")
            cqn, ckvn, krot = _mla_lat_fwd(lat, S["mla_q_norm_g"], S["mla_kv_norm_g"], cc, sa, sb, f"l{i}_mla_latn")
            q = _mm(cqn, W["mla_w_uq"], epi=_epi_q_rope, extras=((cc, "m"), (sa, "m"), (sb, "m")), name=f"l{i}_mla_uq")
            k, kt, v, vt = _mla_ukv(ckvn, W["mla_w_ukv"], krot, f"l{i}_mla_ukv")
            o, lse = _attn_fwd(q, k, vt, f"l{i}_attn")
            x2, y, h2 = _mm(o, W["mla_w_o"], epi=_epi_residual_norm, extras=((x, "mn"), (g1, "n"), *norm2), out_dtypes=(F32, BF16, BF16),
                            tn=D, name=f"l{i}_mla_o")
            st.update(h=h, lat=lat, cqn=cqn, ckvn=ckvn, q=q, k=k, kt=kt, v=v, o=o, lse=lse, y=y)
        W = {**W, **weights_of(i, "mlp", x2)}
        z, r2 = _mm(h2, W["mlp_w1"], epi=_epi_sq_relu, out_dtypes=(BF16, BF16), name=f"l{i}_mlp1")
        if i + 1 < DEPTH:
            norm1 = ((S["norm_mix_g"][i + 1:i + 2], "n"), (mods[i + 1][1], "n"), (mods[i + 1][0], "n"))
            x3, o2, h = _mm(z, W["mlp_w2"], epi=_epi_residual_norm, extras=((x2, "mn"), (g2, "n"), *norm1), out_dtypes=(F32, BF16, h_dtype(i + 1)),
                            tn=D, name=f"l{i}_mlp2")
        else:
            x3, o2 = _mm(z, W["mlp_w2"], epi=_epi_residual, extras=((x2, "mn"), (g2, "n")), out_dtypes=(F32, BF16), name=f"l{i}_mlp2")
        st.update(x2=x2, h2=h2, z=z, r2=r2, o2=o2, W=W)
        saved.append(st)
        x = x3

    loss, dx, dfinal_g = _loss_head(x, target, S["final_g"], "loss_head")

    gS = {"final_g": dfinal_g, "norm_mix_g": [None] * DEPTH, "norm_mlp_g": [None] * DEPTH, "pool_scale": [None] * 2}
    dmod = [None] * DEPTH
    do2, dg2 = _resid_bwd(dx, saved[-1]["o2"], mods[-1][5], f"l{DEPTH - 1}_b_res2")
    started = None
    for i in reversed(range(DEPTH)):
        st = saved[i]
        W, gW = st["W"], {}
        sh1, sc1, g1, sh2, sc2, g2 = mods[i]
        kind, j = i % N_MIXERS, i // N_MIXERS
        gmix, gmlp = S["norm_mix_g"][i:i + 1], S["norm_mlp_g"][i:i + 1]
        da = _mm(do2, W["mlp_w2"], tb=True, epi=lambda acc, rt: (acc * rt.astype(F32),), extras=((st["r2"], "mn"),), after=started, name=f"l{i}_b_dz")
        gW["mlp_w2"] = _mm(st["z"], do2, ta=True, chip_blocks="row", name=f"l{i}_b_dw2")
        dh2 = _mm(da, W["mlp_w1"], tb=True, out_dtypes=(F32,), name=f"l{i}_b_dh2")
        gW["mlp_w1"] = _mm(st["h2"], da, ta=True, chip_blocks="col", name=f"l{i}_b_dw1")
        dx2, dgmlp, dsc2, dsh2, dy, q1 = _norm_mod_bwd(st["x2"], dh2, dx, gmlp, sc2, f"l{i}_b_norm2", res=(st["y"], g1))
        gS["norm_mlp_g"][i] = dgmlp
        if kind == 0:
            dh, dpw, dpsc, dg1 = _pool_bwd(dy, st["pooled"], W["pool_w"], S["pool_scale"][j:j + 1], g1, q1, f"l{i}_b_pool")
            gW["pool_w"] = dpw.astype(BF16)
            gS["pool_scale"][j] = dpsc
        elif kind == 1:
            dg1 = q1
            dgated = _mm(dy, W["sgu_w_out"], tb=True, name=f"l{i}_b_dgated")
            gW["sgu_w_out"] = _mm(st["gated"], dy, ta=True, name=f"l{i}_b_dwout")
            dzz, dws, dbs, dlg, dlb = _sgu_gate_bwd(st["zz"], dgated, S["sgu_ln_g"], S["sgu_ln_b"], S["sgu_w_s"], st["bs_t"], f"l{i}_b_sgu_gate")
            gS.update(sgu_w_s=dws, sgu_b_s=dbs[:, :, 0], sgu_ln_g=dlg, sgu_ln_b=dlb)
            dh = _mm(dzz, W["sgu_w_in"], tb=True, out_dtypes=(F32,), name=f"l{i}_b_dh_sgu")
            gW["sgu_w_in"] = _mm(st["h"], dzz, ta=True, name=f"l{i}_b_dwin")
        else:
            dg1 = q1
            do = _mm(dy, W["mla_w_o"], tb=True, name=f"l{i}_b_do")
            gW["mla_w_o"] = _mm(st["o"], dy, ta=True, name=f"l{i}_b_dwo")
            delta = _attn_delta(do, st["o"], f"l{i}_b_delta")
            dqt, dkv, dkr = _attn_bwd(st["q"], st["k"], st["kt"], st["v"], do, st["lse"], delta, f"l{i}_b_attn")
            dqpad, dkrot = _mla_prep_bwd(dqt, dkr, cc, sa, sb, f"l{i}_b_mla_prep")
            dcqn = _mm(dqpad, W["mla_w_uq"], tb=True, out_dtypes=(F32,), name=f"l{i}_b_dcq")
            gW["mla_w_uq"] = _mm(st["cqn"], dqpad, ta=True, name=f"l{i}_b_dwuq")
            dckvn = _mm(dkv, W["mla_w_ukv"], tb=True, out_dtypes=(F32,), name=f"l{i}_b_dckv")
            gW["mla_w_ukv"] = _mm(st["ckvn"], dkv, ta=True, name=f"l{i}_b_dwukv")
            dlat, dqg, dkvg = _mla_lat_bwd(st["lat"], dcqn, dckvn, dkrot, S["mla_q_norm_g"], S["mla_kv_norm_g"], cc, sa, sb, f"l{i}_b_mla_latn")
            gS.update(mla_q_norm_g=dqg, mla_kv_norm_g=dkvg)
            dh = _mm(dlat, W["mla_w_dq_dkv"], tb=True, out_dtypes=(F32,), name=f"l{i}_b_dh_mla")
            gW["mla_w_dq_dkv"] = _mm(st["h"], dlat, ta=True, name=f"l{i}_b_dwdq")
        if i > 0:
            dx, dgmix, dsc1, dsh1, do2_prev, dg2_prev = _norm_mod_bwd(st["x"], dh, dx2, gmix, sc1, f"l{i}_b_norm1", res=(saved[i - 1]["o2"], mods[i - 1][5]))
        else:
            dx, dgmix, dsc1, dsh1 = _norm_mod_bwd(st["x"], dh, dx2, gmix, sc1, f"l{i}_b_norm1")
        gS["norm_mix_g"][i] = dgmix
        dmod[i] = jnp.concatenate([dsh1, dsc1, dg1, dsh2, dsc2, dg2], axis=1)
        started = grads_of(i, gW, dx)
        if i > 0:
            do2, dg2 = do2_prev, dg2_prev

    for n in ("norm_mix_g", "norm_mlp_g", "pool_scale"):
        gS[n] = jnp.concatenate(gS[n], axis=0)
    return loss, dx, gS, jnp.concatenate(dmod, axis=0)


_SMALL = {
    "norm_mix_g": (DEPTH, D_MODEL), "norm_mlp_g": (DEPTH, D_MODEL), "sgu_ln_g": (1, SGU_W), "sgu_ln_b": (1, SGU_W),
    "sgu_w_s": (SGU_H, SGU_CHUNK, SGU_CHUNK), "sgu_b_s": (SGU_H, SGU_CHUNK), "mla_kv_norm_g": (1, MLA_KVL), "final_g": (1, D_MODEL),
    "pool_scale": (2, D_MODEL), "mla_q_norm_g": (1, MLA_QL), "loss": (1, 128), "dmod": (DEPTH, 6 * D_MODEL),
}
_PACK_W = 1024


def _pack(vals):
    flat = jnp.concatenate([v.reshape(-1) for v in vals])
    rows = -(-flat.shape[0] // (8 * _PACK_W)) * 8
    return jnp.pad(flat, (0, rows * _PACK_W - flat.shape[0])).reshape(rows, _PACK_W)


def _unpack(buf, shapes):
    flat, out, off = buf.reshape(-1), [], 0
    for s in shapes:
        n = math.prod(s)
        out.append(flat[off:off + n].reshape(s))
        off += n
    return out


def kernel(x, c, positions, ada_w, ada_b, norm_mix_g, norm_mlp_g, pool_w, pool_scale, sgu_w_in, sgu_ln_g, sgu_ln_b, sgu_w_s, sgu_b_s, sgu_w_out, mla_w_dq_dkv, mla_q_norm_g, mla_kv_norm_g, mla_w_uq, mla_w_ukv, mla_w_o, mlp_w1, mlp_w2, final_g, loss_target, m_ada_w, m_ada_b, m_norm_mix_g, m_norm_mlp_g, m_pool_w, m_pool_scale, m_sgu_w_in, m_sgu_ln_g, m_sgu_ln_b, m_sgu_w_s, m_sgu_b_s, m_sgu_w_out, m_mla_w_dq_dkv, m_mla_q_norm_g, m_mla_kv_norm_g, m_mla_w_uq, m_mla_w_ukv, m_mla_w_o, m_mlp_w1, m_mlp_w2, m_final_g, v_ada_w, v_ada_b, v_norm_mix_g, v_norm_mlp_g, v_pool_w, v_pool_scale, v_sgu_w_in, v_sgu_ln_g, v_sgu_ln_b, v_sgu_w_s, v_sgu_b_s, v_sgu_w_out, v_mla_w_dq_dkv, v_mla_q_norm_g, v_mla_kv_norm_g, v_mla_w_uq, v_mla_w_ukv, v_mla_w_o, v_mlp_w1, v_mlp_w2, v_final_g):
    P = dict(ada_w=ada_w, ada_b=ada_b, norm_mix_g=norm_mix_g, norm_mlp_g=norm_mlp_g, pool_w=pool_w, pool_scale=pool_scale, sgu_w_in=sgu_w_in,
             sgu_ln_g=sgu_ln_g, sgu_ln_b=sgu_ln_b, sgu_w_s=sgu_w_s, sgu_b_s=sgu_b_s, sgu_w_out=sgu_w_out, mla_w_dq_dkv=mla_w_dq_dkv,
             mla_q_norm_g=mla_q_norm_g, mla_kv_norm_g=mla_kv_norm_g, mla_w_uq=mla_w_uq, mla_w_ukv=mla_w_ukv, mla_w_o=mla_w_o, mlp_w1=mlp_w1,
             mlp_w2=mlp_w2, final_g=final_g)
    M = dict(ada_w=m_ada_w, ada_b=m_ada_b, norm_mix_g=m_norm_mix_g, norm_mlp_g=m_norm_mlp_g, pool_w=m_pool_w, pool_scale=m_pool_scale,
             sgu_w_in=m_sgu_w_in, sgu_ln_g=m_sgu_ln_g, sgu_ln_b=m_sgu_ln_b, sgu_w_s=m_sgu_w_s, sgu_b_s=m_sgu_b_s, sgu_w_out=m_sgu_w_out,
             mla_w_dq_dkv=m_mla_w_dq_dkv, mla_q_norm_g=m_mla_q_norm_g, mla_kv_norm_g=m_mla_kv_norm_g, mla_w_uq=m_mla_w_uq, mla_w_ukv=m_mla_w_ukv,
             mla_w_o=m_mla_w_o, mlp_w1=m_mlp_w1, mlp_w2=m_mlp_w2, final_g=m_final_g)
    V = dict(ada_w=v_ada_w, ada_b=v_ada_b, norm_mix_g=v_norm_mix_g, norm_mlp_g=v_norm_mlp_g, pool_w=v_pool_w, pool_scale=v_pool_scale,
             sgu_w_in=v_sgu_w_in, sgu_ln_g=v_sgu_ln_g, sgu_ln_b=v_sgu_ln_b, sgu_w_s=v_sgu_w_s, sgu_b_s=v_sgu_b_s, sgu_w_out=v_sgu_w_out,
             mla_w_dq_dkv=v_mla_w_dq_dkv, mla_q_norm_g=v_mla_q_norm_g, mla_kv_norm_g=v_mla_kv_norm_g, mla_w_uq=v_mla_w_uq, mla_w_ukv=v_mla_w_ukv,
             mla_w_o=v_mla_w_o, mlp_w1=v_mlp_w1, mlp_w2=v_mlp_w2, final_g=v_final_g)
    order = list(P)
    xi, yi, ci = _idx()
    chip = 2 * xi + yi
    D = D_MODEL
    n_ada = ada_w.shape[2]

    pre = _allgather8(_pack([c, pool_scale, mla_q_norm_g]), "ag_small")
    flat = pre.reshape(N_DEV, -1)
    c_all = flat[:, :D]
    ps_all = flat[0::2, D:D + 2 * (D // N_CHIPS)].reshape(N_CHIPS, 2, D // N_CHIPS).transpose(1, 0, 2).reshape(2, D)
    q0 = D + 2 * (D // N_CHIPS)
    qg_all = flat[0::2, q0:q0 + MLA_QL // N_CHIPS].reshape(1, MLA_QL)

    ada_b_loc = lax.dynamic_slice_in_dim(ada_b, chip * n_ada, n_ada, axis=1)[:, None, :]
    modp = _ada_fwd(c_all, ada_w, ada_b_loc, "ada_fwd")
    mod = _mod_exchange(modp.transpose(1, 0, 2), "mod_exchange").transpose(1, 0, 2).reshape(DEPTH, 6 * D)

    S = dict(norm_mix_g=norm_mix_g, norm_mlp_g=norm_mlp_g, pool_scale=ps_all, sgu_ln_g=sgu_ln_g, sgu_ln_b=sgu_ln_b, sgu_w_s=sgu_w_s[0],
             sgu_b_s=sgu_b_s[0], mla_q_norm_g=qg_all, mla_kv_norm_g=mla_kv_norm_g, final_g=final_g[None, :])
    cidx, ownidx = jnp.reshape(ci, (1,)).astype(jnp.int32), jnp.reshape(N_CHIPS * ci + chip, (1,)).astype(jnp.int32)
    view2d = lambda a: a.reshape(-1, a.shape[-1])

    def piece_rows(kind, blk):
        r = _PIECE_KINDS[kind][0]
        return blk * r, r

    groups = [_layer_pieces(0)[:-2], _layer_pieces(0)[-2:], _layer_pieces(1)[:-2], _layer_pieces(1)[-2:], _layer_pieces(2), _layer_pieces(3)]
    start_after = {1: (2, 3), 2: (4,), 4: (5,)}
    gathers = {}

    def gather_start(g, dep):
        srcs, shapes = [], []
        for kind, blk in groups[g]:
            r0, r = piece_rows(kind, blk)
            cdim = _PIECE_KINDS[kind][1]
            srcs.append(view2d(P[kind])[r0:r0 + r].astype(BF16).reshape(2, r // 2, cdim))
            shapes.append(jax.ShapeDtypeStruct((N_CHIPS, 2, r // 2, cdim), BF16))
        gathers[g] = _xchip_start("gather", srcs, shapes, dep, f"ag_start_g{g}")

    def gather_finish(g, after):
        ssem, rsem, srcs, lands, _ = gathers.pop(g)
        deps = [after]
        for nxt in start_after.get(g, ()):
            gather_start(nxt, deps[-1])
            deps.append(gathers[nxt][-1])
        srcs, lands = _xchip_wait("gather", ssem, rsem, srcs, lands, deps, f"ag_wait_g{g}")
        lands = _sibling_fwd(lands, f"ag_sibling_g{g}")
        W = {}
        for (kind, _), s, land in zip(groups[g], srcs, lands, strict=True):
            r, cdim, to_full, _ = _PIECE_KINDS[kind]
            W[kind] = to_full(lax.dynamic_update_index_in_dim(land, s, chip, 0).reshape(N_CHIPS, r, cdim))
        return W

    def weights_of(i, part, x_i):
        if i < 2:
            return gather_finish(2 * i + (part == "mlp"), x_i)
        return gather_finish(i + 2, x_i) if part == "mix" else {}

    scatters = {}
    bufs = {n: tuple(lax.empty(view2d(P[n]).shape, F32) for _ in range(4)) for n in _PIECE_KINDS}

    def scatter_start(i, gW, dep):
        pcs = _layer_pieces(i)
        blocked = []
        for kind, _ in pcs:
            r, cdim, _, to_blocks = _PIECE_KINDS[kind]
            g = gW[kind]
            blocked.append(g if g.ndim == 4 else to_blocks(g).reshape(N_CHIPS, 2, r // 2, cdim).transpose(1, 0, 2, 3))
        shapes = [jax.ShapeDtypeStruct((N_DEV - 1, *b.shape[2:]), BF16) for b in blocked]
        scatters[i] = (pcs, *_xchip_start("scatter8", blocked, shapes, dep, f"rs_start_l{i}"))
        return scatters[i][-1]

    def scatter_finish(i, after):
        pcs, ssem, rsem, blocked, lands, _ = scatters.pop(i)
        blocked, lands = _xchip_wait("scatter8", ssem, rsem, blocked, lands, after, f"rs_wait_l{i}")
        halves = [_sum_sel(ownidx, b.reshape(2 * N_CHIPS, *b.shape[2:]), [l], f"rs_sum_l{i}_{kind}", F32)
                  for (kind, _), b, l in zip(pcs, blocked, lands, strict=True)]
        got = _sibling_send(halves, f"rs_merge_l{i}")
        for (kind, blk), mine, other in zip(pcs, halves, got, strict=True):
            r0, _ = piece_rows(kind, blk)
            bufs[kind] = tuple(_adamw_piece(cidx, view2d(P[kind]), view2d(M[kind]), view2d(V[kind]), mine, other, bufs[kind], r0,
                                            f"adamw_l{i}_{kind}"))
        return lands[0]

    first_layer = {}

    def grads_of(i, gW, dx_i):
        if i == 0:
            first_layer.update(gW)
            return None
        dep = scatter_finish(i + 1, [dx_i]) if i + 1 in scatters else dx_i
        return scatter_start(i, gW, dep)

    gather_start(0, mod)
    gather_start(1, gathers[0][-1])
    mod = mod + gathers[1][-1][0, 0]
    loss_l, dx, gS, dmod = _local_step(x[0], positions[0], loss_target[0], mod, S, weights_of, grads_of)

    gS["dmod"] = dmod
    gS["loss"] = loss_l
    packed = _pack([gS[n] for n in _SMALL])
    sg = _xchip_start("all8", [packed], [jax.ShapeDtypeStruct((N_DEV, *packed.shape), F32)], dx, "sg_start")
    tok0 = scatter_start(0, first_layer, sg[-1])[0, 0]
    scatter_finish(1, [dx, scatters[0][-1]])
    sg_src, sg_land = _xchip_wait("all8", sg[0], sg[1], sg[2], sg[3], [bufs[n][0] for n in ("mlp_w1", "mlp_w2", "sgu_w_in", "sgu_w_out")], "sg_wait")
    small = lax.dynamic_update_index_in_dim(sg_land[0], sg_src[0], 4 * xi + 2 * yi + ci, 0) + tok0
    small_sum = _unpack(_sum_lead([small], "sum_small_grads"), list(_SMALL.values()))
    G = dict(zip(_SMALL, small_sum, strict=True))
    grads = {
        "ada_b": G["dmod"], "norm_mix_g": G["norm_mix_g"], "norm_mlp_g": G["norm_mlp_g"], "sgu_ln_g": G["sgu_ln_g"], "sgu_ln_b": G["sgu_ln_b"],
        "sgu_w_s": G["sgu_w_s"][None], "sgu_b_s": G["sgu_b_s"][None], "mla_kv_norm_g": G["mla_kv_norm_g"], "final_g": G["final_g"][0],
        "pool_scale": lax.dynamic_slice_in_dim(G["pool_scale"], chip * (D // N_CHIPS), D // N_CHIPS, axis=1),
        "mla_q_norm_g": lax.dynamic_slice_in_dim(G["mla_q_norm_g"], chip * (MLA_QL // N_CHIPS), MLA_QL // N_CHIPS, axis=1),
    }
    dmod_all = _unpack(small, [(N_DEV,) + (small.shape[1] * _PACK_W,)])[0]
    off = sum(math.prod(s) for n, s in _SMALL.items() if n != "dmod")
    dmod_all = dmod_all[:, off:off + DEPTH * 6 * D].reshape(N_DEV, DEPTH, 6 * D)
    dmod_loc = lax.dynamic_slice_in_dim(dmod_all, chip * n_ada, n_ada, axis=2).transpose(1, 0, 2)
    grads["ada_w"] = _ada_bwd(c_all.T, dmod_loc, "ada_bwd")

    deltas, new_m, new_v = {}, {}, {}
    for n in order:
        if n not in _PIECE_KINDS:
            deltas[n], new_m[n], new_v[n] = _adamw(P[n], grads[n].reshape(P[n].shape), M[n], V[n], f"adamw_{n}")
    scatter_finish(0, [deltas["ada_w"], deltas["sgu_w_s"]] + [bufs[n][0] for n in ("mlp_w1", "mlp_w2", "sgu_w_in", "mla_w_o")])
    for n in _PIECE_KINDS:
        grads[n], deltas[n], new_m[n], new_v[n] = (b.reshape(P[n].shape) for b in bufs[n])
    return (G["loss"][0, 0], dx[None], *[grads[n].reshape(P[n].shape) for n in order], *[deltas[n] for n in order], *[new_m[n] for n in order],
            *[new_v[n] for n in order])
```

```python
import math

import jax
import jax.numpy as jnp
from jax import lax
from jax.experimental import pallas as pl
from jax.experimental.pallas import tpu as pltpu

F32, BF16 = jnp.float32, jnp.bfloat16
MESH = pl.DeviceIdType.MESH

D_MODEL = 1024
DEPTH = 4
N_MIXERS = 3
POOL_WINDOWS = (2, 4, 8, 16)
POOL_GD = D_MODEL // len(POOL_WINDOWS)
POOL_HALO = 16
SGU_CHUNK = 128
SGU_W = D_MODEL
SGU_HD = 128
SGU_H = SGU_W // SGU_HD
MLA_H = 16
MLA_QL = 256
MLA_KVL = 128
MLA_NOPE = 128
MLA_ROPE = 64
MLA_V = 128
MLA_HP = 256
MLA_LATP = 512
ROPE_THETA = 10000.0
RMS_EPS = 1e-6
LN_EPS = 1e-5
SM_SCALE = (MLA_NOPE + MLA_ROPE) ** -0.5
NEG_INF = -1e30
ADAM_LR, ADAM_B1, ADAM_B2, ADAM_EPS, ADAM_WD, ADAM_STEP = 0.001, 0.9, 0.999, 1e-08, 0.01, 10
N_CHIPS = 4
N_DEV = 8
ROW_TILE = 512
ATT_TILE = 512
ATT_SUB = 256
ATT_FWD_HEADS = 4
ATT_BWD_HEADS = 2
MM_EPI_COLS = 256
MM_VMEM_BUDGET = 40 << 20


def _idx():
    return lax.axis_index("x"), lax.axis_index("y"), lax.axis_index("c")


def _mm(a, b, *, name, ta=False, tb=False, epi=None, extras=(), out_dtypes=(BF16,), tm=1024, tn=1024, tk=1024, chip_blocks=None, after=None,
        epi_cols=None):
    if ta:
        K, M = a.shape
    else:
        M, K = a.shape
    b_chips = b.ndim == 3
    if b_chips:
        assert b.shape[0] == N_CHIPS
        Kb, N = (N_CHIPS * b.shape[2], b.shape[1]) if tb else (b.shape[1], N_CHIPS * b.shape[2])
    elif tb:
        N, Kb = b.shape
    else:
        Kb, N = b.shape
    assert K == Kb, (a.shape, b.shape, ta, tb)
    if b_chips and not tb:
        tn = min(tn, N // N_CHIPS)
    if chip_blocks == "col":
        tm, tn = min(tm, M // 2), min(tn, N // N_CHIPS)
    elif chip_blocks == "row":
        tm = min(tm, M // N_CHIPS // 2)
    tm, tn, tk = min(tm, M), min(tn, N), min(tk, K)

    def vmem_bytes(tm_, tk_):
        per_mn = sum(arr.dtype.itemsize for arr, kind in extras if kind == "mn") + sum(jnp.dtype(dt).itemsize for dt in out_dtypes)
        return 2 * (tm_ * tk_ * a.dtype.itemsize + tk_ * tn * b.dtype.itemsize + tm_ * tn * per_mn)

    if vmem_bytes(tm, K) <= MM_VMEM_BUDGET:
        tk = K
    elif tm >= 512 and vmem_bytes(tm // 2, K) <= MM_VMEM_BUDGET:
        tm, tk = tm // 2, K
    assert M % tm == 0 and N % tn == 0 and K % tk == 0, (M, N, K, tm, tn, tk)
    nk = K // tk
    assert epi_cols is None or (nk == 1 and not ta and not (b_chips and tb) and tn % epi_cols == 0)
    a_spec = pl.BlockSpec((tk, tm), lambda i, j, k: (k, i)) if ta else pl.BlockSpec((tm, tk), lambda i, j, k: (i, k))
    b_spec = pl.BlockSpec((tn, tk), lambda i, j, k: (j, k)) if tb else pl.BlockSpec((tk, tn), lambda i, j, k: (k, j))
    if b_chips and tb:
        assert nk == 1 and not ta
        b_spec = pl.BlockSpec((N_CHIPS, tn, K // N_CHIPS), lambda i, j, k: (0, j, 0))
    elif b_chips:
        per = N // N_CHIPS // tn
        b_spec = pl.BlockSpec((None, tk, tn), lambda i, j, k: (j // per, k, j % per))
    ex_specs = []
    for arr, kind in extras:
        if kind == "mn":
            ex_specs.append(pl.BlockSpec((tm, tn), lambda i, j, k: (i, j)))
        elif kind == "n":
            ex_specs.append(pl.BlockSpec((1, tn), lambda i, j, k: (0, j)))
        else:
            ex_specs.append(pl.BlockSpec((tm, arr.shape[1]), lambda i, j, k: (i, 0)))
    n_ex, n_out = len(extras), len(out_dtypes)
    n_in = 2 + n_ex + (after is not None)
    dims = (((0 if ta else 1,), (1 if tb else 0,)), ((), ()))

    def body(*refs):
        a_ref, b_ref = refs[0], refs[1]
        ex_refs = refs[2:2 + n_ex]
        out_refs = refs[n_in:n_in + n_out]
        if b_chips and tb:
            kc = K // N_CHIPS
            part = None
            for cb in range(N_CHIPS):
                p = lax.dot_general(a_ref[:, cb * kc:(cb + 1) * kc].astype(BF16), b_ref[cb].astype(BF16), dims, preferred_element_type=F32)
                part = p if part is None else part + p
        elif epi_cols is not None:
            av = a_ref[...].astype(BF16)
            chunk = lambda cc: lax.dot_general(av, (b_ref[cc * epi_cols:(cc + 1) * epi_cols, :] if tb else b_ref[:, cc * epi_cols:(cc + 1) * epi_cols])
                                               .astype(BF16), dims, preferred_element_type=F32)
            acc = chunk(0)
            for cc in range(tn // epi_cols):
                nxt = chunk(cc + 1) if cc + 1 < tn // epi_cols else None
                cs = slice(cc * epi_cols, (cc + 1) * epi_cols)
                for r, o in zip(out_refs, epi(acc, *[r[:, cs] for r in ex_refs]), strict=True):
                    r[:, cs] = o.astype(r.dtype)
                acc = nxt
            return
        else:
            part = lax.dot_general(a_ref[...].astype(BF16), b_ref[...].astype(BF16), dims, preferred_element_type=F32)

        def finish(acc):
            outs = epi(acc, *[r[...] for r in ex_refs]) if epi is not None else (acc,)
            for r, o in zip(out_refs, outs, strict=True):
                r[...] = o.astype(r.dtype)

        if nk == 1:
            finish(part)
        else:
            acc_ref = refs[-1]
            k = pl.program_id(2)

            @pl.when(k == 0)
            def _():
                acc_ref[...] = part

            @pl.when(k > 0)
            def _():
                acc_ref[...] += part

            @pl.when(k == nk - 1)
            def _():
                finish(acc_ref[...])

    out_specs = [pl.BlockSpec((tm, tn), lambda i, j, k: (i, j)) for _ in range(n_out)]
    out_shape = [jax.ShapeDtypeStruct((M, N), dt) for dt in out_dtypes]
    if chip_blocks is not None:
        assert n_out == 1
        if chip_blocks == "col":
            rh, cb = M // 2 // tm, N // N_CHIPS // tn
            out_specs = [pl.BlockSpec((None, None, tm, tn), lambda i, j, k: (i // rh, j // cb, i % rh, j % cb))]
            out_shape = [jax.ShapeDtypeStruct((2, N_CHIPS, M // 2, N // N_CHIPS), out_dtypes[0])]
        else:
            rh = M // N_CHIPS // 2 // tm
            out_specs = [pl.BlockSpec((None, None, tm, tn), lambda i, j, k: ((i // rh) % 2, i // (2 * rh), i % rh, j))]
            out_shape = [jax.ShapeDtypeStruct((2, N_CHIPS, M // N_CHIPS // 2, N), out_dtypes[0])]
    outs = pl.pallas_call(
        body,
        name=name,
        grid=(M // tm, N // tn, nk),
        in_specs=[a_spec, b_spec, *ex_specs] + ([pl.BlockSpec(memory_space=pl.ANY)] if after is not None else []),
        out_specs=out_specs,
        out_shape=out_shape,
        scratch_shapes=[pltpu.VMEM((tm, tn), F32)] if nk > 1 else [],
        compiler_params=pltpu.CompilerParams(dimension_semantics=("parallel", "parallel", "arbitrary")),
    )(a, b, *[arr for arr, _ in extras], *([after] if after is not None else []))
    return outs[0] if n_out == 1 else tuple(outs)


def _epi_sq_relu(acc):
    r = jnp.maximum(acc, 0.0)
    return r * r, 2.0 * r


def _epi_residual(acc, x, g):
    return x + g * acc, acc


def _rms_mod(xv, gain, sc, sh):
    r = lax.rsqrt(jnp.mean(xv * xv, axis=-1, keepdims=True) + RMS_EPS)
    return ((xv * r) * gain) * (1.0 + sc) + sh


def _epi_residual_norm(acc, x, g, gain, sc, sh):
    xn = x + g * acc
    return xn, acc, _rms_mod(xn, gain, sc, sh)


def _row_spec(tr, d):
    return pl.BlockSpec((tr, d), lambda i: (i, 0))


def _vec_spec(d):
    return pl.BlockSpec((1, d), lambda i: (0, 0))


def _colsum(v):
    return jnp.sum(v, axis=0, keepdims=True)


def _norm_mod_fwd(x, gain, sc, sh, out_dtype, name):
    T, D = x.shape
    tr = min(T, ROW_TILE)

    def body(x_ref, g_ref, sc_ref, sh_ref, o_ref):
        o_ref[...] = _rms_mod(x_ref[...], g_ref[...], sc_ref[...], sh_ref[...]).astype(o_ref.dtype)

    return pl.pallas_call(
        body, name=name, grid=(T // tr,),
        in_specs=[_row_spec(tr, D), _vec_spec(D), _vec_spec(D), _vec_spec(D)],
        out_specs=_row_spec(tr, D),
        out_shape=jax.ShapeDtypeStruct((T, D), out_dtype),
        compiler_params=pltpu.CompilerParams(dimension_semantics=("parallel",)),
    )(x, gain, sc, sh)


def _norm_mod_bwd(x, dh, dres, gain, sc, name, res=None):
    T, D = x.shape
    tr = min(T, ROW_TILE)

    def body(x_ref, dh_ref, dres_ref, g_ref, sc_ref, *refs):
        dx_ref, dg_ref, dsc_ref, dsh_ref = refs[-6:-2] if res is not None else refs

        @pl.when(pl.program_id(0) == 0)
        def _():
            dg_ref[...] = jnp.zeros_like(dg_ref)
            dsc_ref[...] = jnp.zeros_like(dsc_ref)
            dsh_ref[...] = jnp.zeros_like(dsh_ref)
            if res is not None:
                refs[-1][...] = jnp.zeros_like(refs[-1])

        xv = x_ref[...]
        r = lax.rsqrt(jnp.mean(xv * xv, axis=-1, keepdims=True) + RMS_EPS)
        xn = xv * r
        dhv = dh_ref[...].astype(F32)
        dsh_ref[...] += _colsum(dhv)
        dsc_ref[...] += _colsum(dhv * (xn * g_ref[...]))
        dt = dhv * (1.0 + sc_ref[...])
        dg_ref[...] += _colsum(dt * xn)
        dxn = dt * g_ref[...]
        dxv = dres_ref[...] + r * (dxn - xn * jnp.mean(dxn * xn, axis=-1, keepdims=True))
        dx_ref[...] = dxv
        if res is not None:
            y_ref, gr_ref, dy_ref, q_ref = refs[0], refs[1], refs[-2], refs[-1]
            dy_ref[...] = (gr_ref[...] * dxv).astype(BF16)
            q_ref[...] += _colsum(dxv * y_ref[...].astype(F32))

    extra_in, extra_spec = ([], []) if res is None else (list(res), [_row_spec(tr, D), _vec_spec(D)])
    return pl.pallas_call(
        body, name=name, grid=(T // tr,),
        in_specs=[_row_spec(tr, D), _row_spec(tr, D), _row_spec(tr, D), _vec_spec(D), _vec_spec(D), *extra_spec],
        out_specs=[_row_spec(tr, D), _vec_spec(D), _vec_spec(D), _vec_spec(D)] + ([_row_spec(tr, D), _vec_spec(D)] if res is not None else []),
        out_shape=[jax.ShapeDtypeStruct((T, D), F32)] + [jax.ShapeDtypeStruct((1, D), F32)] * 3
        + ([jax.ShapeDtypeStruct((T, D), BF16), jax.ShapeDtypeStruct((1, D), F32)] if res is not None else []),
        compiler_params=pltpu.CompilerParams(dimension_semantics=("arbitrary",)),
    )(x, dh, dres, gain, sc, *extra_in)


def _resid_bwd(dx, y, g, name):
    T, D = dx.shape
    tr = min(T, ROW_TILE)

    def body(dx_ref, y_ref, g_ref, dy_ref, q_ref):
        @pl.when(pl.program_id(0) == 0)
        def _():
            q_ref[...] = jnp.zeros_like(q_ref)

        dxv = dx_ref[...]
        dy_ref[...] = (g_ref[...] * dxv).astype(BF16)
        q_ref[...] += _colsum(dxv * y_ref[...].astype(F32))

    return pl.pallas_call(
        body, name=name, grid=(T // tr,),
        in_specs=[_row_spec(tr, D), _row_spec(tr, D), _vec_spec(D)],
        out_specs=[_row_spec(tr, D), _vec_spec(D)],
        out_shape=[jax.ShapeDtypeStruct((T, D), BF16), jax.ShapeDtypeStruct((1, D), F32)],
        compiler_params=pltpu.CompilerParams(dimension_semantics=("arbitrary",)),
    )(dx, y, g)


def _loss_head(x, target, gain, name):
    T, D = x.shape
    tr = min(T, ROW_TILE)

    def body(x_ref, t_ref, g_ref, loss_ref, dx_ref, dg_ref):
        @pl.when(pl.program_id(0) == 0)
        def _():
            loss_ref[...] = jnp.zeros_like(loss_ref)
            dg_ref[...] = jnp.zeros_like(dg_ref)

        xv = x_ref[...]
        r = lax.rsqrt(jnp.mean(xv * xv, axis=-1, keepdims=True) + RMS_EPS)
        xn = xv * r
        err = xn * g_ref[...] - t_ref[...]
        row = jnp.mean(err * err, axis=-1, keepdims=True)
        loss_ref[...] += 0.5 * jnp.sum(row, axis=0, keepdims=True)
        dy = err * (1.0 / D)
        dg_ref[...] += _colsum(dy * xn)
        dxn = dy * g_ref[...]
        dx_ref[...] = r * (dxn - xn * jnp.mean(dxn * xn, axis=-1, keepdims=True))

    return pl.pallas_call(
        body, name=name, grid=(T // tr,),
        in_specs=[_row_spec(tr, D), _row_spec(tr, D), _vec_spec(D)],
        out_specs=[_vec_spec(128), _row_spec(tr, D), _vec_spec(D)],
        out_shape=[jax.ShapeDtypeStruct((1, 128), F32), jax.ShapeDtypeStruct((T, D), F32), jax.ShapeDtypeStruct((1, D), F32)],
        compiler_params=pltpu.CompilerParams(dimension_semantics=("arbitrary",)),
    )(x, target, gain)


def _pool_fwd(h, w, scale, x, g1, gmlp, sc2, sh2, name):
    T, D = h.shape
    tr = min(T, ROW_TILE)

    def body(h_ref, w_ref, sc_ref, x_ref, g_ref, gm_ref, sc2_ref, sh2_ref, x2_ref, pooled_ref, ypre_ref, h2_ref, halo_ref):
        i = pl.program_id(0)

        @pl.when(i == 0)
        def _():
            halo_ref[...] = jnp.zeros_like(halo_ref)

        hv = h_ref[...]
        buf = jnp.concatenate([halo_ref[...], hv], axis=0)
        halo_ref[...] = hv[tr - POOL_HALO:, :]
        t = (i * tr + lax.broadcasted_iota(jnp.int32, (tr, 1), 0)).astype(F32)
        for gi, win in enumerate(POOL_WINDOWS):
            cols = slice(gi * POOL_GD, (gi + 1) * POOL_GD)
            val = buf[:, cols]
            sh = 1
            while sh < win:
                val = val + pltpu.roll(val, sh, axis=0)
                sh *= 2
            pooled = val[POOL_HALO:, :] / jnp.minimum(t + 1.0, float(win)) - hv[:, cols]
            pb = pooled.astype(BF16)
            pooled_ref[:, cols] = pb
            yp = jnp.dot(pb, w_ref[gi], preferred_element_type=F32)
            ypre_ref[:, cols] = yp.astype(BF16)
            x2_ref[:, cols] = x_ref[:, cols] + g_ref[:, cols] * (yp * sc_ref[:, cols])
        h2_ref[...] = _rms_mod(x2_ref[...], gm_ref[...], sc2_ref[...], sh2_ref[...]).astype(BF16)

    return pl.pallas_call(
        body, name=name, grid=(T // tr,),
        in_specs=[_row_spec(tr, D), pl.BlockSpec(w.shape, lambda i: (0, 0, 0)), _vec_spec(D), _row_spec(tr, D), _vec_spec(D), _vec_spec(D),
                  _vec_spec(D), _vec_spec(D)],
        out_specs=[_row_spec(tr, D)] * 4,
        out_shape=[jax.ShapeDtypeStruct((T, D), F32), jax.ShapeDtypeStruct((T, D), BF16), jax.ShapeDtypeStruct((T, D), BF16),
                   jax.ShapeDtypeStruct((T, D), BF16)],
        scratch_shapes=[pltpu.VMEM((POOL_HALO, D), F32)],
        compiler_params=pltpu.CompilerParams(dimension_semantics=("arbitrary",)),
    )(h, w, scale, x, g1, gmlp, sc2, sh2)


def _pool_bwd(dy, pooled, w, scale, g1, q, name):
    T, D = dy.shape
    tr = min(T, ROW_TILE)
    nt = T // tr
    ltot = tr + POOL_HALO

    def body(dy_ref, pooled_ref, w_ref, sc_ref, g_ref, q_ref, dh_ref, dw_ref, dsc_ref, dg_ref, halo_ref):
        i = pl.program_id(0)

        @pl.when(i == 0)
        def _():
            halo_ref[...] = jnp.zeros_like(halo_ref)
            dw_ref[...] = jnp.zeros_like(dw_ref)
            dsc_ref[...] = g_ref[...] * q_ref[...]
            dg_ref[...] = sc_ref[...] * q_ref[...]

        t = ((nt - 1 - i) * tr + lax.broadcasted_iota(jnp.int32, (tr, 1), 0)).astype(F32)
        for gi, win in enumerate(POOL_WINDOWS):
            cols = slice(gi * POOL_GD, (gi + 1) * POOL_GD)
            dyb = (dy_ref[:, cols].astype(F32) * sc_ref[:, cols]).astype(BF16)
            dw_ref[gi] += lax.dot_general(pooled_ref[:, cols], dyb, (((0,), (0,)), ((), ())), preferred_element_type=F32)
            dpool = lax.dot_general(dyb, w_ref[gi], (((1,), (1,)), ((), ())), preferred_element_type=F32)
            qv = dpool / jnp.minimum(t + 1.0, float(win))
            val = jnp.concatenate([qv, halo_ref[:, cols]], axis=0)
            halo_ref[:, cols] = qv[:POOL_HALO, :]
            sh = 1
            while sh < win:
                val = val + pltpu.roll(val, ltot - sh, axis=0)
                sh *= 2
            dh_ref[:, cols] = (val[:tr, :] - dpool).astype(BF16)

    rev = pl.BlockSpec((tr, D), lambda i: (nt - 1 - i, 0))
    return pl.pallas_call(
        body, name=name, grid=(nt,),
        in_specs=[rev, rev, pl.BlockSpec(w.shape, lambda i: (0, 0, 0)), _vec_spec(D), _vec_spec(D), _vec_spec(D)],
        out_specs=[rev, pl.BlockSpec(w.shape, lambda i: (0, 0, 0)), _vec_spec(D), _vec_spec(D)],
        out_shape=[jax.ShapeDtypeStruct((T, D), BF16), jax.ShapeDtypeStruct(w.shape, F32),
                   jax.ShapeDtypeStruct((1, D), F32), jax.ShapeDtypeStruct((1, D), F32)],
        scratch_shapes=[pltpu.VMEM((POOL_HALO, D), F32)],
        compiler_params=pltpu.CompilerParams(dimension_semantics=("arbitrary",)),
    )(dy, pooled, w, scale, g1, q)


_INV_SQRT2 = 0.7071067811865476
_INV_SQRT2PI = 0.3989422804014327


def _gelu(v):
    return 0.5 * v * (1.0 + lax.erf(v * _INV_SQRT2))


def _gelu_grad(v):
    return 0.5 * (1.0 + lax.erf(v * _INV_SQRT2)) + v * jnp.exp(-0.5 * v * v) * _INV_SQRT2PI


def _sgu_ln(v, g, b):
    mu = jnp.mean(v, axis=-1, keepdims=True)
    xc = v - mu
    rstd = lax.rsqrt(jnp.mean(xc * xc, axis=-1, keepdims=True) + LN_EPS)
    xh = xc * rstd
    return xh, rstd, xh * g + b


def _tril_mask():
    return lax.broadcasted_iota(jnp.int32, (SGU_CHUNK, SGU_CHUNK), 0) >= lax.broadcasted_iota(jnp.int32, (SGU_CHUNK, SGU_CHUNK), 1)


SGU_TILE = 256


def _sgu_gate_fwd(zz, ln_g, ln_b, ws, bs_t, name):
    T = zz.shape[0]
    ts = min(T, SGU_TILE)

    def body(zz_ref, g_ref, b_ref, ws_ref, bs_ref, out_ref):
        z = _gelu(zz_ref[...])
        u = z[:, :SGU_W]
        _, _, vn = _sgu_ln(z[:, SGU_W:], g_ref[...], b_ref[...])
        vb = vn.astype(BF16)
        tril = _tril_mask()
        for hh in range(SGU_H):
            wm = jnp.where(tril, ws_ref[hh], 0.0).astype(BF16)
            bcol = bs_ref[:, hh:hh + 1]
            cs = slice(hh * SGU_HD, (hh + 1) * SGU_HD)
            for j in range(ts // SGU_CHUNK):
                rs = slice(j * SGU_CHUNK, (j + 1) * SGU_CHUNK)
                mixed = jnp.dot(wm, vb[rs, cs], preferred_element_type=F32) + bcol
                out_ref[rs, cs] = (u[rs, cs] * mixed).astype(BF16)

    return pl.pallas_call(
        body, name=name, grid=(T // ts,),
        in_specs=[_row_spec(ts, 2 * SGU_W), _vec_spec(SGU_W), _vec_spec(SGU_W),
                  pl.BlockSpec(ws.shape, lambda i: (0, 0, 0)), pl.BlockSpec(bs_t.shape, lambda i: (0, 0))],
        out_specs=_row_spec(ts, SGU_W),
        out_shape=jax.ShapeDtypeStruct((T, SGU_W), BF16),
        compiler_params=pltpu.CompilerParams(dimension_semantics=("parallel",)),
    )(zz, ln_g, ln_b, ws, bs_t)


def _sgu_gate_bwd(zz, dgated, ln_g, ln_b, ws, bs_t, name):
    T = zz.shape[0]
    ts = min(T, SGU_TILE)
    nt = T // ts

    def body(zz_ref, dg_ref, g_ref, b_ref, ws_ref, bs_ref, dzz_ref, dws_ref, dbs_ref, dlg_ref, dlb_ref, dlo_ref, dmx_ref):
        i = pl.program_id(0)

        @pl.when(i == 0)
        def _():
            dws_ref[...] = jnp.zeros_like(dws_ref)
            dmx_ref[...] = jnp.zeros_like(dmx_ref)
            dlg_ref[...] = jnp.zeros_like(dlg_ref)
            dlb_ref[...] = jnp.zeros_like(dlb_ref)

        zzv = zz_ref[...]
        z = _gelu(zzv)
        u = z[:, :SGU_W]
        xh, rstd, vn = _sgu_ln(z[:, SGU_W:], g_ref[...], b_ref[...])
        vb = vn.astype(BF16)
        dgv = dg_ref[...].astype(F32)
        tril = _tril_mask()
        for hh in range(SGU_H):
            wm = jnp.where(tril, ws_ref[hh], 0.0).astype(BF16)
            bcol = bs_ref[:, hh:hh + 1]
            cs = slice(hh * SGU_HD, (hh + 1) * SGU_HD)
            for j in range(ts // SGU_CHUNK):
                rs = slice(j * SGU_CHUNK, (j + 1) * SGU_CHUNK)
                mixed = jnp.dot(wm, vb[rs, cs], preferred_element_type=F32) + bcol
                dmixed = dgv[rs, cs] * u[rs, cs]
                dzz_ref[rs, cs] = (dgv[rs, cs] * mixed * _gelu_grad(zzv[rs, cs])).astype(BF16)
                dmb = dmixed.astype(BF16)
                dws_ref[hh] += lax.dot_general(dmb, vb[rs, cs], (((1,), (1,)), ((), ())), preferred_element_type=F32)
                dmx_ref[hh] += dmixed
                dlo_ref[rs, cs] = lax.dot_general(wm, dmb, (((0,), (0,)), ((), ())), preferred_element_type=F32)
        dlo = dlo_ref[...]
        dlg_ref[...] += _colsum(dlo * xh)
        dlb_ref[...] += _colsum(dlo)
        dxh = dlo * g_ref[...]
        dv = rstd * (dxh - jnp.mean(dxh, axis=-1, keepdims=True) - xh * jnp.mean(dxh * xh, axis=-1, keepdims=True))
        dzz_ref[:, SGU_W:] = (dv * _gelu_grad(zzv[:, SGU_W:])).astype(BF16)

        @pl.when(i == nt - 1)
        def _():
            tril_f = tril.astype(F32)
            for hh in range(SGU_H):
                dws_ref[hh] = dws_ref[hh] * tril_f
                dbs_ref[hh] = jnp.broadcast_to(jnp.sum(dmx_ref[hh], axis=-1, keepdims=True), (SGU_CHUNK, SGU_HD))

    full3 = pl.BlockSpec(ws.shape, lambda i: (0, 0, 0))
    return pl.pallas_call(
        body, name=name, grid=(nt,),
        in_specs=[_row_spec(ts, 2 * SGU_W), _row_spec(ts, SGU_W), _vec_spec(SGU_W), _vec_spec(SGU_W), full3,
                  pl.BlockSpec(bs_t.shape, lambda i: (0, 0))],
        out_specs=[_row_spec(ts, 2 * SGU_W), full3, full3, _vec_spec(SGU_W), _vec_spec(SGU_W)],
        out_shape=[jax.ShapeDtypeStruct((T, 2 * SGU_W), BF16), jax.ShapeDtypeStruct(ws.shape, F32), jax.ShapeDtypeStruct(ws.shape, F32),
                   jax.ShapeDtypeStruct((1, SGU_W), F32), jax.ShapeDtypeStruct((1, SGU_W), F32)],
        scratch_shapes=[pltpu.VMEM((ts, SGU_W), F32), pltpu.VMEM(ws.shape, F32)],
        compiler_params=pltpu.CompilerParams(dimension_semantics=("arbitrary",)),
    )(zz, dgated, ln_g, ln_b, ws, bs_t)


def _rope_fwd(blk, cc, sa, sb):
    return blk * cc + pltpu.roll(blk, 96, axis=1) * sa + pltpu.roll(blk, 32, axis=1) * sb


def _rope_bwd(d, cc, sa, sb):
    return d * cc + pltpu.roll(d * sa, 32, axis=1) + pltpu.roll(d * sb, 96, axis=1)


def _rms(v, g):
    r = lax.rsqrt(jnp.mean(v * v, axis=-1, keepdims=True) + RMS_EPS)
    vn = v * r
    return vn, r, vn * g


def _rms_bwd(dy, vn, r, g):
    dvn = dy * g
    return r * (dvn - vn * jnp.mean(dvn * vn, axis=-1, keepdims=True))


MLA_TILE = 256
_KV0 = MLA_QL
_KR0 = MLA_QL + MLA_KVL


def _mla_lat_fwd(lat, qg, kvg, cc, sa, sb, name):
    T = lat.shape[0]
    tr = min(T, ROW_TILE)

    def body(lat_ref, qg_ref, kvg_ref, cc_ref, sa_ref, sb_ref, cq_ref, ckv_ref, kr_ref):
        lv = lat_ref[...]
        cq_ref[...] = _rms(lv[:, :_KV0], qg_ref[...])[2].astype(BF16)
        ckv_ref[...] = _rms(lv[:, _KV0:_KR0], kvg_ref[...])[2].astype(BF16)
        kr_ref[...] = _rope_fwd(lv[:, _KR0:], cc_ref[...], sa_ref[...], sb_ref[...])

    return pl.pallas_call(
        body, name=name, grid=(T // tr,),
        in_specs=[_row_spec(tr, MLA_LATP), _vec_spec(MLA_QL), _vec_spec(MLA_KVL), _row_spec(tr, 128), _row_spec(tr, 128), _row_spec(tr, 128)],
        out_specs=[_row_spec(tr, MLA_QL), _row_spec(tr, MLA_KVL), _row_spec(tr, 128)],
        out_shape=[jax.ShapeDtypeStruct((T, MLA_QL), BF16), jax.ShapeDtypeStruct((T, MLA_KVL), BF16), jax.ShapeDtypeStruct((T, 128), F32)],
        compiler_params=pltpu.CompilerParams(dimension_semantics=("parallel",)),
    )(lat, qg, kvg, cc, sa, sb)


def _mla_lat_bwd(lat, dcqn, dckvn, dkrot, qg, kvg, cc, sa, sb, name):
    T = lat.shape[0]
    tr = min(T, ROW_TILE)

    def body(lat_ref, dcq_ref, dckv_ref, dkr_ref, qg_ref, kvg_ref, cc_ref, sa_ref, sb_ref, dlat_ref, dqg_ref, dkvg_ref):
        @pl.when(pl.program_id(0) == 0)
        def _():
            dqg_ref[...] = jnp.zeros_like(dqg_ref)
            dkvg_ref[...] = jnp.zeros_like(dkvg_ref)

        lv = lat_ref[...]
        qn, qr, _ = _rms(lv[:, :_KV0], qg_ref[...])
        kn, kr, _ = _rms(lv[:, _KV0:_KR0], kvg_ref[...])
        dcq = dcq_ref[...]
        dckv = dckv_ref[...]
        dqg_ref[...] += _colsum(dcq * qn)
        dkvg_ref[...] += _colsum(dckv * kn)
        dlat_ref[:, :_KV0] = _rms_bwd(dcq, qn, qr, qg_ref[...]).astype(BF16)
        dlat_ref[:, _KV0:_KR0] = _rms_bwd(dckv, kn, kr, kvg_ref[...]).astype(BF16)
        dlat_ref[:, _KR0:] = _rope_bwd(dkr_ref[...], cc_ref[...], sa_ref[...], sb_ref[...]).astype(BF16)

    return pl.pallas_call(
        body, name=name, grid=(T // tr,),
        in_specs=[_row_spec(tr, MLA_LATP), _row_spec(tr, MLA_QL), _row_spec(tr, MLA_KVL), _row_spec(tr, 128),
                  _vec_spec(MLA_QL), _vec_spec(MLA_KVL), _row_spec(tr, 128), _row_spec(tr, 128), _row_spec(tr, 128)],
        out_specs=[_row_spec(tr, MLA_LATP), _vec_spec(MLA_QL), _vec_spec(MLA_KVL)],
        out_shape=[jax.ShapeDtypeStruct((T, MLA_LATP), BF16), jax.ShapeDtypeStruct((1, MLA_QL), F32), jax.ShapeDtypeStruct((1, MLA_KVL), F32)],
        compiler_params=pltpu.CompilerParams(dimension_semantics=("arbitrary",)),
    )(lat, dcqn, dckvn, dkrot, qg, kvg, cc, sa, sb)


def _epi_q_rope(acc, cc, sa, sb):
    out = []
    for hh in range(acc.shape[1] // MLA_HP):
        a, m, b = hh * MLA_HP, hh * MLA_HP + MLA_NOPE, (hh + 1) * MLA_HP
        out += [acc[:, a:m] * SM_SCALE, _rope_fwd(acc[:, m:b], cc, sa, sb) * SM_SCALE]
    return (jnp.concatenate(out, axis=1),)


def _mla_ukv(ckvn, w_ukv, krot, name):
    T = ckvn.shape[0]
    tr = min(T, ATT_TILE)
    hg = ATT_HG
    gw = hg * MLA_HP

    def body(a_ref, w_ref, kr_ref, ko_ref, kt_ref, vo_ref, vt_ref):
        acc = jnp.dot(a_ref[...], w_ref[...], preferred_element_type=F32)
        kr = kr_ref[...]
        krb, krt = kr.astype(BF16), kr.T.astype(BF16)
        for hh in range(hg):
            a, m, b = hh * MLA_HP, hh * MLA_HP + MLA_NOPE, (hh + 1) * MLA_HP
            kn, vh = acc[:, a:m], acc[:, m:b]
            ko_ref[:, a:m] = kn.astype(BF16)
            ko_ref[:, m:b] = krb
            kt_ref[a:m, :] = kn.T.astype(BF16)
            kt_ref[m:b, :] = krt
            vo_ref[:, hh * MLA_V:(hh + 1) * MLA_V] = vh.astype(BF16)
            vt_ref[hh] = vh.T.astype(BF16)

    tk = min(T, ATT_TILE)
    per = tk // tr
    HW = MLA_H * MLA_HP
    return pl.pallas_call(
        body, name=name, grid=(T // tr, MLA_H // hg),
        in_specs=[pl.BlockSpec((tr, MLA_KVL), lambda i, g: (i, 0)), pl.BlockSpec((MLA_KVL, gw), lambda i, g: (0, g)),
                  pl.BlockSpec((tr, 128), lambda i, g: (i, 0))],
        out_specs=[pl.BlockSpec((tr, gw), lambda i, g: (i, g)), pl.BlockSpec((gw, tr), lambda i, g: (g, i)),
                   pl.BlockSpec((tr, hg * MLA_V), lambda i, g: (i, g)),
                   pl.BlockSpec((hg, None, MLA_V, tr), lambda i, g: (g, i // per, 0, i % per))],
        out_shape=[jax.ShapeDtypeStruct((T, HW), BF16), jax.ShapeDtypeStruct((HW, T), BF16), jax.ShapeDtypeStruct((T, MLA_H * MLA_V), BF16),
                   jax.ShapeDtypeStruct((MLA_H, T // tk, MLA_V, tk), BF16)],
        compiler_params=pltpu.CompilerParams(dimension_semantics=("parallel", "parallel")),
    )(ckvn, w_ukv, krot)


ATT_HG = 4


def _mla_prep_bwd(dqt, dkr, cc, sa, sb, name):
    _, nq, _, tq = dqt.shape
    T = nq * tq
    gw = ATT_HG * MLA_HP

    def body(dq_ref, dk_ref, cc_ref, sa_ref, sb_ref, dqp_ref, dkr_ref):
        @pl.when(pl.program_id(1) == 0)
        def _():
            dkr_ref[...] = jnp.zeros_like(dkr_ref)

        cc, sa, sb = cc_ref[...], sa_ref[...], sb_ref[...]
        acc = jnp.zeros((tq, 128), F32)
        for hh in range(ATT_HG):
            a, m, b = hh * MLA_HP, hh * MLA_HP + MLA_NOPE, (hh + 1) * MLA_HP
            dqh = dq_ref[hh].astype(F32).T * SM_SCALE
            dqp_ref[:, a:m] = dqh[:, :MLA_NOPE].astype(BF16)
            dqp_ref[:, m:b] = _rope_bwd(dqh[:, MLA_NOPE:], cc, sa, sb).astype(BF16)
            acc = acc + dk_ref[:, hh * 128:(hh + 1) * 128].astype(F32)
        dkr_ref[...] += acc

    tab = pl.BlockSpec((tq, 128), lambda i, g: (i, 0))
    return pl.pallas_call(
        body, name=name, grid=(nq, MLA_H // ATT_HG),
        in_specs=[pl.BlockSpec((ATT_HG, None, MLA_HP, tq), lambda i, g: (g, i, 0, 0)), pl.BlockSpec((tq, ATT_HG * 128), lambda i, g: (i, g)),
                  tab, tab, tab],
        out_specs=[pl.BlockSpec((tq, gw), lambda i, g: (i, g)), tab],
        out_shape=[jax.ShapeDtypeStruct((T, MLA_H * MLA_HP), BF16), jax.ShapeDtypeStruct((T, 128), F32)],
        compiler_params=pltpu.CompilerParams(dimension_semantics=("parallel", "arbitrary")),
    )(dqt, dkr, cc, sa, sb)


_NT = (((1,), (1,)), ((), ()))


def _as_row(col, n):
    return jnp.broadcast_to(col, (n, 128)).T[0:1, :]


def _attn_fwd(q, k, vt, name):
    T = q.shape[0]
    tq = tk = min(T, ATT_TILE)
    nq = T // tq
    hg = ATT_FWD_HEADS

    def body(q_ref, k_ref, vt_ref, o_ref, lse_ref, m_ref, l_ref, acc_ref):
        i = pl.program_id(1)
        m_ref[...] = jnp.full_like(m_ref, NEG_INF)
        l_ref[...] = jnp.zeros_like(l_ref)
        acc_ref[...] = jnp.zeros_like(acc_ref)

        def step(j, diag):
            off = pl.multiple_of(j * tk, tk)
            sts = [lax.dot_general(k_ref[pl.ds(off, tk), hh * MLA_HP:(hh + 1) * MLA_HP], q_ref[:, hh * MLA_HP:(hh + 1) * MLA_HP], _NT,
                                   preferred_element_type=F32) for hh in range(hg)]
            for hh in range(hg):
                st = sts[hh]
                if diag:
                    st = jnp.where(lax.broadcasted_iota(jnp.int32, (tk, tq), 0) <= lax.broadcasted_iota(jnp.int32, (tk, tq), 1), st, NEG_INF)
                m_prev = m_ref[hh]
                m_new = jnp.maximum(m_prev, jnp.max(st, axis=0, keepdims=True))
                alpha = jnp.exp(m_prev - m_new)
                pt = jnp.exp(st - m_new)
                l_ref[hh] = alpha * l_ref[hh] + jnp.sum(pt, axis=0, keepdims=True)
                acc_ref[hh] = alpha * acc_ref[hh] + jnp.dot(vt_ref[hh, j], pt.astype(BF16), preferred_element_type=F32)
                m_ref[hh] = m_new

        def loop_body(j, carry):
            step(j, False)
            return carry

        lax.fori_loop(0, i, loop_body, 0)
        step(i, True)
        for hh in range(hg):
            o_ref[:, hh * MLA_V:(hh + 1) * MLA_V] = (acc_ref[hh] / l_ref[hh]).T.astype(BF16)
            lse_ref[hh] = m_ref[hh] + jnp.log(l_ref[hh])

    return pl.pallas_call(
        body, name=name, grid=(MLA_H // hg, nq),
        in_specs=[pl.BlockSpec((tq, hg * MLA_HP), lambda h, i: (i, h)), pl.BlockSpec((T, hg * MLA_HP), lambda h, i: (0, h)),
                  pl.BlockSpec((hg, nq, MLA_V, tk), lambda h, i: (h, 0, 0, 0))],
        out_specs=[pl.BlockSpec((tq, hg * MLA_V), lambda h, i: (i, h)), pl.BlockSpec((hg, None, 1, tq), lambda h, i: (h, i, 0, 0))],
        out_shape=[jax.ShapeDtypeStruct((T, MLA_H * MLA_V), BF16), jax.ShapeDtypeStruct((MLA_H, nq, 1, tq), F32)],
        scratch_shapes=[pltpu.VMEM((hg, 1, tq), F32), pltpu.VMEM((hg, 1, tq), F32), pltpu.VMEM((hg, MLA_V, tq), F32)],
        compiler_params=pltpu.CompilerParams(dimension_semantics=("parallel", "arbitrary")),
    )(q, k, vt)


def _attn_delta(do, o, name):
    T = do.shape[0]
    tq = min(T, ATT_TILE)

    def body(do_ref, o_ref, d_ref):
        for hh in range(MLA_H):
            cs = slice(hh * MLA_V, (hh + 1) * MLA_V)
            s = jnp.sum(do_ref[:, cs].astype(F32) * o_ref[:, cs].astype(F32), axis=-1, keepdims=True)
            d_ref[hh] = _as_row(s, tq)

    return pl.pallas_call(
        body, name=name, grid=(T // tq,),
        in_specs=[_row_spec(tq, MLA_H * MLA_V), _row_spec(tq, MLA_H * MLA_V)],
        out_specs=pl.BlockSpec((MLA_H, None, 1, tq), lambda i: (0, i, 0, 0)),
        out_shape=jax.ShapeDtypeStruct((MLA_H, T // tq, 1, tq), F32),
        compiler_params=pltpu.CompilerParams(dimension_semantics=("parallel",)),
    )(do, o)


def _attn_bwd(q, k, kt, v, do, lse, delta, name):
    T = q.shape[0]
    tq = tk = min(T, ATT_TILE)
    nq = nk = T // tq
    tsd = min(tq, ATT_SUB)
    hg = ATT_BWD_HEADS

    def body(q_ref, k_ref, kt_ref, v_ref, do_ref, lse_ref, dl_ref, dqt_ref, dkv_ref, dkr_ref, dq_acc, dk_acc, dv_acc):
        j = pl.program_id(1)

        @pl.when(j == 0)
        def _():
            dq_acc[...] = jnp.zeros_like(dq_acc)

        dk_acc[...] = jnp.zeros_like(dk_acc)
        dv_acc[...] = jnp.zeros_like(dv_acc)

        def step(i, diag):
            off = pl.multiple_of(i * tq, tq)
            ts, nsub = (tsd, tq // tsd) if diag else (tq, 1)
            for u in range(nsub):
                cols = slice(u * ts, (u + 1) * ts)
                nk_u = (u + 1) * ts if diag else tk
                rows = pl.ds(off + u * ts, ts)
                pre = []
                for hh in range(hg):
                    hq, hv = slice(hh * MLA_HP, (hh + 1) * MLA_HP), slice(hh * MLA_V, (hh + 1) * MLA_V)
                    qi, doi = q_ref[rows, hq], do_ref[rows, hv]
                    st = lax.dot_general(k_ref[:nk_u, hq], qi, _NT, preferred_element_type=F32)
                    dpt = lax.dot_general(v_ref[:nk_u, hv], doi, _NT, preferred_element_type=F32)
                    pre.append((qi, doi, st, dpt))
                for hh in range(hg):
                    hq, hv = slice(hh * MLA_HP, (hh + 1) * MLA_HP), slice(hh * MLA_V, (hh + 1) * MLA_V)
                    qi, doi, st, dpt = pre[hh]
                    if diag:
                        qcol = u * ts + lax.broadcasted_iota(jnp.int32, (nk_u, ts), 1)
                        st = jnp.where(lax.broadcasted_iota(jnp.int32, (nk_u, ts), 0) <= qcol, st, NEG_INF)
                    pt = jnp.exp(st - lse_ref[hh, i][:, cols])
                    dv_acc[:nk_u, hv] += jnp.dot(pt.astype(BF16), doi, preferred_element_type=F32)
                    dsb = (pt * (dpt - dl_ref[hh, i][:, cols])).astype(BF16)
                    dk_acc[:nk_u, hq] += jnp.dot(dsb, qi, preferred_element_type=F32)
                    dq_acc[hh, i, :, cols] += jnp.dot(kt_ref[hq, :nk_u], dsb, preferred_element_type=F32)

        def loop_body(i, carry):
            step(i, False)
            return carry

        step(j, True)
        lax.fori_loop(j + 1, nq, loop_body, 0)
        for hh in range(hg):
            a, m, b = hh * MLA_HP, hh * MLA_HP + MLA_NOPE, (hh + 1) * MLA_HP
            dkv_ref[:, a:m] = dk_acc[:, a:m].astype(BF16)
            dkv_ref[:, m:b] = dv_acc[:, hh * MLA_V:(hh + 1) * MLA_V].astype(BF16)
            dkr_ref[:, hh * 128:(hh + 1) * 128] = dk_acc[:, m:b].astype(BF16)

        @pl.when(j == nk - 1)
        def _():
            dqt_ref[...] = dq_acc[...].astype(BF16)

    stat = pl.BlockSpec((hg, nq, 1, tq), lambda h, j: (h, 0, 0, 0))
    return pl.pallas_call(
        body, name=name, grid=(MLA_H // hg, nk),
        in_specs=[pl.BlockSpec((T, hg * MLA_HP), lambda h, j: (0, h)), pl.BlockSpec((tk, hg * MLA_HP), lambda h, j: (j, h)),
                  pl.BlockSpec((hg * MLA_HP, tk), lambda h, j: (h, j)), pl.BlockSpec((tk, hg * MLA_V), lambda h, j: (j, h)),
                  pl.BlockSpec((T, hg * MLA_V), lambda h, j: (0, h)), stat, stat],
        out_specs=[pl.BlockSpec((hg, nq, MLA_HP, tq), lambda h, j: (h, 0, 0, 0)), pl.BlockSpec((tk, hg * MLA_HP), lambda h, j: (j, h)),
                   pl.BlockSpec((tk, hg * 128), lambda h, j: (j, h))],
        out_shape=[jax.ShapeDtypeStruct((MLA_H, nq, MLA_HP, tq), BF16), jax.ShapeDtypeStruct((T, MLA_H * MLA_HP), BF16),
                   jax.ShapeDtypeStruct((T, MLA_H * 128), BF16)],
        scratch_shapes=[pltpu.VMEM((hg, nq, MLA_HP, tq), F32), pltpu.VMEM((tk, hg * MLA_HP), F32), pltpu.VMEM((tk, hg * MLA_V), F32)],
        compiler_params=pltpu.CompilerParams(dimension_semantics=("parallel", "arbitrary")),
    )(q, k, kt, v, do, lse, delta)


ADA_TN = 512


def _silu(v):
    return v * (1.0 / (1.0 + jnp.exp(-v)))


def _ada_fwd(c_all, ada_w, ada_b_loc, name):
    L, D, Nc = ada_w.shape
    B = c_all.shape[0]

    def body(c_ref, w_ref, b_ref, o_ref):
        ca = _silu(c_ref[...]).astype(BF16)
        o_ref[...] = jnp.dot(ca, w_ref[...].astype(BF16), preferred_element_type=F32) + b_ref[...]

    return pl.pallas_call(
        body, name=name, grid=(L, Nc // ADA_TN),
        in_specs=[pl.BlockSpec((B, D), lambda l, n: (0, 0)), pl.BlockSpec((None, D, ADA_TN), lambda l, n: (l, 0, n)),
                  pl.BlockSpec((None, 1, ADA_TN), lambda l, n: (l, 0, n))],
        out_specs=pl.BlockSpec((None, B, ADA_TN), lambda l, n: (l, 0, n)),
        out_shape=jax.ShapeDtypeStruct((L, B, Nc), F32),
        compiler_params=pltpu.CompilerParams(dimension_semantics=("parallel", "parallel")),
    )(c_all, ada_w, ada_b_loc)


def _ada_bwd(c_all_t, dmod_loc, name):
    D, B = c_all_t.shape
    L, _, Nc = dmod_loc.shape

    def body(c_ref, d_ref, o_ref):
        ca = _silu(c_ref[...])
        dv = d_ref[...]
        acc = ca[:, 0:1] * dv[0:1, :]
        for b in range(1, B):
            acc = acc + ca[:, b:b + 1] * dv[b:b + 1, :]
        o_ref[...] = acc

    return pl.pallas_call(
        body, name=name, grid=(L, Nc // ADA_TN),
        in_specs=[pl.BlockSpec((D, B), lambda l, n: (0, 0)), pl.BlockSpec((None, B, ADA_TN), lambda l, n: (l, 0, n))],
        out_specs=pl.BlockSpec((None, D, ADA_TN), lambda l, n: (l, 0, n)),
        out_shape=jax.ShapeDtypeStruct((L, D, Nc), F32),
        compiler_params=pltpu.CompilerParams(dimension_semantics=("parallel", "parallel")),
    )(c_all_t, dmod_loc)


def _sum_lead(parts, name, out_dtype=F32):
    R, C = parts[0].shape[1:]
    n_tot = sum(p.shape[0] for p in parts)
    tr = R
    for cand in (512, 256, 128, 64, 32, 16):
        if R % cand == 0 and cand * C * 4 * n_tot <= (8 << 20):
            tr = cand
            break

    def body(*refs):
        o_ref = refs[-1]
        acc = None
        for r in refs[:-1]:
            for s in range(r.shape[0]):
                acc = r[s].astype(F32) if acc is None else acc + r[s].astype(F32)
        o_ref[...] = acc.astype(o_ref.dtype)

    return pl.pallas_call(
        body, name=name, grid=(R // tr,),
        in_specs=[pl.BlockSpec((p.shape[0], tr, C), lambda i: (0, i, 0)) for p in parts],
        out_specs=pl.BlockSpec((tr, C), lambda i: (i, 0)),
        out_shape=jax.ShapeDtypeStruct((R, C), out_dtype),
        compiler_params=pltpu.CompilerParams(dimension_semantics=("parallel",)),
    )(*parts)


_ADAM_C1 = 1.0 - ADAM_B1 ** ADAM_STEP
_ADAM_C2 = 1.0 - ADAM_B2 ** ADAM_STEP


def _adamw(w, g, m, v, name):
    shape = w.shape
    C = shape[-1]
    R = math.prod(shape[:-1]) if len(shape) > 1 else 1
    w2, g2, m2, v2 = (a.reshape(R, C) for a in (w, g, m, v))
    tr = R
    for cand in (1024, 512, 256, 128, 64, 32, 16, 8):
        if R % cand == 0 and cand * C * 4 <= (1 << 20):
            tr = cand
            break

    def body(w_ref, g_ref, m_ref, v_ref, d_ref, nm_ref, nv_ref):
        gv = g_ref[...]
        mn = ADAM_B1 * m_ref[...] + (1.0 - ADAM_B1) * gv
        vn = ADAM_B2 * v_ref[...] + (1.0 - ADAM_B2) * (gv * gv)
        nm_ref[...] = mn
        nv_ref[...] = vn
        m_hat = mn / _ADAM_C1
        v_hat = vn / _ADAM_C2
        d_ref[...] = -ADAM_LR * (m_hat / (jnp.sqrt(v_hat) + ADAM_EPS) + ADAM_WD * w_ref[...])

    spec = pl.BlockSpec((tr, C), lambda i: (i, 0))
    outs = pl.pallas_call(
        body, name=name, grid=(R // tr,),
        in_specs=[spec] * 4, out_specs=[spec] * 3,
        out_shape=[jax.ShapeDtypeStruct((R, C), F32)] * 3,
        compiler_params=pltpu.CompilerParams(dimension_semantics=("parallel",)),
    )(w2, g2, m2, v2)
    return tuple(o.reshape(shape) for o in outs)


def _row_tile(rows, cols, itemsize, budget):
    for cand in (1024, 512, 256, 128, 64, 32, 16):
        if rows % cand == 0 and cand * cols * itemsize <= budget:
            return cand
    return rows


def _sum_sel(sel, stacked, others, name, out_dtype):
    R, C = stacked.shape[1:]
    n_tot = 1 + sum(o.shape[0] for o in others)
    tr = _row_tile(R, C, 4 * n_tot, 8 << 20)

    def body(sel_ref, s_ref, *refs):
        o_ref = refs[-1]
        acc = s_ref[...].astype(F32)
        for r in refs[:-1]:
            for s in range(r.shape[0]):
                acc = acc + r[s].astype(F32)
        o_ref[...] = acc.astype(o_ref.dtype)

    return pl.pallas_call(
        body, name=name,
        grid_spec=pltpu.PrefetchScalarGridSpec(
            num_scalar_prefetch=1, grid=(R // tr,),
            in_specs=[pl.BlockSpec((None, tr, C), lambda i, s: (s[0], i, 0))] + [pl.BlockSpec((o.shape[0], tr, C), lambda i, s: (0, i, 0)) for o in others],
            out_specs=pl.BlockSpec((tr, C), lambda i, s: (i, 0))),
        out_shape=jax.ShapeDtypeStruct((R, C), out_dtype),
        compiler_params=pltpu.CompilerParams(dimension_semantics=("parallel",)),
    )(sel, stacked, *others)


def _adamw_piece(cidx, w2, m2, v2, mine, got, bufs, row0, name):
    hr, C = mine.shape
    tr = _row_tile(math.gcd(hr, row0) if row0 else hr, C, 4, 1 << 20)
    nt = hr // tr

    def body(c_ref, w_ref, m_ref, v_ref, a_ref, b_ref, _g, _d, _nm, _nv, g_ref, d_ref, nm_ref, nv_ref):
        gv = jnp.where(pl.program_id(0) == c_ref[0], a_ref[...], b_ref[...])
        mn = ADAM_B1 * m_ref[...] + (1.0 - ADAM_B1) * gv
        vn = ADAM_B2 * v_ref[...] + (1.0 - ADAM_B2) * (gv * gv)
        g_ref[...] = gv
        nm_ref[...] = mn
        nv_ref[...] = vn
        d_ref[...] = -ADAM_LR * ((mn / _ADAM_C1) / (jnp.sqrt(vn / _ADAM_C2) + ADAM_EPS) + ADAM_WD * w_ref[...])

    rows = pl.BlockSpec((tr, C), lambda hf, t, c: (row0 // tr + hf * nt + t, 0))
    mine_spec = pl.BlockSpec((tr, C), lambda hf, t, c: (jnp.where(hf == c[0], t, 0), 0))
    got_spec = pl.BlockSpec((tr, C), lambda hf, t, c: (jnp.where(hf == c[0], 0, t), 0))
    return pl.pallas_call(
        body, name=name,
        grid_spec=pltpu.PrefetchScalarGridSpec(num_scalar_prefetch=1, grid=(2, nt), in_specs=[rows] * 3 + [mine_spec, got_spec] + [_ANY_SPEC] * 4,
                                               out_specs=[rows] * 4),
        out_shape=[jax.ShapeDtypeStruct(w2.shape, F32)] * 4,
        input_output_aliases={6 + n: n for n in range(4)},
        compiler_params=pltpu.CompilerParams(dimension_semantics=("parallel", "parallel")),
    )(cidx, w2, m2, v2, mine, got, *bufs)


_VMEM_SPEC = pl.BlockSpec(memory_space=pltpu.VMEM)
_HBM_SPEC = pl.BlockSpec(memory_space=pltpu.HBM)


def _flip(v, bit):
    return (1 - v) if bit else v


def _allgather8(v, name):
    def body(v_ref, out_ref, send_sems, recv_sems, local_sem):
        x, y, c = _idx()
        me = 4 * x + 2 * y + c
        mine = pltpu.make_async_copy(v_ref, out_ref.at[me], local_sem)
        mine.start()
        sends = []
        for k in range(1, N_DEV):
            peer = (_flip(x, k & 4), _flip(y, k & 2), _flip(c, k & 1))
            cp = pltpu.make_async_remote_copy(src_ref=v_ref, dst_ref=out_ref.at[me], send_sem=send_sems.at[k - 1], recv_sem=recv_sems.at[k - 1],
                                              device_id=peer, device_id_type=MESH)
            cp.start()
            sends.append(cp)
        for k in range(1, N_DEV):
            px, py, pc = _flip(x, k & 4), _flip(y, k & 2), _flip(c, k & 1)
            src = 4 * px + 2 * py + pc
            pltpu.make_async_remote_copy(src_ref=v_ref, dst_ref=out_ref.at[src], send_sem=send_sems.at[k - 1], recv_sem=recv_sems.at[k - 1],
                                         device_id=(px, py, pc), device_id_type=MESH).wait_recv()
        for cp in sends:
            cp.wait_send()
        mine.wait()

    return pl.pallas_call(
        body, name=name,
        out_shape=jax.ShapeDtypeStruct((N_DEV, *v.shape), v.dtype),
        in_specs=[_VMEM_SPEC], out_specs=_VMEM_SPEC,
        scratch_shapes=[pltpu.SemaphoreType.DMA((N_DEV - 1,)), pltpu.SemaphoreType.DMA((N_DEV - 1,)), pltpu.SemaphoreType.DMA],
    )(v)


def _mod_exchange(modp, name):
    _, L, Nc = modp.shape

    def body(p_ref, out_ref, send_sems, recv_sems, local_sem):
        x, y, c = _idx()
        me, chip = 4 * x + 2 * y + c, 2 * x + y
        mine = pltpu.make_async_copy(p_ref.at[me], out_ref.at[chip], local_sem)
        mine.start()
        sends = []
        for k in range(1, N_CHIPS):
            px, py = _flip(x, k & 2), _flip(y, k & 1)
            cp = pltpu.make_async_remote_copy(src_ref=p_ref.at[4 * px + 2 * py + c], dst_ref=out_ref.at[chip],
                                              send_sem=send_sems.at[k - 1], recv_sem=recv_sems.at[k - 1], device_id=(px, py, c), device_id_type=MESH)
            cp.start()
            sends.append(cp)
        for k in range(1, N_CHIPS):
            px, py = _flip(x, k & 2), _flip(y, k & 1)
            pltpu.make_async_remote_copy(src_ref=p_ref.at[me], dst_ref=out_ref.at[2 * px + py], send_sem=send_sems.at[k - 1],
                                         recv_sem=recv_sems.at[k - 1], device_id=(px, py, c), device_id_type=MESH).wait_recv()
        for cp in sends:
            cp.wait_send()
        mine.wait()

    return pl.pallas_call(
        body, name=name,
        out_shape=jax.ShapeDtypeStruct((N_CHIPS, L, Nc), modp.dtype),
        in_specs=[_VMEM_SPEC], out_specs=_VMEM_SPEC,
        scratch_shapes=[pltpu.SemaphoreType.DMA((N_CHIPS - 1,)), pltpu.SemaphoreType.DMA((N_CHIPS - 1,)), pltpu.SemaphoreType.DMA],
    )(modp)


_SEM_SPEC = pl.BlockSpec(memory_space=pltpu.SEMAPHORE)
_ANY_SPEC = pl.BlockSpec(memory_space=pl.ANY)
_EFFECT = pltpu.SideEffectType.DATAFLOW_SIDE_EFFECTING


def _hbm(a):
    return pltpu.with_memory_space_constraint(a, pltpu.HBM)


def _xchip_copies(mode, srcs, lands, send_sems, recv_sems, waiting):
    x, y, c = _idx()
    chip = 2 * x + y
    out = []
    for a in range(len(srcs)):
        for k in range(1, _n_peers(mode) + 1):
            if mode == "all8":
                px, py, pc = _flip(x, k & 4), _flip(y, k & 2), _flip(c, k & 1)
                src, dst, mine = srcs[a], lands[a].at[4 * x + 2 * y + c], lands[a].at[4 * px + 2 * py + pc]
            elif mode == "scatter8":
                px, py, pc = _flip(x, k & 4), _flip(y, k & 2), _flip(c, k & 1)
                src, dst, mine = srcs[a].at[pc, 2 * px + py], lands[a].at[k - 1], lands[a].at[k - 1]
            else:
                px, py, pc = _flip(x, k & 2), _flip(y, k & 1), c
                peer = 2 * px + py
                if mode == "gather":
                    src, dst, mine = srcs[a].at[c], lands[a].at[chip, c], lands[a].at[peer, c]
                else:
                    src, dst, mine = srcs[a].at[peer], lands[a].at[k - 1], lands[a].at[k - 1]
            q = a * _n_peers(mode) + k - 1
            out.append(pltpu.make_async_remote_copy(src_ref=src, dst_ref=mine if waiting else dst, send_sem=send_sems[q], recv_sem=recv_sems[q],
                                                    device_id=(px, py, pc), device_id_type=MESH))
    return out


def _n_peers(mode):
    return N_DEV - 1 if mode in ("all8", "scatter8") else N_CHIPS - 1


def _xchip_start(mode, srcs, land_shapes, dep, name):
    n = len(srcs)
    ns = n * _n_peers(mode)

    def body(*refs):
        src_refs, land_refs = refs[:n], refs[n:2 * n]
        outs = refs[2 * n + 1:]
        for cp in _xchip_copies(mode, src_refs, land_refs, outs[:ns], outs[ns:2 * ns], waiting=False):
            cp.start()
        outs[-1][...] = jnp.zeros_like(outs[-1])

    lands = [_hbm(lax.empty(s.shape, s.dtype)) for s in land_shapes]
    outs = pl.pallas_call(
        body, name=name,
        out_shape=(*[pltpu.SemaphoreType.DMA(())] * (2 * ns), *[pltpu.HBM(s.shape, s.dtype) for s in srcs],
                   *[pltpu.HBM(s.shape, s.dtype) for s in land_shapes], jax.ShapeDtypeStruct((8, 128), F32)),
        in_specs=[_HBM_SPEC] * (2 * n) + [_ANY_SPEC],
        out_specs=(*[_SEM_SPEC] * (2 * ns), *[_HBM_SPEC] * (2 * n), _VMEM_SPEC),
        input_output_aliases={i: 2 * ns + i for i in range(2 * n)},
        compiler_params=pltpu.CompilerParams(has_side_effects=_EFFECT),
    )(*[_hbm(s) for s in srcs], *lands, dep)
    return list(outs[:ns]), list(outs[ns:2 * ns]), list(outs[2 * ns:2 * ns + n]), list(outs[2 * ns + n:2 * ns + 2 * n]), outs[-1]


def _xchip_wait(mode, send_sems, recv_sems, srcs, lands, after, name):
    n = len(srcs)
    ns = n * _n_peers(mode)

    def body(*refs):
        src_refs, land_refs = refs[:n], refs[n:2 * n]
        sems = refs[2 * n:2 * n + 2 * ns]
        for cp in _xchip_copies(mode, src_refs, land_refs, sems[:ns], sems[ns:], waiting=True):
            cp.wait_send()
            cp.wait_recv()

    outs = pl.pallas_call(
        body, name=name,
        out_shape=(*[pltpu.HBM(s.shape, s.dtype) for s in srcs], *[pltpu.HBM(s.shape, s.dtype) for s in lands]),
        in_specs=[_HBM_SPEC] * (2 * n) + [_SEM_SPEC] * (2 * ns) + [_ANY_SPEC] * len(after),
        out_specs=tuple([_HBM_SPEC] * (2 * n)),
        input_output_aliases={i: i for i in range(2 * n)},
        compiler_params=pltpu.CompilerParams(has_side_effects=_EFFECT),
    )(*srcs, *lands, *send_sems, *recv_sems, *after)
    return list(outs[:n]), list(outs[n:])


def _sibling_fwd(lands, name):
    n = len(lands)

    def body(*refs):
        outs = refs[n:2 * n]
        send_sems, recv_sems = refs[2 * n:]
        x, y, c = _idx()
        sib = (x, y, 1 - c)
        sends = []
        for a in range(n):
            for k in range(1, N_CHIPS):
                src = 2 * _flip(x, k & 2) + _flip(y, k & 1)
                cp = pltpu.make_async_remote_copy(src_ref=outs[a].at[src, c], dst_ref=outs[a].at[src, c], send_sem=send_sems.at[a, k - 1],
                                                  recv_sem=recv_sems.at[a, k - 1], device_id=sib, device_id_type=MESH)
                cp.start()
                sends.append(cp)
        for a in range(n):
            for k in range(1, N_CHIPS):
                src = 2 * _flip(x, k & 2) + _flip(y, k & 1)
                pltpu.make_async_remote_copy(src_ref=outs[a].at[src, c], dst_ref=outs[a].at[src, 1 - c], send_sem=send_sems.at[a, k - 1],
                                             recv_sem=recv_sems.at[a, k - 1], device_id=sib, device_id_type=MESH).wait_recv()
        for cp in sends:
            cp.wait_send()

    return pl.pallas_call(
        body, name=name,
        out_shape=[jax.ShapeDtypeStruct(s.shape, s.dtype) for s in lands],
        in_specs=[_HBM_SPEC] * n, out_specs=[_HBM_SPEC] * n,
        input_output_aliases={i: i for i in range(n)},
        scratch_shapes=[pltpu.SemaphoreType.DMA((n, N_CHIPS - 1)), pltpu.SemaphoreType.DMA((n, N_CHIPS - 1))],
    )(*lands)


def _sibling_swap(parts, name):
    n = len(parts)

    def body(*refs):
        ins, outs = refs[:n], refs[n:2 * n]
        send_sems, recv_sems = refs[2 * n:]
        x, y, c = _idx()
        cps = []
        for a in range(n):
            cp = pltpu.make_async_remote_copy(src_ref=ins[a].at[1 - c], dst_ref=outs[a], send_sem=send_sems.at[a], recv_sem=recv_sems.at[a],
                                              device_id=(x, y, 1 - c), device_id_type=MESH)
            cp.start()
            cps.append(cp)
        for cp in cps:
            cp.wait()

    return pl.pallas_call(
        body, name=name,
        out_shape=[jax.ShapeDtypeStruct(p.shape[1:], p.dtype) for p in parts],
        in_specs=[_HBM_SPEC] * n, out_specs=[_HBM_SPEC] * n,
        scratch_shapes=[pltpu.SemaphoreType.DMA((n,)), pltpu.SemaphoreType.DMA((n,))],
    )(*parts)


def _sibling_send(halves, name):
    n = len(halves)

    def body(*refs):
        ins, outs = refs[:n], refs[n:2 * n]
        send_sems, recv_sems = refs[2 * n:]
        x, y, c = _idx()
        cps = []
        for a in range(n):
            cp = pltpu.make_async_remote_copy(src_ref=ins[a], dst_ref=outs[a], send_sem=send_sems.at[a], recv_sem=recv_sems.at[a],
                                              device_id=(x, y, 1 - c), device_id_type=MESH)
            cp.start()
            cps.append(cp)
        for cp in cps:
            cp.wait()

    return pl.pallas_call(
        body, name=name,
        out_shape=[jax.ShapeDtypeStruct(h.shape, h.dtype) for h in halves],
        in_specs=[_HBM_SPEC] * n, out_specs=[_HBM_SPEC] * n,
        scratch_shapes=[pltpu.SemaphoreType.DMA((n,)), pltpu.SemaphoreType.DMA((n,))],
    )(*halves)


def _col_full(g):
    k, n = g.shape[1], g.shape[2]
    return g.transpose(1, 0, 2).reshape(k, N_CHIPS * n)


def _col_blocks(w):
    k, n = w.shape
    return w.reshape(k, N_CHIPS, n // N_CHIPS).transpose(1, 0, 2)


def _row_blocks(w):
    k, n = w.shape
    return w.reshape(N_CHIPS, k // N_CHIPS, n)


_UQ_HEAD = MLA_NOPE + MLA_ROPE

_LAT = MLA_QL + MLA_KVL + MLA_ROPE
_POOL_R = len(POOL_WINDOWS) * (POOL_GD // N_CHIPS)

_PIECE_KINDS = {
    "mlp_w1": (D_MODEL, D_MODEL, lambda g: g, _col_blocks),
    "mlp_w2": (D_MODEL, D_MODEL, lambda g: g.reshape(4 * D_MODEL, D_MODEL), _row_blocks),
    "pool_w": (_POOL_R, POOL_GD,
               lambda g: g.reshape(N_CHIPS, len(POOL_WINDOWS), POOL_GD // N_CHIPS, POOL_GD).transpose(1, 0, 2, 3).reshape(len(POOL_WINDOWS), POOL_GD, POOL_GD),
               lambda w: w.reshape(len(POOL_WINDOWS), N_CHIPS, POOL_GD // N_CHIPS, POOL_GD).transpose(1, 0, 2, 3).reshape(N_CHIPS, _POOL_R, POOL_GD)),
    "sgu_w_in": (D_MODEL, 2 * SGU_W // N_CHIPS, _col_full, _col_blocks),
    "sgu_w_out": (SGU_W // N_CHIPS, D_MODEL, lambda g: g.reshape(SGU_W, D_MODEL), _row_blocks),
    "mla_w_dq_dkv": (D_MODEL // N_CHIPS, _LAT, lambda g: jnp.pad(g.reshape(D_MODEL, _LAT), ((0, 0), (0, MLA_LATP - _LAT))),
                     lambda w: _row_blocks(w[:, :_LAT])),
    "mla_w_uq": (MLA_QL, MLA_H * _UQ_HEAD // N_CHIPS,
                 lambda g: jnp.pad(_col_full(g).reshape(MLA_QL, MLA_H, _UQ_HEAD), ((0, 0), (0, 0), (0, MLA_HP - _UQ_HEAD))).reshape(MLA_QL, MLA_H * MLA_HP),
                 lambda w: _col_blocks(w.reshape(MLA_QL, MLA_H, MLA_HP)[:, :, :_UQ_HEAD].reshape(MLA_QL, MLA_H * _UQ_HEAD))),
    "mla_w_ukv": (MLA_KVL, MLA_H * (MLA_NOPE + MLA_V) // N_CHIPS, _col_full, _col_blocks),
    "mla_w_o": (MLA_H * MLA_V // N_CHIPS, D_MODEL, lambda g: g.reshape(MLA_H * MLA_V, D_MODEL), _row_blocks),
}
_MIXER_KINDS = (("pool_w",), ("sgu_w_in", "sgu_w_out"), ("mla_w_dq_dkv", "mla_w_uq", "mla_w_ukv", "mla_w_o"))


def _layer_pieces(i):
    return [(k, i // N_MIXERS) for k in _MIXER_KINDS[i % N_MIXERS]] + [("mlp_w1", i), ("mlp_w2", i)]


def _rope_tables(positions):
    inv_freq = ROPE_THETA ** (-jnp.arange(0, MLA_ROPE, 2, dtype=F32) / MLA_ROPE)
    ang = positions.astype(F32)[:, None] * inv_freq
    cos, sin = jnp.cos(ang), jnp.sin(ang)
    z32, z64 = jnp.zeros_like(cos), jnp.zeros((positions.shape[0], 64), F32)
    return (jnp.concatenate([cos, cos, z64], axis=1), jnp.concatenate([-sin, z32, z64], axis=1), jnp.concatenate([z32, sin, z64], axis=1))


def _local_step(x, positions, target, mod, S, weights_of, grads_of):
    D = D_MODEL
    cc, sa, sb = _rope_tables(positions)
    mods = [[mod[i:i + 1, n * D:(n + 1) * D] for n in range(6)] for i in range(DEPTH)]
    h_dtype = lambda i: F32 if i % N_MIXERS == 0 else BF16
    saved = []
    h = _norm_mod_fwd(x, S["norm_mix_g"][0:1], mods[0][1], mods[0][0], h_dtype(0), "l0_norm1")
    for i in range(DEPTH):
        sh1, sc1, g1, sh2, sc2, g2 = mods[i]
        kind, j = i % N_MIXERS, i // N_MIXERS
        gmlp = S["norm_mlp_g"][i:i + 1]
        W = weights_of(i, "mix", x)
        st = {"x": x}
        norm2 = ((gmlp, "n"), (sc2, "n"), (sh2, "n"))
        if kind == 0:
            x2, pooled, ypre, h2 = _pool_fwd(h, W["pool_w"], S["pool_scale"][j:j + 1], x, g1, gmlp, sc2, sh2, f"l{i}_pool")
            st.update(pooled=pooled, y=ypre)
        elif kind == 1:
            zz = _mm(h, W["sgu_w_in"], out_dtypes=(F32,), name=f"l{i}_sgu_in")
            bs_t = S["sgu_b_s"].T
            gated = _sgu_gate_fwd(zz, S["sgu_ln_g"], S["sgu_ln_b"], S["sgu_w_s"], bs_t, f"l{i}_sgu_gate")
            x2, y, h2 = _mm(gated, W["sgu_w_out"], epi=_epi_residual_norm, extras=((x, "mn"), (g1, "n"), *norm2), out_dtypes=(F32, BF16, BF16),
                            tn=D, name=f"l{i}_sgu_out")
            st.update(h=h, zz=zz, gated=gated, y=y, bs_t=bs_t)
        else:
            lat = _mm(h, W["mla_w_dq_dkv"], out_dtypes=(F32,), name=f"l{i}_mla_lat")
            cqn, ckvn, krot = _mla_lat_fwd(lat, S["mla_q_norm_g"], S["mla_kv_norm_g"], cc, sa, sb, f"l{i}_mla_latn")
            q = _mm(cqn, W["mla_w_uq"], epi=_epi_q_rope, extras=((cc, "m"), (sa, "m"), (sb, "m")), name=f"l{i}_mla_uq")
            k, kt, v, vt = _mla_ukv(ckvn, W["mla_w_ukv"], krot, f"l{i}_mla_ukv")
            o, lse = _attn_fwd(q, k, vt, f"l{i}_attn")
            x2, y, h2 = _mm(o, W["mla_w_o"], epi=_epi_residual_norm, extras=((x, "mn"), (g1, "n"), *norm2), out_dtypes=(F32, BF16, BF16),
                            tn=D, name=f"l{i}_mla_o")
            st.update(h=h, lat=lat, cqn=cqn, ckvn=ckvn, q=q, k=k, kt=kt, v=v, o=o, lse=lse, y=y)
        W = {**W, **weights_of(i, "mlp", x2)}
        z, r2 = _mm(h2, W["mlp_w1"], epi=_epi_sq_relu, out_dtypes=(BF16, BF16), epi_cols=MM_EPI_COLS, name=f"l{i}_mlp1")
        if i + 1 < DEPTH:
            norm1 = ((S["norm_mix_g"][i + 1:i + 2], "n"), (mods[i + 1][1], "n"), (mods[i + 1][0], "n"))
            x3, o2, h = _mm(z, W["mlp_w2"], epi=_epi_residual_norm, extras=((x2, "mn"), (g2, "n"), *norm1), out_dtypes=(F32, BF16, h_dtype(i + 1)),
                            tn=D, name=f"l{i}_mlp2")
        else:
            x3, o2 = _mm(z, W["mlp_w2"], epi=_epi_residual, extras=((x2, "mn"), (g2, "n")), out_dtypes=(F32, BF16), name=f"l{i}_mlp2")
        st.update(x2=x2, h2=h2, z=z, r2=r2, o2=o2, W=W)
        saved.append(st)
        x = x3

    loss, dx, dfinal_g = _loss_head(x, target, S["final_g"], "loss_head")

    gS = {"final_g": dfinal_g, "norm_mix_g": [None] * DEPTH, "norm_mlp_g": [None] * DEPTH, "pool_scale": [None] * 2}
    dmod = [None] * DEPTH
    do2, dg2 = _resid_bwd(dx, saved[-1]["o2"], mods[-1][5], f"l{DEPTH - 1}_b_res2")
    started = None
    for i in reversed(range(DEPTH)):
        st = saved[i]
        W, gW = st["W"], {}
        sh1, sc1, g1, sh2, sc2, g2 = mods[i]
        kind, j = i % N_MIXERS, i // N_MIXERS
        gmix, gmlp = S["norm_mix_g"][i:i + 1], S["norm_mlp_g"][i:i + 1]
        da = _mm(do2, W["mlp_w2"], tb=True, epi=lambda acc, rt: (acc * rt.astype(F32),), extras=((st["r2"], "mn"),), after=started, epi_cols=MM_EPI_COLS,
                 name=f"l{i}_b_dz")
        gW["mlp_w2"] = _mm(st["z"], do2, ta=True, chip_blocks="row", name=f"l{i}_b_dw2")
        dh2 = _mm(da, W["mlp_w1"], tb=True, name=f"l{i}_b_dh2")
        gW["mlp_w1"] = _mm(st["h2"], da, ta=True, chip_blocks="col", name=f"l{i}_b_dw1")
        dx2, dgmlp, dsc2, dsh2, dy, q1 = _norm_mod_bwd(st["x2"], dh2, dx, gmlp, sc2, f"l{i}_b_norm2", res=(st["y"], g1))
        gS["norm_mlp_g"][i] = dgmlp
        if kind == 0:
            dh, dpw, dpsc, dg1 = _pool_bwd(dy, st["pooled"], W["pool_w"], S["pool_scale"][j:j + 1], g1, q1, f"l{i}_b_pool")
            gW["pool_w"] = dpw.astype(BF16)
            gS["pool_scale"][j] = dpsc
        elif kind == 1:
            dg1 = q1
            dgated = _mm(dy, W["sgu_w_out"], tb=True, name=f"l{i}_b_dgated")
            gW["sgu_w_out"] = _mm(st["gated"], dy, ta=True, name=f"l{i}_b_dwout")
            dzz, dws, dbs, dlg, dlb = _sgu_gate_bwd(st["zz"], dgated, S["sgu_ln_g"], S["sgu_ln_b"], S["sgu_w_s"], st["bs_t"], f"l{i}_b_sgu_gate")
            gS.update(sgu_w_s=dws, sgu_b_s=dbs[:, :, 0], sgu_ln_g=dlg, sgu_ln_b=dlb)
            dh = _mm(dzz, W["sgu_w_in"], tb=True, name=f"l{i}_b_dh_sgu")
            gW["sgu_w_in"] = _mm(st["h"], dzz, ta=True, name=f"l{i}_b_dwin")
        else:
            dg1 = q1
            do = _mm(dy, W["mla_w_o"], tb=True, name=f"l{i}_b_do")
            gW["mla_w_o"] = _mm(st["o"], dy, ta=True, name=f"l{i}_b_dwo")
            delta = _attn_delta(do, st["o"], f"l{i}_b_delta")
            dqt, dkv, dkr = _attn_bwd(st["q"], st["k"], st["kt"], st["v"], do, st["lse"], delta, f"l{i}_b_attn")
            dqpad, dkrot = _mla_prep_bwd(dqt, dkr, cc, sa, sb, f"l{i}_b_mla_prep")
            dcqn = _mm(dqpad, W["mla_w_uq"], tb=True, out_dtypes=(F32,), name=f"l{i}_b_dcq")
            gW["mla_w_uq"] = _mm(st["cqn"], dqpad, ta=True, name=f"l{i}_b_dwuq")
            dckvn = _mm(dkv, W["mla_w_ukv"], tb=True, out_dtypes=(F32,), name=f"l{i}_b_dckv")
            gW["mla_w_ukv"] = _mm(st["ckvn"], dkv, ta=True, name=f"l{i}_b_dwukv")
            dlat, dqg, dkvg = _mla_lat_bwd(st["lat"], dcqn, dckvn, dkrot, S["mla_q_norm_g"], S["mla_kv_norm_g"], cc, sa, sb, f"l{i}_b_mla_latn")
            gS.update(mla_q_norm_g=dqg, mla_kv_norm_g=dkvg)
            dh = _mm(dlat, W["mla_w_dq_dkv"], tb=True, name=f"l{i}_b_dh_mla")
            gW["mla_w_dq_dkv"] = _mm(st["h"], dlat, ta=True, name=f"l{i}_b_dwdq")
        if i > 0:
            dx, dgmix, dsc1, dsh1, do2_prev, dg2_prev = _norm_mod_bwd(st["x"], dh, dx2, gmix, sc1, f"l{i}_b_norm1", res=(saved[i - 1]["o2"], mods[i - 1][5]))
        else:
            dx, dgmix, dsc1, dsh1 = _norm_mod_bwd(st["x"], dh, dx2, gmix, sc1, f"l{i}_b_norm1")
        gS["norm_mix_g"][i] = dgmix
        dmod[i] = jnp.concatenate([dsh1, dsc1, dg1, dsh2, dsc2, dg2], axis=1)
        started = grads_of(i, gW, dx)
        if i > 0:
            do2, dg2 = do2_prev, dg2_prev

    for n in ("norm_mix_g", "norm_mlp_g", "pool_scale"):
        gS[n] = jnp.concatenate(gS[n], axis=0)
    return loss, dx, gS, jnp.concatenate(dmod, axis=0)


_SMALL = {
    "norm_mix_g": (DEPTH, D_MODEL), "norm_mlp_g": (DEPTH, D_MODEL), "sgu_ln_g": (1, SGU_W), "sgu_ln_b": (1, SGU_W),
    "sgu_w_s": (SGU_H, SGU_CHUNK, SGU_CHUNK), "sgu_b_s": (SGU_H, SGU_CHUNK), "mla_kv_norm_g": (1, MLA_KVL), "final_g": (1, D_MODEL),
    "pool_scale": (2, D_MODEL), "mla_q_norm_g": (1, MLA_QL), "loss": (1, 128), "dmod": (DEPTH, 6 * D_MODEL),
}
_PACK_W = 1024


def _pack(vals):
    flat = jnp.concatenate([v.reshape(-1) for v in vals])
    rows = -(-flat.shape[0] // (8 * _PACK_W)) * 8
    return jnp.pad(flat, (0, rows * _PACK_W - flat.shape[0])).reshape(rows, _PACK_W)


def _unpack(buf, shapes):
    flat, out, off = buf.reshape(-1), [], 0
    for s in shapes:
        n = math.prod(s)
        out.append(flat[off:off + n].reshape(s))
        off += n
    return out


def kernel(x, c, positions, ada_w, ada_b, norm_mix_g, norm_mlp_g, pool_w, pool_scale, sgu_w_in, sgu_ln_g, sgu_ln_b, sgu_w_s, sgu_b_s, sgu_w_out, mla_w_dq_dkv, mla_q_norm_g, mla_kv_norm_g, mla_w_uq, mla_w_ukv, mla_w_o, mlp_w1, mlp_w2, final_g, loss_target, m_ada_w, m_ada_b, m_norm_mix_g, m_norm_mlp_g, m_pool_w, m_pool_scale, m_sgu_w_in, m_sgu_ln_g, m_sgu_ln_b, m_sgu_w_s, m_sgu_b_s, m_sgu_w_out, m_mla_w_dq_dkv, m_mla_q_norm_g, m_mla_kv_norm_g, m_mla_w_uq, m_mla_w_ukv, m_mla_w_o, m_mlp_w1, m_mlp_w2, m_final_g, v_ada_w, v_ada_b, v_norm_mix_g, v_norm_mlp_g, v_pool_w, v_pool_scale, v_sgu_w_in, v_sgu_ln_g, v_sgu_ln_b, v_sgu_w_s, v_sgu_b_s, v_sgu_w_out, v_mla_w_dq_dkv, v_mla_q_norm_g, v_mla_kv_norm_g, v_mla_w_uq, v_mla_w_ukv, v_mla_w_o, v_mlp_w1, v_mlp_w2, v_final_g):
    P = dict(ada_w=ada_w, ada_b=ada_b, norm_mix_g=norm_mix_g, norm_mlp_g=norm_mlp_g, pool_w=pool_w, pool_scale=pool_scale, sgu_w_in=sgu_w_in,
             sgu_ln_g=sgu_ln_g, sgu_ln_b=sgu_ln_b, sgu_w_s=sgu_w_s, sgu_b_s=sgu_b_s, sgu_w_out=sgu_w_out, mla_w_dq_dkv=mla_w_dq_dkv,
             mla_q_norm_g=mla_q_norm_g, mla_kv_norm_g=mla_kv_norm_g, mla_w_uq=mla_w_uq, mla_w_ukv=mla_w_ukv, mla_w_o=mla_w_o, mlp_w1=mlp_w1,
             mlp_w2=mlp_w2, final_g=final_g)
    M = dict(ada_w=m_ada_w, ada_b=m_ada_b, norm_mix_g=m_norm_mix_g, norm_mlp_g=m_norm_mlp_g, pool_w=m_pool_w, pool_scale=m_pool_scale,
             sgu_w_in=m_sgu_w_in, sgu_ln_g=m_sgu_ln_g, sgu_ln_b=m_sgu_ln_b, sgu_w_s=m_sgu_w_s, sgu_b_s=m_sgu_b_s, sgu_w_out=m_sgu_w_out,
             mla_w_dq_dkv=m_mla_w_dq_dkv, mla_q_norm_g=m_mla_q_norm_g, mla_kv_norm_g=m_mla_kv_norm_g, mla_w_uq=m_mla_w_uq, mla_w_ukv=m_mla_w_ukv,
             mla_w_o=m_mla_w_o, mlp_w1=m_mlp_w1, mlp_w2=m_mlp_w2, final_g=m_final_g)
    V = dict(ada_w=v_ada_w, ada_b=v_ada_b, norm_mix_g=v_norm_mix_g, norm_mlp_g=v_norm_mlp_g, pool_w=v_pool_w, pool_scale=v_pool_scale,
             sgu_w_in=v_sgu_w_in, sgu_ln_g=v_sgu_ln_g, sgu_ln_b=v_sgu_ln_b, sgu_w_s=v_sgu_w_s, sgu_b_s=v_sgu_b_s, sgu_w_out=v_sgu_w_out,
             mla_w_dq_dkv=v_mla_w_dq_dkv, mla_q_norm_g=v_mla_q_norm_g, mla_kv_norm_g=v_mla_kv_norm_g, mla_w_uq=v_mla_w_uq, mla_w_ukv=v_mla_w_ukv,
             mla_w_o=v_mla_w_o, mlp_w1=v_mlp_w1, mlp_w2=v_mlp_w2, final_g=v_final_g)
    order = list(P)
    xi, yi, ci = _idx()
    chip = 2 * xi + yi
    D = D_MODEL
    n_ada = ada_w.shape[2]

    pre = _allgather8(_pack([c, pool_scale, mla_q_norm_g]), "ag_small")
    flat = pre.reshape(N_DEV, -1)
    c_all = flat[:, :D]
    ps_all = flat[0::2, D:D + 2 * (D // N_CHIPS)].reshape(N_CHIPS, 2, D // N_CHIPS).transpose(1, 0, 2).reshape(2, D)
    q0 = D + 2 * (D // N_CHIPS)
    qg_all = flat[0::2, q0:q0 + MLA_QL // N_CHIPS].reshape(1, MLA_QL)

    ada_b_loc = lax.dynamic_slice_in_dim(ada_b, chip * n_ada, n_ada, axis=1)[:, None, :]
    modp = _ada_fwd(c_all, ada_w, ada_b_loc, "ada_fwd")
    mod = _mod_exchange(modp.transpose(1, 0, 2), "mod_exchange").transpose(1, 0, 2).reshape(DEPTH, 6 * D)

    S = dict(norm_mix_g=norm_mix_g, norm_mlp_g=norm_mlp_g, pool_scale=ps_all, sgu_ln_g=sgu_ln_g, sgu_ln_b=sgu_ln_b, sgu_w_s=sgu_w_s[0],
             sgu_b_s=sgu_b_s[0], mla_q_norm_g=qg_all, mla_kv_norm_g=mla_kv_norm_g, final_g=final_g[None, :])
    cidx, ownidx = jnp.reshape(ci, (1,)).astype(jnp.int32), jnp.reshape(N_CHIPS * ci + chip, (1,)).astype(jnp.int32)
    view2d = lambda a: a.reshape(-1, a.shape[-1])

    def piece_rows(kind, blk):
        r = _PIECE_KINDS[kind][0]
        return blk * r, r

    groups = [_layer_pieces(0)[:-2], _layer_pieces(0)[-2:], _layer_pieces(1)[:-2], _layer_pieces(1)[-2:], _layer_pieces(2), _layer_pieces(3)]
    start_after = {1: (2, 3), 2: (4,), 4: (5,)}
    gathers = {}

    def gather_start(g, dep):
        srcs, shapes = [], []
        for kind, blk in groups[g]:
            r0, r = piece_rows(kind, blk)
            cdim = _PIECE_KINDS[kind][1]
            srcs.append(view2d(P[kind])[r0:r0 + r].astype(BF16).reshape(2, r // 2, cdim))
            shapes.append(jax.ShapeDtypeStruct((N_CHIPS, 2, r // 2, cdim), BF16))
        gathers[g] = _xchip_start("gather", srcs, shapes, dep, f"ag_start_g{g}")

    def gather_finish(g, after):
        ssem, rsem, srcs, lands, _ = gathers.pop(g)
        deps = [after]
        for nxt in start_after.get(g, ()):
            gather_start(nxt, deps[-1])
            deps.append(gathers[nxt][-1])
        srcs, lands = _xchip_wait("gather", ssem, rsem, srcs, lands, deps, f"ag_wait_g{g}")
        lands = _sibling_fwd(lands, f"ag_sibling_g{g}")
        W = {}
        for (kind, _), s, land in zip(groups[g], srcs, lands, strict=True):
            r, cdim, to_full, _ = _PIECE_KINDS[kind]
            W[kind] = to_full(lax.dynamic_update_index_in_dim(land, s, chip, 0).reshape(N_CHIPS, r, cdim))
        return W

    def weights_of(i, part, x_i):
        if i < 2:
            return gather_finish(2 * i + (part == "mlp"), x_i)
        return gather_finish(i + 2, x_i) if part == "mix" else {}

    scatters = {}
    bufs = {n: tuple(lax.empty(view2d(P[n]).shape, F32) for _ in range(4)) for n in _PIECE_KINDS}

    def scatter_start(i, gW, dep):
        pcs = _layer_pieces(i)
        blocked = []
        for kind, _ in pcs:
            r, cdim, _, to_blocks = _PIECE_KINDS[kind]
            g = gW[kind]
            blocked.append(g if g.ndim == 4 else to_blocks(g).reshape(N_CHIPS, 2, r // 2, cdim).transpose(1, 0, 2, 3))
        shapes = [jax.ShapeDtypeStruct((N_DEV - 1, *b.shape[2:]), BF16) for b in blocked]
        scatters[i] = (pcs, *_xchip_start("scatter8", blocked, shapes, dep, f"rs_start_l{i}"))
        return scatters[i][-1]

    def scatter_finish(i, after):
        pcs, ssem, rsem, blocked, lands, _ = scatters.pop(i)
        blocked, lands = _xchip_wait("scatter8", ssem, rsem, blocked, lands, after, f"rs_wait_l{i}")
        halves = [_sum_sel(ownidx, b.reshape(2 * N_CHIPS, *b.shape[2:]), [l], f"rs_sum_l{i}_{kind}", F32)
                  for (kind, _), b, l in zip(pcs, blocked, lands, strict=True)]
        got = _sibling_send(halves, f"rs_merge_l{i}")
        for (kind, blk), mine, other in zip(pcs, halves, got, strict=True):
            r0, _ = piece_rows(kind, blk)
            bufs[kind] = tuple(_adamw_piece(cidx, view2d(P[kind]), view2d(M[kind]), view2d(V[kind]), mine, other, bufs[kind], r0,
                                            f"adamw_l{i}_{kind}"))
        return lands[0]

    first_layer = {}

    def grads_of(i, gW, dx_i):
        if i == 0:
            first_layer.update(gW)
            return None
        dep = scatter_finish(i + 1, [dx_i]) if i + 1 in scatters else dx_i
        return scatter_start(i, gW, dep)

    gather_start(0, mod)
    gather_start(1, gathers[0][-1])
    mod = mod + gathers[1][-1][0, 0]
    loss_l, dx, gS, dmod = _local_step(x[0], positions[0], loss_target[0], mod, S, weights_of, grads_of)

    gS["dmod"] = dmod
    gS["loss"] = loss_l
    packed = _pack([gS[n] for n in _SMALL])
    sg = _xchip_start("all8", [packed], [jax.ShapeDtypeStruct((N_DEV, *packed.shape), F32)], dx, "sg_start")
    tok0 = scatter_start(0, first_layer, sg[-1])[0, 0]
    scatter_finish(1, [dx, scatters[0][-1]])
    sg_src, sg_land = _xchip_wait("all8", sg[0], sg[1], sg[2], sg[3], [bufs[n][0] for n in ("mlp_w1", "mlp_w2", "sgu_w_in", "sgu_w_out")], "sg_wait")
    small = lax.dynamic_update_index_in_dim(sg_land[0], sg_src[0], 4 * xi + 2 * yi + ci, 0) + tok0
    small_sum = _unpack(_sum_lead([small], "sum_small_grads"), list(_SMALL.values()))
    G = dict(zip(_SMALL, small_sum, strict=True))
    grads = {
        "ada_b": G["dmod"], "norm_mix_g": G["norm_mix_g"], "norm_mlp_g": G["norm_mlp_g"], "sgu_ln_g": G["sgu_ln_g"], "sgu_ln_b": G["sgu_ln_b"],
        "sgu_w_s": G["sgu_w_s"][None], "sgu_b_s": G["sgu_b_s"][None], "mla_kv_norm_g": G["mla_kv_norm_g"], "final_g": G["final_g"][0],
        "pool_scale": lax.dynamic_slice_in_dim(G["pool_scale"], chip * (D // N_CHIPS), D // N_CHIPS, axis=1),
        "mla_q_norm_g": lax.dynamic_slice_in_dim(G["mla_q_norm_g"], chip * (MLA_QL // N_CHIPS), MLA_QL // N_CHIPS, axis=1),
    }
    dmod_all = _unpack(small, [(N_DEV,) + (small.shape[1] * _PACK_W,)])[0]
    off = sum(math.prod(s) for n, s in _SMALL.items() if n != "dmod")
    dmod_all = dmod_all[:, off:off + DEPTH * 6 * D].reshape(N_DEV, DEPTH, 6 * D)
    dmod_loc = lax.dynamic_slice_in_dim(dmod_all, chip * n_ada, n_ada, axis=2).transpose(1, 0, 2)
    grads["ada_w"] = _ada_bwd(c_all.T, dmod_loc, "ada_bwd")

    deltas, new_m, new_v = {}, {}, {}
    for n in order:
        if n not in _PIECE_KINDS:
            deltas[n], new_m[n], new_v[n] = _adamw(P[n], grads[n].reshape(P[n].shape), M[n], V[n], f"adamw_{n}")
    scatter_finish(0, [deltas["ada_w"], deltas["sgu_w_s"]] + [bufs[n][0] for n in ("mlp_w1", "mlp_w2", "sgu_w_in", "mla_w_o")])
    for n in _PIECE_KINDS:
        grads[n], deltas[n], new_m[n], new_v[n] = (b.reshape(P[n].shape) for b in bufs[n])
    return (G["loss"][0, 0], dx[None], *[grads[n].reshape(P[n].shape) for n in order], *[deltas[n] for n in order], *[new_m[n] for n in order],
            *[new_v[n] for n in order])
```

```python
import math

import jax
import jax.numpy as jnp
from jax import lax
from jax.experimental import pallas as pl
from jax.experimental.pallas import tpu as pltpu

F32, BF16 = jnp.float32, jnp.bfloat16
MESH = pl.DeviceIdType.MESH

D_MODEL = 1024
DEPTH = 4
N_MIXERS = 3
POOL_WINDOWS = (2, 4, 8, 16)
POOL_GD = D_MODEL // len(POOL_WINDOWS)
POOL_HALO = 16
SGU_CHUNK = 128
SGU_W = D_MODEL
SGU_HD = 128
SGU_H = SGU_W // SGU_HD
MLA_H = 16
MLA_QL = 256
MLA_KVL = 128
MLA_NOPE = 128
MLA_ROPE = 64
MLA_V = 128
MLA_HP = 256
MLA_LATP = 512
ROPE_THETA = 10000.0
RMS_EPS = 1e-6
LN_EPS = 1e-5
SM_SCALE = (MLA_NOPE + MLA_ROPE) ** -0.5
NEG_INF = -1e30
ADAM_LR, ADAM_B1, ADAM_B2, ADAM_EPS, ADAM_WD, ADAM_STEP = 0.001, 0.9, 0.999, 1e-08, 0.01, 10
N_CHIPS = 4
N_DEV = 8
ROW_TILE = 512
ATT_TILE = 512
ATT_SUB = 256
ATT_FWD_HEADS = 4
ATT_BWD_HEADS = 2
MM_EPI_COLS = 256
MM_TM_WIDE = 2048
MM_VMEM_BUDGET = 40 << 20


def _idx():
    return lax.axis_index("x"), lax.axis_index("y"), lax.axis_index("c")


def _mm(a, b, *, name, ta=False, tb=False, epi=None, extras=(), out_dtypes=(BF16,), tm=1024, tn=1024, tk=1024, chip_blocks=None, after=None,
        epi_cols=None):
    if ta:
        K, M = a.shape
    else:
        M, K = a.shape
    b_chips = b.ndim == 3
    if b_chips:
        assert b.shape[0] == N_CHIPS
        Kb, N = (N_CHIPS * b.shape[2], b.shape[1]) if tb else (b.shape[1], N_CHIPS * b.shape[2])
    elif tb:
        N, Kb = b.shape
    else:
        Kb, N = b.shape
    assert K == Kb, (a.shape, b.shape, ta, tb)
    if b_chips and not tb:
        tn = min(tn, N // N_CHIPS)
    if chip_blocks == "col":
        tm, tn = min(tm, M // 2), min(tn, N // N_CHIPS)
    elif chip_blocks == "row":
        tm = min(tm, M // N_CHIPS // 2)
    tm, tn, tk = min(tm, M), min(tn, N), min(tk, K)

    def vmem_bytes(tm_, tk_):
        per_mn = sum(arr.dtype.itemsize for arr, kind in extras if kind == "mn") + sum(jnp.dtype(dt).itemsize for dt in out_dtypes)
        return 2 * (tm_ * tk_ * a.dtype.itemsize + tk_ * tn * b.dtype.itemsize + tm_ * tn * per_mn)

    if vmem_bytes(tm, K) <= MM_VMEM_BUDGET:
        tk = K
    elif tm >= 512 and vmem_bytes(tm // 2, K) <= MM_VMEM_BUDGET:
        tm, tk = tm // 2, K
    assert M % tm == 0 and N % tn == 0 and K % tk == 0, (M, N, K, tm, tn, tk)
    nk = K // tk
    assert epi_cols is None or (nk == 1 and not ta and not (b_chips and tb) and tn % epi_cols == 0)
    a_spec = pl.BlockSpec((tk, tm), lambda i, j, k: (k, i)) if ta else pl.BlockSpec((tm, tk), lambda i, j, k: (i, k))
    b_spec = pl.BlockSpec((tn, tk), lambda i, j, k: (j, k)) if tb else pl.BlockSpec((tk, tn), lambda i, j, k: (k, j))
    if b_chips and tb:
        assert nk == 1 and not ta
        b_spec = pl.BlockSpec((N_CHIPS, tn, K // N_CHIPS), lambda i, j, k: (0, j, 0))
    elif b_chips:
        per = N // N_CHIPS // tn
        b_spec = pl.BlockSpec((None, tk, tn), lambda i, j, k: (j // per, k, j % per))
    ex_specs = []
    for arr, kind in extras:
        if kind == "mn":
            ex_specs.append(pl.BlockSpec((tm, tn), lambda i, j, k: (i, j)))
        elif kind == "n":
            ex_specs.append(pl.BlockSpec((1, tn), lambda i, j, k: (0, j)))
        else:
            ex_specs.append(pl.BlockSpec((tm, arr.shape[1]), lambda i, j, k: (i, 0)))
    n_ex, n_out = len(extras), len(out_dtypes)
    n_in = 2 + n_ex + (after is not None)
    dims = (((0 if ta else 1,), (1 if tb else 0,)), ((), ()))

    def body(*refs):
        a_ref, b_ref = refs[0], refs[1]
        ex_refs = refs[2:2 + n_ex]
        out_refs = refs[n_in:n_in + n_out]
        if b_chips and tb:
            kc = K // N_CHIPS
            part = None
            for cb in range(N_CHIPS):
                p = lax.dot_general(a_ref[:, cb * kc:(cb + 1) * kc].astype(BF16), b_ref[cb].astype(BF16), dims, preferred_element_type=F32)
                part = p if part is None else part + p
        elif epi_cols is not None:
            av = a_ref[...].astype(BF16)
            chunk = lambda cc: lax.dot_general(av, (b_ref[cc * epi_cols:(cc + 1) * epi_cols, :] if tb else b_ref[:, cc * epi_cols:(cc + 1) * epi_cols])
                                               .astype(BF16), dims, preferred_element_type=F32)
            acc = chunk(0)
            for cc in range(tn // epi_cols):
                nxt = chunk(cc + 1) if cc + 1 < tn // epi_cols else None
                cs = slice(cc * epi_cols, (cc + 1) * epi_cols)
                for r, o in zip(out_refs, epi(acc, *[r[:, cs] for r in ex_refs]), strict=True):
                    r[:, cs] = o.astype(r.dtype)
                acc = nxt
            return
        else:
            part = lax.dot_general(a_ref[...].astype(BF16), b_ref[...].astype(BF16), dims, preferred_element_type=F32)

        def finish(acc):
            outs = epi(acc, *[r[...] for r in ex_refs]) if epi is not None else (acc,)
            for r, o in zip(out_refs, outs, strict=True):
                r[...] = o.astype(r.dtype)

        if nk == 1:
            finish(part)
        else:
            acc_ref = refs[-1]
            k = pl.program_id(2)

            @pl.when(k == 0)
            def _():
                acc_ref[...] = part

            @pl.when(k > 0)
            def _():
                acc_ref[...] += part

            @pl.when(k == nk - 1)
            def _():
                finish(acc_ref[...])

    out_specs = [pl.BlockSpec((tm, tn), lambda i, j, k: (i, j)) for _ in range(n_out)]
    out_shape = [jax.ShapeDtypeStruct((M, N), dt) for dt in out_dtypes]
    if chip_blocks is not None:
        assert n_out == 1
        if chip_blocks == "col":
            rh, cb = M // 2 // tm, N // N_CHIPS // tn
            out_specs = [pl.BlockSpec((None, None, tm, tn), lambda i, j, k: (i // rh, j // cb, i % rh, j % cb))]
            out_shape = [jax.ShapeDtypeStruct((2, N_CHIPS, M // 2, N // N_CHIPS), out_dtypes[0])]
        else:
            rh = M // N_CHIPS // 2 // tm
            out_specs = [pl.BlockSpec((None, None, tm, tn), lambda i, j, k: ((i // rh) % 2, i // (2 * rh), i % rh, j))]
            out_shape = [jax.ShapeDtypeStruct((2, N_CHIPS, M // N_CHIPS // 2, N), out_dtypes[0])]
    outs = pl.pallas_call(
        body,
        name=name,
        grid=(M // tm, N // tn, nk),
        in_specs=[a_spec, b_spec, *ex_specs] + ([pl.BlockSpec(memory_space=pl.ANY)] if after is not None else []),
        out_specs=out_specs,
        out_shape=out_shape,
        scratch_shapes=[pltpu.VMEM((tm, tn), F32)] if nk > 1 else [],
        compiler_params=pltpu.CompilerParams(dimension_semantics=("parallel", "parallel", "arbitrary")),
    )(a, b, *[arr for arr, _ in extras], *([after] if after is not None else []))
    return outs[0] if n_out == 1 else tuple(outs)


def _epi_sq_relu(acc):
    r = jnp.maximum(acc, 0.0)
    return r * r, 2.0 * r


def _epi_residual(acc, x, g):
    return x + g * acc, acc


def _rms_mod(xv, gain, sc, sh):
    r = lax.rsqrt(jnp.mean(xv * xv, axis=-1, keepdims=True) + RMS_EPS)
    return ((xv * r) * gain) * (1.0 + sc) + sh


def _epi_residual_norm(acc, x, g, gain, sc, sh):
    xn = x + g * acc
    return xn, acc, _rms_mod(xn, gain, sc, sh)


def _row_spec(tr, d):
    return pl.BlockSpec((tr, d), lambda i: (i, 0))


def _vec_spec(d):
    return pl.BlockSpec((1, d), lambda i: (0, 0))


def _colsum(v):
    return jnp.sum(v, axis=0, keepdims=True)


def _norm_mod_fwd(x, gain, sc, sh, out_dtype, name):
    T, D = x.shape
    tr = min(T, ROW_TILE)

    def body(x_ref, g_ref, sc_ref, sh_ref, o_ref):
        o_ref[...] = _rms_mod(x_ref[...], g_ref[...], sc_ref[...], sh_ref[...]).astype(o_ref.dtype)

    return pl.pallas_call(
        body, name=name, grid=(T // tr,),
        in_specs=[_row_spec(tr, D), _vec_spec(D), _vec_spec(D), _vec_spec(D)],
        out_specs=_row_spec(tr, D),
        out_shape=jax.ShapeDtypeStruct((T, D), out_dtype),
        compiler_params=pltpu.CompilerParams(dimension_semantics=("parallel",)),
    )(x, gain, sc, sh)


def _norm_mod_bwd(x, dh, dres, gain, sc, name, res=None):
    T, D = x.shape
    tr = min(T, ROW_TILE)

    def body(x_ref, dh_ref, dres_ref, g_ref, sc_ref, *refs):
        dx_ref, dg_ref, dsc_ref, dsh_ref = refs[-6:-2] if res is not None else refs

        @pl.when(pl.program_id(0) == 0)
        def _():
            dg_ref[...] = jnp.zeros_like(dg_ref)
            dsc_ref[...] = jnp.zeros_like(dsc_ref)
            dsh_ref[...] = jnp.zeros_like(dsh_ref)
            if res is not None:
                refs[-1][...] = jnp.zeros_like(refs[-1])

        xv = x_ref[...]
        r = lax.rsqrt(jnp.mean(xv * xv, axis=-1, keepdims=True) + RMS_EPS)
        xn = xv * r
        dhv = dh_ref[...].astype(F32)
        dsh_ref[...] += _colsum(dhv)
        dsc_ref[...] += _colsum(dhv * (xn * g_ref[...]))
        dt = dhv * (1.0 + sc_ref[...])
        dg_ref[...] += _colsum(dt * xn)
        dxn = dt * g_ref[...]
        dxv = dres_ref[...] + r * (dxn - xn * jnp.mean(dxn * xn, axis=-1, keepdims=True))
        dx_ref[...] = dxv
        if res is not None:
            y_ref, gr_ref, dy_ref, q_ref = refs[0], refs[1], refs[-2], refs[-1]
            dy_ref[...] = (gr_ref[...] * dxv).astype(BF16)
            q_ref[...] += _colsum(dxv * y_ref[...].astype(F32))

    extra_in, extra_spec = ([], []) if res is None else (list(res), [_row_spec(tr, D), _vec_spec(D)])
    return pl.pallas_call(
        body, name=name, grid=(T // tr,),
        in_specs=[_row_spec(tr, D), _row_spec(tr, D), _row_spec(tr, D), _vec_spec(D), _vec_spec(D), *extra_spec],
        out_specs=[_row_spec(tr, D), _vec_spec(D), _vec_spec(D), _vec_spec(D)] + ([_row_spec(tr, D), _vec_spec(D)] if res is not None else []),
        out_shape=[jax.ShapeDtypeStruct((T, D), F32)] + [jax.ShapeDtypeStruct((1, D), F32)] * 3
        + ([jax.ShapeDtypeStruct((T, D), BF16), jax.ShapeDtypeStruct((1, D), F32)] if res is not None else []),
        compiler_params=pltpu.CompilerParams(dimension_semantics=("arbitrary",)),
    )(x, dh, dres, gain, sc, *extra_in)


def _resid_bwd(dx, y, g, name):
    T, D = dx.shape
    tr = min(T, ROW_TILE)

    def body(dx_ref, y_ref, g_ref, dy_ref, q_ref):
        @pl.when(pl.program_id(0) == 0)
        def _():
            q_ref[...] = jnp.zeros_like(q_ref)

        dxv = dx_ref[...]
        dy_ref[...] = (g_ref[...] * dxv).astype(BF16)
        q_ref[...] += _colsum(dxv * y_ref[...].astype(F32))

    return pl.pallas_call(
        body, name=name, grid=(T // tr,),
        in_specs=[_row_spec(tr, D), _row_spec(tr, D), _vec_spec(D)],
        out_specs=[_row_spec(tr, D), _vec_spec(D)],
        out_shape=[jax.ShapeDtypeStruct((T, D), BF16), jax.ShapeDtypeStruct((1, D), F32)],
        compiler_params=pltpu.CompilerParams(dimension_semantics=("arbitrary",)),
    )(dx, y, g)


def _loss_head(x, target, gain, name):
    T, D = x.shape
    tr = min(T, ROW_TILE)

    def body(x_ref, t_ref, g_ref, loss_ref, dx_ref, dg_ref):
        @pl.when(pl.program_id(0) == 0)
        def _():
            loss_ref[...] = jnp.zeros_like(loss_ref)
            dg_ref[...] = jnp.zeros_like(dg_ref)

        xv = x_ref[...]
        r = lax.rsqrt(jnp.mean(xv * xv, axis=-1, keepdims=True) + RMS_EPS)
        xn = xv * r
        err = xn * g_ref[...] - t_ref[...]
        row = jnp.mean(err * err, axis=-1, keepdims=True)
        loss_ref[...] += 0.5 * jnp.sum(row, axis=0, keepdims=True)
        dy = err * (1.0 / D)
        dg_ref[...] += _colsum(dy * xn)
        dxn = dy * g_ref[...]
        dx_ref[...] = r * (dxn - xn * jnp.mean(dxn * xn, axis=-1, keepdims=True))

    return pl.pallas_call(
        body, name=name, grid=(T // tr,),
        in_specs=[_row_spec(tr, D), _row_spec(tr, D), _vec_spec(D)],
        out_specs=[_vec_spec(128), _row_spec(tr, D), _vec_spec(D)],
        out_shape=[jax.ShapeDtypeStruct((1, 128), F32), jax.ShapeDtypeStruct((T, D), F32), jax.ShapeDtypeStruct((1, D), F32)],
        compiler_params=pltpu.CompilerParams(dimension_semantics=("arbitrary",)),
    )(x, target, gain)


def _pool_fwd(h, w, scale, x, g1, gmlp, sc2, sh2, name):
    T, D = h.shape
    tr = min(T, ROW_TILE)

    def body(h_ref, w_ref, sc_ref, x_ref, g_ref, gm_ref, sc2_ref, sh2_ref, x2_ref, pooled_ref, ypre_ref, h2_ref, halo_ref):
        i = pl.program_id(0)

        @pl.when(i == 0)
        def _():
            halo_ref[...] = jnp.zeros_like(halo_ref)

        hv = h_ref[...]
        buf = jnp.concatenate([halo_ref[...], hv], axis=0)
        halo_ref[...] = hv[tr - POOL_HALO:, :]
        t = (i * tr + lax.broadcasted_iota(jnp.int32, (tr, 1), 0)).astype(F32)
        for gi, win in enumerate(POOL_WINDOWS):
            cols = slice(gi * POOL_GD, (gi + 1) * POOL_GD)
            val = buf[:, cols]
            sh = 1
            while sh < win:
                val = val + pltpu.roll(val, sh, axis=0)
                sh *= 2
            pooled = val[POOL_HALO:, :] / jnp.minimum(t + 1.0, float(win)) - hv[:, cols]
            pb = pooled.astype(BF16)
            pooled_ref[:, cols] = pb
            yp = jnp.dot(pb, w_ref[gi], preferred_element_type=F32)
            ypre_ref[:, cols] = yp.astype(BF16)
            x2_ref[:, cols] = x_ref[:, cols] + g_ref[:, cols] * (yp * sc_ref[:, cols])
        h2_ref[...] = _rms_mod(x2_ref[...], gm_ref[...], sc2_ref[...], sh2_ref[...]).astype(BF16)

    return pl.pallas_call(
        body, name=name, grid=(T // tr,),
        in_specs=[_row_spec(tr, D), pl.BlockSpec(w.shape, lambda i: (0, 0, 0)), _vec_spec(D), _row_spec(tr, D), _vec_spec(D), _vec_spec(D),
                  _vec_spec(D), _vec_spec(D)],
        out_specs=[_row_spec(tr, D)] * 4,
        out_shape=[jax.ShapeDtypeStruct((T, D), F32), jax.ShapeDtypeStruct((T, D), BF16), jax.ShapeDtypeStruct((T, D), BF16),
                   jax.ShapeDtypeStruct((T, D), BF16)],
        scratch_shapes=[pltpu.VMEM((POOL_HALO, D), F32)],
        compiler_params=pltpu.CompilerParams(dimension_semantics=("arbitrary",)),
    )(h, w, scale, x, g1, gmlp, sc2, sh2)


def _pool_bwd(dy, pooled, w, scale, g1, q, name):
    T, D = dy.shape
    tr = min(T, ROW_TILE)
    nt = T // tr
    ltot = tr + POOL_HALO

    def body(dy_ref, pooled_ref, w_ref, sc_ref, g_ref, q_ref, dh_ref, dw_ref, dsc_ref, dg_ref, halo_ref):
        i = pl.program_id(0)

        @pl.when(i == 0)
        def _():
            halo_ref[...] = jnp.zeros_like(halo_ref)
            dw_ref[...] = jnp.zeros_like(dw_ref)
            dsc_ref[...] = g_ref[...] * q_ref[...]
            dg_ref[...] = sc_ref[...] * q_ref[...]

        t = ((nt - 1 - i) * tr + lax.broadcasted_iota(jnp.int32, (tr, 1), 0)).astype(F32)
        for gi, win in enumerate(POOL_WINDOWS):
            cols = slice(gi * POOL_GD, (gi + 1) * POOL_GD)
            dyb = (dy_ref[:, cols].astype(F32) * sc_ref[:, cols]).astype(BF16)
            dw_ref[gi] += lax.dot_general(pooled_ref[:, cols], dyb, (((0,), (0,)), ((), ())), preferred_element_type=F32)
            dpool = lax.dot_general(dyb, w_ref[gi], (((1,), (1,)), ((), ())), preferred_element_type=F32)
            qv = dpool / jnp.minimum(t + 1.0, float(win))
            val = jnp.concatenate([qv, halo_ref[:, cols]], axis=0)
            halo_ref[:, cols] = qv[:POOL_HALO, :]
            sh = 1
            while sh < win:
                val = val + pltpu.roll(val, ltot - sh, axis=0)
                sh *= 2
            dh_ref[:, cols] = (val[:tr, :] - dpool).astype(BF16)

    rev = pl.BlockSpec((tr, D), lambda i: (nt - 1 - i, 0))
    return pl.pallas_call(
        body, name=name, grid=(nt,),
        in_specs=[rev, rev, pl.BlockSpec(w.shape, lambda i: (0, 0, 0)), _vec_spec(D), _vec_spec(D), _vec_spec(D)],
        out_specs=[rev, pl.BlockSpec(w.shape, lambda i: (0, 0, 0)), _vec_spec(D), _vec_spec(D)],
        out_shape=[jax.ShapeDtypeStruct((T, D), BF16), jax.ShapeDtypeStruct(w.shape, F32),
                   jax.ShapeDtypeStruct((1, D), F32), jax.ShapeDtypeStruct((1, D), F32)],
        scratch_shapes=[pltpu.VMEM((POOL_HALO, D), F32)],
        compiler_params=pltpu.CompilerParams(dimension_semantics=("arbitrary",)),
    )(dy, pooled, w, scale, g1, q)


_INV_SQRT2 = 0.7071067811865476
_INV_SQRT2PI = 0.3989422804014327


def _gelu(v):
    return 0.5 * v * (1.0 + lax.erf(v * _INV_SQRT2))


def _gelu_grad(v):
    return 0.5 * (1.0 + lax.erf(v * _INV_SQRT2)) + v * jnp.exp(-0.5 * v * v) * _INV_SQRT2PI


def _sgu_ln(v, g, b):
    mu = jnp.mean(v, axis=-1, keepdims=True)
    xc = v - mu
    rstd = lax.rsqrt(jnp.mean(xc * xc, axis=-1, keepdims=True) + LN_EPS)
    xh = xc * rstd
    return xh, rstd, xh * g + b


def _tril_mask():
    return lax.broadcasted_iota(jnp.int32, (SGU_CHUNK, SGU_CHUNK), 0) >= lax.broadcasted_iota(jnp.int32, (SGU_CHUNK, SGU_CHUNK), 1)


SGU_TILE = 256


def _sgu_gate_fwd(zz, ln_g, ln_b, ws, bs_t, name):
    T = zz.shape[0]
    ts = min(T, SGU_TILE)

    def body(zz_ref, g_ref, b_ref, ws_ref, bs_ref, out_ref):
        z = _gelu(zz_ref[...])
        u = z[:, :SGU_W]
        _, _, vn = _sgu_ln(z[:, SGU_W:], g_ref[...], b_ref[...])
        vb = vn.astype(BF16)
        tril = _tril_mask()
        for hh in range(SGU_H):
            wm = jnp.where(tril, ws_ref[hh], 0.0).astype(BF16)
            bcol = bs_ref[:, hh:hh + 1]
            cs = slice(hh * SGU_HD, (hh + 1) * SGU_HD)
            for j in range(ts // SGU_CHUNK):
                rs = slice(j * SGU_CHUNK, (j + 1) * SGU_CHUNK)
                mixed = jnp.dot(wm, vb[rs, cs], preferred_element_type=F32) + bcol
                out_ref[rs, cs] = (u[rs, cs] * mixed).astype(BF16)

    return pl.pallas_call(
        body, name=name, grid=(T // ts,),
        in_specs=[_row_spec(ts, 2 * SGU_W), _vec_spec(SGU_W), _vec_spec(SGU_W),
                  pl.BlockSpec(ws.shape, lambda i: (0, 0, 0)), pl.BlockSpec(bs_t.shape, lambda i: (0, 0))],
        out_specs=_row_spec(ts, SGU_W),
        out_shape=jax.ShapeDtypeStruct((T, SGU_W), BF16),
        compiler_params=pltpu.CompilerParams(dimension_semantics=("parallel",)),
    )(zz, ln_g, ln_b, ws, bs_t)


def _sgu_gate_bwd(zz, dgated, ln_g, ln_b, ws, bs_t, name):
    T = zz.shape[0]
    ts = min(T, SGU_TILE)
    nt = T // ts

    def body(zz_ref, dg_ref, g_ref, b_ref, ws_ref, bs_ref, dzz_ref, dws_ref, dbs_ref, dlg_ref, dlb_ref, dlo_ref, dmx_ref):
        i = pl.program_id(0)

        @pl.when(i == 0)
        def _():
            dws_ref[...] = jnp.zeros_like(dws_ref)
            dmx_ref[...] = jnp.zeros_like(dmx_ref)
            dlg_ref[...] = jnp.zeros_like(dlg_ref)
            dlb_ref[...] = jnp.zeros_like(dlb_ref)

        zzv = zz_ref[...]
        z = _gelu(zzv)
        u = z[:, :SGU_W]
        xh, rstd, vn = _sgu_ln(z[:, SGU_W:], g_ref[...], b_ref[...])
        vb = vn.astype(BF16)
        dgv = dg_ref[...].astype(F32)
        tril = _tril_mask()
        for hh in range(SGU_H):
            wm = jnp.where(tril, ws_ref[hh], 0.0).astype(BF16)
            bcol = bs_ref[:, hh:hh + 1]
            cs = slice(hh * SGU_HD, (hh + 1) * SGU_HD)
            for j in range(ts // SGU_CHUNK):
                rs = slice(j * SGU_CHUNK, (j + 1) * SGU_CHUNK)
                mixed = jnp.dot(wm, vb[rs, cs], preferred_element_type=F32) + bcol
                dmixed = dgv[rs, cs] * u[rs, cs]
                dzz_ref[rs, cs] = (dgv[rs, cs] * mixed * _gelu_grad(zzv[rs, cs])).astype(BF16)
                dmb = dmixed.astype(BF16)
                dws_ref[hh] += lax.dot_general(dmb, vb[rs, cs], (((1,), (1,)), ((), ())), preferred_element_type=F32)
                dmx_ref[hh] += dmixed
                dlo_ref[rs, cs] = lax.dot_general(wm, dmb, (((0,), (0,)), ((), ())), preferred_element_type=F32)
        dlo = dlo_ref[...]
        dlg_ref[...] += _colsum(dlo * xh)
        dlb_ref[...] += _colsum(dlo)
        dxh = dlo * g_ref[...]
        dv = rstd * (dxh - jnp.mean(dxh, axis=-1, keepdims=True) - xh * jnp.mean(dxh * xh, axis=-1, keepdims=True))
        dzz_ref[:, SGU_W:] = (dv * _gelu_grad(zzv[:, SGU_W:])).astype(BF16)

        @pl.when(i == nt - 1)
        def _():
            tril_f = tril.astype(F32)
            for hh in range(SGU_H):
                dws_ref[hh] = dws_ref[hh] * tril_f
                dbs_ref[hh] = jnp.broadcast_to(jnp.sum(dmx_ref[hh], axis=-1, keepdims=True), (SGU_CHUNK, SGU_HD))

    full3 = pl.BlockSpec(ws.shape, lambda i: (0, 0, 0))
    return pl.pallas_call(
        body, name=name, grid=(nt,),
        in_specs=[_row_spec(ts, 2 * SGU_W), _row_spec(ts, SGU_W), _vec_spec(SGU_W), _vec_spec(SGU_W), full3,
                  pl.BlockSpec(bs_t.shape, lambda i: (0, 0))],
        out_specs=[_row_spec(ts, 2 * SGU_W), full3, full3, _vec_spec(SGU_W), _vec_spec(SGU_W)],
        out_shape=[jax.ShapeDtypeStruct((T, 2 * SGU_W), BF16), jax.ShapeDtypeStruct(ws.shape, F32), jax.ShapeDtypeStruct(ws.shape, F32),
                   jax.ShapeDtypeStruct((1, SGU_W), F32), jax.ShapeDtypeStruct((1, SGU_W), F32)],
        scratch_shapes=[pltpu.VMEM((ts, SGU_W), F32), pltpu.VMEM(ws.shape, F32)],
        compiler_params=pltpu.CompilerParams(dimension_semantics=("arbitrary",)),
    )(zz, dgated, ln_g, ln_b, ws, bs_t)


def _rope_fwd(blk, cc, sa, sb):
    return blk * cc + pltpu.roll(blk, 96, axis=1) * sa + pltpu.roll(blk, 32, axis=1) * sb


def _rope_bwd(d, cc, sa, sb):
    return d * cc + pltpu.roll(d * sa, 32, axis=1) + pltpu.roll(d * sb, 96, axis=1)


def _rms(v, g):
    r = lax.rsqrt(jnp.mean(v * v, axis=-1, keepdims=True) + RMS_EPS)
    vn = v * r
    return vn, r, vn * g


def _rms_bwd(dy, vn, r, g):
    dvn = dy * g
    return r * (dvn - vn * jnp.mean(dvn * vn, axis=-1, keepdims=True))


_KV0 = MLA_QL
_KR0 = MLA_QL + MLA_KVL


def _mla_lat_fwd(lat, qg, kvg, cc, sa, sb, name):
    T = lat.shape[0]
    tr = min(T, ROW_TILE)

    def body(lat_ref, qg_ref, kvg_ref, cc_ref, sa_ref, sb_ref, cq_ref, ckv_ref, kr_ref):
        lv = lat_ref[...]
        cq_ref[...] = _rms(lv[:, :_KV0], qg_ref[...])[2].astype(BF16)
        ckv_ref[...] = _rms(lv[:, _KV0:_KR0], kvg_ref[...])[2].astype(BF16)
        kr_ref[...] = _rope_fwd(lv[:, _KR0:], cc_ref[...], sa_ref[...], sb_ref[...])

    return pl.pallas_call(
        body, name=name, grid=(T // tr,),
        in_specs=[_row_spec(tr, MLA_LATP), _vec_spec(MLA_QL), _vec_spec(MLA_KVL), _row_spec(tr, 128), _row_spec(tr, 128), _row_spec(tr, 128)],
        out_specs=[_row_spec(tr, MLA_QL), _row_spec(tr, MLA_KVL), _row_spec(tr, 128)],
        out_shape=[jax.ShapeDtypeStruct((T, MLA_QL), BF16), jax.ShapeDtypeStruct((T, MLA_KVL), BF16), jax.ShapeDtypeStruct((T, 128), F32)],
        compiler_params=pltpu.CompilerParams(dimension_semantics=("parallel",)),
    )(lat, qg, kvg, cc, sa, sb)


def _mla_lat_bwd(lat, dcqn, dckvn, dkrot, qg, kvg, cc, sa, sb, name):
    T = lat.shape[0]
    tr = min(T, ROW_TILE)

    def body(lat_ref, dcq_ref, dckv_ref, dkr_ref, qg_ref, kvg_ref, cc_ref, sa_ref, sb_ref, dlat_ref, dqg_ref, dkvg_ref):
        @pl.when(pl.program_id(0) == 0)
        def _():
            dqg_ref[...] = jnp.zeros_like(dqg_ref)
            dkvg_ref[...] = jnp.zeros_like(dkvg_ref)

        lv = lat_ref[...]
        qn, qr, _ = _rms(lv[:, :_KV0], qg_ref[...])
        kn, kr, _ = _rms(lv[:, _KV0:_KR0], kvg_ref[...])
        dcq = dcq_ref[...]
        dckv = dckv_ref[...]
        dqg_ref[...] += _colsum(dcq * qn)
        dkvg_ref[...] += _colsum(dckv * kn)
        dlat_ref[:, :_KV0] = _rms_bwd(dcq, qn, qr, qg_ref[...]).astype(BF16)
        dlat_ref[:, _KV0:_KR0] = _rms_bwd(dckv, kn, kr, kvg_ref[...]).astype(BF16)
        dlat_ref[:, _KR0:] = _rope_bwd(dkr_ref[...], cc_ref[...], sa_ref[...], sb_ref[...]).astype(BF16)

    return pl.pallas_call(
        body, name=name, grid=(T // tr,),
        in_specs=[_row_spec(tr, MLA_LATP), _row_spec(tr, MLA_QL), _row_spec(tr, MLA_KVL), _row_spec(tr, 128),
                  _vec_spec(MLA_QL), _vec_spec(MLA_KVL), _row_spec(tr, 128), _row_spec(tr, 128), _row_spec(tr, 128)],
        out_specs=[_row_spec(tr, MLA_LATP), _vec_spec(MLA_QL), _vec_spec(MLA_KVL)],
        out_shape=[jax.ShapeDtypeStruct((T, MLA_LATP), BF16), jax.ShapeDtypeStruct((1, MLA_QL), F32), jax.ShapeDtypeStruct((1, MLA_KVL), F32)],
        compiler_params=pltpu.CompilerParams(dimension_semantics=("arbitrary",)),
    )(lat, dcqn, dckvn, dkrot, qg, kvg, cc, sa, sb)


def _epi_q_rope(acc, cc, sa, sb):
    out = []
    for hh in range(acc.shape[1] // MLA_HP):
        a, m, b = hh * MLA_HP, hh * MLA_HP + MLA_NOPE, (hh + 1) * MLA_HP
        out += [acc[:, a:m] * SM_SCALE, _rope_fwd(acc[:, m:b], cc, sa, sb) * SM_SCALE]
    return (jnp.concatenate(out, axis=1),)


def _mla_ukv(ckvn, w_ukv, krot, name):
    T = ckvn.shape[0]
    tr = min(T, ATT_TILE)
    hg = ATT_HG
    gw = hg * MLA_HP

    def body(a_ref, w_ref, kr_ref, ko_ref, kt_ref, vo_ref, vt_ref):
        acc = jnp.dot(a_ref[...], w_ref[...], preferred_element_type=F32)
        kr = kr_ref[...]
        krb, krt = kr.astype(BF16), kr.T.astype(BF16)
        for hh in range(hg):
            a, m, b = hh * MLA_HP, hh * MLA_HP + MLA_NOPE, (hh + 1) * MLA_HP
            kn, vh = acc[:, a:m], acc[:, m:b]
            ko_ref[:, a:m] = kn.astype(BF16)
            ko_ref[:, m:b] = krb
            kt_ref[a:m, :] = kn.T.astype(BF16)
            kt_ref[m:b, :] = krt
            vo_ref[:, hh * MLA_V:(hh + 1) * MLA_V] = vh.astype(BF16)
            vt_ref[hh] = vh.T.astype(BF16)

    tk = min(T, ATT_TILE)
    per = tk // tr
    HW = MLA_H * MLA_HP
    return pl.pallas_call(
        body, name=name, grid=(T // tr, MLA_H // hg),
        in_specs=[pl.BlockSpec((tr, MLA_KVL), lambda i, g: (i, 0)), pl.BlockSpec((MLA_KVL, gw), lambda i, g: (0, g)),
                  pl.BlockSpec((tr, 128), lambda i, g: (i, 0))],
        out_specs=[pl.BlockSpec((tr, gw), lambda i, g: (i, g)), pl.BlockSpec((gw, tr), lambda i, g: (g, i)),
                   pl.BlockSpec((tr, hg * MLA_V), lambda i, g: (i, g)),
                   pl.BlockSpec((hg, None, MLA_V, tr), lambda i, g: (g, i // per, 0, i % per))],
        out_shape=[jax.ShapeDtypeStruct((T, HW), BF16), jax.ShapeDtypeStruct((HW, T), BF16), jax.ShapeDtypeStruct((T, MLA_H * MLA_V), BF16),
                   jax.ShapeDtypeStruct((MLA_H, T // tk, MLA_V, tk), BF16)],
        compiler_params=pltpu.CompilerParams(dimension_semantics=("parallel", "parallel")),
    )(ckvn, w_ukv, krot)


ATT_HG = 4


def _mla_prep_bwd(dqt, dkr, cc, sa, sb, name):
    _, nq, _, tq = dqt.shape
    T = nq * tq
    gw = ATT_HG * MLA_HP

    def body(dq_ref, dk_ref, cc_ref, sa_ref, sb_ref, dqp_ref, dkr_ref):
        @pl.when(pl.program_id(1) == 0)
        def _():
            dkr_ref[...] = jnp.zeros_like(dkr_ref)

        cc, sa, sb = cc_ref[...], sa_ref[...], sb_ref[...]
        acc = jnp.zeros((tq, 128), F32)
        for hh in range(ATT_HG):
            a, m, b = hh * MLA_HP, hh * MLA_HP + MLA_NOPE, (hh + 1) * MLA_HP
            dqh = dq_ref[hh].astype(F32).T * SM_SCALE
            dqp_ref[:, a:m] = dqh[:, :MLA_NOPE].astype(BF16)
            dqp_ref[:, m:b] = _rope_bwd(dqh[:, MLA_NOPE:], cc, sa, sb).astype(BF16)
            acc = acc + dk_ref[:, hh * 128:(hh + 1) * 128].astype(F32)
        dkr_ref[...] += acc

    tab = pl.BlockSpec((tq, 128), lambda i, g: (i, 0))
    return pl.pallas_call(
        body, name=name, grid=(nq, MLA_H // ATT_HG),
        in_specs=[pl.BlockSpec((ATT_HG, None, MLA_HP, tq), lambda i, g: (g, i, 0, 0)), pl.BlockSpec((tq, ATT_HG * 128), lambda i, g: (i, g)),
                  tab, tab, tab],
        out_specs=[pl.BlockSpec((tq, gw), lambda i, g: (i, g)), tab],
        out_shape=[jax.ShapeDtypeStruct((T, MLA_H * MLA_HP), BF16), jax.ShapeDtypeStruct((T, 128), F32)],
        compiler_params=pltpu.CompilerParams(dimension_semantics=("parallel", "arbitrary")),
    )(dqt, dkr, cc, sa, sb)


_NT = (((1,), (1,)), ((), ()))


def _as_row(col, n):
    return jnp.broadcast_to(col, (n, 128)).T[0:1, :]


def _attn_fwd(q, k, vt, name):
    T = q.shape[0]
    tq = tk = min(T, ATT_TILE)
    nq = T // tq
    hg = ATT_FWD_HEADS

    def body(q_ref, k_ref, vt_ref, o_ref, lse_ref, m_ref, l_ref, acc_ref):
        i = pl.program_id(1)
        m_ref[...] = jnp.full_like(m_ref, NEG_INF)
        l_ref[...] = jnp.zeros_like(l_ref)
        acc_ref[...] = jnp.zeros_like(acc_ref)

        def step(j, diag):
            off = pl.multiple_of(j * tk, tk)
            sts = [lax.dot_general(k_ref[pl.ds(off, tk), hh * MLA_HP:(hh + 1) * MLA_HP], q_ref[:, hh * MLA_HP:(hh + 1) * MLA_HP], _NT,
                                   preferred_element_type=F32) for hh in range(hg)]
            for hh in range(hg):
                st = sts[hh]
                if diag:
                    st = jnp.where(lax.broadcasted_iota(jnp.int32, (tk, tq), 0) <= lax.broadcasted_iota(jnp.int32, (tk, tq), 1), st, NEG_INF)
                m_prev = m_ref[hh]
                m_new = jnp.maximum(m_prev, jnp.max(st, axis=0, keepdims=True))
                alpha = jnp.exp(m_prev - m_new)
                pt = jnp.exp(st - m_new)
                l_ref[hh] = alpha * l_ref[hh] + jnp.sum(pt, axis=0, keepdims=True)
                acc_ref[hh] = alpha * acc_ref[hh] + jnp.dot(vt_ref[hh, j], pt.astype(BF16), preferred_element_type=F32)
                m_ref[hh] = m_new

        def loop_body(j, carry):
            step(j, False)
            return carry

        lax.fori_loop(0, i, loop_body, 0)
        step(i, True)
        for hh in range(hg):
            o_ref[:, hh * MLA_V:(hh + 1) * MLA_V] = (acc_ref[hh] / l_ref[hh]).T.astype(BF16)
            lse_ref[hh] = m_ref[hh] + jnp.log(l_ref[hh])

    return pl.pallas_call(
        body, name=name, grid=(MLA_H // hg, nq),
        in_specs=[pl.BlockSpec((tq, hg * MLA_HP), lambda h, i: (i, h)), pl.BlockSpec((T, hg * MLA_HP), lambda h, i: (0, h)),
                  pl.BlockSpec((hg, nq, MLA_V, tk), lambda h, i: (h, 0, 0, 0))],
        out_specs=[pl.BlockSpec((tq, hg * MLA_V), lambda h, i: (i, h)), pl.BlockSpec((hg, None, 1, tq), lambda h, i: (h, i, 0, 0))],
        out_shape=[jax.ShapeDtypeStruct((T, MLA_H * MLA_V), BF16), jax.ShapeDtypeStruct((MLA_H, nq, 1, tq), F32)],
        scratch_shapes=[pltpu.VMEM((hg, 1, tq), F32), pltpu.VMEM((hg, 1, tq), F32), pltpu.VMEM((hg, MLA_V, tq), F32)],
        compiler_params=pltpu.CompilerParams(dimension_semantics=("parallel", "arbitrary")),
    )(q, k, vt)


def _attn_delta(do, o, name):
    T = do.shape[0]
    tq = min(T, ATT_TILE)

    def body(do_ref, o_ref, d_ref):
        for hh in range(MLA_H):
            cs = slice(hh * MLA_V, (hh + 1) * MLA_V)
            s = jnp.sum(do_ref[:, cs].astype(F32) * o_ref[:, cs].astype(F32), axis=-1, keepdims=True)
            d_ref[hh] = _as_row(s, tq)

    return pl.pallas_call(
        body, name=name, grid=(T // tq,),
        in_specs=[_row_spec(tq, MLA_H * MLA_V), _row_spec(tq, MLA_H * MLA_V)],
        out_specs=pl.BlockSpec((MLA_H, None, 1, tq), lambda i: (0, i, 0, 0)),
        out_shape=jax.ShapeDtypeStruct((MLA_H, T // tq, 1, tq), F32),
        compiler_params=pltpu.CompilerParams(dimension_semantics=("parallel",)),
    )(do, o)


def _attn_bwd(q, k, kt, v, do, lse, delta, name):
    T = q.shape[0]
    tq = tk = min(T, ATT_TILE)
    nq = nk = T // tq
    tsd = min(tq, ATT_SUB)
    hg = ATT_BWD_HEADS

    def body(q_ref, k_ref, kt_ref, v_ref, do_ref, lse_ref, dl_ref, dqt_ref, dkv_ref, dkr_ref, dq_acc, dk_acc, dv_acc):
        j = pl.program_id(1)

        @pl.when(j == 0)
        def _():
            dq_acc[...] = jnp.zeros_like(dq_acc)

        dk_acc[...] = jnp.zeros_like(dk_acc)
        dv_acc[...] = jnp.zeros_like(dv_acc)

        def step(i, diag):
            off = pl.multiple_of(i * tq, tq)
            ts, nsub = (tsd, tq // tsd) if diag else (tq, 1)
            for u in range(nsub):
                cols = slice(u * ts, (u + 1) * ts)
                nk_u = (u + 1) * ts if diag else tk
                rows = pl.ds(off + u * ts, ts)
                pre = []
                for hh in range(hg):
                    hq, hv = slice(hh * MLA_HP, (hh + 1) * MLA_HP), slice(hh * MLA_V, (hh + 1) * MLA_V)
                    qi, doi = q_ref[rows, hq], do_ref[rows, hv]
                    st = lax.dot_general(k_ref[:nk_u, hq], qi, _NT, preferred_element_type=F32)
                    dpt = lax.dot_general(v_ref[:nk_u, hv], doi, _NT, preferred_element_type=F32)
                    pre.append((qi, doi, st, dpt))
                for hh in range(hg):
                    hq, hv = slice(hh * MLA_HP, (hh + 1) * MLA_HP), slice(hh * MLA_V, (hh + 1) * MLA_V)
                    qi, doi, st, dpt = pre[hh]
                    if diag:
                        qcol = u * ts + lax.broadcasted_iota(jnp.int32, (nk_u, ts), 1)
                        st = jnp.where(lax.broadcasted_iota(jnp.int32, (nk_u, ts), 0) <= qcol, st, NEG_INF)
                    pt = jnp.exp(st - lse_ref[hh, i][:, cols])
                    dv_acc[:nk_u, hv] += jnp.dot(pt.astype(BF16), doi, preferred_element_type=F32)
                    dsb = (pt * (dpt - dl_ref[hh, i][:, cols])).astype(BF16)
                    dk_acc[:nk_u, hq] += jnp.dot(dsb, qi, preferred_element_type=F32)
                    dq_acc[hh, i, :, cols] += jnp.dot(kt_ref[hq, :nk_u], dsb, preferred_element_type=F32)

        def loop_body(i, carry):
            step(i, False)
            return carry

        step(j, True)
        lax.fori_loop(j + 1, nq, loop_body, 0)
        for hh in range(hg):
            a, m, b = hh * MLA_HP, hh * MLA_HP + MLA_NOPE, (hh + 1) * MLA_HP
            dkv_ref[:, a:m] = dk_acc[:, a:m].astype(BF16)
            dkv_ref[:, m:b] = dv_acc[:, hh * MLA_V:(hh + 1) * MLA_V].astype(BF16)
            dkr_ref[:, hh * 128:(hh + 1) * 128] = dk_acc[:, m:b].astype(BF16)

        @pl.when(j == nk - 1)
        def _():
            dqt_ref[...] = dq_acc[...].astype(BF16)

    stat = pl.BlockSpec((hg, nq, 1, tq), lambda h, j: (h, 0, 0, 0))
    return pl.pallas_call(
        body, name=name, grid=(MLA_H // hg, nk),
        in_specs=[pl.BlockSpec((T, hg * MLA_HP), lambda h, j: (0, h)), pl.BlockSpec((tk, hg * MLA_HP), lambda h, j: (j, h)),
                  pl.BlockSpec((hg * MLA_HP, tk), lambda h, j: (h, j)), pl.BlockSpec((tk, hg * MLA_V), lambda h, j: (j, h)),
                  pl.BlockSpec((T, hg * MLA_V), lambda h, j: (0, h)), stat, stat],
        out_specs=[pl.BlockSpec((hg, nq, MLA_HP, tq), lambda h, j: (h, 0, 0, 0)), pl.BlockSpec((tk, hg * MLA_HP), lambda h, j: (j, h)),
                   pl.BlockSpec((tk, hg * 128), lambda h, j: (j, h))],
        out_shape=[jax.ShapeDtypeStruct((MLA_H, nq, MLA_HP, tq), BF16), jax.ShapeDtypeStruct((T, MLA_H * MLA_HP), BF16),
                   jax.ShapeDtypeStruct((T, MLA_H * 128), BF16)],
        scratch_shapes=[pltpu.VMEM((hg, nq, MLA_HP, tq), F32), pltpu.VMEM((tk, hg * MLA_HP), F32), pltpu.VMEM((tk, hg * MLA_V), F32)],
        compiler_params=pltpu.CompilerParams(dimension_semantics=("parallel", "arbitrary")),
    )(q, k, kt, v, do, lse, delta)


ADA_TN = 512


def _silu(v):
    return v * (1.0 / (1.0 + jnp.exp(-v)))


def _ada_fwd(c_all, ada_w, ada_b_loc, name):
    L, D, Nc = ada_w.shape
    B = c_all.shape[0]

    def body(c_ref, w_ref, b_ref, o_ref):
        ca = _silu(c_ref[...]).astype(BF16)
        o_ref[...] = jnp.dot(ca, w_ref[...].astype(BF16), preferred_element_type=F32) + b_ref[...]

    return pl.pallas_call(
        body, name=name, grid=(L, Nc // ADA_TN),
        in_specs=[pl.BlockSpec((B, D), lambda l, n: (0, 0)), pl.BlockSpec((None, D, ADA_TN), lambda l, n: (l, 0, n)),
                  pl.BlockSpec((None, 1, ADA_TN), lambda l, n: (l, 0, n))],
        out_specs=pl.BlockSpec((None, B, ADA_TN), lambda l, n: (l, 0, n)),
        out_shape=jax.ShapeDtypeStruct((L, B, Nc), F32),
        compiler_params=pltpu.CompilerParams(dimension_semantics=("parallel", "parallel")),
    )(c_all, ada_w, ada_b_loc)


def _ada_bwd(c_all_t, dmod_loc, name):
    D, B = c_all_t.shape
    L, _, Nc = dmod_loc.shape

    def body(c_ref, d_ref, o_ref):
        ca = _silu(c_ref[...])
        dv = d_ref[...]
        acc = ca[:, 0:1] * dv[0:1, :]
        for b in range(1, B):
            acc = acc + ca[:, b:b + 1] * dv[b:b + 1, :]
        o_ref[...] = acc

    return pl.pallas_call(
        body, name=name, grid=(L, Nc // ADA_TN),
        in_specs=[pl.BlockSpec((D, B), lambda l, n: (0, 0)), pl.BlockSpec((None, B, ADA_TN), lambda l, n: (l, 0, n))],
        out_specs=pl.BlockSpec((None, D, ADA_TN), lambda l, n: (l, 0, n)),
        out_shape=jax.ShapeDtypeStruct((L, D, Nc), F32),
        compiler_params=pltpu.CompilerParams(dimension_semantics=("parallel", "parallel")),
    )(c_all_t, dmod_loc)


def _sum_lead(parts, name, out_dtype=F32):
    R, C = parts[0].shape[1:]
    n_tot = sum(p.shape[0] for p in parts)
    tr = R
    for cand in (512, 256, 128, 64, 32, 16):
        if R % cand == 0 and cand * C * 4 * n_tot <= (8 << 20):
            tr = cand
            break

    def body(*refs):
        o_ref = refs[-1]
        acc = None
        for r in refs[:-1]:
            for s in range(r.shape[0]):
                acc = r[s].astype(F32) if acc is None else acc + r[s].astype(F32)
        o_ref[...] = acc.astype(o_ref.dtype)

    return pl.pallas_call(
        body, name=name, grid=(R // tr,),
        in_specs=[pl.BlockSpec((p.shape[0], tr, C), lambda i: (0, i, 0)) for p in parts],
        out_specs=pl.BlockSpec((tr, C), lambda i: (i, 0)),
        out_shape=jax.ShapeDtypeStruct((R, C), out_dtype),
        compiler_params=pltpu.CompilerParams(dimension_semantics=("parallel",)),
    )(*parts)


_ADAM_C1 = 1.0 - ADAM_B1 ** ADAM_STEP
_ADAM_C2 = 1.0 - ADAM_B2 ** ADAM_STEP


def _adamw(w, g, m, v, name):
    shape = w.shape
    C = shape[-1]
    R = math.prod(shape[:-1]) if len(shape) > 1 else 1
    w2, g2, m2, v2 = (a.reshape(R, C) for a in (w, g, m, v))
    tr = R
    for cand in (1024, 512, 256, 128, 64, 32, 16, 8):
        if R % cand == 0 and cand * C * 4 <= (1 << 20):
            tr = cand
            break

    def body(w_ref, g_ref, m_ref, v_ref, d_ref, nm_ref, nv_ref):
        gv = g_ref[...]
        mn = ADAM_B1 * m_ref[...] + (1.0 - ADAM_B1) * gv
        vn = ADAM_B2 * v_ref[...] + (1.0 - ADAM_B2) * (gv * gv)
        nm_ref[...] = mn
        nv_ref[...] = vn
        m_hat = mn / _ADAM_C1
        v_hat = vn / _ADAM_C2
        d_ref[...] = -ADAM_LR * (m_hat / (jnp.sqrt(v_hat) + ADAM_EPS) + ADAM_WD * w_ref[...])

    spec = pl.BlockSpec((tr, C), lambda i: (i, 0))
    outs = pl.pallas_call(
        body, name=name, grid=(R // tr,),
        in_specs=[spec] * 4, out_specs=[spec] * 3,
        out_shape=[jax.ShapeDtypeStruct((R, C), F32)] * 3,
        compiler_params=pltpu.CompilerParams(dimension_semantics=("parallel",)),
    )(w2, g2, m2, v2)
    return tuple(o.reshape(shape) for o in outs)


def _row_tile(rows, cols, itemsize, budget):
    for cand in (1024, 512, 256, 128, 64, 32, 16):
        if rows % cand == 0 and cand * cols * itemsize <= budget:
            return cand
    return rows


def _sum_sel(sel, stacked, others, name, out_dtype):
    R, C = stacked.shape[1:]
    n_tot = 1 + sum(o.shape[0] for o in others)
    tr = _row_tile(R, C, 4 * n_tot, 8 << 20)

    def body(sel_ref, s_ref, *refs):
        o_ref = refs[-1]
        acc = s_ref[...].astype(F32)
        for r in refs[:-1]:
            for s in range(r.shape[0]):
                acc = acc + r[s].astype(F32)
        o_ref[...] = acc.astype(o_ref.dtype)

    return pl.pallas_call(
        body, name=name,
        grid_spec=pltpu.PrefetchScalarGridSpec(
            num_scalar_prefetch=1, grid=(R // tr,),
            in_specs=[pl.BlockSpec((None, tr, C), lambda i, s: (s[0], i, 0))] + [pl.BlockSpec((o.shape[0], tr, C), lambda i, s: (0, i, 0)) for o in others],
            out_specs=pl.BlockSpec((tr, C), lambda i, s: (i, 0))),
        out_shape=jax.ShapeDtypeStruct((R, C), out_dtype),
        compiler_params=pltpu.CompilerParams(dimension_semantics=("parallel",)),
    )(sel, stacked, *others)


def _adamw_piece(cidx, w2, m2, v2, mine, got, bufs, row0, name):
    hr, C = mine.shape
    tr = _row_tile(math.gcd(hr, row0) if row0 else hr, C, 4, 1 << 20)
    nt = hr // tr

    def body(c_ref, w_ref, m_ref, v_ref, a_ref, b_ref, _g, _d, _nm, _nv, g_ref, d_ref, nm_ref, nv_ref):
        gv = jnp.where(pl.program_id(0) == c_ref[0], a_ref[...], b_ref[...])
        mn = ADAM_B1 * m_ref[...] + (1.0 - ADAM_B1) * gv
        vn = ADAM_B2 * v_ref[...] + (1.0 - ADAM_B2) * (gv * gv)
        g_ref[...] = gv
        nm_ref[...] = mn
        nv_ref[...] = vn
        d_ref[...] = -ADAM_LR * ((mn / _ADAM_C1) / (jnp.sqrt(vn / _ADAM_C2) + ADAM_EPS) + ADAM_WD * w_ref[...])

    rows = pl.BlockSpec((tr, C), lambda hf, t, c: (row0 // tr + hf * nt + t, 0))
    mine_spec = pl.BlockSpec((tr, C), lambda hf, t, c: (jnp.where(hf == c[0], t, 0), 0))
    got_spec = pl.BlockSpec((tr, C), lambda hf, t, c: (jnp.where(hf == c[0], 0, t), 0))
    return pl.pallas_call(
        body, name=name,
        grid_spec=pltpu.PrefetchScalarGridSpec(num_scalar_prefetch=1, grid=(2, nt), in_specs=[rows] * 3 + [mine_spec, got_spec] + [_ANY_SPEC] * 4,
                                               out_specs=[rows] * 4),
        out_shape=[jax.ShapeDtypeStruct(w2.shape, F32)] * 4,
        input_output_aliases={6 + n: n for n in range(4)},
        compiler_params=pltpu.CompilerParams(dimension_semantics=("parallel", "parallel")),
    )(cidx, w2, m2, v2, mine, got, *bufs)


_VMEM_SPEC = pl.BlockSpec(memory_space=pltpu.VMEM)
_HBM_SPEC = pl.BlockSpec(memory_space=pltpu.HBM)


def _flip(v, bit):
    return (1 - v) if bit else v


def _allgather8(v, name):
    def body(v_ref, out_ref, send_sems, recv_sems, local_sem):
        x, y, c = _idx()
        me = 4 * x + 2 * y + c
        mine = pltpu.make_async_copy(v_ref, out_ref.at[me], local_sem)
        mine.start()
        sends = []
        for k in range(1, N_DEV):
            peer = (_flip(x, k & 4), _flip(y, k & 2), _flip(c, k & 1))
            cp = pltpu.make_async_remote_copy(src_ref=v_ref, dst_ref=out_ref.at[me], send_sem=send_sems.at[k - 1], recv_sem=recv_sems.at[k - 1],
                                              device_id=peer, device_id_type=MESH)
            cp.start()
            sends.append(cp)
        for k in range(1, N_DEV):
            px, py, pc = _flip(x, k & 4), _flip(y, k & 2), _flip(c, k & 1)
            src = 4 * px + 2 * py + pc
            pltpu.make_async_remote_copy(src_ref=v_ref, dst_ref=out_ref.at[src], send_sem=send_sems.at[k - 1], recv_sem=recv_sems.at[k - 1],
                                         device_id=(px, py, pc), device_id_type=MESH).wait_recv()
        for cp in sends:
            cp.wait_send()
        mine.wait()

    return pl.pallas_call(
        body, name=name,
        out_shape=jax.ShapeDtypeStruct((N_DEV, *v.shape), v.dtype),
        in_specs=[_VMEM_SPEC], out_specs=_VMEM_SPEC,
        scratch_shapes=[pltpu.SemaphoreType.DMA((N_DEV - 1,)), pltpu.SemaphoreType.DMA((N_DEV - 1,)), pltpu.SemaphoreType.DMA],
    )(v)


def _mod_exchange(modp, name):
    _, L, Nc = modp.shape

    def body(p_ref, out_ref, send_sems, recv_sems, local_sem):
        x, y, c = _idx()
        me, chip = 4 * x + 2 * y + c, 2 * x + y
        mine = pltpu.make_async_copy(p_ref.at[me], out_ref.at[chip], local_sem)
        mine.start()
        sends = []
        for k in range(1, N_CHIPS):
            px, py = _flip(x, k & 2), _flip(y, k & 1)
            cp = pltpu.make_async_remote_copy(src_ref=p_ref.at[4 * px + 2 * py + c], dst_ref=out_ref.at[chip],
                                              send_sem=send_sems.at[k - 1], recv_sem=recv_sems.at[k - 1], device_id=(px, py, c), device_id_type=MESH)
            cp.start()
            sends.append(cp)
        for k in range(1, N_CHIPS):
            px, py = _flip(x, k & 2), _flip(y, k & 1)
            pltpu.make_async_remote_copy(src_ref=p_ref.at[me], dst_ref=out_ref.at[2 * px + py], send_sem=send_sems.at[k - 1],
                                         recv_sem=recv_sems.at[k - 1], device_id=(px, py, c), device_id_type=MESH).wait_recv()
        for cp in sends:
            cp.wait_send()
        mine.wait()

    return pl.pallas_call(
        body, name=name,
        out_shape=jax.ShapeDtypeStruct((N_CHIPS, L, Nc), modp.dtype),
        in_specs=[_VMEM_SPEC], out_specs=_VMEM_SPEC,
        scratch_shapes=[pltpu.SemaphoreType.DMA((N_CHIPS - 1,)), pltpu.SemaphoreType.DMA((N_CHIPS - 1,)), pltpu.SemaphoreType.DMA],
    )(modp)


_SEM_SPEC = pl.BlockSpec(memory_space=pltpu.SEMAPHORE)
_ANY_SPEC = pl.BlockSpec(memory_space=pl.ANY)
_EFFECT = pltpu.SideEffectType.DATAFLOW_SIDE_EFFECTING


def _hbm(a):
    return pltpu.with_memory_space_constraint(a, pltpu.HBM)


def _xchip_copies(mode, srcs, lands, send_sems, recv_sems, waiting):
    x, y, c = _idx()
    chip = 2 * x + y
    out = []
    for a in range(len(srcs)):
        for k in range(1, _n_peers(mode) + 1):
            if mode == "all8":
                px, py, pc = _flip(x, k & 4), _flip(y, k & 2), _flip(c, k & 1)
                src, dst, mine = srcs[a], lands[a].at[4 * x + 2 * y + c], lands[a].at[4 * px + 2 * py + pc]
            elif mode == "scatter8":
                px, py, pc = _flip(x, k & 4), _flip(y, k & 2), _flip(c, k & 1)
                src, dst, mine = srcs[a].at[pc, 2 * px + py], lands[a].at[k - 1], lands[a].at[k - 1]
            else:
                px, py, pc = _flip(x, k & 2), _flip(y, k & 1), c
                peer = 2 * px + py
                if mode == "gather":
                    src, dst, mine = srcs[a].at[c], lands[a].at[chip, c], lands[a].at[peer, c]
                else:
                    src, dst, mine = srcs[a].at[peer], lands[a].at[k - 1], lands[a].at[k - 1]
            q = a * _n_peers(mode) + k - 1
            out.append(pltpu.make_async_remote_copy(src_ref=src, dst_ref=mine if waiting else dst, send_sem=send_sems[q], recv_sem=recv_sems[q],
                                                    device_id=(px, py, pc), device_id_type=MESH))
    return out


def _n_peers(mode):
    return N_DEV - 1 if mode in ("all8", "scatter8") else N_CHIPS - 1


def _xchip_start(mode, srcs, land_shapes, dep, name):
    n = len(srcs)
    ns = n * _n_peers(mode)

    def body(*refs):
        src_refs, land_refs = refs[:n], refs[n:2 * n]
        outs = refs[2 * n + 1:]
        for cp in _xchip_copies(mode, src_refs, land_refs, outs[:ns], outs[ns:2 * ns], waiting=False):
            cp.start()
        outs[-1][...] = jnp.zeros_like(outs[-1])

    lands = [_hbm(lax.empty(s.shape, s.dtype)) for s in land_shapes]
    outs = pl.pallas_call(
        body, name=name,
        out_shape=(*[pltpu.SemaphoreType.DMA(())] * (2 * ns), *[pltpu.HBM(s.shape, s.dtype) for s in srcs],
                   *[pltpu.HBM(s.shape, s.dtype) for s in land_shapes], jax.ShapeDtypeStruct((8, 128), F32)),
        in_specs=[_HBM_SPEC] * (2 * n) + [_ANY_SPEC],
        out_specs=(*[_SEM_SPEC] * (2 * ns), *[_HBM_SPEC] * (2 * n), _VMEM_SPEC),
        input_output_aliases={i: 2 * ns + i for i in range(2 * n)},
        compiler_params=pltpu.CompilerParams(has_side_effects=_EFFECT),
    )(*[_hbm(s) for s in srcs], *lands, dep)
    return list(outs[:ns]), list(outs[ns:2 * ns]), list(outs[2 * ns:2 * ns + n]), list(outs[2 * ns + n:2 * ns + 2 * n]), outs[-1]


def _xchip_wait(mode, send_sems, recv_sems, srcs, lands, after, name):
    n = len(srcs)
    ns = n * _n_peers(mode)

    def body(*refs):
        src_refs, land_refs = refs[:n], refs[n:2 * n]
        sems = refs[2 * n:2 * n + 2 * ns]
        for cp in _xchip_copies(mode, src_refs, land_refs, sems[:ns], sems[ns:], waiting=True):
            cp.wait_send()
            cp.wait_recv()

    outs = pl.pallas_call(
        body, name=name,
        out_shape=(*[pltpu.HBM(s.shape, s.dtype) for s in srcs], *[pltpu.HBM(s.shape, s.dtype) for s in lands]),
        in_specs=[_HBM_SPEC] * (2 * n) + [_SEM_SPEC] * (2 * ns) + [_ANY_SPEC] * len(after),
        out_specs=tuple([_HBM_SPEC] * (2 * n)),
        input_output_aliases={i: i for i in range(2 * n)},
        compiler_params=pltpu.CompilerParams(has_side_effects=_EFFECT),
    )(*srcs, *lands, *send_sems, *recv_sems, *after)
    return list(outs[:n]), list(outs[n:])


def _sibling_fwd(lands, name):
    n = len(lands)

    def body(*refs):
        outs = refs[n:2 * n]
        send_sems, recv_sems = refs[2 * n:]
        x, y, c = _idx()
        sib = (x, y, 1 - c)
        sends = []
        for a in range(n):
            for k in range(1, N_CHIPS):
                src = 2 * _flip(x, k & 2) + _flip(y, k & 1)
                cp = pltpu.make_async_remote_copy(src_ref=outs[a].at[src, c], dst_ref=outs[a].at[src, c], send_sem=send_sems.at[a, k - 1],
                                                  recv_sem=recv_sems.at[a, k - 1], device_id=sib, device_id_type=MESH)
                cp.start()
                sends.append(cp)
        for a in range(n):
            for k in range(1, N_CHIPS):
                src = 2 * _flip(x, k & 2) + _flip(y, k & 1)
                pltpu.make_async_remote_copy(src_ref=outs[a].at[src, c], dst_ref=outs[a].at[src, 1 - c], send_sem=send_sems.at[a, k - 1],
                                             recv_sem=recv_sems.at[a, k - 1], device_id=sib, device_id_type=MESH).wait_recv()
        for cp in sends:
            cp.wait_send()

    return pl.pallas_call(
        body, name=name,
        out_shape=[jax.ShapeDtypeStruct(s.shape, s.dtype) for s in lands],
        in_specs=[_HBM_SPEC] * n, out_specs=[_HBM_SPEC] * n,
        input_output_aliases={i: i for i in range(n)},
        scratch_shapes=[pltpu.SemaphoreType.DMA((n, N_CHIPS - 1)), pltpu.SemaphoreType.DMA((n, N_CHIPS - 1))],
    )(*lands)


def _sibling_send(halves, name):
    n = len(halves)

    def body(*refs):
        ins, outs = refs[:n], refs[n:2 * n]
        send_sems, recv_sems = refs[2 * n:]
        x, y, c = _idx()
        cps = []
        for a in range(n):
            cp = pltpu.make_async_remote_copy(src_ref=ins[a], dst_ref=outs[a], send_sem=send_sems.at[a], recv_sem=recv_sems.at[a],
                                              device_id=(x, y, 1 - c), device_id_type=MESH)
            cp.start()
            cps.append(cp)
        for cp in cps:
            cp.wait()

    return pl.pallas_call(
        body, name=name,
        out_shape=[jax.ShapeDtypeStruct(h.shape, h.dtype) for h in halves],
        in_specs=[_HBM_SPEC] * n, out_specs=[_HBM_SPEC] * n,
        scratch_shapes=[pltpu.SemaphoreType.DMA((n,)), pltpu.SemaphoreType.DMA((n,))],
    )(*halves)


def _col_full(g):
    k, n = g.shape[1], g.shape[2]
    return g.transpose(1, 0, 2).reshape(k, N_CHIPS * n)


def _col_blocks(w):
    k, n = w.shape
    return w.reshape(k, N_CHIPS, n // N_CHIPS).transpose(1, 0, 2)


def _row_blocks(w):
    k, n = w.shape
    return w.reshape(N_CHIPS, k // N_CHIPS, n)


_UQ_HEAD = MLA_NOPE + MLA_ROPE

_LAT = MLA_QL + MLA_KVL + MLA_ROPE
_POOL_R = len(POOL_WINDOWS) * (POOL_GD // N_CHIPS)

_PIECE_KINDS = {
    "mlp_w1": (D_MODEL, D_MODEL, lambda g: g, _col_blocks),
    "mlp_w2": (D_MODEL, D_MODEL, lambda g: g.reshape(4 * D_MODEL, D_MODEL), _row_blocks),
    "pool_w": (_POOL_R, POOL_GD,
               lambda g: g.reshape(N_CHIPS, len(POOL_WINDOWS), POOL_GD // N_CHIPS, POOL_GD).transpose(1, 0, 2, 3).reshape(len(POOL_WINDOWS), POOL_GD, POOL_GD),
               lambda w: w.reshape(len(POOL_WINDOWS), N_CHIPS, POOL_GD // N_CHIPS, POOL_GD).transpose(1, 0, 2, 3).reshape(N_CHIPS, _POOL_R, POOL_GD)),
    "sgu_w_in": (D_MODEL, 2 * SGU_W // N_CHIPS, _col_full, _col_blocks),
    "sgu_w_out": (SGU_W // N_CHIPS, D_MODEL, lambda g: g.reshape(SGU_W, D_MODEL), _row_blocks),
    "mla_w_dq_dkv": (D_MODEL // N_CHIPS, _LAT, lambda g: jnp.pad(g.reshape(D_MODEL, _LAT), ((0, 0), (0, MLA_LATP - _LAT))),
                     lambda w: _row_blocks(w[:, :_LAT])),
    "mla_w_uq": (MLA_QL, MLA_H * _UQ_HEAD // N_CHIPS,
                 lambda g: jnp.pad(_col_full(g).reshape(MLA_QL, MLA_H, _UQ_HEAD), ((0, 0), (0, 0), (0, MLA_HP - _UQ_HEAD))).reshape(MLA_QL, MLA_H * MLA_HP),
                 lambda w: _col_blocks(w.reshape(MLA_QL, MLA_H, MLA_HP)[:, :, :_UQ_HEAD].reshape(MLA_QL, MLA_H * _UQ_HEAD))),
    "mla_w_ukv": (MLA_KVL, MLA_H * (MLA_NOPE + MLA_V) // N_CHIPS, _col_full, _col_blocks),
    "mla_w_o": (MLA_H * MLA_V // N_CHIPS, D_MODEL, lambda g: g.reshape(MLA_H * MLA_V, D_MODEL), _row_blocks),
}
_MIXER_KINDS = (("pool_w",), ("sgu_w_in", "sgu_w_out"), ("mla_w_dq_dkv", "mla_w_uq", "mla_w_ukv", "mla_w_o"))


def _layer_pieces(i):
    return [(k, i // N_MIXERS) for k in _MIXER_KINDS[i % N_MIXERS]] + [("mlp_w1", i), ("mlp_w2", i)]


def _rope_tables(positions):
    inv_freq = ROPE_THETA ** (-jnp.arange(0, MLA_ROPE, 2, dtype=F32) / MLA_ROPE)
    ang = positions.astype(F32)[:, None] * inv_freq
    cos, sin = jnp.cos(ang), jnp.sin(ang)
    z32, z64 = jnp.zeros_like(cos), jnp.zeros((positions.shape[0], 64), F32)
    return (jnp.concatenate([cos, cos, z64], axis=1), jnp.concatenate([-sin, z32, z64], axis=1), jnp.concatenate([z32, sin, z64], axis=1))


def _local_step(x, positions, target, mod, S, weights_of, grads_of):
    D = D_MODEL
    cc, sa, sb = _rope_tables(positions)
    mods = [[mod[i:i + 1, n * D:(n + 1) * D] for n in range(6)] for i in range(DEPTH)]
    h_dtype = lambda i: F32 if i % N_MIXERS == 0 else BF16
    saved = []
    h = _norm_mod_fwd(x, S["norm_mix_g"][0:1], mods[0][1], mods[0][0], h_dtype(0), "l0_norm1")
    for i in range(DEPTH):
        sh1, sc1, g1, sh2, sc2, g2 = mods[i]
        kind, j = i % N_MIXERS, i // N_MIXERS
        gmlp = S["norm_mlp_g"][i:i + 1]
        W = weights_of(i, "mix", x)
        st = {"x": x}
        norm2 = ((gmlp, "n"), (sc2, "n"), (sh2, "n"))
        if kind == 0:
            x2, pooled, ypre, h2 = _pool_fwd(h, W["pool_w"], S["pool_scale"][j:j + 1], x, g1, gmlp, sc2, sh2, f"l{i}_pool")
            st.update(pooled=pooled, y=ypre)
        elif kind == 1:
            zz = _mm(h, W["sgu_w_in"], out_dtypes=(F32,), name=f"l{i}_sgu_in")
            bs_t = S["sgu_b_s"].T
            gated = _sgu_gate_fwd(zz, S["sgu_ln_g"], S["sgu_ln_b"], S["sgu_w_s"], bs_t, f"l{i}_sgu_gate")
            x2, y, h2 = _mm(gated, W["sgu_w_out"], epi=_epi_residual_norm, extras=((x, "mn"), (g1, "n"), *norm2), out_dtypes=(F32, BF16, BF16),
                            tn=D, name=f"l{i}_sgu_out")
            st.update(h=h, zz=zz, gated=gated, y=y, bs_t=bs_t)
        else:
            lat = _mm(h, W["mla_w_dq_dkv"], out_dtypes=(F32,), name=f"l{i}_mla_lat")
            cqn, ckvn, krot = _mla_lat_fwd(lat, S["mla_q_norm_g"], S["mla_kv_norm_g"], cc, sa, sb, f"l{i}_mla_latn")
            q = _mm(cqn, W["mla_w_uq"], epi=_epi_q_rope, extras=((cc, "m"), (sa, "m"), (sb, "m")), name=f"l{i}_mla_uq")
            k, kt, v, vt = _mla_ukv(ckvn, W["mla_w_ukv"], krot, f"l{i}_mla_ukv")
            o, lse = _attn_fwd(q, k, vt, f"l{i}_attn")
            x2, y, h2 = _mm(o, W["mla_w_o"], epi=_epi_residual_norm, extras=((x, "mn"), (g1, "n"), *norm2), out_dtypes=(F32, BF16, BF16),
                            tn=D, name=f"l{i}_mla_o")
            st.update(h=h, lat=lat, cqn=cqn, ckvn=ckvn, q=q, k=k, kt=kt, v=v, o=o, lse=lse, y=y)
        W = {**W, **weights_of(i, "mlp", x2)}
        z, r2 = _mm(h2, W["mlp_w1"], epi=_epi_sq_relu, out_dtypes=(BF16, BF16), epi_cols=MM_EPI_COLS, tm=MM_TM_WIDE, name=f"l{i}_mlp1")
        if i + 1 < DEPTH:
            norm1 = ((S["norm_mix_g"][i + 1:i + 2], "n"), (mods[i + 1][1], "n"), (mods[i + 1][0], "n"))
            x3, o2, h = _mm(z, W["mlp_w2"], epi=_epi_residual_norm, extras=((x2, "mn"), (g2, "n"), *norm1), out_dtypes=(F32, BF16, h_dtype(i + 1)),
                            tn=D, name=f"l{i}_mlp2")
        else:
            x3, o2 = _mm(z, W["mlp_w2"], epi=_epi_residual, extras=((x2, "mn"), (g2, "n")), out_dtypes=(F32, BF16), name=f"l{i}_mlp2")
        st.update(x2=x2, h2=h2, z=z, r2=r2, o2=o2, W=W)
        saved.append(st)
        x = x3

    loss, dx, dfinal_g = _loss_head(x, target, S["final_g"], "loss_head")

    gS = {"final_g": dfinal_g, "norm_mix_g": [None] * DEPTH, "norm_mlp_g": [None] * DEPTH, "pool_scale": [None] * 2}
    dmod = [None] * DEPTH
    do2, dg2 = _resid_bwd(dx, saved[-1]["o2"], mods[-1][5], f"l{DEPTH - 1}_b_res2")
    started = None
    for i in reversed(range(DEPTH)):
        st = saved[i]
        W, gW = st["W"], {}
        sh1, sc1, g1, sh2, sc2, g2 = mods[i]
        kind, j = i % N_MIXERS, i // N_MIXERS
        gmix, gmlp = S["norm_mix_g"][i:i + 1], S["norm_mlp_g"][i:i + 1]
        da = _mm(do2, W["mlp_w2"], tb=True, epi=lambda acc, rt: (acc * rt.astype(F32),), extras=((st["r2"], "mn"),), after=started, epi_cols=MM_EPI_COLS,
                 tm=MM_TM_WIDE, name=f"l{i}_b_dz")
        gW["mlp_w2"] = _mm(st["z"], do2, ta=True, chip_blocks="row", name=f"l{i}_b_dw2")
        dh2 = _mm(da, W["mlp_w1"], tb=True, name=f"l{i}_b_dh2")
        gW["mlp_w1"] = _mm(st["h2"], da, ta=True, chip_blocks="col", name=f"l{i}_b_dw1")
        dx2, dgmlp, dsc2, dsh2, dy, q1 = _norm_mod_bwd(st["x2"], dh2, dx, gmlp, sc2, f"l{i}_b_norm2", res=(st["y"], g1))
        gS["norm_mlp_g"][i] = dgmlp
        if kind == 0:
            dh, dpw, dpsc, dg1 = _pool_bwd(dy, st["pooled"], W["pool_w"], S["pool_scale"][j:j + 1], g1, q1, f"l{i}_b_pool")
            gW["pool_w"] = dpw.astype(BF16)
            gS["pool_scale"][j] = dpsc
        elif kind == 1:
            dg1 = q1
            dgated = _mm(dy, W["sgu_w_out"], tb=True, name=f"l{i}_b_dgated")
            gW["sgu_w_out"] = _mm(st["gated"], dy, ta=True, name=f"l{i}_b_dwout")
            dzz, dws, dbs, dlg, dlb = _sgu_gate_bwd(st["zz"], dgated, S["sgu_ln_g"], S["sgu_ln_b"], S["sgu_w_s"], st["bs_t"], f"l{i}_b_sgu_gate")
            gS.update(sgu_w_s=dws, sgu_b_s=dbs[:, :, 0], sgu_ln_g=dlg, sgu_ln_b=dlb)
            dh = _mm(dzz, W["sgu_w_in"], tb=True, name=f"l{i}_b_dh_sgu")
            gW["sgu_w_in"] = _mm(st["h"], dzz, ta=True, name=f"l{i}_b_dwin")
        else:
            dg1 = q1
            do = _mm(dy, W["mla_w_o"], tb=True, name=f"l{i}_b_do")
            gW["mla_w_o"] = _mm(st["o"], dy, ta=True, name=f"l{i}_b_dwo")
            delta = _attn_delta(do, st["o"], f"l{i}_b_delta")
            dqt, dkv, dkr = _attn_bwd(st["q"], st["k"], st["kt"], st["v"], do, st["lse"], delta, f"l{i}_b_attn")
            dqpad, dkrot = _mla_prep_bwd(dqt, dkr, cc, sa, sb, f"l{i}_b_mla_prep")
            dcqn = _mm(dqpad, W["mla_w_uq"], tb=True, out_dtypes=(F32,), name=f"l{i}_b_dcq")
            gW["mla_w_uq"] = _mm(st["cqn"], dqpad, ta=True, name=f"l{i}_b_dwuq")
            dckvn = _mm(dkv, W["mla_w_ukv"], tb=True, out_dtypes=(F32,), name=f"l{i}_b_dckv")
            gW["mla_w_ukv"] = _mm(st["ckvn"], dkv, ta=True, name=f"l{i}_b_dwukv")
            dlat, dqg, dkvg = _mla_lat_bwd(st["lat"], dcqn, dckvn, dkrot, S["mla_q_norm_g"], S["mla_kv_norm_g"], cc, sa, sb, f"l{i}_b_mla_latn")
            gS.update(mla_q_norm_g=dqg, mla_kv_norm_g=dkvg)
            dh = _mm(dlat, W["mla_w_dq_dkv"], tb=True, name=f"l{i}_b_dh_mla")
            gW["mla_w_dq_dkv"] = _mm(st["h"], dlat, ta=True, name=f"l{i}_b_dwdq")
        if i > 0:
            dx, dgmix, dsc1, dsh1, do2_prev, dg2_prev = _norm_mod_bwd(st["x"], dh, dx2, gmix, sc1, f"l{i}_b_norm1", res=(saved[i - 1]["o2"], mods[i - 1][5]))
        else:
            dx, dgmix, dsc1, dsh1 = _norm_mod_bwd(st["x"], dh, dx2, gmix, sc1, f"l{i}_b_norm1")
        gS["norm_mix_g"][i] = dgmix
        dmod[i] = jnp.concatenate([dsh1, dsc1, dg1, dsh2, dsc2, dg2], axis=1)
        started = grads_of(i, gW, dx)
        if i > 0:
            do2, dg2 = do2_prev, dg2_prev

    for n in ("norm_mix_g", "norm_mlp_g", "pool_scale"):
        gS[n] = jnp.concatenate(gS[n], axis=0)
    return loss, dx, gS, jnp.concatenate(dmod, axis=0)


_SMALL = {
    "norm_mix_g": (DEPTH, D_MODEL), "norm_mlp_g": (DEPTH, D_MODEL), "sgu_ln_g": (1, SGU_W), "sgu_ln_b": (1, SGU_W),
    "sgu_w_s": (SGU_H, SGU_CHUNK, SGU_CHUNK), "sgu_b_s": (SGU_H, SGU_CHUNK), "mla_kv_norm_g": (1, MLA_KVL), "final_g": (1, D_MODEL),
    "pool_scale": (2, D_MODEL), "mla_q_norm_g": (1, MLA_QL), "loss": (1, 128), "dmod": (DEPTH, 6 * D_MODEL),
}
_PACK_W = 1024


def _pack(vals):
    flat = jnp.concatenate([v.reshape(-1) for v in vals])
    rows = -(-flat.shape[0] // (8 * _PACK_W)) * 8
    return jnp.pad(flat, (0, rows * _PACK_W - flat.shape[0])).reshape(rows, _PACK_W)


def _unpack(buf, shapes):
    flat, out, off = buf.reshape(-1), [], 0
    for s in shapes:
        n = math.prod(s)
        out.append(flat[off:off + n].reshape(s))
        off += n
    return out


def kernel(x, c, positions, ada_w, ada_b, norm_mix_g, norm_mlp_g, pool_w, pool_scale, sgu_w_in, sgu_ln_g, sgu_ln_b, sgu_w_s, sgu_b_s, sgu_w_out, mla_w_dq_dkv, mla_q_norm_g, mla_kv_norm_g, mla_w_uq, mla_w_ukv, mla_w_o, mlp_w1, mlp_w2, final_g, loss_target, m_ada_w, m_ada_b, m_norm_mix_g, m_norm_mlp_g, m_pool_w, m_pool_scale, m_sgu_w_in, m_sgu_ln_g, m_sgu_ln_b, m_sgu_w_s, m_sgu_b_s, m_sgu_w_out, m_mla_w_dq_dkv, m_mla_q_norm_g, m_mla_kv_norm_g, m_mla_w_uq, m_mla_w_ukv, m_mla_w_o, m_mlp_w1, m_mlp_w2, m_final_g, v_ada_w, v_ada_b, v_norm_mix_g, v_norm_mlp_g, v_pool_w, v_pool_scale, v_sgu_w_in, v_sgu_ln_g, v_sgu_ln_b, v_sgu_w_s, v_sgu_b_s, v_sgu_w_out, v_mla_w_dq_dkv, v_mla_q_norm_g, v_mla_kv_norm_g, v_mla_w_uq, v_mla_w_ukv, v_mla_w_o, v_mlp_w1, v_mlp_w2, v_final_g):
    P = dict(ada_w=ada_w, ada_b=ada_b, norm_mix_g=norm_mix_g, norm_mlp_g=norm_mlp_g, pool_w=pool_w, pool_scale=pool_scale, sgu_w_in=sgu_w_in,
             sgu_ln_g=sgu_ln_g, sgu_ln_b=sgu_ln_b, sgu_w_s=sgu_w_s, sgu_b_s=sgu_b_s, sgu_w_out=sgu_w_out, mla_w_dq_dkv=mla_w_dq_dkv,
             mla_q_norm_g=mla_q_norm_g, mla_kv_norm_g=mla_kv_norm_g, mla_w_uq=mla_w_uq, mla_w_ukv=mla_w_ukv, mla_w_o=mla_w_o, mlp_w1=mlp_w1,
             mlp_w2=mlp_w2, final_g=final_g)
    M = dict(ada_w=m_ada_w, ada_b=m_ada_b, norm_mix_g=m_norm_mix_g, norm_mlp_g=m_norm_mlp_g, pool_w=m_pool_w, pool_scale=m_pool_scale,
             sgu_w_in=m_sgu_w_in, sgu_ln_g=m_sgu_ln_g, sgu_ln_b=m_sgu_ln_b, sgu_w_s=m_sgu_w_s, sgu_b_s=m_sgu_b_s, sgu_w_out=m_sgu_w_out,
             mla_w_dq_dkv=m_mla_w_dq_dkv, mla_q_norm_g=m_mla_q_norm_g, mla_kv_norm_g=m_mla_kv_norm_g, mla_w_uq=m_mla_w_uq, mla_w_ukv=m_mla_w_ukv,
             mla_w_o=m_mla_w_o, mlp_w1=m_mlp_w1, mlp_w2=m_mlp_w2, final_g=m_final_g)
    V = dict(ada_w=v_ada_w, ada_b=v_ada_b, norm_mix_g=v_norm_mix_g, norm_mlp_g=v_norm_mlp_g, pool_w=v_pool_w, pool_scale=v_pool_scale,
             sgu_w_in=v_sgu_w_in, sgu_ln_g=v_sgu_ln_g, sgu_ln_b=v_sgu_ln_b, sgu_w_s=v_sgu_w_s, sgu_b_s=v_sgu_b_s, sgu_w_out=v_sgu_w_out,
             mla_w_dq_dkv=v_mla_w_dq_dkv, mla_q_norm_g=v_mla_q_norm_g, mla_kv_norm_g=v_mla_kv_norm_g, mla_w_uq=v_mla_w_uq, mla_w_ukv=v_mla_w_ukv,
             mla_w_o=v_mla_w_o, mlp_w1=v_mlp_w1, mlp_w2=v_mlp_w2, final_g=v_final_g)
    order = list(P)
    xi, yi, ci = _idx()
    chip = 2 * xi + yi
    D = D_MODEL
    n_ada = ada_w.shape[2]

    pre = _allgather8(_pack([c, pool_scale, mla_q_norm_g]), "ag_small")
    flat = pre.reshape(N_DEV, -1)
    c_all = flat[:, :D]
    ps_all = flat[0::2, D:D + 2 * (D // N_CHIPS)].reshape(N_CHIPS, 2, D // N_CHIPS).transpose(1, 0, 2).reshape(2, D)
    q0 = D + 2 * (D // N_CHIPS)
    qg_all = flat[0::2, q0:q0 + MLA_QL // N_CHIPS].reshape(1, MLA_QL)

    ada_b_loc = lax.dynamic_slice_in_dim(ada_b, chip * n_ada, n_ada, axis=1)[:, None, :]
    modp = _ada_fwd(c_all, ada_w, ada_b_loc, "ada_fwd")
    mod = _mod_exchange(modp.transpose(1, 0, 2), "mod_exchange").transpose(1, 0, 2).reshape(DEPTH, 6 * D)

    S = dict(norm_mix_g=norm_mix_g, norm_mlp_g=norm_mlp_g, pool_scale=ps_all, sgu_ln_g=sgu_ln_g, sgu_ln_b=sgu_ln_b, sgu_w_s=sgu_w_s[0],
             sgu_b_s=sgu_b_s[0], mla_q_norm_g=qg_all, mla_kv_norm_g=mla_kv_norm_g, final_g=final_g[None, :])
    cidx, ownidx = jnp.reshape(ci, (1,)).astype(jnp.int32), jnp.reshape(N_CHIPS * ci + chip, (1,)).astype(jnp.int32)
    view2d = lambda a: a.reshape(-1, a.shape[-1])

    def piece_rows(kind, blk):
        r = _PIECE_KINDS[kind][0]
        return blk * r, r

    groups = [_layer_pieces(0)[:-2], _layer_pieces(0)[-2:], _layer_pieces(1)[:-2], _layer_pieces(1)[-2:], _layer_pieces(2), _layer_pieces(3)]
    start_after = {1: (2, 3), 2: (4,), 4: (5,)}
    gathers = {}

    def gather_start(g, dep):
        srcs, shapes = [], []
        for kind, blk in groups[g]:
            r0, r = piece_rows(kind, blk)
            cdim = _PIECE_KINDS[kind][1]
            srcs.append(view2d(P[kind])[r0:r0 + r].astype(BF16).reshape(2, r // 2, cdim))
            shapes.append(jax.ShapeDtypeStruct((N_CHIPS, 2, r // 2, cdim), BF16))
        gathers[g] = _xchip_start("gather", srcs, shapes, dep, f"ag_start_g{g}")

    def gather_finish(g, after):
        ssem, rsem, srcs, lands, _ = gathers.pop(g)
        deps = [after]
        for nxt in start_after.get(g, ()):
            gather_start(nxt, deps[-1])
            deps.append(gathers[nxt][-1])
        srcs, lands = _xchip_wait("gather", ssem, rsem, srcs, lands, deps, f"ag_wait_g{g}")
        lands = _sibling_fwd(lands, f"ag_sibling_g{g}")
        W = {}
        for (kind, _), s, land in zip(groups[g], srcs, lands, strict=True):
            r, cdim, to_full, _ = _PIECE_KINDS[kind]
            W[kind] = to_full(lax.dynamic_update_index_in_dim(land, s, chip, 0).reshape(N_CHIPS, r, cdim))
        return W

    def weights_of(i, part, x_i):
        if i < 2:
            return gather_finish(2 * i + (part == "mlp"), x_i)
        return gather_finish(i + 2, x_i) if part == "mix" else {}

    scatters = {}
    bufs = {n: tuple(lax.empty(view2d(P[n]).shape, F32) for _ in range(4)) for n in _PIECE_KINDS}

    def scatter_start(i, gW, dep):
        pcs = _layer_pieces(i)
        blocked = []
        for kind, _ in pcs:
            r, cdim, _, to_blocks = _PIECE_KINDS[kind]
            g = gW[kind]
            blocked.append(g if g.ndim == 4 else to_blocks(g).reshape(N_CHIPS, 2, r // 2, cdim).transpose(1, 0, 2, 3))
        shapes = [jax.ShapeDtypeStruct((N_DEV - 1, *b.shape[2:]), BF16) for b in blocked]
        scatters[i] = (pcs, *_xchip_start("scatter8", blocked, shapes, dep, f"rs_start_l{i}"))
        return scatters[i][-1]

    def scatter_finish(i, after):
        pcs, ssem, rsem, blocked, lands, _ = scatters.pop(i)
        blocked, lands = _xchip_wait("scatter8", ssem, rsem, blocked, lands, after, f"rs_wait_l{i}")
        halves = [_sum_sel(ownidx, b.reshape(2 * N_CHIPS, *b.shape[2:]), [l], f"rs_sum_l{i}_{kind}", F32)
                  for (kind, _), b, l in zip(pcs, blocked, lands, strict=True)]
        got = _sibling_send(halves, f"rs_merge_l{i}")
        for (kind, blk), mine, other in zip(pcs, halves, got, strict=True):
            r0, _ = piece_rows(kind, blk)
            bufs[kind] = tuple(_adamw_piece(cidx, view2d(P[kind]), view2d(M[kind]), view2d(V[kind]), mine, other, bufs[kind], r0,
                                            f"adamw_l{i}_{kind}"))
        return lands[0]

    first_layer = {}

    def grads_of(i, gW, dx_i):
        if i == 0:
            first_layer.update(gW)
            return None
        dep = scatter_finish(i + 1, [dx_i]) if i + 1 in scatters else dx_i
        return scatter_start(i, gW, dep)

    gather_start(0, mod)
    gather_start(1, gathers[0][-1])
    mod = mod + gathers[1][-1][0, 0]
    loss_l, dx, gS, dmod = _local_step(x[0], positions[0], loss_target[0], mod, S, weights_of, grads_of)

    gS["dmod"] = dmod
    gS["loss"] = loss_l
    packed = _pack([gS[n] for n in _SMALL])
    sg = _xchip_start("all8", [packed], [jax.ShapeDtypeStruct((N_DEV, *packed.shape), F32)], dx, "sg_start")
    tok0 = scatter_start(0, first_layer, sg[-1])[0, 0]
    scatter_finish(1, [dx, scatters[0][-1]])
    sg_src, sg_land = _xchip_wait("all8", sg[0], sg[1], sg[2], sg[3], [bufs[n][0] for n in ("mlp_w1", "mlp_w2", "sgu_w_in", "sgu_w_out")], "sg_wait")
    small = lax.dynamic_update_index_in_dim(sg_land[0], sg_src[0], 4 * xi + 2 * yi + ci, 0) + tok0
    small_sum = _unpack(_sum_lead([small], "sum_small_grads"), list(_SMALL.values()))
    G = dict(zip(_SMALL, small_sum, strict=True))
    grads = {
        "ada_b": G["dmod"], "norm_mix_g": G["norm_mix_g"], "norm_mlp_g": G["norm_mlp_g"], "sgu_ln_g": G["sgu_ln_g"], "sgu_ln_b": G["sgu_ln_b"],
        "sgu_w_s": G["sgu_w_s"][None], "sgu_b_s": G["sgu_b_s"][None], "mla_kv_norm_g": G["mla_kv_norm_g"], "final_g": G["final_g"][0],
        "pool_scale": lax.dynamic_slice_in_dim(G["pool_scale"], chip * (D // N_CHIPS), D // N_CHIPS, axis=1),
        "mla_q_norm_g": lax.dynamic_slice_in_dim(G["mla_q_norm_g"], chip * (MLA_QL // N_CHIPS), MLA_QL // N_CHIPS, axis=1),
    }
    dmod_all = _unpack(small, [(N_DEV,) + (small.shape[1] * _PACK_W,)])[0]
    off = sum(math.prod(s) for n, s in _SMALL.items() if n != "dmod")
    dmod_all = dmod_all[:, off:off + DEPTH * 6 * D].reshape(N_DEV, DEPTH, 6 * D)
    dmod_loc = lax.dynamic_slice_in_dim(dmod_all, chip * n_ada, n_ada, axis=2).transpose(1, 0, 2)
    grads["ada_w"] = _ada_bwd(c_all.T, dmod_loc, "ada_bwd")

    deltas, new_m, new_v = {}, {}, {}
    for n in order:
        if n not in _PIECE_KINDS:
            deltas[n], new_m[n], new_v[n] = _adamw(P[n], grads[n].reshape(P[n].shape), M[n], V[n], f"adamw_{n}")
    scatter_finish(0, [deltas["ada_w"], deltas["sgu_w_s"]] + [bufs[n][0] for n in ("mlp_w1", "mlp_w2", "sgu_w_in", "mla_w_o")])
    for n in _PIECE_KINDS:
        grads[n], deltas[n], new_m[n], new_v[n] = (b.reshape(P[n].shape) for b in bufs[n])
    return (G["loss"][0, 0], dx[None], *[grads[n].reshape(P[n].shape) for n in order], *[deltas[n] for n in order], *[new_m[n] for n in order],
            *[new_v[n] for n in order])
```

```python
import math

import jax
import jax.numpy as jnp
from jax import lax
from jax.experimental import pallas as pl
from jax.experimental.pallas import tpu as pltpu

F32, BF16 = jnp.float32, jnp.bfloat16
MESH = pl.DeviceIdType.MESH

D_MODEL = 1024
DEPTH = 4
N_MIXERS = 3
POOL_WINDOWS = (2, 4, 8, 16)
POOL_GD = D_MODEL // len(POOL_WINDOWS)
POOL_HALO = 16
SGU_CHUNK = 128
SGU_W = D_MODEL
SGU_HD = 128
SGU_H = SGU_W // SGU_HD
MLA_H = 16
MLA_QL = 256
MLA_KVL = 128
MLA_NOPE = 128
MLA_ROPE = 64
MLA_V = 128
MLA_HP = 256
MLA_LATP = 512
ROPE_THETA = 10000.0
RMS_EPS = 1e-6
LN_EPS = 1e-5
SM_SCALE = (MLA_NOPE + MLA_ROPE) ** -0.5
NEG_INF = -1e30
ADAM_LR, ADAM_B1, ADAM_B2, ADAM_EPS, ADAM_WD, ADAM_STEP = 0.001, 0.9, 0.999, 1e-08, 0.01, 10
N_CHIPS = 4
N_DEV = 8
ROW_TILE = 512
ATT_TILE = 512
ATT_SUB = 256
ATT_FWD_HEADS = 4
ATT_BWD_HEADS = 2
MM_EPI_COLS = 256
MM_TM_WIDE = 2048
MM_VMEM_BUDGET = 40 << 20


def _idx():
    return lax.axis_index("x"), lax.axis_index("y"), lax.axis_index("c")


def _mm(a, b, *, name, ta=False, tb=False, epi=None, extras=(), out_dtypes=(BF16,), tm=1024, tn=1024, tk=1024, chip_blocks=None, after=None,
        epi_cols=None):
    if ta:
        K, M = a.shape
    else:
        M, K = a.shape
    b_chips = b.ndim == 3
    if b_chips:
        assert b.shape[0] == N_CHIPS
        Kb, N = (N_CHIPS * b.shape[2], b.shape[1]) if tb else (b.shape[1], N_CHIPS * b.shape[2])
    elif tb:
        N, Kb = b.shape
    else:
        Kb, N = b.shape
    assert K == Kb, (a.shape, b.shape, ta, tb)
    if b_chips and not tb:
        tn = min(tn, N // N_CHIPS)
    if chip_blocks == "col":
        tm, tn = min(tm, M // 2), min(tn, N // N_CHIPS)
    elif chip_blocks == "row":
        tm = min(tm, M // N_CHIPS // 2)
    tm, tn, tk = min(tm, M), min(tn, N), min(tk, K)

    def vmem_bytes(tm_, tk_):
        per_mn = sum(arr.dtype.itemsize for arr, kind in extras if kind == "mn") + sum(jnp.dtype(dt).itemsize for dt in out_dtypes)
        return 2 * (tm_ * tk_ * a.dtype.itemsize + tk_ * tn * b.dtype.itemsize + tm_ * tn * per_mn)

    if vmem_bytes(tm, K) <= MM_VMEM_BUDGET:
        tk = K
    elif tm >= 512 and vmem_bytes(tm // 2, K) <= MM_VMEM_BUDGET:
        tm, tk = tm // 2, K
    assert M % tm == 0 and N % tn == 0 and K % tk == 0, (M, N, K, tm, tn, tk)
    nk = K // tk
    assert epi_cols is None or (nk == 1 and not ta and not (b_chips and tb) and tn % epi_cols == 0)
    a_spec = pl.BlockSpec((tk, tm), lambda i, j, k: (k, i)) if ta else pl.BlockSpec((tm, tk), lambda i, j, k: (i, k))
    b_spec = pl.BlockSpec((tn, tk), lambda i, j, k: (j, k)) if tb else pl.BlockSpec((tk, tn), lambda i, j, k: (k, j))
    if b_chips and tb:
        assert nk == 1 and not ta
        b_spec = pl.BlockSpec((N_CHIPS, tn, K // N_CHIPS), lambda i, j, k: (0, j, 0))
    elif b_chips:
        per = N // N_CHIPS // tn
        b_spec = pl.BlockSpec((None, tk, tn), lambda i, j, k: (j // per, k, j % per))
    ex_specs = []
    for arr, kind in extras:
        if kind == "mn":
            ex_specs.append(pl.BlockSpec((tm, tn), lambda i, j, k: (i, j)))
        elif kind == "n":
            ex_specs.append(pl.BlockSpec((1, tn), lambda i, j, k: (0, j)))
        else:
            ex_specs.append(pl.BlockSpec((tm, arr.shape[1]), lambda i, j, k: (i, 0)))
    n_ex, n_out = len(extras), len(out_dtypes)
    n_in = 2 + n_ex + (after is not None)
    dims = (((0 if ta else 1,), (1 if tb else 0,)), ((), ()))

    def body(*refs):
        a_ref, b_ref = refs[0], refs[1]
        ex_refs = refs[2:2 + n_ex]
        out_refs = refs[n_in:n_in + n_out]
        if b_chips and tb:
            kc = K // N_CHIPS
            part = None
            for cb in range(N_CHIPS):
                p = lax.dot_general(a_ref[:, cb * kc:(cb + 1) * kc].astype(BF16), b_ref[cb].astype(BF16), dims, preferred_element_type=F32)
                part = p if part is None else part + p
        elif epi_cols is not None:
            av = a_ref[...].astype(BF16)
            chunk = lambda cc: lax.dot_general(av, (b_ref[cc * epi_cols:(cc + 1) * epi_cols, :] if tb else b_ref[:, cc * epi_cols:(cc + 1) * epi_cols])
                                               .astype(BF16), dims, preferred_element_type=F32)
            acc = chunk(0)
            for cc in range(tn // epi_cols):
                nxt = chunk(cc + 1) if cc + 1 < tn // epi_cols else None
                cs = slice(cc * epi_cols, (cc + 1) * epi_cols)
                for r, o in zip(out_refs, epi(acc, *[r[:, cs] for r in ex_refs]), strict=True):
                    r[:, cs] = o.astype(r.dtype)
                acc = nxt
            return
        else:
            part = lax.dot_general(a_ref[...].astype(BF16), b_ref[...].astype(BF16), dims, preferred_element_type=F32)

        def finish(acc):
            outs = epi(acc, *[r[...] for r in ex_refs]) if epi is not None else (acc,)
            for r, o in zip(out_refs, outs, strict=True):
                r[...] = o.astype(r.dtype)

        if nk == 1:
            finish(part)
        else:
            acc_ref = refs[-1]
            k = pl.program_id(2)

            @pl.when(k == 0)
            def _():
                acc_ref[...] = part

            @pl.when(k > 0)
            def _():
                acc_ref[...] += part

            @pl.when(k == nk - 1)
            def _():
                finish(acc_ref[...])

    out_specs = [pl.BlockSpec((tm, tn), lambda i, j, k: (i, j)) for _ in range(n_out)]
    out_shape = [jax.ShapeDtypeStruct((M, N), dt) for dt in out_dtypes]
    if chip_blocks is not None:
        assert n_out == 1
        if chip_blocks == "col":
            rh, cb = M // 2 // tm, N // N_CHIPS // tn
            out_specs = [pl.BlockSpec((None, None, tm, tn), lambda i, j, k: (i // rh, j // cb, i % rh, j % cb))]
            out_shape = [jax.ShapeDtypeStruct((2, N_CHIPS, M // 2, N // N_CHIPS), out_dtypes[0])]
        else:
            rh = M // N_CHIPS // 2 // tm
            out_specs = [pl.BlockSpec((None, None, tm, tn), lambda i, j, k: ((i // rh) % 2, i // (2 * rh), i % rh, j))]
            out_shape = [jax.ShapeDtypeStruct((2, N_CHIPS, M // N_CHIPS // 2, N), out_dtypes[0])]
    outs = pl.pallas_call(
        body,
        name=name,
        grid=(M // tm, N // tn, nk),
        in_specs=[a_spec, b_spec, *ex_specs] + ([pl.BlockSpec(memory_space=pl.ANY)] if after is not None else []),
        out_specs=out_specs,
        out_shape=out_shape,
        scratch_shapes=[pltpu.VMEM((tm, tn), F32)] if nk > 1 else [],
        compiler_params=pltpu.CompilerParams(dimension_semantics=("parallel", "parallel", "arbitrary")),
    )(a, b, *[arr for arr, _ in extras], *([after] if after is not None else []))
    return outs[0] if n_out == 1 else tuple(outs)


def _epi_sq_relu(acc):
    r = jnp.maximum(acc, 0.0)
    return r * r, 2.0 * r


def _epi_residual(acc, x, g):
    return x + g * acc, acc


def _rms_mod(xv, gain, sc, sh):
    r = lax.rsqrt(jnp.mean(xv * xv, axis=-1, keepdims=True) + RMS_EPS)
    return ((xv * r) * gain) * (1.0 + sc) + sh


def _epi_residual_norm(acc, x, g, gain, sc, sh):
    xn = x + g * acc
    return xn, acc, _rms_mod(xn, gain, sc, sh)


def _row_spec(tr, d):
    return pl.BlockSpec((tr, d), lambda i: (i, 0))


def _vec_spec(d):
    return pl.BlockSpec((1, d), lambda i: (0, 0))


def _colsum(v):
    return jnp.sum(v, axis=0, keepdims=True)


def _norm_mod_fwd(x, gain, sc, sh, out_dtype, name):
    T, D = x.shape
    tr = min(T, ROW_TILE)

    def body(x_ref, g_ref, sc_ref, sh_ref, o_ref):
        o_ref[...] = _rms_mod(x_ref[...], g_ref[...], sc_ref[...], sh_ref[...]).astype(o_ref.dtype)

    return pl.pallas_call(
        body, name=name, grid=(T // tr,),
        in_specs=[_row_spec(tr, D), _vec_spec(D), _vec_spec(D), _vec_spec(D)],
        out_specs=_row_spec(tr, D),
        out_shape=jax.ShapeDtypeStruct((T, D), out_dtype),
        compiler_params=pltpu.CompilerParams(dimension_semantics=("parallel",)),
    )(x, gain, sc, sh)


def _norm_mod_bwd(x, dh, dres, gain, sc, name, res=None):
    T, D = x.shape
    tr = min(T, ROW_TILE)

    def body(x_ref, dh_ref, dres_ref, g_ref, sc_ref, *refs):
        dx_ref, dg_ref, dsc_ref, dsh_ref = refs[-6:-2] if res is not None else refs

        @pl.when(pl.program_id(0) == 0)
        def _():
            dg_ref[...] = jnp.zeros_like(dg_ref)
            dsc_ref[...] = jnp.zeros_like(dsc_ref)
            dsh_ref[...] = jnp.zeros_like(dsh_ref)
            if res is not None:
                refs[-1][...] = jnp.zeros_like(refs[-1])

        xv = x_ref[...]
        r = lax.rsqrt(jnp.mean(xv * xv, axis=-1, keepdims=True) + RMS_EPS)
        xn = xv * r
        dhv = dh_ref[...].astype(F32)
        dsh_ref[...] += _colsum(dhv)
        dsc_ref[...] += _colsum(dhv * (xn * g_ref[...]))
        dt = dhv * (1.0 + sc_ref[...])
        dg_ref[...] += _colsum(dt * xn)
        dxn = dt * g_ref[...]
        dxv = dres_ref[...] + r * (dxn - xn * jnp.mean(dxn * xn, axis=-1, keepdims=True))
        dx_ref[...] = dxv
        if res is not None:
            y_ref, gr_ref, dy_ref, q_ref = refs[0], refs[1], refs[-2], refs[-1]
            dy_ref[...] = (gr_ref[...] * dxv).astype(BF16)
            q_ref[...] += _colsum(dxv * y_ref[...].astype(F32))

    extra_in, extra_spec = ([], []) if res is None else (list(res), [_row_spec(tr, D), _vec_spec(D)])
    return pl.pallas_call(
        body, name=name, grid=(T // tr,),
        in_specs=[_row_spec(tr, D), _row_spec(tr, D), _row_spec(tr, D), _vec_spec(D), _vec_spec(D), *extra_spec],
        out_specs=[_row_spec(tr, D), _vec_spec(D), _vec_spec(D), _vec_spec(D)] + ([_row_spec(tr, D), _vec_spec(D)] if res is not None else []),
        out_shape=[jax.ShapeDtypeStruct((T, D), F32)] + [jax.ShapeDtypeStruct((1, D), F32)] * 3
        + ([jax.ShapeDtypeStruct((T, D), BF16), jax.ShapeDtypeStruct((1, D), F32)] if res is not None else []),
        compiler_params=pltpu.CompilerParams(dimension_semantics=("arbitrary",)),
    )(x, dh, dres, gain, sc, *extra_in)


def _resid_bwd(dx, y, g, name):
    T, D = dx.shape
    tr = min(T, ROW_TILE)

    def body(dx_ref, y_ref, g_ref, dy_ref, q_ref):
        @pl.when(pl.program_id(0) == 0)
        def _():
            q_ref[...] = jnp.zeros_like(q_ref)

        dxv = dx_ref[...]
        dy_ref[...] = (g_ref[...] * dxv).astype(BF16)
        q_ref[...] += _colsum(dxv * y_ref[...].astype(F32))

    return pl.pallas_call(
        body, name=name, grid=(T // tr,),
        in_specs=[_row_spec(tr, D), _row_spec(tr, D), _vec_spec(D)],
        out_specs=[_row_spec(tr, D), _vec_spec(D)],
        out_shape=[jax.ShapeDtypeStruct((T, D), BF16), jax.ShapeDtypeStruct((1, D), F32)],
        compiler_params=pltpu.CompilerParams(dimension_semantics=("arbitrary",)),
    )(dx, y, g)


def _loss_head(x, target, gain, name):
    T, D = x.shape
    tr = min(T, ROW_TILE)

    def body(x_ref, t_ref, g_ref, loss_ref, dx_ref, dg_ref):
        @pl.when(pl.program_id(0) == 0)
        def _():
            loss_ref[...] = jnp.zeros_like(loss_ref)
            dg_ref[...] = jnp.zeros_like(dg_ref)

        xv = x_ref[...]
        r = lax.rsqrt(jnp.mean(xv * xv, axis=-1, keepdims=True) + RMS_EPS)
        xn = xv * r
        err = xn * g_ref[...] - t_ref[...]
        row = jnp.mean(err * err, axis=-1, keepdims=True)
        loss_ref[...] += 0.5 * jnp.sum(row, axis=0, keepdims=True)
        dy = err * (1.0 / D)
        dg_ref[...] += _colsum(dy * xn)
        dxn = dy * g_ref[...]
        dx_ref[...] = r * (dxn - xn * jnp.mean(dxn * xn, axis=-1, keepdims=True))

    return pl.pallas_call(
        body, name=name, grid=(T // tr,),
        in_specs=[_row_spec(tr, D), _row_spec(tr, D), _vec_spec(D)],
        out_specs=[_vec_spec(128), _row_spec(tr, D), _vec_spec(D)],
        out_shape=[jax.ShapeDtypeStruct((1, 128), F32), jax.ShapeDtypeStruct((T, D), F32), jax.ShapeDtypeStruct((1, D), F32)],
        compiler_params=pltpu.CompilerParams(dimension_semantics=("arbitrary",)),
    )(x, target, gain)


def _pool_fwd(h, w, scale, x, g1, gmlp, sc2, sh2, name):
    T, D = h.shape
    tr = min(T, ROW_TILE)

    def body(h_ref, w_ref, sc_ref, x_ref, g_ref, gm_ref, sc2_ref, sh2_ref, x2_ref, pooled_ref, ypre_ref, h2_ref, halo_ref):
        i = pl.program_id(0)

        @pl.when(i == 0)
        def _():
            halo_ref[...] = jnp.zeros_like(halo_ref)

        hv = h_ref[...]
        buf = jnp.concatenate([halo_ref[...], hv], axis=0)
        halo_ref[...] = hv[tr - POOL_HALO:, :]
        t = (i * tr + lax.broadcasted_iota(jnp.int32, (tr, 1), 0)).astype(F32)
        for gi, win in enumerate(POOL_WINDOWS):
            cols = slice(gi * POOL_GD, (gi + 1) * POOL_GD)
            val = buf[:, cols]
            sh = 1
            while sh < win:
                val = val + pltpu.roll(val, sh, axis=0)
                sh *= 2
            pooled = val[POOL_HALO:, :] / jnp.minimum(t + 1.0, float(win)) - hv[:, cols]
            pb = pooled.astype(BF16)
            pooled_ref[:, cols] = pb
            yp = jnp.dot(pb, w_ref[gi], preferred_element_type=F32)
            ypre_ref[:, cols] = yp.astype(BF16)
            x2_ref[:, cols] = x_ref[:, cols] + g_ref[:, cols] * (yp * sc_ref[:, cols])
        h2_ref[...] = _rms_mod(x2_ref[...], gm_ref[...], sc2_ref[...], sh2_ref[...]).astype(BF16)

    return pl.pallas_call(
        body, name=name, grid=(T // tr,),
        in_specs=[_row_spec(tr, D), pl.BlockSpec(w.shape, lambda i: (0, 0, 0)), _vec_spec(D), _row_spec(tr, D), _vec_spec(D), _vec_spec(D),
                  _vec_spec(D), _vec_spec(D)],
        out_specs=[_row_spec(tr, D)] * 4,
        out_shape=[jax.ShapeDtypeStruct((T, D), F32), jax.ShapeDtypeStruct((T, D), BF16), jax.ShapeDtypeStruct((T, D), BF16),
                   jax.ShapeDtypeStruct((T, D), BF16)],
        scratch_shapes=[pltpu.VMEM((POOL_HALO, D), F32)],
        compiler_params=pltpu.CompilerParams(dimension_semantics=("arbitrary",)),
    )(h, w, scale, x, g1, gmlp, sc2, sh2)


def _pool_bwd(dy, pooled, w, scale, g1, q, name):
    T, D = dy.shape
    tr = min(T, ROW_TILE)
    nt = T // tr
    ltot = tr + POOL_HALO

    def body(dy_ref, pooled_ref, w_ref, sc_ref, g_ref, q_ref, dh_ref, dw_ref, dsc_ref, dg_ref, halo_ref):
        i = pl.program_id(0)

        @pl.when(i == 0)
        def _():
            halo_ref[...] = jnp.zeros_like(halo_ref)
            dw_ref[...] = jnp.zeros_like(dw_ref)
            dsc_ref[...] = g_ref[...] * q_ref[...]
            dg_ref[...] = sc_ref[...] * q_ref[...]

        t = ((nt - 1 - i) * tr + lax.broadcasted_iota(jnp.int32, (tr, 1), 0)).astype(F32)
        for gi, win in enumerate(POOL_WINDOWS):
            cols = slice(gi * POOL_GD, (gi + 1) * POOL_GD)
            dyb = (dy_ref[:, cols].astype(F32) * sc_ref[:, cols]).astype(BF16)
            dw_ref[gi] += lax.dot_general(pooled_ref[:, cols], dyb, (((0,), (0,)), ((), ())), preferred_element_type=F32)
            dpool = lax.dot_general(dyb, w_ref[gi], (((1,), (1,)), ((), ())), preferred_element_type=F32)
            qv = dpool / jnp.minimum(t + 1.0, float(win))
            val = jnp.concatenate([qv, halo_ref[:, cols]], axis=0)
            halo_ref[:, cols] = qv[:POOL_HALO, :]
            sh = 1
            while sh < win:
                val = val + pltpu.roll(val, ltot - sh, axis=0)
                sh *= 2
            dh_ref[:, cols] = (val[:tr, :] - dpool).astype(BF16)

    rev = pl.BlockSpec((tr, D), lambda i: (nt - 1 - i, 0))
    return pl.pallas_call(
        body, name=name, grid=(nt,),
        in_specs=[rev, rev, pl.BlockSpec(w.shape, lambda i: (0, 0, 0)), _vec_spec(D), _vec_spec(D), _vec_spec(D)],
        out_specs=[rev, pl.BlockSpec(w.shape, lambda i: (0, 0, 0)), _vec_spec(D), _vec_spec(D)],
        out_shape=[jax.ShapeDtypeStruct((T, D), BF16), jax.ShapeDtypeStruct(w.shape, F32),
                   jax.ShapeDtypeStruct((1, D), F32), jax.ShapeDtypeStruct((1, D), F32)],
        scratch_shapes=[pltpu.VMEM((POOL_HALO, D), F32)],
        compiler_params=pltpu.CompilerParams(dimension_semantics=("arbitrary",)),
    )(dy, pooled, w, scale, g1, q)


_INV_SQRT2 = 0.7071067811865476
_INV_SQRT2PI = 0.3989422804014327


def _gelu(v):
    return 0.5 * v * (1.0 + lax.erf(v * _INV_SQRT2))


def _gelu_grad(v):
    return 0.5 * (1.0 + lax.erf(v * _INV_SQRT2)) + v * jnp.exp(-0.5 * v * v) * _INV_SQRT2PI


def _sgu_ln(v, g, b):
    mu = jnp.mean(v, axis=-1, keepdims=True)
    xc = v - mu
    rstd = lax.rsqrt(jnp.mean(xc * xc, axis=-1, keepdims=True) + LN_EPS)
    xh = xc * rstd
    return xh, rstd, xh * g + b


def _tril_mask():
    return lax.broadcasted_iota(jnp.int32, (SGU_CHUNK, SGU_CHUNK), 0) >= lax.broadcasted_iota(jnp.int32, (SGU_CHUNK, SGU_CHUNK), 1)


SGU_TILE = 256


def _sgu_gate_fwd(zz, ln_g, ln_b, ws, bs_t, name):
    T = zz.shape[0]
    ts = min(T, SGU_TILE)

    def body(zz_ref, g_ref, b_ref, ws_ref, bs_ref, out_ref):
        z = _gelu(zz_ref[...])
        u = z[:, :SGU_W]
        _, _, vn = _sgu_ln(z[:, SGU_W:], g_ref[...], b_ref[...])
        vb = vn.astype(BF16)
        tril = _tril_mask()
        for hh in range(SGU_H):
            wm = jnp.where(tril, ws_ref[hh], 0.0).astype(BF16)
            bcol = bs_ref[:, hh:hh + 1]
            cs = slice(hh * SGU_HD, (hh + 1) * SGU_HD)
            for j in range(ts // SGU_CHUNK):
                rs = slice(j * SGU_CHUNK, (j + 1) * SGU_CHUNK)
                mixed = jnp.dot(wm, vb[rs, cs], preferred_element_type=F32) + bcol
                out_ref[rs, cs] = (u[rs, cs] * mixed).astype(BF16)

    return pl.pallas_call(
        body, name=name, grid=(T // ts,),
        in_specs=[_row_spec(ts, 2 * SGU_W), _vec_spec(SGU_W), _vec_spec(SGU_W),
                  pl.BlockSpec(ws.shape, lambda i: (0, 0, 0)), pl.BlockSpec(bs_t.shape, lambda i: (0, 0))],
        out_specs=_row_spec(ts, SGU_W),
        out_shape=jax.ShapeDtypeStruct((T, SGU_W), BF16),
        compiler_params=pltpu.CompilerParams(dimension_semantics=("parallel",)),
    )(zz, ln_g, ln_b, ws, bs_t)


def _sgu_gate_bwd(zz, dgated, ln_g, ln_b, ws, bs_t, name):
    T = zz.shape[0]
    ts = min(T, SGU_TILE)
    nt = T // ts

    def body(zz_ref, dg_ref, g_ref, b_ref, ws_ref, bs_ref, dzz_ref, dws_ref, dbs_ref, dlg_ref, dlb_ref, dlo_ref, dmx_ref):
        i = pl.program_id(0)

        @pl.when(i == 0)
        def _():
            dws_ref[...] = jnp.zeros_like(dws_ref)
            dmx_ref[...] = jnp.zeros_like(dmx_ref)
            dlg_ref[...] = jnp.zeros_like(dlg_ref)
            dlb_ref[...] = jnp.zeros_like(dlb_ref)

        zzv = zz_ref[...]
        z = _gelu(zzv)
        u = z[:, :SGU_W]
        xh, rstd, vn = _sgu_ln(z[:, SGU_W:], g_ref[...], b_ref[...])
        vb = vn.astype(BF16)
        dgv = dg_ref[...].astype(F32)
        tril = _tril_mask()
        for hh in range(SGU_H):
            wm = jnp.where(tril, ws_ref[hh], 0.0).astype(BF16)
            bcol = bs_ref[:, hh:hh + 1]
            cs = slice(hh * SGU_HD, (hh + 1) * SGU_HD)
            for j in range(ts // SGU_CHUNK):
                rs = slice(j * SGU_CHUNK, (j + 1) * SGU_CHUNK)
                mixed = jnp.dot(wm, vb[rs, cs], preferred_element_type=F32) + bcol
                dmixed = dgv[rs, cs] * u[rs, cs]
                dzz_ref[rs, cs] = (dgv[rs, cs] * mixed * _gelu_grad(zzv[rs, cs])).astype(BF16)
                dmb = dmixed.astype(BF16)
                dws_ref[hh] += lax.dot_general(dmb, vb[rs, cs], (((1,), (1,)), ((), ())), preferred_element_type=F32)
                dmx_ref[hh] += dmixed
                dlo_ref[rs, cs] = lax.dot_general(wm, dmb, (((0,), (0,)), ((), ())), preferred_element_type=F32)
        dlo = dlo_ref[...]
        dlg_ref[...] += _colsum(dlo * xh)
        dlb_ref[...] += _colsum(dlo)
        dxh = dlo * g_ref[...]
        dv = rstd * (dxh - jnp.mean(dxh, axis=-1, keepdims=True) - xh * jnp.mean(dxh * xh, axis=-1, keepdims=True))
        dzz_ref[:, SGU_W:] = (dv * _gelu_grad(zzv[:, SGU_W:])).astype(BF16)

        @pl.when(i == nt - 1)
        def _():
            tril_f = tril.astype(F32)
            for hh in range(SGU_H):
                dws_ref[hh] = dws_ref[hh] * tril_f
                dbs_ref[hh] = jnp.broadcast_to(jnp.sum(dmx_ref[hh], axis=-1, keepdims=True), (SGU_CHUNK, SGU_HD))

    full3 = pl.BlockSpec(ws.shape, lambda i: (0, 0, 0))
    return pl.pallas_call(
        body, name=name, grid=(nt,),
        in_specs=[_row_spec(ts, 2 * SGU_W), _row_spec(ts, SGU_W), _vec_spec(SGU_W), _vec_spec(SGU_W), full3,
                  pl.BlockSpec(bs_t.shape, lambda i: (0, 0))],
        out_specs=[_row_spec(ts, 2 * SGU_W), full3, full3, _vec_spec(SGU_W), _vec_spec(SGU_W)],
        out_shape=[jax.ShapeDtypeStruct((T, 2 * SGU_W), BF16), jax.ShapeDtypeStruct(ws.shape, F32), jax.ShapeDtypeStruct(ws.shape, F32),
                   jax.ShapeDtypeStruct((1, SGU_W), F32), jax.ShapeDtypeStruct((1, SGU_W), F32)],
        scratch_shapes=[pltpu.VMEM((ts, SGU_W), F32), pltpu.VMEM(ws.shape, F32)],
        compiler_params=pltpu.CompilerParams(dimension_semantics=("arbitrary",)),
    )(zz, dgated, ln_g, ln_b, ws, bs_t)


def _rope_fwd(blk, cc, sa, sb):
    return blk * cc + pltpu.roll(blk, 96, axis=1) * sa + pltpu.roll(blk, 32, axis=1) * sb


def _rope_bwd(d, cc, sa, sb):
    return d * cc + pltpu.roll(d * sa, 32, axis=1) + pltpu.roll(d * sb, 96, axis=1)


def _rms(v, g):
    r = lax.rsqrt(jnp.mean(v * v, axis=-1, keepdims=True) + RMS_EPS)
    vn = v * r
    return vn, r, vn * g


def _rms_bwd(dy, vn, r, g):
    dvn = dy * g
    return r * (dvn - vn * jnp.mean(dvn * vn, axis=-1, keepdims=True))


_KV0 = MLA_QL
_KR0 = MLA_QL + MLA_KVL


def _mla_lat_fwd(lat, qg, kvg, cc, sa, sb, name):
    T = lat.shape[0]
    tr = min(T, ROW_TILE)

    def body(lat_ref, qg_ref, kvg_ref, cc_ref, sa_ref, sb_ref, cq_ref, ckv_ref, kr_ref):
        lv = lat_ref[...]
        cq_ref[...] = _rms(lv[:, :_KV0], qg_ref[...])[2].astype(BF16)
        ckv_ref[...] = _rms(lv[:, _KV0:_KR0], kvg_ref[...])[2].astype(BF16)
        kr_ref[...] = _rope_fwd(lv[:, _KR0:], cc_ref[...], sa_ref[...], sb_ref[...])

    return pl.pallas_call(
        body, name=name, grid=(T // tr,),
        in_specs=[_row_spec(tr, MLA_LATP), _vec_spec(MLA_QL), _vec_spec(MLA_KVL), _row_spec(tr, 128), _row_spec(tr, 128), _row_spec(tr, 128)],
        out_specs=[_row_spec(tr, MLA_QL), _row_spec(tr, MLA_KVL), _row_spec(tr, 128)],
        out_shape=[jax.ShapeDtypeStruct((T, MLA_QL), BF16), jax.ShapeDtypeStruct((T, MLA_KVL), BF16), jax.ShapeDtypeStruct((T, 128), F32)],
        compiler_params=pltpu.CompilerParams(dimension_semantics=("parallel",)),
    )(lat, qg, kvg, cc, sa, sb)


def _mla_lat_bwd(lat, dcqn, dckvn, dkrot, qg, kvg, cc, sa, sb, name):
    T = lat.shape[0]
    tr = min(T, ROW_TILE)

    def body(lat_ref, dcq_ref, dckv_ref, dkr_ref, qg_ref, kvg_ref, cc_ref, sa_ref, sb_ref, dlat_ref, dqg_ref, dkvg_ref):
        @pl.when(pl.program_id(0) == 0)
        def _():
            dqg_ref[...] = jnp.zeros_like(dqg_ref)
            dkvg_ref[...] = jnp.zeros_like(dkvg_ref)

        lv = lat_ref[...]
        qn, qr, _ = _rms(lv[:, :_KV0], qg_ref[...])
        kn, kr, _ = _rms(lv[:, _KV0:_KR0], kvg_ref[...])
        dcq = dcq_ref[...]
        dckv = dckv_ref[...]
        dqg_ref[...] += _colsum(dcq * qn)
        dkvg_ref[...] += _colsum(dckv * kn)
        dlat_ref[:, :_KV0] = _rms_bwd(dcq, qn, qr, qg_ref[...]).astype(BF16)
        dlat_ref[:, _KV0:_KR0] = _rms_bwd(dckv, kn, kr, kvg_ref[...]).astype(BF16)
        dlat_ref[:, _KR0:] = _rope_bwd(dkr_ref[...], cc_ref[...], sa_ref[...], sb_ref[...]).astype(BF16)

    return pl.pallas_call(
        body, name=name, grid=(T // tr,),
        in_specs=[_row_spec(tr, MLA_LATP), _row_spec(tr, MLA_QL), _row_spec(tr, MLA_KVL), _row_spec(tr, 128),
                  _vec_spec(MLA_QL), _vec_spec(MLA_KVL), _row_spec(tr, 128), _row_spec(tr, 128), _row_spec(tr, 128)],
        out_specs=[_row_spec(tr, MLA_LATP), _vec_spec(MLA_QL), _vec_spec(MLA_KVL)],
        out_shape=[jax.ShapeDtypeStruct((T, MLA_LATP), BF16), jax.ShapeDtypeStruct((1, MLA_QL), F32), jax.ShapeDtypeStruct((1, MLA_KVL), F32)],
        compiler_params=pltpu.CompilerParams(dimension_semantics=("arbitrary",)),
    )(lat, dcqn, dckvn, dkrot, qg, kvg, cc, sa, sb)


LOG2E = 1.4426950408889634
Q_SCALE = SM_SCALE * LOG2E


def _epi_q_rope(acc, cc, sa, sb):
    out = []
    for hh in range(acc.shape[1] // MLA_HP):
        a, m, b = hh * MLA_HP, hh * MLA_HP + MLA_NOPE, (hh + 1) * MLA_HP
        out += [acc[:, a:m] * Q_SCALE, _rope_fwd(acc[:, m:b], cc, sa, sb) * Q_SCALE]
    return (jnp.concatenate(out, axis=1),)


def _mla_ukv(ckvn, w_ukv, krot, name):
    T = ckvn.shape[0]
    tr = min(T, ATT_TILE)
    hg = ATT_HG
    gw = hg * MLA_HP

    def body(a_ref, w_ref, kr_ref, ko_ref, kt_ref, vo_ref, vt_ref):
        acc = jnp.dot(a_ref[...], w_ref[...], preferred_element_type=F32)
        kr = kr_ref[...]
        krb, krt = kr.astype(BF16), kr.T.astype(BF16)
        for hh in range(hg):
            a, m, b = hh * MLA_HP, hh * MLA_HP + MLA_NOPE, (hh + 1) * MLA_HP
            kn, vh = acc[:, a:m], acc[:, m:b]
            ko_ref[:, a:m] = kn.astype(BF16)
            ko_ref[:, m:b] = krb
            kt_ref[a:m, :] = kn.T.astype(BF16)
            kt_ref[m:b, :] = krt
            vo_ref[:, hh * MLA_V:(hh + 1) * MLA_V] = vh.astype(BF16)
            vt_ref[hh] = vh.T.astype(BF16)

    tk = min(T, ATT_TILE)
    per = tk // tr
    HW = MLA_H * MLA_HP
    return pl.pallas_call(
        body, name=name, grid=(T // tr, MLA_H // hg),
        in_specs=[pl.BlockSpec((tr, MLA_KVL), lambda i, g: (i, 0)), pl.BlockSpec((MLA_KVL, gw), lambda i, g: (0, g)),
                  pl.BlockSpec((tr, 128), lambda i, g: (i, 0))],
        out_specs=[pl.BlockSpec((tr, gw), lambda i, g: (i, g)), pl.BlockSpec((gw, tr), lambda i, g: (g, i)),
                   pl.BlockSpec((tr, hg * MLA_V), lambda i, g: (i, g)),
                   pl.BlockSpec((hg, None, MLA_V, tr), lambda i, g: (g, i // per, 0, i % per))],
        out_shape=[jax.ShapeDtypeStruct((T, HW), BF16), jax.ShapeDtypeStruct((HW, T), BF16), jax.ShapeDtypeStruct((T, MLA_H * MLA_V), BF16),
                   jax.ShapeDtypeStruct((MLA_H, T // tk, MLA_V, tk), BF16)],
        compiler_params=pltpu.CompilerParams(dimension_semantics=("parallel", "parallel")),
    )(ckvn, w_ukv, krot)


ATT_HG = 4


def _mla_prep_bwd(dqt, dkr, cc, sa, sb, name):
    _, nq, _, tq = dqt.shape
    T = nq * tq
    gw = ATT_HG * MLA_HP

    def body(dq_ref, dk_ref, cc_ref, sa_ref, sb_ref, dqp_ref, dkr_ref):
        @pl.when(pl.program_id(1) == 0)
        def _():
            dkr_ref[...] = jnp.zeros_like(dkr_ref)

        cc, sa, sb = cc_ref[...], sa_ref[...], sb_ref[...]
        acc = jnp.zeros((tq, 128), F32)
        for hh in range(ATT_HG):
            a, m, b = hh * MLA_HP, hh * MLA_HP + MLA_NOPE, (hh + 1) * MLA_HP
            dqh = dq_ref[hh].astype(F32).T * SM_SCALE
            dqp_ref[:, a:m] = dqh[:, :MLA_NOPE].astype(BF16)
            dqp_ref[:, m:b] = _rope_bwd(dqh[:, MLA_NOPE:], cc, sa, sb).astype(BF16)
            acc = acc + dk_ref[:, hh * 128:(hh + 1) * 128].astype(F32)
        dkr_ref[...] += acc

    tab = pl.BlockSpec((tq, 128), lambda i, g: (i, 0))
    return pl.pallas_call(
        body, name=name, grid=(nq, MLA_H // ATT_HG),
        in_specs=[pl.BlockSpec((ATT_HG, None, MLA_HP, tq), lambda i, g: (g, i, 0, 0)), pl.BlockSpec((tq, ATT_HG * 128), lambda i, g: (i, g)),
                  tab, tab, tab],
        out_specs=[pl.BlockSpec((tq, gw), lambda i, g: (i, g)), tab],
        out_shape=[jax.ShapeDtypeStruct((T, MLA_H * MLA_HP), BF16), jax.ShapeDtypeStruct((T, 128), F32)],
        compiler_params=pltpu.CompilerParams(dimension_semantics=("parallel", "arbitrary")),
    )(dqt, dkr, cc, sa, sb)


_NT = (((1,), (1,)), ((), ()))


def _as_row(col, n):
    return jnp.broadcast_to(col, (n, 128)).T[0:1, :]


def _attn_fwd(q, k, vt, name):
    T = q.shape[0]
    tq = tk = min(T, ATT_TILE)
    nq = T // tq
    hg = ATT_FWD_HEADS

    def body(q_ref, k_ref, vt_ref, o_ref, lse_ref, m_ref, l_ref, acc_ref):
        i = pl.program_id(1)
        m_ref[...] = jnp.full_like(m_ref, NEG_INF)
        l_ref[...] = jnp.zeros_like(l_ref)
        acc_ref[...] = jnp.zeros_like(acc_ref)

        def step(j, diag):
            off = pl.multiple_of(j * tk, tk)
            sts = [lax.dot_general(k_ref[pl.ds(off, tk), hh * MLA_HP:(hh + 1) * MLA_HP], q_ref[:, hh * MLA_HP:(hh + 1) * MLA_HP], _NT,
                                   preferred_element_type=F32) for hh in range(hg)]
            for hh in range(hg):
                st = sts[hh]
                if diag:
                    st = jnp.where(lax.broadcasted_iota(jnp.int32, (tk, tq), 0) <= lax.broadcasted_iota(jnp.int32, (tk, tq), 1), st, NEG_INF)
                m_prev = m_ref[hh]
                m_new = jnp.maximum(m_prev, jnp.max(st, axis=0, keepdims=True))
                alpha = jnp.exp2(m_prev - m_new)
                pt = jnp.exp2(st - m_new)
                l_ref[hh] = alpha * l_ref[hh] + jnp.sum(pt, axis=0, keepdims=True)
                acc_ref[hh] = alpha * acc_ref[hh] + jnp.dot(vt_ref[hh, j], pt.astype(BF16), preferred_element_type=F32)
                m_ref[hh] = m_new

        def loop_body(j, carry):
            step(j, False)
            return carry

        lax.fori_loop(0, i, loop_body, 0)
        step(i, True)
        for hh in range(hg):
            o_ref[:, hh * MLA_V:(hh + 1) * MLA_V] = (acc_ref[hh] / l_ref[hh]).T.astype(BF16)
            lse_ref[hh] = m_ref[hh] + jnp.log2(l_ref[hh])

    return pl.pallas_call(
        body, name=name, grid=(MLA_H // hg, nq),
        in_specs=[pl.BlockSpec((tq, hg * MLA_HP), lambda h, i: (i, h)), pl.BlockSpec((T, hg * MLA_HP), lambda h, i: (0, h)),
                  pl.BlockSpec((hg, nq, MLA_V, tk), lambda h, i: (h, 0, 0, 0))],
        out_specs=[pl.BlockSpec((tq, hg * MLA_V), lambda h, i: (i, h)), pl.BlockSpec((hg, None, 1, tq), lambda h, i: (h, i, 0, 0))],
        out_shape=[jax.ShapeDtypeStruct((T, MLA_H * MLA_V), BF16), jax.ShapeDtypeStruct((MLA_H, nq, 1, tq), F32)],
        scratch_shapes=[pltpu.VMEM((hg, 1, tq), F32), pltpu.VMEM((hg, 1, tq), F32), pltpu.VMEM((hg, MLA_V, tq), F32)],
        compiler_params=pltpu.CompilerParams(dimension_semantics=("parallel", "arbitrary")),
    )(q, k, vt)


def _attn_delta(do, o, name):
    T = do.shape[0]
    tq = min(T, ATT_TILE)

    def body(do_ref, o_ref, d_ref):
        for hh in range(MLA_H):
            cs = slice(hh * MLA_V, (hh + 1) * MLA_V)
            s = jnp.sum(do_ref[:, cs].astype(F32) * o_ref[:, cs].astype(F32), axis=-1, keepdims=True)
            d_ref[hh] = _as_row(s, tq)

    return pl.pallas_call(
        body, name=name, grid=(T // tq,),
        in_specs=[_row_spec(tq, MLA_H * MLA_V), _row_spec(tq, MLA_H * MLA_V)],
        out_specs=pl.BlockSpec((MLA_H, None, 1, tq), lambda i: (0, i, 0, 0)),
        out_shape=jax.ShapeDtypeStruct((MLA_H, T // tq, 1, tq), F32),
        compiler_params=pltpu.CompilerParams(dimension_semantics=("parallel",)),
    )(do, o)


def _attn_bwd(q, k, kt, v, do, lse, delta, name):
    T = q.shape[0]
    tq = tk = min(T, ATT_TILE)
    nq = nk = T // tq
    tsd = min(tq, ATT_SUB)
    hg = ATT_BWD_HEADS

    def body(q_ref, k_ref, kt_ref, v_ref, do_ref, lse_ref, dl_ref, dqt_ref, dkv_ref, dkr_ref, dq_acc, dk_acc, dv_acc):
        j = pl.program_id(1)

        @pl.when(j == 0)
        def _():
            dq_acc[...] = jnp.zeros_like(dq_acc)

        dk_acc[...] = jnp.zeros_like(dk_acc)
        dv_acc[...] = jnp.zeros_like(dv_acc)

        def step(i, diag):
            off = pl.multiple_of(i * tq, tq)
            ts, nsub = (tsd, tq // tsd) if diag else (tq, 1)
            for u in range(nsub):
                cols = slice(u * ts, (u + 1) * ts)
                nk_u = (u + 1) * ts if diag else tk
                rows = pl.ds(off + u * ts, ts)
                pre = []
                for hh in range(hg):
                    hq, hv = slice(hh * MLA_HP, (hh + 1) * MLA_HP), slice(hh * MLA_V, (hh + 1) * MLA_V)
                    qi, doi = q_ref[rows, hq], do_ref[rows, hv]
                    st = lax.dot_general(k_ref[:nk_u, hq], qi, _NT, preferred_element_type=F32)
                    dpt = lax.dot_general(v_ref[:nk_u, hv], doi, _NT, preferred_element_type=F32)
                    pre.append((qi, doi, st, dpt))
                for hh in range(hg):
                    hq, hv = slice(hh * MLA_HP, (hh + 1) * MLA_HP), slice(hh * MLA_V, (hh + 1) * MLA_V)
                    qi, doi, st, dpt = pre[hh]
                    if diag:
                        qcol = u * ts + lax.broadcasted_iota(jnp.int32, (nk_u, ts), 1)
                        st = jnp.where(lax.broadcasted_iota(jnp.int32, (nk_u, ts), 0) <= qcol, st, NEG_INF)
                    pt = jnp.exp2(st - lse_ref[hh, i][:, cols])
                    dv_acc[:nk_u, hv] += jnp.dot(pt.astype(BF16), doi, preferred_element_type=F32)
                    dsb = (pt * (dpt - dl_ref[hh, i][:, cols])).astype(BF16)
                    dk_acc[:nk_u, hq] += jnp.dot(dsb, qi, preferred_element_type=F32)
                    dq_acc[hh, i, :, cols] += jnp.dot(kt_ref[hq, :nk_u], dsb, preferred_element_type=F32)

        def loop_body(i, carry):
            step(i, False)
            return carry

        step(j, True)
        lax.fori_loop(j + 1, nq, loop_body, 0)
        for hh in range(hg):
            a, m, b = hh * MLA_HP, hh * MLA_HP + MLA_NOPE, (hh + 1) * MLA_HP
            dkv_ref[:, a:m] = (dk_acc[:, a:m] * (1.0 / LOG2E)).astype(BF16)
            dkv_ref[:, m:b] = dv_acc[:, hh * MLA_V:(hh + 1) * MLA_V].astype(BF16)
            dkr_ref[:, hh * 128:(hh + 1) * 128] = (dk_acc[:, m:b] * (1.0 / LOG2E)).astype(BF16)

        @pl.when(j == nk - 1)
        def _():
            dqt_ref[...] = dq_acc[...].astype(BF16)

    stat = pl.BlockSpec((hg, nq, 1, tq), lambda h, j: (h, 0, 0, 0))
    return pl.pallas_call(
        body, name=name, grid=(MLA_H // hg, nk),
        in_specs=[pl.BlockSpec((T, hg * MLA_HP), lambda h, j: (0, h)), pl.BlockSpec((tk, hg * MLA_HP), lambda h, j: (j, h)),
                  pl.BlockSpec((hg * MLA_HP, tk), lambda h, j: (h, j)), pl.BlockSpec((tk, hg * MLA_V), lambda h, j: (j, h)),
                  pl.BlockSpec((T, hg * MLA_V), lambda h, j: (0, h)), stat, stat],
        out_specs=[pl.BlockSpec((hg, nq, MLA_HP, tq), lambda h, j: (h, 0, 0, 0)), pl.BlockSpec((tk, hg * MLA_HP), lambda h, j: (j, h)),
                   pl.BlockSpec((tk, hg * 128), lambda h, j: (j, h))],
        out_shape=[jax.ShapeDtypeStruct((MLA_H, nq, MLA_HP, tq), BF16), jax.ShapeDtypeStruct((T, MLA_H * MLA_HP), BF16),
                   jax.ShapeDtypeStruct((T, MLA_H * 128), BF16)],
        scratch_shapes=[pltpu.VMEM((hg, nq, MLA_HP, tq), F32), pltpu.VMEM((tk, hg * MLA_HP), F32), pltpu.VMEM((tk, hg * MLA_V), F32)],
        compiler_params=pltpu.CompilerParams(dimension_semantics=("parallel", "arbitrary")),
    )(q, k, kt, v, do, lse, delta)


ADA_TN = 512


def _silu(v):
    return v * (1.0 / (1.0 + jnp.exp(-v)))


def _ada_fwd(c_all, ada_w, ada_b_loc, name):
    L, D, Nc = ada_w.shape
    B = c_all.shape[0]

    def body(c_ref, w_ref, b_ref, o_ref):
        ca = _silu(c_ref[...]).astype(BF16)
        o_ref[...] = jnp.dot(ca, w_ref[...].astype(BF16), preferred_element_type=F32) + b_ref[...]

    return pl.pallas_call(
        body, name=name, grid=(L, Nc // ADA_TN),
        in_specs=[pl.BlockSpec((B, D), lambda l, n: (0, 0)), pl.BlockSpec((None, D, ADA_TN), lambda l, n: (l, 0, n)),
                  pl.BlockSpec((None, 1, ADA_TN), lambda l, n: (l, 0, n))],
        out_specs=pl.BlockSpec((None, B, ADA_TN), lambda l, n: (l, 0, n)),
        out_shape=jax.ShapeDtypeStruct((L, B, Nc), F32),
        compiler_params=pltpu.CompilerParams(dimension_semantics=("parallel", "parallel")),
    )(c_all, ada_w, ada_b_loc)


def _ada_bwd_adamw(c_all_t, dmod_loc, w, m, v, name):
    D, B = c_all_t.shape
    L, _, Nc = dmod_loc.shape

    def body(c_ref, d_ref, w_ref, m_ref, v_ref, g_ref, dl_ref, nm_ref, nv_ref):
        ca = _silu(c_ref[...])
        dv = d_ref[...]
        gv = ca[:, 0:1] * dv[0:1, :]
        for b in range(1, B):
            gv = gv + ca[:, b:b + 1] * dv[b:b + 1, :]
        mn = ADAM_B1 * m_ref[...] + (1.0 - ADAM_B1) * gv
        vn = ADAM_B2 * v_ref[...] + (1.0 - ADAM_B2) * (gv * gv)
        g_ref[...] = gv
        nm_ref[...] = mn
        nv_ref[...] = vn
        dl_ref[...] = -ADAM_LR * ((mn / _ADAM_C1) / (jnp.sqrt(vn / _ADAM_C2) + ADAM_EPS) + ADAM_WD * w_ref[...])

    blk = pl.BlockSpec((None, D, ADA_TN), lambda l, n: (l, 0, n))
    return pl.pallas_call(
        body, name=name, grid=(L, Nc // ADA_TN),
        in_specs=[pl.BlockSpec((D, B), lambda l, n: (0, 0)), pl.BlockSpec((None, B, ADA_TN), lambda l, n: (l, 0, n)), blk, blk, blk],
        out_specs=[blk] * 4,
        out_shape=[jax.ShapeDtypeStruct((L, D, Nc), F32)] * 4,
        compiler_params=pltpu.CompilerParams(dimension_semantics=("parallel", "parallel")),
    )(c_all_t, dmod_loc, w, m, v)


def _sum_lead(parts, name, out_dtype=F32):
    R, C = parts[0].shape[1:]
    n_tot = sum(p.shape[0] for p in parts)
    tr = R
    for cand in (512, 256, 128, 64, 32, 16):
        if R % cand == 0 and cand * C * 4 * n_tot <= (8 << 20):
            tr = cand
            break

    def body(*refs):
        o_ref = refs[-1]
        acc = None
        for r in refs[:-1]:
            for s in range(r.shape[0]):
                acc = r[s].astype(F32) if acc is None else acc + r[s].astype(F32)
        o_ref[...] = acc.astype(o_ref.dtype)

    return pl.pallas_call(
        body, name=name, grid=(R // tr,),
        in_specs=[pl.BlockSpec((p.shape[0], tr, C), lambda i: (0, i, 0)) for p in parts],
        out_specs=pl.BlockSpec((tr, C), lambda i: (i, 0)),
        out_shape=jax.ShapeDtypeStruct((R, C), out_dtype),
        compiler_params=pltpu.CompilerParams(dimension_semantics=("parallel",)),
    )(*parts)


_ADAM_C1 = 1.0 - ADAM_B1 ** ADAM_STEP
_ADAM_C2 = 1.0 - ADAM_B2 ** ADAM_STEP


def _adamw(w, g, m, v, name):
    shape = w.shape
    C = shape[-1]
    R = math.prod(shape[:-1]) if len(shape) > 1 else 1
    w2, g2, m2, v2 = (a.reshape(R, C) for a in (w, g, m, v))
    tr = R
    for cand in (1024, 512, 256, 128, 64, 32, 16, 8):
        if R % cand == 0 and cand * C * 4 <= (1 << 20):
            tr = cand
            break

    def body(w_ref, g_ref, m_ref, v_ref, d_ref, nm_ref, nv_ref):
        gv = g_ref[...]
        mn = ADAM_B1 * m_ref[...] + (1.0 - ADAM_B1) * gv
        vn = ADAM_B2 * v_ref[...] + (1.0 - ADAM_B2) * (gv * gv)
        nm_ref[...] = mn
        nv_ref[...] = vn
        m_hat = mn / _ADAM_C1
        v_hat = vn / _ADAM_C2
        d_ref[...] = -ADAM_LR * (m_hat / (jnp.sqrt(v_hat) + ADAM_EPS) + ADAM_WD * w_ref[...])

    spec = pl.BlockSpec((tr, C), lambda i: (i, 0))
    outs = pl.pallas_call(
        body, name=name, grid=(R // tr,),
        in_specs=[spec] * 4, out_specs=[spec] * 3,
        out_shape=[jax.ShapeDtypeStruct((R, C), F32)] * 3,
        compiler_params=pltpu.CompilerParams(dimension_semantics=("parallel",)),
    )(w2, g2, m2, v2)
    return tuple(o.reshape(shape) for o in outs)


def _row_tile(rows, cols, itemsize, budget):
    for cand in (1024, 512, 256, 128, 64, 32, 16):
        if rows % cand == 0 and cand * cols * itemsize <= budget:
            return cand
    return rows


def _sum_sel(sel, stacked, others, name, out_dtype):
    R, C = stacked.shape[1:]
    n_tot = 1 + sum(o.shape[0] for o in others)
    tr = _row_tile(R, C, 4 * n_tot, 8 << 20)

    def body(sel_ref, s_ref, *refs):
        o_ref = refs[-1]
        acc = s_ref[...].astype(F32)
        for r in refs[:-1]:
            for s in range(r.shape[0]):
                acc = acc + r[s].astype(F32)
        o_ref[...] = acc.astype(o_ref.dtype)

    return pl.pallas_call(
        body, name=name,
        grid_spec=pltpu.PrefetchScalarGridSpec(
            num_scalar_prefetch=1, grid=(R // tr,),
            in_specs=[pl.BlockSpec((None, tr, C), lambda i, s: (s[0], i, 0))] + [pl.BlockSpec((o.shape[0], tr, C), lambda i, s: (0, i, 0)) for o in others],
            out_specs=pl.BlockSpec((tr, C), lambda i, s: (i, 0))),
        out_shape=jax.ShapeDtypeStruct((R, C), out_dtype),
        compiler_params=pltpu.CompilerParams(dimension_semantics=("parallel",)),
    )(sel, stacked, *others)


def _adamw_piece(cidx, w2, m2, v2, mine, got, bufs, row0, name):
    hr, C = mine.shape
    tr = _row_tile(math.gcd(hr, row0) if row0 else hr, C, 4, 1 << 20)
    nt = hr // tr

    def body(c_ref, w_ref, m_ref, v_ref, a_ref, b_ref, _g, _d, _nm, _nv, g_ref, d_ref, nm_ref, nv_ref):
        gv = jnp.where(pl.program_id(0) == c_ref[0], a_ref[...], b_ref[...])
        mn = ADAM_B1 * m_ref[...] + (1.0 - ADAM_B1) * gv
        vn = ADAM_B2 * v_ref[...] + (1.0 - ADAM_B2) * (gv * gv)
        g_ref[...] = gv
        nm_ref[...] = mn
        nv_ref[...] = vn
        d_ref[...] = -ADAM_LR * ((mn / _ADAM_C1) / (jnp.sqrt(vn / _ADAM_C2) + ADAM_EPS) + ADAM_WD * w_ref[...])

    rows = pl.BlockSpec((tr, C), lambda hf, t, c: (row0 // tr + hf * nt + t, 0))
    mine_spec = pl.BlockSpec((tr, C), lambda hf, t, c: (jnp.where(hf == c[0], t, 0), 0))
    got_spec = pl.BlockSpec((tr, C), lambda hf, t, c: (jnp.where(hf == c[0], 0, t), 0))
    return pl.pallas_call(
        body, name=name,
        grid_spec=pltpu.PrefetchScalarGridSpec(num_scalar_prefetch=1, grid=(2, nt), in_specs=[rows] * 3 + [mine_spec, got_spec] + [_ANY_SPEC] * 4,
                                               out_specs=[rows] * 4),
        out_shape=[jax.ShapeDtypeStruct(w2.shape, F32)] * 4,
        input_output_aliases={6 + n: n for n in range(4)},
        compiler_params=pltpu.CompilerParams(dimension_semantics=("parallel", "parallel")),
    )(cidx, w2, m2, v2, mine, got, *bufs)


_VMEM_SPEC = pl.BlockSpec(memory_space=pltpu.VMEM)
_HBM_SPEC = pl.BlockSpec(memory_space=pltpu.HBM)


def _flip(v, bit):
    return (1 - v) if bit else v


def _allgather8(v, name):
    def body(v_ref, out_ref, send_sems, recv_sems, local_sem):
        x, y, c = _idx()
        me = 4 * x + 2 * y + c
        mine = pltpu.make_async_copy(v_ref, out_ref.at[me], local_sem)
        mine.start()
        sends = []
        for k in range(1, N_DEV):
            peer = (_flip(x, k & 4), _flip(y, k & 2), _flip(c, k & 1))
            cp = pltpu.make_async_remote_copy(src_ref=v_ref, dst_ref=out_ref.at[me], send_sem=send_sems.at[k - 1], recv_sem=recv_sems.at[k - 1],
                                              device_id=peer, device_id_type=MESH)
            cp.start()
            sends.append(cp)
        for k in range(1, N_DEV):
            px, py, pc = _flip(x, k & 4), _flip(y, k & 2), _flip(c, k & 1)
            src = 4 * px + 2 * py + pc
            pltpu.make_async_remote_copy(src_ref=v_ref, dst_ref=out_ref.at[src], send_sem=send_sems.at[k - 1], recv_sem=recv_sems.at[k - 1],
                                         device_id=(px, py, pc), device_id_type=MESH).wait_recv()
        for cp in sends:
            cp.wait_send()
        mine.wait()

    return pl.pallas_call(
        body, name=name,
        out_shape=jax.ShapeDtypeStruct((N_DEV, *v.shape), v.dtype),
        in_specs=[_VMEM_SPEC], out_specs=_VMEM_SPEC,
        scratch_shapes=[pltpu.SemaphoreType.DMA((N_DEV - 1,)), pltpu.SemaphoreType.DMA((N_DEV - 1,)), pltpu.SemaphoreType.DMA],
    )(v)


def _mod_exchange(modp, name):
    _, L, Nc = modp.shape

    def body(p_ref, out_ref, send_sems, recv_sems, local_sem):
        x, y, c = _idx()
        me, chip = 4 * x + 2 * y + c, 2 * x + y
        mine = pltpu.make_async_copy(p_ref.at[me], out_ref.at[chip], local_sem)
        mine.start()
        sends = []
        for k in range(1, N_CHIPS):
            px, py = _flip(x, k & 2), _flip(y, k & 1)
            cp = pltpu.make_async_remote_copy(src_ref=p_ref.at[4 * px + 2 * py + c], dst_ref=out_ref.at[chip],
                                              send_sem=send_sems.at[k - 1], recv_sem=recv_sems.at[k - 1], device_id=(px, py, c), device_id_type=MESH)
            cp.start()
            sends.append(cp)
        for k in range(1, N_CHIPS):
            px, py = _flip(x, k & 2), _flip(y, k & 1)
            pltpu.make_async_remote_copy(src_ref=p_ref.at[me], dst_ref=out_ref.at[2 * px + py], send_sem=send_sems.at[k - 1],
                                         recv_sem=recv_sems.at[k - 1], device_id=(px, py, c), device_id_type=MESH).wait_recv()
        for cp in sends:
            cp.wait_send()
        mine.wait()

    return pl.pallas_call(
        body, name=name,
        out_shape=jax.ShapeDtypeStruct((N_CHIPS, L, Nc), modp.dtype),
        in_specs=[_VMEM_SPEC], out_specs=_VMEM_SPEC,
        scratch_shapes=[pltpu.SemaphoreType.DMA((N_CHIPS - 1,)), pltpu.SemaphoreType.DMA((N_CHIPS - 1,)), pltpu.SemaphoreType.DMA],
    )(modp)


_SEM_SPEC = pl.BlockSpec(memory_space=pltpu.SEMAPHORE)
_ANY_SPEC = pl.BlockSpec(memory_space=pl.ANY)
_EFFECT = pltpu.SideEffectType.DATAFLOW_SIDE_EFFECTING


def _hbm(a):
    return pltpu.with_memory_space_constraint(a, pltpu.HBM)


def _xchip_copies(mode, srcs, lands, send_sems, recv_sems, waiting):
    x, y, c = _idx()
    chip = 2 * x + y
    out = []
    for a in range(len(srcs)):
        for k in range(1, _n_peers(mode) + 1):
            if mode == "all8":
                px, py, pc = _flip(x, k & 4), _flip(y, k & 2), _flip(c, k & 1)
                src, dst, mine = srcs[a], lands[a].at[4 * x + 2 * y + c], lands[a].at[4 * px + 2 * py + pc]
            elif mode == "scatter8":
                px, py, pc = _flip(x, k & 4), _flip(y, k & 2), _flip(c, k & 1)
                src, dst, mine = srcs[a].at[pc, 2 * px + py], lands[a].at[k - 1], lands[a].at[k - 1]
            else:
                px, py, pc = _flip(x, k & 2), _flip(y, k & 1), c
                peer = 2 * px + py
                if mode == "gather":
                    src, dst, mine = srcs[a].at[c], lands[a].at[chip, c], lands[a].at[peer, c]
                else:
                    src, dst, mine = srcs[a].at[peer], lands[a].at[k - 1], lands[a].at[k - 1]
            q = a * _n_peers(mode) + k - 1
            out.append(pltpu.make_async_remote_copy(src_ref=src, dst_ref=mine if waiting else dst, send_sem=send_sems[q], recv_sem=recv_sems[q],
                                                    device_id=(px, py, pc), device_id_type=MESH))
    return out


def _n_peers(mode):
    return N_DEV - 1 if mode in ("all8", "scatter8") else N_CHIPS - 1


def _xchip_start(mode, srcs, land_shapes, dep, name):
    n = len(srcs)
    ns = n * _n_peers(mode)

    def body(*refs):
        src_refs, land_refs = refs[:n], refs[n:2 * n]
        outs = refs[2 * n + 1:]
        for cp in _xchip_copies(mode, src_refs, land_refs, outs[:ns], outs[ns:2 * ns], waiting=False):
            cp.start()
        outs[-1][...] = jnp.zeros_like(outs[-1])

    lands = [_hbm(lax.empty(s.shape, s.dtype)) for s in land_shapes]
    outs = pl.pallas_call(
        body, name=name,
        out_shape=(*[pltpu.SemaphoreType.DMA(())] * (2 * ns), *[pltpu.HBM(s.shape, s.dtype) for s in srcs],
                   *[pltpu.HBM(s.shape, s.dtype) for s in land_shapes], jax.ShapeDtypeStruct((8, 128), F32)),
        in_specs=[_HBM_SPEC] * (2 * n) + [_ANY_SPEC],
        out_specs=(*[_SEM_SPEC] * (2 * ns), *[_HBM_SPEC] * (2 * n), _VMEM_SPEC),
        input_output_aliases={i: 2 * ns + i for i in range(2 * n)},
        compiler_params=pltpu.CompilerParams(has_side_effects=_EFFECT),
    )(*[_hbm(s) for s in srcs], *lands, dep)
    return list(outs[:ns]), list(outs[ns:2 * ns]), list(outs[2 * ns:2 * ns + n]), list(outs[2 * ns + n:2 * ns + 2 * n]), outs[-1]


def _xchip_wait(mode, send_sems, recv_sems, srcs, lands, after, name):
    n = len(srcs)
    ns = n * _n_peers(mode)

    def body(*refs):
        src_refs, land_refs = refs[:n], refs[n:2 * n]
        sems = refs[2 * n:2 * n + 2 * ns]
        for cp in _xchip_copies(mode, src_refs, land_refs, sems[:ns], sems[ns:], waiting=True):
            cp.wait_send()
            cp.wait_recv()

    outs = pl.pallas_call(
        body, name=name,
        out_shape=(*[pltpu.HBM(s.shape, s.dtype) for s in srcs], *[pltpu.HBM(s.shape, s.dtype) for s in lands]),
        in_specs=[_HBM_SPEC] * (2 * n) + [_SEM_SPEC] * (2 * ns) + [_ANY_SPEC] * len(after),
        out_specs=tuple([_HBM_SPEC] * (2 * n)),
        input_output_aliases={i: i for i in range(2 * n)},
        compiler_params=pltpu.CompilerParams(has_side_effects=_EFFECT),
    )(*srcs, *lands, *send_sems, *recv_sems, *after)
    return list(outs[:n]), list(outs[n:])


def _sibling_fwd(lands, name):
    n = len(lands)

    def body(*refs):
        outs = refs[n:2 * n]
        send_sems, recv_sems = refs[2 * n:]
        x, y, c = _idx()
        sib = (x, y, 1 - c)
        sends = []
        for a in range(n):
            for k in range(1, N_CHIPS):
                src = 2 * _flip(x, k & 2) + _flip(y, k & 1)
                cp = pltpu.make_async_remote_copy(src_ref=outs[a].at[src, c], dst_ref=outs[a].at[src, c], send_sem=send_sems.at[a, k - 1],
                                                  recv_sem=recv_sems.at[a, k - 1], device_id=sib, device_id_type=MESH)
                cp.start()
                sends.append(cp)
        for a in range(n):
            for k in range(1, N_CHIPS):
                src = 2 * _flip(x, k & 2) + _flip(y, k & 1)
                pltpu.make_async_remote_copy(src_ref=outs[a].at[src, c], dst_ref=outs[a].at[src, 1 - c], send_sem=send_sems.at[a, k - 1],
                                             recv_sem=recv_sems.at[a, k - 1], device_id=sib, device_id_type=MESH).wait_recv()
        for cp in sends:
            cp.wait_send()

    return pl.pallas_call(
        body, name=name,
        out_shape=[jax.ShapeDtypeStruct(s.shape, s.dtype) for s in lands],
        in_specs=[_HBM_SPEC] * n, out_specs=[_HBM_SPEC] * n,
        input_output_aliases={i: i for i in range(n)},
        scratch_shapes=[pltpu.SemaphoreType.DMA((n, N_CHIPS - 1)), pltpu.SemaphoreType.DMA((n, N_CHIPS - 1))],
    )(*lands)


def _sibling_send(halves, name):
    n = len(halves)

    def body(*refs):
        ins, outs = refs[:n], refs[n:2 * n]
        send_sems, recv_sems = refs[2 * n:]
        x, y, c = _idx()
        cps = []
        for a in range(n):
            cp = pltpu.make_async_remote_copy(src_ref=ins[a], dst_ref=outs[a], send_sem=send_sems.at[a], recv_sem=recv_sems.at[a],
                                              device_id=(x, y, 1 - c), device_id_type=MESH)
            cp.start()
            cps.append(cp)
        for cp in cps:
            cp.wait()

    return pl.pallas_call(
        body, name=name,
        out_shape=[jax.ShapeDtypeStruct(h.shape, h.dtype) for h in halves],
        in_specs=[_HBM_SPEC] * n, out_specs=[_HBM_SPEC] * n,
        scratch_shapes=[pltpu.SemaphoreType.DMA((n,)), pltpu.SemaphoreType.DMA((n,))],
    )(*halves)


def _col_full(g):
    k, n = g.shape[1], g.shape[2]
    return g.transpose(1, 0, 2).reshape(k, N_CHIPS * n)


def _col_blocks(w):
    k, n = w.shape
    return w.reshape(k, N_CHIPS, n // N_CHIPS).transpose(1, 0, 2)


def _row_blocks(w):
    k, n = w.shape
    return w.reshape(N_CHIPS, k // N_CHIPS, n)


_UQ_HEAD = MLA_NOPE + MLA_ROPE

_LAT = MLA_QL + MLA_KVL + MLA_ROPE
_POOL_R = len(POOL_WINDOWS) * (POOL_GD // N_CHIPS)

_PIECE_KINDS = {
    "mlp_w1": (D_MODEL, D_MODEL, lambda g: g, _col_blocks),
    "mlp_w2": (D_MODEL, D_MODEL, lambda g: g.reshape(4 * D_MODEL, D_MODEL), _row_blocks),
    "pool_w": (_POOL_R, POOL_GD,
               lambda g: g.reshape(N_CHIPS, len(POOL_WINDOWS), POOL_GD // N_CHIPS, POOL_GD).transpose(1, 0, 2, 3).reshape(len(POOL_WINDOWS), POOL_GD, POOL_GD),
               lambda w: w.reshape(len(POOL_WINDOWS), N_CHIPS, POOL_GD // N_CHIPS, POOL_GD).transpose(1, 0, 2, 3).reshape(N_CHIPS, _POOL_R, POOL_GD)),
    "sgu_w_in": (D_MODEL, 2 * SGU_W // N_CHIPS, _col_full, _col_blocks),
    "sgu_w_out": (SGU_W // N_CHIPS, D_MODEL, lambda g: g.reshape(SGU_W, D_MODEL), _row_blocks),
    "mla_w_dq_dkv": (D_MODEL // N_CHIPS, _LAT, lambda g: jnp.pad(g.reshape(D_MODEL, _LAT), ((0, 0), (0, MLA_LATP - _LAT))),
                     lambda w: _row_blocks(w[:, :_LAT])),
    "mla_w_uq": (MLA_QL, MLA_H * _UQ_HEAD // N_CHIPS,
                 lambda g: jnp.pad(_col_full(g).reshape(MLA_QL, MLA_H, _UQ_HEAD), ((0, 0), (0, 0), (0, MLA_HP - _UQ_HEAD))).reshape(MLA_QL, MLA_H * MLA_HP),
                 lambda w: _col_blocks(w.reshape(MLA_QL, MLA_H, MLA_HP)[:, :, :_UQ_HEAD].reshape(MLA_QL, MLA_H * _UQ_HEAD))),
    "mla_w_ukv": (MLA_KVL, MLA_H * (MLA_NOPE + MLA_V) // N_CHIPS, _col_full, _col_blocks),
    "mla_w_o": (MLA_H * MLA_V // N_CHIPS, D_MODEL, lambda g: g.reshape(MLA_H * MLA_V, D_MODEL), _row_blocks),
}
_MIXER_KINDS = (("pool_w",), ("sgu_w_in", "sgu_w_out"), ("mla_w_dq_dkv", "mla_w_uq", "mla_w_ukv", "mla_w_o"))


def _layer_pieces(i):
    return [(k, i // N_MIXERS) for k in _MIXER_KINDS[i % N_MIXERS]] + [("mlp_w1", i), ("mlp_w2", i)]


def _rope_tables(positions):
    inv_freq = ROPE_THETA ** (-jnp.arange(0, MLA_ROPE, 2, dtype=F32) / MLA_ROPE)
    ang = positions.astype(F32)[:, None] * inv_freq
    cos, sin = jnp.cos(ang), jnp.sin(ang)
    z32, z64 = jnp.zeros_like(cos), jnp.zeros((positions.shape[0], 64), F32)
    return (jnp.concatenate([cos, cos, z64], axis=1), jnp.concatenate([-sin, z32, z64], axis=1), jnp.concatenate([z32, sin, z64], axis=1))


def _local_step(x, positions, target, mod, S, weights_of, grads_of):
    D = D_MODEL
    cc, sa, sb = _rope_tables(positions)
    mods = [[mod[i:i + 1, n * D:(n + 1) * D] for n in range(6)] for i in range(DEPTH)]
    h_dtype = lambda i: F32 if i % N_MIXERS == 0 else BF16
    saved = []
    h = _norm_mod_fwd(x, S["norm_mix_g"][0:1], mods[0][1], mods[0][0], h_dtype(0), "l0_norm1")
    for i in range(DEPTH):
        sh1, sc1, g1, sh2, sc2, g2 = mods[i]
        kind, j = i % N_MIXERS, i // N_MIXERS
        gmlp = S["norm_mlp_g"][i:i + 1]
        W = weights_of(i, "mix", x)
        st = {"x": x}
        norm2 = ((gmlp, "n"), (sc2, "n"), (sh2, "n"))
        if kind == 0:
            x2, pooled, ypre, h2 = _pool_fwd(h, W["pool_w"], S["pool_scale"][j:j + 1], x, g1, gmlp, sc2, sh2, f"l{i}_pool")
            st.update(pooled=pooled, y=ypre)
        elif kind == 1:
            zz = _mm(h, W["sgu_w_in"], out_dtypes=(F32,), name=f"l{i}_sgu_in")
            bs_t = S["sgu_b_s"].T
            gated = _sgu_gate_fwd(zz, S["sgu_ln_g"], S["sgu_ln_b"], S["sgu_w_s"], bs_t, f"l{i}_sgu_gate")
            x2, y, h2 = _mm(gated, W["sgu_w_out"], epi=_epi_residual_norm, extras=((x, "mn"), (g1, "n"), *norm2), out_dtypes=(F32, BF16, BF16),
                            tn=D, name=f"l{i}_sgu_out")
            st.update(h=h, zz=zz, gated=gated, y=y, bs_t=bs_t)
        else:
            lat = _mm(h, W["mla_w_dq_dkv"], out_dtypes=(F32,), name=f"l{i}_mla_lat")
            cqn, ckvn, krot = _mla_lat_fwd(lat, S["mla_q_norm_g"], S["mla_kv_norm_g"], cc, sa, sb, f"l{i}_mla_latn")
            q = _mm(cqn, W["mla_w_uq"], epi=_epi_q_rope, extras=((cc, "m"), (sa, "m"), (sb, "m")), name=f"l{i}_mla_uq")
            k, kt, v, vt = _mla_ukv(ckvn, W["mla_w_ukv"], krot, f"l{i}_mla_ukv")
            o, lse = _attn_fwd(q, k, vt, f"l{i}_attn")
            x2, y, h2 = _mm(o, W["mla_w_o"], epi=_epi_residual_norm, extras=((x, "mn"), (g1, "n"), *norm2), out_dtypes=(F32, BF16, BF16),
                            tn=D, name=f"l{i}_mla_o")
            st.update(h=h, lat=lat, cqn=cqn, ckvn=ckvn, q=q, k=k, kt=kt, v=v, o=o, lse=lse, y=y)
        W = {**W, **weights_of(i, "mlp", x2)}
        z, r2 = _mm(h2, W["mlp_w1"], epi=_epi_sq_relu, out_dtypes=(BF16, BF16), epi_cols=MM_EPI_COLS, tm=MM_TM_WIDE, name=f"l{i}_mlp1")
        if i + 1 < DEPTH:
            norm1 = ((S["norm_mix_g"][i + 1:i + 2], "n"), (mods[i + 1][1], "n"), (mods[i + 1][0], "n"))
            x3, o2, h = _mm(z, W["mlp_w2"], epi=_epi_residual_norm, extras=((x2, "mn"), (g2, "n"), *norm1), out_dtypes=(F32, BF16, h_dtype(i + 1)),
                            tn=D, name=f"l{i}_mlp2")
        else:
            x3, o2 = _mm(z, W["mlp_w2"], epi=_epi_residual, extras=((x2, "mn"), (g2, "n")), out_dtypes=(F32, BF16), name=f"l{i}_mlp2")
        st.update(x2=x2, h2=h2, z=z, r2=r2, o2=o2, W=W)
        saved.append(st)
        x = x3

    loss, dx, dfinal_g = _loss_head(x, target, S["final_g"], "loss_head")

    gS = {"final_g": dfinal_g, "norm_mix_g": [None] * DEPTH, "norm_mlp_g": [None] * DEPTH, "pool_scale": [None] * 2}
    dmod = [None] * DEPTH
    do2, dg2 = _resid_bwd(dx, saved[-1]["o2"], mods[-1][5], f"l{DEPTH - 1}_b_res2")
    started = None
    for i in reversed(range(DEPTH)):
        st = saved[i]
        W, gW = st["W"], {}
        sh1, sc1, g1, sh2, sc2, g2 = mods[i]
        kind, j = i % N_MIXERS, i // N_MIXERS
        gmix, gmlp = S["norm_mix_g"][i:i + 1], S["norm_mlp_g"][i:i + 1]
        da = _mm(do2, W["mlp_w2"], tb=True, epi=lambda acc, rt: (acc * rt.astype(F32),), extras=((st["r2"], "mn"),), after=started, epi_cols=MM_EPI_COLS,
                 tm=MM_TM_WIDE, name=f"l{i}_b_dz")
        gW["mlp_w2"] = _mm(st["z"], do2, ta=True, chip_blocks="row", name=f"l{i}_b_dw2")
        dh2 = _mm(da, W["mlp_w1"], tb=True, name=f"l{i}_b_dh2")
        gW["mlp_w1"] = _mm(st["h2"], da, ta=True, chip_blocks="col", name=f"l{i}_b_dw1")
        dx2, dgmlp, dsc2, dsh2, dy, q1 = _norm_mod_bwd(st["x2"], dh2, dx, gmlp, sc2, f"l{i}_b_norm2", res=(st["y"], g1))
        gS["norm_mlp_g"][i] = dgmlp
        if kind == 0:
            dh, dpw, dpsc, dg1 = _pool_bwd(dy, st["pooled"], W["pool_w"], S["pool_scale"][j:j + 1], g1, q1, f"l{i}_b_pool")
            gW["pool_w"] = dpw.astype(BF16)
            gS["pool_scale"][j] = dpsc
        elif kind == 1:
            dg1 = q1
            dgated = _mm(dy, W["sgu_w_out"], tb=True, name=f"l{i}_b_dgated")
            gW["sgu_w_out"] = _mm(st["gated"], dy, ta=True, name=f"l{i}_b_dwout")
            dzz, dws, dbs, dlg, dlb = _sgu_gate_bwd(st["zz"], dgated, S["sgu_ln_g"], S["sgu_ln_b"], S["sgu_w_s"], st["bs_t"], f"l{i}_b_sgu_gate")
            gS.update(sgu_w_s=dws, sgu_b_s=dbs[:, :, 0], sgu_ln_g=dlg, sgu_ln_b=dlb)
            dh = _mm(dzz, W["sgu_w_in"], tb=True, name=f"l{i}_b_dh_sgu")
            gW["sgu_w_in"] = _mm(st["h"], dzz, ta=True, name=f"l{i}_b_dwin")
        else:
            dg1 = q1
            do = _mm(dy, W["mla_w_o"], tb=True, name=f"l{i}_b_do")
            gW["mla_w_o"] = _mm(st["o"], dy, ta=True, name=f"l{i}_b_dwo")
            delta = _attn_delta(do, st["o"], f"l{i}_b_delta")
            dqt, dkv, dkr = _attn_bwd(st["q"], st["k"], st["kt"], st["v"], do, st["lse"], delta, f"l{i}_b_attn")
            dqpad, dkrot = _mla_prep_bwd(dqt, dkr, cc, sa, sb, f"l{i}_b_mla_prep")
            dcqn = _mm(dqpad, W["mla_w_uq"], tb=True, out_dtypes=(F32,), name=f"l{i}_b_dcq")
            gW["mla_w_uq"] = _mm(st["cqn"], dqpad, ta=True, name=f"l{i}_b_dwuq")
            dckvn = _mm(dkv, W["mla_w_ukv"], tb=True, out_dtypes=(F32,), name=f"l{i}_b_dckv")
            gW["mla_w_ukv"] = _mm(st["ckvn"], dkv, ta=True, name=f"l{i}_b_dwukv")
            dlat, dqg, dkvg = _mla_lat_bwd(st["lat"], dcqn, dckvn, dkrot, S["mla_q_norm_g"], S["mla_kv_norm_g"], cc, sa, sb, f"l{i}_b_mla_latn")
            gS.update(mla_q_norm_g=dqg, mla_kv_norm_g=dkvg)
            dh = _mm(dlat, W["mla_w_dq_dkv"], tb=True, name=f"l{i}_b_dh_mla")
            gW["mla_w_dq_dkv"] = _mm(st["h"], dlat, ta=True, name=f"l{i}_b_dwdq")
        if i > 0:
            dx, dgmix, dsc1, dsh1, do2_prev, dg2_prev = _norm_mod_bwd(st["x"], dh, dx2, gmix, sc1, f"l{i}_b_norm1", res=(saved[i - 1]["o2"], mods[i - 1][5]))
        else:
            dx, dgmix, dsc1, dsh1 = _norm_mod_bwd(st["x"], dh, dx2, gmix, sc1, f"l{i}_b_norm1")
        gS["norm_mix_g"][i] = dgmix
        dmod[i] = jnp.concatenate([dsh1, dsc1, dg1, dsh2, dsc2, dg2], axis=1)
        started = grads_of(i, gW, dx)
        if i > 0:
            do2, dg2 = do2_prev, dg2_prev

    for n in ("norm_mix_g", "norm_mlp_g", "pool_scale"):
        gS[n] = jnp.concatenate(gS[n], axis=0)
    return loss, dx, gS, jnp.concatenate(dmod, axis=0)


_SMALL = {
    "norm_mix_g": (DEPTH, D_MODEL), "norm_mlp_g": (DEPTH, D_MODEL), "sgu_ln_g": (1, SGU_W), "sgu_ln_b": (1, SGU_W),
    "sgu_w_s": (SGU_H, SGU_CHUNK, SGU_CHUNK), "sgu_b_s": (SGU_H, SGU_CHUNK), "mla_kv_norm_g": (1, MLA_KVL), "final_g": (1, D_MODEL),
    "pool_scale": (2, D_MODEL), "mla_q_norm_g": (1, MLA_QL), "loss": (1, 128), "dmod": (DEPTH, 6 * D_MODEL),
}
_PACK_W = 1024


def _pack(vals):
    flat = jnp.concatenate([v.reshape(-1) for v in vals])
    rows = -(-flat.shape[0] // (8 * _PACK_W)) * 8
    return jnp.pad(flat, (0, rows * _PACK_W - flat.shape[0])).reshape(rows, _PACK_W)


def _unpack(buf, shapes):
    flat, out, off = buf.reshape(-1), [], 0
    for s in shapes:
        n = math.prod(s)
        out.append(flat[off:off + n].reshape(s))
        off += n
    return out


def kernel(x, c, positions, ada_w, ada_b, norm_mix_g, norm_mlp_g, pool_w, pool_scale, sgu_w_in, sgu_ln_g, sgu_ln_b, sgu_w_s, sgu_b_s, sgu_w_out, mla_w_dq_dkv, mla_q_norm_g, mla_kv_norm_g, mla_w_uq, mla_w_ukv, mla_w_o, mlp_w1, mlp_w2, final_g, loss_target, m_ada_w, m_ada_b, m_norm_mix_g, m_norm_mlp_g, m_pool_w, m_pool_scale, m_sgu_w_in, m_sgu_ln_g, m_sgu_ln_b, m_sgu_w_s, m_sgu_b_s, m_sgu_w_out, m_mla_w_dq_dkv, m_mla_q_norm_g, m_mla_kv_norm_g, m_mla_w_uq, m_mla_w_ukv, m_mla_w_o, m_mlp_w1, m_mlp_w2, m_final_g, v_ada_w, v_ada_b, v_norm_mix_g, v_norm_mlp_g, v_pool_w, v_pool_scale, v_sgu_w_in, v_sgu_ln_g, v_sgu_ln_b, v_sgu_w_s, v_sgu_b_s, v_sgu_w_out, v_mla_w_dq_dkv, v_mla_q_norm_g, v_mla_kv_norm_g, v_mla_w_uq, v_mla_w_ukv, v_mla_w_o, v_mlp_w1, v_mlp_w2, v_final_g):
    P = dict(ada_w=ada_w, ada_b=ada_b, norm_mix_g=norm_mix_g, norm_mlp_g=norm_mlp_g, pool_w=pool_w, pool_scale=pool_scale, sgu_w_in=sgu_w_in,
             sgu_ln_g=sgu_ln_g, sgu_ln_b=sgu_ln_b, sgu_w_s=sgu_w_s, sgu_b_s=sgu_b_s, sgu_w_out=sgu_w_out, mla_w_dq_dkv=mla_w_dq_dkv,
             mla_q_norm_g=mla_q_norm_g, mla_kv_norm_g=mla_kv_norm_g, mla_w_uq=mla_w_uq, mla_w_ukv=mla_w_ukv, mla_w_o=mla_w_o, mlp_w1=mlp_w1,
             mlp_w2=mlp_w2, final_g=final_g)
    M = dict(ada_w=m_ada_w, ada_b=m_ada_b, norm_mix_g=m_norm_mix_g, norm_mlp_g=m_norm_mlp_g, pool_w=m_pool_w, pool_scale=m_pool_scale,
             sgu_w_in=m_sgu_w_in, sgu_ln_g=m_sgu_ln_g, sgu_ln_b=m_sgu_ln_b, sgu_w_s=m_sgu_w_s, sgu_b_s=m_sgu_b_s, sgu_w_out=m_sgu_w_out,
             mla_w_dq_dkv=m_mla_w_dq_dkv, mla_q_norm_g=m_mla_q_norm_g, mla_kv_norm_g=m_mla_kv_norm_g, mla_w_uq=m_mla_w_uq, mla_w_ukv=m_mla_w_ukv,
             mla_w_o=m_mla_w_o, mlp_w1=m_mlp_w1, mlp_w2=m_mlp_w2, final_g=m_final_g)
    V = dict(ada_w=v_ada_w, ada_b=v_ada_b, norm_mix_g=v_norm_mix_g, norm_mlp_g=v_norm_mlp_g, pool_w=v_pool_w, pool_scale=v_pool_scale,
             sgu_w_in=v_sgu_w_in, sgu_ln_g=v_sgu_ln_g, sgu_ln_b=v_sgu_ln_b, sgu_w_s=v_sgu_w_s, sgu_b_s=v_sgu_b_s, sgu_w_out=v_sgu_w_out,
             mla_w_dq_dkv=v_mla_w_dq_dkv, mla_q_norm_g=v_mla_q_norm_g, mla_kv_norm_g=v_mla_kv_norm_g, mla_w_uq=v_mla_w_uq, mla_w_ukv=v_mla_w_ukv,
             mla_w_o=v_mla_w_o, mlp_w1=v_mlp_w1, mlp_w2=v_mlp_w2, final_g=v_final_g)
    order = list(P)
    xi, yi, ci = _idx()
    chip = 2 * xi + yi
    D = D_MODEL
    n_ada = ada_w.shape[2]

    pre = _allgather8(_pack([c, pool_scale, mla_q_norm_g]), "ag_small")
    flat = pre.reshape(N_DEV, -1)
    c_all = flat[:, :D]
    ps_all = flat[0::2, D:D + 2 * (D // N_CHIPS)].reshape(N_CHIPS, 2, D // N_CHIPS).transpose(1, 0, 2).reshape(2, D)
    q0 = D + 2 * (D // N_CHIPS)
    qg_all = flat[0::2, q0:q0 + MLA_QL // N_CHIPS].reshape(1, MLA_QL)

    ada_b_loc = lax.dynamic_slice_in_dim(ada_b, chip * n_ada, n_ada, axis=1)[:, None, :]
    modp = _ada_fwd(c_all, ada_w, ada_b_loc, "ada_fwd")
    mod = _mod_exchange(modp.transpose(1, 0, 2), "mod_exchange").transpose(1, 0, 2).reshape(DEPTH, 6 * D)

    S = dict(norm_mix_g=norm_mix_g, norm_mlp_g=norm_mlp_g, pool_scale=ps_all, sgu_ln_g=sgu_ln_g, sgu_ln_b=sgu_ln_b, sgu_w_s=sgu_w_s[0],
             sgu_b_s=sgu_b_s[0], mla_q_norm_g=qg_all, mla_kv_norm_g=mla_kv_norm_g, final_g=final_g[None, :])
    cidx, ownidx = jnp.reshape(ci, (1,)).astype(jnp.int32), jnp.reshape(N_CHIPS * ci + chip, (1,)).astype(jnp.int32)
    view2d = lambda a: a.reshape(-1, a.shape[-1])

    def piece_rows(kind, blk):
        r = _PIECE_KINDS[kind][0]
        return blk * r, r

    groups = [_layer_pieces(0)[:-2], _layer_pieces(0)[-2:], _layer_pieces(1)[:-2], _layer_pieces(1)[-2:], _layer_pieces(2), _layer_pieces(3)]
    start_after = {1: (2, 3), 2: (4,), 4: (5,)}
    gathers = {}

    def gather_start(g, dep):
        srcs, shapes = [], []
        for kind, blk in groups[g]:
            r0, r = piece_rows(kind, blk)
            cdim = _PIECE_KINDS[kind][1]
            srcs.append(view2d(P[kind])[r0:r0 + r].astype(BF16).reshape(2, r // 2, cdim))
            shapes.append(jax.ShapeDtypeStruct((N_CHIPS, 2, r // 2, cdim), BF16))
        gathers[g] = _xchip_start("gather", srcs, shapes, dep, f"ag_start_g{g}")

    def gather_finish(g, after):
        ssem, rsem, srcs, lands, _ = gathers.pop(g)
        deps = [after]
        for nxt in start_after.get(g, ()):
            gather_start(nxt, deps[-1])
            deps.append(gathers[nxt][-1])
        srcs, lands = _xchip_wait("gather", ssem, rsem, srcs, lands, deps, f"ag_wait_g{g}")
        lands = _sibling_fwd(lands, f"ag_sibling_g{g}")
        W = {}
        for (kind, _), s, land in zip(groups[g], srcs, lands, strict=True):
            r, cdim, to_full, _ = _PIECE_KINDS[kind]
            W[kind] = to_full(lax.dynamic_update_index_in_dim(land, s, chip, 0).reshape(N_CHIPS, r, cdim))
        return W

    def weights_of(i, part, x_i):
        if i < 2:
            return gather_finish(2 * i + (part == "mlp"), x_i)
        return gather_finish(i + 2, x_i) if part == "mix" else {}

    scatters = {}
    bufs = {n: tuple(lax.empty(view2d(P[n]).shape, F32) for _ in range(4)) for n in _PIECE_KINDS}

    def scatter_start(i, gW, dep):
        pcs = _layer_pieces(i)
        blocked = []
        for kind, _ in pcs:
            r, cdim, _, to_blocks = _PIECE_KINDS[kind]
            g = gW[kind]
            blocked.append(g if g.ndim == 4 else to_blocks(g).reshape(N_CHIPS, 2, r // 2, cdim).transpose(1, 0, 2, 3))
        shapes = [jax.ShapeDtypeStruct((N_DEV - 1, *b.shape[2:]), BF16) for b in blocked]
        scatters[i] = (pcs, *_xchip_start("scatter8", blocked, shapes, dep, f"rs_start_l{i}"))
        return scatters[i][-1]

    def scatter_finish(i, after):
        pcs, ssem, rsem, blocked, lands, _ = scatters.pop(i)
        blocked, lands = _xchip_wait("scatter8", ssem, rsem, blocked, lands, after, f"rs_wait_l{i}")
        halves = [_sum_sel(ownidx, b.reshape(2 * N_CHIPS, *b.shape[2:]), [l], f"rs_sum_l{i}_{kind}", F32)
                  for (kind, _), b, l in zip(pcs, blocked, lands, strict=True)]
        got = _sibling_send(halves, f"rs_merge_l{i}")
        for (kind, blk), mine, other in zip(pcs, halves, got, strict=True):
            r0, _ = piece_rows(kind, blk)
            bufs[kind] = tuple(_adamw_piece(cidx, view2d(P[kind]), view2d(M[kind]), view2d(V[kind]), mine, other, bufs[kind], r0,
                                            f"adamw_l{i}_{kind}"))
        return lands[0]

    first_layer = {}

    def grads_of(i, gW, dx_i):
        if i == 0:
            first_layer.update(gW)
            return None
        dep = scatter_finish(i + 1, [dx_i]) if i + 1 in scatters else dx_i
        return scatter_start(i, gW, dep)

    gather_start(0, mod)
    gather_start(1, gathers[0][-1])
    mod = mod + gathers[1][-1][0, 0]
    loss_l, dx, gS, dmod = _local_step(x[0], positions[0], loss_target[0], mod, S, weights_of, grads_of)

    gS["dmod"] = dmod
    gS["loss"] = loss_l
    packed = _pack([gS[n] for n in _SMALL])
    sg = _xchip_start("all8", [packed], [jax.ShapeDtypeStruct((N_DEV, *packed.shape), F32)], dx, "sg_start")
    tok0 = scatter_start(0, first_layer, sg[-1])[0, 0]
    scatter_finish(1, [dx, scatters[0][-1]])
    sg_src, sg_land = _xchip_wait("all8", sg[0], sg[1], sg[2], sg[3], [bufs[n][0] for n in ("mlp_w1", "mlp_w2", "sgu_w_in", "sgu_w_out")], "sg_wait")
    small = lax.dynamic_update_index_in_dim(sg_land[0], sg_src[0], 4 * xi + 2 * yi + ci, 0) + tok0
    small_sum = _unpack(_sum_lead([small], "sum_small_grads"), list(_SMALL.values()))
    G = dict(zip(_SMALL, small_sum, strict=True))
    grads = {
        "ada_b": G["dmod"], "norm_mix_g": G["norm_mix_g"], "norm_mlp_g": G["norm_mlp_g"], "sgu_ln_g": G["sgu_ln_g"], "sgu_ln_b": G["sgu_ln_b"],
        "sgu_w_s": G["sgu_w_s"][None], "sgu_b_s": G["sgu_b_s"][None], "mla_kv_norm_g": G["mla_kv_norm_g"], "final_g": G["final_g"][0],
        "pool_scale": lax.dynamic_slice_in_dim(G["pool_scale"], chip * (D // N_CHIPS), D // N_CHIPS, axis=1),
        "mla_q_norm_g": lax.dynamic_slice_in_dim(G["mla_q_norm_g"], chip * (MLA_QL // N_CHIPS), MLA_QL // N_CHIPS, axis=1),
    }
    dmod_all = _unpack(small, [(N_DEV,) + (small.shape[1] * _PACK_W,)])[0]
    off = sum(math.prod(s) for n, s in _SMALL.items() if n != "dmod")
    dmod_all = dmod_all[:, off:off + DEPTH * 6 * D].reshape(N_DEV, DEPTH, 6 * D)
    dmod_loc = lax.dynamic_slice_in_dim(dmod_all, chip * n_ada, n_ada, axis=2).transpose(1, 0, 2)
    deltas, new_m, new_v = {}, {}, {}
    grads["ada_w"], deltas["ada_w"], new_m["ada_w"], new_v["ada_w"] = _ada_bwd_adamw(c_all.T, dmod_loc, ada_w, m_ada_w, v_ada_w, "adamw_ada_w")
    for n in order:
        if n not in _PIECE_KINDS and n != "ada_w":
            deltas[n], new_m[n], new_v[n] = _adamw(P[n], grads[n].reshape(P[n].shape), M[n], V[n], f"adamw_{n}")
    scatter_finish(0, [deltas["ada_w"], deltas["sgu_w_s"]] + [bufs[n][0] for n in ("mlp_w1", "mlp_w2", "sgu_w_in", "mla_w_o")])
    for n in _PIECE_KINDS:
        grads[n], deltas[n], new_m[n], new_v[n] = (b.reshape(P[n].shape) for b in bufs[n])
    return (G["loss"][0, 0], dx[None], *[grads[n].reshape(P[n].shape) for n in order], *[deltas[n] for n in order], *[new_m[n] for n in order],
            *[new_v[n] for n in order])
```

```python
import math

import jax
import jax.numpy as jnp
from jax import lax
from jax.experimental import pallas as pl
from jax.experimental.pallas import tpu as pltpu

F32, BF16 = jnp.float32, jnp.bfloat16
MESH = pl.DeviceIdType.MESH

D_MODEL = 1024
DEPTH = 4
N_MIXERS = 3
POOL_WINDOWS = (2, 4, 8, 16)
POOL_GD = D_MODEL // len(POOL_WINDOWS)
POOL_HALO = 16
SGU_CHUNK = 128
SGU_W = D_MODEL
SGU_HD = 128
SGU_H = SGU_W // SGU_HD
MLA_H = 16
MLA_QL = 256
MLA_KVL = 128
MLA_NOPE = 128
MLA_ROPE = 64
MLA_V = 128
MLA_HP = 256
MLA_LATP = 512
ROPE_THETA = 10000.0
RMS_EPS = 1e-6
LN_EPS = 1e-5
SM_SCALE = (MLA_NOPE + MLA_ROPE) ** -0.5
NEG_INF = -1e30
ADAM_LR, ADAM_B1, ADAM_B2, ADAM_EPS, ADAM_WD, ADAM_STEP = 0.001, 0.9, 0.999, 1e-08, 0.01, 10
N_CHIPS = 4
N_DEV = 8
ROW_TILE = 512
ATT_TILE = 512
ATT_SUB = 256
ATT_FWD_HEADS = 4
ATT_BWD_HEADS = 2
MM_EPI_COLS = 256
MM_TM_WIDE = 2048
MM_VMEM_BUDGET = 40 << 20


def _idx():
    return lax.axis_index("x"), lax.axis_index("y"), lax.axis_index("c")


def _mm(a, b, *, name, ta=False, tb=False, epi=None, extras=(), out_dtypes=(BF16,), tm=1024, tn=1024, tk=1024, chip_blocks=None, after=None,
        epi_cols=None):
    if ta:
        K, M = a.shape
    else:
        M, K = a.shape
    b_chips = b.ndim == 3
    if b_chips:
        assert b.shape[0] == N_CHIPS
        Kb, N = (N_CHIPS * b.shape[2], b.shape[1]) if tb else (b.shape[1], N_CHIPS * b.shape[2])
    elif tb:
        N, Kb = b.shape
    else:
        Kb, N = b.shape
    assert K == Kb, (a.shape, b.shape, ta, tb)
    if b_chips and not tb:
        tn = min(tn, N // N_CHIPS)
    if chip_blocks == "col":
        tm, tn = min(tm, M // 2), min(tn, N // N_CHIPS)
    elif chip_blocks == "row":
        tm = min(tm, M // N_CHIPS // 2)
    tm, tn, tk = min(tm, M), min(tn, N), min(tk, K)

    def vmem_bytes(tm_, tk_):
        per_mn = sum(arr.dtype.itemsize for arr, kind in extras if kind == "mn") + sum(jnp.dtype(dt).itemsize for dt in out_dtypes)
        return 2 * (tm_ * tk_ * a.dtype.itemsize + tk_ * tn * b.dtype.itemsize + tm_ * tn * per_mn)

    if vmem_bytes(tm, K) <= MM_VMEM_BUDGET:
        tk = K
    elif tm >= 512 and vmem_bytes(tm // 2, K) <= MM_VMEM_BUDGET:
        tm, tk = tm // 2, K
    assert M % tm == 0 and N % tn == 0 and K % tk == 0, (M, N, K, tm, tn, tk)
    nk = K // tk
    assert epi_cols is None or (nk == 1 and not ta and not (b_chips and tb) and tn % epi_cols == 0)
    a_spec = pl.BlockSpec((tk, tm), lambda i, j, k: (k, i)) if ta else pl.BlockSpec((tm, tk), lambda i, j, k: (i, k))
    b_spec = pl.BlockSpec((tn, tk), lambda i, j, k: (j, k)) if tb else pl.BlockSpec((tk, tn), lambda i, j, k: (k, j))
    if b_chips and tb:
        assert nk == 1 and not ta
        b_spec = pl.BlockSpec((N_CHIPS, tn, K // N_CHIPS), lambda i, j, k: (0, j, 0))
    elif b_chips:
        per = N // N_CHIPS // tn
        b_spec = pl.BlockSpec((None, tk, tn), lambda i, j, k: (j // per, k, j % per))
    ex_specs = []
    for arr, kind in extras:
        if kind == "mn":
            ex_specs.append(pl.BlockSpec((tm, tn), lambda i, j, k: (i, j)))
        elif kind == "n":
            ex_specs.append(pl.BlockSpec((1, tn), lambda i, j, k: (0, j)))
        else:
            ex_specs.append(pl.BlockSpec((tm, arr.shape[1]), lambda i, j, k: (i, 0)))
    n_ex, n_out = len(extras), len(out_dtypes)
    n_in = 2 + n_ex + (after is not None)
    dims = (((0 if ta else 1,), (1 if tb else 0,)), ((), ()))

    def body(*refs):
        a_ref, b_ref = refs[0], refs[1]
        ex_refs = refs[2:2 + n_ex]
        out_refs = refs[n_in:n_in + n_out]
        if b_chips and tb:
            kc = K // N_CHIPS
            part = None
            for cb in range(N_CHIPS):
                p = lax.dot_general(a_ref[:, cb * kc:(cb + 1) * kc].astype(BF16), b_ref[cb].astype(BF16), dims, preferred_element_type=F32)
                part = p if part is None else part + p
        elif epi_cols is not None:
            av = a_ref[...].astype(BF16)
            chunk = lambda cc: lax.dot_general(av, (b_ref[cc * epi_cols:(cc + 1) * epi_cols, :] if tb else b_ref[:, cc * epi_cols:(cc + 1) * epi_cols])
                                               .astype(BF16), dims, preferred_element_type=F32)
            acc = chunk(0)
            for cc in range(tn // epi_cols):
                nxt = chunk(cc + 1) if cc + 1 < tn // epi_cols else None
                cs = slice(cc * epi_cols, (cc + 1) * epi_cols)
                for r, o in zip(out_refs, epi(acc, *[r[:, cs] for r in ex_refs]), strict=True):
                    r[:, cs] = o.astype(r.dtype)
                acc = nxt
            return
        else:
            part = lax.dot_general(a_ref[...].astype(BF16), b_ref[...].astype(BF16), dims, preferred_element_type=F32)

        def finish(acc):
            outs = epi(acc, *[r[...] for r in ex_refs]) if epi is not None else (acc,)
            for r, o in zip(out_refs, outs, strict=True):
                r[...] = o.astype(r.dtype)

        if nk == 1:
            finish(part)
        else:
            acc_ref = refs[-1]
            k = pl.program_id(2)

            @pl.when(k == 0)
            def _():
                acc_ref[...] = part

            @pl.when(k > 0)
            def _():
                acc_ref[...] += part

            @pl.when(k == nk - 1)
            def _():
                finish(acc_ref[...])

    out_specs = [pl.BlockSpec((tm, tn), lambda i, j, k: (i, j)) for _ in range(n_out)]
    out_shape = [jax.ShapeDtypeStruct((M, N), dt) for dt in out_dtypes]
    if chip_blocks is not None:
        assert n_out == 1
        if chip_blocks == "col":
            rh, cb = M // 2 // tm, N // N_CHIPS // tn
            out_specs = [pl.BlockSpec((None, None, tm, tn), lambda i, j, k: (i // rh, j // cb, i % rh, j % cb))]
            out_shape = [jax.ShapeDtypeStruct((2, N_CHIPS, M // 2, N // N_CHIPS), out_dtypes[0])]
        else:
            rh = M // N_CHIPS // 2 // tm
            out_specs = [pl.BlockSpec((None, None, tm, tn), lambda i, j, k: ((i // rh) % 2, i // (2 * rh), i % rh, j))]
            out_shape = [jax.ShapeDtypeStruct((2, N_CHIPS, M // N_CHIPS // 2, N), out_dtypes[0])]
    outs = pl.pallas_call(
        body,
        name=name,
        grid=(M // tm, N // tn, nk),
        in_specs=[a_spec, b_spec, *ex_specs] + ([pl.BlockSpec(memory_space=pl.ANY)] if after is not None else []),
        out_specs=out_specs,
        out_shape=out_shape,
        scratch_shapes=[pltpu.VMEM((tm, tn), F32)] if nk > 1 else [],
        compiler_params=pltpu.CompilerParams(dimension_semantics=("parallel", "parallel", "arbitrary")),
    )(a, b, *[arr for arr, _ in extras], *([after] if after is not None else []))
    return outs[0] if n_out == 1 else tuple(outs)


def _epi_sq_relu(acc):
    r = jnp.maximum(acc, 0.0)
    return r * r, 2.0 * r


def _epi_residual(acc, x, g):
    return x + g * acc, acc


def _rms_mod(xv, gain, sc, sh):
    r = lax.rsqrt(jnp.mean(xv * xv, axis=-1, keepdims=True) + RMS_EPS)
    return ((xv * r) * gain) * (1.0 + sc) + sh


def _epi_residual_norm(acc, x, g, gain, sc, sh):
    xn = x + g * acc
    return xn, acc, _rms_mod(xn, gain, sc, sh)


def _row_spec(tr, d):
    return pl.BlockSpec((tr, d), lambda i: (i, 0))


def _vec_spec(d):
    return pl.BlockSpec((1, d), lambda i: (0, 0))


def _colsum(v):
    return jnp.sum(v, axis=0, keepdims=True)


def _norm_mod_fwd(x, gain, sc, sh, out_dtype, name):
    T, D = x.shape
    tr = min(T, ROW_TILE)

    def body(x_ref, g_ref, sc_ref, sh_ref, o_ref):
        o_ref[...] = _rms_mod(x_ref[...], g_ref[...], sc_ref[...], sh_ref[...]).astype(o_ref.dtype)

    return pl.pallas_call(
        body, name=name, grid=(T // tr,),
        in_specs=[_row_spec(tr, D), _vec_spec(D), _vec_spec(D), _vec_spec(D)],
        out_specs=_row_spec(tr, D),
        out_shape=jax.ShapeDtypeStruct((T, D), out_dtype),
        compiler_params=pltpu.CompilerParams(dimension_semantics=("parallel",)),
    )(x, gain, sc, sh)


def _norm_mod_bwd(x, dh, dres, gain, sc, name, res=None):
    T, D = x.shape
    tr = min(T, ROW_TILE)

    def body(x_ref, dh_ref, dres_ref, g_ref, sc_ref, *refs):
        dx_ref, dg_ref, dsc_ref, dsh_ref = refs[-6:-2] if res is not None else refs

        @pl.when(pl.program_id(0) == 0)
        def _():
            dg_ref[...] = jnp.zeros_like(dg_ref)
            dsc_ref[...] = jnp.zeros_like(dsc_ref)
            dsh_ref[...] = jnp.zeros_like(dsh_ref)
            if res is not None:
                refs[-1][...] = jnp.zeros_like(refs[-1])

        xv = x_ref[...]
        r = lax.rsqrt(jnp.mean(xv * xv, axis=-1, keepdims=True) + RMS_EPS)
        xn = xv * r
        dhv = dh_ref[...].astype(F32)
        dsh_ref[...] += _colsum(dhv)
        dsc_ref[...] += _colsum(dhv * (xn * g_ref[...]))
        dt = dhv * (1.0 + sc_ref[...])
        dg_ref[...] += _colsum(dt * xn)
        dxn = dt * g_ref[...]
        dxv = dres_ref[...] + r * (dxn - xn * jnp.mean(dxn * xn, axis=-1, keepdims=True))
        dx_ref[...] = dxv
        if res is not None:
            y_ref, gr_ref, dy_ref, q_ref = refs[0], refs[1], refs[-2], refs[-1]
            dy_ref[...] = (gr_ref[...] * dxv).astype(BF16)
            q_ref[...] += _colsum(dxv * y_ref[...].astype(F32))

    extra_in, extra_spec = ([], []) if res is None else (list(res), [_row_spec(tr, D), _vec_spec(D)])
    return pl.pallas_call(
        body, name=name, grid=(T // tr,),
        in_specs=[_row_spec(tr, D), _row_spec(tr, D), _row_spec(tr, D), _vec_spec(D), _vec_spec(D), *extra_spec],
        out_specs=[_row_spec(tr, D), _vec_spec(D), _vec_spec(D), _vec_spec(D)] + ([_row_spec(tr, D), _vec_spec(D)] if res is not None else []),
        out_shape=[jax.ShapeDtypeStruct((T, D), F32)] + [jax.ShapeDtypeStruct((1, D), F32)] * 3
        + ([jax.ShapeDtypeStruct((T, D), BF16), jax.ShapeDtypeStruct((1, D), F32)] if res is not None else []),
        compiler_params=pltpu.CompilerParams(dimension_semantics=("arbitrary",)),
    )(x, dh, dres, gain, sc, *extra_in)


def _loss_head(x, target, gain, y, g, name):
    T, D = x.shape
    tr = min(T, ROW_TILE)

    def body(x_ref, t_ref, g_ref, y_ref, gr_ref, loss_ref, dx_ref, dg_ref, dy_ref, q_ref):
        @pl.when(pl.program_id(0) == 0)
        def _():
            loss_ref[...] = jnp.zeros_like(loss_ref)
            dg_ref[...] = jnp.zeros_like(dg_ref)
            q_ref[...] = jnp.zeros_like(q_ref)

        xv = x_ref[...]
        r = lax.rsqrt(jnp.mean(xv * xv, axis=-1, keepdims=True) + RMS_EPS)
        xn = xv * r
        err = xn * g_ref[...] - t_ref[...]
        row = jnp.mean(err * err, axis=-1, keepdims=True)
        loss_ref[...] += 0.5 * jnp.sum(row, axis=0, keepdims=True)
        dy = err * (1.0 / D)
        dg_ref[...] += _colsum(dy * xn)
        dxn = dy * g_ref[...]
        dxv = r * (dxn - xn * jnp.mean(dxn * xn, axis=-1, keepdims=True))
        dx_ref[...] = dxv
        dy_ref[...] = (gr_ref[...] * dxv).astype(BF16)
        q_ref[...] += _colsum(dxv * y_ref[...].astype(F32))

    return pl.pallas_call(
        body, name=name, grid=(T // tr,),
        in_specs=[_row_spec(tr, D), _row_spec(tr, D), _vec_spec(D), _row_spec(tr, D), _vec_spec(D)],
        out_specs=[_vec_spec(128), _row_spec(tr, D), _vec_spec(D), _row_spec(tr, D), _vec_spec(D)],
        out_shape=[jax.ShapeDtypeStruct((1, 128), F32), jax.ShapeDtypeStruct((T, D), F32), jax.ShapeDtypeStruct((1, D), F32),
                   jax.ShapeDtypeStruct((T, D), BF16), jax.ShapeDtypeStruct((1, D), F32)],
        compiler_params=pltpu.CompilerParams(dimension_semantics=("arbitrary",)),
    )(x, target, gain, y, g)


def _pool_fwd(h, w, scale, x, g1, gmlp, sc2, sh2, name):
    T, D = h.shape
    tr = min(T, ROW_TILE)

    def body(h_ref, w_ref, sc_ref, x_ref, g_ref, gm_ref, sc2_ref, sh2_ref, x2_ref, pooled_ref, ypre_ref, h2_ref, halo_ref):
        i = pl.program_id(0)

        @pl.when(i == 0)
        def _():
            halo_ref[...] = jnp.zeros_like(halo_ref)

        hv = h_ref[...]
        buf = jnp.concatenate([halo_ref[...], hv], axis=0)
        halo_ref[...] = hv[tr - POOL_HALO:, :]
        t = (i * tr + lax.broadcasted_iota(jnp.int32, (tr, 1), 0)).astype(F32)
        for gi, win in enumerate(POOL_WINDOWS):
            cols = slice(gi * POOL_GD, (gi + 1) * POOL_GD)
            val = buf[:, cols]
            sh = 1
            while sh < win:
                val = val + pltpu.roll(val, sh, axis=0)
                sh *= 2
            pooled = val[POOL_HALO:, :] / jnp.minimum(t + 1.0, float(win)) - hv[:, cols]
            pb = pooled.astype(BF16)
            pooled_ref[:, cols] = pb
            yp = jnp.dot(pb, w_ref[gi], preferred_element_type=F32)
            ypre_ref[:, cols] = yp.astype(BF16)
            x2_ref[:, cols] = x_ref[:, cols] + g_ref[:, cols] * (yp * sc_ref[:, cols])
        h2_ref[...] = _rms_mod(x2_ref[...], gm_ref[...], sc2_ref[...], sh2_ref[...]).astype(BF16)

    return pl.pallas_call(
        body, name=name, grid=(T // tr,),
        in_specs=[_row_spec(tr, D), pl.BlockSpec(w.shape, lambda i: (0, 0, 0)), _vec_spec(D), _row_spec(tr, D), _vec_spec(D), _vec_spec(D),
                  _vec_spec(D), _vec_spec(D)],
        out_specs=[_row_spec(tr, D)] * 4,
        out_shape=[jax.ShapeDtypeStruct((T, D), F32), jax.ShapeDtypeStruct((T, D), BF16), jax.ShapeDtypeStruct((T, D), BF16),
                   jax.ShapeDtypeStruct((T, D), BF16)],
        scratch_shapes=[pltpu.VMEM((POOL_HALO, D), F32)],
        compiler_params=pltpu.CompilerParams(dimension_semantics=("arbitrary",)),
    )(h, w, scale, x, g1, gmlp, sc2, sh2)


def _pool_bwd(dy, pooled, w, scale, g1, q, name):
    T, D = dy.shape
    tr = min(T, ROW_TILE)
    nt = T // tr
    ltot = tr + POOL_HALO

    def body(dy_ref, pooled_ref, w_ref, sc_ref, g_ref, q_ref, dh_ref, dw_ref, dsc_ref, dg_ref, halo_ref):
        i = pl.program_id(0)

        @pl.when(i == 0)
        def _():
            halo_ref[...] = jnp.zeros_like(halo_ref)
            dw_ref[...] = jnp.zeros_like(dw_ref)
            dsc_ref[...] = g_ref[...] * q_ref[...]
            dg_ref[...] = sc_ref[...] * q_ref[...]

        t = ((nt - 1 - i) * tr + lax.broadcasted_iota(jnp.int32, (tr, 1), 0)).astype(F32)
        for gi, win in enumerate(POOL_WINDOWS):
            cols = slice(gi * POOL_GD, (gi + 1) * POOL_GD)
            dyb = (dy_ref[:, cols].astype(F32) * sc_ref[:, cols]).astype(BF16)
            dw_ref[gi] += lax.dot_general(pooled_ref[:, cols], dyb, (((0,), (0,)), ((), ())), preferred_element_type=F32)
            dpool = lax.dot_general(dyb, w_ref[gi], (((1,), (1,)), ((), ())), preferred_element_type=F32)
            qv = dpool / jnp.minimum(t + 1.0, float(win))
            val = jnp.concatenate([qv, halo_ref[:, cols]], axis=0)
            halo_ref[:, cols] = qv[:POOL_HALO, :]
            sh = 1
            while sh < win:
                val = val + pltpu.roll(val, ltot - sh, axis=0)
                sh *= 2
            dh_ref[:, cols] = (val[:tr, :] - dpool).astype(BF16)

    rev = pl.BlockSpec((tr, D), lambda i: (nt - 1 - i, 0))
    return pl.pallas_call(
        body, name=name, grid=(nt,),
        in_specs=[rev, rev, pl.BlockSpec(w.shape, lambda i: (0, 0, 0)), _vec_spec(D), _vec_spec(D), _vec_spec(D)],
        out_specs=[rev, pl.BlockSpec(w.shape, lambda i: (0, 0, 0)), _vec_spec(D), _vec_spec(D)],
        out_shape=[jax.ShapeDtypeStruct((T, D), BF16), jax.ShapeDtypeStruct(w.shape, F32),
                   jax.ShapeDtypeStruct((1, D), F32), jax.ShapeDtypeStruct((1, D), F32)],
        scratch_shapes=[pltpu.VMEM((POOL_HALO, D), F32)],
        compiler_params=pltpu.CompilerParams(dimension_semantics=("arbitrary",)),
    )(dy, pooled, w, scale, g1, q)


_INV_SQRT2 = 0.7071067811865476
_INV_SQRT2PI = 0.3989422804014327


def _gelu(v):
    return 0.5 * v * (1.0 + lax.erf(v * _INV_SQRT2))


def _gelu_grad(v):
    return 0.5 * (1.0 + lax.erf(v * _INV_SQRT2)) + v * jnp.exp(-0.5 * v * v) * _INV_SQRT2PI


def _sgu_ln(v, g, b):
    mu = jnp.mean(v, axis=-1, keepdims=True)
    xc = v - mu
    rstd = lax.rsqrt(jnp.mean(xc * xc, axis=-1, keepdims=True) + LN_EPS)
    xh = xc * rstd
    return xh, rstd, xh * g + b


def _tril_mask():
    return lax.broadcasted_iota(jnp.int32, (SGU_CHUNK, SGU_CHUNK), 0) >= lax.broadcasted_iota(jnp.int32, (SGU_CHUNK, SGU_CHUNK), 1)


SGU_TILE = 256


def _sgu_gate_fwd(zz, ln_g, ln_b, ws, bs_t, name):
    T = zz.shape[0]
    ts = min(T, SGU_TILE)

    def body(zz_ref, g_ref, b_ref, ws_ref, bs_ref, out_ref):
        z = _gelu(zz_ref[...])
        u = z[:, :SGU_W]
        _, _, vn = _sgu_ln(z[:, SGU_W:], g_ref[...], b_ref[...])
        vb = vn.astype(BF16)
        tril = _tril_mask()
        for hh in range(SGU_H):
            wm = jnp.where(tril, ws_ref[hh], 0.0).astype(BF16)
            bcol = bs_ref[:, hh:hh + 1]
            cs = slice(hh * SGU_HD, (hh + 1) * SGU_HD)
            for j in range(ts // SGU_CHUNK):
                rs = slice(j * SGU_CHUNK, (j + 1) * SGU_CHUNK)
                mixed = jnp.dot(wm, vb[rs, cs], preferred_element_type=F32) + bcol
                out_ref[rs, cs] = (u[rs, cs] * mixed).astype(BF16)

    return pl.pallas_call(
        body, name=name, grid=(T // ts,),
        in_specs=[_row_spec(ts, 2 * SGU_W), _vec_spec(SGU_W), _vec_spec(SGU_W),
                  pl.BlockSpec(ws.shape, lambda i: (0, 0, 0)), pl.BlockSpec(bs_t.shape, lambda i: (0, 0))],
        out_specs=_row_spec(ts, SGU_W),
        out_shape=jax.ShapeDtypeStruct((T, SGU_W), BF16),
        compiler_params=pltpu.CompilerParams(dimension_semantics=("parallel",)),
    )(zz, ln_g, ln_b, ws, bs_t)


def _sgu_gate_bwd(zz, dgated, ln_g, ln_b, ws, bs_t, name):
    T = zz.shape[0]
    ts = min(T, SGU_TILE)
    nt = T // ts

    def body(zz_ref, dg_ref, g_ref, b_ref, ws_ref, bs_ref, dzz_ref, dws_ref, dbs_ref, dlg_ref, dlb_ref, dlo_ref, dmx_ref):
        i = pl.program_id(0)

        @pl.when(i == 0)
        def _():
            dws_ref[...] = jnp.zeros_like(dws_ref)
            dmx_ref[...] = jnp.zeros_like(dmx_ref)
            dlg_ref[...] = jnp.zeros_like(dlg_ref)
            dlb_ref[...] = jnp.zeros_like(dlb_ref)

        zzv = zz_ref[...]
        z = _gelu(zzv)
        u = z[:, :SGU_W]
        xh, rstd, vn = _sgu_ln(z[:, SGU_W:], g_ref[...], b_ref[...])
        vb = vn.astype(BF16)
        dgv = dg_ref[...].astype(F32)
        tril = _tril_mask()
        for hh in range(SGU_H):
            wm = jnp.where(tril, ws_ref[hh], 0.0).astype(BF16)
            bcol = bs_ref[:, hh:hh + 1]
            cs = slice(hh * SGU_HD, (hh + 1) * SGU_HD)
            for j in range(ts // SGU_CHUNK):
                rs = slice(j * SGU_CHUNK, (j + 1) * SGU_CHUNK)
                mixed = jnp.dot(wm, vb[rs, cs], preferred_element_type=F32) + bcol
                dmixed = dgv[rs, cs] * u[rs, cs]
                dzz_ref[rs, cs] = (dgv[rs, cs] * mixed * _gelu_grad(zzv[rs, cs])).astype(BF16)
                dmb = dmixed.astype(BF16)
                dws_ref[hh] += lax.dot_general(dmb, vb[rs, cs], (((1,), (1,)), ((), ())), preferred_element_type=F32)
                dmx_ref[hh] += dmixed
                dlo_ref[rs, cs] = lax.dot_general(wm, dmb, (((0,), (0,)), ((), ())), preferred_element_type=F32)
        dlo = dlo_ref[...]
        dlg_ref[...] += _colsum(dlo * xh)
        dlb_ref[...] += _colsum(dlo)
        dxh = dlo * g_ref[...]
        dv = rstd * (dxh - jnp.mean(dxh, axis=-1, keepdims=True) - xh * jnp.mean(dxh * xh, axis=-1, keepdims=True))
        dzz_ref[:, SGU_W:] = (dv * _gelu_grad(zzv[:, SGU_W:])).astype(BF16)

        @pl.when(i == nt - 1)
        def _():
            tril_f = tril.astype(F32)
            for hh in range(SGU_H):
                dws_ref[hh] = dws_ref[hh] * tril_f
                dbs_ref[hh] = jnp.broadcast_to(jnp.sum(dmx_ref[hh], axis=-1, keepdims=True), (SGU_CHUNK, SGU_HD))

    full3 = pl.BlockSpec(ws.shape, lambda i: (0, 0, 0))
    return pl.pallas_call(
        body, name=name, grid=(nt,),
        in_specs=[_row_spec(ts, 2 * SGU_W), _row_spec(ts, SGU_W), _vec_spec(SGU_W), _vec_spec(SGU_W), full3,
                  pl.BlockSpec(bs_t.shape, lambda i: (0, 0))],
        out_specs=[_row_spec(ts, 2 * SGU_W), full3, full3, _vec_spec(SGU_W), _vec_spec(SGU_W)],
        out_shape=[jax.ShapeDtypeStruct((T, 2 * SGU_W), BF16), jax.ShapeDtypeStruct(ws.shape, F32), jax.ShapeDtypeStruct(ws.shape, F32),
                   jax.ShapeDtypeStruct((1, SGU_W), F32), jax.ShapeDtypeStruct((1, SGU_W), F32)],
        scratch_shapes=[pltpu.VMEM((ts, SGU_W), F32), pltpu.VMEM(ws.shape, F32)],
        compiler_params=pltpu.CompilerParams(dimension_semantics=("arbitrary",)),
    )(zz, dgated, ln_g, ln_b, ws, bs_t)


def _rope_fwd(blk, cc, sa, sb):
    return blk * cc + pltpu.roll(blk, 96, axis=1) * sa + pltpu.roll(blk, 32, axis=1) * sb


def _rope_bwd(d, cc, sa, sb):
    return d * cc + pltpu.roll(d * sa, 32, axis=1) + pltpu.roll(d * sb, 96, axis=1)


def _rms(v, g):
    r = lax.rsqrt(jnp.mean(v * v, axis=-1, keepdims=True) + RMS_EPS)
    vn = v * r
    return vn, r, vn * g


def _rms_bwd(dy, vn, r, g):
    dvn = dy * g
    return r * (dvn - vn * jnp.mean(dvn * vn, axis=-1, keepdims=True))


_KV0 = MLA_QL
_KR0 = MLA_QL + MLA_KVL


def _mla_lat_fwd(lat, qg, kvg, cc, sa, sb, name):
    T = lat.shape[0]
    tr = min(T, ROW_TILE)

    def body(lat_ref, qg_ref, kvg_ref, cc_ref, sa_ref, sb_ref, cq_ref, ckv_ref, kr_ref):
        lv = lat_ref[...]
        cq_ref[...] = _rms(lv[:, :_KV0], qg_ref[...])[2].astype(BF16)
        ckv_ref[...] = _rms(lv[:, _KV0:_KR0], kvg_ref[...])[2].astype(BF16)
        kr_ref[...] = _rope_fwd(lv[:, _KR0:], cc_ref[...], sa_ref[...], sb_ref[...])

    return pl.pallas_call(
        body, name=name, grid=(T // tr,),
        in_specs=[_row_spec(tr, MLA_LATP), _vec_spec(MLA_QL), _vec_spec(MLA_KVL), _row_spec(tr, 128), _row_spec(tr, 128), _row_spec(tr, 128)],
        out_specs=[_row_spec(tr, MLA_QL), _row_spec(tr, MLA_KVL), _row_spec(tr, 128)],
        out_shape=[jax.ShapeDtypeStruct((T, MLA_QL), BF16), jax.ShapeDtypeStruct((T, MLA_KVL), BF16), jax.ShapeDtypeStruct((T, 128), F32)],
        compiler_params=pltpu.CompilerParams(dimension_semantics=("parallel",)),
    )(lat, qg, kvg, cc, sa, sb)


def _mla_lat_bwd(lat, dcqn, dckvn, dkrot, qg, kvg, cc, sa, sb, name):
    T = lat.shape[0]
    tr = min(T, ROW_TILE)

    def body(lat_ref, dcq_ref, dckv_ref, dkr_ref, qg_ref, kvg_ref, cc_ref, sa_ref, sb_ref, dlat_ref, dqg_ref, dkvg_ref):
        @pl.when(pl.program_id(0) == 0)
        def _():
            dqg_ref[...] = jnp.zeros_like(dqg_ref)
            dkvg_ref[...] = jnp.zeros_like(dkvg_ref)

        lv = lat_ref[...]
        qn, qr, _ = _rms(lv[:, :_KV0], qg_ref[...])
        kn, kr, _ = _rms(lv[:, _KV0:_KR0], kvg_ref[...])
        dcq = dcq_ref[...]
        dckv = dckv_ref[...]
        dqg_ref[...] += _colsum(dcq * qn)
        dkvg_ref[...] += _colsum(dckv * kn)
        dlat_ref[:, :_KV0] = _rms_bwd(dcq, qn, qr, qg_ref[...]).astype(BF16)
        dlat_ref[:, _KV0:_KR0] = _rms_bwd(dckv, kn, kr, kvg_ref[...]).astype(BF16)
        dlat_ref[:, _KR0:] = _rope_bwd(dkr_ref[...], cc_ref[...], sa_ref[...], sb_ref[...]).astype(BF16)

    return pl.pallas_call(
        body, name=name, grid=(T // tr,),
        in_specs=[_row_spec(tr, MLA_LATP), _row_spec(tr, MLA_QL), _row_spec(tr, MLA_KVL), _row_spec(tr, 128),
                  _vec_spec(MLA_QL), _vec_spec(MLA_KVL), _row_spec(tr, 128), _row_spec(tr, 128), _row_spec(tr, 128)],
        out_specs=[_row_spec(tr, MLA_LATP), _vec_spec(MLA_QL), _vec_spec(MLA_KVL)],
        out_shape=[jax.ShapeDtypeStruct((T, MLA_LATP), BF16), jax.ShapeDtypeStruct((1, MLA_QL), F32), jax.ShapeDtypeStruct((1, MLA_KVL), F32)],
        compiler_params=pltpu.CompilerParams(dimension_semantics=("arbitrary",)),
    )(lat, dcqn, dckvn, dkrot, qg, kvg, cc, sa, sb)


LOG2E = 1.4426950408889634
Q_SCALE = SM_SCALE * LOG2E


def _epi_q_rope(acc, cc, sa, sb):
    out = []
    for hh in range(acc.shape[1] // MLA_HP):
        a, m, b = hh * MLA_HP, hh * MLA_HP + MLA_NOPE, (hh + 1) * MLA_HP
        out += [acc[:, a:m] * Q_SCALE, _rope_fwd(acc[:, m:b], cc, sa, sb) * Q_SCALE]
    return (jnp.concatenate(out, axis=1),)


def _mla_ukv(ckvn, w_ukv, krot, name):
    T = ckvn.shape[0]
    tr = min(T, ATT_TILE)
    hg = ATT_HG
    gw = hg * MLA_HP

    def body(a_ref, w_ref, kr_ref, ko_ref, kt_ref, vo_ref, vt_ref):
        acc = jnp.dot(a_ref[...], w_ref[...], preferred_element_type=F32)
        kr = kr_ref[...]
        krb, krt = kr.astype(BF16), kr.T.astype(BF16)
        for hh in range(hg):
            a, m, b = hh * MLA_HP, hh * MLA_HP + MLA_NOPE, (hh + 1) * MLA_HP
            kn, vh = acc[:, a:m], acc[:, m:b]
            ko_ref[:, a:m] = kn.astype(BF16)
            ko_ref[:, m:b] = krb
            kt_ref[a:m, :] = kn.T.astype(BF16)
            kt_ref[m:b, :] = krt
            vo_ref[:, hh * MLA_V:(hh + 1) * MLA_V] = vh.astype(BF16)
            vt_ref[hh] = vh.T.astype(BF16)

    tk = min(T, ATT_TILE)
    per = tk // tr
    HW = MLA_H * MLA_HP
    return pl.pallas_call(
        body, name=name, grid=(T // tr, MLA_H // hg),
        in_specs=[pl.BlockSpec((tr, MLA_KVL), lambda i, g: (i, 0)), pl.BlockSpec((MLA_KVL, gw), lambda i, g: (0, g)),
                  pl.BlockSpec((tr, 128), lambda i, g: (i, 0))],
        out_specs=[pl.BlockSpec((tr, gw), lambda i, g: (i, g)), pl.BlockSpec((gw, tr), lambda i, g: (g, i)),
                   pl.BlockSpec((tr, hg * MLA_V), lambda i, g: (i, g)),
                   pl.BlockSpec((hg, None, MLA_V, tr), lambda i, g: (g, i // per, 0, i % per))],
        out_shape=[jax.ShapeDtypeStruct((T, HW), BF16), jax.ShapeDtypeStruct((HW, T), BF16), jax.ShapeDtypeStruct((T, MLA_H * MLA_V), BF16),
                   jax.ShapeDtypeStruct((MLA_H, T // tk, MLA_V, tk), BF16)],
        compiler_params=pltpu.CompilerParams(dimension_semantics=("parallel", "parallel")),
    )(ckvn, w_ukv, krot)


ATT_HG = 4


def _mla_prep_bwd(dqt, dkr, cc, sa, sb, name):
    _, nq, _, tq = dqt.shape
    T = nq * tq
    gw = ATT_HG * MLA_HP

    def body(dq_ref, dk_ref, cc_ref, sa_ref, sb_ref, dqp_ref, dkr_ref):
        @pl.when(pl.program_id(1) == 0)
        def _():
            dkr_ref[...] = jnp.zeros_like(dkr_ref)

        cc, sa, sb = cc_ref[...], sa_ref[...], sb_ref[...]
        acc = jnp.zeros((tq, 128), F32)
        for hh in range(ATT_HG):
            a, m, b = hh * MLA_HP, hh * MLA_HP + MLA_NOPE, (hh + 1) * MLA_HP
            dqh = dq_ref[hh].astype(F32).T * SM_SCALE
            dqp_ref[:, a:m] = dqh[:, :MLA_NOPE].astype(BF16)
            dqp_ref[:, m:b] = _rope_bwd(dqh[:, MLA_NOPE:], cc, sa, sb).astype(BF16)
            acc = acc + dk_ref[:, hh * 128:(hh + 1) * 128].astype(F32)
        dkr_ref[...] += acc

    tab = pl.BlockSpec((tq, 128), lambda i, g: (i, 0))
    return pl.pallas_call(
        body, name=name, grid=(nq, MLA_H // ATT_HG),
        in_specs=[pl.BlockSpec((ATT_HG, None, MLA_HP, tq), lambda i, g: (g, i, 0, 0)), pl.BlockSpec((tq, ATT_HG * 128), lambda i, g: (i, g)),
                  tab, tab, tab],
        out_specs=[pl.BlockSpec((tq, gw), lambda i, g: (i, g)), tab],
        out_shape=[jax.ShapeDtypeStruct((T, MLA_H * MLA_HP), BF16), jax.ShapeDtypeStruct((T, 128), F32)],
        compiler_params=pltpu.CompilerParams(dimension_semantics=("parallel", "arbitrary")),
    )(dqt, dkr, cc, sa, sb)


_NT = (((1,), (1,)), ((), ()))


def _as_row(col, n):
    return jnp.broadcast_to(col, (n, 128)).T[0:1, :]


def _attn_fwd(q, k, vt, name):
    T = q.shape[0]
    tq = tk = min(T, ATT_TILE)
    nq = T // tq
    hg = ATT_FWD_HEADS

    def body(q_ref, k_ref, vt_ref, o_ref, lse_ref, m_ref, l_ref, acc_ref):
        i = pl.program_id(1)
        m_ref[...] = jnp.full_like(m_ref, NEG_INF)
        l_ref[...] = jnp.zeros_like(l_ref)
        acc_ref[...] = jnp.zeros_like(acc_ref)

        def step(j, diag):
            off = pl.multiple_of(j * tk, tk)
            sts = [lax.dot_general(k_ref[pl.ds(off, tk), hh * MLA_HP:(hh + 1) * MLA_HP], q_ref[:, hh * MLA_HP:(hh + 1) * MLA_HP], _NT,
                                   preferred_element_type=F32) for hh in range(hg)]
            for hh in range(hg):
                st = sts[hh]
                if diag:
                    st = jnp.where(lax.broadcasted_iota(jnp.int32, (tk, tq), 0) <= lax.broadcasted_iota(jnp.int32, (tk, tq), 1), st, NEG_INF)
                m_prev = m_ref[hh]
                m_new = jnp.maximum(m_prev, jnp.max(st, axis=0, keepdims=True))
                alpha = jnp.exp2(m_prev - m_new)
                pt = jnp.exp2(st - m_new)
                l_ref[hh] = alpha * l_ref[hh] + jnp.sum(pt, axis=0, keepdims=True)
                acc_ref[hh] = alpha * acc_ref[hh] + jnp.dot(vt_ref[hh, j], pt.astype(BF16), preferred_element_type=F32)
                m_ref[hh] = m_new

        def loop_body(j, carry):
            step(j, False)
            return carry

        lax.fori_loop(0, i, loop_body, 0)
        step(i, True)
        for hh in range(hg):
            o_ref[:, hh * MLA_V:(hh + 1) * MLA_V] = (acc_ref[hh] / l_ref[hh]).T.astype(BF16)
            lse_ref[hh] = m_ref[hh] + jnp.log2(l_ref[hh])

    return pl.pallas_call(
        body, name=name, grid=(MLA_H // hg, nq),
        in_specs=[pl.BlockSpec((tq, hg * MLA_HP), lambda h, i: (i, h)), pl.BlockSpec((T, hg * MLA_HP), lambda h, i: (0, h)),
                  pl.BlockSpec((hg, nq, MLA_V, tk), lambda h, i: (h, 0, 0, 0))],
        out_specs=[pl.BlockSpec((tq, hg * MLA_V), lambda h, i: (i, h)), pl.BlockSpec((hg, None, 1, tq), lambda h, i: (h, i, 0, 0))],
        out_shape=[jax.ShapeDtypeStruct((T, MLA_H * MLA_V), BF16), jax.ShapeDtypeStruct((MLA_H, nq, 1, tq), F32)],
        scratch_shapes=[pltpu.VMEM((hg, 1, tq), F32), pltpu.VMEM((hg, 1, tq), F32), pltpu.VMEM((hg, MLA_V, tq), F32)],
        compiler_params=pltpu.CompilerParams(dimension_semantics=("parallel", "arbitrary")),
    )(q, k, vt)


def _attn_delta(do, o, name):
    T = do.shape[0]
    tq = min(T, ATT_TILE)

    def body(do_ref, o_ref, d_ref):
        for hh in range(MLA_H):
            cs = slice(hh * MLA_V, (hh + 1) * MLA_V)
            s = jnp.sum(do_ref[:, cs].astype(F32) * o_ref[:, cs].astype(F32), axis=-1, keepdims=True)
            d_ref[hh] = _as_row(s, tq)

    return pl.pallas_call(
        body, name=name, grid=(T // tq,),
        in_specs=[_row_spec(tq, MLA_H * MLA_V), _row_spec(tq, MLA_H * MLA_V)],
        out_specs=pl.BlockSpec((MLA_H, None, 1, tq), lambda i: (0, i, 0, 0)),
        out_shape=jax.ShapeDtypeStruct((MLA_H, T // tq, 1, tq), F32),
        compiler_params=pltpu.CompilerParams(dimension_semantics=("parallel",)),
    )(do, o)


def _attn_bwd(q, k, kt, v, do, lse, delta, name):
    T = q.shape[0]
    tq = tk = min(T, ATT_TILE)
    nq = nk = T // tq
    tsd = min(tq, ATT_SUB)
    hg = ATT_BWD_HEADS

    def body(q_ref, k_ref, kt_ref, v_ref, do_ref, lse_ref, dl_ref, dqt_ref, dkv_ref, dkr_ref, dq_acc, dk_acc, dv_acc):
        j = pl.program_id(1)

        @pl.when(j == 0)
        def _():
            dq_acc[...] = jnp.zeros_like(dq_acc)

        dk_acc[...] = jnp.zeros_like(dk_acc)
        dv_acc[...] = jnp.zeros_like(dv_acc)

        def step(i, diag):
            off = pl.multiple_of(i * tq, tq)
            ts, nsub = (tsd, tq // tsd) if diag else (tq, 1)
            for u in range(nsub):
                cols = slice(u * ts, (u + 1) * ts)
                nk_u = (u + 1) * ts if diag else tk
                rows = pl.ds(off + u * ts, ts)
                pre = []
                for hh in range(hg):
                    hq, hv = slice(hh * MLA_HP, (hh + 1) * MLA_HP), slice(hh * MLA_V, (hh + 1) * MLA_V)
                    qi, doi = q_ref[rows, hq], do_ref[rows, hv]
                    st = lax.dot_general(k_ref[:nk_u, hq], qi, _NT, preferred_element_type=F32)
                    dpt = lax.dot_general(v_ref[:nk_u, hv], doi, _NT, preferred_element_type=F32)
                    pre.append((qi, doi, st, dpt))
                for hh in range(hg):
                    hq, hv = slice(hh * MLA_HP, (hh + 1) * MLA_HP), slice(hh * MLA_V, (hh + 1) * MLA_V)
                    qi, doi, st, dpt = pre[hh]
                    if diag:
                        qcol = u * ts + lax.broadcasted_iota(jnp.int32, (nk_u, ts), 1)
                        st = jnp.where(lax.broadcasted_iota(jnp.int32, (nk_u, ts), 0) <= qcol, st, NEG_INF)
                    pt = jnp.exp2(st - lse_ref[hh, i][:, cols])
                    dv_acc[:nk_u, hv] += jnp.dot(pt.astype(BF16), doi, preferred_element_type=F32)
                    dsb = (pt * (dpt - dl_ref[hh, i][:, cols])).astype(BF16)
                    dk_acc[:nk_u, hq] += jnp.dot(dsb, qi, preferred_element_type=F32)
                    dq_acc[hh, i, :, cols] += jnp.dot(kt_ref[hq, :nk_u], dsb, preferred_element_type=F32)

        def loop_body(i, carry):
            step(i, False)
            return carry

        step(j, True)
        lax.fori_loop(j + 1, nq, loop_body, 0)
        for hh in range(hg):
            a, m, b = hh * MLA_HP, hh * MLA_HP + MLA_NOPE, (hh + 1) * MLA_HP
            dkv_ref[:, a:m] = (dk_acc[:, a:m] * (1.0 / LOG2E)).astype(BF16)
            dkv_ref[:, m:b] = dv_acc[:, hh * MLA_V:(hh + 1) * MLA_V].astype(BF16)
            dkr_ref[:, hh * 128:(hh + 1) * 128] = (dk_acc[:, m:b] * (1.0 / LOG2E)).astype(BF16)

        @pl.when(j == nk - 1)
        def _():
            dqt_ref[...] = dq_acc[...].astype(BF16)

    stat = pl.BlockSpec((hg, nq, 1, tq), lambda h, j: (h, 0, 0, 0))
    return pl.pallas_call(
        body, name=name, grid=(MLA_H // hg, nk),
        in_specs=[pl.BlockSpec((T, hg * MLA_HP), lambda h, j: (0, h)), pl.BlockSpec((tk, hg * MLA_HP), lambda h, j: (j, h)),
                  pl.BlockSpec((hg * MLA_HP, tk), lambda h, j: (h, j)), pl.BlockSpec((tk, hg * MLA_V), lambda h, j: (j, h)),
                  pl.BlockSpec((T, hg * MLA_V), lambda h, j: (0, h)), stat, stat],
        out_specs=[pl.BlockSpec((hg, nq, MLA_HP, tq), lambda h, j: (h, 0, 0, 0)), pl.BlockSpec((tk, hg * MLA_HP), lambda h, j: (j, h)),
                   pl.BlockSpec((tk, hg * 128), lambda h, j: (j, h))],
        out_shape=[jax.ShapeDtypeStruct((MLA_H, nq, MLA_HP, tq), BF16), jax.ShapeDtypeStruct((T, MLA_H * MLA_HP), BF16),
                   jax.ShapeDtypeStruct((T, MLA_H * 128), BF16)],
        scratch_shapes=[pltpu.VMEM((hg, nq, MLA_HP, tq), F32), pltpu.VMEM((tk, hg * MLA_HP), F32), pltpu.VMEM((tk, hg * MLA_V), F32)],
        compiler_params=pltpu.CompilerParams(dimension_semantics=("parallel", "arbitrary")),
    )(q, k, kt, v, do, lse, delta)


ADA_TN = 512


def _silu(v):
    return v * (1.0 / (1.0 + jnp.exp(-v)))


def _ada_fwd(c_all, ada_w, ada_b_loc, name):
    L, D, Nc = ada_w.shape
    B = c_all.shape[0]

    def body(c_ref, w_ref, b_ref, o_ref):
        ca = _silu(c_ref[...]).astype(BF16)
        o_ref[...] = jnp.dot(ca, w_ref[...].astype(BF16), preferred_element_type=F32) + b_ref[...]

    return pl.pallas_call(
        body, name=name, grid=(L, Nc // ADA_TN),
        in_specs=[pl.BlockSpec((B, D), lambda l, n: (0, 0)), pl.BlockSpec((None, D, ADA_TN), lambda l, n: (l, 0, n)),
                  pl.BlockSpec((None, 1, ADA_TN), lambda l, n: (l, 0, n))],
        out_specs=pl.BlockSpec((None, B, ADA_TN), lambda l, n: (l, 0, n)),
        out_shape=jax.ShapeDtypeStruct((L, B, Nc), F32),
        compiler_params=pltpu.CompilerParams(dimension_semantics=("parallel", "parallel")),
    )(c_all, ada_w, ada_b_loc)


def _ada_bwd_adamw(c_all_t, dmod_loc, w, m, v, name):
    D, B = c_all_t.shape
    L, _, Nc = dmod_loc.shape

    def body(c_ref, d_ref, w_ref, m_ref, v_ref, g_ref, dl_ref, nm_ref, nv_ref):
        ca = _silu(c_ref[...])
        dv = d_ref[...]
        gv = ca[:, 0:1] * dv[0:1, :]
        for b in range(1, B):
            gv = gv + ca[:, b:b + 1] * dv[b:b + 1, :]
        mn = ADAM_B1 * m_ref[...] + (1.0 - ADAM_B1) * gv
        vn = ADAM_B2 * v_ref[...] + (1.0 - ADAM_B2) * (gv * gv)
        g_ref[...] = gv
        nm_ref[...] = mn
        nv_ref[...] = vn
        dl_ref[...] = -ADAM_LR * ((mn / _ADAM_C1) / (jnp.sqrt(vn / _ADAM_C2) + ADAM_EPS) + ADAM_WD * w_ref[...])

    blk = pl.BlockSpec((None, D, ADA_TN), lambda l, n: (l, 0, n))
    return pl.pallas_call(
        body, name=name, grid=(L, Nc // ADA_TN),
        in_specs=[pl.BlockSpec((D, B), lambda l, n: (0, 0)), pl.BlockSpec((None, B, ADA_TN), lambda l, n: (l, 0, n)), blk, blk, blk],
        out_specs=[blk] * 4,
        out_shape=[jax.ShapeDtypeStruct((L, D, Nc), F32)] * 4,
        compiler_params=pltpu.CompilerParams(dimension_semantics=("parallel", "parallel")),
    )(c_all_t, dmod_loc, w, m, v)


def _sum_lead(parts, name, out_dtype=F32):
    R, C = parts[0].shape[1:]
    n_tot = sum(p.shape[0] for p in parts)
    tr = R
    for cand in (512, 256, 128, 64, 32, 16):
        if R % cand == 0 and cand * C * 4 * n_tot <= (8 << 20):
            tr = cand
            break

    def body(*refs):
        o_ref = refs[-1]
        acc = None
        for r in refs[:-1]:
            for s in range(r.shape[0]):
                acc = r[s].astype(F32) if acc is None else acc + r[s].astype(F32)
        o_ref[...] = acc.astype(o_ref.dtype)

    return pl.pallas_call(
        body, name=name, grid=(R // tr,),
        in_specs=[pl.BlockSpec((p.shape[0], tr, C), lambda i: (0, i, 0)) for p in parts],
        out_specs=pl.BlockSpec((tr, C), lambda i: (i, 0)),
        out_shape=jax.ShapeDtypeStruct((R, C), out_dtype),
        compiler_params=pltpu.CompilerParams(dimension_semantics=("parallel",)),
    )(*parts)


_ADAM_C1 = 1.0 - ADAM_B1 ** ADAM_STEP
_ADAM_C2 = 1.0 - ADAM_B2 ** ADAM_STEP


def _adamw(w, g, m, v, name):
    shape = w.shape
    C = shape[-1]
    R = math.prod(shape[:-1]) if len(shape) > 1 else 1
    w2, g2, m2, v2 = (a.reshape(R, C) for a in (w, g, m, v))
    tr = R
    for cand in (1024, 512, 256, 128, 64, 32, 16, 8):
        if R % cand == 0 and cand * C * 4 <= (1 << 20):
            tr = cand
            break

    def body(w_ref, g_ref, m_ref, v_ref, d_ref, nm_ref, nv_ref):
        gv = g_ref[...]
        mn = ADAM_B1 * m_ref[...] + (1.0 - ADAM_B1) * gv
        vn = ADAM_B2 * v_ref[...] + (1.0 - ADAM_B2) * (gv * gv)
        nm_ref[...] = mn
        nv_ref[...] = vn
        m_hat = mn / _ADAM_C1
        v_hat = vn / _ADAM_C2
        d_ref[...] = -ADAM_LR * (m_hat / (jnp.sqrt(v_hat) + ADAM_EPS) + ADAM_WD * w_ref[...])

    spec = pl.BlockSpec((tr, C), lambda i: (i, 0))
    outs = pl.pallas_call(
        body, name=name, grid=(R // tr,),
        in_specs=[spec] * 4, out_specs=[spec] * 3,
        out_shape=[jax.ShapeDtypeStruct((R, C), F32)] * 3,
        compiler_params=pltpu.CompilerParams(dimension_semantics=("parallel",)),
    )(w2, g2, m2, v2)
    return tuple(o.reshape(shape) for o in outs)


def _row_tile(rows, cols, itemsize, budget):
    for cand in (1024, 512, 256, 128, 64, 32, 16):
        if rows % cand == 0 and cand * cols * itemsize <= budget:
            return cand
    return rows


def _sum_sel(sel, stacked, others, name, out_dtype):
    R, C = stacked.shape[1:]
    n_tot = 1 + sum(o.shape[0] for o in others)
    tr = _row_tile(R, C, 4 * n_tot, 8 << 20)

    def body(sel_ref, s_ref, *refs):
        o_ref = refs[-1]
        acc = s_ref[...].astype(F32)
        for r in refs[:-1]:
            for s in range(r.shape[0]):
                acc = acc + r[s].astype(F32)
        o_ref[...] = acc.astype(o_ref.dtype)

    return pl.pallas_call(
        body, name=name,
        grid_spec=pltpu.PrefetchScalarGridSpec(
            num_scalar_prefetch=1, grid=(R // tr,),
            in_specs=[pl.BlockSpec((None, tr, C), lambda i, s: (s[0], i, 0))] + [pl.BlockSpec((o.shape[0], tr, C), lambda i, s: (0, i, 0)) for o in others],
            out_specs=pl.BlockSpec((tr, C), lambda i, s: (i, 0))),
        out_shape=jax.ShapeDtypeStruct((R, C), out_dtype),
        compiler_params=pltpu.CompilerParams(dimension_semantics=("parallel",)),
    )(sel, stacked, *others)


def _adamw_piece(cidx, w2, m2, v2, mine, got, bufs, row0, name):
    hr, C = mine.shape
    tr = _row_tile(math.gcd(hr, row0) if row0 else hr, C, 4, 1 << 20)
    nt = hr // tr

    def body(c_ref, w_ref, m_ref, v_ref, a_ref, b_ref, _g, _d, _nm, _nv, g_ref, d_ref, nm_ref, nv_ref):
        gv = jnp.where(pl.program_id(0) == c_ref[0], a_ref[...], b_ref[...])
        mn = ADAM_B1 * m_ref[...] + (1.0 - ADAM_B1) * gv
        vn = ADAM_B2 * v_ref[...] + (1.0 - ADAM_B2) * (gv * gv)
        g_ref[...] = gv
        nm_ref[...] = mn
        nv_ref[...] = vn
        d_ref[...] = -ADAM_LR * ((mn / _ADAM_C1) / (jnp.sqrt(vn / _ADAM_C2) + ADAM_EPS) + ADAM_WD * w_ref[...])

    rows = pl.BlockSpec((tr, C), lambda hf, t, c: (row0 // tr + hf * nt + t, 0))
    mine_spec = pl.BlockSpec((tr, C), lambda hf, t, c: (jnp.where(hf == c[0], t, 0), 0))
    got_spec = pl.BlockSpec((tr, C), lambda hf, t, c: (jnp.where(hf == c[0], 0, t), 0))
    return pl.pallas_call(
        body, name=name,
        grid_spec=pltpu.PrefetchScalarGridSpec(num_scalar_prefetch=1, grid=(2, nt), in_specs=[rows] * 3 + [mine_spec, got_spec] + [_ANY_SPEC] * 4,
                                               out_specs=[rows] * 4),
        out_shape=[jax.ShapeDtypeStruct(w2.shape, F32)] * 4,
        input_output_aliases={6 + n: n for n in range(4)},
        compiler_params=pltpu.CompilerParams(dimension_semantics=("parallel", "parallel")),
    )(cidx, w2, m2, v2, mine, got, *bufs)


_VMEM_SPEC = pl.BlockSpec(memory_space=pltpu.VMEM)
_HBM_SPEC = pl.BlockSpec(memory_space=pltpu.HBM)


def _flip(v, bit):
    return (1 - v) if bit else v


def _allgather8(v, name):
    def body(v_ref, out_ref, send_sems, recv_sems, local_sem):
        x, y, c = _idx()
        me = 4 * x + 2 * y + c
        mine = pltpu.make_async_copy(v_ref, out_ref.at[me], local_sem)
        mine.start()
        sends = []
        for k in range(1, N_DEV):
            peer = (_flip(x, k & 4), _flip(y, k & 2), _flip(c, k & 1))
            cp = pltpu.make_async_remote_copy(src_ref=v_ref, dst_ref=out_ref.at[me], send_sem=send_sems.at[k - 1], recv_sem=recv_sems.at[k - 1],
                                              device_id=peer, device_id_type=MESH)
            cp.start()
            sends.append(cp)
        for k in range(1, N_DEV):
            px, py, pc = _flip(x, k & 4), _flip(y, k & 2), _flip(c, k & 1)
            src = 4 * px + 2 * py + pc
            pltpu.make_async_remote_copy(src_ref=v_ref, dst_ref=out_ref.at[src], send_sem=send_sems.at[k - 1], recv_sem=recv_sems.at[k - 1],
                                         device_id=(px, py, pc), device_id_type=MESH).wait_recv()
        for cp in sends:
            cp.wait_send()
        mine.wait()

    return pl.pallas_call(
        body, name=name,
        out_shape=jax.ShapeDtypeStruct((N_DEV, *v.shape), v.dtype),
        in_specs=[_VMEM_SPEC], out_specs=_VMEM_SPEC,
        scratch_shapes=[pltpu.SemaphoreType.DMA((N_DEV - 1,)), pltpu.SemaphoreType.DMA((N_DEV - 1,)), pltpu.SemaphoreType.DMA],
    )(v)


def _mod_exchange(modp, name):
    _, L, Nc = modp.shape

    def body(p_ref, out_ref, send_sems, recv_sems, local_sem):
        x, y, c = _idx()
        me, chip = 4 * x + 2 * y + c, 2 * x + y
        mine = pltpu.make_async_copy(p_ref.at[me], out_ref.at[chip], local_sem)
        mine.start()
        sends = []
        for k in range(1, N_CHIPS):
            px, py = _flip(x, k & 2), _flip(y, k & 1)
            cp = pltpu.make_async_remote_copy(src_ref=p_ref.at[4 * px + 2 * py + c], dst_ref=out_ref.at[chip],
                                              send_sem=send_sems.at[k - 1], recv_sem=recv_sems.at[k - 1], device_id=(px, py, c), device_id_type=MESH)
            cp.start()
            sends.append(cp)
        for k in range(1, N_CHIPS):
            px, py = _flip(x, k & 2), _flip(y, k & 1)
            pltpu.make_async_remote_copy(src_ref=p_ref.at[me], dst_ref=out_ref.at[2 * px + py], send_sem=send_sems.at[k - 1],
                                         recv_sem=recv_sems.at[k - 1], device_id=(px, py, c), device_id_type=MESH).wait_recv()
        for cp in sends:
            cp.wait_send()
        mine.wait()

    return pl.pallas_call(
        body, name=name,
        out_shape=jax.ShapeDtypeStruct((N_CHIPS, L, Nc), modp.dtype),
        in_specs=[_VMEM_SPEC], out_specs=_VMEM_SPEC,
        scratch_shapes=[pltpu.SemaphoreType.DMA((N_CHIPS - 1,)), pltpu.SemaphoreType.DMA((N_CHIPS - 1,)), pltpu.SemaphoreType.DMA],
    )(modp)


_SEM_SPEC = pl.BlockSpec(memory_space=pltpu.SEMAPHORE)
_ANY_SPEC = pl.BlockSpec(memory_space=pl.ANY)
_EFFECT = pltpu.SideEffectType.DATAFLOW_SIDE_EFFECTING


def _hbm(a):
    return pltpu.with_memory_space_constraint(a, pltpu.HBM)


def _xchip_copies(mode, srcs, lands, send_sems, recv_sems, waiting):
    x, y, c = _idx()
    chip = 2 * x + y
    out = []
    for a in range(len(srcs)):
        for k in range(1, _n_peers(mode) + 1):
            if mode == "all8":
                px, py, pc = _flip(x, k & 4), _flip(y, k & 2), _flip(c, k & 1)
                src, dst, mine = srcs[a], lands[a].at[4 * x + 2 * y + c], lands[a].at[4 * px + 2 * py + pc]
            elif mode == "scatter8":
                px, py, pc = _flip(x, k & 4), _flip(y, k & 2), _flip(c, k & 1)
                src, dst, mine = srcs[a].at[pc, 2 * px + py], lands[a].at[k - 1], lands[a].at[k - 1]
            else:
                px, py, pc = _flip(x, k & 2), _flip(y, k & 1), c
                peer = 2 * px + py
                if mode == "gather":
                    src, dst, mine = srcs[a].at[c], lands[a].at[chip, c], lands[a].at[peer, c]
                else:
                    src, dst, mine = srcs[a].at[peer], lands[a].at[k - 1], lands[a].at[k - 1]
            q = a * _n_peers(mode) + k - 1
            out.append(pltpu.make_async_remote_copy(src_ref=src, dst_ref=mine if waiting else dst, send_sem=send_sems[q], recv_sem=recv_sems[q],
                                                    device_id=(px, py, pc), device_id_type=MESH))
    return out


def _n_peers(mode):
    return N_DEV - 1 if mode in ("all8", "scatter8") else N_CHIPS - 1


def _xchip_start(mode, srcs, land_shapes, dep, name):
    n = len(srcs)
    ns = n * _n_peers(mode)

    def body(*refs):
        src_refs, land_refs = refs[:n], refs[n:2 * n]
        outs = refs[2 * n + 1:]
        for cp in _xchip_copies(mode, src_refs, land_refs, outs[:ns], outs[ns:2 * ns], waiting=False):
            cp.start()
        outs[-1][...] = jnp.zeros_like(outs[-1])

    lands = [_hbm(lax.empty(s.shape, s.dtype)) for s in land_shapes]
    outs = pl.pallas_call(
        body, name=name,
        out_shape=(*[pltpu.SemaphoreType.DMA(())] * (2 * ns), *[pltpu.HBM(s.shape, s.dtype) for s in srcs],
                   *[pltpu.HBM(s.shape, s.dtype) for s in land_shapes], jax.ShapeDtypeStruct((8, 128), F32)),
        in_specs=[_HBM_SPEC] * (2 * n) + [_ANY_SPEC],
        out_specs=(*[_SEM_SPEC] * (2 * ns), *[_HBM_SPEC] * (2 * n), _VMEM_SPEC),
        input_output_aliases={i: 2 * ns + i for i in range(2 * n)},
        compiler_params=pltpu.CompilerParams(has_side_effects=_EFFECT),
    )(*[_hbm(s) for s in srcs], *lands, dep)
    return list(outs[:ns]), list(outs[ns:2 * ns]), list(outs[2 * ns:2 * ns + n]), list(outs[2 * ns + n:2 * ns + 2 * n]), outs[-1]


def _xchip_wait(mode, send_sems, recv_sems, srcs, lands, after, name):
    n = len(srcs)
    ns = n * _n_peers(mode)

    def body(*refs):
        src_refs, land_refs = refs[:n], refs[n:2 * n]
        sems = refs[2 * n:2 * n + 2 * ns]
        for cp in _xchip_copies(mode, src_refs, land_refs, sems[:ns], sems[ns:], waiting=True):
            cp.wait_send()
            cp.wait_recv()

    outs = pl.pallas_call(
        body, name=name,
        out_shape=(*[pltpu.HBM(s.shape, s.dtype) for s in srcs], *[pltpu.HBM(s.shape, s.dtype) for s in lands]),
        in_specs=[_HBM_SPEC] * (2 * n) + [_SEM_SPEC] * (2 * ns) + [_ANY_SPEC] * len(after),
        out_specs=tuple([_HBM_SPEC] * (2 * n)),
        input_output_aliases={i: i for i in range(2 * n)},
        compiler_params=pltpu.CompilerParams(has_side_effects=_EFFECT),
    )(*srcs, *lands, *send_sems, *recv_sems, *after)
    return list(outs[:n]), list(outs[n:])


def _sibling_fwd(lands, name):
    n = len(lands)

    def body(*refs):
        outs = refs[n:2 * n]
        send_sems, recv_sems = refs[2 * n:]
        x, y, c = _idx()
        sib = (x, y, 1 - c)
        sends = []
        for a in range(n):
            for k in range(1, N_CHIPS):
                src = 2 * _flip(x, k & 2) + _flip(y, k & 1)
                cp = pltpu.make_async_remote_copy(src_ref=outs[a].at[src, c], dst_ref=outs[a].at[src, c], send_sem=send_sems.at[a, k - 1],
                                                  recv_sem=recv_sems.at[a, k - 1], device_id=sib, device_id_type=MESH)
                cp.start()
                sends.append(cp)
        for a in range(n):
            for k in range(1, N_CHIPS):
                src = 2 * _flip(x, k & 2) + _flip(y, k & 1)
                pltpu.make_async_remote_copy(src_ref=outs[a].at[src, c], dst_ref=outs[a].at[src, 1 - c], send_sem=send_sems.at[a, k - 1],
                                             recv_sem=recv_sems.at[a, k - 1], device_id=sib, device_id_type=MESH).wait_recv()
        for cp in sends:
            cp.wait_send()

    return pl.pallas_call(
        body, name=name,
        out_shape=[jax.ShapeDtypeStruct(s.shape, s.dtype) for s in lands],
        in_specs=[_HBM_SPEC] * n, out_specs=[_HBM_SPEC] * n,
        input_output_aliases={i: i for i in range(n)},
        scratch_shapes=[pltpu.SemaphoreType.DMA((n, N_CHIPS - 1)), pltpu.SemaphoreType.DMA((n, N_CHIPS - 1))],
    )(*lands)


def _sibling_send(halves, name):
    n = len(halves)

    def body(*refs):
        ins, outs = refs[:n], refs[n:2 * n]
        send_sems, recv_sems = refs[2 * n:]
        x, y, c = _idx()
        cps = []
        for a in range(n):
            cp = pltpu.make_async_remote_copy(src_ref=ins[a], dst_ref=outs[a], send_sem=send_sems.at[a], recv_sem=recv_sems.at[a],
                                              device_id=(x, y, 1 - c), device_id_type=MESH)
            cp.start()
            cps.append(cp)
        for cp in cps:
            cp.wait()

    return pl.pallas_call(
        body, name=name,
        out_shape=[jax.ShapeDtypeStruct(h.shape, h.dtype) for h in halves],
        in_specs=[_HBM_SPEC] * n, out_specs=[_HBM_SPEC] * n,
        scratch_shapes=[pltpu.SemaphoreType.DMA((n,)), pltpu.SemaphoreType.DMA((n,))],
    )(*halves)


def _col_full(g):
    k, n = g.shape[1], g.shape[2]
    return g.transpose(1, 0, 2).reshape(k, N_CHIPS * n)


def _col_blocks(w):
    k, n = w.shape
    return w.reshape(k, N_CHIPS, n // N_CHIPS).transpose(1, 0, 2)


def _row_blocks(w):
    k, n = w.shape
    return w.reshape(N_CHIPS, k // N_CHIPS, n)


_UQ_HEAD = MLA_NOPE + MLA_ROPE

_LAT = MLA_QL + MLA_KVL + MLA_ROPE
_POOL_R = len(POOL_WINDOWS) * (POOL_GD // N_CHIPS)

_PIECE_KINDS = {
    "mlp_w1": (D_MODEL, D_MODEL, lambda g: g, _col_blocks),
    "mlp_w2": (D_MODEL, D_MODEL, lambda g: g.reshape(4 * D_MODEL, D_MODEL), _row_blocks),
    "pool_w": (_POOL_R, POOL_GD,
               lambda g: g.reshape(N_CHIPS, len(POOL_WINDOWS), POOL_GD // N_CHIPS, POOL_GD).transpose(1, 0, 2, 3).reshape(len(POOL_WINDOWS), POOL_GD, POOL_GD),
               lambda w: w.reshape(len(POOL_WINDOWS), N_CHIPS, POOL_GD // N_CHIPS, POOL_GD).transpose(1, 0, 2, 3).reshape(N_CHIPS, _POOL_R, POOL_GD)),
    "sgu_w_in": (D_MODEL, 2 * SGU_W // N_CHIPS, _col_full, _col_blocks),
    "sgu_w_out": (SGU_W // N_CHIPS, D_MODEL, lambda g: g.reshape(SGU_W, D_MODEL), _row_blocks),
    "mla_w_dq_dkv": (D_MODEL // N_CHIPS, _LAT, lambda g: jnp.pad(g.reshape(D_MODEL, _LAT), ((0, 0), (0, MLA_LATP - _LAT))),
                     lambda w: _row_blocks(w[:, :_LAT])),
    "mla_w_uq": (MLA_QL, MLA_H * _UQ_HEAD // N_CHIPS,
                 lambda g: jnp.pad(_col_full(g).reshape(MLA_QL, MLA_H, _UQ_HEAD), ((0, 0), (0, 0), (0, MLA_HP - _UQ_HEAD))).reshape(MLA_QL, MLA_H * MLA_HP),
                 lambda w: _col_blocks(w.reshape(MLA_QL, MLA_H, MLA_HP)[:, :, :_UQ_HEAD].reshape(MLA_QL, MLA_H * _UQ_HEAD))),
    "mla_w_ukv": (MLA_KVL, MLA_H * (MLA_NOPE + MLA_V) // N_CHIPS, _col_full, _col_blocks),
    "mla_w_o": (MLA_H * MLA_V // N_CHIPS, D_MODEL, lambda g: g.reshape(MLA_H * MLA_V, D_MODEL), _row_blocks),
}
_MIXER_KINDS = (("pool_w",), ("sgu_w_in", "sgu_w_out"), ("mla_w_dq_dkv", "mla_w_uq", "mla_w_ukv", "mla_w_o"))


def _layer_pieces(i):
    return [(k, i // N_MIXERS) for k in _MIXER_KINDS[i % N_MIXERS]] + [("mlp_w1", i), ("mlp_w2", i)]


def _rope_tables(positions):
    inv_freq = ROPE_THETA ** (-jnp.arange(0, MLA_ROPE, 2, dtype=F32) / MLA_ROPE)
    ang = positions.astype(F32)[:, None] * inv_freq
    cos, sin = jnp.cos(ang), jnp.sin(ang)
    z32, z64 = jnp.zeros_like(cos), jnp.zeros((positions.shape[0], 64), F32)
    return (jnp.concatenate([cos, cos, z64], axis=1), jnp.concatenate([-sin, z32, z64], axis=1), jnp.concatenate([z32, sin, z64], axis=1))


def _local_step(x, positions, target, mod, S, weights_of, grads_of):
    D = D_MODEL
    cc, sa, sb = _rope_tables(positions)
    mods = [[mod[i:i + 1, n * D:(n + 1) * D] for n in range(6)] for i in range(DEPTH)]
    h_dtype = lambda i: F32 if i % N_MIXERS == 0 else BF16
    saved = []
    h = _norm_mod_fwd(x, S["norm_mix_g"][0:1], mods[0][1], mods[0][0], h_dtype(0), "l0_norm1")
    for i in range(DEPTH):
        sh1, sc1, g1, sh2, sc2, g2 = mods[i]
        kind, j = i % N_MIXERS, i // N_MIXERS
        gmlp = S["norm_mlp_g"][i:i + 1]
        W = weights_of(i, "mix", x)
        st = {"x": x}
        norm2 = ((gmlp, "n"), (sc2, "n"), (sh2, "n"))
        if kind == 0:
            x2, pooled, ypre, h2 = _pool_fwd(h, W["pool_w"], S["pool_scale"][j:j + 1], x, g1, gmlp, sc2, sh2, f"l{i}_pool")
            st.update(pooled=pooled, y=ypre)
        elif kind == 1:
            zz = _mm(h, W["sgu_w_in"], out_dtypes=(F32,), name=f"l{i}_sgu_in")
            bs_t = S["sgu_b_s"].T
            gated = _sgu_gate_fwd(zz, S["sgu_ln_g"], S["sgu_ln_b"], S["sgu_w_s"], bs_t, f"l{i}_sgu_gate")
            x2, y, h2 = _mm(gated, W["sgu_w_out"], epi=_epi_residual_norm, extras=((x, "mn"), (g1, "n"), *norm2), out_dtypes=(F32, BF16, BF16),
                            tn=D, name=f"l{i}_sgu_out")
            st.update(h=h, zz=zz, gated=gated, y=y, bs_t=bs_t)
        else:
            lat = _mm(h, W["mla_w_dq_dkv"], out_dtypes=(F32,), name=f"l{i}_mla_lat")
            cqn, ckvn, krot = _mla_lat_fwd(lat, S["mla_q_norm_g"], S["mla_kv_norm_g"], cc, sa, sb, f"l{i}_mla_latn")
            q = _mm(cqn, W["mla_w_uq"], epi=_epi_q_rope, extras=((cc, "m"), (sa, "m"), (sb, "m")), name=f"l{i}_mla_uq")
            k, kt, v, vt = _mla_ukv(ckvn, W["mla_w_ukv"], krot, f"l{i}_mla_ukv")
            o, lse = _attn_fwd(q, k, vt, f"l{i}_attn")
            x2, y, h2 = _mm(o, W["mla_w_o"], epi=_epi_residual_norm, extras=((x, "mn"), (g1, "n"), *norm2), out_dtypes=(F32, BF16, BF16),
                            tn=D, name=f"l{i}_mla_o")
            st.update(h=h, lat=lat, cqn=cqn, ckvn=ckvn, q=q, k=k, kt=kt, v=v, o=o, lse=lse, y=y)
        W = {**W, **weights_of(i, "mlp", x2)}
        z, r2 = _mm(h2, W["mlp_w1"], epi=_epi_sq_relu, out_dtypes=(BF16, BF16), epi_cols=MM_EPI_COLS, tm=MM_TM_WIDE, name=f"l{i}_mlp1")
        W = {**W, **weights_of(i, "mlp2", z)}
        if i + 1 < DEPTH:
            norm1 = ((S["norm_mix_g"][i + 1:i + 2], "n"), (mods[i + 1][1], "n"), (mods[i + 1][0], "n"))
            x3, o2, h = _mm(z, W["mlp_w2"], epi=_epi_residual_norm, extras=((x2, "mn"), (g2, "n"), *norm1), out_dtypes=(F32, BF16, h_dtype(i + 1)),
                            tn=D, name=f"l{i}_mlp2")
        else:
            x3, o2 = _mm(z, W["mlp_w2"], epi=_epi_residual, extras=((x2, "mn"), (g2, "n")), out_dtypes=(F32, BF16), name=f"l{i}_mlp2")
        st.update(x2=x2, h2=h2, z=z, r2=r2, o2=o2, W=W)
        saved.append(st)
        x = x3

    loss, dx, dfinal_g, do2, dg2 = _loss_head(x, target, S["final_g"], saved[-1]["o2"], mods[-1][5], "loss_head")

    gS = {"final_g": dfinal_g, "norm_mix_g": [None] * DEPTH, "norm_mlp_g": [None] * DEPTH, "pool_scale": [None] * 2}
    dmod = [None] * DEPTH
    started = None
    for i in reversed(range(DEPTH)):
        st = saved[i]
        W, gW = st["W"], {}
        sh1, sc1, g1, sh2, sc2, g2 = mods[i]
        kind, j = i % N_MIXERS, i // N_MIXERS
        gmix, gmlp = S["norm_mix_g"][i:i + 1], S["norm_mlp_g"][i:i + 1]
        da = _mm(do2, W["mlp_w2"], tb=True, epi=lambda acc, rt: (acc * rt.astype(F32),), extras=((st["r2"], "mn"),), after=started, epi_cols=MM_EPI_COLS,
                 tm=MM_TM_WIDE, name=f"l{i}_b_dz")
        gW["mlp_w2"] = _mm(st["z"], do2, ta=True, chip_blocks="row", name=f"l{i}_b_dw2")
        dh2 = _mm(da, W["mlp_w1"], tb=True, name=f"l{i}_b_dh2")
        gW["mlp_w1"] = _mm(st["h2"], da, ta=True, chip_blocks="col", name=f"l{i}_b_dw1")
        dx2, dgmlp, dsc2, dsh2, dy, q1 = _norm_mod_bwd(st["x2"], dh2, dx, gmlp, sc2, f"l{i}_b_norm2", res=(st["y"], g1))
        gS["norm_mlp_g"][i] = dgmlp
        if kind == 0:
            dh, dpw, dpsc, dg1 = _pool_bwd(dy, st["pooled"], W["pool_w"], S["pool_scale"][j:j + 1], g1, q1, f"l{i}_b_pool")
            gW["pool_w"] = dpw.astype(BF16)
            gS["pool_scale"][j] = dpsc
        elif kind == 1:
            dg1 = q1
            dgated = _mm(dy, W["sgu_w_out"], tb=True, name=f"l{i}_b_dgated")
            gW["sgu_w_out"] = _mm(st["gated"], dy, ta=True, name=f"l{i}_b_dwout")
            dzz, dws, dbs, dlg, dlb = _sgu_gate_bwd(st["zz"], dgated, S["sgu_ln_g"], S["sgu_ln_b"], S["sgu_w_s"], st["bs_t"], f"l{i}_b_sgu_gate")
            gS.update(sgu_w_s=dws, sgu_b_s=dbs[:, :, 0], sgu_ln_g=dlg, sgu_ln_b=dlb)
            dh = _mm(dzz, W["sgu_w_in"], tb=True, name=f"l{i}_b_dh_sgu")
            gW["sgu_w_in"] = _mm(st["h"], dzz, ta=True, name=f"l{i}_b_dwin")
        else:
            dg1 = q1
            do = _mm(dy, W["mla_w_o"], tb=True, name=f"l{i}_b_do")
            gW["mla_w_o"] = _mm(st["o"], dy, ta=True, name=f"l{i}_b_dwo")
            delta = _attn_delta(do, st["o"], f"l{i}_b_delta")
            dqt, dkv, dkr = _attn_bwd(st["q"], st["k"], st["kt"], st["v"], do, st["lse"], delta, f"l{i}_b_attn")
            dqpad, dkrot = _mla_prep_bwd(dqt, dkr, cc, sa, sb, f"l{i}_b_mla_prep")
            dcqn = _mm(dqpad, W["mla_w_uq"], tb=True, out_dtypes=(F32,), name=f"l{i}_b_dcq")
            gW["mla_w_uq"] = _mm(st["cqn"], dqpad, ta=True, name=f"l{i}_b_dwuq")
            dckvn = _mm(dkv, W["mla_w_ukv"], tb=True, out_dtypes=(F32,), name=f"l{i}_b_dckv")
            gW["mla_w_ukv"] = _mm(st["ckvn"], dkv, ta=True, name=f"l{i}_b_dwukv")
            dlat, dqg, dkvg = _mla_lat_bwd(st["lat"], dcqn, dckvn, dkrot, S["mla_q_norm_g"], S["mla_kv_norm_g"], cc, sa, sb, f"l{i}_b_mla_latn")
            gS.update(mla_q_norm_g=dqg, mla_kv_norm_g=dkvg)
            dh = _mm(dlat, W["mla_w_dq_dkv"], tb=True, name=f"l{i}_b_dh_mla")
            gW["mla_w_dq_dkv"] = _mm(st["h"], dlat, ta=True, name=f"l{i}_b_dwdq")
        if i > 0:
            dx, dgmix, dsc1, dsh1, do2_prev, dg2_prev = _norm_mod_bwd(st["x"], dh, dx2, gmix, sc1, f"l{i}_b_norm1", res=(saved[i - 1]["o2"], mods[i - 1][5]))
        else:
            dx, dgmix, dsc1, dsh1 = _norm_mod_bwd(st["x"], dh, dx2, gmix, sc1, f"l{i}_b_norm1")
        gS["norm_mix_g"][i] = dgmix
        dmod[i] = jnp.concatenate([dsh1, dsc1, dg1, dsh2, dsc2, dg2], axis=1)
        started = grads_of(i, gW, dx)
        if i > 0:
            do2, dg2 = do2_prev, dg2_prev

    for n in ("norm_mix_g", "norm_mlp_g", "pool_scale"):
        gS[n] = jnp.concatenate(gS[n], axis=0)
    return loss, dx, gS, jnp.concatenate(dmod, axis=0)


_SMALL = {
    "norm_mix_g": (DEPTH, D_MODEL), "norm_mlp_g": (DEPTH, D_MODEL), "sgu_ln_g": (1, SGU_W), "sgu_ln_b": (1, SGU_W),
    "sgu_w_s": (SGU_H, SGU_CHUNK, SGU_CHUNK), "sgu_b_s": (SGU_H, SGU_CHUNK), "mla_kv_norm_g": (1, MLA_KVL), "final_g": (1, D_MODEL),
    "pool_scale": (2, D_MODEL), "mla_q_norm_g": (1, MLA_QL), "loss": (1, 128), "dmod": (DEPTH, 6 * D_MODEL),
}
_PACK_W = 1024


def _pack(vals):
    flat = jnp.concatenate([v.reshape(-1) for v in vals])
    rows = -(-flat.shape[0] // (8 * _PACK_W)) * 8
    return jnp.pad(flat, (0, rows * _PACK_W - flat.shape[0])).reshape(rows, _PACK_W)


def _unpack(buf, shapes):
    flat, out, off = buf.reshape(-1), [], 0
    for s in shapes:
        n = math.prod(s)
        out.append(flat[off:off + n].reshape(s))
        off += n
    return out


def kernel(x, c, positions, ada_w, ada_b, norm_mix_g, norm_mlp_g, pool_w, pool_scale, sgu_w_in, sgu_ln_g, sgu_ln_b, sgu_w_s, sgu_b_s, sgu_w_out, mla_w_dq_dkv, mla_q_norm_g, mla_kv_norm_g, mla_w_uq, mla_w_ukv, mla_w_o, mlp_w1, mlp_w2, final_g, loss_target, m_ada_w, m_ada_b, m_norm_mix_g, m_norm_mlp_g, m_pool_w, m_pool_scale, m_sgu_w_in, m_sgu_ln_g, m_sgu_ln_b, m_sgu_w_s, m_sgu_b_s, m_sgu_w_out, m_mla_w_dq_dkv, m_mla_q_norm_g, m_mla_kv_norm_g, m_mla_w_uq, m_mla_w_ukv, m_mla_w_o, m_mlp_w1, m_mlp_w2, m_final_g, v_ada_w, v_ada_b, v_norm_mix_g, v_norm_mlp_g, v_pool_w, v_pool_scale, v_sgu_w_in, v_sgu_ln_g, v_sgu_ln_b, v_sgu_w_s, v_sgu_b_s, v_sgu_w_out, v_mla_w_dq_dkv, v_mla_q_norm_g, v_mla_kv_norm_g, v_mla_w_uq, v_mla_w_ukv, v_mla_w_o, v_mlp_w1, v_mlp_w2, v_final_g):
    P = dict(ada_w=ada_w, ada_b=ada_b, norm_mix_g=norm_mix_g, norm_mlp_g=norm_mlp_g, pool_w=pool_w, pool_scale=pool_scale, sgu_w_in=sgu_w_in,
             sgu_ln_g=sgu_ln_g, sgu_ln_b=sgu_ln_b, sgu_w_s=sgu_w_s, sgu_b_s=sgu_b_s, sgu_w_out=sgu_w_out, mla_w_dq_dkv=mla_w_dq_dkv,
             mla_q_norm_g=mla_q_norm_g, mla_kv_norm_g=mla_kv_norm_g, mla_w_uq=mla_w_uq, mla_w_ukv=mla_w_ukv, mla_w_o=mla_w_o, mlp_w1=mlp_w1,
             mlp_w2=mlp_w2, final_g=final_g)
    M = dict(ada_w=m_ada_w, ada_b=m_ada_b, norm_mix_g=m_norm_mix_g, norm_mlp_g=m_norm_mlp_g, pool_w=m_pool_w, pool_scale=m_pool_scale,
             sgu_w_in=m_sgu_w_in, sgu_ln_g=m_sgu_ln_g, sgu_ln_b=m_sgu_ln_b, sgu_w_s=m_sgu_w_s, sgu_b_s=m_sgu_b_s, sgu_w_out=m_sgu_w_out,
             mla_w_dq_dkv=m_mla_w_dq_dkv, mla_q_norm_g=m_mla_q_norm_g, mla_kv_norm_g=m_mla_kv_norm_g, mla_w_uq=m_mla_w_uq, mla_w_ukv=m_mla_w_ukv,
             mla_w_o=m_mla_w_o, mlp_w1=m_mlp_w1, mlp_w2=m_mlp_w2, final_g=m_final_g)
    V = dict(ada_w=v_ada_w, ada_b=v_ada_b, norm_mix_g=v_norm_mix_g, norm_mlp_g=v_norm_mlp_g, pool_w=v_pool_w, pool_scale=v_pool_scale,
             sgu_w_in=v_sgu_w_in, sgu_ln_g=v_sgu_ln_g, sgu_ln_b=v_sgu_ln_b, sgu_w_s=v_sgu_w_s, sgu_b_s=v_sgu_b_s, sgu_w_out=v_sgu_w_out,
             mla_w_dq_dkv=v_mla_w_dq_dkv, mla_q_norm_g=v_mla_q_norm_g, mla_kv_norm_g=v_mla_kv_norm_g, mla_w_uq=v_mla_w_uq, mla_w_ukv=v_mla_w_ukv,
             mla_w_o=v_mla_w_o, mlp_w1=v_mlp_w1, mlp_w2=v_mlp_w2, final_g=v_final_g)
    order = list(P)
    xi, yi, ci = _idx()
    chip = 2 * xi + yi
    D = D_MODEL
    n_ada = ada_w.shape[2]

    pre = _allgather8(_pack([c, pool_scale, mla_q_norm_g]), "ag_small")
    flat = pre.reshape(N_DEV, -1)
    c_all = flat[:, :D]
    ps_all = flat[0::2, D:D + 2 * (D // N_CHIPS)].reshape(N_CHIPS, 2, D // N_CHIPS).transpose(1, 0, 2).reshape(2, D)
    q0 = D + 2 * (D // N_CHIPS)
    qg_all = flat[0::2, q0:q0 + MLA_QL // N_CHIPS].reshape(1, MLA_QL)

    ada_b_loc = lax.dynamic_slice_in_dim(ada_b, chip * n_ada, n_ada, axis=1)[:, None, :]
    modp = _ada_fwd(c_all, ada_w, ada_b_loc, "ada_fwd")
    mod = _mod_exchange(modp.transpose(1, 0, 2), "mod_exchange").transpose(1, 0, 2).reshape(DEPTH, 6 * D)

    S = dict(norm_mix_g=norm_mix_g, norm_mlp_g=norm_mlp_g, pool_scale=ps_all, sgu_ln_g=sgu_ln_g, sgu_ln_b=sgu_ln_b, sgu_w_s=sgu_w_s[0],
             sgu_b_s=sgu_b_s[0], mla_q_norm_g=qg_all, mla_kv_norm_g=mla_kv_norm_g, final_g=final_g[None, :])
    cidx, ownidx = jnp.reshape(ci, (1,)).astype(jnp.int32), jnp.reshape(N_CHIPS * ci + chip, (1,)).astype(jnp.int32)
    view2d = lambda a: a.reshape(-1, a.shape[-1])

    def piece_rows(kind, blk):
        r = _PIECE_KINDS[kind][0]
        return blk * r, r

    groups = [_layer_pieces(0)[:-2], _layer_pieces(0)[-2:-1], _layer_pieces(0)[-1:], _layer_pieces(1)[:-2], _layer_pieces(1)[-2:],
              _layer_pieces(2), _layer_pieces(3)]
    start_after = {1: (3, 4), 3: (5,), 5: (6,)}
    gathers = {}

    def gather_start(g, dep):
        srcs, shapes = [], []
        for kind, blk in groups[g]:
            r0, r = piece_rows(kind, blk)
            cdim = _PIECE_KINDS[kind][1]
            srcs.append(view2d(P[kind])[r0:r0 + r].astype(BF16).reshape(2, r // 2, cdim))
            shapes.append(jax.ShapeDtypeStruct((N_CHIPS, 2, r // 2, cdim), BF16))
        gathers[g] = _xchip_start("gather", srcs, shapes, dep, f"ag_start_g{g}")

    def gather_finish(g, after):
        ssem, rsem, srcs, lands, _ = gathers.pop(g)
        deps = [after]
        for nxt in start_after.get(g, ()):
            gather_start(nxt, deps[-1])
            deps.append(gathers[nxt][-1])
        srcs, lands = _xchip_wait("gather", ssem, rsem, srcs, lands, deps, f"ag_wait_g{g}")
        lands = _sibling_fwd(lands, f"ag_sibling_g{g}")
        W = {}
        for (kind, _), s, land in zip(groups[g], srcs, lands, strict=True):
            r, cdim, to_full, _ = _PIECE_KINDS[kind]
            W[kind] = to_full(lax.dynamic_update_index_in_dim(land, s, chip, 0).reshape(N_CHIPS, r, cdim))
        return W

    def weights_of(i, part, x_i):
        g = {(0, "mix"): 0, (0, "mlp"): 1, (0, "mlp2"): 2, (1, "mix"): 3, (1, "mlp"): 4, (2, "mix"): 5, (3, "mix"): 6}.get((i, part))
        return {} if g is None else gather_finish(g, x_i)

    scatters = {}
    bufs = {n: tuple(lax.empty(view2d(P[n]).shape, F32) for _ in range(4)) for n in _PIECE_KINDS}

    def scatter_start(i, gW, dep):
        pcs = _layer_pieces(i)
        blocked = []
        for kind, _ in pcs:
            r, cdim, _, to_blocks = _PIECE_KINDS[kind]
            g = gW[kind]
            blocked.append(g if g.ndim == 4 else to_blocks(g).reshape(N_CHIPS, 2, r // 2, cdim).transpose(1, 0, 2, 3))
        shapes = [jax.ShapeDtypeStruct((N_DEV - 1, *b.shape[2:]), BF16) for b in blocked]
        scatters[i] = (pcs, *_xchip_start("scatter8", blocked, shapes, dep, f"rs_start_l{i}"))
        return scatters[i][-1]

    def scatter_finish(i, after):
        pcs, ssem, rsem, blocked, lands, _ = scatters.pop(i)
        blocked, lands = _xchip_wait("scatter8", ssem, rsem, blocked, lands, after, f"rs_wait_l{i}")
        halves = [_sum_sel(ownidx, b.reshape(2 * N_CHIPS, *b.shape[2:]), [l], f"rs_sum_l{i}_{kind}", F32)
                  for (kind, _), b, l in zip(pcs, blocked, lands, strict=True)]
        got = _sibling_send(halves, f"rs_merge_l{i}")
        for (kind, blk), mine, other in zip(pcs, halves, got, strict=True):
            r0, _ = piece_rows(kind, blk)
            bufs[kind] = tuple(_adamw_piece(cidx, view2d(P[kind]), view2d(M[kind]), view2d(V[kind]), mine, other, bufs[kind], r0,
                                            f"adamw_l{i}_{kind}"))
        return lands[0]

    first_layer = {}

    def grads_of(i, gW, dx_i):
        if i == 0:
            first_layer.update(gW)
            return None
        dep = scatter_finish(i + 1, [dx_i]) if i + 1 in scatters else dx_i
        return scatter_start(i, gW, dep)

    gather_start(0, mod)
    gather_start(1, gathers[0][-1])
    gather_start(2, gathers[1][-1])
    mod = mod + gathers[2][-1][0, 0]
    loss_l, dx, gS, dmod = _local_step(x[0], positions[0], loss_target[0], mod, S, weights_of, grads_of)

    gS["dmod"] = dmod
    gS["loss"] = loss_l
    packed = _pack([gS[n] for n in _SMALL])
    sg = _xchip_start("all8", [packed], [jax.ShapeDtypeStruct((N_DEV, *packed.shape), F32)], dx, "sg_start")
    tok0 = scatter_start(0, first_layer, sg[-1])[0, 0]
    scatter_finish(1, [dx, scatters[0][-1]])
    sg_src, sg_land = _xchip_wait("all8", sg[0], sg[1], sg[2], sg[3], [bufs[n][0] for n in ("mlp_w1", "mlp_w2", "sgu_w_in", "sgu_w_out")], "sg_wait")
    small = lax.dynamic_update_index_in_dim(sg_land[0], sg_src[0], 4 * xi + 2 * yi + ci, 0) + tok0
    small_sum = _unpack(_sum_lead([small], "sum_small_grads"), list(_SMALL.values()))
    G = dict(zip(_SMALL, small_sum, strict=True))
    grads = {
        "ada_b": G["dmod"], "norm_mix_g": G["norm_mix_g"], "norm_mlp_g": G["norm_mlp_g"], "sgu_ln_g": G["sgu_ln_g"], "sgu_ln_b": G["sgu_ln_b"],
        "sgu_w_s": G["sgu_w_s"][None], "sgu_b_s": G["sgu_b_s"][None], "mla_kv_norm_g": G["mla_kv_norm_g"], "final_g": G["final_g"][0],
        "pool_scale": lax.dynamic_slice_in_dim(G["pool_scale"], chip * (D // N_CHIPS), D // N_CHIPS, axis=1),
        "mla_q_norm_g": lax.dynamic_slice_in_dim(G["mla_q_norm_g"], chip * (MLA_QL // N_CHIPS), MLA_QL // N_CHIPS, axis=1),
    }
    dmod_all = _unpack(small, [(N_DEV,) + (small.shape[1] * _PACK_W,)])[0]
    off = sum(math.prod(s) for n, s in _SMALL.items() if n != "dmod")
    dmod_all = dmod_all[:, off:off + DEPTH * 6 * D].reshape(N_DEV, DEPTH, 6 * D)
    dmod_loc = lax.dynamic_slice_in_dim(dmod_all, chip * n_ada, n_ada, axis=2).transpose(1, 0, 2)
    deltas, new_m, new_v = {}, {}, {}
    grads["ada_w"], deltas["ada_w"], new_m["ada_w"], new_v["ada_w"] = _ada_bwd_adamw(c_all.T, dmod_loc, ada_w, m_ada_w, v_ada_w, "adamw_ada_w")
    for n in order:
        if n not in _PIECE_KINDS and n != "ada_w":
            deltas[n], new_m[n], new_v[n] = _adamw(P[n], grads[n].reshape(P[n].shape), M[n], V[n], f"adamw_{n}")
    scatter_finish(0, [deltas["ada_w"], deltas["sgu_w_s"]] + [bufs[n][0] for n in ("mlp_w1", "mlp_w2", "sgu_w_in", "mla_w_o")])
    for n in _PIECE_KINDS:
        grads[n], deltas[n], new_m[n], new_v[n] = (b.reshape(P[n].shape) for b in bufs[n])
    return (G["loss"][0, 0], dx[None], *[grads[n].reshape(P[n].shape) for n in order], *[deltas[n] for n in order], *[new_m[n] for n in order],
            *[new_v[n] for n in order])
```

```python
import math

import jax
import jax.numpy as jnp
from jax import lax
from jax.experimental import pallas as pl
from jax.experimental.pallas import tpu as pltpu

F32, BF16 = jnp.float32, jnp.bfloat16
MESH = pl.DeviceIdType.MESH

D_MODEL = 1024
DEPTH = 4
N_MIXERS = 3
POOL_WINDOWS = (2, 4, 8, 16)
POOL_GD = D_MODEL // len(POOL_WINDOWS)
POOL_HALO = 16
SGU_CHUNK = 128
SGU_W = D_MODEL
SGU_HD = 128
SGU_H = SGU_W // SGU_HD
MLA_H = 16
MLA_QL = 256
MLA_KVL = 128
MLA_NOPE = 128
MLA_ROPE = 64
MLA_V = 128
MLA_HP = 256
MLA_LATP = 512
ROPE_THETA = 10000.0
RMS_EPS = 1e-6
LN_EPS = 1e-5
SM_SCALE = (MLA_NOPE + MLA_ROPE) ** -0.5
NEG_INF = -1e30
ADAM_LR, ADAM_B1, ADAM_B2, ADAM_EPS, ADAM_WD, ADAM_STEP = 0.001, 0.9, 0.999, 1e-08, 0.01, 10
N_CHIPS = 4
N_DEV = 8
ROW_TILE = 512
ATT_TILE = 512
ATT_SUB = 256
ATT_FWD_HEADS = 4
ATT_BWD_HEADS = 2
MM_EPI_COLS = 256
MM_TM_WIDE = 2048
MM_VMEM_BUDGET = 40 << 20


def _idx():
    return lax.axis_index("x"), lax.axis_index("y"), lax.axis_index("c")


def _mm(a, b, *, name, ta=False, tb=False, epi=None, extras=(), out_dtypes=(BF16,), tm=1024, tn=1024, tk=1024, chip_blocks=None, after=None,
        epi_cols=None):
    if ta:
        K, M = a.shape
    else:
        M, K = a.shape
    b_chips = b.ndim == 3
    if b_chips:
        assert b.shape[0] == N_CHIPS
        Kb, N = (N_CHIPS * b.shape[2], b.shape[1]) if tb else (b.shape[1], N_CHIPS * b.shape[2])
    elif tb:
        N, Kb = b.shape
    else:
        Kb, N = b.shape
    assert K == Kb, (a.shape, b.shape, ta, tb)
    if b_chips and not tb:
        tn = min(tn, N // N_CHIPS)
    if chip_blocks == "col":
        tm, tn = min(tm, M // 2), min(tn, N // N_CHIPS)
    elif chip_blocks == "row":
        tm = min(tm, M // N_CHIPS // 2)
    tm, tn, tk = min(tm, M), min(tn, N), min(tk, K)

    def vmem_bytes(tm_, tk_):
        per_mn = sum(arr.dtype.itemsize for arr, kind in extras if kind == "mn") + sum(jnp.dtype(dt).itemsize for dt in out_dtypes)
        return 2 * (tm_ * tk_ * a.dtype.itemsize + tk_ * tn * b.dtype.itemsize + tm_ * tn * per_mn)

    if vmem_bytes(tm, K) <= MM_VMEM_BUDGET:
        tk = K
    elif tm >= 512 and vmem_bytes(tm // 2, K) <= MM_VMEM_BUDGET:
        tm, tk = tm // 2, K
    assert M % tm == 0 and N % tn == 0 and K % tk == 0, (M, N, K, tm, tn, tk)
    nk = K // tk
    assert epi_cols is None or (nk == 1 and not ta and not (b_chips and tb) and tn % epi_cols == 0)
    a_spec = pl.BlockSpec((tk, tm), lambda i, j, k: (k, i)) if ta else pl.BlockSpec((tm, tk), lambda i, j, k: (i, k))
    b_spec = pl.BlockSpec((tn, tk), lambda i, j, k: (j, k)) if tb else pl.BlockSpec((tk, tn), lambda i, j, k: (k, j))
    if b_chips and tb:
        assert nk == 1 and not ta
        b_spec = pl.BlockSpec((N_CHIPS, tn, K // N_CHIPS), lambda i, j, k: (0, j, 0))
    elif b_chips:
        per = N // N_CHIPS // tn
        b_spec = pl.BlockSpec((None, tk, tn), lambda i, j, k: (j // per, k, j % per))
    ex_specs = []
    for arr, kind in extras:
        if kind == "mn":
            ex_specs.append(pl.BlockSpec((tm, tn), lambda i, j, k: (i, j)))
        elif kind == "n":
            ex_specs.append(pl.BlockSpec((1, tn), lambda i, j, k: (0, j)))
        else:
            ex_specs.append(pl.BlockSpec((tm, arr.shape[1]), lambda i, j, k: (i, 0)))
    n_ex, n_out = len(extras), len(out_dtypes)
    n_in = 2 + n_ex + (after is not None)
    dims = (((0 if ta else 1,), (1 if tb else 0,)), ((), ()))

    def body(*refs):
        a_ref, b_ref = refs[0], refs[1]
        ex_refs = refs[2:2 + n_ex]
        out_refs = refs[n_in:n_in + n_out]
        if b_chips and tb:
            kc = K // N_CHIPS
            part = None
            for cb in range(N_CHIPS):
                p = lax.dot_general(a_ref[:, cb * kc:(cb + 1) * kc].astype(BF16), b_ref[cb].astype(BF16), dims, preferred_element_type=F32)
                part = p if part is None else part + p
        elif epi_cols is not None:
            av = a_ref[...].astype(BF16)
            chunk = lambda cc: lax.dot_general(av, (b_ref[cc * epi_cols:(cc + 1) * epi_cols, :] if tb else b_ref[:, cc * epi_cols:(cc + 1) * epi_cols])
                                               .astype(BF16), dims, preferred_element_type=F32)
            acc = chunk(0)
            for cc in range(tn // epi_cols):
                nxt = chunk(cc + 1) if cc + 1 < tn // epi_cols else None
                cs = slice(cc * epi_cols, (cc + 1) * epi_cols)
                for r, o in zip(out_refs, epi(acc, *[r[:, cs] for r in ex_refs]), strict=True):
                    r[:, cs] = o.astype(r.dtype)
                acc = nxt
            return
        else:
            part = lax.dot_general(a_ref[...].astype(BF16), b_ref[...].astype(BF16), dims, preferred_element_type=F32)

        def finish(acc):
            outs = epi(acc, *[r[...] for r in ex_refs]) if epi is not None else (acc,)
            for r, o in zip(out_refs, outs, strict=True):
                r[...] = o.astype(r.dtype)

        if nk == 1:
            finish(part)
        else:
            acc_ref = refs[-1]
            k = pl.program_id(2)

            @pl.when(k == 0)
            def _():
                acc_ref[...] = part

            @pl.when(k > 0)
            def _():
                acc_ref[...] += part

            @pl.when(k == nk - 1)
            def _():
                finish(acc_ref[...])

    out_specs = [pl.BlockSpec((tm, tn), lambda i, j, k: (i, j)) for _ in range(n_out)]
    out_shape = [jax.ShapeDtypeStruct((M, N), dt) for dt in out_dtypes]
    if chip_blocks is not None:
        assert n_out == 1
        if chip_blocks == "col":
            rh, cb = M // 2 // tm, N // N_CHIPS // tn
            out_specs = [pl.BlockSpec((None, None, tm, tn), lambda i, j, k: (i // rh, j // cb, i % rh, j % cb))]
            out_shape = [jax.ShapeDtypeStruct((2, N_CHIPS, M // 2, N // N_CHIPS), out_dtypes[0])]
        else:
            rh = M // N_CHIPS // 2 // tm
            out_specs = [pl.BlockSpec((None, None, tm, tn), lambda i, j, k: ((i // rh) % 2, i // (2 * rh), i % rh, j))]
            out_shape = [jax.ShapeDtypeStruct((2, N_CHIPS, M // N_CHIPS // 2, N), out_dtypes[0])]
    outs = pl.pallas_call(
        body,
        name=name,
        grid=(M // tm, N // tn, nk),
        in_specs=[a_spec, b_spec, *ex_specs] + ([pl.BlockSpec(memory_space=pl.ANY)] if after is not None else []),
        out_specs=out_specs,
        out_shape=out_shape,
        scratch_shapes=[pltpu.VMEM((tm, tn), F32)] if nk > 1 else [],
        compiler_params=pltpu.CompilerParams(dimension_semantics=("parallel", "parallel", "arbitrary")),
    )(a, b, *[arr for arr, _ in extras], *([after] if after is not None else []))
    return outs[0] if n_out == 1 else tuple(outs)


def _epi_sq_relu(acc):
    r = jnp.maximum(acc, 0.0)
    return r * r, 2.0 * r


def _epi_residual(acc, x, g):
    return x + g * acc, acc


def _rms_mod(xv, gain, sc, sh):
    r = lax.rsqrt(jnp.mean(xv * xv, axis=-1, keepdims=True) + RMS_EPS)
    return ((xv * r) * gain) * (1.0 + sc) + sh


def _epi_residual_norm(acc, x, g, gain, sc, sh):
    xn = x + g * acc
    return xn, acc, _rms_mod(xn, gain, sc, sh)


def _row_spec(tr, d):
    return pl.BlockSpec((tr, d), lambda i: (i, 0))


def _vec_spec(d):
    return pl.BlockSpec((1, d), lambda i: (0, 0))


def _colsum(v):
    return jnp.sum(v, axis=0, keepdims=True)


def _norm_mod_fwd(x, gain, sc, sh, out_dtype, name):
    T, D = x.shape
    tr = min(T, ROW_TILE)

    def body(x_ref, g_ref, sc_ref, sh_ref, o_ref):
        o_ref[...] = _rms_mod(x_ref[...], g_ref[...], sc_ref[...], sh_ref[...]).astype(o_ref.dtype)

    return pl.pallas_call(
        body, name=name, grid=(T // tr,),
        in_specs=[_row_spec(tr, D), _vec_spec(D), _vec_spec(D), _vec_spec(D)],
        out_specs=_row_spec(tr, D),
        out_shape=jax.ShapeDtypeStruct((T, D), out_dtype),
        compiler_params=pltpu.CompilerParams(dimension_semantics=("parallel",)),
    )(x, gain, sc, sh)


def _norm_mod_bwd(x, dh, dres, gain, sc, name, res=None):
    T, D = x.shape
    tr = min(T, ROW_TILE)

    def body(x_ref, dh_ref, dres_ref, g_ref, sc_ref, *refs):
        dx_ref, dg_ref, dsc_ref, dsh_ref = refs[-6:-2] if res is not None else refs

        @pl.when(pl.program_id(0) == 0)
        def _():
            dg_ref[...] = jnp.zeros_like(dg_ref)
            dsc_ref[...] = jnp.zeros_like(dsc_ref)
            dsh_ref[...] = jnp.zeros_like(dsh_ref)
            if res is not None:
                refs[-1][...] = jnp.zeros_like(refs[-1])

        xv = x_ref[...]
        r = lax.rsqrt(jnp.mean(xv * xv, axis=-1, keepdims=True) + RMS_EPS)
        xn = xv * r
        dhv = dh_ref[...].astype(F32)
        dsh_ref[...] += _colsum(dhv)
        dsc_ref[...] += _colsum(dhv * (xn * g_ref[...]))
        dt = dhv * (1.0 + sc_ref[...])
        dg_ref[...] += _colsum(dt * xn)
        dxn = dt * g_ref[...]
        dxv = dres_ref[...] + r * (dxn - xn * jnp.mean(dxn * xn, axis=-1, keepdims=True))
        dx_ref[...] = dxv
        if res is not None:
            y_ref, gr_ref, dy_ref, q_ref = refs[0], refs[1], refs[-2], refs[-1]
            dy_ref[...] = (gr_ref[...] * dxv).astype(BF16)
            q_ref[...] += _colsum(dxv * y_ref[...].astype(F32))

    extra_in, extra_spec = ([], []) if res is None else (list(res), [_row_spec(tr, D), _vec_spec(D)])
    return pl.pallas_call(
        body, name=name, grid=(T // tr,),
        in_specs=[_row_spec(tr, D), _row_spec(tr, D), _row_spec(tr, D), _vec_spec(D), _vec_spec(D), *extra_spec],
        out_specs=[_row_spec(tr, D), _vec_spec(D), _vec_spec(D), _vec_spec(D)] + ([_row_spec(tr, D), _vec_spec(D)] if res is not None else []),
        out_shape=[jax.ShapeDtypeStruct((T, D), F32)] + [jax.ShapeDtypeStruct((1, D), F32)] * 3
        + ([jax.ShapeDtypeStruct((T, D), BF16), jax.ShapeDtypeStruct((1, D), F32)] if res is not None else []),
        compiler_params=pltpu.CompilerParams(dimension_semantics=("arbitrary",)),
    )(x, dh, dres, gain, sc, *extra_in)


def _loss_head(x, target, gain, y, g, name):
    T, D = x.shape
    tr = min(T, ROW_TILE)

    def body(x_ref, t_ref, g_ref, y_ref, gr_ref, loss_ref, dx_ref, dg_ref, dy_ref, q_ref):
        @pl.when(pl.program_id(0) == 0)
        def _():
            loss_ref[...] = jnp.zeros_like(loss_ref)
            dg_ref[...] = jnp.zeros_like(dg_ref)
            q_ref[...] = jnp.zeros_like(q_ref)

        xv = x_ref[...]
        r = lax.rsqrt(jnp.mean(xv * xv, axis=-1, keepdims=True) + RMS_EPS)
        xn = xv * r
        err = xn * g_ref[...] - t_ref[...]
        row = jnp.mean(err * err, axis=-1, keepdims=True)
        loss_ref[...] += 0.5 * jnp.sum(row, axis=0, keepdims=True)
        dy = err * (1.0 / D)
        dg_ref[...] += _colsum(dy * xn)
        dxn = dy * g_ref[...]
        dxv = r * (dxn - xn * jnp.mean(dxn * xn, axis=-1, keepdims=True))
        dx_ref[...] = dxv
        dy_ref[...] = (gr_ref[...] * dxv).astype(BF16)
        q_ref[...] += _colsum(dxv * y_ref[...].astype(F32))

    return pl.pallas_call(
        body, name=name, grid=(T // tr,),
        in_specs=[_row_spec(tr, D), _row_spec(tr, D), _vec_spec(D), _row_spec(tr, D), _vec_spec(D)],
        out_specs=[_vec_spec(128), _row_spec(tr, D), _vec_spec(D), _row_spec(tr, D), _vec_spec(D)],
        out_shape=[jax.ShapeDtypeStruct((1, 128), F32), jax.ShapeDtypeStruct((T, D), F32), jax.ShapeDtypeStruct((1, D), F32),
                   jax.ShapeDtypeStruct((T, D), BF16), jax.ShapeDtypeStruct((1, D), F32)],
        compiler_params=pltpu.CompilerParams(dimension_semantics=("arbitrary",)),
    )(x, target, gain, y, g)


def _pool_fwd(h, w, scale, x, g1, gmlp, sc2, sh2, name):
    T, D = h.shape
    tr = min(T, ROW_TILE)

    def body(h_ref, w_ref, sc_ref, x_ref, g_ref, gm_ref, sc2_ref, sh2_ref, x2_ref, pooled_ref, ypre_ref, h2_ref, halo_ref):
        i = pl.program_id(0)

        @pl.when(i == 0)
        def _():
            halo_ref[...] = jnp.zeros_like(halo_ref)

        hv = h_ref[...]
        buf = jnp.concatenate([halo_ref[...], hv], axis=0)
        halo_ref[...] = hv[tr - POOL_HALO:, :]
        t = (i * tr + lax.broadcasted_iota(jnp.int32, (tr, 1), 0)).astype(F32)
        for gi, win in enumerate(POOL_WINDOWS):
            cols = slice(gi * POOL_GD, (gi + 1) * POOL_GD)
            val = buf[:, cols]
            sh = 1
            while sh < win:
                val = val + pltpu.roll(val, sh, axis=0)
                sh *= 2
            pooled = val[POOL_HALO:, :] / jnp.minimum(t + 1.0, float(win)) - hv[:, cols]
            pb = pooled.astype(BF16)
            pooled_ref[:, cols] = pb
            yp = jnp.dot(pb, w_ref[gi], preferred_element_type=F32)
            ypre_ref[:, cols] = yp.astype(BF16)
            x2_ref[:, cols] = x_ref[:, cols] + g_ref[:, cols] * (yp * sc_ref[:, cols])
        h2_ref[...] = _rms_mod(x2_ref[...], gm_ref[...], sc2_ref[...], sh2_ref[...]).astype(BF16)

    return pl.pallas_call(
        body, name=name, grid=(T // tr,),
        in_specs=[_row_spec(tr, D), pl.BlockSpec(w.shape, lambda i: (0, 0, 0)), _vec_spec(D), _row_spec(tr, D), _vec_spec(D), _vec_spec(D),
                  _vec_spec(D), _vec_spec(D)],
        out_specs=[_row_spec(tr, D)] * 4,
        out_shape=[jax.ShapeDtypeStruct((T, D), F32), jax.ShapeDtypeStruct((T, D), BF16), jax.ShapeDtypeStruct((T, D), BF16),
                   jax.ShapeDtypeStruct((T, D), BF16)],
        scratch_shapes=[pltpu.VMEM((POOL_HALO, D), F32)],
        compiler_params=pltpu.CompilerParams(dimension_semantics=("arbitrary",)),
    )(h, w, scale, x, g1, gmlp, sc2, sh2)


def _pool_bwd(dy, pooled, w, scale, g1, q, name):
    T, D = dy.shape
    tr = min(T, ROW_TILE)
    nt = T // tr
    ltot = tr + POOL_HALO

    def body(dy_ref, pooled_ref, w_ref, sc_ref, g_ref, q_ref, dh_ref, dw_ref, dsc_ref, dg_ref, halo_ref):
        i = pl.program_id(0)

        @pl.when(i == 0)
        def _():
            halo_ref[...] = jnp.zeros_like(halo_ref)
            dw_ref[...] = jnp.zeros_like(dw_ref)
            dsc_ref[...] = g_ref[...] * q_ref[...]
            dg_ref[...] = sc_ref[...] * q_ref[...]

        t = ((nt - 1 - i) * tr + lax.broadcasted_iota(jnp.int32, (tr, 1), 0)).astype(F32)
        for gi, win in enumerate(POOL_WINDOWS):
            cols = slice(gi * POOL_GD, (gi + 1) * POOL_GD)
            dyb = (dy_ref[:, cols].astype(F32) * sc_ref[:, cols]).astype(BF16)
            dw_ref[gi] += lax.dot_general(pooled_ref[:, cols], dyb, (((0,), (0,)), ((), ())), preferred_element_type=F32)
            dpool = lax.dot_general(dyb, w_ref[gi], (((1,), (1,)), ((), ())), preferred_element_type=F32)
            qv = dpool / jnp.minimum(t + 1.0, float(win))
            val = jnp.concatenate([qv, halo_ref[:, cols]], axis=0)
            halo_ref[:, cols] = qv[:POOL_HALO, :]
            sh = 1
            while sh < win:
                val = val + pltpu.roll(val, ltot - sh, axis=0)
                sh *= 2
            dh_ref[:, cols] = (val[:tr, :] - dpool).astype(BF16)

    rev = pl.BlockSpec((tr, D), lambda i: (nt - 1 - i, 0))
    return pl.pallas_call(
        body, name=name, grid=(nt,),
        in_specs=[rev, rev, pl.BlockSpec(w.shape, lambda i: (0, 0, 0)), _vec_spec(D), _vec_spec(D), _vec_spec(D)],
        out_specs=[rev, pl.BlockSpec(w.shape, lambda i: (0, 0, 0)), _vec_spec(D), _vec_spec(D)],
        out_shape=[jax.ShapeDtypeStruct((T, D), BF16), jax.ShapeDtypeStruct(w.shape, F32),
                   jax.ShapeDtypeStruct((1, D), F32), jax.ShapeDtypeStruct((1, D), F32)],
        scratch_shapes=[pltpu.VMEM((POOL_HALO, D), F32)],
        compiler_params=pltpu.CompilerParams(dimension_semantics=("arbitrary",)),
    )(dy, pooled, w, scale, g1, q)


_INV_SQRT2 = 0.7071067811865476
_INV_SQRT2PI = 0.3989422804014327


def _gelu(v):
    return 0.5 * v * (1.0 + lax.erf(v * _INV_SQRT2))


def _gelu_grad(v):
    return 0.5 * (1.0 + lax.erf(v * _INV_SQRT2)) + v * jnp.exp(-0.5 * v * v) * _INV_SQRT2PI


def _sgu_ln(v, g, b):
    mu = jnp.mean(v, axis=-1, keepdims=True)
    xc = v - mu
    rstd = lax.rsqrt(jnp.mean(xc * xc, axis=-1, keepdims=True) + LN_EPS)
    xh = xc * rstd
    return xh, rstd, xh * g + b


def _tril_mask():
    return lax.broadcasted_iota(jnp.int32, (SGU_CHUNK, SGU_CHUNK), 0) >= lax.broadcasted_iota(jnp.int32, (SGU_CHUNK, SGU_CHUNK), 1)


SGU_TILE = 256


def _sgu_gate_fwd(zz, ln_g, ln_b, ws, bs_t, name):
    T = zz.shape[0]
    ts = min(T, SGU_TILE)

    def body(zz_ref, g_ref, b_ref, ws_ref, bs_ref, out_ref):
        z = _gelu(zz_ref[...])
        u = z[:, :SGU_W]
        _, _, vn = _sgu_ln(z[:, SGU_W:], g_ref[...], b_ref[...])
        vb = vn.astype(BF16)
        tril = _tril_mask()
        for hh in range(SGU_H):
            wm = jnp.where(tril, ws_ref[hh], 0.0).astype(BF16)
            bcol = bs_ref[:, hh:hh + 1]
            cs = slice(hh * SGU_HD, (hh + 1) * SGU_HD)
            for j in range(ts // SGU_CHUNK):
                rs = slice(j * SGU_CHUNK, (j + 1) * SGU_CHUNK)
                mixed = jnp.dot(wm, vb[rs, cs], preferred_element_type=F32) + bcol
                out_ref[rs, cs] = (u[rs, cs] * mixed).astype(BF16)

    return pl.pallas_call(
        body, name=name, grid=(T // ts,),
        in_specs=[_row_spec(ts, 2 * SGU_W), _vec_spec(SGU_W), _vec_spec(SGU_W),
                  pl.BlockSpec(ws.shape, lambda i: (0, 0, 0)), pl.BlockSpec(bs_t.shape, lambda i: (0, 0))],
        out_specs=_row_spec(ts, SGU_W),
        out_shape=jax.ShapeDtypeStruct((T, SGU_W), BF16),
        compiler_params=pltpu.CompilerParams(dimension_semantics=("parallel",)),
    )(zz, ln_g, ln_b, ws, bs_t)


def _sgu_gate_bwd(zz, dgated, ln_g, ln_b, ws, bs_t, name):
    T = zz.shape[0]
    ts = min(T, SGU_TILE)
    nt = T // ts

    def body(zz_ref, dg_ref, g_ref, b_ref, ws_ref, bs_ref, dzz_ref, dws_ref, dbs_ref, dlg_ref, dlb_ref, dlo_ref, dmx_ref):
        i = pl.program_id(0)

        @pl.when(i == 0)
        def _():
            dws_ref[...] = jnp.zeros_like(dws_ref)
            dmx_ref[...] = jnp.zeros_like(dmx_ref)
            dlg_ref[...] = jnp.zeros_like(dlg_ref)
            dlb_ref[...] = jnp.zeros_like(dlb_ref)

        zzv = zz_ref[...]
        z = _gelu(zzv)
        u = z[:, :SGU_W]
        xh, rstd, vn = _sgu_ln(z[:, SGU_W:], g_ref[...], b_ref[...])
        vb = vn.astype(BF16)
        dgv = dg_ref[...].astype(F32)
        tril = _tril_mask()
        for hh in range(SGU_H):
            wm = jnp.where(tril, ws_ref[hh], 0.0).astype(BF16)
            bcol = bs_ref[:, hh:hh + 1]
            cs = slice(hh * SGU_HD, (hh + 1) * SGU_HD)
            for j in range(ts // SGU_CHUNK):
                rs = slice(j * SGU_CHUNK, (j + 1) * SGU_CHUNK)
                mixed = jnp.dot(wm, vb[rs, cs], preferred_element_type=F32) + bcol
                dmixed = dgv[rs, cs] * u[rs, cs]
                dzz_ref[rs, cs] = (dgv[rs, cs] * mixed * _gelu_grad(zzv[rs, cs])).astype(BF16)
                dmb = dmixed.astype(BF16)
                dws_ref[hh] += lax.dot_general(dmb, vb[rs, cs], (((1,), (1,)), ((), ())), preferred_element_type=F32)
                dmx_ref[hh] += dmixed
                dlo_ref[rs, cs] = lax.dot_general(wm, dmb, (((0,), (0,)), ((), ())), preferred_element_type=F32)
        dlo = dlo_ref[...]
        dlg_ref[...] += _colsum(dlo * xh)
        dlb_ref[...] += _colsum(dlo)
        dxh = dlo * g_ref[...]
        dv = rstd * (dxh - jnp.mean(dxh, axis=-1, keepdims=True) - xh * jnp.mean(dxh * xh, axis=-1, keepdims=True))
        dzz_ref[:, SGU_W:] = (dv * _gelu_grad(zzv[:, SGU_W:])).astype(BF16)

        @pl.when(i == nt - 1)
        def _():
            tril_f = tril.astype(F32)
            for hh in range(SGU_H):
                dws_ref[hh] = dws_ref[hh] * tril_f
                dbs_ref[hh] = jnp.broadcast_to(jnp.sum(dmx_ref[hh], axis=-1, keepdims=True), (SGU_CHUNK, SGU_HD))

    full3 = pl.BlockSpec(ws.shape, lambda i: (0, 0, 0))
    return pl.pallas_call(
        body, name=name, grid=(nt,),
        in_specs=[_row_spec(ts, 2 * SGU_W), _row_spec(ts, SGU_W), _vec_spec(SGU_W), _vec_spec(SGU_W), full3,
                  pl.BlockSpec(bs_t.shape, lambda i: (0, 0))],
        out_specs=[_row_spec(ts, 2 * SGU_W), full3, full3, _vec_spec(SGU_W), _vec_spec(SGU_W)],
        out_shape=[jax.ShapeDtypeStruct((T, 2 * SGU_W), BF16), jax.ShapeDtypeStruct(ws.shape, F32), jax.ShapeDtypeStruct(ws.shape, F32),
                   jax.ShapeDtypeStruct((1, SGU_W), F32), jax.ShapeDtypeStruct((1, SGU_W), F32)],
        scratch_shapes=[pltpu.VMEM((ts, SGU_W), F32), pltpu.VMEM(ws.shape, F32)],
        compiler_params=pltpu.CompilerParams(dimension_semantics=("arbitrary",)),
    )(zz, dgated, ln_g, ln_b, ws, bs_t)


def _rope_fwd(blk, cc, sa, sb):
    return blk * cc + pltpu.roll(blk, 96, axis=1) * sa + pltpu.roll(blk, 32, axis=1) * sb


def _rope_bwd(d, cc, sa, sb):
    return d * cc + pltpu.roll(d * sa, 32, axis=1) + pltpu.roll(d * sb, 96, axis=1)


def _rms(v, g):
    r = lax.rsqrt(jnp.mean(v * v, axis=-1, keepdims=True) + RMS_EPS)
    vn = v * r
    return vn, r, vn * g


def _rms_bwd(dy, vn, r, g):
    dvn = dy * g
    return r * (dvn - vn * jnp.mean(dvn * vn, axis=-1, keepdims=True))


_KV0 = MLA_QL
_KR0 = MLA_QL + MLA_KVL


def _mla_lat_fwd(lat, qg, kvg, cc, sa, sb, name):
    T = lat.shape[0]
    tr = min(T, ROW_TILE)

    def body(lat_ref, qg_ref, kvg_ref, cc_ref, sa_ref, sb_ref, cq_ref, ckv_ref, kr_ref):
        lv = lat_ref[...]
        cq_ref[...] = _rms(lv[:, :_KV0], qg_ref[...])[2].astype(BF16)
        ckv_ref[...] = _rms(lv[:, _KV0:_KR0], kvg_ref[...])[2].astype(BF16)
        kr_ref[...] = _rope_fwd(lv[:, _KR0:], cc_ref[...], sa_ref[...], sb_ref[...])

    return pl.pallas_call(
        body, name=name, grid=(T // tr,),
        in_specs=[_row_spec(tr, MLA_LATP), _vec_spec(MLA_QL), _vec_spec(MLA_KVL), _row_spec(tr, 128), _row_spec(tr, 128), _row_spec(tr, 128)],
        out_specs=[_row_spec(tr, MLA_QL), _row_spec(tr, MLA_KVL), _row_spec(tr, 128)],
        out_shape=[jax.ShapeDtypeStruct((T, MLA_QL), BF16), jax.ShapeDtypeStruct((T, MLA_KVL), BF16), jax.ShapeDtypeStruct((T, 128), F32)],
        compiler_params=pltpu.CompilerParams(dimension_semantics=("parallel",)),
    )(lat, qg, kvg, cc, sa, sb)


def _mla_lat_bwd(lat, dcqn, dckvn, dkrot, qg, kvg, cc, sa, sb, name):
    T = lat.shape[0]
    tr = min(T, ROW_TILE)

    def body(lat_ref, dcq_ref, dckv_ref, dkr_ref, qg_ref, kvg_ref, cc_ref, sa_ref, sb_ref, dlat_ref, dqg_ref, dkvg_ref):
        @pl.when(pl.program_id(0) == 0)
        def _():
            dqg_ref[...] = jnp.zeros_like(dqg_ref)
            dkvg_ref[...] = jnp.zeros_like(dkvg_ref)

        lv = lat_ref[...]
        qn, qr, _ = _rms(lv[:, :_KV0], qg_ref[...])
        kn, kr, _ = _rms(lv[:, _KV0:_KR0], kvg_ref[...])
        dcq = dcq_ref[...]
        dckv = dckv_ref[...]
        dqg_ref[...] += _colsum(dcq * qn)
        dkvg_ref[...] += _colsum(dckv * kn)
        dlat_ref[:, :_KV0] = _rms_bwd(dcq, qn, qr, qg_ref[...]).astype(BF16)
        dlat_ref[:, _KV0:_KR0] = _rms_bwd(dckv, kn, kr, kvg_ref[...]).astype(BF16)
        dlat_ref[:, _KR0:] = _rope_bwd(dkr_ref[...], cc_ref[...], sa_ref[...], sb_ref[...]).astype(BF16)

    return pl.pallas_call(
        body, name=name, grid=(T // tr,),
        in_specs=[_row_spec(tr, MLA_LATP), _row_spec(tr, MLA_QL), _row_spec(tr, MLA_KVL), _row_spec(tr, 128),
                  _vec_spec(MLA_QL), _vec_spec(MLA_KVL), _row_spec(tr, 128), _row_spec(tr, 128), _row_spec(tr, 128)],
        out_specs=[_row_spec(tr, MLA_LATP), _vec_spec(MLA_QL), _vec_spec(MLA_KVL)],
        out_shape=[jax.ShapeDtypeStruct((T, MLA_LATP), BF16), jax.ShapeDtypeStruct((1, MLA_QL), F32), jax.ShapeDtypeStruct((1, MLA_KVL), F32)],
        compiler_params=pltpu.CompilerParams(dimension_semantics=("arbitrary",)),
    )(lat, dcqn, dckvn, dkrot, qg, kvg, cc, sa, sb)


LOG2E = 1.4426950408889634
Q_SCALE = SM_SCALE * LOG2E


def _epi_q_rope(acc, cc, sa, sb):
    out = []
    for hh in range(acc.shape[1] // MLA_HP):
        a, m, b = hh * MLA_HP, hh * MLA_HP + MLA_NOPE, (hh + 1) * MLA_HP
        out += [acc[:, a:m] * Q_SCALE, _rope_fwd(acc[:, m:b], cc, sa, sb) * Q_SCALE]
    return (jnp.concatenate(out, axis=1),)


def _mla_ukv(ckvn, w_ukv, krot, name):
    T = ckvn.shape[0]
    tr = min(T, ATT_TILE)
    hg = ATT_HG
    gw = hg * MLA_HP

    def body(a_ref, w_ref, kr_ref, ko_ref, kt_ref, vo_ref, vt_ref):
        acc = jnp.dot(a_ref[...], w_ref[...], preferred_element_type=F32)
        kr = kr_ref[...]
        krb, krt = kr.astype(BF16), kr.T.astype(BF16)
        for hh in range(hg):
            a, m, b = hh * MLA_HP, hh * MLA_HP + MLA_NOPE, (hh + 1) * MLA_HP
            kn, vh = acc[:, a:m], acc[:, m:b]
            ko_ref[:, a:m] = kn.astype(BF16)
            ko_ref[:, m:b] = krb
            kt_ref[a:m, :] = kn.T.astype(BF16)
            kt_ref[m:b, :] = krt
            vo_ref[:, hh * MLA_V:(hh + 1) * MLA_V] = vh.astype(BF16)
            vt_ref[hh] = vh.T.astype(BF16)

    tk = min(T, ATT_TILE)
    per = tk // tr
    HW = MLA_H * MLA_HP
    return pl.pallas_call(
        body, name=name, grid=(T // tr, MLA_H // hg),
        in_specs=[pl.BlockSpec((tr, MLA_KVL), lambda i, g: (i, 0)), pl.BlockSpec((MLA_KVL, gw), lambda i, g: (0, g)),
                  pl.BlockSpec((tr, 128), lambda i, g: (i, 0))],
        out_specs=[pl.BlockSpec((tr, gw), lambda i, g: (i, g)), pl.BlockSpec((gw, tr), lambda i, g: (g, i)),
                   pl.BlockSpec((tr, hg * MLA_V), lambda i, g: (i, g)),
                   pl.BlockSpec((hg, None, MLA_V, tr), lambda i, g: (g, i // per, 0, i % per))],
        out_shape=[jax.ShapeDtypeStruct((T, HW), BF16), jax.ShapeDtypeStruct((HW, T), BF16), jax.ShapeDtypeStruct((T, MLA_H * MLA_V), BF16),
                   jax.ShapeDtypeStruct((MLA_H, T // tk, MLA_V, tk), BF16)],
        compiler_params=pltpu.CompilerParams(dimension_semantics=("parallel", "parallel")),
    )(ckvn, w_ukv, krot)


ATT_HG = 4


def _mla_prep_bwd(dqt, dkr, cc, sa, sb, name):
    _, nq, _, tq = dqt.shape
    T = nq * tq
    gw = ATT_HG * MLA_HP

    def body(dq_ref, dk_ref, cc_ref, sa_ref, sb_ref, dqp_ref, dkr_ref):
        @pl.when(pl.program_id(1) == 0)
        def _():
            dkr_ref[...] = jnp.zeros_like(dkr_ref)

        cc, sa, sb = cc_ref[...], sa_ref[...], sb_ref[...]
        acc = jnp.zeros((tq, 128), F32)
        for hh in range(ATT_HG):
            a, m, b = hh * MLA_HP, hh * MLA_HP + MLA_NOPE, (hh + 1) * MLA_HP
            dqh = dq_ref[hh].astype(F32).T * SM_SCALE
            dqp_ref[:, a:m] = dqh[:, :MLA_NOPE].astype(BF16)
            dqp_ref[:, m:b] = _rope_bwd(dqh[:, MLA_NOPE:], cc, sa, sb).astype(BF16)
            acc = acc + dk_ref[:, hh * 128:(hh + 1) * 128].astype(F32)
        dkr_ref[...] += acc

    tab = pl.BlockSpec((tq, 128), lambda i, g: (i, 0))
    return pl.pallas_call(
        body, name=name, grid=(nq, MLA_H // ATT_HG),
        in_specs=[pl.BlockSpec((ATT_HG, None, MLA_HP, tq), lambda i, g: (g, i, 0, 0)), pl.BlockSpec((tq, ATT_HG * 128), lambda i, g: (i, g)),
                  tab, tab, tab],
        out_specs=[pl.BlockSpec((tq, gw), lambda i, g: (i, g)), tab],
        out_shape=[jax.ShapeDtypeStruct((T, MLA_H * MLA_HP), BF16), jax.ShapeDtypeStruct((T, 128), F32)],
        compiler_params=pltpu.CompilerParams(dimension_semantics=("parallel", "arbitrary")),
    )(dqt, dkr, cc, sa, sb)


_NT = (((1,), (1,)), ((), ()))


def _as_row(col, n):
    return jnp.broadcast_to(col, (n, 128)).T[0:1, :]


def _attn_fwd(q, k, vt, name):
    T = q.shape[0]
    tq = tk = min(T, ATT_TILE)
    nq = T // tq
    hg = ATT_FWD_HEADS

    def body(q_ref, k_ref, vt_ref, o_ref, lse_ref, m_ref, l_ref, acc_ref):
        i = pl.program_id(1)
        m_ref[...] = jnp.full_like(m_ref, NEG_INF)
        l_ref[...] = jnp.zeros_like(l_ref)
        acc_ref[...] = jnp.zeros_like(acc_ref)

        def step(j, diag):
            off = pl.multiple_of(j * tk, tk)
            sts = [lax.dot_general(k_ref[pl.ds(off, tk), hh * MLA_HP:(hh + 1) * MLA_HP], q_ref[:, hh * MLA_HP:(hh + 1) * MLA_HP], _NT,
                                   preferred_element_type=F32) for hh in range(hg)]
            for hh in range(hg):
                st = sts[hh]
                if diag:
                    st = jnp.where(lax.broadcasted_iota(jnp.int32, (tk, tq), 0) <= lax.broadcasted_iota(jnp.int32, (tk, tq), 1), st, NEG_INF)
                m_prev = m_ref[hh]
                m_new = jnp.maximum(m_prev, jnp.max(st, axis=0, keepdims=True))
                alpha = jnp.exp2(m_prev - m_new)
                pt = jnp.exp2(st - m_new)
                l_ref[hh] = alpha * l_ref[hh] + jnp.sum(pt, axis=0, keepdims=True)
                acc_ref[hh] = alpha * acc_ref[hh] + jnp.dot(vt_ref[hh, j], pt.astype(BF16), preferred_element_type=F32)
                m_ref[hh] = m_new

        def loop_body(j, carry):
            step(j, False)
            return carry

        lax.fori_loop(0, i, loop_body, 0)
        step(i, True)
        for hh in range(hg):
            o_ref[:, hh * MLA_V:(hh + 1) * MLA_V] = (acc_ref[hh] / l_ref[hh]).T.astype(BF16)
            lse_ref[hh] = m_ref[hh] + jnp.log2(l_ref[hh])

    return pl.pallas_call(
        body, name=name, grid=(MLA_H // hg, nq),
        in_specs=[pl.BlockSpec((tq, hg * MLA_HP), lambda h, i: (i, h)), pl.BlockSpec((T, hg * MLA_HP), lambda h, i: (0, h)),
                  pl.BlockSpec((hg, nq, MLA_V, tk), lambda h, i: (h, 0, 0, 0))],
        out_specs=[pl.BlockSpec((tq, hg * MLA_V), lambda h, i: (i, h)), pl.BlockSpec((hg, None, 1, tq), lambda h, i: (h, i, 0, 0))],
        out_shape=[jax.ShapeDtypeStruct((T, MLA_H * MLA_V), BF16), jax.ShapeDtypeStruct((MLA_H, nq, 1, tq), F32)],
        scratch_shapes=[pltpu.VMEM((hg, 1, tq), F32), pltpu.VMEM((hg, 1, tq), F32), pltpu.VMEM((hg, MLA_V, tq), F32)],
        compiler_params=pltpu.CompilerParams(dimension_semantics=("parallel", "arbitrary")),
    )(q, k, vt)


def _attn_delta(do, o, name):
    T = do.shape[0]
    tq = min(T, ATT_TILE)

    def body(do_ref, o_ref, d_ref):
        for hh in range(MLA_H):
            cs = slice(hh * MLA_V, (hh + 1) * MLA_V)
            s = jnp.sum(do_ref[:, cs].astype(F32) * o_ref[:, cs].astype(F32), axis=-1, keepdims=True)
            d_ref[hh] = _as_row(s, tq)

    return pl.pallas_call(
        body, name=name, grid=(T // tq,),
        in_specs=[_row_spec(tq, MLA_H * MLA_V), _row_spec(tq, MLA_H * MLA_V)],
        out_specs=pl.BlockSpec((MLA_H, None, 1, tq), lambda i: (0, i, 0, 0)),
        out_shape=jax.ShapeDtypeStruct((MLA_H, T // tq, 1, tq), F32),
        compiler_params=pltpu.CompilerParams(dimension_semantics=("parallel",)),
    )(do, o)


def _attn_bwd(q, k, kt, v, do, lse, delta, name):
    T = q.shape[0]
    tq = tk = min(T, ATT_TILE)
    nq = nk = T // tq
    tsd = min(tq, ATT_SUB)
    hg = ATT_BWD_HEADS

    def body(q_ref, k_ref, kt_ref, v_ref, do_ref, lse_ref, dl_ref, dqt_ref, dkv_ref, dkr_ref, dq_acc, dk_acc, dv_acc):
        j = pl.program_id(1)

        @pl.when(j == 0)
        def _():
            dq_acc[...] = jnp.zeros_like(dq_acc)

        dk_acc[...] = jnp.zeros_like(dk_acc)
        dv_acc[...] = jnp.zeros_like(dv_acc)

        def step(i, diag):
            off = pl.multiple_of(i * tq, tq)
            ts, nsub = (tsd, tq // tsd) if diag else (tq, 1)
            for u in range(nsub):
                cols = slice(u * ts, (u + 1) * ts)
                nk_u = (u + 1) * ts if diag else tk
                rows = pl.ds(off + u * ts, ts)
                pre = []
                for hh in range(hg):
                    hq, hv = slice(hh * MLA_HP, (hh + 1) * MLA_HP), slice(hh * MLA_V, (hh + 1) * MLA_V)
                    qi, doi = q_ref[rows, hq], do_ref[rows, hv]
                    st = lax.dot_general(k_ref[:nk_u, hq], qi, _NT, preferred_element_type=F32)
                    dpt = lax.dot_general(v_ref[:nk_u, hv], doi, _NT, preferred_element_type=F32)
                    pre.append((qi, doi, st, dpt))
                for hh in range(hg):
                    hq, hv = slice(hh * MLA_HP, (hh + 1) * MLA_HP), slice(hh * MLA_V, (hh + 1) * MLA_V)
                    qi, doi, st, dpt = pre[hh]
                    if diag:
                        qcol = u * ts + lax.broadcasted_iota(jnp.int32, (nk_u, ts), 1)
                        st = jnp.where(lax.broadcasted_iota(jnp.int32, (nk_u, ts), 0) <= qcol, st, NEG_INF)
                    pt = jnp.exp2(st - lse_ref[hh, i][:, cols])
                    dv_acc[:nk_u, hv] += jnp.dot(pt.astype(BF16), doi, preferred_element_type=F32)
                    dsb = (pt * (dpt - dl_ref[hh, i][:, cols])).astype(BF16)
                    dk_acc[:nk_u, hq] += jnp.dot(dsb, qi, preferred_element_type=F32)
                    dq_acc[hh, i, :, cols] += jnp.dot(kt_ref[hq, :nk_u], dsb, preferred_element_type=F32)

        def loop_body(i, carry):
            step(i, False)
            return carry

        step(j, True)
        lax.fori_loop(j + 1, nq, loop_body, 0)
        for hh in range(hg):
            a, m, b = hh * MLA_HP, hh * MLA_HP + MLA_NOPE, (hh + 1) * MLA_HP
            dkv_ref[:, a:m] = (dk_acc[:, a:m] * (1.0 / LOG2E)).astype(BF16)
            dkv_ref[:, m:b] = dv_acc[:, hh * MLA_V:(hh + 1) * MLA_V].astype(BF16)
            dkr_ref[:, hh * 128:(hh + 1) * 128] = (dk_acc[:, m:b] * (1.0 / LOG2E)).astype(BF16)

        @pl.when(j == nk - 1)
        def _():
            dqt_ref[...] = dq_acc[...].astype(BF16)

    stat = pl.BlockSpec((hg, nq, 1, tq), lambda h, j: (h, 0, 0, 0))
    return pl.pallas_call(
        body, name=name, grid=(MLA_H // hg, nk),
        in_specs=[pl.BlockSpec((T, hg * MLA_HP), lambda h, j: (0, h)), pl.BlockSpec((tk, hg * MLA_HP), lambda h, j: (j, h)),
                  pl.BlockSpec((hg * MLA_HP, tk), lambda h, j: (h, j)), pl.BlockSpec((tk, hg * MLA_V), lambda h, j: (j, h)),
                  pl.BlockSpec((T, hg * MLA_V), lambda h, j: (0, h)), stat, stat],
        out_specs=[pl.BlockSpec((hg, nq, MLA_HP, tq), lambda h, j: (h, 0, 0, 0)), pl.BlockSpec((tk, hg * MLA_HP), lambda h, j: (j, h)),
                   pl.BlockSpec((tk, hg * 128), lambda h, j: (j, h))],
        out_shape=[jax.ShapeDtypeStruct((MLA_H, nq, MLA_HP, tq), BF16), jax.ShapeDtypeStruct((T, MLA_H * MLA_HP), BF16),
                   jax.ShapeDtypeStruct((T, MLA_H * 128), BF16)],
        scratch_shapes=[pltpu.VMEM((hg, nq, MLA_HP, tq), F32), pltpu.VMEM((tk, hg * MLA_HP), F32), pltpu.VMEM((tk, hg * MLA_V), F32)],
        compiler_params=pltpu.CompilerParams(dimension_semantics=("parallel", "arbitrary")),
    )(q, k, kt, v, do, lse, delta)


ADA_TN = 512


def _silu(v):
    return v * (1.0 / (1.0 + jnp.exp(-v)))


def _ada_fwd(c_all, ada_w, ada_b_loc, name):
    L, D, Nc = ada_w.shape
    B = c_all.shape[0]

    def body(c_ref, w_ref, b_ref, o_ref):
        ca = _silu(c_ref[...]).astype(BF16)
        o_ref[...] = jnp.dot(ca, w_ref[...].astype(BF16), preferred_element_type=F32) + b_ref[...]

    return pl.pallas_call(
        body, name=name, grid=(L, Nc // ADA_TN),
        in_specs=[pl.BlockSpec((B, D), lambda l, n: (0, 0)), pl.BlockSpec((None, D, ADA_TN), lambda l, n: (l, 0, n)),
                  pl.BlockSpec((None, 1, ADA_TN), lambda l, n: (l, 0, n))],
        out_specs=pl.BlockSpec((None, B, ADA_TN), lambda l, n: (l, 0, n)),
        out_shape=jax.ShapeDtypeStruct((L, B, Nc), F32),
        compiler_params=pltpu.CompilerParams(dimension_semantics=("parallel", "parallel")),
    )(c_all, ada_w, ada_b_loc)


def _ada_bwd_adamw(c_all_t, dmod_loc, w, m, v, name):
    D, B = c_all_t.shape
    L, _, Nc = dmod_loc.shape

    def body(c_ref, d_ref, w_ref, m_ref, v_ref, g_ref, dl_ref, nm_ref, nv_ref):
        ca = _silu(c_ref[...])
        dv = d_ref[...]
        gv = ca[:, 0:1] * dv[0:1, :]
        for b in range(1, B):
            gv = gv + ca[:, b:b + 1] * dv[b:b + 1, :]
        mn = ADAM_B1 * m_ref[...] + (1.0 - ADAM_B1) * gv
        vn = ADAM_B2 * v_ref[...] + (1.0 - ADAM_B2) * (gv * gv)
        g_ref[...] = gv
        nm_ref[...] = mn
        nv_ref[...] = vn
        dl_ref[...] = -ADAM_LR * ((mn / _ADAM_C1) / (jnp.sqrt(vn / _ADAM_C2) + ADAM_EPS) + ADAM_WD * w_ref[...])

    blk = pl.BlockSpec((None, D, ADA_TN), lambda l, n: (l, 0, n))
    return pl.pallas_call(
        body, name=name, grid=(L, Nc // ADA_TN),
        in_specs=[pl.BlockSpec((D, B), lambda l, n: (0, 0)), pl.BlockSpec((None, B, ADA_TN), lambda l, n: (l, 0, n)), blk, blk, blk],
        out_specs=[blk] * 4,
        out_shape=[jax.ShapeDtypeStruct((L, D, Nc), F32)] * 4,
        compiler_params=pltpu.CompilerParams(dimension_semantics=("parallel", "parallel")),
    )(c_all_t, dmod_loc, w, m, v)


def _sum_lead(parts, name, out_dtype=F32):
    R, C = parts[0].shape[1:]
    n_tot = sum(p.shape[0] for p in parts)
    tr = R
    for cand in (512, 256, 128, 64, 32, 16):
        if R % cand == 0 and cand * C * 4 * n_tot <= (8 << 20):
            tr = cand
            break

    def body(*refs):
        o_ref = refs[-1]
        acc = None
        for r in refs[:-1]:
            for s in range(r.shape[0]):
                acc = r[s].astype(F32) if acc is None else acc + r[s].astype(F32)
        o_ref[...] = acc.astype(o_ref.dtype)

    return pl.pallas_call(
        body, name=name, grid=(R // tr,),
        in_specs=[pl.BlockSpec((p.shape[0], tr, C), lambda i: (0, i, 0)) for p in parts],
        out_specs=pl.BlockSpec((tr, C), lambda i: (i, 0)),
        out_shape=jax.ShapeDtypeStruct((R, C), out_dtype),
        compiler_params=pltpu.CompilerParams(dimension_semantics=("parallel",)),
    )(*parts)


_ADAM_C1 = 1.0 - ADAM_B1 ** ADAM_STEP
_ADAM_C2 = 1.0 - ADAM_B2 ** ADAM_STEP


def _adamw(w, g, m, v, name):
    shape = w.shape
    C = shape[-1]
    R = math.prod(shape[:-1]) if len(shape) > 1 else 1
    w2, g2, m2, v2 = (a.reshape(R, C) for a in (w, g, m, v))
    tr = R
    for cand in (1024, 512, 256, 128, 64, 32, 16, 8):
        if R % cand == 0 and cand * C * 4 <= (1 << 20):
            tr = cand
            break

    def body(w_ref, g_ref, m_ref, v_ref, d_ref, nm_ref, nv_ref):
        gv = g_ref[...]
        mn = ADAM_B1 * m_ref[...] + (1.0 - ADAM_B1) * gv
        vn = ADAM_B2 * v_ref[...] + (1.0 - ADAM_B2) * (gv * gv)
        nm_ref[...] = mn
        nv_ref[...] = vn
        m_hat = mn / _ADAM_C1
        v_hat = vn / _ADAM_C2
        d_ref[...] = -ADAM_LR * (m_hat / (jnp.sqrt(v_hat) + ADAM_EPS) + ADAM_WD * w_ref[...])

    spec = pl.BlockSpec((tr, C), lambda i: (i, 0))
    outs = pl.pallas_call(
        body, name=name, grid=(R // tr,),
        in_specs=[spec] * 4, out_specs=[spec] * 3,
        out_shape=[jax.ShapeDtypeStruct((R, C), F32)] * 3,
        compiler_params=pltpu.CompilerParams(dimension_semantics=("parallel",)),
    )(w2, g2, m2, v2)
    return tuple(o.reshape(shape) for o in outs)


def _row_tile(rows, cols, itemsize, budget):
    for cand in (1024, 512, 256, 128, 64, 32, 16):
        if rows % cand == 0 and cand * cols * itemsize <= budget:
            return cand
    return rows


def _sum_sel(sel, stacked, others, name, out_dtype):
    R, C = stacked.shape[1:]
    n_tot = 1 + sum(o.shape[0] for o in others)
    tr = _row_tile(R, C, 4 * n_tot, 8 << 20)

    def body(sel_ref, s_ref, *refs):
        o_ref = refs[-1]
        acc = s_ref[...].astype(F32)
        for r in refs[:-1]:
            for s in range(r.shape[0]):
                acc = acc + r[s].astype(F32)
        o_ref[...] = acc.astype(o_ref.dtype)

    return pl.pallas_call(
        body, name=name,
        grid_spec=pltpu.PrefetchScalarGridSpec(
            num_scalar_prefetch=1, grid=(R // tr,),
            in_specs=[pl.BlockSpec((None, tr, C), lambda i, s: (s[0], i, 0))] + [pl.BlockSpec((o.shape[0], tr, C), lambda i, s: (0, i, 0)) for o in others],
            out_specs=pl.BlockSpec((tr, C), lambda i, s: (i, 0))),
        out_shape=jax.ShapeDtypeStruct((R, C), out_dtype),
        compiler_params=pltpu.CompilerParams(dimension_semantics=("parallel",)),
    )(sel, stacked, *others)


def _adamw_piece(cidx, w2, m2, v2, mine, got, bufs, row0, name):
    hr, C = mine.shape
    tr = _row_tile(math.gcd(hr, row0) if row0 else hr, C, 4, 1 << 20)
    nt = hr // tr

    def body(c_ref, w_ref, m_ref, v_ref, a_ref, b_ref, _g, _d, _nm, _nv, g_ref, d_ref, nm_ref, nv_ref):
        gv = jnp.where(pl.program_id(0) == c_ref[0], a_ref[...], b_ref[...])
        mn = ADAM_B1 * m_ref[...] + (1.0 - ADAM_B1) * gv
        vn = ADAM_B2 * v_ref[...] + (1.0 - ADAM_B2) * (gv * gv)
        g_ref[...] = gv
        nm_ref[...] = mn
        nv_ref[...] = vn
        d_ref[...] = -ADAM_LR * ((mn / _ADAM_C1) / (jnp.sqrt(vn / _ADAM_C2) + ADAM_EPS) + ADAM_WD * w_ref[...])

    rows = pl.BlockSpec((tr, C), lambda hf, t, c: (row0 // tr + hf * nt + t, 0))
    mine_spec = pl.BlockSpec((tr, C), lambda hf, t, c: (jnp.where(hf == c[0], t, 0), 0))
    got_spec = pl.BlockSpec((tr, C), lambda hf, t, c: (jnp.where(hf == c[0], 0, t), 0))
    return pl.pallas_call(
        body, name=name,
        grid_spec=pltpu.PrefetchScalarGridSpec(num_scalar_prefetch=1, grid=(2, nt), in_specs=[rows] * 3 + [mine_spec, got_spec] + [_ANY_SPEC] * 4,
                                               out_specs=[rows] * 4),
        out_shape=[jax.ShapeDtypeStruct(w2.shape, F32)] * 4,
        input_output_aliases={6 + n: n for n in range(4)},
        compiler_params=pltpu.CompilerParams(dimension_semantics=("parallel", "parallel")),
    )(cidx, w2, m2, v2, mine, got, *bufs)


_VMEM_SPEC = pl.BlockSpec(memory_space=pltpu.VMEM)
_HBM_SPEC = pl.BlockSpec(memory_space=pltpu.HBM)


def _flip(v, bit):
    return (1 - v) if bit else v


def _allgather8(v, name):
    def body(v_ref, out_ref, send_sems, recv_sems, local_sem):
        x, y, c = _idx()
        me = 4 * x + 2 * y + c
        mine = pltpu.make_async_copy(v_ref, out_ref.at[me], local_sem)
        mine.start()
        sends = []
        for k in range(1, N_DEV):
            peer = (_flip(x, k & 4), _flip(y, k & 2), _flip(c, k & 1))
            cp = pltpu.make_async_remote_copy(src_ref=v_ref, dst_ref=out_ref.at[me], send_sem=send_sems.at[k - 1], recv_sem=recv_sems.at[k - 1],
                                              device_id=peer, device_id_type=MESH)
            cp.start()
            sends.append(cp)
        for k in range(1, N_DEV):
            px, py, pc = _flip(x, k & 4), _flip(y, k & 2), _flip(c, k & 1)
            src = 4 * px + 2 * py + pc
            pltpu.make_async_remote_copy(src_ref=v_ref, dst_ref=out_ref.at[src], send_sem=send_sems.at[k - 1], recv_sem=recv_sems.at[k - 1],
                                         device_id=(px, py, pc), device_id_type=MESH).wait_recv()
        for cp in sends:
            cp.wait_send()
        mine.wait()

    return pl.pallas_call(
        body, name=name,
        out_shape=jax.ShapeDtypeStruct((N_DEV, *v.shape), v.dtype),
        in_specs=[_VMEM_SPEC], out_specs=_VMEM_SPEC,
        scratch_shapes=[pltpu.SemaphoreType.DMA((N_DEV - 1,)), pltpu.SemaphoreType.DMA((N_DEV - 1,)), pltpu.SemaphoreType.DMA],
    )(v)


def _mod_exchange(modp, name):
    _, L, Nc = modp.shape

    def body(p_ref, out_ref, send_sems, recv_sems, local_sem):
        x, y, c = _idx()
        me, chip = 4 * x + 2 * y + c, 2 * x + y
        mine = pltpu.make_async_copy(p_ref.at[me], out_ref.at[chip], local_sem)
        mine.start()
        sends = []
        for k in range(1, N_CHIPS):
            px, py = _flip(x, k & 2), _flip(y, k & 1)
            cp = pltpu.make_async_remote_copy(src_ref=p_ref.at[4 * px + 2 * py + c], dst_ref=out_ref.at[chip],
                                              send_sem=send_sems.at[k - 1], recv_sem=recv_sems.at[k - 1], device_id=(px, py, c), device_id_type=MESH)
            cp.start()
            sends.append(cp)
        for k in range(1, N_CHIPS):
            px, py = _flip(x, k & 2), _flip(y, k & 1)
            pltpu.make_async_remote_copy(src_ref=p_ref.at[me], dst_ref=out_ref.at[2 * px + py], send_sem=send_sems.at[k - 1],
                                         recv_sem=recv_sems.at[k - 1], device_id=(px, py, c), device_id_type=MESH).wait_recv()
        for cp in sends:
            cp.wait_send()
        mine.wait()

    return pl.pallas_call(
        body, name=name,
        out_shape=jax.ShapeDtypeStruct((N_CHIPS, L, Nc), modp.dtype),
        in_specs=[_VMEM_SPEC], out_specs=_VMEM_SPEC,
        scratch_shapes=[pltpu.SemaphoreType.DMA((N_CHIPS - 1,)), pltpu.SemaphoreType.DMA((N_CHIPS - 1,)), pltpu.SemaphoreType.DMA],
    )(modp)


_SEM_SPEC = pl.BlockSpec(memory_space=pltpu.SEMAPHORE)
_ANY_SPEC = pl.BlockSpec(memory_space=pl.ANY)
_EFFECT = pltpu.SideEffectType.DATAFLOW_SIDE_EFFECTING


def _hbm(a):
    return pltpu.with_memory_space_constraint(a, pltpu.HBM)


def _xchip_copies(mode, srcs, lands, send_sems, recv_sems, waiting):
    x, y, c = _idx()
    chip = 2 * x + y
    out = []
    for a in range(len(srcs)):
        for k in range(1, _n_peers(mode) + 1):
            if mode == "all8":
                px, py, pc = _flip(x, k & 4), _flip(y, k & 2), _flip(c, k & 1)
                src, dst, mine = srcs[a], lands[a].at[4 * x + 2 * y + c], lands[a].at[4 * px + 2 * py + pc]
            elif mode == "scatter8":
                px, py, pc = _flip(x, k & 4), _flip(y, k & 2), _flip(c, k & 1)
                src, dst, mine = srcs[a].at[pc, 2 * px + py], lands[a].at[k - 1], lands[a].at[k - 1]
            else:
                px, py, pc = _flip(x, k & 2), _flip(y, k & 1), c
                peer = 2 * px + py
                if mode == "gather":
                    src, dst, mine = srcs[a].at[c], lands[a].at[chip, c], lands[a].at[peer, c]
                else:
                    src, dst, mine = srcs[a].at[peer], lands[a].at[k - 1], lands[a].at[k - 1]
            q = a * _n_peers(mode) + k - 1
            out.append(pltpu.make_async_remote_copy(src_ref=src, dst_ref=mine if waiting else dst, send_sem=send_sems[q], recv_sem=recv_sems[q],
                                                    device_id=(px, py, pc), device_id_type=MESH))
    return out


def _n_peers(mode):
    return N_DEV - 1 if mode in ("all8", "scatter8") else N_CHIPS - 1


def _xchip_start(mode, srcs, land_shapes, dep, name):
    n = len(srcs)
    ns = n * _n_peers(mode)

    def body(*refs):
        src_refs, land_refs = refs[:n], refs[n:2 * n]
        outs = refs[2 * n + 1:]
        for cp in _xchip_copies(mode, src_refs, land_refs, outs[:ns], outs[ns:2 * ns], waiting=False):
            cp.start()
        outs[-1][...] = jnp.zeros_like(outs[-1])

    lands = [_hbm(lax.empty(s.shape, s.dtype)) for s in land_shapes]
    outs = pl.pallas_call(
        body, name=name,
        out_shape=(*[pltpu.SemaphoreType.DMA(())] * (2 * ns), *[pltpu.HBM(s.shape, s.dtype) for s in srcs],
                   *[pltpu.HBM(s.shape, s.dtype) for s in land_shapes], jax.ShapeDtypeStruct((8, 128), F32)),
        in_specs=[_HBM_SPEC] * (2 * n) + [_ANY_SPEC],
        out_specs=(*[_SEM_SPEC] * (2 * ns), *[_HBM_SPEC] * (2 * n), _VMEM_SPEC),
        input_output_aliases={i: 2 * ns + i for i in range(2 * n)},
        compiler_params=pltpu.CompilerParams(has_side_effects=_EFFECT),
    )(*[_hbm(s) for s in srcs], *lands, dep)
    return list(outs[:ns]), list(outs[ns:2 * ns]), list(outs[2 * ns:2 * ns + n]), list(outs[2 * ns + n:2 * ns + 2 * n]), outs[-1]


def _xchip_wait(mode, send_sems, recv_sems, srcs, lands, after, name):
    n = len(srcs)
    ns = n * _n_peers(mode)

    def body(*refs):
        src_refs, land_refs = refs[:n], refs[n:2 * n]
        sems = refs[2 * n:2 * n + 2 * ns]
        for cp in _xchip_copies(mode, src_refs, land_refs, sems[:ns], sems[ns:], waiting=True):
            cp.wait_send()
            cp.wait_recv()

    outs = pl.pallas_call(
        body, name=name,
        out_shape=(*[pltpu.HBM(s.shape, s.dtype) for s in srcs], *[pltpu.HBM(s.shape, s.dtype) for s in lands]),
        in_specs=[_HBM_SPEC] * (2 * n) + [_SEM_SPEC] * (2 * ns) + [_ANY_SPEC] * len(after),
        out_specs=tuple([_HBM_SPEC] * (2 * n)),
        input_output_aliases={i: i for i in range(2 * n)},
        compiler_params=pltpu.CompilerParams(has_side_effects=_EFFECT),
    )(*srcs, *lands, *send_sems, *recv_sems, *after)
    return list(outs[:n]), list(outs[n:])


def _sibling_fwd(lands, name):
    n = len(lands)

    def body(*refs):
        outs = refs[n:2 * n]
        send_sems, recv_sems = refs[2 * n:]
        x, y, c = _idx()
        sib = (x, y, 1 - c)
        sends = []
        for a in range(n):
            for k in range(1, N_CHIPS):
                src = 2 * _flip(x, k & 2) + _flip(y, k & 1)
                cp = pltpu.make_async_remote_copy(src_ref=outs[a].at[src, c], dst_ref=outs[a].at[src, c], send_sem=send_sems.at[a, k - 1],
                                                  recv_sem=recv_sems.at[a, k - 1], device_id=sib, device_id_type=MESH)
                cp.start()
                sends.append(cp)
        for a in range(n):
            for k in range(1, N_CHIPS):
                src = 2 * _flip(x, k & 2) + _flip(y, k & 1)
                pltpu.make_async_remote_copy(src_ref=outs[a].at[src, c], dst_ref=outs[a].at[src, 1 - c], send_sem=send_sems.at[a, k - 1],
                                             recv_sem=recv_sems.at[a, k - 1], device_id=sib, device_id_type=MESH).wait_recv()
        for cp in sends:
            cp.wait_send()

    return pl.pallas_call(
        body, name=name,
        out_shape=[jax.ShapeDtypeStruct(s.shape, s.dtype) for s in lands],
        in_specs=[_HBM_SPEC] * n, out_specs=[_HBM_SPEC] * n,
        input_output_aliases={i: i for i in range(n)},
        scratch_shapes=[pltpu.SemaphoreType.DMA((n, N_CHIPS - 1)), pltpu.SemaphoreType.DMA((n, N_CHIPS - 1))],
    )(*lands)


def _sibling_send(halves, name):
    n = len(halves)

    def body(*refs):
        ins, outs = refs[:n], refs[n:2 * n]
        send_sems, recv_sems = refs[2 * n:]
        x, y, c = _idx()
        cps = []
        for a in range(n):
            cp = pltpu.make_async_remote_copy(src_ref=ins[a], dst_ref=outs[a], send_sem=send_sems.at[a], recv_sem=recv_sems.at[a],
                                              device_id=(x, y, 1 - c), device_id_type=MESH)
            cp.start()
            cps.append(cp)
        for cp in cps:
            cp.wait()

    return pl.pallas_call(
        body, name=name,
        out_shape=[jax.ShapeDtypeStruct(h.shape, h.dtype) for h in halves],
        in_specs=[_HBM_SPEC] * n, out_specs=[_HBM_SPEC] * n,
        scratch_shapes=[pltpu.SemaphoreType.DMA((n,)), pltpu.SemaphoreType.DMA((n,))],
    )(*halves)


def _col_full(g):
    k, n = g.shape[1], g.shape[2]
    return g.transpose(1, 0, 2).reshape(k, N_CHIPS * n)


def _col_blocks(w):
    k, n = w.shape
    return w.reshape(k, N_CHIPS, n // N_CHIPS).transpose(1, 0, 2)


def _row_blocks(w):
    k, n = w.shape
    return w.reshape(N_CHIPS, k // N_CHIPS, n)


_UQ_HEAD = MLA_NOPE + MLA_ROPE

_LAT = MLA_QL + MLA_KVL + MLA_ROPE
_POOL_R = len(POOL_WINDOWS) * (POOL_GD // N_CHIPS)

_PIECE_KINDS = {
    "mlp_w1": (D_MODEL, D_MODEL, lambda g: g, _col_blocks),
    "mlp_w2": (D_MODEL, D_MODEL, lambda g: g.reshape(4 * D_MODEL, D_MODEL), _row_blocks),
    "pool_w": (_POOL_R, POOL_GD,
               lambda g: g.reshape(N_CHIPS, len(POOL_WINDOWS), POOL_GD // N_CHIPS, POOL_GD).transpose(1, 0, 2, 3).reshape(len(POOL_WINDOWS), POOL_GD, POOL_GD),
               lambda w: w.reshape(len(POOL_WINDOWS), N_CHIPS, POOL_GD // N_CHIPS, POOL_GD).transpose(1, 0, 2, 3).reshape(N_CHIPS, _POOL_R, POOL_GD)),
    "sgu_w_in": (D_MODEL, 2 * SGU_W // N_CHIPS, _col_full, _col_blocks),
    "sgu_w_out": (SGU_W // N_CHIPS, D_MODEL, lambda g: g.reshape(SGU_W, D_MODEL), _row_blocks),
    "mla_w_dq_dkv": (D_MODEL // N_CHIPS, _LAT, lambda g: jnp.pad(g.reshape(D_MODEL, _LAT), ((0, 0), (0, MLA_LATP - _LAT))),
                     lambda w: _row_blocks(w[:, :_LAT])),
    "mla_w_uq": (MLA_QL, MLA_H * _UQ_HEAD // N_CHIPS,
                 lambda g: jnp.pad(_col_full(g).reshape(MLA_QL, MLA_H, _UQ_HEAD), ((0, 0), (0, 0), (0, MLA_HP - _UQ_HEAD))).reshape(MLA_QL, MLA_H * MLA_HP),
                 lambda w: _col_blocks(w.reshape(MLA_QL, MLA_H, MLA_HP)[:, :, :_UQ_HEAD].reshape(MLA_QL, MLA_H * _UQ_HEAD))),
    "mla_w_ukv": (MLA_KVL, MLA_H * (MLA_NOPE + MLA_V) // N_CHIPS, _col_full, _col_blocks),
    "mla_w_o": (MLA_H * MLA_V // N_CHIPS, D_MODEL, lambda g: g.reshape(MLA_H * MLA_V, D_MODEL), _row_blocks),
}
_MIXER_KINDS = (("pool_w",), ("sgu_w_in", "sgu_w_out"), ("mla_w_dq_dkv", "mla_w_uq", "mla_w_ukv", "mla_w_o"))


def _layer_pieces(i):
    return [(k, i // N_MIXERS) for k in _MIXER_KINDS[i % N_MIXERS]] + [("mlp_w1", i), ("mlp_w2", i)]


def _rope_tables(positions):
    inv_freq = ROPE_THETA ** (-jnp.arange(0, MLA_ROPE, 2, dtype=F32) / MLA_ROPE)
    ang = positions.astype(F32)[:, None] * inv_freq
    cos, sin = jnp.cos(ang), jnp.sin(ang)
    z32, z64 = jnp.zeros_like(cos), jnp.zeros((positions.shape[0], 64), F32)
    return (jnp.concatenate([cos, cos, z64], axis=1), jnp.concatenate([-sin, z32, z64], axis=1), jnp.concatenate([z32, sin, z64], axis=1))


def _local_step(x, positions, target, mod, S, weights_of, grads_of):
    D = D_MODEL
    cc, sa, sb = _rope_tables(positions)
    mods = [[mod[i:i + 1, n * D:(n + 1) * D] for n in range(6)] for i in range(DEPTH)]
    h_dtype = lambda i: F32 if i % N_MIXERS == 0 else BF16
    saved = []
    h = _norm_mod_fwd(x, S["norm_mix_g"][0:1], mods[0][1], mods[0][0], h_dtype(0), "l0_norm1")
    for i in range(DEPTH):
        sh1, sc1, g1, sh2, sc2, g2 = mods[i]
        kind, j = i % N_MIXERS, i // N_MIXERS
        gmlp = S["norm_mlp_g"][i:i + 1]
        W = weights_of(i, "mix", x)
        st = {"x": x}
        norm2 = ((gmlp, "n"), (sc2, "n"), (sh2, "n"))
        if kind == 0:
            x2, pooled, ypre, h2 = _pool_fwd(h, W["pool_w"], S["pool_scale"][j:j + 1], x, g1, gmlp, sc2, sh2, f"l{i}_pool")
            st.update(pooled=pooled, y=ypre)
        elif kind == 1:
            zz = _mm(h, W["sgu_w_in"], out_dtypes=(F32,), name=f"l{i}_sgu_in")
            bs_t = S["sgu_b_s"].T
            gated = _sgu_gate_fwd(zz, S["sgu_ln_g"], S["sgu_ln_b"], S["sgu_w_s"], bs_t, f"l{i}_sgu_gate")
            x2, y, h2 = _mm(gated, W["sgu_w_out"], epi=_epi_residual_norm, extras=((x, "mn"), (g1, "n"), *norm2), out_dtypes=(F32, BF16, BF16),
                            tn=D, name=f"l{i}_sgu_out")
            st.update(h=h, zz=zz, gated=gated, y=y, bs_t=bs_t)
        else:
            lat = _mm(h, W["mla_w_dq_dkv"], out_dtypes=(F32,), name=f"l{i}_mla_lat")
            cqn, ckvn, krot = _mla_lat_fwd(lat, S["mla_q_norm_g"], S["mla_kv_norm_g"], cc, sa, sb, f"l{i}_mla_latn")
            q = _mm(cqn, W["mla_w_uq"], epi=_epi_q_rope, extras=((cc, "m"), (sa, "m"), (sb, "m")), name=f"l{i}_mla_uq")
            k, kt, v, vt = _mla_ukv(ckvn, W["mla_w_ukv"], krot, f"l{i}_mla_ukv")
            o, lse = _attn_fwd(q, k, vt, f"l{i}_attn")
            x2, y, h2 = _mm(o, W["mla_w_o"], epi=_epi_residual_norm, extras=((x, "mn"), (g1, "n"), *norm2), out_dtypes=(F32, BF16, BF16),
                            tn=D, name=f"l{i}_mla_o")
            st.update(h=h, lat=lat, cqn=cqn, ckvn=ckvn, q=q, k=k, kt=kt, v=v, o=o, lse=lse, y=y)
        W = {**W, **weights_of(i, "mlp", x2)}
        z, r2 = _mm(h2, W["mlp_w1"], epi=_epi_sq_relu, out_dtypes=(BF16, BF16), epi_cols=MM_EPI_COLS, tm=MM_TM_WIDE, name=f"l{i}_mlp1")
        W = {**W, **weights_of(i, "mlp2", z)}
        if i + 1 < DEPTH:
            norm1 = ((S["norm_mix_g"][i + 1:i + 2], "n"), (mods[i + 1][1], "n"), (mods[i + 1][0], "n"))
            x3, o2, h = _mm(z, W["mlp_w2"], epi=_epi_residual_norm, extras=((x2, "mn"), (g2, "n"), *norm1), out_dtypes=(F32, BF16, h_dtype(i + 1)),
                            tn=D, name=f"l{i}_mlp2")
        else:
            x3, o2 = _mm(z, W["mlp_w2"], epi=_epi_residual, extras=((x2, "mn"), (g2, "n")), out_dtypes=(F32, BF16), name=f"l{i}_mlp2")
        st.update(x2=x2, h2=h2, z=z, r2=r2, o2=o2, W=W)
        saved.append(st)
        x = x3

    loss, dx, dfinal_g, do2, dg2 = _loss_head(x, target, S["final_g"], saved[-1]["o2"], mods[-1][5], "loss_head")

    gS = {"final_g": dfinal_g, "norm_mix_g": [None] * DEPTH, "norm_mlp_g": [None] * DEPTH, "pool_scale": [None] * 2}
    dmod = [None] * DEPTH
    started = None
    for i in reversed(range(DEPTH)):
        st = saved[i]
        W, gW = st["W"], {}
        sh1, sc1, g1, sh2, sc2, g2 = mods[i]
        kind, j = i % N_MIXERS, i // N_MIXERS
        gmix, gmlp = S["norm_mix_g"][i:i + 1], S["norm_mlp_g"][i:i + 1]
        da = _mm(do2, W["mlp_w2"], tb=True, epi=lambda acc, rt: (acc * rt.astype(F32),), extras=((st["r2"], "mn"),), after=started, epi_cols=MM_EPI_COLS,
                 tm=MM_TM_WIDE, name=f"l{i}_b_dz")
        gW["mlp_w2"] = _mm(st["z"], do2, ta=True, chip_blocks="row", name=f"l{i}_b_dw2")
        dh2 = _mm(da, W["mlp_w1"], tb=True, name=f"l{i}_b_dh2")
        gW["mlp_w1"] = _mm(st["h2"], da, ta=True, chip_blocks="col", name=f"l{i}_b_dw1")
        dx2, dgmlp, dsc2, dsh2, dy, q1 = _norm_mod_bwd(st["x2"], dh2, dx, gmlp, sc2, f"l{i}_b_norm2", res=(st["y"], g1))
        gS["norm_mlp_g"][i] = dgmlp
        if kind == 0:
            dh, dpw, dpsc, dg1 = _pool_bwd(dy, st["pooled"], W["pool_w"], S["pool_scale"][j:j + 1], g1, q1, f"l{i}_b_pool")
            gW["pool_w"] = dpw.astype(BF16)
            gS["pool_scale"][j] = dpsc
        elif kind == 1:
            dg1 = q1
            dgated = _mm(dy, W["sgu_w_out"], tb=True, name=f"l{i}_b_dgated")
            gW["sgu_w_out"] = _mm(st["gated"], dy, ta=True, name=f"l{i}_b_dwout")
            dzz, dws, dbs, dlg, dlb = _sgu_gate_bwd(st["zz"], dgated, S["sgu_ln_g"], S["sgu_ln_b"], S["sgu_w_s"], st["bs_t"], f"l{i}_b_sgu_gate")
            gS.update(sgu_w_s=dws, sgu_b_s=dbs[:, :, 0], sgu_ln_g=dlg, sgu_ln_b=dlb)
            dh = _mm(dzz, W["sgu_w_in"], tb=True, name=f"l{i}_b_dh_sgu")
            gW["sgu_w_in"] = _mm(st["h"], dzz, ta=True, name=f"l{i}_b_dwin")
        else:
            dg1 = q1
            do = _mm(dy, W["mla_w_o"], tb=True, name=f"l{i}_b_do")
            gW["mla_w_o"] = _mm(st["o"], dy, ta=True, name=f"l{i}_b_dwo")
            delta = _attn_delta(do, st["o"], f"l{i}_b_delta")
            dqt, dkv, dkr = _attn_bwd(st["q"], st["k"], st["kt"], st["v"], do, st["lse"], delta, f"l{i}_b_attn")
            dqpad, dkrot = _mla_prep_bwd(dqt, dkr, cc, sa, sb, f"l{i}_b_mla_prep")
            dcqn = _mm(dqpad, W["mla_w_uq"], tb=True, out_dtypes=(F32,), name=f"l{i}_b_dcq")
            gW["mla_w_uq"] = _mm(st["cqn"], dqpad, ta=True, name=f"l{i}_b_dwuq")
            dckvn = _mm(dkv, W["mla_w_ukv"], tb=True, out_dtypes=(F32,), name=f"l{i}_b_dckv")
            gW["mla_w_ukv"] = _mm(st["ckvn"], dkv, ta=True, name=f"l{i}_b_dwukv")
            dlat, dqg, dkvg = _mla_lat_bwd(st["lat"], dcqn, dckvn, dkrot, S["mla_q_norm_g"], S["mla_kv_norm_g"], cc, sa, sb, f"l{i}_b_mla_latn")
            gS.update(mla_q_norm_g=dqg, mla_kv_norm_g=dkvg)
            dh = _mm(dlat, W["mla_w_dq_dkv"], tb=True, name=f"l{i}_b_dh_mla")
            gW["mla_w_dq_dkv"] = _mm(st["h"], dlat, ta=True, name=f"l{i}_b_dwdq")
        if i > 0:
            dx, dgmix, dsc1, dsh1, do2_prev, dg2_prev = _norm_mod_bwd(st["x"], dh, dx2, gmix, sc1, f"l{i}_b_norm1", res=(saved[i - 1]["o2"], mods[i - 1][5]))
        else:
            dx, dgmix, dsc1, dsh1 = _norm_mod_bwd(st["x"], dh, dx2, gmix, sc1, f"l{i}_b_norm1")
        gS["norm_mix_g"][i] = dgmix
        dmod[i] = jnp.concatenate([dsh1, dsc1, dg1, dsh2, dsc2, dg2], axis=1)
        started = grads_of(i, gW, dx)
        if i > 0:
            do2, dg2 = do2_prev, dg2_prev

    for n in ("norm_mix_g", "norm_mlp_g", "pool_scale"):
        gS[n] = jnp.concatenate(gS[n], axis=0)
    return loss, dx, gS, jnp.concatenate(dmod, axis=0)


_SMALL = {
    "norm_mix_g": (DEPTH, D_MODEL), "norm_mlp_g": (DEPTH, D_MODEL), "sgu_ln_g": (1, SGU_W), "sgu_ln_b": (1, SGU_W),
    "sgu_w_s": (SGU_H, SGU_CHUNK, SGU_CHUNK), "sgu_b_s": (SGU_H, SGU_CHUNK), "mla_kv_norm_g": (1, MLA_KVL), "final_g": (1, D_MODEL),
    "pool_scale": (2, D_MODEL), "mla_q_norm_g": (1, MLA_QL), "loss": (1, 128), "dmod": (DEPTH, 6 * D_MODEL),
}
_PACK_W = 1024


def _pack(vals):
    flat = jnp.concatenate([v.reshape(-1) for v in vals])
    rows = -(-flat.shape[0] // (8 * _PACK_W)) * 8
    return jnp.pad(flat, (0, rows * _PACK_W - flat.shape[0])).reshape(rows, _PACK_W)


def _unpack(buf, shapes):
    flat, out, off = buf.reshape(-1), [], 0
    for s in shapes:
        n = math.prod(s)
        out.append(flat[off:off + n].reshape(s))
        off += n
    return out


def kernel(x, c, positions, ada_w, ada_b, norm_mix_g, norm_mlp_g, pool_w, pool_scale, sgu_w_in, sgu_ln_g, sgu_ln_b, sgu_w_s, sgu_b_s, sgu_w_out, mla_w_dq_dkv, mla_q_norm_g, mla_kv_norm_g, mla_w_uq, mla_w_ukv, mla_w_o, mlp_w1, mlp_w2, final_g, loss_target, m_ada_w, m_ada_b, m_norm_mix_g, m_norm_mlp_g, m_pool_w, m_pool_scale, m_sgu_w_in, m_sgu_ln_g, m_sgu_ln_b, m_sgu_w_s, m_sgu_b_s, m_sgu_w_out, m_mla_w_dq_dkv, m_mla_q_norm_g, m_mla_kv_norm_g, m_mla_w_uq, m_mla_w_ukv, m_mla_w_o, m_mlp_w1, m_mlp_w2, m_final_g, v_ada_w, v_ada_b, v_norm_mix_g, v_norm_mlp_g, v_pool_w, v_pool_scale, v_sgu_w_in, v_sgu_ln_g, v_sgu_ln_b, v_sgu_w_s, v_sgu_b_s, v_sgu_w_out, v_mla_w_dq_dkv, v_mla_q_norm_g, v_mla_kv_norm_g, v_mla_w_uq, v_mla_w_ukv, v_mla_w_o, v_mlp_w1, v_mlp_w2, v_final_g):
    P = dict(ada_w=ada_w, ada_b=ada_b, norm_mix_g=norm_mix_g, norm_mlp_g=norm_mlp_g, pool_w=pool_w, pool_scale=pool_scale, sgu_w_in=sgu_w_in,
             sgu_ln_g=sgu_ln_g, sgu_ln_b=sgu_ln_b, sgu_w_s=sgu_w_s, sgu_b_s=sgu_b_s, sgu_w_out=sgu_w_out, mla_w_dq_dkv=mla_w_dq_dkv,
             mla_q_norm_g=mla_q_norm_g, mla_kv_norm_g=mla_kv_norm_g, mla_w_uq=mla_w_uq, mla_w_ukv=mla_w_ukv, mla_w_o=mla_w_o, mlp_w1=mlp_w1,
             mlp_w2=mlp_w2, final_g=final_g)
    M = dict(ada_w=m_ada_w, ada_b=m_ada_b, norm_mix_g=m_norm_mix_g, norm_mlp_g=m_norm_mlp_g, pool_w=m_pool_w, pool_scale=m_pool_scale,
             sgu_w_in=m_sgu_w_in, sgu_ln_g=m_sgu_ln_g, sgu_ln_b=m_sgu_ln_b, sgu_w_s=m_sgu_w_s, sgu_b_s=m_sgu_b_s, sgu_w_out=m_sgu_w_out,
             mla_w_dq_dkv=m_mla_w_dq_dkv, mla_q_norm_g=m_mla_q_norm_g, mla_kv_norm_g=m_mla_kv_norm_g, mla_w_uq=m_mla_w_uq, mla_w_ukv=m_mla_w_ukv,
             mla_w_o=m_mla_w_o, mlp_w1=m_mlp_w1, mlp_w2=m_mlp_w2, final_g=m_final_g)
    V = dict(ada_w=v_ada_w, ada_b=v_ada_b, norm_mix_g=v_norm_mix_g, norm_mlp_g=v_norm_mlp_g, pool_w=v_pool_w, pool_scale=v_pool_scale,
             sgu_w_in=v_sgu_w_in, sgu_ln_g=v_sgu_ln_g, sgu_ln_b=v_sgu_ln_b, sgu_w_s=v_sgu_w_s, sgu_b_s=v_sgu_b_s, sgu_w_out=v_sgu_w_out,
             mla_w_dq_dkv=v_mla_w_dq_dkv, mla_q_norm_g=v_mla_q_norm_g, mla_kv_norm_g=v_mla_kv_norm_g, mla_w_uq=v_mla_w_uq, mla_w_ukv=v_mla_w_ukv,
             mla_w_o=v_mla_w_o, mlp_w1=v_mlp_w1, mlp_w2=v_mlp_w2, final_g=v_final_g)
    order = list(P)
    xi, yi, ci = _idx()
    chip = 2 * xi + yi
    D = D_MODEL
    n_ada = ada_w.shape[2]

    pre = _allgather8(_pack([c, pool_scale, mla_q_norm_g]), "ag_small")
    flat = pre.reshape(N_DEV, -1)
    c_all = flat[:, :D]
    ps_all = flat[0::2, D:D + 2 * (D // N_CHIPS)].reshape(N_CHIPS, 2, D // N_CHIPS).transpose(1, 0, 2).reshape(2, D)
    q0 = D + 2 * (D // N_CHIPS)
    qg_all = flat[0::2, q0:q0 + MLA_QL // N_CHIPS].reshape(1, MLA_QL)

    ada_b_loc = lax.dynamic_slice_in_dim(ada_b, chip * n_ada, n_ada, axis=1)[:, None, :]
    modp = _ada_fwd(c_all, ada_w, ada_b_loc, "ada_fwd")
    mod = _mod_exchange(modp.transpose(1, 0, 2), "mod_exchange").transpose(1, 0, 2).reshape(DEPTH, 6 * D)

    S = dict(norm_mix_g=norm_mix_g, norm_mlp_g=norm_mlp_g, pool_scale=ps_all, sgu_ln_g=sgu_ln_g, sgu_ln_b=sgu_ln_b, sgu_w_s=sgu_w_s[0],
             sgu_b_s=sgu_b_s[0], mla_q_norm_g=qg_all, mla_kv_norm_g=mla_kv_norm_g, final_g=final_g[None, :])
    cidx, ownidx = jnp.reshape(ci, (1,)).astype(jnp.int32), jnp.reshape(N_CHIPS * ci + chip, (1,)).astype(jnp.int32)
    view2d = lambda a: a.reshape(-1, a.shape[-1])

    def piece_rows(kind, blk):
        r = _PIECE_KINDS[kind][0]
        return blk * r, r

    groups = [_layer_pieces(0)[:-2], _layer_pieces(0)[-2:-1], _layer_pieces(0)[-1:], _layer_pieces(1)[:-2], _layer_pieces(1)[-2:],
              _layer_pieces(2), _layer_pieces(3)]
    start_after = {1: (3, 4), 3: (5,), 5: (6,)}
    gathers = {}

    def gather_start(g, dep):
        srcs, shapes = [], []
        for kind, blk in groups[g]:
            r0, r = piece_rows(kind, blk)
            cdim = _PIECE_KINDS[kind][1]
            srcs.append(view2d(P[kind])[r0:r0 + r].astype(BF16).reshape(2, r // 2, cdim))
            shapes.append(jax.ShapeDtypeStruct((N_CHIPS, 2, r // 2, cdim), BF16))
        gathers[g] = _xchip_start("gather", srcs, shapes, dep, f"ag_start_g{g}")

    def gather_finish(g, after):
        ssem, rsem, srcs, lands, _ = gathers.pop(g)
        deps = [after]
        for nxt in start_after.get(g, ()):
            gather_start(nxt, deps[-1])
            deps.append(gathers[nxt][-1])
        srcs, lands = _xchip_wait("gather", ssem, rsem, srcs, lands, deps, f"ag_wait_g{g}")
        lands = _sibling_fwd(lands, f"ag_sibling_g{g}")
        W = {}
        for (kind, _), s, land in zip(groups[g], srcs, lands, strict=True):
            r, cdim, to_full, _ = _PIECE_KINDS[kind]
            W[kind] = to_full(lax.dynamic_update_index_in_dim(land, s, chip, 0).reshape(N_CHIPS, r, cdim))
        return W

    def weights_of(i, part, x_i):
        g = {(0, "mix"): 0, (0, "mlp"): 1, (0, "mlp2"): 2, (1, "mix"): 3, (1, "mlp"): 4, (2, "mix"): 5, (3, "mix"): 6}.get((i, part))
        return {} if g is None else gather_finish(g, x_i)

    scatters = {}
    bufs = {n: tuple(lax.empty(view2d(P[n]).shape, F32) for _ in range(4)) for n in _PIECE_KINDS}

    def scatter_start(i, gW, dep):
        pcs = _layer_pieces(i)
        blocked = []
        for kind, _ in pcs:
            r, cdim, _, to_blocks = _PIECE_KINDS[kind]
            g = gW[kind]
            blocked.append(g if g.ndim == 4 else to_blocks(g).reshape(N_CHIPS, 2, r // 2, cdim).transpose(1, 0, 2, 3))
        shapes = [jax.ShapeDtypeStruct((N_DEV - 1, *b.shape[2:]), BF16) for b in blocked]
        scatters[i] = (pcs, *_xchip_start("scatter8", blocked, shapes, dep, f"rs_start_l{i}"))
        return scatters[i][-1]

    def scatter_finish(i, after):
        pcs, ssem, rsem, blocked, lands, _ = scatters.pop(i)
        blocked, lands = _xchip_wait("scatter8", ssem, rsem, blocked, lands, after, f"rs_wait_l{i}")
        halves = [_sum_sel(ownidx, b.reshape(2 * N_CHIPS, *b.shape[2:]), [l], f"rs_sum_l{i}_{kind}", F32)
                  for (kind, _), b, l in zip(pcs, blocked, lands, strict=True)]
        got = _sibling_send(halves, f"rs_merge_l{i}")
        for (kind, blk), mine, other in zip(pcs, halves, got, strict=True):
            r0, _ = piece_rows(kind, blk)
            bufs[kind] = tuple(_adamw_piece(cidx, view2d(P[kind]), view2d(M[kind]), view2d(V[kind]), mine, other, bufs[kind], r0,
                                            f"adamw_l{i}_{kind}"))
        return lands[0]

    first_layer = {}

    def grads_of(i, gW, dx_i):
        if i == 0:
            first_layer.update(gW)
            return None
        dep = scatter_finish(i + 2, [dx_i]) if i + 2 in scatters else dx_i
        return scatter_start(i, gW, dep)

    gather_start(0, mod)
    gather_start(1, gathers[0][-1])
    gather_start(2, gathers[1][-1])
    mod = mod + gathers[2][-1][0, 0]
    loss_l, dx, gS, dmod = _local_step(x[0], positions[0], loss_target[0], mod, S, weights_of, grads_of)

    gS["dmod"] = dmod
    gS["loss"] = loss_l
    packed = _pack([gS[n] for n in _SMALL])
    sg = _xchip_start("all8", [packed], [jax.ShapeDtypeStruct((N_DEV, *packed.shape), F32)], dx, "sg_start")
    tok0 = scatter_start(0, first_layer, sg[-1])[0, 0]
    done = lambda layer: [bufs[kind][0] for kind, _ in _layer_pieces(layer)]
    scatter_finish(2, [dx, scatters[0][-1]])
    scatter_finish(1, done(2))
    sg_src, sg_land = _xchip_wait("all8", sg[0], sg[1], sg[2], sg[3], done(1), "sg_wait")
    small = lax.dynamic_update_index_in_dim(sg_land[0], sg_src[0], 4 * xi + 2 * yi + ci, 0) + tok0
    small_sum = _unpack(_sum_lead([small], "sum_small_grads"), list(_SMALL.values()))
    G = dict(zip(_SMALL, small_sum, strict=True))
    grads = {
        "ada_b": G["dmod"], "norm_mix_g": G["norm_mix_g"], "norm_mlp_g": G["norm_mlp_g"], "sgu_ln_g": G["sgu_ln_g"], "sgu_ln_b": G["sgu_ln_b"],
        "sgu_w_s": G["sgu_w_s"][None], "sgu_b_s": G["sgu_b_s"][None], "mla_kv_norm_g": G["mla_kv_norm_g"], "final_g": G["final_g"][0],
        "pool_scale": lax.dynamic_slice_in_dim(G["pool_scale"], chip * (D // N_CHIPS), D // N_CHIPS, axis=1),
        "mla_q_norm_g": lax.dynamic_slice_in_dim(G["mla_q_norm_g"], chip * (MLA_QL // N_CHIPS), MLA_QL // N_CHIPS, axis=1),
    }
    dmod_all = _unpack(small, [(N_DEV,) + (small.shape[1] * _PACK_W,)])[0]
    off = sum(math.prod(s) for n, s in _SMALL.items() if n != "dmod")
    dmod_all = dmod_all[:, off:off + DEPTH * 6 * D].reshape(N_DEV, DEPTH, 6 * D)
    dmod_loc = lax.dynamic_slice_in_dim(dmod_all, chip * n_ada, n_ada, axis=2).transpose(1, 0, 2)
    deltas, new_m, new_v = {}, {}, {}
    grads["ada_w"], deltas["ada_w"], new_m["ada_w"], new_v["ada_w"] = _ada_bwd_adamw(c_all.T, dmod_loc, ada_w, m_ada_w, v_ada_w, "adamw_ada_w")
    for n in order:
        if n not in _PIECE_KINDS and n != "ada_w":
            deltas[n], new_m[n], new_v[n] = _adamw(P[n], grads[n].reshape(P[n].shape), M[n], V[n], f"adamw_{n}")
    scatter_finish(0, [deltas["ada_w"], deltas["sgu_w_s"]] + [bufs[n][0] for n in ("mlp_w1", "mlp_w2", "sgu_w_in", "mla_w_o")])
    for n in _PIECE_KINDS:
        grads[n], deltas[n], new_m[n], new_v[n] = (b.reshape(P[n].shape) for b in bufs[n])
    return (G["loss"][0, 0], dx[None], *[grads[n].reshape(P[n].shape) for n in order], *[deltas[n] for n in order], *[new_m[n] for n in order],
            *[new_v[n] for n in order])
```

```python
import math

import jax
import jax.numpy as jnp
from jax import lax
from jax.experimental import pallas as pl
from jax.experimental.pallas import tpu as pltpu

F32, BF16 = jnp.float32, jnp.bfloat16
MESH = pl.DeviceIdType.MESH

D_MODEL = 1024
DEPTH = 4
N_MIXERS = 3
POOL_WINDOWS = (2, 4, 8, 16)
POOL_GD = D_MODEL // len(POOL_WINDOWS)
POOL_HALO = 16
SGU_CHUNK = 128
SGU_W = D_MODEL
SGU_HD = 128
SGU_H = SGU_W // SGU_HD
MLA_H = 16
MLA_QL = 256
MLA_KVL = 128
MLA_NOPE = 128
MLA_ROPE = 64
MLA_V = 128
MLA_HP = 256
MLA_LATP = 512
ROPE_THETA = 10000.0
RMS_EPS = 1e-6
LN_EPS = 1e-5
SM_SCALE = (MLA_NOPE + MLA_ROPE) ** -0.5
NEG_INF = -1e30
ADAM_LR, ADAM_B1, ADAM_B2, ADAM_EPS, ADAM_WD, ADAM_STEP = 0.001, 0.9, 0.999, 1e-08, 0.01, 10
N_CHIPS = 4
N_DEV = 8
ROW_TILE = 512
ATT_TILE = 512
ATT_SUB = 256
ATT_FWD_HEADS = 4
ATT_BWD_HEADS = 2
MM_EPI_COLS = 256
MM_TM_WIDE = 2048
MM_VMEM_BUDGET = 40 << 20


def _idx():
    return lax.axis_index("x"), lax.axis_index("y"), lax.axis_index("c")


def _mm(a, b, *, name, ta=False, tb=False, epi=None, extras=(), out_dtypes=(BF16,), tm=1024, tn=1024, tk=1024, chip_blocks=None, after=None,
        epi_cols=None):
    if ta:
        K, M = a.shape
    else:
        M, K = a.shape
    b_chips = b.ndim == 3
    if b_chips:
        assert b.shape[0] == N_CHIPS
        Kb, N = (N_CHIPS * b.shape[2], b.shape[1]) if tb else (b.shape[1], N_CHIPS * b.shape[2])
    elif tb:
        N, Kb = b.shape
    else:
        Kb, N = b.shape
    assert K == Kb, (a.shape, b.shape, ta, tb)
    if b_chips and not tb:
        tn = min(tn, N // N_CHIPS)
    if chip_blocks == "col":
        tm, tn = min(tm, M // 2), min(tn, N // N_CHIPS)
    elif chip_blocks == "row":
        tm = min(tm, M // N_CHIPS // 2)
    tm, tn, tk = min(tm, M), min(tn, N), min(tk, K)

    def vmem_bytes(tm_, tk_):
        per_mn = sum(arr.dtype.itemsize for arr, kind in extras if kind == "mn") + sum(jnp.dtype(dt).itemsize for dt in out_dtypes)
        return 2 * (tm_ * tk_ * a.dtype.itemsize + tk_ * tn * b.dtype.itemsize + tm_ * tn * per_mn)

    if vmem_bytes(tm, K) <= MM_VMEM_BUDGET:
        tk = K
    elif tm >= 512 and vmem_bytes(tm // 2, K) <= MM_VMEM_BUDGET:
        tm, tk = tm // 2, K
    assert M % tm == 0 and N % tn == 0 and K % tk == 0, (M, N, K, tm, tn, tk)
    nk = K // tk
    assert epi_cols is None or (nk == 1 and not ta and not (b_chips and tb) and tn % epi_cols == 0)
    a_spec = pl.BlockSpec((tk, tm), lambda i, j, k: (k, i)) if ta else pl.BlockSpec((tm, tk), lambda i, j, k: (i, k))
    b_spec = pl.BlockSpec((tn, tk), lambda i, j, k: (j, k)) if tb else pl.BlockSpec((tk, tn), lambda i, j, k: (k, j))
    if b_chips and tb:
        assert nk == 1 and not ta
        b_spec = pl.BlockSpec((N_CHIPS, tn, K // N_CHIPS), lambda i, j, k: (0, j, 0))
    elif b_chips:
        per = N // N_CHIPS // tn
        b_spec = pl.BlockSpec((None, tk, tn), lambda i, j, k: (j // per, k, j % per))
    ex_specs = []
    for arr, kind in extras:
        if kind == "mn":
            ex_specs.append(pl.BlockSpec((tm, tn), lambda i, j, k: (i, j)))
        elif kind == "n":
            ex_specs.append(pl.BlockSpec((1, tn), lambda i, j, k: (0, j)))
        else:
            ex_specs.append(pl.BlockSpec((tm, arr.shape[1]), lambda i, j, k: (i, 0)))
    n_ex, n_out = len(extras), len(out_dtypes)
    n_in = 2 + n_ex + (after is not None)
    dims = (((0 if ta else 1,), (1 if tb else 0,)), ((), ()))

    def body(*refs):
        a_ref, b_ref = refs[0], refs[1]
        ex_refs = refs[2:2 + n_ex]
        out_refs = refs[n_in:n_in + n_out]
        if b_chips and tb:
            kc = K // N_CHIPS
            part = None
            for cb in range(N_CHIPS):
                p = lax.dot_general(a_ref[:, cb * kc:(cb + 1) * kc].astype(BF16), b_ref[cb].astype(BF16), dims, preferred_element_type=F32)
                part = p if part is None else part + p
        elif epi_cols is not None:
            av = a_ref[...].astype(BF16)
            chunk = lambda cc: lax.dot_general(av, (b_ref[cc * epi_cols:(cc + 1) * epi_cols, :] if tb else b_ref[:, cc * epi_cols:(cc + 1) * epi_cols])
                                               .astype(BF16), dims, preferred_element_type=F32)
            acc = chunk(0)
            for cc in range(tn // epi_cols):
                nxt = chunk(cc + 1) if cc + 1 < tn // epi_cols else None
                cs = slice(cc * epi_cols, (cc + 1) * epi_cols)
                for r, o in zip(out_refs, epi(acc, *[r[:, cs] for r in ex_refs]), strict=True):
                    r[:, cs] = o.astype(r.dtype)
                acc = nxt
            return
        else:
            part = lax.dot_general(a_ref[...].astype(BF16), b_ref[...].astype(BF16), dims, preferred_element_type=F32)

        def finish(acc):
            outs = epi(acc, *[r[...] for r in ex_refs]) if epi is not None else (acc,)
            for r, o in zip(out_refs, outs, strict=True):
                r[...] = o.astype(r.dtype)

        if nk == 1:
            finish(part)
        else:
            acc_ref = refs[-1]
            k = pl.program_id(2)

            @pl.when(k == 0)
            def _():
                acc_ref[...] = part

            @pl.when(k > 0)
            def _():
                acc_ref[...] += part

            @pl.when(k == nk - 1)
            def _():
                finish(acc_ref[...])

    out_specs = [pl.BlockSpec((tm, tn), lambda i, j, k: (i, j)) for _ in range(n_out)]
    out_shape = [jax.ShapeDtypeStruct((M, N), dt) for dt in out_dtypes]
    if chip_blocks is not None:
        assert n_out == 1
        if chip_blocks == "col":
            rh, cb = M // 2 // tm, N // N_CHIPS // tn
            out_specs = [pl.BlockSpec((None, None, tm, tn), lambda i, j, k: (i // rh, j // cb, i % rh, j % cb))]
            out_shape = [jax.ShapeDtypeStruct((2, N_CHIPS, M // 2, N // N_CHIPS), out_dtypes[0])]
        else:
            rh = M // N_CHIPS // 2 // tm
            out_specs = [pl.BlockSpec((None, None, tm, tn), lambda i, j, k: ((i // rh) % 2, i // (2 * rh), i % rh, j))]
            out_shape = [jax.ShapeDtypeStruct((2, N_CHIPS, M // N_CHIPS // 2, N), out_dtypes[0])]
    outs = pl.pallas_call(
        body,
        name=name,
        grid=(M // tm, N // tn, nk),
        in_specs=[a_spec, b_spec, *ex_specs] + ([pl.BlockSpec(memory_space=pl.ANY)] if after is not None else []),
        out_specs=out_specs,
        out_shape=out_shape,
        scratch_shapes=[pltpu.VMEM((tm, tn), F32)] if nk > 1 else [],
        compiler_params=pltpu.CompilerParams(dimension_semantics=("parallel", "parallel", "arbitrary")),
    )(a, b, *[arr for arr, _ in extras], *([after] if after is not None else []))
    return outs[0] if n_out == 1 else tuple(outs)


def _epi_sq_relu(acc):
    r = jnp.maximum(acc, 0.0)
    return r * r, 2.0 * r


def _epi_residual(acc, x, g):
    return x + g * acc, acc


def _rms_mod(xv, gain, sc, sh):
    r = lax.rsqrt(jnp.mean(xv * xv, axis=-1, keepdims=True) + RMS_EPS)
    return ((xv * r) * gain) * (1.0 + sc) + sh


def _epi_residual_norm(acc, x, g, gain, sc, sh):
    xn = x + g * acc
    return xn, acc, _rms_mod(xn, gain, sc, sh)


def _row_spec(tr, d):
    return pl.BlockSpec((tr, d), lambda i: (i, 0))


def _vec_spec(d):
    return pl.BlockSpec((1, d), lambda i: (0, 0))


def _colsum(v):
    return jnp.sum(v, axis=0, keepdims=True)


def _norm_mod_fwd(x, gain, sc, sh, out_dtype, name):
    T, D = x.shape
    tr = min(T, ROW_TILE)

    def body(x_ref, g_ref, sc_ref, sh_ref, o_ref):
        o_ref[...] = _rms_mod(x_ref[...], g_ref[...], sc_ref[...], sh_ref[...]).astype(o_ref.dtype)

    return pl.pallas_call(
        body, name=name, grid=(T // tr,),
        in_specs=[_row_spec(tr, D), _vec_spec(D), _vec_spec(D), _vec_spec(D)],
        out_specs=_row_spec(tr, D),
        out_shape=jax.ShapeDtypeStruct((T, D), out_dtype),
        compiler_params=pltpu.CompilerParams(dimension_semantics=("parallel",)),
    )(x, gain, sc, sh)


def _norm_mod_bwd(x, dh, dres, gain, sc, name, res=None):
    T, D = x.shape
    tr = min(T, ROW_TILE)

    def body(x_ref, dh_ref, dres_ref, g_ref, sc_ref, *refs):
        dx_ref, dg_ref, dsc_ref, dsh_ref = refs[-6:-2] if res is not None else refs

        @pl.when(pl.program_id(0) == 0)
        def _():
            dg_ref[...] = jnp.zeros_like(dg_ref)
            dsc_ref[...] = jnp.zeros_like(dsc_ref)
            dsh_ref[...] = jnp.zeros_like(dsh_ref)
            if res is not None:
                refs[-1][...] = jnp.zeros_like(refs[-1])

        xv = x_ref[...]
        r = lax.rsqrt(jnp.mean(xv * xv, axis=-1, keepdims=True) + RMS_EPS)
        xn = xv * r
        dhv = dh_ref[...].astype(F32)
        dsh_ref[...] += _colsum(dhv)
        dsc_ref[...] += _colsum(dhv * (xn * g_ref[...]))
        dt = dhv * (1.0 + sc_ref[...])
        dg_ref[...] += _colsum(dt * xn)
        dxn = dt * g_ref[...]
        dxv = dres_ref[...] + r * (dxn - xn * jnp.mean(dxn * xn, axis=-1, keepdims=True))
        dx_ref[...] = dxv
        if res is not None:
            y_ref, gr_ref, dy_ref, q_ref = refs[0], refs[1], refs[-2], refs[-1]
            dy_ref[...] = (gr_ref[...] * dxv).astype(BF16)
            q_ref[...] += _colsum(dxv * y_ref[...].astype(F32))

    extra_in, extra_spec = ([], []) if res is None else (list(res), [_row_spec(tr, D), _vec_spec(D)])
    return pl.pallas_call(
        body, name=name, grid=(T // tr,),
        in_specs=[_row_spec(tr, D), _row_spec(tr, D), _row_spec(tr, D), _vec_spec(D), _vec_spec(D), *extra_spec],
        out_specs=[_row_spec(tr, D), _vec_spec(D), _vec_spec(D), _vec_spec(D)] + ([_row_spec(tr, D), _vec_spec(D)] if res is not None else []),
        out_shape=[jax.ShapeDtypeStruct((T, D), F32)] + [jax.ShapeDtypeStruct((1, D), F32)] * 3
        + ([jax.ShapeDtypeStruct((T, D), BF16), jax.ShapeDtypeStruct((1, D), F32)] if res is not None else []),
        compiler_params=pltpu.CompilerParams(dimension_semantics=("arbitrary",)),
    )(x, dh, dres, gain, sc, *extra_in)


def _loss_head(x, target, gain, y, g, name):
    T, D = x.shape
    tr = min(T, ROW_TILE)

    def body(x_ref, t_ref, g_ref, y_ref, gr_ref, loss_ref, dx_ref, dg_ref, dy_ref, q_ref):
        @pl.when(pl.program_id(0) == 0)
        def _():
            loss_ref[...] = jnp.zeros_like(loss_ref)
            dg_ref[...] = jnp.zeros_like(dg_ref)
            q_ref[...] = jnp.zeros_like(q_ref)

        xv = x_ref[...]
        r = lax.rsqrt(jnp.mean(xv * xv, axis=-1, keepdims=True) + RMS_EPS)
        xn = xv * r
        err = xn * g_ref[...] - t_ref[...]
        row = jnp.mean(err * err, axis=-1, keepdims=True)
        loss_ref[...] += 0.5 * jnp.sum(row, axis=0, keepdims=True)
        dy = err * (1.0 / D)
        dg_ref[...] += _colsum(dy * xn)
        dxn = dy * g_ref[...]
        dxv = r * (dxn - xn * jnp.mean(dxn * xn, axis=-1, keepdims=True))
        dx_ref[...] = dxv
        dy_ref[...] = (gr_ref[...] * dxv).astype(BF16)
        q_ref[...] += _colsum(dxv * y_ref[...].astype(F32))

    return pl.pallas_call(
        body, name=name, grid=(T // tr,),
        in_specs=[_row_spec(tr, D), _row_spec(tr, D), _vec_spec(D), _row_spec(tr, D), _vec_spec(D)],
        out_specs=[_vec_spec(128), _row_spec(tr, D), _vec_spec(D), _row_spec(tr, D), _vec_spec(D)],
        out_shape=[jax.ShapeDtypeStruct((1, 128), F32), jax.ShapeDtypeStruct((T, D), F32), jax.ShapeDtypeStruct((1, D), F32),
                   jax.ShapeDtypeStruct((T, D), BF16), jax.ShapeDtypeStruct((1, D), F32)],
        compiler_params=pltpu.CompilerParams(dimension_semantics=("arbitrary",)),
    )(x, target, gain, y, g)


def _pool_fwd(h, w, scale, x, g1, gmlp, sc2, sh2, name):
    T, D = h.shape
    tr = min(T, ROW_TILE)

    def body(h_ref, w_ref, sc_ref, x_ref, g_ref, gm_ref, sc2_ref, sh2_ref, x2_ref, pooled_ref, ypre_ref, h2_ref, halo_ref):
        i = pl.program_id(0)

        @pl.when(i == 0)
        def _():
            halo_ref[...] = jnp.zeros_like(halo_ref)

        hv = h_ref[...]
        buf = jnp.concatenate([halo_ref[...], hv], axis=0)
        halo_ref[...] = hv[tr - POOL_HALO:, :]
        t = (i * tr + lax.broadcasted_iota(jnp.int32, (tr, 1), 0)).astype(F32)
        for gi, win in enumerate(POOL_WINDOWS):
            cols = slice(gi * POOL_GD, (gi + 1) * POOL_GD)
            val = buf[:, cols]
            sh = 1
            while sh < win:
                val = val + pltpu.roll(val, sh, axis=0)
                sh *= 2
            pooled = val[POOL_HALO:, :] / jnp.minimum(t + 1.0, float(win)) - hv[:, cols]
            pb = pooled.astype(BF16)
            pooled_ref[:, cols] = pb
            yp = jnp.dot(pb, w_ref[gi], preferred_element_type=F32)
            ypre_ref[:, cols] = yp.astype(BF16)
            x2_ref[:, cols] = x_ref[:, cols] + g_ref[:, cols] * (yp * sc_ref[:, cols])
        h2_ref[...] = _rms_mod(x2_ref[...], gm_ref[...], sc2_ref[...], sh2_ref[...]).astype(BF16)

    return pl.pallas_call(
        body, name=name, grid=(T // tr,),
        in_specs=[_row_spec(tr, D), pl.BlockSpec(w.shape, lambda i: (0, 0, 0)), _vec_spec(D), _row_spec(tr, D), _vec_spec(D), _vec_spec(D),
                  _vec_spec(D), _vec_spec(D)],
        out_specs=[_row_spec(tr, D)] * 4,
        out_shape=[jax.ShapeDtypeStruct((T, D), F32), jax.ShapeDtypeStruct((T, D), BF16), jax.ShapeDtypeStruct((T, D), BF16),
                   jax.ShapeDtypeStruct((T, D), BF16)],
        scratch_shapes=[pltpu.VMEM((POOL_HALO, D), F32)],
        compiler_params=pltpu.CompilerParams(dimension_semantics=("arbitrary",)),
    )(h, w, scale, x, g1, gmlp, sc2, sh2)


def _pool_bwd(dy, pooled, w, scale, g1, q, name):
    T, D = dy.shape
    tr = min(T, ROW_TILE)
    nt = T // tr
    ltot = tr + POOL_HALO

    def body(dy_ref, pooled_ref, w_ref, sc_ref, g_ref, q_ref, dh_ref, dw_ref, dsc_ref, dg_ref, halo_ref):
        i = pl.program_id(0)

        @pl.when(i == 0)
        def _():
            halo_ref[...] = jnp.zeros_like(halo_ref)
            dw_ref[...] = jnp.zeros_like(dw_ref)
            dsc_ref[...] = g_ref[...] * q_ref[...]
            dg_ref[...] = sc_ref[...] * q_ref[...]

        t = ((nt - 1 - i) * tr + lax.broadcasted_iota(jnp.int32, (tr, 1), 0)).astype(F32)
        for gi, win in enumerate(POOL_WINDOWS):
            cols = slice(gi * POOL_GD, (gi + 1) * POOL_GD)
            dyb = (dy_ref[:, cols].astype(F32) * sc_ref[:, cols]).astype(BF16)
            dw_ref[gi] += lax.dot_general(pooled_ref[:, cols], dyb, (((0,), (0,)), ((), ())), preferred_element_type=F32)
            dpool = lax.dot_general(dyb, w_ref[gi], (((1,), (1,)), ((), ())), preferred_element_type=F32)
            qv = dpool / jnp.minimum(t + 1.0, float(win))
            val = jnp.concatenate([qv, halo_ref[:, cols]], axis=0)
            halo_ref[:, cols] = qv[:POOL_HALO, :]
            sh = 1
            while sh < win:
                val = val + pltpu.roll(val, ltot - sh, axis=0)
                sh *= 2
            dh_ref[:, cols] = (val[:tr, :] - dpool).astype(BF16)

    rev = pl.BlockSpec((tr, D), lambda i: (nt - 1 - i, 0))
    return pl.pallas_call(
        body, name=name, grid=(nt,),
        in_specs=[rev, rev, pl.BlockSpec(w.shape, lambda i: (0, 0, 0)), _vec_spec(D), _vec_spec(D), _vec_spec(D)],
        out_specs=[rev, pl.BlockSpec(w.shape, lambda i: (0, 0, 0)), _vec_spec(D), _vec_spec(D)],
        out_shape=[jax.ShapeDtypeStruct((T, D), BF16), jax.ShapeDtypeStruct(w.shape, F32),
                   jax.ShapeDtypeStruct((1, D), F32), jax.ShapeDtypeStruct((1, D), F32)],
        scratch_shapes=[pltpu.VMEM((POOL_HALO, D), F32)],
        compiler_params=pltpu.CompilerParams(dimension_semantics=("arbitrary",)),
    )(dy, pooled, w, scale, g1, q)


_INV_SQRT2 = 0.7071067811865476
_INV_SQRT2PI = 0.3989422804014327


def _gelu(v):
    return 0.5 * v * (1.0 + lax.erf(v * _INV_SQRT2))


def _gelu_grad(v):
    return 0.5 * (1.0 + lax.erf(v * _INV_SQRT2)) + v * jnp.exp(-0.5 * v * v) * _INV_SQRT2PI


def _sgu_ln(v, g, b):
    mu = jnp.mean(v, axis=-1, keepdims=True)
    xc = v - mu
    rstd = lax.rsqrt(jnp.mean(xc * xc, axis=-1, keepdims=True) + LN_EPS)
    xh = xc * rstd
    return xh, rstd, xh * g + b


def _tril_mask():
    return lax.broadcasted_iota(jnp.int32, (SGU_CHUNK, SGU_CHUNK), 0) >= lax.broadcasted_iota(jnp.int32, (SGU_CHUNK, SGU_CHUNK), 1)


SGU_TILE = 256


def _sgu_gate_fwd(zz, ln_g, ln_b, ws, bs_t, name):
    T = zz.shape[0]
    ts = min(T, SGU_TILE)

    def body(zz_ref, g_ref, b_ref, ws_ref, bs_ref, out_ref):
        z = _gelu(zz_ref[...])
        u = z[:, :SGU_W]
        _, _, vn = _sgu_ln(z[:, SGU_W:], g_ref[...], b_ref[...])
        vb = vn.astype(BF16)
        tril = _tril_mask()
        for hh in range(SGU_H):
            wm = jnp.where(tril, ws_ref[hh], 0.0).astype(BF16)
            bcol = bs_ref[:, hh:hh + 1]
            cs = slice(hh * SGU_HD, (hh + 1) * SGU_HD)
            for j in range(ts // SGU_CHUNK):
                rs = slice(j * SGU_CHUNK, (j + 1) * SGU_CHUNK)
                mixed = jnp.dot(wm, vb[rs, cs], preferred_element_type=F32) + bcol
                out_ref[rs, cs] = (u[rs, cs] * mixed).astype(BF16)

    return pl.pallas_call(
        body, name=name, grid=(T // ts,),
        in_specs=[_row_spec(ts, 2 * SGU_W), _vec_spec(SGU_W), _vec_spec(SGU_W),
                  pl.BlockSpec(ws.shape, lambda i: (0, 0, 0)), pl.BlockSpec(bs_t.shape, lambda i: (0, 0))],
        out_specs=_row_spec(ts, SGU_W),
        out_shape=jax.ShapeDtypeStruct((T, SGU_W), BF16),
        compiler_params=pltpu.CompilerParams(dimension_semantics=("parallel",)),
    )(zz, ln_g, ln_b, ws, bs_t)


def _sgu_gate_bwd(zz, dgated, ln_g, ln_b, ws, bs_t, name):
    T = zz.shape[0]
    ts = min(T, SGU_TILE)
    nt = T // ts

    def body(zz_ref, dg_ref, g_ref, b_ref, ws_ref, bs_ref, dzz_ref, dws_ref, dbs_ref, dlg_ref, dlb_ref, dlo_ref, dmx_ref):
        i = pl.program_id(0)

        @pl.when(i == 0)
        def _():
            dws_ref[...] = jnp.zeros_like(dws_ref)
            dmx_ref[...] = jnp.zeros_like(dmx_ref)
            dlg_ref[...] = jnp.zeros_like(dlg_ref)
            dlb_ref[...] = jnp.zeros_like(dlb_ref)

        zzv = zz_ref[...]
        z = _gelu(zzv)
        u = z[:, :SGU_W]
        xh, rstd, vn = _sgu_ln(z[:, SGU_W:], g_ref[...], b_ref[...])
        vb = vn.astype(BF16)
        dgv = dg_ref[...].astype(F32)
        tril = _tril_mask()
        for hh in range(SGU_H):
            wm = jnp.where(tril, ws_ref[hh], 0.0).astype(BF16)
            bcol = bs_ref[:, hh:hh + 1]
            cs = slice(hh * SGU_HD, (hh + 1) * SGU_HD)
            for j in range(ts // SGU_CHUNK):
                rs = slice(j * SGU_CHUNK, (j + 1) * SGU_CHUNK)
                mixed = jnp.dot(wm, vb[rs, cs], preferred_element_type=F32) + bcol
                dmixed = dgv[rs, cs] * u[rs, cs]
                dzz_ref[rs, cs] = (dgv[rs, cs] * mixed * _gelu_grad(zzv[rs, cs])).astype(BF16)
                dmb = dmixed.astype(BF16)
                dws_ref[hh] += lax.dot_general(dmb, vb[rs, cs], (((1,), (1,)), ((), ())), preferred_element_type=F32)
                dmx_ref[hh] += dmixed
                dlo_ref[rs, cs] = lax.dot_general(wm, dmb, (((0,), (0,)), ((), ())), preferred_element_type=F32)
        dlo = dlo_ref[...]
        dlg_ref[...] += _colsum(dlo * xh)
        dlb_ref[...] += _colsum(dlo)
        dxh = dlo * g_ref[...]
        dv = rstd * (dxh - jnp.mean(dxh, axis=-1, keepdims=True) - xh * jnp.mean(dxh * xh, axis=-1, keepdims=True))
        dzz_ref[:, SGU_W:] = (dv * _gelu_grad(zzv[:, SGU_W:])).astype(BF16)

        @pl.when(i == nt - 1)
        def _():
            tril_f = tril.astype(F32)
            for hh in range(SGU_H):
                dws_ref[hh] = dws_ref[hh] * tril_f
                dbs_ref[hh] = jnp.broadcast_to(jnp.sum(dmx_ref[hh], axis=-1, keepdims=True), (SGU_CHUNK, SGU_HD))

    full3 = pl.BlockSpec(ws.shape, lambda i: (0, 0, 0))
    return pl.pallas_call(
        body, name=name, grid=(nt,),
        in_specs=[_row_spec(ts, 2 * SGU_W), _row_spec(ts, SGU_W), _vec_spec(SGU_W), _vec_spec(SGU_W), full3,
                  pl.BlockSpec(bs_t.shape, lambda i: (0, 0))],
        out_specs=[_row_spec(ts, 2 * SGU_W), full3, full3, _vec_spec(SGU_W), _vec_spec(SGU_W)],
        out_shape=[jax.ShapeDtypeStruct((T, 2 * SGU_W), BF16), jax.ShapeDtypeStruct(ws.shape, F32), jax.ShapeDtypeStruct(ws.shape, F32),
                   jax.ShapeDtypeStruct((1, SGU_W), F32), jax.ShapeDtypeStruct((1, SGU_W), F32)],
        scratch_shapes=[pltpu.VMEM((ts, SGU_W), F32), pltpu.VMEM(ws.shape, F32)],
        compiler_params=pltpu.CompilerParams(dimension_semantics=("arbitrary",)),
    )(zz, dgated, ln_g, ln_b, ws, bs_t)


def _rope_fwd(blk, cc, sa, sb):
    return blk * cc + pltpu.roll(blk, 96, axis=1) * sa + pltpu.roll(blk, 32, axis=1) * sb


def _rope_bwd(d, cc, sa, sb):
    return d * cc + pltpu.roll(d * sa, 32, axis=1) + pltpu.roll(d * sb, 96, axis=1)


def _rms(v, g):
    r = lax.rsqrt(jnp.mean(v * v, axis=-1, keepdims=True) + RMS_EPS)
    vn = v * r
    return vn, r, vn * g


def _rms_bwd(dy, vn, r, g):
    dvn = dy * g
    return r * (dvn - vn * jnp.mean(dvn * vn, axis=-1, keepdims=True))


_KV0 = MLA_QL
_KR0 = MLA_QL + MLA_KVL


def _mla_lat_fwd(lat, qg, kvg, cc, sa, sb, name):
    T = lat.shape[0]
    tr = min(T, ROW_TILE)

    def body(lat_ref, qg_ref, kvg_ref, cc_ref, sa_ref, sb_ref, cq_ref, ckv_ref, kr_ref):
        lv = lat_ref[...]
        cq_ref[...] = _rms(lv[:, :_KV0], qg_ref[...])[2].astype(BF16)
        ckv_ref[...] = _rms(lv[:, _KV0:_KR0], kvg_ref[...])[2].astype(BF16)
        kr_ref[...] = _rope_fwd(lv[:, _KR0:], cc_ref[...], sa_ref[...], sb_ref[...])

    return pl.pallas_call(
        body, name=name, grid=(T // tr,),
        in_specs=[_row_spec(tr, MLA_LATP), _vec_spec(MLA_QL), _vec_spec(MLA_KVL), _row_spec(tr, 128), _row_spec(tr, 128), _row_spec(tr, 128)],
        out_specs=[_row_spec(tr, MLA_QL), _row_spec(tr, MLA_KVL), _row_spec(tr, 128)],
        out_shape=[jax.ShapeDtypeStruct((T, MLA_QL), BF16), jax.ShapeDtypeStruct((T, MLA_KVL), BF16), jax.ShapeDtypeStruct((T, 128), F32)],
        compiler_params=pltpu.CompilerParams(dimension_semantics=("parallel",)),
    )(lat, qg, kvg, cc, sa, sb)


def _mla_lat_bwd(lat, dcqn, dckvn, dkrot, qg, kvg, cc, sa, sb, name):
    T = lat.shape[0]
    tr = min(T, ROW_TILE)

    def body(lat_ref, dcq_ref, dckv_ref, dkr_ref, qg_ref, kvg_ref, cc_ref, sa_ref, sb_ref, dlat_ref, dqg_ref, dkvg_ref):
        @pl.when(pl.program_id(0) == 0)
        def _():
            dqg_ref[...] = jnp.zeros_like(dqg_ref)
            dkvg_ref[...] = jnp.zeros_like(dkvg_ref)

        lv = lat_ref[...]
        qn, qr, _ = _rms(lv[:, :_KV0], qg_ref[...])
        kn, kr, _ = _rms(lv[:, _KV0:_KR0], kvg_ref[...])
        dcq = dcq_ref[...]
        dckv = dckv_ref[...]
        dqg_ref[...] += _colsum(dcq * qn)
        dkvg_ref[...] += _colsum(dckv * kn)
        dlat_ref[:, :_KV0] = _rms_bwd(dcq, qn, qr, qg_ref[...]).astype(BF16)
        dlat_ref[:, _KV0:_KR0] = _rms_bwd(dckv, kn, kr, kvg_ref[...]).astype(BF16)
        dlat_ref[:, _KR0:] = _rope_bwd(dkr_ref[...], cc_ref[...], sa_ref[...], sb_ref[...]).astype(BF16)

    return pl.pallas_call(
        body, name=name, grid=(T // tr,),
        in_specs=[_row_spec(tr, MLA_LATP), _row_spec(tr, MLA_QL), _row_spec(tr, MLA_KVL), _row_spec(tr, 128),
                  _vec_spec(MLA_QL), _vec_spec(MLA_KVL), _row_spec(tr, 128), _row_spec(tr, 128), _row_spec(tr, 128)],
        out_specs=[_row_spec(tr, MLA_LATP), _vec_spec(MLA_QL), _vec_spec(MLA_KVL)],
        out_shape=[jax.ShapeDtypeStruct((T, MLA_LATP), BF16), jax.ShapeDtypeStruct((1, MLA_QL), F32), jax.ShapeDtypeStruct((1, MLA_KVL), F32)],
        compiler_params=pltpu.CompilerParams(dimension_semantics=("arbitrary",)),
    )(lat, dcqn, dckvn, dkrot, qg, kvg, cc, sa, sb)


LOG2E = 1.4426950408889634
Q_SCALE = SM_SCALE * LOG2E


def _epi_q_rope(acc, cc, sa, sb):
    out = []
    for hh in range(acc.shape[1] // MLA_HP):
        a, m, b = hh * MLA_HP, hh * MLA_HP + MLA_NOPE, (hh + 1) * MLA_HP
        out += [acc[:, a:m] * Q_SCALE, _rope_fwd(acc[:, m:b], cc, sa, sb) * Q_SCALE]
    return (jnp.concatenate(out, axis=1),)


def _mla_ukv(ckvn, w_ukv, krot, name):
    T = ckvn.shape[0]
    tr = min(T, ATT_TILE)
    hg = ATT_HG
    gw = hg * MLA_HP

    def body(a_ref, w_ref, kr_ref, ko_ref, kt_ref, vo_ref, vt_ref):
        acc = jnp.dot(a_ref[...], w_ref[...], preferred_element_type=F32)
        kr = kr_ref[...]
        krb, krt = kr.astype(BF16), kr.T.astype(BF16)
        for hh in range(hg):
            a, m, b = hh * MLA_HP, hh * MLA_HP + MLA_NOPE, (hh + 1) * MLA_HP
            kn, vh = acc[:, a:m], acc[:, m:b]
            ko_ref[:, a:m] = kn.astype(BF16)
            ko_ref[:, m:b] = krb
            kt_ref[a:m, :] = kn.T.astype(BF16)
            kt_ref[m:b, :] = krt
            vo_ref[:, hh * MLA_V:(hh + 1) * MLA_V] = vh.astype(BF16)
            vt_ref[hh] = vh.T.astype(BF16)

    tk = min(T, ATT_TILE)
    per = tk // tr
    HW = MLA_H * MLA_HP
    return pl.pallas_call(
        body, name=name, grid=(T // tr, MLA_H // hg),
        in_specs=[pl.BlockSpec((tr, MLA_KVL), lambda i, g: (i, 0)), pl.BlockSpec((MLA_KVL, gw), lambda i, g: (0, g)),
                  pl.BlockSpec((tr, 128), lambda i, g: (i, 0))],
        out_specs=[pl.BlockSpec((tr, gw), lambda i, g: (i, g)), pl.BlockSpec((gw, tr), lambda i, g: (g, i)),
                   pl.BlockSpec((tr, hg * MLA_V), lambda i, g: (i, g)),
                   pl.BlockSpec((hg, None, MLA_V, tr), lambda i, g: (g, i // per, 0, i % per))],
        out_shape=[jax.ShapeDtypeStruct((T, HW), BF16), jax.ShapeDtypeStruct((HW, T), BF16), jax.ShapeDtypeStruct((T, MLA_H * MLA_V), BF16),
                   jax.ShapeDtypeStruct((MLA_H, T // tk, MLA_V, tk), BF16)],
        compiler_params=pltpu.CompilerParams(dimension_semantics=("parallel", "parallel")),
    )(ckvn, w_ukv, krot)


ATT_HG = 4


def _mla_prep_bwd(dqt, dkr, cc, sa, sb, name):
    _, nq, _, tq = dqt.shape
    T = nq * tq
    gw = ATT_HG * MLA_HP

    def body(dq_ref, dk_ref, cc_ref, sa_ref, sb_ref, dqp_ref, dkr_ref):
        @pl.when(pl.program_id(1) == 0)
        def _():
            dkr_ref[...] = jnp.zeros_like(dkr_ref)

        cc, sa, sb = cc_ref[...], sa_ref[...], sb_ref[...]
        acc = jnp.zeros((tq, 128), F32)
        for hh in range(ATT_HG):
            a, m, b = hh * MLA_HP, hh * MLA_HP + MLA_NOPE, (hh + 1) * MLA_HP
            dqh = dq_ref[hh].astype(F32).T * SM_SCALE
            dqp_ref[:, a:m] = dqh[:, :MLA_NOPE].astype(BF16)
            dqp_ref[:, m:b] = _rope_bwd(dqh[:, MLA_NOPE:], cc, sa, sb).astype(BF16)
            acc = acc + dk_ref[:, hh * 128:(hh + 1) * 128].astype(F32)
        dkr_ref[...] += acc

    tab = pl.BlockSpec((tq, 128), lambda i, g: (i, 0))
    return pl.pallas_call(
        body, name=name, grid=(nq, MLA_H // ATT_HG),
        in_specs=[pl.BlockSpec((ATT_HG, None, MLA_HP, tq), lambda i, g: (g, i, 0, 0)), pl.BlockSpec((tq, ATT_HG * 128), lambda i, g: (i, g)),
                  tab, tab, tab],
        out_specs=[pl.BlockSpec((tq, gw), lambda i, g: (i, g)), tab],
        out_shape=[jax.ShapeDtypeStruct((T, MLA_H * MLA_HP), BF16), jax.ShapeDtypeStruct((T, 128), F32)],
        compiler_params=pltpu.CompilerParams(dimension_semantics=("parallel", "arbitrary")),
    )(dqt, dkr, cc, sa, sb)


_NT = (((1,), (1,)), ((), ()))


def _as_row(col, n):
    return jnp.broadcast_to(col, (n, 128)).T[0:1, :]


def _attn_fwd(q, k, vt, name):
    T = q.shape[0]
    tq = tk = min(T, ATT_TILE)
    nq = T // tq
    hg = ATT_FWD_HEADS
    tsd = min(tq, ATT_SUB)

    def body(q_ref, k_ref, vt_ref, o_ref, lse_ref, m_ref, l_ref, acc_ref):
        i = pl.program_id(1)
        m_ref[...] = jnp.full_like(m_ref, NEG_INF)
        l_ref[...] = jnp.zeros_like(l_ref)
        acc_ref[...] = jnp.zeros_like(acc_ref)

        def step(j, diag):
            off = pl.multiple_of(j * tk, tk)
            ts, nsub = (tsd, tq // tsd) if diag else (tq, 1)
            for u in range(nsub):
                cols = slice(u * ts, (u + 1) * ts)
                nk_u = (u + 1) * ts if diag else tk
                sts = [lax.dot_general(k_ref[pl.ds(off, nk_u), hh * MLA_HP:(hh + 1) * MLA_HP], q_ref[cols, hh * MLA_HP:(hh + 1) * MLA_HP], _NT,
                                       preferred_element_type=F32) for hh in range(hg)]
                for hh in range(hg):
                    st = sts[hh]
                    if diag:
                        qcol = u * ts + lax.broadcasted_iota(jnp.int32, (nk_u, ts), 1)
                        st = jnp.where(lax.broadcasted_iota(jnp.int32, (nk_u, ts), 0) <= qcol, st, NEG_INF)
                    m_prev = m_ref[hh, :, cols]
                    m_new = jnp.maximum(m_prev, jnp.max(st, axis=0, keepdims=True))
                    alpha = jnp.exp2(m_prev - m_new)
                    pt = jnp.exp2(st - m_new)
                    l_ref[hh, :, cols] = alpha * l_ref[hh, :, cols] + jnp.sum(pt, axis=0, keepdims=True)
                    acc_ref[hh, :, cols] = alpha * acc_ref[hh, :, cols] + jnp.dot(vt_ref[hh, j, :, :nk_u], pt.astype(BF16), preferred_element_type=F32)
                    m_ref[hh, :, cols] = m_new

        def loop_body(j, carry):
            step(j, False)
            return carry

        lax.fori_loop(0, i, loop_body, 0)
        step(i, True)
        for hh in range(hg):
            o_ref[:, hh * MLA_V:(hh + 1) * MLA_V] = (acc_ref[hh] / l_ref[hh]).T.astype(BF16)
            lse_ref[hh] = m_ref[hh] + jnp.log2(l_ref[hh])

    return pl.pallas_call(
        body, name=name, grid=(MLA_H // hg, nq),
        in_specs=[pl.BlockSpec((tq, hg * MLA_HP), lambda h, i: (i, h)), pl.BlockSpec((T, hg * MLA_HP), lambda h, i: (0, h)),
                  pl.BlockSpec((hg, nq, MLA_V, tk), lambda h, i: (h, 0, 0, 0))],
        out_specs=[pl.BlockSpec((tq, hg * MLA_V), lambda h, i: (i, h)), pl.BlockSpec((hg, None, 1, tq), lambda h, i: (h, i, 0, 0))],
        out_shape=[jax.ShapeDtypeStruct((T, MLA_H * MLA_V), BF16), jax.ShapeDtypeStruct((MLA_H, nq, 1, tq), F32)],
        scratch_shapes=[pltpu.VMEM((hg, 1, tq), F32), pltpu.VMEM((hg, 1, tq), F32), pltpu.VMEM((hg, MLA_V, tq), F32)],
        compiler_params=pltpu.CompilerParams(dimension_semantics=("parallel", "arbitrary")),
    )(q, k, vt)


def _attn_delta(do, o, name):
    T = do.shape[0]
    tq = min(T, ATT_TILE)

    def body(do_ref, o_ref, d_ref):
        for hh in range(MLA_H):
            cs = slice(hh * MLA_V, (hh + 1) * MLA_V)
            s = jnp.sum(do_ref[:, cs].astype(F32) * o_ref[:, cs].astype(F32), axis=-1, keepdims=True)
            d_ref[hh] = _as_row(s, tq)

    return pl.pallas_call(
        body, name=name, grid=(T // tq,),
        in_specs=[_row_spec(tq, MLA_H * MLA_V), _row_spec(tq, MLA_H * MLA_V)],
        out_specs=pl.BlockSpec((MLA_H, None, 1, tq), lambda i: (0, i, 0, 0)),
        out_shape=jax.ShapeDtypeStruct((MLA_H, T // tq, 1, tq), F32),
        compiler_params=pltpu.CompilerParams(dimension_semantics=("parallel",)),
    )(do, o)


def _attn_bwd(q, k, kt, v, do, lse, delta, name):
    T = q.shape[0]
    tq = tk = min(T, ATT_TILE)
    nq = nk = T // tq
    tsd = min(tq, ATT_SUB)
    hg = ATT_BWD_HEADS

    def body(q_ref, k_ref, kt_ref, v_ref, do_ref, lse_ref, dl_ref, dqt_ref, dkv_ref, dkr_ref, dq_acc, dk_acc, dv_acc):
        j = pl.program_id(1)

        @pl.when(j == 0)
        def _():
            dq_acc[...] = jnp.zeros_like(dq_acc)

        dk_acc[...] = jnp.zeros_like(dk_acc)
        dv_acc[...] = jnp.zeros_like(dv_acc)

        def step(i, diag):
            off = pl.multiple_of(i * tq, tq)
            ts, nsub = (tsd, tq // tsd) if diag else (tq, 1)
            for u in range(nsub):
                cols = slice(u * ts, (u + 1) * ts)
                nk_u = (u + 1) * ts if diag else tk
                rows = pl.ds(off + u * ts, ts)
                pre = []
                for hh in range(hg):
                    hq, hv = slice(hh * MLA_HP, (hh + 1) * MLA_HP), slice(hh * MLA_V, (hh + 1) * MLA_V)
                    qi, doi = q_ref[rows, hq], do_ref[rows, hv]
                    st = lax.dot_general(k_ref[:nk_u, hq], qi, _NT, preferred_element_type=F32)
                    dpt = lax.dot_general(v_ref[:nk_u, hv], doi, _NT, preferred_element_type=F32)
                    pre.append((qi, doi, st, dpt))
                for hh in range(hg):
                    hq, hv = slice(hh * MLA_HP, (hh + 1) * MLA_HP), slice(hh * MLA_V, (hh + 1) * MLA_V)
                    qi, doi, st, dpt = pre[hh]
                    if diag:
                        qcol = u * ts + lax.broadcasted_iota(jnp.int32, (nk_u, ts), 1)
                        st = jnp.where(lax.broadcasted_iota(jnp.int32, (nk_u, ts), 0) <= qcol, st, NEG_INF)
                    pt = jnp.exp2(st - lse_ref[hh, i][:, cols])
                    dv_acc[:nk_u, hv] += jnp.dot(pt.astype(BF16), doi, preferred_element_type=F32)
                    dsb = (pt * (dpt - dl_ref[hh, i][:, cols])).astype(BF16)
                    dk_acc[:nk_u, hq] += jnp.dot(dsb, qi, preferred_element_type=F32)
                    dq_acc[hh, i, :, cols] += jnp.dot(kt_ref[hq, :nk_u], dsb, preferred_element_type=F32)

        def loop_body(i, carry):
            step(i, False)
            return carry

        step(j, True)
        lax.fori_loop(j + 1, nq, loop_body, 0)
        for hh in range(hg):
            a, m, b = hh * MLA_HP, hh * MLA_HP + MLA_NOPE, (hh + 1) * MLA_HP
            dkv_ref[:, a:m] = (dk_acc[:, a:m] * (1.0 / LOG2E)).astype(BF16)
            dkv_ref[:, m:b] = dv_acc[:, hh * MLA_V:(hh + 1) * MLA_V].astype(BF16)
            dkr_ref[:, hh * 128:(hh + 1) * 128] = (dk_acc[:, m:b] * (1.0 / LOG2E)).astype(BF16)

        @pl.when(j == nk - 1)
        def _():
            dqt_ref[...] = dq_acc[...].astype(BF16)

    stat = pl.BlockSpec((hg, nq, 1, tq), lambda h, j: (h, 0, 0, 0))
    return pl.pallas_call(
        body, name=name, grid=(MLA_H // hg, nk),
        in_specs=[pl.BlockSpec((T, hg * MLA_HP), lambda h, j: (0, h)), pl.BlockSpec((tk, hg * MLA_HP), lambda h, j: (j, h)),
                  pl.BlockSpec((hg * MLA_HP, tk), lambda h, j: (h, j)), pl.BlockSpec((tk, hg * MLA_V), lambda h, j: (j, h)),
                  pl.BlockSpec((T, hg * MLA_V), lambda h, j: (0, h)), stat, stat],
        out_specs=[pl.BlockSpec((hg, nq, MLA_HP, tq), lambda h, j: (h, 0, 0, 0)), pl.BlockSpec((tk, hg * MLA_HP), lambda h, j: (j, h)),
                   pl.BlockSpec((tk, hg * 128), lambda h, j: (j, h))],
        out_shape=[jax.ShapeDtypeStruct((MLA_H, nq, MLA_HP, tq), BF16), jax.ShapeDtypeStruct((T, MLA_H * MLA_HP), BF16),
                   jax.ShapeDtypeStruct((T, MLA_H * 128), BF16)],
        scratch_shapes=[pltpu.VMEM((hg, nq, MLA_HP, tq), F32), pltpu.VMEM((tk, hg * MLA_HP), F32), pltpu.VMEM((tk, hg * MLA_V), F32)],
        compiler_params=pltpu.CompilerParams(dimension_semantics=("parallel", "arbitrary")),
    )(q, k, kt, v, do, lse, delta)


ADA_TN = 512


def _silu(v):
    return v * (1.0 / (1.0 + jnp.exp(-v)))


def _ada_fwd(c_all, ada_w, ada_b_loc, name):
    L, D, Nc = ada_w.shape
    B = c_all.shape[0]

    def body(c_ref, w_ref, b_ref, o_ref):
        ca = _silu(c_ref[...]).astype(BF16)
        o_ref[...] = jnp.dot(ca, w_ref[...].astype(BF16), preferred_element_type=F32) + b_ref[...]

    return pl.pallas_call(
        body, name=name, grid=(L, Nc // ADA_TN),
        in_specs=[pl.BlockSpec((B, D), lambda l, n: (0, 0)), pl.BlockSpec((None, D, ADA_TN), lambda l, n: (l, 0, n)),
                  pl.BlockSpec((None, 1, ADA_TN), lambda l, n: (l, 0, n))],
        out_specs=pl.BlockSpec((None, B, ADA_TN), lambda l, n: (l, 0, n)),
        out_shape=jax.ShapeDtypeStruct((L, B, Nc), F32),
        compiler_params=pltpu.CompilerParams(dimension_semantics=("parallel", "parallel")),
    )(c_all, ada_w, ada_b_loc)


def _ada_bwd_adamw(c_all_t, dmod_loc, w, m, v, name):
    D, B = c_all_t.shape
    L, _, Nc = dmod_loc.shape

    def body(c_ref, d_ref, w_ref, m_ref, v_ref, g_ref, dl_ref, nm_ref, nv_ref):
        ca = _silu(c_ref[...])
        dv = d_ref[...]
        gv = ca[:, 0:1] * dv[0:1, :]
        for b in range(1, B):
            gv = gv + ca[:, b:b + 1] * dv[b:b + 1, :]
        mn = ADAM_B1 * m_ref[...] + (1.0 - ADAM_B1) * gv
        vn = ADAM_B2 * v_ref[...] + (1.0 - ADAM_B2) * (gv * gv)
        g_ref[...] = gv
        nm_ref[...] = mn
        nv_ref[...] = vn
        dl_ref[...] = -ADAM_LR * ((mn / _ADAM_C1) / (jnp.sqrt(vn / _ADAM_C2) + ADAM_EPS) + ADAM_WD * w_ref[...])

    blk = pl.BlockSpec((None, D, ADA_TN), lambda l, n: (l, 0, n))
    return pl.pallas_call(
        body, name=name, grid=(L, Nc // ADA_TN),
        in_specs=[pl.BlockSpec((D, B), lambda l, n: (0, 0)), pl.BlockSpec((None, B, ADA_TN), lambda l, n: (l, 0, n)), blk, blk, blk],
        out_specs=[blk] * 4,
        out_shape=[jax.ShapeDtypeStruct((L, D, Nc), F32)] * 4,
        compiler_params=pltpu.CompilerParams(dimension_semantics=("parallel", "parallel")),
    )(c_all_t, dmod_loc, w, m, v)


def _sum_lead(parts, name, out_dtype=F32):
    R, C = parts[0].shape[1:]
    n_tot = sum(p.shape[0] for p in parts)
    tr = R
    for cand in (512, 256, 128, 64, 32, 16):
        if R % cand == 0 and cand * C * 4 * n_tot <= (8 << 20):
            tr = cand
            break

    def body(*refs):
        o_ref = refs[-1]
        acc = None
        for r in refs[:-1]:
            for s in range(r.shape[0]):
                acc = r[s].astype(F32) if acc is None else acc + r[s].astype(F32)
        o_ref[...] = acc.astype(o_ref.dtype)

    return pl.pallas_call(
        body, name=name, grid=(R // tr,),
        in_specs=[pl.BlockSpec((p.shape[0], tr, C), lambda i: (0, i, 0)) for p in parts],
        out_specs=pl.BlockSpec((tr, C), lambda i: (i, 0)),
        out_shape=jax.ShapeDtypeStruct((R, C), out_dtype),
        compiler_params=pltpu.CompilerParams(dimension_semantics=("parallel",)),
    )(*parts)


_ADAM_C1 = 1.0 - ADAM_B1 ** ADAM_STEP
_ADAM_C2 = 1.0 - ADAM_B2 ** ADAM_STEP


def _adamw(w, g, m, v, name):
    shape = w.shape
    C = shape[-1]
    R = math.prod(shape[:-1]) if len(shape) > 1 else 1
    w2, g2, m2, v2 = (a.reshape(R, C) for a in (w, g, m, v))
    tr = R
    for cand in (1024, 512, 256, 128, 64, 32, 16, 8):
        if R % cand == 0 and cand * C * 4 <= (1 << 20):
            tr = cand
            break

    def body(w_ref, g_ref, m_ref, v_ref, d_ref, nm_ref, nv_ref):
        gv = g_ref[...]
        mn = ADAM_B1 * m_ref[...] + (1.0 - ADAM_B1) * gv
        vn = ADAM_B2 * v_ref[...] + (1.0 - ADAM_B2) * (gv * gv)
        nm_ref[...] = mn
        nv_ref[...] = vn
        m_hat = mn / _ADAM_C1
        v_hat = vn / _ADAM_C2
        d_ref[...] = -ADAM_LR * (m_hat / (jnp.sqrt(v_hat) + ADAM_EPS) + ADAM_WD * w_ref[...])

    spec = pl.BlockSpec((tr, C), lambda i: (i, 0))
    outs = pl.pallas_call(
        body, name=name, grid=(R // tr,),
        in_specs=[spec] * 4, out_specs=[spec] * 3,
        out_shape=[jax.ShapeDtypeStruct((R, C), F32)] * 3,
        compiler_params=pltpu.CompilerParams(dimension_semantics=("parallel",)),
    )(w2, g2, m2, v2)
    return tuple(o.reshape(shape) for o in outs)


def _row_tile(rows, cols, itemsize, budget):
    for cand in (1024, 512, 256, 128, 64, 32, 16):
        if rows % cand == 0 and cand * cols * itemsize <= budget:
            return cand
    return rows


def _sum_sel(sel, stacked, others, name, out_dtype):
    R, C = stacked.shape[1:]
    n_tot = 1 + sum(o.shape[0] for o in others)
    tr = _row_tile(R, C, 4 * n_tot, 8 << 20)

    def body(sel_ref, s_ref, *refs):
        o_ref = refs[-1]
        acc = s_ref[...].astype(F32)
        for r in refs[:-1]:
            for s in range(r.shape[0]):
                acc = acc + r[s].astype(F32)
        o_ref[...] = acc.astype(o_ref.dtype)

    return pl.pallas_call(
        body, name=name,
        grid_spec=pltpu.PrefetchScalarGridSpec(
            num_scalar_prefetch=1, grid=(R // tr,),
            in_specs=[pl.BlockSpec((None, tr, C), lambda i, s: (s[0], i, 0))] + [pl.BlockSpec((o.shape[0], tr, C), lambda i, s: (0, i, 0)) for o in others],
            out_specs=pl.BlockSpec((tr, C), lambda i, s: (i, 0))),
        out_shape=jax.ShapeDtypeStruct((R, C), out_dtype),
        compiler_params=pltpu.CompilerParams(dimension_semantics=("parallel",)),
    )(sel, stacked, *others)


def _adamw_piece(cidx, w2, m2, v2, mine, got, bufs, row0, name):
    hr, C = mine.shape
    tr = _row_tile(math.gcd(hr, row0) if row0 else hr, C, 4, 1 << 20)
    nt = hr // tr

    def body(c_ref, w_ref, m_ref, v_ref, a_ref, b_ref, _g, _d, _nm, _nv, g_ref, d_ref, nm_ref, nv_ref):
        gv = jnp.where(pl.program_id(0) == c_ref[0], a_ref[...], b_ref[...])
        mn = ADAM_B1 * m_ref[...] + (1.0 - ADAM_B1) * gv
        vn = ADAM_B2 * v_ref[...] + (1.0 - ADAM_B2) * (gv * gv)
        g_ref[...] = gv
        nm_ref[...] = mn
        nv_ref[...] = vn
        d_ref[...] = -ADAM_LR * ((mn / _ADAM_C1) / (jnp.sqrt(vn / _ADAM_C2) + ADAM_EPS) + ADAM_WD * w_ref[...])

    rows = pl.BlockSpec((tr, C), lambda hf, t, c: (row0 // tr + hf * nt + t, 0))
    mine_spec = pl.BlockSpec((tr, C), lambda hf, t, c: (jnp.where(hf == c[0], t, 0), 0))
    got_spec = pl.BlockSpec((tr, C), lambda hf, t, c: (jnp.where(hf == c[0], 0, t), 0))
    return pl.pallas_call(
        body, name=name,
        grid_spec=pltpu.PrefetchScalarGridSpec(num_scalar_prefetch=1, grid=(2, nt), in_specs=[rows] * 3 + [mine_spec, got_spec] + [_ANY_SPEC] * 4,
                                               out_specs=[rows] * 4),
        out_shape=[jax.ShapeDtypeStruct(w2.shape, F32)] * 4,
        input_output_aliases={6 + n: n for n in range(4)},
        compiler_params=pltpu.CompilerParams(dimension_semantics=("parallel", "parallel")),
    )(cidx, w2, m2, v2, mine, got, *bufs)


_VMEM_SPEC = pl.BlockSpec(memory_space=pltpu.VMEM)
_HBM_SPEC = pl.BlockSpec(memory_space=pltpu.HBM)


def _flip(v, bit):
    return (1 - v) if bit else v


def _allgather8(v, name):
    def body(v_ref, out_ref, send_sems, recv_sems, local_sem):
        x, y, c = _idx()
        me = 4 * x + 2 * y + c
        mine = pltpu.make_async_copy(v_ref, out_ref.at[me], local_sem)
        mine.start()
        sends = []
        for k in range(1, N_DEV):
            peer = (_flip(x, k & 4), _flip(y, k & 2), _flip(c, k & 1))
            cp = pltpu.make_async_remote_copy(src_ref=v_ref, dst_ref=out_ref.at[me], send_sem=send_sems.at[k - 1], recv_sem=recv_sems.at[k - 1],
                                              device_id=peer, device_id_type=MESH)
            cp.start()
            sends.append(cp)
        for k in range(1, N_DEV):
            px, py, pc = _flip(x, k & 4), _flip(y, k & 2), _flip(c, k & 1)
            src = 4 * px + 2 * py + pc
            pltpu.make_async_remote_copy(src_ref=v_ref, dst_ref=out_ref.at[src], send_sem=send_sems.at[k - 1], recv_sem=recv_sems.at[k - 1],
                                         device_id=(px, py, pc), device_id_type=MESH).wait_recv()
        for cp in sends:
            cp.wait_send()
        mine.wait()

    return pl.pallas_call(
        body, name=name,
        out_shape=jax.ShapeDtypeStruct((N_DEV, *v.shape), v.dtype),
        in_specs=[_VMEM_SPEC], out_specs=_VMEM_SPEC,
        scratch_shapes=[pltpu.SemaphoreType.DMA((N_DEV - 1,)), pltpu.SemaphoreType.DMA((N_DEV - 1,)), pltpu.SemaphoreType.DMA],
    )(v)


def _mod_exchange(modp, name):
    _, L, Nc = modp.shape

    def body(p_ref, out_ref, send_sems, recv_sems, local_sem):
        x, y, c = _idx()
        me, chip = 4 * x + 2 * y + c, 2 * x + y
        mine = pltpu.make_async_copy(p_ref.at[me], out_ref.at[chip], local_sem)
        mine.start()
        sends = []
        for k in range(1, N_CHIPS):
            px, py = _flip(x, k & 2), _flip(y, k & 1)
            cp = pltpu.make_async_remote_copy(src_ref=p_ref.at[4 * px + 2 * py + c], dst_ref=out_ref.at[chip],
                                              send_sem=send_sems.at[k - 1], recv_sem=recv_sems.at[k - 1], device_id=(px, py, c), device_id_type=MESH)
            cp.start()
            sends.append(cp)
        for k in range(1, N_CHIPS):
            px, py = _flip(x, k & 2), _flip(y, k & 1)
            pltpu.make_async_remote_copy(src_ref=p_ref.at[me], dst_ref=out_ref.at[2 * px + py], send_sem=send_sems.at[k - 1],
                                         recv_sem=recv_sems.at[k - 1], device_id=(px, py, c), device_id_type=MESH).wait_recv()
        for cp in sends:
            cp.wait_send()
        mine.wait()

    return pl.pallas_call(
        body, name=name,
        out_shape=jax.ShapeDtypeStruct((N_CHIPS, L, Nc), modp.dtype),
        in_specs=[_VMEM_SPEC], out_specs=_VMEM_SPEC,
        scratch_shapes=[pltpu.SemaphoreType.DMA((N_CHIPS - 1,)), pltpu.SemaphoreType.DMA((N_CHIPS - 1,)), pltpu.SemaphoreType.DMA],
    )(modp)


_SEM_SPEC = pl.BlockSpec(memory_space=pltpu.SEMAPHORE)
_ANY_SPEC = pl.BlockSpec(memory_space=pl.ANY)
_EFFECT = pltpu.SideEffectType.DATAFLOW_SIDE_EFFECTING


def _hbm(a):
    return pltpu.with_memory_space_constraint(a, pltpu.HBM)


def _xchip_copies(mode, srcs, lands, send_sems, recv_sems, waiting):
    x, y, c = _idx()
    chip = 2 * x + y
    out = []
    for a in range(len(srcs)):
        for k in range(1, _n_peers(mode) + 1):
            if mode == "all8":
                px, py, pc = _flip(x, k & 4), _flip(y, k & 2), _flip(c, k & 1)
                src, dst, mine = srcs[a], lands[a].at[4 * x + 2 * y + c], lands[a].at[4 * px + 2 * py + pc]
            elif mode == "scatter8":
                px, py, pc = _flip(x, k & 4), _flip(y, k & 2), _flip(c, k & 1)
                src, dst, mine = srcs[a].at[pc, 2 * px + py], lands[a].at[k - 1], lands[a].at[k - 1]
            else:
                px, py, pc = _flip(x, k & 2), _flip(y, k & 1), c
                peer = 2 * px + py
                if mode == "gather":
                    src, dst, mine = srcs[a].at[c], lands[a].at[chip, c], lands[a].at[peer, c]
                else:
                    src, dst, mine = srcs[a].at[peer], lands[a].at[k - 1], lands[a].at[k - 1]
            q = a * _n_peers(mode) + k - 1
            out.append(pltpu.make_async_remote_copy(src_ref=src, dst_ref=mine if waiting else dst, send_sem=send_sems[q], recv_sem=recv_sems[q],
                                                    device_id=(px, py, pc), device_id_type=MESH))
    return out


def _n_peers(mode):
    return N_DEV - 1 if mode in ("all8", "scatter8") else N_CHIPS - 1


def _xchip_start(mode, srcs, land_shapes, dep, name):
    n = len(srcs)
    ns = n * _n_peers(mode)

    def body(*refs):
        src_refs, land_refs = refs[:n], refs[n:2 * n]
        outs = refs[2 * n + 1:]
        for cp in _xchip_copies(mode, src_refs, land_refs, outs[:ns], outs[ns:2 * ns], waiting=False):
            cp.start()
        outs[-1][...] = jnp.zeros_like(outs[-1])

    lands = [_hbm(lax.empty(s.shape, s.dtype)) for s in land_shapes]
    outs = pl.pallas_call(
        body, name=name,
        out_shape=(*[pltpu.SemaphoreType.DMA(())] * (2 * ns), *[pltpu.HBM(s.shape, s.dtype) for s in srcs],
                   *[pltpu.HBM(s.shape, s.dtype) for s in land_shapes], jax.ShapeDtypeStruct((8, 128), F32)),
        in_specs=[_HBM_SPEC] * (2 * n) + [_ANY_SPEC],
        out_specs=(*[_SEM_SPEC] * (2 * ns), *[_HBM_SPEC] * (2 * n), _VMEM_SPEC),
        input_output_aliases={i: 2 * ns + i for i in range(2 * n)},
        compiler_params=pltpu.CompilerParams(has_side_effects=_EFFECT),
    )(*[_hbm(s) for s in srcs], *lands, dep)
    return list(outs[:ns]), list(outs[ns:2 * ns]), list(outs[2 * ns:2 * ns + n]), list(outs[2 * ns + n:2 * ns + 2 * n]), outs[-1]


def _xchip_wait(mode, send_sems, recv_sems, srcs, lands, after, name):
    n = len(srcs)
    ns = n * _n_peers(mode)

    def body(*refs):
        src_refs, land_refs = refs[:n], refs[n:2 * n]
        sems = refs[2 * n:2 * n + 2 * ns]
        for cp in _xchip_copies(mode, src_refs, land_refs, sems[:ns], sems[ns:], waiting=True):
            cp.wait_send()
            cp.wait_recv()

    outs = pl.pallas_call(
        body, name=name,
        out_shape=(*[pltpu.HBM(s.shape, s.dtype) for s in srcs], *[pltpu.HBM(s.shape, s.dtype) for s in lands]),
        in_specs=[_HBM_SPEC] * (2 * n) + [_SEM_SPEC] * (2 * ns) + [_ANY_SPEC] * len(after),
        out_specs=tuple([_HBM_SPEC] * (2 * n)),
        input_output_aliases={i: i for i in range(2 * n)},
        compiler_params=pltpu.CompilerParams(has_side_effects=_EFFECT),
    )(*srcs, *lands, *send_sems, *recv_sems, *after)
    return list(outs[:n]), list(outs[n:])


def _sibling_fwd(lands, name):
    n = len(lands)

    def body(*refs):
        outs = refs[n:2 * n]
        send_sems, recv_sems = refs[2 * n:]
        x, y, c = _idx()
        sib = (x, y, 1 - c)
        sends = []
        for a in range(n):
            for k in range(1, N_CHIPS):
                src = 2 * _flip(x, k & 2) + _flip(y, k & 1)
                cp = pltpu.make_async_remote_copy(src_ref=outs[a].at[src, c], dst_ref=outs[a].at[src, c], send_sem=send_sems.at[a, k - 1],
                                                  recv_sem=recv_sems.at[a, k - 1], device_id=sib, device_id_type=MESH)
                cp.start()
                sends.append(cp)
        for a in range(n):
            for k in range(1, N_CHIPS):
                src = 2 * _flip(x, k & 2) + _flip(y, k & 1)
                pltpu.make_async_remote_copy(src_ref=outs[a].at[src, c], dst_ref=outs[a].at[src, 1 - c], send_sem=send_sems.at[a, k - 1],
                                             recv_sem=recv_sems.at[a, k - 1], device_id=sib, device_id_type=MESH).wait_recv()
        for cp in sends:
            cp.wait_send()

    return pl.pallas_call(
        body, name=name,
        out_shape=[jax.ShapeDtypeStruct(s.shape, s.dtype) for s in lands],
        in_specs=[_HBM_SPEC] * n, out_specs=[_HBM_SPEC] * n,
        input_output_aliases={i: i for i in range(n)},
        scratch_shapes=[pltpu.SemaphoreType.DMA((n, N_CHIPS - 1)), pltpu.SemaphoreType.DMA((n, N_CHIPS - 1))],
    )(*lands)


def _sibling_send(halves, name):
    n = len(halves)

    def body(*refs):
        ins, outs = refs[:n], refs[n:2 * n]
        send_sems, recv_sems = refs[2 * n:]
        x, y, c = _idx()
        cps = []
        for a in range(n):
            cp = pltpu.make_async_remote_copy(src_ref=ins[a], dst_ref=outs[a], send_sem=send_sems.at[a], recv_sem=recv_sems.at[a],
                                              device_id=(x, y, 1 - c), device_id_type=MESH)
            cp.start()
            cps.append(cp)
        for cp in cps:
            cp.wait()

    return pl.pallas_call(
        body, name=name,
        out_shape=[jax.ShapeDtypeStruct(h.shape, h.dtype) for h in halves],
        in_specs=[_HBM_SPEC] * n, out_specs=[_HBM_SPEC] * n,
        scratch_shapes=[pltpu.SemaphoreType.DMA((n,)), pltpu.SemaphoreType.DMA((n,))],
    )(*halves)


def _col_full(g):
    k, n = g.shape[1], g.shape[2]
    return g.transpose(1, 0, 2).reshape(k, N_CHIPS * n)


def _col_blocks(w):
    k, n = w.shape
    return w.reshape(k, N_CHIPS, n // N_CHIPS).transpose(1, 0, 2)


def _row_blocks(w):
    k, n = w.shape
    return w.reshape(N_CHIPS, k // N_CHIPS, n)


_UQ_HEAD = MLA_NOPE + MLA_ROPE

_LAT = MLA_QL + MLA_KVL + MLA_ROPE
_POOL_R = len(POOL_WINDOWS) * (POOL_GD // N_CHIPS)

_PIECE_KINDS = {
    "mlp_w1": (D_MODEL, D_MODEL, lambda g: g, _col_blocks),
    "mlp_w2": (D_MODEL, D_MODEL, lambda g: g.reshape(4 * D_MODEL, D_MODEL), _row_blocks),
    "pool_w": (_POOL_R, POOL_GD,
               lambda g: g.reshape(N_CHIPS, len(POOL_WINDOWS), POOL_GD // N_CHIPS, POOL_GD).transpose(1, 0, 2, 3).reshape(len(POOL_WINDOWS), POOL_GD, POOL_GD),
               lambda w: w.reshape(len(POOL_WINDOWS), N_CHIPS, POOL_GD // N_CHIPS, POOL_GD).transpose(1, 0, 2, 3).reshape(N_CHIPS, _POOL_R, POOL_GD)),
    "sgu_w_in": (D_MODEL, 2 * SGU_W // N_CHIPS, _col_full, _col_blocks),
    "sgu_w_out": (SGU_W // N_CHIPS, D_MODEL, lambda g: g.reshape(SGU_W, D_MODEL), _row_blocks),
    "mla_w_dq_dkv": (D_MODEL // N_CHIPS, _LAT, lambda g: jnp.pad(g.reshape(D_MODEL, _LAT), ((0, 0), (0, MLA_LATP - _LAT))),
                     lambda w: _row_blocks(w[:, :_LAT])),
    "mla_w_uq": (MLA_QL, MLA_H * _UQ_HEAD // N_CHIPS,
                 lambda g: jnp.pad(_col_full(g).reshape(MLA_QL, MLA_H, _UQ_HEAD), ((0, 0), (0, 0), (0, MLA_HP - _UQ_HEAD))).reshape(MLA_QL, MLA_H * MLA_HP),
                 lambda w: _col_blocks(w.reshape(MLA_QL, MLA_H, MLA_HP)[:, :, :_UQ_HEAD].reshape(MLA_QL, MLA_H * _UQ_HEAD))),
    "mla_w_ukv": (MLA_KVL, MLA_H * (MLA_NOPE + MLA_V) // N_CHIPS, _col_full, _col_blocks),
    "mla_w_o": (MLA_H * MLA_V // N_CHIPS, D_MODEL, lambda g: g.reshape(MLA_H * MLA_V, D_MODEL), _row_blocks),
}
_MIXER_KINDS = (("pool_w",), ("sgu_w_in", "sgu_w_out"), ("mla_w_dq_dkv", "mla_w_uq", "mla_w_ukv", "mla_w_o"))


def _layer_pieces(i):
    return [(k, i // N_MIXERS) for k in _MIXER_KINDS[i % N_MIXERS]] + [("mlp_w1", i), ("mlp_w2", i)]


def _rope_tables(positions):
    inv_freq = ROPE_THETA ** (-jnp.arange(0, MLA_ROPE, 2, dtype=F32) / MLA_ROPE)
    ang = positions.astype(F32)[:, None] * inv_freq
    cos, sin = jnp.cos(ang), jnp.sin(ang)
    z32, z64 = jnp.zeros_like(cos), jnp.zeros((positions.shape[0], 64), F32)
    return (jnp.concatenate([cos, cos, z64], axis=1), jnp.concatenate([-sin, z32, z64], axis=1), jnp.concatenate([z32, sin, z64], axis=1))


def _local_step(x, positions, target, mod, S, weights_of, grads_of):
    D = D_MODEL
    cc, sa, sb = _rope_tables(positions)
    mods = [[mod[i:i + 1, n * D:(n + 1) * D] for n in range(6)] for i in range(DEPTH)]
    h_dtype = lambda i: F32 if i % N_MIXERS == 0 else BF16
    saved = []
    h = _norm_mod_fwd(x, S["norm_mix_g"][0:1], mods[0][1], mods[0][0], h_dtype(0), "l0_norm1")
    for i in range(DEPTH):
        sh1, sc1, g1, sh2, sc2, g2 = mods[i]
        kind, j = i % N_MIXERS, i // N_MIXERS
        gmlp = S["norm_mlp_g"][i:i + 1]
        W = weights_of(i, "mix", x)
        st = {"x": x}
        norm2 = ((gmlp, "n"), (sc2, "n"), (sh2, "n"))
        if kind == 0:
            x2, pooled, ypre, h2 = _pool_fwd(h, W["pool_w"], S["pool_scale"][j:j + 1], x, g1, gmlp, sc2, sh2, f"l{i}_pool")
            st.update(pooled=pooled, y=ypre)
        elif kind == 1:
            zz = _mm(h, W["sgu_w_in"], out_dtypes=(F32,), name=f"l{i}_sgu_in")
            bs_t = S["sgu_b_s"].T
            gated = _sgu_gate_fwd(zz, S["sgu_ln_g"], S["sgu_ln_b"], S["sgu_w_s"], bs_t, f"l{i}_sgu_gate")
            x2, y, h2 = _mm(gated, W["sgu_w_out"], epi=_epi_residual_norm, extras=((x, "mn"), (g1, "n"), *norm2), out_dtypes=(F32, BF16, BF16),
                            tn=D, name=f"l{i}_sgu_out")
            st.update(h=h, zz=zz, gated=gated, y=y, bs_t=bs_t)
        else:
            lat = _mm(h, W["mla_w_dq_dkv"], out_dtypes=(F32,), name=f"l{i}_mla_lat")
            cqn, ckvn, krot = _mla_lat_fwd(lat, S["mla_q_norm_g"], S["mla_kv_norm_g"], cc, sa, sb, f"l{i}_mla_latn")
            q = _mm(cqn, W["mla_w_uq"], epi=_epi_q_rope, extras=((cc, "m"), (sa, "m"), (sb, "m")), name=f"l{i}_mla_uq")
            k, kt, v, vt = _mla_ukv(ckvn, W["mla_w_ukv"], krot, f"l{i}_mla_ukv")
            o, lse = _attn_fwd(q, k, vt, f"l{i}_attn")
            x2, y, h2 = _mm(o, W["mla_w_o"], epi=_epi_residual_norm, extras=((x, "mn"), (g1, "n"), *norm2), out_dtypes=(F32, BF16, BF16),
                            tn=D, name=f"l{i}_mla_o")
            st.update(h=h, lat=lat, cqn=cqn, ckvn=ckvn, q=q, k=k, kt=kt, v=v, o=o, lse=lse, y=y)
        W = {**W, **weights_of(i, "mlp", x2)}
        z, r2 = _mm(h2, W["mlp_w1"], epi=_epi_sq_relu, out_dtypes=(BF16, BF16), epi_cols=MM_EPI_COLS, tm=MM_TM_WIDE, name=f"l{i}_mlp1")
        W = {**W, **weights_of(i, "mlp2", z)}
        if i + 1 < DEPTH:
            norm1 = ((S["norm_mix_g"][i + 1:i + 2], "n"), (mods[i + 1][1], "n"), (mods[i + 1][0], "n"))
            x3, o2, h = _mm(z, W["mlp_w2"], epi=_epi_residual_norm, extras=((x2, "mn"), (g2, "n"), *norm1), out_dtypes=(F32, BF16, h_dtype(i + 1)),
                            tn=D, name=f"l{i}_mlp2")
        else:
            x3, o2 = _mm(z, W["mlp_w2"], epi=_epi_residual, extras=((x2, "mn"), (g2, "n")), out_dtypes=(F32, BF16), name=f"l{i}_mlp2")
        st.update(x2=x2, h2=h2, z=z, r2=r2, o2=o2, W=W)
        saved.append(st)
        x = x3

    loss, dx, dfinal_g, do2, dg2 = _loss_head(x, target, S["final_g"], saved[-1]["o2"], mods[-1][5], "loss_head")

    gS = {"final_g": dfinal_g, "norm_mix_g": [None] * DEPTH, "norm_mlp_g": [None] * DEPTH, "pool_scale": [None] * 2}
    dmod = [None] * DEPTH
    started = None
    for i in reversed(range(DEPTH)):
        st = saved[i]
        W, gW = st["W"], {}
        sh1, sc1, g1, sh2, sc2, g2 = mods[i]
        kind, j = i % N_MIXERS, i // N_MIXERS
        gmix, gmlp = S["norm_mix_g"][i:i + 1], S["norm_mlp_g"][i:i + 1]
        da = _mm(do2, W["mlp_w2"], tb=True, epi=lambda acc, rt: (acc * rt.astype(F32),), extras=((st["r2"], "mn"),), after=started, epi_cols=MM_EPI_COLS,
                 tm=MM_TM_WIDE, name=f"l{i}_b_dz")
        gW["mlp_w2"] = _mm(st["z"], do2, ta=True, chip_blocks="row", name=f"l{i}_b_dw2")
        dh2 = _mm(da, W["mlp_w1"], tb=True, name=f"l{i}_b_dh2")
        gW["mlp_w1"] = _mm(st["h2"], da, ta=True, chip_blocks="col", name=f"l{i}_b_dw1")
        dx2, dgmlp, dsc2, dsh2, dy, q1 = _norm_mod_bwd(st["x2"], dh2, dx, gmlp, sc2, f"l{i}_b_norm2", res=(st["y"], g1))
        gS["norm_mlp_g"][i] = dgmlp
        if kind == 0:
            dh, dpw, dpsc, dg1 = _pool_bwd(dy, st["pooled"], W["pool_w"], S["pool_scale"][j:j + 1], g1, q1, f"l{i}_b_pool")
            gW["pool_w"] = dpw.astype(BF16)
            gS["pool_scale"][j] = dpsc
        elif kind == 1:
            dg1 = q1
            dgated = _mm(dy, W["sgu_w_out"], tb=True, name=f"l{i}_b_dgated")
            gW["sgu_w_out"] = _mm(st["gated"], dy, ta=True, name=f"l{i}_b_dwout")
            dzz, dws, dbs, dlg, dlb = _sgu_gate_bwd(st["zz"], dgated, S["sgu_ln_g"], S["sgu_ln_b"], S["sgu_w_s"], st["bs_t"], f"l{i}_b_sgu_gate")
            gS.update(sgu_w_s=dws, sgu_b_s=dbs[:, :, 0], sgu_ln_g=dlg, sgu_ln_b=dlb)
            dh = _mm(dzz, W["sgu_w_in"], tb=True, name=f"l{i}_b_dh_sgu")
            gW["sgu_w_in"] = _mm(st["h"], dzz, ta=True, name=f"l{i}_b_dwin")
        else:
            dg1 = q1
            do = _mm(dy, W["mla_w_o"], tb=True, name=f"l{i}_b_do")
            gW["mla_w_o"] = _mm(st["o"], dy, ta=True, name=f"l{i}_b_dwo")
            delta = _attn_delta(do, st["o"], f"l{i}_b_delta")
            dqt, dkv, dkr = _attn_bwd(st["q"], st["k"], st["kt"], st["v"], do, st["lse"], delta, f"l{i}_b_attn")
            dqpad, dkrot = _mla_prep_bwd(dqt, dkr, cc, sa, sb, f"l{i}_b_mla_prep")
            dcqn = _mm(dqpad, W["mla_w_uq"], tb=True, out_dtypes=(F32,), name=f"l{i}_b_dcq")
            gW["mla_w_uq"] = _mm(st["cqn"], dqpad, ta=True, name=f"l{i}_b_dwuq")
            dckvn = _mm(dkv, W["mla_w_ukv"], tb=True, out_dtypes=(F32,), name=f"l{i}_b_dckv")
            gW["mla_w_ukv"] = _mm(st["ckvn"], dkv, ta=True, name=f"l{i}_b_dwukv")
            dlat, dqg, dkvg = _mla_lat_bwd(st["lat"], dcqn, dckvn, dkrot, S["mla_q_norm_g"], S["mla_kv_norm_g"], cc, sa, sb, f"l{i}_b_mla_latn")
            gS.update(mla_q_norm_g=dqg, mla_kv_norm_g=dkvg)
            dh = _mm(dlat, W["mla_w_dq_dkv"], tb=True, name=f"l{i}_b_dh_mla")
            gW["mla_w_dq_dkv"] = _mm(st["h"], dlat, ta=True, name=f"l{i}_b_dwdq")
        if i > 0:
            dx, dgmix, dsc1, dsh1, do2_prev, dg2_prev = _norm_mod_bwd(st["x"], dh, dx2, gmix, sc1, f"l{i}_b_norm1", res=(saved[i - 1]["o2"], mods[i - 1][5]))
        else:
            dx, dgmix, dsc1, dsh1 = _norm_mod_bwd(st["x"], dh, dx2, gmix, sc1, f"l{i}_b_norm1")
        gS["norm_mix_g"][i] = dgmix
        dmod[i] = jnp.concatenate([dsh1, dsc1, dg1, dsh2, dsc2, dg2], axis=1)
        started = grads_of(i, gW, dx)
        if i > 0:
            do2, dg2 = do2_prev, dg2_prev

    for n in ("norm_mix_g", "norm_mlp_g", "pool_scale"):
        gS[n] = jnp.concatenate(gS[n], axis=0)
    return loss, dx, gS, jnp.concatenate(dmod, axis=0)


_SMALL = {
    "norm_mix_g": (DEPTH, D_MODEL), "norm_mlp_g": (DEPTH, D_MODEL), "sgu_ln_g": (1, SGU_W), "sgu_ln_b": (1, SGU_W),
    "sgu_w_s": (SGU_H, SGU_CHUNK, SGU_CHUNK), "sgu_b_s": (SGU_H, SGU_CHUNK), "mla_kv_norm_g": (1, MLA_KVL), "final_g": (1, D_MODEL),
    "pool_scale": (2, D_MODEL), "mla_q_norm_g": (1, MLA_QL), "loss": (1, 128), "dmod": (DEPTH, 6 * D_MODEL),
}
_PACK_W = 1024


def _pack(vals):
    flat = jnp.concatenate([v.reshape(-1) for v in vals])
    rows = -(-flat.shape[0] // (8 * _PACK_W)) * 8
    return jnp.pad(flat, (0, rows * _PACK_W - flat.shape[0])).reshape(rows, _PACK_W)


def _unpack(buf, shapes):
    flat, out, off = buf.reshape(-1), [], 0
    for s in shapes:
        n = math.prod(s)
        out.append(flat[off:off + n].reshape(s))
        off += n
    return out


def kernel(x, c, positions, ada_w, ada_b, norm_mix_g, norm_mlp_g, pool_w, pool_scale, sgu_w_in, sgu_ln_g, sgu_ln_b, sgu_w_s, sgu_b_s, sgu_w_out, mla_w_dq_dkv, mla_q_norm_g, mla_kv_norm_g, mla_w_uq, mla_w_ukv, mla_w_o, mlp_w1, mlp_w2, final_g, loss_target, m_ada_w, m_ada_b, m_norm_mix_g, m_norm_mlp_g, m_pool_w, m_pool_scale, m_sgu_w_in, m_sgu_ln_g, m_sgu_ln_b, m_sgu_w_s, m_sgu_b_s, m_sgu_w_out, m_mla_w_dq_dkv, m_mla_q_norm_g, m_mla_kv_norm_g, m_mla_w_uq, m_mla_w_ukv, m_mla_w_o, m_mlp_w1, m_mlp_w2, m_final_g, v_ada_w, v_ada_b, v_norm_mix_g, v_norm_mlp_g, v_pool_w, v_pool_scale, v_sgu_w_in, v_sgu_ln_g, v_sgu_ln_b, v_sgu_w_s, v_sgu_b_s, v_sgu_w_out, v_mla_w_dq_dkv, v_mla_q_norm_g, v_mla_kv_norm_g, v_mla_w_uq, v_mla_w_ukv, v_mla_w_o, v_mlp_w1, v_mlp_w2, v_final_g):
    P = dict(ada_w=ada_w, ada_b=ada_b, norm_mix_g=norm_mix_g, norm_mlp_g=norm_mlp_g, pool_w=pool_w, pool_scale=pool_scale, sgu_w_in=sgu_w_in,
             sgu_ln_g=sgu_ln_g, sgu_ln_b=sgu_ln_b, sgu_w_s=sgu_w_s, sgu_b_s=sgu_b_s, sgu_w_out=sgu_w_out, mla_w_dq_dkv=mla_w_dq_dkv,
             mla_q_norm_g=mla_q_norm_g, mla_kv_norm_g=mla_kv_norm_g, mla_w_uq=mla_w_uq, mla_w_ukv=mla_w_ukv, mla_w_o=mla_w_o, mlp_w1=mlp_w1,
             mlp_w2=mlp_w2, final_g=final_g)
    M = dict(ada_w=m_ada_w, ada_b=m_ada_b, norm_mix_g=m_norm_mix_g, norm_mlp_g=m_norm_mlp_g, pool_w=m_pool_w, pool_scale=m_pool_scale,
             sgu_w_in=m_sgu_w_in, sgu_ln_g=m_sgu_ln_g, sgu_ln_b=m_sgu_ln_b, sgu_w_s=m_sgu_w_s, sgu_b_s=m_sgu_b_s, sgu_w_out=m_sgu_w_out,
             mla_w_dq_dkv=m_mla_w_dq_dkv, mla_q_norm_g=m_mla_q_norm_g, mla_kv_norm_g=m_mla_kv_norm_g, mla_w_uq=m_mla_w_uq, mla_w_ukv=m_mla_w_ukv,
             mla_w_o=m_mla_w_o, mlp_w1=m_mlp_w1, mlp_w2=m_mlp_w2, final_g=m_final_g)
    V = dict(ada_w=v_ada_w, ada_b=v_ada_b, norm_mix_g=v_norm_mix_g, norm_mlp_g=v_norm_mlp_g, pool_w=v_pool_w, pool_scale=v_pool_scale,
             sgu_w_in=v_sgu_w_in, sgu_ln_g=v_sgu_ln_g, sgu_ln_b=v_sgu_ln_b, sgu_w_s=v_sgu_w_s, sgu_b_s=v_sgu_b_s, sgu_w_out=v_sgu_w_out,
             mla_w_dq_dkv=v_mla_w_dq_dkv, mla_q_norm_g=v_mla_q_norm_g, mla_kv_norm_g=v_mla_kv_norm_g, mla_w_uq=v_mla_w_uq, mla_w_ukv=v_mla_w_ukv,
             mla_w_o=v_mla_w_o, mlp_w1=v_mlp_w1, mlp_w2=v_mlp_w2, final_g=v_final_g)
    order = list(P)
    xi, yi, ci = _idx()
    chip = 2 * xi + yi
    D = D_MODEL
    n_ada = ada_w.shape[2]

    pre = _allgather8(_pack([c, pool_scale, mla_q_norm_g]), "ag_small")
    flat = pre.reshape(N_DEV, -1)
    c_all = flat[:, :D]
    ps_all = flat[0::2, D:D + 2 * (D // N_CHIPS)].reshape(N_CHIPS, 2, D // N_CHIPS).transpose(1, 0, 2).reshape(2, D)
    q0 = D + 2 * (D // N_CHIPS)
    qg_all = flat[0::2, q0:q0 + MLA_QL // N_CHIPS].reshape(1, MLA_QL)

    ada_b_loc = lax.dynamic_slice_in_dim(ada_b, chip * n_ada, n_ada, axis=1)[:, None, :]
    modp = _ada_fwd(c_all, ada_w, ada_b_loc, "ada_fwd")
    mod = _mod_exchange(modp.transpose(1, 0, 2), "mod_exchange").transpose(1, 0, 2).reshape(DEPTH, 6 * D)

    S = dict(norm_mix_g=norm_mix_g, norm_mlp_g=norm_mlp_g, pool_scale=ps_all, sgu_ln_g=sgu_ln_g, sgu_ln_b=sgu_ln_b, sgu_w_s=sgu_w_s[0],
             sgu_b_s=sgu_b_s[0], mla_q_norm_g=qg_all, mla_kv_norm_g=mla_kv_norm_g, final_g=final_g[None, :])
    cidx, ownidx = jnp.reshape(ci, (1,)).astype(jnp.int32), jnp.reshape(N_CHIPS * ci + chip, (1,)).astype(jnp.int32)
    view2d = lambda a: a.reshape(-1, a.shape[-1])

    def piece_rows(kind, blk):
        r = _PIECE_KINDS[kind][0]
        return blk * r, r

    groups = [_layer_pieces(0)[:-2], _layer_pieces(0)[-2:-1], _layer_pieces(0)[-1:], _layer_pieces(1)[:-2], _layer_pieces(1)[-2:],
              _layer_pieces(2), _layer_pieces(3)]
    start_after = {1: (3, 4), 3: (5,), 5: (6,)}
    gathers = {}

    def gather_start(g, dep):
        srcs, shapes = [], []
        for kind, blk in groups[g]:
            r0, r = piece_rows(kind, blk)
            cdim = _PIECE_KINDS[kind][1]
            srcs.append(view2d(P[kind])[r0:r0 + r].astype(BF16).reshape(2, r // 2, cdim))
            shapes.append(jax.ShapeDtypeStruct((N_CHIPS, 2, r // 2, cdim), BF16))
        gathers[g] = _xchip_start("gather", srcs, shapes, dep, f"ag_start_g{g}")

    def gather_finish(g, after):
        ssem, rsem, srcs, lands, _ = gathers.pop(g)
        deps = [after]
        for nxt in start_after.get(g, ()):
            gather_start(nxt, deps[-1])
            deps.append(gathers[nxt][-1])
        srcs, lands = _xchip_wait("gather", ssem, rsem, srcs, lands, deps, f"ag_wait_g{g}")
        lands = _sibling_fwd(lands, f"ag_sibling_g{g}")
        W = {}
        for (kind, _), s, land in zip(groups[g], srcs, lands, strict=True):
            r, cdim, to_full, _ = _PIECE_KINDS[kind]
            W[kind] = to_full(lax.dynamic_update_index_in_dim(land, s, chip, 0).reshape(N_CHIPS, r, cdim))
        return W

    def weights_of(i, part, x_i):
        g = {(0, "mix"): 0, (0, "mlp"): 1, (0, "mlp2"): 2, (1, "mix"): 3, (1, "mlp"): 4, (2, "mix"): 5, (3, "mix"): 6}.get((i, part))
        return {} if g is None else gather_finish(g, x_i)

    scatters = {}
    bufs = {n: tuple(lax.empty(view2d(P[n]).shape, F32) for _ in range(4)) for n in _PIECE_KINDS}

    def scatter_start(i, gW, dep):
        pcs = _layer_pieces(i)
        blocked = []
        for kind, _ in pcs:
            r, cdim, _, to_blocks = _PIECE_KINDS[kind]
            g = gW[kind]
            blocked.append(g if g.ndim == 4 else to_blocks(g).reshape(N_CHIPS, 2, r // 2, cdim).transpose(1, 0, 2, 3))
        shapes = [jax.ShapeDtypeStruct((N_DEV - 1, *b.shape[2:]), BF16) for b in blocked]
        scatters[i] = (pcs, *_xchip_start("scatter8", blocked, shapes, dep, f"rs_start_l{i}"))
        return scatters[i][-1]

    def scatter_finish(i, after):
        pcs, ssem, rsem, blocked, lands, _ = scatters.pop(i)
        blocked, lands = _xchip_wait("scatter8", ssem, rsem, blocked, lands, after, f"rs_wait_l{i}")
        halves = [_sum_sel(ownidx, b.reshape(2 * N_CHIPS, *b.shape[2:]), [l], f"rs_sum_l{i}_{kind}", F32)
                  for (kind, _), b, l in zip(pcs, blocked, lands, strict=True)]
        got = _sibling_send(halves, f"rs_merge_l{i}")
        for (kind, blk), mine, other in zip(pcs, halves, got, strict=True):
            r0, _ = piece_rows(kind, blk)
            bufs[kind] = tuple(_adamw_piece(cidx, view2d(P[kind]), view2d(M[kind]), view2d(V[kind]), mine, other, bufs[kind], r0,
                                            f"adamw_l{i}_{kind}"))
        return lands[0]

    first_layer = {}

    def grads_of(i, gW, dx_i):
        if i == 0:
            first_layer.update(gW)
            return None
        dep = scatter_finish(i + 2, [dx_i]) if i + 2 in scatters else dx_i
        return scatter_start(i, gW, dep)

    gather_start(0, mod)
    gather_start(1, gathers[0][-1])
    gather_start(2, gathers[1][-1])
    mod = mod + gathers[2][-1][0, 0]
    loss_l, dx, gS, dmod = _local_step(x[0], positions[0], loss_target[0], mod, S, weights_of, grads_of)

    gS["dmod"] = dmod
    gS["loss"] = loss_l
    packed = _pack([gS[n] for n in _SMALL])
    sg = _xchip_start("all8", [packed], [jax.ShapeDtypeStruct((N_DEV, *packed.shape), F32)], dx, "sg_start")
    tok0 = scatter_start(0, first_layer, sg[-1])[0, 0]
    done = lambda layer: [bufs[kind][0] for kind, _ in _layer_pieces(layer)]
    scatter_finish(2, [dx, scatters[0][-1]])
    scatter_finish(1, done(2))
    sg_src, sg_land = _xchip_wait("all8", sg[0], sg[1], sg[2], sg[3], done(1), "sg_wait")
    small = lax.dynamic_update_index_in_dim(sg_land[0], sg_src[0], 4 * xi + 2 * yi + ci, 0) + tok0
    small_sum = _unpack(_sum_lead([small], "sum_small_grads"), list(_SMALL.values()))
    G = dict(zip(_SMALL, small_sum, strict=True))
    grads = {
        "ada_b": G["dmod"], "norm_mix_g": G["norm_mix_g"], "norm_mlp_g": G["norm_mlp_g"], "sgu_ln_g": G["sgu_ln_g"], "sgu_ln_b": G["sgu_ln_b"],
        "sgu_w_s": G["sgu_w_s"][None], "sgu_b_s": G["sgu_b_s"][None], "mla_kv_norm_g": G["mla_kv_norm_g"], "final_g": G["final_g"][0],
        "pool_scale": lax.dynamic_slice_in_dim(G["pool_scale"], chip * (D // N_CHIPS), D // N_CHIPS, axis=1),
        "mla_q_norm_g": lax.dynamic_slice_in_dim(G["mla_q_norm_g"], chip * (MLA_QL // N_CHIPS), MLA_QL // N_CHIPS, axis=1),
    }
    dmod_all = _unpack(small, [(N_DEV,) + (small.shape[1] * _PACK_W,)])[0]
    off = sum(math.prod(s) for n, s in _SMALL.items() if n != "dmod")
    dmod_all = dmod_all[:, off:off + DEPTH * 6 * D].reshape(N_DEV, DEPTH, 6 * D)
    dmod_loc = lax.dynamic_slice_in_dim(dmod_all, chip * n_ada, n_ada, axis=2).transpose(1, 0, 2)
    deltas, new_m, new_v = {}, {}, {}
    grads["ada_w"], deltas["ada_w"], new_m["ada_w"], new_v["ada_w"] = _ada_bwd_adamw(c_all.T, dmod_loc, ada_w, m_ada_w, v_ada_w, "adamw_ada_w")
    for n in order:
        if n not in _PIECE_KINDS and n != "ada_w":
            deltas[n], new_m[n], new_v[n] = _adamw(P[n], grads[n].reshape(P[n].shape), M[n], V[n], f"adamw_{n}")
    scatter_finish(0, [deltas["ada_w"], deltas["sgu_w_s"]] + [bufs[n][0] for n in ("mlp_w1", "mlp_w2", "sgu_w_in", "mla_w_o")])
    for n in _PIECE_KINDS:
        grads[n], deltas[n], new_m[n], new_v[n] = (b.reshape(P[n].shape) for b in bufs[n])
    return (G["loss"][0, 0], dx[None], *[grads[n].reshape(P[n].shape) for n in order], *[deltas[n] for n in order], *[new_m[n] for n in order],
            *[new_v[n] for n in order])
```

```python
import math

import jax
import jax.numpy as jnp
from jax import lax
from jax.experimental import pallas as pl
from jax.experimental.pallas import tpu as pltpu

F32, BF16 = jnp.float32, jnp.bfloat16
MESH = pl.DeviceIdType.MESH

D_MODEL = 1024
DEPTH = 4
N_MIXERS = 3
POOL_WINDOWS = (2, 4, 8, 16)
POOL_GD = D_MODEL // len(POOL_WINDOWS)
POOL_HALO = 16
SGU_CHUNK = 128
SGU_W = D_MODEL
SGU_HD = 128
SGU_H = SGU_W // SGU_HD
MLA_H = 16
MLA_QL = 256
MLA_KVL = 128
MLA_NOPE = 128
MLA_ROPE = 64
MLA_V = 128
MLA_HP = 256
MLA_LATP = 512
ROPE_THETA = 10000.0
RMS_EPS = 1e-6
LN_EPS = 1e-5
SM_SCALE = (MLA_NOPE + MLA_ROPE) ** -0.5
NEG_INF = -1e30
ADAM_LR, ADAM_B1, ADAM_B2, ADAM_EPS, ADAM_WD, ADAM_STEP = 0.001, 0.9, 0.999, 1e-08, 0.01, 10
N_CHIPS = 4
N_DEV = 8
ROW_TILE = 512
ATT_TILE = 512
ATT_SUB = 256
ATT_FWD_HEADS = 4
ATT_BWD_HEADS = 2
MM_EPI_COLS = 256
MM_TM_WIDE = 2048
MM_VMEM_BUDGET = 40 << 20


def _idx():
    return lax.axis_index("x"), lax.axis_index("y"), lax.axis_index("c")


def _mm(a, b, *, name, ta=False, tb=False, epi=None, extras=(), out_dtypes=(BF16,), tm=1024, tn=1024, tk=1024, chip_blocks=None, after=None,
        epi_cols=None):
    if ta:
        K, M = a.shape
    else:
        M, K = a.shape
    b_chips = b.ndim == 3
    if b_chips:
        assert b.shape[0] == N_CHIPS
        Kb, N = (N_CHIPS * b.shape[2], b.shape[1]) if tb else (b.shape[1], N_CHIPS * b.shape[2])
    elif tb:
        N, Kb = b.shape
    else:
        Kb, N = b.shape
    assert K == Kb, (a.shape, b.shape, ta, tb)
    if b_chips and not tb:
        tn = min(tn, N // N_CHIPS)
    if chip_blocks == "col":
        tm, tn = min(tm, M // 2), min(tn, N // N_CHIPS)
    elif chip_blocks == "row":
        tm = min(tm, M // N_CHIPS // 2)
    tm, tn, tk = min(tm, M), min(tn, N), min(tk, K)

    def vmem_bytes(tm_, tk_):
        per_mn = sum(arr.dtype.itemsize for arr, kind in extras if kind == "mn") + sum(jnp.dtype(dt).itemsize for dt in out_dtypes)
        return 2 * (tm_ * tk_ * a.dtype.itemsize + tk_ * tn * b.dtype.itemsize + tm_ * tn * per_mn)

    if vmem_bytes(tm, K) <= MM_VMEM_BUDGET:
        tk = K
    elif tm >= 512 and vmem_bytes(tm // 2, K) <= MM_VMEM_BUDGET:
        tm, tk = tm // 2, K
    assert M % tm == 0 and N % tn == 0 and K % tk == 0, (M, N, K, tm, tn, tk)
    nk = K // tk
    assert epi_cols is None or (nk == 1 and not ta and not (b_chips and tb) and tn % epi_cols == 0)
    a_spec = pl.BlockSpec((tk, tm), lambda i, j, k: (k, i)) if ta else pl.BlockSpec((tm, tk), lambda i, j, k: (i, k))
    b_spec = pl.BlockSpec((tn, tk), lambda i, j, k: (j, k)) if tb else pl.BlockSpec((tk, tn), lambda i, j, k: (k, j))
    if b_chips and tb:
        assert nk == 1 and not ta
        b_spec = pl.BlockSpec((N_CHIPS, tn, K // N_CHIPS), lambda i, j, k: (0, j, 0))
    elif b_chips:
        per = N // N_CHIPS // tn
        b_spec = pl.BlockSpec((None, tk, tn), lambda i, j, k: (j // per, k, j % per))
    ex_specs = []
    for arr, kind in extras:
        if kind == "mn":
            ex_specs.append(pl.BlockSpec((tm, tn), lambda i, j, k: (i, j)))
        elif kind == "n":
            ex_specs.append(pl.BlockSpec((1, tn), lambda i, j, k: (0, j)))
        else:
            ex_specs.append(pl.BlockSpec((tm, arr.shape[1]), lambda i, j, k: (i, 0)))
    n_ex, n_out = len(extras), len(out_dtypes)
    n_in = 2 + n_ex + (after is not None)
    dims = (((0 if ta else 1,), (1 if tb else 0,)), ((), ()))

    def body(*refs):
        a_ref, b_ref = refs[0], refs[1]
        ex_refs = refs[2:2 + n_ex]
        out_refs = refs[n_in:n_in + n_out]
        if b_chips and tb:
            kc = K // N_CHIPS
            part = None
            for cb in range(N_CHIPS):
                p = lax.dot_general(a_ref[:, cb * kc:(cb + 1) * kc].astype(BF16), b_ref[cb].astype(BF16), dims, preferred_element_type=F32)
                part = p if part is None else part + p
        elif epi_cols is not None:
            av = a_ref[...].astype(BF16)
            chunk = lambda cc: lax.dot_general(av, (b_ref[cc * epi_cols:(cc + 1) * epi_cols, :] if tb else b_ref[:, cc * epi_cols:(cc + 1) * epi_cols])
                                               .astype(BF16), dims, preferred_element_type=F32)
            acc = chunk(0)
            for cc in range(tn // epi_cols):
                nxt = chunk(cc + 1) if cc + 1 < tn // epi_cols else None
                cs = slice(cc * epi_cols, (cc + 1) * epi_cols)
                for r, o in zip(out_refs, epi(acc, *[r[:, cs] for r in ex_refs]), strict=True):
                    r[:, cs] = o.astype(r.dtype)
                acc = nxt
            return
        else:
            part = lax.dot_general(a_ref[...].astype(BF16), b_ref[...].astype(BF16), dims, preferred_element_type=F32)

        def finish(acc):
            outs = epi(acc, *[r[...] for r in ex_refs]) if epi is not None else (acc,)
            for r, o in zip(out_refs, outs, strict=True):
                r[...] = o.astype(r.dtype)

        if nk == 1:
            finish(part)
        else:
            acc_ref = refs[-1]
            k = pl.program_id(2)

            @pl.when(k == 0)
            def _():
                acc_ref[...] = part

            @pl.when(k > 0)
            def _():
                acc_ref[...] += part

            @pl.when(k == nk - 1)
            def _():
                finish(acc_ref[...])

    out_specs = [pl.BlockSpec((tm, tn), lambda i, j, k: (i, j)) for _ in range(n_out)]
    out_shape = [jax.ShapeDtypeStruct((M, N), dt) for dt in out_dtypes]
    if chip_blocks is not None:
        assert n_out == 1
        if chip_blocks == "col":
            rh, cb = M // 2 // tm, N // N_CHIPS // tn
            out_specs = [pl.BlockSpec((None, None, tm, tn), lambda i, j, k: (i // rh, j // cb, i % rh, j % cb))]
            out_shape = [jax.ShapeDtypeStruct((2, N_CHIPS, M // 2, N // N_CHIPS), out_dtypes[0])]
        else:
            rh = M // N_CHIPS // 2 // tm
            out_specs = [pl.BlockSpec((None, None, tm, tn), lambda i, j, k: ((i // rh) % 2, i // (2 * rh), i % rh, j))]
            out_shape = [jax.ShapeDtypeStruct((2, N_CHIPS, M // N_CHIPS // 2, N), out_dtypes[0])]
    outs = pl.pallas_call(
        body,
        name=name,
        grid=(M // tm, N // tn, nk),
        in_specs=[a_spec, b_spec, *ex_specs] + ([pl.BlockSpec(memory_space=pl.ANY)] if after is not None else []),
        out_specs=out_specs,
        out_shape=out_shape,
        scratch_shapes=[pltpu.VMEM((tm, tn), F32)] if nk > 1 else [],
        compiler_params=pltpu.CompilerParams(dimension_semantics=("parallel", "parallel", "arbitrary")),
    )(a, b, *[arr for arr, _ in extras], *([after] if after is not None else []))
    return outs[0] if n_out == 1 else tuple(outs)


def _epi_sq_relu(acc):
    r = jnp.maximum(acc, 0.0)
    return r * r, 2.0 * r


def _epi_residual(acc, x, g):
    return x + g * acc, acc


def _rms_mod(xv, gain, sc, sh):
    r = lax.rsqrt(jnp.mean(xv * xv, axis=-1, keepdims=True) + RMS_EPS)
    return ((xv * r) * gain) * (1.0 + sc) + sh


def _epi_residual_norm(acc, x, g, gain, sc, sh):
    xn = x + g * acc
    return xn, acc, _rms_mod(xn, gain, sc, sh)


def _row_spec(tr, d):
    return pl.BlockSpec((tr, d), lambda i: (i, 0))


def _vec_spec(d):
    return pl.BlockSpec((1, d), lambda i: (0, 0))


def _colsum(v):
    return jnp.sum(v, axis=0, keepdims=True)


def _norm_mod_fwd(x, gain, sc, sh, out_dtype, name):
    T, D = x.shape
    tr = min(T, ROW_TILE)

    def body(x_ref, g_ref, sc_ref, sh_ref, o_ref):
        o_ref[...] = _rms_mod(x_ref[...], g_ref[...], sc_ref[...], sh_ref[...]).astype(o_ref.dtype)

    return pl.pallas_call(
        body, name=name, grid=(T // tr,),
        in_specs=[_row_spec(tr, D), _vec_spec(D), _vec_spec(D), _vec_spec(D)],
        out_specs=_row_spec(tr, D),
        out_shape=jax.ShapeDtypeStruct((T, D), out_dtype),
        compiler_params=pltpu.CompilerParams(dimension_semantics=("parallel",)),
    )(x, gain, sc, sh)


def _norm_mod_bwd(x, dh, dres, gain, sc, name, res=None):
    T, D = x.shape
    tr = min(T, 2 * ROW_TILE)

    def body(x_ref, dh_ref, dres_ref, g_ref, sc_ref, *refs):
        dx_ref, dg_ref, dsc_ref, dsh_ref = refs[-6:-2] if res is not None else refs

        @pl.when(pl.program_id(0) == 0)
        def _():
            dg_ref[...] = jnp.zeros_like(dg_ref)
            dsc_ref[...] = jnp.zeros_like(dsc_ref)
            dsh_ref[...] = jnp.zeros_like(dsh_ref)
            if res is not None:
                refs[-1][...] = jnp.zeros_like(refs[-1])

        xv = x_ref[...]
        r = lax.rsqrt(jnp.mean(xv * xv, axis=-1, keepdims=True) + RMS_EPS)
        xn = xv * r
        dhv = dh_ref[...].astype(F32)
        dsh_ref[...] += _colsum(dhv)
        dsc_ref[...] += _colsum(dhv * (xn * g_ref[...]))
        dt = dhv * (1.0 + sc_ref[...])
        dg_ref[...] += _colsum(dt * xn)
        dxn = dt * g_ref[...]
        dxv = dres_ref[...] + r * (dxn - xn * jnp.mean(dxn * xn, axis=-1, keepdims=True))
        dx_ref[...] = dxv
        if res is not None:
            y_ref, gr_ref, dy_ref, q_ref = refs[0], refs[1], refs[-2], refs[-1]
            dy_ref[...] = (gr_ref[...] * dxv).astype(BF16)
            q_ref[...] += _colsum(dxv * y_ref[...].astype(F32))

    extra_in, extra_spec = ([], []) if res is None else (list(res), [_row_spec(tr, D), _vec_spec(D)])
    return pl.pallas_call(
        body, name=name, grid=(T // tr,),
        in_specs=[_row_spec(tr, D), _row_spec(tr, D), _row_spec(tr, D), _vec_spec(D), _vec_spec(D), *extra_spec],
        out_specs=[_row_spec(tr, D), _vec_spec(D), _vec_spec(D), _vec_spec(D)] + ([_row_spec(tr, D), _vec_spec(D)] if res is not None else []),
        out_shape=[jax.ShapeDtypeStruct((T, D), F32)] + [jax.ShapeDtypeStruct((1, D), F32)] * 3
        + ([jax.ShapeDtypeStruct((T, D), BF16), jax.ShapeDtypeStruct((1, D), F32)] if res is not None else []),
        compiler_params=pltpu.CompilerParams(dimension_semantics=("arbitrary",)),
    )(x, dh, dres, gain, sc, *extra_in)


def _loss_head(x, target, gain, y, g, name):
    T, D = x.shape
    tr = min(T, ROW_TILE)

    def body(x_ref, t_ref, g_ref, y_ref, gr_ref, loss_ref, dx_ref, dg_ref, dy_ref, q_ref):
        @pl.when(pl.program_id(0) == 0)
        def _():
            loss_ref[...] = jnp.zeros_like(loss_ref)
            dg_ref[...] = jnp.zeros_like(dg_ref)
            q_ref[...] = jnp.zeros_like(q_ref)

        xv = x_ref[...]
        r = lax.rsqrt(jnp.mean(xv * xv, axis=-1, keepdims=True) + RMS_EPS)
        xn = xv * r
        err = xn * g_ref[...] - t_ref[...]
        row = jnp.mean(err * err, axis=-1, keepdims=True)
        loss_ref[...] += 0.5 * jnp.sum(row, axis=0, keepdims=True)
        dy = err * (1.0 / D)
        dg_ref[...] += _colsum(dy * xn)
        dxn = dy * g_ref[...]
        dxv = r * (dxn - xn * jnp.mean(dxn * xn, axis=-1, keepdims=True))
        dx_ref[...] = dxv
        dy_ref[...] = (gr_ref[...] * dxv).astype(BF16)
        q_ref[...] += _colsum(dxv * y_ref[...].astype(F32))

    return pl.pallas_call(
        body, name=name, grid=(T // tr,),
        in_specs=[_row_spec(tr, D), _row_spec(tr, D), _vec_spec(D), _row_spec(tr, D), _vec_spec(D)],
        out_specs=[_vec_spec(128), _row_spec(tr, D), _vec_spec(D), _row_spec(tr, D), _vec_spec(D)],
        out_shape=[jax.ShapeDtypeStruct((1, 128), F32), jax.ShapeDtypeStruct((T, D), F32), jax.ShapeDtypeStruct((1, D), F32),
                   jax.ShapeDtypeStruct((T, D), BF16), jax.ShapeDtypeStruct((1, D), F32)],
        compiler_params=pltpu.CompilerParams(dimension_semantics=("arbitrary",)),
    )(x, target, gain, y, g)


def _pool_fwd(h, w, scale, x, g1, gmlp, sc2, sh2, name):
    T, D = h.shape
    tr = min(T, ROW_TILE)

    def body(h_ref, w_ref, sc_ref, x_ref, g_ref, gm_ref, sc2_ref, sh2_ref, x2_ref, pooled_ref, ypre_ref, h2_ref, halo_ref):
        i = pl.program_id(0)

        @pl.when(i == 0)
        def _():
            halo_ref[...] = jnp.zeros_like(halo_ref)

        hv = h_ref[...]
        buf = jnp.concatenate([halo_ref[...], hv], axis=0)
        halo_ref[...] = hv[tr - POOL_HALO:, :]
        t = (i * tr + lax.broadcasted_iota(jnp.int32, (tr, 1), 0)).astype(F32)
        for gi, win in enumerate(POOL_WINDOWS):
            cols = slice(gi * POOL_GD, (gi + 1) * POOL_GD)
            val = buf[:, cols]
            sh = 1
            while sh < win:
                val = val + pltpu.roll(val, sh, axis=0)
                sh *= 2
            pooled = val[POOL_HALO:, :] / jnp.minimum(t + 1.0, float(win)) - hv[:, cols]
            pb = pooled.astype(BF16)
            pooled_ref[:, cols] = pb
            yp = jnp.dot(pb, w_ref[gi], preferred_element_type=F32)
            ypre_ref[:, cols] = yp.astype(BF16)
            x2_ref[:, cols] = x_ref[:, cols] + g_ref[:, cols] * (yp * sc_ref[:, cols])
        h2_ref[...] = _rms_mod(x2_ref[...], gm_ref[...], sc2_ref[...], sh2_ref[...]).astype(BF16)

    return pl.pallas_call(
        body, name=name, grid=(T // tr,),
        in_specs=[_row_spec(tr, D), pl.BlockSpec(w.shape, lambda i: (0, 0, 0)), _vec_spec(D), _row_spec(tr, D), _vec_spec(D), _vec_spec(D),
                  _vec_spec(D), _vec_spec(D)],
        out_specs=[_row_spec(tr, D)] * 4,
        out_shape=[jax.ShapeDtypeStruct((T, D), F32), jax.ShapeDtypeStruct((T, D), BF16), jax.ShapeDtypeStruct((T, D), BF16),
                   jax.ShapeDtypeStruct((T, D), BF16)],
        scratch_shapes=[pltpu.VMEM((POOL_HALO, D), F32)],
        compiler_params=pltpu.CompilerParams(dimension_semantics=("arbitrary",)),
    )(h, w, scale, x, g1, gmlp, sc2, sh2)


def _pool_bwd(dy, pooled, w, scale, g1, q, name):
    T, D = dy.shape
    tr = min(T, ROW_TILE)
    nt = T // tr
    ltot = tr + POOL_HALO

    def body(dy_ref, pooled_ref, w_ref, sc_ref, g_ref, q_ref, dh_ref, dw_ref, dsc_ref, dg_ref, halo_ref):
        i = pl.program_id(0)

        @pl.when(i == 0)
        def _():
            halo_ref[...] = jnp.zeros_like(halo_ref)
            dw_ref[...] = jnp.zeros_like(dw_ref)
            dsc_ref[...] = g_ref[...] * q_ref[...]
            dg_ref[...] = sc_ref[...] * q_ref[...]

        t = ((nt - 1 - i) * tr + lax.broadcasted_iota(jnp.int32, (tr, 1), 0)).astype(F32)
        for gi, win in enumerate(POOL_WINDOWS):
            cols = slice(gi * POOL_GD, (gi + 1) * POOL_GD)
            dyb = (dy_ref[:, cols].astype(F32) * sc_ref[:, cols]).astype(BF16)
            dw_ref[gi] += lax.dot_general(pooled_ref[:, cols], dyb, (((0,), (0,)), ((), ())), preferred_element_type=F32)
            dpool = lax.dot_general(dyb, w_ref[gi], (((1,), (1,)), ((), ())), preferred_element_type=F32)
            qv = dpool / jnp.minimum(t + 1.0, float(win))
            val = jnp.concatenate([qv, halo_ref[:, cols]], axis=0)
            halo_ref[:, cols] = qv[:POOL_HALO, :]
            sh = 1
            while sh < win:
                val = val + pltpu.roll(val, ltot - sh, axis=0)
                sh *= 2
            dh_ref[:, cols] = (val[:tr, :] - dpool).astype(BF16)

    rev = pl.BlockSpec((tr, D), lambda i: (nt - 1 - i, 0))
    return pl.pallas_call(
        body, name=name, grid=(nt,),
        in_specs=[rev, rev, pl.BlockSpec(w.shape, lambda i: (0, 0, 0)), _vec_spec(D), _vec_spec(D), _vec_spec(D)],
        out_specs=[rev, pl.BlockSpec(w.shape, lambda i: (0, 0, 0)), _vec_spec(D), _vec_spec(D)],
        out_shape=[jax.ShapeDtypeStruct((T, D), BF16), jax.ShapeDtypeStruct(w.shape, F32),
                   jax.ShapeDtypeStruct((1, D), F32), jax.ShapeDtypeStruct((1, D), F32)],
        scratch_shapes=[pltpu.VMEM((POOL_HALO, D), F32)],
        compiler_params=pltpu.CompilerParams(dimension_semantics=("arbitrary",)),
    )(dy, pooled, w, scale, g1, q)


_INV_SQRT2 = 0.7071067811865476
_INV_SQRT2PI = 0.3989422804014327


def _gelu(v):
    return 0.5 * v * (1.0 + lax.erf(v * _INV_SQRT2))


def _gelu_grad(v):
    return 0.5 * (1.0 + lax.erf(v * _INV_SQRT2)) + v * jnp.exp(-0.5 * v * v) * _INV_SQRT2PI


def _sgu_ln(v, g, b):
    mu = jnp.mean(v, axis=-1, keepdims=True)
    xc = v - mu
    rstd = lax.rsqrt(jnp.mean(xc * xc, axis=-1, keepdims=True) + LN_EPS)
    xh = xc * rstd
    return xh, rstd, xh * g + b


def _tril_mask():
    return lax.broadcasted_iota(jnp.int32, (SGU_CHUNK, SGU_CHUNK), 0) >= lax.broadcasted_iota(jnp.int32, (SGU_CHUNK, SGU_CHUNK), 1)


SGU_TILE = 256


def _sgu_gate_fwd(zz, ln_g, ln_b, ws, bs_t, name):
    T = zz.shape[0]
    ts = min(T, SGU_TILE)

    def body(zz_ref, g_ref, b_ref, ws_ref, bs_ref, out_ref):
        z = _gelu(zz_ref[...])
        u = z[:, :SGU_W]
        _, _, vn = _sgu_ln(z[:, SGU_W:], g_ref[...], b_ref[...])
        vb = vn.astype(BF16)
        tril = _tril_mask()
        for hh in range(SGU_H):
            wm = jnp.where(tril, ws_ref[hh], 0.0).astype(BF16)
            bcol = bs_ref[:, hh:hh + 1]
            cs = slice(hh * SGU_HD, (hh + 1) * SGU_HD)
            for j in range(ts // SGU_CHUNK):
                rs = slice(j * SGU_CHUNK, (j + 1) * SGU_CHUNK)
                mixed = jnp.dot(wm, vb[rs, cs], preferred_element_type=F32) + bcol
                out_ref[rs, cs] = (u[rs, cs] * mixed).astype(BF16)

    return pl.pallas_call(
        body, name=name, grid=(T // ts,),
        in_specs=[_row_spec(ts, 2 * SGU_W), _vec_spec(SGU_W), _vec_spec(SGU_W),
                  pl.BlockSpec(ws.shape, lambda i: (0, 0, 0)), pl.BlockSpec(bs_t.shape, lambda i: (0, 0))],
        out_specs=_row_spec(ts, SGU_W),
        out_shape=jax.ShapeDtypeStruct((T, SGU_W), BF16),
        compiler_params=pltpu.CompilerParams(dimension_semantics=("parallel",)),
    )(zz, ln_g, ln_b, ws, bs_t)


def _sgu_gate_bwd(zz, dgated, ln_g, ln_b, ws, bs_t, name):
    T = zz.shape[0]
    ts = min(T, SGU_TILE)
    nt = T // ts

    def body(zz_ref, dg_ref, g_ref, b_ref, ws_ref, bs_ref, dzz_ref, dws_ref, dbs_ref, dlg_ref, dlb_ref, dlo_ref, dmx_ref):
        i = pl.program_id(0)

        @pl.when(i == 0)
        def _():
            dws_ref[...] = jnp.zeros_like(dws_ref)
            dmx_ref[...] = jnp.zeros_like(dmx_ref)
            dlg_ref[...] = jnp.zeros_like(dlg_ref)
            dlb_ref[...] = jnp.zeros_like(dlb_ref)

        zzv = zz_ref[...]
        z = _gelu(zzv)
        u = z[:, :SGU_W]
        xh, rstd, vn = _sgu_ln(z[:, SGU_W:], g_ref[...], b_ref[...])
        vb = vn.astype(BF16)
        dgv = dg_ref[...].astype(F32)
        tril = _tril_mask()
        for hh in range(SGU_H):
            wm = jnp.where(tril, ws_ref[hh], 0.0).astype(BF16)
            bcol = bs_ref[:, hh:hh + 1]
            cs = slice(hh * SGU_HD, (hh + 1) * SGU_HD)
            for j in range(ts // SGU_CHUNK):
                rs = slice(j * SGU_CHUNK, (j + 1) * SGU_CHUNK)
                mixed = jnp.dot(wm, vb[rs, cs], preferred_element_type=F32) + bcol
                dmixed = dgv[rs, cs] * u[rs, cs]
                dzz_ref[rs, cs] = (dgv[rs, cs] * mixed * _gelu_grad(zzv[rs, cs])).astype(BF16)
                dmb = dmixed.astype(BF16)
                dws_ref[hh] += lax.dot_general(dmb, vb[rs, cs], (((1,), (1,)), ((), ())), preferred_element_type=F32)
                dmx_ref[hh] += dmixed
                dlo_ref[rs, cs] = lax.dot_general(wm, dmb, (((0,), (0,)), ((), ())), preferred_element_type=F32)
        dlo = dlo_ref[...]
        dlg_ref[...] += _colsum(dlo * xh)
        dlb_ref[...] += _colsum(dlo)
        dxh = dlo * g_ref[...]
        dv = rstd * (dxh - jnp.mean(dxh, axis=-1, keepdims=True) - xh * jnp.mean(dxh * xh, axis=-1, keepdims=True))
        dzz_ref[:, SGU_W:] = (dv * _gelu_grad(zzv[:, SGU_W:])).astype(BF16)

        @pl.when(i == nt - 1)
        def _():
            tril_f = tril.astype(F32)
            for hh in range(SGU_H):
                dws_ref[hh] = dws_ref[hh] * tril_f
                dbs_ref[hh] = jnp.broadcast_to(jnp.sum(dmx_ref[hh], axis=-1, keepdims=True), (SGU_CHUNK, SGU_HD))

    full3 = pl.BlockSpec(ws.shape, lambda i: (0, 0, 0))
    return pl.pallas_call(
        body, name=name, grid=(nt,),
        in_specs=[_row_spec(ts, 2 * SGU_W), _row_spec(ts, SGU_W), _vec_spec(SGU_W), _vec_spec(SGU_W), full3,
                  pl.BlockSpec(bs_t.shape, lambda i: (0, 0))],
        out_specs=[_row_spec(ts, 2 * SGU_W), full3, full3, _vec_spec(SGU_W), _vec_spec(SGU_W)],
        out_shape=[jax.ShapeDtypeStruct((T, 2 * SGU_W), BF16), jax.ShapeDtypeStruct(ws.shape, F32), jax.ShapeDtypeStruct(ws.shape, F32),
                   jax.ShapeDtypeStruct((1, SGU_W), F32), jax.ShapeDtypeStruct((1, SGU_W), F32)],
        scratch_shapes=[pltpu.VMEM((ts, SGU_W), F32), pltpu.VMEM(ws.shape, F32)],
        compiler_params=pltpu.CompilerParams(dimension_semantics=("arbitrary",)),
    )(zz, dgated, ln_g, ln_b, ws, bs_t)


def _rope_fwd(blk, cc, sa, sb):
    return blk * cc + pltpu.roll(blk, 96, axis=1) * sa + pltpu.roll(blk, 32, axis=1) * sb


def _rope_bwd(d, cc, sa, sb):
    return d * cc + pltpu.roll(d * sa, 32, axis=1) + pltpu.roll(d * sb, 96, axis=1)


def _rms(v, g):
    r = lax.rsqrt(jnp.mean(v * v, axis=-1, keepdims=True) + RMS_EPS)
    vn = v * r
    return vn, r, vn * g


def _rms_bwd(dy, vn, r, g):
    dvn = dy * g
    return r * (dvn - vn * jnp.mean(dvn * vn, axis=-1, keepdims=True))


_KV0 = MLA_QL
_KR0 = MLA_QL + MLA_KVL


def _mla_lat_fwd(lat, qg, kvg, cc, sa, sb, name):
    T = lat.shape[0]
    tr = min(T, ROW_TILE)

    def body(lat_ref, qg_ref, kvg_ref, cc_ref, sa_ref, sb_ref, cq_ref, ckv_ref, kr_ref):
        lv = lat_ref[...]
        cq_ref[...] = _rms(lv[:, :_KV0], qg_ref[...])[2].astype(BF16)
        ckv_ref[...] = _rms(lv[:, _KV0:_KR0], kvg_ref[...])[2].astype(BF16)
        kr_ref[...] = _rope_fwd(lv[:, _KR0:], cc_ref[...], sa_ref[...], sb_ref[...])

    return pl.pallas_call(
        body, name=name, grid=(T // tr,),
        in_specs=[_row_spec(tr, MLA_LATP), _vec_spec(MLA_QL), _vec_spec(MLA_KVL), _row_spec(tr, 128), _row_spec(tr, 128), _row_spec(tr, 128)],
        out_specs=[_row_spec(tr, MLA_QL), _row_spec(tr, MLA_KVL), _row_spec(tr, 128)],
        out_shape=[jax.ShapeDtypeStruct((T, MLA_QL), BF16), jax.ShapeDtypeStruct((T, MLA_KVL), BF16), jax.ShapeDtypeStruct((T, 128), F32)],
        compiler_params=pltpu.CompilerParams(dimension_semantics=("parallel",)),
    )(lat, qg, kvg, cc, sa, sb)


def _mla_lat_bwd(lat, dcqn, dckvn, dkrot, qg, kvg, cc, sa, sb, name):
    T = lat.shape[0]
    tr = min(T, ROW_TILE)

    def body(lat_ref, dcq_ref, dckv_ref, dkr_ref, qg_ref, kvg_ref, cc_ref, sa_ref, sb_ref, dlat_ref, dqg_ref, dkvg_ref):
        @pl.when(pl.program_id(0) == 0)
        def _():
            dqg_ref[...] = jnp.zeros_like(dqg_ref)
            dkvg_ref[...] = jnp.zeros_like(dkvg_ref)

        lv = lat_ref[...]
        qn, qr, _ = _rms(lv[:, :_KV0], qg_ref[...])
        kn, kr, _ = _rms(lv[:, _KV0:_KR0], kvg_ref[...])
        dcq = dcq_ref[...]
        dckv = dckv_ref[...]
        dqg_ref[...] += _colsum(dcq * qn)
        dkvg_ref[...] += _colsum(dckv * kn)
        dlat_ref[:, :_KV0] = _rms_bwd(dcq, qn, qr, qg_ref[...]).astype(BF16)
        dlat_ref[:, _KV0:_KR0] = _rms_bwd(dckv, kn, kr, kvg_ref[...]).astype(BF16)
        dlat_ref[:, _KR0:] = _rope_bwd(dkr_ref[...], cc_ref[...], sa_ref[...], sb_ref[...]).astype(BF16)

    return pl.pallas_call(
        body, name=name, grid=(T // tr,),
        in_specs=[_row_spec(tr, MLA_LATP), _row_spec(tr, MLA_QL), _row_spec(tr, MLA_KVL), _row_spec(tr, 128),
                  _vec_spec(MLA_QL), _vec_spec(MLA_KVL), _row_spec(tr, 128), _row_spec(tr, 128), _row_spec(tr, 128)],
        out_specs=[_row_spec(tr, MLA_LATP), _vec_spec(MLA_QL), _vec_spec(MLA_KVL)],
        out_shape=[jax.ShapeDtypeStruct((T, MLA_LATP), BF16), jax.ShapeDtypeStruct((1, MLA_QL), F32), jax.ShapeDtypeStruct((1, MLA_KVL), F32)],
        compiler_params=pltpu.CompilerParams(dimension_semantics=("arbitrary",)),
    )(lat, dcqn, dckvn, dkrot, qg, kvg, cc, sa, sb)


LOG2E = 1.4426950408889634
Q_SCALE = SM_SCALE * LOG2E


def _epi_q_rope(acc, cc, sa, sb):
    out = []
    for hh in range(acc.shape[1] // MLA_HP):
        a, m, b = hh * MLA_HP, hh * MLA_HP + MLA_NOPE, (hh + 1) * MLA_HP
        out += [acc[:, a:m] * Q_SCALE, _rope_fwd(acc[:, m:b], cc, sa, sb) * Q_SCALE]
    return (jnp.concatenate(out, axis=1),)


def _mla_ukv(ckvn, w_ukv, krot, name):
    T = ckvn.shape[0]
    tr = min(T, ATT_TILE)
    hg = ATT_HG
    gw = hg * MLA_HP

    def body(a_ref, w_ref, kr_ref, ko_ref, kt_ref, vo_ref, vt_ref):
        acc = jnp.dot(a_ref[...], w_ref[...], preferred_element_type=F32)
        kr = kr_ref[...]
        krb, krt = kr.astype(BF16), kr.T.astype(BF16)
        for hh in range(hg):
            a, m, b = hh * MLA_HP, hh * MLA_HP + MLA_NOPE, (hh + 1) * MLA_HP
            kn, vh = acc[:, a:m], acc[:, m:b]
            ko_ref[:, a:m] = kn.astype(BF16)
            ko_ref[:, m:b] = krb
            kt_ref[a:m, :] = kn.T.astype(BF16)
            kt_ref[m:b, :] = krt
            vo_ref[:, hh * MLA_V:(hh + 1) * MLA_V] = vh.astype(BF16)
            vt_ref[hh] = vh.T.astype(BF16)

    tk = min(T, ATT_TILE)
    per = tk // tr
    HW = MLA_H * MLA_HP
    return pl.pallas_call(
        body, name=name, grid=(T // tr, MLA_H // hg),
        in_specs=[pl.BlockSpec((tr, MLA_KVL), lambda i, g: (i, 0)), pl.BlockSpec((MLA_KVL, gw), lambda i, g: (0, g)),
                  pl.BlockSpec((tr, 128), lambda i, g: (i, 0))],
        out_specs=[pl.BlockSpec((tr, gw), lambda i, g: (i, g)), pl.BlockSpec((gw, tr), lambda i, g: (g, i)),
                   pl.BlockSpec((tr, hg * MLA_V), lambda i, g: (i, g)),
                   pl.BlockSpec((hg, None, MLA_V, tr), lambda i, g: (g, i // per, 0, i % per))],
        out_shape=[jax.ShapeDtypeStruct((T, HW), BF16), jax.ShapeDtypeStruct((HW, T), BF16), jax.ShapeDtypeStruct((T, MLA_H * MLA_V), BF16),
                   jax.ShapeDtypeStruct((MLA_H, T // tk, MLA_V, tk), BF16)],
        compiler_params=pltpu.CompilerParams(dimension_semantics=("parallel", "parallel")),
    )(ckvn, w_ukv, krot)


ATT_HG = 4


def _mla_prep_bwd(dqt, dkr, cc, sa, sb, name):
    _, nq, _, tq = dqt.shape
    T = nq * tq
    gw = ATT_HG * MLA_HP

    def body(dq_ref, dk_ref, cc_ref, sa_ref, sb_ref, dqp_ref, dkr_ref):
        @pl.when(pl.program_id(1) == 0)
        def _():
            dkr_ref[...] = jnp.zeros_like(dkr_ref)

        cc, sa, sb = cc_ref[...], sa_ref[...], sb_ref[...]
        acc = jnp.zeros((tq, 128), F32)
        for hh in range(ATT_HG):
            a, m, b = hh * MLA_HP, hh * MLA_HP + MLA_NOPE, (hh + 1) * MLA_HP
            dqh = dq_ref[hh].astype(F32).T * SM_SCALE
            dqp_ref[:, a:m] = dqh[:, :MLA_NOPE].astype(BF16)
            dqp_ref[:, m:b] = _rope_bwd(dqh[:, MLA_NOPE:], cc, sa, sb).astype(BF16)
            acc = acc + dk_ref[:, hh * 128:(hh + 1) * 128].astype(F32)
        dkr_ref[...] += acc

    tab = pl.BlockSpec((tq, 128), lambda i, g: (i, 0))
    return pl.pallas_call(
        body, name=name, grid=(nq, MLA_H // ATT_HG),
        in_specs=[pl.BlockSpec((ATT_HG, None, MLA_HP, tq), lambda i, g: (g, i, 0, 0)), pl.BlockSpec((tq, ATT_HG * 128), lambda i, g: (i, g)),
                  tab, tab, tab],
        out_specs=[pl.BlockSpec((tq, gw), lambda i, g: (i, g)), tab],
        out_shape=[jax.ShapeDtypeStruct((T, MLA_H * MLA_HP), BF16), jax.ShapeDtypeStruct((T, 128), F32)],
        compiler_params=pltpu.CompilerParams(dimension_semantics=("parallel", "arbitrary")),
    )(dqt, dkr, cc, sa, sb)


_NT = (((1,), (1,)), ((), ()))


def _as_row(col, n):
    return jnp.broadcast_to(col, (n, 128)).T[0:1, :]


def _attn_fwd(q, k, vt, name):
    T = q.shape[0]
    tq = tk = min(T, ATT_TILE)
    nq = T // tq
    hg = ATT_FWD_HEADS

    def body(q_ref, k_ref, vt_ref, o_ref, lse_ref, m_ref, l_ref, acc_ref):
        i = pl.program_id(1)
        m_ref[...] = jnp.full_like(m_ref, NEG_INF)
        l_ref[...] = jnp.zeros_like(l_ref)
        acc_ref[...] = jnp.zeros_like(acc_ref)

        def step(j, diag):
            off = pl.multiple_of(j * tk, tk)
            sts = [lax.dot_general(k_ref[pl.ds(off, tk), hh * MLA_HP:(hh + 1) * MLA_HP], q_ref[:, hh * MLA_HP:(hh + 1) * MLA_HP], _NT,
                                   preferred_element_type=F32) for hh in range(hg)]
            for hh in range(hg):
                st = sts[hh]
                if diag:
                    st = jnp.where(lax.broadcasted_iota(jnp.int32, (tk, tq), 0) <= lax.broadcasted_iota(jnp.int32, (tk, tq), 1), st, NEG_INF)
                m_prev = m_ref[hh]
                m_new = jnp.maximum(m_prev, jnp.max(st, axis=0, keepdims=True))
                alpha = jnp.exp2(m_prev - m_new)
                pt = jnp.exp2(st - m_new)
                l_ref[hh] = alpha * l_ref[hh] + jnp.sum(pt, axis=0, keepdims=True)
                acc_ref[hh] = alpha * acc_ref[hh] + jnp.dot(vt_ref[hh, j], pt.astype(BF16), preferred_element_type=F32)
                m_ref[hh] = m_new

        def loop_body(j, carry):
            step(j, False)
            return carry

        lax.fori_loop(0, i, loop_body, 0)
        step(i, True)
        for hh in range(hg):
            o_ref[:, hh * MLA_V:(hh + 1) * MLA_V] = (acc_ref[hh] / l_ref[hh]).T.astype(BF16)
            lse_ref[hh] = m_ref[hh] + jnp.log2(l_ref[hh])

    return pl.pallas_call(
        body, name=name, grid=(MLA_H // hg, nq),
        in_specs=[pl.BlockSpec((tq, hg * MLA_HP), lambda h, i: (i, h)), pl.BlockSpec((T, hg * MLA_HP), lambda h, i: (0, h)),
                  pl.BlockSpec((hg, nq, MLA_V, tk), lambda h, i: (h, 0, 0, 0))],
        out_specs=[pl.BlockSpec((tq, hg * MLA_V), lambda h, i: (i, h)), pl.BlockSpec((hg, None, 1, tq), lambda h, i: (h, i, 0, 0))],
        out_shape=[jax.ShapeDtypeStruct((T, MLA_H * MLA_V), BF16), jax.ShapeDtypeStruct((MLA_H, nq, 1, tq), F32)],
        scratch_shapes=[pltpu.VMEM((hg, 1, tq), F32), pltpu.VMEM((hg, 1, tq), F32), pltpu.VMEM((hg, MLA_V, tq), F32)],
        compiler_params=pltpu.CompilerParams(dimension_semantics=("parallel", "arbitrary")),
    )(q, k, vt)


def _attn_delta(do, o, name):
    T = do.shape[0]
    tq = min(T, ATT_TILE)

    def body(do_ref, o_ref, d_ref):
        for hh in range(MLA_H):
            cs = slice(hh * MLA_V, (hh + 1) * MLA_V)
            s = jnp.sum(do_ref[:, cs].astype(F32) * o_ref[:, cs].astype(F32), axis=-1, keepdims=True)
            d_ref[hh] = _as_row(s, tq)

    return pl.pallas_call(
        body, name=name, grid=(T // tq,),
        in_specs=[_row_spec(tq, MLA_H * MLA_V), _row_spec(tq, MLA_H * MLA_V)],
        out_specs=pl.BlockSpec((MLA_H, None, 1, tq), lambda i: (0, i, 0, 0)),
        out_shape=jax.ShapeDtypeStruct((MLA_H, T // tq, 1, tq), F32),
        compiler_params=pltpu.CompilerParams(dimension_semantics=("parallel",)),
    )(do, o)


def _attn_bwd(q, k, kt, v, do, lse, delta, name):
    T = q.shape[0]
    tq = tk = min(T, ATT_TILE)
    nq = nk = T // tq
    tsd = min(tq, ATT_SUB)
    hg = ATT_BWD_HEADS

    def body(q_ref, k_ref, kt_ref, v_ref, do_ref, lse_ref, dl_ref, dqt_ref, dkv_ref, dkr_ref, dq_acc, dk_acc, dv_acc):
        j = pl.program_id(1)

        @pl.when(j == 0)
        def _():
            dq_acc[...] = jnp.zeros_like(dq_acc)

        dk_acc[...] = jnp.zeros_like(dk_acc)
        dv_acc[...] = jnp.zeros_like(dv_acc)

        def step(i, diag):
            off = pl.multiple_of(i * tq, tq)
            ts, nsub = (tsd, tq // tsd) if diag else (tq, 1)
            for u in range(nsub):
                cols = slice(u * ts, (u + 1) * ts)
                nk_u = (u + 1) * ts if diag else tk
                rows = pl.ds(off + u * ts, ts)
                pre = []
                for hh in range(hg):
                    hq, hv = slice(hh * MLA_HP, (hh + 1) * MLA_HP), slice(hh * MLA_V, (hh + 1) * MLA_V)
                    qi, doi = q_ref[rows, hq], do_ref[rows, hv]
                    st = lax.dot_general(k_ref[:nk_u, hq], qi, _NT, preferred_element_type=F32)
                    dpt = lax.dot_general(v_ref[:nk_u, hv], doi, _NT, preferred_element_type=F32)
                    pre.append((qi, doi, st, dpt))
                for hh in range(hg):
                    hq, hv = slice(hh * MLA_HP, (hh + 1) * MLA_HP), slice(hh * MLA_V, (hh + 1) * MLA_V)
                    qi, doi, st, dpt = pre[hh]
                    if diag:
                        qcol = u * ts + lax.broadcasted_iota(jnp.int32, (nk_u, ts), 1)
                        st = jnp.where(lax.broadcasted_iota(jnp.int32, (nk_u, ts), 0) <= qcol, st, NEG_INF)
                    pt = jnp.exp2(st - lse_ref[hh, i][:, cols])
                    dv_acc[:nk_u, hv] += jnp.dot(pt.astype(BF16), doi, preferred_element_type=F32)
                    dsb = (pt * (dpt - dl_ref[hh, i][:, cols])).astype(BF16)
                    dk_acc[:nk_u, hq] += jnp.dot(dsb, qi, preferred_element_type=F32)
                    dq_acc[hh, i, :, cols] += jnp.dot(kt_ref[hq, :nk_u], dsb, preferred_element_type=F32)

        def loop_body(i, carry):
            step(i, False)
            return carry

        step(j, True)
        lax.fori_loop(j + 1, nq, loop_body, 0)
        for hh in range(hg):
            a, m, b = hh * MLA_HP, hh * MLA_HP + MLA_NOPE, (hh + 1) * MLA_HP
            dkv_ref[:, a:m] = (dk_acc[:, a:m] * (1.0 / LOG2E)).astype(BF16)
            dkv_ref[:, m:b] = dv_acc[:, hh * MLA_V:(hh + 1) * MLA_V].astype(BF16)
            dkr_ref[:, hh * 128:(hh + 1) * 128] = (dk_acc[:, m:b] * (1.0 / LOG2E)).astype(BF16)

        @pl.when(j == nk - 1)
        def _():
            dqt_ref[...] = dq_acc[...].astype(BF16)

    stat = pl.BlockSpec((hg, nq, 1, tq), lambda h, j: (h, 0, 0, 0))
    return pl.pallas_call(
        body, name=name, grid=(MLA_H // hg, nk),
        in_specs=[pl.BlockSpec((T, hg * MLA_HP), lambda h, j: (0, h)), pl.BlockSpec((tk, hg * MLA_HP), lambda h, j: (j, h)),
                  pl.BlockSpec((hg * MLA_HP, tk), lambda h, j: (h, j)), pl.BlockSpec((tk, hg * MLA_V), lambda h, j: (j, h)),
                  pl.BlockSpec((T, hg * MLA_V), lambda h, j: (0, h)), stat, stat],
        out_specs=[pl.BlockSpec((hg, nq, MLA_HP, tq), lambda h, j: (h, 0, 0, 0)), pl.BlockSpec((tk, hg * MLA_HP), lambda h, j: (j, h)),
                   pl.BlockSpec((tk, hg * 128), lambda h, j: (j, h))],
        out_shape=[jax.ShapeDtypeStruct((MLA_H, nq, MLA_HP, tq), BF16), jax.ShapeDtypeStruct((T, MLA_H * MLA_HP), BF16),
                   jax.ShapeDtypeStruct((T, MLA_H * 128), BF16)],
        scratch_shapes=[pltpu.VMEM((hg, nq, MLA_HP, tq), F32), pltpu.VMEM((tk, hg * MLA_HP), F32), pltpu.VMEM((tk, hg * MLA_V), F32)],
        compiler_params=pltpu.CompilerParams(dimension_semantics=("parallel", "arbitrary")),
    )(q, k, kt, v, do, lse, delta)


ADA_TN = 512


def _silu(v):
    return v * (1.0 / (1.0 + jnp.exp(-v)))


def _ada_fwd(c_all, ada_w, ada_b_loc, name):
    L, D, Nc = ada_w.shape
    B = c_all.shape[0]

    def body(c_ref, w_ref, b_ref, o_ref):
        ca = _silu(c_ref[...]).astype(BF16)
        o_ref[...] = jnp.dot(ca, w_ref[...].astype(BF16), preferred_element_type=F32) + b_ref[...]

    return pl.pallas_call(
        body, name=name, grid=(L, Nc // ADA_TN),
        in_specs=[pl.BlockSpec((B, D), lambda l, n: (0, 0)), pl.BlockSpec((None, D, ADA_TN), lambda l, n: (l, 0, n)),
                  pl.BlockSpec((None, 1, ADA_TN), lambda l, n: (l, 0, n))],
        out_specs=pl.BlockSpec((None, B, ADA_TN), lambda l, n: (l, 0, n)),
        out_shape=jax.ShapeDtypeStruct((L, B, Nc), F32),
        compiler_params=pltpu.CompilerParams(dimension_semantics=("parallel", "parallel")),
    )(c_all, ada_w, ada_b_loc)


def _ada_bwd_adamw(c_all_t, dmod_loc, w, m, v, name):
    D, B = c_all_t.shape
    L, _, Nc = dmod_loc.shape

    def body(c_ref, d_ref, w_ref, m_ref, v_ref, g_ref, dl_ref, nm_ref, nv_ref):
        ca = _silu(c_ref[...])
        dv = d_ref[...]
        gv = ca[:, 0:1] * dv[0:1, :]
        for b in range(1, B):
            gv = gv + ca[:, b:b + 1] * dv[b:b + 1, :]
        mn = ADAM_B1 * m_ref[...] + (1.0 - ADAM_B1) * gv
        vn = ADAM_B2 * v_ref[...] + (1.0 - ADAM_B2) * (gv * gv)
        g_ref[...] = gv
        nm_ref[...] = mn
        nv_ref[...] = vn
        dl_ref[...] = -ADAM_LR * ((mn / _ADAM_C1) / (jnp.sqrt(vn / _ADAM_C2) + ADAM_EPS) + ADAM_WD * w_ref[...])

    blk = pl.BlockSpec((None, D, ADA_TN), lambda l, n: (l, 0, n))
    return pl.pallas_call(
        body, name=name, grid=(L, Nc // ADA_TN),
        in_specs=[pl.BlockSpec((D, B), lambda l, n: (0, 0)), pl.BlockSpec((None, B, ADA_TN), lambda l, n: (l, 0, n)), blk, blk, blk],
        out_specs=[blk] * 4,
        out_shape=[jax.ShapeDtypeStruct((L, D, Nc), F32)] * 4,
        compiler_params=pltpu.CompilerParams(dimension_semantics=("parallel", "parallel")),
    )(c_all_t, dmod_loc, w, m, v)


def _sum_lead(parts, name, out_dtype=F32):
    R, C = parts[0].shape[1:]
    n_tot = sum(p.shape[0] for p in parts)
    tr = R
    for cand in (512, 256, 128, 64, 32, 16):
        if R % cand == 0 and cand * C * 4 * n_tot <= (8 << 20):
            tr = cand
            break

    def body(*refs):
        o_ref = refs[-1]
        acc = None
        for r in refs[:-1]:
            for s in range(r.shape[0]):
                acc = r[s].astype(F32) if acc is None else acc + r[s].astype(F32)
        o_ref[...] = acc.astype(o_ref.dtype)

    return pl.pallas_call(
        body, name=name, grid=(R // tr,),
        in_specs=[pl.BlockSpec((p.shape[0], tr, C), lambda i: (0, i, 0)) for p in parts],
        out_specs=pl.BlockSpec((tr, C), lambda i: (i, 0)),
        out_shape=jax.ShapeDtypeStruct((R, C), out_dtype),
        compiler_params=pltpu.CompilerParams(dimension_semantics=("parallel",)),
    )(*parts)


_ADAM_C1 = 1.0 - ADAM_B1 ** ADAM_STEP
_ADAM_C2 = 1.0 - ADAM_B2 ** ADAM_STEP


def _adamw(w, g, m, v, name):
    shape = w.shape
    C = shape[-1]
    R = math.prod(shape[:-1]) if len(shape) > 1 else 1
    w2, g2, m2, v2 = (a.reshape(R, C) for a in (w, g, m, v))
    tr = R
    for cand in (1024, 512, 256, 128, 64, 32, 16, 8):
        if R % cand == 0 and cand * C * 4 <= (1 << 20):
            tr = cand
            break

    def body(w_ref, g_ref, m_ref, v_ref, d_ref, nm_ref, nv_ref):
        gv = g_ref[...]
        mn = ADAM_B1 * m_ref[...] + (1.0 - ADAM_B1) * gv
        vn = ADAM_B2 * v_ref[...] + (1.0 - ADAM_B2) * (gv * gv)
        nm_ref[...] = mn
        nv_ref[...] = vn
        m_hat = mn / _ADAM_C1
        v_hat = vn / _ADAM_C2
        d_ref[...] = -ADAM_LR * (m_hat / (jnp.sqrt(v_hat) + ADAM_EPS) + ADAM_WD * w_ref[...])

    spec = pl.BlockSpec((tr, C), lambda i: (i, 0))
    outs = pl.pallas_call(
        body, name=name, grid=(R // tr,),
        in_specs=[spec] * 4, out_specs=[spec] * 3,
        out_shape=[jax.ShapeDtypeStruct((R, C), F32)] * 3,
        compiler_params=pltpu.CompilerParams(dimension_semantics=("parallel",)),
    )(w2, g2, m2, v2)
    return tuple(o.reshape(shape) for o in outs)


def _row_tile(rows, cols, itemsize, budget):
    for cand in (1024, 512, 256, 128, 64, 32, 16):
        if rows % cand == 0 and cand * cols * itemsize <= budget:
            return cand
    return rows


def _sum_sel(sel, stacked, others, name, out_dtype):
    R, C = stacked.shape[1:]
    n_tot = 1 + sum(o.shape[0] for o in others)
    tr = _row_tile(R, C, 4 * n_tot, 8 << 20)

    def body(sel_ref, s_ref, *refs):
        o_ref = refs[-1]
        acc = s_ref[...].astype(F32)
        for r in refs[:-1]:
            for s in range(r.shape[0]):
                acc = acc + r[s].astype(F32)
        o_ref[...] = acc.astype(o_ref.dtype)

    return pl.pallas_call(
        body, name=name,
        grid_spec=pltpu.PrefetchScalarGridSpec(
            num_scalar_prefetch=1, grid=(R // tr,),
            in_specs=[pl.BlockSpec((None, tr, C), lambda i, s: (s[0], i, 0))] + [pl.BlockSpec((o.shape[0], tr, C), lambda i, s: (0, i, 0)) for o in others],
            out_specs=pl.BlockSpec((tr, C), lambda i, s: (i, 0))),
        out_shape=jax.ShapeDtypeStruct((R, C), out_dtype),
        compiler_params=pltpu.CompilerParams(dimension_semantics=("parallel",)),
    )(sel, stacked, *others)


def _adamw_piece(cidx, w2, m2, v2, mine, got, bufs, row0, name):
    hr, C = mine.shape
    tr = _row_tile(math.gcd(hr, row0) if row0 else hr, C, 4, 1 << 20)
    nt = hr // tr

    def body(c_ref, w_ref, m_ref, v_ref, a_ref, b_ref, _g, _d, _nm, _nv, g_ref, d_ref, nm_ref, nv_ref):
        gv = jnp.where(pl.program_id(0) == c_ref[0], a_ref[...], b_ref[...])
        mn = ADAM_B1 * m_ref[...] + (1.0 - ADAM_B1) * gv
        vn = ADAM_B2 * v_ref[...] + (1.0 - ADAM_B2) * (gv * gv)
        g_ref[...] = gv
        nm_ref[...] = mn
        nv_ref[...] = vn
        d_ref[...] = -ADAM_LR * ((mn / _ADAM_C1) / (jnp.sqrt(vn / _ADAM_C2) + ADAM_EPS) + ADAM_WD * w_ref[...])

    rows = pl.BlockSpec((tr, C), lambda hf, t, c: (row0 // tr + hf * nt + t, 0))
    mine_spec = pl.BlockSpec((tr, C), lambda hf, t, c: (jnp.where(hf == c[0], t, 0), 0))
    got_spec = pl.BlockSpec((tr, C), lambda hf, t, c: (jnp.where(hf == c[0], 0, t), 0))
    return pl.pallas_call(
        body, name=name,
        grid_spec=pltpu.PrefetchScalarGridSpec(num_scalar_prefetch=1, grid=(2, nt), in_specs=[rows] * 3 + [mine_spec, got_spec] + [_ANY_SPEC] * 4,
                                               out_specs=[rows] * 4),
        out_shape=[jax.ShapeDtypeStruct(w2.shape, F32)] * 4,
        input_output_aliases={6 + n: n for n in range(4)},
        compiler_params=pltpu.CompilerParams(dimension_semantics=("parallel", "parallel")),
    )(cidx, w2, m2, v2, mine, got, *bufs)


_VMEM_SPEC = pl.BlockSpec(memory_space=pltpu.VMEM)
_HBM_SPEC = pl.BlockSpec(memory_space=pltpu.HBM)


def _flip(v, bit):
    return (1 - v) if bit else v


def _allgather8(v, name):
    def body(v_ref, out_ref, send_sems, recv_sems, local_sem):
        x, y, c = _idx()
        me = 4 * x + 2 * y + c
        mine = pltpu.make_async_copy(v_ref, out_ref.at[me], local_sem)
        mine.start()
        sends = []
        for k in range(1, N_DEV):
            peer = (_flip(x, k & 4), _flip(y, k & 2), _flip(c, k & 1))
            cp = pltpu.make_async_remote_copy(src_ref=v_ref, dst_ref=out_ref.at[me], send_sem=send_sems.at[k - 1], recv_sem=recv_sems.at[k - 1],
                                              device_id=peer, device_id_type=MESH)
            cp.start()
            sends.append(cp)
        for k in range(1, N_DEV):
            px, py, pc = _flip(x, k & 4), _flip(y, k & 2), _flip(c, k & 1)
            src = 4 * px + 2 * py + pc
            pltpu.make_async_remote_copy(src_ref=v_ref, dst_ref=out_ref.at[src], send_sem=send_sems.at[k - 1], recv_sem=recv_sems.at[k - 1],
                                         device_id=(px, py, pc), device_id_type=MESH).wait_recv()
        for cp in sends:
            cp.wait_send()
        mine.wait()

    return pl.pallas_call(
        body, name=name,
        out_shape=jax.ShapeDtypeStruct((N_DEV, *v.shape), v.dtype),
        in_specs=[_VMEM_SPEC], out_specs=_VMEM_SPEC,
        scratch_shapes=[pltpu.SemaphoreType.DMA((N_DEV - 1,)), pltpu.SemaphoreType.DMA((N_DEV - 1,)), pltpu.SemaphoreType.DMA],
    )(v)


def _mod_exchange(modp, name):
    _, L, Nc = modp.shape

    def body(p_ref, out_ref, send_sems, recv_sems, local_sem):
        x, y, c = _idx()
        me, chip = 4 * x + 2 * y + c, 2 * x + y
        mine = pltpu.make_async_copy(p_ref.at[me], out_ref.at[chip], local_sem)
        mine.start()
        sends = []
        for k in range(1, N_CHIPS):
            px, py = _flip(x, k & 2), _flip(y, k & 1)
            cp = pltpu.make_async_remote_copy(src_ref=p_ref.at[4 * px + 2 * py + c], dst_ref=out_ref.at[chip],
                                              send_sem=send_sems.at[k - 1], recv_sem=recv_sems.at[k - 1], device_id=(px, py, c), device_id_type=MESH)
            cp.start()
            sends.append(cp)
        for k in range(1, N_CHIPS):
            px, py = _flip(x, k & 2), _flip(y, k & 1)
            pltpu.make_async_remote_copy(src_ref=p_ref.at[me], dst_ref=out_ref.at[2 * px + py], send_sem=send_sems.at[k - 1],
                                         recv_sem=recv_sems.at[k - 1], device_id=(px, py, c), device_id_type=MESH).wait_recv()
        for cp in sends:
            cp.wait_send()
        mine.wait()

    return pl.pallas_call(
        body, name=name,
        out_shape=jax.ShapeDtypeStruct((N_CHIPS, L, Nc), modp.dtype),
        in_specs=[_VMEM_SPEC], out_specs=_VMEM_SPEC,
        scratch_shapes=[pltpu.SemaphoreType.DMA((N_CHIPS - 1,)), pltpu.SemaphoreType.DMA((N_CHIPS - 1,)), pltpu.SemaphoreType.DMA],
    )(modp)


_SEM_SPEC = pl.BlockSpec(memory_space=pltpu.SEMAPHORE)
_ANY_SPEC = pl.BlockSpec(memory_space=pl.ANY)
_EFFECT = pltpu.SideEffectType.DATAFLOW_SIDE_EFFECTING


def _hbm(a):
    return pltpu.with_memory_space_constraint(a, pltpu.HBM)


def _xchip_copies(mode, srcs, lands, send_sems, recv_sems, waiting):
    x, y, c = _idx()
    chip = 2 * x + y
    out = []
    for a in range(len(srcs)):
        for k in range(1, _n_peers(mode) + 1):
            if mode == "all8":
                px, py, pc = _flip(x, k & 4), _flip(y, k & 2), _flip(c, k & 1)
                src, dst, mine = srcs[a], lands[a].at[4 * x + 2 * y + c], lands[a].at[4 * px + 2 * py + pc]
            elif mode == "scatter8":
                px, py, pc = _flip(x, k & 4), _flip(y, k & 2), _flip(c, k & 1)
                src, dst, mine = srcs[a].at[pc, 2 * px + py], lands[a].at[k - 1], lands[a].at[k - 1]
            else:
                px, py, pc = _flip(x, k & 2), _flip(y, k & 1), c
                peer = 2 * px + py
                if mode == "gather":
                    src, dst, mine = srcs[a].at[c], lands[a].at[chip, c], lands[a].at[peer, c]
                else:
                    src, dst, mine = srcs[a].at[peer], lands[a].at[k - 1], lands[a].at[k - 1]
            q = a * _n_peers(mode) + k - 1
            out.append(pltpu.make_async_remote_copy(src_ref=src, dst_ref=mine if waiting else dst, send_sem=send_sems[q], recv_sem=recv_sems[q],
                                                    device_id=(px, py, pc), device_id_type=MESH))
    return out


def _n_peers(mode):
    return N_DEV - 1 if mode in ("all8", "scatter8") else N_CHIPS - 1


def _xchip_start(mode, srcs, land_shapes, dep, name):
    n = len(srcs)
    ns = n * _n_peers(mode)

    def body(*refs):
        src_refs, land_refs = refs[:n], refs[n:2 * n]
        outs = refs[2 * n + 1:]
        for cp in _xchip_copies(mode, src_refs, land_refs, outs[:ns], outs[ns:2 * ns], waiting=False):
            cp.start()
        outs[-1][...] = jnp.zeros_like(outs[-1])

    lands = [_hbm(lax.empty(s.shape, s.dtype)) for s in land_shapes]
    outs = pl.pallas_call(
        body, name=name,
        out_shape=(*[pltpu.SemaphoreType.DMA(())] * (2 * ns), *[pltpu.HBM(s.shape, s.dtype) for s in srcs],
                   *[pltpu.HBM(s.shape, s.dtype) for s in land_shapes], jax.ShapeDtypeStruct((8, 128), F32)),
        in_specs=[_HBM_SPEC] * (2 * n) + [_ANY_SPEC],
        out_specs=(*[_SEM_SPEC] * (2 * ns), *[_HBM_SPEC] * (2 * n), _VMEM_SPEC),
        input_output_aliases={i: 2 * ns + i for i in range(2 * n)},
        compiler_params=pltpu.CompilerParams(has_side_effects=_EFFECT),
    )(*[_hbm(s) for s in srcs], *lands, dep)
    return list(outs[:ns]), list(outs[ns:2 * ns]), list(outs[2 * ns:2 * ns + n]), list(outs[2 * ns + n:2 * ns + 2 * n]), outs[-1]


def _xchip_wait(mode, send_sems, recv_sems, srcs, lands, after, name):
    n = len(srcs)
    ns = n * _n_peers(mode)

    def body(*refs):
        src_refs, land_refs = refs[:n], refs[n:2 * n]
        sems = refs[2 * n:2 * n + 2 * ns]
        for cp in _xchip_copies(mode, src_refs, land_refs, sems[:ns], sems[ns:], waiting=True):
            cp.wait_send()
            cp.wait_recv()

    outs = pl.pallas_call(
        body, name=name,
        out_shape=(*[pltpu.HBM(s.shape, s.dtype) for s in srcs], *[pltpu.HBM(s.shape, s.dtype) for s in lands]),
        in_specs=[_HBM_SPEC] * (2 * n) + [_SEM_SPEC] * (2 * ns) + [_ANY_SPEC] * len(after),
        out_specs=tuple([_HBM_SPEC] * (2 * n)),
        input_output_aliases={i: i for i in range(2 * n)},
        compiler_params=pltpu.CompilerParams(has_side_effects=_EFFECT),
    )(*srcs, *lands, *send_sems, *recv_sems, *after)
    return list(outs[:n]), list(outs[n:])


def _sibling_fwd(lands, name):
    n = len(lands)

    def body(*refs):
        outs = refs[n:2 * n]
        send_sems, recv_sems = refs[2 * n:]
        x, y, c = _idx()
        sib = (x, y, 1 - c)
        sends = []
        for a in range(n):
            for k in range(1, N_CHIPS):
                src = 2 * _flip(x, k & 2) + _flip(y, k & 1)
                cp = pltpu.make_async_remote_copy(src_ref=outs[a].at[src, c], dst_ref=outs[a].at[src, c], send_sem=send_sems.at[a, k - 1],
                                                  recv_sem=recv_sems.at[a, k - 1], device_id=sib, device_id_type=MESH)
                cp.start()
                sends.append(cp)
        for a in range(n):
            for k in range(1, N_CHIPS):
                src = 2 * _flip(x, k & 2) + _flip(y, k & 1)
                pltpu.make_async_remote_copy(src_ref=outs[a].at[src, c], dst_ref=outs[a].at[src, 1 - c], send_sem=send_sems.at[a, k - 1],
                                             recv_sem=recv_sems.at[a, k - 1], device_id=sib, device_id_type=MESH).wait_recv()
        for cp in sends:
            cp.wait_send()

    return pl.pallas_call(
        body, name=name,
        out_shape=[jax.ShapeDtypeStruct(s.shape, s.dtype) for s in lands],
        in_specs=[_HBM_SPEC] * n, out_specs=[_HBM_SPEC] * n,
        input_output_aliases={i: i for i in range(n)},
        scratch_shapes=[pltpu.SemaphoreType.DMA((n, N_CHIPS - 1)), pltpu.SemaphoreType.DMA((n, N_CHIPS - 1))],
    )(*lands)


def _sibling_send(halves, name):
    n = len(halves)

    def body(*refs):
        ins, outs = refs[:n], refs[n:2 * n]
        send_sems, recv_sems = refs[2 * n:]
        x, y, c = _idx()
        cps = []
        for a in range(n):
            cp = pltpu.make_async_remote_copy(src_ref=ins[a], dst_ref=outs[a], send_sem=send_sems.at[a], recv_sem=recv_sems.at[a],
                                              device_id=(x, y, 1 - c), device_id_type=MESH)
            cp.start()
            cps.append(cp)
        for cp in cps:
            cp.wait()

    return pl.pallas_call(
        body, name=name,
        out_shape=[jax.ShapeDtypeStruct(h.shape, h.dtype) for h in halves],
        in_specs=[_HBM_SPEC] * n, out_specs=[_HBM_SPEC] * n,
        scratch_shapes=[pltpu.SemaphoreType.DMA((n,)), pltpu.SemaphoreType.DMA((n,))],
    )(*halves)


def _col_full(g):
    k, n = g.shape[1], g.shape[2]
    return g.transpose(1, 0, 2).reshape(k, N_CHIPS * n)


def _col_blocks(w):
    k, n = w.shape
    return w.reshape(k, N_CHIPS, n // N_CHIPS).transpose(1, 0, 2)


def _row_blocks(w):
    k, n = w.shape
    return w.reshape(N_CHIPS, k // N_CHIPS, n)


_UQ_HEAD = MLA_NOPE + MLA_ROPE

_LAT = MLA_QL + MLA_KVL + MLA_ROPE
_POOL_R = len(POOL_WINDOWS) * (POOL_GD // N_CHIPS)

_PIECE_KINDS = {
    "mlp_w1": (D_MODEL, D_MODEL, lambda g: g, _col_blocks),
    "mlp_w2": (D_MODEL, D_MODEL, lambda g: g.reshape(4 * D_MODEL, D_MODEL), _row_blocks),
    "pool_w": (_POOL_R, POOL_GD,
               lambda g: g.reshape(N_CHIPS, len(POOL_WINDOWS), POOL_GD // N_CHIPS, POOL_GD).transpose(1, 0, 2, 3).reshape(len(POOL_WINDOWS), POOL_GD, POOL_GD),
               lambda w: w.reshape(len(POOL_WINDOWS), N_CHIPS, POOL_GD // N_CHIPS, POOL_GD).transpose(1, 0, 2, 3).reshape(N_CHIPS, _POOL_R, POOL_GD)),
    "sgu_w_in": (D_MODEL, 2 * SGU_W // N_CHIPS, _col_full, _col_blocks),
    "sgu_w_out": (SGU_W // N_CHIPS, D_MODEL, lambda g: g.reshape(SGU_W, D_MODEL), _row_blocks),
    "mla_w_dq_dkv": (D_MODEL // N_CHIPS, _LAT, lambda g: jnp.pad(g.reshape(D_MODEL, _LAT), ((0, 0), (0, MLA_LATP - _LAT))),
                     lambda w: _row_blocks(w[:, :_LAT])),
    "mla_w_uq": (MLA_QL, MLA_H * _UQ_HEAD // N_CHIPS,
                 lambda g: jnp.pad(_col_full(g).reshape(MLA_QL, MLA_H, _UQ_HEAD), ((0, 0), (0, 0), (0, MLA_HP - _UQ_HEAD))).reshape(MLA_QL, MLA_H * MLA_HP),
                 lambda w: _col_blocks(w.reshape(MLA_QL, MLA_H, MLA_HP)[:, :, :_UQ_HEAD].reshape(MLA_QL, MLA_H * _UQ_HEAD))),
    "mla_w_ukv": (MLA_KVL, MLA_H * (MLA_NOPE + MLA_V) // N_CHIPS, _col_full, _col_blocks),
    "mla_w_o": (MLA_H * MLA_V // N_CHIPS, D_MODEL, lambda g: g.reshape(MLA_H * MLA_V, D_MODEL), _row_blocks),
}
_MIXER_KINDS = (("pool_w",), ("sgu_w_in", "sgu_w_out"), ("mla_w_dq_dkv", "mla_w_uq", "mla_w_ukv", "mla_w_o"))


def _layer_pieces(i):
    return [(k, i // N_MIXERS) for k in _MIXER_KINDS[i % N_MIXERS]] + [("mlp_w1", i), ("mlp_w2", i)]


def _rope_tables(positions):
    inv_freq = ROPE_THETA ** (-jnp.arange(0, MLA_ROPE, 2, dtype=F32) / MLA_ROPE)
    ang = positions.astype(F32)[:, None] * inv_freq
    cos, sin = jnp.cos(ang), jnp.sin(ang)
    z32, z64 = jnp.zeros_like(cos), jnp.zeros((positions.shape[0], 64), F32)
    return (jnp.concatenate([cos, cos, z64], axis=1), jnp.concatenate([-sin, z32, z64], axis=1), jnp.concatenate([z32, sin, z64], axis=1))


def _local_step(x, positions, target, mod, S, weights_of, grads_of):
    D = D_MODEL
    cc, sa, sb = _rope_tables(positions)
    mods = [[mod[i:i + 1, n * D:(n + 1) * D] for n in range(6)] for i in range(DEPTH)]
    h_dtype = lambda i: F32 if i % N_MIXERS == 0 else BF16
    saved = []
    h = _norm_mod_fwd(x, S["norm_mix_g"][0:1], mods[0][1], mods[0][0], h_dtype(0), "l0_norm1")
    for i in range(DEPTH):
        sh1, sc1, g1, sh2, sc2, g2 = mods[i]
        kind, j = i % N_MIXERS, i // N_MIXERS
        gmlp = S["norm_mlp_g"][i:i + 1]
        W = weights_of(i, "mix", x)
        st = {"x": x}
        norm2 = ((gmlp, "n"), (sc2, "n"), (sh2, "n"))
        if kind == 0:
            x2, pooled, ypre, h2 = _pool_fwd(h, W["pool_w"], S["pool_scale"][j:j + 1], x, g1, gmlp, sc2, sh2, f"l{i}_pool")
            st.update(pooled=pooled, y=ypre)
        elif kind == 1:
            zz = _mm(h, W["sgu_w_in"], out_dtypes=(F32,), name=f"l{i}_sgu_in")
            bs_t = S["sgu_b_s"].T
            gated = _sgu_gate_fwd(zz, S["sgu_ln_g"], S["sgu_ln_b"], S["sgu_w_s"], bs_t, f"l{i}_sgu_gate")
            x2, y, h2 = _mm(gated, W["sgu_w_out"], epi=_epi_residual_norm, extras=((x, "mn"), (g1, "n"), *norm2), out_dtypes=(F32, BF16, BF16),
                            tn=D, name=f"l{i}_sgu_out")
            st.update(h=h, zz=zz, gated=gated, y=y, bs_t=bs_t)
        else:
            lat = _mm(h, W["mla_w_dq_dkv"], out_dtypes=(F32,), name=f"l{i}_mla_lat")
            cqn, ckvn, krot = _mla_lat_fwd(lat, S["mla_q_norm_g"], S["mla_kv_norm_g"], cc, sa, sb, f"l{i}_mla_latn")
            q = _mm(cqn, W["mla_w_uq"], epi=_epi_q_rope, extras=((cc, "m"), (sa, "m"), (sb, "m")), name=f"l{i}_mla_uq")
            k, kt, v, vt = _mla_ukv(ckvn, W["mla_w_ukv"], krot, f"l{i}_mla_ukv")
            o, lse = _attn_fwd(q, k, vt, f"l{i}_attn")
            x2, y, h2 = _mm(o, W["mla_w_o"], epi=_epi_residual_norm, extras=((x, "mn"), (g1, "n"), *norm2), out_dtypes=(F32, BF16, BF16),
                            tn=D, name=f"l{i}_mla_o")
            st.update(h=h, lat=lat, cqn=cqn, ckvn=ckvn, q=q, k=k, kt=kt, v=v, o=o, lse=lse, y=y)
        W = {**W, **weights_of(i, "mlp", x2)}
        z, r2 = _mm(h2, W["mlp_w1"], epi=_epi_sq_relu, out_dtypes=(BF16, BF16), epi_cols=MM_EPI_COLS, tm=MM_TM_WIDE, name=f"l{i}_mlp1")
        W = {**W, **weights_of(i, "mlp2", z)}
        if i + 1 < DEPTH:
            norm1 = ((S["norm_mix_g"][i + 1:i + 2], "n"), (mods[i + 1][1], "n"), (mods[i + 1][0], "n"))
            x3, o2, h = _mm(z, W["mlp_w2"], epi=_epi_residual_norm, extras=((x2, "mn"), (g2, "n"), *norm1), out_dtypes=(F32, BF16, h_dtype(i + 1)),
                            tn=D, name=f"l{i}_mlp2")
        else:
            x3, o2 = _mm(z, W["mlp_w2"], epi=_epi_residual, extras=((x2, "mn"), (g2, "n")), out_dtypes=(F32, BF16), name=f"l{i}_mlp2")
        st.update(x2=x2, h2=h2, z=z, r2=r2, o2=o2, W=W)
        saved.append(st)
        x = x3

    loss, dx, dfinal_g, do2, dg2 = _loss_head(x, target, S["final_g"], saved[-1]["o2"], mods[-1][5], "loss_head")

    gS = {"final_g": dfinal_g, "norm_mix_g": [None] * DEPTH, "norm_mlp_g": [None] * DEPTH, "pool_scale": [None] * 2}
    dmod = [None] * DEPTH
    started = None
    for i in reversed(range(DEPTH)):
        st = saved[i]
        W, gW = st["W"], {}
        sh1, sc1, g1, sh2, sc2, g2 = mods[i]
        kind, j = i % N_MIXERS, i // N_MIXERS
        gmix, gmlp = S["norm_mix_g"][i:i + 1], S["norm_mlp_g"][i:i + 1]
        da = _mm(do2, W["mlp_w2"], tb=True, epi=lambda acc, rt: (acc * rt.astype(F32),), extras=((st["r2"], "mn"),), after=started, epi_cols=MM_EPI_COLS,
                 tm=MM_TM_WIDE, name=f"l{i}_b_dz")
        gW["mlp_w2"] = _mm(st["z"], do2, ta=True, chip_blocks="row", name=f"l{i}_b_dw2")
        dh2 = _mm(da, W["mlp_w1"], tb=True, name=f"l{i}_b_dh2")
        gW["mlp_w1"] = _mm(st["h2"], da, ta=True, chip_blocks="col", name=f"l{i}_b_dw1")
        dx2, dgmlp, dsc2, dsh2, dy, q1 = _norm_mod_bwd(st["x2"], dh2, dx, gmlp, sc2, f"l{i}_b_norm2", res=(st["y"], g1))
        gS["norm_mlp_g"][i] = dgmlp
        if kind == 0:
            dh, dpw, dpsc, dg1 = _pool_bwd(dy, st["pooled"], W["pool_w"], S["pool_scale"][j:j + 1], g1, q1, f"l{i}_b_pool")
            gW["pool_w"] = dpw.astype(BF16)
            gS["pool_scale"][j] = dpsc
        elif kind == 1:
            dg1 = q1
            dgated = _mm(dy, W["sgu_w_out"], tb=True, name=f"l{i}_b_dgated")
            gW["sgu_w_out"] = _mm(st["gated"], dy, ta=True, name=f"l{i}_b_dwout")
            dzz, dws, dbs, dlg, dlb = _sgu_gate_bwd(st["zz"], dgated, S["sgu_ln_g"], S["sgu_ln_b"], S["sgu_w_s"], st["bs_t"], f"l{i}_b_sgu_gate")
            gS.update(sgu_w_s=dws, sgu_b_s=dbs[:, :, 0], sgu_ln_g=dlg, sgu_ln_b=dlb)
            dh = _mm(dzz, W["sgu_w_in"], tb=True, name=f"l{i}_b_dh_sgu")
            gW["sgu_w_in"] = _mm(st["h"], dzz, ta=True, name=f"l{i}_b_dwin")
        else:
            dg1 = q1
            do = _mm(dy, W["mla_w_o"], tb=True, name=f"l{i}_b_do")
            gW["mla_w_o"] = _mm(st["o"], dy, ta=True, name=f"l{i}_b_dwo")
            delta = _attn_delta(do, st["o"], f"l{i}_b_delta")
            dqt, dkv, dkr = _attn_bwd(st["q"], st["k"], st["kt"], st["v"], do, st["lse"], delta, f"l{i}_b_attn")
            dqpad, dkrot = _mla_prep_bwd(dqt, dkr, cc, sa, sb, f"l{i}_b_mla_prep")
            dcqn = _mm(dqpad, W["mla_w_uq"], tb=True, out_dtypes=(F32,), name=f"l{i}_b_dcq")
            gW["mla_w_uq"] = _mm(st["cqn"], dqpad, ta=True, name=f"l{i}_b_dwuq")
            dckvn = _mm(dkv, W["mla_w_ukv"], tb=True, out_dtypes=(F32,), name=f"l{i}_b_dckv")
            gW["mla_w_ukv"] = _mm(st["ckvn"], dkv, ta=True, name=f"l{i}_b_dwukv")
            dlat, dqg, dkvg = _mla_lat_bwd(st["lat"], dcqn, dckvn, dkrot, S["mla_q_norm_g"], S["mla_kv_norm_g"], cc, sa, sb, f"l{i}_b_mla_latn")
            gS.update(mla_q_norm_g=dqg, mla_kv_norm_g=dkvg)
            dh = _mm(dlat, W["mla_w_dq_dkv"], tb=True, name=f"l{i}_b_dh_mla")
            gW["mla_w_dq_dkv"] = _mm(st["h"], dlat, ta=True, name=f"l{i}_b_dwdq")
        if i > 0:
            dx, dgmix, dsc1, dsh1, do2_prev, dg2_prev = _norm_mod_bwd(st["x"], dh, dx2, gmix, sc1, f"l{i}_b_norm1", res=(saved[i - 1]["o2"], mods[i - 1][5]))
        else:
            dx, dgmix, dsc1, dsh1 = _norm_mod_bwd(st["x"], dh, dx2, gmix, sc1, f"l{i}_b_norm1")
        gS["norm_mix_g"][i] = dgmix
        dmod[i] = jnp.concatenate([dsh1, dsc1, dg1, dsh2, dsc2, dg2], axis=1)
        started = grads_of(i, gW, dx)
        if i > 0:
            do2, dg2 = do2_prev, dg2_prev

    for n in ("norm_mix_g", "norm_mlp_g", "pool_scale"):
        gS[n] = jnp.concatenate(gS[n], axis=0)
    return loss, dx, gS, jnp.concatenate(dmod, axis=0)


_SMALL = {
    "norm_mix_g": (DEPTH, D_MODEL), "norm_mlp_g": (DEPTH, D_MODEL), "sgu_ln_g": (1, SGU_W), "sgu_ln_b": (1, SGU_W),
    "sgu_w_s": (SGU_H, SGU_CHUNK, SGU_CHUNK), "sgu_b_s": (SGU_H, SGU_CHUNK), "mla_kv_norm_g": (1, MLA_KVL), "final_g": (1, D_MODEL),
    "pool_scale": (2, D_MODEL), "mla_q_norm_g": (1, MLA_QL), "loss": (1, 128), "dmod": (DEPTH, 6 * D_MODEL),
}
_PACK_W = 1024


def _pack(vals):
    flat = jnp.concatenate([v.reshape(-1) for v in vals])
    rows = -(-flat.shape[0] // (8 * _PACK_W)) * 8
    return jnp.pad(flat, (0, rows * _PACK_W - flat.shape[0])).reshape(rows, _PACK_W)


def _unpack(buf, shapes):
    flat, out, off = buf.reshape(-1), [], 0
    for s in shapes:
        n = math.prod(s)
        out.append(flat[off:off + n].reshape(s))
        off += n
    return out


def kernel(x, c, positions, ada_w, ada_b, norm_mix_g, norm_mlp_g, pool_w, pool_scale, sgu_w_in, sgu_ln_g, sgu_ln_b, sgu_w_s, sgu_b_s, sgu_w_out, mla_w_dq_dkv, mla_q_norm_g, mla_kv_norm_g, mla_w_uq, mla_w_ukv, mla_w_o, mlp_w1, mlp_w2, final_g, loss_target, m_ada_w, m_ada_b, m_norm_mix_g, m_norm_mlp_g, m_pool_w, m_pool_scale, m_sgu_w_in, m_sgu_ln_g, m_sgu_ln_b, m_sgu_w_s, m_sgu_b_s, m_sgu_w_out, m_mla_w_dq_dkv, m_mla_q_norm_g, m_mla_kv_norm_g, m_mla_w_uq, m_mla_w_ukv, m_mla_w_o, m_mlp_w1, m_mlp_w2, m_final_g, v_ada_w, v_ada_b, v_norm_mix_g, v_norm_mlp_g, v_pool_w, v_pool_scale, v_sgu_w_in, v_sgu_ln_g, v_sgu_ln_b, v_sgu_w_s, v_sgu_b_s, v_sgu_w_out, v_mla_w_dq_dkv, v_mla_q_norm_g, v_mla_kv_norm_g, v_mla_w_uq, v_mla_w_ukv, v_mla_w_o, v_mlp_w1, v_mlp_w2, v_final_g):
    P = dict(ada_w=ada_w, ada_b=ada_b, norm_mix_g=norm_mix_g, norm_mlp_g=norm_mlp_g, pool_w=pool_w, pool_scale=pool_scale, sgu_w_in=sgu_w_in,
             sgu_ln_g=sgu_ln_g, sgu_ln_b=sgu_ln_b, sgu_w_s=sgu_w_s, sgu_b_s=sgu_b_s, sgu_w_out=sgu_w_out, mla_w_dq_dkv=mla_w_dq_dkv,
             mla_q_norm_g=mla_q_norm_g, mla_kv_norm_g=mla_kv_norm_g, mla_w_uq=mla_w_uq, mla_w_ukv=mla_w_ukv, mla_w_o=mla_w_o, mlp_w1=mlp_w1,
             mlp_w2=mlp_w2, final_g=final_g)
    M = dict(ada_w=m_ada_w, ada_b=m_ada_b, norm_mix_g=m_norm_mix_g, norm_mlp_g=m_norm_mlp_g, pool_w=m_pool_w, pool_scale=m_pool_scale,
             sgu_w_in=m_sgu_w_in, sgu_ln_g=m_sgu_ln_g, sgu_ln_b=m_sgu_ln_b, sgu_w_s=m_sgu_w_s, sgu_b_s=m_sgu_b_s, sgu_w_out=m_sgu_w_out,
             mla_w_dq_dkv=m_mla_w_dq_dkv, mla_q_norm_g=m_mla_q_norm_g, mla_kv_norm_g=m_mla_kv_norm_g, mla_w_uq=m_mla_w_uq, mla_w_ukv=m_mla_w_ukv,
             mla_w_o=m_mla_w_o, mlp_w1=m_mlp_w1, mlp_w2=m_mlp_w2, final_g=m_final_g)
    V = dict(ada_w=v_ada_w, ada_b=v_ada_b, norm_mix_g=v_norm_mix_g, norm_mlp_g=v_norm_mlp_g, pool_w=v_pool_w, pool_scale=v_pool_scale,
             sgu_w_in=v_sgu_w_in, sgu_ln_g=v_sgu_ln_g, sgu_ln_b=v_sgu_ln_b, sgu_w_s=v_sgu_w_s, sgu_b_s=v_sgu_b_s, sgu_w_out=v_sgu_w_out,
             mla_w_dq_dkv=v_mla_w_dq_dkv, mla_q_norm_g=v_mla_q_norm_g, mla_kv_norm_g=v_mla_kv_norm_g, mla_w_uq=v_mla_w_uq, mla_w_ukv=v_mla_w_ukv,
             mla_w_o=v_mla_w_o, mlp_w1=v_mlp_w1, mlp_w2=v_mlp_w2, final_g=v_final_g)
    order = list(P)
    xi, yi, ci = _idx()
    chip = 2 * xi + yi
    D = D_MODEL
    n_ada = ada_w.shape[2]

    pre = _allgather8(_pack([c, pool_scale, mla_q_norm_g]), "ag_small")
    flat = pre.reshape(N_DEV, -1)
    c_all = flat[:, :D]
    ps_all = flat[0::2, D:D + 2 * (D // N_CHIPS)].reshape(N_CHIPS, 2, D // N_CHIPS).transpose(1, 0, 2).reshape(2, D)
    q0 = D + 2 * (D // N_CHIPS)
    qg_all = flat[0::2, q0:q0 + MLA_QL // N_CHIPS].reshape(1, MLA_QL)

    ada_b_loc = lax.dynamic_slice_in_dim(ada_b, chip * n_ada, n_ada, axis=1)[:, None, :]
    modp = _ada_fwd(c_all, ada_w, ada_b_loc, "ada_fwd")
    mod = _mod_exchange(modp.transpose(1, 0, 2), "mod_exchange").transpose(1, 0, 2).reshape(DEPTH, 6 * D)

    S = dict(norm_mix_g=norm_mix_g, norm_mlp_g=norm_mlp_g, pool_scale=ps_all, sgu_ln_g=sgu_ln_g, sgu_ln_b=sgu_ln_b, sgu_w_s=sgu_w_s[0],
             sgu_b_s=sgu_b_s[0], mla_q_norm_g=qg_all, mla_kv_norm_g=mla_kv_norm_g, final_g=final_g[None, :])
    cidx, ownidx = jnp.reshape(ci, (1,)).astype(jnp.int32), jnp.reshape(N_CHIPS * ci + chip, (1,)).astype(jnp.int32)
    view2d = lambda a: a.reshape(-1, a.shape[-1])

    def piece_rows(kind, blk):
        r = _PIECE_KINDS[kind][0]
        return blk * r, r

    groups = [_layer_pieces(0)[:-2], _layer_pieces(0)[-2:-1], _layer_pieces(0)[-1:], _layer_pieces(1)[:-2], _layer_pieces(1)[-2:],
              _layer_pieces(2), _layer_pieces(3)]
    start_after = {1: (3, 4), 3: (5,), 5: (6,)}
    gathers = {}

    def gather_start(g, dep):
        srcs, shapes = [], []
        for kind, blk in groups[g]:
            r0, r = piece_rows(kind, blk)
            cdim = _PIECE_KINDS[kind][1]
            srcs.append(view2d(P[kind])[r0:r0 + r].astype(BF16).reshape(2, r // 2, cdim))
            shapes.append(jax.ShapeDtypeStruct((N_CHIPS, 2, r // 2, cdim), BF16))
        gathers[g] = _xchip_start("gather", srcs, shapes, dep, f"ag_start_g{g}")

    def gather_finish(g, after):
        ssem, rsem, srcs, lands, _ = gathers.pop(g)
        deps = [after]
        for nxt in start_after.get(g, ()):
            gather_start(nxt, deps[-1])
            deps.append(gathers[nxt][-1])
        srcs, lands = _xchip_wait("gather", ssem, rsem, srcs, lands, deps, f"ag_wait_g{g}")
        lands = _sibling_fwd(lands, f"ag_sibling_g{g}")
        W = {}
        for (kind, _), s, land in zip(groups[g], srcs, lands, strict=True):
            r, cdim, to_full, _ = _PIECE_KINDS[kind]
            W[kind] = to_full(lax.dynamic_update_index_in_dim(land, s, chip, 0).reshape(N_CHIPS, r, cdim))
        return W

    def weights_of(i, part, x_i):
        g = {(0, "mix"): 0, (0, "mlp"): 1, (0, "mlp2"): 2, (1, "mix"): 3, (1, "mlp"): 4, (2, "mix"): 5, (3, "mix"): 6}.get((i, part))
        return {} if g is None else gather_finish(g, x_i)

    scatters = {}
    bufs = {n: tuple(lax.empty(view2d(P[n]).shape, F32) for _ in range(4)) for n in _PIECE_KINDS}

    def scatter_start(i, gW, dep):
        pcs = _layer_pieces(i)
        blocked = []
        for kind, _ in pcs:
            r, cdim, _, to_blocks = _PIECE_KINDS[kind]
            g = gW[kind]
            blocked.append(g if g.ndim == 4 else to_blocks(g).reshape(N_CHIPS, 2, r // 2, cdim).transpose(1, 0, 2, 3))
        shapes = [jax.ShapeDtypeStruct((N_DEV - 1, *b.shape[2:]), BF16) for b in blocked]
        scatters[i] = (pcs, *_xchip_start("scatter8", blocked, shapes, dep, f"rs_start_l{i}"))
        return scatters[i][-1]

    def scatter_finish(i, after):
        pcs, ssem, rsem, blocked, lands, _ = scatters.pop(i)
        blocked, lands = _xchip_wait("scatter8", ssem, rsem, blocked, lands, after, f"rs_wait_l{i}")
        halves = [_sum_sel(ownidx, b.reshape(2 * N_CHIPS, *b.shape[2:]), [l], f"rs_sum_l{i}_{kind}", F32)
                  for (kind, _), b, l in zip(pcs, blocked, lands, strict=True)]
        got = _sibling_send(halves, f"rs_merge_l{i}")
        for (kind, blk), mine, other in zip(pcs, halves, got, strict=True):
            r0, _ = piece_rows(kind, blk)
            bufs[kind] = tuple(_adamw_piece(cidx, view2d(P[kind]), view2d(M[kind]), view2d(V[kind]), mine, other, bufs[kind], r0,
                                            f"adamw_l{i}_{kind}"))
        return lands[0]

    first_layer = {}

    def grads_of(i, gW, dx_i):
        if i == 0:
            first_layer.update(gW)
            return None
        dep = scatter_finish(i + 2, [dx_i]) if i + 2 in scatters else dx_i
        return scatter_start(i, gW, dep)

    gather_start(0, mod)
    gather_start(1, gathers[0][-1])
    gather_start(2, gathers[1][-1])
    mod = mod + gathers[2][-1][0, 0]
    loss_l, dx, gS, dmod = _local_step(x[0], positions[0], loss_target[0], mod, S, weights_of, grads_of)

    gS["dmod"] = dmod
    gS["loss"] = loss_l
    packed = _pack([gS[n] for n in _SMALL])
    sg = _xchip_start("all8", [packed], [jax.ShapeDtypeStruct((N_DEV, *packed.shape), F32)], dx, "sg_start")
    tok0 = scatter_start(0, first_layer, sg[-1])[0, 0]
    done = lambda layer: [bufs[kind][0] for kind, _ in _layer_pieces(layer)]
    scatter_finish(2, [dx, scatters[0][-1]])
    scatter_finish(1, done(2))
    sg_src, sg_land = _xchip_wait("all8", sg[0], sg[1], sg[2], sg[3], done(1), "sg_wait")
    small = lax.dynamic_update_index_in_dim(sg_land[0], sg_src[0], 4 * xi + 2 * yi + ci, 0) + tok0
    small_sum = _unpack(_sum_lead([small], "sum_small_grads"), list(_SMALL.values()))
    G = dict(zip(_SMALL, small_sum, strict=True))
    grads = {
        "ada_b": G["dmod"], "norm_mix_g": G["norm_mix_g"], "norm_mlp_g": G["norm_mlp_g"], "sgu_ln_g": G["sgu_ln_g"], "sgu_ln_b": G["sgu_ln_b"],
        "sgu_w_s": G["sgu_w_s"][None], "sgu_b_s": G["sgu_b_s"][None], "mla_kv_norm_g": G["mla_kv_norm_g"], "final_g": G["final_g"][0],
        "pool_scale": lax.dynamic_slice_in_dim(G["pool_scale"], chip * (D // N_CHIPS), D // N_CHIPS, axis=1),
        "mla_q_norm_g": lax.dynamic_slice_in_dim(G["mla_q_norm_g"], chip * (MLA_QL // N_CHIPS), MLA_QL // N_CHIPS, axis=1),
    }
    dmod_all = _unpack(small, [(N_DEV,) + (small.shape[1] * _PACK_W,)])[0]
    off = sum(math.prod(s) for n, s in _SMALL.items() if n != "dmod")
    dmod_all = dmod_all[:, off:off + DEPTH * 6 * D].reshape(N_DEV, DEPTH, 6 * D)
    dmod_loc = lax.dynamic_slice_in_dim(dmod_all, chip * n_ada, n_ada, axis=2).transpose(1, 0, 2)
    deltas, new_m, new_v = {}, {}, {}
    grads["ada_w"], deltas["ada_w"], new_m["ada_w"], new_v["ada_w"] = _ada_bwd_adamw(c_all.T, dmod_loc, ada_w, m_ada_w, v_ada_w, "adamw_ada_w")
    for n in order:
        if n not in _PIECE_KINDS and n != "ada_w":
            deltas[n], new_m[n], new_v[n] = _adamw(P[n], grads[n].reshape(P[n].shape), M[n], V[n], f"adamw_{n}")
    scatter_finish(0, [deltas["ada_w"], deltas["sgu_w_s"]] + [bufs[n][0] for n in ("mlp_w1", "mlp_w2", "sgu_w_in", "mla_w_o")])
    for n in _PIECE_KINDS:
        grads[n], deltas[n], new_m[n], new_v[n] = (b.reshape(P[n].shape) for b in bufs[n])
    return (G["loss"][0, 0], dx[None], *[grads[n].reshape(P[n].shape) for n in order], *[deltas[n] for n in order], *[new_m[n] for n in order],
            *[new_v[n] for n in order])
```
